```python
import jax, jax.numpy as jnp
from jax import lax
import numpy as np

D_MODEL = 1024
BATCH = 8
SEQ = 4096
DEPTH = 2

HEAD_DIM = 64
N_Q_HEADS = 8
N_KV_HEADS = 2
Q_PER_KV = N_Q_HEADS // N_KV_HEADS
ATTN_WIDTH = N_Q_HEADS * HEAD_DIM
KV_WIDTH = N_KV_HEADS * HEAD_DIM
WINDOW = 128
ATTN_BLOCK = 128
ROPE_DIM = HEAD_DIM // 4
ROPE_THETA = 500000.0
N_SG_GROUPS = 8
SG_GROUP_DIM = 64
SG_WIDTH = N_SG_GROUPS * SG_GROUP_DIM
SG_CHUNK = 128
EVEN_GATE_WIDTH = ATTN_WIDTH + SG_WIDTH
EVEN_SPLITS = tuple(int(s) for s in np.cumsum([ATTN_WIDTH, KV_WIDTH, KV_WIDTH, SG_WIDTH, SG_WIDTH]))
EVEN_IN_WIDTH = EVEN_SPLITS[-1] + EVEN_GATE_WIDTH
RNN_WIDTH = D_MODEL
RNN_HEADS = 8
RNN_HEAD_DIM = RNN_WIDTH // RNN_HEADS
CONV_WIDTH = 4
CONV_PAD = (2, 1)
RG_LRU_C = 8.0
ODD_IN_WIDTH = 2 * RNN_WIDTH
N_EVEN = (DEPTH + 1) // 2
N_ODD = DEPTH // 2
DEEPNORM_ALPHA = (2 * DEPTH) ** 0.25
DEEPNORM_BETA = (8 * DEPTH) ** -0.25
LN_EPS = 1e-5
NEG_INF = -1e30

kernel_name = "hybrid_swa_gmlp_rglru_deepnorm_encoder"


def layer_norm(x, g, b):
    xf = x.astype(jnp.float32)
    mu = xf.mean(-1, keepdims=True)
    var = jnp.square(xf - mu).mean(-1, keepdims=True)
    y = (xf - mu) * lax.rsqrt(var + LN_EPS)
    return (y * g.astype(jnp.float32) + b.astype(jnp.float32)).astype(x.dtype)


def partial_rotary(t, positions):
    half = ROPE_DIM // 2
    inv_freq = jnp.power(jnp.float32(ROPE_THETA), -jnp.arange(half, dtype=jnp.float32) / half)
    ang = positions.astype(jnp.float32)[:, :, None, None] * inv_freq
    cos, sin = jnp.cos(ang), jnp.sin(ang)
    tr = t[..., :ROPE_DIM].astype(jnp.float32)
    t1, t2 = tr[..., :half], tr[..., half:]
    rot = jnp.concatenate([t1 * cos - t2 * sin, t2 * cos + t1 * sin], axis=-1)
    return jnp.concatenate([rot.astype(t.dtype), t[..., ROPE_DIM:]], axis=-1)


def windowed_gqa_sink(q, k, v, sink):
    B, S = q.shape[0], q.shape[1]
    nb = S // ATTN_BLOCK
    qb = q.reshape(B, nb, ATTN_BLOCK, N_KV_HEADS, Q_PER_KV, HEAD_DIM)

    def band(t):
        tp = jnp.pad(t, ((0, 0), (ATTN_BLOCK, ATTN_BLOCK), (0, 0), (0, 0)))
        parts = [tp[:, o * ATTN_BLOCK:o * ATTN_BLOCK + S].reshape(B, nb, ATTN_BLOCK, N_KV_HEADS, HEAD_DIM)
                 for o in range(3)]
        return jnp.concatenate(parts, axis=2)

    kb, vb = band(k), band(v)
    s = jnp.einsum('bnqhgd,bnkhd->bnhgqk', qb, kb).astype(jnp.float32) * (HEAD_DIM ** -0.5)
    qi = jnp.arange(ATTN_BLOCK)[:, None]
    kj = jnp.arange(3 * ATTN_BLOCK)[None, :]
    blk = jnp.arange(nb)[:, None, None]
    k_abs = blk * ATTN_BLOCK - ATTN_BLOCK + kj
    valid = (jnp.abs(kj - ATTN_BLOCK - qi) <= WINDOW) & (k_abs >= 0) & (k_abs < S)
    s = jnp.where(valid[None, :, None, None], s, NEG_INF)
    sink_l = sink.astype(jnp.float32).reshape(N_KV_HEADS, Q_PER_KV)[None, None, :, :, None, None]
    m = jnp.maximum(s.max(-1, keepdims=True), sink_l)
    p = jnp.exp(s - m)
    denom = p.sum(-1, keepdims=True) + jnp.exp(sink_l - m)
    o = jnp.einsum('bnhgqk,bnkhd->bnqhgd', (p / denom).astype(v.dtype), vb)
    return o.reshape(B, S, ATTN_WIDTH)


def chunked_spatial_gating(u, v, ln_g, ln_b, w_s, b_s):
    B, S = u.shape[0], u.shape[1]
    nc = S // SG_CHUNK
    vg = v.reshape(B, S, N_SG_GROUPS, SG_GROUP_DIM)
    vg = layer_norm(vg, ln_g.reshape(N_SG_GROUPS, SG_GROUP_DIM), ln_b.reshape(N_SG_GROUPS, SG_GROUP_DIM))
    vc = vg.reshape(B, nc, SG_CHUNK, N_SG_GROUPS, SG_GROUP_DIM)
    sv = jnp.einsum('gpq,bcqgd->bcpgd', w_s, vc) + b_s.T[None, None, :, :, None]
    return u * sv.reshape(B, S, SG_WIDTH)


def centred_depthwise_conv(x, w, b):
    y = lax.conv_general_dilated(x, w[:, None, :], window_strides=(1,), padding=[CONV_PAD],
                                 dimension_numbers=('NWC', 'WIO', 'NWC'),
                                 feature_group_count=x.shape[-1])
    return y + b


def rg_lru(x, w_a, b_a, w_x, b_x, lam, reverse):
    B, S = x.shape[0], x.shape[1]
    xh = x.reshape(B, S, RNN_HEADS, RNN_HEAD_DIM)
    pre_r = jnp.einsum('bshi,hij->bshj', xh, w_a).reshape(B, S, RNN_WIDTH) + b_a
    pre_i = jnp.einsum('bshi,hij->bshj', xh, w_x).reshape(B, S, RNN_WIDTH) + b_x
    rec_gate = jax.nn.sigmoid(pre_r.astype(jnp.float32))
    in_gate = jax.nn.sigmoid(pre_i.astype(jnp.float32))
    log_a = -RG_LRU_C * rec_gate * jax.nn.softplus(-lam.astype(jnp.float32))
    a = jnp.exp(log_a)
    bterm = jnp.sqrt(-jnp.expm1(2.0 * log_a)) * in_gate * x.astype(jnp.float32)

    def combine(lhs, rhs):
        a1, b1 = lhs
        a2, b2 = rhs
        return a1 * a2, a2 * b1 + b2

    _, h = lax.associative_scan(combine, (a, bterm), axis=1, reverse=reverse)
    return h


def even_mixer(h, positions, w_in, w_out, sink, sg_ln_g, sg_ln_b, sg_w, sg_b):
    B, S = h.shape[0], h.shape[1]
    q, k, v, su, sv, g = jnp.split(h @ w_in, EVEN_SPLITS, axis=-1)
    q = partial_rotary(q.reshape(B, S, N_Q_HEADS, HEAD_DIM), positions)
    k = partial_rotary(k.reshape(B, S, N_KV_HEADS, HEAD_DIM), positions)
    v = v.reshape(B, S, N_KV_HEADS, HEAD_DIM)
    y_attn = windowed_gqa_sink(q, k, v, sink)
    y_sg = chunked_spatial_gating(su, sv, sg_ln_g, sg_ln_b, sg_w, sg_b)
    y = jnp.concatenate([y_attn, y_sg], axis=-1) * jax.nn.silu(g)
    return y @ w_out


def odd_mixer(h, w_in, conv_w, conv_b, w_a, b_a, w_x, b_x, lam, w_out):
    xr, g = jnp.split(h @ w_in, 2, axis=-1)
    xr = centred_depthwise_conv(xr, conv_w, conv_b)
    y = (rg_lru(xr, w_a[0], b_a[0], w_x[0], b_x[0], lam[0], reverse=False)
         + rg_lru(xr, w_a[1], b_a[1], w_x[1], b_x[1], lam[1], reverse=True))
    y = y.astype(h.dtype) * jax.nn.silu(g)
    return y @ w_out


def _fwd_setup_inputs(seed: int = 0) -> dict:
    key = jax.random.key(seed)
    ks = jax.random.split(key, 24)
    f32 = jnp.float32
    nrm = lambda k, shape, s: jax.random.normal(k, shape, f32) * s
    a_c = jax.random.uniform(ks[22], (N_ODD, 2, RNN_WIDTH), f32, minval=0.9, maxval=0.999)
    p = a_c ** (1.0 / RG_LRU_C)
    lam = jnp.log(p) - jnp.log1p(-p)
    return {
        "x": nrm(ks[0], (BATCH, SEQ, D_MODEL), 1.0),
        "c": nrm(ks[1], (BATCH, D_MODEL), 1.0),
        "positions": jnp.broadcast_to(jnp.arange(SEQ, dtype=jnp.int32), (BATCH, SEQ)),
        "ada_w": nrm(ks[2], (DEPTH, D_MODEL, 3 * D_MODEL), D_MODEL ** -0.5),
        "ada_b": nrm(ks[3], (DEPTH, 3 * D_MODEL), 0.01),
        "ln_g": 1.0 + nrm(ks[4], (DEPTH, D_MODEL), 0.02),
        "ln_b": nrm(ks[5], (DEPTH, D_MODEL), 0.02),
        "ev_w_in": nrm(ks[6], (N_EVEN, D_MODEL, EVEN_IN_WIDTH), D_MODEL ** -0.5),
        "ev_w_out": nrm(ks[7], (N_EVEN, EVEN_GATE_WIDTH, D_MODEL), DEEPNORM_BETA * EVEN_GATE_WIDTH ** -0.5),
        "ev_sink": nrm(ks[8], (N_EVEN, N_Q_HEADS), 1.0),
        "ev_sg_ln_g": 1.0 + nrm(ks[9], (N_EVEN, SG_WIDTH), 0.02),
        "ev_sg_ln_b": nrm(ks[10], (N_EVEN, SG_WIDTH), 0.02),
        "ev_sg_w": nrm(ks[11], (N_EVEN, N_SG_GROUPS, SG_CHUNK, SG_CHUNK), SG_CHUNK ** -0.5),
        "ev_sg_b": 1.0 + nrm(ks[12], (N_EVEN, N_SG_GROUPS, SG_CHUNK), 0.1),
        "od_w_in": nrm(ks[13], (N_ODD, D_MODEL, ODD_IN_WIDTH), D_MODEL ** -0.5),
        "od_conv_w": nrm(ks[14], (N_ODD, CONV_WIDTH, RNN_WIDTH), CONV_WIDTH ** -0.5),
        "od_conv_b": nrm(ks[15], (N_ODD, RNN_WIDTH), 0.01),
        "od_w_a": nrm(ks[16], (N_ODD, 2, RNN_HEADS, RNN_HEAD_DIM, RNN_HEAD_DIM), RNN_HEAD_DIM ** -0.5),
        "od_b_a": nrm(ks[17], (N_ODD, 2, RNN_WIDTH), 0.01),
        "od_w_x": nrm(ks[18], (N_ODD, 2, RNN_HEADS, RNN_HEAD_DIM, RNN_HEAD_DIM), RNN_HEAD_DIM ** -0.5),
        "od_b_x": nrm(ks[19], (N_ODD, 2, RNN_WIDTH), 0.01),
        "od_lam": lam,
        "od_w_out": nrm(ks[20], (N_ODD, RNN_WIDTH, D_MODEL), DEEPNORM_BETA * RNN_WIDTH ** -0.5),
    }


def _fwd_reference(x, c, positions, ada_w, ada_b, ln_g, ln_b, ev_w_in, ev_w_out, ev_sink, ev_sg_ln_g, ev_sg_ln_b,
              ev_sg_w, ev_sg_b, od_w_in, od_conv_w, od_conv_b, od_w_a, od_b_a, od_w_x, od_b_x, od_lam, od_w_out):
    cond = jax.nn.silu(c)
    for layer in range(DEPTH):
        mod = cond @ ada_w[layer] + ada_b[layer]
        shift, scale, gate = jnp.split(mod, 3, axis=-1)
        h = x * (1.0 + scale[:, None, :]) + shift[:, None, :]
        j = layer // 2
        if layer % 2 == 0:
            y = even_mixer(h, positions, ev_w_in[j], ev_w_out[j], ev_sink[j], ev_sg_ln_g[j], ev_sg_ln_b[j],
                           ev_sg_w[j], ev_sg_b[j])
        else:
            y = odd_mixer(h, od_w_in[j], od_conv_w[j], od_conv_b[j], od_w_a[j], od_b_a[j], od_w_x[j],
                          od_b_x[j], od_lam[j], od_w_out[j])
        x = layer_norm(DEEPNORM_ALPHA * x + gate[:, None, :] * y, ln_g[layer], ln_b[layer])
    return x


import jax as _jax
import jax.numpy as _jnp

TWIN_FORMAT = 'train_step'
FWD_PARAMS = ['x', 'c', 'positions', 'ada_w', 'ada_b', 'ln_g', 'ln_b', 'ev_w_in', 'ev_w_out', 'ev_sink', 'ev_sg_ln_g', 'ev_sg_ln_b', 'ev_sg_w', 'ev_sg_b', 'od_w_in', 'od_conv_w', 'od_conv_b', 'od_w_a', 'od_b_a', 'od_w_x', 'od_b_x', 'od_lam', 'od_w_out']
TWIN_WEIGHTS = ['ada_w', 'ada_b', 'ln_g', 'ln_b', 'ev_w_in', 'ev_w_out', 'ev_sink', 'ev_sg_ln_g', 'ev_sg_ln_b', 'ev_sg_w', 'ev_sg_b', 'od_w_in', 'od_conv_w', 'od_conv_b', 'od_w_a', 'od_b_a', 'od_w_x', 'od_b_x', 'od_lam', 'od_w_out']
TWIN_DIFF_INPUT = 'x'
TWIN_INPUTS = ['x', 'c', 'positions', 'ada_w', 'ada_b', 'ln_g', 'ln_b', 'ev_w_in', 'ev_w_out', 'ev_sink', 'ev_sg_ln_g', 'ev_sg_ln_b', 'ev_sg_w', 'ev_sg_b', 'od_w_in', 'od_conv_w', 'od_conv_b', 'od_w_a', 'od_b_a', 'od_w_x', 'od_b_x', 'od_lam', 'od_w_out', 'loss_target', 'm_ada_w', 'm_ada_b', 'm_ln_g', 'm_ln_b', 'm_ev_w_in', 'm_ev_w_out', 'm_ev_sink', 'm_ev_sg_ln_g', 'm_ev_sg_ln_b', 'm_ev_sg_w', 'm_ev_sg_b', 'm_od_w_in', 'm_od_conv_w', 'm_od_conv_b', 'm_od_w_a', 'm_od_b_a', 'm_od_w_x', 'm_od_b_x', 'm_od_lam', 'm_od_w_out', 'v_ada_w', 'v_ada_b', 'v_ln_g', 'v_ln_b', 'v_ev_w_in', 'v_ev_w_out', 'v_ev_sink', 'v_ev_sg_ln_g', 'v_ev_sg_ln_b', 'v_ev_sg_w', 'v_ev_sg_b', 'v_od_w_in', 'v_od_conv_w', 'v_od_conv_b', 'v_od_w_a', 'v_od_b_a', 'v_od_w_x', 'v_od_b_x', 'v_od_lam', 'v_od_w_out']
TWIN_OUTPUTS = ['loss', 'grad_x', 'grad_ada_w', 'grad_ada_b', 'grad_ln_g', 'grad_ln_b', 'grad_ev_w_in', 'grad_ev_w_out', 'grad_ev_sink', 'grad_ev_sg_ln_g', 'grad_ev_sg_ln_b', 'grad_ev_sg_w', 'grad_ev_sg_b', 'grad_od_w_in', 'grad_od_conv_w', 'grad_od_conv_b', 'grad_od_w_a', 'grad_od_b_a', 'grad_od_w_x', 'grad_od_b_x', 'grad_od_lam', 'grad_od_w_out', 'delta_ada_w', 'delta_ada_b', 'delta_ln_g', 'delta_ln_b', 'delta_ev_w_in', 'delta_ev_w_out', 'delta_ev_sink', 'delta_ev_sg_ln_g', 'delta_ev_sg_ln_b', 'delta_ev_sg_w', 'delta_ev_sg_b', 'delta_od_w_in', 'delta_od_conv_w', 'delta_od_conv_b', 'delta_od_w_a', 'delta_od_b_a', 'delta_od_w_x', 'delta_od_b_x', 'delta_od_lam', 'delta_od_w_out', 'new_m_ada_w', 'new_m_ada_b', 'new_m_ln_g', 'new_m_ln_b', 'new_m_ev_w_in', 'new_m_ev_w_out', 'new_m_ev_sink', 'new_m_ev_sg_ln_g', 'new_m_ev_sg_ln_b', 'new_m_ev_sg_w', 'new_m_ev_sg_b', 'new_m_od_w_in', 'new_m_od_conv_w', 'new_m_od_conv_b', 'new_m_od_w_a', 'new_m_od_b_a', 'new_m_od_w_x', 'new_m_od_b_x', 'new_m_od_lam', 'new_m_od_w_out', 'new_v_ada_w', 'new_v_ada_b', 'new_v_ln_g', 'new_v_ln_b', 'new_v_ev_w_in', 'new_v_ev_w_out', 'new_v_ev_sink', 'new_v_ev_sg_ln_g', 'new_v_ev_sg_ln_b', 'new_v_ev_sg_w', 'new_v_ev_sg_b', 'new_v_od_w_in', 'new_v_od_conv_w', 'new_v_od_conv_b', 'new_v_od_w_a', 'new_v_od_b_a', 'new_v_od_w_x', 'new_v_od_b_x', 'new_v_od_lam', 'new_v_od_w_out']
TWIN_LEAF_KINDS = {'loss': 'loss', 'grad_x': 'grad_x', 'grad_ada_w': 'grad_w', 'grad_ada_b': 'grad_w', 'grad_ln_g': 'grad_w', 'grad_ln_b': 'grad_w', 'grad_ev_w_in': 'grad_w', 'grad_ev_w_out': 'grad_w', 'grad_ev_sink': 'grad_w', 'grad_ev_sg_ln_g': 'grad_w', 'grad_ev_sg_ln_b': 'grad_w', 'grad_ev_sg_w': 'grad_w', 'grad_ev_sg_b': 'grad_w', 'grad_od_w_in': 'grad_w', 'grad_od_conv_w': 'grad_w', 'grad_od_conv_b': 'grad_w', 'grad_od_w_a': 'grad_w', 'grad_od_b_a': 'grad_w', 'grad_od_w_x': 'grad_w', 'grad_od_b_x': 'grad_w', 'grad_od_lam': 'grad_w', 'grad_od_w_out': 'grad_w', 'delta_ada_w': 'delta_w', 'delta_ada_b': 'delta_w', 'delta_ln_g': 'delta_w', 'delta_ln_b': 'delta_w', 'delta_ev_w_in': 'delta_w', 'delta_ev_w_out': 'delta_w', 'delta_ev_sink': 'delta_w', 'delta_ev_sg_ln_g': 'delta_w', 'delta_ev_sg_ln_b': 'delta_w', 'delta_ev_sg_w': 'delta_w', 'delta_ev_sg_b': 'delta_w', 'delta_od_w_in': 'delta_w', 'delta_od_conv_w': 'delta_w', 'delta_od_conv_b': 'delta_w', 'delta_od_w_a': 'delta_w', 'delta_od_b_a': 'delta_w', 'delta_od_w_x': 'delta_w', 'delta_od_b_x': 'delta_w', 'delta_od_lam': 'delta_w', 'delta_od_w_out': 'delta_w', 'new_m_ada_w': 'new_m', 'new_m_ada_b': 'new_m', 'new_m_ln_g': 'new_m', 'new_m_ln_b': 'new_m', 'new_m_ev_w_in': 'new_m', 'new_m_ev_w_out': 'new_m', 'new_m_ev_sink': 'new_m', 'new_m_ev_sg_ln_g': 'new_m', 'new_m_ev_sg_ln_b': 'new_m', 'new_m_ev_sg_w': 'new_m', 'new_m_ev_sg_b': 'new_m', 'new_m_od_w_in': 'new_m', 'new_m_od_conv_w': 'new_m', 'new_m_od_conv_b': 'new_m', 'new_m_od_w_a': 'new_m', 'new_m_od_b_a': 'new_m', 'new_m_od_w_x': 'new_m', 'new_m_od_b_x': 'new_m', 'new_m_od_lam': 'new_m', 'new_m_od_w_out': 'new_m', 'new_v_ada_w': 'new_v', 'new_v_ada_b': 'new_v', 'new_v_ln_g': 'new_v', 'new_v_ln_b': 'new_v', 'new_v_ev_w_in': 'new_v', 'new_v_ev_w_out': 'new_v', 'new_v_ev_sink': 'new_v', 'new_v_ev_sg_ln_g': 'new_v', 'new_v_ev_sg_ln_b': 'new_v', 'new_v_ev_sg_w': 'new_v', 'new_v_ev_sg_b': 'new_v', 'new_v_od_w_in': 'new_v', 'new_v_od_conv_w': 'new_v', 'new_v_od_conv_b': 'new_v', 'new_v_od_w_a': 'new_v', 'new_v_od_b_a': 'new_v', 'new_v_od_w_x': 'new_v', 'new_v_od_b_x': 'new_v', 'new_v_od_lam': 'new_v', 'new_v_od_w_out': 'new_v'}


def _forward(args):
    return _fwd_reference(*[args[k] for k in FWD_PARAMS])


def _output_shape():
    def fwd():
        inp = _fwd_setup_inputs(0)
        return _fwd_reference(*[inp[k] for k in FWD_PARAMS])
    out = _jax.eval_shape(fwd)
    return out.shape, out.dtype

N_MICROBATCH = 1
ADAM_LR = 0.001
ADAM_B1 = 0.9
ADAM_B2 = 0.999
ADAM_EPS = 1e-08
ADAM_WD = 0.01
ADAM_STEP = 10
PER_EXAMPLE_BATCH_AXIS = {'x': 0, 'c': 0, 'positions': 0, 'loss_target': 0}
SHARED_INPUTS = []
_WEIGHT_DTYPES = {'ada_w': _jnp.float32, 'ada_b': _jnp.float32, 'ln_g': _jnp.float32, 'ln_b': _jnp.float32, 'ev_w_in': _jnp.float32, 'ev_w_out': _jnp.float32, 'ev_sink': _jnp.float32, 'ev_sg_ln_g': _jnp.float32, 'ev_sg_ln_b': _jnp.float32, 'ev_sg_w': _jnp.float32, 'ev_sg_b': _jnp.float32, 'od_w_in': _jnp.float32, 'od_conv_w': _jnp.float32, 'od_conv_b': _jnp.float32, 'od_w_a': _jnp.float32, 'od_b_a': _jnp.float32, 'od_w_x': _jnp.float32, 'od_b_x': _jnp.float32, 'od_lam': _jnp.float32, 'od_w_out': _jnp.float32}
MOMENT_SCALE = {'ada_w': 1.563952e-01, 'ada_b': 3.025074e-01, 'ln_g': 2.330419e+01, 'ln_b': 3.246534e+00, 'ev_w_in': 4.253754e-02, 'ev_w_out': 9.269708e-02, 'ev_sink': 8.399144e-04, 'ev_sg_ln_g': 3.676146e-02, 'ev_sg_ln_b': 3.660531e-02, 'ev_sg_w': 2.794685e-02, 'ev_sg_b': 2.908469e-02, 'od_w_in': 1.857680e-01, 'od_conv_w': 1.805382e-01, 'od_conv_b': 3.569696e-01, 'od_w_a': 1.257853e-02, 'od_b_a': 1.806369e-02, 'od_w_x': 2.611066e-02, 'od_b_x': 3.624696e-02, 'od_lam': 4.993718e-02, 'od_w_out': 4.030295e-01}


def _to_microbatches(a, axis):
    t = _jnp.moveaxis(a, axis, 0)
    t = t.reshape((N_MICROBATCH, t.shape[0] // N_MICROBATCH) + t.shape[1:])
    return _jnp.moveaxis(t, 1, axis + 1)


def setup_inputs(seed: int = 0) -> dict:
    inp = _fwd_setup_inputs(seed)
    key = _jax.random.fold_in(_jax.random.key(seed), 7919)
    shape, _ = _output_shape()
    out = dict(inp)
    out["loss_target"] = _jax.random.normal(_jax.random.fold_in(key, 0), shape, _jnp.float32)
    for i, name in enumerate(TWIN_WEIGHTS):
        w = inp[name].astype(_jnp.float32)
        if MOMENT_SCALE is None:
            s = _jnp.sqrt(_jnp.mean(_jnp.square(w)) + 1e-30)
        else:
            s = MOMENT_SCALE[name]
        km, kv = _jax.random.split(_jax.random.fold_in(key, i + 1))
        out[name] = w
        out["m_" + name] = s * _jax.random.normal(km, w.shape, _jnp.float32)
        out["v_" + name] = (s * s) * _jax.random.uniform(kv, w.shape, _jnp.float32, 0.5, 1.5)
    if N_MICROBATCH > 1:
        for name, axis in PER_EXAMPLE_BATCH_AXIS.items():
            out[name] = _to_microbatches(out[name], axis)
    return {'x': out['x'], 'c': out['c'], 'positions': out['positions'], 'ada_w': out['ada_w'], 'ada_b': out['ada_b'], 'ln_g': out['ln_g'], 'ln_b': out['ln_b'], 'ev_w_in': out['ev_w_in'], 'ev_w_out': out['ev_w_out'], 'ev_sink': out['ev_sink'], 'ev_sg_ln_g': out['ev_sg_ln_g'], 'ev_sg_ln_b': out['ev_sg_ln_b'], 'ev_sg_w': out['ev_sg_w'], 'ev_sg_b': out['ev_sg_b'], 'od_w_in': out['od_w_in'], 'od_conv_w': out['od_conv_w'], 'od_conv_b': out['od_conv_b'], 'od_w_a': out['od_w_a'], 'od_b_a': out['od_b_a'], 'od_w_x': out['od_w_x'], 'od_b_x': out['od_b_x'], 'od_lam': out['od_lam'], 'od_w_out': out['od_w_out'], 'loss_target': out['loss_target'], 'm_ada_w': out['m_ada_w'], 'm_ada_b': out['m_ada_b'], 'm_ln_g': out['m_ln_g'], 'm_ln_b': out['m_ln_b'], 'm_ev_w_in': out['m_ev_w_in'], 'm_ev_w_out': out['m_ev_w_out'], 'm_ev_sink': out['m_ev_sink'], 'm_ev_sg_ln_g': out['m_ev_sg_ln_g'], 'm_ev_sg_ln_b': out['m_ev_sg_ln_b'], 'm_ev_sg_w': out['m_ev_sg_w'], 'm_ev_sg_b': out['m_ev_sg_b'], 'm_od_w_in': out['m_od_w_in'], 'm_od_conv_w': out['m_od_conv_w'], 'm_od_conv_b': out['m_od_conv_b'], 'm_od_w_a': out['m_od_w_a'], 'm_od_b_a': out['m_od_b_a'], 'm_od_w_x': out['m_od_w_x'], 'm_od_b_x': out['m_od_b_x'], 'm_od_lam': out['m_od_lam'], 'm_od_w_out': out['m_od_w_out'], 'v_ada_w': out['v_ada_w'], 'v_ada_b': out['v_ada_b'], 'v_ln_g': out['v_ln_g'], 'v_ln_b': out['v_ln_b'], 'v_ev_w_in': out['v_ev_w_in'], 'v_ev_w_out': out['v_ev_w_out'], 'v_ev_sink': out['v_ev_sink'], 'v_ev_sg_ln_g': out['v_ev_sg_ln_g'], 'v_ev_sg_ln_b': out['v_ev_sg_ln_b'], 'v_ev_sg_w': out['v_ev_sg_w'], 'v_ev_sg_b': out['v_ev_sg_b'], 'v_od_w_in': out['v_od_w_in'], 'v_od_conv_w': out['v_od_conv_w'], 'v_od_conv_b': out['v_od_conv_b'], 'v_od_w_a': out['v_od_w_a'], 'v_od_b_a': out['v_od_b_a'], 'v_od_w_x': out['v_od_w_x'], 'v_od_b_x': out['v_od_b_x'], 'v_od_lam': out['v_od_lam'], 'v_od_w_out': out['v_od_w_out']}


def _loss(weights, diff, rest, loss_target):
    with _jax.named_scope("forward"):
        args = {**rest, TWIN_DIFF_INPUT: diff, **{k: w.astype(_WEIGHT_DTYPES[k]) for k, w in weights.items()}}
        y = _forward(args)
    with _jax.named_scope("loss_head"):
        err = _jnp.square(y.astype(_jnp.float32) - loss_target)
        return 0.5 * _jnp.sum(_jnp.mean(err, axis=-1)) if err.ndim else 0.5 * err


def _adamw(w, g, m, v):
    m = ADAM_B1 * m + (1.0 - ADAM_B1) * g
    v = ADAM_B2 * v + (1.0 - ADAM_B2) * _jnp.square(g)
    m_hat = m / (1.0 - ADAM_B1 ** ADAM_STEP)
    v_hat = v / (1.0 - ADAM_B2 ** ADAM_STEP)
    delta = -ADAM_LR * (m_hat / (_jnp.sqrt(v_hat) + ADAM_EPS) + ADAM_WD * w)
    return delta, m, v


def reference(x, c, positions, ada_w, ada_b, ln_g, ln_b, ev_w_in, ev_w_out, ev_sink, ev_sg_ln_g, ev_sg_ln_b, ev_sg_w, ev_sg_b, od_w_in, od_conv_w, od_conv_b, od_w_a, od_b_a, od_w_x, od_b_x, od_lam, od_w_out, loss_target, m_ada_w, m_ada_b, m_ln_g, m_ln_b, m_ev_w_in, m_ev_w_out, m_ev_sink, m_ev_sg_ln_g, m_ev_sg_ln_b, m_ev_sg_w, m_ev_sg_b, m_od_w_in, m_od_conv_w, m_od_conv_b, m_od_w_a, m_od_b_a, m_od_w_x, m_od_b_x, m_od_lam, m_od_w_out, v_ada_w, v_ada_b, v_ln_g, v_ln_b, v_ev_w_in, v_ev_w_out, v_ev_sink, v_ev_sg_ln_g, v_ev_sg_ln_b, v_ev_sg_w, v_ev_sg_b, v_od_w_in, v_od_conv_w, v_od_conv_b, v_od_w_a, v_od_b_a, v_od_w_x, v_od_b_x, v_od_lam, v_od_w_out):
    given = dict(x=x, c=c, positions=positions, ada_w=ada_w, ada_b=ada_b, ln_g=ln_g, ln_b=ln_b, ev_w_in=ev_w_in, ev_w_out=ev_w_out, ev_sink=ev_sink, ev_sg_ln_g=ev_sg_ln_g, ev_sg_ln_b=ev_sg_ln_b, ev_sg_w=ev_sg_w, ev_sg_b=ev_sg_b, od_w_in=od_w_in, od_conv_w=od_conv_w, od_conv_b=od_conv_b, od_w_a=od_w_a, od_b_a=od_b_a, od_w_x=od_w_x, od_b_x=od_b_x, od_lam=od_lam, od_w_out=od_w_out, loss_target=loss_target, m_ada_w=m_ada_w, m_ada_b=m_ada_b, m_ln_g=m_ln_g, m_ln_b=m_ln_b, m_ev_w_in=m_ev_w_in, m_ev_w_out=m_ev_w_out, m_ev_sink=m_ev_sink, m_ev_sg_ln_g=m_ev_sg_ln_g, m_ev_sg_ln_b=m_ev_sg_ln_b, m_ev_sg_w=m_ev_sg_w, m_ev_sg_b=m_ev_sg_b, m_od_w_in=m_od_w_in, m_od_conv_w=m_od_conv_w, m_od_conv_b=m_od_conv_b, m_od_w_a=m_od_w_a, m_od_b_a=m_od_b_a, m_od_w_x=m_od_w_x, m_od_b_x=m_od_b_x, m_od_lam=m_od_lam, m_od_w_out=m_od_w_out, v_ada_w=v_ada_w, v_ada_b=v_ada_b, v_ln_g=v_ln_g, v_ln_b=v_ln_b, v_ev_w_in=v_ev_w_in, v_ev_w_out=v_ev_w_out, v_ev_sink=v_ev_sink, v_ev_sg_ln_g=v_ev_sg_ln_g, v_ev_sg_ln_b=v_ev_sg_ln_b, v_ev_sg_w=v_ev_sg_w, v_ev_sg_b=v_ev_sg_b, v_od_w_in=v_od_w_in, v_od_conv_w=v_od_conv_w, v_od_conv_b=v_od_conv_b, v_od_w_a=v_od_w_a, v_od_b_a=v_od_b_a, v_od_w_x=v_od_w_x, v_od_b_x=v_od_b_x, v_od_lam=v_od_lam, v_od_w_out=v_od_w_out)
    weights = {n: given[n] for n in TWIN_WEIGHTS}
    shared = {n: given[n] for n in SHARED_INPUTS}
    per_example = {n: given[n] for n in ['x', 'c', 'positions']}
    grad_fn = _jax.value_and_grad(_loss, argnums=(0, 1))

    def one_microbatch(ex, loss_target):
        ex = dict(ex)
        diff = ex.pop(TWIN_DIFF_INPUT)
        return grad_fn(weights, diff, {**shared, **ex}, loss_target)

    if N_MICROBATCH == 1:
        loss, (grad_w, grad_x) = one_microbatch(per_example, given["loss_target"])
    else:
        def body(carry, xs):
            loss_sum, grad_sum = carry
            l_k, (gw_k, gx_k) = one_microbatch(xs[0], xs[1])
            with _jax.named_scope("update"):
                return (loss_sum + l_k, _jax.tree.map(_jnp.add, grad_sum, gw_k)), gx_k

        init = (_jnp.zeros((), _jnp.float32), _jax.tree.map(_jnp.zeros_like, weights))
        (loss, grad_w), grad_x = _jax.lax.scan(body, init, (per_example, given["loss_target"]))
    with _jax.named_scope("update"):
        delta_w, new_m, new_v = {}, {}, {}
        for n in TWIN_WEIGHTS:
            delta_w[n], new_m[n], new_v[n] = _adamw(weights[n], grad_w[n], given["m_" + n], given["v_" + n])
    return (loss, grad_x, *[grad_w[n] for n in TWIN_WEIGHTS], *[delta_w[n] for n in TWIN_WEIGHTS],
            *[new_m[n] for n in TWIN_WEIGHTS], *[new_v[n] for n in TWIN_WEIGHTS])
```

```python
import functools

import jax
import jax.numpy as jnp
from jax import lax
from jax.experimental import pallas as pl
from jax.experimental.pallas import tpu as pltpu

F32 = jnp.float32
BF16 = jnp.bfloat16
MXU_DTYPE = BF16

D_MODEL = 1024
HEAD_DIM = 64
N_Q_HEADS = 8
Q_PER_KV = 4
ATTN_WIDTH = 512
KV_WIDTH = 128
BLK = 128
ROPE_DIM = 16
ROPE_THETA = 500000.0
N_SG_GROUPS = 8
SG_WIDTH = 512
EVEN_IN = 2816
ODD_IN = 2048
RNN_HEADS = 8
RG_LRU_C = 8.0
ALPHA = (2 * 2) ** 0.25
LN_EPS = 1e-5
NEG_INF = -1e30
ADAM_LR, ADAM_B1, ADAM_B2, ADAM_EPS, ADAM_WD, ADAM_STEP = 0.001, 0.9, 0.999, 1e-08, 0.01, 10

LANES = 128
VMEM_LIMIT = 56 * 1024 * 1024
MESH = pl.DeviceIdType.MESH


def _mm(a, b):
    return jnp.dot(a.astype(MXU_DTYPE), b.astype(MXU_DTYPE), preferred_element_type=F32)


def _mm_nt(a, b):
    return lax.dot_general(a.astype(MXU_DTYPE), b.astype(MXU_DTYPE), (((1,), (1,)), ((), ())), preferred_element_type=F32)


def _mm_tn(a, b):
    return lax.dot_general(a.astype(MXU_DTYPE), b.astype(MXU_DTYPE), (((0,), (0,)), ((), ())), preferred_element_type=F32)


def _sigmoid(x):
    return 1.0 / (1.0 + jnp.exp(-x))


def _ln_stats(z):
    mu = jnp.mean(z, axis=-1, keepdims=True)
    d = z - mu
    var = jnp.mean(d * d, axis=-1, keepdims=True)
    rstd = lax.rsqrt(var + LN_EPS)
    return d * rstd, rstd


def _ln_bwd(dout, zhat, rstd, g):
    dzh = dout * g
    m1 = jnp.mean(dzh, axis=-1, keepdims=True)
    m2 = jnp.mean(dzh * zhat, axis=-1, keepdims=True)
    return rstd * (dzh - m1 - zhat * m2)


def _group_sum(x, e2):
    hi = x.astype(MXU_DTYPE)
    lo = (x - hi.astype(F32)).astype(MXU_DTYPE)
    return jnp.dot(hi, e2, preferred_element_type=F32) + jnp.dot(lo, e2, preferred_element_type=F32)


def _lane_iota(shape):
    return lax.broadcasted_iota(jnp.int32, shape, 1)


def _to_kv_lanes(t, h):
    src_lo = (h % 2 == 0)
    dst_lo = (h // Q_PER_KV == 0)
    if src_lo != dst_lo:
        t = pltpu.roll(t, HEAD_DIM, 1)
    lane = _lane_iota(t.shape)
    keep = (lane < HEAD_DIM) if dst_lo else (lane >= HEAD_DIM)
    return jnp.where(keep, t, 0.0)


def _from_kv_lanes(t, h):
    src_lo = (h // Q_PER_KV == 0)
    dst_lo = (h % 2 == 0)
    lane = _lane_iota(t.shape)
    keep = (lane < HEAD_DIM) if src_lo else (lane >= HEAD_DIM)
    t = jnp.where(keep, t, 0.0)
    if src_lo != dst_lo:
        t = pltpu.roll(t, HEAD_DIM, 1)
    return t


def _rope(t, cos_t, sin_p, sin_m):
    half = ROPE_DIM // 2
    return t * cos_t + pltpu.roll(t, half, 1) * sin_p + pltpu.roll(t, LANES - half, 1) * sin_m


def _rope_t(d, cos_t, sin_p, sin_m):
    half = ROPE_DIM // 2
    return d * cos_t + pltpu.roll(d * sin_p, LANES - half, 1) + pltpu.roll(d * sin_m, half, 1)


def _band(ref, n, nb):
    prev = jnp.maximum(n - 1, 0)
    nxt = jnp.minimum(n + 1, nb - 1)
    rows = [ref[pl.ds(pl.multiple_of(j * BLK, BLK), BLK), :] for j in (prev, n, nxt)]
    return jnp.concatenate(rows, axis=0)


def _band_valid(n, seq):
    qi = lax.broadcasted_iota(jnp.int32, (BLK, 3 * BLK), 0)
    kj = lax.broadcasted_iota(jnp.int32, (BLK, 3 * BLK), 1)
    k_abs = n * BLK - BLK + kj
    return (jnp.abs(kj - BLK - qi) <= BLK) & (k_abs >= 0) & (k_abs < seq)


def _expm1(x):
    poly = x * (1.0 + x * (1.0 / 2 + x * (1.0 / 6 + x * (1.0 / 24 + x * (1.0 / 120 + x * (1.0 / 720 + x * (1.0 / 5040)))))))
    return jnp.where(jnp.abs(x) < 0.25, poly, jnp.exp(x) - 1.0)


def _softplus_neg(lam):
    e = jnp.exp(-jnp.abs(lam))
    u = 1.0 + e
    log1p_e = jnp.where(u == 1.0, e, jnp.log(u) * (e / (u - 1.0)))
    sp = jnp.maximum(-lam, 0.0) + log1p_e
    dsp = -1.0 / (1.0 + jnp.exp(lam))
    return sp, dsp


def _full(shape):
    return pl.BlockSpec(shape, lambda *_: (0,) * len(shape))


def _rows(tm, n):
    return pl.BlockSpec((tm, n), lambda i: (i, 0))


def _params(*sem):
    return pltpu.CompilerParams(dimension_semantics=sem, vmem_limit_bytes=VMEM_LIMIT)


def _sds(shape, dtype=F32):
    return jax.ShapeDtypeStruct(shape, dtype)


def _row_tile(seq, want):
    return want if seq % want == 0 else seq


def _rope_tables(posf, seq):
    half = ROPE_DIM // 2
    inv_freq = jnp.power(jnp.float32(ROPE_THETA), -jnp.arange(half, dtype=F32) / half)
    j = jnp.arange(LANES) % HEAD_DIM
    invf = jnp.where(j < ROPE_DIM, inv_freq[j % half], 0.0).astype(F32).reshape(1, LANES)
    m_p = ((j >= half) & (j < ROPE_DIM)).astype(F32).reshape(1, LANES)
    m_m = -(j < half).astype(F32).reshape(1, LANES)
    tm = _row_tile(seq, 512)

    def body(pos_ref, invf_ref, mp_ref, mm_ref, cos_ref, sp_ref, sm_ref):
        ang = pos_ref[...] * invf_ref[...]
        s = jnp.sin(ang)
        cos_ref[...] = jnp.cos(ang)
        sp_ref[...] = s * mp_ref[...]
        sm_ref[...] = s * mm_ref[...]

    return pl.pallas_call(
        body, name="rope_tables", grid=(seq // tm,),
        in_specs=[_rows(tm, 1), _full((1, LANES)), _full((1, LANES)), _full((1, LANES))],
        out_specs=[_rows(tm, LANES)] * 3, out_shape=[_sds((seq, LANES))] * 3,
        compiler_params=_params("parallel"),
    )(posf, invf, m_p, m_m)


def _even_proj(x, mod, w_in, tabs, seq):
    tm = _row_tile(seq, 512)

    def body(x_ref, mod_ref, w_ref, cos_ref, sp_ref, sm_ref, h_ref, q_ref, k_ref, v_ref, su_ref, sv_ref, g_ref):
        h = x_ref[...] * (1.0 + mod_ref[1:2, :]) + mod_ref[0:1, :]
        hb = h.astype(MXU_DTYPE)
        h_ref[...] = hb
        p = jnp.dot(hb, w_ref[...], preferred_element_type=F32)
        cos_t, sin_p, sin_m = cos_ref[...], sp_ref[...], sm_ref[...]
        for j in range(ATTN_WIDTH // LANES):
            q_ref[:, j * LANES:(j + 1) * LANES] = _rope(p[:, j * LANES:(j + 1) * LANES], cos_t, sin_p, sin_m).astype(q_ref.dtype)
        k_ref[...] = _rope(p[:, 512:640], cos_t, sin_p, sin_m).astype(k_ref.dtype)
        v_ref[...] = p[:, 640:768].astype(v_ref.dtype)
        su_ref[...] = p[:, 768:1280]
        sv_ref[...] = p[:, 1280:1792]
        g_ref[...] = p[:, 1792:2816]

    return pl.pallas_call(
        body, name="even_proj", grid=(seq // tm,),
        in_specs=[_rows(tm, D_MODEL), _full((3, D_MODEL)), _full((D_MODEL, EVEN_IN))] + [_rows(tm, LANES)] * 3,
        out_specs=[_rows(tm, D_MODEL), _rows(tm, 512), _rows(tm, LANES), _rows(tm, LANES), _rows(tm, 512), _rows(tm, 512),
                   _rows(tm, D_MODEL)],
        out_shape=[_sds((seq, D_MODEL), MXU_DTYPE), _sds((seq, 512), MXU_DTYPE), _sds((seq, LANES), MXU_DTYPE),
                   _sds((seq, LANES), MXU_DTYPE), _sds((seq, 512)), _sds((seq, 512)), _sds((seq, D_MODEL))],
        compiler_params=_params("parallel"),
    )(x, mod, w_in, *tabs)


def _sg_forward(sv, lng, lnb, sgw_ref, sgb, e2):
    vn, vhat, rstd, svo = [], [], [], []
    for j in range(SG_WIDTH // LANES):
        t = sv[:, j * LANES:(j + 1) * LANES]
        mu = _group_sum(t, e2) * (1.0 / HEAD_DIM)
        d = t - mu
        var = _group_sum(d * d, e2) * (1.0 / HEAD_DIM)
        r = lax.rsqrt(var + LN_EPS)
        vh = d * r
        vhat.append(vh)
        rstd.append(r)
        vn.append(vh * lng[:, j * LANES:(j + 1) * LANES] + lnb[:, j * LANES:(j + 1) * LANES])
    lane = _lane_iota((BLK, LANES))
    for j in range(SG_WIDTH // LANES):
        lo = _mm(sgw_ref[2 * j], vn[j])
        hi = _mm(sgw_ref[2 * j + 1], vn[j])
        svo.append(jnp.where(lane < HEAD_DIM, lo, hi) + sgb[:, j * LANES:(j + 1) * LANES])
    return svo, vn, vhat, rstd


def _even_mix(q, k, v, su, sv, g, x, mod, sink, sgln_g, sgln_b, sgw, sgb_full, e2, w_out, ln_g, ln_b, seq):
    nb = seq // BLK

    def body(sink_ref, q_ref, k_ref, v_ref, su_ref, sv_ref, g_ref, x_ref, mod_ref, lng_ref, lnb_ref, sgw_ref, sgb_ref,
             e2_ref, wo_ref, g1_ref, b1_ref, ycat_ref, lse_ref, out_ref, z_ref, x1_ref):
        n = pl.program_id(0)
        kband = _band(k_ref, n, nb)
        vband = _band(v_ref, n, nb)
        valid = _band_valid(n, seq)
        lane = _lane_iota((BLK, LANES))
        lse = jnp.zeros((BLK, LANES), F32)
        for j in range(ATTN_WIDTH // LANES):
            qt = q_ref[:, j * LANES:(j + 1) * LANES].astype(F32)
            acc = jnp.zeros((BLK, LANES), F32)
            for h in (2 * j, 2 * j + 1):
                qh = _to_kv_lanes(qt, h)
                s = jnp.where(valid, _mm_nt(qh, kband) * (HEAD_DIM ** -0.5), NEG_INF)
                m = jnp.maximum(jnp.max(s, axis=1, keepdims=True), sink_ref[h])
                p = jnp.exp(s - m)
                denom = jnp.sum(p, axis=1, keepdims=True) + jnp.exp(sink_ref[h] - m)
                acc = acc + _from_kv_lanes(_mm(p / denom, vband), h)
                lse = jnp.where(lane == h, m + jnp.log(denom), lse)
            ycat_ref[:, j * LANES:(j + 1) * LANES] = acc
        lse_ref[...] = lse
        svo, _, _, _ = _sg_forward(sv_ref[...], lng_ref[...], lnb_ref[...], sgw_ref, sgb_ref[...], e2_ref[...])
        for j in range(SG_WIDTH // LANES):
            ycat_ref[:, ATTN_WIDTH + j * LANES:ATTN_WIDTH + (j + 1) * LANES] = su_ref[:, j * LANES:(j + 1) * LANES] * svo[j]
        gg = g_ref[...]
        yg = ycat_ref[...] * (gg * _sigmoid(gg))
        out = _mm(yg, wo_ref[...])
        out_ref[...] = out
        z = ALPHA * x_ref[...] + mod_ref[2:3, :] * out
        z_ref[...] = z
        zhat, _ = _ln_stats(z)
        x1_ref[...] = zhat * g1_ref[...] + b1_ref[...]

    blk = lambda w: pl.BlockSpec((BLK, w), lambda n: (n, 0))
    return pl.pallas_call(
        body, name="even_mix", grid=(nb,),
        in_specs=[pl.BlockSpec(memory_space=pltpu.SMEM), blk(512), _full((seq, LANES)), _full((seq, LANES)), blk(512), blk(512),
                  blk(D_MODEL), blk(D_MODEL), _full((3, D_MODEL)), _full((1, 512)), _full((1, 512)), _full((8, BLK, BLK)),
                  _full((BLK, 512)), _full((LANES, LANES)), _full((D_MODEL, D_MODEL)), _full((1, D_MODEL)), _full((1, D_MODEL))],
        out_specs=[blk(D_MODEL), blk(LANES), blk(D_MODEL), blk(D_MODEL), blk(D_MODEL)],
        out_shape=[_sds((seq, D_MODEL)), _sds((seq, LANES)), _sds((seq, D_MODEL)), _sds((seq, D_MODEL)), _sds((seq, D_MODEL))],
        compiler_params=_params("parallel"),
    )(sink, q, k, v, su, sv, g, x, mod, sgln_g, sgln_b, sgw, sgb_full, e2, w_out, ln_g, ln_b)


def _odd_proj(x1, mod, w_in, seq):
    tm = _row_tile(seq, 512)

    def body(x_ref, mod_ref, w_ref, h_ref, xr_ref, g_ref):
        h = x_ref[...] * (1.0 + mod_ref[1:2, :]) + mod_ref[0:1, :]
        hb = h.astype(MXU_DTYPE)
        h_ref[...] = hb
        p = jnp.dot(hb, w_ref[...], preferred_element_type=F32)
        xr_ref[...] = p[:, :D_MODEL]
        g_ref[...] = p[:, D_MODEL:]

    return pl.pallas_call(
        body, name="odd_proj", grid=(seq // tm,),
        in_specs=[_rows(tm, D_MODEL), _full((3, D_MODEL)), _full((D_MODEL, ODD_IN))],
        out_specs=[_rows(tm, D_MODEL)] * 3,
        out_shape=[_sds((seq, D_MODEL), MXU_DTYPE), _sds((seq, D_MODEL)), _sds((seq, D_MODEL))],
        compiler_params=_params("parallel"),
    )(x1, mod, w_in)


def _halo_specs(tm, seq, width):
    per = tm // 8
    last = seq // 8 - 1
    return [pl.BlockSpec((8, width), lambda i: (jnp.maximum(i * per - 1, 0), 0)),
            pl.BlockSpec((tm, width), lambda i: (i, 0)),
            pl.BlockSpec((8, width), lambda i: (jnp.minimum((i + 1) * per, last), 0))]


def _extended(prev_ref, main_ref, next_ref, i, n_steps):
    prev = jnp.where(i > 0, prev_ref[...], 0.0)
    nxt = jnp.where(i < n_steps - 1, next_ref[...], 0.0)
    return jnp.concatenate([prev, main_ref[...], nxt], axis=0)


def _shifted(ext, off, tm):
    if off == 0:
        return ext[8:8 + tm]
    return pltpu.roll(ext, (-off) % ext.shape[0], 0)[8:8 + tm]


def _lru_gates(xh, pre, bias, sp, hs):
    res = []
    for d in range(2):
        r = _sigmoid(pre[:, (2 * d) * LANES:(2 * d + 1) * LANES] + bias[2 * d:2 * d + 1, hs])
        ig = _sigmoid(pre[:, (2 * d + 1) * LANES:(2 * d + 2) * LANES] + bias[2 * d + 1:2 * d + 2, hs])
        log_a = (-RG_LRU_C) * r * sp[d:d + 1, hs]
        a = jnp.exp(log_a)
        s = jnp.sqrt(-_expm1(2.0 * log_a))
        res.append((r, ig, a, s))
    return res


def _odd_gates(xr, conv_w, conv_b, wcat, bias, lam, seq):
    tm = _row_tile(seq, 512)
    steps = seq // tm

    def body(xp_ref, xm_ref, xn_ref, cw_ref, cb_ref, w_ref, bias_ref, lam_ref, xc_ref, af_ref, bf_ref, ar_ref, br_ref):
        i = pl.program_id(0)
        ext = _extended(xp_ref, xm_ref, xn_ref, i, steps)
        xc = cb_ref[...] + sum(cw_ref[kk:kk + 1, :] * _shifted(ext, kk - 2, tm) for kk in range(4))
        xc_ref[...] = xc
        sp, _ = _softplus_neg(lam_ref[...])
        bias = bias_ref[...]
        for h in range(RNN_HEADS):
            hs = slice(h * LANES, (h + 1) * LANES)
            xh = xc[:, hs]
            (_, i0, a0, s0), (_, i1, a1, s1) = _lru_gates(xh, _mm(xh, w_ref[h]), bias, sp, hs)
            af_ref[:, hs] = a0
            bf_ref[:, hs] = s0 * i0 * xh
            ar_ref[:, hs] = a1
            br_ref[:, hs] = s1 * i1 * xh

    return pl.pallas_call(
        body, name="odd_gates", grid=(steps,),
        in_specs=_halo_specs(tm, seq, D_MODEL) + [_full((4, D_MODEL)), _full((1, D_MODEL)), _full((8, LANES, 512)),
                                                  _full((4, D_MODEL)), _full((2, D_MODEL))],
        out_specs=[_rows(tm, D_MODEL)] * 5, out_shape=[_sds((seq, D_MODEL))] * 5,
        compiler_params=_params("parallel"),
    )(xr, xr, xr, conv_w, conv_b, wcat, bias, lam)


def _scan(a, b, seq, descending, post, name):
    tb = _row_tile(seq, 512)
    steps = seq // tb
    imap = (lambda i: (steps - 1 - i, 0)) if descending else (lambda i: (i, 0))
    spec = pl.BlockSpec((tb, D_MODEL), imap)
    n_out = 1 if post else 2

    def body(a_ref, b_ref, *rest):
        outs, carry = rest[:n_out], rest[n_out]

        @pl.when(pl.program_id(0) == 0)
        def _():
            carry[...] = jnp.zeros_like(carry)

        def step(j, c):
            t = (tb - 1 - j) if descending else j
            at = a_ref[pl.ds(t, 1), :]
            bt = b_ref[pl.ds(t, 1), :]
            if post:
                gcur = bt + c
                outs[0][pl.ds(t, 1), :] = gcur
                return at * gcur
            hcur = at * c + bt
            outs[0][pl.ds(t, 1), :] = hcur
            outs[1][pl.ds(t, 1), :] = c
            return hcur

        carry[...] = lax.fori_loop(0, tb, step, carry[...], unroll=8)

    return pl.pallas_call(
        body, name=name, grid=(steps,), in_specs=[spec, spec], out_specs=[spec] * n_out,
        out_shape=[_sds((seq, D_MODEL))] * n_out, scratch_shapes=[pltpu.VMEM((1, D_MODEL), F32)],
        compiler_params=_params("arbitrary"),
    )(a, b)


def _odd_out_and_loss(hf, hr, g, x1, tgt, mod, w_out, w_out_t, ln_g, ln_b, seq):
    tm = _row_tile(seq, 256)

    def body(hf_ref, hr_ref, g_ref, x_ref, t_ref, mod_ref, w_ref, wt_ref, lg_ref, lb_ref,
             dhs_ref, dg_ref, dres_ref, loss_ref, dw_ref, vec_ref):
        @pl.when(pl.program_id(0) == 0)
        def _():
            loss_ref[...] = jnp.zeros_like(loss_ref)
            dw_ref[...] = jnp.zeros_like(dw_ref)
            vec_ref[...] = jnp.zeros_like(vec_ref)

        gg = g_ref[...]
        sg = _sigmoid(gg)
        silu = gg * sg
        hsum = hf_ref[...] + hr_ref[...]
        y = hsum * silu
        out = _mm(y, w_ref[...])
        gate = mod_ref[2:3, :]
        z = ALPHA * x_ref[...] + gate * out
        zhat, rstd = _ln_stats(z)
        x2 = zhat * lg_ref[...] + lb_ref[...]
        err = x2 - t_ref[...]
        loss_ref[...] += 0.5 * jnp.sum(jnp.mean(err * err, axis=-1, keepdims=True))
        dx2 = err * (1.0 / D_MODEL)
        dz = _ln_bwd(dx2, zhat, rstd, lg_ref[...])
        vec_ref[0:1, :] += jnp.sum(dx2 * zhat, axis=0, keepdims=True)
        vec_ref[1:2, :] += jnp.sum(dx2, axis=0, keepdims=True)
        vec_ref[2:3, :] += jnp.sum(dz * out, axis=0, keepdims=True)
        dres_ref[...] = ALPHA * dz
        dout = gate * dz
        dw_ref[...] += _mm_tn(y, dout)
        dy = _mm(dout, wt_ref[...])
        dhs_ref[...] = dy * silu
        dg_ref[...] = dy * hsum * (sg * (1.0 + gg * (1.0 - sg)))

    return pl.pallas_call(
        body, name="odd_out_loss", grid=(seq // tm,),
        in_specs=[_rows(tm, D_MODEL)] * 5 + [_full((3, D_MODEL)), _full((D_MODEL, D_MODEL)), _full((D_MODEL, D_MODEL)),
                                             _full((1, D_MODEL)), _full((1, D_MODEL))],
        out_specs=[_rows(tm, D_MODEL)] * 3 + [_full((8, LANES)), _full((D_MODEL, D_MODEL)), _full((8, D_MODEL))],
        out_shape=[_sds((seq, D_MODEL))] * 3 + [_sds((8, LANES)), _sds((D_MODEL, D_MODEL)), _sds((8, D_MODEL))],
        compiler_params=_params("arbitrary"),
    )(hf, hr, g, x1, tgt, mod, w_out, w_out_t, ln_g, ln_b)


def _odd_gates_bwd(xc, gf, gr, hpf, hpr, wcat, bias, lam, seq):
    tm = _row_tile(seq, 512)
    steps = seq // tm

    def body(xc_ref, gf_ref, gr_ref, hpf_ref, hpr_ref, w_ref, bias_ref, lam_ref, dxc_ref, dw_ref, vec_ref):
        @pl.when(pl.program_id(0) == 0)
        def _():
            dw_ref[...] = jnp.zeros_like(dw_ref)
            vec_ref[...] = jnp.zeros_like(vec_ref)

        sp, dsp = _softplus_neg(lam_ref[...])
        bias = bias_ref[...]
        for h in range(RNN_HEADS):
            hs = slice(h * LANES, (h + 1) * LANES)
            xh = xc_ref[:, hs]
            gates = _lru_gates(xh, _mm(xh, w_ref[h]), bias, sp, hs)
            dxh = jnp.zeros_like(xh)
            dpre = []
            for d, (g_ref_d, hp_ref_d) in enumerate(((gf_ref, hpf_ref), (gr_ref, hpr_ref))):
                r, ig, a, s = gates[d]
                db = g_ref_d[:, hs]
                da = db * hp_ref_d[:, hs]
                dxh = dxh + db * s * ig
                dlog_a = da * a - (db * ig * xh) * (a * a / s)
                dr = dlog_a * (-RG_LRU_C) * sp[d:d + 1, hs]
                di = db * s * xh
                dpr = dr * r * (1.0 - r)
                dpi = di * ig * (1.0 - ig)
                vec_ref[2 * d:2 * d + 1, hs] += jnp.sum(dpr, axis=0, keepdims=True)
                vec_ref[2 * d + 1:2 * d + 2, hs] += jnp.sum(dpi, axis=0, keepdims=True)
                vec_ref[4 + d:5 + d, hs] += jnp.sum(dlog_a * r, axis=0, keepdims=True) * (-RG_LRU_C) * dsp[d:d + 1, hs]
                dpre += [dpr, dpi]
            dcat = jnp.concatenate(dpre, axis=1)
            dw_ref[h] += _mm_tn(xh, dcat)
            dxc_ref[:, hs] = dxh + _mm_nt(dcat, w_ref[h])

    return pl.pallas_call(
        body, name="odd_gates_bwd", grid=(steps,),
        in_specs=[_rows(tm, D_MODEL)] * 5 + [_full((8, LANES, 512)), _full((4, D_MODEL)), _full((2, D_MODEL))],
        out_specs=[_rows(tm, D_MODEL), _full((8, LANES, 512)), _full((8, D_MODEL))],
        out_shape=[_sds((seq, D_MODEL)), _sds((8, LANES, 512)), _sds((8, D_MODEL))],
        compiler_params=_params("arbitrary"),
    )(xc, gf, gr, hpf, hpr, wcat, bias, lam)


def _odd_proj_bwd(dxc, xr, dg, x1, dres, mod, conv_w, w_in_t, seq):
    tm = _row_tile(seq, 512)
    steps = seq // tm

    def body(dp_ref, dm_ref, dn_ref, xp_ref, xm_ref, xn_ref, dg_ref, x_ref, dres_ref, mod_ref, cw_ref, wt_ref,
             dx_ref, dpb_ref, vec_ref):
        i = pl.program_id(0)

        @pl.when(i == 0)
        def _():
            vec_ref[...] = jnp.zeros_like(vec_ref)

        dext = _extended(dp_ref, dm_ref, dn_ref, i, steps)
        xext = _extended(xp_ref, xm_ref, xn_ref, i, steps)
        dxc_m = dm_ref[...]
        dxr = sum(cw_ref[kk:kk + 1, :] * _shifted(dext, 2 - kk, tm) for kk in range(4))
        for kk in range(4):
            vec_ref[kk:kk + 1, :] += jnp.sum(dxc_m * _shifted(xext, kk - 2, tm), axis=0, keepdims=True)
        vec_ref[4:5, :] += jnp.sum(dxc_m, axis=0, keepdims=True)
        dpb_ref[:, :D_MODEL] = dxr.astype(dpb_ref.dtype)
        dpb_ref[:, D_MODEL:] = dg_ref[...].astype(dpb_ref.dtype)
        dh = jnp.dot(dpb_ref[...], wt_ref[...], preferred_element_type=F32)
        x = x_ref[...]
        vec_ref[5:6, :] += jnp.sum(dh, axis=0, keepdims=True)
        vec_ref[6:7, :] += jnp.sum(dh * x, axis=0, keepdims=True)
        dx_ref[...] = dres_ref[...] + dh * (1.0 + mod_ref[1:2, :])

    return pl.pallas_call(
        body, name="odd_proj_bwd", grid=(steps,),
        in_specs=_halo_specs(tm, seq, D_MODEL) + _halo_specs(tm, seq, D_MODEL) + [_rows(tm, D_MODEL)] * 3
        + [_full((3, D_MODEL)), _full((4, D_MODEL)), _full((ODD_IN, D_MODEL))],
        out_specs=[_rows(tm, D_MODEL), _rows(tm, ODD_IN), _full((8, D_MODEL))],
        out_shape=[_sds((seq, D_MODEL)), _sds((seq, ODD_IN), MXU_DTYPE), _sds((8, D_MODEL))],
        compiler_params=_params("arbitrary"),
    )(dxc, dxc, dxc, xr, xr, xr, dg, x1, dres, mod, conv_w, w_in_t)


def _tn_matmul(a, b, seq, name):
    n = b.shape[1]
    tn = n // 2
    tm = _row_tile(seq, 512)

    def body(a_ref, b_ref, o_ref):
        @pl.when(pl.program_id(1) == 0)
        def _():
            o_ref[...] = jnp.zeros_like(o_ref)

        o_ref[...] += lax.dot_general(a_ref[...], b_ref[...], (((0,), (0,)), ((), ())), preferred_element_type=F32)

    return pl.pallas_call(
        body, name=name, grid=(2, seq // tm),
        in_specs=[pl.BlockSpec((tm, D_MODEL), lambda j, i: (i, 0)), pl.BlockSpec((tm, tn), lambda j, i: (i, j))],
        out_specs=pl.BlockSpec((D_MODEL, tn), lambda j, i: (0, j)), out_shape=_sds((D_MODEL, n)),
        compiler_params=_params("parallel", "arbitrary"),
    )(a, b)


def _even_out_bwd(dx1, z, out, ycat, g, mod, ln_g, w_out_t, seq):
    tm = _row_tile(seq, 256)

    def body(dx_ref, z_ref, out_ref, y_ref, g_ref, mod_ref, lg_ref, wt_ref, dy_ref, dg_ref, dres_ref, dw_ref, vec_ref):
        @pl.when(pl.program_id(0) == 0)
        def _():
            dw_ref[...] = jnp.zeros_like(dw_ref)
            vec_ref[...] = jnp.zeros_like(vec_ref)

        zhat, rstd = _ln_stats(z_ref[...])
        dx1_ = dx_ref[...]
        dz = _ln_bwd(dx1_, zhat, rstd, lg_ref[...])
        vec_ref[0:1, :] += jnp.sum(dx1_ * zhat, axis=0, keepdims=True)
        vec_ref[1:2, :] += jnp.sum(dx1_, axis=0, keepdims=True)
        vec_ref[2:3, :] += jnp.sum(dz * out_ref[...], axis=0, keepdims=True)
        dres_ref[...] = ALPHA * dz
        dout = mod_ref[2:3, :] * dz
        gg = g_ref[...]
        sg = _sigmoid(gg)
        silu = gg * sg
        ycat_ = y_ref[...]
        dw_ref[...] += _mm_tn(ycat_ * silu, dout)
        dy = _mm(dout, wt_ref[...])
        dy_ref[...] = dy * silu
        dg_ref[...] = dy * ycat_ * (sg * (1.0 + gg * (1.0 - sg)))

    return pl.pallas_call(
        body, name="even_out_bwd", grid=(seq // tm,),
        in_specs=[_rows(tm, D_MODEL)] * 5 + [_full((3, D_MODEL)), _full((1, D_MODEL)), _full((D_MODEL, D_MODEL))],
        out_specs=[_rows(tm, D_MODEL)] * 3 + [_full((D_MODEL, D_MODEL)), _full((8, D_MODEL))],
        out_shape=[_sds((seq, D_MODEL))] * 3 + [_sds((D_MODEL, D_MODEL)), _sds((8, D_MODEL))],
        compiler_params=_params("arbitrary"),
    )(dx1, z, out, ycat, g, mod, ln_g, w_out_t)


def _even_mix_bwd(q, k, v, lse, ycat, dycat, su, sv, sink, sgln_g, sgln_b, sgw, sgb_full, e2, e8, seq):
    nb = seq // BLK

    def body(sink_ref, q_ref, k_ref, v_ref, lse_ref, y_ref, dy_ref, su_ref, sv_ref, lng_ref, lnb_ref, sgw_ref, sgb_ref, e2_ref,
             e8_ref, dq_ref, dsu_ref, dsv_ref, dk_ref, dv_ref, dsgw_ref, dsgb_ref, vec_ref, dsink_ref, dsgb_acc):
        n = pl.program_id(0)

        @pl.when(n == 0)
        def _():
            dk_ref[...] = jnp.zeros_like(dk_ref)
            dv_ref[...] = jnp.zeros_like(dv_ref)
            dsgw_ref[...] = jnp.zeros_like(dsgw_ref)
            dsgb_acc[...] = jnp.zeros_like(dsgb_acc)
            vec_ref[...] = jnp.zeros_like(vec_ref)
            dsink_ref[...] = jnp.zeros_like(dsink_ref)

        kband = _band(k_ref, n, nb)
        vband = _band(v_ref, n, nb)
        valid = _band_valid(n, seq)
        lane = _lane_iota((BLK, LANES))
        row8 = lax.broadcasted_iota(jnp.int32, (8, LANES), 0)
        lse = lse_ref[...]
        dkb = jnp.zeros((3 * BLK, LANES), F32)
        dvb = jnp.zeros((3 * BLK, LANES), F32)
        dsink = jnp.zeros((8, LANES), F32)
        for j in range(ATTN_WIDTH // LANES):
            qt = q_ref[:, j * LANES:(j + 1) * LANES].astype(F32)
            ot = y_ref[:, j * LANES:(j + 1) * LANES]
            dot_ = dy_ref[:, j * LANES:(j + 1) * LANES]
            dqt = jnp.zeros((BLK, LANES), F32)
            for h in (2 * j, 2 * j + 1):
                head_lanes = (lane < HEAD_DIM) if h % 2 == 0 else (lane >= HEAD_DIM)
                qh = _to_kv_lanes(qt, h)
                doh = _to_kv_lanes(dot_, h)
                lse_h = jnp.sum(jnp.where(lane == h, lse, 0.0), axis=1, keepdims=True)
                s = jnp.where(valid, _mm_nt(qh, kband) * (HEAD_DIM ** -0.5), NEG_INF)
                p = jnp.exp(s - lse_h)
                psink = jnp.exp(sink_ref[h] - lse_h)
                delta = jnp.sum(jnp.where(head_lanes, dot_ * ot, 0.0), axis=1, keepdims=True)
                ds = p * (_mm_nt(doh, vband) - delta) * (HEAD_DIM ** -0.5)
                dsink = dsink + jnp.where(row8 == h, -jnp.sum(psink * delta), 0.0)
                dqt = dqt + _from_kv_lanes(_mm(ds, kband), h)
                dkb = dkb + _mm_tn(ds, qh)
                dvb = dvb + _mm_tn(p, doh)
            dq_ref[:, j * LANES:(j + 1) * LANES] = dqt
        dsink_ref[...] += dsink
        prev = jnp.maximum(n - 1, 0)
        nxt = jnp.minimum(n + 1, nb - 1)
        for part, blk_i in enumerate((prev, n, nxt)):
            rows = pl.ds(pl.multiple_of(blk_i * BLK, BLK), BLK)
            dk_ref[rows, :] += dkb[part * BLK:(part + 1) * BLK]
            dv_ref[rows, :] += dvb[part * BLK:(part + 1) * BLK]

        e2 = e2_ref[...]
        lng = lng_ref[...]
        svo, vn, vhat, rstd = _sg_forward(sv_ref[...], lng, lnb_ref[...], sgw_ref, sgb_ref[...], e2)
        for j in range(SG_WIDTH // LANES):
            cs = slice(j * LANES, (j + 1) * LANES)
            dysg = dy_ref[:, ATTN_WIDTH + j * LANES:ATTN_WIDTH + (j + 1) * LANES]
            dsu_ref[:, cs] = dysg * svo[j]
            dsvo = dysg * su_ref[:, cs]
            dsgb_acc[:, cs] += dsvo
            d_lo = jnp.where(lane < HEAD_DIM, dsvo, 0.0)
            d_hi = dsvo - d_lo
            dsgw_ref[2 * j] += _mm_nt(d_lo, vn[j])
            dsgw_ref[2 * j + 1] += _mm_nt(d_hi, vn[j])
            dvn = _mm_tn(sgw_ref[2 * j], d_lo) + _mm_tn(sgw_ref[2 * j + 1], d_hi)
            vec_ref[0:1, cs] += jnp.sum(dvn * vhat[j], axis=0, keepdims=True)
            vec_ref[1:2, cs] += jnp.sum(dvn, axis=0, keepdims=True)
            dvh = dvn * lng[:, cs]
            m1 = _group_sum(dvh, e2) * (1.0 / HEAD_DIM)
            m2 = _group_sum(dvh * vhat[j], e2) * (1.0 / HEAD_DIM)
            dsv_ref[:, cs] = rstd[j] * (dvh - m1 - vhat[j] * m2)

        @pl.when(n == nb - 1)
        def _():
            rest = dsgb_acc[...]
            total = jnp.zeros((8, BLK), F32)
            for _ in range(3):
                part = rest.astype(MXU_DTYPE)
                total = total + lax.dot_general(e8_ref[...], part, (((1,), (1,)), ((), ())), preferred_element_type=F32)
                rest = rest - part.astype(F32)
            dsgb_ref[...] = total

    blk = lambda w: pl.BlockSpec((BLK, w), lambda n: (n, 0))
    return pl.pallas_call(
        body, name="even_mix_bwd", grid=(nb,),
        in_specs=[pl.BlockSpec(memory_space=pltpu.SMEM), blk(512), _full((seq, LANES)), _full((seq, LANES)), blk(LANES),
                  blk(D_MODEL), blk(D_MODEL), blk(512), blk(512), _full((1, 512)), _full((1, 512)), _full((8, BLK, BLK)),
                  _full((BLK, 512)), _full((LANES, LANES)), _full((8, 512))],
        out_specs=[blk(512), blk(512), blk(512), _full((seq, LANES)), _full((seq, LANES)), _full((8, BLK, BLK)),
                   _full((8, BLK)), _full((8, 512)), _full((8, LANES))],
        out_shape=[_sds((seq, 512)), _sds((seq, 512)), _sds((seq, 512)), _sds((seq, LANES)), _sds((seq, LANES)),
                   _sds((8, BLK, BLK)), _sds((8, BLK)), _sds((8, 512)), _sds((8, LANES))],
        scratch_shapes=[pltpu.VMEM((BLK, 512), F32)],
        compiler_params=_params("arbitrary"),
    )(sink, q, k, v, lse, ycat, dycat, su, sv, sgln_g, sgln_b, sgw, sgb_full, e2, e8)


def _even_proj_bwd(dq, dk, dv, dsu, dsv, dg, x, dres, mod, tabs, w_in_t, seq):
    tm = _row_tile(seq, 512)

    def body(dq_ref, dk_ref, dv_ref, dsu_ref, dsv_ref, dg_ref, x_ref, dres_ref, mod_ref, cos_ref, sp_ref, sm_ref, wt_ref,
             dx_ref, dpb_ref, vec_ref):
        @pl.when(pl.program_id(0) == 0)
        def _():
            vec_ref[...] = jnp.zeros_like(vec_ref)

        cos_t, sin_p, sin_m = cos_ref[...], sp_ref[...], sm_ref[...]
        dt = dpb_ref.dtype
        for j in range(ATTN_WIDTH // LANES):
            cs = slice(j * LANES, (j + 1) * LANES)
            dpb_ref[:, cs] = _rope_t(dq_ref[:, cs], cos_t, sin_p, sin_m).astype(dt)
        dpb_ref[:, 512:640] = _rope_t(dk_ref[...], cos_t, sin_p, sin_m).astype(dt)
        dpb_ref[:, 640:768] = dv_ref[...].astype(dt)
        dpb_ref[:, 768:1280] = dsu_ref[...].astype(dt)
        dpb_ref[:, 1280:1792] = dsv_ref[...].astype(dt)
        dpb_ref[:, 1792:2816] = dg_ref[...].astype(dt)
        dh = jnp.dot(dpb_ref[...], wt_ref[...], preferred_element_type=F32)
        x_ = x_ref[...]
        vec_ref[0:1, :] += jnp.sum(dh, axis=0, keepdims=True)
        vec_ref[1:2, :] += jnp.sum(dh * x_, axis=0, keepdims=True)
        dx_ref[...] = dres_ref[...] + dh * (1.0 + mod_ref[1:2, :])

    return pl.pallas_call(
        body, name="even_proj_bwd", grid=(seq // tm,),
        in_specs=[_rows(tm, 512), _rows(tm, LANES), _rows(tm, LANES), _rows(tm, 512), _rows(tm, 512), _rows(tm, D_MODEL),
                  _rows(tm, D_MODEL), _rows(tm, D_MODEL), _full((3, D_MODEL))] + [_rows(tm, LANES)] * 3
        + [_full((EVEN_IN, D_MODEL))],
        out_specs=[_rows(tm, D_MODEL), _rows(tm, EVEN_IN), _full((8, D_MODEL))],
        out_shape=[_sds((seq, D_MODEL)), _sds((seq, EVEN_IN), MXU_DTYPE), _sds((8, D_MODEL))],
        compiler_params=_params("arbitrary"),
    )(dq, dk, dv, dsu, dsv, dg, x, dres, mod, *tabs, w_in_t)


def _local_step(x, posf, tgt, mod, w, seq):
    mxu = lambda a: a.astype(MXU_DTYPE)
    row = lambda a: a.reshape(1, -1)
    tabs = _rope_tables(posf, seq)
    e2 = mxu(jnp.kron(jnp.eye(2, dtype=F32), jnp.ones((HEAD_DIM, HEAD_DIM), F32)))
    e8 = mxu(jnp.repeat(jnp.eye(N_SG_GROUPS, dtype=F32), HEAD_DIM, axis=1))
    sgw = mxu(w["ev_sg_w"])
    sgb_full = jnp.repeat(w["ev_sg_b"].T, HEAD_DIM, axis=1)
    sgln_g, sgln_b = row(w["ev_sg_ln_g"]), row(w["ev_sg_ln_b"])
    sink = w["ev_sink"].reshape(N_Q_HEADS)
    ev_w_in, ev_w_out = mxu(w["ev_w_in"]), mxu(w["ev_w_out"])
    od_w_in, od_w_out = mxu(w["od_w_in"]), mxu(w["od_w_out"])
    wcat = mxu(jnp.concatenate([w["od_w_a"][0], w["od_w_x"][0], w["od_w_a"][1], w["od_w_x"][1]], axis=2))
    gate_bias = jnp.stack([w["od_b_a"][0], w["od_b_x"][0], w["od_b_a"][1], w["od_b_x"][1]])
    conv_b = row(w["od_conv_b"])
    ln_g, ln_b = w["ln_g"], w["ln_b"]

    h0, q, k, v, su, sv, g0 = _even_proj(x, mod[0], ev_w_in, tabs, seq)
    ycat, lse, out0, z0, x1 = _even_mix(q, k, v, su, sv, g0, x, mod[0], sink, sgln_g, sgln_b, sgw, sgb_full, e2, ev_w_out,
                                        ln_g[0:1], ln_b[0:1], seq)
    h1, xr, g1 = _odd_proj(x1, mod[1], od_w_in, seq)
    xc, a_f, b_f, a_r, b_r = _odd_gates(xr, w["od_conv_w"], conv_b, wcat, gate_bias, w["od_lam"], seq)
    hf, hpf = _scan(a_f, b_f, seq, descending=False, post=False, name="scan_fwd")
    hr, hpr = _scan(a_r, b_r, seq, descending=True, post=False, name="scan_rev")
    dhs, dg1, dres1, loss, d_od_w_out, vec_o = _odd_out_and_loss(hf, hr, g1, x1, tgt, mod[1], od_w_out, od_w_out.T,
                                                                   ln_g[1:2], ln_b[1:2], seq)
    (gf,) = _scan(a_f, dhs, seq, descending=True, post=True, name="scan_fwd_bwd")
    (gr,) = _scan(a_r, dhs, seq, descending=False, post=True, name="scan_rev_bwd")
    dxc, d_wcat, vec_g = _odd_gates_bwd(xc, gf, gr, hpf, hpr, wcat, gate_bias, w["od_lam"], seq)
    dx1, dp1, vec_p = _odd_proj_bwd(dxc, xr, dg1, x1, dres1, mod[1], w["od_conv_w"], od_w_in.T, seq)
    d_od_w_in = _tn_matmul(h1, dp1, seq, "odd_dw_in")
    dycat, dg0, dres0, d_ev_w_out, vec_e = _even_out_bwd(dx1, z0, out0, ycat, g0, mod[0], ln_g[0:1], ev_w_out.T, seq)
    dq, dsu, dsv, dk, dv, d_sgw, d_sgb, vec_s, d_sink = _even_mix_bwd(q, k, v, lse, ycat, dycat, su, sv, sink, sgln_g, sgln_b,
                                                                      sgw, sgb_full, e2, e8, seq)
    grad_x, dp0, vec_x = _even_proj_bwd(dq, dk, dv, dsu, dsv, dg0, x, dres0, mod[0], tabs, ev_w_in.T, seq)
    d_ev_w_in = _tn_matmul(h0, dp0, seq, "even_dw_in")

    dmod = jnp.stack([jnp.stack([vec_x[0], vec_x[1], vec_e[2]]), jnp.stack([vec_p[5], vec_p[6], vec_o[2]])])
    grads = {
        "ln_g": jnp.stack([vec_e[0], vec_o[0]]), "ln_b": jnp.stack([vec_e[1], vec_o[1]]),
        "ev_w_in": d_ev_w_in, "ev_w_out": d_ev_w_out, "ev_sink": d_sink[:, 0],
        "ev_sg_ln_g": vec_s[0], "ev_sg_ln_b": vec_s[1], "ev_sg_w": d_sgw,
        "ev_sg_b": d_sgb,
        "od_w_in": d_od_w_in, "od_conv_w": vec_p[0:4], "od_conv_b": vec_p[4],
        "od_w_a": jnp.stack([d_wcat[:, :, 0:128], d_wcat[:, :, 256:384]]),
        "od_w_x": jnp.stack([d_wcat[:, :, 128:256], d_wcat[:, :, 384:512]]),
        "od_b_a": jnp.stack([vec_g[0], vec_g[2]]), "od_b_x": jnp.stack([vec_g[1], vec_g[3]]),
        "od_lam": vec_g[4:6], "od_w_out": d_od_w_out,
    }
    return loss[0, 0], grad_x, dmod, grads


def _place():
    return lax.axis_index("x"), lax.axis_index("y"), lax.axis_index("c")


def _allgather8(block, name):
    m_per, n = block.shape

    def body(x_ref, out_ref, send_sems, recv_sems, local_sem):
        x, y, c = _place()
        me, sibling = (x, y, c), (x, y, 1 - c)
        chips = [(1 - x, y), (x, 1 - y), (1 - x, 1 - y)]

        def rows(px, py, pc):
            return out_ref.at[pl.ds((4 * px + 2 * py + pc) * m_per, m_per), :]

        def copy(k, blk, to, src=None):
            return pltpu.make_async_remote_copy(src_ref=rows(*blk) if src is None else src, dst_ref=rows(*blk),
                                                send_sem=send_sems.at[k], recv_sem=recv_sems.at[k], device_id=to,
                                                device_id_type=MESH)

        mine = pltpu.make_async_copy(x_ref, rows(*me), local_sem)
        mine.start()
        first = [copy(0, me, sibling, src=x_ref)] + [copy(1 + j, me, (*chip, c), src=x_ref) for j, chip in enumerate(chips)]
        for cp in first:
            cp.start()
        passed = [copy(4 + j, (*chip, c), sibling) for j, chip in enumerate(chips)]
        for j, chip in enumerate(chips):
            copy(1 + j, (*chip, c), me).wait_recv()
            passed[j].start()
        copy(0, sibling, me).wait_recv()
        for j, chip in enumerate(chips):
            copy(4 + j, (*chip, 1 - c), me).wait_recv()
        for cp in first + passed:
            cp.wait_send()
        mine.wait()

    return pl.pallas_call(
        body, name=name, out_shape=_sds((8 * m_per, n), block.dtype),
        in_specs=[pl.BlockSpec(memory_space=pltpu.VMEM)], out_specs=pl.BlockSpec(memory_space=pltpu.VMEM),
        scratch_shapes=[pltpu.SemaphoreType.DMA((7,)), pltpu.SemaphoreType.DMA((7,)), pltpu.SemaphoreType.DMA],
        compiler_params=pltpu.CompilerParams(vmem_limit_bytes=VMEM_LIMIT),
    )(block)


def _sibling_swap(send, name):
    def body(src_ref, dst_ref, send_sem, recv_sem):
        x, y, c = _place()
        cp = pltpu.make_async_remote_copy(src_ref=src_ref, dst_ref=dst_ref, send_sem=send_sem, recv_sem=recv_sem,
                                          device_id=(x, y, 1 - c), device_id_type=MESH)
        cp.start()
        cp.wait()

    return pl.pallas_call(
        body, name=name, out_shape=_sds(send.shape, send.dtype),
        in_specs=[pl.BlockSpec(memory_space=pl.ANY)], out_specs=pl.BlockSpec(memory_space=pl.ANY),
        scratch_shapes=[pltpu.SemaphoreType.DMA, pltpu.SemaphoreType.DMA],
    )(send)


def _chip_exchange(parts, name):
    def body(src_ref, dst_ref, send_sems, recv_sems, local_sem):
        x, y, c = _place()
        mine = 2 * x + y
        chips = [(1 - x, y), (x, 1 - y), (1 - x, 1 - y)]
        local = pltpu.make_async_copy(src_ref.at[mine], dst_ref.at[mine], local_sem)
        local.start()
        sends = [pltpu.make_async_remote_copy(src_ref=src_ref.at[2 * px + py], dst_ref=dst_ref.at[mine], send_sem=send_sems.at[j],
                                              recv_sem=recv_sems.at[j], device_id=(px, py, c), device_id_type=MESH)
                 for j, (px, py) in enumerate(chips)]
        for cp in sends:
            cp.start()
        for j, (px, py) in enumerate(chips):
            pltpu.make_async_remote_copy(src_ref=src_ref.at[mine], dst_ref=dst_ref.at[2 * px + py], send_sem=send_sems.at[j],
                                         recv_sem=recv_sems.at[j], device_id=(px, py, c), device_id_type=MESH).wait_recv()
        for cp in sends:
            cp.wait_send()
        local.wait()

    return pl.pallas_call(
        body, name=name, out_shape=_sds(parts.shape, parts.dtype),
        in_specs=[pl.BlockSpec(memory_space=pl.ANY)], out_specs=pl.BlockSpec(memory_space=pl.ANY),
        scratch_shapes=[pltpu.SemaphoreType.DMA((3,)), pltpu.SemaphoreType.DMA((3,)), pltpu.SemaphoreType.DMA],
    )(parts)


def _sibling_gather(block, name):
    def body(src_ref, dst_ref, send_sem, recv_sem, local_sem):
        x, y, c = _place()
        local = pltpu.make_async_copy(src_ref, dst_ref.at[c], local_sem)
        local.start()
        cp = pltpu.make_async_remote_copy(src_ref=src_ref, dst_ref=dst_ref.at[c], send_sem=send_sem, recv_sem=recv_sem,
                                          device_id=(x, y, 1 - c), device_id_type=MESH)
        cp.start()
        pltpu.make_async_remote_copy(src_ref=src_ref, dst_ref=dst_ref.at[1 - c], send_sem=send_sem, recv_sem=recv_sem,
                                     device_id=(x, y, 1 - c), device_id_type=MESH).wait_recv()
        cp.wait_send()
        local.wait()

    return pl.pallas_call(
        body, name=name, out_shape=_sds((2,) + block.shape, block.dtype),
        in_specs=[pl.BlockSpec(memory_space=pl.ANY)], out_specs=pl.BlockSpec(memory_space=pl.ANY),
        scratch_shapes=[pltpu.SemaphoreType.DMA, pltpu.SemaphoreType.DMA, pltpu.SemaphoreType.DMA],
    )(block)


def _add2(a, b, name):
    rows, n = a.shape
    tr = rows // 8

    def body(a_ref, b_ref, o_ref):
        o_ref[...] = a_ref[...] + b_ref[...]

    return pl.pallas_call(body, name=name, grid=(8,), in_specs=[_rows(tr, n)] * 2, out_specs=_rows(tr, n),
                          out_shape=_sds((rows, n)), compiler_params=_params("parallel"))(a, b)


def _sum_slots(parts, name):
    k, rows, n = parts.shape
    tr = rows // 2 if rows % 16 == 0 else rows

    def body(p_ref, o_ref):
        acc = p_ref[0]
        for j in range(1, k):
            acc = acc + p_ref[j]
        o_ref[...] = acc

    return pl.pallas_call(body, name=name, grid=(rows // tr,), in_specs=[pl.BlockSpec((k, tr, n), lambda i: (0, i, 0))],
                          out_specs=_rows(tr, n), out_shape=_sds((rows, n)), compiler_params=_params("parallel"))(parts)


def _modulation(c_all, ada_w, ada_b):
    cols = ada_w.shape[2]

    def body(c_ref, w_ref, b_ref, o_ref):
        cc = c_ref[...]
        o_ref[0] = _mm(cc * _sigmoid(cc), w_ref[0]) + b_ref[0]

    return pl.pallas_call(
        body, name="modulation", grid=(2,),
        in_specs=[_full((8, D_MODEL)), pl.BlockSpec((1, D_MODEL, cols), lambda l: (l, 0, 0)), pl.BlockSpec((1, 1, cols), lambda l: (l, 0, 0))],
        out_specs=pl.BlockSpec((1, 8, cols), lambda l: (l, 0, 0)), out_shape=_sds((2, 8, cols)),
        compiler_params=_params("parallel"),
    )(c_all, ada_w, ada_b)


def _adamw_math(w, g, m, v):
    m = ADAM_B1 * m + (1.0 - ADAM_B1) * g
    v = ADAM_B2 * v + (1.0 - ADAM_B2) * (g * g)
    m_hat = m / (1.0 - ADAM_B1 ** ADAM_STEP)
    v_hat = v / (1.0 - ADAM_B2 ** ADAM_STEP)
    delta = -ADAM_LR * (m_hat / (jnp.sqrt(v_hat) + ADAM_EPS) + ADAM_WD * w)
    return delta, m, v


def _ada_update(c_all, dmod, w, m, v):
    cols = w.shape[2]
    tr = 256
    spec3 = pl.BlockSpec((1, tr, cols), lambda l, i: (l, i, 0))

    def body(c_ref, d_ref, w_ref, m_ref, v_ref, g_ref, dl_ref, nm_ref, nv_ref):
        cc = c_ref[...]
        g = _mm_tn(cc * _sigmoid(cc), d_ref[0])
        g_ref[0] = g
        dl_ref[0], nm_ref[0], nv_ref[0] = _adamw_math(w_ref[0], g, m_ref[0], v_ref[0])

    return pl.pallas_call(
        body, name="ada_update", grid=(2, D_MODEL // tr),
        in_specs=[pl.BlockSpec((8, tr), lambda l, i: (0, i)), pl.BlockSpec((1, 8, cols), lambda l, i: (l, 0, 0)), spec3, spec3, spec3],
        out_specs=[spec3] * 4, out_shape=[_sds(w.shape)] * 4, compiler_params=_params("parallel", "parallel"),
    )(c_all, dmod, w, m, v)


def _adamw(w, g, m, v, name):
    rows, n = w.shape
    tr = next(t for t in (256, 128, 64, 32, 16, 8, rows) if rows % t == 0)

    def body(w_ref, g_ref, m_ref, v_ref, dl_ref, nm_ref, nv_ref):
        dl_ref[...], nm_ref[...], nv_ref[...] = _adamw_math(w_ref[...], g_ref[...], m_ref[...], v_ref[...])

    return pl.pallas_call(body, name=name, grid=(rows // tr,), in_specs=[_rows(tr, n)] * 4, out_specs=[_rows(tr, n)] * 3,
                          out_shape=[_sds((rows, n))] * 3, compiler_params=_params("parallel"))(w, g, m, v)


def _adamw_small(params):
    n = len(params)

    def body(*refs):
        ins, outs = refs[:4 * n], refs[4 * n:]
        for j in range(n):
            w_ref, g_ref, m_ref, v_ref = ins[4 * j:4 * j + 4]
            outs[3 * j][...], outs[3 * j + 1][...], outs[3 * j + 2][...] = _adamw_math(w_ref[...], g_ref[...], m_ref[...], v_ref[...])

    flat = [a for p in params for a in p]
    res = pl.pallas_call(body, name="adamw_small", out_shape=[_sds(p[0].shape) for p in params for _ in range(3)])(*flat)
    return [tuple(res[3 * j:3 * j + 3]) for j in range(n)]


SHARD_SIZES = (512 * 704, 128 * 1024, 512 * 512, 128 * 1024)
SHARD_ROWS = sum(SHARD_SIZES) // D_MODEL
REPL_SIZES = (16384, 32768, 32768)
REPL_ROWS = sum(REPL_SIZES) // D_MODEL


def _cols(a, start, size):
    return lax.dynamic_slice_in_dim(a, start, size, axis=a.ndim - 1)


def kernel(x, c, positions, ada_w, ada_b, ln_g, ln_b, ev_w_in, ev_w_out, ev_sink, ev_sg_ln_g, ev_sg_ln_b, ev_sg_w, ev_sg_b, od_w_in, od_conv_w, od_conv_b, od_w_a, od_b_a, od_w_x, od_b_x, od_lam, od_w_out, loss_target, m_ada_w, m_ada_b, m_ln_g, m_ln_b, m_ev_w_in, m_ev_w_out, m_ev_sink, m_ev_sg_ln_g, m_ev_sg_ln_b, m_ev_sg_w, m_ev_sg_b, m_od_w_in, m_od_conv_w, m_od_conv_b, m_od_w_a, m_od_b_a, m_od_w_x, m_od_b_x, m_od_lam, m_od_w_out, v_ada_w, v_ada_b, v_ln_g, v_ln_b, v_ev_w_in, v_ev_w_out, v_ev_sink, v_ev_sg_ln_g, v_ev_sg_ln_b, v_ev_sg_w, v_ev_sg_b, v_od_w_in, v_od_conv_w, v_od_conv_b, v_od_w_a, v_od_b_a, v_od_w_x, v_od_b_x, v_od_lam, v_od_w_out):
    seq = x.shape[1]
    px, py, pc = _place()
    chip = 2 * px + py
    dev = 2 * chip + pc

    small = jnp.concatenate([od_conv_w[0].reshape(-1), od_conv_b[0], od_b_a[0].reshape(-1), jnp.zeros((256,), F32),
                             od_b_x[0].reshape(-1), od_lam[0].reshape(-1)]).reshape(3, D_MODEL)
    blk = jnp.concatenate([c, small, jnp.zeros((4, D_MODEL), F32)], axis=0)
    g_small = _allgather8(blk, "gather_small").reshape(8, 8, D_MODEL)
    c_all = g_small[:, 0, :]
    per_chip = g_small[0::2]
    conv_w = per_chip[:, 1].reshape(4, 4, 256).transpose(1, 0, 2).reshape(4, D_MODEL)
    conv_b = per_chip[:, 2, 0:256].reshape(D_MODEL)
    b_a = per_chip[:, 2, 256:768].reshape(4, 2, 256).transpose(1, 0, 2).reshape(2, D_MODEL)
    b_x = per_chip[:, 3, 0:512].reshape(4, 2, 256).transpose(1, 0, 2).reshape(2, D_MODEL)
    lam = per_chip[:, 3, 512:1024].reshape(4, 2, 256).transpose(1, 0, 2).reshape(2, D_MODEL)

    half = lambda a: lax.dynamic_slice_in_dim(a, pc * (a.shape[0] // 2), a.shape[0] // 2, axis=0).reshape(-1)
    wpack = jnp.concatenate([half(ev_w_in[0]), half(ev_w_out[0]), half(od_w_in[0]), half(od_w_out[0])]).astype(MXU_DTYPE)
    wall = _allgather8(wpack.reshape(SHARD_ROWS, D_MODEL), "gather_weights").reshape(4, 2, SHARD_ROWS * D_MODEL)
    o1, o2, o3 = SHARD_SIZES[0], SHARD_SIZES[0] + SHARD_SIZES[1], SHARD_SIZES[0] + SHARD_SIZES[1] + SHARD_SIZES[2]
    w_full = {
        "ev_w_in": wall[:, :, :o1].reshape(4, D_MODEL, 704).transpose(1, 0, 2).reshape(D_MODEL, EVEN_IN),
        "ev_w_out": wall[:, :, o1:o2].reshape(D_MODEL, D_MODEL),
        "od_w_in": wall[:, :, o2:o3].reshape(4, D_MODEL, 512).transpose(1, 0, 2).reshape(D_MODEL, ODD_IN),
        "od_w_out": wall[:, :, o3:].reshape(D_MODEL, D_MODEL),
        "ev_sink": ev_sink[0], "ev_sg_ln_g": ev_sg_ln_g[0], "ev_sg_ln_b": ev_sg_ln_b[0], "ev_sg_w": ev_sg_w[0],
        "ev_sg_b": ev_sg_b[0], "od_conv_w": conv_w, "od_conv_b": conv_b, "od_w_a": od_w_a[0], "od_b_a": b_a,
        "od_w_x": od_w_x[0], "od_b_x": b_x, "od_lam": lam, "ln_g": ln_g, "ln_b": ln_b,
    }

    ada_cols = ada_w.shape[2]
    mod_sh = _modulation(c_all, ada_w, _cols(ada_b, chip * ada_cols, ada_cols).reshape(2, 1, ada_cols))
    mod_all = _allgather8(mod_sh.reshape(16, ada_cols), "gather_mod").reshape(4, 2, 2, 8, ada_cols)[:, 0]
    mod_mine = lax.dynamic_index_in_dim(mod_all, dev, axis=2, keepdims=False)
    mod = mod_mine.transpose(1, 0, 2).reshape(2, 3, D_MODEL)

    posf = positions.astype(F32).reshape(seq, 1)
    loss_local, grad_x, dmod, g = _local_step(x[0], posf, loss_target[0], mod, w_full, seq)
    loss = lax.psum(loss_local, ("x", "y", "c"))

    pad = lambda a, n: jnp.concatenate([a.reshape(-1), jnp.zeros((n - a.size,), F32)])
    rows_small = jnp.concatenate([
        dmod.reshape(6, D_MODEL), g["ln_g"][0:1], g["ln_b"][0:1], g["ln_g"][1:2], g["ln_b"][1:2],
        jnp.concatenate([g["ev_sg_ln_g"], g["ev_sg_ln_b"]]).reshape(1, D_MODEL), g["ev_sg_b"].reshape(1, D_MODEL),
        g["od_conv_w"], g["od_conv_b"].reshape(1, D_MODEL), g["od_b_a"], g["od_b_x"], g["od_lam"],
        pad(g["ev_sink"], D_MODEL).reshape(1, D_MODEL), jnp.zeros((8, D_MODEL), F32)], axis=0)
    small_all = _allgather8(rows_small, "gather_small_grads").reshape(8, 32, D_MODEL)
    gs = _sum_slots(small_all, "sum_small_grads")
    dmod_all = small_all[:, 0:6].reshape(8, 2, 3 * D_MODEL)
    dmod_sh = _cols(dmod_all, chip * ada_cols, ada_cols).transpose(1, 0, 2)
    g_ada_w, d_ada_w, nm_ada_w, nv_ada_w = _ada_update(c_all, dmod_sh, ada_w, m_ada_w, v_ada_w)

    shard_major = lambda a, n: a.reshape(D_MODEL, 4, n).transpose(1, 0, 2).reshape(4, 2, -1)
    gpack = jnp.concatenate([
        shard_major(g["ev_w_in"], 704), g["ev_w_out"].reshape(4, 2, -1), shard_major(g["od_w_in"], 512),
        g["od_w_out"].reshape(4, 2, -1), g["ev_sg_w"].reshape(4, 2, -1), g["od_w_a"].reshape(4, 2, -1),
        g["od_w_x"].reshape(4, 2, -1)], axis=2).reshape(4, 2, SHARD_ROWS + REPL_ROWS, D_MODEL)
    rows_all = SHARD_ROWS + REPL_ROWS
    keep = lax.dynamic_index_in_dim(gpack, pc, axis=1, keepdims=False)
    send = lax.dynamic_index_in_dim(gpack, 1 - pc, axis=1, keepdims=False)
    got = _sibling_swap(send, "reduce_sibling")
    chip_sum = _add2(keep.reshape(4 * rows_all, D_MODEL), got.reshape(4 * rows_all, D_MODEL), "sum_sibling")
    slots = _chip_exchange(chip_sum.reshape(4, rows_all, D_MODEL), "reduce_chips")
    mine = _sum_slots(slots, "sum_chips")
    shard = _sibling_gather(mine[:SHARD_ROWS], "gather_shard").reshape(2, SHARD_ROWS * D_MODEL)
    repl = _allgather8(mine[SHARD_ROWS:], "gather_replicated").reshape(8, REPL_ROWS * D_MODEL)
    g_ev_w_in = shard[:, :o1].reshape(D_MODEL, 704)
    g_ev_w_out = shard[:, o1:o2].reshape(256, D_MODEL)
    g_od_w_in = shard[:, o2:o3].reshape(D_MODEL, 512)
    g_od_w_out = shard[:, o3:].reshape(256, D_MODEL)
    r1, r2 = REPL_SIZES[0], REPL_SIZES[0] + REPL_SIZES[1]
    g_sg_w = repl[:, :r1].reshape(8 * BLK, BLK)
    g_w_a = repl[:, r1:r2].reshape(16 * BLK, BLK)
    g_w_x = repl[:, r2:].reshape(16 * BLK, BLK)

    big = {}
    for name, w_, g_, m_, v_ in (
            ("ev_w_in", ev_w_in, g_ev_w_in, m_ev_w_in, v_ev_w_in), ("ev_w_out", ev_w_out, g_ev_w_out, m_ev_w_out, v_ev_w_out),
            ("od_w_in", od_w_in, g_od_w_in, m_od_w_in, v_od_w_in), ("od_w_out", od_w_out, g_od_w_out, m_od_w_out, v_od_w_out),
            ("ev_sg_w", ev_sg_w, g_sg_w, m_ev_sg_w, v_ev_sg_w), ("od_w_a", od_w_a, g_w_a, m_od_w_a, v_od_w_a),
            ("od_w_x", od_w_x, g_w_x, m_od_w_x, v_od_w_x)):
        two_d = lambda a: a.reshape(g_.shape)
        d_, nm_, nv_ = _adamw(two_d(w_), g_, two_d(m_), two_d(v_), "adamw_" + name)
        big[name] = tuple(a.reshape(w_.shape) for a in (g_, d_, nm_, nv_))
    big["ada_w"] = (g_ada_w, d_ada_w, nm_ada_w, nv_ada_w)

    sh = lambda a: _cols(a, chip * 256, 256)
    small_g = {
        "ada_b": gs[0:6].reshape(2, 3 * D_MODEL), "ln_g": jnp.stack([gs[6], gs[8]]), "ln_b": jnp.stack([gs[7], gs[9]]),
        "ev_sink": gs[23:24, 0:8], "ev_sg_ln_g": gs[10:11, 0:512], "ev_sg_ln_b": gs[10:11, 512:1024],
        "ev_sg_b": gs[11].reshape(8, BLK), "od_conv_w": sh(gs[12:16]), "od_conv_b": sh(gs[16:17]), "od_b_a": sh(gs[17:19]),
        "od_b_x": sh(gs[19:21]), "od_lam": sh(gs[21:23]),
    }
    small_in = {"ada_b": (ada_b, m_ada_b, v_ada_b), "ln_g": (ln_g, m_ln_g, v_ln_g), "ln_b": (ln_b, m_ln_b, v_ln_b),
                "ev_sink": (ev_sink, m_ev_sink, v_ev_sink), "ev_sg_ln_g": (ev_sg_ln_g, m_ev_sg_ln_g, v_ev_sg_ln_g),
                "ev_sg_ln_b": (ev_sg_ln_b, m_ev_sg_ln_b, v_ev_sg_ln_b), "ev_sg_b": (ev_sg_b, m_ev_sg_b, v_ev_sg_b),
                "od_conv_w": (od_conv_w, m_od_conv_w, v_od_conv_w), "od_conv_b": (od_conv_b, m_od_conv_b, v_od_conv_b),
                "od_b_a": (od_b_a, m_od_b_a, v_od_b_a), "od_b_x": (od_b_x, m_od_b_x, v_od_b_x),
                "od_lam": (od_lam, m_od_lam, v_od_lam)}
    names_small = list(small_g)
    upd = _adamw_small([(small_in[n][0].reshape(small_g[n].shape), small_g[n], small_in[n][1].reshape(small_g[n].shape),
                         small_in[n][2].reshape(small_g[n].shape)) for n in names_small])
    res = dict(big)
    for n, (d_, nm_, nv_) in zip(names_small, upd):
        shape = small_in[n][0].shape
        res[n] = tuple(a.reshape(shape) for a in (small_g[n], d_, nm_, nv_))

    order = ["ada_w", "ada_b", "ln_g", "ln_b", "ev_w_in", "ev_w_out", "ev_sink", "ev_sg_ln_g", "ev_sg_ln_b", "ev_sg_w", "ev_sg_b",
             "od_w_in", "od_conv_w", "od_conv_b", "od_w_a", "od_b_a", "od_w_x", "od_b_x", "od_lam", "od_w_out"]
    return (loss, grad_x.reshape(x.shape), *[res[n][0] for n in order], *[res[n][1] for n in order],
            *[res[n][2] for n in order], *[res[n][3] for n in order])
```

```python
import functools

import jax
import jax.numpy as jnp
from jax import lax
from jax.experimental import pallas as pl
from jax.experimental.pallas import tpu as pltpu

F32 = jnp.float32
BF16 = jnp.bfloat16
MXU_DTYPE = BF16

D_MODEL = 1024
HEAD_DIM = 64
N_Q_HEADS = 8
Q_PER_KV = 4
ATTN_WIDTH = 512
KV_WIDTH = 128
BLK = 128
ROPE_DIM = 16
ROPE_THETA = 500000.0
N_SG_GROUPS = 8
SG_WIDTH = 512
EVEN_IN = 2816
ODD_IN = 2048
RNN_HEADS = 8
RG_LRU_C = 8.0
ALPHA = (2 * 2) ** 0.25
LN_EPS = 1e-5
NEG_INF = -1e30
ADAM_LR, ADAM_B1, ADAM_B2, ADAM_EPS, ADAM_WD, ADAM_STEP = 0.001, 0.9, 0.999, 1e-08, 0.01, 10

LANES = 128
VMEM_LIMIT = 56 * 1024 * 1024
MESH = pl.DeviceIdType.MESH


def _mm(a, b):
    return jnp.dot(a.astype(MXU_DTYPE), b.astype(MXU_DTYPE), preferred_element_type=F32)


def _mm_nt(a, b):
    return lax.dot_general(a.astype(MXU_DTYPE), b.astype(MXU_DTYPE), (((1,), (1,)), ((), ())), preferred_element_type=F32)


def _mm_tn(a, b):
    return lax.dot_general(a.astype(MXU_DTYPE), b.astype(MXU_DTYPE), (((0,), (0,)), ((), ())), preferred_element_type=F32)


def _sigmoid(x):
    return 1.0 / (1.0 + jnp.exp(-x))


def _ln_stats(z):
    mu = jnp.mean(z, axis=-1, keepdims=True)
    d = z - mu
    var = jnp.mean(d * d, axis=-1, keepdims=True)
    rstd = lax.rsqrt(var + LN_EPS)
    return d * rstd, rstd


def _ln_bwd(dout, zhat, rstd, g):
    dzh = dout * g
    m1 = jnp.mean(dzh, axis=-1, keepdims=True)
    m2 = jnp.mean(dzh * zhat, axis=-1, keepdims=True)
    return rstd * (dzh - m1 - zhat * m2)


def _group_sum(x, e2):
    hi = x.astype(MXU_DTYPE)
    lo = (x - hi.astype(F32)).astype(MXU_DTYPE)
    return jnp.dot(hi, e2, preferred_element_type=F32) + jnp.dot(lo, e2, preferred_element_type=F32)


def _lane_iota(shape):
    return lax.broadcasted_iota(jnp.int32, shape, 1)


def _to_kv_lanes(t, h):
    src_lo = (h % 2 == 0)
    dst_lo = (h // Q_PER_KV == 0)
    if src_lo != dst_lo:
        t = pltpu.roll(t, HEAD_DIM, 1)
    lane = _lane_iota(t.shape)
    keep = (lane < HEAD_DIM) if dst_lo else (lane >= HEAD_DIM)
    return jnp.where(keep, t, 0.0)


def _from_kv_lanes(t, h):
    src_lo = (h // Q_PER_KV == 0)
    dst_lo = (h % 2 == 0)
    lane = _lane_iota(t.shape)
    keep = (lane < HEAD_DIM) if src_lo else (lane >= HEAD_DIM)
    t = jnp.where(keep, t, 0.0)
    if src_lo != dst_lo:
        t = pltpu.roll(t, HEAD_DIM, 1)
    return t


def _rope(t, cos_t, sin_p, sin_m):
    half = ROPE_DIM // 2
    return t * cos_t + pltpu.roll(t, half, 1) * sin_p + pltpu.roll(t, LANES - half, 1) * sin_m


def _rope_t(d, cos_t, sin_p, sin_m):
    half = ROPE_DIM // 2
    return d * cos_t + pltpu.roll(d * sin_p, LANES - half, 1) + pltpu.roll(d * sin_m, half, 1)


def _band(ref, n, nb):
    prev = jnp.maximum(n - 1, 0)
    nxt = jnp.minimum(n + 1, nb - 1)
    rows = [ref[pl.ds(pl.multiple_of(j * BLK, BLK), BLK), :] for j in (prev, n, nxt)]
    return jnp.concatenate(rows, axis=0)


def _band_valid(n, seq):
    qi = lax.broadcasted_iota(jnp.int32, (BLK, 3 * BLK), 0)
    kj = lax.broadcasted_iota(jnp.int32, (BLK, 3 * BLK), 1)
    k_abs = n * BLK - BLK + kj
    return (jnp.abs(kj - BLK - qi) <= BLK) & (k_abs >= 0) & (k_abs < seq)


def _expm1(x):
    poly = x * (1.0 + x * (1.0 / 2 + x * (1.0 / 6 + x * (1.0 / 24 + x * (1.0 / 120 + x * (1.0 / 720 + x * (1.0 / 5040)))))))
    return jnp.where(jnp.abs(x) < 0.25, poly, jnp.exp(x) - 1.0)


def _softplus_neg(lam):
    e = jnp.exp(-jnp.abs(lam))
    u = 1.0 + e
    log1p_e = jnp.where(u == 1.0, e, jnp.log(u) * (e / (u - 1.0)))
    sp = jnp.maximum(-lam, 0.0) + log1p_e
    dsp = -1.0 / (1.0 + jnp.exp(lam))
    return sp, dsp


def _full(shape):
    return pl.BlockSpec(shape, lambda *_: (0,) * len(shape))


def _rows(tm, n):
    return pl.BlockSpec((tm, n), lambda i: (i, 0))


def _params(*sem):
    return pltpu.CompilerParams(dimension_semantics=sem, vmem_limit_bytes=VMEM_LIMIT)


def _sds(shape, dtype=F32):
    return jax.ShapeDtypeStruct(shape, dtype)


def _row_tile(seq, want):
    return want if seq % want == 0 else seq


def _rope_tables(posf, seq):
    half = ROPE_DIM // 2
    inv_freq = jnp.power(jnp.float32(ROPE_THETA), -jnp.arange(half, dtype=F32) / half)
    j = jnp.arange(LANES) % HEAD_DIM
    invf = jnp.where(j < ROPE_DIM, inv_freq[j % half], 0.0).astype(F32).reshape(1, LANES)
    m_p = ((j >= half) & (j < ROPE_DIM)).astype(F32).reshape(1, LANES)
    m_m = -(j < half).astype(F32).reshape(1, LANES)
    tm = _row_tile(seq, 512)

    def body(pos_ref, invf_ref, mp_ref, mm_ref, cos_ref, sp_ref, sm_ref):
        ang = pos_ref[...] * invf_ref[...]
        s = jnp.sin(ang)
        cos_ref[...] = jnp.cos(ang)
        sp_ref[...] = s * mp_ref[...]
        sm_ref[...] = s * mm_ref[...]

    return pl.pallas_call(
        body, name="rope_tables", grid=(seq // tm,),
        in_specs=[_rows(tm, 1), _full((1, LANES)), _full((1, LANES)), _full((1, LANES))],
        out_specs=[_rows(tm, LANES)] * 3, out_shape=[_sds((seq, LANES))] * 3,
        compiler_params=_params("parallel"),
    )(posf, invf, m_p, m_m)


def _even_proj(x, mod, w_in, tabs, seq):
    tm = _row_tile(seq, 512)

    def body(x_ref, mod_ref, w_ref, cos_ref, sp_ref, sm_ref, h_ref, q_ref, k_ref, v_ref, su_ref, sv_ref, g_ref):
        h = x_ref[...] * (1.0 + mod_ref[1:2, :]) + mod_ref[0:1, :]
        hb = h.astype(MXU_DTYPE)
        h_ref[...] = hb
        p = jnp.dot(hb, w_ref[...], preferred_element_type=F32)
        cos_t, sin_p, sin_m = cos_ref[...], sp_ref[...], sm_ref[...]
        for j in range(ATTN_WIDTH // LANES):
            q_ref[:, j * LANES:(j + 1) * LANES] = _rope(p[:, j * LANES:(j + 1) * LANES], cos_t, sin_p, sin_m).astype(q_ref.dtype)
        k_ref[...] = _rope(p[:, 512:640], cos_t, sin_p, sin_m).astype(k_ref.dtype)
        v_ref[...] = p[:, 640:768].astype(v_ref.dtype)
        su_ref[...] = p[:, 768:1280]
        sv_ref[...] = p[:, 1280:1792]
        g_ref[...] = p[:, 1792:2816]

    return pl.pallas_call(
        body, name="even_proj", grid=(seq // tm,),
        in_specs=[_rows(tm, D_MODEL), _full((3, D_MODEL)), _full((D_MODEL, EVEN_IN))] + [_rows(tm, LANES)] * 3,
        out_specs=[_rows(tm, D_MODEL), _rows(tm, 512), _rows(tm, LANES), _rows(tm, LANES), _rows(tm, 512), _rows(tm, 512),
                   _rows(tm, D_MODEL)],
        out_shape=[_sds((seq, D_MODEL), MXU_DTYPE), _sds((seq, 512), MXU_DTYPE), _sds((seq, LANES), MXU_DTYPE),
                   _sds((seq, LANES), MXU_DTYPE), _sds((seq, 512)), _sds((seq, 512)), _sds((seq, D_MODEL))],
        compiler_params=_params("parallel"),
    )(x, mod, w_in, *tabs)


def _sg_forward(sv, lng, lnb, sgw_ref, sgb, e2):
    vn, vhat, rstd, svo = [], [], [], []
    for j in range(SG_WIDTH // LANES):
        t = sv[:, j * LANES:(j + 1) * LANES]
        mu = _group_sum(t, e2) * (1.0 / HEAD_DIM)
        d = t - mu
        var = _group_sum(d * d, e2) * (1.0 / HEAD_DIM)
        r = lax.rsqrt(var + LN_EPS)
        vh = d * r
        vhat.append(vh)
        rstd.append(r)
        vn.append(vh * lng[:, j * LANES:(j + 1) * LANES] + lnb[:, j * LANES:(j + 1) * LANES])
    lane = _lane_iota((BLK, LANES))
    for j in range(SG_WIDTH // LANES):
        lo = _mm(sgw_ref[2 * j], vn[j])
        hi = _mm(sgw_ref[2 * j + 1], vn[j])
        svo.append(jnp.where(lane < HEAD_DIM, lo, hi) + sgb[:, j * LANES:(j + 1) * LANES])
    return svo, vn, vhat, rstd


def _even_mix(q, k, v, su, sv, g, x, mod, sink, sgln_g, sgln_b, sgw, sgb_full, e2, w_out, ln_g, ln_b, seq):
    nb = seq // BLK

    def body(sink_ref, q_ref, k_ref, v_ref, su_ref, sv_ref, g_ref, x_ref, mod_ref, lng_ref, lnb_ref, sgw_ref, sgb_ref,
             e2_ref, wo_ref, g1_ref, b1_ref, ycat_ref, lse_ref, out_ref, z_ref, x1_ref):
        n = pl.program_id(0)
        kband = _band(k_ref, n, nb)
        vband = _band(v_ref, n, nb)
        valid = _band_valid(n, seq)
        lane = _lane_iota((BLK, LANES))
        lse = jnp.zeros((BLK, LANES), F32)
        for j in range(ATTN_WIDTH // LANES):
            qt = q_ref[:, j * LANES:(j + 1) * LANES].astype(F32)
            acc = jnp.zeros((BLK, LANES), F32)
            for h in (2 * j, 2 * j + 1):
                qh = _to_kv_lanes(qt, h)
                s = jnp.where(valid, _mm_nt(qh, kband) * (HEAD_DIM ** -0.5), NEG_INF)
                m = jnp.maximum(jnp.max(s, axis=1, keepdims=True), sink_ref[h])
                p = jnp.exp(s - m)
                denom = jnp.sum(p, axis=1, keepdims=True) + jnp.exp(sink_ref[h] - m)
                acc = acc + _from_kv_lanes(_mm(p / denom, vband), h)
                lse = jnp.where(lane == h, m + jnp.log(denom), lse)
            ycat_ref[:, j * LANES:(j + 1) * LANES] = acc
        lse_ref[...] = lse
        svo, _, _, _ = _sg_forward(sv_ref[...], lng_ref[...], lnb_ref[...], sgw_ref, sgb_ref[...], e2_ref[...])
        for j in range(SG_WIDTH // LANES):
            ycat_ref[:, ATTN_WIDTH + j * LANES:ATTN_WIDTH + (j + 1) * LANES] = su_ref[:, j * LANES:(j + 1) * LANES] * svo[j]
        gg = g_ref[...]
        yg = ycat_ref[...] * (gg * _sigmoid(gg))
        out = _mm(yg, wo_ref[...])
        out_ref[...] = out
        z = ALPHA * x_ref[...] + mod_ref[2:3, :] * out
        z_ref[...] = z
        zhat, _ = _ln_stats(z)
        x1_ref[...] = zhat * g1_ref[...] + b1_ref[...]

    blk = lambda w: pl.BlockSpec((BLK, w), lambda n: (n, 0))
    return pl.pallas_call(
        body, name="even_mix", grid=(nb,),
        in_specs=[pl.BlockSpec(memory_space=pltpu.SMEM), blk(512), _full((seq, LANES)), _full((seq, LANES)), blk(512), blk(512),
                  blk(D_MODEL), blk(D_MODEL), _full((3, D_MODEL)), _full((1, 512)), _full((1, 512)), _full((8, BLK, BLK)),
                  _full((BLK, 512)), _full((LANES, LANES)), _full((D_MODEL, D_MODEL)), _full((1, D_MODEL)), _full((1, D_MODEL))],
        out_specs=[blk(D_MODEL), blk(LANES), blk(D_MODEL), blk(D_MODEL), blk(D_MODEL)],
        out_shape=[_sds((seq, D_MODEL)), _sds((seq, LANES)), _sds((seq, D_MODEL)), _sds((seq, D_MODEL)), _sds((seq, D_MODEL))],
        compiler_params=_params("parallel"),
    )(sink, q, k, v, su, sv, g, x, mod, sgln_g, sgln_b, sgw, sgb_full, e2, w_out, ln_g, ln_b)


def _assemble_cols(w4):
    _, rows, cols = w4.shape
    tr = 256

    def body(x_ref, o_ref):
        for s in range(4):
            o_ref[:, s * cols:(s + 1) * cols] = x_ref[s]

    return pl.pallas_call(
        body, name="assemble_cols", grid=(rows // tr,), in_specs=[pl.BlockSpec((4, tr, cols), lambda i: (0, i, 0))],
        out_specs=_rows(tr, 4 * cols), out_shape=_sds((rows, 4 * cols), w4.dtype), compiler_params=_params("parallel"),
    )(w4)


def _odd_proj(x1, mod, w_in4, seq):
    tm = _row_tile(seq, 512)
    cs = ODD_IN // 4

    def body(x_ref, mod_ref, w_ref, h_ref, xr_ref, g_ref):
        h = x_ref[...] * (1.0 + mod_ref[1:2, :]) + mod_ref[0:1, :]
        hb = h.astype(MXU_DTYPE)
        h_ref[...] = hb
        for s in range(2):
            xr_ref[:, s * cs:(s + 1) * cs] = jnp.dot(hb, w_ref[s], preferred_element_type=F32)
            g_ref[:, s * cs:(s + 1) * cs] = jnp.dot(hb, w_ref[2 + s], preferred_element_type=F32)

    return pl.pallas_call(
        body, name="odd_proj", grid=(seq // tm,),
        in_specs=[_rows(tm, D_MODEL), _full((3, D_MODEL)), _full((4, D_MODEL, cs))],
        out_specs=[_rows(tm, D_MODEL)] * 3,
        out_shape=[_sds((seq, D_MODEL), MXU_DTYPE), _sds((seq, D_MODEL)), _sds((seq, D_MODEL))],
        compiler_params=_params("parallel"),
    )(x1, mod, w_in4)


def _halo_specs(tm, seq, width):
    per = tm // 8
    last = seq // 8 - 1
    return [pl.BlockSpec((8, width), lambda i: (jnp.maximum(i * per - 1, 0), 0)),
            pl.BlockSpec((tm, width), lambda i: (i, 0)),
            pl.BlockSpec((8, width), lambda i: (jnp.minimum((i + 1) * per, last), 0))]


def _extended(prev_ref, main_ref, next_ref, i, n_steps):
    prev = jnp.where(i > 0, prev_ref[...], 0.0)
    nxt = jnp.where(i < n_steps - 1, next_ref[...], 0.0)
    return jnp.concatenate([prev, main_ref[...], nxt], axis=0)


def _shifted(ext, off, tm):
    if off == 0:
        return ext[8:8 + tm]
    return pltpu.roll(ext, (-off) % ext.shape[0], 0)[8:8 + tm]


def _lru_gates(xh, pre, bias, sp, hs):
    res = []
    for d in range(2):
        r = _sigmoid(pre[:, (2 * d) * LANES:(2 * d + 1) * LANES] + bias[2 * d:2 * d + 1, hs])
        ig = _sigmoid(pre[:, (2 * d + 1) * LANES:(2 * d + 2) * LANES] + bias[2 * d + 1:2 * d + 2, hs])
        log_a = (-RG_LRU_C) * r * sp[d:d + 1, hs]
        a = jnp.exp(log_a)
        s = jnp.sqrt(-_expm1(2.0 * log_a))
        res.append((r, ig, a, s))
    return res


def _odd_gates(xr, conv_w, conv_b, wcat, bias, lam, seq):
    tm = _row_tile(seq, 512)
    steps = seq // tm

    def body(xp_ref, xm_ref, xn_ref, cw_ref, cb_ref, w_ref, bias_ref, lam_ref, xc_ref, af_ref, bf_ref, ar_ref, br_ref):
        i = pl.program_id(0)
        ext = _extended(xp_ref, xm_ref, xn_ref, i, steps)
        xc = cb_ref[...] + sum(cw_ref[kk:kk + 1, :] * _shifted(ext, kk - 2, tm) for kk in range(4))
        xc_ref[...] = xc
        sp, _ = _softplus_neg(lam_ref[...])
        bias = bias_ref[...]
        for h in range(RNN_HEADS):
            hs = slice(h * LANES, (h + 1) * LANES)
            xh = xc[:, hs]
            (_, i0, a0, s0), (_, i1, a1, s1) = _lru_gates(xh, _mm(xh, w_ref[h]), bias, sp, hs)
            af_ref[:, hs] = a0
            bf_ref[:, hs] = s0 * i0 * xh
            ar_ref[:, hs] = a1
            br_ref[:, hs] = s1 * i1 * xh

    return pl.pallas_call(
        body, name="odd_gates", grid=(steps,),
        in_specs=_halo_specs(tm, seq, D_MODEL) + [_full((4, D_MODEL)), _full((1, D_MODEL)), _full((8, LANES, 512)),
                                                  _full((4, D_MODEL)), _full((2, D_MODEL))],
        out_specs=[_rows(tm, D_MODEL)] * 5, out_shape=[_sds((seq, D_MODEL))] * 5,
        compiler_params=_params("parallel"),
    )(xr, xr, xr, conv_w, conv_b, wcat, bias, lam)


def _scan(a, b, seq, descending, post, name):
    tb = _row_tile(seq, 512)
    steps = seq // tb
    imap = (lambda i: (steps - 1 - i, 0)) if descending else (lambda i: (i, 0))
    spec = pl.BlockSpec((tb, D_MODEL), imap)
    n_out = 1 if post else 2

    def body(a_ref, b_ref, *rest):
        outs, carry = rest[:n_out], rest[n_out]

        @pl.when(pl.program_id(0) == 0)
        def _():
            carry[...] = jnp.zeros_like(carry)

        def step(j, c):
            t = (tb - 1 - j) if descending else j
            at = a_ref[pl.ds(t, 1), :]
            bt = b_ref[pl.ds(t, 1), :]
            if post:
                gcur = bt + c
                outs[0][pl.ds(t, 1), :] = gcur
                return at * gcur
            hcur = at * c + bt
            outs[0][pl.ds(t, 1), :] = hcur
            outs[1][pl.ds(t, 1), :] = c
            return hcur

        carry[...] = lax.fori_loop(0, tb, step, carry[...], unroll=8)

    return pl.pallas_call(
        body, name=name, grid=(steps,), in_specs=[spec, spec], out_specs=[spec] * n_out,
        out_shape=[_sds((seq, D_MODEL))] * n_out, scratch_shapes=[pltpu.VMEM((1, D_MODEL), F32)],
        compiler_params=_params("arbitrary"),
    )(a, b)


def _odd_out_and_loss(hf, hr, g, x1, tgt, mod, w_out, w_out_t, ln_g, ln_b, seq):
    tm = _row_tile(seq, 256)

    def body(hf_ref, hr_ref, g_ref, x_ref, t_ref, mod_ref, w_ref, wt_ref, lg_ref, lb_ref,
             dhs_ref, dg_ref, dres_ref, loss_ref, dw_ref, vec_ref):
        @pl.when(pl.program_id(0) == 0)
        def _():
            loss_ref[...] = jnp.zeros_like(loss_ref)
            dw_ref[...] = jnp.zeros_like(dw_ref)
            vec_ref[...] = jnp.zeros_like(vec_ref)

        gg = g_ref[...]
        sg = _sigmoid(gg)
        silu = gg * sg
        hsum = hf_ref[...] + hr_ref[...]
        y = hsum * silu
        out = _mm(y, w_ref[...])
        gate = mod_ref[2:3, :]
        z = ALPHA * x_ref[...] + gate * out
        zhat, rstd = _ln_stats(z)
        x2 = zhat * lg_ref[...] + lb_ref[...]
        err = x2 - t_ref[...]
        loss_ref[...] += 0.5 * jnp.sum(jnp.mean(err * err, axis=-1, keepdims=True))
        dx2 = err * (1.0 / D_MODEL)
        dz = _ln_bwd(dx2, zhat, rstd, lg_ref[...])
        vec_ref[0:1, :] += jnp.sum(dx2 * zhat, axis=0, keepdims=True)
        vec_ref[1:2, :] += jnp.sum(dx2, axis=0, keepdims=True)
        vec_ref[2:3, :] += jnp.sum(dz * out, axis=0, keepdims=True)
        dres_ref[...] = ALPHA * dz
        dout = gate * dz
        dw_ref[...] += _mm_tn(y, dout)
        dy = _mm(dout, wt_ref[...])
        dhs_ref[...] = dy * silu
        dg_ref[...] = dy * hsum * (sg * (1.0 + gg * (1.0 - sg)))

    return pl.pallas_call(
        body, name="odd_out_loss", grid=(seq // tm,),
        in_specs=[_rows(tm, D_MODEL)] * 5 + [_full((3, D_MODEL)), _full((D_MODEL, D_MODEL)), _full((D_MODEL, D_MODEL)),
                                             _full((1, D_MODEL)), _full((1, D_MODEL))],
        out_specs=[_rows(tm, D_MODEL)] * 3 + [_full((8, LANES)), _full((D_MODEL, D_MODEL)), _full((8, D_MODEL))],
        out_shape=[_sds((seq, D_MODEL))] * 3 + [_sds((8, LANES)), _sds((D_MODEL, D_MODEL)), _sds((8, D_MODEL))],
        compiler_params=_params("arbitrary"),
    )(hf, hr, g, x1, tgt, mod, w_out, w_out_t, ln_g, ln_b)


def _odd_gates_bwd(xc, gf, gr, hpf, hpr, wcat, bias, lam, seq):
    tm = _row_tile(seq, 512)
    steps = seq // tm

    def body(xc_ref, gf_ref, gr_ref, hpf_ref, hpr_ref, w_ref, bias_ref, lam_ref, dxc_ref, dw_ref, vec_ref):
        @pl.when(pl.program_id(0) == 0)
        def _():
            dw_ref[...] = jnp.zeros_like(dw_ref)
            vec_ref[...] = jnp.zeros_like(vec_ref)

        sp, dsp = _softplus_neg(lam_ref[...])
        bias = bias_ref[...]
        for h in range(RNN_HEADS):
            hs = slice(h * LANES, (h + 1) * LANES)
            xh = xc_ref[:, hs]
            gates = _lru_gates(xh, _mm(xh, w_ref[h]), bias, sp, hs)
            dxh = jnp.zeros_like(xh)
            dpre = []
            for d, (g_ref_d, hp_ref_d) in enumerate(((gf_ref, hpf_ref), (gr_ref, hpr_ref))):
                r, ig, a, s = gates[d]
                db = g_ref_d[:, hs]
                da = db * hp_ref_d[:, hs]
                dxh = dxh + db * s * ig
                dlog_a = da * a - (db * ig * xh) * (a * a / s)
                dr = dlog_a * (-RG_LRU_C) * sp[d:d + 1, hs]
                di = db * s * xh
                dpr = dr * r * (1.0 - r)
                dpi = di * ig * (1.0 - ig)
                vec_ref[2 * d:2 * d + 1, hs] += jnp.sum(dpr, axis=0, keepdims=True)
                vec_ref[2 * d + 1:2 * d + 2, hs] += jnp.sum(dpi, axis=0, keepdims=True)
                vec_ref[4 + d:5 + d, hs] += jnp.sum(dlog_a * r, axis=0, keepdims=True) * (-RG_LRU_C) * dsp[d:d + 1, hs]
                dpre += [dpr, dpi]
            dcat = jnp.concatenate(dpre, axis=1)
            dw_ref[h] += _mm_tn(xh, dcat)
            dxc_ref[:, hs] = dxh + _mm_nt(dcat, w_ref[h])

    return pl.pallas_call(
        body, name="odd_gates_bwd", grid=(steps,),
        in_specs=[_rows(tm, D_MODEL)] * 5 + [_full((8, LANES, 512)), _full((4, D_MODEL)), _full((2, D_MODEL))],
        out_specs=[_rows(tm, D_MODEL), _full((8, LANES, 512)), _full((8, D_MODEL))],
        out_shape=[_sds((seq, D_MODEL)), _sds((8, LANES, 512)), _sds((8, D_MODEL))],
        compiler_params=_params("arbitrary"),
    )(xc, gf, gr, hpf, hpr, wcat, bias, lam)


def _odd_proj_bwd(dxc, xr, dg, x1, dres, mod, conv_w, w_in_t, seq):
    tm = _row_tile(seq, 512)
    steps = seq // tm

    def body(dp_ref, dm_ref, dn_ref, xp_ref, xm_ref, xn_ref, dg_ref, x_ref, dres_ref, mod_ref, cw_ref, wt_ref,
             dx_ref, dpb_ref, vec_ref):
        i = pl.program_id(0)

        @pl.when(i == 0)
        def _():
            vec_ref[...] = jnp.zeros_like(vec_ref)

        dext = _extended(dp_ref, dm_ref, dn_ref, i, steps)
        xext = _extended(xp_ref, xm_ref, xn_ref, i, steps)
        dxc_m = dm_ref[...]
        dxr = sum(cw_ref[kk:kk + 1, :] * _shifted(dext, 2 - kk, tm) for kk in range(4))
        for kk in range(4):
            vec_ref[kk:kk + 1, :] += jnp.sum(dxc_m * _shifted(xext, kk - 2, tm), axis=0, keepdims=True)
        vec_ref[4:5, :] += jnp.sum(dxc_m, axis=0, keepdims=True)
        dpb_ref[:, :D_MODEL] = dxr.astype(dpb_ref.dtype)
        dpb_ref[:, D_MODEL:] = dg_ref[...].astype(dpb_ref.dtype)
        dh = jnp.dot(dpb_ref[...], wt_ref[...], preferred_element_type=F32)
        x = x_ref[...]
        vec_ref[5:6, :] += jnp.sum(dh, axis=0, keepdims=True)
        vec_ref[6:7, :] += jnp.sum(dh * x, axis=0, keepdims=True)
        dx_ref[...] = dres_ref[...] + dh * (1.0 + mod_ref[1:2, :])

    return pl.pallas_call(
        body, name="odd_proj_bwd", grid=(steps,),
        in_specs=_halo_specs(tm, seq, D_MODEL) + _halo_specs(tm, seq, D_MODEL) + [_rows(tm, D_MODEL)] * 3
        + [_full((3, D_MODEL)), _full((4, D_MODEL)), _full((ODD_IN, D_MODEL))],
        out_specs=[_rows(tm, D_MODEL), _rows(tm, ODD_IN), _full((8, D_MODEL))],
        out_shape=[_sds((seq, D_MODEL)), _sds((seq, ODD_IN), MXU_DTYPE), _sds((8, D_MODEL))],
        compiler_params=_params("arbitrary"),
    )(dxc, dxc, dxc, xr, xr, xr, dg, x1, dres, mod, conv_w, w_in_t)


def _tn_matmul(a, b, seq, name):
    n = b.shape[1]
    tn = n // 2
    cs = n // 4
    tm = _row_tile(seq, 512)
    steps = seq // tm

    def body(a_ref, b_ref, o_ref, acc_ref):
        i = pl.program_id(1)

        @pl.when(i == 0)
        def _():
            acc_ref[...] = jnp.zeros_like(acc_ref)

        acc_ref[...] += lax.dot_general(a_ref[...], b_ref[...], (((0,), (0,)), ((), ())), preferred_element_type=F32)

        @pl.when(i == steps - 1)
        def _():
            o_ref[0] = acc_ref[:, 0:cs]
            o_ref[1] = acc_ref[:, cs:2 * cs]

    return pl.pallas_call(
        body, name=name, grid=(2, steps),
        in_specs=[pl.BlockSpec((tm, D_MODEL), lambda j, i: (i, 0)), pl.BlockSpec((tm, tn), lambda j, i: (i, j))],
        out_specs=pl.BlockSpec((2, D_MODEL, cs), lambda j, i: (j, 0, 0)), out_shape=_sds((4, D_MODEL, cs)),
        scratch_shapes=[pltpu.VMEM((D_MODEL, tn), F32)], compiler_params=_params("parallel", "arbitrary"),
    )(a, b)


def _even_out_bwd(dx1, z, out, ycat, g, mod, ln_g, w_out_t, seq):
    tm = _row_tile(seq, 256)

    def body(dx_ref, z_ref, out_ref, y_ref, g_ref, mod_ref, lg_ref, wt_ref, dy_ref, dg_ref, dres_ref, dw_ref, vec_ref):
        @pl.when(pl.program_id(0) == 0)
        def _():
            dw_ref[...] = jnp.zeros_like(dw_ref)
            vec_ref[...] = jnp.zeros_like(vec_ref)

        zhat, rstd = _ln_stats(z_ref[...])
        dx1_ = dx_ref[...]
        dz = _ln_bwd(dx1_, zhat, rstd, lg_ref[...])
        vec_ref[0:1, :] += jnp.sum(dx1_ * zhat, axis=0, keepdims=True)
        vec_ref[1:2, :] += jnp.sum(dx1_, axis=0, keepdims=True)
        vec_ref[2:3, :] += jnp.sum(dz * out_ref[...], axis=0, keepdims=True)
        dres_ref[...] = ALPHA * dz
        dout = mod_ref[2:3, :] * dz
        gg = g_ref[...]
        sg = _sigmoid(gg)
        silu = gg * sg
        ycat_ = y_ref[...]
        dw_ref[...] += _mm_tn(ycat_ * silu, dout)
        dy = _mm(dout, wt_ref[...])
        dy_ref[...] = dy * silu
        dg_ref[...] = dy * ycat_ * (sg * (1.0 + gg * (1.0 - sg)))

    return pl.pallas_call(
        body, name="even_out_bwd", grid=(seq // tm,),
        in_specs=[_rows(tm, D_MODEL)] * 5 + [_full((3, D_MODEL)), _full((1, D_MODEL)), _full((D_MODEL, D_MODEL))],
        out_specs=[_rows(tm, D_MODEL)] * 3 + [_full((D_MODEL, D_MODEL)), _full((8, D_MODEL))],
        out_shape=[_sds((seq, D_MODEL))] * 3 + [_sds((D_MODEL, D_MODEL)), _sds((8, D_MODEL))],
        compiler_params=_params("arbitrary"),
    )(dx1, z, out, ycat, g, mod, ln_g, w_out_t)


def _even_mix_bwd(q, k, v, lse, ycat, dycat, su, sv, sink, sgln_g, sgln_b, sgw, sgb_full, e2, e8, seq):
    nb = seq // BLK

    def body(sink_ref, q_ref, k_ref, v_ref, lse_ref, y_ref, dy_ref, su_ref, sv_ref, lng_ref, lnb_ref, sgw_ref, sgb_ref, e2_ref,
             e8_ref, dq_ref, dsu_ref, dsv_ref, dk_ref, dv_ref, dsgw_ref, dsgb_ref, vec_ref, dsink_ref, dsgb_acc):
        n = pl.program_id(0)

        @pl.when(n == 0)
        def _():
            dk_ref[...] = jnp.zeros_like(dk_ref)
            dv_ref[...] = jnp.zeros_like(dv_ref)
            dsgw_ref[...] = jnp.zeros_like(dsgw_ref)
            dsgb_acc[...] = jnp.zeros_like(dsgb_acc)
            vec_ref[...] = jnp.zeros_like(vec_ref)
            dsink_ref[...] = jnp.zeros_like(dsink_ref)

        kband = _band(k_ref, n, nb)
        vband = _band(v_ref, n, nb)
        valid = _band_valid(n, seq)
        lane = _lane_iota((BLK, LANES))
        row8 = lax.broadcasted_iota(jnp.int32, (8, LANES), 0)
        lse = lse_ref[...]
        dkb = jnp.zeros((3 * BLK, LANES), F32)
        dvb = jnp.zeros((3 * BLK, LANES), F32)
        dsink = jnp.zeros((8, LANES), F32)
        for j in range(ATTN_WIDTH // LANES):
            qt = q_ref[:, j * LANES:(j + 1) * LANES].astype(F32)
            ot = y_ref[:, j * LANES:(j + 1) * LANES]
            dot_ = dy_ref[:, j * LANES:(j + 1) * LANES]
            dqt = jnp.zeros((BLK, LANES), F32)
            for h in (2 * j, 2 * j + 1):
                head_lanes = (lane < HEAD_DIM) if h % 2 == 0 else (lane >= HEAD_DIM)
                qh = _to_kv_lanes(qt, h)
                doh = _to_kv_lanes(dot_, h)
                lse_h = jnp.sum(jnp.where(lane == h, lse, 0.0), axis=1, keepdims=True)
                s = jnp.where(valid, _mm_nt(qh, kband) * (HEAD_DIM ** -0.5), NEG_INF)
                p = jnp.exp(s - lse_h)
                psink = jnp.exp(sink_ref[h] - lse_h)
                delta = jnp.sum(jnp.where(head_lanes, dot_ * ot, 0.0), axis=1, keepdims=True)
                ds = p * (_mm_nt(doh, vband) - delta) * (HEAD_DIM ** -0.5)
                dsink = dsink + jnp.where(row8 == h, -jnp.sum(psink * delta), 0.0)
                dqt = dqt + _from_kv_lanes(_mm(ds, kband), h)
                dkb = dkb + _mm_tn(ds, qh)
                dvb = dvb + _mm_tn(p, doh)
            dq_ref[:, j * LANES:(j + 1) * LANES] = dqt
        dsink_ref[...] += dsink
        prev = jnp.maximum(n - 1, 0)
        nxt = jnp.minimum(n + 1, nb - 1)
        for part, blk_i in enumerate((prev, n, nxt)):
            rows = pl.ds(pl.multiple_of(blk_i * BLK, BLK), BLK)
            dk_ref[rows, :] += dkb[part * BLK:(part + 1) * BLK]
            dv_ref[rows, :] += dvb[part * BLK:(part + 1) * BLK]

        e2 = e2_ref[...]
        lng = lng_ref[...]
        svo, vn, vhat, rstd = _sg_forward(sv_ref[...], lng, lnb_ref[...], sgw_ref, sgb_ref[...], e2)
        for j in range(SG_WIDTH // LANES):
            cs = slice(j * LANES, (j + 1) * LANES)
            dysg = dy_ref[:, ATTN_WIDTH + j * LANES:ATTN_WIDTH + (j + 1) * LANES]
            dsu_ref[:, cs] = dysg * svo[j]
            dsvo = dysg * su_ref[:, cs]
            dsgb_acc[:, cs] += dsvo
            d_lo = jnp.where(lane < HEAD_DIM, dsvo, 0.0)
            d_hi = dsvo - d_lo
            dsgw_ref[2 * j] += _mm_nt(d_lo, vn[j])
            dsgw_ref[2 * j + 1] += _mm_nt(d_hi, vn[j])
            dvn = _mm_tn(sgw_ref[2 * j], d_lo) + _mm_tn(sgw_ref[2 * j + 1], d_hi)
            vec_ref[0:1, cs] += jnp.sum(dvn * vhat[j], axis=0, keepdims=True)
            vec_ref[1:2, cs] += jnp.sum(dvn, axis=0, keepdims=True)
            dvh = dvn * lng[:, cs]
            m1 = _group_sum(dvh, e2) * (1.0 / HEAD_DIM)
            m2 = _group_sum(dvh * vhat[j], e2) * (1.0 / HEAD_DIM)
            dsv_ref[:, cs] = rstd[j] * (dvh - m1 - vhat[j] * m2)

        @pl.when(n == nb - 1)
        def _():
            rest = dsgb_acc[...]
            total = jnp.zeros((8, BLK), F32)
            for _ in range(3):
                part = rest.astype(MXU_DTYPE)
                total = total + lax.dot_general(e8_ref[...], part, (((1,), (1,)), ((), ())), preferred_element_type=F32)
                rest = rest - part.astype(F32)
            dsgb_ref[...] = total

    blk = lambda w: pl.BlockSpec((BLK, w), lambda n: (n, 0))
    return pl.pallas_call(
        body, name="even_mix_bwd", grid=(nb,),
        in_specs=[pl.BlockSpec(memory_space=pltpu.SMEM), blk(512), _full((seq, LANES)), _full((seq, LANES)), blk(LANES),
                  blk(D_MODEL), blk(D_MODEL), blk(512), blk(512), _full((1, 512)), _full((1, 512)), _full((8, BLK, BLK)),
                  _full((BLK, 512)), _full((LANES, LANES)), _full((8, 512))],
        out_specs=[blk(512), blk(512), blk(512), _full((seq, LANES)), _full((seq, LANES)), _full((8, BLK, BLK)),
                   _full((8, BLK)), _full((8, 512)), _full((8, LANES))],
        out_shape=[_sds((seq, 512)), _sds((seq, 512)), _sds((seq, 512)), _sds((seq, LANES)), _sds((seq, LANES)),
                   _sds((8, BLK, BLK)), _sds((8, BLK)), _sds((8, 512)), _sds((8, LANES))],
        scratch_shapes=[pltpu.VMEM((BLK, 512), F32)],
        compiler_params=_params("arbitrary"),
    )(sink, q, k, v, lse, ycat, dycat, su, sv, sgln_g, sgln_b, sgw, sgb_full, e2, e8)


def _even_proj_bwd(dq, dk, dv, dsu, dsv, dg, x, dres, mod, tabs, w_in_t, seq):
    tm = _row_tile(seq, 512)

    def body(dq_ref, dk_ref, dv_ref, dsu_ref, dsv_ref, dg_ref, x_ref, dres_ref, mod_ref, cos_ref, sp_ref, sm_ref, wt_ref,
             dx_ref, dpb_ref, vec_ref):
        @pl.when(pl.program_id(0) == 0)
        def _():
            vec_ref[...] = jnp.zeros_like(vec_ref)

        cos_t, sin_p, sin_m = cos_ref[...], sp_ref[...], sm_ref[...]
        dt = dpb_ref.dtype
        for j in range(ATTN_WIDTH // LANES):
            cs = slice(j * LANES, (j + 1) * LANES)
            dpb_ref[:, cs] = _rope_t(dq_ref[:, cs], cos_t, sin_p, sin_m).astype(dt)
        dpb_ref[:, 512:640] = _rope_t(dk_ref[...], cos_t, sin_p, sin_m).astype(dt)
        dpb_ref[:, 640:768] = dv_ref[...].astype(dt)
        dpb_ref[:, 768:1280] = dsu_ref[...].astype(dt)
        dpb_ref[:, 1280:1792] = dsv_ref[...].astype(dt)
        dpb_ref[:, 1792:2816] = dg_ref[...].astype(dt)
        dh = jnp.dot(dpb_ref[...], wt_ref[...], preferred_element_type=F32)
        x_ = x_ref[...]
        vec_ref[0:1, :] += jnp.sum(dh, axis=0, keepdims=True)
        vec_ref[1:2, :] += jnp.sum(dh * x_, axis=0, keepdims=True)
        dx_ref[...] = dres_ref[...] + dh * (1.0 + mod_ref[1:2, :])

    return pl.pallas_call(
        body, name="even_proj_bwd", grid=(seq // tm,),
        in_specs=[_rows(tm, 512), _rows(tm, LANES), _rows(tm, LANES), _rows(tm, 512), _rows(tm, 512), _rows(tm, D_MODEL),
                  _rows(tm, D_MODEL), _rows(tm, D_MODEL), _full((3, D_MODEL))] + [_rows(tm, LANES)] * 3
        + [_full((EVEN_IN, D_MODEL))],
        out_specs=[_rows(tm, D_MODEL), _rows(tm, EVEN_IN), _full((8, D_MODEL))],
        out_shape=[_sds((seq, D_MODEL)), _sds((seq, EVEN_IN), MXU_DTYPE), _sds((8, D_MODEL))],
        compiler_params=_params("arbitrary"),
    )(dq, dk, dv, dsu, dsv, dg, x, dres, mod, *tabs, w_in_t)


def _local_step(x, posf, tgt, mod, w, seq):
    mxu = lambda a: a.astype(MXU_DTYPE)
    row = lambda a: a.reshape(1, -1)
    tabs = _rope_tables(posf, seq)
    e2 = mxu(jnp.kron(jnp.eye(2, dtype=F32), jnp.ones((HEAD_DIM, HEAD_DIM), F32)))
    e8 = mxu(jnp.repeat(jnp.eye(N_SG_GROUPS, dtype=F32), HEAD_DIM, axis=1))
    sgw = mxu(w["ev_sg_w"])
    sgb_full = jnp.repeat(w["ev_sg_b"].T, HEAD_DIM, axis=1)
    sgln_g, sgln_b = row(w["ev_sg_ln_g"]), row(w["ev_sg_ln_b"])
    sink = w["ev_sink"].reshape(N_Q_HEADS)
    ev_w_in, ev_w_out = _assemble_cols(mxu(w["ev_w_in"])), mxu(w["ev_w_out"])
    od_w_in, od_w_out = mxu(w["od_w_in"]), mxu(w["od_w_out"])
    od_w_in_t = jnp.swapaxes(od_w_in, 1, 2).reshape(ODD_IN, D_MODEL)
    wcat = mxu(jnp.concatenate([w["od_w_a"][0], w["od_w_x"][0], w["od_w_a"][1], w["od_w_x"][1]], axis=2))
    gate_bias = jnp.stack([w["od_b_a"][0], w["od_b_x"][0], w["od_b_a"][1], w["od_b_x"][1]])
    conv_b = row(w["od_conv_b"])
    ln_g, ln_b = w["ln_g"], w["ln_b"]

    h0, q, k, v, su, sv, g0 = _even_proj(x, mod[0], ev_w_in, tabs, seq)
    ycat, lse, out0, z0, x1 = _even_mix(q, k, v, su, sv, g0, x, mod[0], sink, sgln_g, sgln_b, sgw, sgb_full, e2, ev_w_out,
                                        ln_g[0:1], ln_b[0:1], seq)
    h1, xr, g1 = _odd_proj(x1, mod[1], od_w_in, seq)
    xc, a_f, b_f, a_r, b_r = _odd_gates(xr, w["od_conv_w"], conv_b, wcat, gate_bias, w["od_lam"], seq)
    hf, hpf = _scan(a_f, b_f, seq, descending=False, post=False, name="scan_fwd")
    hr, hpr = _scan(a_r, b_r, seq, descending=True, post=False, name="scan_rev")
    dhs, dg1, dres1, loss, d_od_w_out, vec_o = _odd_out_and_loss(hf, hr, g1, x1, tgt, mod[1], od_w_out, od_w_out.T,
                                                                   ln_g[1:2], ln_b[1:2], seq)
    (gf,) = _scan(a_f, dhs, seq, descending=True, post=True, name="scan_fwd_bwd")
    (gr,) = _scan(a_r, dhs, seq, descending=False, post=True, name="scan_rev_bwd")
    dxc, d_wcat, vec_g = _odd_gates_bwd(xc, gf, gr, hpf, hpr, wcat, gate_bias, w["od_lam"], seq)
    dx1, dp1, vec_p = _odd_proj_bwd(dxc, xr, dg1, x1, dres1, mod[1], w["od_conv_w"], od_w_in_t, seq)
    d_od_w_in = _tn_matmul(h1, dp1, seq, "odd_dw_in")
    dycat, dg0, dres0, d_ev_w_out, vec_e = _even_out_bwd(dx1, z0, out0, ycat, g0, mod[0], ln_g[0:1], ev_w_out.T, seq)
    dq, dsu, dsv, dk, dv, d_sgw, d_sgb, vec_s, d_sink = _even_mix_bwd(q, k, v, lse, ycat, dycat, su, sv, sink, sgln_g, sgln_b,
                                                                      sgw, sgb_full, e2, e8, seq)
    grad_x, dp0, vec_x = _even_proj_bwd(dq, dk, dv, dsu, dsv, dg0, x, dres0, mod[0], tabs, ev_w_in.T, seq)
    d_ev_w_in = _tn_matmul(h0, dp0, seq, "even_dw_in")

    dmod = jnp.stack([jnp.stack([vec_x[0], vec_x[1], vec_e[2]]), jnp.stack([vec_p[5], vec_p[6], vec_o[2]])])
    grads = {
        "ln_g": jnp.stack([vec_e[0], vec_o[0]]), "ln_b": jnp.stack([vec_e[1], vec_o[1]]),
        "ev_w_in": d_ev_w_in, "ev_w_out": d_ev_w_out, "ev_sink": d_sink[:, 0],
        "ev_sg_ln_g": vec_s[0], "ev_sg_ln_b": vec_s[1], "ev_sg_w": d_sgw,
        "ev_sg_b": d_sgb,
        "od_w_in": d_od_w_in, "od_conv_w": vec_p[0:4], "od_conv_b": vec_p[4],
        "od_w_a": jnp.stack([d_wcat[:, :, 0:128], d_wcat[:, :, 256:384]]),
        "od_w_x": jnp.stack([d_wcat[:, :, 128:256], d_wcat[:, :, 384:512]]),
        "od_b_a": jnp.stack([vec_g[0], vec_g[2]]), "od_b_x": jnp.stack([vec_g[1], vec_g[3]]),
        "od_lam": vec_g[4:6], "od_w_out": d_od_w_out,
    }
    return loss[0, 0], grad_x, dmod, grads


def _place():
    return lax.axis_index("x"), lax.axis_index("y"), lax.axis_index("c")


def _allgather8(block, name):
    m_per, n = block.shape

    def body(x_ref, out_ref, send_sems, recv_sems, local_sem):
        x, y, c = _place()
        me, sibling = (x, y, c), (x, y, 1 - c)
        chips = [(1 - x, y), (x, 1 - y), (1 - x, 1 - y)]

        def rows(px, py, pc):
            return out_ref.at[pl.ds((4 * px + 2 * py + pc) * m_per, m_per), :]

        def copy(k, blk, to, src=None):
            return pltpu.make_async_remote_copy(src_ref=rows(*blk) if src is None else src, dst_ref=rows(*blk),
                                                send_sem=send_sems.at[k], recv_sem=recv_sems.at[k], device_id=to,
                                                device_id_type=MESH)

        mine = pltpu.make_async_copy(x_ref, rows(*me), local_sem)
        mine.start()
        first = [copy(0, me, sibling, src=x_ref)] + [copy(1 + j, me, (*chip, c), src=x_ref) for j, chip in enumerate(chips)]
        for cp in first:
            cp.start()
        passed = [copy(4 + j, (*chip, c), sibling) for j, chip in enumerate(chips)]
        for j, chip in enumerate(chips):
            copy(1 + j, (*chip, c), me).wait_recv()
            passed[j].start()
        copy(0, sibling, me).wait_recv()
        for j, chip in enumerate(chips):
            copy(4 + j, (*chip, 1 - c), me).wait_recv()
        for cp in first + passed:
            cp.wait_send()
        mine.wait()

    return pl.pallas_call(
        body, name=name, out_shape=_sds((8 * m_per, n), block.dtype),
        in_specs=[pl.BlockSpec(memory_space=pltpu.VMEM)], out_specs=pl.BlockSpec(memory_space=pltpu.VMEM),
        scratch_shapes=[pltpu.SemaphoreType.DMA((7,)), pltpu.SemaphoreType.DMA((7,)), pltpu.SemaphoreType.DMA],
        compiler_params=pltpu.CompilerParams(vmem_limit_bytes=VMEM_LIMIT),
    )(block)


class _Copies:
    def __init__(self, send_sems, recv_sems, local_sems):
        self.send_sems, self.recv_sems, self.local_sems = send_sems, recv_sems, local_sems
        self.n_remote = self.n_local = 0
        self.sent, self.locals = [], []

    def slot(self):
        self.n_remote += 1
        return self.n_remote - 1

    def remote(self, k, src, dst, to):
        return pltpu.make_async_remote_copy(src_ref=src, dst_ref=dst, send_sem=self.send_sems.at[k], recv_sem=self.recv_sems.at[k],
                                            device_id=to, device_id_type=MESH)

    def send(self, k, src, dst, to):
        cp = self.remote(k, src, dst, to)
        cp.start()
        self.sent.append(cp)

    def arrived(self, k, dst, frm):
        self.remote(k, dst, dst, frm).wait_recv()

    def local(self, src, dst):
        cp = pltpu.make_async_copy(src, dst, self.local_sems.at[self.n_local])
        self.n_local += 1
        cp.start()
        self.locals.append(cp)

    def drain(self):
        for cp in self.sent:
            cp.wait_send()
        for cp in self.locals:
            cp.wait()


def _comm_call(body, name, ins, out_shapes, n_remote, n_local):
    n_in = len(ins)

    def kern(*refs):
        in_refs, out_refs = refs[:n_in], refs[n_in:n_in + len(out_shapes)]
        send_sems, recv_sems, local_sems = refs[n_in + len(out_shapes):]
        body(_Copies(send_sems, recv_sems, local_sems), in_refs, out_refs)

    hbm = pl.BlockSpec(memory_space=pl.ANY)
    return pl.pallas_call(
        kern, name=name, out_shape=out_shapes, in_specs=[hbm] * n_in, out_specs=[hbm] * len(out_shapes),
        scratch_shapes=[pltpu.SemaphoreType.DMA((n_remote,)), pltpu.SemaphoreType.DMA((n_remote,)),
                        pltpu.SemaphoreType.DMA((max(n_local, 1),))],
    )(*ins)


def _gather_to_all(cps, src, dst, me, sibling, other_chips, c, base):
    idx = lambda p: 4 * p[0] + 2 * p[1] + p[2]
    cps.local(src, dst.at[idx(me)])
    cps.send(base, src, dst.at[idx(me)], sibling)
    for j, chip in enumerate(other_chips):
        cps.send(base + 1 + j, src, dst.at[idx(me)], (*chip, c))
    for j, chip in enumerate(other_chips):
        got = dst.at[idx((*chip, c))]
        cps.arrived(base + 1 + j, got, (*chip, c))
        cps.send(base + 4 + j, got, got, sibling)
    cps.arrived(base, dst.at[idx(sibling)], sibling)
    for j, chip in enumerate(other_chips):
        cps.arrived(base + 4 + j, dst.at[idx((*chip, 1 - c))], sibling)


def _gather_weights(shards, small):
    n = len(shards)

    def body(cps, ins, outs):
        x, y, c = _place()
        me, sibling, mine = (x, y, c), (x, y, 1 - c), 2 * x + y
        chips = [(1 - x, y), (x, 1 - y), (1 - x, 1 - y)]
        for i in range(n):
            cps.local(ins[i], outs[i].at[mine])
        for j, (px, py) in enumerate(chips):
            for i in range(n):
                hr = shards[i].shape[0] // 2
                rows = pl.ds(c * hr, hr)
                cps.send(6 * i + j, ins[i].at[rows], outs[i].at[mine, rows], (px, py, c))
        _gather_to_all(cps, ins[n], outs[n], me, sibling, chips, c, 6 * n)
        for j, (px, py) in enumerate(chips):
            for i in range(n):
                hr = shards[i].shape[0] // 2
                got = outs[i].at[2 * px + py, pl.ds(c * hr, hr)]
                cps.arrived(6 * i + j, got, (px, py, c))
                cps.send(6 * i + 3 + j, got, got, sibling)
        for j, (px, py) in enumerate(chips):
            for i in range(n):
                hr = shards[i].shape[0] // 2
                cps.arrived(6 * i + 3 + j, outs[i].at[2 * px + py, pl.ds((1 - c) * hr, hr)], sibling)
        cps.drain()

    return _comm_call(body, "gather_weights", list(shards) + [small],
                      [_sds((4,) + s.shape, s.dtype) for s in shards] + [_sds((8,) + small.shape, small.dtype)], 6 * n + 7, n + 1)


def _reduce_sibling(parts, dmod_rows):
    n = len(parts)

    def body(cps, ins, outs):
        x, y, c = _place()
        me, sibling = (x, y, c), (x, y, 1 - c)
        chips = [(1 - x, y), (x, 1 - y), (1 - x, 1 - y)]
        for i in range(n):
            cps.send(i, ins[i].at[:, 1 - c], outs[i], sibling)
        _gather_to_all(cps, ins[n], outs[n], me, sibling, chips, c, n)
        for i in range(n):
            cps.arrived(i, outs[i], sibling)
        cps.drain()

    return _comm_call(body, "reduce_sibling", list(parts) + [dmod_rows],
                      [_sds((4,) + p.shape[2:], p.dtype) for p in parts] + [_sds((8,) + dmod_rows.shape, dmod_rows.dtype)], n + 7, 1)


def _reduce_chips(parts):
    n = len(parts)

    def body(cps, ins, outs):
        x, y, c = _place()
        mine = 2 * x + y
        chips = [(1 - x, y), (x, 1 - y), (1 - x, 1 - y)]
        for i in range(n):
            cps.local(ins[i].at[mine], outs[i].at[mine])
        for j, (px, py) in enumerate(chips):
            for i in range(n):
                cps.send(3 * i + j, ins[i].at[2 * px + py], outs[i].at[mine], (px, py, c))
        for j, (px, py) in enumerate(chips):
            for i in range(n):
                cps.arrived(3 * i + j, outs[i].at[2 * px + py], (px, py, c))
        cps.drain()

    return _comm_call(body, "reduce_chips", list(parts), [_sds(p.shape, p.dtype) for p in parts], 3 * n, n)


def _gather_reduced(shard_parts, repl_parts):
    ns, nr = len(shard_parts), len(repl_parts)

    def body(cps, ins, outs):
        x, y, c = _place()
        me, sibling = (x, y, c), (x, y, 1 - c)
        chips = [(1 - x, y), (x, 1 - y), (1 - x, 1 - y)]
        for i in range(ns):
            cps.local(ins[i], outs[i].at[c])
            cps.send(i, ins[i], outs[i].at[c], sibling)
        for i in range(nr):
            _gather_to_all(cps, ins[ns + i], outs[ns + i], me, sibling, chips, c, ns + 7 * i)
        for i in range(ns):
            cps.arrived(i, outs[i].at[1 - c], sibling)
        cps.drain()

    return _comm_call(body, "gather_reduced", list(shard_parts) + list(repl_parts),
                      [_sds((2,) + p.shape, p.dtype) for p in shard_parts] + [_sds((8,) + p.shape, p.dtype) for p in repl_parts],
                      ns + 7 * nr, ns + nr)


def _sum_sibling(core, parts, got, wire):
    n = len(parts)

    def body(core_ref, *refs):
        for i in range(n):
            refs[2 * n + i][0] = (refs[i][0] + refs[n + i][0]).astype(wire[i])

    keep_spec = lambda p: pl.BlockSpec((1, None) + p.shape[2:], lambda s, core_ref: (s, core_ref[0], 0, 0))
    slot_spec = lambda p: pl.BlockSpec((1,) + p.shape[2:], lambda s, core_ref: (s, 0, 0))
    return pl.pallas_call(
        body, name="sum_sibling",
        grid_spec=pltpu.PrefetchScalarGridSpec(
            num_scalar_prefetch=1, grid=(4,), in_specs=[keep_spec(p) for p in parts] + [slot_spec(p) for p in parts],
            out_specs=[slot_spec(p) for p in parts]),
        out_shape=[_sds((4,) + p.shape[2:], wire[i]) for i, p in enumerate(parts)],
        compiler_params=_params("parallel"),
    )(core, *parts, *got)


def _sum_slots(slots, name):
    n = len(slots)

    def spec_pair(p):
        k, rows, cols = p.shape
        sub = 16 if p.dtype == BF16 else 8
        if (rows // 2) % sub == 0:
            return pl.BlockSpec((k, rows // 2, cols), lambda i: (0, i, 0)), pl.BlockSpec((rows // 2, cols), lambda i: (i, 0))
        return pl.BlockSpec((k, rows, cols), lambda i: (0, 0, 0)), pl.BlockSpec((rows, cols), lambda i: (0, 0))

    pairs = [spec_pair(p) for p in slots]

    def body(*refs):
        for i in range(n):
            acc = refs[i][0].astype(F32)
            for j in range(1, slots[i].shape[0]):
                acc = acc + refs[i][j].astype(F32)
            refs[n + i][...] = acc

    return pl.pallas_call(
        body, name=name, grid=(2,), in_specs=[a for a, _ in pairs], out_specs=[b for _, b in pairs],
        out_shape=[_sds(p.shape[1:]) for p in slots], compiler_params=_params("arbitrary"),
    )(*slots)


def _modulation(c_all, ada_w, ada_b):
    cols = ada_w.shape[2]

    def body(c_ref, w_ref, b_ref, o_ref):
        cc = c_ref[...]
        o_ref[0] = _mm(cc * _sigmoid(cc), w_ref[0]) + b_ref[0]

    return pl.pallas_call(
        body, name="modulation", grid=(2,),
        in_specs=[_full((8, D_MODEL)), pl.BlockSpec((1, D_MODEL, cols), lambda l: (l, 0, 0)), pl.BlockSpec((1, 1, cols), lambda l: (l, 0, 0))],
        out_specs=pl.BlockSpec((1, 8, cols), lambda l: (l, 0, 0)), out_shape=_sds((2, 8, cols)),
        compiler_params=_params("parallel"),
    )(c_all, ada_w, ada_b)


def _adamw_math(w, g, m, v):
    m = ADAM_B1 * m + (1.0 - ADAM_B1) * g
    v = ADAM_B2 * v + (1.0 - ADAM_B2) * (g * g)
    m_hat = m / (1.0 - ADAM_B1 ** ADAM_STEP)
    v_hat = v / (1.0 - ADAM_B2 ** ADAM_STEP)
    delta = -ADAM_LR * (m_hat / (jnp.sqrt(v_hat) + ADAM_EPS) + ADAM_WD * w)
    return delta, m, v


def _ada_update(c_all, dmod, w, m, v):
    cols = w.shape[2]
    tr = 256
    spec3 = pl.BlockSpec((1, tr, cols), lambda l, i: (l, i, 0))

    def body(c_ref, d_ref, w_ref, m_ref, v_ref, g_ref, dl_ref, nm_ref, nv_ref):
        cc = c_ref[...]
        g = _mm_tn(cc * _sigmoid(cc), d_ref[0])
        g_ref[0] = g
        dl_ref[0], nm_ref[0], nv_ref[0] = _adamw_math(w_ref[0], g, m_ref[0], v_ref[0])

    return pl.pallas_call(
        body, name="ada_update", grid=(2, D_MODEL // tr),
        in_specs=[pl.BlockSpec((8, tr), lambda l, i: (0, i)), pl.BlockSpec((1, 8, cols), lambda l, i: (l, 0, 0)), spec3, spec3, spec3],
        out_specs=[spec3] * 4, out_shape=[_sds(w.shape)] * 4, compiler_params=_params("parallel", "parallel"),
    )(c_all, dmod, w, m, v)


def _adamw(w, g, m, v, name):
    rows, n = w.shape
    tr = next(t for t in (256, 128, 64, 32, 16, 8, rows) if rows % t == 0)

    def body(w_ref, g_ref, m_ref, v_ref, dl_ref, nm_ref, nv_ref):
        dl_ref[...], nm_ref[...], nv_ref[...] = _adamw_math(w_ref[...], g_ref[...], m_ref[...], v_ref[...])

    return pl.pallas_call(body, name=name, grid=(rows // tr,), in_specs=[_rows(tr, n)] * 4, out_specs=[_rows(tr, n)] * 3,
                          out_shape=[_sds((rows, n))] * 3, compiler_params=_params("parallel"))(w, g, m, v)


def _adamw_small(params):
    n = len(params)

    def body(*refs):
        ins, outs = refs[:4 * n], refs[4 * n:]
        for j in range(n):
            w_ref, g_ref, m_ref, v_ref = ins[4 * j:4 * j + 4]
            outs[3 * j][...], outs[3 * j + 1][...], outs[3 * j + 2][...] = _adamw_math(w_ref[...], g_ref[...], m_ref[...], v_ref[...])

    flat = [a for p in params for a in p]
    res = pl.pallas_call(body, name="adamw_small", out_shape=[_sds(p[0].shape) for p in params for _ in range(3)])(*flat)
    return [tuple(res[3 * j:3 * j + 3]) for j in range(n)]


def _cols(a, start, size):
    return lax.dynamic_slice_in_dim(a, start, size, axis=a.ndim - 1)


def kernel(x, c, positions, ada_w, ada_b, ln_g, ln_b, ev_w_in, ev_w_out, ev_sink, ev_sg_ln_g, ev_sg_ln_b, ev_sg_w, ev_sg_b, od_w_in, od_conv_w, od_conv_b, od_w_a, od_b_a, od_w_x, od_b_x, od_lam, od_w_out, loss_target, m_ada_w, m_ada_b, m_ln_g, m_ln_b, m_ev_w_in, m_ev_w_out, m_ev_sink, m_ev_sg_ln_g, m_ev_sg_ln_b, m_ev_sg_w, m_ev_sg_b, m_od_w_in, m_od_conv_w, m_od_conv_b, m_od_w_a, m_od_b_a, m_od_w_x, m_od_b_x, m_od_lam, m_od_w_out, v_ada_w, v_ada_b, v_ln_g, v_ln_b, v_ev_w_in, v_ev_w_out, v_ev_sink, v_ev_sg_ln_g, v_ev_sg_ln_b, v_ev_sg_w, v_ev_sg_b, v_od_w_in, v_od_conv_w, v_od_conv_b, v_od_w_a, v_od_b_a, v_od_w_x, v_od_b_x, v_od_lam, v_od_w_out):
    seq = x.shape[1]
    px, py, pc = _place()
    chip = 2 * px + py
    dev = 2 * chip + pc

    small = jnp.concatenate([od_conv_w[0].reshape(-1), od_conv_b[0], od_b_a[0].reshape(-1), jnp.zeros((256,), F32),
                             od_b_x[0].reshape(-1), od_lam[0].reshape(-1)]).reshape(3, D_MODEL)
    blk = jnp.concatenate([c, small, jnp.zeros((4, D_MODEL), F32)], axis=0)
    wire_w = lambda a: a[0].astype(MXU_DTYPE)
    ev_w_in4, ev_w_out4, od_w_in4, od_w_out4, g_small = _gather_weights(
        [wire_w(ev_w_in), wire_w(ev_w_out), wire_w(od_w_in), wire_w(od_w_out)], blk)
    c_all = g_small[:, 0, :]
    per_chip = g_small[0::2]
    conv_w = per_chip[:, 1].reshape(4, 4, 256).transpose(1, 0, 2).reshape(4, D_MODEL)
    conv_b = per_chip[:, 2, 0:256].reshape(D_MODEL)
    b_a = per_chip[:, 2, 256:768].reshape(4, 2, 256).transpose(1, 0, 2).reshape(2, D_MODEL)
    b_x = per_chip[:, 3, 0:512].reshape(4, 2, 256).transpose(1, 0, 2).reshape(2, D_MODEL)
    lam = per_chip[:, 3, 512:1024].reshape(4, 2, 256).transpose(1, 0, 2).reshape(2, D_MODEL)

    w_full = {
        "ev_w_in": ev_w_in4, "ev_w_out": ev_w_out4.reshape(D_MODEL, D_MODEL),
        "od_w_in": od_w_in4, "od_w_out": od_w_out4.reshape(D_MODEL, D_MODEL),
        "ev_sink": ev_sink[0], "ev_sg_ln_g": ev_sg_ln_g[0], "ev_sg_ln_b": ev_sg_ln_b[0], "ev_sg_w": ev_sg_w[0],
        "ev_sg_b": ev_sg_b[0], "od_conv_w": conv_w, "od_conv_b": conv_b, "od_w_a": od_w_a[0], "od_b_a": b_a,
        "od_w_x": od_w_x[0], "od_b_x": b_x, "od_lam": lam, "ln_g": ln_g, "ln_b": ln_b,
    }

    ada_cols = ada_w.shape[2]
    mod_sh = _modulation(c_all, ada_w, _cols(ada_b, chip * ada_cols, ada_cols).reshape(2, 1, ada_cols))
    mod_all = _allgather8(mod_sh.reshape(16, ada_cols), "gather_mod").reshape(4, 2, 2, 8, ada_cols)[:, 0]
    mod_mine = lax.dynamic_index_in_dim(mod_all, dev, axis=2, keepdims=False)
    mod = mod_mine.transpose(1, 0, 2).reshape(2, 3, D_MODEL)

    posf = positions.astype(F32).reshape(seq, 1)
    loss_local, grad_x, dmod, g = _local_step(x[0], posf, loss_target[0], mod, w_full, seq)

    pad = lambda a, n: jnp.concatenate([a.reshape(-1), jnp.zeros((n - a.size,), F32)])
    rows_small = jnp.concatenate([
        dmod.reshape(6, D_MODEL), g["ln_g"][0:1], g["ln_b"][0:1], g["ln_g"][1:2], g["ln_b"][1:2],
        jnp.concatenate([g["ev_sg_ln_g"], g["ev_sg_ln_b"]]).reshape(1, D_MODEL), g["ev_sg_b"].reshape(1, D_MODEL),
        g["od_conv_w"], g["od_conv_b"].reshape(1, D_MODEL), g["od_b_a"], g["od_b_x"], g["od_lam"],
        pad(g["ev_sink"], D_MODEL).reshape(1, D_MODEL), pad(loss_local, D_MODEL).reshape(1, D_MODEL),
        jnp.zeros((39, D_MODEL), F32)], axis=0)
    parts = [g["ev_w_in"].reshape(4, 2, 512, 704), g["ev_w_out"].reshape(4, 2, 128, D_MODEL),
             g["od_w_in"].reshape(4, 2, 512, 512), g["od_w_out"].reshape(4, 2, 128, D_MODEL),
             g["ev_sg_w"].reshape(4, 2, BLK, BLK), g["od_w_a"].reshape(4, 2, 2 * BLK, BLK),
             g["od_w_x"].reshape(4, 2, 2 * BLK, BLK), rows_small.reshape(4, 2, 8, D_MODEL)]
    wire = [MXU_DTYPE] * 7 + [F32]
    dmod_blk = jnp.concatenate([dmod.reshape(6, D_MODEL), jnp.zeros((2, D_MODEL), F32)], axis=0)
    *got, dmod_gathered = _reduce_sibling(parts, dmod_blk)
    chip_sums = _sum_sibling(pc.astype(jnp.int32).reshape(1), parts, got, wire)
    mine = _sum_slots(_reduce_chips(chip_sums), "sum_chips")
    reduced = _gather_reduced(mine[:4], mine[4:])
    g_ev_w_in = reduced[0].reshape(D_MODEL, 704)
    g_ev_w_out = reduced[1].reshape(256, D_MODEL)
    g_od_w_in = reduced[2].reshape(D_MODEL, 512)
    g_od_w_out = reduced[3].reshape(256, D_MODEL)
    g_sg_w = reduced[4].reshape(8 * BLK, BLK)
    g_w_a = reduced[5].reshape(16 * BLK, BLK)
    g_w_x = reduced[6].reshape(16 * BLK, BLK)
    gs = reduced[7].reshape(64, D_MODEL)
    loss = gs[24, 0]
    dmod_all = dmod_gathered[:, 0:6].reshape(8, 2, 3 * D_MODEL)
    dmod_sh = _cols(dmod_all, chip * ada_cols, ada_cols).transpose(1, 0, 2)
    g_ada_w, d_ada_w, nm_ada_w, nv_ada_w = _ada_update(c_all, dmod_sh, ada_w, m_ada_w, v_ada_w)

    big = {}
    for name, w_, g_, m_, v_ in (
            ("ev_w_in", ev_w_in, g_ev_w_in, m_ev_w_in, v_ev_w_in), ("ev_w_out", ev_w_out, g_ev_w_out, m_ev_w_out, v_ev_w_out),
            ("od_w_in", od_w_in, g_od_w_in, m_od_w_in, v_od_w_in), ("od_w_out", od_w_out, g_od_w_out, m_od_w_out, v_od_w_out),
            ("ev_sg_w", ev_sg_w, g_sg_w, m_ev_sg_w, v_ev_sg_w), ("od_w_a", od_w_a, g_w_a, m_od_w_a, v_od_w_a),
            ("od_w_x", od_w_x, g_w_x, m_od_w_x, v_od_w_x)):
        two_d = lambda a: a.reshape(g_.shape)
        d_, nm_, nv_ = _adamw(two_d(w_), g_, two_d(m_), two_d(v_), "adamw_" + name)
        big[name] = tuple(a.reshape(w_.shape) for a in (g_, d_, nm_, nv_))
    big["ada_w"] = (g_ada_w, d_ada_w, nm_ada_w, nv_ada_w)

    sh = lambda a: _cols(a, chip * 256, 256)
    small_g = {
        "ada_b": gs[0:6].reshape(2, 3 * D_MODEL), "ln_g": jnp.stack([gs[6], gs[8]]), "ln_b": jnp.stack([gs[7], gs[9]]),
        "ev_sink": gs[23:24, 0:8], "ev_sg_ln_g": gs[10:11, 0:512], "ev_sg_ln_b": gs[10:11, 512:1024],
        "ev_sg_b": gs[11].reshape(8, BLK), "od_conv_w": sh(gs[12:16]), "od_conv_b": sh(gs[16:17]), "od_b_a": sh(gs[17:19]),
        "od_b_x": sh(gs[19:21]), "od_lam": sh(gs[21:23]),
    }
    small_in = {"ada_b": (ada_b, m_ada_b, v_ada_b), "ln_g": (ln_g, m_ln_g, v_ln_g), "ln_b": (ln_b, m_ln_b, v_ln_b),
                "ev_sink": (ev_sink, m_ev_sink, v_ev_sink), "ev_sg_ln_g": (ev_sg_ln_g, m_ev_sg_ln_g, v_ev_sg_ln_g),
                "ev_sg_ln_b": (ev_sg_ln_b, m_ev_sg_ln_b, v_ev_sg_ln_b), "ev_sg_b": (ev_sg_b, m_ev_sg_b, v_ev_sg_b),
                "od_conv_w": (od_conv_w, m_od_conv_w, v_od_conv_w), "od_conv_b": (od_conv_b, m_od_conv_b, v_od_conv_b),
                "od_b_a": (od_b_a, m_od_b_a, v_od_b_a), "od_b_x": (od_b_x, m_od_b_x, v_od_b_x),
                "od_lam": (od_lam, m_od_lam, v_od_lam)}
    names_small = list(small_g)
    upd = _adamw_small([(small_in[n][0].reshape(small_g[n].shape), small_g[n], small_in[n][1].reshape(small_g[n].shape),
                         small_in[n][2].reshape(small_g[n].shape)) for n in names_small])
    res = dict(big)
    for n, (d_, nm_, nv_) in zip(names_small, upd):
        shape = small_in[n][0].shape
        res[n] = tuple(a.reshape(shape) for a in (small_g[n], d_, nm_, nv_))

    order = ["ada_w", "ada_b", "ln_g", "ln_b", "ev_w_in", "ev_w_out", "ev_sink", "ev_sg_ln_g", "ev_sg_ln_b", "ev_sg_w", "ev_sg_b",
             "od_w_in", "od_conv_w", "od_conv_b", "od_w_a", "od_b_a", "od_w_x", "od_b_x", "od_lam", "od_w_out"]
    return (loss, grad_x.reshape(x.shape), *[res[n][0] for n in order], *[res[n][1] for n in order],
            *[res[n][2] for n in order], *[res[n][3] for n in order])
```

```python
import functools

import jax
import jax.numpy as jnp
from jax import lax
from jax.experimental import pallas as pl
from jax.experimental.pallas import tpu as pltpu

F32 = jnp.float32
BF16 = jnp.bfloat16
MXU_DTYPE = BF16

D_MODEL = 1024
HEAD_DIM = 64
N_Q_HEADS = 8
Q_PER_KV = 4
ATTN_WIDTH = 512
KV_WIDTH = 128
BLK = 128
ROPE_DIM = 16
ROPE_THETA = 500000.0
N_SG_GROUPS = 8
SG_WIDTH = 512
EVEN_IN = 2816
ODD_IN = 2048
RNN_HEADS = 8
RG_LRU_C = 8.0
ALPHA = (2 * 2) ** 0.25
LN_EPS = 1e-5
NEG_INF = -1e30
ADAM_LR, ADAM_B1, ADAM_B2, ADAM_EPS, ADAM_WD, ADAM_STEP = 0.001, 0.9, 0.999, 1e-08, 0.01, 10

LANES = 128
VMEM_LIMIT = 56 * 1024 * 1024
MESH = pl.DeviceIdType.MESH


def _mm(a, b):
    return jnp.dot(a.astype(MXU_DTYPE), b.astype(MXU_DTYPE), preferred_element_type=F32)


def _mm_nt(a, b):
    return lax.dot_general(a.astype(MXU_DTYPE), b.astype(MXU_DTYPE), (((1,), (1,)), ((), ())), preferred_element_type=F32)


def _mm_tn(a, b):
    return lax.dot_general(a.astype(MXU_DTYPE), b.astype(MXU_DTYPE), (((0,), (0,)), ((), ())), preferred_element_type=F32)


def _sigmoid(x):
    return 1.0 / (1.0 + jnp.exp(-x))


def _ln_stats(z):
    mu = jnp.mean(z, axis=-1, keepdims=True)
    d = z - mu
    var = jnp.mean(d * d, axis=-1, keepdims=True)
    rstd = lax.rsqrt(var + LN_EPS)
    return d * rstd, rstd


def _ln_bwd(dout, zhat, rstd, g):
    dzh = dout * g
    m1 = jnp.mean(dzh, axis=-1, keepdims=True)
    m2 = jnp.mean(dzh * zhat, axis=-1, keepdims=True)
    return rstd * (dzh - m1 - zhat * m2)


def _group_sum(x, e2):
    hi = x.astype(MXU_DTYPE)
    lo = (x - hi.astype(F32)).astype(MXU_DTYPE)
    return jnp.dot(hi, e2, preferred_element_type=F32) + jnp.dot(lo, e2, preferred_element_type=F32)


def _lane_iota(shape):
    return lax.broadcasted_iota(jnp.int32, shape, 1)


def _to_kv_lanes(t, h):
    src_lo = (h % 2 == 0)
    dst_lo = (h // Q_PER_KV == 0)
    if src_lo != dst_lo:
        t = pltpu.roll(t, HEAD_DIM, 1)
    lane = _lane_iota(t.shape)
    keep = (lane < HEAD_DIM) if dst_lo else (lane >= HEAD_DIM)
    return jnp.where(keep, t, 0.0)


def _from_kv_lanes(t, h):
    src_lo = (h // Q_PER_KV == 0)
    dst_lo = (h % 2 == 0)
    lane = _lane_iota(t.shape)
    keep = (lane < HEAD_DIM) if src_lo else (lane >= HEAD_DIM)
    t = jnp.where(keep, t, 0.0)
    if src_lo != dst_lo:
        t = pltpu.roll(t, HEAD_DIM, 1)
    return t


def _rope(t, cos_t, sin_p, sin_m):
    half = ROPE_DIM // 2
    return t * cos_t + pltpu.roll(t, half, 1) * sin_p + pltpu.roll(t, LANES - half, 1) * sin_m


def _rope_t(d, cos_t, sin_p, sin_m):
    half = ROPE_DIM // 2
    return d * cos_t + pltpu.roll(d * sin_p, LANES - half, 1) + pltpu.roll(d * sin_m, half, 1)


def _band(ref, n, nb):
    prev = jnp.maximum(n - 1, 0)
    nxt = jnp.minimum(n + 1, nb - 1)
    rows = [ref[pl.ds(pl.multiple_of(j * BLK, BLK), BLK), :] for j in (prev, n, nxt)]
    return jnp.concatenate(rows, axis=0)


def _band_valid(n, seq):
    qi = lax.broadcasted_iota(jnp.int32, (BLK, 3 * BLK), 0)
    kj = lax.broadcasted_iota(jnp.int32, (BLK, 3 * BLK), 1)
    k_abs = n * BLK - BLK + kj
    return (jnp.abs(kj - BLK - qi) <= BLK) & (k_abs >= 0) & (k_abs < seq)


def _softplus_neg(lam):
    e = jnp.exp(-jnp.abs(lam))
    u = 1.0 + e
    log1p_e = jnp.where(u == 1.0, e, jnp.log(u) * (e / (u - 1.0)))
    sp = jnp.maximum(-lam, 0.0) + log1p_e
    dsp = -1.0 / (1.0 + jnp.exp(lam))
    return sp, dsp


def _full(shape):
    return pl.BlockSpec(shape, lambda *_: (0,) * len(shape))


def _const(shape):
    return pl.BlockSpec(shape, lambda *_: (0,) * len(shape), pipeline_mode=pl.Buffered(1))


def _rows(tm, n):
    return pl.BlockSpec((tm, n), lambda i: (i, 0))


def _params(*sem):
    return pltpu.CompilerParams(dimension_semantics=sem, vmem_limit_bytes=VMEM_LIMIT)


def _sds(shape, dtype=F32):
    return jax.ShapeDtypeStruct(shape, dtype)


def _row_tile(seq, want):
    return want if seq % want == 0 else seq


def _rope_tables(posf, seq):
    half = ROPE_DIM // 2
    inv_freq = jnp.power(jnp.float32(ROPE_THETA), -jnp.arange(half, dtype=F32) / half)
    j = jnp.arange(LANES) % HEAD_DIM
    invf = jnp.where(j < ROPE_DIM, inv_freq[j % half], 0.0).astype(F32).reshape(1, LANES)
    m_p = ((j >= half) & (j < ROPE_DIM)).astype(F32).reshape(1, LANES)
    m_m = -(j < half).astype(F32).reshape(1, LANES)
    tm = _row_tile(seq, 512)

    def body(pos_ref, invf_ref, mp_ref, mm_ref, cos_ref, sp_ref, sm_ref):
        ang = pos_ref[...] * invf_ref[...]
        s = jnp.sin(ang)
        cos_ref[...] = jnp.cos(ang)
        sp_ref[...] = s * mp_ref[...]
        sm_ref[...] = s * mm_ref[...]

    return pl.pallas_call(
        body, name="rope_tables", grid=(seq // tm,),
        in_specs=[_rows(tm, 1), _full((1, LANES)), _full((1, LANES)), _full((1, LANES))],
        out_specs=[_rows(tm, LANES)] * 3, out_shape=[_sds((seq, LANES))] * 3,
        compiler_params=_params("parallel"),
    )(posf, invf, m_p, m_m)


def _even_proj(x, mod, w_in, tabs, seq):
    tm = _row_tile(seq, 512)

    def body(x_ref, mod_ref, w_ref, cos_ref, sp_ref, sm_ref, h_ref, q_ref, k_ref, v_ref, su_ref, sv_ref, g_ref):
        h = x_ref[...] * (1.0 + mod_ref[1:2, :]) + mod_ref[0:1, :]
        hb = h.astype(MXU_DTYPE)
        h_ref[...] = hb
        p = jnp.dot(hb, w_ref[...], preferred_element_type=F32)
        cos_t, sin_p, sin_m = cos_ref[...], sp_ref[...], sm_ref[...]
        for j in range(ATTN_WIDTH // LANES):
            q_ref[:, j * LANES:(j + 1) * LANES] = _rope(p[:, j * LANES:(j + 1) * LANES], cos_t, sin_p, sin_m).astype(q_ref.dtype)
        k_ref[...] = _rope(p[:, 512:640], cos_t, sin_p, sin_m).astype(k_ref.dtype)
        v_ref[...] = p[:, 640:768].astype(v_ref.dtype)
        su_ref[...] = p[:, 768:1280]
        sv_ref[...] = p[:, 1280:1792]
        g_ref[...] = p[:, 1792:2816]

    return pl.pallas_call(
        body, name="even_proj", grid=(seq // tm,),
        in_specs=[_rows(tm, D_MODEL), _full((3, D_MODEL)), _full((D_MODEL, EVEN_IN))] + [_rows(tm, LANES)] * 3,
        out_specs=[_rows(tm, D_MODEL), _rows(tm, 512), _rows(tm, LANES), _rows(tm, LANES), _rows(tm, 512), _rows(tm, 512),
                   _rows(tm, D_MODEL)],
        out_shape=[_sds((seq, D_MODEL), MXU_DTYPE), _sds((seq, 512), MXU_DTYPE), _sds((seq, LANES), MXU_DTYPE),
                   _sds((seq, LANES), MXU_DTYPE), _sds((seq, 512)), _sds((seq, 512)), _sds((seq, D_MODEL))],
        compiler_params=_params("parallel"),
    )(x, mod, w_in, *tabs)


def _sg_forward(sv, lng, lnb, sgw_ref, sgb, e2):
    vn, vhat, rstd, svo = [], [], [], []
    for j in range(SG_WIDTH // LANES):
        t = sv[:, j * LANES:(j + 1) * LANES]
        mu = _group_sum(t, e2) * (1.0 / HEAD_DIM)
        d = t - mu
        var = _group_sum(d * d, e2) * (1.0 / HEAD_DIM)
        r = lax.rsqrt(var + LN_EPS)
        vh = d * r
        vhat.append(vh)
        rstd.append(r)
        vn.append(vh * lng[:, j * LANES:(j + 1) * LANES] + lnb[:, j * LANES:(j + 1) * LANES])
    lane = _lane_iota((BLK, LANES))
    for j in range(SG_WIDTH // LANES):
        lo = _mm(sgw_ref[2 * j], vn[j])
        hi = _mm(sgw_ref[2 * j + 1], vn[j])
        svo.append(jnp.where(lane < HEAD_DIM, lo, hi) + sgb[:, j * LANES:(j + 1) * LANES])
    return svo, vn, vhat, rstd


def _even_mix(q, k, v, su, sv, sink, sgln_g, sgln_b, sgw, sgb_full, e2, seq):
    nb = seq // BLK

    def body(sink_ref, q_ref, k_ref, v_ref, su_ref, sv_ref, lng_ref, lnb_ref, sgw_ref, sgb_ref, e2_ref, ycat_ref, lse_ref):
        n = pl.program_id(0)
        kband = _band(k_ref, n, nb)
        vband = _band(v_ref, n, nb)
        valid = _band_valid(n, seq)
        lane = _lane_iota((BLK, LANES))
        lse = jnp.zeros((BLK, LANES), F32)
        for j in range(ATTN_WIDTH // LANES):
            qt = q_ref[:, j * LANES:(j + 1) * LANES].astype(F32)
            acc = jnp.zeros((BLK, LANES), F32)
            for h in (2 * j, 2 * j + 1):
                qh = _to_kv_lanes(qt, h)
                s = jnp.where(valid, _mm_nt(qh, kband) * (HEAD_DIM ** -0.5), NEG_INF)
                m = jnp.maximum(jnp.max(s, axis=1, keepdims=True), sink_ref[h])
                p = jnp.exp(s - m)
                denom = jnp.sum(p, axis=1, keepdims=True) + jnp.exp(sink_ref[h] - m)
                acc = acc + _from_kv_lanes(_mm(p / denom, vband), h)
                lse = jnp.where(lane == h, m + jnp.log(denom), lse)
            ycat_ref[:, j * LANES:(j + 1) * LANES] = acc
        lse_ref[...] = lse
        svo, _, _, _ = _sg_forward(sv_ref[...], lng_ref[...], lnb_ref[...], sgw_ref, sgb_ref[...], e2_ref[...])
        for j in range(SG_WIDTH // LANES):
            ycat_ref[:, ATTN_WIDTH + j * LANES:ATTN_WIDTH + (j + 1) * LANES] = su_ref[:, j * LANES:(j + 1) * LANES] * svo[j]

    blk = lambda w: pl.BlockSpec((BLK, w), lambda n: (n, 0))
    return pl.pallas_call(
        body, name="even_mix", grid=(nb,),
        in_specs=[pl.BlockSpec(memory_space=pltpu.SMEM), blk(512), _full((seq, LANES)), _full((seq, LANES)), blk(512), blk(512),
                  _full((1, 512)), _full((1, 512)), _full((8, BLK, BLK)), _full((BLK, 512)), _full((LANES, LANES))],
        out_specs=[blk(D_MODEL), blk(LANES)], out_shape=[_sds((seq, D_MODEL)), _sds((seq, LANES))],
        compiler_params=_params("parallel"),
    )(sink, q, k, v, su, sv, sgln_g, sgln_b, sgw, sgb_full, e2)


def _even_out(ycat, g, x, mod, w_out, ln_g, ln_b, seq):
    tm = _row_tile(seq, 512)

    def body(y_ref, g_ref, x_ref, mod_ref, wo_ref, g1_ref, b1_ref, out_ref, z_ref, x1_ref):
        gg = g_ref[...]
        out = _mm(y_ref[...] * (gg * _sigmoid(gg)), wo_ref[...])
        out_ref[...] = out
        z = ALPHA * x_ref[...] + mod_ref[2:3, :] * out
        z_ref[...] = z
        zhat, _ = _ln_stats(z)
        x1_ref[...] = zhat * g1_ref[...] + b1_ref[...]

    return pl.pallas_call(
        body, name="even_out", grid=(seq // tm,),
        in_specs=[_rows(tm, D_MODEL)] * 3 + [_full((3, D_MODEL)), _const((D_MODEL, D_MODEL)), _full((1, D_MODEL)), _full((1, D_MODEL))],
        out_specs=[_rows(tm, D_MODEL)] * 3, out_shape=[_sds((seq, D_MODEL))] * 3, compiler_params=_params("parallel"),
    )(ycat, g, x, mod, w_out, ln_g, ln_b)


def _assemble_cols(w4):
    _, rows, cols = w4.shape
    tr = 256

    def body(x_ref, o_ref):
        for s in range(4):
            o_ref[:, s * cols:(s + 1) * cols] = x_ref[s]

    return pl.pallas_call(
        body, name="assemble_cols", grid=(rows // tr,), in_specs=[pl.BlockSpec((4, tr, cols), lambda i: (0, i, 0))],
        out_specs=_rows(tr, 4 * cols), out_shape=_sds((rows, 4 * cols), w4.dtype), compiler_params=_params("parallel"),
    )(w4)


def _odd_proj(x1, mod, w_in4, seq):
    tm = _row_tile(seq, 512)
    cs = ODD_IN // 4

    def body(x_ref, mod_ref, w_ref, h_ref, xr_ref, g_ref):
        h = x_ref[...] * (1.0 + mod_ref[1:2, :]) + mod_ref[0:1, :]
        hb = h.astype(MXU_DTYPE)
        h_ref[...] = hb
        for s in range(2):
            xr_ref[:, s * cs:(s + 1) * cs] = jnp.dot(hb, w_ref[s], preferred_element_type=F32)
            g_ref[:, s * cs:(s + 1) * cs] = jnp.dot(hb, w_ref[2 + s], preferred_element_type=F32)

    return pl.pallas_call(
        body, name="odd_proj", grid=(seq // tm,),
        in_specs=[_rows(tm, D_MODEL), _full((3, D_MODEL)), _full((4, D_MODEL, cs))],
        out_specs=[_rows(tm, D_MODEL)] * 3,
        out_shape=[_sds((seq, D_MODEL), MXU_DTYPE), _sds((seq, D_MODEL)), _sds((seq, D_MODEL))],
        compiler_params=_params("parallel"),
    )(x1, mod, w_in4)


def _halo_specs(tm, seq, width):
    per = tm // 8
    last = seq // 8 - 1
    return [pl.BlockSpec((8, width), lambda i: (jnp.maximum(i * per - 1, 0), 0)),
            pl.BlockSpec((tm, width), lambda i: (i, 0)),
            pl.BlockSpec((8, width), lambda i: (jnp.minimum((i + 1) * per, last), 0))]


def _extended(prev_ref, main_ref, next_ref, i, n_steps):
    prev = jnp.where(i > 0, prev_ref[...], 0.0)
    nxt = jnp.where(i < n_steps - 1, next_ref[...], 0.0)
    return jnp.concatenate([prev, main_ref[...], nxt], axis=0)


def _shifted(ext, off, tm):
    if off == 0:
        return ext[8:8 + tm]
    return pltpu.roll(ext, (-off) % ext.shape[0], 0)[8:8 + tm]


def _lru_gates(xh, pre, bias, sp, hs):
    res = []
    for d in range(2):
        r = _sigmoid(pre[:, (2 * d) * LANES:(2 * d + 1) * LANES] + bias[2 * d:2 * d + 1, hs])
        ig = _sigmoid(pre[:, (2 * d + 1) * LANES:(2 * d + 2) * LANES] + bias[2 * d + 1:2 * d + 2, hs])
        neg_log_a = RG_LRU_C * r * sp[d:d + 1, hs]
        a = jnp.exp(-neg_log_a)
        s = jnp.sqrt(jnp.tanh(neg_log_a) * (a * a + 1.0))
        res.append((r, ig, a, s))
    return res


def _odd_gates(xr, conv_w, conv_b, wcat, bias, lam, seq):
    tm = _row_tile(seq, 512)
    steps = seq // tm

    def body(xp_ref, xm_ref, xn_ref, cw_ref, cb_ref, w_ref, bias_ref, lam_ref, xc_ref, af_ref, bf_ref, ar_ref, br_ref):
        i = pl.program_id(0)
        ext = _extended(xp_ref, xm_ref, xn_ref, i, steps)
        xc = cb_ref[...] + sum(cw_ref[kk:kk + 1, :] * _shifted(ext, kk - 2, tm) for kk in range(4))
        xc_ref[...] = xc
        sp, _ = _softplus_neg(lam_ref[...])
        bias = bias_ref[...]
        for h in range(RNN_HEADS):
            hs = slice(h * LANES, (h + 1) * LANES)
            xh = xc[:, hs]
            (_, i0, a0, s0), (_, i1, a1, s1) = _lru_gates(xh, _mm(xh, w_ref[h]), bias, sp, hs)
            af_ref[:, hs] = a0
            bf_ref[:, hs] = s0 * i0 * xh
            ar_ref[:, hs] = a1
            br_ref[:, hs] = s1 * i1 * xh

    return pl.pallas_call(
        body, name="odd_gates", grid=(steps,),
        in_specs=_halo_specs(tm, seq, D_MODEL) + [_full((4, D_MODEL)), _full((1, D_MODEL)), _full((8, LANES, 512)),
                                                  _full((4, D_MODEL)), _full((2, D_MODEL))],
        out_specs=[_rows(tm, D_MODEL)] * 5, out_shape=[_sds((seq, D_MODEL))] * 5,
        compiler_params=_params("parallel"),
    )(xr, xr, xr, conv_w, conv_b, wcat, bias, lam)


def _scan(a, b, seq, descending, post, name):
    tb = _row_tile(seq, 512)
    steps = seq // tb
    imap = (lambda i: (steps - 1 - i, 0)) if descending else (lambda i: (i, 0))
    spec = pl.BlockSpec((tb, D_MODEL), imap)
    n_out = 1 if post else 2

    sub = 8
    tiles = tb // sub

    def body(a_ref, b_ref, *rest):
        outs, carry_h, carry_a = rest[:n_out], rest[n_out], rest[n_out + 1]

        @pl.when(pl.program_id(0) == 0)
        def _():
            carry_h[...] = jnp.zeros_like(carry_h)
            carry_a[...] = jnp.zeros_like(carry_a)

        row = lax.broadcasted_iota(jnp.int32, (sub, D_MODEL), 0)

        def shift(v, d, fill):
            if descending:
                return jnp.where(row <= sub - 1 - d, pltpu.roll(v, sub - d, 0), fill)
            return jnp.where(row >= d, pltpu.roll(v, d, 0), fill)

        def last(v):
            return jnp.broadcast_to(v[0:1, :] if descending else v[sub - 1:sub, :], v.shape)

        def tile(j, c):
            ch, ca = c
            r0 = pl.multiple_of(((tiles - 1 - j) if descending else j) * sub, sub)
            at = a_ref[pl.ds(r0, sub), :]
            bt = b_ref[pl.ds(r0, sub), :]
            coef = shift(at, 1, ca) if post else at
            acc_a, acc_b = coef, bt
            for d in (1, 2, 4):
                acc_b = acc_b + acc_a * shift(acc_b, d, 0.0)
                acc_a = acc_a * shift(acc_a, d, 1.0)
            h = acc_b + acc_a * ch
            outs[0][pl.ds(r0, sub), :] = h
            if post:
                return last(h), last(at)
            outs[1][pl.ds(r0, sub), :] = shift(h, 1, ch)
            return last(h), ca

        ch, ca = lax.fori_loop(0, tiles, tile, (carry_h[...], carry_a[...]), unroll=4)
        carry_h[...] = ch
        carry_a[...] = ca

    return pl.pallas_call(
        body, name=name, grid=(steps,), in_specs=[spec, spec], out_specs=[spec] * n_out,
        out_shape=[_sds((seq, D_MODEL))] * n_out, scratch_shapes=[pltpu.VMEM((sub, D_MODEL), F32)] * 2,
        compiler_params=_params("arbitrary"),
    )(a, b)


def _odd_out_and_loss(hf, hr, g, x1, tgt, mod, w_out, w_out_t, ln_g, ln_b, seq):
    tm = _row_tile(seq, 512)

    def body(hf_ref, hr_ref, g_ref, x_ref, t_ref, mod_ref, w_ref, wt_ref, lg_ref, lb_ref,
             dhs_ref, dg_ref, dres_ref, loss_ref, dw_ref, vec_ref):
        @pl.when(pl.program_id(0) == 0)
        def _():
            loss_ref[...] = jnp.zeros_like(loss_ref)
            dw_ref[...] = jnp.zeros_like(dw_ref)
            vec_ref[...] = jnp.zeros_like(vec_ref)

        gg = g_ref[...]
        sg = _sigmoid(gg)
        silu = gg * sg
        hsum = hf_ref[...] + hr_ref[...]
        y = hsum * silu
        out = _mm(y, w_ref[...])
        gate = mod_ref[2:3, :]
        z = ALPHA * x_ref[...] + gate * out
        zhat, rstd = _ln_stats(z)
        x2 = zhat * lg_ref[...] + lb_ref[...]
        err = x2 - t_ref[...]
        loss_ref[...] += 0.5 * jnp.sum(jnp.mean(err * err, axis=-1, keepdims=True))
        dx2 = err * (1.0 / D_MODEL)
        dz = _ln_bwd(dx2, zhat, rstd, lg_ref[...])
        vec_ref[0:1, :] += jnp.sum(dx2 * zhat, axis=0, keepdims=True)
        vec_ref[1:2, :] += jnp.sum(dx2, axis=0, keepdims=True)
        vec_ref[2:3, :] += jnp.sum(dz * out, axis=0, keepdims=True)
        dres_ref[...] = ALPHA * dz
        dout = gate * dz
        dw_ref[...] += _mm_tn(y, dout)
        dy = _mm(dout, wt_ref[...])
        dhs_ref[...] = dy * silu
        dg_ref[...] = dy * hsum * (sg * (1.0 + gg * (1.0 - sg)))

    return pl.pallas_call(
        body, name="odd_out_loss", grid=(seq // tm,),
        in_specs=[_rows(tm, D_MODEL)] * 5 + [_full((3, D_MODEL)), _const((D_MODEL, D_MODEL)), _const((D_MODEL, D_MODEL)),
                                             _full((1, D_MODEL)), _full((1, D_MODEL))],
        out_specs=[_rows(tm, D_MODEL)] * 3 + [_full((8, LANES)), _full((D_MODEL, D_MODEL)), _full((8, D_MODEL))],
        out_shape=[_sds((seq, D_MODEL))] * 3 + [_sds((8, LANES)), _sds((D_MODEL, D_MODEL)), _sds((8, D_MODEL))],
        compiler_params=_params("arbitrary"),
    )(hf, hr, g, x1, tgt, mod, w_out, w_out_t, ln_g, ln_b)


def _odd_gates_bwd(xc, gf, gr, hpf, hpr, wcat, bias, lam, seq):
    tm = _row_tile(seq, 512)
    steps = seq // tm

    def body(xc_ref, gf_ref, gr_ref, hpf_ref, hpr_ref, w_ref, bias_ref, lam_ref, dxc_ref, dw_ref, vec_ref):
        @pl.when(pl.program_id(0) == 0)
        def _():
            dw_ref[...] = jnp.zeros_like(dw_ref)
            vec_ref[...] = jnp.zeros_like(vec_ref)

        sp, dsp = _softplus_neg(lam_ref[...])
        bias = bias_ref[...]
        for h in range(RNN_HEADS):
            hs = slice(h * LANES, (h + 1) * LANES)
            xh = xc_ref[:, hs]
            gates = _lru_gates(xh, _mm(xh, w_ref[h]), bias, sp, hs)
            dxh = jnp.zeros_like(xh)
            dpre = []
            for d, (g_ref_d, hp_ref_d) in enumerate(((gf_ref, hpf_ref), (gr_ref, hpr_ref))):
                r, ig, a, s = gates[d]
                db = g_ref_d[:, hs]
                da = db * hp_ref_d[:, hs]
                dxh = dxh + db * s * ig
                dlog_a = da * a - (db * ig * xh) * (a * a / s)
                dr = dlog_a * (-RG_LRU_C) * sp[d:d + 1, hs]
                di = db * s * xh
                dpr = dr * r * (1.0 - r)
                dpi = di * ig * (1.0 - ig)
                vec_ref[2 * d:2 * d + 1, hs] += jnp.sum(dpr, axis=0, keepdims=True)
                vec_ref[2 * d + 1:2 * d + 2, hs] += jnp.sum(dpi, axis=0, keepdims=True)
                vec_ref[4 + d:5 + d, hs] += jnp.sum(dlog_a * r, axis=0, keepdims=True) * (-RG_LRU_C) * dsp[d:d + 1, hs]
                dpre += [dpr, dpi]
            dcat = jnp.concatenate(dpre, axis=1)
            dw_ref[h] += _mm_tn(xh, dcat)
            dxc_ref[:, hs] = dxh + _mm_nt(dcat, w_ref[h])

    return pl.pallas_call(
        body, name="odd_gates_bwd", grid=(steps,),
        in_specs=[_rows(tm, D_MODEL)] * 5 + [_full((8, LANES, 512)), _full((4, D_MODEL)), _full((2, D_MODEL))],
        out_specs=[_rows(tm, D_MODEL), _full((8, LANES, 512)), _full((8, D_MODEL))],
        out_shape=[_sds((seq, D_MODEL)), _sds((8, LANES, 512)), _sds((8, D_MODEL))],
        compiler_params=_params("arbitrary"),
    )(xc, gf, gr, hpf, hpr, wcat, bias, lam)


def _odd_proj_bwd(dxc, xr, dg, x1, dres, mod, conv_w, w_in_t, seq):
    tm = _row_tile(seq, 512)
    steps = seq // tm

    def body(dp_ref, dm_ref, dn_ref, xp_ref, xm_ref, xn_ref, dg_ref, x_ref, dres_ref, mod_ref, cw_ref, wt_ref,
             dx_ref, dpb_ref, vec_ref):
        i = pl.program_id(0)

        @pl.when(i == 0)
        def _():
            vec_ref[...] = jnp.zeros_like(vec_ref)

        dext = _extended(dp_ref, dm_ref, dn_ref, i, steps)
        xext = _extended(xp_ref, xm_ref, xn_ref, i, steps)
        dxc_m = dm_ref[...]
        dxr = sum(cw_ref[kk:kk + 1, :] * _shifted(dext, 2 - kk, tm) for kk in range(4))
        for kk in range(4):
            vec_ref[kk:kk + 1, :] += jnp.sum(dxc_m * _shifted(xext, kk - 2, tm), axis=0, keepdims=True)
        vec_ref[4:5, :] += jnp.sum(dxc_m, axis=0, keepdims=True)
        dpb_ref[:, :D_MODEL] = dxr.astype(dpb_ref.dtype)
        dpb_ref[:, D_MODEL:] = dg_ref[...].astype(dpb_ref.dtype)
        dh = jnp.dot(dpb_ref[...], wt_ref[...], preferred_element_type=F32)
        x = x_ref[...]
        vec_ref[5:6, :] += jnp.sum(dh, axis=0, keepdims=True)
        vec_ref[6:7, :] += jnp.sum(dh * x, axis=0, keepdims=True)
        dx_ref[...] = dres_ref[...] + dh * (1.0 + mod_ref[1:2, :])

    return pl.pallas_call(
        body, name="odd_proj_bwd", grid=(steps,),
        in_specs=_halo_specs(tm, seq, D_MODEL) + _halo_specs(tm, seq, D_MODEL) + [_rows(tm, D_MODEL)] * 3
        + [_full((3, D_MODEL)), _full((4, D_MODEL)), _full((ODD_IN, D_MODEL))],
        out_specs=[_rows(tm, D_MODEL), _rows(tm, ODD_IN), _full((8, D_MODEL))],
        out_shape=[_sds((seq, D_MODEL)), _sds((seq, ODD_IN), MXU_DTYPE), _sds((8, D_MODEL))],
        compiler_params=_params("arbitrary"),
    )(dxc, dxc, dxc, xr, xr, xr, dg, x1, dres, mod, conv_w, w_in_t)


def _tn_matmul(a, b, seq, name):
    n = b.shape[1]
    tn = n // 2
    cs = n // 4
    tm = _row_tile(seq, 512)
    steps = seq // tm

    def body(a_ref, b_ref, o_ref, acc_ref):
        i = pl.program_id(1)

        @pl.when(i == 0)
        def _():
            acc_ref[...] = jnp.zeros_like(acc_ref)

        acc_ref[...] += lax.dot_general(a_ref[...], b_ref[...], (((0,), (0,)), ((), ())), preferred_element_type=F32)

        @pl.when(i == steps - 1)
        def _():
            o_ref[0] = acc_ref[:, 0:cs]
            o_ref[1] = acc_ref[:, cs:2 * cs]

    return pl.pallas_call(
        body, name=name, grid=(2, steps),
        in_specs=[pl.BlockSpec((tm, D_MODEL), lambda j, i: (i, 0)), pl.BlockSpec((tm, tn), lambda j, i: (i, j))],
        out_specs=pl.BlockSpec((2, D_MODEL, cs), lambda j, i: (j, 0, 0)), out_shape=_sds((4, D_MODEL, cs)),
        scratch_shapes=[pltpu.VMEM((D_MODEL, tn), F32)], compiler_params=_params("parallel", "arbitrary"),
    )(a, b)


def _even_out_bwd(dx1, z, out, ycat, g, mod, ln_g, w_out_t, seq):
    tm = _row_tile(seq, 512)

    def body(dx_ref, z_ref, out_ref, y_ref, g_ref, mod_ref, lg_ref, wt_ref, dy_ref, dg_ref, dres_ref, dw_ref, vec_ref):
        @pl.when(pl.program_id(0) == 0)
        def _():
            dw_ref[...] = jnp.zeros_like(dw_ref)
            vec_ref[...] = jnp.zeros_like(vec_ref)

        zhat, rstd = _ln_stats(z_ref[...])
        dx1_ = dx_ref[...]
        dz = _ln_bwd(dx1_, zhat, rstd, lg_ref[...])
        vec_ref[0:1, :] += jnp.sum(dx1_ * zhat, axis=0, keepdims=True)
        vec_ref[1:2, :] += jnp.sum(dx1_, axis=0, keepdims=True)
        vec_ref[2:3, :] += jnp.sum(dz * out_ref[...], axis=0, keepdims=True)
        dres_ref[...] = ALPHA * dz
        dout = mod_ref[2:3, :] * dz
        gg = g_ref[...]
        sg = _sigmoid(gg)
        silu = gg * sg
        ycat_ = y_ref[...]
        dw_ref[...] += _mm_tn(ycat_ * silu, dout)
        dy = _mm(dout, wt_ref[...])
        dy_ref[...] = dy * silu
        dg_ref[...] = dy * ycat_ * (sg * (1.0 + gg * (1.0 - sg)))

    return pl.pallas_call(
        body, name="even_out_bwd", grid=(seq // tm,),
        in_specs=[_rows(tm, D_MODEL)] * 5 + [_full((3, D_MODEL)), _full((1, D_MODEL)), _const((D_MODEL, D_MODEL))],
        out_specs=[_rows(tm, D_MODEL)] * 3 + [_full((D_MODEL, D_MODEL)), _full((8, D_MODEL))],
        out_shape=[_sds((seq, D_MODEL))] * 3 + [_sds((D_MODEL, D_MODEL)), _sds((8, D_MODEL))],
        compiler_params=_params("arbitrary"),
    )(dx1, z, out, ycat, g, mod, ln_g, w_out_t)


def _even_mix_bwd(q, k, v, lse, ycat, dycat, su, sv, sink, sgln_g, sgln_b, sgw, sgb_full, e2, e8, seq):
    nb = seq // BLK

    def body(sink_ref, q_ref, k_ref, v_ref, lse_ref, y_ref, dy_ref, su_ref, sv_ref, lng_ref, lnb_ref, sgw_ref, sgb_ref, e2_ref,
             e8_ref, dq_ref, dsu_ref, dsv_ref, dk_ref, dv_ref, dsgw_ref, dsgb_ref, vec_ref, dsink_ref, dsgb_acc):
        n = pl.program_id(0)

        @pl.when(n == 0)
        def _():
            dk_ref[...] = jnp.zeros_like(dk_ref)
            dv_ref[...] = jnp.zeros_like(dv_ref)
            dsgw_ref[...] = jnp.zeros_like(dsgw_ref)
            dsgb_acc[...] = jnp.zeros_like(dsgb_acc)
            vec_ref[...] = jnp.zeros_like(vec_ref)
            dsink_ref[...] = jnp.zeros_like(dsink_ref)

        kband = _band(k_ref, n, nb)
        vband = _band(v_ref, n, nb)
        valid = _band_valid(n, seq)
        lane = _lane_iota((BLK, LANES))
        row8 = lax.broadcasted_iota(jnp.int32, (8, LANES), 0)
        lse = lse_ref[...]
        dkb = jnp.zeros((3 * BLK, LANES), F32)
        dvb = jnp.zeros((3 * BLK, LANES), F32)
        dsink = jnp.zeros((8, LANES), F32)
        for j in range(ATTN_WIDTH // LANES):
            qt = q_ref[:, j * LANES:(j + 1) * LANES].astype(F32)
            ot = y_ref[:, j * LANES:(j + 1) * LANES]
            dot_ = dy_ref[:, j * LANES:(j + 1) * LANES]
            dqt = jnp.zeros((BLK, LANES), F32)
            for h in (2 * j, 2 * j + 1):
                head_lanes = (lane < HEAD_DIM) if h % 2 == 0 else (lane >= HEAD_DIM)
                qh = _to_kv_lanes(qt, h)
                doh = _to_kv_lanes(dot_, h)
                lse_h = jnp.sum(jnp.where(lane == h, lse, 0.0), axis=1, keepdims=True)
                s = jnp.where(valid, _mm_nt(qh, kband) * (HEAD_DIM ** -0.5), NEG_INF)
                p = jnp.exp(s - lse_h)
                psink = jnp.exp(sink_ref[h] - lse_h)
                delta = jnp.sum(jnp.where(head_lanes, dot_ * ot, 0.0), axis=1, keepdims=True)
                ds = p * (_mm_nt(doh, vband) - delta) * (HEAD_DIM ** -0.5)
                dsink = dsink + jnp.where(row8 == h, -jnp.sum(psink * delta), 0.0)
                dqt = dqt + _from_kv_lanes(_mm(ds, kband), h)
                dkb = dkb + _mm_tn(ds, qh)
                dvb = dvb + _mm_tn(p, doh)
            dq_ref[:, j * LANES:(j + 1) * LANES] = dqt
        dsink_ref[...] += dsink
        prev = jnp.maximum(n - 1, 0)
        nxt = jnp.minimum(n + 1, nb - 1)
        for part, blk_i in enumerate((prev, n, nxt)):
            rows = pl.ds(pl.multiple_of(blk_i * BLK, BLK), BLK)
            dk_ref[rows, :] += dkb[part * BLK:(part + 1) * BLK]
            dv_ref[rows, :] += dvb[part * BLK:(part + 1) * BLK]

        e2 = e2_ref[...]
        lng = lng_ref[...]
        svo, vn, vhat, rstd = _sg_forward(sv_ref[...], lng, lnb_ref[...], sgw_ref, sgb_ref[...], e2)
        for j in range(SG_WIDTH // LANES):
            cs = slice(j * LANES, (j + 1) * LANES)
            dysg = dy_ref[:, ATTN_WIDTH + j * LANES:ATTN_WIDTH + (j + 1) * LANES]
            dsu_ref[:, cs] = dysg * svo[j]
            dsvo = dysg * su_ref[:, cs]
            dsgb_acc[:, cs] += dsvo
            d_lo = jnp.where(lane < HEAD_DIM, dsvo, 0.0)
            d_hi = dsvo - d_lo
            dsgw_ref[2 * j] += _mm_nt(d_lo, vn[j])
            dsgw_ref[2 * j + 1] += _mm_nt(d_hi, vn[j])
            dvn = _mm_tn(sgw_ref[2 * j], d_lo) + _mm_tn(sgw_ref[2 * j + 1], d_hi)
            vec_ref[0:1, cs] += jnp.sum(dvn * vhat[j], axis=0, keepdims=True)
            vec_ref[1:2, cs] += jnp.sum(dvn, axis=0, keepdims=True)
            dvh = dvn * lng[:, cs]
            m1 = _group_sum(dvh, e2) * (1.0 / HEAD_DIM)
            m2 = _group_sum(dvh * vhat[j], e2) * (1.0 / HEAD_DIM)
            dsv_ref[:, cs] = rstd[j] * (dvh - m1 - vhat[j] * m2)

        @pl.when(n == nb - 1)
        def _():
            rest = dsgb_acc[...]
            total = jnp.zeros((8, BLK), F32)
            for _ in range(3):
                part = rest.astype(MXU_DTYPE)
                total = total + lax.dot_general(e8_ref[...], part, (((1,), (1,)), ((), ())), preferred_element_type=F32)
                rest = rest - part.astype(F32)
            dsgb_ref[...] = total

    blk = lambda w: pl.BlockSpec((BLK, w), lambda n: (n, 0))
    return pl.pallas_call(
        body, name="even_mix_bwd", grid=(nb,),
        in_specs=[pl.BlockSpec(memory_space=pltpu.SMEM), blk(512), _full((seq, LANES)), _full((seq, LANES)), blk(LANES),
                  blk(D_MODEL), blk(D_MODEL), blk(512), blk(512), _full((1, 512)), _full((1, 512)), _full((8, BLK, BLK)),
                  _full((BLK, 512)), _full((LANES, LANES)), _full((8, 512))],
        out_specs=[blk(512), blk(512), blk(512), _full((seq, LANES)), _full((seq, LANES)), _full((8, BLK, BLK)),
                   _full((8, BLK)), _full((8, 512)), _full((8, LANES))],
        out_shape=[_sds((seq, 512)), _sds((seq, 512)), _sds((seq, 512)), _sds((seq, LANES)), _sds((seq, LANES)),
                   _sds((8, BLK, BLK)), _sds((8, BLK)), _sds((8, 512)), _sds((8, LANES))],
        scratch_shapes=[pltpu.VMEM((BLK, 512), F32)],
        compiler_params=_params("arbitrary"),
    )(sink, q, k, v, lse, ycat, dycat, su, sv, sgln_g, sgln_b, sgw, sgb_full, e2, e8)


def _even_proj_bwd(dq, dk, dv, dsu, dsv, dg, x, dres, mod, tabs, w_in_t, seq):
    tm = _row_tile(seq, 512)

    def body(dq_ref, dk_ref, dv_ref, dsu_ref, dsv_ref, dg_ref, x_ref, dres_ref, mod_ref, cos_ref, sp_ref, sm_ref, wt_ref,
             dx_ref, dpb_ref, vec_ref):
        @pl.when(pl.program_id(0) == 0)
        def _():
            vec_ref[...] = jnp.zeros_like(vec_ref)

        cos_t, sin_p, sin_m = cos_ref[...], sp_ref[...], sm_ref[...]
        dt = dpb_ref.dtype
        for j in range(ATTN_WIDTH // LANES):
            cs = slice(j * LANES, (j + 1) * LANES)
            dpb_ref[:, cs] = _rope_t(dq_ref[:, cs], cos_t, sin_p, sin_m).astype(dt)
        dpb_ref[:, 512:640] = _rope_t(dk_ref[...], cos_t, sin_p, sin_m).astype(dt)
        dpb_ref[:, 640:768] = dv_ref[...].astype(dt)
        dpb_ref[:, 768:1280] = dsu_ref[...].astype(dt)
        dpb_ref[:, 1280:1792] = dsv_ref[...].astype(dt)
        dpb_ref[:, 1792:2816] = dg_ref[...].astype(dt)
        dh = jnp.dot(dpb_ref[...], wt_ref[...], preferred_element_type=F32)
        x_ = x_ref[...]
        vec_ref[0:1, :] += jnp.sum(dh, axis=0, keepdims=True)
        vec_ref[1:2, :] += jnp.sum(dh * x_, axis=0, keepdims=True)
        dx_ref[...] = dres_ref[...] + dh * (1.0 + mod_ref[1:2, :])

    return pl.pallas_call(
        body, name="even_proj_bwd", grid=(seq // tm,),
        in_specs=[_rows(tm, 512), _rows(tm, LANES), _rows(tm, LANES), _rows(tm, 512), _rows(tm, 512), _rows(tm, D_MODEL),
                  _rows(tm, D_MODEL), _rows(tm, D_MODEL), _full((3, D_MODEL))] + [_rows(tm, LANES)] * 3
        + [_full((EVEN_IN, D_MODEL))],
        out_specs=[_rows(tm, D_MODEL), _rows(tm, EVEN_IN), _full((8, D_MODEL))],
        out_shape=[_sds((seq, D_MODEL)), _sds((seq, EVEN_IN), MXU_DTYPE), _sds((8, D_MODEL))],
        compiler_params=_params("arbitrary"),
    )(dq, dk, dv, dsu, dsv, dg, x, dres, mod, *tabs, w_in_t)


def _local_step(x, posf, tgt, mod, w, seq):
    mxu = lambda a: a.astype(MXU_DTYPE)
    row = lambda a: a.reshape(1, -1)
    tabs = _rope_tables(posf, seq)
    e2 = mxu(jnp.kron(jnp.eye(2, dtype=F32), jnp.ones((HEAD_DIM, HEAD_DIM), F32)))
    e8 = mxu(jnp.repeat(jnp.eye(N_SG_GROUPS, dtype=F32), HEAD_DIM, axis=1))
    sgw = mxu(w["ev_sg_w"])
    sgb_full = jnp.repeat(w["ev_sg_b"].T, HEAD_DIM, axis=1)
    sgln_g, sgln_b = row(w["ev_sg_ln_g"]), row(w["ev_sg_ln_b"])
    sink = w["ev_sink"].reshape(N_Q_HEADS)
    ev_w_in, ev_w_out = _assemble_cols(mxu(w["ev_w_in"])), mxu(w["ev_w_out"])
    od_w_in, od_w_out = mxu(w["od_w_in"]), mxu(w["od_w_out"])
    od_w_in_t = jnp.swapaxes(od_w_in, 1, 2).reshape(ODD_IN, D_MODEL)
    wcat = mxu(jnp.concatenate([w["od_w_a"][0], w["od_w_x"][0], w["od_w_a"][1], w["od_w_x"][1]], axis=2))
    gate_bias = jnp.stack([w["od_b_a"][0], w["od_b_x"][0], w["od_b_a"][1], w["od_b_x"][1]])
    conv_b = row(w["od_conv_b"])
    ln_g, ln_b = w["ln_g"], w["ln_b"]

    h0, q, k, v, su, sv, g0 = _even_proj(x, mod[0], ev_w_in, tabs, seq)
    ycat, lse = _even_mix(q, k, v, su, sv, sink, sgln_g, sgln_b, sgw, sgb_full, e2, seq)
    out0, z0, x1 = _even_out(ycat, g0, x, mod[0], ev_w_out, ln_g[0:1], ln_b[0:1], seq)
    h1, xr, g1 = _odd_proj(x1, mod[1], od_w_in, seq)
    xc, a_f, b_f, a_r, b_r = _odd_gates(xr, w["od_conv_w"], conv_b, wcat, gate_bias, w["od_lam"], seq)
    hf, hpf = _scan(a_f, b_f, seq, descending=False, post=False, name="scan_fwd")
    hr, hpr = _scan(a_r, b_r, seq, descending=True, post=False, name="scan_rev")
    dhs, dg1, dres1, loss, d_od_w_out, vec_o = _odd_out_and_loss(hf, hr, g1, x1, tgt, mod[1], od_w_out, od_w_out.T,
                                                                   ln_g[1:2], ln_b[1:2], seq)
    (gf,) = _scan(a_f, dhs, seq, descending=True, post=True, name="scan_fwd_bwd")
    (gr,) = _scan(a_r, dhs, seq, descending=False, post=True, name="scan_rev_bwd")
    dxc, d_wcat, vec_g = _odd_gates_bwd(xc, gf, gr, hpf, hpr, wcat, gate_bias, w["od_lam"], seq)
    dx1, dp1, vec_p = _odd_proj_bwd(dxc, xr, dg1, x1, dres1, mod[1], w["od_conv_w"], od_w_in_t, seq)
    d_od_w_in = _tn_matmul(h1, dp1, seq, "odd_dw_in")
    dycat, dg0, dres0, d_ev_w_out, vec_e = _even_out_bwd(dx1, z0, out0, ycat, g0, mod[0], ln_g[0:1], ev_w_out.T, seq)
    dq, dsu, dsv, dk, dv, d_sgw, d_sgb, vec_s, d_sink = _even_mix_bwd(q, k, v, lse, ycat, dycat, su, sv, sink, sgln_g, sgln_b,
                                                                      sgw, sgb_full, e2, e8, seq)
    grad_x, dp0, vec_x = _even_proj_bwd(dq, dk, dv, dsu, dsv, dg0, x, dres0, mod[0], tabs, ev_w_in.T, seq)
    d_ev_w_in = _tn_matmul(h0, dp0, seq, "even_dw_in")

    dmod = jnp.stack([jnp.stack([vec_x[0], vec_x[1], vec_e[2]]), jnp.stack([vec_p[5], vec_p[6], vec_o[2]])])
    grads = {
        "ln_g": jnp.stack([vec_e[0], vec_o[0]]), "ln_b": jnp.stack([vec_e[1], vec_o[1]]),
        "ev_w_in": d_ev_w_in, "ev_w_out": d_ev_w_out, "ev_sink": d_sink[:, 0],
        "ev_sg_ln_g": vec_s[0], "ev_sg_ln_b": vec_s[1], "ev_sg_w": d_sgw,
        "ev_sg_b": d_sgb,
        "od_w_in": d_od_w_in, "od_conv_w": vec_p[0:4], "od_conv_b": vec_p[4],
        "od_w_a": jnp.stack([d_wcat[:, :, 0:128], d_wcat[:, :, 256:384]]),
        "od_w_x": jnp.stack([d_wcat[:, :, 128:256], d_wcat[:, :, 384:512]]),
        "od_b_a": jnp.stack([vec_g[0], vec_g[2]]), "od_b_x": jnp.stack([vec_g[1], vec_g[3]]),
        "od_lam": vec_g[4:6], "od_w_out": d_od_w_out,
    }
    return loss[0, 0], grad_x, dmod, grads


def _place():
    return lax.axis_index("x"), lax.axis_index("y"), lax.axis_index("c")


def _allgather8(block, name):
    m_per, n = block.shape

    def body(x_ref, out_ref, send_sems, recv_sems, local_sem):
        x, y, c = _place()
        me, sibling = (x, y, c), (x, y, 1 - c)
        chips = [(1 - x, y), (x, 1 - y), (1 - x, 1 - y)]

        def rows(px, py, pc):
            return out_ref.at[pl.ds((4 * px + 2 * py + pc) * m_per, m_per), :]

        def copy(k, blk, to, src=None):
            return pltpu.make_async_remote_copy(src_ref=rows(*blk) if src is None else src, dst_ref=rows(*blk),
                                                send_sem=send_sems.at[k], recv_sem=recv_sems.at[k], device_id=to,
                                                device_id_type=MESH)

        mine = pltpu.make_async_copy(x_ref, rows(*me), local_sem)
        mine.start()
        first = [copy(0, me, sibling, src=x_ref)] + [copy(1 + j, me, (*chip, c), src=x_ref) for j, chip in enumerate(chips)]
        for cp in first:
            cp.start()
        passed = [copy(4 + j, (*chip, c), sibling) for j, chip in enumerate(chips)]
        for j, chip in enumerate(chips):
            copy(1 + j, (*chip, c), me).wait_recv()
            passed[j].start()
        copy(0, sibling, me).wait_recv()
        for j, chip in enumerate(chips):
            copy(4 + j, (*chip, 1 - c), me).wait_recv()
        for cp in first + passed:
            cp.wait_send()
        mine.wait()

    return pl.pallas_call(
        body, name=name, out_shape=_sds((8 * m_per, n), block.dtype),
        in_specs=[pl.BlockSpec(memory_space=pltpu.VMEM)], out_specs=pl.BlockSpec(memory_space=pltpu.VMEM),
        scratch_shapes=[pltpu.SemaphoreType.DMA((7,)), pltpu.SemaphoreType.DMA((7,)), pltpu.SemaphoreType.DMA],
        compiler_params=pltpu.CompilerParams(vmem_limit_bytes=VMEM_LIMIT),
    )(block)


class _Copies:
    def __init__(self, send_sems, recv_sems, local_sems, stages):
        self.send_sems, self.recv_sems, self.local_sems, self.stages = send_sems, recv_sems, local_sems, stages
        self.sent, self.staged, self.locals = [], [], []

    def remote(self, k, src, dst, to):
        return pltpu.make_async_remote_copy(src_ref=src, dst_ref=dst, send_sem=self.send_sems.at[k], recv_sem=self.recv_sems.at[k],
                                            device_id=to, device_id_type=MESH)

    def send(self, k, src, dst, to):
        cp = self.remote(k, src, dst, to)
        cp.start()
        self.sent.append(cp)

    def arrived(self, k, dst, frm):
        self.remote(k, dst, dst, frm).wait_recv()

    def local(self, src, dst):
        k = len(self.staged)
        cp = pltpu.make_async_copy(src, self.stages[k], self.local_sems.at[2 * k])
        cp.start()
        self.staged.append((cp, dst))

    def flush(self):
        for k in range(len(self.locals), len(self.staged)):
            cp, dst = self.staged[k]
            cp.wait()
            out = pltpu.make_async_copy(self.stages[k], dst, self.local_sems.at[2 * k + 1])
            out.start()
            self.locals.append(out)

    def drain(self):
        self.flush()
        for cp in self.sent:
            cp.wait_send()
        for cp in self.locals:
            cp.wait()


def _comm_call(body, name, ins, out_shapes, n_remote, stages):
    n_in, n_out = len(ins), len(out_shapes)

    def kern(*refs):
        in_refs, out_refs = refs[:n_in], refs[n_in:n_in + n_out]
        send_sems, recv_sems, local_sems = refs[n_in + n_out:n_in + n_out + 3]
        body(_Copies(send_sems, recv_sems, local_sems, refs[n_in + n_out + 3:]), in_refs, out_refs)

    hbm = pl.BlockSpec(memory_space=pl.ANY)
    return pl.pallas_call(
        kern, name=name, out_shape=out_shapes, in_specs=[hbm] * n_in, out_specs=[hbm] * n_out,
        scratch_shapes=[pltpu.SemaphoreType.DMA((n_remote,)), pltpu.SemaphoreType.DMA((n_remote,)),
                        pltpu.SemaphoreType.DMA((2 * len(stages),))] + [pltpu.VMEM(s, d) for s, d in stages],
        compiler_params=pltpu.CompilerParams(vmem_limit_bytes=VMEM_LIMIT),
    )(*ins)


def _gather_to_all(cps, src, dst, me, sibling, other_chips, c, base):
    idx = lambda p: 4 * p[0] + 2 * p[1] + p[2]
    cps.local(src, dst.at[idx(me)])
    cps.send(base, src, dst.at[idx(me)], sibling)
    for j, chip in enumerate(other_chips):
        cps.send(base + 1 + j, src, dst.at[idx(me)], (*chip, c))
    cps.flush()
    for j, chip in enumerate(other_chips):
        got = dst.at[idx((*chip, c))]
        cps.arrived(base + 1 + j, got, (*chip, c))
        cps.send(base + 4 + j, got, got, sibling)
    cps.arrived(base, dst.at[idx(sibling)], sibling)
    for j, chip in enumerate(other_chips):
        cps.arrived(base + 4 + j, dst.at[idx((*chip, 1 - c))], sibling)


def _gather_weights(shards, small):
    n = len(shards)

    def body(cps, ins, outs):
        x, y, c = _place()
        me, sibling, mine = (x, y, c), (x, y, 1 - c), 2 * x + y
        chips = [(1 - x, y), (x, 1 - y), (1 - x, 1 - y)]
        for i in range(n):
            cps.local(ins[i], outs[i].at[mine])
        for j, (px, py) in enumerate(chips):
            for i in range(n):
                hr = shards[i].shape[0] // 2
                rows = pl.ds(c * hr, hr)
                cps.send(6 * i + j, ins[i].at[rows], outs[i].at[mine, rows], (px, py, c))
        _gather_to_all(cps, ins[n], outs[n], me, sibling, chips, c, 6 * n)
        for j, (px, py) in enumerate(chips):
            for i in range(n):
                hr = shards[i].shape[0] // 2
                got = outs[i].at[2 * px + py, pl.ds(c * hr, hr)]
                cps.arrived(6 * i + j, got, (px, py, c))
                cps.send(6 * i + 3 + j, got, got, sibling)
        for j, (px, py) in enumerate(chips):
            for i in range(n):
                hr = shards[i].shape[0] // 2
                cps.arrived(6 * i + 3 + j, outs[i].at[2 * px + py, pl.ds((1 - c) * hr, hr)], sibling)
        cps.drain()

    return _comm_call(body, "gather_weights", list(shards) + [small],
                      [_sds((4,) + s.shape, s.dtype) for s in shards] + [_sds((8,) + small.shape, small.dtype)], 6 * n + 7,
                      [(a.shape, a.dtype) for a in list(shards) + [small]])


def _reduce_sibling(parts, dmod_rows):
    n = len(parts)

    def body(cps, ins, outs):
        x, y, c = _place()
        me, sibling = (x, y, c), (x, y, 1 - c)
        chips = [(1 - x, y), (x, 1 - y), (1 - x, 1 - y)]
        for i in range(n):
            cps.send(i, ins[i].at[:, 1 - c], outs[i], sibling)
        _gather_to_all(cps, ins[n], outs[n], me, sibling, chips, c, n)
        for i in range(n):
            cps.arrived(i, outs[i], sibling)
        cps.drain()

    return _comm_call(body, "reduce_sibling", list(parts) + [dmod_rows],
                      [_sds((4,) + p.shape[2:], p.dtype) for p in parts] + [_sds((8,) + dmod_rows.shape, dmod_rows.dtype)], n + 7,
                      [(dmod_rows.shape, dmod_rows.dtype)])


def _reduce_chips(parts):
    n = len(parts)

    def body(cps, ins, outs):
        x, y, c = _place()
        mine = 2 * x + y
        chips = [(1 - x, y), (x, 1 - y), (1 - x, 1 - y)]
        for i in range(n):
            cps.local(ins[i].at[mine], outs[i].at[mine])
        for j, (px, py) in enumerate(chips):
            for i in range(n):
                cps.send(3 * i + j, ins[i].at[2 * px + py], outs[i].at[mine], (px, py, c))
        cps.flush()
        for j, (px, py) in enumerate(chips):
            for i in range(n):
                cps.arrived(3 * i + j, outs[i].at[2 * px + py], (px, py, c))
        cps.drain()

    return _comm_call(body, "reduce_chips", list(parts), [_sds(p.shape, p.dtype) for p in parts], 3 * n,
                      [(p.shape[1:], p.dtype) for p in parts])


def _gather_reduced(shard_parts, repl_parts):
    ns, nr = len(shard_parts), len(repl_parts)

    def body(cps, ins, outs):
        x, y, c = _place()
        me, sibling = (x, y, c), (x, y, 1 - c)
        chips = [(1 - x, y), (x, 1 - y), (1 - x, 1 - y)]
        for i in range(ns):
            cps.local(ins[i], outs[i].at[c])
            cps.send(i, ins[i], outs[i].at[c], sibling)
        for i in range(nr):
            _gather_to_all(cps, ins[ns + i], outs[ns + i], me, sibling, chips, c, ns + 7 * i)
        for i in range(ns):
            cps.arrived(i, outs[i].at[1 - c], sibling)
        cps.drain()

    return _comm_call(body, "gather_reduced", list(shard_parts) + list(repl_parts),
                      [_sds((2,) + p.shape, p.dtype) for p in shard_parts] + [_sds((8,) + p.shape, p.dtype) for p in repl_parts],
                      ns + 7 * nr, [(p.shape, p.dtype) for p in list(shard_parts) + list(repl_parts)])


def _sum_sibling(core, parts, got, wire):
    n = len(parts)

    def body(core_ref, *refs):
        for i in range(n):
            refs[2 * n + i][0] = (refs[i][0] + refs[n + i][0]).astype(wire[i])

    keep_spec = lambda p: pl.BlockSpec((1, None) + p.shape[2:], lambda s, core_ref: (s, core_ref[0], 0, 0))
    slot_spec = lambda p: pl.BlockSpec((1,) + p.shape[2:], lambda s, core_ref: (s, 0, 0))
    return pl.pallas_call(
        body, name="sum_sibling",
        grid_spec=pltpu.PrefetchScalarGridSpec(
            num_scalar_prefetch=1, grid=(4,), in_specs=[keep_spec(p) for p in parts] + [slot_spec(p) for p in parts],
            out_specs=[slot_spec(p) for p in parts]),
        out_shape=[_sds((4,) + p.shape[2:], wire[i]) for i, p in enumerate(parts)],
        compiler_params=_params("parallel"),
    )(core, *parts, *got)


def _sum_slots(slots, name):
    n = len(slots)

    def spec_pair(p):
        k, rows, cols = p.shape
        sub = 16 if p.dtype == BF16 else 8
        if (rows // 2) % sub == 0:
            return pl.BlockSpec((k, rows // 2, cols), lambda i: (0, i, 0)), pl.BlockSpec((rows // 2, cols), lambda i: (i, 0))
        return pl.BlockSpec((k, rows, cols), lambda i: (0, 0, 0)), pl.BlockSpec((rows, cols), lambda i: (0, 0))

    pairs = [spec_pair(p) for p in slots]

    def body(*refs):
        for i in range(n):
            acc = refs[i][0].astype(F32)
            for j in range(1, slots[i].shape[0]):
                acc = acc + refs[i][j].astype(F32)
            refs[n + i][...] = acc

    return pl.pallas_call(
        body, name=name, grid=(2,), in_specs=[a for a, _ in pairs], out_specs=[b for _, b in pairs],
        out_shape=[_sds(p.shape[1:]) for p in slots], compiler_params=_params("arbitrary"),
    )(*slots)


def _modulation(c_all, ada_w, ada_b):
    cols = ada_w.shape[2]

    def body(c_ref, w_ref, b_ref, o_ref):
        cc = c_ref[...]
        o_ref[0] = _mm(cc * _sigmoid(cc), w_ref[0]) + b_ref[0]

    return pl.pallas_call(
        body, name="modulation", grid=(2,),
        in_specs=[_full((8, D_MODEL)), pl.BlockSpec((1, D_MODEL, cols), lambda l: (l, 0, 0)), pl.BlockSpec((1, 1, cols), lambda l: (l, 0, 0))],
        out_specs=pl.BlockSpec((1, 8, cols), lambda l: (l, 0, 0)), out_shape=_sds((2, 8, cols)),
        compiler_params=_params("parallel"),
    )(c_all, ada_w, ada_b)


def _adamw_math(w, g, m, v):
    m = ADAM_B1 * m + (1.0 - ADAM_B1) * g
    v = ADAM_B2 * v + (1.0 - ADAM_B2) * (g * g)
    m_hat = m / (1.0 - ADAM_B1 ** ADAM_STEP)
    v_hat = v / (1.0 - ADAM_B2 ** ADAM_STEP)
    delta = -ADAM_LR * (m_hat / (jnp.sqrt(v_hat) + ADAM_EPS) + ADAM_WD * w)
    return delta, m, v


def _ada_update(c_all, dmod, w, m, v):
    cols = w.shape[2]
    tr = 256
    spec3 = pl.BlockSpec((1, tr, cols), lambda l, i: (l, i, 0))

    def body(c_ref, d_ref, w_ref, m_ref, v_ref, g_ref, dl_ref, nm_ref, nv_ref):
        cc = c_ref[...]
        g = _mm_tn(cc * _sigmoid(cc), d_ref[0])
        g_ref[0] = g
        dl_ref[0], nm_ref[0], nv_ref[0] = _adamw_math(w_ref[0], g, m_ref[0], v_ref[0])

    return pl.pallas_call(
        body, name="ada_update", grid=(2, D_MODEL // tr),
        in_specs=[pl.BlockSpec((8, tr), lambda l, i: (0, i)), pl.BlockSpec((1, 8, cols), lambda l, i: (l, 0, 0)), spec3, spec3, spec3],
        out_specs=[spec3] * 4, out_shape=[_sds(w.shape)] * 4, compiler_params=_params("parallel", "parallel"),
    )(c_all, dmod, w, m, v)


def _adamw(w, g, m, v, name):
    rows, n = w.shape
    tr = next(t for t in (256, 128, 64, 32, 16, 8, rows) if rows % t == 0)

    def body(w_ref, g_ref, m_ref, v_ref, dl_ref, nm_ref, nv_ref):
        dl_ref[...], nm_ref[...], nv_ref[...] = _adamw_math(w_ref[...], g_ref[...], m_ref[...], v_ref[...])

    return pl.pallas_call(body, name=name, grid=(rows // tr,), in_specs=[_rows(tr, n)] * 4, out_specs=[_rows(tr, n)] * 3,
                          out_shape=[_sds((rows, n))] * 3, compiler_params=_params("parallel"))(w, g, m, v)


def _adamw_small(params):
    n = len(params)

    def body(*refs):
        ins, outs = refs[:4 * n], refs[4 * n:]
        for j in range(n):
            w_ref, g_ref, m_ref, v_ref = ins[4 * j:4 * j + 4]
            outs[3 * j][...], outs[3 * j + 1][...], outs[3 * j + 2][...] = _adamw_math(w_ref[...], g_ref[...], m_ref[...], v_ref[...])

    flat = [a for p in params for a in p]
    res = pl.pallas_call(body, name="adamw_small", out_shape=[_sds(p[0].shape) for p in params for _ in range(3)])(*flat)
    return [tuple(res[3 * j:3 * j + 3]) for j in range(n)]


def _cols(a, start, size):
    return lax.dynamic_slice_in_dim(a, start, size, axis=a.ndim - 1)


def kernel(x, c, positions, ada_w, ada_b, ln_g, ln_b, ev_w_in, ev_w_out, ev_sink, ev_sg_ln_g, ev_sg_ln_b, ev_sg_w, ev_sg_b, od_w_in, od_conv_w, od_conv_b, od_w_a, od_b_a, od_w_x, od_b_x, od_lam, od_w_out, loss_target, m_ada_w, m_ada_b, m_ln_g, m_ln_b, m_ev_w_in, m_ev_w_out, m_ev_sink, m_ev_sg_ln_g, m_ev_sg_ln_b, m_ev_sg_w, m_ev_sg_b, m_od_w_in, m_od_conv_w, m_od_conv_b, m_od_w_a, m_od_b_a, m_od_w_x, m_od_b_x, m_od_lam, m_od_w_out, v_ada_w, v_ada_b, v_ln_g, v_ln_b, v_ev_w_in, v_ev_w_out, v_ev_sink, v_ev_sg_ln_g, v_ev_sg_ln_b, v_ev_sg_w, v_ev_sg_b, v_od_w_in, v_od_conv_w, v_od_conv_b, v_od_w_a, v_od_b_a, v_od_w_x, v_od_b_x, v_od_lam, v_od_w_out):
    seq = x.shape[1]
    px, py, pc = _place()
    chip = 2 * px + py
    dev = 2 * chip + pc

    small = jnp.concatenate([od_conv_w[0].reshape(-1), od_conv_b[0], od_b_a[0].reshape(-1), jnp.zeros((256,), F32),
                             od_b_x[0].reshape(-1), od_lam[0].reshape(-1)]).reshape(3, D_MODEL)
    blk = jnp.concatenate([c, small, jnp.zeros((4, D_MODEL), F32)], axis=0)
    wire_w = lambda a: a[0].astype(MXU_DTYPE)
    ev_w_in4, ev_w_out4, od_w_in4, od_w_out4, g_small = _gather_weights(
        [wire_w(ev_w_in), wire_w(ev_w_out), wire_w(od_w_in), wire_w(od_w_out)], blk)
    c_all = g_small[:, 0, :]
    per_chip = g_small[0::2]
    conv_w = per_chip[:, 1].reshape(4, 4, 256).transpose(1, 0, 2).reshape(4, D_MODEL)
    conv_b = per_chip[:, 2, 0:256].reshape(D_MODEL)
    b_a = per_chip[:, 2, 256:768].reshape(4, 2, 256).transpose(1, 0, 2).reshape(2, D_MODEL)
    b_x = per_chip[:, 3, 0:512].reshape(4, 2, 256).transpose(1, 0, 2).reshape(2, D_MODEL)
    lam = per_chip[:, 3, 512:1024].reshape(4, 2, 256).transpose(1, 0, 2).reshape(2, D_MODEL)

    w_full = {
        "ev_w_in": ev_w_in4, "ev_w_out": ev_w_out4.reshape(D_MODEL, D_MODEL),
        "od_w_in": od_w_in4, "od_w_out": od_w_out4.reshape(D_MODEL, D_MODEL),
        "ev_sink": ev_sink[0], "ev_sg_ln_g": ev_sg_ln_g[0], "ev_sg_ln_b": ev_sg_ln_b[0], "ev_sg_w": ev_sg_w[0],
        "ev_sg_b": ev_sg_b[0], "od_conv_w": conv_w, "od_conv_b": conv_b, "od_w_a": od_w_a[0], "od_b_a": b_a,
        "od_w_x": od_w_x[0], "od_b_x": b_x, "od_lam": lam, "ln_g": ln_g, "ln_b": ln_b,
    }

    ada_cols = ada_w.shape[2]
    mod_sh = _modulation(c_all, ada_w, _cols(ada_b, chip * ada_cols, ada_cols).reshape(2, 1, ada_cols))
    mod_all = _allgather8(mod_sh.reshape(16, ada_cols), "gather_mod").reshape(4, 2, 2, 8, ada_cols)[:, 0]
    mod_mine = lax.dynamic_index_in_dim(mod_all, dev, axis=2, keepdims=False)
    mod = mod_mine.transpose(1, 0, 2).reshape(2, 3, D_MODEL)

    posf = positions.astype(F32).reshape(seq, 1)
    loss_local, grad_x, dmod, g = _local_step(x[0], posf, loss_target[0], mod, w_full, seq)

    pad = lambda a, n: jnp.concatenate([a.reshape(-1), jnp.zeros((n - a.size,), F32)])
    rows_small = jnp.concatenate([
        dmod.reshape(6, D_MODEL), g["ln_g"][0:1], g["ln_b"][0:1], g["ln_g"][1:2], g["ln_b"][1:2],
        jnp.concatenate([g["ev_sg_ln_g"], g["ev_sg_ln_b"]]).reshape(1, D_MODEL), g["ev_sg_b"].reshape(1, D_MODEL),
        g["od_conv_w"], g["od_conv_b"].reshape(1, D_MODEL), g["od_b_a"], g["od_b_x"], g["od_lam"],
        pad(g["ev_sink"], D_MODEL).reshape(1, D_MODEL), pad(loss_local, D_MODEL).reshape(1, D_MODEL),
        jnp.zeros((39, D_MODEL), F32)], axis=0)
    parts = [g["ev_w_in"].reshape(4, 2, 512, 704), g["ev_w_out"].reshape(4, 2, 128, D_MODEL),
             g["od_w_in"].reshape(4, 2, 512, 512), g["od_w_out"].reshape(4, 2, 128, D_MODEL),
             g["ev_sg_w"].reshape(4, 2, BLK, BLK), g["od_w_a"].reshape(4, 2, 2 * BLK, BLK),
             g["od_w_x"].reshape(4, 2, 2 * BLK, BLK), rows_small.reshape(4, 2, 8, D_MODEL)]
    wire = [MXU_DTYPE] * 7 + [F32]
    dmod_blk = jnp.concatenate([dmod.reshape(6, D_MODEL), jnp.zeros((2, D_MODEL), F32)], axis=0)
    *got, dmod_gathered = _reduce_sibling(parts, dmod_blk)
    chip_sums = _sum_sibling(pc.astype(jnp.int32).reshape(1), parts, got, wire)
    mine = _sum_slots(_reduce_chips(chip_sums), "sum_chips")
    reduced = _gather_reduced(mine[:4], mine[4:])
    g_ev_w_in = reduced[0].reshape(D_MODEL, 704)
    g_ev_w_out = reduced[1].reshape(256, D_MODEL)
    g_od_w_in = reduced[2].reshape(D_MODEL, 512)
    g_od_w_out = reduced[3].reshape(256, D_MODEL)
    g_sg_w = reduced[4].reshape(8 * BLK, BLK)
    g_w_a = reduced[5].reshape(16 * BLK, BLK)
    g_w_x = reduced[6].reshape(16 * BLK, BLK)
    gs = reduced[7].reshape(64, D_MODEL)
    loss = gs[24, 0]
    dmod_all = dmod_gathered[:, 0:6].reshape(8, 2, 3 * D_MODEL)
    dmod_sh = _cols(dmod_all, chip * ada_cols, ada_cols).transpose(1, 0, 2)
    g_ada_w, d_ada_w, nm_ada_w, nv_ada_w = _ada_update(c_all, dmod_sh, ada_w, m_ada_w, v_ada_w)

    big = {}
    for name, w_, g_, m_, v_ in (
            ("ev_w_in", ev_w_in, g_ev_w_in, m_ev_w_in, v_ev_w_in), ("ev_w_out", ev_w_out, g_ev_w_out, m_ev_w_out, v_ev_w_out),
            ("od_w_in", od_w_in, g_od_w_in, m_od_w_in, v_od_w_in), ("od_w_out", od_w_out, g_od_w_out, m_od_w_out, v_od_w_out),
            ("ev_sg_w", ev_sg_w, g_sg_w, m_ev_sg_w, v_ev_sg_w), ("od_w_a", od_w_a, g_w_a, m_od_w_a, v_od_w_a),
            ("od_w_x", od_w_x, g_w_x, m_od_w_x, v_od_w_x)):
        two_d = lambda a: a.reshape(g_.shape)
        d_, nm_, nv_ = _adamw(two_d(w_), g_, two_d(m_), two_d(v_), "adamw_" + name)
        big[name] = tuple(a.reshape(w_.shape) for a in (g_, d_, nm_, nv_))
    big["ada_w"] = (g_ada_w, d_ada_w, nm_ada_w, nv_ada_w)

    sh = lambda a: _cols(a, chip * 256, 256)
    small_g = {
        "ada_b": gs[0:6].reshape(2, 3 * D_MODEL), "ln_g": jnp.stack([gs[6], gs[8]]), "ln_b": jnp.stack([gs[7], gs[9]]),
        "ev_sink": gs[23:24, 0:8], "ev_sg_ln_g": gs[10:11, 0:512], "ev_sg_ln_b": gs[10:11, 512:1024],
        "ev_sg_b": gs[11].reshape(8, BLK), "od_conv_w": sh(gs[12:16]), "od_conv_b": sh(gs[16:17]), "od_b_a": sh(gs[17:19]),
        "od_b_x": sh(gs[19:21]), "od_lam": sh(gs[21:23]),
    }
    small_in = {"ada_b": (ada_b, m_ada_b, v_ada_b), "ln_g": (ln_g, m_ln_g, v_ln_g), "ln_b": (ln_b, m_ln_b, v_ln_b),
                "ev_sink": (ev_sink, m_ev_sink, v_ev_sink), "ev_sg_ln_g": (ev_sg_ln_g, m_ev_sg_ln_g, v_ev_sg_ln_g),
                "ev_sg_ln_b": (ev_sg_ln_b, m_ev_sg_ln_b, v_ev_sg_ln_b), "ev_sg_b": (ev_sg_b, m_ev_sg_b, v_ev_sg_b),
                "od_conv_w": (od_conv_w, m_od_conv_w, v_od_conv_w), "od_conv_b": (od_conv_b, m_od_conv_b, v_od_conv_b),
                "od_b_a": (od_b_a, m_od_b_a, v_od_b_a), "od_b_x": (od_b_x, m_od_b_x, v_od_b_x),
                "od_lam": (od_lam, m_od_lam, v_od_lam)}
    names_small = list(small_g)
    upd = _adamw_small([(small_in[n][0].reshape(small_g[n].shape), small_g[n], small_in[n][1].reshape(small_g[n].shape),
                         small_in[n][2].reshape(small_g[n].shape)) for n in names_small])
    res = dict(big)
    for n, (d_, nm_, nv_) in zip(names_small, upd):
        shape = small_in[n][0].shape
        res[n] = tuple(a.reshape(shape) for a in (small_g[n], d_, nm_, nv_))

    order = ["ada_w", "ada_b", "ln_g", "ln_b", "ev_w_in", "ev_w_out", "ev_sink", "ev_sg_ln_g", "ev_sg_ln_b", "ev_sg_w", "ev_sg_b",
             "od_w_in", "od_conv_w", "od_conv_b", "od_w_a", "od_b_a", "od_w_x", "od_b_x", "od_lam", "od_w_out"]
    return (loss, grad_x.reshape(x.shape), *[res[n][0] for n in order], *[res[n][1] for n in order],
            *[res[n][2] for n in order], *[res[n][3] for n in order])
```

```python
import functools

import jax
import jax.numpy as jnp
from jax import lax
from jax.experimental import pallas as pl
from jax.experimental.pallas import tpu as pltpu

F32 = jnp.float32
BF16 = jnp.bfloat16
MXU_DTYPE = BF16

D_MODEL = 1024
HEAD_DIM = 64
N_Q_HEADS = 8
Q_PER_KV = 4
ATTN_WIDTH = 512
KV_WIDTH = 128
BLK = 128
ROPE_DIM = 16
ROPE_THETA = 500000.0
N_SG_GROUPS = 8
SG_WIDTH = 512
EVEN_IN = 2816
ODD_IN = 2048
RNN_HEADS = 8
RG_LRU_C = 8.0
ALPHA = (2 * 2) ** 0.25
LN_EPS = 1e-5
NEG_INF = -1e30
ADAM_LR, ADAM_B1, ADAM_B2, ADAM_EPS, ADAM_WD, ADAM_STEP = 0.001, 0.9, 0.999, 1e-08, 0.01, 10

LANES = 128
VMEM_LIMIT = 56 * 1024 * 1024
MESH = pl.DeviceIdType.MESH


def _mm(a, b):
    return jnp.dot(a.astype(MXU_DTYPE), b.astype(MXU_DTYPE), preferred_element_type=F32)


def _mm_nt(a, b):
    return lax.dot_general(a.astype(MXU_DTYPE), b.astype(MXU_DTYPE), (((1,), (1,)), ((), ())), preferred_element_type=F32)


def _mm_tn(a, b):
    return lax.dot_general(a.astype(MXU_DTYPE), b.astype(MXU_DTYPE), (((0,), (0,)), ((), ())), preferred_element_type=F32)


def _sigmoid(x):
    return 1.0 / (1.0 + jnp.exp(-x))


def _ln_stats(z):
    mu = jnp.mean(z, axis=-1, keepdims=True)
    d = z - mu
    var = jnp.mean(d * d, axis=-1, keepdims=True)
    rstd = lax.rsqrt(var + LN_EPS)
    return d * rstd, rstd


def _ln_bwd(dout, zhat, rstd, g):
    dzh = dout * g
    m1 = jnp.mean(dzh, axis=-1, keepdims=True)
    m2 = jnp.mean(dzh * zhat, axis=-1, keepdims=True)
    return rstd * (dzh - m1 - zhat * m2)


def _group_sum(x, e2):
    hi = x.astype(MXU_DTYPE)
    lo = (x - hi.astype(F32)).astype(MXU_DTYPE)
    return jnp.dot(hi, e2, preferred_element_type=F32) + jnp.dot(lo, e2, preferred_element_type=F32)


def _lane_iota(shape):
    return lax.broadcasted_iota(jnp.int32, shape, 1)


def _to_kv_lanes(t, h):
    src_lo = (h % 2 == 0)
    dst_lo = (h // Q_PER_KV == 0)
    if src_lo != dst_lo:
        t = pltpu.roll(t, HEAD_DIM, 1)
    lane = _lane_iota(t.shape)
    keep = (lane < HEAD_DIM) if dst_lo else (lane >= HEAD_DIM)
    return jnp.where(keep, t, 0.0)


def _from_kv_lanes(t, h):
    src_lo = (h // Q_PER_KV == 0)
    dst_lo = (h % 2 == 0)
    lane = _lane_iota(t.shape)
    keep = (lane < HEAD_DIM) if src_lo else (lane >= HEAD_DIM)
    t = jnp.where(keep, t, 0.0)
    if src_lo != dst_lo:
        t = pltpu.roll(t, HEAD_DIM, 1)
    return t


def _rope(t, cos_t, sin_p, sin_m):
    half = ROPE_DIM // 2
    return t * cos_t + pltpu.roll(t, half, 1) * sin_p + pltpu.roll(t, LANES - half, 1) * sin_m


def _rope_t(d, cos_t, sin_p, sin_m):
    half = ROPE_DIM // 2
    return d * cos_t + pltpu.roll(d * sin_p, LANES - half, 1) + pltpu.roll(d * sin_m, half, 1)


def _band(ref, n, nb):
    prev = jnp.maximum(n - 1, 0)
    nxt = jnp.minimum(n + 1, nb - 1)
    rows = [ref[pl.ds(pl.multiple_of(j * BLK, BLK), BLK), :] for j in (prev, n, nxt)]
    return jnp.concatenate(rows, axis=0)


def _band_bias(n, seq):
    qi = lax.broadcasted_iota(jnp.int32, (BLK, 3 * BLK), 0)
    kj = lax.broadcasted_iota(jnp.int32, (BLK, 3 * BLK), 1)
    k_abs = n * BLK - BLK + kj
    valid = (jnp.abs(kj - BLK - qi) <= BLK) & (k_abs >= 0) & (k_abs < seq)
    bias = jnp.where(valid, 0.0, NEG_INF)
    return jnp.concatenate([bias] * Q_PER_KV, axis=0)


def _stack_heads(tile_of, kv):
    return jnp.concatenate([_to_kv_lanes(tile_of(h // 2), h) for h in range(Q_PER_KV * kv, Q_PER_KV * (kv + 1))], axis=0)


def _per_head_column(vals):
    row = lax.broadcasted_iota(jnp.int32, (Q_PER_KV * BLK, 1), 0)
    return jnp.where(row < BLK, vals[0], jnp.where(row < 2 * BLK, vals[1], jnp.where(row < 3 * BLK, vals[2], vals[3])))


def _softplus_neg(lam):
    e = jnp.exp(-jnp.abs(lam))
    u = 1.0 + e
    log1p_e = jnp.where(u == 1.0, e, jnp.log(u) * (e / (u - 1.0)))
    sp = jnp.maximum(-lam, 0.0) + log1p_e
    dsp = -1.0 / (1.0 + jnp.exp(lam))
    return sp, dsp


def _full(shape):
    return pl.BlockSpec(shape, lambda *_: (0,) * len(shape))


def _const(shape):
    return pl.BlockSpec(shape, lambda *_: (0,) * len(shape), pipeline_mode=pl.Buffered(1))


def _rows(tm, n):
    return pl.BlockSpec((tm, n), lambda i: (i, 0))


def _params(*sem):
    return pltpu.CompilerParams(dimension_semantics=sem, vmem_limit_bytes=VMEM_LIMIT)


def _sds(shape, dtype=F32):
    return jax.ShapeDtypeStruct(shape, dtype)


def _row_tile(seq, want):
    return want if seq % want == 0 else seq


def _rope_tables(posf, seq):
    half = ROPE_DIM // 2
    inv_freq = jnp.power(jnp.float32(ROPE_THETA), -jnp.arange(half, dtype=F32) / half)
    j = jnp.arange(LANES) % HEAD_DIM
    invf = jnp.where(j < ROPE_DIM, inv_freq[j % half], 0.0).astype(F32).reshape(1, LANES)
    m_p = ((j >= half) & (j < ROPE_DIM)).astype(F32).reshape(1, LANES)
    m_m = -(j < half).astype(F32).reshape(1, LANES)
    tm = _row_tile(seq, 512)

    def body(pos_ref, invf_ref, mp_ref, mm_ref, cos_ref, sp_ref, sm_ref):
        ang = pos_ref[...] * invf_ref[...]
        s = jnp.sin(ang)
        cos_ref[...] = jnp.cos(ang)
        sp_ref[...] = s * mp_ref[...]
        sm_ref[...] = s * mm_ref[...]

    return pl.pallas_call(
        body, name="rope_tables", grid=(seq // tm,),
        in_specs=[_rows(tm, 1), _full((1, LANES)), _full((1, LANES)), _full((1, LANES))],
        out_specs=[_rows(tm, LANES)] * 3, out_shape=[_sds((seq, LANES))] * 3,
        compiler_params=_params("parallel"),
    )(posf, invf, m_p, m_m)


def _even_proj(x, mod, w_in, tabs, seq):
    tm = _row_tile(seq, 512)

    def body(x_ref, mod_ref, w_ref, cos_ref, sp_ref, sm_ref, h_ref, q_ref, k_ref, v_ref, su_ref, sv_ref, g_ref):
        h = x_ref[...] * (1.0 + mod_ref[1:2, :]) + mod_ref[0:1, :]
        hb = h.astype(MXU_DTYPE)
        h_ref[...] = hb
        p = jnp.dot(hb, w_ref[...], preferred_element_type=F32)
        cos_t, sin_p, sin_m = cos_ref[...], sp_ref[...], sm_ref[...]
        for j in range(ATTN_WIDTH // LANES):
            q_ref[:, j * LANES:(j + 1) * LANES] = _rope(p[:, j * LANES:(j + 1) * LANES], cos_t, sin_p, sin_m).astype(q_ref.dtype)
        k_ref[...] = _rope(p[:, 512:640], cos_t, sin_p, sin_m).astype(k_ref.dtype)
        v_ref[...] = p[:, 640:768].astype(v_ref.dtype)
        su_ref[...] = p[:, 768:1280]
        sv_ref[...] = p[:, 1280:1792]
        g_ref[...] = p[:, 1792:2816]

    return pl.pallas_call(
        body, name="even_proj", grid=(seq // tm,),
        in_specs=[_rows(tm, D_MODEL), _full((3, D_MODEL)), _full((D_MODEL, EVEN_IN))] + [_rows(tm, LANES)] * 3,
        out_specs=[_rows(tm, D_MODEL), _rows(tm, 512), _rows(tm, LANES), _rows(tm, LANES), _rows(tm, 512), _rows(tm, 512),
                   _rows(tm, D_MODEL)],
        out_shape=[_sds((seq, D_MODEL), MXU_DTYPE), _sds((seq, 512), MXU_DTYPE), _sds((seq, LANES), MXU_DTYPE),
                   _sds((seq, LANES), MXU_DTYPE), _sds((seq, 512)), _sds((seq, 512)), _sds((seq, D_MODEL))],
        compiler_params=_params("parallel"),
    )(x, mod, w_in, *tabs)


def _sg_forward(sv, lng, lnb, sgw_ref, sgb, e2):
    vn, vhat, rstd, svo = [], [], [], []
    for j in range(SG_WIDTH // LANES):
        t = sv[:, j * LANES:(j + 1) * LANES]
        mu = _group_sum(t, e2) * (1.0 / HEAD_DIM)
        d = t - mu
        var = _group_sum(d * d, e2) * (1.0 / HEAD_DIM)
        r = lax.rsqrt(var + LN_EPS)
        vh = d * r
        vhat.append(vh)
        rstd.append(r)
        vn.append(vh * lng[:, j * LANES:(j + 1) * LANES] + lnb[:, j * LANES:(j + 1) * LANES])
    lane = _lane_iota((BLK, LANES))
    for j in range(SG_WIDTH // LANES):
        lo = _mm(sgw_ref[2 * j], vn[j])
        hi = _mm(sgw_ref[2 * j + 1], vn[j])
        svo.append(jnp.where(lane < HEAD_DIM, lo, hi) + sgb[:, j * LANES:(j + 1) * LANES])
    return svo, vn, vhat, rstd


def _even_mix(q, k, v, su, sv, sink, sgln_g, sgln_b, sgw, sgb_full, e2, seq):
    nb = seq // BLK

    def body(sink_ref, q_ref, k_ref, v_ref, su_ref, sv_ref, lng_ref, lnb_ref, sgw_ref, sgb_ref, e2_ref, ycat_ref, lse_ref):
        n = pl.program_id(0)
        kband = _band(k_ref, n, nb)
        vband = _band(v_ref, n, nb)
        bias = _band_bias(n, seq)
        lane = _lane_iota((BLK, LANES))
        lse = jnp.zeros((BLK, LANES), F32)
        q_tile = lambda j: q_ref[:, j * LANES:(j + 1) * LANES].astype(F32)
        acc = [jnp.zeros((BLK, LANES), F32) for _ in range(ATTN_WIDTH // LANES)]
        for kv in range(N_Q_HEADS // Q_PER_KV):
            heads = range(Q_PER_KV * kv, Q_PER_KV * (kv + 1))
            sink = _per_head_column([sink_ref[h] for h in heads])
            s = _mm_nt(_stack_heads(q_tile, kv), kband) * (HEAD_DIM ** -0.5) + bias
            m = jnp.maximum(jnp.max(s, axis=1, keepdims=True), sink)
            p = jnp.exp(s - m)
            denom = jnp.sum(p, axis=1, keepdims=True) + jnp.exp(sink - m)
            o4 = _mm(p / denom, vband)
            l4 = m + jnp.log(denom)
            for g, h in enumerate(heads):
                acc[h // 2] = acc[h // 2] + _from_kv_lanes(o4[g * BLK:(g + 1) * BLK], h)
                lse = jnp.where(lane == h, l4[g * BLK:(g + 1) * BLK], lse)
        for j in range(ATTN_WIDTH // LANES):
            ycat_ref[:, j * LANES:(j + 1) * LANES] = acc[j]
        lse_ref[...] = lse
        svo, _, _, _ = _sg_forward(sv_ref[...], lng_ref[...], lnb_ref[...], sgw_ref, sgb_ref[...], e2_ref[...])
        for j in range(SG_WIDTH // LANES):
            ycat_ref[:, ATTN_WIDTH + j * LANES:ATTN_WIDTH + (j + 1) * LANES] = su_ref[:, j * LANES:(j + 1) * LANES] * svo[j]

    blk = lambda w: pl.BlockSpec((BLK, w), lambda n: (n, 0))
    return pl.pallas_call(
        body, name="even_mix", grid=(nb,),
        in_specs=[pl.BlockSpec(memory_space=pltpu.SMEM), blk(512), _full((seq, LANES)), _full((seq, LANES)), blk(512), blk(512),
                  _full((1, 512)), _full((1, 512)), _full((8, BLK, BLK)), _full((BLK, 512)), _full((LANES, LANES))],
        out_specs=[blk(D_MODEL), blk(LANES)], out_shape=[_sds((seq, D_MODEL)), _sds((seq, LANES))],
        compiler_params=_params("parallel"),
    )(sink, q, k, v, su, sv, sgln_g, sgln_b, sgw, sgb_full, e2)


def _even_out(ycat, g, x, mod, w_out, ln_g, ln_b, seq):
    tm = _row_tile(seq, 512)

    def body(y_ref, g_ref, x_ref, mod_ref, wo_ref, g1_ref, b1_ref, out_ref, z_ref, x1_ref):
        gg = g_ref[...]
        out = _mm(y_ref[...] * (gg * _sigmoid(gg)), wo_ref[...])
        out_ref[...] = out
        z = ALPHA * x_ref[...] + mod_ref[2:3, :] * out
        z_ref[...] = z
        zhat, _ = _ln_stats(z)
        x1_ref[...] = zhat * g1_ref[...] + b1_ref[...]

    return pl.pallas_call(
        body, name="even_out", grid=(seq // tm,),
        in_specs=[_rows(tm, D_MODEL)] * 3 + [_full((3, D_MODEL)), _const((D_MODEL, D_MODEL)), _full((1, D_MODEL)), _full((1, D_MODEL))],
        out_specs=[_rows(tm, D_MODEL)] * 3, out_shape=[_sds((seq, D_MODEL))] * 3, compiler_params=_params("parallel"),
    )(ycat, g, x, mod, w_out, ln_g, ln_b)


def _assemble_cols(w4):
    _, rows, cols = w4.shape
    tr = 256

    def body(x_ref, o_ref):
        for s in range(4):
            o_ref[:, s * cols:(s + 1) * cols] = x_ref[s]

    return pl.pallas_call(
        body, name="assemble_cols", grid=(rows // tr,), in_specs=[pl.BlockSpec((4, tr, cols), lambda i: (0, i, 0))],
        out_specs=_rows(tr, 4 * cols), out_shape=_sds((rows, 4 * cols), w4.dtype), compiler_params=_params("parallel"),
    )(w4)


def _odd_proj(x1, mod, w_in4, seq):
    tm = _row_tile(seq, 512)
    cs = ODD_IN // 4

    def body(x_ref, mod_ref, w_ref, h_ref, xr_ref, g_ref):
        h = x_ref[...] * (1.0 + mod_ref[1:2, :]) + mod_ref[0:1, :]
        hb = h.astype(MXU_DTYPE)
        h_ref[...] = hb
        for s in range(2):
            xr_ref[:, s * cs:(s + 1) * cs] = jnp.dot(hb, w_ref[s], preferred_element_type=F32)
            g_ref[:, s * cs:(s + 1) * cs] = jnp.dot(hb, w_ref[2 + s], preferred_element_type=F32)

    return pl.pallas_call(
        body, name="odd_proj", grid=(seq // tm,),
        in_specs=[_rows(tm, D_MODEL), _full((3, D_MODEL)), _full((4, D_MODEL, cs))],
        out_specs=[_rows(tm, D_MODEL)] * 3,
        out_shape=[_sds((seq, D_MODEL), MXU_DTYPE), _sds((seq, D_MODEL)), _sds((seq, D_MODEL))],
        compiler_params=_params("parallel"),
    )(x1, mod, w_in4)


def _halo_specs(tm, seq, width):
    per = tm // 8
    last = seq // 8 - 1
    return [pl.BlockSpec((8, width), lambda i: (jnp.maximum(i * per - 1, 0), 0)),
            pl.BlockSpec((tm, width), lambda i: (i, 0)),
            pl.BlockSpec((8, width), lambda i: (jnp.minimum((i + 1) * per, last), 0))]


def _extended(prev_ref, main_ref, next_ref, i, n_steps):
    prev = jnp.where(i > 0, prev_ref[...], 0.0)
    nxt = jnp.where(i < n_steps - 1, next_ref[...], 0.0)
    return jnp.concatenate([prev, main_ref[...], nxt], axis=0)


def _shifted(ext, off, tm):
    if off == 0:
        return ext[8:8 + tm]
    return pltpu.roll(ext, (-off) % ext.shape[0], 0)[8:8 + tm]


def _lru_gates(xh, pre, bias, sp, hs):
    res = []
    for d in range(2):
        r = _sigmoid(pre[:, (2 * d) * LANES:(2 * d + 1) * LANES] + bias[2 * d:2 * d + 1, hs])
        ig = _sigmoid(pre[:, (2 * d + 1) * LANES:(2 * d + 2) * LANES] + bias[2 * d + 1:2 * d + 2, hs])
        neg_log_a = RG_LRU_C * r * sp[d:d + 1, hs]
        a = jnp.exp(-neg_log_a)
        s = jnp.sqrt(jnp.tanh(neg_log_a) * (a * a + 1.0))
        res.append((r, ig, a, s))
    return res


def _odd_gates(xr, conv_w, conv_b, wcat, bias, lam, seq):
    tm = _row_tile(seq, 512)
    steps = seq // tm

    def body(xp_ref, xm_ref, xn_ref, cw_ref, cb_ref, w_ref, bias_ref, lam_ref, xc_ref, af_ref, bf_ref, ar_ref, br_ref):
        i = pl.program_id(0)
        ext = _extended(xp_ref, xm_ref, xn_ref, i, steps)
        xc = cb_ref[...] + sum(cw_ref[kk:kk + 1, :] * _shifted(ext, kk - 2, tm) for kk in range(4))
        xc_ref[...] = xc
        sp, _ = _softplus_neg(lam_ref[...])
        bias = bias_ref[...]
        for h in range(RNN_HEADS):
            hs = slice(h * LANES, (h + 1) * LANES)
            xh = xc[:, hs]
            (_, i0, a0, s0), (_, i1, a1, s1) = _lru_gates(xh, _mm(xh, w_ref[h]), bias, sp, hs)
            af_ref[:, hs] = a0
            bf_ref[:, hs] = s0 * i0 * xh
            ar_ref[:, hs] = a1
            br_ref[:, hs] = s1 * i1 * xh

    return pl.pallas_call(
        body, name="odd_gates", grid=(steps,),
        in_specs=_halo_specs(tm, seq, D_MODEL) + [_full((4, D_MODEL)), _full((1, D_MODEL)), _full((8, LANES, 512)),
                                                  _full((4, D_MODEL)), _full((2, D_MODEL))],
        out_specs=[_rows(tm, D_MODEL)] * 5, out_shape=[_sds((seq, D_MODEL))] * 5,
        compiler_params=_params("parallel"),
    )(xr, xr, xr, conv_w, conv_b, wcat, bias, lam)


def _scan(a, b, seq, descending, post, name):
    tb = _row_tile(seq, 512)
    steps = seq // tb
    imap = (lambda i: (steps - 1 - i, 0)) if descending else (lambda i: (i, 0))
    spec = pl.BlockSpec((tb, D_MODEL), imap)
    n_out = 1 if post else 2

    sub = 8
    tiles = tb // sub

    def body(a_ref, b_ref, *rest):
        outs, carry_h, carry_a = rest[:n_out], rest[n_out], rest[n_out + 1]

        @pl.when(pl.program_id(0) == 0)
        def _():
            carry_h[...] = jnp.zeros_like(carry_h)
            carry_a[...] = jnp.zeros_like(carry_a)

        row = lax.broadcasted_iota(jnp.int32, (sub, D_MODEL), 0)

        def shift(v, d, fill):
            if descending:
                return jnp.where(row <= sub - 1 - d, pltpu.roll(v, sub - d, 0), fill)
            return jnp.where(row >= d, pltpu.roll(v, d, 0), fill)

        def last(v):
            return jnp.broadcast_to(v[0:1, :] if descending else v[sub - 1:sub, :], v.shape)

        def tile(j, c):
            ch, ca = c
            r0 = pl.multiple_of(((tiles - 1 - j) if descending else j) * sub, sub)
            at = a_ref[pl.ds(r0, sub), :]
            bt = b_ref[pl.ds(r0, sub), :]
            coef = shift(at, 1, ca) if post else at
            acc_a, acc_b = coef, bt
            for d in (1, 2, 4):
                acc_b = acc_b + acc_a * shift(acc_b, d, 0.0)
                acc_a = acc_a * shift(acc_a, d, 1.0)
            h = acc_b + acc_a * ch
            outs[0][pl.ds(r0, sub), :] = h
            if post:
                return last(h), last(at)
            outs[1][pl.ds(r0, sub), :] = shift(h, 1, ch)
            return last(h), ca

        ch, ca = lax.fori_loop(0, tiles, tile, (carry_h[...], carry_a[...]), unroll=4)
        carry_h[...] = ch
        carry_a[...] = ca

    return pl.pallas_call(
        body, name=name, grid=(steps,), in_specs=[spec, spec], out_specs=[spec] * n_out,
        out_shape=[_sds((seq, D_MODEL))] * n_out, scratch_shapes=[pltpu.VMEM((sub, D_MODEL), F32)] * 2,
        compiler_params=_params("arbitrary"),
    )(a, b)


def _odd_out_and_loss(hf, hr, g, x1, tgt, mod, w_out, w_out_t, ln_g, ln_b, seq):
    tm = _row_tile(seq, 512)

    def body(hf_ref, hr_ref, g_ref, x_ref, t_ref, mod_ref, w_ref, wt_ref, lg_ref, lb_ref,
             dhs_ref, dg_ref, dres_ref, loss_ref, dw_ref, vec_ref):
        @pl.when(pl.program_id(0) == 0)
        def _():
            loss_ref[...] = jnp.zeros_like(loss_ref)
            dw_ref[...] = jnp.zeros_like(dw_ref)
            vec_ref[...] = jnp.zeros_like(vec_ref)

        gg = g_ref[...]
        sg = _sigmoid(gg)
        silu = gg * sg
        hsum = hf_ref[...] + hr_ref[...]
        y = hsum * silu
        out = _mm(y, w_ref[...])
        gate = mod_ref[2:3, :]
        z = ALPHA * x_ref[...] + gate * out
        zhat, rstd = _ln_stats(z)
        x2 = zhat * lg_ref[...] + lb_ref[...]
        err = x2 - t_ref[...]
        loss_ref[...] += 0.5 * jnp.sum(jnp.mean(err * err, axis=-1, keepdims=True))
        dx2 = err * (1.0 / D_MODEL)
        dz = _ln_bwd(dx2, zhat, rstd, lg_ref[...])
        vec_ref[0:1, :] += jnp.sum(dx2 * zhat, axis=0, keepdims=True)
        vec_ref[1:2, :] += jnp.sum(dx2, axis=0, keepdims=True)
        vec_ref[2:3, :] += jnp.sum(dz * out, axis=0, keepdims=True)
        dres_ref[...] = ALPHA * dz
        dout = gate * dz
        dw_ref[...] += _mm_tn(y, dout)
        dy = _mm(dout, wt_ref[...])
        dhs_ref[...] = dy * silu
        dg_ref[...] = dy * hsum * (sg * (1.0 + gg * (1.0 - sg)))

    return pl.pallas_call(
        body, name="odd_out_loss", grid=(seq // tm,),
        in_specs=[_rows(tm, D_MODEL)] * 5 + [_full((3, D_MODEL)), _const((D_MODEL, D_MODEL)), _const((D_MODEL, D_MODEL)),
                                             _full((1, D_MODEL)), _full((1, D_MODEL))],
        out_specs=[_rows(tm, D_MODEL)] * 3 + [_full((8, LANES)), _full((D_MODEL, D_MODEL)), _full((8, D_MODEL))],
        out_shape=[_sds((seq, D_MODEL))] * 3 + [_sds((8, LANES)), _sds((D_MODEL, D_MODEL)), _sds((8, D_MODEL))],
        compiler_params=_params("arbitrary"),
    )(hf, hr, g, x1, tgt, mod, w_out, w_out_t, ln_g, ln_b)


def _odd_gates_bwd(xc, gf, gr, hpf, hpr, wcat, bias, lam, seq):
    tm = _row_tile(seq, 512)
    steps = seq // tm

    def body(xc_ref, gf_ref, gr_ref, hpf_ref, hpr_ref, w_ref, bias_ref, lam_ref, dxc_ref, dw_ref, vec_ref):
        @pl.when(pl.program_id(0) == 0)
        def _():
            dw_ref[...] = jnp.zeros_like(dw_ref)
            vec_ref[...] = jnp.zeros_like(vec_ref)

        sp, dsp = _softplus_neg(lam_ref[...])
        bias = bias_ref[...]
        for h in range(RNN_HEADS):
            hs = slice(h * LANES, (h + 1) * LANES)
            xh = xc_ref[:, hs]
            gates = _lru_gates(xh, _mm(xh, w_ref[h]), bias, sp, hs)
            dxh = jnp.zeros_like(xh)
            dpre = []
            for d, (g_ref_d, hp_ref_d) in enumerate(((gf_ref, hpf_ref), (gr_ref, hpr_ref))):
                r, ig, a, s = gates[d]
                db = g_ref_d[:, hs]
                da = db * hp_ref_d[:, hs]
                dxh = dxh + db * s * ig
                dlog_a = da * a - (db * ig * xh) * (a * a / s)
                dr = dlog_a * (-RG_LRU_C) * sp[d:d + 1, hs]
                di = db * s * xh
                dpr = dr * r * (1.0 - r)
                dpi = di * ig * (1.0 - ig)
                vec_ref[2 * d:2 * d + 1, hs] += jnp.sum(dpr, axis=0, keepdims=True)
                vec_ref[2 * d + 1:2 * d + 2, hs] += jnp.sum(dpi, axis=0, keepdims=True)
                vec_ref[4 + d:5 + d, hs] += jnp.sum(dlog_a * r, axis=0, keepdims=True) * (-RG_LRU_C) * dsp[d:d + 1, hs]
                dpre += [dpr, dpi]
            dcat = jnp.concatenate(dpre, axis=1)
            dw_ref[h] += _mm_tn(xh, dcat)
            dxc_ref[:, hs] = dxh + _mm_nt(dcat, w_ref[h])

    return pl.pallas_call(
        body, name="odd_gates_bwd", grid=(steps,),
        in_specs=[_rows(tm, D_MODEL)] * 5 + [_full((8, LANES, 512)), _full((4, D_MODEL)), _full((2, D_MODEL))],
        out_specs=[_rows(tm, D_MODEL), _full((8, LANES, 512)), _full((8, D_MODEL))],
        out_shape=[_sds((seq, D_MODEL)), _sds((8, LANES, 512)), _sds((8, D_MODEL))],
        compiler_params=_params("arbitrary"),
    )(xc, gf, gr, hpf, hpr, wcat, bias, lam)


def _odd_proj_bwd(dxc, xr, dg, x1, dres, mod, conv_w, w_in_t, seq):
    tm = _row_tile(seq, 512)
    steps = seq // tm

    def body(dp_ref, dm_ref, dn_ref, xp_ref, xm_ref, xn_ref, dg_ref, x_ref, dres_ref, mod_ref, cw_ref, wt_ref,
             dx_ref, dpb_ref, vec_ref):
        i = pl.program_id(0)

        @pl.when(i == 0)
        def _():
            vec_ref[...] = jnp.zeros_like(vec_ref)

        dext = _extended(dp_ref, dm_ref, dn_ref, i, steps)
        xext = _extended(xp_ref, xm_ref, xn_ref, i, steps)
        dxc_m = dm_ref[...]
        dxr = sum(cw_ref[kk:kk + 1, :] * _shifted(dext, 2 - kk, tm) for kk in range(4))
        for kk in range(4):
            vec_ref[kk:kk + 1, :] += jnp.sum(dxc_m * _shifted(xext, kk - 2, tm), axis=0, keepdims=True)
        vec_ref[4:5, :] += jnp.sum(dxc_m, axis=0, keepdims=True)
        dpb_ref[:, :D_MODEL] = dxr.astype(dpb_ref.dtype)
        dpb_ref[:, D_MODEL:] = dg_ref[...].astype(dpb_ref.dtype)
        dh = jnp.dot(dpb_ref[...], wt_ref[...], preferred_element_type=F32)
        x = x_ref[...]
        vec_ref[5:6, :] += jnp.sum(dh, axis=0, keepdims=True)
        vec_ref[6:7, :] += jnp.sum(dh * x, axis=0, keepdims=True)
        dx_ref[...] = dres_ref[...] + dh * (1.0 + mod_ref[1:2, :])

    return pl.pallas_call(
        body, name="odd_proj_bwd", grid=(steps,),
        in_specs=_halo_specs(tm, seq, D_MODEL) + _halo_specs(tm, seq, D_MODEL) + [_rows(tm, D_MODEL)] * 3
        + [_full((3, D_MODEL)), _full((4, D_MODEL)), _full((ODD_IN, D_MODEL))],
        out_specs=[_rows(tm, D_MODEL), _rows(tm, ODD_IN), _full((8, D_MODEL))],
        out_shape=[_sds((seq, D_MODEL)), _sds((seq, ODD_IN), MXU_DTYPE), _sds((8, D_MODEL))],
        compiler_params=_params("arbitrary"),
    )(dxc, dxc, dxc, xr, xr, xr, dg, x1, dres, mod, conv_w, w_in_t)


def _tn_matmul(a, b, seq, name):
    n = b.shape[1]
    tn = n // 2
    cs = n // 4
    tm = _row_tile(seq, 512)
    steps = seq // tm

    def body(a_ref, b_ref, o_ref, acc_ref):
        i = pl.program_id(1)

        @pl.when(i == 0)
        def _():
            acc_ref[...] = jnp.zeros_like(acc_ref)

        acc_ref[...] += lax.dot_general(a_ref[...], b_ref[...], (((0,), (0,)), ((), ())), preferred_element_type=F32)

        @pl.when(i == steps - 1)
        def _():
            o_ref[0] = acc_ref[:, 0:cs]
            o_ref[1] = acc_ref[:, cs:2 * cs]

    return pl.pallas_call(
        body, name=name, grid=(2, steps),
        in_specs=[pl.BlockSpec((tm, D_MODEL), lambda j, i: (i, 0)), pl.BlockSpec((tm, tn), lambda j, i: (i, j))],
        out_specs=pl.BlockSpec((2, D_MODEL, cs), lambda j, i: (j, 0, 0)), out_shape=_sds((4, D_MODEL, cs)),
        scratch_shapes=[pltpu.VMEM((D_MODEL, tn), F32)], compiler_params=_params("parallel", "arbitrary"),
    )(a, b)


def _even_out_bwd(dx1, z, out, ycat, g, mod, ln_g, w_out_t, seq):
    tm = _row_tile(seq, 512)

    def body(dx_ref, z_ref, out_ref, y_ref, g_ref, mod_ref, lg_ref, wt_ref, dy_ref, dg_ref, dres_ref, dw_ref, vec_ref):
        @pl.when(pl.program_id(0) == 0)
        def _():
            dw_ref[...] = jnp.zeros_like(dw_ref)
            vec_ref[...] = jnp.zeros_like(vec_ref)

        zhat, rstd = _ln_stats(z_ref[...])
        dx1_ = dx_ref[...]
        dz = _ln_bwd(dx1_, zhat, rstd, lg_ref[...])
        vec_ref[0:1, :] += jnp.sum(dx1_ * zhat, axis=0, keepdims=True)
        vec_ref[1:2, :] += jnp.sum(dx1_, axis=0, keepdims=True)
        vec_ref[2:3, :] += jnp.sum(dz * out_ref[...], axis=0, keepdims=True)
        dres_ref[...] = ALPHA * dz
        dout = mod_ref[2:3, :] * dz
        gg = g_ref[...]
        sg = _sigmoid(gg)
        silu = gg * sg
        ycat_ = y_ref[...]
        dw_ref[...] += _mm_tn(ycat_ * silu, dout)
        dy = _mm(dout, wt_ref[...])
        dy_ref[...] = dy * silu
        dg_ref[...] = dy * ycat_ * (sg * (1.0 + gg * (1.0 - sg)))

    return pl.pallas_call(
        body, name="even_out_bwd", grid=(seq // tm,),
        in_specs=[_rows(tm, D_MODEL)] * 5 + [_full((3, D_MODEL)), _full((1, D_MODEL)), _const((D_MODEL, D_MODEL))],
        out_specs=[_rows(tm, D_MODEL)] * 3 + [_full((D_MODEL, D_MODEL)), _full((8, D_MODEL))],
        out_shape=[_sds((seq, D_MODEL))] * 3 + [_sds((D_MODEL, D_MODEL)), _sds((8, D_MODEL))],
        compiler_params=_params("arbitrary"),
    )(dx1, z, out, ycat, g, mod, ln_g, w_out_t)


def _even_mix_bwd(q, k, v, lse, ycat, dycat, su, sv, sink, sgln_g, sgln_b, sgw, sgb_full, e2, e8, seq):
    nb = seq // BLK

    def body(sink_ref, q_ref, k_ref, v_ref, lse_ref, y_ref, dy_ref, su_ref, sv_ref, lng_ref, lnb_ref, sgw_ref, sgb_ref, e2_ref,
             e8_ref, dq_ref, dsu_ref, dsv_ref, dk_ref, dv_ref, dsgw_ref, dsgb_ref, vec_ref, dsink_ref, dsgb_acc):
        n = pl.program_id(0)

        @pl.when(n == 0)
        def _():
            dk_ref[...] = jnp.zeros_like(dk_ref)
            dv_ref[...] = jnp.zeros_like(dv_ref)
            dsgw_ref[...] = jnp.zeros_like(dsgw_ref)
            dsgb_acc[...] = jnp.zeros_like(dsgb_acc)
            vec_ref[...] = jnp.zeros_like(vec_ref)
            dsink_ref[...] = jnp.zeros_like(dsink_ref)

        kband = _band(k_ref, n, nb)
        vband = _band(v_ref, n, nb)
        bias = _band_bias(n, seq)
        lane = _lane_iota((BLK, LANES))
        row8 = lax.broadcasted_iota(jnp.int32, (8, LANES), 0)
        lse = lse_ref[...]
        dkb = jnp.zeros((LANES, 3 * BLK), F32)
        dvb = jnp.zeros((LANES, 3 * BLK), F32)
        dsink = jnp.zeros((8, LANES), F32)
        q_tile = lambda j: q_ref[:, j * LANES:(j + 1) * LANES].astype(F32)
        do_tile = lambda j: dy_ref[:, j * LANES:(j + 1) * LANES]
        dq = [jnp.zeros((BLK, LANES), F32) for _ in range(ATTN_WIDTH // LANES)]
        for kv in range(N_Q_HEADS // Q_PER_KV):
            heads = range(Q_PER_KV * kv, Q_PER_KV * (kv + 1))
            lse4, delta4 = [], []
            for h in heads:
                head_lanes = (lane < HEAD_DIM) if h % 2 == 0 else (lane >= HEAD_DIM)
                lse4.append(jnp.sum(jnp.where(lane == h, lse, 0.0), axis=1, keepdims=True))
                delta4.append(jnp.sum(jnp.where(head_lanes, do_tile(h // 2) * y_ref[:, (h // 2) * LANES:(h // 2 + 1) * LANES], 0.0),
                                      axis=1, keepdims=True))
            lse4, delta4 = jnp.concatenate(lse4, axis=0), jnp.concatenate(delta4, axis=0)
            q4, do4 = _stack_heads(q_tile, kv), _stack_heads(do_tile, kv)
            s = _mm_nt(q4, kband) * (HEAD_DIM ** -0.5) + bias
            p = jnp.exp(s - lse4)
            wsink = jnp.exp(_per_head_column([sink_ref[h] for h in heads]) - lse4) * delta4
            ds = p * (_mm_nt(do4, vband) - delta4) * (HEAD_DIM ** -0.5)
            dq4 = _mm(ds, kband)
            dkb = dkb + _mm_tn(q4, ds)
            dvb = dvb + _mm_tn(do4, p)
            for g, h in enumerate(heads):
                dq[h // 2] = dq[h // 2] + _from_kv_lanes(dq4[g * BLK:(g + 1) * BLK], h)
                dsink = dsink + jnp.where(row8 == h, -jnp.sum(wsink[g * BLK:(g + 1) * BLK]), 0.0)
        for j in range(ATTN_WIDTH // LANES):
            dq_ref[:, j * LANES:(j + 1) * LANES] = dq[j]
        dsink_ref[...] += dsink
        prev = jnp.maximum(n - 1, 0)
        nxt = jnp.minimum(n + 1, nb - 1)
        for part, blk_i in enumerate((prev, n, nxt)):
            rows = pl.ds(pl.multiple_of(blk_i * BLK, BLK), BLK)
            dk_ref[rows, :] += dkb[:, part * BLK:(part + 1) * BLK].T
            dv_ref[rows, :] += dvb[:, part * BLK:(part + 1) * BLK].T

        e2 = e2_ref[...]
        lng = lng_ref[...]
        svo, vn, vhat, rstd = _sg_forward(sv_ref[...], lng, lnb_ref[...], sgw_ref, sgb_ref[...], e2)
        for j in range(SG_WIDTH // LANES):
            cs = slice(j * LANES, (j + 1) * LANES)
            dysg = dy_ref[:, ATTN_WIDTH + j * LANES:ATTN_WIDTH + (j + 1) * LANES]
            dsu_ref[:, cs] = dysg * svo[j]
            dsvo = dysg * su_ref[:, cs]
            dsgb_acc[:, cs] += dsvo
            d_lo = jnp.where(lane < HEAD_DIM, dsvo, 0.0)
            d_hi = dsvo - d_lo
            dsgw_ref[2 * j] += _mm_nt(d_lo, vn[j])
            dsgw_ref[2 * j + 1] += _mm_nt(d_hi, vn[j])
            dvn = _mm_tn(sgw_ref[2 * j], d_lo) + _mm_tn(sgw_ref[2 * j + 1], d_hi)
            vec_ref[0:1, cs] += jnp.sum(dvn * vhat[j], axis=0, keepdims=True)
            vec_ref[1:2, cs] += jnp.sum(dvn, axis=0, keepdims=True)
            dvh = dvn * lng[:, cs]
            m1 = _group_sum(dvh, e2) * (1.0 / HEAD_DIM)
            m2 = _group_sum(dvh * vhat[j], e2) * (1.0 / HEAD_DIM)
            dsv_ref[:, cs] = rstd[j] * (dvh - m1 - vhat[j] * m2)

        @pl.when(n == nb - 1)
        def _():
            rest = dsgb_acc[...]
            total = jnp.zeros((8, BLK), F32)
            for _ in range(3):
                part = rest.astype(MXU_DTYPE)
                total = total + lax.dot_general(e8_ref[...], part, (((1,), (1,)), ((), ())), preferred_element_type=F32)
                rest = rest - part.astype(F32)
            dsgb_ref[...] = total

    blk = lambda w: pl.BlockSpec((BLK, w), lambda n: (n, 0))
    return pl.pallas_call(
        body, name="even_mix_bwd", grid=(nb,),
        in_specs=[pl.BlockSpec(memory_space=pltpu.SMEM), blk(512), _full((seq, LANES)), _full((seq, LANES)), blk(LANES),
                  blk(D_MODEL), blk(D_MODEL), blk(512), blk(512), _full((1, 512)), _full((1, 512)), _full((8, BLK, BLK)),
                  _full((BLK, 512)), _full((LANES, LANES)), _full((8, 512))],
        out_specs=[blk(512), blk(512), blk(512), _full((seq, LANES)), _full((seq, LANES)), _full((8, BLK, BLK)),
                   _full((8, BLK)), _full((8, 512)), _full((8, LANES))],
        out_shape=[_sds((seq, 512)), _sds((seq, 512)), _sds((seq, 512)), _sds((seq, LANES)), _sds((seq, LANES)),
                   _sds((8, BLK, BLK)), _sds((8, BLK)), _sds((8, 512)), _sds((8, LANES))],
        scratch_shapes=[pltpu.VMEM((BLK, 512), F32)],
        compiler_params=_params("arbitrary"),
    )(sink, q, k, v, lse, ycat, dycat, su, sv, sgln_g, sgln_b, sgw, sgb_full, e2, e8)


def _even_proj_bwd(dq, dk, dv, dsu, dsv, dg, x, dres, mod, tabs, w_in_t, seq):
    tm = _row_tile(seq, 512)

    def body(dq_ref, dk_ref, dv_ref, dsu_ref, dsv_ref, dg_ref, x_ref, dres_ref, mod_ref, cos_ref, sp_ref, sm_ref, wt_ref,
             dx_ref, dpb_ref, vec_ref):
        @pl.when(pl.program_id(0) == 0)
        def _():
            vec_ref[...] = jnp.zeros_like(vec_ref)

        cos_t, sin_p, sin_m = cos_ref[...], sp_ref[...], sm_ref[...]
        dt = dpb_ref.dtype
        for j in range(ATTN_WIDTH // LANES):
            cs = slice(j * LANES, (j + 1) * LANES)
            dpb_ref[:, cs] = _rope_t(dq_ref[:, cs], cos_t, sin_p, sin_m).astype(dt)
        dpb_ref[:, 512:640] = _rope_t(dk_ref[...], cos_t, sin_p, sin_m).astype(dt)
        dpb_ref[:, 640:768] = dv_ref[...].astype(dt)
        dpb_ref[:, 768:1280] = dsu_ref[...].astype(dt)
        dpb_ref[:, 1280:1792] = dsv_ref[...].astype(dt)
        dpb_ref[:, 1792:2816] = dg_ref[...].astype(dt)
        dh = jnp.dot(dpb_ref[...], wt_ref[...], preferred_element_type=F32)
        x_ = x_ref[...]
        vec_ref[0:1, :] += jnp.sum(dh, axis=0, keepdims=True)
        vec_ref[1:2, :] += jnp.sum(dh * x_, axis=0, keepdims=True)
        dx_ref[...] = dres_ref[...] + dh * (1.0 + mod_ref[1:2, :])

    return pl.pallas_call(
        body, name="even_proj_bwd", grid=(seq // tm,),
        in_specs=[_rows(tm, 512), _rows(tm, LANES), _rows(tm, LANES), _rows(tm, 512), _rows(tm, 512), _rows(tm, D_MODEL),
                  _rows(tm, D_MODEL), _rows(tm, D_MODEL), _full((3, D_MODEL))] + [_rows(tm, LANES)] * 3
        + [_full((EVEN_IN, D_MODEL))],
        out_specs=[_rows(tm, D_MODEL), _rows(tm, EVEN_IN), _full((8, D_MODEL))],
        out_shape=[_sds((seq, D_MODEL)), _sds((seq, EVEN_IN), MXU_DTYPE), _sds((8, D_MODEL))],
        compiler_params=_params("arbitrary"),
    )(dq, dk, dv, dsu, dsv, dg, x, dres, mod, *tabs, w_in_t)


def _local_step(x, posf, tgt, mod, w, seq):
    mxu = lambda a: a.astype(MXU_DTYPE)
    row = lambda a: a.reshape(1, -1)
    tabs = _rope_tables(posf, seq)
    e2 = mxu(jnp.kron(jnp.eye(2, dtype=F32), jnp.ones((HEAD_DIM, HEAD_DIM), F32)))
    e8 = mxu(jnp.repeat(jnp.eye(N_SG_GROUPS, dtype=F32), HEAD_DIM, axis=1))
    sgw = mxu(w["ev_sg_w"])
    sgb_full = jnp.repeat(w["ev_sg_b"].T, HEAD_DIM, axis=1)
    sgln_g, sgln_b = row(w["ev_sg_ln_g"]), row(w["ev_sg_ln_b"])
    sink = w["ev_sink"].reshape(N_Q_HEADS)
    ev_w_in, ev_w_out = _assemble_cols(mxu(w["ev_w_in"])), mxu(w["ev_w_out"])
    od_w_in, od_w_out = mxu(w["od_w_in"]), mxu(w["od_w_out"])
    od_w_in_t = jnp.swapaxes(od_w_in, 1, 2).reshape(ODD_IN, D_MODEL)
    wcat = mxu(jnp.concatenate([w["od_w_a"][0], w["od_w_x"][0], w["od_w_a"][1], w["od_w_x"][1]], axis=2))
    gate_bias = jnp.stack([w["od_b_a"][0], w["od_b_x"][0], w["od_b_a"][1], w["od_b_x"][1]])
    conv_b = row(w["od_conv_b"])
    ln_g, ln_b = w["ln_g"], w["ln_b"]

    h0, q, k, v, su, sv, g0 = _even_proj(x, mod[0], ev_w_in, tabs, seq)
    ycat, lse = _even_mix(q, k, v, su, sv, sink, sgln_g, sgln_b, sgw, sgb_full, e2, seq)
    out0, z0, x1 = _even_out(ycat, g0, x, mod[0], ev_w_out, ln_g[0:1], ln_b[0:1], seq)
    h1, xr, g1 = _odd_proj(x1, mod[1], od_w_in, seq)
    xc, a_f, b_f, a_r, b_r = _odd_gates(xr, w["od_conv_w"], conv_b, wcat, gate_bias, w["od_lam"], seq)
    hf, hpf = _scan(a_f, b_f, seq, descending=False, post=False, name="scan_fwd")
    hr, hpr = _scan(a_r, b_r, seq, descending=True, post=False, name="scan_rev")
    dhs, dg1, dres1, loss, d_od_w_out, vec_o = _odd_out_and_loss(hf, hr, g1, x1, tgt, mod[1], od_w_out, od_w_out.T,
                                                                   ln_g[1:2], ln_b[1:2], seq)
    (gf,) = _scan(a_f, dhs, seq, descending=True, post=True, name="scan_fwd_bwd")
    (gr,) = _scan(a_r, dhs, seq, descending=False, post=True, name="scan_rev_bwd")
    dxc, d_wcat, vec_g = _odd_gates_bwd(xc, gf, gr, hpf, hpr, wcat, gate_bias, w["od_lam"], seq)
    dx1, dp1, vec_p = _odd_proj_bwd(dxc, xr, dg1, x1, dres1, mod[1], w["od_conv_w"], od_w_in_t, seq)
    d_od_w_in = _tn_matmul(h1, dp1, seq, "odd_dw_in")
    dycat, dg0, dres0, d_ev_w_out, vec_e = _even_out_bwd(dx1, z0, out0, ycat, g0, mod[0], ln_g[0:1], ev_w_out.T, seq)
    dq, dsu, dsv, dk, dv, d_sgw, d_sgb, vec_s, d_sink = _even_mix_bwd(q, k, v, lse, ycat, dycat, su, sv, sink, sgln_g, sgln_b,
                                                                      sgw, sgb_full, e2, e8, seq)
    grad_x, dp0, vec_x = _even_proj_bwd(dq, dk, dv, dsu, dsv, dg0, x, dres0, mod[0], tabs, ev_w_in.T, seq)
    d_ev_w_in = _tn_matmul(h0, dp0, seq, "even_dw_in")

    dmod = jnp.stack([jnp.stack([vec_x[0], vec_x[1], vec_e[2]]), jnp.stack([vec_p[5], vec_p[6], vec_o[2]])])
    grads = {
        "ln_g": jnp.stack([vec_e[0], vec_o[0]]), "ln_b": jnp.stack([vec_e[1], vec_o[1]]),
        "ev_w_in": d_ev_w_in, "ev_w_out": d_ev_w_out, "ev_sink": d_sink[:, 0],
        "ev_sg_ln_g": vec_s[0], "ev_sg_ln_b": vec_s[1], "ev_sg_w": d_sgw,
        "ev_sg_b": d_sgb,
        "od_w_in": d_od_w_in, "od_conv_w": vec_p[0:4], "od_conv_b": vec_p[4],
        "od_w_a": jnp.stack([d_wcat[:, :, 0:128], d_wcat[:, :, 256:384]]),
        "od_w_x": jnp.stack([d_wcat[:, :, 128:256], d_wcat[:, :, 384:512]]),
        "od_b_a": jnp.stack([vec_g[0], vec_g[2]]), "od_b_x": jnp.stack([vec_g[1], vec_g[3]]),
        "od_lam": vec_g[4:6], "od_w_out": d_od_w_out,
    }
    return loss[0, 0], grad_x, dmod, grads


def _place():
    return lax.axis_index("x"), lax.axis_index("y"), lax.axis_index("c")


def _allgather8(block, name):
    m_per, n = block.shape

    def body(x_ref, out_ref, send_sems, recv_sems, local_sem):
        x, y, c = _place()
        me, sibling = (x, y, c), (x, y, 1 - c)
        chips = [(1 - x, y), (x, 1 - y), (1 - x, 1 - y)]

        def rows(px, py, pc):
            return out_ref.at[pl.ds((4 * px + 2 * py + pc) * m_per, m_per), :]

        def copy(k, blk, to, src=None):
            return pltpu.make_async_remote_copy(src_ref=rows(*blk) if src is None else src, dst_ref=rows(*blk),
                                                send_sem=send_sems.at[k], recv_sem=recv_sems.at[k], device_id=to,
                                                device_id_type=MESH)

        mine = pltpu.make_async_copy(x_ref, rows(*me), local_sem)
        mine.start()
        first = [copy(0, me, sibling, src=x_ref)] + [copy(1 + j, me, (*chip, c), src=x_ref) for j, chip in enumerate(chips)]
        for cp in first:
            cp.start()
        passed = [copy(4 + j, (*chip, c), sibling) for j, chip in enumerate(chips)]
        for j, chip in enumerate(chips):
            copy(1 + j, (*chip, c), me).wait_recv()
            passed[j].start()
        copy(0, sibling, me).wait_recv()
        for j, chip in enumerate(chips):
            copy(4 + j, (*chip, 1 - c), me).wait_recv()
        for cp in first + passed:
            cp.wait_send()
        mine.wait()

    return pl.pallas_call(
        body, name=name, out_shape=_sds((8 * m_per, n), block.dtype),
        in_specs=[pl.BlockSpec(memory_space=pltpu.VMEM)], out_specs=pl.BlockSpec(memory_space=pltpu.VMEM),
        scratch_shapes=[pltpu.SemaphoreType.DMA((7,)), pltpu.SemaphoreType.DMA((7,)), pltpu.SemaphoreType.DMA],
        compiler_params=pltpu.CompilerParams(vmem_limit_bytes=VMEM_LIMIT),
    )(block)


class _Copies:
    def __init__(self, send_sems, recv_sems, local_sems, stages):
        self.send_sems, self.recv_sems, self.local_sems, self.stages = send_sems, recv_sems, local_sems, stages
        self.sent, self.staged, self.locals = [], [], []

    def remote(self, k, src, dst, to):
        return pltpu.make_async_remote_copy(src_ref=src, dst_ref=dst, send_sem=self.send_sems.at[k], recv_sem=self.recv_sems.at[k],
                                            device_id=to, device_id_type=MESH)

    def send(self, k, src, dst, to):
        cp = self.remote(k, src, dst, to)
        cp.start()
        self.sent.append(cp)

    def arrived(self, k, dst, frm):
        self.remote(k, dst, dst, frm).wait_recv()

    def local(self, src, dst):
        k = len(self.staged)
        cp = pltpu.make_async_copy(src, self.stages[k], self.local_sems.at[2 * k])
        cp.start()
        self.staged.append((cp, dst))

    def flush(self):
        for k in range(len(self.locals), len(self.staged)):
            cp, dst = self.staged[k]
            cp.wait()
            out = pltpu.make_async_copy(self.stages[k], dst, self.local_sems.at[2 * k + 1])
            out.start()
            self.locals.append(out)

    def drain(self):
        self.flush()
        for cp in self.sent:
            cp.wait_send()
        for cp in self.locals:
            cp.wait()


def _comm_call(body, name, ins, out_shapes, n_remote, stages):
    n_in, n_out = len(ins), len(out_shapes)

    def kern(*refs):
        in_refs, out_refs = refs[:n_in], refs[n_in:n_in + n_out]
        send_sems, recv_sems, local_sems = refs[n_in + n_out:n_in + n_out + 3]
        body(_Copies(send_sems, recv_sems, local_sems, refs[n_in + n_out + 3:]), in_refs, out_refs)

    hbm = pl.BlockSpec(memory_space=pl.ANY)
    return pl.pallas_call(
        kern, name=name, out_shape=out_shapes, in_specs=[hbm] * n_in, out_specs=[hbm] * n_out,
        scratch_shapes=[pltpu.SemaphoreType.DMA((n_remote,)), pltpu.SemaphoreType.DMA((n_remote,)),
                        pltpu.SemaphoreType.DMA((2 * len(stages),))] + [pltpu.VMEM(s, d) for s, d in stages],
        compiler_params=pltpu.CompilerParams(vmem_limit_bytes=VMEM_LIMIT),
    )(*ins)


def _gather_to_all(cps, src, dst, me, sibling, other_chips, c, base):
    idx = lambda p: 4 * p[0] + 2 * p[1] + p[2]
    cps.local(src, dst.at[idx(me)])
    cps.send(base, src, dst.at[idx(me)], sibling)
    for j, chip in enumerate(other_chips):
        cps.send(base + 1 + j, src, dst.at[idx(me)], (*chip, c))
    cps.flush()
    for j, chip in enumerate(other_chips):
        got = dst.at[idx((*chip, c))]
        cps.arrived(base + 1 + j, got, (*chip, c))
        cps.send(base + 4 + j, got, got, sibling)
    cps.arrived(base, dst.at[idx(sibling)], sibling)
    for j, chip in enumerate(other_chips):
        cps.arrived(base + 4 + j, dst.at[idx((*chip, 1 - c))], sibling)


def _gather_weights(shards, small):
    n = len(shards)

    def body(cps, ins, outs):
        x, y, c = _place()
        me, sibling, mine = (x, y, c), (x, y, 1 - c), 2 * x + y
        chips = [(1 - x, y), (x, 1 - y), (1 - x, 1 - y)]
        for i in range(n):
            cps.local(ins[i], outs[i].at[mine])
        for j, (px, py) in enumerate(chips):
            for i in range(n):
                hr = shards[i].shape[0] // 2
                rows = pl.ds(c * hr, hr)
                cps.send(6 * i + j, ins[i].at[rows], outs[i].at[mine, rows], (px, py, c))
        _gather_to_all(cps, ins[n], outs[n], me, sibling, chips, c, 6 * n)
        for j, (px, py) in enumerate(chips):
            for i in range(n):
                hr = shards[i].shape[0] // 2
                got = outs[i].at[2 * px + py, pl.ds(c * hr, hr)]
                cps.arrived(6 * i + j, got, (px, py, c))
                cps.send(6 * i + 3 + j, got, got, sibling)
        for j, (px, py) in enumerate(chips):
            for i in range(n):
                hr = shards[i].shape[0] // 2
                cps.arrived(6 * i + 3 + j, outs[i].at[2 * px + py, pl.ds((1 - c) * hr, hr)], sibling)
        cps.drain()

    return _comm_call(body, "gather_weights", list(shards) + [small],
                      [_sds((4,) + s.shape, s.dtype) for s in shards] + [_sds((8,) + small.shape, small.dtype)], 6 * n + 7,
                      [(a.shape, a.dtype) for a in list(shards) + [small]])


def _reduce_sibling(parts, dmod_rows):
    n = len(parts)

    def body(cps, ins, outs):
        x, y, c = _place()
        me, sibling = (x, y, c), (x, y, 1 - c)
        chips = [(1 - x, y), (x, 1 - y), (1 - x, 1 - y)]
        for i in range(n):
            cps.send(i, ins[i].at[:, 1 - c], outs[i], sibling)
        _gather_to_all(cps, ins[n], outs[n], me, sibling, chips, c, n)
        for i in range(n):
            cps.arrived(i, outs[i], sibling)
        cps.drain()

    return _comm_call(body, "reduce_sibling", list(parts) + [dmod_rows],
                      [_sds((4,) + p.shape[2:], p.dtype) for p in parts] + [_sds((8,) + dmod_rows.shape, dmod_rows.dtype)], n + 7,
                      [(dmod_rows.shape, dmod_rows.dtype)])


def _reduce_chips(parts):
    n = len(parts)

    def body(cps, ins, outs):
        x, y, c = _place()
        mine = 2 * x + y
        chips = [(1 - x, y), (x, 1 - y), (1 - x, 1 - y)]
        for i in range(n):
            cps.local(ins[i].at[mine], outs[i].at[mine])
        for j, (px, py) in enumerate(chips):
            for i in range(n):
                cps.send(3 * i + j, ins[i].at[2 * px + py], outs[i].at[mine], (px, py, c))
        cps.flush()
        for j, (px, py) in enumerate(chips):
            for i in range(n):
                cps.arrived(3 * i + j, outs[i].at[2 * px + py], (px, py, c))
        cps.drain()

    return _comm_call(body, "reduce_chips", list(parts), [_sds(p.shape, p.dtype) for p in parts], 3 * n,
                      [(p.shape[1:], p.dtype) for p in parts])


def _gather_reduced(shard_parts, repl_parts):
    ns, nr = len(shard_parts), len(repl_parts)

    def body(cps, ins, outs):
        x, y, c = _place()
        me, sibling = (x, y, c), (x, y, 1 - c)
        chips = [(1 - x, y), (x, 1 - y), (1 - x, 1 - y)]
        for i in range(ns):
            cps.local(ins[i], outs[i].at[c])
            cps.send(i, ins[i], outs[i].at[c], sibling)
        for i in range(nr):
            _gather_to_all(cps, ins[ns + i], outs[ns + i], me, sibling, chips, c, ns + 7 * i)
        for i in range(ns):
            cps.arrived(i, outs[i].at[1 - c], sibling)
        cps.drain()

    return _comm_call(body, "gather_reduced", list(shard_parts) + list(repl_parts),
                      [_sds((2,) + p.shape, p.dtype) for p in shard_parts] + [_sds((8,) + p.shape, p.dtype) for p in repl_parts],
                      ns + 7 * nr, [(p.shape, p.dtype) for p in list(shard_parts) + list(repl_parts)])


def _sum_sibling(core, parts, got, wire):
    n = len(parts)

    def body(core_ref, *refs):
        for i in range(n):
            refs[2 * n + i][0] = (refs[i][0] + refs[n + i][0]).astype(wire[i])

    keep_spec = lambda p: pl.BlockSpec((1, None) + p.shape[2:], lambda s, core_ref: (s, core_ref[0], 0, 0))
    slot_spec = lambda p: pl.BlockSpec((1,) + p.shape[2:], lambda s, core_ref: (s, 0, 0))
    return pl.pallas_call(
        body, name="sum_sibling",
        grid_spec=pltpu.PrefetchScalarGridSpec(
            num_scalar_prefetch=1, grid=(4,), in_specs=[keep_spec(p) for p in parts] + [slot_spec(p) for p in parts],
            out_specs=[slot_spec(p) for p in parts]),
        out_shape=[_sds((4,) + p.shape[2:], wire[i]) for i, p in enumerate(parts)],
        compiler_params=_params("parallel"),
    )(core, *parts, *got)


def _sum_slots(slots, name):
    n = len(slots)

    def spec_pair(p):
        k, rows, cols = p.shape
        sub = 16 if p.dtype == BF16 else 8
        if (rows // 2) % sub == 0:
            return pl.BlockSpec((k, rows // 2, cols), lambda i: (0, i, 0)), pl.BlockSpec((rows // 2, cols), lambda i: (i, 0))
        return pl.BlockSpec((k, rows, cols), lambda i: (0, 0, 0)), pl.BlockSpec((rows, cols), lambda i: (0, 0))

    pairs = [spec_pair(p) for p in slots]

    def body(*refs):
        for i in range(n):
            acc = refs[i][0].astype(F32)
            for j in range(1, slots[i].shape[0]):
                acc = acc + refs[i][j].astype(F32)
            refs[n + i][...] = acc

    return pl.pallas_call(
        body, name=name, grid=(2,), in_specs=[a for a, _ in pairs], out_specs=[b for _, b in pairs],
        out_shape=[_sds(p.shape[1:]) for p in slots], compiler_params=_params("arbitrary"),
    )(*slots)


def _modulation(c_all, ada_w, ada_b):
    cols = ada_w.shape[2]

    def body(c_ref, w_ref, b_ref, o_ref):
        cc = c_ref[...]
        o_ref[0] = _mm(cc * _sigmoid(cc), w_ref[0]) + b_ref[0]

    return pl.pallas_call(
        body, name="modulation", grid=(2,),
        in_specs=[_full((8, D_MODEL)), pl.BlockSpec((1, D_MODEL, cols), lambda l: (l, 0, 0)), pl.BlockSpec((1, 1, cols), lambda l: (l, 0, 0))],
        out_specs=pl.BlockSpec((1, 8, cols), lambda l: (l, 0, 0)), out_shape=_sds((2, 8, cols)),
        compiler_params=_params("parallel"),
    )(c_all, ada_w, ada_b)


def _adamw_math(w, g, m, v):
    m = ADAM_B1 * m + (1.0 - ADAM_B1) * g
    v = ADAM_B2 * v + (1.0 - ADAM_B2) * (g * g)
    m_hat = m / (1.0 - ADAM_B1 ** ADAM_STEP)
    v_hat = v / (1.0 - ADAM_B2 ** ADAM_STEP)
    delta = -ADAM_LR * (m_hat / (jnp.sqrt(v_hat) + ADAM_EPS) + ADAM_WD * w)
    return delta, m, v


def _ada_update(c_all, dmod, w, m, v):
    cols = w.shape[2]
    tr = 256
    spec3 = pl.BlockSpec((1, tr, cols), lambda l, i: (l, i, 0))

    def body(c_ref, d_ref, w_ref, m_ref, v_ref, g_ref, dl_ref, nm_ref, nv_ref):
        cc = c_ref[...]
        g = _mm_tn(cc * _sigmoid(cc), d_ref[0])
        g_ref[0] = g
        dl_ref[0], nm_ref[0], nv_ref[0] = _adamw_math(w_ref[0], g, m_ref[0], v_ref[0])

    return pl.pallas_call(
        body, name="ada_update", grid=(2, D_MODEL // tr),
        in_specs=[pl.BlockSpec((8, tr), lambda l, i: (0, i)), pl.BlockSpec((1, 8, cols), lambda l, i: (l, 0, 0)), spec3, spec3, spec3],
        out_specs=[spec3] * 4, out_shape=[_sds(w.shape)] * 4, compiler_params=_params("parallel", "parallel"),
    )(c_all, dmod, w, m, v)


def _adamw(w, g, m, v, name):
    rows, n = w.shape
    tr = next(t for t in (256, 128, 64, 32, 16, 8, rows) if rows % t == 0)

    def body(w_ref, g_ref, m_ref, v_ref, dl_ref, nm_ref, nv_ref):
        dl_ref[...], nm_ref[...], nv_ref[...] = _adamw_math(w_ref[...], g_ref[...], m_ref[...], v_ref[...])

    return pl.pallas_call(body, name=name, grid=(rows // tr,), in_specs=[_rows(tr, n)] * 4, out_specs=[_rows(tr, n)] * 3,
                          out_shape=[_sds((rows, n))] * 3, compiler_params=_params("parallel"))(w, g, m, v)


def _adamw_small(params):
    n = len(params)

    def body(*refs):
        ins, outs = refs[:4 * n], refs[4 * n:]
        for j in range(n):
            w_ref, g_ref, m_ref, v_ref = ins[4 * j:4 * j + 4]
            outs[3 * j][...], outs[3 * j + 1][...], outs[3 * j + 2][...] = _adamw_math(w_ref[...], g_ref[...], m_ref[...], v_ref[...])

    flat = [a for p in params for a in p]
    res = pl.pallas_call(body, name="adamw_small", out_shape=[_sds(p[0].shape) for p in params for _ in range(3)])(*flat)
    return [tuple(res[3 * j:3 * j + 3]) for j in range(n)]


def _cols(a, start, size):
    return lax.dynamic_slice_in_dim(a, start, size, axis=a.ndim - 1)


def kernel(x, c, positions, ada_w, ada_b, ln_g, ln_b, ev_w_in, ev_w_out, ev_sink, ev_sg_ln_g, ev_sg_ln_b, ev_sg_w, ev_sg_b, od_w_in, od_conv_w, od_conv_b, od_w_a, od_b_a, od_w_x, od_b_x, od_lam, od_w_out, loss_target, m_ada_w, m_ada_b, m_ln_g, m_ln_b, m_ev_w_in, m_ev_w_out, m_ev_sink, m_ev_sg_ln_g, m_ev_sg_ln_b, m_ev_sg_w, m_ev_sg_b, m_od_w_in, m_od_conv_w, m_od_conv_b, m_od_w_a, m_od_b_a, m_od_w_x, m_od_b_x, m_od_lam, m_od_w_out, v_ada_w, v_ada_b, v_ln_g, v_ln_b, v_ev_w_in, v_ev_w_out, v_ev_sink, v_ev_sg_ln_g, v_ev_sg_ln_b, v_ev_sg_w, v_ev_sg_b, v_od_w_in, v_od_conv_w, v_od_conv_b, v_od_w_a, v_od_b_a, v_od_w_x, v_od_b_x, v_od_lam, v_od_w_out):
    seq = x.shape[1]
    px, py, pc = _place()
    chip = 2 * px + py
    dev = 2 * chip + pc

    small = jnp.concatenate([od_conv_w[0].reshape(-1), od_conv_b[0], od_b_a[0].reshape(-1), jnp.zeros((256,), F32),
                             od_b_x[0].reshape(-1), od_lam[0].reshape(-1)]).reshape(3, D_MODEL)
    blk = jnp.concatenate([c, small, jnp.zeros((4, D_MODEL), F32)], axis=0)
    wire_w = lambda a: a[0].astype(MXU_DTYPE)
    ev_w_in4, ev_w_out4, od_w_in4, od_w_out4, g_small = _gather_weights(
        [wire_w(ev_w_in), wire_w(ev_w_out), wire_w(od_w_in), wire_w(od_w_out)], blk)
    c_all = g_small[:, 0, :]
    per_chip = g_small[0::2]
    conv_w = per_chip[:, 1].reshape(4, 4, 256).transpose(1, 0, 2).reshape(4, D_MODEL)
    conv_b = per_chip[:, 2, 0:256].reshape(D_MODEL)
    b_a = per_chip[:, 2, 256:768].reshape(4, 2, 256).transpose(1, 0, 2).reshape(2, D_MODEL)
    b_x = per_chip[:, 3, 0:512].reshape(4, 2, 256).transpose(1, 0, 2).reshape(2, D_MODEL)
    lam = per_chip[:, 3, 512:1024].reshape(4, 2, 256).transpose(1, 0, 2).reshape(2, D_MODEL)

    w_full = {
        "ev_w_in": ev_w_in4, "ev_w_out": ev_w_out4.reshape(D_MODEL, D_MODEL),
        "od_w_in": od_w_in4, "od_w_out": od_w_out4.reshape(D_MODEL, D_MODEL),
        "ev_sink": ev_sink[0], "ev_sg_ln_g": ev_sg_ln_g[0], "ev_sg_ln_b": ev_sg_ln_b[0], "ev_sg_w": ev_sg_w[0],
        "ev_sg_b": ev_sg_b[0], "od_conv_w": conv_w, "od_conv_b": conv_b, "od_w_a": od_w_a[0], "od_b_a": b_a,
        "od_w_x": od_w_x[0], "od_b_x": b_x, "od_lam": lam, "ln_g": ln_g, "ln_b": ln_b,
    }

    ada_cols = ada_w.shape[2]
    mod_sh = _modulation(c_all, ada_w, _cols(ada_b, chip * ada_cols, ada_cols).reshape(2, 1, ada_cols))
    mod_all = _allgather8(mod_sh.reshape(16, ada_cols), "gather_mod").reshape(4, 2, 2, 8, ada_cols)[:, 0]
    mod_mine = lax.dynamic_index_in_dim(mod_all, dev, axis=2, keepdims=False)
    mod = mod_mine.transpose(1, 0, 2).reshape(2, 3, D_MODEL)

    posf = positions.astype(F32).reshape(seq, 1)
    loss_local, grad_x, dmod, g = _local_step(x[0], posf, loss_target[0], mod, w_full, seq)

    pad = lambda a, n: jnp.concatenate([a.reshape(-1), jnp.zeros((n - a.size,), F32)])
    rows_small = jnp.concatenate([
        dmod.reshape(6, D_MODEL), g["ln_g"][0:1], g["ln_b"][0:1], g["ln_g"][1:2], g["ln_b"][1:2],
        jnp.concatenate([g["ev_sg_ln_g"], g["ev_sg_ln_b"]]).reshape(1, D_MODEL), g["ev_sg_b"].reshape(1, D_MODEL),
        g["od_conv_w"], g["od_conv_b"].reshape(1, D_MODEL), g["od_b_a"], g["od_b_x"], g["od_lam"],
        pad(g["ev_sink"], D_MODEL).reshape(1, D_MODEL), pad(loss_local, D_MODEL).reshape(1, D_MODEL),
        jnp.zeros((39, D_MODEL), F32)], axis=0)
    parts = [g["ev_w_in"].reshape(4, 2, 512, 704), g["ev_w_out"].reshape(4, 2, 128, D_MODEL),
             g["od_w_in"].reshape(4, 2, 512, 512), g["od_w_out"].reshape(4, 2, 128, D_MODEL),
             g["ev_sg_w"].reshape(4, 2, BLK, BLK), g["od_w_a"].reshape(4, 2, 2 * BLK, BLK),
             g["od_w_x"].reshape(4, 2, 2 * BLK, BLK), rows_small.reshape(4, 2, 8, D_MODEL)]
    wire = [MXU_DTYPE] * 7 + [F32]
    dmod_blk = jnp.concatenate([dmod.reshape(6, D_MODEL), jnp.zeros((2, D_MODEL), F32)], axis=0)
    *got, dmod_gathered = _reduce_sibling(parts, dmod_blk)
    chip_sums = _sum_sibling(pc.astype(jnp.int32).reshape(1), parts, got, wire)
    mine = _sum_slots(_reduce_chips(chip_sums), "sum_chips")
    reduced = _gather_reduced(mine[:4], mine[4:])
    g_ev_w_in = reduced[0].reshape(D_MODEL, 704)
    g_ev_w_out = reduced[1].reshape(256, D_MODEL)
    g_od_w_in = reduced[2].reshape(D_MODEL, 512)
    g_od_w_out = reduced[3].reshape(256, D_MODEL)
    g_sg_w = reduced[4].reshape(8 * BLK, BLK)
    g_w_a = reduced[5].reshape(16 * BLK, BLK)
    g_w_x = reduced[6].reshape(16 * BLK, BLK)
    gs = reduced[7].reshape(64, D_MODEL)
    loss = gs[24, 0]
    dmod_all = dmod_gathered[:, 0:6].reshape(8, 2, 3 * D_MODEL)
    dmod_sh = _cols(dmod_all, chip * ada_cols, ada_cols).transpose(1, 0, 2)
    g_ada_w, d_ada_w, nm_ada_w, nv_ada_w = _ada_update(c_all, dmod_sh, ada_w, m_ada_w, v_ada_w)

    big = {}
    for name, w_, g_, m_, v_ in (
            ("ev_w_in", ev_w_in, g_ev_w_in, m_ev_w_in, v_ev_w_in), ("ev_w_out", ev_w_out, g_ev_w_out, m_ev_w_out, v_ev_w_out),
            ("od_w_in", od_w_in, g_od_w_in, m_od_w_in, v_od_w_in), ("od_w_out", od_w_out, g_od_w_out, m_od_w_out, v_od_w_out),
            ("ev_sg_w", ev_sg_w, g_sg_w, m_ev_sg_w, v_ev_sg_w), ("od_w_a", od_w_a, g_w_a, m_od_w_a, v_od_w_a),
            ("od_w_x", od_w_x, g_w_x, m_od_w_x, v_od_w_x)):
        two_d = lambda a: a.reshape(g_.shape)
        d_, nm_, nv_ = _adamw(two_d(w_), g_, two_d(m_), two_d(v_), "adamw_" + name)
        big[name] = tuple(a.reshape(w_.shape) for a in (g_, d_, nm_, nv_))
    big["ada_w"] = (g_ada_w, d_ada_w, nm_ada_w, nv_ada_w)

    sh = lambda a: _cols(a, chip * 256, 256)
    small_g = {
        "ada_b": gs[0:6].reshape(2, 3 * D_MODEL), "ln_g": jnp.stack([gs[6], gs[8]]), "ln_b": jnp.stack([gs[7], gs[9]]),
        "ev_sink": gs[23:24, 0:8], "ev_sg_ln_g": gs[10:11, 0:512], "ev_sg_ln_b": gs[10:11, 512:1024],
        "ev_sg_b": gs[11].reshape(8, BLK), "od_conv_w": sh(gs[12:16]), "od_conv_b": sh(gs[16:17]), "od_b_a": sh(gs[17:19]),
        "od_b_x": sh(gs[19:21]), "od_lam": sh(gs[21:23]),
    }
    small_in = {"ada_b": (ada_b, m_ada_b, v_ada_b), "ln_g": (ln_g, m_ln_g, v_ln_g), "ln_b": (ln_b, m_ln_b, v_ln_b),
                "ev_sink": (ev_sink, m_ev_sink, v_ev_sink), "ev_sg_ln_g": (ev_sg_ln_g, m_ev_sg_ln_g, v_ev_sg_ln_g),
                "ev_sg_ln_b": (ev_sg_ln_b, m_ev_sg_ln_b, v_ev_sg_ln_b), "ev_sg_b": (ev_sg_b, m_ev_sg_b, v_ev_sg_b),
                "od_conv_w": (od_conv_w, m_od_conv_w, v_od_conv_w), "od_conv_b": (od_conv_b, m_od_conv_b, v_od_conv_b),
                "od_b_a": (od_b_a, m_od_b_a, v_od_b_a), "od_b_x": (od_b_x, m_od_b_x, v_od_b_x),
                "od_lam": (od_lam, m_od_lam, v_od_lam)}
    names_small = list(small_g)
    upd = _adamw_small([(small_in[n][0].reshape(small_g[n].shape), small_g[n], small_in[n][1].reshape(small_g[n].shape),
                         small_in[n][2].reshape(small_g[n].shape)) for n in names_small])
    res = dict(big)
    for n, (d_, nm_, nv_) in zip(names_small, upd):
        shape = small_in[n][0].shape
        res[n] = tuple(a.reshape(shape) for a in (small_g[n], d_, nm_, nv_))

    order = ["ada_w", "ada_b", "ln_g", "ln_b", "ev_w_in", "ev_w_out", "ev_sink", "ev_sg_ln_g", "ev_sg_ln_b", "ev_sg_w", "ev_sg_b",
             "od_w_in", "od_conv_w", "od_conv_b", "od_w_a", "od_b_a", "od_w_x", "od_b_x", "od_lam", "od_w_out"]
    return (loss, grad_x.reshape(x.shape), *[res[n][0] for n in order], *[res[n][1] for n in order],
            *[res[n][2] for n in order], *[res[n][3] for n in order])
```

```python
import functools

import jax
import jax.numpy as jnp
from jax import lax
from jax.experimental import pallas as pl
from jax.experimental.pallas import tpu as pltpu

F32 = jnp.float32
BF16 = jnp.bfloat16
MXU_DTYPE = BF16

D_MODEL = 1024
HEAD_DIM = 64
N_Q_HEADS = 8
Q_PER_KV = 4
ATTN_WIDTH = 512
KV_WIDTH = 128
BLK = 128
ROPE_DIM = 16
ROPE_THETA = 500000.0
N_SG_GROUPS = 8
SG_WIDTH = 512
EVEN_IN = 2816
ODD_IN = 2048
RNN_HEADS = 8
RG_LRU_C = 8.0
ALPHA = (2 * 2) ** 0.25
LN_EPS = 1e-5
NEG_INF = -1e30
ADAM_LR, ADAM_B1, ADAM_B2, ADAM_EPS, ADAM_WD, ADAM_STEP = 0.001, 0.9, 0.999, 1e-08, 0.01, 10

LANES = 128
VMEM_LIMIT = 56 * 1024 * 1024
MESH = pl.DeviceIdType.MESH


def _mm(a, b):
    return jnp.dot(a.astype(MXU_DTYPE), b.astype(MXU_DTYPE), preferred_element_type=F32)


def _mm_nt(a, b):
    return lax.dot_general(a.astype(MXU_DTYPE), b.astype(MXU_DTYPE), (((1,), (1,)), ((), ())), preferred_element_type=F32)


def _mm_tn(a, b):
    return lax.dot_general(a.astype(MXU_DTYPE), b.astype(MXU_DTYPE), (((0,), (0,)), ((), ())), preferred_element_type=F32)


def _sigmoid(x):
    return 1.0 / (1.0 + jnp.exp(-x))


def _ln_stats(z):
    mu = jnp.mean(z, axis=-1, keepdims=True)
    d = z - mu
    var = jnp.mean(d * d, axis=-1, keepdims=True)
    rstd = lax.rsqrt(var + LN_EPS)
    return d * rstd, rstd


def _ln_bwd(dout, zhat, rstd, g):
    dzh = dout * g
    m1 = jnp.mean(dzh, axis=-1, keepdims=True)
    m2 = jnp.mean(dzh * zhat, axis=-1, keepdims=True)
    return rstd * (dzh - m1 - zhat * m2)


def _group_sum(x, e2):
    hi = x.astype(MXU_DTYPE)
    lo = (x - hi.astype(F32)).astype(MXU_DTYPE)
    return jnp.dot(hi, e2, preferred_element_type=F32) + jnp.dot(lo, e2, preferred_element_type=F32)


def _lane_iota(shape):
    return lax.broadcasted_iota(jnp.int32, shape, 1)


def _to_kv_lanes(t, h):
    src_lo = (h % 2 == 0)
    dst_lo = (h // Q_PER_KV == 0)
    if src_lo != dst_lo:
        t = pltpu.roll(t, HEAD_DIM, 1)
    lane = _lane_iota(t.shape)
    keep = (lane < HEAD_DIM) if dst_lo else (lane >= HEAD_DIM)
    return jnp.where(keep, t, 0.0)


def _from_kv_lanes(t, h):
    src_lo = (h // Q_PER_KV == 0)
    dst_lo = (h % 2 == 0)
    lane = _lane_iota(t.shape)
    keep = (lane < HEAD_DIM) if src_lo else (lane >= HEAD_DIM)
    t = jnp.where(keep, t, 0.0)
    if src_lo != dst_lo:
        t = pltpu.roll(t, HEAD_DIM, 1)
    return t


def _rope(t, cos_t, sin_p, sin_m):
    half = ROPE_DIM // 2
    return t * cos_t + pltpu.roll(t, half, 1) * sin_p + pltpu.roll(t, LANES - half, 1) * sin_m


def _rope_t(d, cos_t, sin_p, sin_m):
    half = ROPE_DIM // 2
    return d * cos_t + pltpu.roll(d * sin_p, LANES - half, 1) + pltpu.roll(d * sin_m, half, 1)


def _band(ref, n, nb):
    prev = jnp.maximum(n - 1, 0)
    nxt = jnp.minimum(n + 1, nb - 1)
    rows = [ref[pl.ds(pl.multiple_of(j * BLK, BLK), BLK), :] for j in (prev, n, nxt)]
    return jnp.concatenate(rows, axis=0)


def _band_bias(n, seq):
    qi = lax.broadcasted_iota(jnp.int32, (BLK, 3 * BLK), 0)
    kj = lax.broadcasted_iota(jnp.int32, (BLK, 3 * BLK), 1)
    k_abs = n * BLK - BLK + kj
    valid = (jnp.abs(kj - BLK - qi) <= BLK) & (k_abs >= 0) & (k_abs < seq)
    bias = jnp.where(valid, 0.0, NEG_INF)
    return jnp.concatenate([bias] * Q_PER_KV, axis=0)


def _stack_heads(tile_of, kv):
    return jnp.concatenate([_to_kv_lanes(tile_of(h // 2), h) for h in range(Q_PER_KV * kv, Q_PER_KV * (kv + 1))], axis=0)


def _per_head_column(vals):
    row = lax.broadcasted_iota(jnp.int32, (Q_PER_KV * BLK, 1), 0)
    return jnp.where(row < BLK, vals[0], jnp.where(row < 2 * BLK, vals[1], jnp.where(row < 3 * BLK, vals[2], vals[3])))


def _softplus_neg(lam):
    e = jnp.exp(-jnp.abs(lam))
    u = 1.0 + e
    log1p_e = jnp.where(u == 1.0, e, jnp.log(u) * (e / (u - 1.0)))
    sp = jnp.maximum(-lam, 0.0) + log1p_e
    dsp = -1.0 / (1.0 + jnp.exp(lam))
    return sp, dsp


def _full(shape):
    return pl.BlockSpec(shape, lambda *_: (0,) * len(shape))


def _const(shape):
    return pl.BlockSpec(shape, lambda *_: (0,) * len(shape), pipeline_mode=pl.Buffered(1))


def _rows(tm, n):
    return pl.BlockSpec((tm, n), lambda i: (i, 0))


def _params(*sem):
    return pltpu.CompilerParams(dimension_semantics=sem, vmem_limit_bytes=VMEM_LIMIT)


def _sds(shape, dtype=F32):
    return jax.ShapeDtypeStruct(shape, dtype)


def _row_tile(seq, want):
    return want if seq % want == 0 else seq


def _rope_tables(posf, seq):
    half = ROPE_DIM // 2
    inv_freq = jnp.power(jnp.float32(ROPE_THETA), -jnp.arange(half, dtype=F32) / half)
    j = jnp.arange(LANES) % HEAD_DIM
    invf = jnp.where(j < ROPE_DIM, inv_freq[j % half], 0.0).astype(F32).reshape(1, LANES)
    m_p = ((j >= half) & (j < ROPE_DIM)).astype(F32).reshape(1, LANES)
    m_m = -(j < half).astype(F32).reshape(1, LANES)
    tm = _row_tile(seq, 512)

    def body(pos_ref, invf_ref, mp_ref, mm_ref, cos_ref, sp_ref, sm_ref):
        ang = pos_ref[...] * invf_ref[...]
        s = jnp.sin(ang)
        cos_ref[...] = jnp.cos(ang)
        sp_ref[...] = s * mp_ref[...]
        sm_ref[...] = s * mm_ref[...]

    return pl.pallas_call(
        body, name="rope_tables", grid=(seq // tm,),
        in_specs=[_rows(tm, 1), _full((1, LANES)), _full((1, LANES)), _full((1, LANES))],
        out_specs=[_rows(tm, LANES)] * 3, out_shape=[_sds((seq, LANES))] * 3,
        compiler_params=_params("parallel"),
    )(posf, invf, m_p, m_m)


def _even_proj(x, mod, w_in_t, tabs, seq):
    tm = _row_tile(seq, 512)

    def body(x_ref, mod_ref, w_ref, cos_ref, sp_ref, sm_ref, h_ref, q_ref, k_ref, v_ref, su_ref, sv_ref, g_ref):
        h = x_ref[...] * (1.0 + mod_ref[1:2, :]) + mod_ref[0:1, :]
        hb = h.astype(MXU_DTYPE)
        h_ref[...] = hb
        p = _mm_nt(hb, w_ref[...])
        cos_t, sin_p, sin_m = cos_ref[...], sp_ref[...], sm_ref[...]
        for j in range(ATTN_WIDTH // LANES):
            q_ref[:, j * LANES:(j + 1) * LANES] = _rope(p[:, j * LANES:(j + 1) * LANES], cos_t, sin_p, sin_m).astype(q_ref.dtype)
        k_ref[...] = _rope(p[:, 512:640], cos_t, sin_p, sin_m).astype(k_ref.dtype)
        v_ref[...] = p[:, 640:768].astype(v_ref.dtype)
        su_ref[...] = p[:, 768:1280]
        sv_ref[...] = p[:, 1280:1792]
        g_ref[...] = p[:, 1792:2816]

    return pl.pallas_call(
        body, name="even_proj", grid=(seq // tm,),
        in_specs=[_rows(tm, D_MODEL), _full((3, D_MODEL)), _const((EVEN_IN, D_MODEL))] + [_rows(tm, LANES)] * 3,
        out_specs=[_rows(tm, D_MODEL), _rows(tm, 512), _rows(tm, LANES), _rows(tm, LANES), _rows(tm, 512), _rows(tm, 512),
                   _rows(tm, D_MODEL)],
        out_shape=[_sds((seq, D_MODEL), MXU_DTYPE), _sds((seq, 512), MXU_DTYPE), _sds((seq, LANES), MXU_DTYPE),
                   _sds((seq, LANES), MXU_DTYPE), _sds((seq, 512)), _sds((seq, 512)), _sds((seq, D_MODEL))],
        compiler_params=_params("parallel"),
    )(x, mod, w_in_t, *tabs)


def _sg_forward(sv, lng, lnb, sgw_ref, sgb, e2):
    vn, vhat, rstd, svo = [], [], [], []
    for j in range(SG_WIDTH // LANES):
        t = sv[:, j * LANES:(j + 1) * LANES]
        mu = _group_sum(t, e2) * (1.0 / HEAD_DIM)
        d = t - mu
        var = _group_sum(d * d, e2) * (1.0 / HEAD_DIM)
        r = lax.rsqrt(var + LN_EPS)
        vh = d * r
        vhat.append(vh)
        rstd.append(r)
        vn.append(vh * lng[:, j * LANES:(j + 1) * LANES] + lnb[:, j * LANES:(j + 1) * LANES])
    lane = _lane_iota((BLK, LANES))
    for j in range(SG_WIDTH // LANES):
        lo = _mm(sgw_ref[2 * j], vn[j])
        hi = _mm(sgw_ref[2 * j + 1], vn[j])
        svo.append(jnp.where(lane < HEAD_DIM, lo, hi) + sgb[:, j * LANES:(j + 1) * LANES])
    return svo, vn, vhat, rstd


def _even_mix(q, k, v, su, sv, sink, sgln_g, sgln_b, sgw, sgb_full, e2, seq):
    nb = seq // BLK

    def body(sink_ref, q_ref, k_ref, v_ref, su_ref, sv_ref, lng_ref, lnb_ref, sgw_ref, sgb_ref, e2_ref, ycat_ref, lse_ref):
        n = pl.program_id(0)
        kband = _band(k_ref, n, nb)
        vband = _band(v_ref, n, nb)
        bias = _band_bias(n, seq)
        lane = _lane_iota((BLK, LANES))
        lse = jnp.zeros((BLK, LANES), F32)
        q_tile = lambda j: q_ref[:, j * LANES:(j + 1) * LANES].astype(F32)
        acc = [jnp.zeros((BLK, LANES), F32) for _ in range(ATTN_WIDTH // LANES)]
        for kv in range(N_Q_HEADS // Q_PER_KV):
            heads = range(Q_PER_KV * kv, Q_PER_KV * (kv + 1))
            sink = _per_head_column([sink_ref[h] for h in heads])
            s = _mm_nt(_stack_heads(q_tile, kv), kband) * (HEAD_DIM ** -0.5) + bias
            m = jnp.maximum(jnp.max(s, axis=1, keepdims=True), sink)
            p = jnp.exp(s - m)
            denom = jnp.sum(p, axis=1, keepdims=True) + jnp.exp(sink - m)
            o4 = _mm(p / denom, vband)
            l4 = m + jnp.log(denom)
            for g, h in enumerate(heads):
                acc[h // 2] = acc[h // 2] + _from_kv_lanes(o4[g * BLK:(g + 1) * BLK], h)
                lse = jnp.where(lane == h, l4[g * BLK:(g + 1) * BLK], lse)
        for j in range(ATTN_WIDTH // LANES):
            ycat_ref[:, j * LANES:(j + 1) * LANES] = acc[j]
        lse_ref[...] = lse
        svo, _, _, _ = _sg_forward(sv_ref[...], lng_ref[...], lnb_ref[...], sgw_ref, sgb_ref[...], e2_ref[...])
        for j in range(SG_WIDTH // LANES):
            ycat_ref[:, ATTN_WIDTH + j * LANES:ATTN_WIDTH + (j + 1) * LANES] = su_ref[:, j * LANES:(j + 1) * LANES] * svo[j]

    blk = lambda w: pl.BlockSpec((BLK, w), lambda n: (n, 0))
    return pl.pallas_call(
        body, name="even_mix", grid=(nb,),
        in_specs=[pl.BlockSpec(memory_space=pltpu.SMEM), blk(512), _full((seq, LANES)), _full((seq, LANES)), blk(512), blk(512),
                  _full((1, 512)), _full((1, 512)), _full((8, BLK, BLK)), _full((BLK, 512)), _full((LANES, LANES))],
        out_specs=[blk(D_MODEL), blk(LANES)], out_shape=[_sds((seq, D_MODEL)), _sds((seq, LANES))],
        compiler_params=_params("parallel"),
    )(sink, q, k, v, su, sv, sgln_g, sgln_b, sgw, sgb_full, e2)


def _even_out(ycat, g, x, mod, w_out, ln_g, ln_b, seq):
    tm = _row_tile(seq, 512)

    def body(y_ref, g_ref, x_ref, mod_ref, wo_ref, g1_ref, b1_ref, z_ref, x1_ref):
        gg = g_ref[...]
        out = _mm(y_ref[...] * (gg * _sigmoid(gg)), wo_ref[...])
        z = ALPHA * x_ref[...] + mod_ref[2:3, :] * out
        z_ref[...] = z
        zhat, _ = _ln_stats(z)
        x1_ref[...] = zhat * g1_ref[...] + b1_ref[...]

    return pl.pallas_call(
        body, name="even_out", grid=(seq // tm,),
        in_specs=[_rows(tm, D_MODEL)] * 3 + [_full((3, D_MODEL)), _const((D_MODEL, D_MODEL)), _full((1, D_MODEL)), _full((1, D_MODEL))],
        out_specs=[_rows(tm, D_MODEL)] * 2, out_shape=[_sds((seq, D_MODEL))] * 2, compiler_params=_params("parallel"),
    )(ycat, g, x, mod, w_out, ln_g, ln_b)


def _odd_proj(x1, mod, w_in4, seq):
    tm = _row_tile(seq, 512)
    cs = ODD_IN // 4

    def body(x_ref, mod_ref, w_ref, h_ref, xr_ref, g_ref):
        h = x_ref[...] * (1.0 + mod_ref[1:2, :]) + mod_ref[0:1, :]
        hb = h.astype(MXU_DTYPE)
        h_ref[...] = hb
        for s in range(2):
            xr_ref[:, s * cs:(s + 1) * cs] = jnp.dot(hb, w_ref[s], preferred_element_type=F32)
            g_ref[:, s * cs:(s + 1) * cs] = jnp.dot(hb, w_ref[2 + s], preferred_element_type=F32)

    return pl.pallas_call(
        body, name="odd_proj", grid=(seq // tm,),
        in_specs=[_rows(tm, D_MODEL), _full((3, D_MODEL)), _full((4, D_MODEL, cs))],
        out_specs=[_rows(tm, D_MODEL)] * 3,
        out_shape=[_sds((seq, D_MODEL), MXU_DTYPE), _sds((seq, D_MODEL)), _sds((seq, D_MODEL))],
        compiler_params=_params("parallel"),
    )(x1, mod, w_in4)


def _halo_specs(tm, seq, width):
    per = tm // 8
    last = seq // 8 - 1
    return [pl.BlockSpec((8, width), lambda i: (jnp.maximum(i * per - 1, 0), 0)),
            pl.BlockSpec((tm, width), lambda i: (i, 0)),
            pl.BlockSpec((8, width), lambda i: (jnp.minimum((i + 1) * per, last), 0))]


def _extended(prev_ref, main_ref, next_ref, i, n_steps):
    prev = jnp.where(i > 0, prev_ref[...], 0.0)
    nxt = jnp.where(i < n_steps - 1, next_ref[...], 0.0)
    return jnp.concatenate([prev, main_ref[...], nxt], axis=0)


def _shifted(ext, off, tm):
    if off == 0:
        return ext[8:8 + tm]
    return pltpu.roll(ext, (-off) % ext.shape[0], 0)[8:8 + tm]


def _lru_gates(xh, pre, bias, sp, hs):
    res = []
    for d in range(2):
        r = _sigmoid(pre[:, (2 * d) * LANES:(2 * d + 1) * LANES] + bias[2 * d:2 * d + 1, hs])
        ig = _sigmoid(pre[:, (2 * d + 1) * LANES:(2 * d + 2) * LANES] + bias[2 * d + 1:2 * d + 2, hs])
        neg_log_a = RG_LRU_C * r * sp[d:d + 1, hs]
        a = jnp.exp(-neg_log_a)
        s = jnp.sqrt(jnp.tanh(neg_log_a) * (a * a + 1.0))
        res.append((r, ig, a, s))
    return res


def _odd_gates(xr, conv_w, conv_b, wcat, bias, lam, seq):
    tm = _row_tile(seq, 512)
    steps = seq // tm

    def body(xp_ref, xm_ref, xn_ref, cw_ref, cb_ref, w_ref, bias_ref, lam_ref, xc_ref, af_ref, bf_ref, ar_ref, br_ref):
        i = pl.program_id(0)
        ext = _extended(xp_ref, xm_ref, xn_ref, i, steps)
        xc = cb_ref[...] + sum(cw_ref[kk:kk + 1, :] * _shifted(ext, kk - 2, tm) for kk in range(4))
        xc_ref[...] = xc
        sp, _ = _softplus_neg(lam_ref[...])
        bias = bias_ref[...]
        for h in range(RNN_HEADS):
            hs = slice(h * LANES, (h + 1) * LANES)
            xh = xc[:, hs]
            (_, i0, a0, s0), (_, i1, a1, s1) = _lru_gates(xh, _mm(xh, w_ref[h]), bias, sp, hs)
            af_ref[:, hs] = a0
            bf_ref[:, hs] = s0 * i0 * xh
            ar_ref[:, hs] = a1
            br_ref[:, hs] = s1 * i1 * xh

    return pl.pallas_call(
        body, name="odd_gates", grid=(steps,),
        in_specs=_halo_specs(tm, seq, D_MODEL) + [_full((4, D_MODEL)), _full((1, D_MODEL)), _full((8, LANES, 512)),
                                                  _full((4, D_MODEL)), _full((2, D_MODEL))],
        out_specs=[_rows(tm, D_MODEL)] * 5, out_shape=[_sds((seq, D_MODEL))] * 5,
        compiler_params=_params("parallel"),
    )(xr, xr, xr, conv_w, conv_b, wcat, bias, lam)


def _scan(a, b, seq, descending, post, name):
    tb = _row_tile(seq, 512)
    steps = seq // tb
    imap = (lambda i: (steps - 1 - i, 0)) if descending else (lambda i: (i, 0))
    spec = pl.BlockSpec((tb, D_MODEL), imap)
    n_out = 1 if post else 2

    sub = 8
    tiles = tb // sub

    def body(a_ref, b_ref, *rest):
        outs, carry_h, carry_a = rest[:n_out], rest[n_out], rest[n_out + 1]

        @pl.when(pl.program_id(0) == 0)
        def _():
            carry_h[...] = jnp.zeros_like(carry_h)
            carry_a[...] = jnp.zeros_like(carry_a)

        row = lax.broadcasted_iota(jnp.int32, (sub, D_MODEL), 0)

        def shift(v, d, fill):
            if descending:
                return jnp.where(row <= sub - 1 - d, pltpu.roll(v, sub - d, 0), fill)
            return jnp.where(row >= d, pltpu.roll(v, d, 0), fill)

        def last(v):
            return jnp.broadcast_to(v[0:1, :] if descending else v[sub - 1:sub, :], v.shape)

        def tile(j, c):
            ch, ca = c
            r0 = pl.multiple_of(((tiles - 1 - j) if descending else j) * sub, sub)
            at = a_ref[pl.ds(r0, sub), :]
            bt = b_ref[pl.ds(r0, sub), :]
            coef = shift(at, 1, ca) if post else at
            acc_a, acc_b = coef, bt
            for d in (1, 2, 4):
                acc_b = acc_b + acc_a * shift(acc_b, d, 0.0)
                acc_a = acc_a * shift(acc_a, d, 1.0)
            h = acc_b + acc_a * ch
            outs[0][pl.ds(r0, sub), :] = h
            if post:
                return last(h), last(at)
            outs[1][pl.ds(r0, sub), :] = shift(h, 1, ch)
            return last(h), ca

        ch, ca = lax.fori_loop(0, tiles, tile, (carry_h[...], carry_a[...]), unroll=4)
        carry_h[...] = ch
        carry_a[...] = ca

    return pl.pallas_call(
        body, name=name, grid=(steps,), in_specs=[spec, spec], out_specs=[spec] * n_out,
        out_shape=[_sds((seq, D_MODEL))] * n_out, scratch_shapes=[pltpu.VMEM((sub, D_MODEL), F32)] * 2,
        compiler_params=_params("arbitrary"),
    )(a, b)


def _odd_out_and_loss(hf, hr, g, x1, tgt, mod, w_out, ln_g, ln_b, seq):
    tm = _row_tile(seq, 512)

    def body(hf_ref, hr_ref, g_ref, x_ref, t_ref, mod_ref, w_ref, lg_ref, lb_ref,
             dhs_ref, dg_ref, dres_ref, loss_ref, dw_ref, vec_ref):
        @pl.when(pl.program_id(0) == 0)
        def _():
            loss_ref[...] = jnp.zeros_like(loss_ref)
            dw_ref[...] = jnp.zeros_like(dw_ref)
            vec_ref[...] = jnp.zeros_like(vec_ref)

        gg = g_ref[...]
        sg = _sigmoid(gg)
        silu = gg * sg
        hsum = hf_ref[...] + hr_ref[...]
        y = hsum * silu
        out = _mm(y, w_ref[...])
        gate = mod_ref[2:3, :]
        z = ALPHA * x_ref[...] + gate * out
        zhat, rstd = _ln_stats(z)
        x2 = zhat * lg_ref[...] + lb_ref[...]
        err = x2 - t_ref[...]
        loss_ref[...] += 0.5 * jnp.sum(jnp.mean(err * err, axis=-1, keepdims=True))
        dx2 = err * (1.0 / D_MODEL)
        dz = _ln_bwd(dx2, zhat, rstd, lg_ref[...])
        vec_ref[0:1, :] += jnp.sum(dx2 * zhat, axis=0, keepdims=True)
        vec_ref[1:2, :] += jnp.sum(dx2, axis=0, keepdims=True)
        vec_ref[2:3, :] += jnp.sum(dz * out, axis=0, keepdims=True)
        dres_ref[...] = ALPHA * dz
        dout = gate * dz
        dw_ref[...] += _mm_tn(y, dout)
        dy = _mm_nt(dout, w_ref[...])
        dhs_ref[...] = dy * silu
        dg_ref[...] = dy * hsum * (sg * (1.0 + gg * (1.0 - sg)))

    return pl.pallas_call(
        body, name="odd_out_loss", grid=(seq // tm,),
        in_specs=[_rows(tm, D_MODEL)] * 5 + [_full((3, D_MODEL)), _const((D_MODEL, D_MODEL)),
                                             _full((1, D_MODEL)), _full((1, D_MODEL))],
        out_specs=[_rows(tm, D_MODEL)] * 3 + [_full((8, LANES)), _full((D_MODEL, D_MODEL)), _full((8, D_MODEL))],
        out_shape=[_sds((seq, D_MODEL))] * 3 + [_sds((8, LANES)), _sds((D_MODEL, D_MODEL)), _sds((8, D_MODEL))],
        compiler_params=_params("arbitrary"),
    )(hf, hr, g, x1, tgt, mod, w_out, ln_g, ln_b)


def _odd_gates_bwd(xc, gf, gr, hpf, hpr, wcat, bias, lam, seq):
    tm = _row_tile(seq, 512)
    steps = seq // tm

    def body(xc_ref, gf_ref, gr_ref, hpf_ref, hpr_ref, w_ref, bias_ref, lam_ref, dxc_ref, dw_ref, vec_ref):
        @pl.when(pl.program_id(0) == 0)
        def _():
            dw_ref[...] = jnp.zeros_like(dw_ref)
            vec_ref[...] = jnp.zeros_like(vec_ref)

        sp, dsp = _softplus_neg(lam_ref[...])
        bias = bias_ref[...]
        for h in range(RNN_HEADS):
            hs = slice(h * LANES, (h + 1) * LANES)
            xh = xc_ref[:, hs]
            gates = _lru_gates(xh, _mm(xh, w_ref[h]), bias, sp, hs)
            dxh = jnp.zeros_like(xh)
            dpre = []
            for d, (g_ref_d, hp_ref_d) in enumerate(((gf_ref, hpf_ref), (gr_ref, hpr_ref))):
                r, ig, a, s = gates[d]
                db = g_ref_d[:, hs]
                da = db * hp_ref_d[:, hs]
                dxh = dxh + db * s * ig
                dlog_a = da * a - (db * ig * xh) * (a * a / s)
                dr = dlog_a * (-RG_LRU_C) * sp[d:d + 1, hs]
                di = db * s * xh
                dpr = dr * r * (1.0 - r)
                dpi = di * ig * (1.0 - ig)
                vec_ref[2 * d:2 * d + 1, hs] += jnp.sum(dpr, axis=0, keepdims=True)
                vec_ref[2 * d + 1:2 * d + 2, hs] += jnp.sum(dpi, axis=0, keepdims=True)
                vec_ref[4 + d:5 + d, hs] += jnp.sum(dlog_a * r, axis=0, keepdims=True) * (-RG_LRU_C) * dsp[d:d + 1, hs]
                dpre += [dpr, dpi]
            dcat = jnp.concatenate(dpre, axis=1)
            dw_ref[h] += _mm_tn(xh, dcat)
            dxc_ref[:, hs] = dxh + _mm_nt(dcat, w_ref[h])

    return pl.pallas_call(
        body, name="odd_gates_bwd", grid=(steps,),
        in_specs=[_rows(tm, D_MODEL)] * 5 + [_full((8, LANES, 512)), _full((4, D_MODEL)), _full((2, D_MODEL))],
        out_specs=[_rows(tm, D_MODEL), _full((8, LANES, 512)), _full((8, D_MODEL))],
        out_shape=[_sds((seq, D_MODEL)), _sds((8, LANES, 512)), _sds((8, D_MODEL))],
        compiler_params=_params("arbitrary"),
    )(xc, gf, gr, hpf, hpr, wcat, bias, lam)


def _odd_proj_bwd(dxc, xr, dg, x1, dres, mod, conv_w, w_in4, seq):
    tm = _row_tile(seq, 512)
    steps = seq // tm

    def body(dp_ref, dm_ref, dn_ref, xp_ref, xm_ref, xn_ref, dg_ref, x_ref, dres_ref, mod_ref, cw_ref, w_ref,
             dx_ref, dpb_ref, vec_ref):
        i = pl.program_id(0)

        @pl.when(i == 0)
        def _():
            vec_ref[...] = jnp.zeros_like(vec_ref)

        dext = _extended(dp_ref, dm_ref, dn_ref, i, steps)
        xext = _extended(xp_ref, xm_ref, xn_ref, i, steps)
        dxc_m = dm_ref[...]
        dxr = sum(cw_ref[kk:kk + 1, :] * _shifted(dext, 2 - kk, tm) for kk in range(4))
        for kk in range(4):
            vec_ref[kk:kk + 1, :] += jnp.sum(dxc_m * _shifted(xext, kk - 2, tm), axis=0, keepdims=True)
        vec_ref[4:5, :] += jnp.sum(dxc_m, axis=0, keepdims=True)
        dpb_ref[:, :D_MODEL] = dxr.astype(dpb_ref.dtype)
        dpb_ref[:, D_MODEL:] = dg_ref[...].astype(dpb_ref.dtype)
        cs = ODD_IN // 4
        dh = sum(_mm_nt(dpb_ref[:, s * cs:(s + 1) * cs], w_ref[s]) for s in range(4))
        x = x_ref[...]
        vec_ref[5:6, :] += jnp.sum(dh, axis=0, keepdims=True)
        vec_ref[6:7, :] += jnp.sum(dh * x, axis=0, keepdims=True)
        dx_ref[...] = dres_ref[...] + dh * (1.0 + mod_ref[1:2, :])

    return pl.pallas_call(
        body, name="odd_proj_bwd", grid=(steps,),
        in_specs=_halo_specs(tm, seq, D_MODEL) + _halo_specs(tm, seq, D_MODEL) + [_rows(tm, D_MODEL)] * 3
        + [_full((3, D_MODEL)), _full((4, D_MODEL)), _const((4, D_MODEL, ODD_IN // 4))],
        out_specs=[_rows(tm, D_MODEL), _rows(tm, ODD_IN), _full((8, D_MODEL))],
        out_shape=[_sds((seq, D_MODEL)), _sds((seq, ODD_IN), MXU_DTYPE), _sds((8, D_MODEL))],
        compiler_params=_params("arbitrary"),
    )(dxc, dxc, dxc, xr, xr, xr, dg, x1, dres, mod, conv_w, w_in4)


def _tn_matmul(a, b, seq, name, transposed):
    n = b.shape[1]
    tn = n // 2
    cs = n // 4
    tm = _row_tile(seq, 512)
    steps = seq // tm

    def body(a_ref, b_ref, o_ref, acc_ref):
        i = pl.program_id(1)

        @pl.when(i == 0)
        def _():
            acc_ref[...] = jnp.zeros_like(acc_ref)

        acc_ref[...] += lax.dot_general(a_ref[...], b_ref[...], (((0,), (0,)), ((), ())), preferred_element_type=F32)

        @pl.when(i == steps - 1)
        def _():
            if transposed:
                o_ref[...] = acc_ref[...].T
            else:
                o_ref[0] = acc_ref[:, 0:cs]
                o_ref[1] = acc_ref[:, cs:2 * cs]

    if transposed:
        out_spec, out_shape = pl.BlockSpec((tn, D_MODEL), lambda j, i: (j, 0)), _sds((n, D_MODEL))
    else:
        out_spec, out_shape = pl.BlockSpec((2, D_MODEL, cs), lambda j, i: (j, 0, 0)), _sds((4, D_MODEL, cs))
    return pl.pallas_call(
        body, name=name, grid=(2, steps),
        in_specs=[pl.BlockSpec((tm, D_MODEL), lambda j, i: (i, 0)), pl.BlockSpec((tm, tn), lambda j, i: (i, j))],
        out_specs=out_spec, out_shape=out_shape,
        scratch_shapes=[pltpu.VMEM((D_MODEL, tn), F32)], compiler_params=_params("parallel", "arbitrary"),
    )(a, b)


def _even_out_bwd(dx1, z, ycat, g, mod, ln_g, w_out, seq):
    tm = _row_tile(seq, 512)
    steps = seq // tm

    def body(dx_ref, z_ref, y_ref, g_ref, mod_ref, lg_ref, w_ref, dy_ref, dg_ref, dres_ref, dw_ref, vec_ref):
        i = pl.program_id(0)

        @pl.when(i == 0)
        def _():
            dw_ref[...] = jnp.zeros_like(dw_ref)
            vec_ref[...] = jnp.zeros_like(vec_ref)

        zhat, rstd = _ln_stats(z_ref[...])
        dx1_ = dx_ref[...]
        dz = _ln_bwd(dx1_, zhat, rstd, lg_ref[...])
        vec_ref[0:1, :] += jnp.sum(dx1_ * zhat, axis=0, keepdims=True)
        vec_ref[1:2, :] += jnp.sum(dx1_, axis=0, keepdims=True)
        dres_ref[...] = ALPHA * dz
        gate = mod_ref[2:3, :]
        gg = g_ref[...]
        sg = _sigmoid(gg)
        silu = gg * sg
        ycat_ = y_ref[...]
        dw_ref[...] += _mm_tn(ycat_ * silu, dz)
        dy = _mm_nt(gate * dz, w_ref[...])
        dy_ref[...] = dy * silu
        dg_ref[...] = dy * ycat_ * (sg * (1.0 + gg * (1.0 - sg)))

        @pl.when(i == steps - 1)
        def _():
            m_acc = dw_ref[...]
            vec_ref[2:3, :] = jnp.sum(w_ref[...].astype(F32) * m_acc, axis=0, keepdims=True)
            dw_ref[...] = m_acc * gate

    return pl.pallas_call(
        body, name="even_out_bwd", grid=(steps,),
        in_specs=[_rows(tm, D_MODEL)] * 4 + [_full((3, D_MODEL)), _full((1, D_MODEL)), _const((D_MODEL, D_MODEL))],
        out_specs=[_rows(tm, D_MODEL)] * 3 + [_full((D_MODEL, D_MODEL)), _full((8, D_MODEL))],
        out_shape=[_sds((seq, D_MODEL))] * 3 + [_sds((D_MODEL, D_MODEL)), _sds((8, D_MODEL))],
        compiler_params=_params("arbitrary"),
    )(dx1, z, ycat, g, mod, ln_g, w_out)


def _even_mix_bwd(q, k, v, lse, ycat, dycat, su, sv, sink, sgln_g, sgln_b, sgw, sgb_full, e2, e8, seq):
    nb = seq // BLK

    def body(sink_ref, q_ref, k_ref, v_ref, lse_ref, y_ref, dy_ref, su_ref, sv_ref, lng_ref, lnb_ref, sgw_ref, sgb_ref, e2_ref,
             e8_ref, dq_ref, dsu_ref, dsv_ref, dk_ref, dv_ref, dsgw_ref, dsgb_ref, vec_ref, dsink_ref, dsgb_acc):
        n = pl.program_id(0)

        @pl.when(n == 0)
        def _():
            dk_ref[...] = jnp.zeros_like(dk_ref)
            dv_ref[...] = jnp.zeros_like(dv_ref)
            dsgw_ref[...] = jnp.zeros_like(dsgw_ref)
            dsgb_acc[...] = jnp.zeros_like(dsgb_acc)
            vec_ref[...] = jnp.zeros_like(vec_ref)
            dsink_ref[...] = jnp.zeros_like(dsink_ref)

        kband = _band(k_ref, n, nb)
        vband = _band(v_ref, n, nb)
        bias = _band_bias(n, seq)
        lane = _lane_iota((BLK, LANES))
        row8 = lax.broadcasted_iota(jnp.int32, (8, LANES), 0)
        lse = lse_ref[...]
        dkb = jnp.zeros((LANES, 3 * BLK), F32)
        dvb = jnp.zeros((LANES, 3 * BLK), F32)
        dsink = jnp.zeros((8, LANES), F32)
        q_tile = lambda j: q_ref[:, j * LANES:(j + 1) * LANES].astype(F32)
        do_tile = lambda j: dy_ref[:, j * LANES:(j + 1) * LANES]
        dq = [jnp.zeros((BLK, LANES), F32) for _ in range(ATTN_WIDTH // LANES)]
        for kv in range(N_Q_HEADS // Q_PER_KV):
            heads = range(Q_PER_KV * kv, Q_PER_KV * (kv + 1))
            lse4, delta4 = [], []
            for h in heads:
                head_lanes = (lane < HEAD_DIM) if h % 2 == 0 else (lane >= HEAD_DIM)
                lse4.append(jnp.sum(jnp.where(lane == h, lse, 0.0), axis=1, keepdims=True))
                delta4.append(jnp.sum(jnp.where(head_lanes, do_tile(h // 2) * y_ref[:, (h // 2) * LANES:(h // 2 + 1) * LANES], 0.0),
                                      axis=1, keepdims=True))
            lse4, delta4 = jnp.concatenate(lse4, axis=0), jnp.concatenate(delta4, axis=0)
            q4, do4 = _stack_heads(q_tile, kv), _stack_heads(do_tile, kv)
            s = _mm_nt(q4, kband) * (HEAD_DIM ** -0.5) + bias
            p = jnp.exp(s - lse4)
            wsink = jnp.exp(_per_head_column([sink_ref[h] for h in heads]) - lse4) * delta4
            ds = p * (_mm_nt(do4, vband) - delta4) * (HEAD_DIM ** -0.5)
            dq4 = _mm(ds, kband)
            dkb = dkb + _mm_tn(q4, ds)
            dvb = dvb + _mm_tn(do4, p)
            for g, h in enumerate(heads):
                dq[h // 2] = dq[h // 2] + _from_kv_lanes(dq4[g * BLK:(g + 1) * BLK], h)
                dsink = dsink + jnp.where(row8 == h, -jnp.sum(wsink[g * BLK:(g + 1) * BLK]), 0.0)
        for j in range(ATTN_WIDTH // LANES):
            dq_ref[:, j * LANES:(j + 1) * LANES] = dq[j]
        dsink_ref[...] += dsink
        prev = jnp.maximum(n - 1, 0)
        nxt = jnp.minimum(n + 1, nb - 1)
        for part, blk_i in enumerate((prev, n, nxt)):
            rows = pl.ds(pl.multiple_of(blk_i * BLK, BLK), BLK)
            dk_ref[rows, :] += dkb[:, part * BLK:(part + 1) * BLK].T
            dv_ref[rows, :] += dvb[:, part * BLK:(part + 1) * BLK].T

        e2 = e2_ref[...]
        lng = lng_ref[...]
        svo, vn, vhat, rstd = _sg_forward(sv_ref[...], lng, lnb_ref[...], sgw_ref, sgb_ref[...], e2)
        for j in range(SG_WIDTH // LANES):
            cs = slice(j * LANES, (j + 1) * LANES)
            dysg = dy_ref[:, ATTN_WIDTH + j * LANES:ATTN_WIDTH + (j + 1) * LANES]
            dsu_ref[:, cs] = dysg * svo[j]
            dsvo = dysg * su_ref[:, cs]
            dsgb_acc[:, cs] += dsvo
            d_lo = jnp.where(lane < HEAD_DIM, dsvo, 0.0)
            d_hi = dsvo - d_lo
            dsgw_ref[2 * j] += _mm_nt(d_lo, vn[j])
            dsgw_ref[2 * j + 1] += _mm_nt(d_hi, vn[j])
            dvn = _mm_tn(sgw_ref[2 * j], d_lo) + _mm_tn(sgw_ref[2 * j + 1], d_hi)
            vec_ref[0:1, cs] += jnp.sum(dvn * vhat[j], axis=0, keepdims=True)
            vec_ref[1:2, cs] += jnp.sum(dvn, axis=0, keepdims=True)
            dvh = dvn * lng[:, cs]
            m1 = _group_sum(dvh, e2) * (1.0 / HEAD_DIM)
            m2 = _group_sum(dvh * vhat[j], e2) * (1.0 / HEAD_DIM)
            dsv_ref[:, cs] = rstd[j] * (dvh - m1 - vhat[j] * m2)

        @pl.when(n == nb - 1)
        def _():
            rest = dsgb_acc[...]
            total = jnp.zeros((8, BLK), F32)
            for _ in range(3):
                part = rest.astype(MXU_DTYPE)
                total = total + lax.dot_general(e8_ref[...], part, (((1,), (1,)), ((), ())), preferred_element_type=F32)
                rest = rest - part.astype(F32)
            dsgb_ref[...] = total

    blk = lambda w: pl.BlockSpec((BLK, w), lambda n: (n, 0))
    return pl.pallas_call(
        body, name="even_mix_bwd", grid=(nb,),
        in_specs=[pl.BlockSpec(memory_space=pltpu.SMEM), blk(512), _full((seq, LANES)), _full((seq, LANES)), blk(LANES),
                  blk(D_MODEL), blk(D_MODEL), blk(512), blk(512), _full((1, 512)), _full((1, 512)), _full((8, BLK, BLK)),
                  _full((BLK, 512)), _full((LANES, LANES)), _full((8, 512))],
        out_specs=[blk(512), blk(512), blk(512), _full((seq, LANES)), _full((seq, LANES)), _full((8, BLK, BLK)),
                   _full((8, BLK)), _full((8, 512)), _full((8, LANES))],
        out_shape=[_sds((seq, 512)), _sds((seq, 512)), _sds((seq, 512)), _sds((seq, LANES)), _sds((seq, LANES)),
                   _sds((8, BLK, BLK)), _sds((8, BLK)), _sds((8, 512)), _sds((8, LANES))],
        scratch_shapes=[pltpu.VMEM((BLK, 512), F32)],
        compiler_params=_params("arbitrary"),
    )(sink, q, k, v, lse, ycat, dycat, su, sv, sgln_g, sgln_b, sgw, sgb_full, e2, e8)


def _even_proj_bwd(dq, dk, dv, dsu, dsv, dg, x, dres, mod, tabs, w_in_t, seq):
    tm = _row_tile(seq, 512)

    def body(dq_ref, dk_ref, dv_ref, dsu_ref, dsv_ref, dg_ref, x_ref, dres_ref, mod_ref, cos_ref, sp_ref, sm_ref, wt_ref,
             dx_ref, dpb_ref, vec_ref):
        @pl.when(pl.program_id(0) == 0)
        def _():
            vec_ref[...] = jnp.zeros_like(vec_ref)

        cos_t, sin_p, sin_m = cos_ref[...], sp_ref[...], sm_ref[...]
        dt = dpb_ref.dtype
        for j in range(ATTN_WIDTH // LANES):
            cs = slice(j * LANES, (j + 1) * LANES)
            dpb_ref[:, cs] = _rope_t(dq_ref[:, cs], cos_t, sin_p, sin_m).astype(dt)
        dpb_ref[:, 512:640] = _rope_t(dk_ref[...], cos_t, sin_p, sin_m).astype(dt)
        dpb_ref[:, 640:768] = dv_ref[...].astype(dt)
        dpb_ref[:, 768:1280] = dsu_ref[...].astype(dt)
        dpb_ref[:, 1280:1792] = dsv_ref[...].astype(dt)
        dpb_ref[:, 1792:2816] = dg_ref[...].astype(dt)
        dh = jnp.dot(dpb_ref[...], wt_ref[...], preferred_element_type=F32)
        x_ = x_ref[...]
        vec_ref[0:1, :] += jnp.sum(dh, axis=0, keepdims=True)
        vec_ref[1:2, :] += jnp.sum(dh * x_, axis=0, keepdims=True)
        dx_ref[...] = dres_ref[...] + dh * (1.0 + mod_ref[1:2, :])

    return pl.pallas_call(
        body, name="even_proj_bwd", grid=(seq // tm,),
        in_specs=[_rows(tm, 512), _rows(tm, LANES), _rows(tm, LANES), _rows(tm, 512), _rows(tm, 512), _rows(tm, D_MODEL),
                  _rows(tm, D_MODEL), _rows(tm, D_MODEL), _full((3, D_MODEL))] + [_rows(tm, LANES)] * 3
        + [_const((EVEN_IN, D_MODEL))],
        out_specs=[_rows(tm, D_MODEL), _rows(tm, EVEN_IN), _full((8, D_MODEL))],
        out_shape=[_sds((seq, D_MODEL)), _sds((seq, EVEN_IN), MXU_DTYPE), _sds((8, D_MODEL))],
        compiler_params=_params("arbitrary"),
    )(dq, dk, dv, dsu, dsv, dg, x, dres, mod, *tabs, w_in_t)


def _local_step(x, posf, tgt, mod, w, seq):
    mxu = lambda a: a.astype(MXU_DTYPE)
    row = lambda a: a.reshape(1, -1)
    tabs = _rope_tables(posf, seq)
    e2 = mxu(jnp.kron(jnp.eye(2, dtype=F32), jnp.ones((HEAD_DIM, HEAD_DIM), F32)))
    e8 = mxu(jnp.repeat(jnp.eye(N_SG_GROUPS, dtype=F32), HEAD_DIM, axis=1))
    sgw = mxu(w["ev_sg_w"])
    sgb_full = jnp.repeat(w["ev_sg_b"].T, HEAD_DIM, axis=1)
    sgln_g, sgln_b = row(w["ev_sg_ln_g"]), row(w["ev_sg_ln_b"])
    sink = w["ev_sink"].reshape(N_Q_HEADS)
    ev_w_in_t, ev_w_out = mxu(w["ev_w_in_t"]), mxu(w["ev_w_out"])
    od_w_in, od_w_out = mxu(w["od_w_in"]), mxu(w["od_w_out"])
    wcat = mxu(jnp.concatenate([w["od_w_a"][0], w["od_w_x"][0], w["od_w_a"][1], w["od_w_x"][1]], axis=2))
    gate_bias = jnp.stack([w["od_b_a"][0], w["od_b_x"][0], w["od_b_a"][1], w["od_b_x"][1]])
    conv_b = row(w["od_conv_b"])
    ln_g, ln_b = w["ln_g"], w["ln_b"]

    h0, q, k, v, su, sv, g0 = _even_proj(x, mod[0], ev_w_in_t, tabs, seq)
    ycat, lse = _even_mix(q, k, v, su, sv, sink, sgln_g, sgln_b, sgw, sgb_full, e2, seq)
    z0, x1 = _even_out(ycat, g0, x, mod[0], ev_w_out, ln_g[0:1], ln_b[0:1], seq)
    h1, xr, g1 = _odd_proj(x1, mod[1], od_w_in, seq)
    xc, a_f, b_f, a_r, b_r = _odd_gates(xr, w["od_conv_w"], conv_b, wcat, gate_bias, w["od_lam"], seq)
    hf, hpf = _scan(a_f, b_f, seq, descending=False, post=False, name="scan_fwd")
    hr, hpr = _scan(a_r, b_r, seq, descending=True, post=False, name="scan_rev")
    dhs, dg1, dres1, loss, d_od_w_out, vec_o = _odd_out_and_loss(hf, hr, g1, x1, tgt, mod[1], od_w_out, ln_g[1:2], ln_b[1:2], seq)
    (gf,) = _scan(a_f, dhs, seq, descending=True, post=True, name="scan_fwd_bwd")
    (gr,) = _scan(a_r, dhs, seq, descending=False, post=True, name="scan_rev_bwd")
    dxc, d_wcat, vec_g = _odd_gates_bwd(xc, gf, gr, hpf, hpr, wcat, gate_bias, w["od_lam"], seq)
    dx1, dp1, vec_p = _odd_proj_bwd(dxc, xr, dg1, x1, dres1, mod[1], w["od_conv_w"], od_w_in, seq)
    d_od_w_in = _tn_matmul(h1, dp1, seq, "odd_dw_in", transposed=False)
    dycat, dg0, dres0, d_ev_w_out, vec_e = _even_out_bwd(dx1, z0, ycat, g0, mod[0], ln_g[0:1], ev_w_out, seq)
    dq, dsu, dsv, dk, dv, d_sgw, d_sgb, vec_s, d_sink = _even_mix_bwd(q, k, v, lse, ycat, dycat, su, sv, sink, sgln_g, sgln_b,
                                                                      sgw, sgb_full, e2, e8, seq)
    grad_x, dp0, vec_x = _even_proj_bwd(dq, dk, dv, dsu, dsv, dg0, x, dres0, mod[0], tabs, ev_w_in_t, seq)
    d_ev_w_in_t = _tn_matmul(h0, dp0, seq, "even_dw_in", transposed=True)

    dmod = jnp.stack([jnp.stack([vec_x[0], vec_x[1], vec_e[2]]), jnp.stack([vec_p[5], vec_p[6], vec_o[2]])])
    grads = {
        "ln_g": jnp.stack([vec_e[0], vec_o[0]]), "ln_b": jnp.stack([vec_e[1], vec_o[1]]),
        "ev_w_in_t": d_ev_w_in_t, "ev_w_out": d_ev_w_out, "ev_sink": d_sink[:, 0],
        "ev_sg_ln_g": vec_s[0], "ev_sg_ln_b": vec_s[1], "ev_sg_w": d_sgw,
        "ev_sg_b": d_sgb,
        "od_w_in": d_od_w_in, "od_conv_w": vec_p[0:4], "od_conv_b": vec_p[4],
        "od_w_a": jnp.stack([d_wcat[:, :, 0:128], d_wcat[:, :, 256:384]]),
        "od_w_x": jnp.stack([d_wcat[:, :, 128:256], d_wcat[:, :, 384:512]]),
        "od_b_a": jnp.stack([vec_g[0], vec_g[2]]), "od_b_x": jnp.stack([vec_g[1], vec_g[3]]),
        "od_lam": vec_g[4:6], "od_w_out": d_od_w_out,
    }
    return loss[0, 0], grad_x, dmod, grads


def _place():
    return lax.axis_index("x"), lax.axis_index("y"), lax.axis_index("c")


def _allgather8(block, name):
    m_per, n = block.shape

    def body(x_ref, out_ref, send_sems, recv_sems, local_sem):
        x, y, c = _place()
        me, sibling = (x, y, c), (x, y, 1 - c)
        chips = [(1 - x, y), (x, 1 - y), (1 - x, 1 - y)]

        def rows(px, py, pc):
            return out_ref.at[pl.ds((4 * px + 2 * py + pc) * m_per, m_per), :]

        def copy(k, blk, to, src=None):
            return pltpu.make_async_remote_copy(src_ref=rows(*blk) if src is None else src, dst_ref=rows(*blk),
                                                send_sem=send_sems.at[k], recv_sem=recv_sems.at[k], device_id=to,
                                                device_id_type=MESH)

        mine = pltpu.make_async_copy(x_ref, rows(*me), local_sem)
        mine.start()
        first = [copy(0, me, sibling, src=x_ref)] + [copy(1 + j, me, (*chip, c), src=x_ref) for j, chip in enumerate(chips)]
        for cp in first:
            cp.start()
        passed = [copy(4 + j, (*chip, c), sibling) for j, chip in enumerate(chips)]
        for j, chip in enumerate(chips):
            copy(1 + j, (*chip, c), me).wait_recv()
            passed[j].start()
        copy(0, sibling, me).wait_recv()
        for j, chip in enumerate(chips):
            copy(4 + j, (*chip, 1 - c), me).wait_recv()
        for cp in first + passed:
            cp.wait_send()
        mine.wait()

    return pl.pallas_call(
        body, name=name, out_shape=_sds((8 * m_per, n), block.dtype),
        in_specs=[pl.BlockSpec(memory_space=pltpu.VMEM)], out_specs=pl.BlockSpec(memory_space=pltpu.VMEM),
        scratch_shapes=[pltpu.SemaphoreType.DMA((7,)), pltpu.SemaphoreType.DMA((7,)), pltpu.SemaphoreType.DMA],
        compiler_params=pltpu.CompilerParams(vmem_limit_bytes=VMEM_LIMIT),
    )(block)


class _Copies:
    def __init__(self, send_sems, recv_sems, local_sems, stages):
        self.send_sems, self.recv_sems, self.local_sems, self.stages = send_sems, recv_sems, local_sems, stages
        self.sent, self.staged, self.locals = [], [], []

    def remote(self, k, src, dst, to):
        return pltpu.make_async_remote_copy(src_ref=src, dst_ref=dst, send_sem=self.send_sems.at[k], recv_sem=self.recv_sems.at[k],
                                            device_id=to, device_id_type=MESH)

    def send(self, k, src, dst, to):
        cp = self.remote(k, src, dst, to)
        cp.start()
        self.sent.append(cp)

    def arrived(self, k, dst, frm):
        self.remote(k, dst, dst, frm).wait_recv()

    def local(self, src, dst):
        k = len(self.staged)
        cp = pltpu.make_async_copy(src, self.stages[k], self.local_sems.at[2 * k])
        cp.start()
        self.staged.append((cp, dst))

    def flush(self):
        for k in range(len(self.locals), len(self.staged)):
            cp, dst = self.staged[k]
            cp.wait()
            out = pltpu.make_async_copy(self.stages[k], dst, self.local_sems.at[2 * k + 1])
            out.start()
            self.locals.append(out)

    def drain(self):
        self.flush()
        for cp in self.sent:
            cp.wait_send()
        for cp in self.locals:
            cp.wait()


def _comm_call(body, name, ins, out_shapes, n_remote, stages):
    n_in, n_out = len(ins), len(out_shapes)

    def kern(*refs):
        in_refs, out_refs = refs[:n_in], refs[n_in:n_in + n_out]
        send_sems, recv_sems, local_sems = refs[n_in + n_out:n_in + n_out + 3]
        body(_Copies(send_sems, recv_sems, local_sems, refs[n_in + n_out + 3:]), in_refs, out_refs)

    hbm = pl.BlockSpec(memory_space=pl.ANY)
    return pl.pallas_call(
        kern, name=name, out_shape=out_shapes, in_specs=[hbm] * n_in, out_specs=[hbm] * n_out,
        scratch_shapes=[pltpu.SemaphoreType.DMA((n_remote,)), pltpu.SemaphoreType.DMA((n_remote,)),
                        pltpu.SemaphoreType.DMA((2 * len(stages),))] + [pltpu.VMEM(s, d) for s, d in stages],
        compiler_params=pltpu.CompilerParams(vmem_limit_bytes=VMEM_LIMIT),
    )(*ins)


def _gather_to_all(cps, src, dst, me, sibling, other_chips, c, base):
    idx = lambda p: 4 * p[0] + 2 * p[1] + p[2]
    cps.local(src, dst.at[idx(me)])
    cps.send(base, src, dst.at[idx(me)], sibling)
    for j, chip in enumerate(other_chips):
        cps.send(base + 1 + j, src, dst.at[idx(me)], (*chip, c))
    cps.flush()
    for j, chip in enumerate(other_chips):
        got = dst.at[idx((*chip, c))]
        cps.arrived(base + 1 + j, got, (*chip, c))
        cps.send(base + 4 + j, got, got, sibling)
    cps.arrived(base, dst.at[idx(sibling)], sibling)
    for j, chip in enumerate(other_chips):
        cps.arrived(base + 4 + j, dst.at[idx((*chip, 1 - c))], sibling)


def _gather_weights(shards, small):
    n = len(shards)

    def body(cps, ins, outs):
        x, y, c = _place()
        me, sibling, mine = (x, y, c), (x, y, 1 - c), 2 * x + y
        chips = [(1 - x, y), (x, 1 - y), (1 - x, 1 - y)]
        for i in range(n):
            cps.local(ins[i], outs[i].at[mine])
        for j, (px, py) in enumerate(chips):
            for i in range(n):
                hr = shards[i].shape[0] // 2
                rows = pl.ds(c * hr, hr)
                cps.send(6 * i + j, ins[i].at[rows], outs[i].at[mine, rows], (px, py, c))
        _gather_to_all(cps, ins[n], outs[n], me, sibling, chips, c, 6 * n)
        for j, (px, py) in enumerate(chips):
            for i in range(n):
                hr = shards[i].shape[0] // 2
                got = outs[i].at[2 * px + py, pl.ds(c * hr, hr)]
                cps.arrived(6 * i + j, got, (px, py, c))
                cps.send(6 * i + 3 + j, got, got, sibling)
        for j, (px, py) in enumerate(chips):
            for i in range(n):
                hr = shards[i].shape[0] // 2
                cps.arrived(6 * i + 3 + j, outs[i].at[2 * px + py, pl.ds((1 - c) * hr, hr)], sibling)
        cps.drain()

    return _comm_call(body, "gather_weights", list(shards) + [small],
                      [_sds((4,) + s.shape, s.dtype) for s in shards] + [_sds((8,) + small.shape, small.dtype)], 6 * n + 7,
                      [(a.shape, a.dtype) for a in list(shards) + [small]])


def _reduce_sibling(parts, dmod_rows):
    n = len(parts)

    def body(cps, ins, outs):
        x, y, c = _place()
        me, sibling = (x, y, c), (x, y, 1 - c)
        chips = [(1 - x, y), (x, 1 - y), (1 - x, 1 - y)]
        for i in range(n):
            cps.send(i, ins[i].at[:, 1 - c], outs[i], sibling)
        _gather_to_all(cps, ins[n], outs[n], me, sibling, chips, c, n)
        for i in range(n):
            cps.arrived(i, outs[i], sibling)
        cps.drain()

    return _comm_call(body, "reduce_sibling", list(parts) + [dmod_rows],
                      [_sds((4,) + p.shape[2:], p.dtype) for p in parts] + [_sds((8,) + dmod_rows.shape, dmod_rows.dtype)], n + 7,
                      [(dmod_rows.shape, dmod_rows.dtype)])


def _reduce_chips(parts):
    n = len(parts)

    def body(cps, ins, outs):
        x, y, c = _place()
        mine = 2 * x + y
        chips = [(1 - x, y), (x, 1 - y), (1 - x, 1 - y)]
        for i in range(n):
            cps.local(ins[i].at[mine], outs[i].at[mine])
        for j, (px, py) in enumerate(chips):
            for i in range(n):
                cps.send(3 * i + j, ins[i].at[2 * px + py], outs[i].at[mine], (px, py, c))
        cps.flush()
        for j, (px, py) in enumerate(chips):
            for i in range(n):
                cps.arrived(3 * i + j, outs[i].at[2 * px + py], (px, py, c))
        cps.drain()

    return _comm_call(body, "reduce_chips", list(parts), [_sds(p.shape, p.dtype) for p in parts], 3 * n,
                      [(p.shape[1:], p.dtype) for p in parts])


def _gather_reduced(shard_parts, repl_parts):
    ns, nr = len(shard_parts), len(repl_parts)

    def body(cps, ins, outs):
        x, y, c = _place()
        me, sibling = (x, y, c), (x, y, 1 - c)
        chips = [(1 - x, y), (x, 1 - y), (1 - x, 1 - y)]
        for i in range(ns):
            cps.local(ins[i], outs[i].at[c])
            cps.send(i, ins[i], outs[i].at[c], sibling)
        for i in range(nr):
            _gather_to_all(cps, ins[ns + i], outs[ns + i], me, sibling, chips, c, ns + 7 * i)
        for i in range(ns):
            cps.arrived(i, outs[i].at[1 - c], sibling)
        cps.drain()

    return _comm_call(body, "gather_reduced", list(shard_parts) + list(repl_parts),
                      [_sds((2,) + p.shape, p.dtype) for p in shard_parts] + [_sds((8,) + p.shape, p.dtype) for p in repl_parts],
                      ns + 7 * nr, [(p.shape, p.dtype) for p in list(shard_parts) + list(repl_parts)])


def _sum_sibling(core, parts, got, wire):
    n = len(parts)

    def body(core_ref, *refs):
        for i in range(n):
            refs[2 * n + i][0] = (refs[i][0] + refs[n + i][0]).astype(wire[i])

    keep_spec = lambda p: pl.BlockSpec((1, None) + p.shape[2:], lambda s, core_ref: (s, core_ref[0], 0, 0))
    slot_spec = lambda p: pl.BlockSpec((1,) + p.shape[2:], lambda s, core_ref: (s, 0, 0))
    return pl.pallas_call(
        body, name="sum_sibling",
        grid_spec=pltpu.PrefetchScalarGridSpec(
            num_scalar_prefetch=1, grid=(4,), in_specs=[keep_spec(p) for p in parts] + [slot_spec(p) for p in parts],
            out_specs=[slot_spec(p) for p in parts]),
        out_shape=[_sds((4,) + p.shape[2:], wire[i]) for i, p in enumerate(parts)],
        compiler_params=_params("parallel"),
    )(core, *parts, *got)


def _sum_slots(slots, name):
    n = len(slots)

    def spec_pair(p):
        k, rows, cols = p.shape
        sub = 16 if p.dtype == BF16 else 8
        if (rows // 2) % sub == 0:
            return pl.BlockSpec((k, rows // 2, cols), lambda i: (0, i, 0)), pl.BlockSpec((rows // 2, cols), lambda i: (i, 0))
        return pl.BlockSpec((k, rows, cols), lambda i: (0, 0, 0)), pl.BlockSpec((rows, cols), lambda i: (0, 0))

    pairs = [spec_pair(p) for p in slots]

    def body(*refs):
        for i in range(n):
            acc = refs[i][0].astype(F32)
            for j in range(1, slots[i].shape[0]):
                acc = acc + refs[i][j].astype(F32)
            refs[n + i][...] = acc

    return pl.pallas_call(
        body, name=name, grid=(2,), in_specs=[a for a, _ in pairs], out_specs=[b for _, b in pairs],
        out_shape=[_sds(p.shape[1:]) for p in slots], compiler_params=_params("arbitrary"),
    )(*slots)


def _modulation(c_all, ada_w, ada_b):
    cols = ada_w.shape[2]

    def body(c_ref, w_ref, b_ref, o_ref):
        cc = c_ref[...]
        o_ref[0] = _mm(cc * _sigmoid(cc), w_ref[0]) + b_ref[0]

    return pl.pallas_call(
        body, name="modulation", grid=(2,),
        in_specs=[_full((8, D_MODEL)), pl.BlockSpec((1, D_MODEL, cols), lambda l: (l, 0, 0)), pl.BlockSpec((1, 1, cols), lambda l: (l, 0, 0))],
        out_specs=pl.BlockSpec((1, 8, cols), lambda l: (l, 0, 0)), out_shape=_sds((2, 8, cols)),
        compiler_params=_params("parallel"),
    )(c_all, ada_w, ada_b)


def _adamw_math(w, g, m, v):
    m = ADAM_B1 * m + (1.0 - ADAM_B1) * g
    v = ADAM_B2 * v + (1.0 - ADAM_B2) * (g * g)
    m_hat = m / (1.0 - ADAM_B1 ** ADAM_STEP)
    v_hat = v / (1.0 - ADAM_B2 ** ADAM_STEP)
    delta = -ADAM_LR * (m_hat / (jnp.sqrt(v_hat) + ADAM_EPS) + ADAM_WD * w)
    return delta, m, v


def _ada_update(c_all, dmod, w, m, v):
    cols = w.shape[2]
    tr = 256
    spec3 = pl.BlockSpec((1, tr, cols), lambda l, i: (l, i, 0))

    def body(c_ref, d_ref, w_ref, m_ref, v_ref, g_ref, dl_ref, nm_ref, nv_ref):
        cc = c_ref[...]
        g = _mm_tn(cc * _sigmoid(cc), d_ref[0])
        g_ref[0] = g
        dl_ref[0], nm_ref[0], nv_ref[0] = _adamw_math(w_ref[0], g, m_ref[0], v_ref[0])

    return pl.pallas_call(
        body, name="ada_update", grid=(2, D_MODEL // tr),
        in_specs=[pl.BlockSpec((8, tr), lambda l, i: (0, i)), pl.BlockSpec((1, 8, cols), lambda l, i: (l, 0, 0)), spec3, spec3, spec3],
        out_specs=[spec3] * 4, out_shape=[_sds(w.shape)] * 4, compiler_params=_params("parallel", "parallel"),
    )(c_all, dmod, w, m, v)


def _adamw(w, g, m, v, name):
    rows, n = w.shape
    tr = next(t for t in (256, 128, 64, 32, 16, 8, rows) if rows % t == 0)

    def body(w_ref, g_ref, m_ref, v_ref, dl_ref, nm_ref, nv_ref):
        dl_ref[...], nm_ref[...], nv_ref[...] = _adamw_math(w_ref[...], g_ref[...], m_ref[...], v_ref[...])

    return pl.pallas_call(body, name=name, grid=(rows // tr,), in_specs=[_rows(tr, n)] * 4, out_specs=[_rows(tr, n)] * 3,
                          out_shape=[_sds((rows, n))] * 3, compiler_params=_params("parallel"))(w, g, m, v)


def _adamw_small(params):
    n = len(params)

    def body(*refs):
        ins, outs = refs[:4 * n], refs[4 * n:]
        for j in range(n):
            w_ref, g_ref, m_ref, v_ref = ins[4 * j:4 * j + 4]
            outs[3 * j][...], outs[3 * j + 1][...], outs[3 * j + 2][...] = _adamw_math(w_ref[...], g_ref[...], m_ref[...], v_ref[...])

    flat = [a for p in params for a in p]
    res = pl.pallas_call(body, name="adamw_small", out_shape=[_sds(p[0].shape) for p in params for _ in range(3)])(*flat)
    return [tuple(res[3 * j:3 * j + 3]) for j in range(n)]


def _cols(a, start, size):
    return lax.dynamic_slice_in_dim(a, start, size, axis=a.ndim - 1)


def kernel(x, c, positions, ada_w, ada_b, ln_g, ln_b, ev_w_in, ev_w_out, ev_sink, ev_sg_ln_g, ev_sg_ln_b, ev_sg_w, ev_sg_b, od_w_in, od_conv_w, od_conv_b, od_w_a, od_b_a, od_w_x, od_b_x, od_lam, od_w_out, loss_target, m_ada_w, m_ada_b, m_ln_g, m_ln_b, m_ev_w_in, m_ev_w_out, m_ev_sink, m_ev_sg_ln_g, m_ev_sg_ln_b, m_ev_sg_w, m_ev_sg_b, m_od_w_in, m_od_conv_w, m_od_conv_b, m_od_w_a, m_od_b_a, m_od_w_x, m_od_b_x, m_od_lam, m_od_w_out, v_ada_w, v_ada_b, v_ln_g, v_ln_b, v_ev_w_in, v_ev_w_out, v_ev_sink, v_ev_sg_ln_g, v_ev_sg_ln_b, v_ev_sg_w, v_ev_sg_b, v_od_w_in, v_od_conv_w, v_od_conv_b, v_od_w_a, v_od_b_a, v_od_w_x, v_od_b_x, v_od_lam, v_od_w_out):
    seq = x.shape[1]
    px, py, pc = _place()
    chip = 2 * px + py
    dev = 2 * chip + pc

    small = jnp.concatenate([od_conv_w[0].reshape(-1), od_conv_b[0], od_b_a[0].reshape(-1), jnp.zeros((256,), F32),
                             od_b_x[0].reshape(-1), od_lam[0].reshape(-1)]).reshape(3, D_MODEL)
    blk = jnp.concatenate([c, small, jnp.zeros((4, D_MODEL), F32)], axis=0)
    tr = lambda a: jnp.swapaxes(a, -1, -2)
    wire_w = lambda a: a.astype(MXU_DTYPE)
    ev_w_in4, ev_w_out4, od_w_in4, od_w_out4, g_small = _gather_weights(
        [wire_w(tr(ev_w_in[0])), wire_w(ev_w_out[0]), wire_w(od_w_in[0]), wire_w(od_w_out[0])], blk)
    c_all = g_small[:, 0, :]
    per_chip = g_small[0::2]
    conv_w = per_chip[:, 1].reshape(4, 4, 256).transpose(1, 0, 2).reshape(4, D_MODEL)
    conv_b = per_chip[:, 2, 0:256].reshape(D_MODEL)
    b_a = per_chip[:, 2, 256:768].reshape(4, 2, 256).transpose(1, 0, 2).reshape(2, D_MODEL)
    b_x = per_chip[:, 3, 0:512].reshape(4, 2, 256).transpose(1, 0, 2).reshape(2, D_MODEL)
    lam = per_chip[:, 3, 512:1024].reshape(4, 2, 256).transpose(1, 0, 2).reshape(2, D_MODEL)

    w_full = {
        "ev_w_in_t": ev_w_in4.reshape(EVEN_IN, D_MODEL), "ev_w_out": ev_w_out4.reshape(D_MODEL, D_MODEL),
        "od_w_in": od_w_in4, "od_w_out": od_w_out4.reshape(D_MODEL, D_MODEL),
        "ev_sink": ev_sink[0], "ev_sg_ln_g": ev_sg_ln_g[0], "ev_sg_ln_b": ev_sg_ln_b[0], "ev_sg_w": ev_sg_w[0],
        "ev_sg_b": ev_sg_b[0], "od_conv_w": conv_w, "od_conv_b": conv_b, "od_w_a": od_w_a[0], "od_b_a": b_a,
        "od_w_x": od_w_x[0], "od_b_x": b_x, "od_lam": lam, "ln_g": ln_g, "ln_b": ln_b,
    }

    ada_cols = ada_w.shape[2]
    mod_sh = _modulation(c_all, ada_w, _cols(ada_b, chip * ada_cols, ada_cols).reshape(2, 1, ada_cols))
    mod_all = _allgather8(mod_sh.reshape(16, ada_cols), "gather_mod").reshape(4, 2, 2, 8, ada_cols)[:, 0]
    mod_mine = lax.dynamic_index_in_dim(mod_all, dev, axis=2, keepdims=False)
    mod = mod_mine.transpose(1, 0, 2).reshape(2, 3, D_MODEL)

    posf = positions.astype(F32).reshape(seq, 1)
    loss_local, grad_x, dmod, g = _local_step(x[0], posf, loss_target[0], mod, w_full, seq)

    pad = lambda a, n: jnp.concatenate([a.reshape(-1), jnp.zeros((n - a.size,), F32)])
    rows_small = jnp.concatenate([
        dmod.reshape(6, D_MODEL), g["ln_g"][0:1], g["ln_b"][0:1], g["ln_g"][1:2], g["ln_b"][1:2],
        jnp.concatenate([g["ev_sg_ln_g"], g["ev_sg_ln_b"]]).reshape(1, D_MODEL), g["ev_sg_b"].reshape(1, D_MODEL),
        g["od_conv_w"], g["od_conv_b"].reshape(1, D_MODEL), g["od_b_a"], g["od_b_x"], g["od_lam"],
        pad(g["ev_sink"], D_MODEL).reshape(1, D_MODEL), pad(loss_local, D_MODEL).reshape(1, D_MODEL),
        jnp.zeros((39, D_MODEL), F32)], axis=0)
    parts = [g["ev_w_in_t"].reshape(4, 2, 352, D_MODEL), g["ev_w_out"].reshape(4, 2, 128, D_MODEL),
             g["od_w_in"].reshape(4, 2, 512, 512), g["od_w_out"].reshape(4, 2, 128, D_MODEL),
             g["ev_sg_w"].reshape(4, 2, BLK, BLK), g["od_w_a"].reshape(4, 2, 2 * BLK, BLK),
             g["od_w_x"].reshape(4, 2, 2 * BLK, BLK), rows_small.reshape(4, 2, 8, D_MODEL)]
    wire = [MXU_DTYPE] * 7 + [F32]
    dmod_blk = jnp.concatenate([dmod.reshape(6, D_MODEL), jnp.zeros((2, D_MODEL), F32)], axis=0)
    *got, dmod_gathered = _reduce_sibling(parts, dmod_blk)
    chip_sums = _sum_sibling(pc.astype(jnp.int32).reshape(1), parts, got, wire)
    mine = _sum_slots(_reduce_chips(chip_sums), "sum_chips")
    reduced = _gather_reduced(mine[:4], mine[4:])
    g_ev_w_in_t = reduced[0].reshape(704, D_MODEL)
    g_ev_w_out = reduced[1].reshape(256, D_MODEL)
    g_od_w_in = reduced[2].reshape(D_MODEL, 512)
    g_od_w_out = reduced[3].reshape(256, D_MODEL)
    g_sg_w = reduced[4].reshape(8 * BLK, BLK)
    g_w_a = reduced[5].reshape(16 * BLK, BLK)
    g_w_x = reduced[6].reshape(16 * BLK, BLK)
    gs = reduced[7].reshape(64, D_MODEL)
    loss = gs[24, 0]
    dmod_all = dmod_gathered[:, 0:6].reshape(8, 2, 3 * D_MODEL)
    dmod_sh = _cols(dmod_all, chip * ada_cols, ada_cols).transpose(1, 0, 2)
    g_ada_w, d_ada_w, nm_ada_w, nv_ada_w = _ada_update(c_all, dmod_sh, ada_w, m_ada_w, v_ada_w)

    big = {}
    d_, nm_, nv_ = _adamw(tr(ev_w_in[0]), g_ev_w_in_t, tr(m_ev_w_in[0]), tr(v_ev_w_in[0]), "adamw_ev_w_in")
    big["ev_w_in"] = tuple(tr(a).reshape(ev_w_in.shape) for a in (g_ev_w_in_t, d_, nm_, nv_))
    for name, w_, g_, m_, v_ in (
            ("ev_w_out", ev_w_out, g_ev_w_out, m_ev_w_out, v_ev_w_out),
            ("od_w_in", od_w_in, g_od_w_in, m_od_w_in, v_od_w_in), ("od_w_out", od_w_out, g_od_w_out, m_od_w_out, v_od_w_out),
            ("ev_sg_w", ev_sg_w, g_sg_w, m_ev_sg_w, v_ev_sg_w), ("od_w_a", od_w_a, g_w_a, m_od_w_a, v_od_w_a),
            ("od_w_x", od_w_x, g_w_x, m_od_w_x, v_od_w_x)):
        two_d = lambda a: a.reshape(g_.shape)
        d_, nm_, nv_ = _adamw(two_d(w_), g_, two_d(m_), two_d(v_), "adamw_" + name)
        big[name] = tuple(a.reshape(w_.shape) for a in (g_, d_, nm_, nv_))
    big["ada_w"] = (g_ada_w, d_ada_w, nm_ada_w, nv_ada_w)

    sh = lambda a: _cols(a, chip * 256, 256)
    small_g = {
        "ada_b": gs[0:6].reshape(2, 3 * D_MODEL), "ln_g": jnp.stack([gs[6], gs[8]]), "ln_b": jnp.stack([gs[7], gs[9]]),
        "ev_sink": gs[23:24, 0:8], "ev_sg_ln_g": gs[10:11, 0:512], "ev_sg_ln_b": gs[10:11, 512:1024],
        "ev_sg_b": gs[11].reshape(8, BLK), "od_conv_w": sh(gs[12:16]), "od_conv_b": sh(gs[16:17]), "od_b_a": sh(gs[17:19]),
        "od_b_x": sh(gs[19:21]), "od_lam": sh(gs[21:23]),
    }
    small_in = {"ada_b": (ada_b, m_ada_b, v_ada_b), "ln_g": (ln_g, m_ln_g, v_ln_g), "ln_b": (ln_b, m_ln_b, v_ln_b),
                "ev_sink": (ev_sink, m_ev_sink, v_ev_sink), "ev_sg_ln_g": (ev_sg_ln_g, m_ev_sg_ln_g, v_ev_sg_ln_g),
                "ev_sg_ln_b": (ev_sg_ln_b, m_ev_sg_ln_b, v_ev_sg_ln_b), "ev_sg_b": (ev_sg_b, m_ev_sg_b, v_ev_sg_b),
                "od_conv_w": (od_conv_w, m_od_conv_w, v_od_conv_w), "od_conv_b": (od_conv_b, m_od_conv_b, v_od_conv_b),
                "od_b_a": (od_b_a, m_od_b_a, v_od_b_a), "od_b_x": (od_b_x, m_od_b_x, v_od_b_x),
                "od_lam": (od_lam, m_od_lam, v_od_lam)}
    names_small = list(small_g)
    upd = _adamw_small([(small_in[n][0].reshape(small_g[n].shape), small_g[n], small_in[n][1].reshape(small_g[n].shape),
                         small_in[n][2].reshape(small_g[n].shape)) for n in names_small])
    res = dict(big)
    for n, (d_, nm_, nv_) in zip(names_small, upd):
        shape = small_in[n][0].shape
        res[n] = tuple(a.reshape(shape) for a in (small_g[n], d_, nm_, nv_))

    order = ["ada_w", "ada_b", "ln_g", "ln_b", "ev_w_in", "ev_w_out", "ev_sink", "ev_sg_ln_g", "ev_sg_ln_b", "ev_sg_w", "ev_sg_b",
             "od_w_in", "od_conv_w", "od_conv_b", "od_w_a", "od_b_a", "od_w_x", "od_b_x", "od_lam", "od_w_out"]
    return (loss, grad_x.reshape(x.shape), *[res[n][0] for n in order], *[res[n][1] for n in order],
            *[res[n][2] for n in order], *[res[n][3] for n in order])
```

```python
import functools

import jax
import jax.numpy as jnp
from jax import lax
from jax.experimental import pallas as pl
from jax.experimental.pallas import tpu as pltpu

F32 = jnp.float32
BF16 = jnp.bfloat16
MXU_DTYPE = BF16

D_MODEL = 1024
HEAD_DIM = 64
N_Q_HEADS = 8
Q_PER_KV = 4
ATTN_WIDTH = 512
KV_WIDTH = 128
BLK = 128
ROPE_DIM = 16
ROPE_THETA = 500000.0
N_SG_GROUPS = 8
SG_WIDTH = 512
EVEN_IN = 2816
ODD_IN = 2048
RNN_HEADS = 8
RG_LRU_C = 8.0
ALPHA = (2 * 2) ** 0.25
LN_EPS = 1e-5
NEG_INF = -1e30
ADAM_LR, ADAM_B1, ADAM_B2, ADAM_EPS, ADAM_WD, ADAM_STEP = 0.001, 0.9, 0.999, 1e-08, 0.01, 10

LANES = 128
VMEM_LIMIT = 56 * 1024 * 1024
MESH = pl.DeviceIdType.MESH


def _mm(a, b):
    return jnp.dot(a.astype(MXU_DTYPE), b.astype(MXU_DTYPE), preferred_element_type=F32)


def _mm_nt(a, b):
    return lax.dot_general(a.astype(MXU_DTYPE), b.astype(MXU_DTYPE), (((1,), (1,)), ((), ())), preferred_element_type=F32)


def _mm_tn(a, b):
    return lax.dot_general(a.astype(MXU_DTYPE), b.astype(MXU_DTYPE), (((0,), (0,)), ((), ())), preferred_element_type=F32)


def _sigmoid(x):
    return 1.0 / (1.0 + jnp.exp(-x))


def _ln_stats(z):
    mu = jnp.mean(z, axis=-1, keepdims=True)
    d = z - mu
    var = jnp.mean(d * d, axis=-1, keepdims=True)
    rstd = lax.rsqrt(var + LN_EPS)
    return d * rstd, rstd


def _ln_bwd(dout, zhat, rstd, g):
    dzh = dout * g
    m1 = jnp.mean(dzh, axis=-1, keepdims=True)
    m2 = jnp.mean(dzh * zhat, axis=-1, keepdims=True)
    return rstd * (dzh - m1 - zhat * m2)


def _group_sum(x, e2):
    hi = x.astype(MXU_DTYPE)
    lo = (x - hi.astype(F32)).astype(MXU_DTYPE)
    return jnp.dot(hi, e2, preferred_element_type=F32) + jnp.dot(lo, e2, preferred_element_type=F32)


def _lane_iota(shape):
    return lax.broadcasted_iota(jnp.int32, shape, 1)


def _to_kv_lanes(t, h):
    src_lo = (h % 2 == 0)
    dst_lo = (h // Q_PER_KV == 0)
    if src_lo != dst_lo:
        t = pltpu.roll(t, HEAD_DIM, 1)
    lane = _lane_iota(t.shape)
    keep = (lane < HEAD_DIM) if dst_lo else (lane >= HEAD_DIM)
    return jnp.where(keep, t, 0.0)


def _from_kv_lanes(t, h):
    src_lo = (h // Q_PER_KV == 0)
    dst_lo = (h % 2 == 0)
    lane = _lane_iota(t.shape)
    keep = (lane < HEAD_DIM) if src_lo else (lane >= HEAD_DIM)
    t = jnp.where(keep, t, 0.0)
    if src_lo != dst_lo:
        t = pltpu.roll(t, HEAD_DIM, 1)
    return t


def _rope(t, cos_t, sin_p, sin_m):
    half = ROPE_DIM // 2
    return t * cos_t + pltpu.roll(t, half, 1) * sin_p + pltpu.roll(t, LANES - half, 1) * sin_m


def _rope_t(d, cos_t, sin_p, sin_m):
    half = ROPE_DIM // 2
    return d * cos_t + pltpu.roll(d * sin_p, LANES - half, 1) + pltpu.roll(d * sin_m, half, 1)


def _band(ref, n, nb):
    prev = jnp.maximum(n - 1, 0)
    nxt = jnp.minimum(n + 1, nb - 1)
    rows = [ref[pl.ds(pl.multiple_of(j * BLK, BLK), BLK), :] for j in (prev, n, nxt)]
    return jnp.concatenate(rows, axis=0)


def _band_bias(n, seq):
    qi = lax.broadcasted_iota(jnp.int32, (BLK, 3 * BLK), 0)
    kj = lax.broadcasted_iota(jnp.int32, (BLK, 3 * BLK), 1)
    k_abs = n * BLK - BLK + kj
    valid = (jnp.abs(kj - BLK - qi) <= BLK) & (k_abs >= 0) & (k_abs < seq)
    bias = jnp.where(valid, 0.0, NEG_INF)
    return jnp.concatenate([bias] * Q_PER_KV, axis=0)


def _stack_heads(tile_of, kv):
    return jnp.concatenate([_to_kv_lanes(tile_of(h // 2), h) for h in range(Q_PER_KV * kv, Q_PER_KV * (kv + 1))], axis=0)


def _per_head_column(vals):
    row = lax.broadcasted_iota(jnp.int32, (Q_PER_KV * BLK, 1), 0)
    return jnp.where(row < BLK, vals[0], jnp.where(row < 2 * BLK, vals[1], jnp.where(row < 3 * BLK, vals[2], vals[3])))


def _softplus_neg(lam):
    e = jnp.exp(-jnp.abs(lam))
    u = 1.0 + e
    log1p_e = jnp.where(u == 1.0, e, jnp.log(u) * (e / (u - 1.0)))
    sp = jnp.maximum(-lam, 0.0) + log1p_e
    dsp = -1.0 / (1.0 + jnp.exp(lam))
    return sp, dsp


def _full(shape):
    return pl.BlockSpec(shape, lambda *_: (0,) * len(shape))


def _const(shape):
    return pl.BlockSpec(shape, lambda *_: (0,) * len(shape), pipeline_mode=pl.Buffered(1))


def _rows(tm, n):
    return pl.BlockSpec((tm, n), lambda i: (i, 0))


def _params(*sem):
    return pltpu.CompilerParams(dimension_semantics=sem, vmem_limit_bytes=VMEM_LIMIT)


def _sds(shape, dtype=F32):
    return jax.ShapeDtypeStruct(shape, dtype)


def _place():
    return lax.axis_index("x"), lax.axis_index("y"), lax.axis_index("c")


class _Rider:
    def __init__(self, ins, out_shapes, n_remote, n_local, plan):
        self.ins, self.out_shapes, self.n_remote, self.n_local, self.plan = list(ins), list(out_shapes), n_remote, n_local, plan

    def scratch(self):
        return [pltpu.SemaphoreType.DMA((self.n_remote,)), pltpu.SemaphoreType.DMA((self.n_remote,)),
                pltpu.SemaphoreType.DMA((max(self.n_local, 1),))]

    def run(self, first, in_refs, out_refs, sems):
        send_sems, recv_sems, local_sems = sems
        sends, recvs, locals_ = self.plan(in_refs, out_refs)
        remote = lambda k, src, dst, to: pltpu.make_async_remote_copy(
            src_ref=src, dst_ref=dst, send_sem=send_sems.at[k], recv_sem=recv_sems.at[k], device_id=to, device_id_type=MESH)
        if first:
            for k, src, dst, to in sends:
                remote(k, src, dst, to).start()
            for j, (src, dst) in enumerate(locals_):
                pltpu.make_async_copy(src, dst, local_sems.at[j]).start()
        else:
            for k, dst, frm in recvs:
                remote(k, dst, dst, frm).wait_recv()
            for k, src, dst, to in sends:
                remote(k, src, dst, to).wait_send()
            for j, (src, dst) in enumerate(locals_):
                pltpu.make_async_copy(src, dst, local_sems.at[j]).wait()


def _other_chips(x, y):
    return [(1 - x, y), (x, 1 - y), (1 - x, 1 - y)]


def _gather_rider(shard):
    hr = shard.shape[0] // 2

    def plan(ins, outs):
        x, y, c = _place()
        mine, src, dst = 2 * x + y, ins[0], outs[0]
        sends, recvs = [], []
        for j, (px, py) in enumerate(_other_chips(x, y)):
            for flip in range(2):
                tc = c if flip == 0 else 1 - c
                sends.append((2 * j + flip, src.at[pl.ds(c * hr, hr)], dst.at[mine, pl.ds(c * hr, hr)], (px, py, tc)))
                recvs.append((2 * j + flip, dst.at[2 * px + py, pl.ds(tc * hr, hr)], (px, py, tc)))
        return sends, recvs, [(src, dst.at[mine])]

    return _Rider([shard], [_sds((4,) + shard.shape, shard.dtype)], 6, 1, plan)


def _sibling_swap_rider(parts):
    n = len(parts)

    def plan(ins, outs):
        x, y, c = _place()
        sibling = (x, y, 1 - c)
        return ([(i, ins[i].at[:, 1 - c], outs[i], sibling) for i in range(n)], [(i, outs[i], sibling) for i in range(n)], [])

    return _Rider(parts, [_sds((4,) + p.shape[2:], p.dtype) for p in parts], n, 0, plan)


def _chip_exchange_rider(parts):
    n = len(parts)

    def plan(ins, outs):
        x, y, c = _place()
        mine = 2 * x + y
        sends, recvs = [], []
        for i in range(n):
            for j, (px, py) in enumerate(_other_chips(x, y)):
                sends.append((3 * i + j, ins[i].at[2 * px + py], outs[i].at[mine], (px, py, c)))
                recvs.append((3 * i + j, outs[i].at[2 * px + py], (px, py, c)))
        return sends, recvs, [(ins[i].at[mine], outs[i].at[mine]) for i in range(n)]

    return _Rider(parts, [_sds(p.shape, p.dtype) for p in parts], 3 * n, n, plan)


def _call(body, name, grid, in_specs, out_specs, out_shape, args, sem, scratch=(), rider=None):
    if rider is None:
        return list(pl.pallas_call(body, name=name, grid=grid, in_specs=in_specs, out_specs=out_specs, out_shape=out_shape,
                                   scratch_shapes=list(scratch), compiler_params=_params(sem))(*args)), []
    n_in, n_out, n_scr = len(in_specs), len(out_specs), len(scratch)
    r_in, r_out = len(rider.ins), len(rider.out_shapes)
    steps = grid[0]

    def riding(*refs):
        ins, r_ins = refs[:n_in], refs[n_in:n_in + r_in]
        outs = refs[n_in + r_in:n_in + r_in + n_out]
        r_outs = refs[n_in + r_in + n_out:n_in + r_in + n_out + r_out]
        scr = refs[n_in + r_in + n_out + r_out:n_in + r_in + n_out + r_out + n_scr]
        sems = refs[n_in + r_in + n_out + r_out + n_scr:]

        @pl.when(pl.program_id(0) == 0)
        def _():
            rider.run(True, r_ins, r_outs, sems)

        body(*ins, *outs, *scr)

        @pl.when(pl.program_id(0) == steps - 1)
        def _():
            rider.run(False, r_ins, r_outs, sems)

    hbm = pl.BlockSpec(memory_space=pl.ANY)
    res = pl.pallas_call(
        riding, name=name, grid=grid, in_specs=list(in_specs) + [hbm] * r_in, out_specs=list(out_specs) + [hbm] * r_out,
        out_shape=list(out_shape) + rider.out_shapes, scratch_shapes=list(scratch) + rider.scratch(),
        compiler_params=_params("arbitrary"),
    )(*args, *rider.ins)
    return list(res[:n_out]), list(res[n_out:])


def _row_tile(seq, want):
    return want if seq % want == 0 else seq


def _rope_tables(posf, seq):
    half = ROPE_DIM // 2
    inv_freq = jnp.power(jnp.float32(ROPE_THETA), -jnp.arange(half, dtype=F32) / half)
    j = jnp.arange(LANES) % HEAD_DIM
    invf = jnp.where(j < ROPE_DIM, inv_freq[j % half], 0.0).astype(F32).reshape(1, LANES)
    m_p = ((j >= half) & (j < ROPE_DIM)).astype(F32).reshape(1, LANES)
    m_m = -(j < half).astype(F32).reshape(1, LANES)
    tm = _row_tile(seq, 512)

    def body(pos_ref, invf_ref, mp_ref, mm_ref, cos_ref, sp_ref, sm_ref):
        ang = pos_ref[...] * invf_ref[...]
        s = jnp.sin(ang)
        cos_ref[...] = jnp.cos(ang)
        sp_ref[...] = s * mp_ref[...]
        sm_ref[...] = s * mm_ref[...]

    return pl.pallas_call(
        body, name="rope_tables", grid=(seq // tm,),
        in_specs=[_rows(tm, 1), _full((1, LANES)), _full((1, LANES)), _full((1, LANES))],
        out_specs=[_rows(tm, LANES)] * 3, out_shape=[_sds((seq, LANES))] * 3,
        compiler_params=_params("parallel"),
    )(posf, invf, m_p, m_m)


def _even_proj(x, mod, w_in_t, tabs, seq, rider=None):
    tm = _row_tile(seq, 512)

    def body(x_ref, mod_ref, w_ref, cos_ref, sp_ref, sm_ref, h_ref, q_ref, k_ref, v_ref, su_ref, sv_ref, g_ref):
        h = x_ref[...] * (1.0 + mod_ref[1:2, :]) + mod_ref[0:1, :]
        hb = h.astype(MXU_DTYPE)
        h_ref[...] = hb
        p = _mm_nt(hb, w_ref[...])
        cos_t, sin_p, sin_m = cos_ref[...], sp_ref[...], sm_ref[...]
        for j in range(ATTN_WIDTH // LANES):
            q_ref[:, j * LANES:(j + 1) * LANES] = _rope(p[:, j * LANES:(j + 1) * LANES], cos_t, sin_p, sin_m).astype(q_ref.dtype)
        k_ref[...] = _rope(p[:, 512:640], cos_t, sin_p, sin_m).astype(k_ref.dtype)
        v_ref[...] = p[:, 640:768].astype(v_ref.dtype)
        su_ref[...] = p[:, 768:1280]
        sv_ref[...] = p[:, 1280:1792]
        g_ref[...] = p[:, 1792:2816]

    return _call(
        body, "even_proj", (seq // tm,),
        [_rows(tm, D_MODEL), _full((3, D_MODEL)), _const((EVEN_IN, D_MODEL))] + [_rows(tm, LANES)] * 3,
        [_rows(tm, D_MODEL), _rows(tm, 512), _rows(tm, LANES), _rows(tm, LANES), _rows(tm, 512), _rows(tm, 512),
         _rows(tm, D_MODEL)],
        [_sds((seq, D_MODEL), MXU_DTYPE), _sds((seq, 512), MXU_DTYPE), _sds((seq, LANES), MXU_DTYPE),
         _sds((seq, LANES), MXU_DTYPE), _sds((seq, 512)), _sds((seq, 512)), _sds((seq, D_MODEL))],
        (x, mod, w_in_t, *tabs), "parallel", rider=rider)


def _sg_forward(sv, lng, lnb, sgw_ref, sgb, e2):
    vn, vhat, rstd, svo = [], [], [], []
    for j in range(SG_WIDTH // LANES):
        t = sv[:, j * LANES:(j + 1) * LANES]
        mu = _group_sum(t, e2) * (1.0 / HEAD_DIM)
        d = t - mu
        var = _group_sum(d * d, e2) * (1.0 / HEAD_DIM)
        r = lax.rsqrt(var + LN_EPS)
        vh = d * r
        vhat.append(vh)
        rstd.append(r)
        vn.append(vh * lng[:, j * LANES:(j + 1) * LANES] + lnb[:, j * LANES:(j + 1) * LANES])
    lane = _lane_iota((BLK, LANES))
    for j in range(SG_WIDTH // LANES):
        lo = _mm(sgw_ref[2 * j], vn[j])
        hi = _mm(sgw_ref[2 * j + 1], vn[j])
        svo.append(jnp.where(lane < HEAD_DIM, lo, hi) + sgb[:, j * LANES:(j + 1) * LANES])
    return svo, vn, vhat, rstd


def _even_mix(q, k, v, su, sv, sink, sgln_g, sgln_b, sgw, sgb_full, e2, seq, rider=None):
    nb = seq // BLK

    def body(sink_ref, q_ref, k_ref, v_ref, su_ref, sv_ref, lng_ref, lnb_ref, sgw_ref, sgb_ref, e2_ref, ycat_ref, lse_ref):
        n = pl.program_id(0)
        kband = _band(k_ref, n, nb)
        vband = _band(v_ref, n, nb)
        bias = _band_bias(n, seq)
        lane = _lane_iota((BLK, LANES))
        lse = jnp.zeros((BLK, LANES), F32)
        q_tile = lambda j: q_ref[:, j * LANES:(j + 1) * LANES].astype(F32)
        acc = [jnp.zeros((BLK, LANES), F32) for _ in range(ATTN_WIDTH // LANES)]
        for kv in range(N_Q_HEADS // Q_PER_KV):
            heads = range(Q_PER_KV * kv, Q_PER_KV * (kv + 1))
            sink = _per_head_column([sink_ref[h] for h in heads])
            s = _mm_nt(_stack_heads(q_tile, kv), kband) * (HEAD_DIM ** -0.5) + bias
            m = jnp.maximum(jnp.max(s, axis=1, keepdims=True), sink)
            p = jnp.exp(s - m)
            denom = jnp.sum(p, axis=1, keepdims=True) + jnp.exp(sink - m)
            o4 = _mm(p / denom, vband)
            l4 = m + jnp.log(denom)
            for g, h in enumerate(heads):
                acc[h // 2] = acc[h // 2] + _from_kv_lanes(o4[g * BLK:(g + 1) * BLK], h)
                lse = jnp.where(lane == h, l4[g * BLK:(g + 1) * BLK], lse)
        for j in range(ATTN_WIDTH // LANES):
            ycat_ref[:, j * LANES:(j + 1) * LANES] = acc[j]
        lse_ref[...] = lse
        svo, _, _, _ = _sg_forward(sv_ref[...], lng_ref[...], lnb_ref[...], sgw_ref, sgb_ref[...], e2_ref[...])
        for j in range(SG_WIDTH // LANES):
            ycat_ref[:, ATTN_WIDTH + j * LANES:ATTN_WIDTH + (j + 1) * LANES] = su_ref[:, j * LANES:(j + 1) * LANES] * svo[j]

    blk = lambda w: pl.BlockSpec((BLK, w), lambda n: (n, 0))
    return _call(
        body, "even_mix", (nb,),
        [pl.BlockSpec(memory_space=pltpu.SMEM), blk(512), _full((seq, LANES)), _full((seq, LANES)), blk(512), blk(512),
         _full((1, 512)), _full((1, 512)), _full((8, BLK, BLK)), _full((BLK, 512)), _full((LANES, LANES))],
        [blk(D_MODEL), blk(LANES)], [_sds((seq, D_MODEL)), _sds((seq, LANES))],
        (sink, q, k, v, su, sv, sgln_g, sgln_b, sgw, sgb_full, e2), "parallel", rider=rider)


def _even_out(ycat, g, x, mod, w_out, ln_g, ln_b, seq, rider=None):
    tm = _row_tile(seq, 512)

    def body(y_ref, g_ref, x_ref, mod_ref, wo_ref, g1_ref, b1_ref, z_ref, x1_ref):
        gg = g_ref[...]
        out = _mm(y_ref[...] * (gg * _sigmoid(gg)), wo_ref[...])
        z = ALPHA * x_ref[...] + mod_ref[2:3, :] * out
        z_ref[...] = z
        zhat, _ = _ln_stats(z)
        x1_ref[...] = zhat * g1_ref[...] + b1_ref[...]

    return _call(
        body, "even_out", (seq // tm,),
        [_rows(tm, D_MODEL)] * 3 + [_full((3, D_MODEL)), _const((D_MODEL, D_MODEL)), _full((1, D_MODEL)), _full((1, D_MODEL))],
        [_rows(tm, D_MODEL)] * 2, [_sds((seq, D_MODEL))] * 2, (ycat, g, x, mod, w_out, ln_g, ln_b), "parallel", rider=rider)


def _odd_proj(x1, mod, w_in4, seq):
    tm = _row_tile(seq, 512)
    cs = ODD_IN // 4

    def body(x_ref, mod_ref, w_ref, h_ref, xr_ref, g_ref):
        h = x_ref[...] * (1.0 + mod_ref[1:2, :]) + mod_ref[0:1, :]
        hb = h.astype(MXU_DTYPE)
        h_ref[...] = hb
        for s in range(2):
            xr_ref[:, s * cs:(s + 1) * cs] = jnp.dot(hb, w_ref[s], preferred_element_type=F32)
            g_ref[:, s * cs:(s + 1) * cs] = jnp.dot(hb, w_ref[2 + s], preferred_element_type=F32)

    return pl.pallas_call(
        body, name="odd_proj", grid=(seq // tm,),
        in_specs=[_rows(tm, D_MODEL), _full((3, D_MODEL)), _full((4, D_MODEL, cs))],
        out_specs=[_rows(tm, D_MODEL)] * 3,
        out_shape=[_sds((seq, D_MODEL), MXU_DTYPE), _sds((seq, D_MODEL)), _sds((seq, D_MODEL))],
        compiler_params=_params("parallel"),
    )(x1, mod, w_in4)


def _halo_specs(tm, seq, width):
    per = tm // 8
    last = seq // 8 - 1
    return [pl.BlockSpec((8, width), lambda i: (jnp.maximum(i * per - 1, 0), 0)),
            pl.BlockSpec((tm, width), lambda i: (i, 0)),
            pl.BlockSpec((8, width), lambda i: (jnp.minimum((i + 1) * per, last), 0))]


def _extended(prev_ref, main_ref, next_ref, i, n_steps):
    prev = jnp.where(i > 0, prev_ref[...], 0.0)
    nxt = jnp.where(i < n_steps - 1, next_ref[...], 0.0)
    return jnp.concatenate([prev, main_ref[...], nxt], axis=0)


def _shifted(ext, off, tm):
    if off == 0:
        return ext[8:8 + tm]
    return pltpu.roll(ext, (-off) % ext.shape[0], 0)[8:8 + tm]


def _lru_gates(xh, pre, bias, sp, hs):
    res = []
    for d in range(2):
        r = _sigmoid(pre[:, (2 * d) * LANES:(2 * d + 1) * LANES] + bias[2 * d:2 * d + 1, hs])
        ig = _sigmoid(pre[:, (2 * d + 1) * LANES:(2 * d + 2) * LANES] + bias[2 * d + 1:2 * d + 2, hs])
        neg_log_a = RG_LRU_C * r * sp[d:d + 1, hs]
        a = jnp.exp(-neg_log_a)
        s = jnp.sqrt(jnp.tanh(neg_log_a) * (a * a + 1.0))
        res.append((r, ig, a, s))
    return res


def _odd_gates(xr, conv_w, conv_b, wcat, bias, lam, seq):
    tm = _row_tile(seq, 512)
    steps = seq // tm

    def body(xp_ref, xm_ref, xn_ref, cw_ref, cb_ref, w_ref, bias_ref, lam_ref, xc_ref, af_ref, bf_ref, ar_ref, br_ref):
        i = pl.program_id(0)
        ext = _extended(xp_ref, xm_ref, xn_ref, i, steps)
        xc = cb_ref[...] + sum(cw_ref[kk:kk + 1, :] * _shifted(ext, kk - 2, tm) for kk in range(4))
        xc_ref[...] = xc
        sp, _ = _softplus_neg(lam_ref[...])
        bias = bias_ref[...]
        for h in range(RNN_HEADS):
            hs = slice(h * LANES, (h + 1) * LANES)
            xh = xc[:, hs]
            (_, i0, a0, s0), (_, i1, a1, s1) = _lru_gates(xh, _mm(xh, w_ref[h]), bias, sp, hs)
            af_ref[:, hs] = a0
            bf_ref[:, hs] = s0 * i0 * xh
            ar_ref[:, hs] = a1
            br_ref[:, hs] = s1 * i1 * xh

    return pl.pallas_call(
        body, name="odd_gates", grid=(steps,),
        in_specs=_halo_specs(tm, seq, D_MODEL) + [_full((4, D_MODEL)), _full((1, D_MODEL)), _full((8, LANES, 512)),
                                                  _full((4, D_MODEL)), _full((2, D_MODEL))],
        out_specs=[_rows(tm, D_MODEL)] * 5, out_shape=[_sds((seq, D_MODEL))] * 5,
        compiler_params=_params("parallel"),
    )(xr, xr, xr, conv_w, conv_b, wcat, bias, lam)


def _scan(a, b, seq, descending, post, name):
    tb = _row_tile(seq, 512)
    steps = seq // tb
    imap = (lambda i: (steps - 1 - i, 0)) if descending else (lambda i: (i, 0))
    spec = pl.BlockSpec((tb, D_MODEL), imap)
    n_out = 1 if post else 2

    sub = 8
    tiles = tb // sub

    def body(a_ref, b_ref, *rest):
        outs, carry_h, carry_a = rest[:n_out], rest[n_out], rest[n_out + 1]

        @pl.when(pl.program_id(0) == 0)
        def _():
            carry_h[...] = jnp.zeros_like(carry_h)
            carry_a[...] = jnp.zeros_like(carry_a)

        row = lax.broadcasted_iota(jnp.int32, (sub, D_MODEL), 0)

        def shift(v, d, fill):
            if descending:
                return jnp.where(row <= sub - 1 - d, pltpu.roll(v, sub - d, 0), fill)
            return jnp.where(row >= d, pltpu.roll(v, d, 0), fill)

        def last(v):
            return jnp.broadcast_to(v[0:1, :] if descending else v[sub - 1:sub, :], v.shape)

        def tile(j, c):
            ch, ca = c
            r0 = pl.multiple_of(((tiles - 1 - j) if descending else j) * sub, sub)
            at = a_ref[pl.ds(r0, sub), :]
            bt = b_ref[pl.ds(r0, sub), :]
            coef = shift(at, 1, ca) if post else at
            acc_a, acc_b = coef, bt
            for d in (1, 2, 4):
                acc_b = acc_b + acc_a * shift(acc_b, d, 0.0)
                acc_a = acc_a * shift(acc_a, d, 1.0)
            h = acc_b + acc_a * ch
            outs[0][pl.ds(r0, sub), :] = h
            if post:
                return last(h), last(at)
            outs[1][pl.ds(r0, sub), :] = shift(h, 1, ch)
            return last(h), ca

        ch, ca = lax.fori_loop(0, tiles, tile, (carry_h[...], carry_a[...]), unroll=4)
        carry_h[...] = ch
        carry_a[...] = ca

    return pl.pallas_call(
        body, name=name, grid=(steps,), in_specs=[spec, spec], out_specs=[spec] * n_out,
        out_shape=[_sds((seq, D_MODEL))] * n_out, scratch_shapes=[pltpu.VMEM((sub, D_MODEL), F32)] * 2,
        compiler_params=_params("arbitrary"),
    )(a, b)


def _odd_out_and_loss(hf, hr, g, x1, tgt, mod, w_out, ln_g, ln_b, seq):
    tm = _row_tile(seq, 512)

    def body(hf_ref, hr_ref, g_ref, x_ref, t_ref, mod_ref, w_ref, lg_ref, lb_ref,
             dhs_ref, dg_ref, dres_ref, loss_ref, dw_ref, vec_ref):
        @pl.when(pl.program_id(0) == 0)
        def _():
            loss_ref[...] = jnp.zeros_like(loss_ref)
            dw_ref[...] = jnp.zeros_like(dw_ref)
            vec_ref[...] = jnp.zeros_like(vec_ref)

        gg = g_ref[...]
        sg = _sigmoid(gg)
        silu = gg * sg
        hsum = hf_ref[...] + hr_ref[...]
        y = hsum * silu
        out = _mm(y, w_ref[...])
        gate = mod_ref[2:3, :]
        z = ALPHA * x_ref[...] + gate * out
        zhat, rstd = _ln_stats(z)
        x2 = zhat * lg_ref[...] + lb_ref[...]
        err = x2 - t_ref[...]
        loss_ref[...] += 0.5 * jnp.sum(jnp.mean(err * err, axis=-1, keepdims=True))
        dx2 = err * (1.0 / D_MODEL)
        dz = _ln_bwd(dx2, zhat, rstd, lg_ref[...])
        vec_ref[0:1, :] += jnp.sum(dx2 * zhat, axis=0, keepdims=True)
        vec_ref[1:2, :] += jnp.sum(dx2, axis=0, keepdims=True)
        vec_ref[2:3, :] += jnp.sum(dz * out, axis=0, keepdims=True)
        dres_ref[...] = ALPHA * dz
        dout = gate * dz
        dw_ref[...] += _mm_tn(y, dout)
        dy = _mm_nt(dout, w_ref[...])
        dhs_ref[...] = dy * silu
        dg_ref[...] = dy * hsum * (sg * (1.0 + gg * (1.0 - sg)))

    return pl.pallas_call(
        body, name="odd_out_loss", grid=(seq // tm,),
        in_specs=[_rows(tm, D_MODEL)] * 5 + [_full((3, D_MODEL)), _const((D_MODEL, D_MODEL)),
                                             _full((1, D_MODEL)), _full((1, D_MODEL))],
        out_specs=[_rows(tm, D_MODEL)] * 3 + [_full((8, LANES)), _full((D_MODEL, D_MODEL)), _full((8, D_MODEL))],
        out_shape=[_sds((seq, D_MODEL))] * 3 + [_sds((8, LANES)), _sds((D_MODEL, D_MODEL)), _sds((8, D_MODEL))],
        compiler_params=_params("arbitrary"),
    )(hf, hr, g, x1, tgt, mod, w_out, ln_g, ln_b)


def _odd_gates_bwd(xc, gf, gr, hpf, hpr, wcat, bias, lam, seq):
    tm = _row_tile(seq, 512)
    steps = seq // tm

    def body(xc_ref, gf_ref, gr_ref, hpf_ref, hpr_ref, w_ref, bias_ref, lam_ref, dxc_ref, dw_ref, vec_ref):
        @pl.when(pl.program_id(0) == 0)
        def _():
            dw_ref[...] = jnp.zeros_like(dw_ref)
            vec_ref[...] = jnp.zeros_like(vec_ref)

        sp, dsp = _softplus_neg(lam_ref[...])
        bias = bias_ref[...]
        for h in range(RNN_HEADS):
            hs = slice(h * LANES, (h + 1) * LANES)
            xh = xc_ref[:, hs]
            gates = _lru_gates(xh, _mm(xh, w_ref[h]), bias, sp, hs)
            dxh = jnp.zeros_like(xh)
            dpre = []
            for d, (g_ref_d, hp_ref_d) in enumerate(((gf_ref, hpf_ref), (gr_ref, hpr_ref))):
                r, ig, a, s = gates[d]
                db = g_ref_d[:, hs]
                da = db * hp_ref_d[:, hs]
                dxh = dxh + db * s * ig
                dlog_a = da * a - (db * ig * xh) * (a * a / s)
                dr = dlog_a * (-RG_LRU_C) * sp[d:d + 1, hs]
                di = db * s * xh
                dpr = dr * r * (1.0 - r)
                dpi = di * ig * (1.0 - ig)
                vec_ref[2 * d:2 * d + 1, hs] += jnp.sum(dpr, axis=0, keepdims=True)
                vec_ref[2 * d + 1:2 * d + 2, hs] += jnp.sum(dpi, axis=0, keepdims=True)
                vec_ref[4 + d:5 + d, hs] += jnp.sum(dlog_a * r, axis=0, keepdims=True) * (-RG_LRU_C) * dsp[d:d + 1, hs]
                dpre += [dpr, dpi]
            dcat = jnp.concatenate(dpre, axis=1)
            dw_ref[h] += _mm_tn(xh, dcat)
            dxc_ref[:, hs] = dxh + _mm_nt(dcat, w_ref[h])

    return pl.pallas_call(
        body, name="odd_gates_bwd", grid=(steps,),
        in_specs=[_rows(tm, D_MODEL)] * 5 + [_full((8, LANES, 512)), _full((4, D_MODEL)), _full((2, D_MODEL))],
        out_specs=[_rows(tm, D_MODEL), _full((8, LANES, 512)), _full((8, D_MODEL))],
        out_shape=[_sds((seq, D_MODEL)), _sds((8, LANES, 512)), _sds((8, D_MODEL))],
        compiler_params=_params("arbitrary"),
    )(xc, gf, gr, hpf, hpr, wcat, bias, lam)


def _odd_proj_bwd(dxc, xr, dg, x1, dres, mod, conv_w, w_in4, seq):
    tm = _row_tile(seq, 512)
    steps = seq // tm

    def body(dp_ref, dm_ref, dn_ref, xp_ref, xm_ref, xn_ref, dg_ref, x_ref, dres_ref, mod_ref, cw_ref, w_ref,
             dx_ref, dpb_ref, vec_ref):
        i = pl.program_id(0)

        @pl.when(i == 0)
        def _():
            vec_ref[...] = jnp.zeros_like(vec_ref)

        dext = _extended(dp_ref, dm_ref, dn_ref, i, steps)
        xext = _extended(xp_ref, xm_ref, xn_ref, i, steps)
        dxc_m = dm_ref[...]
        dxr = sum(cw_ref[kk:kk + 1, :] * _shifted(dext, 2 - kk, tm) for kk in range(4))
        for kk in range(4):
            vec_ref[kk:kk + 1, :] += jnp.sum(dxc_m * _shifted(xext, kk - 2, tm), axis=0, keepdims=True)
        vec_ref[4:5, :] += jnp.sum(dxc_m, axis=0, keepdims=True)
        dpb_ref[:, :D_MODEL] = dxr.astype(dpb_ref.dtype)
        dpb_ref[:, D_MODEL:] = dg_ref[...].astype(dpb_ref.dtype)
        cs = ODD_IN // 4
        dh = sum(_mm_nt(dpb_ref[:, s * cs:(s + 1) * cs], w_ref[s]) for s in range(4))
        x = x_ref[...]
        vec_ref[5:6, :] += jnp.sum(dh, axis=0, keepdims=True)
        vec_ref[6:7, :] += jnp.sum(dh * x, axis=0, keepdims=True)
        dx_ref[...] = dres_ref[...] + dh * (1.0 + mod_ref[1:2, :])

    return pl.pallas_call(
        body, name="odd_proj_bwd", grid=(steps,),
        in_specs=_halo_specs(tm, seq, D_MODEL) + _halo_specs(tm, seq, D_MODEL) + [_rows(tm, D_MODEL)] * 3
        + [_full((3, D_MODEL)), _full((4, D_MODEL)), _const((4, D_MODEL, ODD_IN // 4))],
        out_specs=[_rows(tm, D_MODEL), _rows(tm, ODD_IN), _full((8, D_MODEL))],
        out_shape=[_sds((seq, D_MODEL)), _sds((seq, ODD_IN), MXU_DTYPE), _sds((8, D_MODEL))],
        compiler_params=_params("arbitrary"),
    )(dxc, dxc, dxc, xr, xr, xr, dg, x1, dres, mod, conv_w, w_in4)


def _tn_matmul(a, b, seq, name, transposed):
    n = b.shape[1]
    tn = n // 2
    cs = n // 4
    tm = _row_tile(seq, 512)
    steps = seq // tm

    def body(a_ref, b_ref, o_ref, acc_ref):
        i = pl.program_id(1)

        @pl.when(i == 0)
        def _():
            acc_ref[...] = jnp.zeros_like(acc_ref)

        acc_ref[...] += lax.dot_general(a_ref[...], b_ref[...], (((0,), (0,)), ((), ())), preferred_element_type=F32)

        @pl.when(i == steps - 1)
        def _():
            if transposed:
                o_ref[...] = acc_ref[...].T
            else:
                o_ref[0] = acc_ref[:, 0:cs]
                o_ref[1] = acc_ref[:, cs:2 * cs]

    if transposed:
        out_spec, out_shape = pl.BlockSpec((tn, D_MODEL), lambda j, i: (j, 0)), _sds((n, D_MODEL))
    else:
        out_spec, out_shape = pl.BlockSpec((2, D_MODEL, cs), lambda j, i: (j, 0, 0)), _sds((4, D_MODEL, cs))
    return pl.pallas_call(
        body, name=name, grid=(2, steps),
        in_specs=[pl.BlockSpec((tm, D_MODEL), lambda j, i: (i, 0)), pl.BlockSpec((tm, tn), lambda j, i: (i, j))],
        out_specs=out_spec, out_shape=out_shape,
        scratch_shapes=[pltpu.VMEM((D_MODEL, tn), F32)], compiler_params=_params("parallel", "arbitrary"),
    )(a, b)


def _even_out_bwd(dx1, z, ycat, g, mod, ln_g, w_out, seq, rider=None):
    tm = _row_tile(seq, 512)
    steps = seq // tm

    def body(dx_ref, z_ref, y_ref, g_ref, mod_ref, lg_ref, w_ref, dy_ref, dg_ref, dres_ref, dw_ref, vec_ref):
        i = pl.program_id(0)

        @pl.when(i == 0)
        def _():
            dw_ref[...] = jnp.zeros_like(dw_ref)
            vec_ref[...] = jnp.zeros_like(vec_ref)

        zhat, rstd = _ln_stats(z_ref[...])
        dx1_ = dx_ref[...]
        dz = _ln_bwd(dx1_, zhat, rstd, lg_ref[...])
        vec_ref[0:1, :] += jnp.sum(dx1_ * zhat, axis=0, keepdims=True)
        vec_ref[1:2, :] += jnp.sum(dx1_, axis=0, keepdims=True)
        dres_ref[...] = ALPHA * dz
        gate = mod_ref[2:3, :]
        gg = g_ref[...]
        sg = _sigmoid(gg)
        silu = gg * sg
        ycat_ = y_ref[...]
        dw_ref[...] += _mm_tn(ycat_ * silu, dz)
        dy = _mm_nt(gate * dz, w_ref[...])
        dy_ref[...] = dy * silu
        dg_ref[...] = dy * ycat_ * (sg * (1.0 + gg * (1.0 - sg)))

        @pl.when(i == steps - 1)
        def _():
            m_acc = dw_ref[...]
            vec_ref[2:3, :] = jnp.sum(w_ref[...].astype(F32) * m_acc, axis=0, keepdims=True)
            dw_ref[...] = m_acc * gate

    return _call(
        body, "even_out_bwd", (steps,),
        [_rows(tm, D_MODEL)] * 4 + [_full((3, D_MODEL)), _full((1, D_MODEL)), _const((D_MODEL, D_MODEL))],
        [_rows(tm, D_MODEL)] * 3 + [_full((D_MODEL, D_MODEL)), _full((8, D_MODEL))],
        [_sds((seq, D_MODEL))] * 3 + [_sds((D_MODEL, D_MODEL)), _sds((8, D_MODEL))],
        (dx1, z, ycat, g, mod, ln_g, w_out), "arbitrary", rider=rider)


def _even_mix_bwd(q, k, v, lse, ycat, dycat, su, sv, sink, sgln_g, sgln_b, sgw, sgb_full, e2, e8, seq, rider=None):
    nb = seq // BLK

    def body(sink_ref, q_ref, k_ref, v_ref, lse_ref, y_ref, dy_ref, su_ref, sv_ref, lng_ref, lnb_ref, sgw_ref, sgb_ref, e2_ref,
             e8_ref, dq_ref, dsu_ref, dsv_ref, dk_ref, dv_ref, dsgw_ref, dsgb_ref, vec_ref, dsink_ref, dsgb_acc):
        n = pl.program_id(0)

        @pl.when(n == 0)
        def _():
            dk_ref[...] = jnp.zeros_like(dk_ref)
            dv_ref[...] = jnp.zeros_like(dv_ref)
            dsgw_ref[...] = jnp.zeros_like(dsgw_ref)
            dsgb_acc[...] = jnp.zeros_like(dsgb_acc)
            vec_ref[...] = jnp.zeros_like(vec_ref)
            dsink_ref[...] = jnp.zeros_like(dsink_ref)

        kband = _band(k_ref, n, nb)
        vband = _band(v_ref, n, nb)
        bias = _band_bias(n, seq)
        lane = _lane_iota((BLK, LANES))
        row8 = lax.broadcasted_iota(jnp.int32, (8, LANES), 0)
        lse = lse_ref[...]
        dkb = jnp.zeros((LANES, 3 * BLK), F32)
        dvb = jnp.zeros((LANES, 3 * BLK), F32)
        dsink = jnp.zeros((8, LANES), F32)
        q_tile = lambda j: q_ref[:, j * LANES:(j + 1) * LANES].astype(F32)
        do_tile = lambda j: dy_ref[:, j * LANES:(j + 1) * LANES]
        dq = [jnp.zeros((BLK, LANES), F32) for _ in range(ATTN_WIDTH // LANES)]
        for kv in range(N_Q_HEADS // Q_PER_KV):
            heads = range(Q_PER_KV * kv, Q_PER_KV * (kv + 1))
            lse4, delta4 = [], []
            for h in heads:
                head_lanes = (lane < HEAD_DIM) if h % 2 == 0 else (lane >= HEAD_DIM)
                lse4.append(jnp.sum(jnp.where(lane == h, lse, 0.0), axis=1, keepdims=True))
                delta4.append(jnp.sum(jnp.where(head_lanes, do_tile(h // 2) * y_ref[:, (h // 2) * LANES:(h // 2 + 1) * LANES], 0.0),
                                      axis=1, keepdims=True))
            lse4, delta4 = jnp.concatenate(lse4, axis=0), jnp.concatenate(delta4, axis=0)
            q4, do4 = _stack_heads(q_tile, kv), _stack_heads(do_tile, kv)
            s = _mm_nt(q4, kband) * (HEAD_DIM ** -0.5) + bias
            p = jnp.exp(s - lse4)
            wsink = jnp.exp(_per_head_column([sink_ref[h] for h in heads]) - lse4) * delta4
            ds = p * (_mm_nt(do4, vband) - delta4) * (HEAD_DIM ** -0.5)
            dq4 = _mm(ds, kband)
            dkb = dkb + _mm_tn(q4, ds)
            dvb = dvb + _mm_tn(do4, p)
            for g, h in enumerate(heads):
                dq[h // 2] = dq[h // 2] + _from_kv_lanes(dq4[g * BLK:(g + 1) * BLK], h)
                dsink = dsink + jnp.where(row8 == h, -jnp.sum(wsink[g * BLK:(g + 1) * BLK]), 0.0)
        for j in range(ATTN_WIDTH // LANES):
            dq_ref[:, j * LANES:(j + 1) * LANES] = dq[j]
        dsink_ref[...] += dsink
        prev = jnp.maximum(n - 1, 0)
        nxt = jnp.minimum(n + 1, nb - 1)
        for part, blk_i in enumerate((prev, n, nxt)):
            rows = pl.ds(pl.multiple_of(blk_i * BLK, BLK), BLK)
            dk_ref[rows, :] += dkb[:, part * BLK:(part + 1) * BLK].T
            dv_ref[rows, :] += dvb[:, part * BLK:(part + 1) * BLK].T

        e2 = e2_ref[...]
        lng = lng_ref[...]
        svo, vn, vhat, rstd = _sg_forward(sv_ref[...], lng, lnb_ref[...], sgw_ref, sgb_ref[...], e2)
        for j in range(SG_WIDTH // LANES):
            cs = slice(j * LANES, (j + 1) * LANES)
            dysg = dy_ref[:, ATTN_WIDTH + j * LANES:ATTN_WIDTH + (j + 1) * LANES]
            dsu_ref[:, cs] = dysg * svo[j]
            dsvo = dysg * su_ref[:, cs]
            dsgb_acc[:, cs] += dsvo
            d_lo = jnp.where(lane < HEAD_DIM, dsvo, 0.0)
            d_hi = dsvo - d_lo
            dsgw_ref[2 * j] += _mm_nt(d_lo, vn[j])
            dsgw_ref[2 * j + 1] += _mm_nt(d_hi, vn[j])
            dvn = _mm_tn(sgw_ref[2 * j], d_lo) + _mm_tn(sgw_ref[2 * j + 1], d_hi)
            vec_ref[0:1, cs] += jnp.sum(dvn * vhat[j], axis=0, keepdims=True)
            vec_ref[1:2, cs] += jnp.sum(dvn, axis=0, keepdims=True)
            dvh = dvn * lng[:, cs]
            m1 = _group_sum(dvh, e2) * (1.0 / HEAD_DIM)
            m2 = _group_sum(dvh * vhat[j], e2) * (1.0 / HEAD_DIM)
            dsv_ref[:, cs] = rstd[j] * (dvh - m1 - vhat[j] * m2)

        @pl.when(n == nb - 1)
        def _():
            rest = dsgb_acc[...]
            total = jnp.zeros((8, BLK), F32)
            for _ in range(3):
                part = rest.astype(MXU_DTYPE)
                total = total + lax.dot_general(e8_ref[...], part, (((1,), (1,)), ((), ())), preferred_element_type=F32)
                rest = rest - part.astype(F32)
            dsgb_ref[...] = total

    blk = lambda w: pl.BlockSpec((BLK, w), lambda n: (n, 0))
    return _call(
        body, "even_mix_bwd", (nb,),
        [pl.BlockSpec(memory_space=pltpu.SMEM), blk(512), _full((seq, LANES)), _full((seq, LANES)), blk(LANES),
         blk(D_MODEL), blk(D_MODEL), blk(512), blk(512), _full((1, 512)), _full((1, 512)), _full((8, BLK, BLK)),
         _full((BLK, 512)), _full((LANES, LANES)), _full((8, 512))],
        [blk(512), blk(512), blk(512), _full((seq, LANES)), _full((seq, LANES)), _full((8, BLK, BLK)),
         _full((8, BLK)), _full((8, 512)), _full((8, LANES))],
        [_sds((seq, 512)), _sds((seq, 512)), _sds((seq, 512)), _sds((seq, LANES)), _sds((seq, LANES)),
         _sds((8, BLK, BLK)), _sds((8, BLK)), _sds((8, 512)), _sds((8, LANES))],
        (sink, q, k, v, lse, ycat, dycat, su, sv, sgln_g, sgln_b, sgw, sgb_full, e2, e8), "arbitrary",
        scratch=[pltpu.VMEM((BLK, 512), F32)], rider=rider)


def _even_proj_bwd(dq, dk, dv, dsu, dsv, dg, x, dres, mod, tabs, w_in_t, seq):
    tm = _row_tile(seq, 512)

    def body(dq_ref, dk_ref, dv_ref, dsu_ref, dsv_ref, dg_ref, x_ref, dres_ref, mod_ref, cos_ref, sp_ref, sm_ref, wt_ref,
             dx_ref, dpb_ref, vec_ref):
        @pl.when(pl.program_id(0) == 0)
        def _():
            vec_ref[...] = jnp.zeros_like(vec_ref)

        cos_t, sin_p, sin_m = cos_ref[...], sp_ref[...], sm_ref[...]
        dt = dpb_ref.dtype
        for j in range(ATTN_WIDTH // LANES):
            cs = slice(j * LANES, (j + 1) * LANES)
            dpb_ref[:, cs] = _rope_t(dq_ref[:, cs], cos_t, sin_p, sin_m).astype(dt)
        dpb_ref[:, 512:640] = _rope_t(dk_ref[...], cos_t, sin_p, sin_m).astype(dt)
        dpb_ref[:, 640:768] = dv_ref[...].astype(dt)
        dpb_ref[:, 768:1280] = dsu_ref[...].astype(dt)
        dpb_ref[:, 1280:1792] = dsv_ref[...].astype(dt)
        dpb_ref[:, 1792:2816] = dg_ref[...].astype(dt)
        dh = jnp.dot(dpb_ref[...], wt_ref[...], preferred_element_type=F32)
        x_ = x_ref[...]
        vec_ref[0:1, :] += jnp.sum(dh, axis=0, keepdims=True)
        vec_ref[1:2, :] += jnp.sum(dh * x_, axis=0, keepdims=True)
        dx_ref[...] = dres_ref[...] + dh * (1.0 + mod_ref[1:2, :])

    return pl.pallas_call(
        body, name="even_proj_bwd", grid=(seq // tm,),
        in_specs=[_rows(tm, 512), _rows(tm, LANES), _rows(tm, LANES), _rows(tm, 512), _rows(tm, 512), _rows(tm, D_MODEL),
                  _rows(tm, D_MODEL), _rows(tm, D_MODEL), _full((3, D_MODEL))] + [_rows(tm, LANES)] * 3
        + [_const((EVEN_IN, D_MODEL))],
        out_specs=[_rows(tm, D_MODEL), _rows(tm, EVEN_IN), _full((8, D_MODEL))],
        out_shape=[_sds((seq, D_MODEL)), _sds((seq, EVEN_IN), MXU_DTYPE), _sds((8, D_MODEL))],
        compiler_params=_params("arbitrary"),
    )(dq, dk, dv, dsu, dsv, dg, x, dres, mod, *tabs, w_in_t)


def _local_step(x, posf, tgt, mod, w, seq, ride=None):
    rid = lambda make, *a: None if ride is None else make(*a)
    mxu = lambda a: a.astype(MXU_DTYPE)
    row = lambda a: a.reshape(1, -1)
    tabs = _rope_tables(posf, seq)
    e2 = mxu(jnp.kron(jnp.eye(2, dtype=F32), jnp.ones((HEAD_DIM, HEAD_DIM), F32)))
    e8 = mxu(jnp.repeat(jnp.eye(N_SG_GROUPS, dtype=F32), HEAD_DIM, axis=1))
    sgw = mxu(w["ev_sg_w"])
    sgb_full = jnp.repeat(w["ev_sg_b"].T, HEAD_DIM, axis=1)
    sgln_g, sgln_b = row(w["ev_sg_ln_g"]), row(w["ev_sg_ln_b"])
    sink = w["ev_sink"].reshape(N_Q_HEADS)
    ev_w_in_t = mxu(w["ev_w_in_t"])
    if ride is None:
        ev_w_out, od_w_in, od_w_out = mxu(w["ev_w_out"]), mxu(w["od_w_in"]), mxu(w["od_w_out"])
    wcat = mxu(jnp.concatenate([w["od_w_a"][0], w["od_w_x"][0], w["od_w_a"][1], w["od_w_x"][1]], axis=2))
    gate_bias = jnp.stack([w["od_b_a"][0], w["od_b_x"][0], w["od_b_a"][1], w["od_b_x"][1]])
    conv_b = row(w["od_conv_b"])
    ln_g, ln_b = w["ln_g"], w["ln_b"]

    (h0, q, k, v, su, sv, g0), got = _even_proj(x, mod[0], ev_w_in_t, tabs, seq, rid(_gather_rider, ride and ride["ev_w_out"]))
    if ride is not None:
        ev_w_out = got[0].reshape(D_MODEL, D_MODEL)
    (ycat, lse), got = _even_mix(q, k, v, su, sv, sink, sgln_g, sgln_b, sgw, sgb_full, e2, seq,
                                 rid(_gather_rider, ride and ride["od_w_in"]))
    if ride is not None:
        od_w_in = got[0]
    (z0, x1), got = _even_out(ycat, g0, x, mod[0], ev_w_out, ln_g[0:1], ln_b[0:1], seq, rid(_gather_rider, ride and ride["od_w_out"]))
    if ride is not None:
        od_w_out = got[0].reshape(D_MODEL, D_MODEL)
    h1, xr, g1 = _odd_proj(x1, mod[1], od_w_in, seq)
    xc, a_f, b_f, a_r, b_r = _odd_gates(xr, w["od_conv_w"], conv_b, wcat, gate_bias, w["od_lam"], seq)
    hf, hpf = _scan(a_f, b_f, seq, descending=False, post=False, name="scan_fwd")
    hr, hpr = _scan(a_r, b_r, seq, descending=True, post=False, name="scan_rev")
    dhs, dg1, dres1, loss, d_od_w_out, vec_o = _odd_out_and_loss(hf, hr, g1, x1, tgt, mod[1], od_w_out, ln_g[1:2], ln_b[1:2], seq)
    (gf,) = _scan(a_f, dhs, seq, descending=True, post=True, name="scan_fwd_bwd")
    (gr,) = _scan(a_r, dhs, seq, descending=False, post=True, name="scan_rev_bwd")
    dxc, d_wcat, vec_g = _odd_gates_bwd(xc, gf, gr, hpf, hpr, wcat, gate_bias, w["od_lam"], seq)
    dx1, dp1, vec_p = _odd_proj_bwd(dxc, xr, dg1, x1, dres1, mod[1], w["od_conv_w"], od_w_in, seq)
    d_od_w_in = _tn_matmul(h1, dp1, seq, "odd_dw_in", transposed=False)
    d_od_w_a = jnp.stack([d_wcat[:, :, 0:128], d_wcat[:, :, 256:384]])
    d_od_w_x = jnp.stack([d_wcat[:, :, 128:256], d_wcat[:, :, 384:512]])
    od_parts = [d_od_w_in.reshape(4, 2, 512, 512), d_od_w_out.reshape(4, 2, 128, D_MODEL),
                d_od_w_a.reshape(4, 2, 2 * BLK, BLK), d_od_w_x.reshape(4, 2, 2 * BLK, BLK)]
    (dycat, dg0, dres0, d_ev_w_out, vec_e), got_od = _even_out_bwd(dx1, z0, ycat, g0, mod[0], ln_g[0:1], ev_w_out, seq,
                                                                   rid(_sibling_swap_rider, od_parts))
    if ride is not None:
        od_sums = _sum_sibling(ride["core"], od_parts, got_od, [ride["wire"]] * 4, "sum_sibling_od")
    (dq, dsu, dsv, dk, dv, d_sgw, d_sgb, vec_s, d_sink), od_slots = _even_mix_bwd(
        q, k, v, lse, ycat, dycat, su, sv, sink, sgln_g, sgln_b, sgw, sgb_full, e2, e8, seq,
        rid(_chip_exchange_rider, ride and od_sums))
    grad_x, dp0, vec_x = _even_proj_bwd(dq, dk, dv, dsu, dsv, dg0, x, dres0, mod[0], tabs, ev_w_in_t, seq)
    d_ev_w_in_t = _tn_matmul(h0, dp0, seq, "even_dw_in", transposed=True)

    dmod = jnp.stack([jnp.stack([vec_x[0], vec_x[1], vec_e[2]]), jnp.stack([vec_p[5], vec_p[6], vec_o[2]])])
    grads = {
        "ln_g": jnp.stack([vec_e[0], vec_o[0]]), "ln_b": jnp.stack([vec_e[1], vec_o[1]]),
        "ev_w_in_t": d_ev_w_in_t, "ev_w_out": d_ev_w_out, "ev_sink": d_sink[:, 0],
        "ev_sg_ln_g": vec_s[0], "ev_sg_ln_b": vec_s[1], "ev_sg_w": d_sgw,
        "ev_sg_b": d_sgb,
        "od_conv_w": vec_p[0:4], "od_conv_b": vec_p[4],
        "od_b_a": jnp.stack([vec_g[0], vec_g[2]]), "od_b_x": jnp.stack([vec_g[1], vec_g[3]]), "od_lam": vec_g[4:6],
    }
    if ride is None:
        grads.update({"od_w_in": d_od_w_in, "od_w_out": d_od_w_out, "od_w_a": d_od_w_a, "od_w_x": d_od_w_x})
    else:
        grads["od_slots"] = od_slots
    return loss[0, 0], grad_x, dmod, grads


def _allgather8(block, name):
    m_per, n = block.shape

    def body(x_ref, out_ref, send_sems, recv_sems, local_sem):
        x, y, c = _place()
        me, sibling = (x, y, c), (x, y, 1 - c)
        chips = [(1 - x, y), (x, 1 - y), (1 - x, 1 - y)]

        def rows(px, py, pc):
            return out_ref.at[pl.ds((4 * px + 2 * py + pc) * m_per, m_per), :]

        def copy(k, blk, to, src=None):
            return pltpu.make_async_remote_copy(src_ref=rows(*blk) if src is None else src, dst_ref=rows(*blk),
                                                send_sem=send_sems.at[k], recv_sem=recv_sems.at[k], device_id=to,
                                                device_id_type=MESH)

        mine = pltpu.make_async_copy(x_ref, rows(*me), local_sem)
        mine.start()
        first = [copy(0, me, sibling, src=x_ref)] + [copy(1 + j, me, (*chip, c), src=x_ref) for j, chip in enumerate(chips)]
        for cp in first:
            cp.start()
        passed = [copy(4 + j, (*chip, c), sibling) for j, chip in enumerate(chips)]
        for j, chip in enumerate(chips):
            copy(1 + j, (*chip, c), me).wait_recv()
            passed[j].start()
        copy(0, sibling, me).wait_recv()
        for j, chip in enumerate(chips):
            copy(4 + j, (*chip, 1 - c), me).wait_recv()
        for cp in first + passed:
            cp.wait_send()
        mine.wait()

    return pl.pallas_call(
        body, name=name, out_shape=_sds((8 * m_per, n), block.dtype),
        in_specs=[pl.BlockSpec(memory_space=pltpu.VMEM)], out_specs=pl.BlockSpec(memory_space=pltpu.VMEM),
        scratch_shapes=[pltpu.SemaphoreType.DMA((7,)), pltpu.SemaphoreType.DMA((7,)), pltpu.SemaphoreType.DMA],
        compiler_params=pltpu.CompilerParams(vmem_limit_bytes=VMEM_LIMIT),
    )(block)


class _Copies:
    def __init__(self, send_sems, recv_sems, local_sems, stages):
        self.send_sems, self.recv_sems, self.local_sems, self.stages = send_sems, recv_sems, local_sems, stages
        self.sent, self.staged, self.locals = [], [], []

    def remote(self, k, src, dst, to):
        return pltpu.make_async_remote_copy(src_ref=src, dst_ref=dst, send_sem=self.send_sems.at[k], recv_sem=self.recv_sems.at[k],
                                            device_id=to, device_id_type=MESH)

    def send(self, k, src, dst, to):
        cp = self.remote(k, src, dst, to)
        cp.start()
        self.sent.append(cp)

    def arrived(self, k, dst, frm):
        self.remote(k, dst, dst, frm).wait_recv()

    def local(self, src, dst):
        k = len(self.staged)
        cp = pltpu.make_async_copy(src, self.stages[k], self.local_sems.at[2 * k])
        cp.start()
        self.staged.append((cp, dst))

    def flush(self):
        for k in range(len(self.locals), len(self.staged)):
            cp, dst = self.staged[k]
            cp.wait()
            out = pltpu.make_async_copy(self.stages[k], dst, self.local_sems.at[2 * k + 1])
            out.start()
            self.locals.append(out)

    def drain(self):
        self.flush()
        for cp in self.sent:
            cp.wait_send()
        for cp in self.locals:
            cp.wait()


def _comm_call(body, name, ins, out_shapes, n_remote, stages):
    n_in, n_out = len(ins), len(out_shapes)

    def kern(*refs):
        in_refs, out_refs = refs[:n_in], refs[n_in:n_in + n_out]
        send_sems, recv_sems, local_sems = refs[n_in + n_out:n_in + n_out + 3]
        body(_Copies(send_sems, recv_sems, local_sems, refs[n_in + n_out + 3:]), in_refs, out_refs)

    hbm = pl.BlockSpec(memory_space=pl.ANY)
    return pl.pallas_call(
        kern, name=name, out_shape=out_shapes, in_specs=[hbm] * n_in, out_specs=[hbm] * n_out,
        scratch_shapes=[pltpu.SemaphoreType.DMA((n_remote,)), pltpu.SemaphoreType.DMA((n_remote,)),
                        pltpu.SemaphoreType.DMA((2 * len(stages),))] + [pltpu.VMEM(s, d) for s, d in stages],
        compiler_params=pltpu.CompilerParams(vmem_limit_bytes=VMEM_LIMIT),
    )(*ins)


def _gather_to_all(cps, pairs, me, sibling, other_chips, c, base):
    idx = lambda p: 4 * p[0] + 2 * p[1] + p[2]
    for i, (src, dst) in enumerate(pairs):
        cps.local(src, dst.at[idx(me)])
        cps.send(base + 7 * i, src, dst.at[idx(me)], sibling)
        for j, chip in enumerate(other_chips):
            cps.send(base + 7 * i + 1 + j, src, dst.at[idx(me)], (*chip, c))
    cps.flush()
    for j, chip in enumerate(other_chips):
        for i, (_, dst) in enumerate(pairs):
            got = dst.at[idx((*chip, c))]
            cps.arrived(base + 7 * i + 1 + j, got, (*chip, c))
            cps.send(base + 7 * i + 4 + j, got, got, sibling)
    for i, (_, dst) in enumerate(pairs):
        cps.arrived(base + 7 * i, dst.at[idx(sibling)], sibling)
        for j, chip in enumerate(other_chips):
            cps.arrived(base + 7 * i + 4 + j, dst.at[idx((*chip, 1 - c))], sibling)


def _gather_weights(shards, small):
    n = len(shards)

    def body(cps, ins, outs):
        x, y, c = _place()
        me, sibling, mine = (x, y, c), (x, y, 1 - c), 2 * x + y
        chips = [(1 - x, y), (x, 1 - y), (1 - x, 1 - y)]
        for i in range(n):
            cps.local(ins[i], outs[i].at[mine])
        for j, (px, py) in enumerate(chips):
            for i in range(n):
                hr = shards[i].shape[0] // 2
                rows = pl.ds(c * hr, hr)
                cps.send(6 * i + j, ins[i].at[rows], outs[i].at[mine, rows], (px, py, c))
        _gather_to_all(cps, [(ins[n], outs[n])], me, sibling, chips, c, 6 * n)
        for j, (px, py) in enumerate(chips):
            for i in range(n):
                hr = shards[i].shape[0] // 2
                got = outs[i].at[2 * px + py, pl.ds(c * hr, hr)]
                cps.arrived(6 * i + j, got, (px, py, c))
                cps.send(6 * i + 3 + j, got, got, sibling)
        for j, (px, py) in enumerate(chips):
            for i in range(n):
                hr = shards[i].shape[0] // 2
                cps.arrived(6 * i + 3 + j, outs[i].at[2 * px + py, pl.ds((1 - c) * hr, hr)], sibling)
        cps.drain()

    return _comm_call(body, "gather_weights", list(shards) + [small],
                      [_sds((4,) + s.shape, s.dtype) for s in shards] + [_sds((8,) + small.shape, small.dtype)], 6 * n + 7,
                      [(a.shape, a.dtype) for a in list(shards) + [small]])


def _reduce_sibling(parts, dmod_rows):
    n = len(parts)

    def body(cps, ins, outs):
        x, y, c = _place()
        me, sibling = (x, y, c), (x, y, 1 - c)
        chips = [(1 - x, y), (x, 1 - y), (1 - x, 1 - y)]
        for i in range(n):
            cps.send(i, ins[i].at[:, 1 - c], outs[i], sibling)
        _gather_to_all(cps, [(ins[n], outs[n])], me, sibling, chips, c, n)
        for i in range(n):
            cps.arrived(i, outs[i], sibling)
        cps.drain()

    return _comm_call(body, "reduce_sibling", list(parts) + [dmod_rows],
                      [_sds((4,) + p.shape[2:], p.dtype) for p in parts] + [_sds((8,) + dmod_rows.shape, dmod_rows.dtype)], n + 7,
                      [(dmod_rows.shape, dmod_rows.dtype)])


def _reduce_chips(parts):
    n = len(parts)

    def body(cps, ins, outs):
        x, y, c = _place()
        mine = 2 * x + y
        chips = [(1 - x, y), (x, 1 - y), (1 - x, 1 - y)]
        for i in range(n):
            cps.local(ins[i].at[mine], outs[i].at[mine])
        for j, (px, py) in enumerate(chips):
            for i in range(n):
                cps.send(3 * i + j, ins[i].at[2 * px + py], outs[i].at[mine], (px, py, c))
        cps.flush()
        for j, (px, py) in enumerate(chips):
            for i in range(n):
                cps.arrived(3 * i + j, outs[i].at[2 * px + py], (px, py, c))
        cps.drain()

    return _comm_call(body, "reduce_chips", list(parts), [_sds(p.shape, p.dtype) for p in parts], 3 * n,
                      [(p.shape[1:], p.dtype) for p in parts])


def _gather_reduced(shard_parts, repl_parts):
    ns, nr = len(shard_parts), len(repl_parts)

    def body(cps, ins, outs):
        x, y, c = _place()
        me, sibling = (x, y, c), (x, y, 1 - c)
        chips = [(1 - x, y), (x, 1 - y), (1 - x, 1 - y)]
        for i in range(ns):
            cps.local(ins[i], outs[i].at[c])
            cps.send(i, ins[i], outs[i].at[c], sibling)
        _gather_to_all(cps, [(ins[ns + i], outs[ns + i]) for i in range(nr)], me, sibling, chips, c, ns)
        for i in range(ns):
            cps.arrived(i, outs[i].at[1 - c], sibling)
        cps.drain()

    return _comm_call(body, "gather_reduced", list(shard_parts) + list(repl_parts),
                      [_sds((2,) + p.shape, p.dtype) for p in shard_parts] + [_sds((8,) + p.shape, p.dtype) for p in repl_parts],
                      ns + 7 * nr, [(p.shape, p.dtype) for p in list(shard_parts) + list(repl_parts)])


def _sum_sibling(core, parts, got, wire, name):
    n = len(parts)

    def body(core_ref, *refs):
        for i in range(n):
            refs[2 * n + i][0] = (refs[i][0] + refs[n + i][0]).astype(wire[i])

    keep_spec = lambda p: pl.BlockSpec((1, None) + p.shape[2:], lambda s, core_ref: (s, core_ref[0], 0, 0))
    slot_spec = lambda p: pl.BlockSpec((1,) + p.shape[2:], lambda s, core_ref: (s, 0, 0))
    return pl.pallas_call(
        body, name=name,
        grid_spec=pltpu.PrefetchScalarGridSpec(
            num_scalar_prefetch=1, grid=(4,), in_specs=[keep_spec(p) for p in parts] + [slot_spec(p) for p in parts],
            out_specs=[slot_spec(p) for p in parts]),
        out_shape=[_sds((4,) + p.shape[2:], wire[i]) for i, p in enumerate(parts)],
        compiler_params=_params("parallel"),
    )(core, *parts, *got)


def _sum_slots(slots, name):
    n = len(slots)

    def spec_pair(p):
        k, rows, cols = p.shape
        sub = 16 if p.dtype == BF16 else 8
        if (rows // 2) % sub == 0:
            return pl.BlockSpec((k, rows // 2, cols), lambda i: (0, i, 0)), pl.BlockSpec((rows // 2, cols), lambda i: (i, 0))
        return pl.BlockSpec((k, rows, cols), lambda i: (0, 0, 0)), pl.BlockSpec((rows, cols), lambda i: (0, 0))

    pairs = [spec_pair(p) for p in slots]

    def body(*refs):
        for i in range(n):
            acc = refs[i][0].astype(F32)
            for j in range(1, slots[i].shape[0]):
                acc = acc + refs[i][j].astype(F32)
            refs[n + i][...] = acc

    return pl.pallas_call(
        body, name=name, grid=(2,), in_specs=[a for a, _ in pairs], out_specs=[b for _, b in pairs],
        out_shape=[_sds(p.shape[1:]) for p in slots], compiler_params=_params("arbitrary"),
    )(*slots)


def _modulation(c_all, ada_w, ada_b):
    cols = ada_w.shape[2]

    def body(c_ref, w_ref, b_ref, o_ref):
        cc = c_ref[...]
        o_ref[0] = _mm(cc * _sigmoid(cc), w_ref[0]) + b_ref[0]

    return pl.pallas_call(
        body, name="modulation", grid=(2,),
        in_specs=[_full((8, D_MODEL)), pl.BlockSpec((1, D_MODEL, cols), lambda l: (l, 0, 0)), pl.BlockSpec((1, 1, cols), lambda l: (l, 0, 0))],
        out_specs=pl.BlockSpec((1, 8, cols), lambda l: (l, 0, 0)), out_shape=_sds((2, 8, cols)),
        compiler_params=_params("parallel"),
    )(c_all, ada_w, ada_b)


def _adamw_math(w, g, m, v):
    m = ADAM_B1 * m + (1.0 - ADAM_B1) * g
    v = ADAM_B2 * v + (1.0 - ADAM_B2) * (g * g)
    m_hat = m / (1.0 - ADAM_B1 ** ADAM_STEP)
    v_hat = v / (1.0 - ADAM_B2 ** ADAM_STEP)
    delta = -ADAM_LR * (m_hat / (jnp.sqrt(v_hat) + ADAM_EPS) + ADAM_WD * w)
    return delta, m, v


def _ada_update(c_all, dmod, w, m, v):
    cols = w.shape[2]
    tr = 256
    spec3 = pl.BlockSpec((1, tr, cols), lambda l, i: (l, i, 0))

    def body(c_ref, d_ref, w_ref, m_ref, v_ref, g_ref, dl_ref, nm_ref, nv_ref):
        cc = c_ref[...]
        g = _mm_tn(cc * _sigmoid(cc), d_ref[0])
        g_ref[0] = g
        dl_ref[0], nm_ref[0], nv_ref[0] = _adamw_math(w_ref[0], g, m_ref[0], v_ref[0])

    return pl.pallas_call(
        body, name="ada_update", grid=(2, D_MODEL // tr),
        in_specs=[pl.BlockSpec((8, tr), lambda l, i: (0, i)), pl.BlockSpec((1, 8, cols), lambda l, i: (l, 0, 0)), spec3, spec3, spec3],
        out_specs=[spec3] * 4, out_shape=[_sds(w.shape)] * 4, compiler_params=_params("parallel", "parallel"),
    )(c_all, dmod, w, m, v)


def _adamw(w, g, m, v, name):
    rows, n = w.shape
    tr = next(t for t in (256, 128, 64, 32, 16, 8, rows) if rows % t == 0)

    def body(w_ref, g_ref, m_ref, v_ref, dl_ref, nm_ref, nv_ref):
        dl_ref[...], nm_ref[...], nv_ref[...] = _adamw_math(w_ref[...], g_ref[...], m_ref[...], v_ref[...])

    return pl.pallas_call(body, name=name, grid=(rows // tr,), in_specs=[_rows(tr, n)] * 4, out_specs=[_rows(tr, n)] * 3,
                          out_shape=[_sds((rows, n))] * 3, compiler_params=_params("parallel"))(w, g, m, v)


def _adamw_small(params):
    n = len(params)

    def body(*refs):
        ins, outs = refs[:4 * n], refs[4 * n:]
        for j in range(n):
            w_ref, g_ref, m_ref, v_ref = ins[4 * j:4 * j + 4]
            outs[3 * j][...], outs[3 * j + 1][...], outs[3 * j + 2][...] = _adamw_math(w_ref[...], g_ref[...], m_ref[...], v_ref[...])

    flat = [a for p in params for a in p]
    res = pl.pallas_call(body, name="adamw_small", out_shape=[_sds(p[0].shape) for p in params for _ in range(3)])(*flat)
    return [tuple(res[3 * j:3 * j + 3]) for j in range(n)]


def _cols(a, start, size):
    return lax.dynamic_slice_in_dim(a, start, size, axis=a.ndim - 1)


def kernel(x, c, positions, ada_w, ada_b, ln_g, ln_b, ev_w_in, ev_w_out, ev_sink, ev_sg_ln_g, ev_sg_ln_b, ev_sg_w, ev_sg_b, od_w_in, od_conv_w, od_conv_b, od_w_a, od_b_a, od_w_x, od_b_x, od_lam, od_w_out, loss_target, m_ada_w, m_ada_b, m_ln_g, m_ln_b, m_ev_w_in, m_ev_w_out, m_ev_sink, m_ev_sg_ln_g, m_ev_sg_ln_b, m_ev_sg_w, m_ev_sg_b, m_od_w_in, m_od_conv_w, m_od_conv_b, m_od_w_a, m_od_b_a, m_od_w_x, m_od_b_x, m_od_lam, m_od_w_out, v_ada_w, v_ada_b, v_ln_g, v_ln_b, v_ev_w_in, v_ev_w_out, v_ev_sink, v_ev_sg_ln_g, v_ev_sg_ln_b, v_ev_sg_w, v_ev_sg_b, v_od_w_in, v_od_conv_w, v_od_conv_b, v_od_w_a, v_od_b_a, v_od_w_x, v_od_b_x, v_od_lam, v_od_w_out):
    seq = x.shape[1]
    px, py, pc = _place()
    chip = 2 * px + py
    dev = 2 * chip + pc

    small = jnp.concatenate([od_conv_w[0].reshape(-1), od_conv_b[0], od_b_a[0].reshape(-1), jnp.zeros((256,), F32),
                             od_b_x[0].reshape(-1), od_lam[0].reshape(-1)]).reshape(3, D_MODEL)
    blk = jnp.concatenate([c, small, jnp.zeros((4, D_MODEL), F32)], axis=0)
    tr = lambda a: jnp.swapaxes(a, -1, -2)
    wire_w = lambda a: a.astype(MXU_DTYPE)
    ev_w_in4, g_small = _gather_weights([wire_w(tr(ev_w_in[0]))], blk)
    core = pc.astype(jnp.int32).reshape(1)
    ride = {"ev_w_out": wire_w(ev_w_out[0]), "od_w_in": wire_w(od_w_in[0]), "od_w_out": wire_w(od_w_out[0]),
            "core": core, "wire": MXU_DTYPE}
    c_all = g_small[:, 0, :]
    per_chip = g_small[0::2]
    conv_w = per_chip[:, 1].reshape(4, 4, 256).transpose(1, 0, 2).reshape(4, D_MODEL)
    conv_b = per_chip[:, 2, 0:256].reshape(D_MODEL)
    b_a = per_chip[:, 2, 256:768].reshape(4, 2, 256).transpose(1, 0, 2).reshape(2, D_MODEL)
    b_x = per_chip[:, 3, 0:512].reshape(4, 2, 256).transpose(1, 0, 2).reshape(2, D_MODEL)
    lam = per_chip[:, 3, 512:1024].reshape(4, 2, 256).transpose(1, 0, 2).reshape(2, D_MODEL)

    w_full = {
        "ev_w_in_t": ev_w_in4.reshape(EVEN_IN, D_MODEL),
        "ev_sink": ev_sink[0], "ev_sg_ln_g": ev_sg_ln_g[0], "ev_sg_ln_b": ev_sg_ln_b[0], "ev_sg_w": ev_sg_w[0],
        "ev_sg_b": ev_sg_b[0], "od_conv_w": conv_w, "od_conv_b": conv_b, "od_w_a": od_w_a[0], "od_b_a": b_a,
        "od_w_x": od_w_x[0], "od_b_x": b_x, "od_lam": lam, "ln_g": ln_g, "ln_b": ln_b,
    }

    ada_cols = ada_w.shape[2]
    mod_sh = _modulation(c_all, ada_w, _cols(ada_b, chip * ada_cols, ada_cols).reshape(2, 1, ada_cols))
    mod_all = _allgather8(mod_sh.reshape(16, ada_cols), "gather_mod").reshape(4, 2, 2, 8, ada_cols)[:, 0]
    mod_mine = lax.dynamic_index_in_dim(mod_all, dev, axis=2, keepdims=False)
    mod = mod_mine.transpose(1, 0, 2).reshape(2, 3, D_MODEL)

    posf = positions.astype(F32).reshape(seq, 1)
    loss_local, grad_x, dmod, g = _local_step(x[0], posf, loss_target[0], mod, w_full, seq, ride)

    pad = lambda a, n: jnp.concatenate([a.reshape(-1), jnp.zeros((n - a.size,), F32)])
    rows_small = jnp.concatenate([
        dmod.reshape(6, D_MODEL), g["ln_g"][0:1], g["ln_b"][0:1], g["ln_g"][1:2], g["ln_b"][1:2],
        jnp.concatenate([g["ev_sg_ln_g"], g["ev_sg_ln_b"]]).reshape(1, D_MODEL), g["ev_sg_b"].reshape(1, D_MODEL),
        g["od_conv_w"], g["od_conv_b"].reshape(1, D_MODEL), g["od_b_a"], g["od_b_x"], g["od_lam"],
        pad(g["ev_sink"], D_MODEL).reshape(1, D_MODEL), pad(loss_local, D_MODEL).reshape(1, D_MODEL),
        jnp.zeros((39, D_MODEL), F32)], axis=0)
    parts = [g["ev_w_in_t"].reshape(4, 2, 352, D_MODEL), g["ev_w_out"].reshape(4, 2, 128, D_MODEL),
             g["ev_sg_w"].reshape(4, 2, BLK, BLK), rows_small.reshape(4, 2, 8, D_MODEL)]
    wire = [MXU_DTYPE] * 3 + [F32]
    dmod_blk = jnp.concatenate([dmod.reshape(6, D_MODEL), jnp.zeros((2, D_MODEL), F32)], axis=0)
    *got, dmod_gathered = _reduce_sibling(parts, dmod_blk)
    ev_slots = list(_reduce_chips(_sum_sibling(core, parts, got, wire, "sum_sibling")))
    od_slots = list(g["od_slots"])
    mine = _sum_slots(ev_slots[0:2] + od_slots[0:2] + ev_slots[2:3] + od_slots[2:4] + ev_slots[3:4], "sum_chips")
    reduced = _gather_reduced(mine[:4], mine[4:])
    g_ev_w_in_t = reduced[0].reshape(704, D_MODEL)
    g_ev_w_out = reduced[1].reshape(256, D_MODEL)
    g_od_w_in = reduced[2].reshape(D_MODEL, 512)
    g_od_w_out = reduced[3].reshape(256, D_MODEL)
    g_sg_w = reduced[4].reshape(8 * BLK, BLK)
    g_w_a = reduced[5].reshape(16 * BLK, BLK)
    g_w_x = reduced[6].reshape(16 * BLK, BLK)
    gs = reduced[7].reshape(64, D_MODEL)
    loss = gs[24, 0]
    dmod_all = dmod_gathered[:, 0:6].reshape(8, 2, 3 * D_MODEL)
    dmod_sh = _cols(dmod_all, chip * ada_cols, ada_cols).transpose(1, 0, 2)
    g_ada_w, d_ada_w, nm_ada_w, nv_ada_w = _ada_update(c_all, dmod_sh, ada_w, m_ada_w, v_ada_w)

    big = {}
    d_, nm_, nv_ = _adamw(tr(ev_w_in[0]), g_ev_w_in_t, tr(m_ev_w_in[0]), tr(v_ev_w_in[0]), "adamw_ev_w_in")
    big["ev_w_in"] = tuple(tr(a).reshape(ev_w_in.shape) for a in (g_ev_w_in_t, d_, nm_, nv_))
    for name, w_, g_, m_, v_ in (
            ("ev_w_out", ev_w_out, g_ev_w_out, m_ev_w_out, v_ev_w_out),
            ("od_w_in", od_w_in, g_od_w_in, m_od_w_in, v_od_w_in), ("od_w_out", od_w_out, g_od_w_out, m_od_w_out, v_od_w_out),
            ("ev_sg_w", ev_sg_w, g_sg_w, m_ev_sg_w, v_ev_sg_w), ("od_w_a", od_w_a, g_w_a, m_od_w_a, v_od_w_a),
            ("od_w_x", od_w_x, g_w_x, m_od_w_x, v_od_w_x)):
        two_d = lambda a: a.reshape(g_.shape)
        d_, nm_, nv_ = _adamw(two_d(w_), g_, two_d(m_), two_d(v_), "adamw_" + name)
        big[name] = tuple(a.reshape(w_.shape) for a in (g_, d_, nm_, nv_))
    big["ada_w"] = (g_ada_w, d_ada_w, nm_ada_w, nv_ada_w)

    sh = lambda a: _cols(a, chip * 256, 256)
    small_g = {
        "ada_b": gs[0:6].reshape(2, 3 * D_MODEL), "ln_g": jnp.stack([gs[6], gs[8]]), "ln_b": jnp.stack([gs[7], gs[9]]),
        "ev_sink": gs[23:24, 0:8], "ev_sg_ln_g": gs[10:11, 0:512], "ev_sg_ln_b": gs[10:11, 512:1024],
        "ev_sg_b": gs[11].reshape(8, BLK), "od_conv_w": sh(gs[12:16]), "od_conv_b": sh(gs[16:17]), "od_b_a": sh(gs[17:19]),
        "od_b_x": sh(gs[19:21]), "od_lam": sh(gs[21:23]),
    }
    small_in = {"ada_b": (ada_b, m_ada_b, v_ada_b), "ln_g": (ln_g, m_ln_g, v_ln_g), "ln_b": (ln_b, m_ln_b, v_ln_b),
                "ev_sink": (ev_sink, m_ev_sink, v_ev_sink), "ev_sg_ln_g": (ev_sg_ln_g, m_ev_sg_ln_g, v_ev_sg_ln_g),
                "ev_sg_ln_b": (ev_sg_ln_b, m_ev_sg_ln_b, v_ev_sg_ln_b), "ev_sg_b": (ev_sg_b, m_ev_sg_b, v_ev_sg_b),
                "od_conv_w": (od_conv_w, m_od_conv_w, v_od_conv_w), "od_conv_b": (od_conv_b, m_od_conv_b, v_od_conv_b),
                "od_b_a": (od_b_a, m_od_b_a, v_od_b_a), "od_b_x": (od_b_x, m_od_b_x, v_od_b_x),
                "od_lam": (od_lam, m_od_lam, v_od_lam)}
    names_small = list(small_g)
    upd = _adamw_small([(small_in[n][0].reshape(small_g[n].shape), small_g[n], small_in[n][1].reshape(small_g[n].shape),
                         small_in[n][2].reshape(small_g[n].shape)) for n in names_small])
    res = dict(big)
    for n, (d_, nm_, nv_) in zip(names_small, upd):
        shape = small_in[n][0].shape
        res[n] = tuple(a.reshape(shape) for a in (small_g[n], d_, nm_, nv_))

    order = ["ada_w", "ada_b", "ln_g", "ln_b", "ev_w_in", "ev_w_out", "ev_sink", "ev_sg_ln_g", "ev_sg_ln_b", "ev_sg_w", "ev_sg_b",
             "od_w_in", "od_conv_w", "od_conv_b", "od_w_a", "od_b_a", "od_w_x", "od_b_x", "od_lam", "od_w_out"]
    return (loss, grad_x.reshape(x.shape), *[res[n][0] for n in order], *[res[n][1] for n in order],
            *[res[n][2] for n in order], *[res[n][3] for n in order])
```

```python
import functools

import jax
import jax.numpy as jnp
from jax import lax
from jax.experimental import pallas as pl
from jax.experimental.pallas import tpu as pltpu

F32 = jnp.float32
BF16 = jnp.bfloat16
MXU_DTYPE = BF16
ACT_DTYPE = MXU_DTYPE

D_MODEL = 1024
HEAD_DIM = 64
N_Q_HEADS = 8
Q_PER_KV = 4
ATTN_WIDTH = 512
KV_WIDTH = 128
BLK = 128
ROPE_DIM = 16
ROPE_THETA = 500000.0
N_SG_GROUPS = 8
SG_WIDTH = 512
EVEN_IN = 2816
ODD_IN = 2048
RNN_HEADS = 8
RG_LRU_C = 8.0
ALPHA = (2 * 2) ** 0.25
LN_EPS = 1e-5
NEG_INF = -1e30
ADAM_LR, ADAM_B1, ADAM_B2, ADAM_EPS, ADAM_WD, ADAM_STEP = 0.001, 0.9, 0.999, 1e-08, 0.01, 10

LANES = 128
VMEM_LIMIT = 56 * 1024 * 1024
MESH = pl.DeviceIdType.MESH


def _mm(a, b):
    return jnp.dot(a.astype(MXU_DTYPE), b.astype(MXU_DTYPE), preferred_element_type=F32)


def _mm_nt(a, b):
    return lax.dot_general(a.astype(MXU_DTYPE), b.astype(MXU_DTYPE), (((1,), (1,)), ((), ())), preferred_element_type=F32)


def _mm_tn(a, b):
    return lax.dot_general(a.astype(MXU_DTYPE), b.astype(MXU_DTYPE), (((0,), (0,)), ((), ())), preferred_element_type=F32)


def _sigmoid(x):
    return 1.0 / (1.0 + jnp.exp(-x))


def _ln_stats(z):
    mu = jnp.mean(z, axis=-1, keepdims=True)
    d = z - mu
    var = jnp.mean(d * d, axis=-1, keepdims=True)
    rstd = lax.rsqrt(var + LN_EPS)
    return d * rstd, rstd


def _ln_bwd(dout, zhat, rstd, g):
    dzh = dout * g
    m1 = jnp.mean(dzh, axis=-1, keepdims=True)
    m2 = jnp.mean(dzh * zhat, axis=-1, keepdims=True)
    return rstd * (dzh - m1 - zhat * m2)


def _group_sum(x, e2):
    hi = x.astype(MXU_DTYPE)
    lo = (x - hi.astype(F32)).astype(MXU_DTYPE)
    return jnp.dot(hi, e2, preferred_element_type=F32) + jnp.dot(lo, e2, preferred_element_type=F32)


def _lane_iota(shape):
    return lax.broadcasted_iota(jnp.int32, shape, 1)


def _to_kv_lanes(t, h):
    src_lo = (h % 2 == 0)
    dst_lo = (h // Q_PER_KV == 0)
    if src_lo != dst_lo:
        t = pltpu.roll(t, HEAD_DIM, 1)
    lane = _lane_iota(t.shape)
    keep = (lane < HEAD_DIM) if dst_lo else (lane >= HEAD_DIM)
    return jnp.where(keep, t, 0.0)


def _from_kv_lanes(t, h):
    src_lo = (h // Q_PER_KV == 0)
    dst_lo = (h % 2 == 0)
    lane = _lane_iota(t.shape)
    keep = (lane < HEAD_DIM) if src_lo else (lane >= HEAD_DIM)
    t = jnp.where(keep, t, 0.0)
    if src_lo != dst_lo:
        t = pltpu.roll(t, HEAD_DIM, 1)
    return t


def _rope(t, cos_t, sin_p, sin_m):
    half = ROPE_DIM // 2
    return t * cos_t + pltpu.roll(t, half, 1) * sin_p + pltpu.roll(t, LANES - half, 1) * sin_m


def _rope_t(d, cos_t, sin_p, sin_m):
    half = ROPE_DIM // 2
    return d * cos_t + pltpu.roll(d * sin_p, LANES - half, 1) + pltpu.roll(d * sin_m, half, 1)


def _band(ref, n, nb):
    prev = jnp.maximum(n - 1, 0)
    nxt = jnp.minimum(n + 1, nb - 1)
    rows = [ref[pl.ds(pl.multiple_of(j * BLK, BLK), BLK), :] for j in (prev, n, nxt)]
    return jnp.concatenate(rows, axis=0)


def _band_bias(n, seq):
    qi = lax.broadcasted_iota(jnp.int32, (BLK, 3 * BLK), 0)
    kj = lax.broadcasted_iota(jnp.int32, (BLK, 3 * BLK), 1)
    k_abs = n * BLK - BLK + kj
    valid = (jnp.abs(kj - BLK - qi) <= BLK) & (k_abs >= 0) & (k_abs < seq)
    bias = jnp.where(valid, 0.0, NEG_INF)
    return jnp.concatenate([bias] * Q_PER_KV, axis=0)


def _stack_heads(tile_of, kv):
    return jnp.concatenate([_to_kv_lanes(tile_of(h // 2), h) for h in range(Q_PER_KV * kv, Q_PER_KV * (kv + 1))], axis=0)


def _per_head_column(vals):
    row = lax.broadcasted_iota(jnp.int32, (Q_PER_KV * BLK, 1), 0)
    return jnp.where(row < BLK, vals[0], jnp.where(row < 2 * BLK, vals[1], jnp.where(row < 3 * BLK, vals[2], vals[3])))


def _softplus_neg(lam):
    e = jnp.exp(-jnp.abs(lam))
    u = 1.0 + e
    log1p_e = jnp.where(u == 1.0, e, jnp.log(u) * (e / (u - 1.0)))
    sp = jnp.maximum(-lam, 0.0) + log1p_e
    dsp = -1.0 / (1.0 + jnp.exp(lam))
    return sp, dsp


def _full(shape):
    return pl.BlockSpec(shape, lambda *_: (0,) * len(shape))


def _const(shape):
    return pl.BlockSpec(shape, lambda *_: (0,) * len(shape), pipeline_mode=pl.Buffered(1))


def _rows(tm, n):
    return pl.BlockSpec((tm, n), lambda i: (i, 0))


def _params(*sem):
    return pltpu.CompilerParams(dimension_semantics=sem, vmem_limit_bytes=VMEM_LIMIT)


def _sds(shape, dtype=F32):
    return jax.ShapeDtypeStruct(shape, dtype)


def _place():
    return lax.axis_index("x"), lax.axis_index("y"), lax.axis_index("c")


class _Rider:
    def __init__(self, ins, out_shapes, n_remote, n_local, plan):
        self.ins, self.out_shapes, self.n_remote, self.n_local, self.plan = list(ins), list(out_shapes), n_remote, n_local, plan

    def scratch(self):
        return [pltpu.SemaphoreType.DMA((self.n_remote,)), pltpu.SemaphoreType.DMA((self.n_remote,)),
                pltpu.SemaphoreType.DMA((max(self.n_local, 1),))]

    def run(self, first, in_refs, out_refs, sems):
        send_sems, recv_sems, local_sems = sems
        sends, recvs, locals_ = self.plan(in_refs, out_refs)
        remote = lambda k, src, dst, to: pltpu.make_async_remote_copy(
            src_ref=src, dst_ref=dst, send_sem=send_sems.at[k], recv_sem=recv_sems.at[k], device_id=to, device_id_type=MESH)
        if first:
            for k, src, dst, to in sends:
                remote(k, src, dst, to).start()
            for j, (src, dst) in enumerate(locals_):
                pltpu.make_async_copy(src, dst, local_sems.at[j]).start()
        else:
            for k, dst, frm in recvs:
                remote(k, dst, dst, frm).wait_recv()
            for k, src, dst, to in sends:
                remote(k, src, dst, to).wait_send()
            for j, (src, dst) in enumerate(locals_):
                pltpu.make_async_copy(src, dst, local_sems.at[j]).wait()


def _other_chips(x, y):
    return [(1 - x, y), (x, 1 - y), (1 - x, 1 - y)]


def _gather_rider(shard):
    hr = shard.shape[0] // 2

    def plan(ins, outs):
        x, y, c = _place()
        mine, src, dst = 2 * x + y, ins[0], outs[0]
        sends, recvs = [], []
        for j, (px, py) in enumerate(_other_chips(x, y)):
            for flip in range(2):
                tc = c if flip == 0 else 1 - c
                sends.append((2 * j + flip, src.at[pl.ds(c * hr, hr)], dst.at[mine, pl.ds(c * hr, hr)], (px, py, tc)))
                recvs.append((2 * j + flip, dst.at[2 * px + py, pl.ds(tc * hr, hr)], (px, py, tc)))
        return sends, recvs, [(src, dst.at[mine])]

    return _Rider([shard], [_sds((4,) + shard.shape, shard.dtype)], 6, 1, plan)


def _sibling_swap_rider(parts):
    n = len(parts)

    def plan(ins, outs):
        x, y, c = _place()
        sibling = (x, y, 1 - c)
        return ([(i, ins[i].at[:, 1 - c], outs[i], sibling) for i in range(n)], [(i, outs[i], sibling) for i in range(n)], [])

    return _Rider(parts, [_sds((4,) + p.shape[2:], p.dtype) for p in parts], n, 0, plan)


def _chip_exchange_rider(parts):
    n = len(parts)

    def plan(ins, outs):
        x, y, c = _place()
        mine = 2 * x + y
        sends, recvs = [], []
        for i in range(n):
            for j, (px, py) in enumerate(_other_chips(x, y)):
                sends.append((3 * i + j, ins[i].at[2 * px + py], outs[i].at[mine], (px, py, c)))
                recvs.append((3 * i + j, outs[i].at[2 * px + py], (px, py, c)))
        return sends, recvs, [(ins[i].at[mine], outs[i].at[mine]) for i in range(n)]

    return _Rider(parts, [_sds(p.shape, p.dtype) for p in parts], 3 * n, n, plan)


def _call(body, name, grid, in_specs, out_specs, out_shape, args, sem, scratch=(), rider=None):
    if rider is None:
        return list(pl.pallas_call(body, name=name, grid=grid, in_specs=in_specs, out_specs=out_specs, out_shape=out_shape,
                                   scratch_shapes=list(scratch), compiler_params=_params(sem))(*args)), []
    n_in, n_out, n_scr = len(in_specs), len(out_specs), len(scratch)
    r_in, r_out = len(rider.ins), len(rider.out_shapes)
    steps = grid[0]

    def riding(*refs):
        ins, r_ins = refs[:n_in], refs[n_in:n_in + r_in]
        outs = refs[n_in + r_in:n_in + r_in + n_out]
        r_outs = refs[n_in + r_in + n_out:n_in + r_in + n_out + r_out]
        scr = refs[n_in + r_in + n_out + r_out:n_in + r_in + n_out + r_out + n_scr]
        sems = refs[n_in + r_in + n_out + r_out + n_scr:]

        @pl.when(pl.program_id(0) == 0)
        def _():
            rider.run(True, r_ins, r_outs, sems)

        body(*ins, *outs, *scr)

        @pl.when(pl.program_id(0) == steps - 1)
        def _():
            rider.run(False, r_ins, r_outs, sems)

    hbm = pl.BlockSpec(memory_space=pl.ANY)
    res = pl.pallas_call(
        riding, name=name, grid=grid, in_specs=list(in_specs) + [hbm] * r_in, out_specs=list(out_specs) + [hbm] * r_out,
        out_shape=list(out_shape) + rider.out_shapes, scratch_shapes=list(scratch) + rider.scratch(),
        compiler_params=_params("arbitrary"),
    )(*args, *rider.ins)
    return list(res[:n_out]), list(res[n_out:])


def _row_tile(seq, want):
    return want if seq % want == 0 else seq


def _rope_tables(posf, seq):
    half = ROPE_DIM // 2
    inv_freq = jnp.power(jnp.float32(ROPE_THETA), -jnp.arange(half, dtype=F32) / half)
    j = jnp.arange(LANES) % HEAD_DIM
    invf = jnp.where(j < ROPE_DIM, inv_freq[j % half], 0.0).astype(F32).reshape(1, LANES)
    m_p = ((j >= half) & (j < ROPE_DIM)).astype(F32).reshape(1, LANES)
    m_m = -(j < half).astype(F32).reshape(1, LANES)
    tm = _row_tile(seq, 512)

    def body(pos_ref, invf_ref, mp_ref, mm_ref, cos_ref, sp_ref, sm_ref):
        ang = pos_ref[...] * invf_ref[...]
        s = jnp.sin(ang)
        cos_ref[...] = jnp.cos(ang)
        sp_ref[...] = s * mp_ref[...]
        sm_ref[...] = s * mm_ref[...]

    return pl.pallas_call(
        body, name="rope_tables", grid=(seq // tm,),
        in_specs=[_rows(tm, 1), _full((1, LANES)), _full((1, LANES)), _full((1, LANES))],
        out_specs=[_rows(tm, LANES)] * 3, out_shape=[_sds((seq, LANES))] * 3,
        compiler_params=_params("parallel"),
    )(posf, invf, m_p, m_m)


def _even_proj(x, mod, w_in_t, tabs, seq, rider=None):
    tm = _row_tile(seq, 512)

    def body(x_ref, mod_ref, w_ref, cos_ref, sp_ref, sm_ref, h_ref, q_ref, k_ref, v_ref, su_ref, sv_ref, g_ref):
        h = x_ref[...] * (1.0 + mod_ref[1:2, :]) + mod_ref[0:1, :]
        hb = h.astype(MXU_DTYPE)
        h_ref[...] = hb
        p = _mm_nt(hb, w_ref[...])
        cos_t, sin_p, sin_m = cos_ref[...], sp_ref[...], sm_ref[...]
        for j in range(ATTN_WIDTH // LANES):
            q_ref[:, j * LANES:(j + 1) * LANES] = _rope(p[:, j * LANES:(j + 1) * LANES], cos_t, sin_p, sin_m).astype(q_ref.dtype)
        k_ref[...] = _rope(p[:, 512:640], cos_t, sin_p, sin_m).astype(k_ref.dtype)
        v_ref[...] = p[:, 640:768].astype(v_ref.dtype)
        su_ref[...] = p[:, 768:1280].astype(su_ref.dtype)
        sv_ref[...] = p[:, 1280:1792].astype(sv_ref.dtype)
        g_ref[...] = p[:, 1792:2816].astype(g_ref.dtype)

    return _call(
        body, "even_proj", (seq // tm,),
        [_rows(tm, D_MODEL), _full((3, D_MODEL)), _const((EVEN_IN, D_MODEL))] + [_rows(tm, LANES)] * 3,
        [_rows(tm, D_MODEL), _rows(tm, 512), _rows(tm, LANES), _rows(tm, LANES), _rows(tm, 512), _rows(tm, 512),
         _rows(tm, D_MODEL)],
        [_sds((seq, D_MODEL), MXU_DTYPE), _sds((seq, 512), MXU_DTYPE), _sds((seq, LANES), MXU_DTYPE),
         _sds((seq, LANES), MXU_DTYPE), _sds((seq, 512), ACT_DTYPE), _sds((seq, 512), ACT_DTYPE), _sds((seq, D_MODEL), ACT_DTYPE)],
        (x, mod, w_in_t, *tabs), "parallel", rider=rider)


def _sg_forward(sv, lng, lnb, sgw_ref, sgb, e2):
    vn, vhat, rstd, svo = [], [], [], []
    for j in range(SG_WIDTH // LANES):
        t = sv[:, j * LANES:(j + 1) * LANES]
        mu = _group_sum(t, e2) * (1.0 / HEAD_DIM)
        d = t - mu
        var = _group_sum(d * d, e2) * (1.0 / HEAD_DIM)
        r = lax.rsqrt(var + LN_EPS)
        vh = d * r
        vhat.append(vh)
        rstd.append(r)
        vn.append(vh * lng[:, j * LANES:(j + 1) * LANES] + lnb[:, j * LANES:(j + 1) * LANES])
    lane = _lane_iota((BLK, LANES))
    for j in range(SG_WIDTH // LANES):
        lo = _mm(sgw_ref[2 * j], vn[j])
        hi = _mm(sgw_ref[2 * j + 1], vn[j])
        svo.append(jnp.where(lane < HEAD_DIM, lo, hi) + sgb[:, j * LANES:(j + 1) * LANES])
    return svo, vn, vhat, rstd


def _even_mix(q, k, v, su, sv, sink, sgln_g, sgln_b, sgw, sgb_full, e2, seq, rider=None):
    nb = seq // BLK

    def body(sink_ref, q_ref, k_ref, v_ref, su_ref, sv_ref, lng_ref, lnb_ref, sgw_ref, sgb_ref, e2_ref, ycat_ref, lse_ref):
        n = pl.program_id(0)
        kband = _band(k_ref, n, nb)
        vband = _band(v_ref, n, nb)
        bias = _band_bias(n, seq)
        lane = _lane_iota((BLK, LANES))
        lse = jnp.zeros((BLK, LANES), F32)
        q_tile = lambda j: q_ref[:, j * LANES:(j + 1) * LANES].astype(F32)
        acc = [jnp.zeros((BLK, LANES), F32) for _ in range(ATTN_WIDTH // LANES)]
        for kv in range(N_Q_HEADS // Q_PER_KV):
            heads = range(Q_PER_KV * kv, Q_PER_KV * (kv + 1))
            sink = _per_head_column([sink_ref[h] for h in heads])
            s = _mm_nt(_stack_heads(q_tile, kv), kband) * (HEAD_DIM ** -0.5) + bias
            m = jnp.maximum(jnp.max(s, axis=1, keepdims=True), sink)
            p = jnp.exp(s - m)
            denom = jnp.sum(p, axis=1, keepdims=True) + jnp.exp(sink - m)
            o4 = _mm(p / denom, vband)
            l4 = m + jnp.log(denom)
            for g, h in enumerate(heads):
                acc[h // 2] = acc[h // 2] + _from_kv_lanes(o4[g * BLK:(g + 1) * BLK], h)
                lse = jnp.where(lane == h, l4[g * BLK:(g + 1) * BLK], lse)
        for j in range(ATTN_WIDTH // LANES):
            ycat_ref[:, j * LANES:(j + 1) * LANES] = acc[j].astype(ycat_ref.dtype)
        lse_ref[...] = lse
        svo, _, _, _ = _sg_forward(sv_ref[...].astype(F32), lng_ref[...], lnb_ref[...], sgw_ref, sgb_ref[...], e2_ref[...])
        for j in range(SG_WIDTH // LANES):
            ysg = su_ref[:, j * LANES:(j + 1) * LANES].astype(F32) * svo[j]
            ycat_ref[:, ATTN_WIDTH + j * LANES:ATTN_WIDTH + (j + 1) * LANES] = ysg.astype(ycat_ref.dtype)

    blk = lambda w: pl.BlockSpec((BLK, w), lambda n: (n, 0))
    return _call(
        body, "even_mix", (nb,),
        [pl.BlockSpec(memory_space=pltpu.SMEM), blk(512), _full((seq, LANES)), _full((seq, LANES)), blk(512), blk(512),
         _full((1, 512)), _full((1, 512)), _full((8, BLK, BLK)), _full((BLK, 512)), _full((LANES, LANES))],
        [blk(D_MODEL), blk(LANES)], [_sds((seq, D_MODEL), ACT_DTYPE), _sds((seq, LANES))],
        (sink, q, k, v, su, sv, sgln_g, sgln_b, sgw, sgb_full, e2), "parallel", rider=rider)


def _even_out(ycat, g, x, mod, w_out, ln_g, ln_b, seq, rider=None):
    tm = _row_tile(seq, 512)

    def body(y_ref, g_ref, x_ref, mod_ref, wo_ref, g1_ref, b1_ref, z_ref, x1_ref):
        gg = g_ref[...].astype(F32)
        out = _mm(y_ref[...].astype(F32) * (gg * _sigmoid(gg)), wo_ref[...])
        z = ALPHA * x_ref[...] + mod_ref[2:3, :] * out
        z_ref[...] = z
        zhat, _ = _ln_stats(z)
        x1_ref[...] = zhat * g1_ref[...] + b1_ref[...]

    return _call(
        body, "even_out", (seq // tm,),
        [_rows(tm, D_MODEL)] * 3 + [_full((3, D_MODEL)), _const((D_MODEL, D_MODEL)), _full((1, D_MODEL)), _full((1, D_MODEL))],
        [_rows(tm, D_MODEL)] * 2, [_sds((seq, D_MODEL))] * 2, (ycat, g, x, mod, w_out, ln_g, ln_b), "parallel", rider=rider)


def _odd_proj(x1, mod, w_in4, seq):
    tm = _row_tile(seq, 512)
    cs = ODD_IN // 4

    def body(x_ref, mod_ref, w_ref, h_ref, xr_ref, g_ref):
        h = x_ref[...] * (1.0 + mod_ref[1:2, :]) + mod_ref[0:1, :]
        hb = h.astype(MXU_DTYPE)
        h_ref[...] = hb
        for s in range(2):
            xr_ref[:, s * cs:(s + 1) * cs] = jnp.dot(hb, w_ref[s], preferred_element_type=F32)
            g_ref[:, s * cs:(s + 1) * cs] = jnp.dot(hb, w_ref[2 + s], preferred_element_type=F32).astype(g_ref.dtype)

    return pl.pallas_call(
        body, name="odd_proj", grid=(seq // tm,),
        in_specs=[_rows(tm, D_MODEL), _full((3, D_MODEL)), _full((4, D_MODEL, cs))],
        out_specs=[_rows(tm, D_MODEL)] * 3,
        out_shape=[_sds((seq, D_MODEL), MXU_DTYPE), _sds((seq, D_MODEL)), _sds((seq, D_MODEL), ACT_DTYPE)],
        compiler_params=_params("parallel"),
    )(x1, mod, w_in4)


def _halo_specs(tm, seq, width):
    per = tm // 8
    last = seq // 8 - 1
    return [pl.BlockSpec((8, width), lambda i: (jnp.maximum(i * per - 1, 0), 0)),
            pl.BlockSpec((tm, width), lambda i: (i, 0)),
            pl.BlockSpec((8, width), lambda i: (jnp.minimum((i + 1) * per, last), 0))]


def _extended(prev_ref, main_ref, next_ref, i, n_steps):
    prev = jnp.where(i > 0, prev_ref[...], 0.0)
    nxt = jnp.where(i < n_steps - 1, next_ref[...], 0.0)
    return jnp.concatenate([prev, main_ref[...], nxt], axis=0)


def _shifted(ext, off, tm):
    if off == 0:
        return ext[8:8 + tm]
    return pltpu.roll(ext, (-off) % ext.shape[0], 0)[8:8 + tm]


def _lru_gates(xh, pre, bias, sp, hs):
    res = []
    for d in range(2):
        r = _sigmoid(pre[:, (2 * d) * LANES:(2 * d + 1) * LANES] + bias[2 * d:2 * d + 1, hs])
        ig = _sigmoid(pre[:, (2 * d + 1) * LANES:(2 * d + 2) * LANES] + bias[2 * d + 1:2 * d + 2, hs])
        neg_log_a = RG_LRU_C * r * sp[d:d + 1, hs]
        a = jnp.exp(-neg_log_a)
        s = jnp.sqrt(jnp.tanh(neg_log_a) * (a * a + 1.0))
        res.append((r, ig, a, s))
    return res


def _odd_gates(xr, conv_w, conv_b, wcat, bias, lam, seq):
    tm = _row_tile(seq, 512)
    steps = seq // tm

    def body(xp_ref, xm_ref, xn_ref, cw_ref, cb_ref, w_ref, bias_ref, lam_ref, xc_ref, af_ref, bf_ref, ar_ref, br_ref):
        i = pl.program_id(0)
        ext = _extended(xp_ref, xm_ref, xn_ref, i, steps)
        xc = cb_ref[...] + sum(cw_ref[kk:kk + 1, :] * _shifted(ext, kk - 2, tm) for kk in range(4))
        xc_ref[...] = xc
        sp, _ = _softplus_neg(lam_ref[...])
        bias = bias_ref[...]
        for h in range(RNN_HEADS):
            hs = slice(h * LANES, (h + 1) * LANES)
            xh = xc[:, hs]
            (_, i0, a0, s0), (_, i1, a1, s1) = _lru_gates(xh, _mm(xh, w_ref[h]), bias, sp, hs)
            af_ref[:, hs] = a0
            bf_ref[:, hs] = s0 * i0 * xh
            ar_ref[:, hs] = a1
            br_ref[:, hs] = s1 * i1 * xh

    return pl.pallas_call(
        body, name="odd_gates", grid=(steps,),
        in_specs=_halo_specs(tm, seq, D_MODEL) + [_full((4, D_MODEL)), _full((1, D_MODEL)), _full((8, LANES, 512)),
                                                  _full((4, D_MODEL)), _full((2, D_MODEL))],
        out_specs=[_rows(tm, D_MODEL)] * 5, out_shape=[_sds((seq, D_MODEL))] * 5,
        compiler_params=_params("parallel"),
    )(xr, xr, xr, conv_w, conv_b, wcat, bias, lam)


def _scan(a, b, seq, descending, post, name):
    tb = _row_tile(seq, 512)
    steps = seq // tb
    imap = (lambda i: (steps - 1 - i, 0)) if descending else (lambda i: (i, 0))
    spec = pl.BlockSpec((tb, D_MODEL), imap)
    n_out = 1 if post else 2

    sub = 8
    tiles = tb // sub

    def body(a_ref, b_ref, *rest):
        outs, carry_h, carry_a = rest[:n_out], rest[n_out], rest[n_out + 1]

        @pl.when(pl.program_id(0) == 0)
        def _():
            carry_h[...] = jnp.zeros_like(carry_h)
            carry_a[...] = jnp.zeros_like(carry_a)

        row = lax.broadcasted_iota(jnp.int32, (sub, D_MODEL), 0)

        def shift(v, d, fill):
            if descending:
                return jnp.where(row <= sub - 1 - d, pltpu.roll(v, sub - d, 0), fill)
            return jnp.where(row >= d, pltpu.roll(v, d, 0), fill)

        def last(v):
            return jnp.broadcast_to(v[0:1, :] if descending else v[sub - 1:sub, :], v.shape)

        def tile(j, c):
            ch, ca = c
            r0 = pl.multiple_of(((tiles - 1 - j) if descending else j) * sub, sub)
            at = a_ref[pl.ds(r0, sub), :]
            bt = b_ref[pl.ds(r0, sub), :]
            coef = shift(at, 1, ca) if post else at
            acc_a, acc_b = coef, bt
            for d in (1, 2, 4):
                acc_b = acc_b + acc_a * shift(acc_b, d, 0.0)
                acc_a = acc_a * shift(acc_a, d, 1.0)
            h = acc_b + acc_a * ch
            outs[0][pl.ds(r0, sub), :] = h
            if post:
                return last(h), last(at)
            outs[1][pl.ds(r0, sub), :] = shift(h, 1, ch)
            return last(h), ca

        ch, ca = lax.fori_loop(0, tiles, tile, (carry_h[...], carry_a[...]), unroll=4)
        carry_h[...] = ch
        carry_a[...] = ca

    return pl.pallas_call(
        body, name=name, grid=(steps,), in_specs=[spec, spec], out_specs=[spec] * n_out,
        out_shape=[_sds((seq, D_MODEL))] * n_out, scratch_shapes=[pltpu.VMEM((sub, D_MODEL), F32)] * 2,
        compiler_params=_params("arbitrary"),
    )(a, b)


def _odd_out_and_loss(hf, hr, g, x1, tgt, mod, w_out, ln_g, ln_b, seq):
    tm = _row_tile(seq, 512)

    def body(hf_ref, hr_ref, g_ref, x_ref, t_ref, mod_ref, w_ref, lg_ref, lb_ref,
             dhs_ref, dg_ref, dres_ref, loss_ref, dw_ref, vec_ref):
        @pl.when(pl.program_id(0) == 0)
        def _():
            loss_ref[...] = jnp.zeros_like(loss_ref)
            dw_ref[...] = jnp.zeros_like(dw_ref)
            vec_ref[...] = jnp.zeros_like(vec_ref)

        gg = g_ref[...].astype(F32)
        sg = _sigmoid(gg)
        silu = gg * sg
        hsum = hf_ref[...] + hr_ref[...]
        y = hsum * silu
        out = _mm(y, w_ref[...])
        gate = mod_ref[2:3, :]
        z = ALPHA * x_ref[...] + gate * out
        zhat, rstd = _ln_stats(z)
        x2 = zhat * lg_ref[...] + lb_ref[...]
        err = x2 - t_ref[...]
        loss_ref[...] += 0.5 * jnp.sum(jnp.mean(err * err, axis=-1, keepdims=True))
        dx2 = err * (1.0 / D_MODEL)
        dz = _ln_bwd(dx2, zhat, rstd, lg_ref[...])
        vec_ref[0:1, :] += jnp.sum(dx2 * zhat, axis=0, keepdims=True)
        vec_ref[1:2, :] += jnp.sum(dx2, axis=0, keepdims=True)
        vec_ref[2:3, :] += jnp.sum(dz * out, axis=0, keepdims=True)
        dres_ref[...] = ALPHA * dz
        dout = gate * dz
        dw_ref[...] += _mm_tn(y, dout)
        dy = _mm_nt(dout, w_ref[...])
        dhs_ref[...] = dy * silu
        dg_ref[...] = (dy * hsum * (sg * (1.0 + gg * (1.0 - sg)))).astype(dg_ref.dtype)

    return pl.pallas_call(
        body, name="odd_out_loss", grid=(seq // tm,),
        in_specs=[_rows(tm, D_MODEL)] * 5 + [_full((3, D_MODEL)), _const((D_MODEL, D_MODEL)),
                                             _full((1, D_MODEL)), _full((1, D_MODEL))],
        out_specs=[_rows(tm, D_MODEL)] * 3 + [_full((8, LANES)), _full((D_MODEL, D_MODEL)), _full((8, D_MODEL))],
        out_shape=[_sds((seq, D_MODEL)), _sds((seq, D_MODEL), ACT_DTYPE), _sds((seq, D_MODEL)), _sds((8, LANES)),
                   _sds((D_MODEL, D_MODEL)), _sds((8, D_MODEL))],
        compiler_params=_params("arbitrary"),
    )(hf, hr, g, x1, tgt, mod, w_out, ln_g, ln_b)


def _odd_gates_bwd(xc, gf, gr, hpf, hpr, wcat, bias, lam, seq):
    tm = _row_tile(seq, 512)
    steps = seq // tm

    def body(xc_ref, gf_ref, gr_ref, hpf_ref, hpr_ref, w_ref, bias_ref, lam_ref, dxc_ref, dw_ref, vec_ref):
        @pl.when(pl.program_id(0) == 0)
        def _():
            dw_ref[...] = jnp.zeros_like(dw_ref)
            vec_ref[...] = jnp.zeros_like(vec_ref)

        sp, dsp = _softplus_neg(lam_ref[...])
        bias = bias_ref[...]
        for h in range(RNN_HEADS):
            hs = slice(h * LANES, (h + 1) * LANES)
            xh = xc_ref[:, hs]
            gates = _lru_gates(xh, _mm(xh, w_ref[h]), bias, sp, hs)
            dxh = jnp.zeros_like(xh)
            dpre = []
            for d, (g_ref_d, hp_ref_d) in enumerate(((gf_ref, hpf_ref), (gr_ref, hpr_ref))):
                r, ig, a, s = gates[d]
                db = g_ref_d[:, hs]
                da = db * hp_ref_d[:, hs]
                dxh = dxh + db * s * ig
                dlog_a = da * a - (db * ig * xh) * (a * a / s)
                dr = dlog_a * (-RG_LRU_C) * sp[d:d + 1, hs]
                di = db * s * xh
                dpr = dr * r * (1.0 - r)
                dpi = di * ig * (1.0 - ig)
                vec_ref[2 * d:2 * d + 1, hs] += jnp.sum(dpr, axis=0, keepdims=True)
                vec_ref[2 * d + 1:2 * d + 2, hs] += jnp.sum(dpi, axis=0, keepdims=True)
                vec_ref[4 + d:5 + d, hs] += jnp.sum(dlog_a * r, axis=0, keepdims=True) * (-RG_LRU_C) * dsp[d:d + 1, hs]
                dpre += [dpr, dpi]
            dcat = jnp.concatenate(dpre, axis=1)
            dw_ref[h] += _mm_tn(xh, dcat)
            dxc_ref[:, hs] = dxh + _mm_nt(dcat, w_ref[h])

    return pl.pallas_call(
        body, name="odd_gates_bwd", grid=(steps,),
        in_specs=[_rows(tm, D_MODEL)] * 5 + [_full((8, LANES, 512)), _full((4, D_MODEL)), _full((2, D_MODEL))],
        out_specs=[_rows(tm, D_MODEL), _full((8, LANES, 512)), _full((8, D_MODEL))],
        out_shape=[_sds((seq, D_MODEL)), _sds((8, LANES, 512)), _sds((8, D_MODEL))],
        compiler_params=_params("arbitrary"),
    )(xc, gf, gr, hpf, hpr, wcat, bias, lam)


def _odd_proj_bwd(dxc, xr, dg, x1, dres, mod, conv_w, w_in4, seq):
    tm = _row_tile(seq, 512)
    steps = seq // tm

    def body(dp_ref, dm_ref, dn_ref, xp_ref, xm_ref, xn_ref, dg_ref, x_ref, dres_ref, mod_ref, cw_ref, w_ref,
             dx_ref, dpb_ref, vec_ref):
        i = pl.program_id(0)

        @pl.when(i == 0)
        def _():
            vec_ref[...] = jnp.zeros_like(vec_ref)

        dext = _extended(dp_ref, dm_ref, dn_ref, i, steps)
        xext = _extended(xp_ref, xm_ref, xn_ref, i, steps)
        dxc_m = dm_ref[...]
        dxr = sum(cw_ref[kk:kk + 1, :] * _shifted(dext, 2 - kk, tm) for kk in range(4))
        for kk in range(4):
            vec_ref[kk:kk + 1, :] += jnp.sum(dxc_m * _shifted(xext, kk - 2, tm), axis=0, keepdims=True)
        vec_ref[4:5, :] += jnp.sum(dxc_m, axis=0, keepdims=True)
        dpb_ref[:, :D_MODEL] = dxr.astype(dpb_ref.dtype)
        dpb_ref[:, D_MODEL:] = dg_ref[...].astype(dpb_ref.dtype)
        cs = ODD_IN // 4
        dh = sum(_mm_nt(dpb_ref[:, s * cs:(s + 1) * cs], w_ref[s]) for s in range(4))
        x = x_ref[...]
        vec_ref[5:6, :] += jnp.sum(dh, axis=0, keepdims=True)
        vec_ref[6:7, :] += jnp.sum(dh * x, axis=0, keepdims=True)
        dx_ref[...] = dres_ref[...] + dh * (1.0 + mod_ref[1:2, :])

    return pl.pallas_call(
        body, name="odd_proj_bwd", grid=(steps,),
        in_specs=_halo_specs(tm, seq, D_MODEL) + _halo_specs(tm, seq, D_MODEL) + [_rows(tm, D_MODEL)] * 3
        + [_full((3, D_MODEL)), _full((4, D_MODEL)), _const((4, D_MODEL, ODD_IN // 4))],
        out_specs=[_rows(tm, D_MODEL), _rows(tm, ODD_IN), _full((8, D_MODEL))],
        out_shape=[_sds((seq, D_MODEL)), _sds((seq, ODD_IN), MXU_DTYPE), _sds((8, D_MODEL))],
        compiler_params=_params("arbitrary"),
    )(dxc, dxc, dxc, xr, xr, xr, dg, x1, dres, mod, conv_w, w_in4)


def _tn_matmul(a, b, seq, name, transposed):
    n = b.shape[1]
    tn = n // 2
    cs = n // 4
    tm = _row_tile(seq, 512)
    steps = seq // tm

    def body(a_ref, b_ref, o_ref, acc_ref):
        i = pl.program_id(1)

        @pl.when(i == 0)
        def _():
            acc_ref[...] = jnp.zeros_like(acc_ref)

        acc_ref[...] += lax.dot_general(a_ref[...], b_ref[...], (((0,), (0,)), ((), ())), preferred_element_type=F32)

        @pl.when(i == steps - 1)
        def _():
            if transposed:
                o_ref[...] = acc_ref[...].T
            else:
                o_ref[0] = acc_ref[:, 0:cs]
                o_ref[1] = acc_ref[:, cs:2 * cs]

    if transposed:
        out_spec, out_shape = pl.BlockSpec((tn, D_MODEL), lambda j, i: (j, 0)), _sds((n, D_MODEL))
    else:
        out_spec, out_shape = pl.BlockSpec((2, D_MODEL, cs), lambda j, i: (j, 0, 0)), _sds((4, D_MODEL, cs))
    return pl.pallas_call(
        body, name=name, grid=(2, steps),
        in_specs=[pl.BlockSpec((tm, D_MODEL), lambda j, i: (i, 0)), pl.BlockSpec((tm, tn), lambda j, i: (i, j))],
        out_specs=out_spec, out_shape=out_shape,
        scratch_shapes=[pltpu.VMEM((D_MODEL, tn), F32)], compiler_params=_params("parallel", "arbitrary"),
    )(a, b)


def _even_out_bwd(dx1, z, ycat, g, mod, ln_g, w_out, seq, rider=None):
    tm = _row_tile(seq, 512)
    steps = seq // tm

    def body(dx_ref, z_ref, y_ref, g_ref, mod_ref, lg_ref, w_ref, dy_ref, dg_ref, dres_ref, dw_ref, vec_ref):
        i = pl.program_id(0)

        @pl.when(i == 0)
        def _():
            dw_ref[...] = jnp.zeros_like(dw_ref)
            vec_ref[...] = jnp.zeros_like(vec_ref)

        zhat, rstd = _ln_stats(z_ref[...])
        dx1_ = dx_ref[...]
        dz = _ln_bwd(dx1_, zhat, rstd, lg_ref[...])
        vec_ref[0:1, :] += jnp.sum(dx1_ * zhat, axis=0, keepdims=True)
        vec_ref[1:2, :] += jnp.sum(dx1_, axis=0, keepdims=True)
        dres_ref[...] = ALPHA * dz
        gate = mod_ref[2:3, :]
        gg = g_ref[...].astype(F32)
        sg = _sigmoid(gg)
        silu = gg * sg
        ycat_ = y_ref[...].astype(F32)
        dw_ref[...] += _mm_tn(ycat_ * silu, dz)
        dy = _mm_nt(gate * dz, w_ref[...])
        dy_ref[...] = (dy * silu).astype(dy_ref.dtype)
        dg_ref[...] = (dy * ycat_ * (sg * (1.0 + gg * (1.0 - sg)))).astype(dg_ref.dtype)

        @pl.when(i == steps - 1)
        def _():
            m_acc = dw_ref[...]
            vec_ref[2:3, :] = jnp.sum(w_ref[...].astype(F32) * m_acc, axis=0, keepdims=True)
            dw_ref[...] = m_acc * gate

    return _call(
        body, "even_out_bwd", (steps,),
        [_rows(tm, D_MODEL)] * 4 + [_full((3, D_MODEL)), _full((1, D_MODEL)), _const((D_MODEL, D_MODEL))],
        [_rows(tm, D_MODEL)] * 3 + [_full((D_MODEL, D_MODEL)), _full((8, D_MODEL))],
        [_sds((seq, D_MODEL), ACT_DTYPE), _sds((seq, D_MODEL), ACT_DTYPE), _sds((seq, D_MODEL)), _sds((D_MODEL, D_MODEL)),
         _sds((8, D_MODEL))],
        (dx1, z, ycat, g, mod, ln_g, w_out), "arbitrary", rider=rider)


def _even_mix_bwd(q, k, v, lse, ycat, dycat, su, sv, sink, sgln_g, sgln_b, sgw, sgb_full, e2, e8, seq, rider=None):
    nb = seq // BLK

    def body(sink_ref, q_ref, k_ref, v_ref, lse_ref, y_ref, dy_ref, su_ref, sv_ref, lng_ref, lnb_ref, sgw_ref, sgb_ref, e2_ref,
             e8_ref, dq_ref, dsu_ref, dsv_ref, dk_ref, dv_ref, dsgw_ref, dsgb_ref, vec_ref, dsink_ref, dsgb_acc):
        n = pl.program_id(0)

        @pl.when(n == 0)
        def _():
            dk_ref[...] = jnp.zeros_like(dk_ref)
            dv_ref[...] = jnp.zeros_like(dv_ref)
            dsgw_ref[...] = jnp.zeros_like(dsgw_ref)
            dsgb_acc[...] = jnp.zeros_like(dsgb_acc)
            vec_ref[...] = jnp.zeros_like(vec_ref)
            dsink_ref[...] = jnp.zeros_like(dsink_ref)

        kband = _band(k_ref, n, nb)
        vband = _band(v_ref, n, nb)
        bias = _band_bias(n, seq)
        lane = _lane_iota((BLK, LANES))
        row8 = lax.broadcasted_iota(jnp.int32, (8, LANES), 0)
        lse = lse_ref[...]
        dkb = jnp.zeros((LANES, 3 * BLK), F32)
        dvb = jnp.zeros((LANES, 3 * BLK), F32)
        dsink = jnp.zeros((8, LANES), F32)
        q_tile = lambda j: q_ref[:, j * LANES:(j + 1) * LANES].astype(F32)
        do_tile = lambda j: dy_ref[:, j * LANES:(j + 1) * LANES].astype(F32)
        dq = [jnp.zeros((BLK, LANES), F32) for _ in range(ATTN_WIDTH // LANES)]
        for kv in range(N_Q_HEADS // Q_PER_KV):
            heads = range(Q_PER_KV * kv, Q_PER_KV * (kv + 1))
            lse4, delta4 = [], []
            for h in heads:
                head_lanes = (lane < HEAD_DIM) if h % 2 == 0 else (lane >= HEAD_DIM)
                lse4.append(jnp.sum(jnp.where(lane == h, lse, 0.0), axis=1, keepdims=True))
                o_tile = y_ref[:, (h // 2) * LANES:(h // 2 + 1) * LANES].astype(F32)
                delta4.append(jnp.sum(jnp.where(head_lanes, do_tile(h // 2) * o_tile, 0.0), axis=1, keepdims=True))
            lse4, delta4 = jnp.concatenate(lse4, axis=0), jnp.concatenate(delta4, axis=0)
            q4, do4 = _stack_heads(q_tile, kv), _stack_heads(do_tile, kv)
            s = _mm_nt(q4, kband) * (HEAD_DIM ** -0.5) + bias
            p = jnp.exp(s - lse4)
            wsink = jnp.exp(_per_head_column([sink_ref[h] for h in heads]) - lse4) * delta4
            ds = p * (_mm_nt(do4, vband) - delta4) * (HEAD_DIM ** -0.5)
            dq4 = _mm(ds, kband)
            dkb = dkb + _mm_tn(q4, ds)
            dvb = dvb + _mm_tn(do4, p)
            for g, h in enumerate(heads):
                dq[h // 2] = dq[h // 2] + _from_kv_lanes(dq4[g * BLK:(g + 1) * BLK], h)
                dsink = dsink + jnp.where(row8 == h, -jnp.sum(wsink[g * BLK:(g + 1) * BLK]), 0.0)
        for j in range(ATTN_WIDTH // LANES):
            dq_ref[:, j * LANES:(j + 1) * LANES] = dq[j].astype(dq_ref.dtype)
        dsink_ref[...] += dsink
        prev = jnp.maximum(n - 1, 0)
        nxt = jnp.minimum(n + 1, nb - 1)
        for part, blk_i in enumerate((prev, n, nxt)):
            rows = pl.ds(pl.multiple_of(blk_i * BLK, BLK), BLK)
            dk_ref[rows, :] += dkb[:, part * BLK:(part + 1) * BLK].T
            dv_ref[rows, :] += dvb[:, part * BLK:(part + 1) * BLK].T

        e2 = e2_ref[...]
        lng = lng_ref[...]
        svo, vn, vhat, rstd = _sg_forward(sv_ref[...].astype(F32), lng, lnb_ref[...], sgw_ref, sgb_ref[...], e2)
        for j in range(SG_WIDTH // LANES):
            cs = slice(j * LANES, (j + 1) * LANES)
            dysg = dy_ref[:, ATTN_WIDTH + j * LANES:ATTN_WIDTH + (j + 1) * LANES].astype(F32)
            dsu_ref[:, cs] = (dysg * svo[j]).astype(dsu_ref.dtype)
            dsvo = dysg * su_ref[:, cs].astype(F32)
            dsgb_acc[:, cs] += dsvo
            d_lo = jnp.where(lane < HEAD_DIM, dsvo, 0.0)
            d_hi = dsvo - d_lo
            dsgw_ref[2 * j] += _mm_nt(d_lo, vn[j])
            dsgw_ref[2 * j + 1] += _mm_nt(d_hi, vn[j])
            dvn = _mm_tn(sgw_ref[2 * j], d_lo) + _mm_tn(sgw_ref[2 * j + 1], d_hi)
            vec_ref[0:1, cs] += jnp.sum(dvn * vhat[j], axis=0, keepdims=True)
            vec_ref[1:2, cs] += jnp.sum(dvn, axis=0, keepdims=True)
            dvh = dvn * lng[:, cs]
            m1 = _group_sum(dvh, e2) * (1.0 / HEAD_DIM)
            m2 = _group_sum(dvh * vhat[j], e2) * (1.0 / HEAD_DIM)
            dsv_ref[:, cs] = (rstd[j] * (dvh - m1 - vhat[j] * m2)).astype(dsv_ref.dtype)

        @pl.when(n == nb - 1)
        def _():
            rest = dsgb_acc[...]
            total = jnp.zeros((8, BLK), F32)
            for _ in range(3):
                part = rest.astype(MXU_DTYPE)
                total = total + lax.dot_general(e8_ref[...], part, (((1,), (1,)), ((), ())), preferred_element_type=F32)
                rest = rest - part.astype(F32)
            dsgb_ref[...] = total

    blk = lambda w: pl.BlockSpec((BLK, w), lambda n: (n, 0))
    return _call(
        body, "even_mix_bwd", (nb,),
        [pl.BlockSpec(memory_space=pltpu.SMEM), blk(512), _full((seq, LANES)), _full((seq, LANES)), blk(LANES),
         blk(D_MODEL), blk(D_MODEL), blk(512), blk(512), _full((1, 512)), _full((1, 512)), _full((8, BLK, BLK)),
         _full((BLK, 512)), _full((LANES, LANES)), _full((8, 512))],
        [blk(512), blk(512), blk(512), _full((seq, LANES)), _full((seq, LANES)), _full((8, BLK, BLK)),
         _full((8, BLK)), _full((8, 512)), _full((8, LANES))],
        [_sds((seq, 512), ACT_DTYPE), _sds((seq, 512), ACT_DTYPE), _sds((seq, 512), ACT_DTYPE), _sds((seq, LANES)), _sds((seq, LANES)),
         _sds((8, BLK, BLK)), _sds((8, BLK)), _sds((8, 512)), _sds((8, LANES))],
        (sink, q, k, v, lse, ycat, dycat, su, sv, sgln_g, sgln_b, sgw, sgb_full, e2, e8), "arbitrary",
        scratch=[pltpu.VMEM((BLK, 512), F32)], rider=rider)


def _even_proj_bwd(dq, dk, dv, dsu, dsv, dg, x, dres, mod, tabs, w_in_t, seq):
    tm = _row_tile(seq, 512)

    def body(dq_ref, dk_ref, dv_ref, dsu_ref, dsv_ref, dg_ref, x_ref, dres_ref, mod_ref, cos_ref, sp_ref, sm_ref, wt_ref,
             dx_ref, dpb_ref, vec_ref):
        @pl.when(pl.program_id(0) == 0)
        def _():
            vec_ref[...] = jnp.zeros_like(vec_ref)

        cos_t, sin_p, sin_m = cos_ref[...], sp_ref[...], sm_ref[...]
        dt = dpb_ref.dtype
        for j in range(ATTN_WIDTH // LANES):
            cs = slice(j * LANES, (j + 1) * LANES)
            dpb_ref[:, cs] = _rope_t(dq_ref[:, cs].astype(F32), cos_t, sin_p, sin_m).astype(dt)
        dpb_ref[:, 512:640] = _rope_t(dk_ref[...], cos_t, sin_p, sin_m).astype(dt)
        dpb_ref[:, 640:768] = dv_ref[...].astype(dt)
        dpb_ref[:, 768:1280] = dsu_ref[...].astype(dt)
        dpb_ref[:, 1280:1792] = dsv_ref[...].astype(dt)
        dpb_ref[:, 1792:2816] = dg_ref[...].astype(dt)
        dh = jnp.dot(dpb_ref[...], wt_ref[...], preferred_element_type=F32)
        x_ = x_ref[...]
        vec_ref[0:1, :] += jnp.sum(dh, axis=0, keepdims=True)
        vec_ref[1:2, :] += jnp.sum(dh * x_, axis=0, keepdims=True)
        dx_ref[...] = dres_ref[...] + dh * (1.0 + mod_ref[1:2, :])

    return pl.pallas_call(
        body, name="even_proj_bwd", grid=(seq // tm,),
        in_specs=[_rows(tm, 512), _rows(tm, LANES), _rows(tm, LANES), _rows(tm, 512), _rows(tm, 512), _rows(tm, D_MODEL),
                  _rows(tm, D_MODEL), _rows(tm, D_MODEL), _full((3, D_MODEL))] + [_rows(tm, LANES)] * 3
        + [_const((EVEN_IN, D_MODEL))],
        out_specs=[_rows(tm, D_MODEL), _rows(tm, EVEN_IN), _full((8, D_MODEL))],
        out_shape=[_sds((seq, D_MODEL)), _sds((seq, EVEN_IN), MXU_DTYPE), _sds((8, D_MODEL))],
        compiler_params=_params("arbitrary"),
    )(dq, dk, dv, dsu, dsv, dg, x, dres, mod, *tabs, w_in_t)


def _local_step(x, posf, tgt, mod, w, seq, ride=None):
    rid = lambda make, *a: None if ride is None else make(*a)
    mxu = lambda a: a.astype(MXU_DTYPE)
    row = lambda a: a.reshape(1, -1)
    tabs = _rope_tables(posf, seq)
    e2 = mxu(jnp.kron(jnp.eye(2, dtype=F32), jnp.ones((HEAD_DIM, HEAD_DIM), F32)))
    e8 = mxu(jnp.repeat(jnp.eye(N_SG_GROUPS, dtype=F32), HEAD_DIM, axis=1))
    sgw = mxu(w["ev_sg_w"])
    sgb_full = jnp.repeat(w["ev_sg_b"].T, HEAD_DIM, axis=1)
    sgln_g, sgln_b = row(w["ev_sg_ln_g"]), row(w["ev_sg_ln_b"])
    sink = w["ev_sink"].reshape(N_Q_HEADS)
    ev_w_in_t = mxu(w["ev_w_in_t"])
    if ride is None:
        ev_w_out, od_w_in, od_w_out = mxu(w["ev_w_out"]), mxu(w["od_w_in"]), mxu(w["od_w_out"])
    wcat = mxu(jnp.concatenate([w["od_w_a"][0], w["od_w_x"][0], w["od_w_a"][1], w["od_w_x"][1]], axis=2))
    gate_bias = jnp.stack([w["od_b_a"][0], w["od_b_x"][0], w["od_b_a"][1], w["od_b_x"][1]])
    conv_b = row(w["od_conv_b"])
    ln_g, ln_b = w["ln_g"], w["ln_b"]

    (h0, q, k, v, su, sv, g0), got = _even_proj(x, mod[0], ev_w_in_t, tabs, seq, rid(_gather_rider, ride and ride["ev_w_out"]))
    if ride is not None:
        ev_w_out = got[0].reshape(D_MODEL, D_MODEL)
    (ycat, lse), got = _even_mix(q, k, v, su, sv, sink, sgln_g, sgln_b, sgw, sgb_full, e2, seq,
                                 rid(_gather_rider, ride and ride["od_w_in"]))
    if ride is not None:
        od_w_in = got[0]
    (z0, x1), got = _even_out(ycat, g0, x, mod[0], ev_w_out, ln_g[0:1], ln_b[0:1], seq, rid(_gather_rider, ride and ride["od_w_out"]))
    if ride is not None:
        od_w_out = got[0].reshape(D_MODEL, D_MODEL)
    h1, xr, g1 = _odd_proj(x1, mod[1], od_w_in, seq)
    xc, a_f, b_f, a_r, b_r = _odd_gates(xr, w["od_conv_w"], conv_b, wcat, gate_bias, w["od_lam"], seq)
    hf, hpf = _scan(a_f, b_f, seq, descending=False, post=False, name="scan_fwd")
    hr, hpr = _scan(a_r, b_r, seq, descending=True, post=False, name="scan_rev")
    dhs, dg1, dres1, loss, d_od_w_out, vec_o = _odd_out_and_loss(hf, hr, g1, x1, tgt, mod[1], od_w_out, ln_g[1:2], ln_b[1:2], seq)
    (gf,) = _scan(a_f, dhs, seq, descending=True, post=True, name="scan_fwd_bwd")
    (gr,) = _scan(a_r, dhs, seq, descending=False, post=True, name="scan_rev_bwd")
    dxc, d_wcat, vec_g = _odd_gates_bwd(xc, gf, gr, hpf, hpr, wcat, gate_bias, w["od_lam"], seq)
    dx1, dp1, vec_p = _odd_proj_bwd(dxc, xr, dg1, x1, dres1, mod[1], w["od_conv_w"], od_w_in, seq)
    d_od_w_in = _tn_matmul(h1, dp1, seq, "odd_dw_in", transposed=False)
    d_od_w_a = jnp.stack([d_wcat[:, :, 0:128], d_wcat[:, :, 256:384]])
    d_od_w_x = jnp.stack([d_wcat[:, :, 128:256], d_wcat[:, :, 384:512]])
    od_parts = [d_od_w_in.reshape(4, 2, 512, 512), d_od_w_out.reshape(4, 2, 128, D_MODEL),
                d_od_w_a.reshape(4, 2, 2 * BLK, BLK), d_od_w_x.reshape(4, 2, 2 * BLK, BLK)]
    (dycat, dg0, dres0, d_ev_w_out, vec_e), got_od = _even_out_bwd(dx1, z0, ycat, g0, mod[0], ln_g[0:1], ev_w_out, seq,
                                                                   rid(_sibling_swap_rider, od_parts))
    if ride is not None:
        od_sums = _sum_sibling(ride["core"], od_parts, got_od, [ride["wire"]] * 4, "sum_sibling_od")
    (dq, dsu, dsv, dk, dv, d_sgw, d_sgb, vec_s, d_sink), od_slots = _even_mix_bwd(
        q, k, v, lse, ycat, dycat, su, sv, sink, sgln_g, sgln_b, sgw, sgb_full, e2, e8, seq,
        rid(_chip_exchange_rider, ride and od_sums))
    grad_x, dp0, vec_x = _even_proj_bwd(dq, dk, dv, dsu, dsv, dg0, x, dres0, mod[0], tabs, ev_w_in_t, seq)
    d_ev_w_in_t = _tn_matmul(h0, dp0, seq, "even_dw_in", transposed=True)

    dmod = jnp.stack([jnp.stack([vec_x[0], vec_x[1], vec_e[2]]), jnp.stack([vec_p[5], vec_p[6], vec_o[2]])])
    grads = {
        "ln_g": jnp.stack([vec_e[0], vec_o[0]]), "ln_b": jnp.stack([vec_e[1], vec_o[1]]),
        "ev_w_in_t": d_ev_w_in_t, "ev_w_out": d_ev_w_out, "ev_sink": d_sink[:, 0],
        "ev_sg_ln_g": vec_s[0], "ev_sg_ln_b": vec_s[1], "ev_sg_w": d_sgw,
        "ev_sg_b": d_sgb,
        "od_conv_w": vec_p[0:4], "od_conv_b": vec_p[4],
        "od_b_a": jnp.stack([vec_g[0], vec_g[2]]), "od_b_x": jnp.stack([vec_g[1], vec_g[3]]), "od_lam": vec_g[4:6],
    }
    if ride is None:
        grads.update({"od_w_in": d_od_w_in, "od_w_out": d_od_w_out, "od_w_a": d_od_w_a, "od_w_x": d_od_w_x})
    else:
        grads["od_slots"] = od_slots
    return loss[0, 0], grad_x, dmod, grads


def _allgather8(block, name):
    m_per, n = block.shape

    def body(x_ref, out_ref, send_sems, recv_sems, local_sem):
        x, y, c = _place()
        me, sibling = (x, y, c), (x, y, 1 - c)
        chips = [(1 - x, y), (x, 1 - y), (1 - x, 1 - y)]

        def rows(px, py, pc):
            return out_ref.at[pl.ds((4 * px + 2 * py + pc) * m_per, m_per), :]

        def copy(k, blk, to, src=None):
            return pltpu.make_async_remote_copy(src_ref=rows(*blk) if src is None else src, dst_ref=rows(*blk),
                                                send_sem=send_sems.at[k], recv_sem=recv_sems.at[k], device_id=to,
                                                device_id_type=MESH)

        mine = pltpu.make_async_copy(x_ref, rows(*me), local_sem)
        mine.start()
        first = [copy(0, me, sibling, src=x_ref)] + [copy(1 + j, me, (*chip, c), src=x_ref) for j, chip in enumerate(chips)]
        for cp in first:
            cp.start()
        passed = [copy(4 + j, (*chip, c), sibling) for j, chip in enumerate(chips)]
        for j, chip in enumerate(chips):
            copy(1 + j, (*chip, c), me).wait_recv()
            passed[j].start()
        copy(0, sibling, me).wait_recv()
        for j, chip in enumerate(chips):
            copy(4 + j, (*chip, 1 - c), me).wait_recv()
        for cp in first + passed:
            cp.wait_send()
        mine.wait()

    return pl.pallas_call(
        body, name=name, out_shape=_sds((8 * m_per, n), block.dtype),
        in_specs=[pl.BlockSpec(memory_space=pltpu.VMEM)], out_specs=pl.BlockSpec(memory_space=pltpu.VMEM),
        scratch_shapes=[pltpu.SemaphoreType.DMA((7,)), pltpu.SemaphoreType.DMA((7,)), pltpu.SemaphoreType.DMA],
        compiler_params=pltpu.CompilerParams(vmem_limit_bytes=VMEM_LIMIT),
    )(block)


class _Copies:
    def __init__(self, send_sems, recv_sems, local_sems, stages):
        self.send_sems, self.recv_sems, self.local_sems, self.stages = send_sems, recv_sems, local_sems, stages
        self.sent, self.staged, self.locals = [], [], []

    def remote(self, k, src, dst, to):
        return pltpu.make_async_remote_copy(src_ref=src, dst_ref=dst, send_sem=self.send_sems.at[k], recv_sem=self.recv_sems.at[k],
                                            device_id=to, device_id_type=MESH)

    def send(self, k, src, dst, to):
        cp = self.remote(k, src, dst, to)
        cp.start()
        self.sent.append(cp)

    def arrived(self, k, dst, frm):
        self.remote(k, dst, dst, frm).wait_recv()

    def local(self, src, dst):
        k = len(self.staged)
        cp = pltpu.make_async_copy(src, self.stages[k], self.local_sems.at[2 * k])
        cp.start()
        self.staged.append((cp, dst))

    def flush(self):
        for k in range(len(self.locals), len(self.staged)):
            cp, dst = self.staged[k]
            cp.wait()
            out = pltpu.make_async_copy(self.stages[k], dst, self.local_sems.at[2 * k + 1])
            out.start()
            self.locals.append(out)

    def drain(self):
        self.flush()
        for cp in self.sent:
            cp.wait_send()
        for cp in self.locals:
            cp.wait()


def _comm_call(body, name, ins, out_shapes, n_remote, stages):
    n_in, n_out = len(ins), len(out_shapes)

    def kern(*refs):
        in_refs, out_refs = refs[:n_in], refs[n_in:n_in + n_out]
        send_sems, recv_sems, local_sems = refs[n_in + n_out:n_in + n_out + 3]
        body(_Copies(send_sems, recv_sems, local_sems, refs[n_in + n_out + 3:]), in_refs, out_refs)

    hbm = pl.BlockSpec(memory_space=pl.ANY)
    return pl.pallas_call(
        kern, name=name, out_shape=out_shapes, in_specs=[hbm] * n_in, out_specs=[hbm] * n_out,
        scratch_shapes=[pltpu.SemaphoreType.DMA((n_remote,)), pltpu.SemaphoreType.DMA((n_remote,)),
                        pltpu.SemaphoreType.DMA((2 * len(stages),))] + [pltpu.VMEM(s, d) for s, d in stages],
        compiler_params=pltpu.CompilerParams(vmem_limit_bytes=VMEM_LIMIT),
    )(*ins)


def _gather_to_all(cps, pairs, me, sibling, other_chips, c, base):
    idx = lambda p: 4 * p[0] + 2 * p[1] + p[2]
    for i, (src, dst) in enumerate(pairs):
        cps.local(src, dst.at[idx(me)])
        cps.send(base + 7 * i, src, dst.at[idx(me)], sibling)
        for j, chip in enumerate(other_chips):
            cps.send(base + 7 * i + 1 + j, src, dst.at[idx(me)], (*chip, c))
    cps.flush()
    for j, chip in enumerate(other_chips):
        for i, (_, dst) in enumerate(pairs):
            got = dst.at[idx((*chip, c))]
            cps.arrived(base + 7 * i + 1 + j, got, (*chip, c))
            cps.send(base + 7 * i + 4 + j, got, got, sibling)
    for i, (_, dst) in enumerate(pairs):
        cps.arrived(base + 7 * i, dst.at[idx(sibling)], sibling)
        for j, chip in enumerate(other_chips):
            cps.arrived(base + 7 * i + 4 + j, dst.at[idx((*chip, 1 - c))], sibling)


def _gather_weights(shards, small):
    n = len(shards)

    def body(cps, ins, outs):
        x, y, c = _place()
        me, sibling, mine = (x, y, c), (x, y, 1 - c), 2 * x + y
        chips = [(1 - x, y), (x, 1 - y), (1 - x, 1 - y)]
        for i in range(n):
            cps.local(ins[i], outs[i].at[mine])
        for j, (px, py) in enumerate(chips):
            for i in range(n):
                hr = shards[i].shape[0] // 2
                rows = pl.ds(c * hr, hr)
                cps.send(6 * i + j, ins[i].at[rows], outs[i].at[mine, rows], (px, py, c))
        _gather_to_all(cps, [(ins[n], outs[n])], me, sibling, chips, c, 6 * n)
        for j, (px, py) in enumerate(chips):
            for i in range(n):
                hr = shards[i].shape[0] // 2
                got = outs[i].at[2 * px + py, pl.ds(c * hr, hr)]
                cps.arrived(6 * i + j, got, (px, py, c))
                cps.send(6 * i + 3 + j, got, got, sibling)
        for j, (px, py) in enumerate(chips):
            for i in range(n):
                hr = shards[i].shape[0] // 2
                cps.arrived(6 * i + 3 + j, outs[i].at[2 * px + py, pl.ds((1 - c) * hr, hr)], sibling)
        cps.drain()

    return _comm_call(body, "gather_weights", list(shards) + [small],
                      [_sds((4,) + s.shape, s.dtype) for s in shards] + [_sds((8,) + small.shape, small.dtype)], 6 * n + 7,
                      [(a.shape, a.dtype) for a in list(shards) + [small]])


def _reduce_sibling(parts, dmod_rows):
    n = len(parts)

    def body(cps, ins, outs):
        x, y, c = _place()
        me, sibling = (x, y, c), (x, y, 1 - c)
        chips = [(1 - x, y), (x, 1 - y), (1 - x, 1 - y)]
        for i in range(n):
            cps.send(i, ins[i].at[:, 1 - c], outs[i], sibling)
        _gather_to_all(cps, [(ins[n], outs[n])], me, sibling, chips, c, n)
        for i in range(n):
            cps.arrived(i, outs[i], sibling)
        cps.drain()

    return _comm_call(body, "reduce_sibling", list(parts) + [dmod_rows],
                      [_sds((4,) + p.shape[2:], p.dtype) for p in parts] + [_sds((8,) + dmod_rows.shape, dmod_rows.dtype)], n + 7,
                      [(dmod_rows.shape, dmod_rows.dtype)])


def _reduce_chips(parts):
    n = len(parts)

    def body(cps, ins, outs):
        x, y, c = _place()
        mine = 2 * x + y
        chips = [(1 - x, y), (x, 1 - y), (1 - x, 1 - y)]
        for i in range(n):
            cps.local(ins[i].at[mine], outs[i].at[mine])
        for j, (px, py) in enumerate(chips):
            for i in range(n):
                cps.send(3 * i + j, ins[i].at[2 * px + py], outs[i].at[mine], (px, py, c))
        cps.flush()
        for j, (px, py) in enumerate(chips):
            for i in range(n):
                cps.arrived(3 * i + j, outs[i].at[2 * px + py], (px, py, c))
        cps.drain()

    return _comm_call(body, "reduce_chips", list(parts), [_sds(p.shape, p.dtype) for p in parts], 3 * n,
                      [(p.shape[1:], p.dtype) for p in parts])


def _gather_reduced(shard_parts, repl_parts):
    ns, nr = len(shard_parts), len(repl_parts)

    def body(cps, ins, outs):
        x, y, c = _place()
        me, sibling = (x, y, c), (x, y, 1 - c)
        chips = [(1 - x, y), (x, 1 - y), (1 - x, 1 - y)]
        for i in range(ns):
            cps.local(ins[i], outs[i].at[c])
            cps.send(i, ins[i], outs[i].at[c], sibling)
        _gather_to_all(cps, [(ins[ns + i], outs[ns + i]) for i in range(nr)], me, sibling, chips, c, ns)
        for i in range(ns):
            cps.arrived(i, outs[i].at[1 - c], sibling)
        cps.drain()

    return _comm_call(body, "gather_reduced", list(shard_parts) + list(repl_parts),
                      [_sds((2,) + p.shape, p.dtype) for p in shard_parts] + [_sds((8,) + p.shape, p.dtype) for p in repl_parts],
                      ns + 7 * nr, [(p.shape, p.dtype) for p in list(shard_parts) + list(repl_parts)])


def _sum_sibling(core, parts, got, wire, name):
    n = len(parts)

    def body(core_ref, *refs):
        for i in range(n):
            refs[2 * n + i][0] = (refs[i][0] + refs[n + i][0]).astype(wire[i])

    keep_spec = lambda p: pl.BlockSpec((1, None) + p.shape[2:], lambda s, core_ref: (s, core_ref[0], 0, 0))
    slot_spec = lambda p: pl.BlockSpec((1,) + p.shape[2:], lambda s, core_ref: (s, 0, 0))
    return pl.pallas_call(
        body, name=name,
        grid_spec=pltpu.PrefetchScalarGridSpec(
            num_scalar_prefetch=1, grid=(4,), in_specs=[keep_spec(p) for p in parts] + [slot_spec(p) for p in parts],
            out_specs=[slot_spec(p) for p in parts]),
        out_shape=[_sds((4,) + p.shape[2:], wire[i]) for i, p in enumerate(parts)],
        compiler_params=_params("parallel"),
    )(core, *parts, *got)


def _sum_slots(slots, name):
    n = len(slots)

    def spec_pair(p):
        k, rows, cols = p.shape
        sub = 16 if p.dtype == BF16 else 8
        if (rows // 2) % sub == 0:
            return pl.BlockSpec((k, rows // 2, cols), lambda i: (0, i, 0)), pl.BlockSpec((rows // 2, cols), lambda i: (i, 0))
        return pl.BlockSpec((k, rows, cols), lambda i: (0, 0, 0)), pl.BlockSpec((rows, cols), lambda i: (0, 0))

    pairs = [spec_pair(p) for p in slots]

    def body(*refs):
        for i in range(n):
            acc = refs[i][0].astype(F32)
            for j in range(1, slots[i].shape[0]):
                acc = acc + refs[i][j].astype(F32)
            refs[n + i][...] = acc

    return pl.pallas_call(
        body, name=name, grid=(2,), in_specs=[a for a, _ in pairs], out_specs=[b for _, b in pairs],
        out_shape=[_sds(p.shape[1:]) for p in slots], compiler_params=_params("arbitrary"),
    )(*slots)


def _modulation(c_all, ada_w, ada_b):
    cols = ada_w.shape[2]

    def body(c_ref, w_ref, b_ref, o_ref):
        cc = c_ref[...]
        o_ref[0] = _mm(cc * _sigmoid(cc), w_ref[0]) + b_ref[0]

    return pl.pallas_call(
        body, name="modulation", grid=(2,),
        in_specs=[_full((8, D_MODEL)), pl.BlockSpec((1, D_MODEL, cols), lambda l: (l, 0, 0)), pl.BlockSpec((1, 1, cols), lambda l: (l, 0, 0))],
        out_specs=pl.BlockSpec((1, 8, cols), lambda l: (l, 0, 0)), out_shape=_sds((2, 8, cols)),
        compiler_params=_params("parallel"),
    )(c_all, ada_w, ada_b)


def _adamw_math(w, g, m, v):
    m = ADAM_B1 * m + (1.0 - ADAM_B1) * g
    v = ADAM_B2 * v + (1.0 - ADAM_B2) * (g * g)
    m_hat = m / (1.0 - ADAM_B1 ** ADAM_STEP)
    v_hat = v / (1.0 - ADAM_B2 ** ADAM_STEP)
    delta = -ADAM_LR * (m_hat / (jnp.sqrt(v_hat) + ADAM_EPS) + ADAM_WD * w)
    return delta, m, v


def _ada_update(c_all, dmod, w, m, v):
    cols = w.shape[2]
    tr = 256
    spec3 = pl.BlockSpec((1, tr, cols), lambda l, i: (l, i, 0))

    def body(c_ref, d_ref, w_ref, m_ref, v_ref, g_ref, dl_ref, nm_ref, nv_ref):
        cc = c_ref[...]
        g = _mm_tn(cc * _sigmoid(cc), d_ref[0])
        g_ref[0] = g
        dl_ref[0], nm_ref[0], nv_ref[0] = _adamw_math(w_ref[0], g, m_ref[0], v_ref[0])

    return pl.pallas_call(
        body, name="ada_update", grid=(2, D_MODEL // tr),
        in_specs=[pl.BlockSpec((8, tr), lambda l, i: (0, i)), pl.BlockSpec((1, 8, cols), lambda l, i: (l, 0, 0)), spec3, spec3, spec3],
        out_specs=[spec3] * 4, out_shape=[_sds(w.shape)] * 4, compiler_params=_params("parallel", "parallel"),
    )(c_all, dmod, w, m, v)


def _adamw(w, g, m, v, name):
    rows, n = w.shape
    tr = next(t for t in (256, 128, 64, 32, 16, 8, rows) if rows % t == 0)

    def body(w_ref, g_ref, m_ref, v_ref, dl_ref, nm_ref, nv_ref):
        dl_ref[...], nm_ref[...], nv_ref[...] = _adamw_math(w_ref[...], g_ref[...], m_ref[...], v_ref[...])

    return pl.pallas_call(body, name=name, grid=(rows // tr,), in_specs=[_rows(tr, n)] * 4, out_specs=[_rows(tr, n)] * 3,
                          out_shape=[_sds((rows, n))] * 3, compiler_params=_params("parallel"))(w, g, m, v)


def _adamw_small(params):
    n = len(params)

    def body(*refs):
        ins, outs = refs[:4 * n], refs[4 * n:]
        for j in range(n):
            w_ref, g_ref, m_ref, v_ref = ins[4 * j:4 * j + 4]
            outs[3 * j][...], outs[3 * j + 1][...], outs[3 * j + 2][...] = _adamw_math(w_ref[...], g_ref[...], m_ref[...], v_ref[...])

    flat = [a for p in params for a in p]
    res = pl.pallas_call(body, name="adamw_small", out_shape=[_sds(p[0].shape) for p in params for _ in range(3)])(*flat)
    return [tuple(res[3 * j:3 * j + 3]) for j in range(n)]


def _cols(a, start, size):
    return lax.dynamic_slice_in_dim(a, start, size, axis=a.ndim - 1)


def kernel(x, c, positions, ada_w, ada_b, ln_g, ln_b, ev_w_in, ev_w_out, ev_sink, ev_sg_ln_g, ev_sg_ln_b, ev_sg_w, ev_sg_b, od_w_in, od_conv_w, od_conv_b, od_w_a, od_b_a, od_w_x, od_b_x, od_lam, od_w_out, loss_target, m_ada_w, m_ada_b, m_ln_g, m_ln_b, m_ev_w_in, m_ev_w_out, m_ev_sink, m_ev_sg_ln_g, m_ev_sg_ln_b, m_ev_sg_w, m_ev_sg_b, m_od_w_in, m_od_conv_w, m_od_conv_b, m_od_w_a, m_od_b_a, m_od_w_x, m_od_b_x, m_od_lam, m_od_w_out, v_ada_w, v_ada_b, v_ln_g, v_ln_b, v_ev_w_in, v_ev_w_out, v_ev_sink, v_ev_sg_ln_g, v_ev_sg_ln_b, v_ev_sg_w, v_ev_sg_b, v_od_w_in, v_od_conv_w, v_od_conv_b, v_od_w_a, v_od_b_a, v_od_w_x, v_od_b_x, v_od_lam, v_od_w_out):
    seq = x.shape[1]
    px, py, pc = _place()
    chip = 2 * px + py
    dev = 2 * chip + pc

    small = jnp.concatenate([od_conv_w[0].reshape(-1), od_conv_b[0], od_b_a[0].reshape(-1), jnp.zeros((256,), F32),
                             od_b_x[0].reshape(-1), od_lam[0].reshape(-1)]).reshape(3, D_MODEL)
    blk = jnp.concatenate([c, small, jnp.zeros((4, D_MODEL), F32)], axis=0)
    tr = lambda a: jnp.swapaxes(a, -1, -2)
    wire_w = lambda a: a.astype(MXU_DTYPE)
    ev_w_in4, g_small = _gather_weights([wire_w(tr(ev_w_in[0]))], blk)
    core = pc.astype(jnp.int32).reshape(1)
    ride = {"ev_w_out": wire_w(ev_w_out[0]), "od_w_in": wire_w(od_w_in[0]), "od_w_out": wire_w(od_w_out[0]),
            "core": core, "wire": MXU_DTYPE}
    c_all = g_small[:, 0, :]
    per_chip = g_small[0::2]
    conv_w = per_chip[:, 1].reshape(4, 4, 256).transpose(1, 0, 2).reshape(4, D_MODEL)
    conv_b = per_chip[:, 2, 0:256].reshape(D_MODEL)
    b_a = per_chip[:, 2, 256:768].reshape(4, 2, 256).transpose(1, 0, 2).reshape(2, D_MODEL)
    b_x = per_chip[:, 3, 0:512].reshape(4, 2, 256).transpose(1, 0, 2).reshape(2, D_MODEL)
    lam = per_chip[:, 3, 512:1024].reshape(4, 2, 256).transpose(1, 0, 2).reshape(2, D_MODEL)

    w_full = {
        "ev_w_in_t": ev_w_in4.reshape(EVEN_IN, D_MODEL),
        "ev_sink": ev_sink[0], "ev_sg_ln_g": ev_sg_ln_g[0], "ev_sg_ln_b": ev_sg_ln_b[0], "ev_sg_w": ev_sg_w[0],
        "ev_sg_b": ev_sg_b[0], "od_conv_w": conv_w, "od_conv_b": conv_b, "od_w_a": od_w_a[0], "od_b_a": b_a,
        "od_w_x": od_w_x[0], "od_b_x": b_x, "od_lam": lam, "ln_g": ln_g, "ln_b": ln_b,
    }

    ada_cols = ada_w.shape[2]
    mod_sh = _modulation(c_all, ada_w, _cols(ada_b, chip * ada_cols, ada_cols).reshape(2, 1, ada_cols))
    mod_all = _allgather8(mod_sh.reshape(16, ada_cols), "gather_mod").reshape(4, 2, 2, 8, ada_cols)[:, 0]
    mod_mine = lax.dynamic_index_in_dim(mod_all, dev, axis=2, keepdims=False)
    mod = mod_mine.transpose(1, 0, 2).reshape(2, 3, D_MODEL)

    posf = positions.astype(F32).reshape(seq, 1)
    loss_local, grad_x, dmod, g = _local_step(x[0], posf, loss_target[0], mod, w_full, seq, ride)

    pad = lambda a, n: jnp.concatenate([a.reshape(-1), jnp.zeros((n - a.size,), F32)])
    rows_small = jnp.concatenate([
        dmod.reshape(6, D_MODEL), g["ln_g"][0:1], g["ln_b"][0:1], g["ln_g"][1:2], g["ln_b"][1:2],
        jnp.concatenate([g["ev_sg_ln_g"], g["ev_sg_ln_b"]]).reshape(1, D_MODEL), g["ev_sg_b"].reshape(1, D_MODEL),
        g["od_conv_w"], g["od_conv_b"].reshape(1, D_MODEL), g["od_b_a"], g["od_b_x"], g["od_lam"],
        pad(g["ev_sink"], D_MODEL).reshape(1, D_MODEL), pad(loss_local, D_MODEL).reshape(1, D_MODEL),
        jnp.zeros((39, D_MODEL), F32)], axis=0)
    parts = [g["ev_w_in_t"].reshape(4, 2, 352, D_MODEL), g["ev_w_out"].reshape(4, 2, 128, D_MODEL),
             g["ev_sg_w"].reshape(4, 2, BLK, BLK), rows_small.reshape(4, 2, 8, D_MODEL)]
    wire = [MXU_DTYPE] * 3 + [F32]
    dmod_blk = jnp.concatenate([dmod.reshape(6, D_MODEL), jnp.zeros((2, D_MODEL), F32)], axis=0)
    *got, dmod_gathered = _reduce_sibling(parts, dmod_blk)
    ev_slots = list(_reduce_chips(_sum_sibling(core, parts, got, wire, "sum_sibling")))
    od_slots = list(g["od_slots"])
    mine = _sum_slots(ev_slots[0:2] + od_slots[0:2] + ev_slots[2:3] + od_slots[2:4] + ev_slots[3:4], "sum_chips")
    reduced = _gather_reduced(mine[:4], mine[4:])
    g_ev_w_in_t = reduced[0].reshape(704, D_MODEL)
    g_ev_w_out = reduced[1].reshape(256, D_MODEL)
    g_od_w_in = reduced[2].reshape(D_MODEL, 512)
    g_od_w_out = reduced[3].reshape(256, D_MODEL)
    g_sg_w = reduced[4].reshape(8 * BLK, BLK)
    g_w_a = reduced[5].reshape(16 * BLK, BLK)
    g_w_x = reduced[6].reshape(16 * BLK, BLK)
    gs = reduced[7].reshape(64, D_MODEL)
    loss = gs[24, 0]
    dmod_all = dmod_gathered[:, 0:6].reshape(8, 2, 3 * D_MODEL)
    dmod_sh = _cols(dmod_all, chip * ada_cols, ada_cols).transpose(1, 0, 2)
    g_ada_w, d_ada_w, nm_ada_w, nv_ada_w = _ada_update(c_all, dmod_sh, ada_w, m_ada_w, v_ada_w)

    big = {}
    d_, nm_, nv_ = _adamw(tr(ev_w_in[0]), g_ev_w_in_t, tr(m_ev_w_in[0]), tr(v_ev_w_in[0]), "adamw_ev_w_in")
    big["ev_w_in"] = tuple(tr(a).reshape(ev_w_in.shape) for a in (g_ev_w_in_t, d_, nm_, nv_))
    for name, w_, g_, m_, v_ in (
            ("ev_w_out", ev_w_out, g_ev_w_out, m_ev_w_out, v_ev_w_out),
            ("od_w_in", od_w_in, g_od_w_in, m_od_w_in, v_od_w_in), ("od_w_out", od_w_out, g_od_w_out, m_od_w_out, v_od_w_out),
            ("ev_sg_w", ev_sg_w, g_sg_w, m_ev_sg_w, v_ev_sg_w), ("od_w_a", od_w_a, g_w_a, m_od_w_a, v_od_w_a),
            ("od_w_x", od_w_x, g_w_x, m_od_w_x, v_od_w_x)):
        two_d = lambda a: a.reshape(g_.shape)
        d_, nm_, nv_ = _adamw(two_d(w_), g_, two_d(m_), two_d(v_), "adamw_" + name)
        big[name] = tuple(a.reshape(w_.shape) for a in (g_, d_, nm_, nv_))
    big["ada_w"] = (g_ada_w, d_ada_w, nm_ada_w, nv_ada_w)

    sh = lambda a: _cols(a, chip * 256, 256)
    small_g = {
        "ada_b": gs[0:6].reshape(2, 3 * D_MODEL), "ln_g": jnp.stack([gs[6], gs[8]]), "ln_b": jnp.stack([gs[7], gs[9]]),
        "ev_sink": gs[23:24, 0:8], "ev_sg_ln_g": gs[10:11, 0:512], "ev_sg_ln_b": gs[10:11, 512:1024],
        "ev_sg_b": gs[11].reshape(8, BLK), "od_conv_w": sh(gs[12:16]), "od_conv_b": sh(gs[16:17]), "od_b_a": sh(gs[17:19]),
        "od_b_x": sh(gs[19:21]), "od_lam": sh(gs[21:23]),
    }
    small_in = {"ada_b": (ada_b, m_ada_b, v_ada_b), "ln_g": (ln_g, m_ln_g, v_ln_g), "ln_b": (ln_b, m_ln_b, v_ln_b),
                "ev_sink": (ev_sink, m_ev_sink, v_ev_sink), "ev_sg_ln_g": (ev_sg_ln_g, m_ev_sg_ln_g, v_ev_sg_ln_g),
                "ev_sg_ln_b": (ev_sg_ln_b, m_ev_sg_ln_b, v_ev_sg_ln_b), "ev_sg_b": (ev_sg_b, m_ev_sg_b, v_ev_sg_b),
                "od_conv_w": (od_conv_w, m_od_conv_w, v_od_conv_w), "od_conv_b": (od_conv_b, m_od_conv_b, v_od_conv_b),
                "od_b_a": (od_b_a, m_od_b_a, v_od_b_a), "od_b_x": (od_b_x, m_od_b_x, v_od_b_x),
                "od_lam": (od_lam, m_od_lam, v_od_lam)}
    names_small = list(small_g)
    upd = _adamw_small([(small_in[n][0].reshape(small_g[n].shape), small_g[n], small_in[n][1].reshape(small_g[n].shape),
                         small_in[n][2].reshape(small_g[n].shape)) for n in names_small])
    res = dict(big)
    for n, (d_, nm_, nv_) in zip(names_small, upd):
        shape = small_in[n][0].shape
        res[n] = tuple(a.reshape(shape) for a in (small_g[n], d_, nm_, nv_))

    order = ["ada_w", "ada_b", "ln_g", "ln_b", "ev_w_in", "ev_w_out", "ev_sink", "ev_sg_ln_g", "ev_sg_ln_b", "ev_sg_w", "ev_sg_b",
             "od_w_in", "od_conv_w", "od_conv_b", "od_w_a", "od_b_a", "od_w_x", "od_b_x", "od_lam", "od_w_out"]
    return (loss, grad_x.reshape(x.shape), *[res[n][0] for n in order], *[res[n][1] for n in order],
            *[res[n][2] for n in order], *[res[n][3] for n in order])
```

```python
import functools

import jax
import jax.numpy as jnp
import numpy as np
from jax import lax
from jax.experimental import pallas as pl
from jax.experimental.pallas import tpu as pltpu

F32 = jnp.float32
BF16 = jnp.bfloat16
MXU_DTYPE = BF16
ACT_DTYPE = MXU_DTYPE

D_MODEL = 1024
HEAD_DIM = 64
N_Q_HEADS = 8
Q_PER_KV = 4
ATTN_WIDTH = 512
KV_WIDTH = 128
BLK = 128
ROPE_DIM = 16
ROPE_THETA = 500000.0
N_SG_GROUPS = 8
SG_WIDTH = 512
EVEN_IN = 2816
ODD_IN = 2048
RNN_HEADS = 8
RG_LRU_C = 8.0
ALPHA = (2 * 2) ** 0.25
LN_EPS = 1e-5
NEG_INF = -1e30
ADAM_LR, ADAM_B1, ADAM_B2, ADAM_EPS, ADAM_WD, ADAM_STEP = 0.001, 0.9, 0.999, 1e-08, 0.01, 10

LANES = 128
VMEM_LIMIT = 56 * 1024 * 1024
MESH = pl.DeviceIdType.MESH


def _mm(a, b):
    return jnp.dot(a.astype(MXU_DTYPE), b.astype(MXU_DTYPE), preferred_element_type=F32)


def _mm_nt(a, b):
    return lax.dot_general(a.astype(MXU_DTYPE), b.astype(MXU_DTYPE), (((1,), (1,)), ((), ())), preferred_element_type=F32)


def _mm_tn(a, b):
    return lax.dot_general(a.astype(MXU_DTYPE), b.astype(MXU_DTYPE), (((0,), (0,)), ((), ())), preferred_element_type=F32)


def _sigmoid(x):
    return 1.0 / (1.0 + jnp.exp(-x))


def _ln_stats(z):
    mu = jnp.mean(z, axis=-1, keepdims=True)
    d = z - mu
    var = jnp.mean(d * d, axis=-1, keepdims=True)
    rstd = lax.rsqrt(var + LN_EPS)
    return d * rstd, rstd


def _ln_bwd(dout, zhat, rstd, g):
    dzh = dout * g
    m1 = jnp.mean(dzh, axis=-1, keepdims=True)
    m2 = jnp.mean(dzh * zhat, axis=-1, keepdims=True)
    return rstd * (dzh - m1 - zhat * m2)


def _group_sum(x, e2):
    hi = x.astype(MXU_DTYPE)
    lo = (x - hi.astype(F32)).astype(MXU_DTYPE)
    return jnp.dot(hi, e2, preferred_element_type=F32) + jnp.dot(lo, e2, preferred_element_type=F32)


def _lane_iota(shape):
    return lax.broadcasted_iota(jnp.int32, shape, 1)


def _to_kv_lanes(t, h):
    src_lo = (h % 2 == 0)
    dst_lo = (h // Q_PER_KV == 0)
    if src_lo != dst_lo:
        t = pltpu.roll(t, HEAD_DIM, 1)
    lane = _lane_iota(t.shape)
    keep = (lane < HEAD_DIM) if dst_lo else (lane >= HEAD_DIM)
    return jnp.where(keep, t, 0.0)


def _from_kv_lanes(t, h):
    src_lo = (h // Q_PER_KV == 0)
    dst_lo = (h % 2 == 0)
    lane = _lane_iota(t.shape)
    keep = (lane < HEAD_DIM) if src_lo else (lane >= HEAD_DIM)
    t = jnp.where(keep, t, 0.0)
    if src_lo != dst_lo:
        t = pltpu.roll(t, HEAD_DIM, 1)
    return t


def _rope(t, cos_t, sin_p, sin_m):
    half = ROPE_DIM // 2
    return t * cos_t + pltpu.roll(t, half, 1) * sin_p + pltpu.roll(t, LANES - half, 1) * sin_m


def _rope_t(d, cos_t, sin_p, sin_m):
    half = ROPE_DIM // 2
    return d * cos_t + pltpu.roll(d * sin_p, LANES - half, 1) + pltpu.roll(d * sin_m, half, 1)


def _band(ref, n, nb):
    prev = jnp.maximum(n - 1, 0)
    nxt = jnp.minimum(n + 1, nb - 1)
    rows = [ref[pl.ds(pl.multiple_of(j * BLK, BLK), BLK), :] for j in (prev, n, nxt)]
    return jnp.concatenate(rows, axis=0)


def _band_bias(n, seq):
    qi = lax.broadcasted_iota(jnp.int32, (BLK, 3 * BLK), 0)
    kj = lax.broadcasted_iota(jnp.int32, (BLK, 3 * BLK), 1)
    k_abs = n * BLK - BLK + kj
    valid = (jnp.abs(kj - BLK - qi) <= BLK) & (k_abs >= 0) & (k_abs < seq)
    bias = jnp.where(valid, 0.0, NEG_INF)
    return jnp.concatenate([bias] * Q_PER_KV, axis=0)


def _stack_heads(tile_of, kv):
    return jnp.concatenate([_to_kv_lanes(tile_of(h // 2), h) for h in range(Q_PER_KV * kv, Q_PER_KV * (kv + 1))], axis=0)


def _per_head_column(vals):
    row = lax.broadcasted_iota(jnp.int32, (Q_PER_KV * BLK, 1), 0)
    return jnp.where(row < BLK, vals[0], jnp.where(row < 2 * BLK, vals[1], jnp.where(row < 3 * BLK, vals[2], vals[3])))


def _softplus_neg(lam):
    e = jnp.exp(-jnp.abs(lam))
    u = 1.0 + e
    log1p_e = jnp.where(u == 1.0, e, jnp.log(u) * (e / (u - 1.0)))
    sp = jnp.maximum(-lam, 0.0) + log1p_e
    dsp = -1.0 / (1.0 + jnp.exp(lam))
    return sp, dsp


def _full(shape):
    return pl.BlockSpec(shape, lambda *_: (0,) * len(shape))


def _const(shape):
    return pl.BlockSpec(shape, lambda *_: (0,) * len(shape), pipeline_mode=pl.Buffered(1))


def _rows(tm, n):
    return pl.BlockSpec((tm, n), lambda i: (i, 0))


def _params(*sem):
    return pltpu.CompilerParams(dimension_semantics=sem, vmem_limit_bytes=VMEM_LIMIT)


def _sds(shape, dtype=F32):
    return jax.ShapeDtypeStruct(shape, dtype)


def _place():
    return lax.axis_index("x"), lax.axis_index("y"), lax.axis_index("c")


class _Rider:
    def __init__(self, ins, out_shapes, n_remote, n_local, plan):
        self.ins, self.out_shapes, self.n_remote, self.n_local, self.plan = list(ins), list(out_shapes), n_remote, n_local, plan

    def scratch(self):
        return [pltpu.SemaphoreType.DMA((self.n_remote,)), pltpu.SemaphoreType.DMA((self.n_remote,)),
                pltpu.SemaphoreType.DMA((max(self.n_local, 1),))]

    def run(self, first, in_refs, out_refs, sems):
        send_sems, recv_sems, local_sems = sems
        sends, recvs, locals_ = self.plan(in_refs, out_refs)
        remote = lambda k, src, dst, to: pltpu.make_async_remote_copy(
            src_ref=src, dst_ref=dst, send_sem=send_sems.at[k], recv_sem=recv_sems.at[k], device_id=to, device_id_type=MESH)
        if first:
            for k, src, dst, to in sends:
                remote(k, src, dst, to).start()
            for j, (src, dst) in enumerate(locals_):
                pltpu.make_async_copy(src, dst, local_sems.at[j]).start()
        else:
            for k, dst, frm in recvs:
                remote(k, dst, dst, frm).wait_recv()
            for k, src, dst, to in sends:
                remote(k, src, dst, to).wait_send()
            for j, (src, dst) in enumerate(locals_):
                pltpu.make_async_copy(src, dst, local_sems.at[j]).wait()


def _other_chips(x, y):
    return [(1 - x, y), (x, 1 - y), (1 - x, 1 - y)]


def _gather_rider(shard):
    hr = shard.shape[0] // 2

    def plan(ins, outs):
        x, y, c = _place()
        mine, src, dst = 2 * x + y, ins[0], outs[0]
        sends, recvs = [], []
        for j, (px, py) in enumerate(_other_chips(x, y)):
            for flip in range(2):
                tc = c if flip == 0 else 1 - c
                sends.append((2 * j + flip, src.at[pl.ds(c * hr, hr)], dst.at[mine, pl.ds(c * hr, hr)], (px, py, tc)))
                recvs.append((2 * j + flip, dst.at[2 * px + py, pl.ds(tc * hr, hr)], (px, py, tc)))
        return sends, recvs, [(src, dst.at[mine])]

    return _Rider([shard], [_sds((4,) + shard.shape, shard.dtype)], 6, 1, plan)


def _sibling_swap_rider(parts):
    n = len(parts)

    def plan(ins, outs):
        x, y, c = _place()
        sibling = (x, y, 1 - c)
        return ([(i, ins[i].at[:, 1 - c], outs[i], sibling) for i in range(n)], [(i, outs[i], sibling) for i in range(n)], [])

    return _Rider(parts, [_sds((4,) + p.shape[2:], p.dtype) for p in parts], n, 0, plan)


def _chip_exchange_rider(parts):
    n = len(parts)

    def plan(ins, outs):
        x, y, c = _place()
        mine = 2 * x + y
        sends, recvs = [], []
        for i in range(n):
            for j, (px, py) in enumerate(_other_chips(x, y)):
                sends.append((3 * i + j, ins[i].at[2 * px + py], outs[i].at[mine], (px, py, c)))
                recvs.append((3 * i + j, outs[i].at[2 * px + py], (px, py, c)))
        return sends, recvs, [(ins[i].at[mine], outs[i].at[mine]) for i in range(n)]

    return _Rider(parts, [_sds(p.shape, p.dtype) for p in parts], 3 * n, n, plan)


def _call(body, name, grid, in_specs, out_specs, out_shape, args, sem, scratch=(), rider=None):
    if rider is None:
        return list(pl.pallas_call(body, name=name, grid=grid, in_specs=in_specs, out_specs=out_specs, out_shape=out_shape,
                                   scratch_shapes=list(scratch), compiler_params=_params(sem))(*args)), []
    n_in, n_out, n_scr = len(in_specs), len(out_specs), len(scratch)
    r_in, r_out = len(rider.ins), len(rider.out_shapes)
    steps = grid[0]

    def riding(*refs):
        ins, r_ins = refs[:n_in], refs[n_in:n_in + r_in]
        outs = refs[n_in + r_in:n_in + r_in + n_out]
        r_outs = refs[n_in + r_in + n_out:n_in + r_in + n_out + r_out]
        scr = refs[n_in + r_in + n_out + r_out:n_in + r_in + n_out + r_out + n_scr]
        sems = refs[n_in + r_in + n_out + r_out + n_scr:]

        @pl.when(pl.program_id(0) == 0)
        def _():
            rider.run(True, r_ins, r_outs, sems)

        body(*ins, *outs, *scr)

        @pl.when(pl.program_id(0) == steps - 1)
        def _():
            rider.run(False, r_ins, r_outs, sems)

    hbm = pl.BlockSpec(memory_space=pl.ANY)
    res = pl.pallas_call(
        riding, name=name, grid=grid, in_specs=list(in_specs) + [hbm] * r_in, out_specs=list(out_specs) + [hbm] * r_out,
        out_shape=list(out_shape) + rider.out_shapes, scratch_shapes=list(scratch) + rider.scratch(),
        compiler_params=_params("arbitrary"),
    )(*args, *rider.ins)
    return list(res[:n_out]), list(res[n_out:])


def _row_tile(seq, want):
    return want if seq % want == 0 else seq


def _rope_tables(posf, seq):
    half = ROPE_DIM // 2
    inv_freq = np.power(np.float32(ROPE_THETA), -np.arange(half, dtype=np.float32) / np.float32(half)).astype(np.float32)
    j = np.arange(LANES) % HEAD_DIM
    invf = jnp.asarray(np.where(j < ROPE_DIM, inv_freq[j % half], 0.0).astype(np.float32).reshape(1, LANES))
    m_p = jnp.asarray(((j >= half) & (j < ROPE_DIM)).astype(np.float32).reshape(1, LANES))
    m_m = jnp.asarray(-(j < half).astype(np.float32).reshape(1, LANES))
    tm = _row_tile(seq, 512)

    def body(pos_ref, invf_ref, mp_ref, mm_ref, cos_ref, sp_ref, sm_ref):
        ang = pos_ref[...] * invf_ref[...]
        s = jnp.sin(ang)
        cos_ref[...] = jnp.cos(ang)
        sp_ref[...] = s * mp_ref[...]
        sm_ref[...] = s * mm_ref[...]

    return pl.pallas_call(
        body, name="rope_tables", grid=(seq // tm,),
        in_specs=[_rows(tm, 1), _full((1, LANES)), _full((1, LANES)), _full((1, LANES))],
        out_specs=[_rows(tm, LANES)] * 3, out_shape=[_sds((seq, LANES))] * 3,
        compiler_params=_params("parallel"),
    )(posf, invf, m_p, m_m)


def _even_proj(x, mod, w_in_t, tabs, seq, rider=None):
    tm = _row_tile(seq, 512)

    def body(x_ref, mod_ref, w_ref, cos_ref, sp_ref, sm_ref, q_ref, k_ref, v_ref, su_ref, sv_ref, g_ref):
        h = x_ref[...] * (1.0 + mod_ref[1:2, :]) + mod_ref[0:1, :]
        p = _mm_nt(h, w_ref[...])
        cos_t, sin_p, sin_m = cos_ref[...], sp_ref[...], sm_ref[...]
        for j in range(ATTN_WIDTH // LANES):
            q_ref[:, j * LANES:(j + 1) * LANES] = _rope(p[:, j * LANES:(j + 1) * LANES], cos_t, sin_p, sin_m).astype(q_ref.dtype)
        k_ref[...] = _rope(p[:, 512:640], cos_t, sin_p, sin_m).astype(k_ref.dtype)
        v_ref[...] = p[:, 640:768].astype(v_ref.dtype)
        su_ref[...] = p[:, 768:1280].astype(su_ref.dtype)
        sv_ref[...] = p[:, 1280:1792].astype(sv_ref.dtype)
        g_ref[...] = p[:, 1792:2816].astype(g_ref.dtype)

    return _call(
        body, "even_proj", (seq // tm,),
        [_rows(tm, D_MODEL), _full((3, D_MODEL)), _const((EVEN_IN, D_MODEL))] + [_rows(tm, LANES)] * 3,
        [_rows(tm, 512), _rows(tm, LANES), _rows(tm, LANES), _rows(tm, 512), _rows(tm, 512), _rows(tm, D_MODEL)],
        [_sds((seq, 512), MXU_DTYPE), _sds((seq, LANES), MXU_DTYPE), _sds((seq, LANES), MXU_DTYPE), _sds((seq, 512), ACT_DTYPE),
         _sds((seq, 512), ACT_DTYPE), _sds((seq, D_MODEL), ACT_DTYPE)],
        (x, mod, w_in_t, *tabs), "parallel", rider=rider)


def _sg_forward(sv, lng, lnb, sgw_ref, sgb, e2):
    vn, vhat, rstd, svo = [], [], [], []
    for j in range(SG_WIDTH // LANES):
        t = sv[:, j * LANES:(j + 1) * LANES]
        mu = _group_sum(t, e2) * (1.0 / HEAD_DIM)
        d = t - mu
        var = _group_sum(d * d, e2) * (1.0 / HEAD_DIM)
        r = lax.rsqrt(var + LN_EPS)
        vh = d * r
        vhat.append(vh)
        rstd.append(r)
        vn.append(vh * lng[:, j * LANES:(j + 1) * LANES] + lnb[:, j * LANES:(j + 1) * LANES])
    lane = _lane_iota((BLK, LANES))
    for j in range(SG_WIDTH // LANES):
        lo = _mm(sgw_ref[2 * j], vn[j])
        hi = _mm(sgw_ref[2 * j + 1], vn[j])
        svo.append(jnp.where(lane < HEAD_DIM, lo, hi) + sgb[:, j * LANES:(j + 1) * LANES])
    return svo, vn, vhat, rstd


def _even_mix(q, k, v, su, sv, sink, sgln_g, sgln_b, sgw, sgb_full, e2, seq, rider=None):
    nb = seq // BLK

    def body(sink_ref, q_ref, k_ref, v_ref, su_ref, sv_ref, lng_ref, lnb_ref, sgw_ref, sgb_ref, e2_ref, ycat_ref, lse_ref):
        n = pl.program_id(0)
        kband = _band(k_ref, n, nb)
        vband = _band(v_ref, n, nb)
        bias = _band_bias(n, seq)
        lane = _lane_iota((BLK, LANES))
        lse = jnp.zeros((BLK, LANES), F32)
        q_tile = lambda j: q_ref[:, j * LANES:(j + 1) * LANES].astype(F32)
        acc = [jnp.zeros((BLK, LANES), F32) for _ in range(ATTN_WIDTH // LANES)]
        for kv in range(N_Q_HEADS // Q_PER_KV):
            heads = range(Q_PER_KV * kv, Q_PER_KV * (kv + 1))
            sink = _per_head_column([sink_ref[h] for h in heads])
            s = _mm_nt(_stack_heads(q_tile, kv), kband) * (HEAD_DIM ** -0.5) + bias
            m = jnp.maximum(jnp.max(s, axis=1, keepdims=True), sink)
            p = jnp.exp(s - m)
            denom = jnp.sum(p, axis=1, keepdims=True) + jnp.exp(sink - m)
            o4 = _mm(p / denom, vband)
            l4 = m + jnp.log(denom)
            for g, h in enumerate(heads):
                acc[h // 2] = acc[h // 2] + _from_kv_lanes(o4[g * BLK:(g + 1) * BLK], h)
                lse = jnp.where(lane == h, l4[g * BLK:(g + 1) * BLK], lse)
        for j in range(ATTN_WIDTH // LANES):
            ycat_ref[:, j * LANES:(j + 1) * LANES] = acc[j].astype(ycat_ref.dtype)
        lse_ref[...] = lse
        svo, _, _, _ = _sg_forward(sv_ref[...].astype(F32), lng_ref[...], lnb_ref[...], sgw_ref, sgb_ref[...], e2_ref[...])
        for j in range(SG_WIDTH // LANES):
            ysg = su_ref[:, j * LANES:(j + 1) * LANES].astype(F32) * svo[j]
            ycat_ref[:, ATTN_WIDTH + j * LANES:ATTN_WIDTH + (j + 1) * LANES] = ysg.astype(ycat_ref.dtype)

    blk = lambda w: pl.BlockSpec((BLK, w), lambda n: (n, 0))
    return _call(
        body, "even_mix", (nb,),
        [pl.BlockSpec(memory_space=pltpu.SMEM), blk(512), _full((seq, LANES)), _full((seq, LANES)), blk(512), blk(512),
         _full((1, 512)), _full((1, 512)), _full((8, BLK, BLK)), _full((BLK, 512)), _full((LANES, LANES))],
        [blk(D_MODEL), blk(LANES)], [_sds((seq, D_MODEL), ACT_DTYPE), _sds((seq, LANES))],
        (sink, q, k, v, su, sv, sgln_g, sgln_b, sgw, sgb_full, e2), "parallel", rider=rider)


def _even_out(ycat, g, x, mod, w_out, ln_g, ln_b, seq, rider=None):
    tm = _row_tile(seq, 512)

    def body(y_ref, g_ref, x_ref, mod_ref, wo_ref, g1_ref, b1_ref, z_ref, x1_ref):
        gg = g_ref[...].astype(F32)
        out = _mm(y_ref[...].astype(F32) * (gg * _sigmoid(gg)), wo_ref[...])
        z = ALPHA * x_ref[...] + mod_ref[2:3, :] * out
        z_ref[...] = z
        zhat, _ = _ln_stats(z)
        x1_ref[...] = zhat * g1_ref[...] + b1_ref[...]

    return _call(
        body, "even_out", (seq // tm,),
        [_rows(tm, D_MODEL)] * 3 + [_full((3, D_MODEL)), _const((D_MODEL, D_MODEL)), _full((1, D_MODEL)), _full((1, D_MODEL))],
        [_rows(tm, D_MODEL)] * 2, [_sds((seq, D_MODEL))] * 2, (ycat, g, x, mod, w_out, ln_g, ln_b), "parallel", rider=rider)


def _odd_proj(x1, mod, w_in4, seq):
    tm = _row_tile(seq, 512)
    cs = ODD_IN // 4

    def body(x_ref, mod_ref, w_ref, xr_ref, g_ref):
        h = x_ref[...] * (1.0 + mod_ref[1:2, :]) + mod_ref[0:1, :]
        hb = h.astype(MXU_DTYPE)
        for s in range(2):
            xr_ref[:, s * cs:(s + 1) * cs] = jnp.dot(hb, w_ref[s], preferred_element_type=F32)
            g_ref[:, s * cs:(s + 1) * cs] = jnp.dot(hb, w_ref[2 + s], preferred_element_type=F32).astype(g_ref.dtype)

    return pl.pallas_call(
        body, name="odd_proj", grid=(seq // tm,),
        in_specs=[_rows(tm, D_MODEL), _full((3, D_MODEL)), _full((4, D_MODEL, cs))],
        out_specs=[_rows(tm, D_MODEL)] * 2,
        out_shape=[_sds((seq, D_MODEL)), _sds((seq, D_MODEL), ACT_DTYPE)],
        compiler_params=_params("parallel"),
    )(x1, mod, w_in4)


def _halo_specs(tm, seq, width):
    per = tm // 8
    last = seq // 8 - 1
    return [pl.BlockSpec((8, width), lambda i: (jnp.maximum(i * per - 1, 0), 0)),
            pl.BlockSpec((tm, width), lambda i: (i, 0)),
            pl.BlockSpec((8, width), lambda i: (jnp.minimum((i + 1) * per, last), 0))]


def _extended(prev_ref, main_ref, next_ref, i, n_steps):
    prev = jnp.where(i > 0, prev_ref[...], 0.0)
    nxt = jnp.where(i < n_steps - 1, next_ref[...], 0.0)
    return jnp.concatenate([prev, main_ref[...], nxt], axis=0)


def _shifted(ext, off, tm):
    if off == 0:
        return ext[8:8 + tm]
    return pltpu.roll(ext, (-off) % ext.shape[0], 0)[8:8 + tm]


def _lru_gates(xh, pre, bias, sp, hs):
    res = []
    for d in range(2):
        r = _sigmoid(pre[:, (2 * d) * LANES:(2 * d + 1) * LANES] + bias[2 * d:2 * d + 1, hs])
        ig = _sigmoid(pre[:, (2 * d + 1) * LANES:(2 * d + 2) * LANES] + bias[2 * d + 1:2 * d + 2, hs])
        neg_log_a = RG_LRU_C * r * sp[d:d + 1, hs]
        a = jnp.exp(-neg_log_a)
        s = jnp.sqrt(jnp.tanh(neg_log_a) * (a * a + 1.0))
        res.append((r, ig, a, s))
    return res


def _odd_gates(xr, conv_w, conv_b, wcat, bias, lam, seq):
    tm = _row_tile(seq, 512)
    steps = seq // tm

    def body(xp_ref, xm_ref, xn_ref, cw_ref, cb_ref, w_ref, bias_ref, lam_ref, xc_ref, af_ref, bf_ref, ar_ref, br_ref):
        i = pl.program_id(0)
        ext = _extended(xp_ref, xm_ref, xn_ref, i, steps)
        xc = cb_ref[...] + sum(cw_ref[kk:kk + 1, :] * _shifted(ext, kk - 2, tm) for kk in range(4))
        xc_ref[...] = xc
        sp, _ = _softplus_neg(lam_ref[...])
        bias = bias_ref[...]
        for h in range(RNN_HEADS):
            hs = slice(h * LANES, (h + 1) * LANES)
            xh = xc[:, hs]
            (_, i0, a0, s0), (_, i1, a1, s1) = _lru_gates(xh, _mm(xh, w_ref[h]), bias, sp, hs)
            af_ref[:, hs] = a0
            bf_ref[:, hs] = s0 * i0 * xh
            ar_ref[:, hs] = a1
            br_ref[:, hs] = s1 * i1 * xh

    return pl.pallas_call(
        body, name="odd_gates", grid=(steps,),
        in_specs=_halo_specs(tm, seq, D_MODEL) + [_full((4, D_MODEL)), _full((1, D_MODEL)), _full((8, LANES, 512)),
                                                  _full((4, D_MODEL)), _full((2, D_MODEL))],
        out_specs=[_rows(tm, D_MODEL)] * 5, out_shape=[_sds((seq, D_MODEL))] * 5,
        compiler_params=_params("parallel"),
    )(xr, xr, xr, conv_w, conv_b, wcat, bias, lam)


def _scan(a, b, seq, descending, post, name):
    tb = _row_tile(seq, 512)
    steps = seq // tb
    imap = (lambda i: (steps - 1 - i, 0)) if descending else (lambda i: (i, 0))
    spec = pl.BlockSpec((tb, D_MODEL), imap)
    n_out = 1 if post else 2

    sub = 8
    tiles = tb // sub

    def body(a_ref, b_ref, *rest):
        outs, carry_h, carry_a = rest[:n_out], rest[n_out], rest[n_out + 1]

        @pl.when(pl.program_id(0) == 0)
        def _():
            carry_h[...] = jnp.zeros_like(carry_h)
            carry_a[...] = jnp.zeros_like(carry_a)

        row = lax.broadcasted_iota(jnp.int32, (sub, D_MODEL), 0)

        def shift(v, d, fill):
            if descending:
                return jnp.where(row <= sub - 1 - d, pltpu.roll(v, sub - d, 0), fill)
            return jnp.where(row >= d, pltpu.roll(v, d, 0), fill)

        def last(v):
            return jnp.broadcast_to(v[0:1, :] if descending else v[sub - 1:sub, :], v.shape)

        def tile(j, c):
            ch, ca = c
            r0 = pl.multiple_of(((tiles - 1 - j) if descending else j) * sub, sub)
            at = a_ref[pl.ds(r0, sub), :]
            bt = b_ref[pl.ds(r0, sub), :]
            coef = shift(at, 1, ca) if post else at
            acc_a, acc_b = coef, bt
            for d in (1, 2, 4):
                acc_b = acc_b + acc_a * shift(acc_b, d, 0.0)
                acc_a = acc_a * shift(acc_a, d, 1.0)
            h = acc_b + acc_a * ch
            outs[0][pl.ds(r0, sub), :] = h
            if post:
                return last(h), last(at)
            outs[1][pl.ds(r0, sub), :] = shift(h, 1, ch)
            return last(h), ca

        ch, ca = lax.fori_loop(0, tiles, tile, (carry_h[...], carry_a[...]), unroll=4)
        carry_h[...] = ch
        carry_a[...] = ca

    return pl.pallas_call(
        body, name=name, grid=(steps,), in_specs=[spec, spec], out_specs=[spec] * n_out,
        out_shape=[_sds((seq, D_MODEL))] * n_out, scratch_shapes=[pltpu.VMEM((sub, D_MODEL), F32)] * 2,
        compiler_params=_params("arbitrary"),
    )(a, b)


def _odd_out_and_loss(hf, hr, g, x1, tgt, mod, w_out, ln_g, ln_b, seq):
    tm = _row_tile(seq, 512)

    def body(hf_ref, hr_ref, g_ref, x_ref, t_ref, mod_ref, w_ref, lg_ref, lb_ref,
             dhs_ref, dg_ref, dres_ref, loss_ref, dw_ref, vec_ref):
        @pl.when(pl.program_id(0) == 0)
        def _():
            loss_ref[...] = jnp.zeros_like(loss_ref)
            dw_ref[...] = jnp.zeros_like(dw_ref)
            vec_ref[...] = jnp.zeros_like(vec_ref)

        gg = g_ref[...].astype(F32)
        sg = _sigmoid(gg)
        silu = gg * sg
        hsum = hf_ref[...] + hr_ref[...]
        y = hsum * silu
        out = _mm(y, w_ref[...])
        gate = mod_ref[2:3, :]
        z = ALPHA * x_ref[...] + gate * out
        zhat, rstd = _ln_stats(z)
        x2 = zhat * lg_ref[...] + lb_ref[...]
        err = x2 - t_ref[...]
        loss_ref[...] += 0.5 * jnp.sum(jnp.mean(err * err, axis=-1, keepdims=True))
        dx2 = err * (1.0 / D_MODEL)
        dz = _ln_bwd(dx2, zhat, rstd, lg_ref[...])
        vec_ref[0:1, :] += jnp.sum(dx2 * zhat, axis=0, keepdims=True)
        vec_ref[1:2, :] += jnp.sum(dx2, axis=0, keepdims=True)
        vec_ref[2:3, :] += jnp.sum(dz * out, axis=0, keepdims=True)
        dres_ref[...] = ALPHA * dz
        dout = gate * dz
        dw_ref[...] += _mm_tn(y, dout)
        dy = _mm_nt(dout, w_ref[...])
        dhs_ref[...] = dy * silu
        dg_ref[...] = (dy * hsum * (sg * (1.0 + gg * (1.0 - sg)))).astype(dg_ref.dtype)

    return pl.pallas_call(
        body, name="odd_out_loss", grid=(seq // tm,),
        in_specs=[_rows(tm, D_MODEL)] * 5 + [_full((3, D_MODEL)), _const((D_MODEL, D_MODEL)),
                                             _full((1, D_MODEL)), _full((1, D_MODEL))],
        out_specs=[_rows(tm, D_MODEL)] * 3 + [_full((8, LANES)), _full((D_MODEL, D_MODEL)), _full((8, D_MODEL))],
        out_shape=[_sds((seq, D_MODEL)), _sds((seq, D_MODEL), ACT_DTYPE), _sds((seq, D_MODEL)), _sds((8, LANES)),
                   _sds((D_MODEL, D_MODEL)), _sds((8, D_MODEL))],
        compiler_params=_params("arbitrary"),
    )(hf, hr, g, x1, tgt, mod, w_out, ln_g, ln_b)


def _odd_gates_bwd(xc, gf, gr, hpf, hpr, wcat, bias, lam, seq):
    tm = _row_tile(seq, 512)
    steps = seq // tm

    def body(xc_ref, gf_ref, gr_ref, hpf_ref, hpr_ref, w_ref, bias_ref, lam_ref, dxc_ref, dw_ref, vec_ref):
        @pl.when(pl.program_id(0) == 0)
        def _():
            dw_ref[...] = jnp.zeros_like(dw_ref)
            vec_ref[...] = jnp.zeros_like(vec_ref)

        sp, dsp = _softplus_neg(lam_ref[...])
        bias = bias_ref[...]
        for h in range(RNN_HEADS):
            hs = slice(h * LANES, (h + 1) * LANES)
            xh = xc_ref[:, hs]
            gates = _lru_gates(xh, _mm(xh, w_ref[h]), bias, sp, hs)
            dxh = jnp.zeros_like(xh)
            dpre = []
            for d, (g_ref_d, hp_ref_d) in enumerate(((gf_ref, hpf_ref), (gr_ref, hpr_ref))):
                r, ig, a, s = gates[d]
                db = g_ref_d[:, hs]
                da = db * hp_ref_d[:, hs]
                dxh = dxh + db * s * ig
                dlog_a = da * a - (db * ig * xh) * (a * a / s)
                dr = dlog_a * (-RG_LRU_C) * sp[d:d + 1, hs]
                di = db * s * xh
                dpr = dr * r * (1.0 - r)
                dpi = di * ig * (1.0 - ig)
                vec_ref[2 * d:2 * d + 1, hs] += jnp.sum(dpr, axis=0, keepdims=True)
                vec_ref[2 * d + 1:2 * d + 2, hs] += jnp.sum(dpi, axis=0, keepdims=True)
                vec_ref[4 + d:5 + d, hs] += jnp.sum(dlog_a * r, axis=0, keepdims=True) * (-RG_LRU_C) * dsp[d:d + 1, hs]
                dpre += [dpr, dpi]
            dcat = jnp.concatenate(dpre, axis=1)
            dw_ref[h] += _mm_tn(xh, dcat)
            dxc_ref[:, hs] = dxh + _mm_nt(dcat, w_ref[h])

    return pl.pallas_call(
        body, name="odd_gates_bwd", grid=(steps,),
        in_specs=[_rows(tm, D_MODEL)] * 5 + [_full((8, LANES, 512)), _full((4, D_MODEL)), _full((2, D_MODEL))],
        out_specs=[_rows(tm, D_MODEL), _full((8, LANES, 512)), _full((8, D_MODEL))],
        out_shape=[_sds((seq, D_MODEL)), _sds((8, LANES, 512)), _sds((8, D_MODEL))],
        compiler_params=_params("arbitrary"),
    )(xc, gf, gr, hpf, hpr, wcat, bias, lam)


def _odd_proj_bwd(dxc, xr, dg, x1, dres, mod, conv_w, w_in4, seq):
    tm = _row_tile(seq, 512)
    steps = seq // tm

    def body(dp_ref, dm_ref, dn_ref, xp_ref, xm_ref, xn_ref, dg_ref, x_ref, dres_ref, mod_ref, cw_ref, w_ref,
             dx_ref, dw_ref, vec_ref, dpb_ref):
        i = pl.program_id(0)

        @pl.when(i == 0)
        def _():
            vec_ref[...] = jnp.zeros_like(vec_ref)
            dw_ref[...] = jnp.zeros_like(dw_ref)

        dext = _extended(dp_ref, dm_ref, dn_ref, i, steps)
        xext = _extended(xp_ref, xm_ref, xn_ref, i, steps)
        dxc_m = dm_ref[...]
        dxr = sum(cw_ref[kk:kk + 1, :] * _shifted(dext, 2 - kk, tm) for kk in range(4))
        for kk in range(4):
            vec_ref[kk:kk + 1, :] += jnp.sum(dxc_m * _shifted(xext, kk - 2, tm), axis=0, keepdims=True)
        vec_ref[4:5, :] += jnp.sum(dxc_m, axis=0, keepdims=True)
        dpb_ref[:, :D_MODEL] = dxr.astype(dpb_ref.dtype)
        dpb_ref[:, D_MODEL:] = dg_ref[...].astype(dpb_ref.dtype)
        cs = ODD_IN // 4
        dh = sum(_mm_nt(dpb_ref[:, s * cs:(s + 1) * cs], w_ref[s]) for s in range(4))
        x = x_ref[...]
        h_t = (x * (1.0 + mod_ref[1:2, :]) + mod_ref[0:1, :]).T.astype(MXU_DTYPE)
        for s in range(4):
            dw_ref[s] += jnp.dot(h_t, dpb_ref[:, s * cs:(s + 1) * cs], preferred_element_type=F32)
        vec_ref[5:6, :] += jnp.sum(dh, axis=0, keepdims=True)
        vec_ref[6:7, :] += jnp.sum(dh * x, axis=0, keepdims=True)
        dx_ref[...] = dres_ref[...] + dh * (1.0 + mod_ref[1:2, :])

    return pl.pallas_call(
        body, name="odd_proj_bwd", grid=(steps,),
        in_specs=_halo_specs(tm, seq, D_MODEL) + _halo_specs(tm, seq, D_MODEL) + [_rows(tm, D_MODEL)] * 3
        + [_full((3, D_MODEL)), _full((4, D_MODEL)), _const((4, D_MODEL, ODD_IN // 4))],
        out_specs=[_rows(tm, D_MODEL), _const((4, D_MODEL, ODD_IN // 4)), _full((8, D_MODEL))],
        out_shape=[_sds((seq, D_MODEL)), _sds((4, D_MODEL, ODD_IN // 4)), _sds((8, D_MODEL))],
        scratch_shapes=[pltpu.VMEM((tm, ODD_IN), MXU_DTYPE)],
        compiler_params=_params("arbitrary"),
    )(dxc, dxc, dxc, xr, xr, xr, dg, x1, dres, mod, conv_w, w_in4)


def _even_out_bwd(dx1, z, ycat, g, mod, ln_g, w_out, seq, rider=None):
    tm = _row_tile(seq, 512)
    steps = seq // tm

    def body(dx_ref, z_ref, y_ref, g_ref, mod_ref, lg_ref, w_ref, dy_ref, dg_ref, dres_ref, dw_ref, vec_ref):
        i = pl.program_id(0)

        @pl.when(i == 0)
        def _():
            dw_ref[...] = jnp.zeros_like(dw_ref)
            vec_ref[...] = jnp.zeros_like(vec_ref)

        zhat, rstd = _ln_stats(z_ref[...])
        dx1_ = dx_ref[...]
        dz = _ln_bwd(dx1_, zhat, rstd, lg_ref[...])
        vec_ref[0:1, :] += jnp.sum(dx1_ * zhat, axis=0, keepdims=True)
        vec_ref[1:2, :] += jnp.sum(dx1_, axis=0, keepdims=True)
        dres_ref[...] = ALPHA * dz
        gate = mod_ref[2:3, :]
        gg = g_ref[...].astype(F32)
        sg = _sigmoid(gg)
        silu = gg * sg
        ycat_ = y_ref[...].astype(F32)
        dw_ref[...] += _mm_tn(ycat_ * silu, dz)
        dy = _mm_nt(gate * dz, w_ref[...])
        dy_ref[...] = (dy * silu).astype(dy_ref.dtype)
        dg_ref[...] = (dy * ycat_ * (sg * (1.0 + gg * (1.0 - sg)))).astype(dg_ref.dtype)

        @pl.when(i == steps - 1)
        def _():
            m_acc = dw_ref[...]
            vec_ref[2:3, :] = jnp.sum(w_ref[...].astype(F32) * m_acc, axis=0, keepdims=True)
            dw_ref[...] = m_acc * gate

    return _call(
        body, "even_out_bwd", (steps,),
        [_rows(tm, D_MODEL)] * 4 + [_full((3, D_MODEL)), _full((1, D_MODEL)), _const((D_MODEL, D_MODEL))],
        [_rows(tm, D_MODEL)] * 3 + [_full((D_MODEL, D_MODEL)), _full((8, D_MODEL))],
        [_sds((seq, D_MODEL), ACT_DTYPE), _sds((seq, D_MODEL), ACT_DTYPE), _sds((seq, D_MODEL)), _sds((D_MODEL, D_MODEL)),
         _sds((8, D_MODEL))],
        (dx1, z, ycat, g, mod, ln_g, w_out), "arbitrary", rider=rider)


def _even_mix_bwd(q, k, v, lse, ycat, dycat, su, sv, sink, sgln_g, sgln_b, sgw, sgb_full, e2, e8, seq, rider=None):
    nb = seq // BLK

    def body(sink_ref, q_ref, k_ref, v_ref, lse_ref, y_ref, dy_ref, su_ref, sv_ref, lng_ref, lnb_ref, sgw_ref, sgb_ref, e2_ref,
             e8_ref, dq_ref, dsu_ref, dsv_ref, dk_ref, dv_ref, dsgw_ref, dsgb_ref, vec_ref, dsink_ref, dsgb_acc):
        n = pl.program_id(0)

        @pl.when(n == 0)
        def _():
            dk_ref[...] = jnp.zeros_like(dk_ref)
            dv_ref[...] = jnp.zeros_like(dv_ref)
            dsgw_ref[...] = jnp.zeros_like(dsgw_ref)
            dsgb_acc[...] = jnp.zeros_like(dsgb_acc)
            vec_ref[...] = jnp.zeros_like(vec_ref)
            dsink_ref[...] = jnp.zeros_like(dsink_ref)

        kband = _band(k_ref, n, nb)
        vband = _band(v_ref, n, nb)
        bias = _band_bias(n, seq)
        lane = _lane_iota((BLK, LANES))
        row8 = lax.broadcasted_iota(jnp.int32, (8, LANES), 0)
        lse = lse_ref[...]
        dkb = jnp.zeros((LANES, 3 * BLK), F32)
        dvb = jnp.zeros((LANES, 3 * BLK), F32)
        dsink = jnp.zeros((8, LANES), F32)
        q_tile = lambda j: q_ref[:, j * LANES:(j + 1) * LANES].astype(F32)
        do_tile = lambda j: dy_ref[:, j * LANES:(j + 1) * LANES].astype(F32)
        dq = [jnp.zeros((BLK, LANES), F32) for _ in range(ATTN_WIDTH // LANES)]
        for kv in range(N_Q_HEADS // Q_PER_KV):
            heads = range(Q_PER_KV * kv, Q_PER_KV * (kv + 1))
            lse4, delta4 = [], []
            for h in heads:
                head_lanes = (lane < HEAD_DIM) if h % 2 == 0 else (lane >= HEAD_DIM)
                lse4.append(jnp.sum(jnp.where(lane == h, lse, 0.0), axis=1, keepdims=True))
                o_tile = y_ref[:, (h // 2) * LANES:(h // 2 + 1) * LANES].astype(F32)
                delta4.append(jnp.sum(jnp.where(head_lanes, do_tile(h // 2) * o_tile, 0.0), axis=1, keepdims=True))
            lse4, delta4 = jnp.concatenate(lse4, axis=0), jnp.concatenate(delta4, axis=0)
            q4, do4 = _stack_heads(q_tile, kv), _stack_heads(do_tile, kv)
            s = _mm_nt(q4, kband) * (HEAD_DIM ** -0.5) + bias
            p = jnp.exp(s - lse4)
            wsink = jnp.exp(_per_head_column([sink_ref[h] for h in heads]) - lse4) * delta4
            ds = p * (_mm_nt(do4, vband) - delta4) * (HEAD_DIM ** -0.5)
            dq4 = _mm(ds, kband)
            dkb = dkb + _mm_tn(q4, ds)
            dvb = dvb + _mm_tn(do4, p)
            for g, h in enumerate(heads):
                dq[h // 2] = dq[h // 2] + _from_kv_lanes(dq4[g * BLK:(g + 1) * BLK], h)
                dsink = dsink + jnp.where(row8 == h, -jnp.sum(wsink[g * BLK:(g + 1) * BLK]), 0.0)
        for j in range(ATTN_WIDTH // LANES):
            dq_ref[:, j * LANES:(j + 1) * LANES] = dq[j].astype(dq_ref.dtype)
        dsink_ref[...] += dsink
        prev = jnp.maximum(n - 1, 0)
        nxt = jnp.minimum(n + 1, nb - 1)
        for part, blk_i in enumerate((prev, n, nxt)):
            rows = pl.ds(pl.multiple_of(blk_i * BLK, BLK), BLK)
            dk_ref[rows, :] += dkb[:, part * BLK:(part + 1) * BLK].T
            dv_ref[rows, :] += dvb[:, part * BLK:(part + 1) * BLK].T

        e2 = e2_ref[...]
        lng = lng_ref[...]
        svo, vn, vhat, rstd = _sg_forward(sv_ref[...].astype(F32), lng, lnb_ref[...], sgw_ref, sgb_ref[...], e2)
        for j in range(SG_WIDTH // LANES):
            cs = slice(j * LANES, (j + 1) * LANES)
            dysg = dy_ref[:, ATTN_WIDTH + j * LANES:ATTN_WIDTH + (j + 1) * LANES].astype(F32)
            dsu_ref[:, cs] = (dysg * svo[j]).astype(dsu_ref.dtype)
            dsvo = dysg * su_ref[:, cs].astype(F32)
            dsgb_acc[:, cs] += dsvo
            d_lo = jnp.where(lane < HEAD_DIM, dsvo, 0.0)
            d_hi = dsvo - d_lo
            dsgw_ref[2 * j] += _mm_nt(d_lo, vn[j])
            dsgw_ref[2 * j + 1] += _mm_nt(d_hi, vn[j])
            dvn = _mm_tn(sgw_ref[2 * j], d_lo) + _mm_tn(sgw_ref[2 * j + 1], d_hi)
            vec_ref[0:1, cs] += jnp.sum(dvn * vhat[j], axis=0, keepdims=True)
            vec_ref[1:2, cs] += jnp.sum(dvn, axis=0, keepdims=True)
            dvh = dvn * lng[:, cs]
            m1 = _group_sum(dvh, e2) * (1.0 / HEAD_DIM)
            m2 = _group_sum(dvh * vhat[j], e2) * (1.0 / HEAD_DIM)
            dsv_ref[:, cs] = (rstd[j] * (dvh - m1 - vhat[j] * m2)).astype(dsv_ref.dtype)

        @pl.when(n == nb - 1)
        def _():
            rest = dsgb_acc[...]
            total = jnp.zeros((8, BLK), F32)
            for _ in range(3):
                part = rest.astype(MXU_DTYPE)
                total = total + lax.dot_general(e8_ref[...], part, (((1,), (1,)), ((), ())), preferred_element_type=F32)
                rest = rest - part.astype(F32)
            dsgb_ref[...] = total

    blk = lambda w: pl.BlockSpec((BLK, w), lambda n: (n, 0))
    return _call(
        body, "even_mix_bwd", (nb,),
        [pl.BlockSpec(memory_space=pltpu.SMEM), blk(512), _full((seq, LANES)), _full((seq, LANES)), blk(LANES),
         blk(D_MODEL), blk(D_MODEL), blk(512), blk(512), _full((1, 512)), _full((1, 512)), _full((8, BLK, BLK)),
         _full((BLK, 512)), _full((LANES, LANES)), _full((8, 512))],
        [blk(512), blk(512), blk(512), _full((seq, LANES)), _full((seq, LANES)), _full((8, BLK, BLK)),
         _full((8, BLK)), _full((8, 512)), _full((8, LANES))],
        [_sds((seq, 512), ACT_DTYPE), _sds((seq, 512), ACT_DTYPE), _sds((seq, 512), ACT_DTYPE), _sds((seq, LANES)), _sds((seq, LANES)),
         _sds((8, BLK, BLK)), _sds((8, BLK)), _sds((8, 512)), _sds((8, LANES))],
        (sink, q, k, v, lse, ycat, dycat, su, sv, sgln_g, sgln_b, sgw, sgb_full, e2, e8), "arbitrary",
        scratch=[pltpu.VMEM((BLK, 512), F32)], rider=rider)


def _even_proj_bwd(dq, dk, dv, dsu, dsv, dg, x, dres, mod, tabs, w_in_t, seq):
    tm = _row_tile(seq, 512)

    def body(dq_ref, dk_ref, dv_ref, dsu_ref, dsv_ref, dg_ref, x_ref, dres_ref, mod_ref, cos_ref, sp_ref, sm_ref, wt_ref,
             dx_ref, dw_ref, vec_ref, dpb_ref):
        @pl.when(pl.program_id(0) == 0)
        def _():
            vec_ref[...] = jnp.zeros_like(vec_ref)
            dw_ref[...] = jnp.zeros_like(dw_ref)

        cos_t, sin_p, sin_m = cos_ref[...], sp_ref[...], sm_ref[...]
        dt = dpb_ref.dtype
        for j in range(ATTN_WIDTH // LANES):
            cs = slice(j * LANES, (j + 1) * LANES)
            dpb_ref[:, cs] = _rope_t(dq_ref[:, cs].astype(F32), cos_t, sin_p, sin_m).astype(dt)
        dpb_ref[:, 512:640] = _rope_t(dk_ref[...], cos_t, sin_p, sin_m).astype(dt)
        dpb_ref[:, 640:768] = dv_ref[...].astype(dt)
        dpb_ref[:, 768:1280] = dsu_ref[...].astype(dt)
        dpb_ref[:, 1280:1792] = dsv_ref[...].astype(dt)
        dpb_ref[:, 1792:2816] = dg_ref[...].astype(dt)
        dpb = dpb_ref[...]
        dh = jnp.dot(dpb, wt_ref[...], preferred_element_type=F32)
        x_ = x_ref[...]
        hb = (x_ * (1.0 + mod_ref[1:2, :]) + mod_ref[0:1, :]).astype(MXU_DTYPE)
        dw_ref[...] += _mm_tn(dpb, hb)
        vec_ref[0:1, :] += jnp.sum(dh, axis=0, keepdims=True)
        vec_ref[1:2, :] += jnp.sum(dh * x_, axis=0, keepdims=True)
        dx_ref[...] = dres_ref[...] + dh * (1.0 + mod_ref[1:2, :])

    return pl.pallas_call(
        body, name="even_proj_bwd", grid=(seq // tm,),
        in_specs=[_rows(tm, 512), _rows(tm, LANES), _rows(tm, LANES), _rows(tm, 512), _rows(tm, 512), _rows(tm, D_MODEL),
                  _rows(tm, D_MODEL), _rows(tm, D_MODEL), _full((3, D_MODEL))] + [_rows(tm, LANES)] * 3
        + [_const((EVEN_IN, D_MODEL))],
        out_specs=[_rows(tm, D_MODEL), _const((EVEN_IN, D_MODEL)), _full((8, D_MODEL))],
        out_shape=[_sds((seq, D_MODEL)), _sds((EVEN_IN, D_MODEL)), _sds((8, D_MODEL))],
        scratch_shapes=[pltpu.VMEM((tm, EVEN_IN), MXU_DTYPE)],
        compiler_params=_params("arbitrary"),
    )(dq, dk, dv, dsu, dsv, dg, x, dres, mod, *tabs, w_in_t)


def _local_step(x, posf, tgt, mod, w, seq, ride=None):
    rid = lambda make, *a: None if ride is None else make(*a)
    mxu = lambda a: a.astype(MXU_DTYPE)
    row = lambda a: a.reshape(1, -1)
    tabs = _rope_tables(posf, seq)
    e2 = mxu(jnp.kron(jnp.eye(2, dtype=F32), jnp.ones((HEAD_DIM, HEAD_DIM), F32)))
    e8 = mxu(jnp.repeat(jnp.eye(N_SG_GROUPS, dtype=F32), HEAD_DIM, axis=1))
    sgw = mxu(w["ev_sg_w"])
    sgb_full = jnp.repeat(w["ev_sg_b"].T, HEAD_DIM, axis=1)
    sgln_g, sgln_b = row(w["ev_sg_ln_g"]), row(w["ev_sg_ln_b"])
    sink = w["ev_sink"].reshape(N_Q_HEADS)
    ev_w_in_t = mxu(w["ev_w_in_t"])
    if ride is None:
        ev_w_out, od_w_in, od_w_out = mxu(w["ev_w_out"]), mxu(w["od_w_in"]), mxu(w["od_w_out"])
    wcat = mxu(jnp.concatenate([w["od_w_a"][0], w["od_w_x"][0], w["od_w_a"][1], w["od_w_x"][1]], axis=2))
    gate_bias = jnp.stack([w["od_b_a"][0], w["od_b_x"][0], w["od_b_a"][1], w["od_b_x"][1]])
    conv_b = row(w["od_conv_b"])
    ln_g, ln_b = w["ln_g"], w["ln_b"]

    (q, k, v, su, sv, g0), got = _even_proj(x, mod[0], ev_w_in_t, tabs, seq, rid(_gather_rider, ride and ride["ev_w_out"]))
    if ride is not None:
        ev_w_out = got[0].reshape(D_MODEL, D_MODEL)
    (ycat, lse), got = _even_mix(q, k, v, su, sv, sink, sgln_g, sgln_b, sgw, sgb_full, e2, seq,
                                 rid(_gather_rider, ride and ride["od_w_in"]))
    if ride is not None:
        od_w_in = got[0]
    (z0, x1), got = _even_out(ycat, g0, x, mod[0], ev_w_out, ln_g[0:1], ln_b[0:1], seq, rid(_gather_rider, ride and ride["od_w_out"]))
    if ride is not None:
        od_w_out = got[0].reshape(D_MODEL, D_MODEL)
    xr, g1 = _odd_proj(x1, mod[1], od_w_in, seq)
    xc, a_f, b_f, a_r, b_r = _odd_gates(xr, w["od_conv_w"], conv_b, wcat, gate_bias, w["od_lam"], seq)
    hf, hpf = _scan(a_f, b_f, seq, descending=False, post=False, name="scan_fwd")
    hr, hpr = _scan(a_r, b_r, seq, descending=True, post=False, name="scan_rev")
    dhs, dg1, dres1, loss, d_od_w_out, vec_o = _odd_out_and_loss(hf, hr, g1, x1, tgt, mod[1], od_w_out, ln_g[1:2], ln_b[1:2], seq)
    (gf,) = _scan(a_f, dhs, seq, descending=True, post=True, name="scan_fwd_bwd")
    (gr,) = _scan(a_r, dhs, seq, descending=False, post=True, name="scan_rev_bwd")
    dxc, d_wcat, vec_g = _odd_gates_bwd(xc, gf, gr, hpf, hpr, wcat, gate_bias, w["od_lam"], seq)
    dx1, d_od_w_in, vec_p = _odd_proj_bwd(dxc, xr, dg1, x1, dres1, mod[1], w["od_conv_w"], od_w_in, seq)
    d_od_w_a = jnp.stack([d_wcat[:, :, 0:128], d_wcat[:, :, 256:384]])
    d_od_w_x = jnp.stack([d_wcat[:, :, 128:256], d_wcat[:, :, 384:512]])
    od_parts = [d_od_w_in.reshape(4, 2, 512, 512), d_od_w_out.reshape(4, 2, 128, D_MODEL),
                d_od_w_a.reshape(4, 2, 2 * BLK, BLK), d_od_w_x.reshape(4, 2, 2 * BLK, BLK)]
    (dycat, dg0, dres0, d_ev_w_out, vec_e), got_od = _even_out_bwd(dx1, z0, ycat, g0, mod[0], ln_g[0:1], ev_w_out, seq,
                                                                   rid(_sibling_swap_rider, od_parts))
    if ride is not None:
        od_sums = _sum_sibling(ride["core"], od_parts, got_od, [ride["wire"]] * 4, "sum_sibling_od")
    (dq, dsu, dsv, dk, dv, d_sgw, d_sgb, vec_s, d_sink), od_slots = _even_mix_bwd(
        q, k, v, lse, ycat, dycat, su, sv, sink, sgln_g, sgln_b, sgw, sgb_full, e2, e8, seq,
        rid(_chip_exchange_rider, ride and od_sums))
    grad_x, d_ev_w_in_t, vec_x = _even_proj_bwd(dq, dk, dv, dsu, dsv, dg0, x, dres0, mod[0], tabs, ev_w_in_t, seq)

    dmod = jnp.stack([jnp.stack([vec_x[0], vec_x[1], vec_e[2]]), jnp.stack([vec_p[5], vec_p[6], vec_o[2]])])
    grads = {
        "ln_g": jnp.stack([vec_e[0], vec_o[0]]), "ln_b": jnp.stack([vec_e[1], vec_o[1]]),
        "ev_w_in_t": d_ev_w_in_t, "ev_w_out": d_ev_w_out, "ev_sink": d_sink[:, 0],
        "ev_sg_ln_g": vec_s[0], "ev_sg_ln_b": vec_s[1], "ev_sg_w": d_sgw,
        "ev_sg_b": d_sgb,
        "od_conv_w": vec_p[0:4], "od_conv_b": vec_p[4],
        "od_b_a": jnp.stack([vec_g[0], vec_g[2]]), "od_b_x": jnp.stack([vec_g[1], vec_g[3]]), "od_lam": vec_g[4:6],
    }
    if ride is None:
        grads.update({"od_w_in": d_od_w_in, "od_w_out": d_od_w_out, "od_w_a": d_od_w_a, "od_w_x": d_od_w_x})
    else:
        grads["od_slots"] = od_slots
    return loss[0, 0], grad_x, dmod, grads


def _allgather8(block, name):
    m_per, n = block.shape

    def body(x_ref, out_ref, send_sems, recv_sems, local_sem):
        x, y, c = _place()
        me, sibling = (x, y, c), (x, y, 1 - c)
        chips = [(1 - x, y), (x, 1 - y), (1 - x, 1 - y)]

        def rows(px, py, pc):
            return out_ref.at[pl.ds((4 * px + 2 * py + pc) * m_per, m_per), :]

        def copy(k, blk, to, src=None):
            return pltpu.make_async_remote_copy(src_ref=rows(*blk) if src is None else src, dst_ref=rows(*blk),
                                                send_sem=send_sems.at[k], recv_sem=recv_sems.at[k], device_id=to,
                                                device_id_type=MESH)

        mine = pltpu.make_async_copy(x_ref, rows(*me), local_sem)
        mine.start()
        first = [copy(0, me, sibling, src=x_ref)] + [copy(1 + j, me, (*chip, c), src=x_ref) for j, chip in enumerate(chips)]
        for cp in first:
            cp.start()
        passed = [copy(4 + j, (*chip, c), sibling) for j, chip in enumerate(chips)]
        for j, chip in enumerate(chips):
            copy(1 + j, (*chip, c), me).wait_recv()
            passed[j].start()
        copy(0, sibling, me).wait_recv()
        for j, chip in enumerate(chips):
            copy(4 + j, (*chip, 1 - c), me).wait_recv()
        for cp in first + passed:
            cp.wait_send()
        mine.wait()

    return pl.pallas_call(
        body, name=name, out_shape=_sds((8 * m_per, n), block.dtype),
        in_specs=[pl.BlockSpec(memory_space=pltpu.VMEM)], out_specs=pl.BlockSpec(memory_space=pltpu.VMEM),
        scratch_shapes=[pltpu.SemaphoreType.DMA((7,)), pltpu.SemaphoreType.DMA((7,)), pltpu.SemaphoreType.DMA],
        compiler_params=pltpu.CompilerParams(vmem_limit_bytes=VMEM_LIMIT),
    )(block)


class _Copies:
    def __init__(self, send_sems, recv_sems, local_sems, stages):
        self.send_sems, self.recv_sems, self.local_sems, self.stages = send_sems, recv_sems, local_sems, stages
        self.sent, self.staged, self.locals = [], [], []

    def remote(self, k, src, dst, to):
        return pltpu.make_async_remote_copy(src_ref=src, dst_ref=dst, send_sem=self.send_sems.at[k], recv_sem=self.recv_sems.at[k],
                                            device_id=to, device_id_type=MESH)

    def send(self, k, src, dst, to):
        cp = self.remote(k, src, dst, to)
        cp.start()
        self.sent.append(cp)

    def arrived(self, k, dst, frm):
        self.remote(k, dst, dst, frm).wait_recv()

    def local(self, src, dst):
        k = len(self.staged)
        cp = pltpu.make_async_copy(src, self.stages[k], self.local_sems.at[2 * k])
        cp.start()
        self.staged.append((cp, dst))

    def flush(self):
        for k in range(len(self.locals), len(self.staged)):
            cp, dst = self.staged[k]
            cp.wait()
            out = pltpu.make_async_copy(self.stages[k], dst, self.local_sems.at[2 * k + 1])
            out.start()
            self.locals.append(out)

    def drain(self):
        self.flush()
        for cp in self.sent:
            cp.wait_send()
        for cp in self.locals:
            cp.wait()


def _comm_call(body, name, ins, out_shapes, n_remote, stages):
    n_in, n_out = len(ins), len(out_shapes)

    def kern(*refs):
        in_refs, out_refs = refs[:n_in], refs[n_in:n_in + n_out]
        send_sems, recv_sems, local_sems = refs[n_in + n_out:n_in + n_out + 3]
        body(_Copies(send_sems, recv_sems, local_sems, refs[n_in + n_out + 3:]), in_refs, out_refs)

    hbm = pl.BlockSpec(memory_space=pl.ANY)
    return pl.pallas_call(
        kern, name=name, out_shape=out_shapes, in_specs=[hbm] * n_in, out_specs=[hbm] * n_out,
        scratch_shapes=[pltpu.SemaphoreType.DMA((n_remote,)), pltpu.SemaphoreType.DMA((n_remote,)),
                        pltpu.SemaphoreType.DMA((2 * len(stages),))] + [pltpu.VMEM(s, d) for s, d in stages],
        compiler_params=pltpu.CompilerParams(vmem_limit_bytes=VMEM_LIMIT),
    )(*ins)


def _gather_to_all(cps, pairs, me, sibling, other_chips, c, base):
    idx = lambda p: 4 * p[0] + 2 * p[1] + p[2]
    for i, (src, dst) in enumerate(pairs):
        cps.local(src, dst.at[idx(me)])
        cps.send(base + 7 * i, src, dst.at[idx(me)], sibling)
        for j, chip in enumerate(other_chips):
            cps.send(base + 7 * i + 1 + j, src, dst.at[idx(me)], (*chip, c))
    cps.flush()
    for j, chip in enumerate(other_chips):
        for i, (_, dst) in enumerate(pairs):
            got = dst.at[idx((*chip, c))]
            cps.arrived(base + 7 * i + 1 + j, got, (*chip, c))
            cps.send(base + 7 * i + 4 + j, got, got, sibling)
    for i, (_, dst) in enumerate(pairs):
        cps.arrived(base + 7 * i, dst.at[idx(sibling)], sibling)
        for j, chip in enumerate(other_chips):
            cps.arrived(base + 7 * i + 4 + j, dst.at[idx((*chip, 1 - c))], sibling)


def _gather_weights(shards, small):
    n = len(shards)

    def body(cps, ins, outs):
        x, y, c = _place()
        me, sibling, mine = (x, y, c), (x, y, 1 - c), 2 * x + y
        chips = [(1 - x, y), (x, 1 - y), (1 - x, 1 - y)]
        for i in range(n):
            cps.local(ins[i], outs[i].at[mine])
        for j, (px, py) in enumerate(chips):
            for i in range(n):
                hr = shards[i].shape[0] // 2
                rows = pl.ds(c * hr, hr)
                cps.send(6 * i + j, ins[i].at[rows], outs[i].at[mine, rows], (px, py, c))
        _gather_to_all(cps, [(ins[n], outs[n])], me, sibling, chips, c, 6 * n)
        for j, (px, py) in enumerate(chips):
            for i in range(n):
                hr = shards[i].shape[0] // 2
                got = outs[i].at[2 * px + py, pl.ds(c * hr, hr)]
                cps.arrived(6 * i + j, got, (px, py, c))
                cps.send(6 * i + 3 + j, got, got, sibling)
        for j, (px, py) in enumerate(chips):
            for i in range(n):
                hr = shards[i].shape[0] // 2
                cps.arrived(6 * i + 3 + j, outs[i].at[2 * px + py, pl.ds((1 - c) * hr, hr)], sibling)
        cps.drain()

    return _comm_call(body, "gather_weights", list(shards) + [small],
                      [_sds((4,) + s.shape, s.dtype) for s in shards] + [_sds((8,) + small.shape, small.dtype)], 6 * n + 7,
                      [(a.shape, a.dtype) for a in list(shards) + [small]])


def _reduce_sibling(parts, dmod_rows):
    n = len(parts)

    def body(cps, ins, outs):
        x, y, c = _place()
        me, sibling = (x, y, c), (x, y, 1 - c)
        chips = [(1 - x, y), (x, 1 - y), (1 - x, 1 - y)]
        for i in range(n):
            cps.send(i, ins[i].at[:, 1 - c], outs[i], sibling)
        _gather_to_all(cps, [(ins[n], outs[n])], me, sibling, chips, c, n)
        for i in range(n):
            cps.arrived(i, outs[i], sibling)
        cps.drain()

    return _comm_call(body, "reduce_sibling", list(parts) + [dmod_rows],
                      [_sds((4,) + p.shape[2:], p.dtype) for p in parts] + [_sds((8,) + dmod_rows.shape, dmod_rows.dtype)], n + 7,
                      [(dmod_rows.shape, dmod_rows.dtype)])


def _reduce_chips(parts):
    n = len(parts)

    def body(cps, ins, outs):
        x, y, c = _place()
        mine = 2 * x + y
        chips = [(1 - x, y), (x, 1 - y), (1 - x, 1 - y)]
        for i in range(n):
            cps.local(ins[i].at[mine], outs[i].at[mine])
        for j, (px, py) in enumerate(chips):
            for i in range(n):
                cps.send(3 * i + j, ins[i].at[2 * px + py], outs[i].at[mine], (px, py, c))
        cps.flush()
        for j, (px, py) in enumerate(chips):
            for i in range(n):
                cps.arrived(3 * i + j, outs[i].at[2 * px + py], (px, py, c))
        cps.drain()

    return _comm_call(body, "reduce_chips", list(parts), [_sds(p.shape, p.dtype) for p in parts], 3 * n,
                      [(p.shape[1:], p.dtype) for p in parts])


def _gather_reduced(shard_parts, repl_parts):
    ns, nr = len(shard_parts), len(repl_parts)

    def body(cps, ins, outs):
        x, y, c = _place()
        me, sibling = (x, y, c), (x, y, 1 - c)
        chips = [(1 - x, y), (x, 1 - y), (1 - x, 1 - y)]
        for i in range(ns):
            cps.local(ins[i], outs[i].at[c])
            cps.send(i, ins[i], outs[i].at[c], sibling)
        _gather_to_all(cps, [(ins[ns + i], outs[ns + i]) for i in range(nr)], me, sibling, chips, c, ns)
        for i in range(ns):
            cps.arrived(i, outs[i].at[1 - c], sibling)
        cps.drain()

    return _comm_call(body, "gather_reduced", list(shard_parts) + list(repl_parts),
                      [_sds((2,) + p.shape, p.dtype) for p in shard_parts] + [_sds((8,) + p.shape, p.dtype) for p in repl_parts],
                      ns + 7 * nr, [(p.shape, p.dtype) for p in list(shard_parts) + list(repl_parts)])


def _sum_sibling(core, parts, got, wire, name):
    n = len(parts)

    def body(core_ref, *refs):
        for i in range(n):
            refs[2 * n + i][0] = (refs[i][0] + refs[n + i][0]).astype(wire[i])

    keep_spec = lambda p: pl.BlockSpec((1, None) + p.shape[2:], lambda s, core_ref: (s, core_ref[0], 0, 0))
    slot_spec = lambda p: pl.BlockSpec((1,) + p.shape[2:], lambda s, core_ref: (s, 0, 0))
    return pl.pallas_call(
        body, name=name,
        grid_spec=pltpu.PrefetchScalarGridSpec(
            num_scalar_prefetch=1, grid=(4,), in_specs=[keep_spec(p) for p in parts] + [slot_spec(p) for p in parts],
            out_specs=[slot_spec(p) for p in parts]),
        out_shape=[_sds((4,) + p.shape[2:], wire[i]) for i, p in enumerate(parts)],
        compiler_params=_params("parallel"),
    )(core, *parts, *got)


def _sum_slots(slots, name):
    n = len(slots)

    def spec_pair(p):
        k, rows, cols = p.shape
        sub = 16 if p.dtype == BF16 else 8
        if (rows // 2) % sub == 0:
            return pl.BlockSpec((k, rows // 2, cols), lambda i: (0, i, 0)), pl.BlockSpec((rows // 2, cols), lambda i: (i, 0))
        return pl.BlockSpec((k, rows, cols), lambda i: (0, 0, 0)), pl.BlockSpec((rows, cols), lambda i: (0, 0))

    pairs = [spec_pair(p) for p in slots]

    def body(*refs):
        for i in range(n):
            acc = refs[i][0].astype(F32)
            for j in range(1, slots[i].shape[0]):
                acc = acc + refs[i][j].astype(F32)
            refs[n + i][...] = acc

    return pl.pallas_call(
        body, name=name, grid=(2,), in_specs=[a for a, _ in pairs], out_specs=[b for _, b in pairs],
        out_shape=[_sds(p.shape[1:]) for p in slots], compiler_params=_params("arbitrary"),
    )(*slots)


def _modulation(c_all, ada_w, ada_b):
    cols = ada_w.shape[2]

    def body(c_ref, w_ref, b_ref, o_ref):
        cc = c_ref[...]
        o_ref[0] = _mm(cc * _sigmoid(cc), w_ref[0]) + b_ref[0]

    return pl.pallas_call(
        body, name="modulation", grid=(2,),
        in_specs=[_full((8, D_MODEL)), pl.BlockSpec((1, D_MODEL, cols), lambda l: (l, 0, 0)), pl.BlockSpec((1, 1, cols), lambda l: (l, 0, 0))],
        out_specs=pl.BlockSpec((1, 8, cols), lambda l: (l, 0, 0)), out_shape=_sds((2, 8, cols)),
        compiler_params=_params("parallel"),
    )(c_all, ada_w, ada_b)


def _adamw_math(w, g, m, v):
    m = ADAM_B1 * m + (1.0 - ADAM_B1) * g
    v = ADAM_B2 * v + (1.0 - ADAM_B2) * (g * g)
    m_hat = m / (1.0 - ADAM_B1 ** ADAM_STEP)
    v_hat = v / (1.0 - ADAM_B2 ** ADAM_STEP)
    delta = -ADAM_LR * (m_hat / (jnp.sqrt(v_hat) + ADAM_EPS) + ADAM_WD * w)
    return delta, m, v


def _ada_update(c_all, dmod, w, m, v):
    cols = w.shape[2]
    tr = 256
    spec3 = pl.BlockSpec((1, tr, cols), lambda l, i: (l, i, 0))

    def body(c_ref, d_ref, w_ref, m_ref, v_ref, g_ref, dl_ref, nm_ref, nv_ref):
        cc = c_ref[...]
        g = _mm_tn(cc * _sigmoid(cc), d_ref[0])
        g_ref[0] = g
        dl_ref[0], nm_ref[0], nv_ref[0] = _adamw_math(w_ref[0], g, m_ref[0], v_ref[0])

    return pl.pallas_call(
        body, name="ada_update", grid=(2, D_MODEL // tr),
        in_specs=[pl.BlockSpec((8, tr), lambda l, i: (0, i)), pl.BlockSpec((1, 8, cols), lambda l, i: (l, 0, 0)), spec3, spec3, spec3],
        out_specs=[spec3] * 4, out_shape=[_sds(w.shape)] * 4, compiler_params=_params("parallel", "parallel"),
    )(c_all, dmod, w, m, v)


def _adamw(w, g, m, v, name):
    rows, n = w.shape
    tr = next(t for t in (256, 128, 64, 32, 16, 8, rows) if rows % t == 0)

    def body(w_ref, g_ref, m_ref, v_ref, dl_ref, nm_ref, nv_ref):
        dl_ref[...], nm_ref[...], nv_ref[...] = _adamw_math(w_ref[...], g_ref[...], m_ref[...], v_ref[...])

    return pl.pallas_call(body, name=name, grid=(rows // tr,), in_specs=[_rows(tr, n)] * 4, out_specs=[_rows(tr, n)] * 3,
                          out_shape=[_sds((rows, n))] * 3, compiler_params=_params("parallel"))(w, g, m, v)


def _adamw_small(params):
    n = len(params)

    def body(*refs):
        ins, outs = refs[:4 * n], refs[4 * n:]
        for j in range(n):
            w_ref, g_ref, m_ref, v_ref = ins[4 * j:4 * j + 4]
            outs[3 * j][...], outs[3 * j + 1][...], outs[3 * j + 2][...] = _adamw_math(w_ref[...], g_ref[...], m_ref[...], v_ref[...])

    flat = [a for p in params for a in p]
    res = pl.pallas_call(body, name="adamw_small", out_shape=[_sds(p[0].shape) for p in params for _ in range(3)])(*flat)
    return [tuple(res[3 * j:3 * j + 3]) for j in range(n)]


def _cols(a, start, size):
    return lax.dynamic_slice_in_dim(a, start, size, axis=a.ndim - 1)


def kernel(x, c, positions, ada_w, ada_b, ln_g, ln_b, ev_w_in, ev_w_out, ev_sink, ev_sg_ln_g, ev_sg_ln_b, ev_sg_w, ev_sg_b, od_w_in, od_conv_w, od_conv_b, od_w_a, od_b_a, od_w_x, od_b_x, od_lam, od_w_out, loss_target, m_ada_w, m_ada_b, m_ln_g, m_ln_b, m_ev_w_in, m_ev_w_out, m_ev_sink, m_ev_sg_ln_g, m_ev_sg_ln_b, m_ev_sg_w, m_ev_sg_b, m_od_w_in, m_od_conv_w, m_od_conv_b, m_od_w_a, m_od_b_a, m_od_w_x, m_od_b_x, m_od_lam, m_od_w_out, v_ada_w, v_ada_b, v_ln_g, v_ln_b, v_ev_w_in, v_ev_w_out, v_ev_sink, v_ev_sg_ln_g, v_ev_sg_ln_b, v_ev_sg_w, v_ev_sg_b, v_od_w_in, v_od_conv_w, v_od_conv_b, v_od_w_a, v_od_b_a, v_od_w_x, v_od_b_x, v_od_lam, v_od_w_out):
    seq = x.shape[1]
    px, py, pc = _place()
    chip = 2 * px + py
    dev = 2 * chip + pc

    small = jnp.concatenate([od_conv_w[0].reshape(-1), od_conv_b[0], od_b_a[0].reshape(-1), jnp.zeros((256,), F32),
                             od_b_x[0].reshape(-1), od_lam[0].reshape(-1)]).reshape(3, D_MODEL)
    blk = jnp.concatenate([c, small, jnp.zeros((4, D_MODEL), F32)], axis=0)
    tr = lambda a: jnp.swapaxes(a, -1, -2)
    wire_w = lambda a: a.astype(MXU_DTYPE)
    ev_w_in4, g_small = _gather_weights([wire_w(tr(ev_w_in[0]))], blk)
    core = pc.astype(jnp.int32).reshape(1)
    ride = {"ev_w_out": wire_w(ev_w_out[0]), "od_w_in": wire_w(od_w_in[0]), "od_w_out": wire_w(od_w_out[0]),
            "core": core, "wire": MXU_DTYPE}
    c_all = g_small[:, 0, :]
    per_chip = g_small[0::2]
    conv_w = per_chip[:, 1].reshape(4, 4, 256).transpose(1, 0, 2).reshape(4, D_MODEL)
    conv_b = per_chip[:, 2, 0:256].reshape(D_MODEL)
    b_a = per_chip[:, 2, 256:768].reshape(4, 2, 256).transpose(1, 0, 2).reshape(2, D_MODEL)
    b_x = per_chip[:, 3, 0:512].reshape(4, 2, 256).transpose(1, 0, 2).reshape(2, D_MODEL)
    lam = per_chip[:, 3, 512:1024].reshape(4, 2, 256).transpose(1, 0, 2).reshape(2, D_MODEL)

    w_full = {
        "ev_w_in_t": ev_w_in4.reshape(EVEN_IN, D_MODEL),
        "ev_sink": ev_sink[0], "ev_sg_ln_g": ev_sg_ln_g[0], "ev_sg_ln_b": ev_sg_ln_b[0], "ev_sg_w": ev_sg_w[0],
        "ev_sg_b": ev_sg_b[0], "od_conv_w": conv_w, "od_conv_b": conv_b, "od_w_a": od_w_a[0], "od_b_a": b_a,
        "od_w_x": od_w_x[0], "od_b_x": b_x, "od_lam": lam, "ln_g": ln_g, "ln_b": ln_b,
    }

    ada_cols = ada_w.shape[2]
    mod_sh = _modulation(c_all, ada_w, _cols(ada_b, chip * ada_cols, ada_cols).reshape(2, 1, ada_cols))
    mod_all = _allgather8(mod_sh.reshape(16, ada_cols), "gather_mod").reshape(4, 2, 2, 8, ada_cols)[:, 0]
    mod_mine = lax.dynamic_index_in_dim(mod_all, dev, axis=2, keepdims=False)
    mod = mod_mine.transpose(1, 0, 2).reshape(2, 3, D_MODEL)

    posf = positions.astype(F32).reshape(seq, 1)
    loss_local, grad_x, dmod, g = _local_step(x[0], posf, loss_target[0], mod, w_full, seq, ride)

    pad = lambda a, n: jnp.concatenate([a.reshape(-1), jnp.zeros((n - a.size,), F32)])
    rows_small = jnp.concatenate([
        dmod.reshape(6, D_MODEL), g["ln_g"][0:1], g["ln_b"][0:1], g["ln_g"][1:2], g["ln_b"][1:2],
        jnp.concatenate([g["ev_sg_ln_g"], g["ev_sg_ln_b"]]).reshape(1, D_MODEL), g["ev_sg_b"].reshape(1, D_MODEL),
        g["od_conv_w"], g["od_conv_b"].reshape(1, D_MODEL), g["od_b_a"], g["od_b_x"], g["od_lam"],
        pad(g["ev_sink"], D_MODEL).reshape(1, D_MODEL), pad(loss_local, D_MODEL).reshape(1, D_MODEL),
        jnp.zeros((39, D_MODEL), F32)], axis=0)
    parts = [g["ev_w_in_t"].reshape(4, 2, 352, D_MODEL), g["ev_w_out"].reshape(4, 2, 128, D_MODEL),
             g["ev_sg_w"].reshape(4, 2, BLK, BLK), rows_small.reshape(4, 2, 8, D_MODEL)]
    wire = [MXU_DTYPE] * 3 + [F32]
    dmod_blk = jnp.concatenate([dmod.reshape(6, D_MODEL), jnp.zeros((2, D_MODEL), F32)], axis=0)
    *got, dmod_gathered = _reduce_sibling(parts, dmod_blk)
    ev_slots = list(_reduce_chips(_sum_sibling(core, parts, got, wire, "sum_sibling")))
    od_slots = list(g["od_slots"])
    mine = _sum_slots(ev_slots[0:2] + od_slots[0:2] + ev_slots[2:3] + od_slots[2:4] + ev_slots[3:4], "sum_chips")
    reduced = _gather_reduced(mine[:4], mine[4:])
    g_ev_w_in_t = reduced[0].reshape(704, D_MODEL)
    g_ev_w_out = reduced[1].reshape(256, D_MODEL)
    g_od_w_in = reduced[2].reshape(D_MODEL, 512)
    g_od_w_out = reduced[3].reshape(256, D_MODEL)
    g_sg_w = reduced[4].reshape(8 * BLK, BLK)
    g_w_a = reduced[5].reshape(16 * BLK, BLK)
    g_w_x = reduced[6].reshape(16 * BLK, BLK)
    gs = reduced[7].reshape(64, D_MODEL)
    loss = gs[24, 0]
    dmod_all = dmod_gathered[:, 0:6].reshape(8, 2, 3 * D_MODEL)
    dmod_sh = _cols(dmod_all, chip * ada_cols, ada_cols).transpose(1, 0, 2)
    g_ada_w, d_ada_w, nm_ada_w, nv_ada_w = _ada_update(c_all, dmod_sh, ada_w, m_ada_w, v_ada_w)

    big = {}
    d_, nm_, nv_ = _adamw(tr(ev_w_in[0]), g_ev_w_in_t, tr(m_ev_w_in[0]), tr(v_ev_w_in[0]), "adamw_ev_w_in")
    big["ev_w_in"] = tuple(tr(a).reshape(ev_w_in.shape) for a in (g_ev_w_in_t, d_, nm_, nv_))
    for name, w_, g_, m_, v_ in (
            ("ev_w_out", ev_w_out, g_ev_w_out, m_ev_w_out, v_ev_w_out),
            ("od_w_in", od_w_in, g_od_w_in, m_od_w_in, v_od_w_in), ("od_w_out", od_w_out, g_od_w_out, m_od_w_out, v_od_w_out),
            ("ev_sg_w", ev_sg_w, g_sg_w, m_ev_sg_w, v_ev_sg_w), ("od_w_a", od_w_a, g_w_a, m_od_w_a, v_od_w_a),
            ("od_w_x", od_w_x, g_w_x, m_od_w_x, v_od_w_x)):
        two_d = lambda a: a.reshape(g_.shape)
        d_, nm_, nv_ = _adamw(two_d(w_), g_, two_d(m_), two_d(v_), "adamw_" + name)
        big[name] = tuple(a.reshape(w_.shape) for a in (g_, d_, nm_, nv_))
    big["ada_w"] = (g_ada_w, d_ada_w, nm_ada_w, nv_ada_w)

    sh = lambda a: _cols(a, chip * 256, 256)
    small_g = {
        "ada_b": gs[0:6].reshape(2, 3 * D_MODEL), "ln_g": jnp.stack([gs[6], gs[8]]), "ln_b": jnp.stack([gs[7], gs[9]]),
        "ev_sink": gs[23:24, 0:8], "ev_sg_ln_g": gs[10:11, 0:512], "ev_sg_ln_b": gs[10:11, 512:1024],
        "ev_sg_b": gs[11].reshape(8, BLK), "od_conv_w": sh(gs[12:16]), "od_conv_b": sh(gs[16:17]), "od_b_a": sh(gs[17:19]),
        "od_b_x": sh(gs[19:21]), "od_lam": sh(gs[21:23]),
    }
    small_in = {"ada_b": (ada_b, m_ada_b, v_ada_b), "ln_g": (ln_g, m_ln_g, v_ln_g), "ln_b": (ln_b, m_ln_b, v_ln_b),
                "ev_sink": (ev_sink, m_ev_sink, v_ev_sink), "ev_sg_ln_g": (ev_sg_ln_g, m_ev_sg_ln_g, v_ev_sg_ln_g),
                "ev_sg_ln_b": (ev_sg_ln_b, m_ev_sg_ln_b, v_ev_sg_ln_b), "ev_sg_b": (ev_sg_b, m_ev_sg_b, v_ev_sg_b),
                "od_conv_w": (od_conv_w, m_od_conv_w, v_od_conv_w), "od_conv_b": (od_conv_b, m_od_conv_b, v_od_conv_b),
                "od_b_a": (od_b_a, m_od_b_a, v_od_b_a), "od_b_x": (od_b_x, m_od_b_x, v_od_b_x),
                "od_lam": (od_lam, m_od_lam, v_od_lam)}
    names_small = list(small_g)
    upd = _adamw_small([(small_in[n][0].reshape(small_g[n].shape), small_g[n], small_in[n][1].reshape(small_g[n].shape),
                         small_in[n][2].reshape(small_g[n].shape)) for n in names_small])
    res = dict(big)
    for n, (d_, nm_, nv_) in zip(names_small, upd):
        shape = small_in[n][0].shape
        res[n] = tuple(a.reshape(shape) for a in (small_g[n], d_, nm_, nv_))

    order = ["ada_w", "ada_b", "ln_g", "ln_b", "ev_w_in", "ev_w_out", "ev_sink", "ev_sg_ln_g", "ev_sg_ln_b", "ev_sg_w", "ev_sg_b",
             "od_w_in", "od_conv_w", "od_conv_b", "od_w_a", "od_b_a", "od_w_x", "od_b_x", "od_lam", "od_w_out"]
    return (loss, grad_x.reshape(x.shape), *[res[n][0] for n in order], *[res[n][1] for n in order],
            *[res[n][2] for n in order], *[res[n][3] for n in order])
```

```python
import functools

import jax
import jax.numpy as jnp
import numpy as np
from jax import lax
from jax.experimental import pallas as pl
from jax.experimental.pallas import tpu as pltpu

F32 = jnp.float32
BF16 = jnp.bfloat16
MXU_DTYPE = BF16
ACT_DTYPE = MXU_DTYPE

D_MODEL = 1024
HEAD_DIM = 64
N_Q_HEADS = 8
Q_PER_KV = 4
ATTN_WIDTH = 512
KV_WIDTH = 128
BLK = 128
ROPE_DIM = 16
ROPE_THETA = 500000.0
N_SG_GROUPS = 8
SG_WIDTH = 512
EVEN_IN = 2816
ODD_IN = 2048
RNN_HEADS = 8
RG_LRU_C = 8.0
ALPHA = (2 * 2) ** 0.25
LN_EPS = 1e-5
NEG_INF = -1e30
ADAM_LR, ADAM_B1, ADAM_B2, ADAM_EPS, ADAM_WD, ADAM_STEP = 0.001, 0.9, 0.999, 1e-08, 0.01, 10

LANES = 128
VMEM_LIMIT = 56 * 1024 * 1024
MESH = pl.DeviceIdType.MESH


def _mm(a, b):
    return jnp.dot(a.astype(MXU_DTYPE), b.astype(MXU_DTYPE), preferred_element_type=F32)


def _mm_nt(a, b):
    return lax.dot_general(a.astype(MXU_DTYPE), b.astype(MXU_DTYPE), (((1,), (1,)), ((), ())), preferred_element_type=F32)


def _mm_tn(a, b):
    return lax.dot_general(a.astype(MXU_DTYPE), b.astype(MXU_DTYPE), (((0,), (0,)), ((), ())), preferred_element_type=F32)


def _sigmoid(x):
    return 1.0 / (1.0 + jnp.exp(-x))


def _ln_stats(z):
    mu = jnp.mean(z, axis=-1, keepdims=True)
    d = z - mu
    var = jnp.mean(d * d, axis=-1, keepdims=True)
    rstd = lax.rsqrt(var + LN_EPS)
    return d * rstd, rstd


def _ln_bwd(dout, zhat, rstd, g):
    dzh = dout * g
    m1 = jnp.mean(dzh, axis=-1, keepdims=True)
    m2 = jnp.mean(dzh * zhat, axis=-1, keepdims=True)
    return rstd * (dzh - m1 - zhat * m2)


def _group_sum(x, e2):
    hi = x.astype(MXU_DTYPE)
    lo = (x - hi.astype(F32)).astype(MXU_DTYPE)
    return jnp.dot(hi, e2, preferred_element_type=F32) + jnp.dot(lo, e2, preferred_element_type=F32)


def _lane_iota(shape):
    return lax.broadcasted_iota(jnp.int32, shape, 1)


def _to_kv_lanes(t, h):
    src_lo = (h % 2 == 0)
    dst_lo = (h // Q_PER_KV == 0)
    if src_lo != dst_lo:
        t = pltpu.roll(t, HEAD_DIM, 1)
    lane = _lane_iota(t.shape)
    keep = (lane < HEAD_DIM) if dst_lo else (lane >= HEAD_DIM)
    return jnp.where(keep, t, 0.0)


def _from_kv_lanes(t, h):
    src_lo = (h // Q_PER_KV == 0)
    dst_lo = (h % 2 == 0)
    lane = _lane_iota(t.shape)
    keep = (lane < HEAD_DIM) if src_lo else (lane >= HEAD_DIM)
    t = jnp.where(keep, t, 0.0)
    if src_lo != dst_lo:
        t = pltpu.roll(t, HEAD_DIM, 1)
    return t


def _rope(t, cos_t, sin_p, sin_m):
    half = ROPE_DIM // 2
    return t * cos_t + pltpu.roll(t, half, 1) * sin_p + pltpu.roll(t, LANES - half, 1) * sin_m


def _rope_t(d, cos_t, sin_p, sin_m):
    half = ROPE_DIM // 2
    return d * cos_t + pltpu.roll(d * sin_p, LANES - half, 1) + pltpu.roll(d * sin_m, half, 1)


def _band(ref, n, nb):
    prev = jnp.maximum(n - 1, 0)
    nxt = jnp.minimum(n + 1, nb - 1)
    rows = [ref[pl.ds(pl.multiple_of(j * BLK, BLK), BLK), :] for j in (prev, n, nxt)]
    return jnp.concatenate(rows, axis=0)


def _band_bias(n, seq):
    qi = lax.broadcasted_iota(jnp.int32, (BLK, 3 * BLK), 0)
    kj = lax.broadcasted_iota(jnp.int32, (BLK, 3 * BLK), 1)
    k_abs = n * BLK - BLK + kj
    valid = (jnp.abs(kj - BLK - qi) <= BLK) & (k_abs >= 0) & (k_abs < seq)
    bias = jnp.where(valid, 0.0, NEG_INF)
    return jnp.concatenate([bias] * Q_PER_KV, axis=0)


def _stack_heads(tile_of, kv):
    return jnp.concatenate([_to_kv_lanes(tile_of(h // 2), h) for h in range(Q_PER_KV * kv, Q_PER_KV * (kv + 1))], axis=0)


def _per_head_column(vals):
    row = lax.broadcasted_iota(jnp.int32, (Q_PER_KV * BLK, 1), 0)
    return jnp.where(row < BLK, vals[0], jnp.where(row < 2 * BLK, vals[1], jnp.where(row < 3 * BLK, vals[2], vals[3])))


def _softplus_neg(lam):
    e = jnp.exp(-jnp.abs(lam))
    u = 1.0 + e
    log1p_e = jnp.where(u == 1.0, e, jnp.log(u) * (e / (u - 1.0)))
    sp = jnp.maximum(-lam, 0.0) + log1p_e
    dsp = -1.0 / (1.0 + jnp.exp(lam))
    return sp, dsp


def _full(shape):
    return pl.BlockSpec(shape, lambda *_: (0,) * len(shape))


def _const(shape):
    return pl.BlockSpec(shape, lambda *_: (0,) * len(shape), pipeline_mode=pl.Buffered(1))


def _rows(tm, n):
    return pl.BlockSpec((tm, n), lambda i: (i, 0))


def _params(*sem):
    return pltpu.CompilerParams(dimension_semantics=sem, vmem_limit_bytes=VMEM_LIMIT)


def _sds(shape, dtype=F32):
    return jax.ShapeDtypeStruct(shape, dtype)


def _place():
    return lax.axis_index("x"), lax.axis_index("y"), lax.axis_index("c")


class _Rider:
    def __init__(self, ins, out_shapes, n_remote, n_local, plan):
        self.ins, self.out_shapes, self.n_remote, self.n_local, self.plan = list(ins), list(out_shapes), n_remote, n_local, plan

    def scratch(self):
        return [pltpu.SemaphoreType.DMA((self.n_remote,)), pltpu.SemaphoreType.DMA((self.n_remote,)),
                pltpu.SemaphoreType.DMA((max(self.n_local, 1),))]

    def run(self, first, in_refs, out_refs, sems):
        send_sems, recv_sems, local_sems = sems
        sends, recvs, locals_ = self.plan(in_refs, out_refs)
        remote = lambda k, src, dst, to: pltpu.make_async_remote_copy(
            src_ref=src, dst_ref=dst, send_sem=send_sems.at[k], recv_sem=recv_sems.at[k], device_id=to, device_id_type=MESH)
        if first:
            for k, src, dst, to in sends:
                remote(k, src, dst, to).start()
            for j, (src, dst) in enumerate(locals_):
                pltpu.make_async_copy(src, dst, local_sems.at[j]).start()
        else:
            for k, dst, frm in recvs:
                remote(k, dst, dst, frm).wait_recv()
            for k, src, dst, to in sends:
                remote(k, src, dst, to).wait_send()
            for j, (src, dst) in enumerate(locals_):
                pltpu.make_async_copy(src, dst, local_sems.at[j]).wait()


def _other_chips(x, y):
    return [(1 - x, y), (x, 1 - y), (1 - x, 1 - y)]


def _gather_rider(shard):
    hr = shard.shape[0] // 2

    def plan(ins, outs):
        x, y, c = _place()
        mine, src, dst = 2 * x + y, ins[0], outs[0]
        sends, recvs = [], []
        for j, (px, py) in enumerate(_other_chips(x, y)):
            for flip in range(2):
                tc = c if flip == 0 else 1 - c
                sends.append((2 * j + flip, src.at[pl.ds(c * hr, hr)], dst.at[mine, pl.ds(c * hr, hr)], (px, py, tc)))
                recvs.append((2 * j + flip, dst.at[2 * px + py, pl.ds(tc * hr, hr)], (px, py, tc)))
        return sends, recvs, [(src, dst.at[mine])]

    return _Rider([shard], [_sds((4,) + shard.shape, shard.dtype)], 6, 1, plan)


def _sibling_swap_rider(parts):
    n = len(parts)

    def plan(ins, outs):
        x, y, c = _place()
        sibling = (x, y, 1 - c)
        return ([(i, ins[i].at[:, 1 - c], outs[i], sibling) for i in range(n)], [(i, outs[i], sibling) for i in range(n)], [])

    return _Rider(parts, [_sds((4,) + p.shape[2:], p.dtype) for p in parts], n, 0, plan)


def _chip_exchange_rider(parts):
    n = len(parts)

    def plan(ins, outs):
        x, y, c = _place()
        mine = 2 * x + y
        sends, recvs = [], []
        for i in range(n):
            for j, (px, py) in enumerate(_other_chips(x, y)):
                sends.append((3 * i + j, ins[i].at[2 * px + py], outs[i].at[mine], (px, py, c)))
                recvs.append((3 * i + j, outs[i].at[2 * px + py], (px, py, c)))
        return sends, recvs, [(ins[i].at[mine], outs[i].at[mine]) for i in range(n)]

    return _Rider(parts, [_sds(p.shape, p.dtype) for p in parts], 3 * n, n, plan)


def _call(body, name, grid, in_specs, out_specs, out_shape, args, sem, scratch=(), rider=None):
    if rider is None:
        return list(pl.pallas_call(body, name=name, grid=grid, in_specs=in_specs, out_specs=out_specs, out_shape=out_shape,
                                   scratch_shapes=list(scratch), compiler_params=_params(sem))(*args)), []
    n_in, n_out, n_scr = len(in_specs), len(out_specs), len(scratch)
    r_in, r_out = len(rider.ins), len(rider.out_shapes)
    steps = grid[0]

    def riding(*refs):
        ins, r_ins = refs[:n_in], refs[n_in:n_in + r_in]
        outs = refs[n_in + r_in:n_in + r_in + n_out]
        r_outs = refs[n_in + r_in + n_out:n_in + r_in + n_out + r_out]
        scr = refs[n_in + r_in + n_out + r_out:n_in + r_in + n_out + r_out + n_scr]
        sems = refs[n_in + r_in + n_out + r_out + n_scr:]

        @pl.when(pl.program_id(0) == 0)
        def _():
            rider.run(True, r_ins, r_outs, sems)

        body(*ins, *outs, *scr)

        @pl.when(pl.program_id(0) == steps - 1)
        def _():
            rider.run(False, r_ins, r_outs, sems)

    hbm = pl.BlockSpec(memory_space=pl.ANY)
    res = pl.pallas_call(
        riding, name=name, grid=grid, in_specs=list(in_specs) + [hbm] * r_in, out_specs=list(out_specs) + [hbm] * r_out,
        out_shape=list(out_shape) + rider.out_shapes, scratch_shapes=list(scratch) + rider.scratch(),
        compiler_params=_params("arbitrary"),
    )(*args, *rider.ins)
    return list(res[:n_out]), list(res[n_out:])


def _row_tile(seq, want):
    return want if seq % want == 0 else seq


def _rope_tables(posf, seq):
    half = ROPE_DIM // 2
    inv_freq = np.power(np.float32(ROPE_THETA), -np.arange(half, dtype=np.float32) / np.float32(half)).astype(np.float32)
    j = np.arange(LANES) % HEAD_DIM
    invf = jnp.asarray(np.where(j < ROPE_DIM, inv_freq[j % half], 0.0).astype(np.float32).reshape(1, LANES))
    m_p = jnp.asarray(((j >= half) & (j < ROPE_DIM)).astype(np.float32).reshape(1, LANES))
    m_m = jnp.asarray(-(j < half).astype(np.float32).reshape(1, LANES))
    tm = _row_tile(seq, 512)

    def body(pos_ref, invf_ref, mp_ref, mm_ref, cos_ref, sp_ref, sm_ref):
        ang = pos_ref[...] * invf_ref[...]
        s = jnp.sin(ang)
        cos_ref[...] = jnp.cos(ang)
        sp_ref[...] = s * mp_ref[...]
        sm_ref[...] = s * mm_ref[...]

    return pl.pallas_call(
        body, name="rope_tables", grid=(seq // tm,),
        in_specs=[_rows(tm, 1), _full((1, LANES)), _full((1, LANES)), _full((1, LANES))],
        out_specs=[_rows(tm, LANES)] * 3, out_shape=[_sds((seq, LANES))] * 3,
        compiler_params=_params("parallel"),
    )(posf, invf, m_p, m_m)


def _even_proj(x, mod, w_in_t, tabs, seq, rider=None):
    tm = _row_tile(seq, 512)

    def body(x_ref, mod_ref, w_ref, cos_ref, sp_ref, sm_ref, q_ref, k_ref, v_ref, su_ref, sv_ref, g_ref):
        h = x_ref[...] * (1.0 + mod_ref[1:2, :]) + mod_ref[0:1, :]
        p = _mm_nt(h, w_ref[...])
        cos_t, sin_p, sin_m = cos_ref[...], sp_ref[...], sm_ref[...]
        for j in range(ATTN_WIDTH // LANES):
            q_ref[:, j * LANES:(j + 1) * LANES] = _rope(p[:, j * LANES:(j + 1) * LANES], cos_t, sin_p, sin_m).astype(q_ref.dtype)
        k_ref[...] = _rope(p[:, 512:640], cos_t, sin_p, sin_m).astype(k_ref.dtype)
        v_ref[...] = p[:, 640:768].astype(v_ref.dtype)
        su_ref[...] = p[:, 768:1280].astype(su_ref.dtype)
        sv_ref[...] = p[:, 1280:1792].astype(sv_ref.dtype)
        g_ref[...] = p[:, 1792:2816].astype(g_ref.dtype)

    return _call(
        body, "even_proj", (seq // tm,),
        [_rows(tm, D_MODEL), _full((3, D_MODEL)), _const((EVEN_IN, D_MODEL))] + [_rows(tm, LANES)] * 3,
        [_rows(tm, 512), _rows(tm, LANES), _rows(tm, LANES), _rows(tm, 512), _rows(tm, 512), _rows(tm, D_MODEL)],
        [_sds((seq, 512), MXU_DTYPE), _sds((seq, LANES), MXU_DTYPE), _sds((seq, LANES), MXU_DTYPE), _sds((seq, 512), ACT_DTYPE),
         _sds((seq, 512), ACT_DTYPE), _sds((seq, D_MODEL), ACT_DTYPE)],
        (x, mod, w_in_t, *tabs), "parallel", rider=rider)


def _sg_forward(sv, lng, lnb, sgw_ref, sgb, e2):
    vn, vhat, rstd, svo = [], [], [], []
    for j in range(SG_WIDTH // LANES):
        t = sv[:, j * LANES:(j + 1) * LANES]
        mu = _group_sum(t, e2) * (1.0 / HEAD_DIM)
        d = t - mu
        var = _group_sum(d * d, e2) * (1.0 / HEAD_DIM)
        r = lax.rsqrt(var + LN_EPS)
        vh = d * r
        vhat.append(vh)
        rstd.append(r)
        vn.append(vh * lng[:, j * LANES:(j + 1) * LANES] + lnb[:, j * LANES:(j + 1) * LANES])
    lane = _lane_iota((BLK, LANES))
    for j in range(SG_WIDTH // LANES):
        lo = _mm(sgw_ref[2 * j], vn[j])
        hi = _mm(sgw_ref[2 * j + 1], vn[j])
        svo.append(jnp.where(lane < HEAD_DIM, lo, hi) + sgb[:, j * LANES:(j + 1) * LANES])
    return svo, vn, vhat, rstd


def _even_mix(q, k, v, su, sv, sink, sgln_g, sgln_b, sgw, sgb_full, e2, seq, rider=None):
    nb = seq // BLK

    def body(sink_ref, q_ref, k_ref, v_ref, su_ref, sv_ref, lng_ref, lnb_ref, sgw_ref, sgb_ref, e2_ref, ycat_ref, lse_ref):
        n = pl.program_id(0)
        kband = _band(k_ref, n, nb)
        vband = _band(v_ref, n, nb)
        bias = _band_bias(n, seq)
        lane = _lane_iota((BLK, LANES))
        lse = jnp.zeros((BLK, LANES), F32)
        q_tile = lambda j: q_ref[:, j * LANES:(j + 1) * LANES].astype(F32)
        acc = [jnp.zeros((BLK, LANES), F32) for _ in range(ATTN_WIDTH // LANES)]
        for kv in range(N_Q_HEADS // Q_PER_KV):
            heads = range(Q_PER_KV * kv, Q_PER_KV * (kv + 1))
            sink = _per_head_column([sink_ref[h] for h in heads])
            s = _mm_nt(_stack_heads(q_tile, kv), kband) * (HEAD_DIM ** -0.5) + bias
            m = jnp.maximum(jnp.max(s, axis=1, keepdims=True), sink)
            p = jnp.exp(s - m)
            denom = jnp.sum(p, axis=1, keepdims=True) + jnp.exp(sink - m)
            o4 = _mm(p / denom, vband)
            l4 = m + jnp.log(denom)
            for g, h in enumerate(heads):
                acc[h // 2] = acc[h // 2] + _from_kv_lanes(o4[g * BLK:(g + 1) * BLK], h)
                lse = jnp.where(lane == h, l4[g * BLK:(g + 1) * BLK], lse)
        for j in range(ATTN_WIDTH // LANES):
            ycat_ref[:, j * LANES:(j + 1) * LANES] = acc[j].astype(ycat_ref.dtype)
        lse_ref[...] = lse
        svo, _, _, _ = _sg_forward(sv_ref[...].astype(F32), lng_ref[...], lnb_ref[...], sgw_ref, sgb_ref[...], e2_ref[...])
        for j in range(SG_WIDTH // LANES):
            ysg = su_ref[:, j * LANES:(j + 1) * LANES].astype(F32) * svo[j]
            ycat_ref[:, ATTN_WIDTH + j * LANES:ATTN_WIDTH + (j + 1) * LANES] = ysg.astype(ycat_ref.dtype)

    blk = lambda w: pl.BlockSpec((BLK, w), lambda n: (n, 0))
    return _call(
        body, "even_mix", (nb,),
        [pl.BlockSpec(memory_space=pltpu.SMEM), blk(512), _full((seq, LANES)), _full((seq, LANES)), blk(512), blk(512),
         _full((1, 512)), _full((1, 512)), _full((8, BLK, BLK)), _full((BLK, 512)), _full((LANES, LANES))],
        [blk(D_MODEL), blk(LANES)], [_sds((seq, D_MODEL), ACT_DTYPE), _sds((seq, LANES))],
        (sink, q, k, v, su, sv, sgln_g, sgln_b, sgw, sgb_full, e2), "parallel", rider=rider)


def _even_out(ycat, g, x, mod, w_out, ln_g, ln_b, seq, rider=None):
    tm = _row_tile(seq, 512)

    def body(y_ref, g_ref, x_ref, mod_ref, wo_ref, g1_ref, b1_ref, z_ref, x1_ref):
        gg = g_ref[...].astype(F32)
        out = _mm(y_ref[...].astype(F32) * (gg * _sigmoid(gg)), wo_ref[...])
        z = ALPHA * x_ref[...] + mod_ref[2:3, :] * out
        z_ref[...] = z
        zhat, _ = _ln_stats(z)
        x1_ref[...] = zhat * g1_ref[...] + b1_ref[...]

    return _call(
        body, "even_out", (seq // tm,),
        [_rows(tm, D_MODEL)] * 3 + [_full((3, D_MODEL)), _const((D_MODEL, D_MODEL)), _full((1, D_MODEL)), _full((1, D_MODEL))],
        [_rows(tm, D_MODEL)] * 2, [_sds((seq, D_MODEL))] * 2, (ycat, g, x, mod, w_out, ln_g, ln_b), "parallel", rider=rider)


def _odd_proj(x1, mod, w_in4, seq):
    tm = _row_tile(seq, 512)
    cs = ODD_IN // 4

    def body(x_ref, mod_ref, w_ref, xr_ref, g_ref):
        h = x_ref[...] * (1.0 + mod_ref[1:2, :]) + mod_ref[0:1, :]
        hb = h.astype(MXU_DTYPE)
        for s in range(2):
            xr_ref[:, s * cs:(s + 1) * cs] = jnp.dot(hb, w_ref[s], preferred_element_type=F32)
            g_ref[:, s * cs:(s + 1) * cs] = jnp.dot(hb, w_ref[2 + s], preferred_element_type=F32).astype(g_ref.dtype)

    return pl.pallas_call(
        body, name="odd_proj", grid=(seq // tm,),
        in_specs=[_rows(tm, D_MODEL), _full((3, D_MODEL)), _full((4, D_MODEL, cs))],
        out_specs=[_rows(tm, D_MODEL)] * 2,
        out_shape=[_sds((seq, D_MODEL)), _sds((seq, D_MODEL), ACT_DTYPE)],
        compiler_params=_params("parallel"),
    )(x1, mod, w_in4)


def _halo_specs(tm, seq, width, order=lambda i: i):
    per = tm // 8
    last = seq // 8 - 1
    return [pl.BlockSpec((8, width), lambda i: (jnp.maximum(order(i) * per - 1, 0), 0)),
            pl.BlockSpec((tm, width), lambda i: (order(i), 0)),
            pl.BlockSpec((8, width), lambda i: (jnp.minimum((order(i) + 1) * per, last), 0))]


def _extended(prev_ref, main_ref, next_ref, i, n_steps):
    prev = jnp.where(i > 0, prev_ref[...], 0.0)
    nxt = jnp.where(i < n_steps - 1, next_ref[...], 0.0)
    return jnp.concatenate([prev, main_ref[...], nxt], axis=0)


def _shifted(ext, off, tm):
    if off == 0:
        return ext[8:8 + tm]
    return pltpu.roll(ext, (-off) % ext.shape[0], 0)[8:8 + tm]


SCAN_SUB = 8


def _lru_gate(xh, pre, bias, sp, hs, d):
    r = _sigmoid(pre[:, 0:LANES] + bias[2 * d:2 * d + 1, hs])
    ig = _sigmoid(pre[:, LANES:2 * LANES] + bias[2 * d + 1:2 * d + 2, hs])
    neg_log_a = RG_LRU_C * r * sp[d:d + 1, hs]
    a = jnp.exp(-neg_log_a)
    s = jnp.sqrt(jnp.tanh(neg_log_a) * (a * a + 1.0))
    return r, ig, a, s


def _conv_block(xp_ref, xm_ref, xn_ref, cw_ref, cb_ref, blk, steps, tm):
    ext = _extended(xp_ref, xm_ref, xn_ref, blk, steps)
    return cb_ref[...] + sum(cw_ref[kk:kk + 1, :] * _shifted(ext, kk - 2, tm) for kk in range(4))


def _scan_tiles(a_ref, b_ref, h_ref, hprev_ref, carry_h, carry_a, rows, descending, post):
    sub = SCAN_SUB
    tiles = rows // sub
    row = lax.broadcasted_iota(jnp.int32, (sub, D_MODEL), 0)

    def shift(v, d, fill):
        if descending:
            return jnp.where(row <= sub - 1 - d, pltpu.roll(v, sub - d, 0), fill)
        return jnp.where(row >= d, pltpu.roll(v, d, 0), fill)

    def last(v):
        return jnp.broadcast_to(v[0:1, :] if descending else v[sub - 1:sub, :], v.shape)

    def tile(j, c):
        ch, ca = c
        r0 = pl.multiple_of(((tiles - 1 - j) if descending else j) * sub, sub)
        at = a_ref[pl.ds(r0, sub), :]
        bt = b_ref[pl.ds(r0, sub), :]
        coef = shift(at, 1, ca) if post else at
        acc_a, acc_b = coef, bt
        for d in (1, 2, 4):
            acc_b = acc_b + acc_a * shift(acc_b, d, 0.0)
            acc_a = acc_a * shift(acc_a, d, 1.0)
        h = acc_b + acc_a * ch
        h_ref[pl.ds(r0, sub), :] = h
        if post:
            return last(h), last(at)
        hprev_ref[pl.ds(r0, sub), :] = shift(h, 1, ch)
        return last(h), ca

    ch, ca = lax.fori_loop(0, tiles, tile, (carry_h[...], carry_a[...]), unroll=4)
    carry_h[...] = ch
    carry_a[...] = ca


def _lru_fwd(xr, conv_w, conv_b, wcat, bias, lam, seq, d):
    tb = _row_tile(seq, 512)
    steps = seq // tb
    descending = d == 1
    order = (lambda i: steps - 1 - i) if descending else (lambda i: i)

    def body(xp_ref, xm_ref, xn_ref, cw_ref, cb_ref, w_ref, bias_ref, lam_ref, h_ref, hp_ref, a_scr, b_scr, carry_h, carry_a):
        i = pl.program_id(0)

        @pl.when(i == 0)
        def _():
            carry_h[...] = jnp.zeros_like(carry_h)
            carry_a[...] = jnp.zeros_like(carry_a)

        xc = _conv_block(xp_ref, xm_ref, xn_ref, cw_ref, cb_ref, order(i), steps, tb)
        sp, _ = _softplus_neg(lam_ref[...])
        bias = bias_ref[...]
        for h in range(RNN_HEADS):
            hs = slice(h * LANES, (h + 1) * LANES)
            xh = xc[:, hs]
            _, ig, a, s = _lru_gate(xh, _mm(xh, w_ref[h, :, 2 * d * LANES:2 * (d + 1) * LANES]), bias, sp, hs, d)
            a_scr[:, hs] = a
            b_scr[:, hs] = s * ig * xh
        _scan_tiles(a_scr, b_scr, h_ref, hp_ref, carry_h, carry_a, tb, descending, post=False)

    out_spec = pl.BlockSpec((tb, D_MODEL), lambda i: (order(i), 0))
    return pl.pallas_call(
        body, name="lru_fwd_%d" % d, grid=(steps,),
        in_specs=_halo_specs(tb, seq, D_MODEL, order) + [_full((4, D_MODEL)), _full((1, D_MODEL)), _full((8, LANES, 512)),
                                                         _full((4, D_MODEL)), _full((2, D_MODEL))],
        out_specs=[out_spec] * 2, out_shape=[_sds((seq, D_MODEL))] * 2,
        scratch_shapes=[pltpu.VMEM((tb, D_MODEL), F32)] * 2 + [pltpu.VMEM((SCAN_SUB, D_MODEL), F32)] * 2,
        compiler_params=_params("arbitrary"),
    )(xr, xr, xr, conv_w, conv_b, wcat, bias, lam)


def _odd_out_and_loss(hf, hr, g, x1, tgt, mod, w_out, ln_g, ln_b, seq):
    tm = _row_tile(seq, 512)

    def body(hf_ref, hr_ref, g_ref, x_ref, t_ref, mod_ref, w_ref, lg_ref, lb_ref,
             dhs_ref, dg_ref, dres_ref, loss_ref, dw_ref, vec_ref):
        @pl.when(pl.program_id(0) == 0)
        def _():
            loss_ref[...] = jnp.zeros_like(loss_ref)
            dw_ref[...] = jnp.zeros_like(dw_ref)
            vec_ref[...] = jnp.zeros_like(vec_ref)

        gg = g_ref[...].astype(F32)
        sg = _sigmoid(gg)
        silu = gg * sg
        hsum = hf_ref[...] + hr_ref[...]
        y = hsum * silu
        out = _mm(y, w_ref[...])
        gate = mod_ref[2:3, :]
        z = ALPHA * x_ref[...] + gate * out
        zhat, rstd = _ln_stats(z)
        x2 = zhat * lg_ref[...] + lb_ref[...]
        err = x2 - t_ref[...]
        loss_ref[...] += 0.5 * jnp.sum(jnp.mean(err * err, axis=-1, keepdims=True))
        dx2 = err * (1.0 / D_MODEL)
        dz = _ln_bwd(dx2, zhat, rstd, lg_ref[...])
        vec_ref[0:1, :] += jnp.sum(dx2 * zhat, axis=0, keepdims=True)
        vec_ref[1:2, :] += jnp.sum(dx2, axis=0, keepdims=True)
        vec_ref[2:3, :] += jnp.sum(dz * out, axis=0, keepdims=True)
        dres_ref[...] = ALPHA * dz
        dout = gate * dz
        dw_ref[...] += _mm_tn(y, dout)
        dy = _mm_nt(dout, w_ref[...])
        dhs_ref[...] = dy * silu
        dg_ref[...] = (dy * hsum * (sg * (1.0 + gg * (1.0 - sg)))).astype(dg_ref.dtype)

    return pl.pallas_call(
        body, name="odd_out_loss", grid=(seq // tm,),
        in_specs=[_rows(tm, D_MODEL)] * 5 + [_full((3, D_MODEL)), _const((D_MODEL, D_MODEL)),
                                             _full((1, D_MODEL)), _full((1, D_MODEL))],
        out_specs=[_rows(tm, D_MODEL)] * 3 + [_full((8, LANES)), _full((D_MODEL, D_MODEL)), _full((8, D_MODEL))],
        out_shape=[_sds((seq, D_MODEL)), _sds((seq, D_MODEL), ACT_DTYPE), _sds((seq, D_MODEL)), _sds((8, LANES)),
                   _sds((D_MODEL, D_MODEL)), _sds((8, D_MODEL))],
        compiler_params=_params("arbitrary"),
    )(hf, hr, g, x1, tgt, mod, w_out, ln_g, ln_b)


def _lru_bwd(xr, dhs, hprev, conv_w, conv_b, wcat, bias, lam, seq, d):
    tb = _row_tile(seq, 512)
    steps = seq // tb
    descending = d == 0
    order = (lambda i: steps - 1 - i) if descending else (lambda i: i)
    cols = slice(2 * d * LANES, 2 * (d + 1) * LANES)

    def body(xp_ref, xm_ref, xn_ref, dhs_ref, hp_ref, cw_ref, cb_ref, w_ref, bias_ref, lam_ref, dxc_ref, dw_ref, vec_ref,
             xc_scr, a_scr, g_scr, r_scr, i_scr, s_scr, carry_h, carry_a):
        i = pl.program_id(0)

        @pl.when(i == 0)
        def _():
            dw_ref[...] = jnp.zeros_like(dw_ref)
            vec_ref[...] = jnp.zeros_like(vec_ref)
            carry_h[...] = jnp.zeros_like(carry_h)
            carry_a[...] = jnp.zeros_like(carry_a)

        xc_scr[...] = _conv_block(xp_ref, xm_ref, xn_ref, cw_ref, cb_ref, order(i), steps, tb)
        sp, dsp = _softplus_neg(lam_ref[...])
        bias = bias_ref[...]
        for h in range(RNN_HEADS):
            hs = slice(h * LANES, (h + 1) * LANES)
            xh = xc_scr[:, hs]
            r_scr[:, hs], i_scr[:, hs], a_scr[:, hs], s_scr[:, hs] = _lru_gate(xh, _mm(xh, w_ref[h, :, cols]), bias, sp, hs, d)
        _scan_tiles(a_scr, dhs_ref, g_scr, None, carry_h, carry_a, tb, descending, post=True)
        for h in range(RNN_HEADS):
            hs = slice(h * LANES, (h + 1) * LANES)
            xh, r, ig, a, s = xc_scr[:, hs], r_scr[:, hs], i_scr[:, hs], a_scr[:, hs], s_scr[:, hs]
            db = g_scr[:, hs]
            da = db * hp_ref[:, hs]
            dlog_a = da * a - (db * ig * xh) * (a * a / s)
            dpr = dlog_a * (-RG_LRU_C) * sp[d:d + 1, hs] * r * (1.0 - r)
            dpi = db * s * xh * ig * (1.0 - ig)
            vec_ref[0:1, hs] += jnp.sum(dpr, axis=0, keepdims=True)
            vec_ref[1:2, hs] += jnp.sum(dpi, axis=0, keepdims=True)
            vec_ref[2:3, hs] += jnp.sum(dlog_a * r, axis=0, keepdims=True) * (-RG_LRU_C) * dsp[d:d + 1, hs]
            dcat = jnp.concatenate([dpr, dpi], axis=1)
            dw_ref[h] += _mm_tn(xh, dcat)
            dxc_ref[:, hs] = db * s * ig + _mm_nt(dcat, w_ref[h, :, cols])

    row_spec = pl.BlockSpec((tb, D_MODEL), lambda i: (order(i), 0))
    return pl.pallas_call(
        body, name="lru_bwd_%d" % d, grid=(steps,),
        in_specs=_halo_specs(tb, seq, D_MODEL, order) + [row_spec, row_spec, _full((4, D_MODEL)), _full((1, D_MODEL)),
                                                         _full((8, LANES, 512)), _full((4, D_MODEL)), _full((2, D_MODEL))],
        out_specs=[row_spec, _full((8, LANES, 2 * LANES)), _full((8, D_MODEL))],
        out_shape=[_sds((seq, D_MODEL)), _sds((8, LANES, 2 * LANES)), _sds((8, D_MODEL))],
        scratch_shapes=[pltpu.VMEM((tb, D_MODEL), F32)] * 6 + [pltpu.VMEM((SCAN_SUB, D_MODEL), F32)] * 2,
        compiler_params=_params("arbitrary"),
    )(xr, xr, xr, dhs, hprev, conv_w, conv_b, wcat, bias, lam)


def _odd_proj_bwd(dxc_f, dxc_r, xr, dg, x1, dres, mod, conv_w, w_in4, seq):
    tm = _row_tile(seq, 512)
    steps = seq // tm

    def body(fp_ref, fm_ref, fn_ref, rp_ref, rm_ref, rn_ref, xp_ref, xm_ref, xn_ref, dg_ref, x_ref, dres_ref, mod_ref, cw_ref,
             w_ref, dx_ref, dw_ref, vec_ref, dpb_ref):
        i = pl.program_id(0)

        @pl.when(i == 0)
        def _():
            vec_ref[...] = jnp.zeros_like(vec_ref)
            dw_ref[...] = jnp.zeros_like(dw_ref)

        dext = _extended(fp_ref, fm_ref, fn_ref, i, steps) + _extended(rp_ref, rm_ref, rn_ref, i, steps)
        xext = _extended(xp_ref, xm_ref, xn_ref, i, steps)
        dxc_m = fm_ref[...] + rm_ref[...]
        dxr = sum(cw_ref[kk:kk + 1, :] * _shifted(dext, 2 - kk, tm) for kk in range(4))
        for kk in range(4):
            vec_ref[kk:kk + 1, :] += jnp.sum(dxc_m * _shifted(xext, kk - 2, tm), axis=0, keepdims=True)
        vec_ref[4:5, :] += jnp.sum(dxc_m, axis=0, keepdims=True)
        dpb_ref[:, :D_MODEL] = dxr.astype(dpb_ref.dtype)
        dpb_ref[:, D_MODEL:] = dg_ref[...].astype(dpb_ref.dtype)
        cs = ODD_IN // 4
        dh = sum(_mm_nt(dpb_ref[:, s * cs:(s + 1) * cs], w_ref[s]) for s in range(4))
        x = x_ref[...]
        h_t = (x * (1.0 + mod_ref[1:2, :]) + mod_ref[0:1, :]).T.astype(MXU_DTYPE)
        for s in range(4):
            dw_ref[s] += jnp.dot(h_t, dpb_ref[:, s * cs:(s + 1) * cs], preferred_element_type=F32)
        vec_ref[5:6, :] += jnp.sum(dh, axis=0, keepdims=True)
        vec_ref[6:7, :] += jnp.sum(dh * x, axis=0, keepdims=True)
        dx_ref[...] = dres_ref[...] + dh * (1.0 + mod_ref[1:2, :])

    return pl.pallas_call(
        body, name="odd_proj_bwd", grid=(steps,),
        in_specs=_halo_specs(tm, seq, D_MODEL) * 3 + [_rows(tm, D_MODEL)] * 3
        + [_full((3, D_MODEL)), _full((4, D_MODEL)), _const((4, D_MODEL, ODD_IN // 4))],
        out_specs=[_rows(tm, D_MODEL), _const((4, D_MODEL, ODD_IN // 4)), _full((8, D_MODEL))],
        out_shape=[_sds((seq, D_MODEL)), _sds((4, D_MODEL, ODD_IN // 4)), _sds((8, D_MODEL))],
        scratch_shapes=[pltpu.VMEM((tm, ODD_IN), MXU_DTYPE)],
        compiler_params=_params("arbitrary"),
    )(dxc_f, dxc_f, dxc_f, dxc_r, dxc_r, dxc_r, xr, xr, xr, dg, x1, dres, mod, conv_w, w_in4)


def _even_out_bwd(dx1, z, ycat, g, mod, ln_g, w_out, seq, rider=None):
    tm = _row_tile(seq, 512)
    steps = seq // tm

    def body(dx_ref, z_ref, y_ref, g_ref, mod_ref, lg_ref, w_ref, dy_ref, dg_ref, dres_ref, dw_ref, vec_ref):
        i = pl.program_id(0)

        @pl.when(i == 0)
        def _():
            dw_ref[...] = jnp.zeros_like(dw_ref)
            vec_ref[...] = jnp.zeros_like(vec_ref)

        zhat, rstd = _ln_stats(z_ref[...])
        dx1_ = dx_ref[...]
        dz = _ln_bwd(dx1_, zhat, rstd, lg_ref[...])
        vec_ref[0:1, :] += jnp.sum(dx1_ * zhat, axis=0, keepdims=True)
        vec_ref[1:2, :] += jnp.sum(dx1_, axis=0, keepdims=True)
        dres_ref[...] = ALPHA * dz
        gate = mod_ref[2:3, :]
        gg = g_ref[...].astype(F32)
        sg = _sigmoid(gg)
        silu = gg * sg
        ycat_ = y_ref[...].astype(F32)
        dw_ref[...] += _mm_tn(ycat_ * silu, dz)
        dy = _mm_nt(gate * dz, w_ref[...])
        dy_ref[...] = (dy * silu).astype(dy_ref.dtype)
        dg_ref[...] = (dy * ycat_ * (sg * (1.0 + gg * (1.0 - sg)))).astype(dg_ref.dtype)

        @pl.when(i == steps - 1)
        def _():
            m_acc = dw_ref[...]
            vec_ref[2:3, :] = jnp.sum(w_ref[...].astype(F32) * m_acc, axis=0, keepdims=True)
            dw_ref[...] = m_acc * gate

    return _call(
        body, "even_out_bwd", (steps,),
        [_rows(tm, D_MODEL)] * 4 + [_full((3, D_MODEL)), _full((1, D_MODEL)), _const((D_MODEL, D_MODEL))],
        [_rows(tm, D_MODEL)] * 3 + [_full((D_MODEL, D_MODEL)), _full((8, D_MODEL))],
        [_sds((seq, D_MODEL), ACT_DTYPE), _sds((seq, D_MODEL), ACT_DTYPE), _sds((seq, D_MODEL)), _sds((D_MODEL, D_MODEL)),
         _sds((8, D_MODEL))],
        (dx1, z, ycat, g, mod, ln_g, w_out), "arbitrary", rider=rider)


def _even_mix_bwd(q, k, v, lse, ycat, dycat, su, sv, sink, sgln_g, sgln_b, sgw, sgb_full, e2, e8, seq, rider=None):
    nb = seq // BLK

    def body(sink_ref, q_ref, k_ref, v_ref, lse_ref, y_ref, dy_ref, su_ref, sv_ref, lng_ref, lnb_ref, sgw_ref, sgb_ref, e2_ref,
             e8_ref, dq_ref, dsu_ref, dsv_ref, dk_ref, dv_ref, dsgw_ref, dsgb_ref, vec_ref, dsink_ref, dsgb_acc):
        n = pl.program_id(0)

        @pl.when(n == 0)
        def _():
            dk_ref[...] = jnp.zeros_like(dk_ref)
            dv_ref[...] = jnp.zeros_like(dv_ref)
            dsgw_ref[...] = jnp.zeros_like(dsgw_ref)
            dsgb_acc[...] = jnp.zeros_like(dsgb_acc)
            vec_ref[...] = jnp.zeros_like(vec_ref)
            dsink_ref[...] = jnp.zeros_like(dsink_ref)

        kband = _band(k_ref, n, nb)
        vband = _band(v_ref, n, nb)
        bias = _band_bias(n, seq)
        lane = _lane_iota((BLK, LANES))
        row8 = lax.broadcasted_iota(jnp.int32, (8, LANES), 0)
        lse = lse_ref[...]
        dkb = jnp.zeros((LANES, 3 * BLK), F32)
        dvb = jnp.zeros((LANES, 3 * BLK), F32)
        dsink = jnp.zeros((8, LANES), F32)
        q_tile = lambda j: q_ref[:, j * LANES:(j + 1) * LANES].astype(F32)
        do_tile = lambda j: dy_ref[:, j * LANES:(j + 1) * LANES].astype(F32)
        dq = [jnp.zeros((BLK, LANES), F32) for _ in range(ATTN_WIDTH // LANES)]
        for kv in range(N_Q_HEADS // Q_PER_KV):
            heads = range(Q_PER_KV * kv, Q_PER_KV * (kv + 1))
            lse4, delta4 = [], []
            for h in heads:
                head_lanes = (lane < HEAD_DIM) if h % 2 == 0 else (lane >= HEAD_DIM)
                lse4.append(jnp.sum(jnp.where(lane == h, lse, 0.0), axis=1, keepdims=True))
                o_tile = y_ref[:, (h // 2) * LANES:(h // 2 + 1) * LANES].astype(F32)
                delta4.append(jnp.sum(jnp.where(head_lanes, do_tile(h // 2) * o_tile, 0.0), axis=1, keepdims=True))
            lse4, delta4 = jnp.concatenate(lse4, axis=0), jnp.concatenate(delta4, axis=0)
            q4, do4 = _stack_heads(q_tile, kv), _stack_heads(do_tile, kv)
            s = _mm_nt(q4, kband) * (HEAD_DIM ** -0.5) + bias
            p = jnp.exp(s - lse4)
            wsink = jnp.exp(_per_head_column([sink_ref[h] for h in heads]) - lse4) * delta4
            ds = p * (_mm_nt(do4, vband) - delta4) * (HEAD_DIM ** -0.5)
            dq4 = _mm(ds, kband)
            dkb = dkb + _mm_tn(q4, ds)
            dvb = dvb + _mm_tn(do4, p)
            for g, h in enumerate(heads):
                dq[h // 2] = dq[h // 2] + _from_kv_lanes(dq4[g * BLK:(g + 1) * BLK], h)
                dsink = dsink + jnp.where(row8 == h, -jnp.sum(wsink[g * BLK:(g + 1) * BLK]), 0.0)
        for j in range(ATTN_WIDTH // LANES):
            dq_ref[:, j * LANES:(j + 1) * LANES] = dq[j].astype(dq_ref.dtype)
        dsink_ref[...] += dsink
        prev = jnp.maximum(n - 1, 0)
        nxt = jnp.minimum(n + 1, nb - 1)
        for part, blk_i in enumerate((prev, n, nxt)):
            rows = pl.ds(pl.multiple_of(blk_i * BLK, BLK), BLK)
            dk_ref[rows, :] += dkb[:, part * BLK:(part + 1) * BLK].T
            dv_ref[rows, :] += dvb[:, part * BLK:(part + 1) * BLK].T

        e2 = e2_ref[...]
        lng = lng_ref[...]
        svo, vn, vhat, rstd = _sg_forward(sv_ref[...].astype(F32), lng, lnb_ref[...], sgw_ref, sgb_ref[...], e2)
        for j in range(SG_WIDTH // LANES):
            cs = slice(j * LANES, (j + 1) * LANES)
            dysg = dy_ref[:, ATTN_WIDTH + j * LANES:ATTN_WIDTH + (j + 1) * LANES].astype(F32)
            dsu_ref[:, cs] = (dysg * svo[j]).astype(dsu_ref.dtype)
            dsvo = dysg * su_ref[:, cs].astype(F32)
            dsgb_acc[:, cs] += dsvo
            d_lo = jnp.where(lane < HEAD_DIM, dsvo, 0.0)
            d_hi = dsvo - d_lo
            dsgw_ref[2 * j] += _mm_nt(d_lo, vn[j])
            dsgw_ref[2 * j + 1] += _mm_nt(d_hi, vn[j])
            dvn = _mm_tn(sgw_ref[2 * j], d_lo) + _mm_tn(sgw_ref[2 * j + 1], d_hi)
            vec_ref[0:1, cs] += jnp.sum(dvn * vhat[j], axis=0, keepdims=True)
            vec_ref[1:2, cs] += jnp.sum(dvn, axis=0, keepdims=True)
            dvh = dvn * lng[:, cs]
            m1 = _group_sum(dvh, e2) * (1.0 / HEAD_DIM)
            m2 = _group_sum(dvh * vhat[j], e2) * (1.0 / HEAD_DIM)
            dsv_ref[:, cs] = (rstd[j] * (dvh - m1 - vhat[j] * m2)).astype(dsv_ref.dtype)

        @pl.when(n == nb - 1)
        def _():
            rest = dsgb_acc[...]
            total = jnp.zeros((8, BLK), F32)
            for _ in range(3):
                part = rest.astype(MXU_DTYPE)
                total = total + lax.dot_general(e8_ref[...], part, (((1,), (1,)), ((), ())), preferred_element_type=F32)
                rest = rest - part.astype(F32)
            dsgb_ref[...] = total

    blk = lambda w: pl.BlockSpec((BLK, w), lambda n: (n, 0))
    return _call(
        body, "even_mix_bwd", (nb,),
        [pl.BlockSpec(memory_space=pltpu.SMEM), blk(512), _full((seq, LANES)), _full((seq, LANES)), blk(LANES),
         blk(D_MODEL), blk(D_MODEL), blk(512), blk(512), _full((1, 512)), _full((1, 512)), _full((8, BLK, BLK)),
         _full((BLK, 512)), _full((LANES, LANES)), _full((8, 512))],
        [blk(512), blk(512), blk(512), _full((seq, LANES)), _full((seq, LANES)), _full((8, BLK, BLK)),
         _full((8, BLK)), _full((8, 512)), _full((8, LANES))],
        [_sds((seq, 512), ACT_DTYPE), _sds((seq, 512), ACT_DTYPE), _sds((seq, 512), ACT_DTYPE), _sds((seq, LANES)), _sds((seq, LANES)),
         _sds((8, BLK, BLK)), _sds((8, BLK)), _sds((8, 512)), _sds((8, LANES))],
        (sink, q, k, v, lse, ycat, dycat, su, sv, sgln_g, sgln_b, sgw, sgb_full, e2, e8), "arbitrary",
        scratch=[pltpu.VMEM((BLK, 512), F32)], rider=rider)


def _even_proj_bwd(dq, dk, dv, dsu, dsv, dg, x, dres, mod, tabs, w_in_t, seq):
    tm = _row_tile(seq, 512)

    def body(dq_ref, dk_ref, dv_ref, dsu_ref, dsv_ref, dg_ref, x_ref, dres_ref, mod_ref, cos_ref, sp_ref, sm_ref, wt_ref,
             dx_ref, dw_ref, vec_ref, dpb_ref):
        @pl.when(pl.program_id(0) == 0)
        def _():
            vec_ref[...] = jnp.zeros_like(vec_ref)
            dw_ref[...] = jnp.zeros_like(dw_ref)

        cos_t, sin_p, sin_m = cos_ref[...], sp_ref[...], sm_ref[...]
        dt = dpb_ref.dtype
        for j in range(ATTN_WIDTH // LANES):
            cs = slice(j * LANES, (j + 1) * LANES)
            dpb_ref[:, cs] = _rope_t(dq_ref[:, cs].astype(F32), cos_t, sin_p, sin_m).astype(dt)
        dpb_ref[:, 512:640] = _rope_t(dk_ref[...], cos_t, sin_p, sin_m).astype(dt)
        dpb_ref[:, 640:768] = dv_ref[...].astype(dt)
        dpb_ref[:, 768:1280] = dsu_ref[...].astype(dt)
        dpb_ref[:, 1280:1792] = dsv_ref[...].astype(dt)
        dpb_ref[:, 1792:2816] = dg_ref[...].astype(dt)
        dpb = dpb_ref[...]
        dh = jnp.dot(dpb, wt_ref[...], preferred_element_type=F32)
        x_ = x_ref[...]
        hb = (x_ * (1.0 + mod_ref[1:2, :]) + mod_ref[0:1, :]).astype(MXU_DTYPE)
        dw_ref[...] += _mm_tn(dpb, hb)
        vec_ref[0:1, :] += jnp.sum(dh, axis=0, keepdims=True)
        vec_ref[1:2, :] += jnp.sum(dh * x_, axis=0, keepdims=True)
        dx_ref[...] = dres_ref[...] + dh * (1.0 + mod_ref[1:2, :])

    return pl.pallas_call(
        body, name="even_proj_bwd", grid=(seq // tm,),
        in_specs=[_rows(tm, 512), _rows(tm, LANES), _rows(tm, LANES), _rows(tm, 512), _rows(tm, 512), _rows(tm, D_MODEL),
                  _rows(tm, D_MODEL), _rows(tm, D_MODEL), _full((3, D_MODEL))] + [_rows(tm, LANES)] * 3
        + [_const((EVEN_IN, D_MODEL))],
        out_specs=[_rows(tm, D_MODEL), _const((EVEN_IN, D_MODEL)), _full((8, D_MODEL))],
        out_shape=[_sds((seq, D_MODEL)), _sds((EVEN_IN, D_MODEL)), _sds((8, D_MODEL))],
        scratch_shapes=[pltpu.VMEM((tm, EVEN_IN), MXU_DTYPE)],
        compiler_params=_params("arbitrary"),
    )(dq, dk, dv, dsu, dsv, dg, x, dres, mod, *tabs, w_in_t)


def _local_step(x, posf, tgt, mod, w, seq, ride=None):
    rid = lambda make, *a: None if ride is None else make(*a)
    mxu = lambda a: a.astype(MXU_DTYPE)
    row = lambda a: a.reshape(1, -1)
    tabs = _rope_tables(posf, seq)
    e2 = mxu(jnp.kron(jnp.eye(2, dtype=F32), jnp.ones((HEAD_DIM, HEAD_DIM), F32)))
    e8 = mxu(jnp.repeat(jnp.eye(N_SG_GROUPS, dtype=F32), HEAD_DIM, axis=1))
    sgw = mxu(w["ev_sg_w"])
    sgb_full = jnp.repeat(w["ev_sg_b"].T, HEAD_DIM, axis=1)
    sgln_g, sgln_b = row(w["ev_sg_ln_g"]), row(w["ev_sg_ln_b"])
    sink = w["ev_sink"].reshape(N_Q_HEADS)
    ev_w_in_t = mxu(w["ev_w_in_t"])
    if ride is None:
        ev_w_out, od_w_in, od_w_out = mxu(w["ev_w_out"]), mxu(w["od_w_in"]), mxu(w["od_w_out"])
    wcat = mxu(jnp.concatenate([w["od_w_a"][0], w["od_w_x"][0], w["od_w_a"][1], w["od_w_x"][1]], axis=2))
    gate_bias = jnp.stack([w["od_b_a"][0], w["od_b_x"][0], w["od_b_a"][1], w["od_b_x"][1]])
    conv_b = row(w["od_conv_b"])
    ln_g, ln_b = w["ln_g"], w["ln_b"]

    (q, k, v, su, sv, g0), got = _even_proj(x, mod[0], ev_w_in_t, tabs, seq, rid(_gather_rider, ride and ride["ev_w_out"]))
    if ride is not None:
        ev_w_out = got[0].reshape(D_MODEL, D_MODEL)
    (ycat, lse), got = _even_mix(q, k, v, su, sv, sink, sgln_g, sgln_b, sgw, sgb_full, e2, seq,
                                 rid(_gather_rider, ride and ride["od_w_in"]))
    if ride is not None:
        od_w_in = got[0]
    (z0, x1), got = _even_out(ycat, g0, x, mod[0], ev_w_out, ln_g[0:1], ln_b[0:1], seq, rid(_gather_rider, ride and ride["od_w_out"]))
    if ride is not None:
        od_w_out = got[0].reshape(D_MODEL, D_MODEL)
    xr, g1 = _odd_proj(x1, mod[1], od_w_in, seq)
    lru = (xr, w["od_conv_w"], conv_b, wcat, gate_bias, w["od_lam"], seq)
    hf, hpf = _lru_fwd(*lru, 0)
    hr, hpr = _lru_fwd(*lru, 1)
    dhs, dg1, dres1, loss, d_od_w_out, vec_o = _odd_out_and_loss(hf, hr, g1, x1, tgt, mod[1], od_w_out, ln_g[1:2], ln_b[1:2], seq)
    dxc_f, dw_f, vec_f = _lru_bwd(xr, dhs, hpf, *lru[1:], 0)
    dxc_r, dw_r, vec_r = _lru_bwd(xr, dhs, hpr, *lru[1:], 1)
    dx1, d_od_w_in, vec_p = _odd_proj_bwd(dxc_f, dxc_r, xr, dg1, x1, dres1, mod[1], w["od_conv_w"], od_w_in, seq)
    d_od_w_a = jnp.stack([dw_f[:, :, 0:128], dw_r[:, :, 0:128]])
    d_od_w_x = jnp.stack([dw_f[:, :, 128:256], dw_r[:, :, 128:256]])
    od_parts = [d_od_w_in.reshape(4, 2, 512, 512), d_od_w_out.reshape(4, 2, 128, D_MODEL),
                d_od_w_a.reshape(4, 2, 2 * BLK, BLK), d_od_w_x.reshape(4, 2, 2 * BLK, BLK)]
    (dycat, dg0, dres0, d_ev_w_out, vec_e), got_od = _even_out_bwd(dx1, z0, ycat, g0, mod[0], ln_g[0:1], ev_w_out, seq,
                                                                   rid(_sibling_swap_rider, od_parts))
    if ride is not None:
        od_sums = _sum_sibling(ride["core"], od_parts, got_od, [ride["wire"]] * 4, "sum_sibling_od")
    (dq, dsu, dsv, dk, dv, d_sgw, d_sgb, vec_s, d_sink), od_slots = _even_mix_bwd(
        q, k, v, lse, ycat, dycat, su, sv, sink, sgln_g, sgln_b, sgw, sgb_full, e2, e8, seq,
        rid(_chip_exchange_rider, ride and od_sums))
    grad_x, d_ev_w_in_t, vec_x = _even_proj_bwd(dq, dk, dv, dsu, dsv, dg0, x, dres0, mod[0], tabs, ev_w_in_t, seq)

    dmod = jnp.stack([jnp.stack([vec_x[0], vec_x[1], vec_e[2]]), jnp.stack([vec_p[5], vec_p[6], vec_o[2]])])
    grads = {
        "ln_g": jnp.stack([vec_e[0], vec_o[0]]), "ln_b": jnp.stack([vec_e[1], vec_o[1]]),
        "ev_w_in_t": d_ev_w_in_t, "ev_w_out": d_ev_w_out, "ev_sink": d_sink[:, 0],
        "ev_sg_ln_g": vec_s[0], "ev_sg_ln_b": vec_s[1], "ev_sg_w": d_sgw,
        "ev_sg_b": d_sgb,
        "od_conv_w": vec_p[0:4], "od_conv_b": vec_p[4],
        "od_b_a": jnp.stack([vec_f[0], vec_r[0]]), "od_b_x": jnp.stack([vec_f[1], vec_r[1]]),
        "od_lam": jnp.stack([vec_f[2], vec_r[2]]),
    }
    if ride is None:
        grads.update({"od_w_in": d_od_w_in, "od_w_out": d_od_w_out, "od_w_a": d_od_w_a, "od_w_x": d_od_w_x})
    else:
        grads["od_slots"] = od_slots
    return loss[0, 0], grad_x, dmod, grads


def _allgather8(block, name):
    m_per, n = block.shape

    def body(x_ref, out_ref, send_sems, recv_sems, local_sem):
        x, y, c = _place()
        me, sibling = (x, y, c), (x, y, 1 - c)
        chips = [(1 - x, y), (x, 1 - y), (1 - x, 1 - y)]

        def rows(px, py, pc):
            return out_ref.at[pl.ds((4 * px + 2 * py + pc) * m_per, m_per), :]

        def copy(k, blk, to, src=None):
            return pltpu.make_async_remote_copy(src_ref=rows(*blk) if src is None else src, dst_ref=rows(*blk),
                                                send_sem=send_sems.at[k], recv_sem=recv_sems.at[k], device_id=to,
                                                device_id_type=MESH)

        mine = pltpu.make_async_copy(x_ref, rows(*me), local_sem)
        mine.start()
        first = [copy(0, me, sibling, src=x_ref)] + [copy(1 + j, me, (*chip, c), src=x_ref) for j, chip in enumerate(chips)]
        for cp in first:
            cp.start()
        passed = [copy(4 + j, (*chip, c), sibling) for j, chip in enumerate(chips)]
        for j, chip in enumerate(chips):
            copy(1 + j, (*chip, c), me).wait_recv()
            passed[j].start()
        copy(0, sibling, me).wait_recv()
        for j, chip in enumerate(chips):
            copy(4 + j, (*chip, 1 - c), me).wait_recv()
        for cp in first + passed:
            cp.wait_send()
        mine.wait()

    return pl.pallas_call(
        body, name=name, out_shape=_sds((8 * m_per, n), block.dtype),
        in_specs=[pl.BlockSpec(memory_space=pltpu.VMEM)], out_specs=pl.BlockSpec(memory_space=pltpu.VMEM),
        scratch_shapes=[pltpu.SemaphoreType.DMA((7,)), pltpu.SemaphoreType.DMA((7,)), pltpu.SemaphoreType.DMA],
        compiler_params=pltpu.CompilerParams(vmem_limit_bytes=VMEM_LIMIT),
    )(block)


class _Copies:
    def __init__(self, send_sems, recv_sems, local_sems, stages):
        self.send_sems, self.recv_sems, self.local_sems, self.stages = send_sems, recv_sems, local_sems, stages
        self.sent, self.staged, self.locals = [], [], []

    def remote(self, k, src, dst, to):
        return pltpu.make_async_remote_copy(src_ref=src, dst_ref=dst, send_sem=self.send_sems.at[k], recv_sem=self.recv_sems.at[k],
                                            device_id=to, device_id_type=MESH)

    def send(self, k, src, dst, to):
        cp = self.remote(k, src, dst, to)
        cp.start()
        self.sent.append(cp)

    def arrived(self, k, dst, frm):
        self.remote(k, dst, dst, frm).wait_recv()

    def local(self, src, dst):
        k = len(self.staged)
        cp = pltpu.make_async_copy(src, self.stages[k], self.local_sems.at[2 * k])
        cp.start()
        self.staged.append((cp, dst))

    def flush(self):
        for k in range(len(self.locals), len(self.staged)):
            cp, dst = self.staged[k]
            cp.wait()
            out = pltpu.make_async_copy(self.stages[k], dst, self.local_sems.at[2 * k + 1])
            out.start()
            self.locals.append(out)

    def drain(self):
        self.flush()
        for cp in self.sent:
            cp.wait_send()
        for cp in self.locals:
            cp.wait()


def _comm_call(body, name, ins, out_shapes, n_remote, stages):
    n_in, n_out = len(ins), len(out_shapes)

    def kern(*refs):
        in_refs, out_refs = refs[:n_in], refs[n_in:n_in + n_out]
        send_sems, recv_sems, local_sems = refs[n_in + n_out:n_in + n_out + 3]
        body(_Copies(send_sems, recv_sems, local_sems, refs[n_in + n_out + 3:]), in_refs, out_refs)

    hbm = pl.BlockSpec(memory_space=pl.ANY)
    return pl.pallas_call(
        kern, name=name, out_shape=out_shapes, in_specs=[hbm] * n_in, out_specs=[hbm] * n_out,
        scratch_shapes=[pltpu.SemaphoreType.DMA((n_remote,)), pltpu.SemaphoreType.DMA((n_remote,)),
                        pltpu.SemaphoreType.DMA((2 * len(stages),))] + [pltpu.VMEM(s, d) for s, d in stages],
        compiler_params=pltpu.CompilerParams(vmem_limit_bytes=VMEM_LIMIT),
    )(*ins)


def _gather_to_all(cps, pairs, me, sibling, other_chips, c, base):
    idx = lambda p: 4 * p[0] + 2 * p[1] + p[2]
    for i, (src, dst) in enumerate(pairs):
        cps.local(src, dst.at[idx(me)])
        cps.send(base + 7 * i, src, dst.at[idx(me)], sibling)
        for j, chip in enumerate(other_chips):
            cps.send(base + 7 * i + 1 + j, src, dst.at[idx(me)], (*chip, c))
    cps.flush()
    for j, chip in enumerate(other_chips):
        for i, (_, dst) in enumerate(pairs):
            got = dst.at[idx((*chip, c))]
            cps.arrived(base + 7 * i + 1 + j, got, (*chip, c))
            cps.send(base + 7 * i + 4 + j, got, got, sibling)
    for i, (_, dst) in enumerate(pairs):
        cps.arrived(base + 7 * i, dst.at[idx(sibling)], sibling)
        for j, chip in enumerate(other_chips):
            cps.arrived(base + 7 * i + 4 + j, dst.at[idx((*chip, 1 - c))], sibling)


def _gather_weights(shards, small):
    n = len(shards)

    def body(cps, ins, outs):
        x, y, c = _place()
        me, sibling, mine = (x, y, c), (x, y, 1 - c), 2 * x + y
        chips = [(1 - x, y), (x, 1 - y), (1 - x, 1 - y)]
        for i in range(n):
            cps.local(ins[i], outs[i].at[mine])
        for j, (px, py) in enumerate(chips):
            for i in range(n):
                hr = shards[i].shape[0] // 2
                rows = pl.ds(c * hr, hr)
                cps.send(6 * i + j, ins[i].at[rows], outs[i].at[mine, rows], (px, py, c))
        _gather_to_all(cps, [(ins[n], outs[n])], me, sibling, chips, c, 6 * n)
        for j, (px, py) in enumerate(chips):
            for i in range(n):
                hr = shards[i].shape[0] // 2
                got = outs[i].at[2 * px + py, pl.ds(c * hr, hr)]
                cps.arrived(6 * i + j, got, (px, py, c))
                cps.send(6 * i + 3 + j, got, got, sibling)
        for j, (px, py) in enumerate(chips):
            for i in range(n):
                hr = shards[i].shape[0] // 2
                cps.arrived(6 * i + 3 + j, outs[i].at[2 * px + py, pl.ds((1 - c) * hr, hr)], sibling)
        cps.drain()

    return _comm_call(body, "gather_weights", list(shards) + [small],
                      [_sds((4,) + s.shape, s.dtype) for s in shards] + [_sds((8,) + small.shape, small.dtype)], 6 * n + 7,
                      [(a.shape, a.dtype) for a in list(shards) + [small]])


def _reduce_sibling(parts, dmod_rows):
    n = len(parts)

    def body(cps, ins, outs):
        x, y, c = _place()
        me, sibling = (x, y, c), (x, y, 1 - c)
        chips = [(1 - x, y), (x, 1 - y), (1 - x, 1 - y)]
        for i in range(n):
            cps.send(i, ins[i].at[:, 1 - c], outs[i], sibling)
        _gather_to_all(cps, [(ins[n], outs[n])], me, sibling, chips, c, n)
        for i in range(n):
            cps.arrived(i, outs[i], sibling)
        cps.drain()

    return _comm_call(body, "reduce_sibling", list(parts) + [dmod_rows],
                      [_sds((4,) + p.shape[2:], p.dtype) for p in parts] + [_sds((8,) + dmod_rows.shape, dmod_rows.dtype)], n + 7,
                      [(dmod_rows.shape, dmod_rows.dtype)])


def _reduce_chips(parts):
    n = len(parts)

    def body(cps, ins, outs):
        x, y, c = _place()
        mine = 2 * x + y
        chips = [(1 - x, y), (x, 1 - y), (1 - x, 1 - y)]
        for i in range(n):
            cps.local(ins[i].at[mine], outs[i].at[mine])
        for j, (px, py) in enumerate(chips):
            for i in range(n):
                cps.send(3 * i + j, ins[i].at[2 * px + py], outs[i].at[mine], (px, py, c))
        cps.flush()
        for j, (px, py) in enumerate(chips):
            for i in range(n):
                cps.arrived(3 * i + j, outs[i].at[2 * px + py], (px, py, c))
        cps.drain()

    return _comm_call(body, "reduce_chips", list(parts), [_sds(p.shape, p.dtype) for p in parts], 3 * n,
                      [(p.shape[1:], p.dtype) for p in parts])


def _gather_reduced(shard_parts, repl_parts):
    ns, nr = len(shard_parts), len(repl_parts)

    def body(cps, ins, outs):
        x, y, c = _place()
        me, sibling = (x, y, c), (x, y, 1 - c)
        chips = [(1 - x, y), (x, 1 - y), (1 - x, 1 - y)]
        for i in range(ns):
            cps.local(ins[i], outs[i].at[c])
            cps.send(i, ins[i], outs[i].at[c], sibling)
        _gather_to_all(cps, [(ins[ns + i], outs[ns + i]) for i in range(nr)], me, sibling, chips, c, ns)
        for i in range(ns):
            cps.arrived(i, outs[i].at[1 - c], sibling)
        cps.drain()

    return _comm_call(body, "gather_reduced", list(shard_parts) + list(repl_parts),
                      [_sds((2,) + p.shape, p.dtype) for p in shard_parts] + [_sds((8,) + p.shape, p.dtype) for p in repl_parts],
                      ns + 7 * nr, [(p.shape, p.dtype) for p in list(shard_parts) + list(repl_parts)])


def _sum_sibling(core, parts, got, wire, name):
    n = len(parts)

    def body(core_ref, *refs):
        for i in range(n):
            refs[2 * n + i][0] = (refs[i][0] + refs[n + i][0]).astype(wire[i])

    keep_spec = lambda p: pl.BlockSpec((1, None) + p.shape[2:], lambda s, core_ref: (s, core_ref[0], 0, 0))
    slot_spec = lambda p: pl.BlockSpec((1,) + p.shape[2:], lambda s, core_ref: (s, 0, 0))
    return pl.pallas_call(
        body, name=name,
        grid_spec=pltpu.PrefetchScalarGridSpec(
            num_scalar_prefetch=1, grid=(4,), in_specs=[keep_spec(p) for p in parts] + [slot_spec(p) for p in parts],
            out_specs=[slot_spec(p) for p in parts]),
        out_shape=[_sds((4,) + p.shape[2:], wire[i]) for i, p in enumerate(parts)],
        compiler_params=_params("parallel"),
    )(core, *parts, *got)


def _sum_slots(slots, name):
    n = len(slots)

    def spec_pair(p):
        k, rows, cols = p.shape
        sub = 16 if p.dtype == BF16 else 8
        if (rows // 2) % sub == 0:
            return pl.BlockSpec((k, rows // 2, cols), lambda i: (0, i, 0)), pl.BlockSpec((rows // 2, cols), lambda i: (i, 0))
        return pl.BlockSpec((k, rows, cols), lambda i: (0, 0, 0)), pl.BlockSpec((rows, cols), lambda i: (0, 0))

    pairs = [spec_pair(p) for p in slots]

    def body(*refs):
        for i in range(n):
            acc = refs[i][0].astype(F32)
            for j in range(1, slots[i].shape[0]):
                acc = acc + refs[i][j].astype(F32)
            refs[n + i][...] = acc

    return pl.pallas_call(
        body, name=name, grid=(2,), in_specs=[a for a, _ in pairs], out_specs=[b for _, b in pairs],
        out_shape=[_sds(p.shape[1:]) for p in slots], compiler_params=_params("arbitrary"),
    )(*slots)


def _modulation(c_all, ada_w, ada_b):
    cols = ada_w.shape[2]

    def body(c_ref, w_ref, b_ref, o_ref):
        cc = c_ref[...]
        o_ref[0] = _mm(cc * _sigmoid(cc), w_ref[0]) + b_ref[0]

    return pl.pallas_call(
        body, name="modulation", grid=(2,),
        in_specs=[_full((8, D_MODEL)), pl.BlockSpec((1, D_MODEL, cols), lambda l: (l, 0, 0)), pl.BlockSpec((1, 1, cols), lambda l: (l, 0, 0))],
        out_specs=pl.BlockSpec((1, 8, cols), lambda l: (l, 0, 0)), out_shape=_sds((2, 8, cols)),
        compiler_params=_params("parallel"),
    )(c_all, ada_w, ada_b)


def _adamw_math(w, g, m, v):
    m = ADAM_B1 * m + (1.0 - ADAM_B1) * g
    v = ADAM_B2 * v + (1.0 - ADAM_B2) * (g * g)
    m_hat = m / (1.0 - ADAM_B1 ** ADAM_STEP)
    v_hat = v / (1.0 - ADAM_B2 ** ADAM_STEP)
    delta = -ADAM_LR * (m_hat / (jnp.sqrt(v_hat) + ADAM_EPS) + ADAM_WD * w)
    return delta, m, v


def _ada_update(c_all, dmod, w, m, v):
    cols = w.shape[2]
    tr = 256
    spec3 = pl.BlockSpec((1, tr, cols), lambda l, i: (l, i, 0))

    def body(c_ref, d_ref, w_ref, m_ref, v_ref, g_ref, dl_ref, nm_ref, nv_ref):
        cc = c_ref[...]
        g = _mm_tn(cc * _sigmoid(cc), d_ref[0])
        g_ref[0] = g
        dl_ref[0], nm_ref[0], nv_ref[0] = _adamw_math(w_ref[0], g, m_ref[0], v_ref[0])

    return pl.pallas_call(
        body, name="ada_update", grid=(2, D_MODEL // tr),
        in_specs=[pl.BlockSpec((8, tr), lambda l, i: (0, i)), pl.BlockSpec((1, 8, cols), lambda l, i: (l, 0, 0)), spec3, spec3, spec3],
        out_specs=[spec3] * 4, out_shape=[_sds(w.shape)] * 4, compiler_params=_params("parallel", "parallel"),
    )(c_all, dmod, w, m, v)


def _adamw(w, g, m, v, name):
    rows, n = w.shape
    tr = next(t for t in (256, 128, 64, 32, 16, 8, rows) if rows % t == 0)

    def body(w_ref, g_ref, m_ref, v_ref, dl_ref, nm_ref, nv_ref):
        dl_ref[...], nm_ref[...], nv_ref[...] = _adamw_math(w_ref[...], g_ref[...], m_ref[...], v_ref[...])

    return pl.pallas_call(body, name=name, grid=(rows // tr,), in_specs=[_rows(tr, n)] * 4, out_specs=[_rows(tr, n)] * 3,
                          out_shape=[_sds((rows, n))] * 3, compiler_params=_params("parallel"))(w, g, m, v)


def _adamw_small(params):
    n = len(params)

    def body(*refs):
        ins, outs = refs[:4 * n], refs[4 * n:]
        for j in range(n):
            w_ref, g_ref, m_ref, v_ref = ins[4 * j:4 * j + 4]
            outs[3 * j][...], outs[3 * j + 1][...], outs[3 * j + 2][...] = _adamw_math(w_ref[...], g_ref[...], m_ref[...], v_ref[...])

    flat = [a for p in params for a in p]
    res = pl.pallas_call(body, name="adamw_small", out_shape=[_sds(p[0].shape) for p in params for _ in range(3)])(*flat)
    return [tuple(res[3 * j:3 * j + 3]) for j in range(n)]


def _cols(a, start, size):
    return lax.dynamic_slice_in_dim(a, start, size, axis=a.ndim - 1)


def kernel(x, c, positions, ada_w, ada_b, ln_g, ln_b, ev_w_in, ev_w_out, ev_sink, ev_sg_ln_g, ev_sg_ln_b, ev_sg_w, ev_sg_b, od_w_in, od_conv_w, od_conv_b, od_w_a, od_b_a, od_w_x, od_b_x, od_lam, od_w_out, loss_target, m_ada_w, m_ada_b, m_ln_g, m_ln_b, m_ev_w_in, m_ev_w_out, m_ev_sink, m_ev_sg_ln_g, m_ev_sg_ln_b, m_ev_sg_w, m_ev_sg_b, m_od_w_in, m_od_conv_w, m_od_conv_b, m_od_w_a, m_od_b_a, m_od_w_x, m_od_b_x, m_od_lam, m_od_w_out, v_ada_w, v_ada_b, v_ln_g, v_ln_b, v_ev_w_in, v_ev_w_out, v_ev_sink, v_ev_sg_ln_g, v_ev_sg_ln_b, v_ev_sg_w, v_ev_sg_b, v_od_w_in, v_od_conv_w, v_od_conv_b, v_od_w_a, v_od_b_a, v_od_w_x, v_od_b_x, v_od_lam, v_od_w_out):
    seq = x.shape[1]
    px, py, pc = _place()
    chip = 2 * px + py
    dev = 2 * chip + pc

    small = jnp.concatenate([od_conv_w[0].reshape(-1), od_conv_b[0], od_b_a[0].reshape(-1), jnp.zeros((256,), F32),
                             od_b_x[0].reshape(-1), od_lam[0].reshape(-1)]).reshape(3, D_MODEL)
    blk = jnp.concatenate([c, small, jnp.zeros((4, D_MODEL), F32)], axis=0)
    tr = lambda a: jnp.swapaxes(a, -1, -2)
    wire_w = lambda a: a.astype(MXU_DTYPE)
    ev_w_in4, g_small = _gather_weights([wire_w(tr(ev_w_in[0]))], blk)
    core = pc.astype(jnp.int32).reshape(1)
    ride = {"ev_w_out": wire_w(ev_w_out[0]), "od_w_in": wire_w(od_w_in[0]), "od_w_out": wire_w(od_w_out[0]),
            "core": core, "wire": MXU_DTYPE}
    c_all = g_small[:, 0, :]
    per_chip = g_small[0::2]
    conv_w = per_chip[:, 1].reshape(4, 4, 256).transpose(1, 0, 2).reshape(4, D_MODEL)
    conv_b = per_chip[:, 2, 0:256].reshape(D_MODEL)
    b_a = per_chip[:, 2, 256:768].reshape(4, 2, 256).transpose(1, 0, 2).reshape(2, D_MODEL)
    b_x = per_chip[:, 3, 0:512].reshape(4, 2, 256).transpose(1, 0, 2).reshape(2, D_MODEL)
    lam = per_chip[:, 3, 512:1024].reshape(4, 2, 256).transpose(1, 0, 2).reshape(2, D_MODEL)

    w_full = {
        "ev_w_in_t": ev_w_in4.reshape(EVEN_IN, D_MODEL),
        "ev_sink": ev_sink[0], "ev_sg_ln_g": ev_sg_ln_g[0], "ev_sg_ln_b": ev_sg_ln_b[0], "ev_sg_w": ev_sg_w[0],
        "ev_sg_b": ev_sg_b[0], "od_conv_w": conv_w, "od_conv_b": conv_b, "od_w_a": od_w_a[0], "od_b_a": b_a,
        "od_w_x": od_w_x[0], "od_b_x": b_x, "od_lam": lam, "ln_g": ln_g, "ln_b": ln_b,
    }

    ada_cols = ada_w.shape[2]
    mod_sh = _modulation(c_all, ada_w, _cols(ada_b, chip * ada_cols, ada_cols).reshape(2, 1, ada_cols))
    mod_all = _allgather8(mod_sh.reshape(16, ada_cols), "gather_mod").reshape(4, 2, 2, 8, ada_cols)[:, 0]
    mod_mine = lax.dynamic_index_in_dim(mod_all, dev, axis=2, keepdims=False)
    mod = mod_mine.transpose(1, 0, 2).reshape(2, 3, D_MODEL)

    posf = positions.astype(F32).reshape(seq, 1)
    loss_local, grad_x, dmod, g = _local_step(x[0], posf, loss_target[0], mod, w_full, seq, ride)

    pad = lambda a, n: jnp.concatenate([a.reshape(-1), jnp.zeros((n - a.size,), F32)])
    rows_small = jnp.concatenate([
        dmod.reshape(6, D_MODEL), g["ln_g"][0:1], g["ln_b"][0:1], g["ln_g"][1:2], g["ln_b"][1:2],
        jnp.concatenate([g["ev_sg_ln_g"], g["ev_sg_ln_b"]]).reshape(1, D_MODEL), g["ev_sg_b"].reshape(1, D_MODEL),
        g["od_conv_w"], g["od_conv_b"].reshape(1, D_MODEL), g["od_b_a"], g["od_b_x"], g["od_lam"],
        pad(g["ev_sink"], D_MODEL).reshape(1, D_MODEL), pad(loss_local, D_MODEL).reshape(1, D_MODEL),
        jnp.zeros((39, D_MODEL), F32)], axis=0)
    parts = [g["ev_w_in_t"].reshape(4, 2, 352, D_MODEL), g["ev_w_out"].reshape(4, 2, 128, D_MODEL),
             g["ev_sg_w"].reshape(4, 2, BLK, BLK), rows_small.reshape(4, 2, 8, D_MODEL)]
    wire = [MXU_DTYPE] * 3 + [F32]
    dmod_blk = jnp.concatenate([dmod.reshape(6, D_MODEL), jnp.zeros((2, D_MODEL), F32)], axis=0)
    *got, dmod_gathered = _reduce_sibling(parts, dmod_blk)
    ev_slots = list(_reduce_chips(_sum_sibling(core, parts, got, wire, "sum_sibling")))
    od_slots = list(g["od_slots"])
    mine = _sum_slots(ev_slots[0:2] + od_slots[0:2] + ev_slots[2:3] + od_slots[2:4] + ev_slots[3:4], "sum_chips")
    reduced = _gather_reduced(mine[:4], mine[4:])
    g_ev_w_in_t = reduced[0].reshape(704, D_MODEL)
    g_ev_w_out = reduced[1].reshape(256, D_MODEL)
    g_od_w_in = reduced[2].reshape(D_MODEL, 512)
    g_od_w_out = reduced[3].reshape(256, D_MODEL)
    g_sg_w = reduced[4].reshape(8 * BLK, BLK)
    g_w_a = reduced[5].reshape(16 * BLK, BLK)
    g_w_x = reduced[6].reshape(16 * BLK, BLK)
    gs = reduced[7].reshape(64, D_MODEL)
    loss = gs[24, 0]
    dmod_all = dmod_gathered[:, 0:6].reshape(8, 2, 3 * D_MODEL)
    dmod_sh = _cols(dmod_all, chip * ada_cols, ada_cols).transpose(1, 0, 2)
    g_ada_w, d_ada_w, nm_ada_w, nv_ada_w = _ada_update(c_all, dmod_sh, ada_w, m_ada_w, v_ada_w)

    big = {}
    d_, nm_, nv_ = _adamw(tr(ev_w_in[0]), g_ev_w_in_t, tr(m_ev_w_in[0]), tr(v_ev_w_in[0]), "adamw_ev_w_in")
    big["ev_w_in"] = tuple(tr(a).reshape(ev_w_in.shape) for a in (g_ev_w_in_t, d_, nm_, nv_))
    for name, w_, g_, m_, v_ in (
            ("ev_w_out", ev_w_out, g_ev_w_out, m_ev_w_out, v_ev_w_out),
            ("od_w_in", od_w_in, g_od_w_in, m_od_w_in, v_od_w_in), ("od_w_out", od_w_out, g_od_w_out, m_od_w_out, v_od_w_out),
            ("ev_sg_w", ev_sg_w, g_sg_w, m_ev_sg_w, v_ev_sg_w), ("od_w_a", od_w_a, g_w_a, m_od_w_a, v_od_w_a),
            ("od_w_x", od_w_x, g_w_x, m_od_w_x, v_od_w_x)):
        two_d = lambda a: a.reshape(g_.shape)
        d_, nm_, nv_ = _adamw(two_d(w_), g_, two_d(m_), two_d(v_), "adamw_" + name)
        big[name] = tuple(a.reshape(w_.shape) for a in (g_, d_, nm_, nv_))
    big["ada_w"] = (g_ada_w, d_ada_w, nm_ada_w, nv_ada_w)

    sh = lambda a: _cols(a, chip * 256, 256)
    small_g = {
        "ada_b": gs[0:6].reshape(2, 3 * D_MODEL), "ln_g": jnp.stack([gs[6], gs[8]]), "ln_b": jnp.stack([gs[7], gs[9]]),
        "ev_sink": gs[23:24, 0:8], "ev_sg_ln_g": gs[10:11, 0:512], "ev_sg_ln_b": gs[10:11, 512:1024],
        "ev_sg_b": gs[11].reshape(8, BLK), "od_conv_w": sh(gs[12:16]), "od_conv_b": sh(gs[16:17]), "od_b_a": sh(gs[17:19]),
        "od_b_x": sh(gs[19:21]), "od_lam": sh(gs[21:23]),
    }
    small_in = {"ada_b": (ada_b, m_ada_b, v_ada_b), "ln_g": (ln_g, m_ln_g, v_ln_g), "ln_b": (ln_b, m_ln_b, v_ln_b),
                "ev_sink": (ev_sink, m_ev_sink, v_ev_sink), "ev_sg_ln_g": (ev_sg_ln_g, m_ev_sg_ln_g, v_ev_sg_ln_g),
                "ev_sg_ln_b": (ev_sg_ln_b, m_ev_sg_ln_b, v_ev_sg_ln_b), "ev_sg_b": (ev_sg_b, m_ev_sg_b, v_ev_sg_b),
                "od_conv_w": (od_conv_w, m_od_conv_w, v_od_conv_w), "od_conv_b": (od_conv_b, m_od_conv_b, v_od_conv_b),
                "od_b_a": (od_b_a, m_od_b_a, v_od_b_a), "od_b_x": (od_b_x, m_od_b_x, v_od_b_x),
                "od_lam": (od_lam, m_od_lam, v_od_lam)}
    names_small = list(small_g)
    upd = _adamw_small([(small_in[n][0].reshape(small_g[n].shape), small_g[n], small_in[n][1].reshape(small_g[n].shape),
                         small_in[n][2].reshape(small_g[n].shape)) for n in names_small])
    res = dict(big)
    for n, (d_, nm_, nv_) in zip(names_small, upd):
        shape = small_in[n][0].shape
        res[n] = tuple(a.reshape(shape) for a in (small_g[n], d_, nm_, nv_))

    order = ["ada_w", "ada_b", "ln_g", "ln_b", "ev_w_in", "ev_w_out", "ev_sink", "ev_sg_ln_g", "ev_sg_ln_b", "ev_sg_w", "ev_sg_b",
             "od_w_in", "od_conv_w", "od_conv_b", "od_w_a", "od_b_a", "od_w_x", "od_b_x", "od_lam", "od_w_out"]
    return (loss, grad_x.reshape(x.shape), *[res[n][0] for n in order], *[res[n][1] for n in order],
            *[res[n][2] for n in order], *[res[n][3] for n in order])
```

```python
import functools

import jax
import jax.numpy as jnp
import numpy as np
from jax import lax
from jax.experimental import pallas as pl
from jax.experimental.pallas import tpu as pltpu

F32 = jnp.float32
BF16 = jnp.bfloat16
MXU_DTYPE = BF16
ACT_DTYPE = MXU_DTYPE

D_MODEL = 1024
HEAD_DIM = 64
N_Q_HEADS = 8
Q_PER_KV = 4
ATTN_WIDTH = 512
KV_WIDTH = 128
BLK = 128
ROPE_DIM = 16
ROPE_THETA = 500000.0
N_SG_GROUPS = 8
SG_WIDTH = 512
EVEN_IN = 2816
ODD_IN = 2048
RNN_HEADS = 8
RG_LRU_C = 8.0
ALPHA = (2 * 2) ** 0.25
LN_EPS = 1e-5
NEG_INF = -1e30
ADAM_LR, ADAM_B1, ADAM_B2, ADAM_EPS, ADAM_WD, ADAM_STEP = 0.001, 0.9, 0.999, 1e-08, 0.01, 10

LANES = 128
VMEM_LIMIT = 56 * 1024 * 1024
MESH = pl.DeviceIdType.MESH


def _mm(a, b):
    return jnp.dot(a.astype(MXU_DTYPE), b.astype(MXU_DTYPE), preferred_element_type=F32)


def _mm_nt(a, b):
    return lax.dot_general(a.astype(MXU_DTYPE), b.astype(MXU_DTYPE), (((1,), (1,)), ((), ())), preferred_element_type=F32)


def _mm_tn(a, b):
    return lax.dot_general(a.astype(MXU_DTYPE), b.astype(MXU_DTYPE), (((0,), (0,)), ((), ())), preferred_element_type=F32)


def _sigmoid(x):
    return 1.0 / (1.0 + jnp.exp(-x))


def _ln_stats(z):
    mu = jnp.mean(z, axis=-1, keepdims=True)
    d = z - mu
    var = jnp.mean(d * d, axis=-1, keepdims=True)
    rstd = lax.rsqrt(var + LN_EPS)
    return d * rstd, rstd


def _ln_bwd(dout, zhat, rstd, g):
    dzh = dout * g
    m1 = jnp.mean(dzh, axis=-1, keepdims=True)
    m2 = jnp.mean(dzh * zhat, axis=-1, keepdims=True)
    return rstd * (dzh - m1 - zhat * m2)


def _group_sum(x, e2):
    hi = x.astype(MXU_DTYPE)
    lo = (x - hi.astype(F32)).astype(MXU_DTYPE)
    return jnp.dot(hi, e2, preferred_element_type=F32) + jnp.dot(lo, e2, preferred_element_type=F32)


def _lane_iota(shape):
    return lax.broadcasted_iota(jnp.int32, shape, 1)


def _to_kv_lanes(t, h):
    src_lo = (h % 2 == 0)
    dst_lo = (h // Q_PER_KV == 0)
    if src_lo != dst_lo:
        t = pltpu.roll(t, HEAD_DIM, 1)
    lane = _lane_iota(t.shape)
    keep = (lane < HEAD_DIM) if dst_lo else (lane >= HEAD_DIM)
    return jnp.where(keep, t, 0.0)


def _from_kv_lanes(t, h):
    src_lo = (h // Q_PER_KV == 0)
    dst_lo = (h % 2 == 0)
    lane = _lane_iota(t.shape)
    keep = (lane < HEAD_DIM) if src_lo else (lane >= HEAD_DIM)
    t = jnp.where(keep, t, 0.0)
    if src_lo != dst_lo:
        t = pltpu.roll(t, HEAD_DIM, 1)
    return t


def _rope(t, cos_t, sin_p, sin_m):
    half = ROPE_DIM // 2
    return t * cos_t + pltpu.roll(t, half, 1) * sin_p + pltpu.roll(t, LANES - half, 1) * sin_m


def _rope_t(d, cos_t, sin_p, sin_m):
    half = ROPE_DIM // 2
    return d * cos_t + pltpu.roll(d * sin_p, LANES - half, 1) + pltpu.roll(d * sin_m, half, 1)


def _band(ref, n, nb):
    prev = jnp.maximum(n - 1, 0)
    nxt = jnp.minimum(n + 1, nb - 1)
    rows = [ref[pl.ds(pl.multiple_of(j * BLK, BLK), BLK), :] for j in (prev, n, nxt)]
    return jnp.concatenate(rows, axis=0)


def _band_bias(n, seq):
    qi = lax.broadcasted_iota(jnp.int32, (BLK, 3 * BLK), 0)
    kj = lax.broadcasted_iota(jnp.int32, (BLK, 3 * BLK), 1)
    k_abs = n * BLK - BLK + kj
    valid = (jnp.abs(kj - BLK - qi) <= BLK) & (k_abs >= 0) & (k_abs < seq)
    bias = jnp.where(valid, 0.0, NEG_INF)
    return jnp.concatenate([bias] * Q_PER_KV, axis=0)


def _stack_heads(tile_of, kv):
    return jnp.concatenate([_to_kv_lanes(tile_of(h // 2), h) for h in range(Q_PER_KV * kv, Q_PER_KV * (kv + 1))], axis=0)


def _per_head_column(vals):
    row = lax.broadcasted_iota(jnp.int32, (Q_PER_KV * BLK, 1), 0)
    return jnp.where(row < BLK, vals[0], jnp.where(row < 2 * BLK, vals[1], jnp.where(row < 3 * BLK, vals[2], vals[3])))


def _softplus_neg(lam):
    e = jnp.exp(-jnp.abs(lam))
    u = 1.0 + e
    log1p_e = jnp.where(u == 1.0, e, jnp.log(u) * (e / (u - 1.0)))
    sp = jnp.maximum(-lam, 0.0) + log1p_e
    dsp = -1.0 / (1.0 + jnp.exp(lam))
    return sp, dsp


def _full(shape):
    return pl.BlockSpec(shape, lambda *_: (0,) * len(shape))


def _const(shape):
    return pl.BlockSpec(shape, lambda *_: (0,) * len(shape), pipeline_mode=pl.Buffered(1))


def _rows(tm, n):
    return pl.BlockSpec((tm, n), lambda i: (i, 0))


def _params(*sem):
    return pltpu.CompilerParams(dimension_semantics=sem, vmem_limit_bytes=VMEM_LIMIT)


def _sds(shape, dtype=F32):
    return jax.ShapeDtypeStruct(shape, dtype)


def _place():
    return lax.axis_index("x"), lax.axis_index("y"), lax.axis_index("c")


class _Rider:
    def __init__(self, ins, out_shapes, n_remote, n_local, plan):
        self.ins, self.out_shapes, self.n_remote, self.n_local, self.plan = list(ins), list(out_shapes), n_remote, n_local, plan

    def scratch(self):
        return [pltpu.SemaphoreType.DMA((self.n_remote,)), pltpu.SemaphoreType.DMA((self.n_remote,)),
                pltpu.SemaphoreType.DMA((max(self.n_local, 1),))]

    def run(self, first, in_refs, out_refs, sems):
        send_sems, recv_sems, local_sems = sems
        sends, recvs, locals_ = self.plan(in_refs, out_refs)
        remote = lambda k, src, dst, to: pltpu.make_async_remote_copy(
            src_ref=src, dst_ref=dst, send_sem=send_sems.at[k], recv_sem=recv_sems.at[k], device_id=to, device_id_type=MESH)
        if first:
            for k, src, dst, to in sends:
                remote(k, src, dst, to).start()
            for j, (src, dst) in enumerate(locals_):
                pltpu.make_async_copy(src, dst, local_sems.at[j]).start()
        else:
            for k, dst, frm in recvs:
                remote(k, dst, dst, frm).wait_recv()
            for k, src, dst, to in sends:
                remote(k, src, dst, to).wait_send()
            for j, (src, dst) in enumerate(locals_):
                pltpu.make_async_copy(src, dst, local_sems.at[j]).wait()


def _other_chips(x, y):
    return [(1 - x, y), (x, 1 - y), (1 - x, 1 - y)]


def _gather_rider(shard):
    hr = shard.shape[0] // 2

    def plan(ins, outs):
        x, y, c = _place()
        mine, src, dst = 2 * x + y, ins[0], outs[0]
        sends, recvs = [], []
        for j, (px, py) in enumerate(_other_chips(x, y)):
            for flip in range(2):
                tc = c if flip == 0 else 1 - c
                sends.append((2 * j + flip, src.at[pl.ds(c * hr, hr)], dst.at[mine, pl.ds(c * hr, hr)], (px, py, tc)))
                recvs.append((2 * j + flip, dst.at[2 * px + py, pl.ds(tc * hr, hr)], (px, py, tc)))
        return sends, recvs, [(src, dst.at[mine])]

    return _Rider([shard], [_sds((4,) + shard.shape, shard.dtype)], 6, 1, plan)


def _sibling_swap_rider(parts):
    n = len(parts)

    def plan(ins, outs):
        x, y, c = _place()
        sibling = (x, y, 1 - c)
        return ([(i, ins[i].at[:, 1 - c], outs[i], sibling) for i in range(n)], [(i, outs[i], sibling) for i in range(n)], [])

    return _Rider(parts, [_sds((4,) + p.shape[2:], p.dtype) for p in parts], n, 0, plan)


def _chip_exchange_rider(parts):
    n = len(parts)

    def plan(ins, outs):
        x, y, c = _place()
        mine = 2 * x + y
        sends, recvs = [], []
        for i in range(n):
            for j, (px, py) in enumerate(_other_chips(x, y)):
                sends.append((3 * i + j, ins[i].at[2 * px + py], outs[i].at[mine], (px, py, c)))
                recvs.append((3 * i + j, outs[i].at[2 * px + py], (px, py, c)))
        return sends, recvs, [(ins[i].at[mine], outs[i].at[mine]) for i in range(n)]

    return _Rider(parts, [_sds(p.shape, p.dtype) for p in parts], 3 * n, n, plan)


def _call(body, name, grid, in_specs, out_specs, out_shape, args, sem, scratch=(), rider=None):
    if rider is None:
        return list(pl.pallas_call(body, name=name, grid=grid, in_specs=in_specs, out_specs=out_specs, out_shape=out_shape,
                                   scratch_shapes=list(scratch), compiler_params=_params(sem))(*args)), []
    n_in, n_out, n_scr = len(in_specs), len(out_specs), len(scratch)
    r_in, r_out = len(rider.ins), len(rider.out_shapes)
    steps = grid[0]

    def riding(*refs):
        ins, r_ins = refs[:n_in], refs[n_in:n_in + r_in]
        outs = refs[n_in + r_in:n_in + r_in + n_out]
        r_outs = refs[n_in + r_in + n_out:n_in + r_in + n_out + r_out]
        scr = refs[n_in + r_in + n_out + r_out:n_in + r_in + n_out + r_out + n_scr]
        sems = refs[n_in + r_in + n_out + r_out + n_scr:]

        @pl.when(pl.program_id(0) == 0)
        def _():
            rider.run(True, r_ins, r_outs, sems)

        body(*ins, *outs, *scr)

        @pl.when(pl.program_id(0) == steps - 1)
        def _():
            rider.run(False, r_ins, r_outs, sems)

    hbm = pl.BlockSpec(memory_space=pl.ANY)
    res = pl.pallas_call(
        riding, name=name, grid=grid, in_specs=list(in_specs) + [hbm] * r_in, out_specs=list(out_specs) + [hbm] * r_out,
        out_shape=list(out_shape) + rider.out_shapes, scratch_shapes=list(scratch) + rider.scratch(),
        compiler_params=_params("arbitrary"),
    )(*args, *rider.ins)
    return list(res[:n_out]), list(res[n_out:])


def _row_tile(seq, want):
    return want if seq % want == 0 else seq


def _rope_tables(posf, seq):
    half = ROPE_DIM // 2
    inv_freq = np.power(np.float32(ROPE_THETA), -np.arange(half, dtype=np.float32) / np.float32(half)).astype(np.float32)
    j = np.arange(LANES) % HEAD_DIM
    invf = jnp.asarray(np.where(j < ROPE_DIM, inv_freq[j % half], 0.0).astype(np.float32).reshape(1, LANES))
    m_p = jnp.asarray(((j >= half) & (j < ROPE_DIM)).astype(np.float32).reshape(1, LANES))
    m_m = jnp.asarray(-(j < half).astype(np.float32).reshape(1, LANES))
    tm = _row_tile(seq, 512)

    def body(pos_ref, invf_ref, mp_ref, mm_ref, cos_ref, sp_ref, sm_ref):
        ang = pos_ref[...] * invf_ref[...]
        s = jnp.sin(ang)
        cos_ref[...] = jnp.cos(ang)
        sp_ref[...] = s * mp_ref[...]
        sm_ref[...] = s * mm_ref[...]

    return pl.pallas_call(
        body, name="rope_tables", grid=(seq // tm,),
        in_specs=[_rows(tm, 1), _full((1, LANES)), _full((1, LANES)), _full((1, LANES))],
        out_specs=[_rows(tm, LANES)] * 3, out_shape=[_sds((seq, LANES))] * 3,
        compiler_params=_params("parallel"),
    )(posf, invf, m_p, m_m)


def _even_proj(x, mod, w_in_t, tabs, seq, rider=None):
    tm = _row_tile(seq, 512)

    def body(x_ref, mod_ref, w_ref, cos_ref, sp_ref, sm_ref, q_ref, k_ref, v_ref, su_ref, sv_ref, g_ref):
        h = x_ref[...] * (1.0 + mod_ref[1:2, :]) + mod_ref[0:1, :]
        p = _mm_nt(h, w_ref[...])
        cos_t, sin_p, sin_m = cos_ref[...], sp_ref[...], sm_ref[...]
        for j in range(ATTN_WIDTH // LANES):
            q_ref[:, j * LANES:(j + 1) * LANES] = _rope(p[:, j * LANES:(j + 1) * LANES], cos_t, sin_p, sin_m).astype(q_ref.dtype)
        k_ref[...] = _rope(p[:, 512:640], cos_t, sin_p, sin_m).astype(k_ref.dtype)
        v_ref[...] = p[:, 640:768].astype(v_ref.dtype)
        su_ref[...] = p[:, 768:1280].astype(su_ref.dtype)
        sv_ref[...] = p[:, 1280:1792].astype(sv_ref.dtype)
        g_ref[...] = p[:, 1792:2816].astype(g_ref.dtype)

    return _call(
        body, "even_proj", (seq // tm,),
        [_rows(tm, D_MODEL), _full((3, D_MODEL)), _const((EVEN_IN, D_MODEL))] + [_rows(tm, LANES)] * 3,
        [_rows(tm, 512), _rows(tm, LANES), _rows(tm, LANES), _rows(tm, 512), _rows(tm, 512), _rows(tm, D_MODEL)],
        [_sds((seq, 512), MXU_DTYPE), _sds((seq, LANES), MXU_DTYPE), _sds((seq, LANES), MXU_DTYPE), _sds((seq, 512), ACT_DTYPE),
         _sds((seq, 512), ACT_DTYPE), _sds((seq, D_MODEL), ACT_DTYPE)],
        (x, mod, w_in_t, *tabs), "parallel", rider=rider)


def _sg_forward(sv, lng, lnb, sgw_ref, sgb, e2):
    vn, vhat, rstd, svo = [], [], [], []
    for j in range(SG_WIDTH // LANES):
        t = sv[:, j * LANES:(j + 1) * LANES]
        mu = _group_sum(t, e2) * (1.0 / HEAD_DIM)
        d = t - mu
        var = _group_sum(d * d, e2) * (1.0 / HEAD_DIM)
        r = lax.rsqrt(var + LN_EPS)
        vh = d * r
        vhat.append(vh)
        rstd.append(r)
        vn.append(vh * lng[:, j * LANES:(j + 1) * LANES] + lnb[:, j * LANES:(j + 1) * LANES])
    lane = _lane_iota((BLK, LANES))
    for j in range(SG_WIDTH // LANES):
        lo = _mm(sgw_ref[2 * j], vn[j])
        hi = _mm(sgw_ref[2 * j + 1], vn[j])
        svo.append(jnp.where(lane < HEAD_DIM, lo, hi) + sgb[:, j * LANES:(j + 1) * LANES])
    return svo, vn, vhat, rstd


def _even_mix(q, k, v, su, sv, sink, sgln_g, sgln_b, sgw, sgb_full, e2, seq, rider=None):
    nb = seq // BLK

    def body(sink_ref, q_ref, k_ref, v_ref, su_ref, sv_ref, lng_ref, lnb_ref, sgw_ref, sgb_ref, e2_ref, ycat_ref, lse_ref):
        n = pl.program_id(0)
        kband = _band(k_ref, n, nb)
        vband = _band(v_ref, n, nb)
        bias = _band_bias(n, seq)
        lane = _lane_iota((BLK, LANES))
        lse = jnp.zeros((BLK, LANES), F32)
        q_tile = lambda j: q_ref[:, j * LANES:(j + 1) * LANES].astype(F32)
        acc = [jnp.zeros((BLK, LANES), F32) for _ in range(ATTN_WIDTH // LANES)]
        for kv in range(N_Q_HEADS // Q_PER_KV):
            heads = range(Q_PER_KV * kv, Q_PER_KV * (kv + 1))
            sink = _per_head_column([sink_ref[h] for h in heads])
            s = _mm_nt(_stack_heads(q_tile, kv), kband) * (HEAD_DIM ** -0.5) + bias
            m = jnp.maximum(jnp.max(s, axis=1, keepdims=True), sink)
            p = jnp.exp(s - m)
            denom = jnp.sum(p, axis=1, keepdims=True) + jnp.exp(sink - m)
            o4 = _mm(p / denom, vband)
            l4 = m + jnp.log(denom)
            for g, h in enumerate(heads):
                acc[h // 2] = acc[h // 2] + _from_kv_lanes(o4[g * BLK:(g + 1) * BLK], h)
                lse = jnp.where(lane == h, l4[g * BLK:(g + 1) * BLK], lse)
        for j in range(ATTN_WIDTH // LANES):
            ycat_ref[:, j * LANES:(j + 1) * LANES] = acc[j].astype(ycat_ref.dtype)
        lse_ref[...] = lse
        svo, _, _, _ = _sg_forward(sv_ref[...].astype(F32), lng_ref[...], lnb_ref[...], sgw_ref, sgb_ref[...], e2_ref[...])
        for j in range(SG_WIDTH // LANES):
            ysg = su_ref[:, j * LANES:(j + 1) * LANES].astype(F32) * svo[j]
            ycat_ref[:, ATTN_WIDTH + j * LANES:ATTN_WIDTH + (j + 1) * LANES] = ysg.astype(ycat_ref.dtype)

    blk = lambda w: pl.BlockSpec((BLK, w), lambda n: (n, 0))
    return _call(
        body, "even_mix", (nb,),
        [pl.BlockSpec(memory_space=pltpu.SMEM), blk(512), _full((seq, LANES)), _full((seq, LANES)), blk(512), blk(512),
         _full((1, 512)), _full((1, 512)), _full((8, BLK, BLK)), _full((BLK, 512)), _full((LANES, LANES))],
        [blk(D_MODEL), blk(LANES)], [_sds((seq, D_MODEL), ACT_DTYPE), _sds((seq, LANES))],
        (sink, q, k, v, su, sv, sgln_g, sgln_b, sgw, sgb_full, e2), "parallel", rider=rider)


def _even_out(ycat, g, x, mod, mod_next, w_out, w_in4_next, ln_g, ln_b, seq, rider=None):
    tm = _row_tile(seq, 512)
    cs = ODD_IN // 4

    def body(y_ref, g_ref, x_ref, mod_ref, modn_ref, wo_ref, wi_ref, g1_ref, b1_ref, z_ref, x1_ref, xr_ref, gn_ref):
        gg = g_ref[...].astype(F32)
        out = _mm(y_ref[...].astype(F32) * (gg * _sigmoid(gg)), wo_ref[...])
        z = ALPHA * x_ref[...] + mod_ref[2:3, :] * out
        z_ref[...] = z
        zhat, _ = _ln_stats(z)
        x1 = zhat * g1_ref[...] + b1_ref[...]
        x1_ref[...] = x1
        hb = (x1 * (1.0 + modn_ref[1:2, :]) + modn_ref[0:1, :]).astype(MXU_DTYPE)
        for s in range(2):
            xr_ref[:, s * cs:(s + 1) * cs] = jnp.dot(hb, wi_ref[s], preferred_element_type=F32)
            gn_ref[:, s * cs:(s + 1) * cs] = jnp.dot(hb, wi_ref[2 + s], preferred_element_type=F32).astype(gn_ref.dtype)

    return _call(
        body, "even_out", (seq // tm,),
        [_rows(tm, D_MODEL)] * 3 + [_full((3, D_MODEL)), _full((3, D_MODEL)), _const((D_MODEL, D_MODEL)), _const((4, D_MODEL, cs)),
                                    _full((1, D_MODEL)), _full((1, D_MODEL))],
        [_rows(tm, D_MODEL)] * 4, [_sds((seq, D_MODEL))] * 3 + [_sds((seq, D_MODEL), ACT_DTYPE)],
        (ycat, g, x, mod, mod_next, w_out, w_in4_next, ln_g, ln_b), "parallel", rider=rider)


def _halo_specs(tm, seq, width, order=lambda i: i):
    per = tm // 8
    last = seq // 8 - 1
    return [pl.BlockSpec((8, width), lambda i: (jnp.maximum(order(i) * per - 1, 0), 0)),
            pl.BlockSpec((tm, width), lambda i: (order(i), 0)),
            pl.BlockSpec((8, width), lambda i: (jnp.minimum((order(i) + 1) * per, last), 0))]


def _extended(prev_ref, main_ref, next_ref, i, n_steps):
    prev = jnp.where(i > 0, prev_ref[...], 0.0)
    nxt = jnp.where(i < n_steps - 1, next_ref[...], 0.0)
    return jnp.concatenate([prev, main_ref[...], nxt], axis=0)


def _shifted(ext, off, tm):
    if off == 0:
        return ext[8:8 + tm]
    return pltpu.roll(ext, (-off) % ext.shape[0], 0)[8:8 + tm]


SCAN_SUB = 8


def _lru_gate(xh, pre, bias, sp, hs, d):
    r = _sigmoid(pre[:, 0:LANES] + bias[2 * d:2 * d + 1, hs])
    ig = _sigmoid(pre[:, LANES:2 * LANES] + bias[2 * d + 1:2 * d + 2, hs])
    neg_log_a = RG_LRU_C * r * sp[d:d + 1, hs]
    a = jnp.exp(-neg_log_a)
    u = jnp.tanh(neg_log_a) * (a * a + 1.0)
    inv_s = lax.rsqrt(jnp.maximum(u, jnp.finfo(F32).tiny))
    return r, ig, a, u * inv_s, inv_s


def _conv_block(xp_ref, xm_ref, xn_ref, cw_ref, cb_ref, blk, steps, tm):
    ext = _extended(xp_ref, xm_ref, xn_ref, blk, steps)
    return cb_ref[...] + sum(cw_ref[kk:kk + 1, :] * _shifted(ext, kk - 2, tm) for kk in range(4))


def _scan_tiles(a_ref, b_ref, h_ref, hprev_ref, carry_h, carry_a, rows, descending, post):
    sub = SCAN_SUB
    tiles = rows // sub
    row = lax.broadcasted_iota(jnp.int32, (sub, D_MODEL), 0)

    def shift(v, d, fill):
        if descending:
            return jnp.where(row <= sub - 1 - d, pltpu.roll(v, sub - d, 0), fill)
        return jnp.where(row >= d, pltpu.roll(v, d, 0), fill)

    def last(v):
        return jnp.broadcast_to(v[0:1, :] if descending else v[sub - 1:sub, :], v.shape)

    def tile(j, c):
        ch, ca = c
        r0 = pl.multiple_of(((tiles - 1 - j) if descending else j) * sub, sub)
        at = a_ref[pl.ds(r0, sub), :]
        bt = b_ref[pl.ds(r0, sub), :]
        coef = shift(at, 1, ca) if post else at
        acc_a, acc_b = coef, bt
        for d in (1, 2, 4):
            acc_b = acc_b + acc_a * shift(acc_b, d, 0.0)
            acc_a = acc_a * shift(acc_a, d, 1.0)
        h = acc_b + acc_a * ch
        h_ref[pl.ds(r0, sub), :] = h
        if post:
            return last(h), last(at)
        hprev_ref[pl.ds(r0, sub), :] = shift(h, 1, ch)
        return last(h), ca

    ch, ca = lax.fori_loop(0, tiles, tile, (carry_h[...], carry_a[...]), unroll=4)
    carry_h[...] = ch
    carry_a[...] = ca


def _lru_fwd(xr, conv_w, conv_b, wcat, bias, lam, seq, d):
    tb = _row_tile(seq, 512)
    steps = seq // tb
    descending = d == 1
    order = (lambda i: steps - 1 - i) if descending else (lambda i: i)

    def body(xp_ref, xm_ref, xn_ref, cw_ref, cb_ref, w_ref, bias_ref, lam_ref, h_ref, hp_ref, a_scr, b_scr, carry_h, carry_a):
        i = pl.program_id(0)

        @pl.when(i == 0)
        def _():
            carry_h[...] = jnp.zeros_like(carry_h)
            carry_a[...] = jnp.zeros_like(carry_a)

        xc = _conv_block(xp_ref, xm_ref, xn_ref, cw_ref, cb_ref, order(i), steps, tb)
        sp, _ = _softplus_neg(lam_ref[...])
        bias = bias_ref[...]
        for h in range(RNN_HEADS):
            hs = slice(h * LANES, (h + 1) * LANES)
            xh = xc[:, hs]
            _, ig, a, s, _ = _lru_gate(xh, _mm(xh, w_ref[h, :, 2 * d * LANES:2 * (d + 1) * LANES]), bias, sp, hs, d)
            a_scr[:, hs] = a
            b_scr[:, hs] = s * ig * xh
        _scan_tiles(a_scr, b_scr, h_ref, hp_ref, carry_h, carry_a, tb, descending, post=False)

    out_spec = pl.BlockSpec((tb, D_MODEL), lambda i: (order(i), 0))
    return pl.pallas_call(
        body, name="lru_fwd_%d" % d, grid=(steps,),
        in_specs=_halo_specs(tb, seq, D_MODEL, order) + [_full((4, D_MODEL)), _full((1, D_MODEL)), _full((8, LANES, 512)),
                                                         _full((4, D_MODEL)), _full((2, D_MODEL))],
        out_specs=[out_spec] * 2, out_shape=[_sds((seq, D_MODEL))] * 2,
        scratch_shapes=[pltpu.VMEM((tb, D_MODEL), F32)] * 2 + [pltpu.VMEM((SCAN_SUB, D_MODEL), F32)] * 2,
        compiler_params=_params("arbitrary"),
    )(xr, xr, xr, conv_w, conv_b, wcat, bias, lam)


def _odd_out_and_loss(hf, hr, g, x1, tgt, mod, w_out, ln_g, ln_b, seq):
    tm = _row_tile(seq, 512)

    def body(hf_ref, hr_ref, g_ref, x_ref, t_ref, mod_ref, w_ref, lg_ref, lb_ref,
             dhs_ref, dg_ref, dres_ref, loss_ref, dw_ref, vec_ref):
        @pl.when(pl.program_id(0) == 0)
        def _():
            loss_ref[...] = jnp.zeros_like(loss_ref)
            dw_ref[...] = jnp.zeros_like(dw_ref)
            vec_ref[...] = jnp.zeros_like(vec_ref)

        gg = g_ref[...].astype(F32)
        sg = _sigmoid(gg)
        silu = gg * sg
        hsum = hf_ref[...] + hr_ref[...]
        y = hsum * silu
        out = _mm(y, w_ref[...])
        gate = mod_ref[2:3, :]
        z = ALPHA * x_ref[...] + gate * out
        zhat, rstd = _ln_stats(z)
        x2 = zhat * lg_ref[...] + lb_ref[...]
        err = x2 - t_ref[...]
        loss_ref[...] += 0.5 * jnp.sum(jnp.mean(err * err, axis=-1, keepdims=True))
        dx2 = err * (1.0 / D_MODEL)
        dz = _ln_bwd(dx2, zhat, rstd, lg_ref[...])
        vec_ref[0:1, :] += jnp.sum(dx2 * zhat, axis=0, keepdims=True)
        vec_ref[1:2, :] += jnp.sum(dx2, axis=0, keepdims=True)
        vec_ref[2:3, :] += jnp.sum(dz * out, axis=0, keepdims=True)
        dres_ref[...] = ALPHA * dz
        dout = gate * dz
        dw_ref[...] += _mm_tn(y, dout)
        dy = _mm_nt(dout, w_ref[...])
        dhs_ref[...] = dy * silu
        dg_ref[...] = (dy * hsum * (sg * (1.0 + gg * (1.0 - sg)))).astype(dg_ref.dtype)

    return pl.pallas_call(
        body, name="odd_out_loss", grid=(seq // tm,),
        in_specs=[_rows(tm, D_MODEL)] * 5 + [_full((3, D_MODEL)), _const((D_MODEL, D_MODEL)),
                                             _full((1, D_MODEL)), _full((1, D_MODEL))],
        out_specs=[_rows(tm, D_MODEL)] * 3 + [_full((8, LANES)), _full((D_MODEL, D_MODEL)), _full((8, D_MODEL))],
        out_shape=[_sds((seq, D_MODEL)), _sds((seq, D_MODEL), ACT_DTYPE), _sds((seq, D_MODEL)), _sds((8, LANES)),
                   _sds((D_MODEL, D_MODEL)), _sds((8, D_MODEL))],
        compiler_params=_params("arbitrary"),
    )(hf, hr, g, x1, tgt, mod, w_out, ln_g, ln_b)


def _lru_bwd(xr, dhs, hprev, conv_w, conv_b, wcat, bias, lam, seq, d):
    tb = _row_tile(seq, 512)
    steps = seq // tb
    descending = d == 0
    order = (lambda i: steps - 1 - i) if descending else (lambda i: i)
    cols = slice(2 * d * LANES, 2 * (d + 1) * LANES)

    def body(xp_ref, xm_ref, xn_ref, dhs_ref, hp_ref, cw_ref, cb_ref, w_ref, bias_ref, lam_ref, dxc_ref, dw_ref, vec_ref,
             xc_scr, a_scr, g_scr, r_scr, i_scr, s_scr, q_scr, carry_h, carry_a):
        i = pl.program_id(0)

        @pl.when(i == 0)
        def _():
            dw_ref[...] = jnp.zeros_like(dw_ref)
            vec_ref[...] = jnp.zeros_like(vec_ref)
            carry_h[...] = jnp.zeros_like(carry_h)
            carry_a[...] = jnp.zeros_like(carry_a)

        xc_scr[...] = _conv_block(xp_ref, xm_ref, xn_ref, cw_ref, cb_ref, order(i), steps, tb)
        sp, dsp = _softplus_neg(lam_ref[...])
        bias = bias_ref[...]
        for h in range(RNN_HEADS):
            hs = slice(h * LANES, (h + 1) * LANES)
            xh = xc_scr[:, hs]
            r_scr[:, hs], i_scr[:, hs], a_scr[:, hs], s_scr[:, hs], q_scr[:, hs] = _lru_gate(
                xh, _mm(xh, w_ref[h, :, cols]), bias, sp, hs, d)
        _scan_tiles(a_scr, dhs_ref, g_scr, None, carry_h, carry_a, tb, descending, post=True)
        for h in range(RNN_HEADS):
            hs = slice(h * LANES, (h + 1) * LANES)
            xh, r, ig, a, s = xc_scr[:, hs], r_scr[:, hs], i_scr[:, hs], a_scr[:, hs], s_scr[:, hs]
            db = g_scr[:, hs]
            da = db * hp_ref[:, hs]
            dlog_a = da * a - (db * ig * xh) * (a * a * q_scr[:, hs])
            dpr = dlog_a * (-RG_LRU_C) * sp[d:d + 1, hs] * r * (1.0 - r)
            dpi = db * s * xh * ig * (1.0 - ig)
            vec_ref[0:1, hs] += jnp.sum(dpr, axis=0, keepdims=True)
            vec_ref[1:2, hs] += jnp.sum(dpi, axis=0, keepdims=True)
            vec_ref[2:3, hs] += jnp.sum(dlog_a * r, axis=0, keepdims=True) * (-RG_LRU_C) * dsp[d:d + 1, hs]
            dcat = jnp.concatenate([dpr, dpi], axis=1)
            dw_ref[h] += _mm_tn(xh, dcat)
            dxc_ref[:, hs] = db * s * ig + _mm_nt(dcat, w_ref[h, :, cols])

    row_spec = pl.BlockSpec((tb, D_MODEL), lambda i: (order(i), 0))
    return pl.pallas_call(
        body, name="lru_bwd_%d" % d, grid=(steps,),
        in_specs=_halo_specs(tb, seq, D_MODEL, order) + [row_spec, row_spec, _full((4, D_MODEL)), _full((1, D_MODEL)),
                                                         _full((8, LANES, 512)), _full((4, D_MODEL)), _full((2, D_MODEL))],
        out_specs=[row_spec, _full((8, LANES, 2 * LANES)), _full((8, D_MODEL))],
        out_shape=[_sds((seq, D_MODEL)), _sds((8, LANES, 2 * LANES)), _sds((8, D_MODEL))],
        scratch_shapes=[pltpu.VMEM((tb, D_MODEL), F32)] * 7 + [pltpu.VMEM((SCAN_SUB, D_MODEL), F32)] * 2,
        compiler_params=_params("arbitrary"),
    )(xr, xr, xr, dhs, hprev, conv_w, conv_b, wcat, bias, lam)


def _odd_proj_bwd(dxc_f, dxc_r, xr, dg, x1, dres, mod, conv_w, w_in4, seq):
    tm = _row_tile(seq, 512)
    steps = seq // tm

    def body(fp_ref, fm_ref, fn_ref, rp_ref, rm_ref, rn_ref, xp_ref, xm_ref, xn_ref, dg_ref, x_ref, dres_ref, mod_ref, cw_ref,
             w_ref, dx_ref, dw_ref, vec_ref, dpb_ref):
        i = pl.program_id(0)

        @pl.when(i == 0)
        def _():
            vec_ref[...] = jnp.zeros_like(vec_ref)
            dw_ref[...] = jnp.zeros_like(dw_ref)

        dext = _extended(fp_ref, fm_ref, fn_ref, i, steps) + _extended(rp_ref, rm_ref, rn_ref, i, steps)
        xext = _extended(xp_ref, xm_ref, xn_ref, i, steps)
        dxc_m = fm_ref[...] + rm_ref[...]
        dxr = sum(cw_ref[kk:kk + 1, :] * _shifted(dext, 2 - kk, tm) for kk in range(4))
        for kk in range(4):
            vec_ref[kk:kk + 1, :] += jnp.sum(dxc_m * _shifted(xext, kk - 2, tm), axis=0, keepdims=True)
        vec_ref[4:5, :] += jnp.sum(dxc_m, axis=0, keepdims=True)
        dpb_ref[:, :D_MODEL] = dxr.astype(dpb_ref.dtype)
        dpb_ref[:, D_MODEL:] = dg_ref[...].astype(dpb_ref.dtype)
        cs = ODD_IN // 4
        dh = sum(_mm_nt(dpb_ref[:, s * cs:(s + 1) * cs], w_ref[s]) for s in range(4))
        x = x_ref[...]
        h_t = (x * (1.0 + mod_ref[1:2, :]) + mod_ref[0:1, :]).T.astype(MXU_DTYPE)
        for s in range(4):
            dw_ref[s] += jnp.dot(h_t, dpb_ref[:, s * cs:(s + 1) * cs], preferred_element_type=F32)
        vec_ref[5:6, :] += jnp.sum(dh, axis=0, keepdims=True)
        vec_ref[6:7, :] += jnp.sum(dh * x, axis=0, keepdims=True)
        dx_ref[...] = dres_ref[...] + dh * (1.0 + mod_ref[1:2, :])

    return pl.pallas_call(
        body, name="odd_proj_bwd", grid=(steps,),
        in_specs=_halo_specs(tm, seq, D_MODEL) * 3 + [_rows(tm, D_MODEL)] * 3
        + [_full((3, D_MODEL)), _full((4, D_MODEL)), _const((4, D_MODEL, ODD_IN // 4))],
        out_specs=[_rows(tm, D_MODEL), _const((4, D_MODEL, ODD_IN // 4)), _full((8, D_MODEL))],
        out_shape=[_sds((seq, D_MODEL)), _sds((4, D_MODEL, ODD_IN // 4)), _sds((8, D_MODEL))],
        scratch_shapes=[pltpu.VMEM((tm, ODD_IN), MXU_DTYPE)],
        compiler_params=_params("arbitrary"),
    )(dxc_f, dxc_f, dxc_f, dxc_r, dxc_r, dxc_r, xr, xr, xr, dg, x1, dres, mod, conv_w, w_in4)


def _even_out_bwd(dx1, z, ycat, g, mod, ln_g, w_out, seq, rider=None):
    tm = _row_tile(seq, 512)
    steps = seq // tm

    def body(dx_ref, z_ref, y_ref, g_ref, mod_ref, lg_ref, w_ref, dy_ref, dg_ref, dres_ref, dw_ref, vec_ref):
        i = pl.program_id(0)

        @pl.when(i == 0)
        def _():
            dw_ref[...] = jnp.zeros_like(dw_ref)
            vec_ref[...] = jnp.zeros_like(vec_ref)

        zhat, rstd = _ln_stats(z_ref[...])
        dx1_ = dx_ref[...]
        dz = _ln_bwd(dx1_, zhat, rstd, lg_ref[...])
        vec_ref[0:1, :] += jnp.sum(dx1_ * zhat, axis=0, keepdims=True)
        vec_ref[1:2, :] += jnp.sum(dx1_, axis=0, keepdims=True)
        dres_ref[...] = ALPHA * dz
        gate = mod_ref[2:3, :]
        gg = g_ref[...].astype(F32)
        sg = _sigmoid(gg)
        silu = gg * sg
        ycat_ = y_ref[...].astype(F32)
        dw_ref[...] += _mm_tn(ycat_ * silu, dz)
        dy = _mm_nt(gate * dz, w_ref[...])
        dy_ref[...] = (dy * silu).astype(dy_ref.dtype)
        dg_ref[...] = (dy * ycat_ * (sg * (1.0 + gg * (1.0 - sg)))).astype(dg_ref.dtype)

        @pl.when(i == steps - 1)
        def _():
            m_acc = dw_ref[...]
            vec_ref[2:3, :] = jnp.sum(w_ref[...].astype(F32) * m_acc, axis=0, keepdims=True)
            dw_ref[...] = m_acc * gate

    return _call(
        body, "even_out_bwd", (steps,),
        [_rows(tm, D_MODEL)] * 4 + [_full((3, D_MODEL)), _full((1, D_MODEL)), _const((D_MODEL, D_MODEL))],
        [_rows(tm, D_MODEL)] * 3 + [_full((D_MODEL, D_MODEL)), _full((8, D_MODEL))],
        [_sds((seq, D_MODEL), ACT_DTYPE), _sds((seq, D_MODEL), ACT_DTYPE), _sds((seq, D_MODEL)), _sds((D_MODEL, D_MODEL)),
         _sds((8, D_MODEL))],
        (dx1, z, ycat, g, mod, ln_g, w_out), "arbitrary", rider=rider)


def _even_mix_bwd(q, k, v, lse, ycat, dycat, su, sv, sink, sgln_g, sgln_b, sgw, sgb_full, e2, e8, seq, rider=None):
    nb = seq // BLK

    def body(sink_ref, q_ref, k_ref, v_ref, lse_ref, y_ref, dy_ref, su_ref, sv_ref, lng_ref, lnb_ref, sgw_ref, sgb_ref, e2_ref,
             e8_ref, dq_ref, dsu_ref, dsv_ref, dk_ref, dv_ref, dsgw_ref, dsgb_ref, vec_ref, dsink_ref, dsgb_acc):
        n = pl.program_id(0)

        @pl.when(n == 0)
        def _():
            dk_ref[...] = jnp.zeros_like(dk_ref)
            dv_ref[...] = jnp.zeros_like(dv_ref)
            dsgw_ref[...] = jnp.zeros_like(dsgw_ref)
            dsgb_acc[...] = jnp.zeros_like(dsgb_acc)
            vec_ref[...] = jnp.zeros_like(vec_ref)
            dsink_ref[...] = jnp.zeros_like(dsink_ref)

        kband = _band(k_ref, n, nb)
        vband = _band(v_ref, n, nb)
        bias = _band_bias(n, seq)
        lane = _lane_iota((BLK, LANES))
        row8 = lax.broadcasted_iota(jnp.int32, (8, LANES), 0)
        lse = lse_ref[...]
        dkb = jnp.zeros((LANES, 3 * BLK), F32)
        dvb = jnp.zeros((LANES, 3 * BLK), F32)
        dsink = jnp.zeros((8, LANES), F32)
        q_tile = lambda j: q_ref[:, j * LANES:(j + 1) * LANES].astype(F32)
        do_tile = lambda j: dy_ref[:, j * LANES:(j + 1) * LANES].astype(F32)
        dq = [jnp.zeros((BLK, LANES), F32) for _ in range(ATTN_WIDTH // LANES)]
        for kv in range(N_Q_HEADS // Q_PER_KV):
            heads = range(Q_PER_KV * kv, Q_PER_KV * (kv + 1))
            lse4, delta4 = [], []
            for h in heads:
                head_lanes = (lane < HEAD_DIM) if h % 2 == 0 else (lane >= HEAD_DIM)
                lse4.append(jnp.sum(jnp.where(lane == h, lse, 0.0), axis=1, keepdims=True))
                o_tile = y_ref[:, (h // 2) * LANES:(h // 2 + 1) * LANES].astype(F32)
                delta4.append(jnp.sum(jnp.where(head_lanes, do_tile(h // 2) * o_tile, 0.0), axis=1, keepdims=True))
            lse4, delta4 = jnp.concatenate(lse4, axis=0), jnp.concatenate(delta4, axis=0)
            q4, do4 = _stack_heads(q_tile, kv), _stack_heads(do_tile, kv)
            s = _mm_nt(q4, kband) * (HEAD_DIM ** -0.5) + bias
            p = jnp.exp(s - lse4)
            wsink = jnp.exp(_per_head_column([sink_ref[h] for h in heads]) - lse4) * delta4
            ds = p * (_mm_nt(do4, vband) - delta4) * (HEAD_DIM ** -0.5)
            dq4 = _mm(ds, kband)
            dkb = dkb + _mm_tn(q4, ds)
            dvb = dvb + _mm_tn(do4, p)
            for g, h in enumerate(heads):
                dq[h // 2] = dq[h // 2] + _from_kv_lanes(dq4[g * BLK:(g + 1) * BLK], h)
                dsink = dsink + jnp.where(row8 == h, -jnp.sum(wsink[g * BLK:(g + 1) * BLK]), 0.0)
        for j in range(ATTN_WIDTH // LANES):
            dq_ref[:, j * LANES:(j + 1) * LANES] = dq[j].astype(dq_ref.dtype)
        dsink_ref[...] += dsink
        prev = jnp.maximum(n - 1, 0)
        nxt = jnp.minimum(n + 1, nb - 1)
        for part, blk_i in enumerate((prev, n, nxt)):
            rows = pl.ds(pl.multiple_of(blk_i * BLK, BLK), BLK)
            dk_ref[rows, :] += dkb[:, part * BLK:(part + 1) * BLK].T
            dv_ref[rows, :] += dvb[:, part * BLK:(part + 1) * BLK].T

        e2 = e2_ref[...]
        lng = lng_ref[...]
        svo, vn, vhat, rstd = _sg_forward(sv_ref[...].astype(F32), lng, lnb_ref[...], sgw_ref, sgb_ref[...], e2)
        for j in range(SG_WIDTH // LANES):
            cs = slice(j * LANES, (j + 1) * LANES)
            dysg = dy_ref[:, ATTN_WIDTH + j * LANES:ATTN_WIDTH + (j + 1) * LANES].astype(F32)
            dsu_ref[:, cs] = (dysg * svo[j]).astype(dsu_ref.dtype)
            dsvo = dysg * su_ref[:, cs].astype(F32)
            dsgb_acc[:, cs] += dsvo
            d_lo = jnp.where(lane < HEAD_DIM, dsvo, 0.0)
            d_hi = dsvo - d_lo
            dsgw_ref[2 * j] += _mm_nt(d_lo, vn[j])
            dsgw_ref[2 * j + 1] += _mm_nt(d_hi, vn[j])
            dvn = _mm_tn(sgw_ref[2 * j], d_lo) + _mm_tn(sgw_ref[2 * j + 1], d_hi)
            vec_ref[0:1, cs] += jnp.sum(dvn * vhat[j], axis=0, keepdims=True)
            vec_ref[1:2, cs] += jnp.sum(dvn, axis=0, keepdims=True)
            dvh = dvn * lng[:, cs]
            m1 = _group_sum(dvh, e2) * (1.0 / HEAD_DIM)
            m2 = _group_sum(dvh * vhat[j], e2) * (1.0 / HEAD_DIM)
            dsv_ref[:, cs] = (rstd[j] * (dvh - m1 - vhat[j] * m2)).astype(dsv_ref.dtype)

        @pl.when(n == nb - 1)
        def _():
            rest = dsgb_acc[...]
            total = jnp.zeros((8, BLK), F32)
            for _ in range(3):
                part = rest.astype(MXU_DTYPE)
                total = total + lax.dot_general(e8_ref[...], part, (((1,), (1,)), ((), ())), preferred_element_type=F32)
                rest = rest - part.astype(F32)
            dsgb_ref[...] = total

    blk = lambda w: pl.BlockSpec((BLK, w), lambda n: (n, 0))
    return _call(
        body, "even_mix_bwd", (nb,),
        [pl.BlockSpec(memory_space=pltpu.SMEM), blk(512), _full((seq, LANES)), _full((seq, LANES)), blk(LANES),
         blk(D_MODEL), blk(D_MODEL), blk(512), blk(512), _full((1, 512)), _full((1, 512)), _full((8, BLK, BLK)),
         _full((BLK, 512)), _full((LANES, LANES)), _full((8, 512))],
        [blk(512), blk(512), blk(512), _full((seq, LANES)), _full((seq, LANES)), _full((8, BLK, BLK)),
         _full((8, BLK)), _full((8, 512)), _full((8, LANES))],
        [_sds((seq, 512), ACT_DTYPE), _sds((seq, 512), ACT_DTYPE), _sds((seq, 512), ACT_DTYPE), _sds((seq, LANES)), _sds((seq, LANES)),
         _sds((8, BLK, BLK)), _sds((8, BLK)), _sds((8, 512)), _sds((8, LANES))],
        (sink, q, k, v, lse, ycat, dycat, su, sv, sgln_g, sgln_b, sgw, sgb_full, e2, e8), "arbitrary",
        scratch=[pltpu.VMEM((BLK, 512), F32)], rider=rider)


def _even_proj_bwd(dq, dk, dv, dsu, dsv, dg, x, dres, mod, tabs, w_in_t, seq):
    tm = _row_tile(seq, 512)

    def body(dq_ref, dk_ref, dv_ref, dsu_ref, dsv_ref, dg_ref, x_ref, dres_ref, mod_ref, cos_ref, sp_ref, sm_ref, wt_ref,
             dx_ref, dw_ref, vec_ref, dpb_ref):
        @pl.when(pl.program_id(0) == 0)
        def _():
            vec_ref[...] = jnp.zeros_like(vec_ref)
            dw_ref[...] = jnp.zeros_like(dw_ref)

        cos_t, sin_p, sin_m = cos_ref[...], sp_ref[...], sm_ref[...]
        dt = dpb_ref.dtype
        for j in range(ATTN_WIDTH // LANES):
            cs = slice(j * LANES, (j + 1) * LANES)
            dpb_ref[:, cs] = _rope_t(dq_ref[:, cs].astype(F32), cos_t, sin_p, sin_m).astype(dt)
        dpb_ref[:, 512:640] = _rope_t(dk_ref[...], cos_t, sin_p, sin_m).astype(dt)
        dpb_ref[:, 640:768] = dv_ref[...].astype(dt)
        dpb_ref[:, 768:1280] = dsu_ref[...].astype(dt)
        dpb_ref[:, 1280:1792] = dsv_ref[...].astype(dt)
        dpb_ref[:, 1792:2816] = dg_ref[...].astype(dt)
        dpb = dpb_ref[...]
        dh = jnp.dot(dpb, wt_ref[...], preferred_element_type=F32)
        x_ = x_ref[...]
        hb = (x_ * (1.0 + mod_ref[1:2, :]) + mod_ref[0:1, :]).astype(MXU_DTYPE)
        dw_ref[...] += _mm_tn(dpb, hb)
        vec_ref[0:1, :] += jnp.sum(dh, axis=0, keepdims=True)
        vec_ref[1:2, :] += jnp.sum(dh * x_, axis=0, keepdims=True)
        dx_ref[...] = dres_ref[...] + dh * (1.0 + mod_ref[1:2, :])

    return pl.pallas_call(
        body, name="even_proj_bwd", grid=(seq // tm,),
        in_specs=[_rows(tm, 512), _rows(tm, LANES), _rows(tm, LANES), _rows(tm, 512), _rows(tm, 512), _rows(tm, D_MODEL),
                  _rows(tm, D_MODEL), _rows(tm, D_MODEL), _full((3, D_MODEL))] + [_rows(tm, LANES)] * 3
        + [_const((EVEN_IN, D_MODEL))],
        out_specs=[_rows(tm, D_MODEL), _const((EVEN_IN, D_MODEL)), _full((8, D_MODEL))],
        out_shape=[_sds((seq, D_MODEL)), _sds((EVEN_IN, D_MODEL)), _sds((8, D_MODEL))],
        scratch_shapes=[pltpu.VMEM((tm, EVEN_IN), MXU_DTYPE)],
        compiler_params=_params("arbitrary"),
    )(dq, dk, dv, dsu, dsv, dg, x, dres, mod, *tabs, w_in_t)


def _local_step(x, posf, tgt, mod, w, seq, ride=None):
    rid = lambda make, *a: None if ride is None else make(*a)
    mxu = lambda a: a.astype(MXU_DTYPE)
    row = lambda a: a.reshape(1, -1)
    tabs = _rope_tables(posf, seq)
    e2 = mxu(jnp.kron(jnp.eye(2, dtype=F32), jnp.ones((HEAD_DIM, HEAD_DIM), F32)))
    e8 = mxu(jnp.repeat(jnp.eye(N_SG_GROUPS, dtype=F32), HEAD_DIM, axis=1))
    sgw = mxu(w["ev_sg_w"])
    sgb_full = jnp.repeat(w["ev_sg_b"].T, HEAD_DIM, axis=1)
    sgln_g, sgln_b = row(w["ev_sg_ln_g"]), row(w["ev_sg_ln_b"])
    sink = w["ev_sink"].reshape(N_Q_HEADS)
    ev_w_in_t = mxu(w["ev_w_in_t"])
    if ride is None:
        ev_w_out, od_w_in, od_w_out = mxu(w["ev_w_out"]), mxu(w["od_w_in"]), mxu(w["od_w_out"])
    wcat = mxu(jnp.concatenate([w["od_w_a"][0], w["od_w_x"][0], w["od_w_a"][1], w["od_w_x"][1]], axis=2))
    gate_bias = jnp.stack([w["od_b_a"][0], w["od_b_x"][0], w["od_b_a"][1], w["od_b_x"][1]])
    conv_b = row(w["od_conv_b"])
    ln_g, ln_b = w["ln_g"], w["ln_b"]

    (q, k, v, su, sv, g0), got = _even_proj(x, mod[0], ev_w_in_t, tabs, seq, rid(_gather_rider, ride and ride["ev_w_out"]))
    if ride is not None:
        ev_w_out = got[0].reshape(D_MODEL, D_MODEL)
    (ycat, lse), got = _even_mix(q, k, v, su, sv, sink, sgln_g, sgln_b, sgw, sgb_full, e2, seq,
                                 rid(_gather_rider, ride and ride["od_w_in"]))
    if ride is not None:
        od_w_in = got[0]
    (z0, x1, xr, g1), got = _even_out(ycat, g0, x, mod[0], mod[1], ev_w_out, od_w_in, ln_g[0:1], ln_b[0:1], seq,
                                      rid(_gather_rider, ride and ride["od_w_out"]))
    if ride is not None:
        od_w_out = got[0].reshape(D_MODEL, D_MODEL)
    lru = (xr, w["od_conv_w"], conv_b, wcat, gate_bias, w["od_lam"], seq)
    hf, hpf = _lru_fwd(*lru, 0)
    hr, hpr = _lru_fwd(*lru, 1)
    dhs, dg1, dres1, loss, d_od_w_out, vec_o = _odd_out_and_loss(hf, hr, g1, x1, tgt, mod[1], od_w_out, ln_g[1:2], ln_b[1:2], seq)
    dxc_f, dw_f, vec_f = _lru_bwd(xr, dhs, hpf, *lru[1:], 0)
    dxc_r, dw_r, vec_r = _lru_bwd(xr, dhs, hpr, *lru[1:], 1)
    dx1, d_od_w_in, vec_p = _odd_proj_bwd(dxc_f, dxc_r, xr, dg1, x1, dres1, mod[1], w["od_conv_w"], od_w_in, seq)
    d_od_w_a = jnp.stack([dw_f[:, :, 0:128], dw_r[:, :, 0:128]])
    d_od_w_x = jnp.stack([dw_f[:, :, 128:256], dw_r[:, :, 128:256]])
    od_parts = [d_od_w_in.reshape(4, 2, 512, 512), d_od_w_out.reshape(4, 2, 128, D_MODEL),
                d_od_w_a.reshape(4, 2, 2 * BLK, BLK), d_od_w_x.reshape(4, 2, 2 * BLK, BLK)]
    (dycat, dg0, dres0, d_ev_w_out, vec_e), got_od = _even_out_bwd(dx1, z0, ycat, g0, mod[0], ln_g[0:1], ev_w_out, seq,
                                                                   rid(_sibling_swap_rider, od_parts))
    if ride is not None:
        od_sums = _sum_sibling(ride["core"], od_parts, got_od, [ride["wire"]] * 4, "sum_sibling_od")
    (dq, dsu, dsv, dk, dv, d_sgw, d_sgb, vec_s, d_sink), od_slots = _even_mix_bwd(
        q, k, v, lse, ycat, dycat, su, sv, sink, sgln_g, sgln_b, sgw, sgb_full, e2, e8, seq,
        rid(_chip_exchange_rider, ride and od_sums))
    grad_x, d_ev_w_in_t, vec_x = _even_proj_bwd(dq, dk, dv, dsu, dsv, dg0, x, dres0, mod[0], tabs, ev_w_in_t, seq)

    dmod = jnp.stack([jnp.stack([vec_x[0], vec_x[1], vec_e[2]]), jnp.stack([vec_p[5], vec_p[6], vec_o[2]])])
    grads = {
        "ln_g": jnp.stack([vec_e[0], vec_o[0]]), "ln_b": jnp.stack([vec_e[1], vec_o[1]]),
        "ev_w_in_t": d_ev_w_in_t, "ev_w_out": d_ev_w_out, "ev_sink": d_sink[:, 0],
        "ev_sg_ln_g": vec_s[0], "ev_sg_ln_b": vec_s[1], "ev_sg_w": d_sgw,
        "ev_sg_b": d_sgb,
        "od_conv_w": vec_p[0:4], "od_conv_b": vec_p[4],
        "od_b_a": jnp.stack([vec_f[0], vec_r[0]]), "od_b_x": jnp.stack([vec_f[1], vec_r[1]]),
        "od_lam": jnp.stack([vec_f[2], vec_r[2]]),
    }
    if ride is None:
        grads.update({"od_w_in": d_od_w_in, "od_w_out": d_od_w_out, "od_w_a": d_od_w_a, "od_w_x": d_od_w_x})
    else:
        grads["od_slots"] = od_slots
    return loss[0, 0], grad_x, dmod, grads


def _allgather8(block, name):
    m_per, n = block.shape

    def body(x_ref, out_ref, send_sems, recv_sems, local_sem):
        x, y, c = _place()
        me, sibling = (x, y, c), (x, y, 1 - c)
        chips = [(1 - x, y), (x, 1 - y), (1 - x, 1 - y)]

        def rows(px, py, pc):
            return out_ref.at[pl.ds((4 * px + 2 * py + pc) * m_per, m_per), :]

        def copy(k, blk, to, src=None):
            return pltpu.make_async_remote_copy(src_ref=rows(*blk) if src is None else src, dst_ref=rows(*blk),
                                                send_sem=send_sems.at[k], recv_sem=recv_sems.at[k], device_id=to,
                                                device_id_type=MESH)

        mine = pltpu.make_async_copy(x_ref, rows(*me), local_sem)
        mine.start()
        first = [copy(0, me, sibling, src=x_ref)] + [copy(1 + j, me, (*chip, c), src=x_ref) for j, chip in enumerate(chips)]
        for cp in first:
            cp.start()
        passed = [copy(4 + j, (*chip, c), sibling) for j, chip in enumerate(chips)]
        for j, chip in enumerate(chips):
            copy(1 + j, (*chip, c), me).wait_recv()
            passed[j].start()
        copy(0, sibling, me).wait_recv()
        for j, chip in enumerate(chips):
            copy(4 + j, (*chip, 1 - c), me).wait_recv()
        for cp in first + passed:
            cp.wait_send()
        mine.wait()

    return pl.pallas_call(
        body, name=name, out_shape=_sds((8 * m_per, n), block.dtype),
        in_specs=[pl.BlockSpec(memory_space=pltpu.VMEM)], out_specs=pl.BlockSpec(memory_space=pltpu.VMEM),
        scratch_shapes=[pltpu.SemaphoreType.DMA((7,)), pltpu.SemaphoreType.DMA((7,)), pltpu.SemaphoreType.DMA],
        compiler_params=pltpu.CompilerParams(vmem_limit_bytes=VMEM_LIMIT),
    )(block)


class _Copies:
    def __init__(self, send_sems, recv_sems, local_sems, stages):
        self.send_sems, self.recv_sems, self.local_sems, self.stages = send_sems, recv_sems, local_sems, stages
        self.sent, self.staged, self.locals = [], [], []

    def remote(self, k, src, dst, to):
        return pltpu.make_async_remote_copy(src_ref=src, dst_ref=dst, send_sem=self.send_sems.at[k], recv_sem=self.recv_sems.at[k],
                                            device_id=to, device_id_type=MESH)

    def send(self, k, src, dst, to):
        cp = self.remote(k, src, dst, to)
        cp.start()
        self.sent.append(cp)

    def arrived(self, k, dst, frm):
        self.remote(k, dst, dst, frm).wait_recv()

    def local(self, src, dst):
        k = len(self.staged)
        cp = pltpu.make_async_copy(src, self.stages[k], self.local_sems.at[2 * k])
        cp.start()
        self.staged.append((cp, dst))

    def flush(self):
        for k in range(len(self.locals), len(self.staged)):
            cp, dst = self.staged[k]
            cp.wait()
            out = pltpu.make_async_copy(self.stages[k], dst, self.local_sems.at[2 * k + 1])
            out.start()
            self.locals.append(out)

    def drain(self):
        self.flush()
        for cp in self.sent:
            cp.wait_send()
        for cp in self.locals:
            cp.wait()


def _comm_call(body, name, ins, out_shapes, n_remote, stages):
    n_in, n_out = len(ins), len(out_shapes)

    def kern(*refs):
        in_refs, out_refs = refs[:n_in], refs[n_in:n_in + n_out]
        send_sems, recv_sems, local_sems = refs[n_in + n_out:n_in + n_out + 3]
        body(_Copies(send_sems, recv_sems, local_sems, refs[n_in + n_out + 3:]), in_refs, out_refs)

    hbm = pl.BlockSpec(memory_space=pl.ANY)
    return pl.pallas_call(
        kern, name=name, out_shape=out_shapes, in_specs=[hbm] * n_in, out_specs=[hbm] * n_out,
        scratch_shapes=[pltpu.SemaphoreType.DMA((n_remote,)), pltpu.SemaphoreType.DMA((n_remote,)),
                        pltpu.SemaphoreType.DMA((2 * len(stages),))] + [pltpu.VMEM(s, d) for s, d in stages],
        compiler_params=pltpu.CompilerParams(vmem_limit_bytes=VMEM_LIMIT),
    )(*ins)


def _gather_to_all(cps, pairs, me, sibling, other_chips, c, base):
    idx = lambda p: 4 * p[0] + 2 * p[1] + p[2]
    for i, (src, dst) in enumerate(pairs):
        cps.local(src, dst.at[idx(me)])
        cps.send(base + 7 * i, src, dst.at[idx(me)], sibling)
        for j, chip in enumerate(other_chips):
            cps.send(base + 7 * i + 1 + j, src, dst.at[idx(me)], (*chip, c))
    cps.flush()
    for j, chip in enumerate(other_chips):
        for i, (_, dst) in enumerate(pairs):
            got = dst.at[idx((*chip, c))]
            cps.arrived(base + 7 * i + 1 + j, got, (*chip, c))
            cps.send(base + 7 * i + 4 + j, got, got, sibling)
    for i, (_, dst) in enumerate(pairs):
        cps.arrived(base + 7 * i, dst.at[idx(sibling)], sibling)
        for j, chip in enumerate(other_chips):
            cps.arrived(base + 7 * i + 4 + j, dst.at[idx((*chip, 1 - c))], sibling)


def _gather_weights(shards, small):
    n = len(shards)

    def body(cps, ins, outs):
        x, y, c = _place()
        me, sibling, mine = (x, y, c), (x, y, 1 - c), 2 * x + y
        chips = [(1 - x, y), (x, 1 - y), (1 - x, 1 - y)]
        for i in range(n):
            cps.local(ins[i], outs[i].at[mine])
        for j, (px, py) in enumerate(chips):
            for i in range(n):
                hr = shards[i].shape[0] // 2
                rows = pl.ds(c * hr, hr)
                cps.send(6 * i + j, ins[i].at[rows], outs[i].at[mine, rows], (px, py, c))
        _gather_to_all(cps, [(ins[n], outs[n])], me, sibling, chips, c, 6 * n)
        for j, (px, py) in enumerate(chips):
            for i in range(n):
                hr = shards[i].shape[0] // 2
                got = outs[i].at[2 * px + py, pl.ds(c * hr, hr)]
                cps.arrived(6 * i + j, got, (px, py, c))
                cps.send(6 * i + 3 + j, got, got, sibling)
        for j, (px, py) in enumerate(chips):
            for i in range(n):
                hr = shards[i].shape[0] // 2
                cps.arrived(6 * i + 3 + j, outs[i].at[2 * px + py, pl.ds((1 - c) * hr, hr)], sibling)
        cps.drain()

    return _comm_call(body, "gather_weights", list(shards) + [small],
                      [_sds((4,) + s.shape, s.dtype) for s in shards] + [_sds((8,) + small.shape, small.dtype)], 6 * n + 7,
                      [(a.shape, a.dtype) for a in list(shards) + [small]])


def _reduce_sibling(parts, dmod_rows):
    n = len(parts)

    def body(cps, ins, outs):
        x, y, c = _place()
        me, sibling = (x, y, c), (x, y, 1 - c)
        chips = [(1 - x, y), (x, 1 - y), (1 - x, 1 - y)]
        for i in range(n):
            cps.send(i, ins[i].at[:, 1 - c], outs[i], sibling)
        _gather_to_all(cps, [(ins[n], outs[n])], me, sibling, chips, c, n)
        for i in range(n):
            cps.arrived(i, outs[i], sibling)
        cps.drain()

    return _comm_call(body, "reduce_sibling", list(parts) + [dmod_rows],
                      [_sds((4,) + p.shape[2:], p.dtype) for p in parts] + [_sds((8,) + dmod_rows.shape, dmod_rows.dtype)], n + 7,
                      [(dmod_rows.shape, dmod_rows.dtype)])


def _reduce_chips(parts):
    n = len(parts)

    def body(cps, ins, outs):
        x, y, c = _place()
        mine = 2 * x + y
        chips = [(1 - x, y), (x, 1 - y), (1 - x, 1 - y)]
        for i in range(n):
            cps.local(ins[i].at[mine], outs[i].at[mine])
        for j, (px, py) in enumerate(chips):
            for i in range(n):
                cps.send(3 * i + j, ins[i].at[2 * px + py], outs[i].at[mine], (px, py, c))
        cps.flush()
        for j, (px, py) in enumerate(chips):
            for i in range(n):
                cps.arrived(3 * i + j, outs[i].at[2 * px + py], (px, py, c))
        cps.drain()

    return _comm_call(body, "reduce_chips", list(parts), [_sds(p.shape, p.dtype) for p in parts], 3 * n,
                      [(p.shape[1:], p.dtype) for p in parts])


def _gather_reduced(shard_parts, repl_parts):
    ns, nr = len(shard_parts), len(repl_parts)

    def body(cps, ins, outs):
        x, y, c = _place()
        me, sibling = (x, y, c), (x, y, 1 - c)
        chips = [(1 - x, y), (x, 1 - y), (1 - x, 1 - y)]
        for i in range(ns):
            cps.local(ins[i], outs[i].at[c])
            cps.send(i, ins[i], outs[i].at[c], sibling)
        _gather_to_all(cps, [(ins[ns + i], outs[ns + i]) for i in range(nr)], me, sibling, chips, c, ns)
        for i in range(ns):
            cps.arrived(i, outs[i].at[1 - c], sibling)
        cps.drain()

    return _comm_call(body, "gather_reduced", list(shard_parts) + list(repl_parts),
                      [_sds((2,) + p.shape, p.dtype) for p in shard_parts] + [_sds((8,) + p.shape, p.dtype) for p in repl_parts],
                      ns + 7 * nr, [(p.shape, p.dtype) for p in list(shard_parts) + list(repl_parts)])


def _sum_sibling(core, parts, got, wire, name):
    n = len(parts)

    def body(core_ref, *refs):
        for i in range(n):
            refs[2 * n + i][0] = (refs[i][0] + refs[n + i][0]).astype(wire[i])

    keep_spec = lambda p: pl.BlockSpec((1, None) + p.shape[2:], lambda s, core_ref: (s, core_ref[0], 0, 0))
    slot_spec = lambda p: pl.BlockSpec((1,) + p.shape[2:], lambda s, core_ref: (s, 0, 0))
    return pl.pallas_call(
        body, name=name,
        grid_spec=pltpu.PrefetchScalarGridSpec(
            num_scalar_prefetch=1, grid=(4,), in_specs=[keep_spec(p) for p in parts] + [slot_spec(p) for p in parts],
            out_specs=[slot_spec(p) for p in parts]),
        out_shape=[_sds((4,) + p.shape[2:], wire[i]) for i, p in enumerate(parts)],
        compiler_params=_params("parallel"),
    )(core, *parts, *got)


def _sum_slots(slots, name):
    n = len(slots)

    def spec_pair(p):
        k, rows, cols = p.shape
        sub = 16 if p.dtype == BF16 else 8
        if (rows // 2) % sub == 0:
            return pl.BlockSpec((k, rows // 2, cols), lambda i: (0, i, 0)), pl.BlockSpec((rows // 2, cols), lambda i: (i, 0))
        return pl.BlockSpec((k, rows, cols), lambda i: (0, 0, 0)), pl.BlockSpec((rows, cols), lambda i: (0, 0))

    pairs = [spec_pair(p) for p in slots]

    def body(*refs):
        for i in range(n):
            acc = refs[i][0].astype(F32)
            for j in range(1, slots[i].shape[0]):
                acc = acc + refs[i][j].astype(F32)
            refs[n + i][...] = acc

    return pl.pallas_call(
        body, name=name, grid=(2,), in_specs=[a for a, _ in pairs], out_specs=[b for _, b in pairs],
        out_shape=[_sds(p.shape[1:]) for p in slots], compiler_params=_params("arbitrary"),
    )(*slots)


def _modulation(c_all, ada_w, ada_b):
    cols = ada_w.shape[2]

    def body(c_ref, w_ref, b_ref, o_ref):
        cc = c_ref[...]
        o_ref[0] = _mm(cc * _sigmoid(cc), w_ref[0]) + b_ref[0]

    return pl.pallas_call(
        body, name="modulation", grid=(2,),
        in_specs=[_full((8, D_MODEL)), pl.BlockSpec((1, D_MODEL, cols), lambda l: (l, 0, 0)), pl.BlockSpec((1, 1, cols), lambda l: (l, 0, 0))],
        out_specs=pl.BlockSpec((1, 8, cols), lambda l: (l, 0, 0)), out_shape=_sds((2, 8, cols)),
        compiler_params=_params("parallel"),
    )(c_all, ada_w, ada_b)


def _adamw_math(w, g, m, v):
    m = ADAM_B1 * m + (1.0 - ADAM_B1) * g
    v = ADAM_B2 * v + (1.0 - ADAM_B2) * (g * g)
    m_hat = m / (1.0 - ADAM_B1 ** ADAM_STEP)
    v_hat = v / (1.0 - ADAM_B2 ** ADAM_STEP)
    delta = -ADAM_LR * (m_hat / (jnp.sqrt(v_hat) + ADAM_EPS) + ADAM_WD * w)
    return delta, m, v


def _ada_update(c_all, dmod, w, m, v):
    cols = w.shape[2]
    tr = 256
    spec3 = pl.BlockSpec((1, tr, cols), lambda l, i: (l, i, 0))

    def body(c_ref, d_ref, w_ref, m_ref, v_ref, g_ref, dl_ref, nm_ref, nv_ref):
        cc = c_ref[...]
        g = _mm_tn(cc * _sigmoid(cc), d_ref[0])
        g_ref[0] = g
        dl_ref[0], nm_ref[0], nv_ref[0] = _adamw_math(w_ref[0], g, m_ref[0], v_ref[0])

    return pl.pallas_call(
        body, name="ada_update", grid=(2, D_MODEL // tr),
        in_specs=[pl.BlockSpec((8, tr), lambda l, i: (0, i)), pl.BlockSpec((1, 8, cols), lambda l, i: (l, 0, 0)), spec3, spec3, spec3],
        out_specs=[spec3] * 4, out_shape=[_sds(w.shape)] * 4, compiler_params=_params("parallel", "parallel"),
    )(c_all, dmod, w, m, v)


def _adamw_matrices(params):
    n = len(params)
    steps = 8

    def body(*refs):
        ins, outs = refs[:4 * n], refs[4 * n:]
        for j in range(n):
            w_ref, g_ref, m_ref, v_ref = ins[4 * j:4 * j + 4]
            outs[3 * j][...], outs[3 * j + 1][...], outs[3 * j + 2][...] = _adamw_math(w_ref[...], g_ref[...], m_ref[...], v_ref[...])

    spec = lambda p: _rows(p[0].shape[0] // steps, p[0].shape[1])
    res = pl.pallas_call(
        body, name="adamw_matrices", grid=(steps,), in_specs=[spec(p) for p in params for _ in range(4)],
        out_specs=[spec(p) for p in params for _ in range(3)], out_shape=[_sds(p[0].shape) for p in params for _ in range(3)],
        compiler_params=_params("parallel"),
    )(*[a for p in params for a in p])
    return [tuple(res[3 * j:3 * j + 3]) for j in range(n)]


def _adamw_small(params):
    n = len(params)

    def body(*refs):
        ins, outs = refs[:4 * n], refs[4 * n:]
        for j in range(n):
            w_ref, g_ref, m_ref, v_ref = ins[4 * j:4 * j + 4]
            outs[3 * j][...], outs[3 * j + 1][...], outs[3 * j + 2][...] = _adamw_math(w_ref[...], g_ref[...], m_ref[...], v_ref[...])

    flat = [a for p in params for a in p]
    res = pl.pallas_call(body, name="adamw_small", out_shape=[_sds(p[0].shape) for p in params for _ in range(3)])(*flat)
    return [tuple(res[3 * j:3 * j + 3]) for j in range(n)]


def _cols(a, start, size):
    return lax.dynamic_slice_in_dim(a, start, size, axis=a.ndim - 1)


def kernel(x, c, positions, ada_w, ada_b, ln_g, ln_b, ev_w_in, ev_w_out, ev_sink, ev_sg_ln_g, ev_sg_ln_b, ev_sg_w, ev_sg_b, od_w_in, od_conv_w, od_conv_b, od_w_a, od_b_a, od_w_x, od_b_x, od_lam, od_w_out, loss_target, m_ada_w, m_ada_b, m_ln_g, m_ln_b, m_ev_w_in, m_ev_w_out, m_ev_sink, m_ev_sg_ln_g, m_ev_sg_ln_b, m_ev_sg_w, m_ev_sg_b, m_od_w_in, m_od_conv_w, m_od_conv_b, m_od_w_a, m_od_b_a, m_od_w_x, m_od_b_x, m_od_lam, m_od_w_out, v_ada_w, v_ada_b, v_ln_g, v_ln_b, v_ev_w_in, v_ev_w_out, v_ev_sink, v_ev_sg_ln_g, v_ev_sg_ln_b, v_ev_sg_w, v_ev_sg_b, v_od_w_in, v_od_conv_w, v_od_conv_b, v_od_w_a, v_od_b_a, v_od_w_x, v_od_b_x, v_od_lam, v_od_w_out):
    seq = x.shape[1]
    px, py, pc = _place()
    chip = 2 * px + py
    dev = 2 * chip + pc

    small = jnp.concatenate([od_conv_w[0].reshape(-1), od_conv_b[0], od_b_a[0].reshape(-1), jnp.zeros((256,), F32),
                             od_b_x[0].reshape(-1), od_lam[0].reshape(-1)]).reshape(3, D_MODEL)
    blk = jnp.concatenate([c, small, jnp.zeros((4, D_MODEL), F32)], axis=0)
    tr = lambda a: jnp.swapaxes(a, -1, -2)
    wire_w = lambda a: a.astype(MXU_DTYPE)
    ev_w_in4, g_small = _gather_weights([wire_w(tr(ev_w_in[0]))], blk)
    core = pc.astype(jnp.int32).reshape(1)
    ride = {"ev_w_out": wire_w(ev_w_out[0]), "od_w_in": wire_w(od_w_in[0]), "od_w_out": wire_w(od_w_out[0]),
            "core": core, "wire": MXU_DTYPE}
    c_all = g_small[:, 0, :]
    per_chip = g_small[0::2]
    conv_w = per_chip[:, 1].reshape(4, 4, 256).transpose(1, 0, 2).reshape(4, D_MODEL)
    conv_b = per_chip[:, 2, 0:256].reshape(D_MODEL)
    b_a = per_chip[:, 2, 256:768].reshape(4, 2, 256).transpose(1, 0, 2).reshape(2, D_MODEL)
    b_x = per_chip[:, 3, 0:512].reshape(4, 2, 256).transpose(1, 0, 2).reshape(2, D_MODEL)
    lam = per_chip[:, 3, 512:1024].reshape(4, 2, 256).transpose(1, 0, 2).reshape(2, D_MODEL)

    w_full = {
        "ev_w_in_t": ev_w_in4.reshape(EVEN_IN, D_MODEL),
        "ev_sink": ev_sink[0], "ev_sg_ln_g": ev_sg_ln_g[0], "ev_sg_ln_b": ev_sg_ln_b[0], "ev_sg_w": ev_sg_w[0],
        "ev_sg_b": ev_sg_b[0], "od_conv_w": conv_w, "od_conv_b": conv_b, "od_w_a": od_w_a[0], "od_b_a": b_a,
        "od_w_x": od_w_x[0], "od_b_x": b_x, "od_lam": lam, "ln_g": ln_g, "ln_b": ln_b,
    }

    ada_cols = ada_w.shape[2]
    mod_sh = _modulation(c_all, ada_w, _cols(ada_b, chip * ada_cols, ada_cols).reshape(2, 1, ada_cols))
    mod_all = _allgather8(mod_sh.reshape(16, ada_cols), "gather_mod").reshape(4, 2, 2, 8, ada_cols)[:, 0]
    mod_mine = lax.dynamic_index_in_dim(mod_all, dev, axis=2, keepdims=False)
    mod = mod_mine.transpose(1, 0, 2).reshape(2, 3, D_MODEL)

    posf = positions.astype(F32).reshape(seq, 1)
    loss_local, grad_x, dmod, g = _local_step(x[0], posf, loss_target[0], mod, w_full, seq, ride)

    pad = lambda a, n: jnp.concatenate([a.reshape(-1), jnp.zeros((n - a.size,), F32)])
    rows_small = jnp.concatenate([
        dmod.reshape(6, D_MODEL), g["ln_g"][0:1], g["ln_b"][0:1], g["ln_g"][1:2], g["ln_b"][1:2],
        jnp.concatenate([g["ev_sg_ln_g"], g["ev_sg_ln_b"]]).reshape(1, D_MODEL), g["ev_sg_b"].reshape(1, D_MODEL),
        g["od_conv_w"], g["od_conv_b"].reshape(1, D_MODEL), g["od_b_a"], g["od_b_x"], g["od_lam"],
        pad(g["ev_sink"], D_MODEL).reshape(1, D_MODEL), pad(loss_local, D_MODEL).reshape(1, D_MODEL),
        jnp.zeros((39, D_MODEL), F32)], axis=0)
    parts = [g["ev_w_in_t"].reshape(4, 2, 352, D_MODEL), g["ev_w_out"].reshape(4, 2, 128, D_MODEL),
             g["ev_sg_w"].reshape(4, 2, BLK, BLK), rows_small.reshape(4, 2, 8, D_MODEL)]
    wire = [MXU_DTYPE] * 3 + [F32]
    dmod_blk = jnp.concatenate([dmod.reshape(6, D_MODEL), jnp.zeros((2, D_MODEL), F32)], axis=0)
    *got, dmod_gathered = _reduce_sibling(parts, dmod_blk)
    ev_slots = list(_reduce_chips(_sum_sibling(core, parts, got, wire, "sum_sibling")))
    od_slots = list(g["od_slots"])
    mine = _sum_slots(ev_slots[0:2] + od_slots[0:2] + ev_slots[2:3] + od_slots[2:4] + ev_slots[3:4], "sum_chips")
    reduced = _gather_reduced(mine[:4], mine[4:])
    g_ev_w_in_t = reduced[0].reshape(704, D_MODEL)
    g_ev_w_out = reduced[1].reshape(256, D_MODEL)
    g_od_w_in = reduced[2].reshape(D_MODEL, 512)
    g_od_w_out = reduced[3].reshape(256, D_MODEL)
    g_sg_w = reduced[4].reshape(8 * BLK, BLK)
    g_w_a = reduced[5].reshape(16 * BLK, BLK)
    g_w_x = reduced[6].reshape(16 * BLK, BLK)
    gs = reduced[7].reshape(64, D_MODEL)
    loss = gs[24, 0]
    dmod_all = dmod_gathered[:, 0:6].reshape(8, 2, 3 * D_MODEL)
    dmod_sh = _cols(dmod_all, chip * ada_cols, ada_cols).transpose(1, 0, 2)
    g_ada_w, d_ada_w, nm_ada_w, nv_ada_w = _ada_update(c_all, dmod_sh, ada_w, m_ada_w, v_ada_w)

    mats = (("ev_w_out", ev_w_out, g_ev_w_out, m_ev_w_out, v_ev_w_out), ("od_w_in", od_w_in, g_od_w_in, m_od_w_in, v_od_w_in),
            ("od_w_out", od_w_out, g_od_w_out, m_od_w_out, v_od_w_out), ("ev_sg_w", ev_sg_w, g_sg_w, m_ev_sg_w, v_ev_sg_w),
            ("od_w_a", od_w_a, g_w_a, m_od_w_a, v_od_w_a), ("od_w_x", od_w_x, g_w_x, m_od_w_x, v_od_w_x))
    upd = _adamw_matrices([(tr(ev_w_in[0]), g_ev_w_in_t, tr(m_ev_w_in[0]), tr(v_ev_w_in[0]))]
                          + [(w_.reshape(g_.shape), g_, m_.reshape(g_.shape), v_.reshape(g_.shape)) for _, w_, g_, m_, v_ in mats])
    big = {"ev_w_in": tuple(tr(a).reshape(ev_w_in.shape) for a in (g_ev_w_in_t, *upd[0]))}
    for (name, w_, g_, _, _), u in zip(mats, upd[1:]):
        big[name] = tuple(a.reshape(w_.shape) for a in (g_, *u))
    big["ada_w"] = (g_ada_w, d_ada_w, nm_ada_w, nv_ada_w)

    sh = lambda a: _cols(a, chip * 256, 256)
    small_g = {
        "ada_b": gs[0:6].reshape(2, 3 * D_MODEL), "ln_g": jnp.stack([gs[6], gs[8]]), "ln_b": jnp.stack([gs[7], gs[9]]),
        "ev_sink": gs[23:24, 0:8], "ev_sg_ln_g": gs[10:11, 0:512], "ev_sg_ln_b": gs[10:11, 512:1024],
        "ev_sg_b": gs[11].reshape(8, BLK), "od_conv_w": sh(gs[12:16]), "od_conv_b": sh(gs[16:17]), "od_b_a": sh(gs[17:19]),
        "od_b_x": sh(gs[19:21]), "od_lam": sh(gs[21:23]),
    }
    small_in = {"ada_b": (ada_b, m_ada_b, v_ada_b), "ln_g": (ln_g, m_ln_g, v_ln_g), "ln_b": (ln_b, m_ln_b, v_ln_b),
                "ev_sink": (ev_sink, m_ev_sink, v_ev_sink), "ev_sg_ln_g": (ev_sg_ln_g, m_ev_sg_ln_g, v_ev_sg_ln_g),
                "ev_sg_ln_b": (ev_sg_ln_b, m_ev_sg_ln_b, v_ev_sg_ln_b), "ev_sg_b": (ev_sg_b, m_ev_sg_b, v_ev_sg_b),
                "od_conv_w": (od_conv_w, m_od_conv_w, v_od_conv_w), "od_conv_b": (od_conv_b, m_od_conv_b, v_od_conv_b),
                "od_b_a": (od_b_a, m_od_b_a, v_od_b_a), "od_b_x": (od_b_x, m_od_b_x, v_od_b_x),
                "od_lam": (od_lam, m_od_lam, v_od_lam)}
    names_small = list(small_g)
    upd = _adamw_small([(small_in[n][0].reshape(small_g[n].shape), small_g[n], small_in[n][1].reshape(small_g[n].shape),
                         small_in[n][2].reshape(small_g[n].shape)) for n in names_small])
    res = dict(big)
    for n, (d_, nm_, nv_) in zip(names_small, upd):
        shape = small_in[n][0].shape
        res[n] = tuple(a.reshape(shape) for a in (small_g[n], d_, nm_, nv_))

    order = ["ada_w", "ada_b", "ln_g", "ln_b", "ev_w_in", "ev_w_out", "ev_sink", "ev_sg_ln_g", "ev_sg_ln_b", "ev_sg_w", "ev_sg_b",
             "od_w_in", "od_conv_w", "od_conv_b", "od_w_a", "od_b_a", "od_w_x", "od_b_x", "od_lam", "od_w_out"]
    return (loss, grad_x.reshape(x.shape), *[res[n][0] for n in order], *[res[n][1] for n in order],
            *[res[n][2] for n in order], *[res[n][3] for n in order])
```

```python
import functools

import jax
import jax.numpy as jnp
import numpy as np
from jax import lax
from jax.experimental import pallas as pl
from jax.experimental.pallas import tpu as pltpu

F32 = jnp.float32
BF16 = jnp.bfloat16
MXU_DTYPE = BF16
ACT_DTYPE = MXU_DTYPE

D_MODEL = 1024
HEAD_DIM = 64
N_Q_HEADS = 8
Q_PER_KV = 4
ATTN_WIDTH = 512
KV_WIDTH = 128
BLK = 128
ROPE_DIM = 16
ROPE_THETA = 500000.0
N_SG_GROUPS = 8
SG_WIDTH = 512
EVEN_IN = 2816
ODD_IN = 2048
RNN_HEADS = 8
RG_LRU_C = 8.0
ALPHA = (2 * 2) ** 0.25
LN_EPS = 1e-5
NEG_INF = -1e30
ADAM_LR, ADAM_B1, ADAM_B2, ADAM_EPS, ADAM_WD, ADAM_STEP = 0.001, 0.9, 0.999, 1e-08, 0.01, 10

LANES = 128
VMEM_LIMIT = 56 * 1024 * 1024
MESH = pl.DeviceIdType.MESH


def _mm(a, b):
    return jnp.dot(a.astype(MXU_DTYPE), b.astype(MXU_DTYPE), preferred_element_type=F32)


def _mm_nt(a, b):
    return lax.dot_general(a.astype(MXU_DTYPE), b.astype(MXU_DTYPE), (((1,), (1,)), ((), ())), preferred_element_type=F32)


def _mm_tn(a, b):
    return lax.dot_general(a.astype(MXU_DTYPE), b.astype(MXU_DTYPE), (((0,), (0,)), ((), ())), preferred_element_type=F32)


def _sigmoid(x):
    return 1.0 / (1.0 + jnp.exp(-x))


def _ln_stats(z):
    mu = jnp.mean(z, axis=-1, keepdims=True)
    d = z - mu
    var = jnp.mean(d * d, axis=-1, keepdims=True)
    rstd = lax.rsqrt(var + LN_EPS)
    return d * rstd, rstd


def _ln_bwd(dout, zhat, rstd, g):
    dzh = dout * g
    m1 = jnp.mean(dzh, axis=-1, keepdims=True)
    m2 = jnp.mean(dzh * zhat, axis=-1, keepdims=True)
    return rstd * (dzh - m1 - zhat * m2)


def _group_sum(x, e2):
    hi = x.astype(MXU_DTYPE)
    lo = (x - hi.astype(F32)).astype(MXU_DTYPE)
    return jnp.dot(hi, e2, preferred_element_type=F32) + jnp.dot(lo, e2, preferred_element_type=F32)


def _lane_iota(shape):
    return lax.broadcasted_iota(jnp.int32, shape, 1)


def _to_kv_lanes(t, h):
    src_lo = (h % 2 == 0)
    dst_lo = (h // Q_PER_KV == 0)
    if src_lo != dst_lo:
        t = pltpu.roll(t, HEAD_DIM, 1)
    lane = _lane_iota(t.shape)
    keep = (lane < HEAD_DIM) if dst_lo else (lane >= HEAD_DIM)
    return jnp.where(keep, t, 0.0)


def _from_kv_lanes(t, h):
    src_lo = (h // Q_PER_KV == 0)
    dst_lo = (h % 2 == 0)
    lane = _lane_iota(t.shape)
    keep = (lane < HEAD_DIM) if src_lo else (lane >= HEAD_DIM)
    t = jnp.where(keep, t, 0.0)
    if src_lo != dst_lo:
        t = pltpu.roll(t, HEAD_DIM, 1)
    return t


def _rope(t, cos_t, sin_p, sin_m):
    half = ROPE_DIM // 2
    return t * cos_t + pltpu.roll(t, half, 1) * sin_p + pltpu.roll(t, LANES - half, 1) * sin_m


def _rope_t(d, cos_t, sin_p, sin_m):
    half = ROPE_DIM // 2
    return d * cos_t + pltpu.roll(d * sin_p, LANES - half, 1) + pltpu.roll(d * sin_m, half, 1)


def _band(ref, n, nb):
    prev = jnp.maximum(n - 1, 0)
    nxt = jnp.minimum(n + 1, nb - 1)
    rows = [ref[pl.ds(pl.multiple_of(j * BLK, BLK), BLK), :] for j in (prev, n, nxt)]
    return jnp.concatenate(rows, axis=0)


def _band_bias(n, seq):
    qi = lax.broadcasted_iota(jnp.int32, (BLK, 3 * BLK), 0)
    kj = lax.broadcasted_iota(jnp.int32, (BLK, 3 * BLK), 1)
    k_abs = n * BLK - BLK + kj
    valid = (jnp.abs(kj - BLK - qi) <= BLK) & (k_abs >= 0) & (k_abs < seq)
    bias = jnp.where(valid, 0.0, NEG_INF)
    return jnp.concatenate([bias] * Q_PER_KV, axis=0)


def _stack_heads(tile_of, kv):
    return jnp.concatenate([_to_kv_lanes(tile_of(h // 2), h) for h in range(Q_PER_KV * kv, Q_PER_KV * (kv + 1))], axis=0)


def _per_head_column(vals):
    row = lax.broadcasted_iota(jnp.int32, (Q_PER_KV * BLK, 1), 0)
    return jnp.where(row < BLK, vals[0], jnp.where(row < 2 * BLK, vals[1], jnp.where(row < 3 * BLK, vals[2], vals[3])))


def _softplus_neg(lam):
    e = jnp.exp(-jnp.abs(lam))
    u = 1.0 + e
    log1p_e = jnp.where(u == 1.0, e, jnp.log(u) * (e / (u - 1.0)))
    sp = jnp.maximum(-lam, 0.0) + log1p_e
    dsp = -1.0 / (1.0 + jnp.exp(lam))
    return sp, dsp


def _full(shape):
    return pl.BlockSpec(shape, lambda *_: (0,) * len(shape))


def _const(shape):
    return pl.BlockSpec(shape, lambda *_: (0,) * len(shape), pipeline_mode=pl.Buffered(1))


def _rows(tm, n):
    return pl.BlockSpec((tm, n), lambda i: (i, 0))


def _params(*sem):
    return pltpu.CompilerParams(dimension_semantics=sem, vmem_limit_bytes=VMEM_LIMIT)


def _sds(shape, dtype=F32):
    return jax.ShapeDtypeStruct(shape, dtype)


def _place():
    return lax.axis_index("x"), lax.axis_index("y"), lax.axis_index("c")


class _Rider:
    def __init__(self, ins, out_shapes, n_remote, n_local, plan):
        self.ins, self.out_shapes, self.n_remote, self.n_local, self.plan = list(ins), list(out_shapes), n_remote, n_local, plan

    def scratch(self):
        return [pltpu.SemaphoreType.DMA((self.n_remote,)), pltpu.SemaphoreType.DMA((self.n_remote,)),
                pltpu.SemaphoreType.DMA((max(self.n_local, 1),))]

    def run(self, first, in_refs, out_refs, sems):
        send_sems, recv_sems, local_sems = sems
        sends, recvs, locals_ = self.plan(in_refs, out_refs)
        remote = lambda k, src, dst, to: pltpu.make_async_remote_copy(
            src_ref=src, dst_ref=dst, send_sem=send_sems.at[k], recv_sem=recv_sems.at[k], device_id=to, device_id_type=MESH)
        if first:
            for k, src, dst, to in sends:
                remote(k, src, dst, to).start()
            for j, (src, dst) in enumerate(locals_):
                pltpu.make_async_copy(src, dst, local_sems.at[j]).start()
        else:
            for k, dst, frm in recvs:
                remote(k, dst, dst, frm).wait_recv()
            for k, src, dst, to in sends:
                remote(k, src, dst, to).wait_send()
            for j, (src, dst) in enumerate(locals_):
                pltpu.make_async_copy(src, dst, local_sems.at[j]).wait()


def _other_chips(x, y):
    return [(1 - x, y), (x, 1 - y), (1 - x, 1 - y)]


def _gather_rider(shard):
    hr = shard.shape[0] // 2

    def plan(ins, outs):
        x, y, c = _place()
        mine, src, dst = 2 * x + y, ins[0], outs[0]
        sends, recvs = [], []
        for j, (px, py) in enumerate(_other_chips(x, y)):
            for flip in range(2):
                tc = c if flip == 0 else 1 - c
                sends.append((2 * j + flip, src.at[pl.ds(c * hr, hr)], dst.at[mine, pl.ds(c * hr, hr)], (px, py, tc)))
                recvs.append((2 * j + flip, dst.at[2 * px + py, pl.ds(tc * hr, hr)], (px, py, tc)))
        return sends, recvs, [(src, dst.at[mine])]

    return _Rider([shard], [_sds((4,) + shard.shape, shard.dtype)], 6, 1, plan)


def _sibling_swap_rider(parts):
    n = len(parts)

    def plan(ins, outs):
        x, y, c = _place()
        sibling = (x, y, 1 - c)
        return ([(i, ins[i].at[:, 1 - c], outs[i], sibling) for i in range(n)], [(i, outs[i], sibling) for i in range(n)], [])

    return _Rider(parts, [_sds((4,) + p.shape[2:], p.dtype) for p in parts], n, 0, plan)


def _chip_exchange_rider(parts):
    n = len(parts)

    def plan(ins, outs):
        x, y, c = _place()
        mine = 2 * x + y
        sends, recvs = [], []
        for i in range(n):
            for j, (px, py) in enumerate(_other_chips(x, y)):
                sends.append((3 * i + j, ins[i].at[2 * px + py], outs[i].at[mine], (px, py, c)))
                recvs.append((3 * i + j, outs[i].at[2 * px + py], (px, py, c)))
        return sends, recvs, [(ins[i].at[mine], outs[i].at[mine]) for i in range(n)]

    return _Rider(parts, [_sds(p.shape, p.dtype) for p in parts], 3 * n, n, plan)


def _call(body, name, grid, in_specs, out_specs, out_shape, args, sem, scratch=(), rider=None):
    if rider is None:
        return list(pl.pallas_call(body, name=name, grid=grid, in_specs=in_specs, out_specs=out_specs, out_shape=out_shape,
                                   scratch_shapes=list(scratch), compiler_params=_params(sem))(*args)), []
    n_in, n_out, n_scr = len(in_specs), len(out_specs), len(scratch)
    r_in, r_out = len(rider.ins), len(rider.out_shapes)
    steps = grid[0]

    def riding(*refs):
        ins, r_ins = refs[:n_in], refs[n_in:n_in + r_in]
        outs = refs[n_in + r_in:n_in + r_in + n_out]
        r_outs = refs[n_in + r_in + n_out:n_in + r_in + n_out + r_out]
        scr = refs[n_in + r_in + n_out + r_out:n_in + r_in + n_out + r_out + n_scr]
        sems = refs[n_in + r_in + n_out + r_out + n_scr:]

        @pl.when(pl.program_id(0) == 0)
        def _():
            rider.run(True, r_ins, r_outs, sems)

        body(*ins, *outs, *scr)

        @pl.when(pl.program_id(0) == steps - 1)
        def _():
            rider.run(False, r_ins, r_outs, sems)

    hbm = pl.BlockSpec(memory_space=pl.ANY)
    res = pl.pallas_call(
        riding, name=name, grid=grid, in_specs=list(in_specs) + [hbm] * r_in, out_specs=list(out_specs) + [hbm] * r_out,
        out_shape=list(out_shape) + rider.out_shapes, scratch_shapes=list(scratch) + rider.scratch(),
        compiler_params=_params("arbitrary"),
    )(*args, *rider.ins)
    return list(res[:n_out]), list(res[n_out:])


def _row_tile(seq, want):
    return want if seq % want == 0 else seq


def _rope_tables(posf, seq):
    half = ROPE_DIM // 2
    inv_freq = np.power(np.float32(ROPE_THETA), -np.arange(half, dtype=np.float32) / np.float32(half)).astype(np.float32)
    j = np.arange(LANES) % HEAD_DIM
    invf = jnp.asarray(np.where(j < ROPE_DIM, inv_freq[j % half], 0.0).astype(np.float32).reshape(1, LANES))
    m_p = jnp.asarray(((j >= half) & (j < ROPE_DIM)).astype(np.float32).reshape(1, LANES))
    m_m = jnp.asarray(-(j < half).astype(np.float32).reshape(1, LANES))
    tm = _row_tile(seq, 512)

    def body(pos_ref, invf_ref, mp_ref, mm_ref, cos_ref, sp_ref, sm_ref):
        ang = pos_ref[...] * invf_ref[...]
        s = jnp.sin(ang)
        cos_ref[...] = jnp.cos(ang)
        sp_ref[...] = s * mp_ref[...]
        sm_ref[...] = s * mm_ref[...]

    return pl.pallas_call(
        body, name="rope_tables", grid=(seq // tm,),
        in_specs=[_rows(tm, 1), _full((1, LANES)), _full((1, LANES)), _full((1, LANES))],
        out_specs=[_rows(tm, LANES)] * 3, out_shape=[_sds((seq, LANES))] * 3,
        compiler_params=_params("parallel"),
    )(posf, invf, m_p, m_m)


def _even_proj(x, mod, w_in_t, tabs, seq, rider=None):
    tm = _row_tile(seq, 512)

    def body(x_ref, mod_ref, w_ref, cos_ref, sp_ref, sm_ref, q_ref, k_ref, v_ref, su_ref, sv_ref, g_ref):
        h = x_ref[...] * (1.0 + mod_ref[1:2, :]) + mod_ref[0:1, :]
        p = _mm_nt(h, w_ref[...])
        cos_t, sin_p, sin_m = cos_ref[...], sp_ref[...], sm_ref[...]
        for j in range(ATTN_WIDTH // LANES):
            q_ref[:, j * LANES:(j + 1) * LANES] = _rope(p[:, j * LANES:(j + 1) * LANES], cos_t, sin_p, sin_m).astype(q_ref.dtype)
        k_ref[...] = _rope(p[:, 512:640], cos_t, sin_p, sin_m).astype(k_ref.dtype)
        v_ref[...] = p[:, 640:768].astype(v_ref.dtype)
        su_ref[...] = p[:, 768:1280].astype(su_ref.dtype)
        sv_ref[...] = p[:, 1280:1792].astype(sv_ref.dtype)
        g_ref[...] = p[:, 1792:2816].astype(g_ref.dtype)

    return _call(
        body, "even_proj", (seq // tm,),
        [_rows(tm, D_MODEL), _full((3, D_MODEL)), _const((EVEN_IN, D_MODEL))] + [_rows(tm, LANES)] * 3,
        [_rows(tm, 512), _rows(tm, LANES), _rows(tm, LANES), _rows(tm, 512), _rows(tm, 512), _rows(tm, D_MODEL)],
        [_sds((seq, 512), MXU_DTYPE), _sds((seq, LANES), MXU_DTYPE), _sds((seq, LANES), MXU_DTYPE), _sds((seq, 512), ACT_DTYPE),
         _sds((seq, 512), ACT_DTYPE), _sds((seq, D_MODEL), ACT_DTYPE)],
        (x, mod, w_in_t, *tabs), "parallel", rider=rider)


def _sg_forward(sv, lng, lnb, sgw_ref, sgb, e2):
    vn, vhat, rstd, svo = [], [], [], []
    for j in range(SG_WIDTH // LANES):
        t = sv[:, j * LANES:(j + 1) * LANES]
        mu = _group_sum(t, e2) * (1.0 / HEAD_DIM)
        d = t - mu
        var = _group_sum(d * d, e2) * (1.0 / HEAD_DIM)
        r = lax.rsqrt(var + LN_EPS)
        vh = d * r
        vhat.append(vh)
        rstd.append(r)
        vn.append(vh * lng[:, j * LANES:(j + 1) * LANES] + lnb[:, j * LANES:(j + 1) * LANES])
    lane = _lane_iota((BLK, LANES))
    for j in range(SG_WIDTH // LANES):
        lo = _mm(sgw_ref[2 * j], vn[j])
        hi = _mm(sgw_ref[2 * j + 1], vn[j])
        svo.append(jnp.where(lane < HEAD_DIM, lo, hi) + sgb[:, j * LANES:(j + 1) * LANES])
    return svo, vn, vhat, rstd


def _even_mix(q, k, v, su, sv, sink, sgln_g, sgln_b, sgw, sgb_full, e2, seq, rider=None):
    nb = seq // BLK

    def body(sink_ref, q_ref, k_ref, v_ref, su_ref, sv_ref, lng_ref, lnb_ref, sgw_ref, sgb_ref, e2_ref, ycat_ref, lse_ref):
        n = pl.program_id(0)
        kband = _band(k_ref, n, nb)
        vband = _band(v_ref, n, nb)
        bias = _band_bias(n, seq)
        lane = _lane_iota((BLK, LANES))
        lse = jnp.zeros((BLK, LANES), F32)
        q_tile = lambda j: q_ref[:, j * LANES:(j + 1) * LANES].astype(F32)
        acc = [jnp.zeros((BLK, LANES), F32) for _ in range(ATTN_WIDTH // LANES)]
        for kv in range(N_Q_HEADS // Q_PER_KV):
            heads = range(Q_PER_KV * kv, Q_PER_KV * (kv + 1))
            sink = _per_head_column([sink_ref[h] for h in heads])
            s = _mm_nt(_stack_heads(q_tile, kv), kband) * (HEAD_DIM ** -0.5) + bias
            m = jnp.maximum(jnp.max(s, axis=1, keepdims=True), sink)
            p = jnp.exp(s - m)
            denom = jnp.sum(p, axis=1, keepdims=True) + jnp.exp(sink - m)
            o4 = _mm(p / denom, vband)
            l4 = m + jnp.log(denom)
            for g, h in enumerate(heads):
                acc[h // 2] = acc[h // 2] + _from_kv_lanes(o4[g * BLK:(g + 1) * BLK], h)
                lse = jnp.where(lane == h, l4[g * BLK:(g + 1) * BLK], lse)
        for j in range(ATTN_WIDTH // LANES):
            ycat_ref[:, j * LANES:(j + 1) * LANES] = acc[j].astype(ycat_ref.dtype)
        lse_ref[...] = lse
        svo, _, _, _ = _sg_forward(sv_ref[...].astype(F32), lng_ref[...], lnb_ref[...], sgw_ref, sgb_ref[...], e2_ref[...])
        for j in range(SG_WIDTH // LANES):
            ysg = su_ref[:, j * LANES:(j + 1) * LANES].astype(F32) * svo[j]
            ycat_ref[:, ATTN_WIDTH + j * LANES:ATTN_WIDTH + (j + 1) * LANES] = ysg.astype(ycat_ref.dtype)

    blk = lambda w: pl.BlockSpec((BLK, w), lambda n: (n, 0))
    return _call(
        body, "even_mix", (nb,),
        [pl.BlockSpec(memory_space=pltpu.SMEM), blk(512), _full((seq, LANES)), _full((seq, LANES)), blk(512), blk(512),
         _full((1, 512)), _full((1, 512)), _full((8, BLK, BLK)), _full((BLK, 512)), _full((LANES, LANES))],
        [blk(D_MODEL), blk(LANES)], [_sds((seq, D_MODEL), ACT_DTYPE), _sds((seq, LANES))],
        (sink, q, k, v, su, sv, sgln_g, sgln_b, sgw, sgb_full, e2), "parallel", rider=rider)


def _even_out(ycat, g, x, mod, mod_next, w_out, w_in4_next, ln_g, ln_b, seq, rider=None):
    tm = _row_tile(seq, 512)
    cs = ODD_IN // 4

    def body(y_ref, g_ref, x_ref, mod_ref, modn_ref, wo_ref, wi_ref, g1_ref, b1_ref, z_ref, x1_ref, xr_ref, gn_ref):
        gg = g_ref[...].astype(F32)
        out = _mm(y_ref[...].astype(F32) * (gg * _sigmoid(gg)), wo_ref[...])
        z = ALPHA * x_ref[...] + mod_ref[2:3, :] * out
        z_ref[...] = z
        zhat, _ = _ln_stats(z)
        x1 = zhat * g1_ref[...] + b1_ref[...]
        x1_ref[...] = x1
        hb = (x1 * (1.0 + modn_ref[1:2, :]) + modn_ref[0:1, :]).astype(MXU_DTYPE)
        for s in range(2):
            xr_ref[:, s * cs:(s + 1) * cs] = jnp.dot(hb, wi_ref[s], preferred_element_type=F32)
            gn_ref[:, s * cs:(s + 1) * cs] = jnp.dot(hb, wi_ref[2 + s], preferred_element_type=F32).astype(gn_ref.dtype)

    return _call(
        body, "even_out", (seq // tm,),
        [_rows(tm, D_MODEL)] * 3 + [_full((3, D_MODEL)), _full((3, D_MODEL)), _const((D_MODEL, D_MODEL)), _const((4, D_MODEL, cs)),
                                    _full((1, D_MODEL)), _full((1, D_MODEL))],
        [_rows(tm, D_MODEL)] * 4, [_sds((seq, D_MODEL))] * 3 + [_sds((seq, D_MODEL), ACT_DTYPE)],
        (ycat, g, x, mod, mod_next, w_out, w_in4_next, ln_g, ln_b), "parallel", rider=rider)


def _halo_specs(tm, seq, width, order=lambda i: i):
    per = tm // 8
    last = seq // 8 - 1
    return [pl.BlockSpec((8, width), lambda i: (jnp.maximum(order(i) * per - 1, 0), 0)),
            pl.BlockSpec((tm, width), lambda i: (order(i), 0)),
            pl.BlockSpec((8, width), lambda i: (jnp.minimum((order(i) + 1) * per, last), 0))]


def _extended(prev_ref, main_ref, next_ref, i, n_steps):
    prev = jnp.where(i > 0, prev_ref[...], 0.0)
    nxt = jnp.where(i < n_steps - 1, next_ref[...], 0.0)
    return jnp.concatenate([prev, main_ref[...], nxt], axis=0)


def _shifted(ext, off, tm):
    if off == 0:
        return ext[8:8 + tm]
    return pltpu.roll(ext, (-off) % ext.shape[0], 0)[8:8 + tm]


SCAN_SUB = 8


def _lru_gate(xh, pre, bias, sp, hs, d):
    r = _sigmoid(pre[:, 0:LANES] + bias[2 * d:2 * d + 1, hs])
    ig = _sigmoid(pre[:, LANES:2 * LANES] + bias[2 * d + 1:2 * d + 2, hs])
    neg_log_a = RG_LRU_C * r * sp[d:d + 1, hs]
    a = jnp.exp(-neg_log_a)
    u = jnp.tanh(neg_log_a) * (a * a + 1.0)
    inv_s = lax.rsqrt(jnp.maximum(u, jnp.finfo(F32).tiny))
    return r, ig, a, u * inv_s, inv_s


def _conv_block(xp_ref, xm_ref, xn_ref, cw_ref, cb_ref, blk, steps, tm):
    ext = _extended(xp_ref, xm_ref, xn_ref, blk, steps)
    return cb_ref[...] + sum(cw_ref[kk:kk + 1, :] * _shifted(ext, kk - 2, tm) for kk in range(4))


def _scan_tiles(a_ref, b_ref, h_ref, hprev_ref, carry_h, carry_a, rows, descending, post):
    sub = SCAN_SUB
    tiles = rows // sub
    row = lax.broadcasted_iota(jnp.int32, (sub, D_MODEL), 0)

    def shift(v, d, fill):
        if descending:
            return jnp.where(row <= sub - 1 - d, pltpu.roll(v, sub - d, 0), fill)
        return jnp.where(row >= d, pltpu.roll(v, d, 0), fill)

    def last(v):
        return jnp.broadcast_to(v[0:1, :] if descending else v[sub - 1:sub, :], v.shape)

    def tile(j, c):
        ch, ca = c
        r0 = pl.multiple_of(((tiles - 1 - j) if descending else j) * sub, sub)
        at = a_ref[pl.ds(r0, sub), :]
        bt = b_ref[pl.ds(r0, sub), :]
        coef = shift(at, 1, ca) if post else at
        acc_a, acc_b = coef, bt
        for d in (1, 2, 4):
            acc_b = acc_b + acc_a * shift(acc_b, d, 0.0)
            acc_a = acc_a * shift(acc_a, d, 1.0)
        h = acc_b + acc_a * ch
        h_ref[pl.ds(r0, sub), :] = h
        if post:
            return last(h), last(at)
        hprev_ref[pl.ds(r0, sub), :] = shift(h, 1, ch)
        return last(h), ca

    ch, ca = lax.fori_loop(0, tiles, tile, (carry_h[...], carry_a[...]), unroll=4)
    carry_h[...] = ch
    carry_a[...] = ca


def _lru_fwd(xr, xc, conv_w, conv_b, wcat, bias, lam, seq, d):
    tb = _row_tile(seq, 512)
    steps = seq // tb
    descending = d == 1
    order = (lambda i: steps - 1 - i) if descending else (lambda i: i)
    with_conv = xc is None
    n_x = 5 if with_conv else 1

    def body(*refs):
        x_refs, (w_ref, bias_ref, lam_ref) = refs[:n_x], refs[n_x:n_x + 3]
        h_ref, hp_ref, a_ref, r_ref, i_ref, s_ref, q_ref = refs[n_x + 3:n_x + 10]
        b_scr, carry_h, carry_a = refs[-3:]
        i = pl.program_id(0)

        @pl.when(i == 0)
        def _():
            carry_h[...] = jnp.zeros_like(carry_h)
            carry_a[...] = jnp.zeros_like(carry_a)

        if with_conv:
            xc_ref = refs[n_x + 10]
            xc_ref[...] = _conv_block(*x_refs, order(i), steps, tb)
        else:
            xc_ref = x_refs[0]
        sp, _ = _softplus_neg(lam_ref[...])
        bias = bias_ref[...]
        for h in range(RNN_HEADS):
            hs = slice(h * LANES, (h + 1) * LANES)
            xh = xc_ref[:, hs]
            r, ig, a, s, q = _lru_gate(xh, _mm(xh, w_ref[h, :, 2 * d * LANES:2 * (d + 1) * LANES]), bias, sp, hs, d)
            a_ref[:, hs] = a
            b_scr[:, hs] = s * ig * xh
            for ref, val in ((r_ref, r), (i_ref, ig), (s_ref, s), (q_ref, q)):
                ref[:, hs] = val.astype(ref.dtype)
        _scan_tiles(a_ref, b_scr, h_ref, hp_ref, carry_h, carry_a, tb, descending, post=False)

    row_spec = pl.BlockSpec((tb, D_MODEL), lambda i: (order(i), 0))
    if with_conv:
        x_specs, x_args = _halo_specs(tb, seq, D_MODEL, order) + [_full((4, D_MODEL)), _full((1, D_MODEL))], (xr, xr, xr, conv_w, conv_b)
    else:
        x_specs, x_args = [row_spec], (xc,)
    n_out = 8 if with_conv else 7
    return pl.pallas_call(
        body, name="lru_fwd_%d" % d, grid=(steps,),
        in_specs=x_specs + [_full((8, LANES, 512)), _full((4, D_MODEL)), _full((2, D_MODEL))],
        out_specs=[row_spec] * n_out,
        out_shape=[_sds((seq, D_MODEL))] * 3 + [_sds((seq, D_MODEL), ACT_DTYPE)] * 4 + [_sds((seq, D_MODEL))] * (n_out - 7),
        scratch_shapes=[pltpu.VMEM((tb, D_MODEL), F32)] + [pltpu.VMEM((SCAN_SUB, D_MODEL), F32)] * 2,
        compiler_params=_params("arbitrary"),
    )(*x_args, wcat, bias, lam)


def _odd_out_and_loss(hf, hr, g, x1, tgt, mod, w_out, ln_g, ln_b, seq):
    tm = _row_tile(seq, 512)

    def body(hf_ref, hr_ref, g_ref, x_ref, t_ref, mod_ref, w_ref, lg_ref, lb_ref,
             dhs_ref, dg_ref, dres_ref, loss_ref, dw_ref, vec_ref):
        @pl.when(pl.program_id(0) == 0)
        def _():
            loss_ref[...] = jnp.zeros_like(loss_ref)
            dw_ref[...] = jnp.zeros_like(dw_ref)
            vec_ref[...] = jnp.zeros_like(vec_ref)

        gg = g_ref[...].astype(F32)
        sg = _sigmoid(gg)
        silu = gg * sg
        hsum = hf_ref[...] + hr_ref[...]
        y = hsum * silu
        out = _mm(y, w_ref[...])
        gate = mod_ref[2:3, :]
        z = ALPHA * x_ref[...] + gate * out
        zhat, rstd = _ln_stats(z)
        x2 = zhat * lg_ref[...] + lb_ref[...]
        err = x2 - t_ref[...]
        loss_ref[...] += 0.5 * jnp.sum(jnp.mean(err * err, axis=-1, keepdims=True))
        dx2 = err * (1.0 / D_MODEL)
        dz = _ln_bwd(dx2, zhat, rstd, lg_ref[...])
        vec_ref[0:1, :] += jnp.sum(dx2 * zhat, axis=0, keepdims=True)
        vec_ref[1:2, :] += jnp.sum(dx2, axis=0, keepdims=True)
        vec_ref[2:3, :] += jnp.sum(dz * out, axis=0, keepdims=True)
        dres_ref[...] = ALPHA * dz
        dout = gate * dz
        dw_ref[...] += _mm_tn(y, dout)
        dy = _mm_nt(dout, w_ref[...])
        dhs_ref[...] = dy * silu
        dg_ref[...] = (dy * hsum * (sg * (1.0 + gg * (1.0 - sg)))).astype(dg_ref.dtype)

    return pl.pallas_call(
        body, name="odd_out_loss", grid=(seq // tm,),
        in_specs=[_rows(tm, D_MODEL)] * 5 + [_full((3, D_MODEL)), _const((D_MODEL, D_MODEL)),
                                             _full((1, D_MODEL)), _full((1, D_MODEL))],
        out_specs=[_rows(tm, D_MODEL)] * 3 + [_full((8, LANES)), _full((D_MODEL, D_MODEL)), _full((8, D_MODEL))],
        out_shape=[_sds((seq, D_MODEL)), _sds((seq, D_MODEL), ACT_DTYPE), _sds((seq, D_MODEL)), _sds((8, LANES)),
                   _sds((D_MODEL, D_MODEL)), _sds((8, D_MODEL))],
        compiler_params=_params("arbitrary"),
    )(hf, hr, g, x1, tgt, mod, w_out, ln_g, ln_b)


def _lru_bwd(xc, dhs, hprev, a_d, r_d, i_d, s_d, q_d, wcat, lam, seq, d):
    tb = _row_tile(seq, 512)
    steps = seq // tb
    descending = d == 0
    order = (lambda i: steps - 1 - i) if descending else (lambda i: i)
    cols = slice(2 * d * LANES, 2 * (d + 1) * LANES)

    def body(xc_ref, dhs_ref, hp_ref, a_ref, r_ref, i_ref, s_ref, q_ref, w_ref, lam_ref, dxc_ref, dw_ref, vec_ref,
             g_scr, carry_h, carry_a):
        i = pl.program_id(0)

        @pl.when(i == 0)
        def _():
            dw_ref[...] = jnp.zeros_like(dw_ref)
            vec_ref[...] = jnp.zeros_like(vec_ref)
            carry_h[...] = jnp.zeros_like(carry_h)
            carry_a[...] = jnp.zeros_like(carry_a)

        sp, dsp = _softplus_neg(lam_ref[...])
        _scan_tiles(a_ref, dhs_ref, g_scr, None, carry_h, carry_a, tb, descending, post=True)
        for h in range(RNN_HEADS):
            hs = slice(h * LANES, (h + 1) * LANES)
            xh, a = xc_ref[:, hs], a_ref[:, hs]
            r, ig, s = r_ref[:, hs].astype(F32), i_ref[:, hs].astype(F32), s_ref[:, hs].astype(F32)
            db = g_scr[:, hs]
            da = db * hp_ref[:, hs]
            dlog_a = da * a - (db * ig * xh) * (a * a * q_ref[:, hs].astype(F32))
            dpr = dlog_a * (-RG_LRU_C) * sp[d:d + 1, hs] * r * (1.0 - r)
            dpi = db * s * xh * ig * (1.0 - ig)
            vec_ref[0:1, hs] += jnp.sum(dpr, axis=0, keepdims=True)
            vec_ref[1:2, hs] += jnp.sum(dpi, axis=0, keepdims=True)
            vec_ref[2:3, hs] += jnp.sum(dlog_a * r, axis=0, keepdims=True) * (-RG_LRU_C) * dsp[d:d + 1, hs]
            dcat = jnp.concatenate([dpr, dpi], axis=1)
            dw_ref[h] += _mm_tn(xh, dcat)
            dxc_ref[:, hs] = db * s * ig + _mm_nt(dcat, w_ref[h, :, cols])

    row_spec = pl.BlockSpec((tb, D_MODEL), lambda i: (order(i), 0))
    return pl.pallas_call(
        body, name="lru_bwd_%d" % d, grid=(steps,),
        in_specs=[row_spec] * 8 + [_full((8, LANES, 512)), _full((2, D_MODEL))],
        out_specs=[row_spec, _full((8, LANES, 2 * LANES)), _full((8, D_MODEL))],
        out_shape=[_sds((seq, D_MODEL)), _sds((8, LANES, 2 * LANES)), _sds((8, D_MODEL))],
        scratch_shapes=[pltpu.VMEM((tb, D_MODEL), F32)] + [pltpu.VMEM((SCAN_SUB, D_MODEL), F32)] * 2,
        compiler_params=_params("arbitrary"),
    )(xc, dhs, hprev, a_d, r_d, i_d, s_d, q_d, wcat, lam)


def _odd_proj_bwd(dxc_f, dxc_r, xr, dg, x1, dres, mod, conv_w, w_in4, seq):
    tm = _row_tile(seq, 512)
    steps = seq // tm

    def body(fp_ref, fm_ref, fn_ref, rp_ref, rm_ref, rn_ref, xp_ref, xm_ref, xn_ref, dg_ref, x_ref, dres_ref, mod_ref, cw_ref,
             w_ref, dx_ref, dw_ref, vec_ref, dpb_ref):
        i = pl.program_id(0)

        @pl.when(i == 0)
        def _():
            vec_ref[...] = jnp.zeros_like(vec_ref)
            dw_ref[...] = jnp.zeros_like(dw_ref)

        dext = _extended(fp_ref, fm_ref, fn_ref, i, steps) + _extended(rp_ref, rm_ref, rn_ref, i, steps)
        xext = _extended(xp_ref, xm_ref, xn_ref, i, steps)
        dxc_m = fm_ref[...] + rm_ref[...]
        dxr = sum(cw_ref[kk:kk + 1, :] * _shifted(dext, 2 - kk, tm) for kk in range(4))
        for kk in range(4):
            vec_ref[kk:kk + 1, :] += jnp.sum(dxc_m * _shifted(xext, kk - 2, tm), axis=0, keepdims=True)
        vec_ref[4:5, :] += jnp.sum(dxc_m, axis=0, keepdims=True)
        dpb_ref[:, :D_MODEL] = dxr.astype(dpb_ref.dtype)
        dpb_ref[:, D_MODEL:] = dg_ref[...].astype(dpb_ref.dtype)
        cs = ODD_IN // 4
        dh = sum(_mm_nt(dpb_ref[:, s * cs:(s + 1) * cs], w_ref[s]) for s in range(4))
        x = x_ref[...]
        h_t = (x * (1.0 + mod_ref[1:2, :]) + mod_ref[0:1, :]).T.astype(MXU_DTYPE)
        for s in range(4):
            dw_ref[s] += jnp.dot(h_t, dpb_ref[:, s * cs:(s + 1) * cs], preferred_element_type=F32)
        vec_ref[5:6, :] += jnp.sum(dh, axis=0, keepdims=True)
        vec_ref[6:7, :] += jnp.sum(dh * x, axis=0, keepdims=True)
        dx_ref[...] = dres_ref[...] + dh * (1.0 + mod_ref[1:2, :])

    return pl.pallas_call(
        body, name="odd_proj_bwd", grid=(steps,),
        in_specs=_halo_specs(tm, seq, D_MODEL) * 3 + [_rows(tm, D_MODEL)] * 3
        + [_full((3, D_MODEL)), _full((4, D_MODEL)), _const((4, D_MODEL, ODD_IN // 4))],
        out_specs=[_rows(tm, D_MODEL), _const((4, D_MODEL, ODD_IN // 4)), _full((8, D_MODEL))],
        out_shape=[_sds((seq, D_MODEL)), _sds((4, D_MODEL, ODD_IN // 4)), _sds((8, D_MODEL))],
        scratch_shapes=[pltpu.VMEM((tm, ODD_IN), MXU_DTYPE)],
        compiler_params=_params("arbitrary"),
    )(dxc_f, dxc_f, dxc_f, dxc_r, dxc_r, dxc_r, xr, xr, xr, dg, x1, dres, mod, conv_w, w_in4)


def _even_out_bwd(dx1, z, ycat, g, mod, ln_g, w_out, seq, rider=None):
    tm = _row_tile(seq, 512)
    steps = seq // tm

    def body(dx_ref, z_ref, y_ref, g_ref, mod_ref, lg_ref, w_ref, dy_ref, dg_ref, dres_ref, dw_ref, vec_ref):
        i = pl.program_id(0)

        @pl.when(i == 0)
        def _():
            dw_ref[...] = jnp.zeros_like(dw_ref)
            vec_ref[...] = jnp.zeros_like(vec_ref)

        zhat, rstd = _ln_stats(z_ref[...])
        dx1_ = dx_ref[...]
        dz = _ln_bwd(dx1_, zhat, rstd, lg_ref[...])
        vec_ref[0:1, :] += jnp.sum(dx1_ * zhat, axis=0, keepdims=True)
        vec_ref[1:2, :] += jnp.sum(dx1_, axis=0, keepdims=True)
        dres_ref[...] = ALPHA * dz
        gate = mod_ref[2:3, :]
        gg = g_ref[...].astype(F32)
        sg = _sigmoid(gg)
        silu = gg * sg
        ycat_ = y_ref[...].astype(F32)
        dw_ref[...] += _mm_tn(ycat_ * silu, dz)
        dy = _mm_nt(gate * dz, w_ref[...])
        dy_ref[...] = (dy * silu).astype(dy_ref.dtype)
        dg_ref[...] = (dy * ycat_ * (sg * (1.0 + gg * (1.0 - sg)))).astype(dg_ref.dtype)

        @pl.when(i == steps - 1)
        def _():
            m_acc = dw_ref[...]
            vec_ref[2:3, :] = jnp.sum(w_ref[...].astype(F32) * m_acc, axis=0, keepdims=True)
            dw_ref[...] = m_acc * gate

    return _call(
        body, "even_out_bwd", (steps,),
        [_rows(tm, D_MODEL)] * 4 + [_full((3, D_MODEL)), _full((1, D_MODEL)), _const((D_MODEL, D_MODEL))],
        [_rows(tm, D_MODEL)] * 3 + [_full((D_MODEL, D_MODEL)), _full((8, D_MODEL))],
        [_sds((seq, D_MODEL), ACT_DTYPE), _sds((seq, D_MODEL), ACT_DTYPE), _sds((seq, D_MODEL)), _sds((D_MODEL, D_MODEL)),
         _sds((8, D_MODEL))],
        (dx1, z, ycat, g, mod, ln_g, w_out), "arbitrary", rider=rider)


def _even_mix_bwd(q, k, v, lse, ycat, dycat, su, sv, sink, sgln_g, sgln_b, sgw, sgb_full, e2, e8, seq, rider=None):
    nb = seq // BLK

    def body(sink_ref, q_ref, k_ref, v_ref, lse_ref, y_ref, dy_ref, su_ref, sv_ref, lng_ref, lnb_ref, sgw_ref, sgb_ref, e2_ref,
             e8_ref, dq_ref, dsu_ref, dsv_ref, dk_ref, dv_ref, dsgw_ref, dsgb_ref, vec_ref, dsink_ref, dsgb_acc):
        n = pl.program_id(0)

        @pl.when(n == 0)
        def _():
            dk_ref[...] = jnp.zeros_like(dk_ref)
            dv_ref[...] = jnp.zeros_like(dv_ref)
            dsgw_ref[...] = jnp.zeros_like(dsgw_ref)
            dsgb_acc[...] = jnp.zeros_like(dsgb_acc)
            vec_ref[...] = jnp.zeros_like(vec_ref)
            dsink_ref[...] = jnp.zeros_like(dsink_ref)

        kband = _band(k_ref, n, nb)
        vband = _band(v_ref, n, nb)
        bias = _band_bias(n, seq)
        lane = _lane_iota((BLK, LANES))
        row8 = lax.broadcasted_iota(jnp.int32, (8, LANES), 0)
        lse = lse_ref[...]
        dkb = jnp.zeros((LANES, 3 * BLK), F32)
        dvb = jnp.zeros((LANES, 3 * BLK), F32)
        dsink = jnp.zeros((8, LANES), F32)
        q_tile = lambda j: q_ref[:, j * LANES:(j + 1) * LANES].astype(F32)
        do_tile = lambda j: dy_ref[:, j * LANES:(j + 1) * LANES].astype(F32)
        dq = [jnp.zeros((BLK, LANES), F32) for _ in range(ATTN_WIDTH // LANES)]
        for kv in range(N_Q_HEADS // Q_PER_KV):
            heads = range(Q_PER_KV * kv, Q_PER_KV * (kv + 1))
            lse4, delta4 = [], []
            for h in heads:
                head_lanes = (lane < HEAD_DIM) if h % 2 == 0 else (lane >= HEAD_DIM)
                lse4.append(jnp.sum(jnp.where(lane == h, lse, 0.0), axis=1, keepdims=True))
                o_tile = y_ref[:, (h // 2) * LANES:(h // 2 + 1) * LANES].astype(F32)
                delta4.append(jnp.sum(jnp.where(head_lanes, do_tile(h // 2) * o_tile, 0.0), axis=1, keepdims=True))
            lse4, delta4 = jnp.concatenate(lse4, axis=0), jnp.concatenate(delta4, axis=0)
            q4, do4 = _stack_heads(q_tile, kv), _stack_heads(do_tile, kv)
            s = _mm_nt(q4, kband) * (HEAD_DIM ** -0.5) + bias
            p = jnp.exp(s - lse4)
            wsink = jnp.exp(_per_head_column([sink_ref[h] for h in heads]) - lse4) * delta4
            ds = p * (_mm_nt(do4, vband) - delta4) * (HEAD_DIM ** -0.5)
            dq4 = _mm(ds, kband)
            dkb = dkb + _mm_tn(q4, ds)
            dvb = dvb + _mm_tn(do4, p)
            for g, h in enumerate(heads):
                dq[h // 2] = dq[h // 2] + _from_kv_lanes(dq4[g * BLK:(g + 1) * BLK], h)
                dsink = dsink + jnp.where(row8 == h, -jnp.sum(wsink[g * BLK:(g + 1) * BLK]), 0.0)
        for j in range(ATTN_WIDTH // LANES):
            dq_ref[:, j * LANES:(j + 1) * LANES] = dq[j].astype(dq_ref.dtype)
        dsink_ref[...] += dsink
        prev = jnp.maximum(n - 1, 0)
        nxt = jnp.minimum(n + 1, nb - 1)
        for part, blk_i in enumerate((prev, n, nxt)):
            rows = pl.ds(pl.multiple_of(blk_i * BLK, BLK), BLK)
            dk_ref[rows, :] += dkb[:, part * BLK:(part + 1) * BLK].T
            dv_ref[rows, :] += dvb[:, part * BLK:(part + 1) * BLK].T

        e2 = e2_ref[...]
        lng = lng_ref[...]
        svo, vn, vhat, rstd = _sg_forward(sv_ref[...].astype(F32), lng, lnb_ref[...], sgw_ref, sgb_ref[...], e2)
        for j in range(SG_WIDTH // LANES):
            cs = slice(j * LANES, (j + 1) * LANES)
            dysg = dy_ref[:, ATTN_WIDTH + j * LANES:ATTN_WIDTH + (j + 1) * LANES].astype(F32)
            dsu_ref[:, cs] = (dysg * svo[j]).astype(dsu_ref.dtype)
            dsvo = dysg * su_ref[:, cs].astype(F32)
            dsgb_acc[:, cs] += dsvo
            d_lo = jnp.where(lane < HEAD_DIM, dsvo, 0.0)
            d_hi = dsvo - d_lo
            dsgw_ref[2 * j] += _mm_nt(d_lo, vn[j])
            dsgw_ref[2 * j + 1] += _mm_nt(d_hi, vn[j])
            dvn = _mm_tn(sgw_ref[2 * j], d_lo) + _mm_tn(sgw_ref[2 * j + 1], d_hi)
            vec_ref[0:1, cs] += jnp.sum(dvn * vhat[j], axis=0, keepdims=True)
            vec_ref[1:2, cs] += jnp.sum(dvn, axis=0, keepdims=True)
            dvh = dvn * lng[:, cs]
            m1 = _group_sum(dvh, e2) * (1.0 / HEAD_DIM)
            m2 = _group_sum(dvh * vhat[j], e2) * (1.0 / HEAD_DIM)
            dsv_ref[:, cs] = (rstd[j] * (dvh - m1 - vhat[j] * m2)).astype(dsv_ref.dtype)

        @pl.when(n == nb - 1)
        def _():
            rest = dsgb_acc[...]
            total = jnp.zeros((8, BLK), F32)
            for _ in range(3):
                part = rest.astype(MXU_DTYPE)
                total = total + lax.dot_general(e8_ref[...], part, (((1,), (1,)), ((), ())), preferred_element_type=F32)
                rest = rest - part.astype(F32)
            dsgb_ref[...] = total

    blk = lambda w: pl.BlockSpec((BLK, w), lambda n: (n, 0))
    return _call(
        body, "even_mix_bwd", (nb,),
        [pl.BlockSpec(memory_space=pltpu.SMEM), blk(512), _full((seq, LANES)), _full((seq, LANES)), blk(LANES),
         blk(D_MODEL), blk(D_MODEL), blk(512), blk(512), _full((1, 512)), _full((1, 512)), _full((8, BLK, BLK)),
         _full((BLK, 512)), _full((LANES, LANES)), _full((8, 512))],
        [blk(512), blk(512), blk(512), _full((seq, LANES)), _full((seq, LANES)), _full((8, BLK, BLK)),
         _full((8, BLK)), _full((8, 512)), _full((8, LANES))],
        [_sds((seq, 512), ACT_DTYPE), _sds((seq, 512), ACT_DTYPE), _sds((seq, 512), ACT_DTYPE), _sds((seq, LANES)), _sds((seq, LANES)),
         _sds((8, BLK, BLK)), _sds((8, BLK)), _sds((8, 512)), _sds((8, LANES))],
        (sink, q, k, v, lse, ycat, dycat, su, sv, sgln_g, sgln_b, sgw, sgb_full, e2, e8), "arbitrary",
        scratch=[pltpu.VMEM((BLK, 512), F32)], rider=rider)


def _even_proj_bwd(dq, dk, dv, dsu, dsv, dg, x, dres, mod, tabs, w_in_t, seq):
    tm = _row_tile(seq, 512)

    def body(dq_ref, dk_ref, dv_ref, dsu_ref, dsv_ref, dg_ref, x_ref, dres_ref, mod_ref, cos_ref, sp_ref, sm_ref, wt_ref,
             dx_ref, dw_ref, vec_ref, dpb_ref):
        @pl.when(pl.program_id(0) == 0)
        def _():
            vec_ref[...] = jnp.zeros_like(vec_ref)
            dw_ref[...] = jnp.zeros_like(dw_ref)

        cos_t, sin_p, sin_m = cos_ref[...], sp_ref[...], sm_ref[...]
        dt = dpb_ref.dtype
        for j in range(ATTN_WIDTH // LANES):
            cs = slice(j * LANES, (j + 1) * LANES)
            dpb_ref[:, cs] = _rope_t(dq_ref[:, cs].astype(F32), cos_t, sin_p, sin_m).astype(dt)
        dpb_ref[:, 512:640] = _rope_t(dk_ref[...], cos_t, sin_p, sin_m).astype(dt)
        dpb_ref[:, 640:768] = dv_ref[...].astype(dt)
        dpb_ref[:, 768:1280] = dsu_ref[...].astype(dt)
        dpb_ref[:, 1280:1792] = dsv_ref[...].astype(dt)
        dpb_ref[:, 1792:2816] = dg_ref[...].astype(dt)
        dpb = dpb_ref[...]
        dh = jnp.dot(dpb, wt_ref[...], preferred_element_type=F32)
        x_ = x_ref[...]
        hb = (x_ * (1.0 + mod_ref[1:2, :]) + mod_ref[0:1, :]).astype(MXU_DTYPE)
        dw_ref[...] += _mm_tn(dpb, hb)
        vec_ref[0:1, :] += jnp.sum(dh, axis=0, keepdims=True)
        vec_ref[1:2, :] += jnp.sum(dh * x_, axis=0, keepdims=True)
        dx_ref[...] = dres_ref[...] + dh * (1.0 + mod_ref[1:2, :])

    return pl.pallas_call(
        body, name="even_proj_bwd", grid=(seq // tm,),
        in_specs=[_rows(tm, 512), _rows(tm, LANES), _rows(tm, LANES), _rows(tm, 512), _rows(tm, 512), _rows(tm, D_MODEL),
                  _rows(tm, D_MODEL), _rows(tm, D_MODEL), _full((3, D_MODEL))] + [_rows(tm, LANES)] * 3
        + [_const((EVEN_IN, D_MODEL))],
        out_specs=[_rows(tm, D_MODEL), _const((EVEN_IN, D_MODEL)), _full((8, D_MODEL))],
        out_shape=[_sds((seq, D_MODEL)), _sds((EVEN_IN, D_MODEL)), _sds((8, D_MODEL))],
        scratch_shapes=[pltpu.VMEM((tm, EVEN_IN), MXU_DTYPE)],
        compiler_params=_params("arbitrary"),
    )(dq, dk, dv, dsu, dsv, dg, x, dres, mod, *tabs, w_in_t)


def _local_step(x, posf, tgt, mod, w, seq, ride=None):
    rid = lambda make, *a: None if ride is None else make(*a)
    mxu = lambda a: a.astype(MXU_DTYPE)
    row = lambda a: a.reshape(1, -1)
    tabs = _rope_tables(posf, seq)
    e2 = mxu(jnp.kron(jnp.eye(2, dtype=F32), jnp.ones((HEAD_DIM, HEAD_DIM), F32)))
    e8 = mxu(jnp.repeat(jnp.eye(N_SG_GROUPS, dtype=F32), HEAD_DIM, axis=1))
    sgw = mxu(w["ev_sg_w"])
    sgb_full = jnp.repeat(w["ev_sg_b"].T, HEAD_DIM, axis=1)
    sgln_g, sgln_b = row(w["ev_sg_ln_g"]), row(w["ev_sg_ln_b"])
    sink = w["ev_sink"].reshape(N_Q_HEADS)
    ev_w_in_t = mxu(w["ev_w_in_t"])
    if ride is None:
        ev_w_out, od_w_in, od_w_out = mxu(w["ev_w_out"]), mxu(w["od_w_in"]), mxu(w["od_w_out"])
    wcat = mxu(jnp.concatenate([w["od_w_a"][0], w["od_w_x"][0], w["od_w_a"][1], w["od_w_x"][1]], axis=2))
    gate_bias = jnp.stack([w["od_b_a"][0], w["od_b_x"][0], w["od_b_a"][1], w["od_b_x"][1]])
    conv_b = row(w["od_conv_b"])
    ln_g, ln_b = w["ln_g"], w["ln_b"]

    (q, k, v, su, sv, g0), got = _even_proj(x, mod[0], ev_w_in_t, tabs, seq, rid(_gather_rider, ride and ride["ev_w_out"]))
    if ride is not None:
        ev_w_out = got[0].reshape(D_MODEL, D_MODEL)
    (ycat, lse), got = _even_mix(q, k, v, su, sv, sink, sgln_g, sgln_b, sgw, sgb_full, e2, seq,
                                 rid(_gather_rider, ride and ride["od_w_in"]))
    if ride is not None:
        od_w_in = got[0]
    (z0, x1, xr, g1), got = _even_out(ycat, g0, x, mod[0], mod[1], ev_w_out, od_w_in, ln_g[0:1], ln_b[0:1], seq,
                                      rid(_gather_rider, ride and ride["od_w_out"]))
    if ride is not None:
        od_w_out = got[0].reshape(D_MODEL, D_MODEL)
    lru = (w["od_conv_w"], conv_b, wcat, gate_bias, w["od_lam"], seq)
    hf, hpf, *saved_f, xc = _lru_fwd(xr, None, *lru, 0)
    hr, hpr, *saved_r = _lru_fwd(xr, xc, *lru, 1)
    dhs, dg1, dres1, loss, d_od_w_out, vec_o = _odd_out_and_loss(hf, hr, g1, x1, tgt, mod[1], od_w_out, ln_g[1:2], ln_b[1:2], seq)
    dxc_f, dw_f, vec_f = _lru_bwd(xc, dhs, hpf, *saved_f, wcat, w["od_lam"], seq, 0)
    dxc_r, dw_r, vec_r = _lru_bwd(xc, dhs, hpr, *saved_r, wcat, w["od_lam"], seq, 1)
    dx1, d_od_w_in, vec_p = _odd_proj_bwd(dxc_f, dxc_r, xr, dg1, x1, dres1, mod[1], w["od_conv_w"], od_w_in, seq)
    d_od_w_a = jnp.stack([dw_f[:, :, 0:128], dw_r[:, :, 0:128]])
    d_od_w_x = jnp.stack([dw_f[:, :, 128:256], dw_r[:, :, 128:256]])
    od_parts = [d_od_w_in.reshape(4, 2, 512, 512), d_od_w_out.reshape(4, 2, 128, D_MODEL),
                d_od_w_a.reshape(4, 2, 2 * BLK, BLK), d_od_w_x.reshape(4, 2, 2 * BLK, BLK)]
    (dycat, dg0, dres0, d_ev_w_out, vec_e), got_od = _even_out_bwd(dx1, z0, ycat, g0, mod[0], ln_g[0:1], ev_w_out, seq,
                                                                   rid(_sibling_swap_rider, od_parts))
    if ride is not None:
        od_sums = _sum_sibling(ride["core"], od_parts, got_od, [ride["wire"]] * 4, "sum_sibling_od")
    (dq, dsu, dsv, dk, dv, d_sgw, d_sgb, vec_s, d_sink), od_slots = _even_mix_bwd(
        q, k, v, lse, ycat, dycat, su, sv, sink, sgln_g, sgln_b, sgw, sgb_full, e2, e8, seq,
        rid(_chip_exchange_rider, ride and od_sums))
    grad_x, d_ev_w_in_t, vec_x = _even_proj_bwd(dq, dk, dv, dsu, dsv, dg0, x, dres0, mod[0], tabs, ev_w_in_t, seq)

    dmod = jnp.stack([jnp.stack([vec_x[0], vec_x[1], vec_e[2]]), jnp.stack([vec_p[5], vec_p[6], vec_o[2]])])
    grads = {
        "ln_g": jnp.stack([vec_e[0], vec_o[0]]), "ln_b": jnp.stack([vec_e[1], vec_o[1]]),
        "ev_w_in_t": d_ev_w_in_t, "ev_w_out": d_ev_w_out, "ev_sink": d_sink[:, 0],
        "ev_sg_ln_g": vec_s[0], "ev_sg_ln_b": vec_s[1], "ev_sg_w": d_sgw,
        "ev_sg_b": d_sgb,
        "od_conv_w": vec_p[0:4], "od_conv_b": vec_p[4],
        "od_b_a": jnp.stack([vec_f[0], vec_r[0]]), "od_b_x": jnp.stack([vec_f[1], vec_r[1]]),
        "od_lam": jnp.stack([vec_f[2], vec_r[2]]),
    }
    if ride is None:
        grads.update({"od_w_in": d_od_w_in, "od_w_out": d_od_w_out, "od_w_a": d_od_w_a, "od_w_x": d_od_w_x})
    else:
        grads["od_slots"] = od_slots
    return loss[0, 0], grad_x, dmod, grads


def _allgather8(block, name):
    m_per, n = block.shape

    def body(x_ref, out_ref, send_sems, recv_sems, local_sem):
        x, y, c = _place()
        me, sibling = (x, y, c), (x, y, 1 - c)
        chips = [(1 - x, y), (x, 1 - y), (1 - x, 1 - y)]

        def rows(px, py, pc):
            return out_ref.at[pl.ds((4 * px + 2 * py + pc) * m_per, m_per), :]

        def copy(k, blk, to, src=None):
            return pltpu.make_async_remote_copy(src_ref=rows(*blk) if src is None else src, dst_ref=rows(*blk),
                                                send_sem=send_sems.at[k], recv_sem=recv_sems.at[k], device_id=to,
                                                device_id_type=MESH)

        mine = pltpu.make_async_copy(x_ref, rows(*me), local_sem)
        mine.start()
        first = [copy(0, me, sibling, src=x_ref)] + [copy(1 + j, me, (*chip, c), src=x_ref) for j, chip in enumerate(chips)]
        for cp in first:
            cp.start()
        passed = [copy(4 + j, (*chip, c), sibling) for j, chip in enumerate(chips)]
        for j, chip in enumerate(chips):
            copy(1 + j, (*chip, c), me).wait_recv()
            passed[j].start()
        copy(0, sibling, me).wait_recv()
        for j, chip in enumerate(chips):
            copy(4 + j, (*chip, 1 - c), me).wait_recv()
        for cp in first + passed:
            cp.wait_send()
        mine.wait()

    return pl.pallas_call(
        body, name=name, out_shape=_sds((8 * m_per, n), block.dtype),
        in_specs=[pl.BlockSpec(memory_space=pltpu.VMEM)], out_specs=pl.BlockSpec(memory_space=pltpu.VMEM),
        scratch_shapes=[pltpu.SemaphoreType.DMA((7,)), pltpu.SemaphoreType.DMA((7,)), pltpu.SemaphoreType.DMA],
        compiler_params=pltpu.CompilerParams(vmem_limit_bytes=VMEM_LIMIT),
    )(block)


class _Copies:
    def __init__(self, send_sems, recv_sems, local_sems, stages):
        self.send_sems, self.recv_sems, self.local_sems, self.stages = send_sems, recv_sems, local_sems, stages
        self.sent, self.staged, self.locals = [], [], []

    def remote(self, k, src, dst, to):
        return pltpu.make_async_remote_copy(src_ref=src, dst_ref=dst, send_sem=self.send_sems.at[k], recv_sem=self.recv_sems.at[k],
                                            device_id=to, device_id_type=MESH)

    def send(self, k, src, dst, to):
        cp = self.remote(k, src, dst, to)
        cp.start()
        self.sent.append(cp)

    def arrived(self, k, dst, frm):
        self.remote(k, dst, dst, frm).wait_recv()

    def local(self, src, dst):
        k = len(self.staged)
        cp = pltpu.make_async_copy(src, self.stages[k], self.local_sems.at[2 * k])
        cp.start()
        self.staged.append((cp, dst))

    def flush(self):
        for k in range(len(self.locals), len(self.staged)):
            cp, dst = self.staged[k]
            cp.wait()
            out = pltpu.make_async_copy(self.stages[k], dst, self.local_sems.at[2 * k + 1])
            out.start()
            self.locals.append(out)

    def drain(self):
        self.flush()
        for cp in self.sent:
            cp.wait_send()
        for cp in self.locals:
            cp.wait()


def _comm_call(body, name, ins, out_shapes, n_remote, stages):
    n_in, n_out = len(ins), len(out_shapes)

    def kern(*refs):
        in_refs, out_refs = refs[:n_in], refs[n_in:n_in + n_out]
        send_sems, recv_sems, local_sems = refs[n_in + n_out:n_in + n_out + 3]
        body(_Copies(send_sems, recv_sems, local_sems, refs[n_in + n_out + 3:]), in_refs, out_refs)

    hbm = pl.BlockSpec(memory_space=pl.ANY)
    return pl.pallas_call(
        kern, name=name, out_shape=out_shapes, in_specs=[hbm] * n_in, out_specs=[hbm] * n_out,
        scratch_shapes=[pltpu.SemaphoreType.DMA((n_remote,)), pltpu.SemaphoreType.DMA((n_remote,)),
                        pltpu.SemaphoreType.DMA((2 * len(stages),))] + [pltpu.VMEM(s, d) for s, d in stages],
        compiler_params=pltpu.CompilerParams(vmem_limit_bytes=VMEM_LIMIT),
    )(*ins)


def _gather_to_all(cps, pairs, me, sibling, other_chips, c, base):
    idx = lambda p: 4 * p[0] + 2 * p[1] + p[2]
    for i, (src, dst) in enumerate(pairs):
        cps.local(src, dst.at[idx(me)])
        cps.send(base + 7 * i, src, dst.at[idx(me)], sibling)
        for j, chip in enumerate(other_chips):
            cps.send(base + 7 * i + 1 + j, src, dst.at[idx(me)], (*chip, c))
    cps.flush()
    for j, chip in enumerate(other_chips):
        for i, (_, dst) in enumerate(pairs):
            got = dst.at[idx((*chip, c))]
            cps.arrived(base + 7 * i + 1 + j, got, (*chip, c))
            cps.send(base + 7 * i + 4 + j, got, got, sibling)
    for i, (_, dst) in enumerate(pairs):
        cps.arrived(base + 7 * i, dst.at[idx(sibling)], sibling)
        for j, chip in enumerate(other_chips):
            cps.arrived(base + 7 * i + 4 + j, dst.at[idx((*chip, 1 - c))], sibling)


def _gather_weights(shards, small):
    n = len(shards)

    def body(cps, ins, outs):
        x, y, c = _place()
        me, sibling, mine = (x, y, c), (x, y, 1 - c), 2 * x + y
        chips = [(1 - x, y), (x, 1 - y), (1 - x, 1 - y)]
        for i in range(n):
            cps.local(ins[i], outs[i].at[mine])
        for j, (px, py) in enumerate(chips):
            for i in range(n):
                hr = shards[i].shape[0] // 2
                rows = pl.ds(c * hr, hr)
                cps.send(6 * i + j, ins[i].at[rows], outs[i].at[mine, rows], (px, py, c))
        _gather_to_all(cps, [(ins[n], outs[n])], me, sibling, chips, c, 6 * n)
        for j, (px, py) in enumerate(chips):
            for i in range(n):
                hr = shards[i].shape[0] // 2
                got = outs[i].at[2 * px + py, pl.ds(c * hr, hr)]
                cps.arrived(6 * i + j, got, (px, py, c))
                cps.send(6 * i + 3 + j, got, got, sibling)
        for j, (px, py) in enumerate(chips):
            for i in range(n):
                hr = shards[i].shape[0] // 2
                cps.arrived(6 * i + 3 + j, outs[i].at[2 * px + py, pl.ds((1 - c) * hr, hr)], sibling)
        cps.drain()

    return _comm_call(body, "gather_weights", list(shards) + [small],
                      [_sds((4,) + s.shape, s.dtype) for s in shards] + [_sds((8,) + small.shape, small.dtype)], 6 * n + 7,
                      [(a.shape, a.dtype) for a in list(shards) + [small]])


def _reduce_sibling(parts, dmod_rows):
    n = len(parts)

    def body(cps, ins, outs):
        x, y, c = _place()
        me, sibling = (x, y, c), (x, y, 1 - c)
        chips = [(1 - x, y), (x, 1 - y), (1 - x, 1 - y)]
        for i in range(n):
            cps.send(i, ins[i].at[:, 1 - c], outs[i], sibling)
        _gather_to_all(cps, [(ins[n], outs[n])], me, sibling, chips, c, n)
        for i in range(n):
            cps.arrived(i, outs[i], sibling)
        cps.drain()

    return _comm_call(body, "reduce_sibling", list(parts) + [dmod_rows],
                      [_sds((4,) + p.shape[2:], p.dtype) for p in parts] + [_sds((8,) + dmod_rows.shape, dmod_rows.dtype)], n + 7,
                      [(dmod_rows.shape, dmod_rows.dtype)])


def _reduce_chips(parts):
    n = len(parts)

    def body(cps, ins, outs):
        x, y, c = _place()
        mine = 2 * x + y
        chips = [(1 - x, y), (x, 1 - y), (1 - x, 1 - y)]
        for i in range(n):
            cps.local(ins[i].at[mine], outs[i].at[mine])
        for j, (px, py) in enumerate(chips):
            for i in range(n):
                cps.send(3 * i + j, ins[i].at[2 * px + py], outs[i].at[mine], (px, py, c))
        cps.flush()
        for j, (px, py) in enumerate(chips):
            for i in range(n):
                cps.arrived(3 * i + j, outs[i].at[2 * px + py], (px, py, c))
        cps.drain()

    return _comm_call(body, "reduce_chips", list(parts), [_sds(p.shape, p.dtype) for p in parts], 3 * n,
                      [(p.shape[1:], p.dtype) for p in parts])


def _gather_reduced(shard_parts, repl_parts):
    ns, nr = len(shard_parts), len(repl_parts)

    def body(cps, ins, outs):
        x, y, c = _place()
        me, sibling = (x, y, c), (x, y, 1 - c)
        chips = [(1 - x, y), (x, 1 - y), (1 - x, 1 - y)]
        for i in range(ns):
            cps.local(ins[i], outs[i].at[c])
            cps.send(i, ins[i], outs[i].at[c], sibling)
        _gather_to_all(cps, [(ins[ns + i], outs[ns + i]) for i in range(nr)], me, sibling, chips, c, ns)
        for i in range(ns):
            cps.arrived(i, outs[i].at[1 - c], sibling)
        cps.drain()

    return _comm_call(body, "gather_reduced", list(shard_parts) + list(repl_parts),
                      [_sds((2,) + p.shape, p.dtype) for p in shard_parts] + [_sds((8,) + p.shape, p.dtype) for p in repl_parts],
                      ns + 7 * nr, [(p.shape, p.dtype) for p in list(shard_parts) + list(repl_parts)])


def _sum_sibling(core, parts, got, wire, name):
    n = len(parts)

    def body(core_ref, *refs):
        for i in range(n):
            refs[2 * n + i][0] = (refs[i][0] + refs[n + i][0]).astype(wire[i])

    keep_spec = lambda p: pl.BlockSpec((1, None) + p.shape[2:], lambda s, core_ref: (s, core_ref[0], 0, 0))
    slot_spec = lambda p: pl.BlockSpec((1,) + p.shape[2:], lambda s, core_ref: (s, 0, 0))
    return pl.pallas_call(
        body, name=name,
        grid_spec=pltpu.PrefetchScalarGridSpec(
            num_scalar_prefetch=1, grid=(4,), in_specs=[keep_spec(p) for p in parts] + [slot_spec(p) for p in parts],
            out_specs=[slot_spec(p) for p in parts]),
        out_shape=[_sds((4,) + p.shape[2:], wire[i]) for i, p in enumerate(parts)],
        compiler_params=_params("parallel"),
    )(core, *parts, *got)


def _sum_slots(slots, name):
    n = len(slots)

    def spec_pair(p):
        k, rows, cols = p.shape
        sub = 16 if p.dtype == BF16 else 8
        if (rows // 2) % sub == 0:
            return pl.BlockSpec((k, rows // 2, cols), lambda i: (0, i, 0)), pl.BlockSpec((rows // 2, cols), lambda i: (i, 0))
        return pl.BlockSpec((k, rows, cols), lambda i: (0, 0, 0)), pl.BlockSpec((rows, cols), lambda i: (0, 0))

    pairs = [spec_pair(p) for p in slots]

    def body(*refs):
        for i in range(n):
            acc = refs[i][0].astype(F32)
            for j in range(1, slots[i].shape[0]):
                acc = acc + refs[i][j].astype(F32)
            refs[n + i][...] = acc

    return pl.pallas_call(
        body, name=name, grid=(2,), in_specs=[a for a, _ in pairs], out_specs=[b for _, b in pairs],
        out_shape=[_sds(p.shape[1:]) for p in slots], compiler_params=_params("arbitrary"),
    )(*slots)


def _modulation(c_all, ada_w, ada_b):
    cols = ada_w.shape[2]

    def body(c_ref, w_ref, b_ref, o_ref):
        cc = c_ref[...]
        o_ref[0] = _mm(cc * _sigmoid(cc), w_ref[0]) + b_ref[0]

    return pl.pallas_call(
        body, name="modulation", grid=(2,),
        in_specs=[_full((8, D_MODEL)), pl.BlockSpec((1, D_MODEL, cols), lambda l: (l, 0, 0)), pl.BlockSpec((1, 1, cols), lambda l: (l, 0, 0))],
        out_specs=pl.BlockSpec((1, 8, cols), lambda l: (l, 0, 0)), out_shape=_sds((2, 8, cols)),
        compiler_params=_params("parallel"),
    )(c_all, ada_w, ada_b)


def _adamw_math(w, g, m, v):
    m = ADAM_B1 * m + (1.0 - ADAM_B1) * g
    v = ADAM_B2 * v + (1.0 - ADAM_B2) * (g * g)
    m_hat = m / (1.0 - ADAM_B1 ** ADAM_STEP)
    v_hat = v / (1.0 - ADAM_B2 ** ADAM_STEP)
    delta = -ADAM_LR * (m_hat / (jnp.sqrt(v_hat) + ADAM_EPS) + ADAM_WD * w)
    return delta, m, v


def _ada_update(c_all, dmod, w, m, v):
    cols = w.shape[2]
    tr = 256
    spec3 = pl.BlockSpec((1, tr, cols), lambda l, i: (l, i, 0))

    def body(c_ref, d_ref, w_ref, m_ref, v_ref, g_ref, dl_ref, nm_ref, nv_ref):
        cc = c_ref[...]
        g = _mm_tn(cc * _sigmoid(cc), d_ref[0])
        g_ref[0] = g
        dl_ref[0], nm_ref[0], nv_ref[0] = _adamw_math(w_ref[0], g, m_ref[0], v_ref[0])

    return pl.pallas_call(
        body, name="ada_update", grid=(2, D_MODEL // tr),
        in_specs=[pl.BlockSpec((8, tr), lambda l, i: (0, i)), pl.BlockSpec((1, 8, cols), lambda l, i: (l, 0, 0)), spec3, spec3, spec3],
        out_specs=[spec3] * 4, out_shape=[_sds(w.shape)] * 4, compiler_params=_params("parallel", "parallel"),
    )(c_all, dmod, w, m, v)


def _adamw_matrices(params):
    n = len(params)
    steps = 8

    def body(*refs):
        ins, outs = refs[:4 * n], refs[4 * n:]
        for j in range(n):
            w_ref, g_ref, m_ref, v_ref = ins[4 * j:4 * j + 4]
            outs[3 * j][...], outs[3 * j + 1][...], outs[3 * j + 2][...] = _adamw_math(w_ref[...], g_ref[...], m_ref[...], v_ref[...])

    spec = lambda p: _rows(p[0].shape[0] // steps, p[0].shape[1])
    res = pl.pallas_call(
        body, name="adamw_matrices", grid=(steps,), in_specs=[spec(p) for p in params for _ in range(4)],
        out_specs=[spec(p) for p in params for _ in range(3)], out_shape=[_sds(p[0].shape) for p in params for _ in range(3)],
        compiler_params=_params("parallel"),
    )(*[a for p in params for a in p])
    return [tuple(res[3 * j:3 * j + 3]) for j in range(n)]


def _adamw_small(params):
    n = len(params)

    def body(*refs):
        ins, outs = refs[:4 * n], refs[4 * n:]
        for j in range(n):
            w_ref, g_ref, m_ref, v_ref = ins[4 * j:4 * j + 4]
            outs[3 * j][...], outs[3 * j + 1][...], outs[3 * j + 2][...] = _adamw_math(w_ref[...], g_ref[...], m_ref[...], v_ref[...])

    flat = [a for p in params for a in p]
    res = pl.pallas_call(body, name="adamw_small", out_shape=[_sds(p[0].shape) for p in params for _ in range(3)])(*flat)
    return [tuple(res[3 * j:3 * j + 3]) for j in range(n)]


def _cols(a, start, size):
    return lax.dynamic_slice_in_dim(a, start, size, axis=a.ndim - 1)


def kernel(x, c, positions, ada_w, ada_b, ln_g, ln_b, ev_w_in, ev_w_out, ev_sink, ev_sg_ln_g, ev_sg_ln_b, ev_sg_w, ev_sg_b, od_w_in, od_conv_w, od_conv_b, od_w_a, od_b_a, od_w_x, od_b_x, od_lam, od_w_out, loss_target, m_ada_w, m_ada_b, m_ln_g, m_ln_b, m_ev_w_in, m_ev_w_out, m_ev_sink, m_ev_sg_ln_g, m_ev_sg_ln_b, m_ev_sg_w, m_ev_sg_b, m_od_w_in, m_od_conv_w, m_od_conv_b, m_od_w_a, m_od_b_a, m_od_w_x, m_od_b_x, m_od_lam, m_od_w_out, v_ada_w, v_ada_b, v_ln_g, v_ln_b, v_ev_w_in, v_ev_w_out, v_ev_sink, v_ev_sg_ln_g, v_ev_sg_ln_b, v_ev_sg_w, v_ev_sg_b, v_od_w_in, v_od_conv_w, v_od_conv_b, v_od_w_a, v_od_b_a, v_od_w_x, v_od_b_x, v_od_lam, v_od_w_out):
    seq = x.shape[1]
    px, py, pc = _place()
    chip = 2 * px + py
    dev = 2 * chip + pc

    small = jnp.concatenate([od_conv_w[0].reshape(-1), od_conv_b[0], od_b_a[0].reshape(-1), jnp.zeros((256,), F32),
                             od_b_x[0].reshape(-1), od_lam[0].reshape(-1)]).reshape(3, D_MODEL)
    blk = jnp.concatenate([c, small, jnp.zeros((4, D_MODEL), F32)], axis=0)
    tr = lambda a: jnp.swapaxes(a, -1, -2)
    wire_w = lambda a: a.astype(MXU_DTYPE)
    ev_w_in4, g_small = _gather_weights([wire_w(tr(ev_w_in[0]))], blk)
    core = pc.astype(jnp.int32).reshape(1)
    ride = {"ev_w_out": wire_w(ev_w_out[0]), "od_w_in": wire_w(od_w_in[0]), "od_w_out": wire_w(od_w_out[0]),
            "core": core, "wire": MXU_DTYPE}
    c_all = g_small[:, 0, :]
    per_chip = g_small[0::2]
    conv_w = per_chip[:, 1].reshape(4, 4, 256).transpose(1, 0, 2).reshape(4, D_MODEL)
    conv_b = per_chip[:, 2, 0:256].reshape(D_MODEL)
    b_a = per_chip[:, 2, 256:768].reshape(4, 2, 256).transpose(1, 0, 2).reshape(2, D_MODEL)
    b_x = per_chip[:, 3, 0:512].reshape(4, 2, 256).transpose(1, 0, 2).reshape(2, D_MODEL)
    lam = per_chip[:, 3, 512:1024].reshape(4, 2, 256).transpose(1, 0, 2).reshape(2, D_MODEL)

    w_full = {
        "ev_w_in_t": ev_w_in4.reshape(EVEN_IN, D_MODEL),
        "ev_sink": ev_sink[0], "ev_sg_ln_g": ev_sg_ln_g[0], "ev_sg_ln_b": ev_sg_ln_b[0], "ev_sg_w": ev_sg_w[0],
        "ev_sg_b": ev_sg_b[0], "od_conv_w": conv_w, "od_conv_b": conv_b, "od_w_a": od_w_a[0], "od_b_a": b_a,
        "od_w_x": od_w_x[0], "od_b_x": b_x, "od_lam": lam, "ln_g": ln_g, "ln_b": ln_b,
    }

    ada_cols = ada_w.shape[2]
    mod_sh = _modulation(c_all, ada_w, _cols(ada_b, chip * ada_cols, ada_cols).reshape(2, 1, ada_cols))
    mod_all = _allgather8(mod_sh.reshape(16, ada_cols), "gather_mod").reshape(4, 2, 2, 8, ada_cols)[:, 0]
    mod_mine = lax.dynamic_index_in_dim(mod_all, dev, axis=2, keepdims=False)
    mod = mod_mine.transpose(1, 0, 2).reshape(2, 3, D_MODEL)

    posf = positions.astype(F32).reshape(seq, 1)
    loss_local, grad_x, dmod, g = _local_step(x[0], posf, loss_target[0], mod, w_full, seq, ride)

    pad = lambda a, n: jnp.concatenate([a.reshape(-1), jnp.zeros((n - a.size,), F32)])
    rows_small = jnp.concatenate([
        dmod.reshape(6, D_MODEL), g["ln_g"][0:1], g["ln_b"][0:1], g["ln_g"][1:2], g["ln_b"][1:2],
        jnp.concatenate([g["ev_sg_ln_g"], g["ev_sg_ln_b"]]).reshape(1, D_MODEL), g["ev_sg_b"].reshape(1, D_MODEL),
        g["od_conv_w"], g["od_conv_b"].reshape(1, D_MODEL), g["od_b_a"], g["od_b_x"], g["od_lam"],
        pad(g["ev_sink"], D_MODEL).reshape(1, D_MODEL), pad(loss_local, D_MODEL).reshape(1, D_MODEL),
        jnp.zeros((39, D_MODEL), F32)], axis=0)
    parts = [g["ev_w_in_t"].reshape(4, 2, 352, D_MODEL), g["ev_w_out"].reshape(4, 2, 128, D_MODEL),
             g["ev_sg_w"].reshape(4, 2, BLK, BLK), rows_small.reshape(4, 2, 8, D_MODEL)]
    wire = [MXU_DTYPE] * 3 + [F32]
    dmod_blk = jnp.concatenate([dmod.reshape(6, D_MODEL), jnp.zeros((2, D_MODEL), F32)], axis=0)
    *got, dmod_gathered = _reduce_sibling(parts, dmod_blk)
    ev_slots = list(_reduce_chips(_sum_sibling(core, parts, got, wire, "sum_sibling")))
    od_slots = list(g["od_slots"])
    mine = _sum_slots(ev_slots[0:2] + od_slots[0:2] + ev_slots[2:3] + od_slots[2:4] + ev_slots[3:4], "sum_chips")
    reduced = _gather_reduced(mine[:4], mine[4:])
    g_ev_w_in_t = reduced[0].reshape(704, D_MODEL)
    g_ev_w_out = reduced[1].reshape(256, D_MODEL)
    g_od_w_in = reduced[2].reshape(D_MODEL, 512)
    g_od_w_out = reduced[3].reshape(256, D_MODEL)
    g_sg_w = reduced[4].reshape(8 * BLK, BLK)
    g_w_a = reduced[5].reshape(16 * BLK, BLK)
    g_w_x = reduced[6].reshape(16 * BLK, BLK)
    gs = reduced[7].reshape(64, D_MODEL)
    loss = gs[24, 0]
    dmod_all = dmod_gathered[:, 0:6].reshape(8, 2, 3 * D_MODEL)
    dmod_sh = _cols(dmod_all, chip * ada_cols, ada_cols).transpose(1, 0, 2)
    g_ada_w, d_ada_w, nm_ada_w, nv_ada_w = _ada_update(c_all, dmod_sh, ada_w, m_ada_w, v_ada_w)

    mats = (("ev_w_out", ev_w_out, g_ev_w_out, m_ev_w_out, v_ev_w_out), ("od_w_in", od_w_in, g_od_w_in, m_od_w_in, v_od_w_in),
            ("od_w_out", od_w_out, g_od_w_out, m_od_w_out, v_od_w_out), ("ev_sg_w", ev_sg_w, g_sg_w, m_ev_sg_w, v_ev_sg_w),
            ("od_w_a", od_w_a, g_w_a, m_od_w_a, v_od_w_a), ("od_w_x", od_w_x, g_w_x, m_od_w_x, v_od_w_x))
    upd = _adamw_matrices([(tr(ev_w_in[0]), g_ev_w_in_t, tr(m_ev_w_in[0]), tr(v_ev_w_in[0]))]
                          + [(w_.reshape(g_.shape), g_, m_.reshape(g_.shape), v_.reshape(g_.shape)) for _, w_, g_, m_, v_ in mats])
    big = {"ev_w_in": tuple(tr(a).reshape(ev_w_in.shape) for a in (g_ev_w_in_t, *upd[0]))}
    for (name, w_, g_, _, _), u in zip(mats, upd[1:]):
        big[name] = tuple(a.reshape(w_.shape) for a in (g_, *u))
    big["ada_w"] = (g_ada_w, d_ada_w, nm_ada_w, nv_ada_w)

    sh = lambda a: _cols(a, chip * 256, 256)
    small_g = {
        "ada_b": gs[0:6].reshape(2, 3 * D_MODEL), "ln_g": jnp.stack([gs[6], gs[8]]), "ln_b": jnp.stack([gs[7], gs[9]]),
        "ev_sink": gs[23:24, 0:8], "ev_sg_ln_g": gs[10:11, 0:512], "ev_sg_ln_b": gs[10:11, 512:1024],
        "ev_sg_b": gs[11].reshape(8, BLK), "od_conv_w": sh(gs[12:16]), "od_conv_b": sh(gs[16:17]), "od_b_a": sh(gs[17:19]),
        "od_b_x": sh(gs[19:21]), "od_lam": sh(gs[21:23]),
    }
    small_in = {"ada_b": (ada_b, m_ada_b, v_ada_b), "ln_g": (ln_g, m_ln_g, v_ln_g), "ln_b": (ln_b, m_ln_b, v_ln_b),
                "ev_sink": (ev_sink, m_ev_sink, v_ev_sink), "ev_sg_ln_g": (ev_sg_ln_g, m_ev_sg_ln_g, v_ev_sg_ln_g),
                "ev_sg_ln_b": (ev_sg_ln_b, m_ev_sg_ln_b, v_ev_sg_ln_b), "ev_sg_b": (ev_sg_b, m_ev_sg_b, v_ev_sg_b),
                "od_conv_w": (od_conv_w, m_od_conv_w, v_od_conv_w), "od_conv_b": (od_conv_b, m_od_conv_b, v_od_conv_b),
                "od_b_a": (od_b_a, m_od_b_a, v_od_b_a), "od_b_x": (od_b_x, m_od_b_x, v_od_b_x),
                "od_lam": (od_lam, m_od_lam, v_od_lam)}
    names_small = list(small_g)
    upd = _adamw_small([(small_in[n][0].reshape(small_g[n].shape), small_g[n], small_in[n][1].reshape(small_g[n].shape),
                         small_in[n][2].reshape(small_g[n].shape)) for n in names_small])
    res = dict(big)
    for n, (d_, nm_, nv_) in zip(names_small, upd):
        shape = small_in[n][0].shape
        res[n] = tuple(a.reshape(shape) for a in (small_g[n], d_, nm_, nv_))

    order = ["ada_w", "ada_b", "ln_g", "ln_b", "ev_w_in", "ev_w_out", "ev_sink", "ev_sg_ln_g", "ev_sg_ln_b", "ev_sg_w", "ev_sg_b",
             "od_w_in", "od_conv_w", "od_conv_b", "od_w_a", "od_b_a", "od_w_x", "od_b_x", "od_lam", "od_w_out"]
    return (loss, grad_x.reshape(x.shape), *[res[n][0] for n in order], *[res[n][1] for n in order],
            *[res[n][2] for n in order], *[res[n][3] for n in order])
```

```python
import jax
import jax.numpy as jnp
import numpy as np
from jax import lax
from jax.experimental import pallas as pl
from jax.experimental.pallas import tpu as pltpu

F32 = jnp.float32
BF16 = jnp.bfloat16
MXU_DTYPE = BF16
ACT_DTYPE = MXU_DTYPE

D_MODEL = 1024
HEAD_DIM = 64
N_Q_HEADS = 8
Q_PER_KV = 4
ATTN_WIDTH = 512
BLK = 128
ROPE_DIM = 16
ROPE_THETA = 500000.0
N_SG_GROUPS = 8
SG_WIDTH = 512
EVEN_IN = 2816
ODD_IN = 2048
RNN_HEADS = 8
RG_LRU_C = 8.0
ALPHA = (2 * 2) ** 0.25
LN_EPS = 1e-5
NEG_INF = -1e30
ADAM_LR, ADAM_B1, ADAM_B2, ADAM_EPS, ADAM_WD, ADAM_STEP = 0.001, 0.9, 0.999, 1e-08, 0.01, 10

LANES = 128
VMEM_LIMIT = 56 * 1024 * 1024
MESH = pl.DeviceIdType.MESH


def _mm(a, b):
    return jnp.dot(a.astype(MXU_DTYPE), b.astype(MXU_DTYPE), preferred_element_type=F32)


def _mm_nt(a, b):
    return lax.dot_general(a.astype(MXU_DTYPE), b.astype(MXU_DTYPE), (((1,), (1,)), ((), ())), preferred_element_type=F32)


def _mm_tn(a, b):
    return lax.dot_general(a.astype(MXU_DTYPE), b.astype(MXU_DTYPE), (((0,), (0,)), ((), ())), preferred_element_type=F32)


def _sigmoid(x):
    return 1.0 / (1.0 + jnp.exp(-x))


def _ln_stats(z):
    mu = jnp.mean(z, axis=-1, keepdims=True)
    d = z - mu
    var = jnp.mean(d * d, axis=-1, keepdims=True)
    rstd = lax.rsqrt(var + LN_EPS)
    return d * rstd, rstd


def _ln_bwd(dout, zhat, rstd, g):
    dzh = dout * g
    m1 = jnp.mean(dzh, axis=-1, keepdims=True)
    m2 = jnp.mean(dzh * zhat, axis=-1, keepdims=True)
    return rstd * (dzh - m1 - zhat * m2)


def _group_sum(x, e2):
    hi = x.astype(MXU_DTYPE)
    lo = (x - hi.astype(F32)).astype(MXU_DTYPE)
    return jnp.dot(hi, e2, preferred_element_type=F32) + jnp.dot(lo, e2, preferred_element_type=F32)


def _lane_iota(shape):
    return lax.broadcasted_iota(jnp.int32, shape, 1)


def _to_kv_lanes(t, h):
    src_lo = (h % 2 == 0)
    dst_lo = (h // Q_PER_KV == 0)
    if src_lo != dst_lo:
        t = pltpu.roll(t, HEAD_DIM, 1)
    lane = _lane_iota(t.shape)
    keep = (lane < HEAD_DIM) if dst_lo else (lane >= HEAD_DIM)
    return jnp.where(keep, t, 0.0)


def _from_kv_lanes(t, h):
    src_lo = (h // Q_PER_KV == 0)
    dst_lo = (h % 2 == 0)
    lane = _lane_iota(t.shape)
    keep = (lane < HEAD_DIM) if src_lo else (lane >= HEAD_DIM)
    t = jnp.where(keep, t, 0.0)
    if src_lo != dst_lo:
        t = pltpu.roll(t, HEAD_DIM, 1)
    return t


def _rope(t, cos_t, sin_p, sin_m):
    half = ROPE_DIM // 2
    return t * cos_t + pltpu.roll(t, half, 1) * sin_p + pltpu.roll(t, LANES - half, 1) * sin_m


def _rope_t(d, cos_t, sin_p, sin_m):
    half = ROPE_DIM // 2
    return d * cos_t + pltpu.roll(d * sin_p, LANES - half, 1) + pltpu.roll(d * sin_m, half, 1)


def _band(ref, n, nb):
    prev = jnp.maximum(n - 1, 0)
    nxt = jnp.minimum(n + 1, nb - 1)
    rows = [ref[pl.ds(pl.multiple_of(j * BLK, BLK), BLK), :] for j in (prev, n, nxt)]
    return jnp.concatenate(rows, axis=0)


def _band_bias(n, seq):
    qi = lax.broadcasted_iota(jnp.int32, (BLK, 3 * BLK), 0)
    kj = lax.broadcasted_iota(jnp.int32, (BLK, 3 * BLK), 1)
    k_abs = n * BLK - BLK + kj
    valid = (jnp.abs(kj - BLK - qi) <= BLK) & (k_abs >= 0) & (k_abs < seq)
    bias = jnp.where(valid, 0.0, NEG_INF)
    return jnp.concatenate([bias] * Q_PER_KV, axis=0)


def _stack_heads(tile_of, kv):
    return jnp.concatenate([_to_kv_lanes(tile_of(h // 2), h) for h in range(Q_PER_KV * kv, Q_PER_KV * (kv + 1))], axis=0)


def _per_head_column(vals):
    row = lax.broadcasted_iota(jnp.int32, (Q_PER_KV * BLK, 1), 0)
    return jnp.where(row < BLK, vals[0], jnp.where(row < 2 * BLK, vals[1], jnp.where(row < 3 * BLK, vals[2], vals[3])))


def _softplus_neg(lam):
    e = jnp.exp(-jnp.abs(lam))
    u = 1.0 + e
    log1p_e = jnp.where(u == 1.0, e, jnp.log(u) * (e / (u - 1.0)))
    sp = jnp.maximum(-lam, 0.0) + log1p_e
    dsp = -1.0 / (1.0 + jnp.exp(lam))
    return sp, dsp


def _full(shape):
    return pl.BlockSpec(shape, lambda *_: (0,) * len(shape))


def _const(shape):
    return pl.BlockSpec(shape, lambda *_: (0,) * len(shape), pipeline_mode=pl.Buffered(1))


def _rows(tm, n):
    return pl.BlockSpec((tm, n), lambda i: (i, 0))


def _params(*sem):
    return pltpu.CompilerParams(dimension_semantics=sem, vmem_limit_bytes=VMEM_LIMIT)


def _sds(shape, dtype=F32):
    return jax.ShapeDtypeStruct(shape, dtype)


def _place():
    return lax.axis_index("x"), lax.axis_index("y"), lax.axis_index("c")


class _Rider:
    def __init__(self, ins, out_shapes, n_remote, n_local, plan):
        self.ins, self.out_shapes, self.n_remote, self.n_local, self.plan = list(ins), list(out_shapes), n_remote, n_local, plan

    def scratch(self):
        return [pltpu.SemaphoreType.DMA((self.n_remote,)), pltpu.SemaphoreType.DMA((self.n_remote,)),
                pltpu.SemaphoreType.DMA((max(self.n_local, 1),))]

    def run(self, first, in_refs, out_refs, sems):
        send_sems, recv_sems, local_sems = sems
        sends, recvs, locals_ = self.plan(in_refs, out_refs)
        remote = lambda k, src, dst, to: pltpu.make_async_remote_copy(
            src_ref=src, dst_ref=dst, send_sem=send_sems.at[k], recv_sem=recv_sems.at[k], device_id=to, device_id_type=MESH)
        if first:
            for k, src, dst, to in sends:
                remote(k, src, dst, to).start()
            for j, (src, dst) in enumerate(locals_):
                pltpu.make_async_copy(src, dst, local_sems.at[j]).start()
        else:
            for k, dst, frm in recvs:
                remote(k, dst, dst, frm).wait_recv()
            for k, src, dst, to in sends:
                remote(k, src, dst, to).wait_send()
            for j, (src, dst) in enumerate(locals_):
                pltpu.make_async_copy(src, dst, local_sems.at[j]).wait()


def _other_chips(x, y):
    return [(1 - x, y), (x, 1 - y), (1 - x, 1 - y)]


def _gather_rider(shard):
    hr = shard.shape[0] // 2

    def plan(ins, outs):
        x, y, c = _place()
        mine, src, dst = 2 * x + y, ins[0], outs[0]
        sends, recvs = [], []
        for j, (px, py) in enumerate(_other_chips(x, y)):
            for flip in range(2):
                tc = c if flip == 0 else 1 - c
                sends.append((2 * j + flip, src.at[pl.ds(c * hr, hr)], dst.at[mine, pl.ds(c * hr, hr)], (px, py, tc)))
                recvs.append((2 * j + flip, dst.at[2 * px + py, pl.ds(tc * hr, hr)], (px, py, tc)))
        return sends, recvs, [(src, dst.at[mine])]

    return _Rider([shard], [_sds((4,) + shard.shape, shard.dtype)], 6, 1, plan)


def _sibling_swap_rider(parts):
    n = len(parts)

    def plan(ins, outs):
        x, y, c = _place()
        sibling = (x, y, 1 - c)
        return ([(i, ins[i].at[:, 1 - c], outs[i], sibling) for i in range(n)], [(i, outs[i], sibling) for i in range(n)], [])

    return _Rider(parts, [_sds((4,) + p.shape[2:], p.dtype) for p in parts], n, 0, plan)


def _chip_exchange_rider(parts):
    n = len(parts)

    def plan(ins, outs):
        x, y, c = _place()
        mine = 2 * x + y
        sends, recvs = [], []
        for i in range(n):
            for j, (px, py) in enumerate(_other_chips(x, y)):
                sends.append((3 * i + j, ins[i].at[2 * px + py], outs[i].at[mine], (px, py, c)))
                recvs.append((3 * i + j, outs[i].at[2 * px + py], (px, py, c)))
        return sends, recvs, [(ins[i].at[mine], outs[i].at[mine]) for i in range(n)]

    return _Rider(parts, [_sds(p.shape, p.dtype) for p in parts], 3 * n, n, plan)


def _call(body, name, grid, in_specs, out_specs, out_shape, args, sem, scratch=(), rider=None):
    if rider is None:
        return list(pl.pallas_call(body, name=name, grid=grid, in_specs=in_specs, out_specs=out_specs, out_shape=out_shape,
                                   scratch_shapes=list(scratch), compiler_params=_params(sem))(*args)), []
    n_in, n_out, n_scr = len(in_specs), len(out_specs), len(scratch)
    r_in, r_out = len(rider.ins), len(rider.out_shapes)
    steps = grid[0]

    def riding(*refs):
        ins, r_ins = refs[:n_in], refs[n_in:n_in + r_in]
        outs = refs[n_in + r_in:n_in + r_in + n_out]
        r_outs = refs[n_in + r_in + n_out:n_in + r_in + n_out + r_out]
        scr = refs[n_in + r_in + n_out + r_out:n_in + r_in + n_out + r_out + n_scr]
        sems = refs[n_in + r_in + n_out + r_out + n_scr:]

        @pl.when(pl.program_id(0) == 0)
        def _():
            rider.run(True, r_ins, r_outs, sems)

        body(*ins, *outs, *scr)

        @pl.when(pl.program_id(0) == steps - 1)
        def _():
            rider.run(False, r_ins, r_outs, sems)

    hbm = pl.BlockSpec(memory_space=pl.ANY)
    res = pl.pallas_call(
        riding, name=name, grid=grid, in_specs=list(in_specs) + [hbm] * r_in, out_specs=list(out_specs) + [hbm] * r_out,
        out_shape=list(out_shape) + rider.out_shapes, scratch_shapes=list(scratch) + rider.scratch(),
        compiler_params=_params("arbitrary"),
    )(*args, *rider.ins)
    return list(res[:n_out]), list(res[n_out:])


def _row_tile(seq, want):
    return want if seq % want == 0 else seq


def _rope_tables(posf, seq):
    half = ROPE_DIM // 2
    inv_freq = np.power(np.float32(ROPE_THETA), -np.arange(half, dtype=np.float32) / np.float32(half)).astype(np.float32)
    j = np.arange(LANES) % HEAD_DIM
    invf = jnp.asarray(np.where(j < ROPE_DIM, inv_freq[j % half], 0.0).astype(np.float32).reshape(1, LANES))
    m_p = jnp.asarray(((j >= half) & (j < ROPE_DIM)).astype(np.float32).reshape(1, LANES))
    m_m = jnp.asarray(-(j < half).astype(np.float32).reshape(1, LANES))
    tm = _row_tile(seq, 512)

    def body(pos_ref, invf_ref, mp_ref, mm_ref, cos_ref, sp_ref, sm_ref):
        ang = pos_ref[...] * invf_ref[...]
        s = jnp.sin(ang)
        cos_ref[...] = jnp.cos(ang)
        sp_ref[...] = s * mp_ref[...]
        sm_ref[...] = s * mm_ref[...]

    return pl.pallas_call(
        body, name="rope_tables", grid=(seq // tm,),
        in_specs=[_rows(tm, 1), _full((1, LANES)), _full((1, LANES)), _full((1, LANES))],
        out_specs=[_rows(tm, LANES)] * 3, out_shape=[_sds((seq, LANES))] * 3,
        compiler_params=_params("parallel"),
    )(posf, invf, m_p, m_m)


def _even_proj(x, mod, w_in_t, tabs, seq, rider=None):
    tm = _row_tile(seq, 512)

    def body(x_ref, mod_ref, w_ref, cos_ref, sp_ref, sm_ref, q_ref, k_ref, v_ref, su_ref, sv_ref, g_ref):
        h = x_ref[...] * (1.0 + mod_ref[1:2, :]) + mod_ref[0:1, :]
        p = _mm_nt(h, w_ref[...])
        cos_t, sin_p, sin_m = cos_ref[...], sp_ref[...], sm_ref[...]
        for j in range(ATTN_WIDTH // LANES):
            q_ref[:, j * LANES:(j + 1) * LANES] = _rope(p[:, j * LANES:(j + 1) * LANES], cos_t, sin_p, sin_m).astype(q_ref.dtype)
        k_ref[...] = _rope(p[:, 512:640], cos_t, sin_p, sin_m).astype(k_ref.dtype)
        v_ref[...] = p[:, 640:768].astype(v_ref.dtype)
        su_ref[...] = p[:, 768:1280].astype(su_ref.dtype)
        sv_ref[...] = p[:, 1280:1792].astype(sv_ref.dtype)
        g_ref[...] = p[:, 1792:2816].astype(g_ref.dtype)

    return _call(
        body, "even_proj", (seq // tm,),
        [_rows(tm, D_MODEL), _full((3, D_MODEL)), _const((EVEN_IN, D_MODEL))] + [_rows(tm, LANES)] * 3,
        [_rows(tm, 512), _rows(tm, LANES), _rows(tm, LANES), _rows(tm, 512), _rows(tm, 512), _rows(tm, D_MODEL)],
        [_sds((seq, 512), MXU_DTYPE), _sds((seq, LANES), MXU_DTYPE), _sds((seq, LANES), MXU_DTYPE), _sds((seq, 512), ACT_DTYPE),
         _sds((seq, 512), ACT_DTYPE), _sds((seq, D_MODEL), ACT_DTYPE)],
        (x, mod, w_in_t, *tabs), "parallel", rider=rider)


def _sg_forward(sv, lng, lnb, sgw_ref, sgb, e2):
    vn, vhat, rstd, svo = [], [], [], []
    for j in range(SG_WIDTH // LANES):
        t = sv[:, j * LANES:(j + 1) * LANES]
        mu = _group_sum(t, e2) * (1.0 / HEAD_DIM)
        d = t - mu
        var = _group_sum(d * d, e2) * (1.0 / HEAD_DIM)
        r = lax.rsqrt(var + LN_EPS)
        vh = d * r
        vhat.append(vh)
        rstd.append(r)
        vn.append(vh * lng[:, j * LANES:(j + 1) * LANES] + lnb[:, j * LANES:(j + 1) * LANES])
    lane = _lane_iota((BLK, LANES))
    for j in range(SG_WIDTH // LANES):
        lo = _mm(sgw_ref[2 * j], vn[j])
        hi = _mm(sgw_ref[2 * j + 1], vn[j])
        svo.append(jnp.where(lane < HEAD_DIM, lo, hi) + sgb[:, j * LANES:(j + 1) * LANES])
    return svo, vn, vhat, rstd


def _even_mix(q, k, v, su, sv, sink, sgln_g, sgln_b, sgw, sgb_full, e2, seq, rider=None):
    nb = seq // BLK

    def body(sink_ref, q_ref, k_ref, v_ref, su_ref, sv_ref, lng_ref, lnb_ref, sgw_ref, sgb_ref, e2_ref, ycat_ref, lse_ref):
        n = pl.program_id(0)
        kband = _band(k_ref, n, nb)
        vband = _band(v_ref, n, nb)
        bias = _band_bias(n, seq)
        lane = _lane_iota((BLK, LANES))
        lse = jnp.zeros((BLK, LANES), F32)
        q_tile = lambda j: q_ref[:, j * LANES:(j + 1) * LANES].astype(F32)
        acc = [jnp.zeros((BLK, LANES), F32) for _ in range(ATTN_WIDTH // LANES)]
        for kv in range(N_Q_HEADS // Q_PER_KV):
            heads = range(Q_PER_KV * kv, Q_PER_KV * (kv + 1))
            sink = _per_head_column([sink_ref[h] for h in heads])
            s = _mm_nt(_stack_heads(q_tile, kv), kband) * (HEAD_DIM ** -0.5) + bias
            m = jnp.maximum(jnp.max(s, axis=1, keepdims=True), sink)
            p = jnp.exp(s - m)
            denom = jnp.sum(p, axis=1, keepdims=True) + jnp.exp(sink - m)
            o4 = _mm(p / denom, vband)
            l4 = m + jnp.log(denom)
            for g, h in enumerate(heads):
                acc[h // 2] = acc[h // 2] + _from_kv_lanes(o4[g * BLK:(g + 1) * BLK], h)
                lse = jnp.where(lane == h, l4[g * BLK:(g + 1) * BLK], lse)
        for j in range(ATTN_WIDTH // LANES):
            ycat_ref[:, j * LANES:(j + 1) * LANES] = acc[j].astype(ycat_ref.dtype)
        lse_ref[...] = lse
        svo, _, _, _ = _sg_forward(sv_ref[...].astype(F32), lng_ref[...], lnb_ref[...], sgw_ref, sgb_ref[...], e2_ref[...])
        for j in range(SG_WIDTH // LANES):
            ysg = su_ref[:, j * LANES:(j + 1) * LANES].astype(F32) * svo[j]
            ycat_ref[:, ATTN_WIDTH + j * LANES:ATTN_WIDTH + (j + 1) * LANES] = ysg.astype(ycat_ref.dtype)

    blk = lambda w: pl.BlockSpec((BLK, w), lambda n: (n, 0))
    return _call(
        body, "even_mix", (nb,),
        [pl.BlockSpec(memory_space=pltpu.SMEM), blk(512), _full((seq, LANES)), _full((seq, LANES)), blk(512), blk(512),
         _full((1, 512)), _full((1, 512)), _full((8, BLK, BLK)), _full((BLK, 512)), _full((LANES, LANES))],
        [blk(D_MODEL), blk(LANES)], [_sds((seq, D_MODEL), ACT_DTYPE), _sds((seq, LANES))],
        (sink, q, k, v, su, sv, sgln_g, sgln_b, sgw, sgb_full, e2), "parallel", rider=rider)


def _even_out(ycat, g, x, mod, mod_next, w_out, w_in4_next, ln_g, ln_b, seq, rider=None):
    tm = _row_tile(seq, 512)
    cs = ODD_IN // 4

    def body(y_ref, g_ref, x_ref, mod_ref, modn_ref, wo_ref, wi_ref, g1_ref, b1_ref, z_ref, x1_ref, xr_ref, gn_ref):
        gg = g_ref[...].astype(F32)
        out = _mm(y_ref[...].astype(F32) * (gg * _sigmoid(gg)), wo_ref[...])
        z = ALPHA * x_ref[...] + mod_ref[2:3, :] * out
        z_ref[...] = z
        zhat, _ = _ln_stats(z)
        x1 = zhat * g1_ref[...] + b1_ref[...]
        x1_ref[...] = x1
        hb = (x1 * (1.0 + modn_ref[1:2, :]) + modn_ref[0:1, :]).astype(MXU_DTYPE)
        for s in range(2):
            xr_ref[:, s * cs:(s + 1) * cs] = jnp.dot(hb, wi_ref[s], preferred_element_type=F32)
            gn_ref[:, s * cs:(s + 1) * cs] = jnp.dot(hb, wi_ref[2 + s], preferred_element_type=F32).astype(gn_ref.dtype)

    return _call(
        body, "even_out", (seq // tm,),
        [_rows(tm, D_MODEL)] * 3 + [_full((3, D_MODEL)), _full((3, D_MODEL)), _const((D_MODEL, D_MODEL)), _const((4, D_MODEL, cs)),
                                    _full((1, D_MODEL)), _full((1, D_MODEL))],
        [_rows(tm, D_MODEL)] * 4, [_sds((seq, D_MODEL))] * 3 + [_sds((seq, D_MODEL), ACT_DTYPE)],
        (ycat, g, x, mod, mod_next, w_out, w_in4_next, ln_g, ln_b), "parallel", rider=rider)


def _halo_specs(tm, seq, width, order=lambda i: i):
    per = tm // 8
    last = seq // 8 - 1
    return [pl.BlockSpec((8, width), lambda i: (jnp.maximum(order(i) * per - 1, 0), 0)),
            pl.BlockSpec((tm, width), lambda i: (order(i), 0)),
            pl.BlockSpec((8, width), lambda i: (jnp.minimum((order(i) + 1) * per, last), 0))]


def _extended(prev_ref, main_ref, next_ref, i, n_steps):
    prev = jnp.where(i > 0, prev_ref[...], 0.0)
    nxt = jnp.where(i < n_steps - 1, next_ref[...], 0.0)
    return jnp.concatenate([prev, main_ref[...], nxt], axis=0)


def _shifted(ext, off, tm):
    if off == 0:
        return ext[8:8 + tm]
    return pltpu.roll(ext, (-off) % ext.shape[0], 0)[8:8 + tm]


SCAN_SUB = 8


def _lru_gate(xh, pre, bias, sp, hs, d):
    r = _sigmoid(pre[:, 0:LANES] + bias[2 * d:2 * d + 1, hs])
    ig = _sigmoid(pre[:, LANES:2 * LANES] + bias[2 * d + 1:2 * d + 2, hs])
    neg_log_a = RG_LRU_C * r * sp[d:d + 1, hs]
    a = jnp.exp(-neg_log_a)
    u = jnp.tanh(neg_log_a) * (a * a + 1.0)
    inv_s = lax.rsqrt(jnp.maximum(u, jnp.finfo(F32).tiny))
    return r, ig, a, u * inv_s, inv_s


def _conv_block(xp_ref, xm_ref, xn_ref, cw_ref, cb_ref, blk, steps, tm):
    ext = _extended(xp_ref, xm_ref, xn_ref, blk, steps)
    return cb_ref[...] + sum(cw_ref[kk:kk + 1, :] * _shifted(ext, kk - 2, tm) for kk in range(4))


def _scan_tiles(a_ref, b_ref, h_ref, hprev_ref, carry_h, carry_a, rows, descending, post):
    sub = SCAN_SUB
    tiles = rows // sub
    row = lax.broadcasted_iota(jnp.int32, (sub, D_MODEL), 0)

    def shift(v, d, fill):
        if descending:
            return jnp.where(row <= sub - 1 - d, pltpu.roll(v, sub - d, 0), fill)
        return jnp.where(row >= d, pltpu.roll(v, d, 0), fill)

    def last(v):
        return jnp.broadcast_to(v[0:1, :] if descending else v[sub - 1:sub, :], v.shape)

    def tile(j, c):
        ch, ca = c
        r0 = pl.multiple_of(((tiles - 1 - j) if descending else j) * sub, sub)
        at = a_ref[pl.ds(r0, sub), :]
        bt = b_ref[pl.ds(r0, sub), :]
        coef = shift(at, 1, ca) if post else at
        acc_a, acc_b = coef, bt
        for d in (1, 2, 4):
            acc_b = acc_b + acc_a * shift(acc_b, d, 0.0)
            acc_a = acc_a * shift(acc_a, d, 1.0)
        h = acc_b + acc_a * ch
        h_ref[pl.ds(r0, sub), :] = h
        if post:
            return last(h), last(at)
        hprev_ref[pl.ds(r0, sub), :] = shift(h, 1, ch)
        return last(h), ca

    ch, ca = lax.fori_loop(0, tiles, tile, (carry_h[...], carry_a[...]), unroll=4)
    carry_h[...] = ch
    carry_a[...] = ca


def _lru_fwd(xr, xc, conv_w, conv_b, wcat, bias, lam, seq, d):
    tb = _row_tile(seq, 256)
    steps = seq // tb
    descending = d == 1
    order = (lambda i: steps - 1 - i) if descending else (lambda i: i)
    with_conv = xc is None
    n_x = 5 if with_conv else 1

    def body(*refs):
        x_refs, (w_ref, bias_ref, lam_ref) = refs[:n_x], refs[n_x:n_x + 3]
        h_ref, hp_ref, a_ref, r_ref, i_ref, s_ref, q_ref = refs[n_x + 3:n_x + 10]
        b_scr, carry_h, carry_a = refs[-3:]
        i = pl.program_id(0)

        @pl.when(i == 0)
        def _():
            carry_h[...] = jnp.zeros_like(carry_h)
            carry_a[...] = jnp.zeros_like(carry_a)

        if with_conv:
            xc_ref = refs[n_x + 10]
            xc_ref[...] = _conv_block(*x_refs, order(i), steps, tb)
        else:
            xc_ref = x_refs[0]
        sp, _ = _softplus_neg(lam_ref[...])
        bias = bias_ref[...]
        for h in range(RNN_HEADS):
            hs = slice(h * LANES, (h + 1) * LANES)
            xh = xc_ref[:, hs]
            r, ig, a, s, q = _lru_gate(xh, _mm(xh, w_ref[h, :, 2 * d * LANES:2 * (d + 1) * LANES]), bias, sp, hs, d)
            a_ref[:, hs] = a
            b_scr[:, hs] = s * ig * xh
            for ref, val in ((r_ref, r), (i_ref, ig), (s_ref, s), (q_ref, q)):
                ref[:, hs] = val.astype(ref.dtype)
        _scan_tiles(a_ref, b_scr, h_ref, hp_ref, carry_h, carry_a, tb, descending, post=False)

    row_spec = pl.BlockSpec((tb, D_MODEL), lambda i: (order(i), 0))
    if with_conv:
        x_specs, x_args = _halo_specs(tb, seq, D_MODEL, order) + [_full((4, D_MODEL)), _full((1, D_MODEL))], (xr, xr, xr, conv_w, conv_b)
    else:
        x_specs, x_args = [row_spec], (xc,)
    n_out = 8 if with_conv else 7
    return pl.pallas_call(
        body, name="lru_fwd_%d" % d, grid=(steps,),
        in_specs=x_specs + [_full((8, LANES, 512)), _full((4, D_MODEL)), _full((2, D_MODEL))],
        out_specs=[row_spec] * n_out,
        out_shape=[_sds((seq, D_MODEL))] * 3 + [_sds((seq, D_MODEL), ACT_DTYPE)] * 4 + [_sds((seq, D_MODEL))] * (n_out - 7),
        scratch_shapes=[pltpu.VMEM((tb, D_MODEL), F32)] + [pltpu.VMEM((SCAN_SUB, D_MODEL), F32)] * 2,
        compiler_params=_params("arbitrary"),
    )(*x_args, wcat, bias, lam)


def _odd_out_and_loss(hf, hr, g, x1, tgt, mod, w_out, ln_g, ln_b, seq):
    tm = _row_tile(seq, 512)

    def body(hf_ref, hr_ref, g_ref, x_ref, t_ref, mod_ref, w_ref, lg_ref, lb_ref,
             dhs_ref, dg_ref, dres_ref, loss_ref, dw_ref, vec_ref):
        @pl.when(pl.program_id(0) == 0)
        def _():
            loss_ref[...] = jnp.zeros_like(loss_ref)
            dw_ref[...] = jnp.zeros_like(dw_ref)
            vec_ref[...] = jnp.zeros_like(vec_ref)

        gg = g_ref[...].astype(F32)
        sg = _sigmoid(gg)
        silu = gg * sg
        hsum = hf_ref[...] + hr_ref[...]
        y = hsum * silu
        out = _mm(y, w_ref[...])
        gate = mod_ref[2:3, :]
        z = ALPHA * x_ref[...] + gate * out
        zhat, rstd = _ln_stats(z)
        x2 = zhat * lg_ref[...] + lb_ref[...]
        err = x2 - t_ref[...]
        loss_ref[...] += 0.5 * jnp.sum(jnp.mean(err * err, axis=-1, keepdims=True))
        dx2 = err * (1.0 / D_MODEL)
        dz = _ln_bwd(dx2, zhat, rstd, lg_ref[...])
        vec_ref[0:1, :] += jnp.sum(dx2 * zhat, axis=0, keepdims=True)
        vec_ref[1:2, :] += jnp.sum(dx2, axis=0, keepdims=True)
        vec_ref[2:3, :] += jnp.sum(dz * out, axis=0, keepdims=True)
        dres_ref[...] = ALPHA * dz
        dout = gate * dz
        dw_ref[...] += _mm_tn(y, dout)
        dy = _mm_nt(dout, w_ref[...])
        dhs_ref[...] = dy * silu
        dg_ref[...] = (dy * hsum * (sg * (1.0 + gg * (1.0 - sg)))).astype(dg_ref.dtype)

    return pl.pallas_call(
        body, name="odd_out_loss", grid=(seq // tm,),
        in_specs=[_rows(tm, D_MODEL)] * 5 + [_full((3, D_MODEL)), _const((D_MODEL, D_MODEL)),
                                             _full((1, D_MODEL)), _full((1, D_MODEL))],
        out_specs=[_rows(tm, D_MODEL)] * 3 + [_full((8, LANES)), _full((D_MODEL, D_MODEL)), _full((8, D_MODEL))],
        out_shape=[_sds((seq, D_MODEL)), _sds((seq, D_MODEL), ACT_DTYPE), _sds((seq, D_MODEL)), _sds((8, LANES)),
                   _sds((D_MODEL, D_MODEL)), _sds((8, D_MODEL))],
        compiler_params=_params("arbitrary"),
    )(hf, hr, g, x1, tgt, mod, w_out, ln_g, ln_b)


def _lru_bwd(xc, dhs, hprev, a_d, r_d, i_d, s_d, q_d, wcat, lam, seq, d):
    tb = _row_tile(seq, 256)
    steps = seq // tb
    descending = d == 0
    order = (lambda i: steps - 1 - i) if descending else (lambda i: i)
    cols = slice(2 * d * LANES, 2 * (d + 1) * LANES)

    def body(xc_ref, dhs_ref, hp_ref, a_ref, r_ref, i_ref, s_ref, q_ref, w_ref, lam_ref, dxc_ref, dw_ref, vec_ref,
             g_scr, carry_h, carry_a):
        i = pl.program_id(0)

        @pl.when(i == 0)
        def _():
            dw_ref[...] = jnp.zeros_like(dw_ref)
            vec_ref[...] = jnp.zeros_like(vec_ref)
            carry_h[...] = jnp.zeros_like(carry_h)
            carry_a[...] = jnp.zeros_like(carry_a)

        sp, dsp = _softplus_neg(lam_ref[...])
        _scan_tiles(a_ref, dhs_ref, g_scr, None, carry_h, carry_a, tb, descending, post=True)
        for h in range(RNN_HEADS):
            hs = slice(h * LANES, (h + 1) * LANES)
            xh, a = xc_ref[:, hs], a_ref[:, hs]
            r, ig, s = r_ref[:, hs].astype(F32), i_ref[:, hs].astype(F32), s_ref[:, hs].astype(F32)
            db = g_scr[:, hs]
            da = db * hp_ref[:, hs]
            dlog_a = da * a - (db * ig * xh) * (a * a * q_ref[:, hs].astype(F32))
            dpr = dlog_a * (-RG_LRU_C) * sp[d:d + 1, hs] * r * (1.0 - r)
            dpi = db * s * xh * ig * (1.0 - ig)
            vec_ref[0:1, hs] += jnp.sum(dpr, axis=0, keepdims=True)
            vec_ref[1:2, hs] += jnp.sum(dpi, axis=0, keepdims=True)
            vec_ref[2:3, hs] += jnp.sum(dlog_a * r, axis=0, keepdims=True) * (-RG_LRU_C) * dsp[d:d + 1, hs]
            dcat = jnp.concatenate([dpr, dpi], axis=1)
            dw_ref[h] += _mm_tn(xh, dcat)
            dxc_ref[:, hs] = db * s * ig + _mm_nt(dcat, w_ref[h, :, cols])

    row_spec = pl.BlockSpec((tb, D_MODEL), lambda i: (order(i), 0))
    return pl.pallas_call(
        body, name="lru_bwd_%d" % d, grid=(steps,),
        in_specs=[row_spec] * 8 + [_full((8, LANES, 512)), _full((2, D_MODEL))],
        out_specs=[row_spec, _full((8, LANES, 2 * LANES)), _full((8, D_MODEL))],
        out_shape=[_sds((seq, D_MODEL)), _sds((8, LANES, 2 * LANES)), _sds((8, D_MODEL))],
        scratch_shapes=[pltpu.VMEM((tb, D_MODEL), F32)] + [pltpu.VMEM((SCAN_SUB, D_MODEL), F32)] * 2,
        compiler_params=_params("arbitrary"),
    )(xc, dhs, hprev, a_d, r_d, i_d, s_d, q_d, wcat, lam)


def _odd_proj_bwd(dxc_f, dxc_r, xr, dg, x1, dres, mod, conv_w, w_in4, seq):
    tm = _row_tile(seq, 512)
    steps = seq // tm

    def body(fp_ref, fm_ref, fn_ref, rp_ref, rm_ref, rn_ref, xp_ref, xm_ref, xn_ref, dg_ref, x_ref, dres_ref, mod_ref, cw_ref,
             w_ref, dx_ref, dw_ref, vec_ref, dpb_ref):
        i = pl.program_id(0)

        @pl.when(i == 0)
        def _():
            vec_ref[...] = jnp.zeros_like(vec_ref)
            dw_ref[...] = jnp.zeros_like(dw_ref)

        dxc_m = fm_ref[...] + rm_ref[...]
        dext = jnp.concatenate([jnp.where(i > 0, fp_ref[...] + rp_ref[...], 0.0), dxc_m,
                                jnp.where(i < steps - 1, fn_ref[...] + rn_ref[...], 0.0)], axis=0)
        xext = _extended(xp_ref, xm_ref, xn_ref, i, steps)
        dxr = sum(cw_ref[kk:kk + 1, :] * _shifted(dext, 2 - kk, tm) for kk in range(4))
        for kk in range(4):
            vec_ref[kk:kk + 1, :] += jnp.sum(dxc_m * _shifted(xext, kk - 2, tm), axis=0, keepdims=True)
        vec_ref[4:5, :] += jnp.sum(dxc_m, axis=0, keepdims=True)
        dpb_ref[:, :D_MODEL] = dxr.astype(dpb_ref.dtype)
        dpb_ref[:, D_MODEL:] = dg_ref[...].astype(dpb_ref.dtype)
        cs = ODD_IN // 4
        dh = sum(_mm_nt(dpb_ref[:, s * cs:(s + 1) * cs], w_ref[s]) for s in range(4))
        x = x_ref[...]
        h_t = (x * (1.0 + mod_ref[1:2, :]) + mod_ref[0:1, :]).T.astype(MXU_DTYPE)
        for s in range(4):
            dw_ref[s] += jnp.dot(h_t, dpb_ref[:, s * cs:(s + 1) * cs], preferred_element_type=F32)
        vec_ref[5:6, :] += jnp.sum(dh, axis=0, keepdims=True)
        vec_ref[6:7, :] += jnp.sum(dh * x, axis=0, keepdims=True)
        dx_ref[...] = dres_ref[...] + dh * (1.0 + mod_ref[1:2, :])

    return pl.pallas_call(
        body, name="odd_proj_bwd", grid=(steps,),
        in_specs=_halo_specs(tm, seq, D_MODEL) * 3 + [_rows(tm, D_MODEL)] * 3
        + [_full((3, D_MODEL)), _full((4, D_MODEL)), _const((4, D_MODEL, ODD_IN // 4))],
        out_specs=[_rows(tm, D_MODEL), _const((4, D_MODEL, ODD_IN // 4)), _full((8, D_MODEL))],
        out_shape=[_sds((seq, D_MODEL)), _sds((4, D_MODEL, ODD_IN // 4)), _sds((8, D_MODEL))],
        scratch_shapes=[pltpu.VMEM((tm, ODD_IN), MXU_DTYPE)],
        compiler_params=_params("arbitrary"),
    )(dxc_f, dxc_f, dxc_f, dxc_r, dxc_r, dxc_r, xr, xr, xr, dg, x1, dres, mod, conv_w, w_in4)


def _even_out_bwd(dx1, z, ycat, g, mod, ln_g, w_out, seq, rider=None):
    tm = _row_tile(seq, 512)
    steps = seq // tm

    def body(dx_ref, z_ref, y_ref, g_ref, mod_ref, lg_ref, w_ref, dy_ref, dg_ref, dres_ref, dw_ref, vec_ref):
        i = pl.program_id(0)

        @pl.when(i == 0)
        def _():
            dw_ref[...] = jnp.zeros_like(dw_ref)
            vec_ref[...] = jnp.zeros_like(vec_ref)

        zhat, rstd = _ln_stats(z_ref[...])
        dx1_ = dx_ref[...]
        dz = _ln_bwd(dx1_, zhat, rstd, lg_ref[...])
        vec_ref[0:1, :] += jnp.sum(dx1_ * zhat, axis=0, keepdims=True)
        vec_ref[1:2, :] += jnp.sum(dx1_, axis=0, keepdims=True)
        dres_ref[...] = ALPHA * dz
        gate = mod_ref[2:3, :]
        gg = g_ref[...].astype(F32)
        sg = _sigmoid(gg)
        silu = gg * sg
        ycat_ = y_ref[...].astype(F32)
        dw_ref[...] += _mm_tn(ycat_ * silu, dz)
        dy = _mm_nt(gate * dz, w_ref[...])
        dy_ref[...] = (dy * silu).astype(dy_ref.dtype)
        dg_ref[...] = (dy * ycat_ * (sg * (1.0 + gg * (1.0 - sg)))).astype(dg_ref.dtype)

        @pl.when(i == steps - 1)
        def _():
            m_acc = dw_ref[...]
            vec_ref[2:3, :] = jnp.sum(w_ref[...].astype(F32) * m_acc, axis=0, keepdims=True)
            dw_ref[...] = m_acc * gate

    return _call(
        body, "even_out_bwd", (steps,),
        [_rows(tm, D_MODEL)] * 4 + [_full((3, D_MODEL)), _full((1, D_MODEL)), _const((D_MODEL, D_MODEL))],
        [_rows(tm, D_MODEL)] * 3 + [_full((D_MODEL, D_MODEL)), _full((8, D_MODEL))],
        [_sds((seq, D_MODEL), ACT_DTYPE), _sds((seq, D_MODEL), ACT_DTYPE), _sds((seq, D_MODEL)), _sds((D_MODEL, D_MODEL)),
         _sds((8, D_MODEL))],
        (dx1, z, ycat, g, mod, ln_g, w_out), "arbitrary", rider=rider)


def _even_mix_bwd(q, k, v, lse, ycat, dycat, su, sv, sink, sgln_g, sgln_b, sgw, sgb_full, e2, e8, seq, rider=None):
    nb = seq // BLK

    def body(sink_ref, q_ref, k_ref, v_ref, lse_ref, y_ref, dy_ref, su_ref, sv_ref, lng_ref, lnb_ref, sgw_ref, sgb_ref, e2_ref,
             e8_ref, dq_ref, dsu_ref, dsv_ref, dk_ref, dv_ref, dsgw_ref, dsgb_ref, vec_ref, dsink_ref, dsgb_acc):
        n = pl.program_id(0)

        @pl.when(n == 0)
        def _():
            dk_ref[...] = jnp.zeros_like(dk_ref)
            dv_ref[...] = jnp.zeros_like(dv_ref)
            dsgw_ref[...] = jnp.zeros_like(dsgw_ref)
            dsgb_acc[...] = jnp.zeros_like(dsgb_acc)
            vec_ref[...] = jnp.zeros_like(vec_ref)
            dsink_ref[...] = jnp.zeros_like(dsink_ref)

        kband = _band(k_ref, n, nb)
        vband = _band(v_ref, n, nb)
        bias = _band_bias(n, seq)
        lane = _lane_iota((BLK, LANES))
        row8 = lax.broadcasted_iota(jnp.int32, (8, LANES), 0)
        lse = lse_ref[...]
        dkb = jnp.zeros((LANES, 3 * BLK), F32)
        dvb = jnp.zeros((LANES, 3 * BLK), F32)
        dsink = jnp.zeros((8, LANES), F32)
        q_tile = lambda j: q_ref[:, j * LANES:(j + 1) * LANES].astype(F32)
        do_tile = lambda j: dy_ref[:, j * LANES:(j + 1) * LANES].astype(F32)
        dq = [jnp.zeros((BLK, LANES), F32) for _ in range(ATTN_WIDTH // LANES)]
        for kv in range(N_Q_HEADS // Q_PER_KV):
            heads = range(Q_PER_KV * kv, Q_PER_KV * (kv + 1))
            lse4, delta4 = [], []
            for h in heads:
                head_lanes = (lane < HEAD_DIM) if h % 2 == 0 else (lane >= HEAD_DIM)
                lse4.append(jnp.sum(jnp.where(lane == h, lse, 0.0), axis=1, keepdims=True))
                o_tile = y_ref[:, (h // 2) * LANES:(h // 2 + 1) * LANES].astype(F32)
                delta4.append(jnp.sum(jnp.where(head_lanes, do_tile(h // 2) * o_tile, 0.0), axis=1, keepdims=True))
            lse4, delta4 = jnp.concatenate(lse4, axis=0), jnp.concatenate(delta4, axis=0)
            q4, do4 = _stack_heads(q_tile, kv), _stack_heads(do_tile, kv)
            s = _mm_nt(q4, kband) * (HEAD_DIM ** -0.5) + bias
            p = jnp.exp(s - lse4)
            wsink = jnp.exp(_per_head_column([sink_ref[h] for h in heads]) - lse4) * delta4
            ds = p * (_mm_nt(do4, vband) - delta4) * (HEAD_DIM ** -0.5)
            dq4 = _mm(ds, kband)
            dkb = dkb + _mm_tn(q4, ds)
            dvb = dvb + _mm_tn(do4, p)
            for g, h in enumerate(heads):
                dq[h // 2] = dq[h // 2] + _from_kv_lanes(dq4[g * BLK:(g + 1) * BLK], h)
                dsink = dsink + jnp.where(row8 == h, -jnp.sum(wsink[g * BLK:(g + 1) * BLK]), 0.0)
        for j in range(ATTN_WIDTH // LANES):
            dq_ref[:, j * LANES:(j + 1) * LANES] = dq[j].astype(dq_ref.dtype)
        dsink_ref[...] += dsink
        prev = jnp.maximum(n - 1, 0)
        nxt = jnp.minimum(n + 1, nb - 1)
        for part, blk_i in enumerate((prev, n, nxt)):
            rows = pl.ds(pl.multiple_of(blk_i * BLK, BLK), BLK)
            dk_ref[rows, :] += dkb[:, part * BLK:(part + 1) * BLK].T
            dv_ref[rows, :] += dvb[:, part * BLK:(part + 1) * BLK].T

        e2 = e2_ref[...]
        lng = lng_ref[...]
        svo, vn, vhat, rstd = _sg_forward(sv_ref[...].astype(F32), lng, lnb_ref[...], sgw_ref, sgb_ref[...], e2)
        for j in range(SG_WIDTH // LANES):
            cs = slice(j * LANES, (j + 1) * LANES)
            dysg = dy_ref[:, ATTN_WIDTH + j * LANES:ATTN_WIDTH + (j + 1) * LANES].astype(F32)
            dsu_ref[:, cs] = (dysg * svo[j]).astype(dsu_ref.dtype)
            dsvo = dysg * su_ref[:, cs].astype(F32)
            dsgb_acc[:, cs] += dsvo
            d_lo = jnp.where(lane < HEAD_DIM, dsvo, 0.0)
            d_hi = dsvo - d_lo
            dsgw_ref[2 * j] += _mm_nt(d_lo, vn[j])
            dsgw_ref[2 * j + 1] += _mm_nt(d_hi, vn[j])
            dvn = _mm_tn(sgw_ref[2 * j], d_lo) + _mm_tn(sgw_ref[2 * j + 1], d_hi)
            vec_ref[0:1, cs] += jnp.sum(dvn * vhat[j], axis=0, keepdims=True)
            vec_ref[1:2, cs] += jnp.sum(dvn, axis=0, keepdims=True)
            dvh = dvn * lng[:, cs]
            m1 = _group_sum(dvh, e2) * (1.0 / HEAD_DIM)
            m2 = _group_sum(dvh * vhat[j], e2) * (1.0 / HEAD_DIM)
            dsv_ref[:, cs] = (rstd[j] * (dvh - m1 - vhat[j] * m2)).astype(dsv_ref.dtype)

        @pl.when(n == nb - 1)
        def _():
            rest = dsgb_acc[...]
            total = jnp.zeros((8, BLK), F32)
            for _ in range(3):
                part = rest.astype(MXU_DTYPE)
                total = total + lax.dot_general(e8_ref[...], part, (((1,), (1,)), ((), ())), preferred_element_type=F32)
                rest = rest - part.astype(F32)
            dsgb_ref[...] = total

    blk = lambda w: pl.BlockSpec((BLK, w), lambda n: (n, 0))
    return _call(
        body, "even_mix_bwd", (nb,),
        [pl.BlockSpec(memory_space=pltpu.SMEM), blk(512), _full((seq, LANES)), _full((seq, LANES)), blk(LANES),
         blk(D_MODEL), blk(D_MODEL), blk(512), blk(512), _full((1, 512)), _full((1, 512)), _full((8, BLK, BLK)),
         _full((BLK, 512)), _full((LANES, LANES)), _full((8, 512))],
        [blk(512), blk(512), blk(512), _full((seq, LANES)), _full((seq, LANES)), _full((8, BLK, BLK)),
         _full((8, BLK)), _full((8, 512)), _full((8, LANES))],
        [_sds((seq, 512), ACT_DTYPE), _sds((seq, 512), ACT_DTYPE), _sds((seq, 512), ACT_DTYPE), _sds((seq, LANES)), _sds((seq, LANES)),
         _sds((8, BLK, BLK)), _sds((8, BLK)), _sds((8, 512)), _sds((8, LANES))],
        (sink, q, k, v, lse, ycat, dycat, su, sv, sgln_g, sgln_b, sgw, sgb_full, e2, e8), "arbitrary",
        scratch=[pltpu.VMEM((BLK, 512), F32)], rider=rider)


def _even_proj_bwd(dq, dk, dv, dsu, dsv, dg, x, dres, mod, tabs, w_in_t, seq):
    tm = _row_tile(seq, 512)

    def body(dq_ref, dk_ref, dv_ref, dsu_ref, dsv_ref, dg_ref, x_ref, dres_ref, mod_ref, cos_ref, sp_ref, sm_ref, wt_ref,
             dx_ref, dw_ref, vec_ref, dpb_ref):
        @pl.when(pl.program_id(0) == 0)
        def _():
            vec_ref[...] = jnp.zeros_like(vec_ref)
            dw_ref[...] = jnp.zeros_like(dw_ref)

        cos_t, sin_p, sin_m = cos_ref[...], sp_ref[...], sm_ref[...]
        dt = dpb_ref.dtype
        for j in range(ATTN_WIDTH // LANES):
            cs = slice(j * LANES, (j + 1) * LANES)
            dpb_ref[:, cs] = _rope_t(dq_ref[:, cs].astype(F32), cos_t, sin_p, sin_m).astype(dt)
        dpb_ref[:, 512:640] = _rope_t(dk_ref[...], cos_t, sin_p, sin_m).astype(dt)
        dpb_ref[:, 640:768] = dv_ref[...].astype(dt)
        dpb_ref[:, 768:1280] = dsu_ref[...].astype(dt)
        dpb_ref[:, 1280:1792] = dsv_ref[...].astype(dt)
        dpb_ref[:, 1792:2816] = dg_ref[...].astype(dt)
        dpb = dpb_ref[...]
        dh = jnp.dot(dpb, wt_ref[...], preferred_element_type=F32)
        x_ = x_ref[...]
        hb = (x_ * (1.0 + mod_ref[1:2, :]) + mod_ref[0:1, :]).astype(MXU_DTYPE)
        dw_ref[...] += _mm_tn(dpb, hb)
        vec_ref[0:1, :] += jnp.sum(dh, axis=0, keepdims=True)
        vec_ref[1:2, :] += jnp.sum(dh * x_, axis=0, keepdims=True)
        dx_ref[...] = dres_ref[...] + dh * (1.0 + mod_ref[1:2, :])

    return pl.pallas_call(
        body, name="even_proj_bwd", grid=(seq // tm,),
        in_specs=[_rows(tm, 512), _rows(tm, LANES), _rows(tm, LANES), _rows(tm, 512), _rows(tm, 512), _rows(tm, D_MODEL),
                  _rows(tm, D_MODEL), _rows(tm, D_MODEL), _full((3, D_MODEL))] + [_rows(tm, LANES)] * 3
        + [_const((EVEN_IN, D_MODEL))],
        out_specs=[_rows(tm, D_MODEL), _const((EVEN_IN, D_MODEL)), _full((8, D_MODEL))],
        out_shape=[_sds((seq, D_MODEL)), _sds((EVEN_IN, D_MODEL)), _sds((8, D_MODEL))],
        scratch_shapes=[pltpu.VMEM((tm, EVEN_IN), MXU_DTYPE)],
        compiler_params=_params("arbitrary"),
    )(dq, dk, dv, dsu, dsv, dg, x, dres, mod, *tabs, w_in_t)


def _local_step(x, posf, tgt, mod, w, seq, ride=None):
    rid = lambda make, *a: None if ride is None else make(*a)
    mxu = lambda a: a.astype(MXU_DTYPE)
    row = lambda a: a.reshape(1, -1)
    tabs = _rope_tables(posf, seq)
    e2 = mxu(jnp.kron(jnp.eye(2, dtype=F32), jnp.ones((HEAD_DIM, HEAD_DIM), F32)))
    e8 = mxu(jnp.repeat(jnp.eye(N_SG_GROUPS, dtype=F32), HEAD_DIM, axis=1))
    sgw = mxu(w["ev_sg_w"])
    sgb_full = jnp.repeat(w["ev_sg_b"].T, HEAD_DIM, axis=1)
    sgln_g, sgln_b = row(w["ev_sg_ln_g"]), row(w["ev_sg_ln_b"])
    sink = w["ev_sink"].reshape(N_Q_HEADS)
    ev_w_in_t = mxu(w["ev_w_in_t"])
    if ride is None:
        ev_w_out, od_w_in, od_w_out = mxu(w["ev_w_out"]), mxu(w["od_w_in"]), mxu(w["od_w_out"])
    wcat = mxu(jnp.concatenate([w["od_w_a"][0], w["od_w_x"][0], w["od_w_a"][1], w["od_w_x"][1]], axis=2))
    gate_bias = jnp.stack([w["od_b_a"][0], w["od_b_x"][0], w["od_b_a"][1], w["od_b_x"][1]])
    conv_b = row(w["od_conv_b"])
    ln_g, ln_b = w["ln_g"], w["ln_b"]

    (q, k, v, su, sv, g0), got = _even_proj(x, mod[0], ev_w_in_t, tabs, seq, rid(_gather_rider, ride and ride["ev_w_out"]))
    if ride is not None:
        ev_w_out = got[0].reshape(D_MODEL, D_MODEL)
    (ycat, lse), got = _even_mix(q, k, v, su, sv, sink, sgln_g, sgln_b, sgw, sgb_full, e2, seq,
                                 rid(_gather_rider, ride and ride["od_w_in"]))
    if ride is not None:
        od_w_in = got[0]
    (z0, x1, xr, g1), got = _even_out(ycat, g0, x, mod[0], mod[1], ev_w_out, od_w_in, ln_g[0:1], ln_b[0:1], seq,
                                      rid(_gather_rider, ride and ride["od_w_out"]))
    if ride is not None:
        od_w_out = got[0].reshape(D_MODEL, D_MODEL)
    lru = (w["od_conv_w"], conv_b, wcat, gate_bias, w["od_lam"], seq)
    hf, hpf, *saved_f, xc = _lru_fwd(xr, None, *lru, 0)
    hr, hpr, *saved_r = _lru_fwd(xr, xc, *lru, 1)
    dhs, dg1, dres1, loss, d_od_w_out, vec_o = _odd_out_and_loss(hf, hr, g1, x1, tgt, mod[1], od_w_out, ln_g[1:2], ln_b[1:2], seq)
    dxc_f, dw_f, vec_f = _lru_bwd(xc, dhs, hpf, *saved_f, wcat, w["od_lam"], seq, 0)
    dxc_r, dw_r, vec_r = _lru_bwd(xc, dhs, hpr, *saved_r, wcat, w["od_lam"], seq, 1)
    dx1, d_od_w_in, vec_p = _odd_proj_bwd(dxc_f, dxc_r, xr, dg1, x1, dres1, mod[1], w["od_conv_w"], od_w_in, seq)
    d_od_w_a = jnp.stack([dw_f[:, :, 0:128], dw_r[:, :, 0:128]])
    d_od_w_x = jnp.stack([dw_f[:, :, 128:256], dw_r[:, :, 128:256]])
    od_parts = [d_od_w_in.reshape(4, 2, 512, 512), d_od_w_out.reshape(4, 2, 128, D_MODEL),
                d_od_w_a.reshape(4, 2, 2 * BLK, BLK), d_od_w_x.reshape(4, 2, 2 * BLK, BLK)]
    (dycat, dg0, dres0, d_ev_w_out, vec_e), got_od = _even_out_bwd(dx1, z0, ycat, g0, mod[0], ln_g[0:1], ev_w_out, seq,
                                                                   rid(_sibling_swap_rider, od_parts))
    if ride is not None:
        od_sums = _sum_sibling(ride["core"], od_parts, got_od, [ride["wire"]] * 4, "sum_sibling_od")
    (dq, dsu, dsv, dk, dv, d_sgw, d_sgb, vec_s, d_sink), od_slots = _even_mix_bwd(
        q, k, v, lse, ycat, dycat, su, sv, sink, sgln_g, sgln_b, sgw, sgb_full, e2, e8, seq,
        rid(_chip_exchange_rider, ride and od_sums))
    grad_x, d_ev_w_in_t, vec_x = _even_proj_bwd(dq, dk, dv, dsu, dsv, dg0, x, dres0, mod[0], tabs, ev_w_in_t, seq)

    dmod = jnp.stack([jnp.stack([vec_x[0], vec_x[1], vec_e[2]]), jnp.stack([vec_p[5], vec_p[6], vec_o[2]])])
    grads = {
        "ln_g": jnp.stack([vec_e[0], vec_o[0]]), "ln_b": jnp.stack([vec_e[1], vec_o[1]]),
        "ev_w_in_t": d_ev_w_in_t, "ev_w_out": d_ev_w_out, "ev_sink": d_sink[:, 0],
        "ev_sg_ln_g": vec_s[0], "ev_sg_ln_b": vec_s[1], "ev_sg_w": d_sgw,
        "ev_sg_b": d_sgb,
        "od_conv_w": vec_p[0:4], "od_conv_b": vec_p[4],
        "od_b_a": jnp.stack([vec_f[0], vec_r[0]]), "od_b_x": jnp.stack([vec_f[1], vec_r[1]]),
        "od_lam": jnp.stack([vec_f[2], vec_r[2]]),
    }
    if ride is None:
        grads.update({"od_w_in": d_od_w_in, "od_w_out": d_od_w_out, "od_w_a": d_od_w_a, "od_w_x": d_od_w_x})
    else:
        grads["od_slots"] = od_slots
    return loss[0, 0], grad_x, dmod, grads


def _allgather8(block, name):
    m_per, n = block.shape

    def body(x_ref, out_ref, send_sems, recv_sems, local_sem):
        x, y, c = _place()
        me, sibling = (x, y, c), (x, y, 1 - c)
        chips = [(1 - x, y), (x, 1 - y), (1 - x, 1 - y)]

        def rows(px, py, pc):
            return out_ref.at[pl.ds((4 * px + 2 * py + pc) * m_per, m_per), :]

        def copy(k, blk, to, src=None):
            return pltpu.make_async_remote_copy(src_ref=rows(*blk) if src is None else src, dst_ref=rows(*blk),
                                                send_sem=send_sems.at[k], recv_sem=recv_sems.at[k], device_id=to,
                                                device_id_type=MESH)

        mine = pltpu.make_async_copy(x_ref, rows(*me), local_sem)
        mine.start()
        first = [copy(0, me, sibling, src=x_ref)] + [copy(1 + j, me, (*chip, c), src=x_ref) for j, chip in enumerate(chips)]
        for cp in first:
            cp.start()
        passed = [copy(4 + j, (*chip, c), sibling) for j, chip in enumerate(chips)]
        for j, chip in enumerate(chips):
            copy(1 + j, (*chip, c), me).wait_recv()
            passed[j].start()
        copy(0, sibling, me).wait_recv()
        for j, chip in enumerate(chips):
            copy(4 + j, (*chip, 1 - c), me).wait_recv()
        for cp in first + passed:
            cp.wait_send()
        mine.wait()

    return pl.pallas_call(
        body, name=name, out_shape=_sds((8 * m_per, n), block.dtype),
        in_specs=[pl.BlockSpec(memory_space=pltpu.VMEM)], out_specs=pl.BlockSpec(memory_space=pltpu.VMEM),
        scratch_shapes=[pltpu.SemaphoreType.DMA((7,)), pltpu.SemaphoreType.DMA((7,)), pltpu.SemaphoreType.DMA],
        compiler_params=pltpu.CompilerParams(vmem_limit_bytes=VMEM_LIMIT),
    )(block)


class _Copies:
    def __init__(self, send_sems, recv_sems, local_sems, stages):
        self.send_sems, self.recv_sems, self.local_sems, self.stages = send_sems, recv_sems, local_sems, stages
        self.sent, self.staged, self.locals = [], [], []

    def remote(self, k, src, dst, to):
        return pltpu.make_async_remote_copy(src_ref=src, dst_ref=dst, send_sem=self.send_sems.at[k], recv_sem=self.recv_sems.at[k],
                                            device_id=to, device_id_type=MESH)

    def send(self, k, src, dst, to):
        cp = self.remote(k, src, dst, to)
        cp.start()
        self.sent.append(cp)

    def arrived(self, k, dst, frm):
        self.remote(k, dst, dst, frm).wait_recv()

    def local(self, src, dst):
        k = len(self.staged)
        cp = pltpu.make_async_copy(src, self.stages[k], self.local_sems.at[2 * k])
        cp.start()
        self.staged.append((cp, dst))

    def flush(self):
        for k in range(len(self.locals), len(self.staged)):
            cp, dst = self.staged[k]
            cp.wait()
            out = pltpu.make_async_copy(self.stages[k], dst, self.local_sems.at[2 * k + 1])
            out.start()
            self.locals.append(out)

    def drain(self):
        self.flush()
        for cp in self.sent:
            cp.wait_send()
        for cp in self.locals:
            cp.wait()


def _comm_call(body, name, ins, out_shapes, n_remote, stages):
    n_in, n_out = len(ins), len(out_shapes)

    def kern(*refs):
        in_refs, out_refs = refs[:n_in], refs[n_in:n_in + n_out]
        send_sems, recv_sems, local_sems = refs[n_in + n_out:n_in + n_out + 3]
        body(_Copies(send_sems, recv_sems, local_sems, refs[n_in + n_out + 3:]), in_refs, out_refs)

    hbm = pl.BlockSpec(memory_space=pl.ANY)
    return pl.pallas_call(
        kern, name=name, out_shape=out_shapes, in_specs=[hbm] * n_in, out_specs=[hbm] * n_out,
        scratch_shapes=[pltpu.SemaphoreType.DMA((n_remote,)), pltpu.SemaphoreType.DMA((n_remote,)),
                        pltpu.SemaphoreType.DMA((2 * len(stages),))] + [pltpu.VMEM(s, d) for s, d in stages],
        compiler_params=pltpu.CompilerParams(vmem_limit_bytes=VMEM_LIMIT),
    )(*ins)


def _gather_to_all(cps, pairs, me, sibling, other_chips, c, base):
    idx = lambda p: 4 * p[0] + 2 * p[1] + p[2]
    for i, (src, dst) in enumerate(pairs):
        cps.local(src, dst.at[idx(me)])
        cps.send(base + 7 * i, src, dst.at[idx(me)], sibling)
        for j, chip in enumerate(other_chips):
            cps.send(base + 7 * i + 1 + j, src, dst.at[idx(me)], (*chip, c))
    cps.flush()
    for j, chip in enumerate(other_chips):
        for i, (_, dst) in enumerate(pairs):
            got = dst.at[idx((*chip, c))]
            cps.arrived(base + 7 * i + 1 + j, got, (*chip, c))
            cps.send(base + 7 * i + 4 + j, got, got, sibling)
    for i, (_, dst) in enumerate(pairs):
        cps.arrived(base + 7 * i, dst.at[idx(sibling)], sibling)
        for j, chip in enumerate(other_chips):
            cps.arrived(base + 7 * i + 4 + j, dst.at[idx((*chip, 1 - c))], sibling)


def _gather_weights(shards, small):
    n = len(shards)

    def body(cps, ins, outs):
        x, y, c = _place()
        me, sibling, mine = (x, y, c), (x, y, 1 - c), 2 * x + y
        chips = [(1 - x, y), (x, 1 - y), (1 - x, 1 - y)]
        for i in range(n):
            cps.local(ins[i], outs[i].at[mine])
        for j, (px, py) in enumerate(chips):
            for i in range(n):
                hr = shards[i].shape[0] // 2
                rows = pl.ds(c * hr, hr)
                cps.send(6 * i + j, ins[i].at[rows], outs[i].at[mine, rows], (px, py, c))
        _gather_to_all(cps, [(ins[n], outs[n])], me, sibling, chips, c, 6 * n)
        for j, (px, py) in enumerate(chips):
            for i in range(n):
                hr = shards[i].shape[0] // 2
                got = outs[i].at[2 * px + py, pl.ds(c * hr, hr)]
                cps.arrived(6 * i + j, got, (px, py, c))
                cps.send(6 * i + 3 + j, got, got, sibling)
        for j, (px, py) in enumerate(chips):
            for i in range(n):
                hr = shards[i].shape[0] // 2
                cps.arrived(6 * i + 3 + j, outs[i].at[2 * px + py, pl.ds((1 - c) * hr, hr)], sibling)
        cps.drain()

    return _comm_call(body, "gather_weights", list(shards) + [small],
                      [_sds((4,) + s.shape, s.dtype) for s in shards] + [_sds((8,) + small.shape, small.dtype)], 6 * n + 7,
                      [(a.shape, a.dtype) for a in list(shards) + [small]])


def _reduce_sibling(parts, dmod_rows):
    n = len(parts)

    def body(cps, ins, outs):
        x, y, c = _place()
        me, sibling = (x, y, c), (x, y, 1 - c)
        chips = [(1 - x, y), (x, 1 - y), (1 - x, 1 - y)]
        for i in range(n):
            cps.send(i, ins[i].at[:, 1 - c], outs[i], sibling)
        _gather_to_all(cps, [(ins[n], outs[n])], me, sibling, chips, c, n)
        for i in range(n):
            cps.arrived(i, outs[i], sibling)
        cps.drain()

    return _comm_call(body, "reduce_sibling", list(parts) + [dmod_rows],
                      [_sds((4,) + p.shape[2:], p.dtype) for p in parts] + [_sds((8,) + dmod_rows.shape, dmod_rows.dtype)], n + 7,
                      [(dmod_rows.shape, dmod_rows.dtype)])


def _gather_reduced(shard_parts, repl_parts):
    ns, nr = len(shard_parts), len(repl_parts)

    def body(cps, ins, outs):
        x, y, c = _place()
        me, sibling = (x, y, c), (x, y, 1 - c)
        chips = [(1 - x, y), (x, 1 - y), (1 - x, 1 - y)]
        for i in range(ns):
            cps.local(ins[i], outs[i].at[c])
            cps.send(i, ins[i], outs[i].at[c], sibling)
        _gather_to_all(cps, [(ins[ns + i], outs[ns + i]) for i in range(nr)], me, sibling, chips, c, ns)
        for i in range(ns):
            cps.arrived(i, outs[i].at[1 - c], sibling)
        cps.drain()

    return _comm_call(body, "gather_reduced", list(shard_parts) + list(repl_parts),
                      [_sds((2,) + p.shape, p.dtype) for p in shard_parts] + [_sds((8,) + p.shape, p.dtype) for p in repl_parts],
                      ns + 7 * nr, [(p.shape, p.dtype) for p in list(shard_parts) + list(repl_parts)])


def _sum_sibling(core, parts, got, wire, name):
    n = len(parts)

    def body(core_ref, *refs):
        for i in range(n):
            refs[2 * n + i][0] = (refs[i][0] + refs[n + i][0]).astype(wire[i])

    keep_spec = lambda p: pl.BlockSpec((1, None) + p.shape[2:], lambda s, core_ref: (s, core_ref[0], 0, 0))
    slot_spec = lambda p: pl.BlockSpec((1,) + p.shape[2:], lambda s, core_ref: (s, 0, 0))
    return pl.pallas_call(
        body, name=name,
        grid_spec=pltpu.PrefetchScalarGridSpec(
            num_scalar_prefetch=1, grid=(4,), in_specs=[keep_spec(p) for p in parts] + [slot_spec(p) for p in parts],
            out_specs=[slot_spec(p) for p in parts]),
        out_shape=[_sds((4,) + p.shape[2:], wire[i]) for i, p in enumerate(parts)],
        compiler_params=_params("parallel"),
    )(core, *parts, *got)


def _sum_slots(slots, name):
    n = len(slots)

    def spec_pair(p):
        k, rows, cols = p.shape
        sub = 16 if p.dtype == BF16 else 8
        if (rows // 2) % sub == 0:
            return pl.BlockSpec((k, rows // 2, cols), lambda i: (0, i, 0)), pl.BlockSpec((rows // 2, cols), lambda i: (i, 0))
        return pl.BlockSpec((k, rows, cols), lambda i: (0, 0, 0)), pl.BlockSpec((rows, cols), lambda i: (0, 0))

    pairs = [spec_pair(p) for p in slots]

    def body(*refs):
        for i in range(n):
            acc = refs[i][0].astype(F32)
            for j in range(1, slots[i].shape[0]):
                acc = acc + refs[i][j].astype(F32)
            refs[n + i][...] = acc

    return pl.pallas_call(
        body, name=name, grid=(2,), in_specs=[a for a, _ in pairs], out_specs=[b for _, b in pairs],
        out_shape=[_sds(p.shape[1:]) for p in slots], compiler_params=_params("arbitrary"),
    )(*slots)


def _modulation(c_all, ada_w, ada_b):
    cols = ada_w.shape[2]

    def body(c_ref, w_ref, b_ref, o_ref):
        cc = c_ref[...]
        o_ref[0] = _mm(cc * _sigmoid(cc), w_ref[0]) + b_ref[0]

    return pl.pallas_call(
        body, name="modulation", grid=(2,),
        in_specs=[_full((8, D_MODEL)), pl.BlockSpec((1, D_MODEL, cols), lambda l: (l, 0, 0)), pl.BlockSpec((1, 1, cols), lambda l: (l, 0, 0))],
        out_specs=pl.BlockSpec((1, 8, cols), lambda l: (l, 0, 0)), out_shape=_sds((2, 8, cols)),
        compiler_params=_params("parallel"),
    )(c_all, ada_w, ada_b)


def _adamw_math(w, g, m, v):
    m = ADAM_B1 * m + (1.0 - ADAM_B1) * g
    v = ADAM_B2 * v + (1.0 - ADAM_B2) * (g * g)
    m_hat = m / (1.0 - ADAM_B1 ** ADAM_STEP)
    v_hat = v / (1.0 - ADAM_B2 ** ADAM_STEP)
    delta = -ADAM_LR * (m_hat / (jnp.sqrt(v_hat) + ADAM_EPS) + ADAM_WD * w)
    return delta, m, v


def _ada_update(c_all, dmod, w, m, v, rider=None):
    cols = w.shape[2]
    tr = 256
    per = D_MODEL // tr
    spec3 = pl.BlockSpec((1, tr, cols), lambda i: (i // per, i % per, 0))

    def body(c_ref, d_ref, w_ref, m_ref, v_ref, g_ref, dl_ref, nm_ref, nv_ref):
        cc = c_ref[...]
        g = _mm_tn(cc * _sigmoid(cc), d_ref[0])
        g_ref[0] = g
        dl_ref[0], nm_ref[0], nv_ref[0] = _adamw_math(w_ref[0], g, m_ref[0], v_ref[0])

    return _call(
        body, "ada_update", (2 * per,),
        [pl.BlockSpec((8, tr), lambda i: (0, i % per)), pl.BlockSpec((1, 8, cols), lambda i: (i // per, 0, 0)), spec3, spec3, spec3],
        [spec3] * 4, [_sds(w.shape)] * 4, (c_all, dmod, w, m, v), "parallel", rider=rider)


def _adamw_matrices(params):
    n = len(params)
    steps = 8

    def body(*refs):
        ins, outs = refs[:4 * n], refs[4 * n:]
        for j in range(n):
            w_ref, g_ref, m_ref, v_ref = ins[4 * j:4 * j + 4]
            outs[3 * j][...], outs[3 * j + 1][...], outs[3 * j + 2][...] = _adamw_math(w_ref[...], g_ref[...], m_ref[...], v_ref[...])

    spec = lambda p: _rows(p[0].shape[0] // steps, p[0].shape[1])
    res = pl.pallas_call(
        body, name="adamw_matrices", grid=(steps,), in_specs=[spec(p) for p in params for _ in range(4)],
        out_specs=[spec(p) for p in params for _ in range(3)], out_shape=[_sds(p[0].shape) for p in params for _ in range(3)],
        compiler_params=_params("parallel"),
    )(*[a for p in params for a in p])
    return [tuple(res[3 * j:3 * j + 3]) for j in range(n)]


def _adamw_small(params):
    n = len(params)

    def body(*refs):
        ins, outs = refs[:4 * n], refs[4 * n:]
        for j in range(n):
            w_ref, g_ref, m_ref, v_ref = ins[4 * j:4 * j + 4]
            outs[3 * j][...], outs[3 * j + 1][...], outs[3 * j + 2][...] = _adamw_math(w_ref[...], g_ref[...], m_ref[...], v_ref[...])

    flat = [a for p in params for a in p]
    res = pl.pallas_call(body, name="adamw_small", out_shape=[_sds(p[0].shape) for p in params for _ in range(3)])(*flat)
    return [tuple(res[3 * j:3 * j + 3]) for j in range(n)]


def _cols(a, start, size):
    return lax.dynamic_slice_in_dim(a, start, size, axis=a.ndim - 1)


def kernel(x, c, positions, ada_w, ada_b, ln_g, ln_b, ev_w_in, ev_w_out, ev_sink, ev_sg_ln_g, ev_sg_ln_b, ev_sg_w, ev_sg_b, od_w_in, od_conv_w, od_conv_b, od_w_a, od_b_a, od_w_x, od_b_x, od_lam, od_w_out, loss_target, m_ada_w, m_ada_b, m_ln_g, m_ln_b, m_ev_w_in, m_ev_w_out, m_ev_sink, m_ev_sg_ln_g, m_ev_sg_ln_b, m_ev_sg_w, m_ev_sg_b, m_od_w_in, m_od_conv_w, m_od_conv_b, m_od_w_a, m_od_b_a, m_od_w_x, m_od_b_x, m_od_lam, m_od_w_out, v_ada_w, v_ada_b, v_ln_g, v_ln_b, v_ev_w_in, v_ev_w_out, v_ev_sink, v_ev_sg_ln_g, v_ev_sg_ln_b, v_ev_sg_w, v_ev_sg_b, v_od_w_in, v_od_conv_w, v_od_conv_b, v_od_w_a, v_od_b_a, v_od_w_x, v_od_b_x, v_od_lam, v_od_w_out):
    seq = x.shape[1]
    px, py, pc = _place()
    chip = 2 * px + py
    dev = 2 * chip + pc

    small = jnp.concatenate([od_conv_w[0].reshape(-1), od_conv_b[0], od_b_a[0].reshape(-1), jnp.zeros((256,), F32),
                             od_b_x[0].reshape(-1), od_lam[0].reshape(-1)]).reshape(3, D_MODEL)
    blk = jnp.concatenate([c, small, jnp.zeros((4, D_MODEL), F32)], axis=0)
    tr = lambda a: jnp.swapaxes(a, -1, -2)
    wire_w = lambda a: a.astype(MXU_DTYPE)
    ev_w_in4, g_small = _gather_weights([wire_w(tr(ev_w_in[0]))], blk)
    core = pc.astype(jnp.int32).reshape(1)
    ride = {"ev_w_out": wire_w(ev_w_out[0]), "od_w_in": wire_w(od_w_in[0]), "od_w_out": wire_w(od_w_out[0]),
            "core": core, "wire": MXU_DTYPE}
    c_all = g_small[:, 0, :]
    per_chip = g_small[0::2]
    conv_w = per_chip[:, 1].reshape(4, 4, 256).transpose(1, 0, 2).reshape(4, D_MODEL)
    conv_b = per_chip[:, 2, 0:256].reshape(D_MODEL)
    b_a = per_chip[:, 2, 256:768].reshape(4, 2, 256).transpose(1, 0, 2).reshape(2, D_MODEL)
    b_x = per_chip[:, 3, 0:512].reshape(4, 2, 256).transpose(1, 0, 2).reshape(2, D_MODEL)
    lam = per_chip[:, 3, 512:1024].reshape(4, 2, 256).transpose(1, 0, 2).reshape(2, D_MODEL)

    w_full = {
        "ev_w_in_t": ev_w_in4.reshape(EVEN_IN, D_MODEL),
        "ev_sink": ev_sink[0], "ev_sg_ln_g": ev_sg_ln_g[0], "ev_sg_ln_b": ev_sg_ln_b[0], "ev_sg_w": ev_sg_w[0],
        "ev_sg_b": ev_sg_b[0], "od_conv_w": conv_w, "od_conv_b": conv_b, "od_w_a": od_w_a[0], "od_b_a": b_a,
        "od_w_x": od_w_x[0], "od_b_x": b_x, "od_lam": lam, "ln_g": ln_g, "ln_b": ln_b,
    }

    ada_cols = ada_w.shape[2]
    mod_sh = _modulation(c_all, ada_w, _cols(ada_b, chip * ada_cols, ada_cols).reshape(2, 1, ada_cols))
    mod_all = _allgather8(mod_sh.reshape(16, ada_cols), "gather_mod").reshape(4, 2, 2, 8, ada_cols)[:, 0]
    mod_mine = lax.dynamic_index_in_dim(mod_all, dev, axis=2, keepdims=False)
    mod = mod_mine.transpose(1, 0, 2).reshape(2, 3, D_MODEL)

    posf = positions.astype(F32).reshape(seq, 1)
    loss_local, grad_x, dmod, g = _local_step(x[0], posf, loss_target[0], mod, w_full, seq, ride)

    pad = lambda a, n: jnp.concatenate([a.reshape(-1), jnp.zeros((n - a.size,), F32)])
    rows_small = jnp.concatenate([
        dmod.reshape(6, D_MODEL), g["ln_g"][0:1], g["ln_b"][0:1], g["ln_g"][1:2], g["ln_b"][1:2],
        jnp.concatenate([g["ev_sg_ln_g"], g["ev_sg_ln_b"]]).reshape(1, D_MODEL), g["ev_sg_b"].reshape(1, D_MODEL),
        g["od_conv_w"], g["od_conv_b"].reshape(1, D_MODEL), g["od_b_a"], g["od_b_x"], g["od_lam"],
        pad(g["ev_sink"], D_MODEL).reshape(1, D_MODEL), pad(loss_local, D_MODEL).reshape(1, D_MODEL),
        jnp.zeros((39, D_MODEL), F32)], axis=0)
    parts = [g["ev_w_in_t"].reshape(4, 2, 352, D_MODEL), g["ev_w_out"].reshape(4, 2, 128, D_MODEL),
             g["ev_sg_w"].reshape(4, 2, BLK, BLK), rows_small.reshape(4, 2, 8, D_MODEL)]
    wire = [MXU_DTYPE] * 3 + [F32]
    dmod_blk = jnp.concatenate([dmod.reshape(6, D_MODEL), jnp.zeros((2, D_MODEL), F32)], axis=0)
    *got, dmod_gathered = _reduce_sibling(parts, dmod_blk)
    dmod_all = dmod_gathered[:, 0:6].reshape(8, 2, 3 * D_MODEL)
    dmod_sh = _cols(dmod_all, chip * ada_cols, ada_cols).transpose(1, 0, 2)
    (g_ada_w, d_ada_w, nm_ada_w, nv_ada_w), ev_slots = _ada_update(
        c_all, dmod_sh, ada_w, m_ada_w, v_ada_w, _chip_exchange_rider(_sum_sibling(core, parts, got, wire, "sum_sibling")))
    od_slots = list(g["od_slots"])
    mine = _sum_slots(ev_slots[0:2] + od_slots[0:2] + ev_slots[2:3] + od_slots[2:4] + ev_slots[3:4], "sum_chips")
    reduced = _gather_reduced(mine[:4], mine[4:])
    g_ev_w_in_t = reduced[0].reshape(704, D_MODEL)
    g_ev_w_out = reduced[1].reshape(256, D_MODEL)
    g_od_w_in = reduced[2].reshape(D_MODEL, 512)
    g_od_w_out = reduced[3].reshape(256, D_MODEL)
    g_sg_w = reduced[4].reshape(8 * BLK, BLK)
    g_w_a = reduced[5].reshape(16 * BLK, BLK)
    g_w_x = reduced[6].reshape(16 * BLK, BLK)
    gs = reduced[7].reshape(64, D_MODEL)
    loss = gs[24, 0]

    mats = (("ev_w_out", ev_w_out, g_ev_w_out, m_ev_w_out, v_ev_w_out), ("od_w_in", od_w_in, g_od_w_in, m_od_w_in, v_od_w_in),
            ("od_w_out", od_w_out, g_od_w_out, m_od_w_out, v_od_w_out), ("ev_sg_w", ev_sg_w, g_sg_w, m_ev_sg_w, v_ev_sg_w),
            ("od_w_a", od_w_a, g_w_a, m_od_w_a, v_od_w_a), ("od_w_x", od_w_x, g_w_x, m_od_w_x, v_od_w_x))
    upd = _adamw_matrices([(tr(ev_w_in[0]), g_ev_w_in_t, tr(m_ev_w_in[0]), tr(v_ev_w_in[0]))]
                          + [(w_.reshape(g_.shape), g_, m_.reshape(g_.shape), v_.reshape(g_.shape)) for _, w_, g_, m_, v_ in mats])
    big = {"ev_w_in": tuple(tr(a).reshape(ev_w_in.shape) for a in (g_ev_w_in_t, *upd[0]))}
    for (name, w_, g_, _, _), u in zip(mats, upd[1:]):
        big[name] = tuple(a.reshape(w_.shape) for a in (g_, *u))
    big["ada_w"] = (g_ada_w, d_ada_w, nm_ada_w, nv_ada_w)

    sh = lambda a: _cols(a, chip * 256, 256)
    small_g = {
        "ada_b": gs[0:6].reshape(2, 3 * D_MODEL), "ln_g": jnp.stack([gs[6], gs[8]]), "ln_b": jnp.stack([gs[7], gs[9]]),
        "ev_sink": gs[23:24, 0:8], "ev_sg_ln_g": gs[10:11, 0:512], "ev_sg_ln_b": gs[10:11, 512:1024],
        "ev_sg_b": gs[11].reshape(8, BLK), "od_conv_w": sh(gs[12:16]), "od_conv_b": sh(gs[16:17]), "od_b_a": sh(gs[17:19]),
        "od_b_x": sh(gs[19:21]), "od_lam": sh(gs[21:23]),
    }
    small_in = {"ada_b": (ada_b, m_ada_b, v_ada_b), "ln_g": (ln_g, m_ln_g, v_ln_g), "ln_b": (ln_b, m_ln_b, v_ln_b),
                "ev_sink": (ev_sink, m_ev_sink, v_ev_sink), "ev_sg_ln_g": (ev_sg_ln_g, m_ev_sg_ln_g, v_ev_sg_ln_g),
                "ev_sg_ln_b": (ev_sg_ln_b, m_ev_sg_ln_b, v_ev_sg_ln_b), "ev_sg_b": (ev_sg_b, m_ev_sg_b, v_ev_sg_b),
                "od_conv_w": (od_conv_w, m_od_conv_w, v_od_conv_w), "od_conv_b": (od_conv_b, m_od_conv_b, v_od_conv_b),
                "od_b_a": (od_b_a, m_od_b_a, v_od_b_a), "od_b_x": (od_b_x, m_od_b_x, v_od_b_x),
                "od_lam": (od_lam, m_od_lam, v_od_lam)}
    names_small = list(small_g)
    upd = _adamw_small([(small_in[n][0].reshape(small_g[n].shape), small_g[n], small_in[n][1].reshape(small_g[n].shape),
                         small_in[n][2].reshape(small_g[n].shape)) for n in names_small])
    res = dict(big)
    for n, (d_, nm_, nv_) in zip(names_small, upd):
        shape = small_in[n][0].shape
        res[n] = tuple(a.reshape(shape) for a in (small_g[n], d_, nm_, nv_))

    order = ["ada_w", "ada_b", "ln_g", "ln_b", "ev_w_in", "ev_w_out", "ev_sink", "ev_sg_ln_g", "ev_sg_ln_b", "ev_sg_w", "ev_sg_b",
             "od_w_in", "od_conv_w", "od_conv_b", "od_w_a", "od_b_a", "od_w_x", "od_b_x", "od_lam", "od_w_out"]
    return (loss, grad_x.reshape(x.shape), *[res[n][0] for n in order], *[res[n][1] for n in order],
            *[res[n][2] for n in order], *[res[n][3] for n in order])
```

```python
import jax
import jax.numpy as jnp
import numpy as np
from jax import lax
from jax.experimental import pallas as pl
from jax.experimental.pallas import tpu as pltpu

F32 = jnp.float32
BF16 = jnp.bfloat16
MXU_DTYPE = BF16
ACT_DTYPE = MXU_DTYPE

D_MODEL = 1024
HEAD_DIM = 64
N_Q_HEADS = 8
Q_PER_KV = 4
ATTN_WIDTH = 512
BLK = 128
ROPE_DIM = 16
ROPE_THETA = 500000.0
N_SG_GROUPS = 8
SG_WIDTH = 512
EVEN_IN = 2816
ODD_IN = 2048
RNN_HEADS = 8
RG_LRU_C = 8.0
ALPHA = (2 * 2) ** 0.25
LN_EPS = 1e-5
NEG_INF = -1e30
ADAM_LR, ADAM_B1, ADAM_B2, ADAM_EPS, ADAM_WD, ADAM_STEP = 0.001, 0.9, 0.999, 1e-08, 0.01, 10

LANES = 128
VMEM_LIMIT = 56 * 1024 * 1024
MESH = pl.DeviceIdType.MESH


def _mm(a, b):
    return jnp.dot(a.astype(MXU_DTYPE), b.astype(MXU_DTYPE), preferred_element_type=F32)


def _mm_nt(a, b):
    return lax.dot_general(a.astype(MXU_DTYPE), b.astype(MXU_DTYPE), (((1,), (1,)), ((), ())), preferred_element_type=F32)


def _mm_tn(a, b):
    return lax.dot_general(a.astype(MXU_DTYPE), b.astype(MXU_DTYPE), (((0,), (0,)), ((), ())), preferred_element_type=F32)


def _sigmoid(x):
    return 1.0 / (1.0 + jnp.exp(-x))


def _ln_stats(z):
    mu = jnp.mean(z, axis=-1, keepdims=True)
    d = z - mu
    var = jnp.mean(d * d, axis=-1, keepdims=True)
    rstd = lax.rsqrt(var + LN_EPS)
    return d * rstd, rstd


def _ln_bwd(dout, zhat, rstd, g):
    dzh = dout * g
    m1 = jnp.mean(dzh, axis=-1, keepdims=True)
    m2 = jnp.mean(dzh * zhat, axis=-1, keepdims=True)
    return rstd * (dzh - m1 - zhat * m2)


def _group_sum(x, e2):
    hi = x.astype(MXU_DTYPE)
    lo = (x - hi.astype(F32)).astype(MXU_DTYPE)
    return jnp.dot(hi, e2, preferred_element_type=F32) + jnp.dot(lo, e2, preferred_element_type=F32)


def _lane_iota(shape):
    return lax.broadcasted_iota(jnp.int32, shape, 1)


def _to_kv_lanes(t, h):
    src_lo = (h % 2 == 0)
    dst_lo = (h // Q_PER_KV == 0)
    if src_lo != dst_lo:
        t = pltpu.roll(t, HEAD_DIM, 1)
    lane = _lane_iota(t.shape)
    keep = (lane < HEAD_DIM) if dst_lo else (lane >= HEAD_DIM)
    return jnp.where(keep, t, 0.0)


def _from_kv_lanes(t, h):
    src_lo = (h // Q_PER_KV == 0)
    dst_lo = (h % 2 == 0)
    lane = _lane_iota(t.shape)
    keep = (lane < HEAD_DIM) if src_lo else (lane >= HEAD_DIM)
    t = jnp.where(keep, t, 0.0)
    if src_lo != dst_lo:
        t = pltpu.roll(t, HEAD_DIM, 1)
    return t


def _rope(t, cos_t, sin_p, sin_m):
    half = ROPE_DIM // 2
    return t * cos_t + pltpu.roll(t, half, 1) * sin_p + pltpu.roll(t, LANES - half, 1) * sin_m


def _rope_t(d, cos_t, sin_p, sin_m):
    half = ROPE_DIM // 2
    return d * cos_t + pltpu.roll(d * sin_p, LANES - half, 1) + pltpu.roll(d * sin_m, half, 1)


def _band(ref, n, nb):
    prev = jnp.maximum(n - 1, 0)
    nxt = jnp.minimum(n + 1, nb - 1)
    rows = [ref[pl.ds(pl.multiple_of(j * BLK, BLK), BLK), :] for j in (prev, n, nxt)]
    return jnp.concatenate(rows, axis=0)


def _band_bias(n, seq):
    qi = lax.broadcasted_iota(jnp.int32, (BLK, 3 * BLK), 0)
    kj = lax.broadcasted_iota(jnp.int32, (BLK, 3 * BLK), 1)
    k_abs = n * BLK - BLK + kj
    valid = (jnp.abs(kj - BLK - qi) <= BLK) & (k_abs >= 0) & (k_abs < seq)
    bias = jnp.where(valid, 0.0, NEG_INF)
    return jnp.concatenate([bias] * Q_PER_KV, axis=0)


def _stack_heads(tile_of, kv):
    return jnp.concatenate([_to_kv_lanes(tile_of(h // 2), h) for h in range(Q_PER_KV * kv, Q_PER_KV * (kv + 1))], axis=0)


def _per_head_column(vals):
    row = lax.broadcasted_iota(jnp.int32, (Q_PER_KV * BLK, 1), 0)
    return jnp.where(row < BLK, vals[0], jnp.where(row < 2 * BLK, vals[1], jnp.where(row < 3 * BLK, vals[2], vals[3])))


def _softplus_neg(lam):
    e = jnp.exp(-jnp.abs(lam))
    u = 1.0 + e
    log1p_e = jnp.where(u == 1.0, e, jnp.log(u) * (e / (u - 1.0)))
    sp = jnp.maximum(-lam, 0.0) + log1p_e
    dsp = -1.0 / (1.0 + jnp.exp(lam))
    return sp, dsp


def _full(shape):
    return pl.BlockSpec(shape, lambda *_: (0,) * len(shape))


def _const(shape):
    return pl.BlockSpec(shape, lambda *_: (0,) * len(shape), pipeline_mode=pl.Buffered(1))


def _rows(tm, n):
    return pl.BlockSpec((tm, n), lambda i: (i, 0))


def _params(*sem):
    return pltpu.CompilerParams(dimension_semantics=sem, vmem_limit_bytes=VMEM_LIMIT)


def _sds(shape, dtype=F32):
    return jax.ShapeDtypeStruct(shape, dtype)


def _place():
    return lax.axis_index("x"), lax.axis_index("y"), lax.axis_index("c")


class _Rider:
    def __init__(self, ins, out_shapes, n_remote, n_local, plan):
        self.ins, self.out_shapes, self.n_remote, self.n_local, self.plan = list(ins), list(out_shapes), n_remote, n_local, plan

    def scratch(self):
        return [pltpu.SemaphoreType.DMA((self.n_remote,)), pltpu.SemaphoreType.DMA((self.n_remote,)),
                pltpu.SemaphoreType.DMA((max(self.n_local, 1),))]

    def run(self, first, in_refs, out_refs, sems):
        send_sems, recv_sems, local_sems = sems
        sends, recvs, locals_ = self.plan(in_refs, out_refs)
        remote = lambda k, src, dst, to: pltpu.make_async_remote_copy(
            src_ref=src, dst_ref=dst, send_sem=send_sems.at[k], recv_sem=recv_sems.at[k], device_id=to, device_id_type=MESH)
        if first:
            for k, src, dst, to in sends:
                remote(k, src, dst, to).start()
            for j, (src, dst) in enumerate(locals_):
                pltpu.make_async_copy(src, dst, local_sems.at[j]).start()
        else:
            for k, dst, frm in recvs:
                remote(k, dst, dst, frm).wait_recv()
            for k, src, dst, to in sends:
                remote(k, src, dst, to).wait_send()
            for j, (src, dst) in enumerate(locals_):
                pltpu.make_async_copy(src, dst, local_sems.at[j]).wait()


def _other_chips(x, y):
    return [(1 - x, y), (x, 1 - y), (1 - x, 1 - y)]


def _gather_rider(shard):
    hr = shard.shape[0] // 2

    def plan(ins, outs):
        x, y, c = _place()
        mine, src, dst = 2 * x + y, ins[0], outs[0]
        sends, recvs = [], []
        for j, (px, py) in enumerate(_other_chips(x, y)):
            for flip in range(2):
                tc = c if flip == 0 else 1 - c
                sends.append((2 * j + flip, src.at[pl.ds(c * hr, hr)], dst.at[mine, pl.ds(c * hr, hr)], (px, py, tc)))
                recvs.append((2 * j + flip, dst.at[2 * px + py, pl.ds(tc * hr, hr)], (px, py, tc)))
        return sends, recvs, [(src, dst.at[mine])]

    return _Rider([shard], [_sds((4,) + shard.shape, shard.dtype)], 6, 1, plan)


def _sibling_swap_rider(parts):
    n = len(parts)

    def plan(ins, outs):
        x, y, c = _place()
        sibling = (x, y, 1 - c)
        return ([(i, ins[i].at[:, 1 - c], outs[i], sibling) for i in range(n)], [(i, outs[i], sibling) for i in range(n)], [])

    return _Rider(parts, [_sds((4,) + p.shape[2:], p.dtype) for p in parts], n, 0, plan)


def _chip_exchange_rider(parts):
    n = len(parts)

    def plan(ins, outs):
        x, y, c = _place()
        mine = 2 * x + y
        sends, recvs = [], []
        for i in range(n):
            for j, (px, py) in enumerate(_other_chips(x, y)):
                sends.append((3 * i + j, ins[i].at[2 * px + py], outs[i].at[mine], (px, py, c)))
                recvs.append((3 * i + j, outs[i].at[2 * px + py], (px, py, c)))
        return sends, recvs, [(ins[i].at[mine], outs[i].at[mine]) for i in range(n)]

    return _Rider(parts, [_sds(p.shape, p.dtype) for p in parts], 3 * n, n, plan)


def _call(body, name, grid, in_specs, out_specs, out_shape, args, sem, scratch=(), rider=None):
    if rider is None:
        return list(pl.pallas_call(body, name=name, grid=grid, in_specs=in_specs, out_specs=out_specs, out_shape=out_shape,
                                   scratch_shapes=list(scratch), compiler_params=_params(sem))(*args)), []
    n_in, n_out, n_scr = len(in_specs), len(out_specs), len(scratch)
    r_in, r_out = len(rider.ins), len(rider.out_shapes)
    steps = grid[0]

    def riding(*refs):
        ins, r_ins = refs[:n_in], refs[n_in:n_in + r_in]
        outs = refs[n_in + r_in:n_in + r_in + n_out]
        r_outs = refs[n_in + r_in + n_out:n_in + r_in + n_out + r_out]
        scr = refs[n_in + r_in + n_out + r_out:n_in + r_in + n_out + r_out + n_scr]
        sems = refs[n_in + r_in + n_out + r_out + n_scr:]

        @pl.when(pl.program_id(0) == 0)
        def _():
            rider.run(True, r_ins, r_outs, sems)

        body(*ins, *outs, *scr)

        @pl.when(pl.program_id(0) == steps - 1)
        def _():
            rider.run(False, r_ins, r_outs, sems)

    hbm = pl.BlockSpec(memory_space=pl.ANY)
    res = pl.pallas_call(
        riding, name=name, grid=grid, in_specs=list(in_specs) + [hbm] * r_in, out_specs=list(out_specs) + [hbm] * r_out,
        out_shape=list(out_shape) + rider.out_shapes, scratch_shapes=list(scratch) + rider.scratch(),
        compiler_params=_params("arbitrary"),
    )(*args, *rider.ins)
    return list(res[:n_out]), list(res[n_out:])


def _row_tile(seq, want):
    return want if seq % want == 0 else seq


def _rope_tables(posf, seq):
    half = ROPE_DIM // 2
    inv_freq = np.power(np.float32(ROPE_THETA), -np.arange(half, dtype=np.float32) / np.float32(half)).astype(np.float32)
    j = np.arange(LANES) % HEAD_DIM
    invf = jnp.asarray(np.where(j < ROPE_DIM, inv_freq[j % half], 0.0).astype(np.float32).reshape(1, LANES))
    m_p = jnp.asarray(((j >= half) & (j < ROPE_DIM)).astype(np.float32).reshape(1, LANES))
    m_m = jnp.asarray(-(j < half).astype(np.float32).reshape(1, LANES))
    tm = _row_tile(seq, 512)

    def body(pos_ref, invf_ref, mp_ref, mm_ref, cos_ref, sp_ref, sm_ref):
        ang = pos_ref[...] * invf_ref[...]
        s = jnp.sin(ang)
        cos_ref[...] = jnp.cos(ang)
        sp_ref[...] = s * mp_ref[...]
        sm_ref[...] = s * mm_ref[...]

    return pl.pallas_call(
        body, name="rope_tables", grid=(seq // tm,),
        in_specs=[_rows(tm, 1), _full((1, LANES)), _full((1, LANES)), _full((1, LANES))],
        out_specs=[_rows(tm, LANES)] * 3, out_shape=[_sds((seq, LANES))] * 3,
        compiler_params=_params("parallel"),
    )(posf, invf, m_p, m_m)


def _even_proj(x, mod, w_in_t, tabs, seq, rider=None):
    tm = _row_tile(seq, 512)

    def body(x_ref, mod_ref, w_ref, cos_ref, sp_ref, sm_ref, q_ref, k_ref, v_ref, su_ref, sv_ref, g_ref):
        h = x_ref[...] * (1.0 + mod_ref[1:2, :]) + mod_ref[0:1, :]
        p = _mm_nt(h, w_ref[...])
        cos_t, sin_p, sin_m = cos_ref[...], sp_ref[...], sm_ref[...]
        for j in range(ATTN_WIDTH // LANES):
            q_ref[:, j * LANES:(j + 1) * LANES] = _rope(p[:, j * LANES:(j + 1) * LANES], cos_t, sin_p, sin_m).astype(q_ref.dtype)
        k_ref[...] = _rope(p[:, 512:640], cos_t, sin_p, sin_m).astype(k_ref.dtype)
        v_ref[...] = p[:, 640:768].astype(v_ref.dtype)
        su_ref[...] = p[:, 768:1280].astype(su_ref.dtype)
        sv_ref[...] = p[:, 1280:1792].astype(sv_ref.dtype)
        g_ref[...] = p[:, 1792:2816].astype(g_ref.dtype)

    return _call(
        body, "even_proj", (seq // tm,),
        [_rows(tm, D_MODEL), _full((3, D_MODEL)), _const((EVEN_IN, D_MODEL))] + [_rows(tm, LANES)] * 3,
        [_rows(tm, 512), _rows(tm, LANES), _rows(tm, LANES), _rows(tm, 512), _rows(tm, 512), _rows(tm, D_MODEL)],
        [_sds((seq, 512), MXU_DTYPE), _sds((seq, LANES), MXU_DTYPE), _sds((seq, LANES), MXU_DTYPE), _sds((seq, 512), ACT_DTYPE),
         _sds((seq, 512), ACT_DTYPE), _sds((seq, D_MODEL), ACT_DTYPE)],
        (x, mod, w_in_t, *tabs), "parallel", rider=rider)


def _sg_forward(sv, lng, lnb, sgw_ref, sgb, e2):
    vn, vhat, rstd, svo = [], [], [], []
    for j in range(SG_WIDTH // LANES):
        t = sv[:, j * LANES:(j + 1) * LANES]
        mu = _group_sum(t, e2) * (1.0 / HEAD_DIM)
        d = t - mu
        var = _group_sum(d * d, e2) * (1.0 / HEAD_DIM)
        r = lax.rsqrt(var + LN_EPS)
        vh = d * r
        vhat.append(vh)
        rstd.append(r)
        vn.append(vh * lng[:, j * LANES:(j + 1) * LANES] + lnb[:, j * LANES:(j + 1) * LANES])
    lane = _lane_iota((BLK, LANES))
    for j in range(SG_WIDTH // LANES):
        lo = _mm(sgw_ref[2 * j], vn[j])
        hi = _mm(sgw_ref[2 * j + 1], vn[j])
        svo.append(jnp.where(lane < HEAD_DIM, lo, hi) + sgb[:, j * LANES:(j + 1) * LANES])
    return svo, vn, vhat, rstd


def _even_mix(q, k, v, su, sv, sink, sgln_g, sgln_b, sgw, sgb_full, e2, seq, rider=None):
    nb = seq // BLK

    def body(sink_ref, q_ref, k_ref, v_ref, su_ref, sv_ref, lng_ref, lnb_ref, sgw_ref, sgb_ref, e2_ref, ycat_ref, lse_ref):
        n = pl.program_id(0)
        kband = _band(k_ref, n, nb)
        vband = _band(v_ref, n, nb)
        bias = _band_bias(n, seq)
        lane = _lane_iota((BLK, LANES))
        lse = jnp.zeros((BLK, LANES), F32)
        q_tile = lambda j: q_ref[:, j * LANES:(j + 1) * LANES].astype(F32)
        acc = [jnp.zeros((BLK, LANES), F32) for _ in range(ATTN_WIDTH // LANES)]
        for kv in range(N_Q_HEADS // Q_PER_KV):
            heads = range(Q_PER_KV * kv, Q_PER_KV * (kv + 1))
            sink = _per_head_column([sink_ref[h] for h in heads])
            s = _mm_nt(_stack_heads(q_tile, kv), kband) * (HEAD_DIM ** -0.5) + bias
            m = jnp.maximum(jnp.max(s, axis=1, keepdims=True), sink)
            p = jnp.exp(s - m)
            denom = jnp.sum(p, axis=1, keepdims=True) + jnp.exp(sink - m)
            o4 = _mm(p / denom, vband)
            l4 = m + jnp.log(denom)
            for g, h in enumerate(heads):
                acc[h // 2] = acc[h // 2] + _from_kv_lanes(o4[g * BLK:(g + 1) * BLK], h)
                lse = jnp.where(lane == h, l4[g * BLK:(g + 1) * BLK], lse)
        for j in range(ATTN_WIDTH // LANES):
            ycat_ref[:, j * LANES:(j + 1) * LANES] = acc[j].astype(ycat_ref.dtype)
        lse_ref[...] = lse
        svo, _, _, _ = _sg_forward(sv_ref[...].astype(F32), lng_ref[...], lnb_ref[...], sgw_ref, sgb_ref[...], e2_ref[...])
        for j in range(SG_WIDTH // LANES):
            ysg = su_ref[:, j * LANES:(j + 1) * LANES].astype(F32) * svo[j]
            ycat_ref[:, ATTN_WIDTH + j * LANES:ATTN_WIDTH + (j + 1) * LANES] = ysg.astype(ycat_ref.dtype)

    blk = lambda w: pl.BlockSpec((BLK, w), lambda n: (n, 0))
    return _call(
        body, "even_mix", (nb,),
        [pl.BlockSpec(memory_space=pltpu.SMEM), blk(512), _full((seq, LANES)), _full((seq, LANES)), blk(512), blk(512),
         _full((1, 512)), _full((1, 512)), _full((8, BLK, BLK)), _full((BLK, 512)), _full((LANES, LANES))],
        [blk(D_MODEL), blk(LANES)], [_sds((seq, D_MODEL), ACT_DTYPE), _sds((seq, LANES))],
        (sink, q, k, v, su, sv, sgln_g, sgln_b, sgw, sgb_full, e2), "parallel", rider=rider)


def _even_out(ycat, g, x, mod, mod_next, w_out, w_in4_next, ln_g, ln_b, seq, rider=None):
    tm = _row_tile(seq, 512)
    cs = ODD_IN // 4

    def body(y_ref, g_ref, x_ref, mod_ref, modn_ref, wo_ref, wi_ref, g1_ref, b1_ref, z_ref, x1_ref, xr_ref, gn_ref):
        gg = g_ref[...].astype(F32)
        out = _mm(y_ref[...].astype(F32) * (gg * _sigmoid(gg)), wo_ref[...])
        z = ALPHA * x_ref[...] + mod_ref[2:3, :] * out
        z_ref[...] = z
        zhat, _ = _ln_stats(z)
        x1 = zhat * g1_ref[...] + b1_ref[...]
        x1_ref[...] = x1
        hb = (x1 * (1.0 + modn_ref[1:2, :]) + modn_ref[0:1, :]).astype(MXU_DTYPE)
        for s in range(2):
            xr_ref[:, s * cs:(s + 1) * cs] = jnp.dot(hb, wi_ref[s], preferred_element_type=F32)
            gn_ref[:, s * cs:(s + 1) * cs] = jnp.dot(hb, wi_ref[2 + s], preferred_element_type=F32).astype(gn_ref.dtype)

    return _call(
        body, "even_out", (seq // tm,),
        [_rows(tm, D_MODEL)] * 3 + [_full((3, D_MODEL)), _full((3, D_MODEL)), _const((D_MODEL, D_MODEL)), _const((4, D_MODEL, cs)),
                                    _full((1, D_MODEL)), _full((1, D_MODEL))],
        [_rows(tm, D_MODEL)] * 4, [_sds((seq, D_MODEL))] * 3 + [_sds((seq, D_MODEL), ACT_DTYPE)],
        (ycat, g, x, mod, mod_next, w_out, w_in4_next, ln_g, ln_b), "parallel", rider=rider)


def _halo_specs(tm, seq, width, order=lambda i: i):
    per = tm // 8
    last = seq // 8 - 1
    return [pl.BlockSpec((8, width), lambda i: (jnp.maximum(order(i) * per - 1, 0), 0)),
            pl.BlockSpec((tm, width), lambda i: (order(i), 0)),
            pl.BlockSpec((8, width), lambda i: (jnp.minimum((order(i) + 1) * per, last), 0))]


def _extended(prev_ref, main_ref, next_ref, i, n_steps):
    prev = jnp.where(i > 0, prev_ref[...], 0.0)
    nxt = jnp.where(i < n_steps - 1, next_ref[...], 0.0)
    return jnp.concatenate([prev, main_ref[...], nxt], axis=0)


def _shifted(ext, off, tm):
    if off == 0:
        return ext[8:8 + tm]
    return pltpu.roll(ext, (-off) % ext.shape[0], 0)[8:8 + tm]


SCAN_SUB = 8


def _lru_gate(xh, pre, bias, sp, hs, d):
    r = _sigmoid(pre[:, 0:LANES] + bias[2 * d:2 * d + 1, hs])
    ig = _sigmoid(pre[:, LANES:2 * LANES] + bias[2 * d + 1:2 * d + 2, hs])
    neg_log_a = RG_LRU_C * r * sp[d:d + 1, hs]
    a = jnp.exp(-neg_log_a)
    u = jnp.tanh(neg_log_a) * (a * a + 1.0)
    inv_s = lax.rsqrt(jnp.maximum(u, jnp.finfo(F32).tiny))
    return r, ig, a, u * inv_s, inv_s


def _conv_block(xp_ref, xm_ref, xn_ref, cw_ref, cb_ref, blk, steps, tm):
    ext = _extended(xp_ref, xm_ref, xn_ref, blk, steps)
    return cb_ref[...] + sum(cw_ref[kk:kk + 1, :] * _shifted(ext, kk - 2, tm) for kk in range(4))


def _scan_tiles(a_ref, b_ref, h_ref, hprev_ref, carry_h, carry_a, rows, descending, post):
    sub = SCAN_SUB
    tiles = rows // sub
    row = lax.broadcasted_iota(jnp.int32, (sub, D_MODEL), 0)

    def shift(v, d, fill):
        if descending:
            return jnp.where(row <= sub - 1 - d, pltpu.roll(v, sub - d, 0), fill)
        return jnp.where(row >= d, pltpu.roll(v, d, 0), fill)

    def last(v):
        return jnp.broadcast_to(v[0:1, :] if descending else v[sub - 1:sub, :], v.shape)

    def tile(j, c):
        ch, ca = c
        r0 = pl.multiple_of(((tiles - 1 - j) if descending else j) * sub, sub)
        at = a_ref[pl.ds(r0, sub), :]
        bt = b_ref[pl.ds(r0, sub), :]
        coef = shift(at, 1, ca) if post else at
        acc_a, acc_b = coef, bt
        for d in (1, 2, 4):
            acc_b = acc_b + acc_a * shift(acc_b, d, 0.0)
            acc_a = acc_a * shift(acc_a, d, 1.0)
        h = acc_b + acc_a * ch
        h_ref[pl.ds(r0, sub), :] = h
        if post:
            return last(h), last(at)
        hprev_ref[pl.ds(r0, sub), :] = shift(h, 1, ch)
        return last(h), ca

    ch, ca = lax.fori_loop(0, tiles, tile, (carry_h[...], carry_a[...]), unroll=4)
    carry_h[...] = ch
    carry_a[...] = ca


def _lru_fwd(xr, xc, conv_w, conv_b, wcat, bias, lam, seq, d):
    tb = _row_tile(seq, 512)
    steps = seq // tb
    descending = d == 1
    order = (lambda i: steps - 1 - i) if descending else (lambda i: i)
    with_conv = xc is None
    n_x = 5 if with_conv else 1

    def body(*refs):
        x_refs, (w_ref, bias_ref, lam_ref) = refs[:n_x], refs[n_x:n_x + 3]
        h_ref, hp_ref, a_ref, r_ref, i_ref, s_ref, q_ref = refs[n_x + 3:n_x + 10]
        b_scr, carry_h, carry_a = refs[-3:]
        i = pl.program_id(0)

        @pl.when(i == 0)
        def _():
            carry_h[...] = jnp.zeros_like(carry_h)
            carry_a[...] = jnp.zeros_like(carry_a)

        if with_conv:
            xc_ref = refs[n_x + 10]
            xc_ref[...] = _conv_block(*x_refs, order(i), steps, tb)
        else:
            xc_ref = x_refs[0]
        sp, _ = _softplus_neg(lam_ref[...])
        bias = bias_ref[...]
        for h in range(RNN_HEADS):
            hs = slice(h * LANES, (h + 1) * LANES)
            xh = xc_ref[:, hs]
            r, ig, a, s, q = _lru_gate(xh, _mm(xh, w_ref[h, :, 2 * d * LANES:2 * (d + 1) * LANES]), bias, sp, hs, d)
            a_ref[:, hs] = a
            b_scr[:, hs] = s * ig * xh
            for ref, val in ((r_ref, r), (i_ref, ig), (s_ref, s), (q_ref, q)):
                ref[:, hs] = val.astype(ref.dtype)
        _scan_tiles(a_ref, b_scr, h_ref, hp_ref, carry_h, carry_a, tb, descending, post=False)

    row_spec = pl.BlockSpec((tb, D_MODEL), lambda i: (order(i), 0))
    if with_conv:
        x_specs, x_args = _halo_specs(tb, seq, D_MODEL, order) + [_full((4, D_MODEL)), _full((1, D_MODEL))], (xr, xr, xr, conv_w, conv_b)
    else:
        x_specs, x_args = [row_spec], (xc,)
    n_out = 8 if with_conv else 7
    return pl.pallas_call(
        body, name="lru_fwd_%d" % d, grid=(steps,),
        in_specs=x_specs + [_full((8, LANES, 512)), _full((4, D_MODEL)), _full((2, D_MODEL))],
        out_specs=[row_spec] * n_out,
        out_shape=[_sds((seq, D_MODEL))] * 3 + [_sds((seq, D_MODEL), ACT_DTYPE)] * 4 + [_sds((seq, D_MODEL))] * (n_out - 7),
        scratch_shapes=[pltpu.VMEM((tb, D_MODEL), F32)] + [pltpu.VMEM((SCAN_SUB, D_MODEL), F32)] * 2,
        compiler_params=_params("arbitrary"),
    )(*x_args, wcat, bias, lam)


def _odd_out_and_loss(hf, hr, g, x1, tgt, mod, w_out, ln_g, ln_b, seq):
    tm = _row_tile(seq, 512)

    def body(hf_ref, hr_ref, g_ref, x_ref, t_ref, mod_ref, w_ref, lg_ref, lb_ref,
             dhs_ref, dg_ref, dres_ref, loss_ref, dw_ref, vec_ref):
        @pl.when(pl.program_id(0) == 0)
        def _():
            loss_ref[...] = jnp.zeros_like(loss_ref)
            dw_ref[...] = jnp.zeros_like(dw_ref)
            vec_ref[...] = jnp.zeros_like(vec_ref)

        gg = g_ref[...].astype(F32)
        sg = _sigmoid(gg)
        silu = gg * sg
        hsum = hf_ref[...] + hr_ref[...]
        y = hsum * silu
        out = _mm(y, w_ref[...])
        gate = mod_ref[2:3, :]
        z = ALPHA * x_ref[...] + gate * out
        zhat, rstd = _ln_stats(z)
        x2 = zhat * lg_ref[...] + lb_ref[...]
        err = x2 - t_ref[...]
        loss_ref[...] += 0.5 * jnp.sum(jnp.mean(err * err, axis=-1, keepdims=True))
        dx2 = err * (1.0 / D_MODEL)
        dz = _ln_bwd(dx2, zhat, rstd, lg_ref[...])
        vec_ref[0:1, :] += jnp.sum(dx2 * zhat, axis=0, keepdims=True)
        vec_ref[1:2, :] += jnp.sum(dx2, axis=0, keepdims=True)
        vec_ref[2:3, :] += jnp.sum(dz * out, axis=0, keepdims=True)
        dres_ref[...] = ALPHA * dz
        dout = gate * dz
        dw_ref[...] += _mm_tn(y, dout)
        dy = _mm_nt(dout, w_ref[...])
        dhs_ref[...] = dy * silu
        dg_ref[...] = (dy * hsum * (sg * (1.0 + gg * (1.0 - sg)))).astype(dg_ref.dtype)

    return pl.pallas_call(
        body, name="odd_out_loss", grid=(seq // tm,),
        in_specs=[_rows(tm, D_MODEL)] * 5 + [_full((3, D_MODEL)), _const((D_MODEL, D_MODEL)),
                                             _full((1, D_MODEL)), _full((1, D_MODEL))],
        out_specs=[_rows(tm, D_MODEL)] * 3 + [_full((8, LANES)), _full((D_MODEL, D_MODEL)), _full((8, D_MODEL))],
        out_shape=[_sds((seq, D_MODEL)), _sds((seq, D_MODEL), ACT_DTYPE), _sds((seq, D_MODEL)), _sds((8, LANES)),
                   _sds((D_MODEL, D_MODEL)), _sds((8, D_MODEL))],
        compiler_params=_params("arbitrary"),
    )(hf, hr, g, x1, tgt, mod, w_out, ln_g, ln_b)


def _lru_bwd(xc, dhs, hprev, a_d, r_d, i_d, s_d, q_d, wcat, lam, seq, d):
    tb = _row_tile(seq, 512)
    steps = seq // tb
    descending = d == 0
    order = (lambda i: steps - 1 - i) if descending else (lambda i: i)
    cols = slice(2 * d * LANES, 2 * (d + 1) * LANES)

    def body(xc_ref, dhs_ref, hp_ref, a_ref, r_ref, i_ref, s_ref, q_ref, w_ref, lam_ref, dxc_ref, dw_ref, vec_ref,
             g_scr, carry_h, carry_a):
        i = pl.program_id(0)

        @pl.when(i == 0)
        def _():
            dw_ref[...] = jnp.zeros_like(dw_ref)
            vec_ref[...] = jnp.zeros_like(vec_ref)
            carry_h[...] = jnp.zeros_like(carry_h)
            carry_a[...] = jnp.zeros_like(carry_a)

        sp, dsp = _softplus_neg(lam_ref[...])
        _scan_tiles(a_ref, dhs_ref, g_scr, None, carry_h, carry_a, tb, descending, post=True)
        for h in range(RNN_HEADS):
            hs = slice(h * LANES, (h + 1) * LANES)
            xh, a = xc_ref[:, hs], a_ref[:, hs]
            r, ig, s = r_ref[:, hs].astype(F32), i_ref[:, hs].astype(F32), s_ref[:, hs].astype(F32)
            db = g_scr[:, hs]
            da = db * hp_ref[:, hs]
            dlog_a = da * a - (db * ig * xh) * (a * a * q_ref[:, hs].astype(F32))
            dpr = dlog_a * (-RG_LRU_C) * sp[d:d + 1, hs] * r * (1.0 - r)
            dpi = db * s * xh * ig * (1.0 - ig)
            vec_ref[0:1, hs] += jnp.sum(dpr, axis=0, keepdims=True)
            vec_ref[1:2, hs] += jnp.sum(dpi, axis=0, keepdims=True)
            vec_ref[2:3, hs] += jnp.sum(dlog_a * r, axis=0, keepdims=True) * (-RG_LRU_C) * dsp[d:d + 1, hs]
            dcat = jnp.concatenate([dpr, dpi], axis=1)
            dw_ref[h] += _mm_tn(xh, dcat)
            dxc_ref[:, hs] = db * s * ig + _mm_nt(dcat, w_ref[h, :, cols])

    row_spec = pl.BlockSpec((tb, D_MODEL), lambda i: (order(i), 0))
    return pl.pallas_call(
        body, name="lru_bwd_%d" % d, grid=(steps,),
        in_specs=[row_spec] * 8 + [_full((8, LANES, 512)), _full((2, D_MODEL))],
        out_specs=[row_spec, _full((8, LANES, 2 * LANES)), _full((8, D_MODEL))],
        out_shape=[_sds((seq, D_MODEL)), _sds((8, LANES, 2 * LANES)), _sds((8, D_MODEL))],
        scratch_shapes=[pltpu.VMEM((tb, D_MODEL), F32)] + [pltpu.VMEM((SCAN_SUB, D_MODEL), F32)] * 2,
        compiler_params=_params("arbitrary"),
    )(xc, dhs, hprev, a_d, r_d, i_d, s_d, q_d, wcat, lam)


def _odd_proj_bwd(dxc_f, dxc_r, xr, dg, x1, dres, mod, conv_w, w_in4, seq):
    tm = _row_tile(seq, 512)
    steps = seq // tm

    def body(fp_ref, fm_ref, fn_ref, rp_ref, rm_ref, rn_ref, xp_ref, xm_ref, xn_ref, dg_ref, x_ref, dres_ref, mod_ref, cw_ref,
             w_ref, dx_ref, dw_ref, vec_ref, dpb_ref):
        i = pl.program_id(0)

        @pl.when(i == 0)
        def _():
            vec_ref[...] = jnp.zeros_like(vec_ref)
            dw_ref[...] = jnp.zeros_like(dw_ref)

        dxc_m = fm_ref[...] + rm_ref[...]
        dext = jnp.concatenate([jnp.where(i > 0, fp_ref[...] + rp_ref[...], 0.0), dxc_m,
                                jnp.where(i < steps - 1, fn_ref[...] + rn_ref[...], 0.0)], axis=0)
        xext = _extended(xp_ref, xm_ref, xn_ref, i, steps)
        dxr = sum(cw_ref[kk:kk + 1, :] * _shifted(dext, 2 - kk, tm) for kk in range(4))
        for kk in range(4):
            vec_ref[kk:kk + 1, :] += jnp.sum(dxc_m * _shifted(xext, kk - 2, tm), axis=0, keepdims=True)
        vec_ref[4:5, :] += jnp.sum(dxc_m, axis=0, keepdims=True)
        dpb_ref[:, :D_MODEL] = dxr.astype(dpb_ref.dtype)
        dpb_ref[:, D_MODEL:] = dg_ref[...].astype(dpb_ref.dtype)
        cs = ODD_IN // 4
        dh = sum(_mm_nt(dpb_ref[:, s * cs:(s + 1) * cs], w_ref[s]) for s in range(4))
        x = x_ref[...]
        h_t = (x * (1.0 + mod_ref[1:2, :]) + mod_ref[0:1, :]).T.astype(MXU_DTYPE)
        for s in range(4):
            dw_ref[s] += jnp.dot(h_t, dpb_ref[:, s * cs:(s + 1) * cs], preferred_element_type=F32)
        vec_ref[5:6, :] += jnp.sum(dh, axis=0, keepdims=True)
        vec_ref[6:7, :] += jnp.sum(dh * x, axis=0, keepdims=True)
        dx_ref[...] = dres_ref[...] + dh * (1.0 + mod_ref[1:2, :])

    return pl.pallas_call(
        body, name="odd_proj_bwd", grid=(steps,),
        in_specs=_halo_specs(tm, seq, D_MODEL) * 3 + [_rows(tm, D_MODEL)] * 3
        + [_full((3, D_MODEL)), _full((4, D_MODEL)), _const((4, D_MODEL, ODD_IN // 4))],
        out_specs=[_rows(tm, D_MODEL), _const((4, D_MODEL, ODD_IN // 4)), _full((8, D_MODEL))],
        out_shape=[_sds((seq, D_MODEL)), _sds((4, D_MODEL, ODD_IN // 4)), _sds((8, D_MODEL))],
        scratch_shapes=[pltpu.VMEM((tm, ODD_IN), MXU_DTYPE)],
        compiler_params=_params("arbitrary"),
    )(dxc_f, dxc_f, dxc_f, dxc_r, dxc_r, dxc_r, xr, xr, xr, dg, x1, dres, mod, conv_w, w_in4)


def _even_out_bwd(dx1, z, ycat, g, mod, ln_g, w_out, seq, rider=None):
    tm = _row_tile(seq, 512)
    steps = seq // tm

    def body(dx_ref, z_ref, y_ref, g_ref, mod_ref, lg_ref, w_ref, dy_ref, dg_ref, dres_ref, dw_ref, vec_ref):
        i = pl.program_id(0)

        @pl.when(i == 0)
        def _():
            dw_ref[...] = jnp.zeros_like(dw_ref)
            vec_ref[...] = jnp.zeros_like(vec_ref)

        zhat, rstd = _ln_stats(z_ref[...])
        dx1_ = dx_ref[...]
        dz = _ln_bwd(dx1_, zhat, rstd, lg_ref[...])
        vec_ref[0:1, :] += jnp.sum(dx1_ * zhat, axis=0, keepdims=True)
        vec_ref[1:2, :] += jnp.sum(dx1_, axis=0, keepdims=True)
        dres_ref[...] = ALPHA * dz
        gate = mod_ref[2:3, :]
        gg = g_ref[...].astype(F32)
        sg = _sigmoid(gg)
        silu = gg * sg
        ycat_ = y_ref[...].astype(F32)
        dw_ref[...] += _mm_tn(ycat_ * silu, dz)
        dy = _mm_nt(gate * dz, w_ref[...])
        dy_ref[...] = (dy * silu).astype(dy_ref.dtype)
        dg_ref[...] = (dy * ycat_ * (sg * (1.0 + gg * (1.0 - sg)))).astype(dg_ref.dtype)

        @pl.when(i == steps - 1)
        def _():
            m_acc = dw_ref[...]
            vec_ref[2:3, :] = jnp.sum(w_ref[...].astype(F32) * m_acc, axis=0, keepdims=True)
            dw_ref[...] = m_acc * gate

    return _call(
        body, "even_out_bwd", (steps,),
        [_rows(tm, D_MODEL)] * 4 + [_full((3, D_MODEL)), _full((1, D_MODEL)), _const((D_MODEL, D_MODEL))],
        [_rows(tm, D_MODEL)] * 3 + [_full((D_MODEL, D_MODEL)), _full((8, D_MODEL))],
        [_sds((seq, D_MODEL), ACT_DTYPE), _sds((seq, D_MODEL), ACT_DTYPE), _sds((seq, D_MODEL)), _sds((D_MODEL, D_MODEL)),
         _sds((8, D_MODEL))],
        (dx1, z, ycat, g, mod, ln_g, w_out), "arbitrary", rider=rider)


def _even_mix_bwd(q, k, v, lse, ycat, dycat, su, sv, sink, sgln_g, sgln_b, sgw, sgb_full, e2, e8, seq, rider=None):
    nb = seq // BLK

    def body(sink_ref, q_ref, k_ref, v_ref, lse_ref, y_ref, dy_ref, su_ref, sv_ref, lng_ref, lnb_ref, sgw_ref, sgb_ref, e2_ref,
             e8_ref, dq_ref, dsu_ref, dsv_ref, dk_ref, dv_ref, dsgw_ref, dsgb_ref, vec_ref, dsink_ref, dsgb_acc):
        n = pl.program_id(0)

        @pl.when(n == 0)
        def _():
            dk_ref[...] = jnp.zeros_like(dk_ref)
            dv_ref[...] = jnp.zeros_like(dv_ref)
            dsgw_ref[...] = jnp.zeros_like(dsgw_ref)
            dsgb_acc[...] = jnp.zeros_like(dsgb_acc)
            vec_ref[...] = jnp.zeros_like(vec_ref)
            dsink_ref[...] = jnp.zeros_like(dsink_ref)

        kband = _band(k_ref, n, nb)
        vband = _band(v_ref, n, nb)
        bias = _band_bias(n, seq)
        lane = _lane_iota((BLK, LANES))
        row8 = lax.broadcasted_iota(jnp.int32, (8, LANES), 0)
        lse = lse_ref[...]
        dkb = jnp.zeros((LANES, 3 * BLK), F32)
        dvb = jnp.zeros((LANES, 3 * BLK), F32)
        dsink = jnp.zeros((8, LANES), F32)
        q_tile = lambda j: q_ref[:, j * LANES:(j + 1) * LANES].astype(F32)
        do_tile = lambda j: dy_ref[:, j * LANES:(j + 1) * LANES].astype(F32)
        dq = [jnp.zeros((BLK, LANES), F32) for _ in range(ATTN_WIDTH // LANES)]
        for kv in range(N_Q_HEADS // Q_PER_KV):
            heads = range(Q_PER_KV * kv, Q_PER_KV * (kv + 1))
            lse4, delta4 = [], []
            for h in heads:
                head_lanes = (lane < HEAD_DIM) if h % 2 == 0 else (lane >= HEAD_DIM)
                lse4.append(jnp.sum(jnp.where(lane == h, lse, 0.0), axis=1, keepdims=True))
                o_tile = y_ref[:, (h // 2) * LANES:(h // 2 + 1) * LANES].astype(F32)
                delta4.append(jnp.sum(jnp.where(head_lanes, do_tile(h // 2) * o_tile, 0.0), axis=1, keepdims=True))
            lse4, delta4 = jnp.concatenate(lse4, axis=0), jnp.concatenate(delta4, axis=0)
            q4, do4 = _stack_heads(q_tile, kv), _stack_heads(do_tile, kv)
            s = _mm_nt(q4, kband) * (HEAD_DIM ** -0.5) + bias
            p = jnp.exp(s - lse4)
            wsink = jnp.exp(_per_head_column([sink_ref[h] for h in heads]) - lse4) * delta4
            ds = p * (_mm_nt(do4, vband) - delta4) * (HEAD_DIM ** -0.5)
            dq4 = _mm(ds, kband)
            dkb = dkb + _mm_tn(q4, ds)
            dvb = dvb + _mm_tn(do4, p)
            for g, h in enumerate(heads):
                dq[h // 2] = dq[h // 2] + _from_kv_lanes(dq4[g * BLK:(g + 1) * BLK], h)
                dsink = dsink + jnp.where(row8 == h, -jnp.sum(wsink[g * BLK:(g + 1) * BLK]), 0.0)
        for j in range(ATTN_WIDTH // LANES):
            dq_ref[:, j * LANES:(j + 1) * LANES] = dq[j].astype(dq_ref.dtype)
        dsink_ref[...] += dsink
        prev = jnp.maximum(n - 1, 0)
        nxt = jnp.minimum(n + 1, nb - 1)
        for part, blk_i in enumerate((prev, n, nxt)):
            rows = pl.ds(pl.multiple_of(blk_i * BLK, BLK), BLK)
            dk_ref[rows, :] += dkb[:, part * BLK:(part + 1) * BLK].T
            dv_ref[rows, :] += dvb[:, part * BLK:(part + 1) * BLK].T

        e2 = e2_ref[...]
        lng = lng_ref[...]
        svo, vn, vhat, rstd = _sg_forward(sv_ref[...].astype(F32), lng, lnb_ref[...], sgw_ref, sgb_ref[...], e2)
        for j in range(SG_WIDTH // LANES):
            cs = slice(j * LANES, (j + 1) * LANES)
            dysg = dy_ref[:, ATTN_WIDTH + j * LANES:ATTN_WIDTH + (j + 1) * LANES].astype(F32)
            dsu_ref[:, cs] = (dysg * svo[j]).astype(dsu_ref.dtype)
            dsvo = dysg * su_ref[:, cs].astype(F32)
            dsgb_acc[:, cs] += dsvo
            d_lo = jnp.where(lane < HEAD_DIM, dsvo, 0.0)
            d_hi = dsvo - d_lo
            dsgw_ref[2 * j] += _mm_nt(d_lo, vn[j])
            dsgw_ref[2 * j + 1] += _mm_nt(d_hi, vn[j])
            dvn = _mm_tn(sgw_ref[2 * j], d_lo) + _mm_tn(sgw_ref[2 * j + 1], d_hi)
            vec_ref[0:1, cs] += jnp.sum(dvn * vhat[j], axis=0, keepdims=True)
            vec_ref[1:2, cs] += jnp.sum(dvn, axis=0, keepdims=True)
            dvh = dvn * lng[:, cs]
            m1 = _group_sum(dvh, e2) * (1.0 / HEAD_DIM)
            m2 = _group_sum(dvh * vhat[j], e2) * (1.0 / HEAD_DIM)
            dsv_ref[:, cs] = (rstd[j] * (dvh - m1 - vhat[j] * m2)).astype(dsv_ref.dtype)

        @pl.when(n == nb - 1)
        def _():
            rest = dsgb_acc[...]
            total = jnp.zeros((8, BLK), F32)
            for _ in range(3):
                part = rest.astype(MXU_DTYPE)
                total = total + lax.dot_general(e8_ref[...], part, (((1,), (1,)), ((), ())), preferred_element_type=F32)
                rest = rest - part.astype(F32)
            dsgb_ref[...] = total

    blk = lambda w: pl.BlockSpec((BLK, w), lambda n: (n, 0))
    return _call(
        body, "even_mix_bwd", (nb,),
        [pl.BlockSpec(memory_space=pltpu.SMEM), blk(512), _full((seq, LANES)), _full((seq, LANES)), blk(LANES),
         blk(D_MODEL), blk(D_MODEL), blk(512), blk(512), _full((1, 512)), _full((1, 512)), _full((8, BLK, BLK)),
         _full((BLK, 512)), _full((LANES, LANES)), _full((8, 512))],
        [blk(512), blk(512), blk(512), _full((seq, LANES)), _full((seq, LANES)), _full((8, BLK, BLK)),
         _full((8, BLK)), _full((8, 512)), _full((8, LANES))],
        [_sds((seq, 512), ACT_DTYPE), _sds((seq, 512), ACT_DTYPE), _sds((seq, 512), ACT_DTYPE), _sds((seq, LANES)), _sds((seq, LANES)),
         _sds((8, BLK, BLK)), _sds((8, BLK)), _sds((8, 512)), _sds((8, LANES))],
        (sink, q, k, v, lse, ycat, dycat, su, sv, sgln_g, sgln_b, sgw, sgb_full, e2, e8), "arbitrary",
        scratch=[pltpu.VMEM((BLK, 512), F32)], rider=rider)


def _even_proj_bwd(dq, dk, dv, dsu, dsv, dg, x, dres, mod, tabs, w_in_t, seq):
    tm = _row_tile(seq, 512)

    def body(dq_ref, dk_ref, dv_ref, dsu_ref, dsv_ref, dg_ref, x_ref, dres_ref, mod_ref, cos_ref, sp_ref, sm_ref, wt_ref,
             dx_ref, dw_ref, vec_ref, dpb_ref):
        @pl.when(pl.program_id(0) == 0)
        def _():
            vec_ref[...] = jnp.zeros_like(vec_ref)
            dw_ref[...] = jnp.zeros_like(dw_ref)

        cos_t, sin_p, sin_m = cos_ref[...], sp_ref[...], sm_ref[...]
        dt = dpb_ref.dtype
        for j in range(ATTN_WIDTH // LANES):
            cs = slice(j * LANES, (j + 1) * LANES)
            dpb_ref[:, cs] = _rope_t(dq_ref[:, cs].astype(F32), cos_t, sin_p, sin_m).astype(dt)
        dpb_ref[:, 512:640] = _rope_t(dk_ref[...], cos_t, sin_p, sin_m).astype(dt)
        dpb_ref[:, 640:768] = dv_ref[...].astype(dt)
        dpb_ref[:, 768:1280] = dsu_ref[...].astype(dt)
        dpb_ref[:, 1280:1792] = dsv_ref[...].astype(dt)
        dpb_ref[:, 1792:2816] = dg_ref[...].astype(dt)
        dpb = dpb_ref[...]
        dh = jnp.dot(dpb, wt_ref[...], preferred_element_type=F32)
        x_ = x_ref[...]
        hb = (x_ * (1.0 + mod_ref[1:2, :]) + mod_ref[0:1, :]).astype(MXU_DTYPE)
        dw_ref[...] += _mm_tn(dpb, hb)
        vec_ref[0:1, :] += jnp.sum(dh, axis=0, keepdims=True)
        vec_ref[1:2, :] += jnp.sum(dh * x_, axis=0, keepdims=True)
        dx_ref[...] = dres_ref[...] + dh * (1.0 + mod_ref[1:2, :])

    return pl.pallas_call(
        body, name="even_proj_bwd", grid=(seq // tm,),
        in_specs=[_rows(tm, 512), _rows(tm, LANES), _rows(tm, LANES), _rows(tm, 512), _rows(tm, 512), _rows(tm, D_MODEL),
                  _rows(tm, D_MODEL), _rows(tm, D_MODEL), _full((3, D_MODEL))] + [_rows(tm, LANES)] * 3
        + [_const((EVEN_IN, D_MODEL))],
        out_specs=[_rows(tm, D_MODEL), _const((EVEN_IN, D_MODEL)), _full((8, D_MODEL))],
        out_shape=[_sds((seq, D_MODEL)), _sds((EVEN_IN, D_MODEL)), _sds((8, D_MODEL))],
        scratch_shapes=[pltpu.VMEM((tm, EVEN_IN), MXU_DTYPE)],
        compiler_params=_params("arbitrary"),
    )(dq, dk, dv, dsu, dsv, dg, x, dres, mod, *tabs, w_in_t)


def _local_step(x, posf, tgt, mod, w, seq, ride=None):
    rid = lambda make, *a: None if ride is None else make(*a)
    mxu = lambda a: a.astype(MXU_DTYPE)
    row = lambda a: a.reshape(1, -1)
    tabs = _rope_tables(posf, seq)
    e2 = mxu(jnp.kron(jnp.eye(2, dtype=F32), jnp.ones((HEAD_DIM, HEAD_DIM), F32)))
    e8 = mxu(jnp.repeat(jnp.eye(N_SG_GROUPS, dtype=F32), HEAD_DIM, axis=1))
    sgw = mxu(w["ev_sg_w"])
    sgb_full = jnp.repeat(w["ev_sg_b"].T, HEAD_DIM, axis=1)
    sgln_g, sgln_b = row(w["ev_sg_ln_g"]), row(w["ev_sg_ln_b"])
    sink = w["ev_sink"].reshape(N_Q_HEADS)
    ev_w_in_t = mxu(w["ev_w_in_t"])
    if ride is None:
        ev_w_out, od_w_in, od_w_out = mxu(w["ev_w_out"]), mxu(w["od_w_in"]), mxu(w["od_w_out"])
    wcat = mxu(jnp.concatenate([w["od_w_a"][0], w["od_w_x"][0], w["od_w_a"][1], w["od_w_x"][1]], axis=2))
    gate_bias = jnp.stack([w["od_b_a"][0], w["od_b_x"][0], w["od_b_a"][1], w["od_b_x"][1]])
    conv_b = row(w["od_conv_b"])
    ln_g, ln_b = w["ln_g"], w["ln_b"]

    (q, k, v, su, sv, g0), got = _even_proj(x, mod[0], ev_w_in_t, tabs, seq, rid(_gather_rider, ride and ride["ev_w_out"]))
    if ride is not None:
        ev_w_out = got[0].reshape(D_MODEL, D_MODEL)
    (ycat, lse), got = _even_mix(q, k, v, su, sv, sink, sgln_g, sgln_b, sgw, sgb_full, e2, seq,
                                 rid(_gather_rider, ride and ride["od_w_in"]))
    if ride is not None:
        od_w_in = got[0]
    (z0, x1, xr, g1), got = _even_out(ycat, g0, x, mod[0], mod[1], ev_w_out, od_w_in, ln_g[0:1], ln_b[0:1], seq,
                                      rid(_gather_rider, ride and ride["od_w_out"]))
    if ride is not None:
        od_w_out = got[0].reshape(D_MODEL, D_MODEL)
    lru = (w["od_conv_w"], conv_b, wcat, gate_bias, w["od_lam"], seq)
    hf, hpf, *saved_f, xc = _lru_fwd(xr, None, *lru, 0)
    hr, hpr, *saved_r = _lru_fwd(xr, xc, *lru, 1)
    dhs, dg1, dres1, loss, d_od_w_out, vec_o = _odd_out_and_loss(hf, hr, g1, x1, tgt, mod[1], od_w_out, ln_g[1:2], ln_b[1:2], seq)
    dxc_f, dw_f, vec_f = _lru_bwd(xc, dhs, hpf, *saved_f, wcat, w["od_lam"], seq, 0)
    dxc_r, dw_r, vec_r = _lru_bwd(xc, dhs, hpr, *saved_r, wcat, w["od_lam"], seq, 1)
    dx1, d_od_w_in, vec_p = _odd_proj_bwd(dxc_f, dxc_r, xr, dg1, x1, dres1, mod[1], w["od_conv_w"], od_w_in, seq)
    d_od_w_a = jnp.stack([dw_f[:, :, 0:128], dw_r[:, :, 0:128]])
    d_od_w_x = jnp.stack([dw_f[:, :, 128:256], dw_r[:, :, 128:256]])
    od_parts = [d_od_w_in.reshape(4, 2, 512, 512), d_od_w_out.reshape(4, 2, 128, D_MODEL),
                d_od_w_a.reshape(4, 2, 2 * BLK, BLK), d_od_w_x.reshape(4, 2, 2 * BLK, BLK)]
    (dycat, dg0, dres0, d_ev_w_out, vec_e), got_od = _even_out_bwd(dx1, z0, ycat, g0, mod[0], ln_g[0:1], ev_w_out, seq,
                                                                   rid(_sibling_swap_rider, od_parts))
    if ride is not None:
        od_sums = _sum_sibling(ride["core"], od_parts, got_od, [ride["wire"]] * 4, "sum_sibling_od")
    (dq, dsu, dsv, dk, dv, d_sgw, d_sgb, vec_s, d_sink), od_slots = _even_mix_bwd(
        q, k, v, lse, ycat, dycat, su, sv, sink, sgln_g, sgln_b, sgw, sgb_full, e2, e8, seq,
        rid(_chip_exchange_rider, ride and od_sums))
    grad_x, d_ev_w_in_t, vec_x = _even_proj_bwd(dq, dk, dv, dsu, dsv, dg0, x, dres0, mod[0], tabs, ev_w_in_t, seq)

    rows, dmod_blk = _pack_small(vec_x, vec_e, vec_p, vec_o, vec_f, vec_r, vec_s, d_sink, d_sgb, loss)
    grads = {"rows": rows, "dmod_blk": dmod_blk, "ev_w_in_t": d_ev_w_in_t, "ev_w_out": d_ev_w_out, "ev_sg_w": d_sgw}
    if ride is None:
        grads.update({"od_w_in": d_od_w_in, "od_w_out": d_od_w_out, "od_w_a": d_od_w_a, "od_w_x": d_od_w_x})
    else:
        grads["od_slots"] = od_slots
    return grad_x, grads


ROW_DMOD, ROW_LN, ROW_SG_LN, ROW_SG_B, ROW_CONV_W, ROW_CONV_B, ROW_B_A, ROW_B_X, ROW_LAM, ROW_SINK, ROW_LOSS = (
    0, 6, 10, 11, 12, 16, 17, 19, 21, 23, 24)
SMALL_ROWS = 64


def _pack_small(vec_x, vec_e, vec_p, vec_o, vec_f, vec_r, vec_s, d_sink, d_sgb, loss):
    def body(x_ref, e_ref, p_ref, o_ref, f_ref, r_ref, s_ref, sink_ref, sgb_ref, loss_ref, rows_ref, dmod_ref):
        rows_ref[...] = jnp.zeros_like(rows_ref)
        dmod_ref[...] = jnp.zeros_like(dmod_ref)
        put = [(ROW_DMOD, x_ref, 0), (ROW_DMOD + 1, x_ref, 1), (ROW_DMOD + 2, e_ref, 2), (ROW_DMOD + 3, p_ref, 5),
               (ROW_DMOD + 4, p_ref, 6), (ROW_DMOD + 5, o_ref, 2), (ROW_LN, e_ref, 0), (ROW_LN + 1, e_ref, 1),
               (ROW_LN + 2, o_ref, 0), (ROW_LN + 3, o_ref, 1), (ROW_CONV_B, p_ref, 4), (ROW_B_A, f_ref, 0),
               (ROW_B_A + 1, r_ref, 0), (ROW_B_X, f_ref, 1), (ROW_B_X + 1, r_ref, 1), (ROW_LAM, f_ref, 2), (ROW_LAM + 1, r_ref, 2)]
        put += [(ROW_CONV_W + k, p_ref, k) for k in range(4)]
        for dst, ref, src in put:
            rows_ref[dst:dst + 1, :] = ref[src:src + 1, :]
            if dst < 6:
                dmod_ref[dst:dst + 1, :] = ref[src:src + 1, :]
        rows_ref[ROW_SG_LN:ROW_SG_LN + 1, 0:SG_WIDTH] = s_ref[0:1, :]
        rows_ref[ROW_SG_LN:ROW_SG_LN + 1, SG_WIDTH:2 * SG_WIDTH] = s_ref[1:2, :]
        lane = _lane_iota((1, LANES))
        sink = jnp.zeros((1, LANES), F32)
        for h in range(N_Q_HEADS):
            rows_ref[ROW_SG_B:ROW_SG_B + 1, h * LANES:(h + 1) * LANES] = sgb_ref[h:h + 1, :]
            sink = jnp.where(lane == h, sink_ref[h:h + 1, :], sink)
        rows_ref[ROW_SINK:ROW_SINK + 1, 0:LANES] = sink
        rows_ref[ROW_LOSS:ROW_LOSS + 1, 0:LANES] = jnp.where(lane == 0, loss_ref[0:1, :], 0.0)

    return pl.pallas_call(body, name="pack_small", out_shape=[_sds((SMALL_ROWS, D_MODEL)), _sds((8, D_MODEL))])(
        vec_x, vec_e, vec_p, vec_o, vec_f, vec_r, vec_s, d_sink, d_sgb, loss)


def _allgather8(block, name):
    m_per, n = block.shape

    def body(x_ref, out_ref, send_sems, recv_sems, local_sem):
        x, y, c = _place()
        me, sibling = (x, y, c), (x, y, 1 - c)
        chips = [(1 - x, y), (x, 1 - y), (1 - x, 1 - y)]

        def rows(px, py, pc):
            return out_ref.at[pl.ds((4 * px + 2 * py + pc) * m_per, m_per), :]

        def copy(k, blk, to, src=None):
            return pltpu.make_async_remote_copy(src_ref=rows(*blk) if src is None else src, dst_ref=rows(*blk),
                                                send_sem=send_sems.at[k], recv_sem=recv_sems.at[k], device_id=to,
                                                device_id_type=MESH)

        mine = pltpu.make_async_copy(x_ref, rows(*me), local_sem)
        mine.start()
        first = [copy(0, me, sibling, src=x_ref)] + [copy(1 + j, me, (*chip, c), src=x_ref) for j, chip in enumerate(chips)]
        for cp in first:
            cp.start()
        passed = [copy(4 + j, (*chip, c), sibling) for j, chip in enumerate(chips)]
        for j, chip in enumerate(chips):
            copy(1 + j, (*chip, c), me).wait_recv()
            passed[j].start()
        copy(0, sibling, me).wait_recv()
        for j, chip in enumerate(chips):
            copy(4 + j, (*chip, 1 - c), me).wait_recv()
        for cp in first + passed:
            cp.wait_send()
        mine.wait()

    return pl.pallas_call(
        body, name=name, out_shape=_sds((8 * m_per, n), block.dtype),
        in_specs=[pl.BlockSpec(memory_space=pltpu.VMEM)], out_specs=pl.BlockSpec(memory_space=pltpu.VMEM),
        scratch_shapes=[pltpu.SemaphoreType.DMA((7,)), pltpu.SemaphoreType.DMA((7,)), pltpu.SemaphoreType.DMA],
        compiler_params=pltpu.CompilerParams(vmem_limit_bytes=VMEM_LIMIT),
    )(block)


class _Copies:
    def __init__(self, send_sems, recv_sems, local_sems, stages):
        self.send_sems, self.recv_sems, self.local_sems, self.stages = send_sems, recv_sems, local_sems, stages
        self.sent, self.staged, self.locals = [], [], []

    def remote(self, k, src, dst, to):
        return pltpu.make_async_remote_copy(src_ref=src, dst_ref=dst, send_sem=self.send_sems.at[k], recv_sem=self.recv_sems.at[k],
                                            device_id=to, device_id_type=MESH)

    def send(self, k, src, dst, to):
        cp = self.remote(k, src, dst, to)
        cp.start()
        self.sent.append(cp)

    def arrived(self, k, dst, frm):
        self.remote(k, dst, dst, frm).wait_recv()

    def local(self, src, dst):
        k = len(self.staged)
        cp = pltpu.make_async_copy(src, self.stages[k], self.local_sems.at[2 * k])
        cp.start()
        self.staged.append((cp, dst))

    def flush(self):
        for k in range(len(self.locals), len(self.staged)):
            cp, dst = self.staged[k]
            cp.wait()
            out = pltpu.make_async_copy(self.stages[k], dst, self.local_sems.at[2 * k + 1])
            out.start()
            self.locals.append(out)

    def drain(self):
        self.flush()
        for cp in self.sent:
            cp.wait_send()
        for cp in self.locals:
            cp.wait()


def _comm_call(body, name, ins, out_shapes, n_remote, stages):
    n_in, n_out = len(ins), len(out_shapes)

    def kern(*refs):
        in_refs, out_refs = refs[:n_in], refs[n_in:n_in + n_out]
        send_sems, recv_sems, local_sems = refs[n_in + n_out:n_in + n_out + 3]
        body(_Copies(send_sems, recv_sems, local_sems, refs[n_in + n_out + 3:]), in_refs, out_refs)

    hbm = pl.BlockSpec(memory_space=pl.ANY)
    return pl.pallas_call(
        kern, name=name, out_shape=out_shapes, in_specs=[hbm] * n_in, out_specs=[hbm] * n_out,
        scratch_shapes=[pltpu.SemaphoreType.DMA((n_remote,)), pltpu.SemaphoreType.DMA((n_remote,)),
                        pltpu.SemaphoreType.DMA((2 * len(stages),))] + [pltpu.VMEM(s, d) for s, d in stages],
        compiler_params=pltpu.CompilerParams(vmem_limit_bytes=VMEM_LIMIT),
    )(*ins)


def _gather_to_all(cps, pairs, me, sibling, other_chips, c, base):
    idx = lambda p: 4 * p[0] + 2 * p[1] + p[2]
    for i, (src, dst) in enumerate(pairs):
        cps.local(src, dst.at[idx(me)])
        cps.send(base + 7 * i, src, dst.at[idx(me)], sibling)
        for j, chip in enumerate(other_chips):
            cps.send(base + 7 * i + 1 + j, src, dst.at[idx(me)], (*chip, c))
    cps.flush()
    for j, chip in enumerate(other_chips):
        for i, (_, dst) in enumerate(pairs):
            got = dst.at[idx((*chip, c))]
            cps.arrived(base + 7 * i + 1 + j, got, (*chip, c))
            cps.send(base + 7 * i + 4 + j, got, got, sibling)
    for i, (_, dst) in enumerate(pairs):
        cps.arrived(base + 7 * i, dst.at[idx(sibling)], sibling)
        for j, chip in enumerate(other_chips):
            cps.arrived(base + 7 * i + 4 + j, dst.at[idx((*chip, 1 - c))], sibling)


def _gather_weights(shards, small):
    n = len(shards)

    def body(cps, ins, outs):
        x, y, c = _place()
        me, sibling, mine = (x, y, c), (x, y, 1 - c), 2 * x + y
        chips = [(1 - x, y), (x, 1 - y), (1 - x, 1 - y)]
        for i in range(n):
            cps.local(ins[i], outs[i].at[mine])
        for j, (px, py) in enumerate(chips):
            for i in range(n):
                hr = shards[i].shape[0] // 2
                rows = pl.ds(c * hr, hr)
                cps.send(6 * i + j, ins[i].at[rows], outs[i].at[mine, rows], (px, py, c))
        _gather_to_all(cps, [(ins[n], outs[n])], me, sibling, chips, c, 6 * n)
        for j, (px, py) in enumerate(chips):
            for i in range(n):
                hr = shards[i].shape[0] // 2
                got = outs[i].at[2 * px + py, pl.ds(c * hr, hr)]
                cps.arrived(6 * i + j, got, (px, py, c))
                cps.send(6 * i + 3 + j, got, got, sibling)
        for j, (px, py) in enumerate(chips):
            for i in range(n):
                hr = shards[i].shape[0] // 2
                cps.arrived(6 * i + 3 + j, outs[i].at[2 * px + py, pl.ds((1 - c) * hr, hr)], sibling)
        cps.drain()

    return _comm_call(body, "gather_weights", list(shards) + [small],
                      [_sds((4,) + s.shape, s.dtype) for s in shards] + [_sds((8,) + small.shape, small.dtype)], 6 * n + 7,
                      [(a.shape, a.dtype) for a in list(shards) + [small]])


def _reduce_sibling(parts, dmod_rows):
    n = len(parts)

    def body(cps, ins, outs):
        x, y, c = _place()
        me, sibling = (x, y, c), (x, y, 1 - c)
        chips = [(1 - x, y), (x, 1 - y), (1 - x, 1 - y)]
        for i in range(n):
            cps.send(i, ins[i].at[:, 1 - c], outs[i], sibling)
        _gather_to_all(cps, [(ins[n], outs[n])], me, sibling, chips, c, n)
        for i in range(n):
            cps.arrived(i, outs[i], sibling)
        cps.drain()

    return _comm_call(body, "reduce_sibling", list(parts) + [dmod_rows],
                      [_sds((4,) + p.shape[2:], p.dtype) for p in parts] + [_sds((8,) + dmod_rows.shape, dmod_rows.dtype)], n + 7,
                      [(dmod_rows.shape, dmod_rows.dtype)])


def _reduce_chips(parts):
    n = len(parts)

    def body(cps, ins, outs):
        x, y, c = _place()
        mine = 2 * x + y
        chips = _other_chips(x, y)
        for i in range(n):
            cps.local(ins[i].at[mine], outs[i].at[mine])
        for j, (px, py) in enumerate(chips):
            for i in range(n):
                cps.send(3 * i + j, ins[i].at[2 * px + py], outs[i].at[mine], (px, py, c))
        cps.flush()
        for j, (px, py) in enumerate(chips):
            for i in range(n):
                cps.arrived(3 * i + j, outs[i].at[2 * px + py], (px, py, c))
        cps.drain()

    return _comm_call(body, "reduce_chips", list(parts), [_sds(p.shape, p.dtype) for p in parts], 3 * n,
                      [(p.shape[1:], p.dtype) for p in parts])


def _gather_reduced(shard_parts, repl_parts):
    ns, nr = len(shard_parts), len(repl_parts)

    def body(cps, ins, outs):
        x, y, c = _place()
        me, sibling = (x, y, c), (x, y, 1 - c)
        chips = [(1 - x, y), (x, 1 - y), (1 - x, 1 - y)]
        for i in range(ns):
            cps.local(ins[i], outs[i].at[c])
            cps.send(i, ins[i], outs[i].at[c], sibling)
        _gather_to_all(cps, [(ins[ns + i], outs[ns + i]) for i in range(nr)], me, sibling, chips, c, ns)
        for i in range(ns):
            cps.arrived(i, outs[i].at[1 - c], sibling)
        cps.drain()

    return _comm_call(body, "gather_reduced", list(shard_parts) + list(repl_parts),
                      [_sds((2,) + p.shape, p.dtype) for p in shard_parts] + [_sds((8,) + p.shape, p.dtype) for p in repl_parts],
                      ns + 7 * nr, [(p.shape, p.dtype) for p in list(shard_parts) + list(repl_parts)])


def _sum_sibling(core, parts, got, wire, name):
    n = len(parts)

    def body(core_ref, *refs):
        for i in range(n):
            refs[2 * n + i][0] = (refs[i][0] + refs[n + i][0]).astype(wire[i])

    keep_spec = lambda p: pl.BlockSpec((1, None) + p.shape[2:], lambda s, core_ref: (s, core_ref[0], 0, 0))
    slot_spec = lambda p: pl.BlockSpec((1,) + p.shape[2:], lambda s, core_ref: (s, 0, 0))
    return pl.pallas_call(
        body, name=name,
        grid_spec=pltpu.PrefetchScalarGridSpec(
            num_scalar_prefetch=1, grid=(4,), in_specs=[keep_spec(p) for p in parts] + [slot_spec(p) for p in parts],
            out_specs=[slot_spec(p) for p in parts]),
        out_shape=[_sds((4,) + p.shape[2:], wire[i]) for i, p in enumerate(parts)],
        compiler_params=_params("parallel"),
    )(core, *parts, *got)


def _sum_slots(slots, name):
    n = len(slots)

    def spec_pair(p):
        k, rows, cols = p.shape
        sub = 16 if p.dtype == BF16 else 8
        if (rows // 2) % sub == 0:
            return pl.BlockSpec((k, rows // 2, cols), lambda i: (0, i, 0)), pl.BlockSpec((rows // 2, cols), lambda i: (i, 0))
        return pl.BlockSpec((k, rows, cols), lambda i: (0, 0, 0)), pl.BlockSpec((rows, cols), lambda i: (0, 0))

    pairs = [spec_pair(p) for p in slots]

    def body(*refs):
        for i in range(n):
            acc = refs[i][0].astype(F32)
            for j in range(1, slots[i].shape[0]):
                acc = acc + refs[i][j].astype(F32)
            refs[n + i][...] = acc

    return pl.pallas_call(
        body, name=name, grid=(2,), in_specs=[a for a, _ in pairs], out_specs=[b for _, b in pairs],
        out_shape=[_sds(p.shape[1:]) for p in slots], compiler_params=_params("arbitrary"),
    )(*slots)


def _modulation(c_all, ada_w, ada_b):
    cols = ada_w.shape[2]

    def body(c_ref, w_ref, b_ref, o_ref):
        cc = c_ref[...]
        o_ref[0] = _mm(cc * _sigmoid(cc), w_ref[0]) + b_ref[0]

    return pl.pallas_call(
        body, name="modulation", grid=(2,),
        in_specs=[_full((8, D_MODEL)), pl.BlockSpec((1, D_MODEL, cols), lambda l: (l, 0, 0)), pl.BlockSpec((1, 1, cols), lambda l: (l, 0, 0))],
        out_specs=pl.BlockSpec((1, 8, cols), lambda l: (l, 0, 0)), out_shape=_sds((2, 8, cols)),
        compiler_params=_params("parallel"),
    )(c_all, ada_w, ada_b)


def _adamw_math(w, g, m, v):
    m = ADAM_B1 * m + (1.0 - ADAM_B1) * g
    v = ADAM_B2 * v + (1.0 - ADAM_B2) * (g * g)
    m_hat = m / (1.0 - ADAM_B1 ** ADAM_STEP)
    v_hat = v / (1.0 - ADAM_B2 ** ADAM_STEP)
    delta = -ADAM_LR * (m_hat / (jnp.sqrt(v_hat) + ADAM_EPS) + ADAM_WD * w)
    return delta, m, v


def _ada_update(c_all, dmod, w, m, v, rider=None):
    cols = w.shape[2]
    tr = 256
    per = D_MODEL // tr
    spec3 = pl.BlockSpec((1, tr, cols), lambda i: (i // per, i % per, 0))

    def body(c_ref, d_ref, w_ref, m_ref, v_ref, g_ref, dl_ref, nm_ref, nv_ref):
        cc = c_ref[...]
        g = _mm_tn(cc * _sigmoid(cc), d_ref[0])
        g_ref[0] = g
        dl_ref[0], nm_ref[0], nv_ref[0] = _adamw_math(w_ref[0], g, m_ref[0], v_ref[0])

    return _call(
        body, "ada_update", (2 * per,),
        [pl.BlockSpec((8, tr), lambda i: (0, i % per)), pl.BlockSpec((1, 8, cols), lambda i: (i // per, 0, 0)), spec3, spec3, spec3],
        [spec3] * 4, [_sds(w.shape)] * 4, (c_all, dmod, w, m, v), "parallel", rider=rider)


def _adamw_matrices(params):
    n = len(params)
    steps = 8

    def body(*refs):
        ins, outs = refs[:4 * n], refs[4 * n:]
        for j in range(n):
            w_ref, g_ref, m_ref, v_ref = ins[4 * j:4 * j + 4]
            g = g_ref[...]
            outs[4 * j][...] = g
            outs[4 * j + 1][...], outs[4 * j + 2][...], outs[4 * j + 3][...] = _adamw_math(w_ref[...], g, m_ref[...], v_ref[...])

    spec = lambda p: _rows(p[0].shape[0] // steps, p[0].shape[1])
    res = pl.pallas_call(
        body, name="adamw_matrices", grid=(steps,), in_specs=[spec(p) for p in params for _ in range(4)],
        out_specs=[spec(p) for p in params for _ in range(4)], out_shape=[_sds(p[0].shape) for p in params for _ in range(4)],
        compiler_params=_params("parallel"),
    )(*[a for p in params for a in p])
    return [tuple(res[4 * j:4 * j + 4]) for j in range(n)]


def _adamw_small(params):
    n = len(params)

    def body(*refs):
        ins, outs = refs[:4 * n], refs[4 * n:]
        for j in range(n):
            w_ref, g_ref, m_ref, v_ref = ins[4 * j:4 * j + 4]
            outs[3 * j][...], outs[3 * j + 1][...], outs[3 * j + 2][...] = _adamw_math(w_ref[...], g_ref[...], m_ref[...], v_ref[...])

    flat = [a for p in params for a in p]
    res = pl.pallas_call(body, name="adamw_small", out_shape=[_sds(p[0].shape) for p in params for _ in range(3)])(*flat)
    return [tuple(res[3 * j:3 * j + 3]) for j in range(n)]


def _cols(a, start, size):
    return lax.dynamic_slice_in_dim(a, start, size, axis=a.ndim - 1)


def kernel(x, c, positions, ada_w, ada_b, ln_g, ln_b, ev_w_in, ev_w_out, ev_sink, ev_sg_ln_g, ev_sg_ln_b, ev_sg_w, ev_sg_b, od_w_in, od_conv_w, od_conv_b, od_w_a, od_b_a, od_w_x, od_b_x, od_lam, od_w_out, loss_target, m_ada_w, m_ada_b, m_ln_g, m_ln_b, m_ev_w_in, m_ev_w_out, m_ev_sink, m_ev_sg_ln_g, m_ev_sg_ln_b, m_ev_sg_w, m_ev_sg_b, m_od_w_in, m_od_conv_w, m_od_conv_b, m_od_w_a, m_od_b_a, m_od_w_x, m_od_b_x, m_od_lam, m_od_w_out, v_ada_w, v_ada_b, v_ln_g, v_ln_b, v_ev_w_in, v_ev_w_out, v_ev_sink, v_ev_sg_ln_g, v_ev_sg_ln_b, v_ev_sg_w, v_ev_sg_b, v_od_w_in, v_od_conv_w, v_od_conv_b, v_od_w_a, v_od_b_a, v_od_w_x, v_od_b_x, v_od_lam, v_od_w_out):
    seq = x.shape[1]
    px, py, pc = _place()
    chip = 2 * px + py
    dev = 2 * chip + pc

    small = jnp.concatenate([od_conv_w[0].reshape(-1), od_conv_b[0], od_b_a[0].reshape(-1), jnp.zeros((256,), F32),
                             od_b_x[0].reshape(-1), od_lam[0].reshape(-1)]).reshape(3, D_MODEL)
    blk = jnp.concatenate([c, small, jnp.zeros((4, D_MODEL), F32)], axis=0)
    tr = lambda a: jnp.swapaxes(a, -1, -2)
    wire_w = lambda a: a.astype(MXU_DTYPE)
    ev_w_in4, g_small = _gather_weights([wire_w(tr(ev_w_in[0]))], blk)
    core = pc.astype(jnp.int32).reshape(1)
    ride = {"ev_w_out": wire_w(ev_w_out[0]), "od_w_in": wire_w(od_w_in[0]), "od_w_out": wire_w(od_w_out[0]),
            "core": core, "wire": MXU_DTYPE}
    c_all = g_small[:, 0, :]
    per_chip = g_small[0::2]
    conv_w = per_chip[:, 1].reshape(4, 4, 256).transpose(1, 0, 2).reshape(4, D_MODEL)
    conv_b = per_chip[:, 2, 0:256].reshape(D_MODEL)
    b_a = per_chip[:, 2, 256:768].reshape(4, 2, 256).transpose(1, 0, 2).reshape(2, D_MODEL)
    b_x = per_chip[:, 3, 0:512].reshape(4, 2, 256).transpose(1, 0, 2).reshape(2, D_MODEL)
    lam = per_chip[:, 3, 512:1024].reshape(4, 2, 256).transpose(1, 0, 2).reshape(2, D_MODEL)

    w_full = {
        "ev_w_in_t": ev_w_in4.reshape(EVEN_IN, D_MODEL),
        "ev_sink": ev_sink[0], "ev_sg_ln_g": ev_sg_ln_g[0], "ev_sg_ln_b": ev_sg_ln_b[0], "ev_sg_w": ev_sg_w[0],
        "ev_sg_b": ev_sg_b[0], "od_conv_w": conv_w, "od_conv_b": conv_b, "od_w_a": od_w_a[0], "od_b_a": b_a,
        "od_w_x": od_w_x[0], "od_b_x": b_x, "od_lam": lam, "ln_g": ln_g, "ln_b": ln_b,
    }

    ada_cols = ada_w.shape[2]
    mod_sh = _modulation(c_all, ada_w, _cols(ada_b, chip * ada_cols, ada_cols).reshape(2, 1, ada_cols))
    mod_all = _allgather8(mod_sh.reshape(16, ada_cols), "gather_mod").reshape(4, 2, 2, 8, ada_cols)[:, 0]
    mod_mine = lax.dynamic_index_in_dim(mod_all, dev, axis=2, keepdims=False)
    mod = mod_mine.transpose(1, 0, 2).reshape(2, 3, D_MODEL)

    posf = positions.astype(F32).reshape(seq, 1)
    grad_x, g = _local_step(x[0], posf, loss_target[0], mod, w_full, seq, ride)

    parts = [g["ev_w_in_t"].reshape(4, 2, 352, D_MODEL), g["ev_w_out"].reshape(4, 2, 128, D_MODEL),
             g["ev_sg_w"].reshape(4, 2, BLK, BLK), g["rows"].reshape(4, 2, SMALL_ROWS // 8, D_MODEL)]
    wire = [MXU_DTYPE] * 3 + [F32]
    *got, dmod_gathered = _reduce_sibling(parts, g["dmod_blk"])
    ev_slots = list(_reduce_chips(_sum_sibling(core, parts, got, wire, "sum_sibling")))
    od_slots = list(g["od_slots"])
    mine = _sum_slots(ev_slots[0:2] + od_slots[0:2] + ev_slots[2:3] + od_slots[2:4] + ev_slots[3:4], "sum_chips")
    reduced = _gather_reduced(mine[:4], mine[4:])
    g_ev_w_in_t = reduced[0].reshape(704, D_MODEL)
    g_ev_w_out = reduced[1].reshape(256, D_MODEL)
    g_od_w_in = reduced[2].reshape(D_MODEL, 512)
    g_od_w_out = reduced[3].reshape(256, D_MODEL)
    g_sg_w = reduced[4].reshape(8 * BLK, BLK)
    g_w_a = reduced[5].reshape(16 * BLK, BLK)
    g_w_x = reduced[6].reshape(16 * BLK, BLK)
    gs = reduced[7].reshape(SMALL_ROWS, D_MODEL)
    loss = gs[ROW_LOSS, 0]
    dmod_all = dmod_gathered[:, 0:6].reshape(8, 2, 3 * D_MODEL)
    dmod_sh = _cols(dmod_all, chip * ada_cols, ada_cols).transpose(1, 0, 2)
    (g_ada_w, d_ada_w, nm_ada_w, nv_ada_w), _ = _ada_update(c_all, dmod_sh, ada_w, m_ada_w, v_ada_w)

    mats = (("ev_w_out", ev_w_out, g_ev_w_out, m_ev_w_out, v_ev_w_out), ("od_w_in", od_w_in, g_od_w_in, m_od_w_in, v_od_w_in),
            ("od_w_out", od_w_out, g_od_w_out, m_od_w_out, v_od_w_out), ("ev_sg_w", ev_sg_w, g_sg_w, m_ev_sg_w, v_ev_sg_w),
            ("od_w_a", od_w_a, g_w_a, m_od_w_a, v_od_w_a), ("od_w_x", od_w_x, g_w_x, m_od_w_x, v_od_w_x))
    upd = _adamw_matrices([(tr(ev_w_in[0]), g_ev_w_in_t, tr(m_ev_w_in[0]), tr(v_ev_w_in[0]))]
                          + [(w_.reshape(g_.shape), g_, m_.reshape(g_.shape), v_.reshape(g_.shape)) for _, w_, g_, m_, v_ in mats])
    big = {"ev_w_in": tuple(tr(a).reshape(ev_w_in.shape) for a in upd[0])}
    for (name, w_, _, _, _), u in zip(mats, upd[1:]):
        big[name] = tuple(a.reshape(w_.shape) for a in u)
    big["ada_w"] = (g_ada_w, d_ada_w, nm_ada_w, nv_ada_w)

    sh = lambda a: _cols(a, chip * 256, 256)
    small_g = {
        "ada_b": gs[ROW_DMOD:ROW_DMOD + 6].reshape(2, 3 * D_MODEL),
        "ln_g": jnp.stack([gs[ROW_LN], gs[ROW_LN + 2]]), "ln_b": jnp.stack([gs[ROW_LN + 1], gs[ROW_LN + 3]]),
        "ev_sink": gs[ROW_SINK:ROW_SINK + 1, 0:N_Q_HEADS], "ev_sg_ln_g": gs[ROW_SG_LN:ROW_SG_LN + 1, 0:SG_WIDTH],
        "ev_sg_ln_b": gs[ROW_SG_LN:ROW_SG_LN + 1, SG_WIDTH:2 * SG_WIDTH], "ev_sg_b": gs[ROW_SG_B].reshape(N_SG_GROUPS, BLK),
        "od_conv_w": sh(gs[ROW_CONV_W:ROW_CONV_W + 4]), "od_conv_b": sh(gs[ROW_CONV_B:ROW_CONV_B + 1]),
        "od_b_a": sh(gs[ROW_B_A:ROW_B_A + 2]), "od_b_x": sh(gs[ROW_B_X:ROW_B_X + 2]), "od_lam": sh(gs[ROW_LAM:ROW_LAM + 2]),
    }
    small_in = {"ada_b": (ada_b, m_ada_b, v_ada_b), "ln_g": (ln_g, m_ln_g, v_ln_g), "ln_b": (ln_b, m_ln_b, v_ln_b),
                "ev_sink": (ev_sink, m_ev_sink, v_ev_sink), "ev_sg_ln_g": (ev_sg_ln_g, m_ev_sg_ln_g, v_ev_sg_ln_g),
                "ev_sg_ln_b": (ev_sg_ln_b, m_ev_sg_ln_b, v_ev_sg_ln_b), "ev_sg_b": (ev_sg_b, m_ev_sg_b, v_ev_sg_b),
                "od_conv_w": (od_conv_w, m_od_conv_w, v_od_conv_w), "od_conv_b": (od_conv_b, m_od_conv_b, v_od_conv_b),
                "od_b_a": (od_b_a, m_od_b_a, v_od_b_a), "od_b_x": (od_b_x, m_od_b_x, v_od_b_x),
                "od_lam": (od_lam, m_od_lam, v_od_lam)}
    names_small = list(small_g)
    upd = _adamw_small([(small_in[n][0].reshape(small_g[n].shape), small_g[n], small_in[n][1].reshape(small_g[n].shape),
                         small_in[n][2].reshape(small_g[n].shape)) for n in names_small])
    res = dict(big)
    for n, (d_, nm_, nv_) in zip(names_small, upd):
        shape = small_in[n][0].shape
        res[n] = tuple(a.reshape(shape) for a in (small_g[n], d_, nm_, nv_))

    order = ["ada_w", "ada_b", "ln_g", "ln_b", "ev_w_in", "ev_w_out", "ev_sink", "ev_sg_ln_g", "ev_sg_ln_b", "ev_sg_w", "ev_sg_b",
             "od_w_in", "od_conv_w", "od_conv_b", "od_w_a", "od_b_a", "od_w_x", "od_b_x", "od_lam", "od_w_out"]
    return (loss, grad_x.reshape(x.shape), *[res[n][0] for n in order], *[res[n][1] for n in order],
            *[res[n][2] for n in order], *[res[n][3] for n in order])
```

```python
import jax
import jax.numpy as jnp
import numpy as np
from jax import lax
from jax.experimental import pallas as pl
from jax.experimental.pallas import tpu as pltpu

F32 = jnp.float32
BF16 = jnp.bfloat16
MXU_DTYPE = BF16
ACT_DTYPE = MXU_DTYPE

D_MODEL = 1024
HEAD_DIM = 64
N_Q_HEADS = 8
Q_PER_KV = 4
ATTN_WIDTH = 512
BLK = 128
ROPE_DIM = 16
ROPE_THETA = 500000.0
N_SG_GROUPS = 8
SG_WIDTH = 512
EVEN_IN = 2816
ODD_IN = 2048
RNN_HEADS = 8
RG_LRU_C = 8.0
ALPHA = (2 * 2) ** 0.25
LN_EPS = 1e-5
NEG_INF = -1e30
ADAM_LR, ADAM_B1, ADAM_B2, ADAM_EPS, ADAM_WD, ADAM_STEP = 0.001, 0.9, 0.999, 1e-08, 0.01, 10

LANES = 128
VMEM_LIMIT = 56 * 1024 * 1024
MESH = pl.DeviceIdType.MESH


def _mm(a, b):
    return jnp.dot(a.astype(MXU_DTYPE), b.astype(MXU_DTYPE), preferred_element_type=F32)


def _mm_nt(a, b):
    return lax.dot_general(a.astype(MXU_DTYPE), b.astype(MXU_DTYPE), (((1,), (1,)), ((), ())), preferred_element_type=F32)


def _mm_tn(a, b):
    return lax.dot_general(a.astype(MXU_DTYPE), b.astype(MXU_DTYPE), (((0,), (0,)), ((), ())), preferred_element_type=F32)


def _sigmoid(x):
    return 1.0 / (1.0 + jnp.exp(-x))


def _ln_stats(z):
    mu = jnp.mean(z, axis=-1, keepdims=True)
    d = z - mu
    var = jnp.mean(d * d, axis=-1, keepdims=True)
    rstd = lax.rsqrt(var + LN_EPS)
    return d * rstd, rstd


def _ln_bwd(dout, zhat, rstd, g):
    dzh = dout * g
    m1 = jnp.mean(dzh, axis=-1, keepdims=True)
    m2 = jnp.mean(dzh * zhat, axis=-1, keepdims=True)
    return rstd * (dzh - m1 - zhat * m2)


def _group_sum(x, e2):
    hi = x.astype(MXU_DTYPE)
    lo = (x - hi.astype(F32)).astype(MXU_DTYPE)
    return jnp.dot(hi, e2, preferred_element_type=F32) + jnp.dot(lo, e2, preferred_element_type=F32)


def _lane_iota(shape):
    return lax.broadcasted_iota(jnp.int32, shape, 1)


def _to_kv_lanes(t, h):
    src_lo = (h % 2 == 0)
    dst_lo = (h // Q_PER_KV == 0)
    if src_lo != dst_lo:
        t = pltpu.roll(t, HEAD_DIM, 1)
    lane = _lane_iota(t.shape)
    keep = (lane < HEAD_DIM) if dst_lo else (lane >= HEAD_DIM)
    return jnp.where(keep, t, 0.0)


def _from_kv_lanes(t, h):
    src_lo = (h // Q_PER_KV == 0)
    dst_lo = (h % 2 == 0)
    lane = _lane_iota(t.shape)
    keep = (lane < HEAD_DIM) if src_lo else (lane >= HEAD_DIM)
    t = jnp.where(keep, t, 0.0)
    if src_lo != dst_lo:
        t = pltpu.roll(t, HEAD_DIM, 1)
    return t


def _rope(t, cos_t, sin_p, sin_m):
    half = ROPE_DIM // 2
    return t * cos_t + pltpu.roll(t, half, 1) * sin_p + pltpu.roll(t, LANES - half, 1) * sin_m


def _rope_t(d, cos_t, sin_p, sin_m):
    half = ROPE_DIM // 2
    return d * cos_t + pltpu.roll(d * sin_p, LANES - half, 1) + pltpu.roll(d * sin_m, half, 1)


def _band(ref, n, nb):
    prev = jnp.maximum(n - 1, 0)
    nxt = jnp.minimum(n + 1, nb - 1)
    rows = [ref[pl.ds(pl.multiple_of(j * BLK, BLK), BLK), :] for j in (prev, n, nxt)]
    return jnp.concatenate(rows, axis=0)


def _band_bias(n, seq):
    qi = lax.broadcasted_iota(jnp.int32, (BLK, 3 * BLK), 0)
    kj = lax.broadcasted_iota(jnp.int32, (BLK, 3 * BLK), 1)
    k_abs = n * BLK - BLK + kj
    valid = (jnp.abs(kj - BLK - qi) <= BLK) & (k_abs >= 0) & (k_abs < seq)
    bias = jnp.where(valid, 0.0, NEG_INF)
    return jnp.concatenate([bias] * Q_PER_KV, axis=0)


def _stack_heads(tile_of, kv):
    return jnp.concatenate([_to_kv_lanes(tile_of(h // 2), h) for h in range(Q_PER_KV * kv, Q_PER_KV * (kv + 1))], axis=0)


def _per_head_column(vals):
    row = lax.broadcasted_iota(jnp.int32, (Q_PER_KV * BLK, 1), 0)
    return jnp.where(row < BLK, vals[0], jnp.where(row < 2 * BLK, vals[1], jnp.where(row < 3 * BLK, vals[2], vals[3])))


def _softplus_neg(lam):
    e = jnp.exp(-jnp.abs(lam))
    u = 1.0 + e
    log1p_e = jnp.where(u == 1.0, e, jnp.log(u) * (e / (u - 1.0)))
    sp = jnp.maximum(-lam, 0.0) + log1p_e
    dsp = -1.0 / (1.0 + jnp.exp(lam))
    return sp, dsp


def _full(shape):
    return pl.BlockSpec(shape, lambda *_: (0,) * len(shape))


def _const(shape):
    return pl.BlockSpec(shape, lambda *_: (0,) * len(shape), pipeline_mode=pl.Buffered(1))


def _rows(tm, n):
    return pl.BlockSpec((tm, n), lambda i: (i, 0))


def _params(*sem):
    return pltpu.CompilerParams(dimension_semantics=sem, vmem_limit_bytes=VMEM_LIMIT)


def _sds(shape, dtype=F32):
    return jax.ShapeDtypeStruct(shape, dtype)


def _place():
    return lax.axis_index("x"), lax.axis_index("y"), lax.axis_index("c")


class _Rider:
    def __init__(self, ins, out_shapes, n_remote, n_local, plan):
        self.ins, self.out_shapes, self.n_remote, self.n_local, self.plan = list(ins), list(out_shapes), n_remote, n_local, plan

    def scratch(self):
        return [pltpu.SemaphoreType.DMA((self.n_remote,)), pltpu.SemaphoreType.DMA((self.n_remote,)),
                pltpu.SemaphoreType.DMA((max(self.n_local, 1),))]

    def run(self, first, in_refs, out_refs, sems):
        send_sems, recv_sems, local_sems = sems
        sends, recvs, locals_ = self.plan(in_refs, out_refs)
        remote = lambda k, src, dst, to: pltpu.make_async_remote_copy(
            src_ref=src, dst_ref=dst, send_sem=send_sems.at[k], recv_sem=recv_sems.at[k], device_id=to, device_id_type=MESH)
        if first:
            for k, src, dst, to in sends:
                remote(k, src, dst, to).start()
            for j, (src, dst) in enumerate(locals_):
                pltpu.make_async_copy(src, dst, local_sems.at[j]).start()
        else:
            for k, dst, frm in recvs:
                remote(k, dst, dst, frm).wait_recv()
            for k, src, dst, to in sends:
                remote(k, src, dst, to).wait_send()
            for j, (src, dst) in enumerate(locals_):
                pltpu.make_async_copy(src, dst, local_sems.at[j]).wait()


def _other_chips(x, y):
    return [(1 - x, y), (x, 1 - y), (1 - x, 1 - y)]


def _gather_rider(shard):
    hr = shard.shape[0] // 2

    def plan(ins, outs):
        x, y, c = _place()
        mine, src, dst = 2 * x + y, ins[0], outs[0]
        sends, recvs = [], []
        for j, (px, py) in enumerate(_other_chips(x, y)):
            for flip in range(2):
                tc = c if flip == 0 else 1 - c
                sends.append((2 * j + flip, src.at[pl.ds(c * hr, hr)], dst.at[mine, pl.ds(c * hr, hr)], (px, py, tc)))
                recvs.append((2 * j + flip, dst.at[2 * px + py, pl.ds(tc * hr, hr)], (px, py, tc)))
        return sends, recvs, [(src, dst.at[mine])]

    return _Rider([shard], [_sds((4,) + shard.shape, shard.dtype)], 6, 1, plan)


def _sibling_swap_rider(parts):
    n = len(parts)

    def plan(ins, outs):
        x, y, c = _place()
        sibling = (x, y, 1 - c)
        return ([(i, ins[i].at[:, 1 - c], outs[i], sibling) for i in range(n)], [(i, outs[i], sibling) for i in range(n)], [])

    return _Rider(parts, [_sds((4,) + p.shape[2:], p.dtype) for p in parts], n, 0, plan)


def _chip_exchange_rider(parts):
    n = len(parts)

    def plan(ins, outs):
        x, y, c = _place()
        mine = 2 * x + y
        sends, recvs = [], []
        for i in range(n):
            for j, (px, py) in enumerate(_other_chips(x, y)):
                sends.append((3 * i + j, ins[i].at[2 * px + py], outs[i].at[mine], (px, py, c)))
                recvs.append((3 * i + j, outs[i].at[2 * px + py], (px, py, c)))
        return sends, recvs, [(ins[i].at[mine], outs[i].at[mine]) for i in range(n)]

    return _Rider(parts, [_sds(p.shape, p.dtype) for p in parts], 3 * n, n, plan)


def _call(body, name, grid, in_specs, out_specs, out_shape, args, sem, scratch=(), rider=None):
    if rider is None:
        return list(pl.pallas_call(body, name=name, grid=grid, in_specs=in_specs, out_specs=out_specs, out_shape=out_shape,
                                   scratch_shapes=list(scratch), compiler_params=_params(sem))(*args)), []
    n_in, n_out, n_scr = len(in_specs), len(out_specs), len(scratch)
    r_in, r_out = len(rider.ins), len(rider.out_shapes)
    steps = grid[0]

    def riding(*refs):
        ins, r_ins = refs[:n_in], refs[n_in:n_in + r_in]
        outs = refs[n_in + r_in:n_in + r_in + n_out]
        r_outs = refs[n_in + r_in + n_out:n_in + r_in + n_out + r_out]
        scr = refs[n_in + r_in + n_out + r_out:n_in + r_in + n_out + r_out + n_scr]
        sems = refs[n_in + r_in + n_out + r_out + n_scr:]

        @pl.when(pl.program_id(0) == 0)
        def _():
            rider.run(True, r_ins, r_outs, sems)

        body(*ins, *outs, *scr)

        @pl.when(pl.program_id(0) == steps - 1)
        def _():
            rider.run(False, r_ins, r_outs, sems)

    hbm = pl.BlockSpec(memory_space=pl.ANY)
    res = pl.pallas_call(
        riding, name=name, grid=grid, in_specs=list(in_specs) + [hbm] * r_in, out_specs=list(out_specs) + [hbm] * r_out,
        out_shape=list(out_shape) + rider.out_shapes, scratch_shapes=list(scratch) + rider.scratch(),
        compiler_params=_params("arbitrary"),
    )(*args, *rider.ins)
    return list(res[:n_out]), list(res[n_out:])


def _row_tile(seq, want):
    return want if seq % want == 0 else seq


def _rope_tables(posf, seq):
    half = ROPE_DIM // 2
    inv_freq = np.power(np.float32(ROPE_THETA), -np.arange(half, dtype=np.float32) / np.float32(half)).astype(np.float32)
    j = np.arange(LANES) % HEAD_DIM
    invf = jnp.asarray(np.where(j < ROPE_DIM, inv_freq[j % half], 0.0).astype(np.float32).reshape(1, LANES))
    m_p = jnp.asarray(((j >= half) & (j < ROPE_DIM)).astype(np.float32).reshape(1, LANES))
    m_m = jnp.asarray(-(j < half).astype(np.float32).reshape(1, LANES))
    tm = _row_tile(seq, 512)

    def body(pos_ref, invf_ref, mp_ref, mm_ref, cos_ref, sp_ref, sm_ref):
        ang = pos_ref[...] * invf_ref[...]
        s = jnp.sin(ang)
        cos_ref[...] = jnp.cos(ang)
        sp_ref[...] = s * mp_ref[...]
        sm_ref[...] = s * mm_ref[...]

    return pl.pallas_call(
        body, name="rope_tables", grid=(seq // tm,),
        in_specs=[_rows(tm, 1), _full((1, LANES)), _full((1, LANES)), _full((1, LANES))],
        out_specs=[_rows(tm, LANES)] * 3, out_shape=[_sds((seq, LANES))] * 3,
        compiler_params=_params("parallel"),
    )(posf, invf, m_p, m_m)


def _even_proj(x, mod, w_in_t, tabs, seq, rider=None):
    tm = _row_tile(seq, 512)

    def body(x_ref, mod_ref, w_ref, cos_ref, sp_ref, sm_ref, q_ref, k_ref, v_ref, su_ref, sv_ref, g_ref):
        h = x_ref[...] * (1.0 + mod_ref[1:2, :]) + mod_ref[0:1, :]
        p = _mm_nt(h, w_ref[...])
        cos_t, sin_p, sin_m = cos_ref[...], sp_ref[...], sm_ref[...]
        for j in range(ATTN_WIDTH // LANES):
            q_ref[:, j * LANES:(j + 1) * LANES] = _rope(p[:, j * LANES:(j + 1) * LANES], cos_t, sin_p, sin_m).astype(q_ref.dtype)
        k_ref[...] = _rope(p[:, 512:640], cos_t, sin_p, sin_m).astype(k_ref.dtype)
        v_ref[...] = p[:, 640:768].astype(v_ref.dtype)
        su_ref[...] = p[:, 768:1280].astype(su_ref.dtype)
        sv_ref[...] = p[:, 1280:1792].astype(sv_ref.dtype)
        g_ref[...] = p[:, 1792:2816].astype(g_ref.dtype)

    return _call(
        body, "even_proj", (seq // tm,),
        [_rows(tm, D_MODEL), _full((3, D_MODEL)), _const((EVEN_IN, D_MODEL))] + [_rows(tm, LANES)] * 3,
        [_rows(tm, 512), _rows(tm, LANES), _rows(tm, LANES), _rows(tm, 512), _rows(tm, 512), _rows(tm, D_MODEL)],
        [_sds((seq, 512), MXU_DTYPE), _sds((seq, LANES), MXU_DTYPE), _sds((seq, LANES), MXU_DTYPE), _sds((seq, 512), ACT_DTYPE),
         _sds((seq, 512), ACT_DTYPE), _sds((seq, D_MODEL), ACT_DTYPE)],
        (x, mod, w_in_t, *tabs), "parallel", rider=rider)


def _sg_forward(sv, lng, lnb, sgw_ref, sgb, e2):
    vn, vhat, rstd, svo = [], [], [], []
    for j in range(SG_WIDTH // LANES):
        t = sv[:, j * LANES:(j + 1) * LANES]
        mu = _group_sum(t, e2) * (1.0 / HEAD_DIM)
        d = t - mu
        var = _group_sum(d * d, e2) * (1.0 / HEAD_DIM)
        r = lax.rsqrt(var + LN_EPS)
        vh = d * r
        vhat.append(vh)
        rstd.append(r)
        vn.append(vh * lng[:, j * LANES:(j + 1) * LANES] + lnb[:, j * LANES:(j + 1) * LANES])
    lane = _lane_iota((BLK, LANES))
    for j in range(SG_WIDTH // LANES):
        lo = _mm(sgw_ref[2 * j], vn[j])
        hi = _mm(sgw_ref[2 * j + 1], vn[j])
        svo.append(jnp.where(lane < HEAD_DIM, lo, hi) + sgb[:, j * LANES:(j + 1) * LANES])
    return svo, vn, vhat, rstd


def _even_mix(q, k, v, su, sv, sink, sgln_g, sgln_b, sgw, sgb_full, e2, seq, rider=None):
    nb = seq // BLK

    def body(sink_ref, q_ref, k_ref, v_ref, su_ref, sv_ref, lng_ref, lnb_ref, sgw_ref, sgb_ref, e2_ref, ycat_ref, lse_ref,
             svo_ref, vhat_ref, rstd_ref):
        n = pl.program_id(0)
        kband = _band(k_ref, n, nb)
        vband = _band(v_ref, n, nb)
        bias = _band_bias(n, seq)
        lane = _lane_iota((BLK, LANES))
        lse = jnp.zeros((BLK, LANES), F32)
        q_tile = lambda j: q_ref[:, j * LANES:(j + 1) * LANES].astype(F32)
        acc = [jnp.zeros((BLK, LANES), F32) for _ in range(ATTN_WIDTH // LANES)]
        for kv in range(N_Q_HEADS // Q_PER_KV):
            heads = range(Q_PER_KV * kv, Q_PER_KV * (kv + 1))
            sink = _per_head_column([sink_ref[h] for h in heads])
            s = _mm_nt(_stack_heads(q_tile, kv), kband) * (HEAD_DIM ** -0.5) + bias
            m = jnp.maximum(jnp.max(s, axis=1, keepdims=True), sink)
            p = jnp.exp(s - m)
            denom = jnp.sum(p, axis=1, keepdims=True) + jnp.exp(sink - m)
            o4 = _mm(p / denom, vband)
            l4 = m + jnp.log(denom)
            for g, h in enumerate(heads):
                acc[h // 2] = acc[h // 2] + _from_kv_lanes(o4[g * BLK:(g + 1) * BLK], h)
                lse = jnp.where(lane == h, l4[g * BLK:(g + 1) * BLK], lse)
        for j in range(ATTN_WIDTH // LANES):
            ycat_ref[:, j * LANES:(j + 1) * LANES] = acc[j].astype(ycat_ref.dtype)
        lse_ref[...] = lse
        svo, _, vhat, rstd = _sg_forward(sv_ref[...].astype(F32), lng_ref[...], lnb_ref[...], sgw_ref, sgb_ref[...], e2_ref[...])
        for j in range(SG_WIDTH // LANES):
            cs = slice(j * LANES, (j + 1) * LANES)
            ysg = su_ref[:, cs].astype(F32) * svo[j]
            ycat_ref[:, ATTN_WIDTH + j * LANES:ATTN_WIDTH + (j + 1) * LANES] = ysg.astype(ycat_ref.dtype)
            svo_ref[:, cs], vhat_ref[:, cs], rstd_ref[:, cs] = (t.astype(svo_ref.dtype) for t in (svo[j], vhat[j], rstd[j]))

    blk = lambda w: pl.BlockSpec((BLK, w), lambda n: (n, 0))
    return _call(
        body, "even_mix", (nb,),
        [pl.BlockSpec(memory_space=pltpu.SMEM), blk(512), _full((seq, LANES)), _full((seq, LANES)), blk(512), blk(512),
         _full((1, 512)), _full((1, 512)), _full((8, BLK, BLK)), _full((BLK, 512)), _full((LANES, LANES))],
        [blk(D_MODEL), blk(LANES)] + [blk(SG_WIDTH)] * 3,
        [_sds((seq, D_MODEL), ACT_DTYPE), _sds((seq, LANES))] + [_sds((seq, SG_WIDTH), ACT_DTYPE)] * 3,
        (sink, q, k, v, su, sv, sgln_g, sgln_b, sgw, sgb_full, e2), "parallel", rider=rider)


def _even_out(ycat, g, x, mod, mod_next, w_out, w_in4_next, ln_g, ln_b, seq, rider=None):
    tm = _row_tile(seq, 512)
    cs = ODD_IN // 4

    def body(y_ref, g_ref, x_ref, mod_ref, modn_ref, wo_ref, wi_ref, g1_ref, b1_ref, z_ref, x1_ref, xr_ref, gn_ref):
        gg = g_ref[...].astype(F32)
        out = _mm(y_ref[...].astype(F32) * (gg * _sigmoid(gg)), wo_ref[...])
        z = ALPHA * x_ref[...] + mod_ref[2:3, :] * out
        z_ref[...] = z
        zhat, _ = _ln_stats(z)
        x1 = zhat * g1_ref[...] + b1_ref[...]
        x1_ref[...] = x1
        hb = (x1 * (1.0 + modn_ref[1:2, :]) + modn_ref[0:1, :]).astype(MXU_DTYPE)
        for s in range(2):
            xr_ref[:, s * cs:(s + 1) * cs] = jnp.dot(hb, wi_ref[s], preferred_element_type=F32)
            gn_ref[:, s * cs:(s + 1) * cs] = jnp.dot(hb, wi_ref[2 + s], preferred_element_type=F32).astype(gn_ref.dtype)

    return _call(
        body, "even_out", (seq // tm,),
        [_rows(tm, D_MODEL)] * 3 + [_full((3, D_MODEL)), _full((3, D_MODEL)), _const((D_MODEL, D_MODEL)), _const((4, D_MODEL, cs)),
                                    _full((1, D_MODEL)), _full((1, D_MODEL))],
        [_rows(tm, D_MODEL)] * 4, [_sds((seq, D_MODEL))] * 3 + [_sds((seq, D_MODEL), ACT_DTYPE)],
        (ycat, g, x, mod, mod_next, w_out, w_in4_next, ln_g, ln_b), "parallel", rider=rider)


def _halo_specs(tm, seq, width, order=lambda i: i):
    per = tm // 8
    last = seq // 8 - 1
    return [pl.BlockSpec((8, width), lambda i: (jnp.maximum(order(i) * per - 1, 0), 0)),
            pl.BlockSpec((tm, width), lambda i: (order(i), 0)),
            pl.BlockSpec((8, width), lambda i: (jnp.minimum((order(i) + 1) * per, last), 0))]


def _extended(prev_ref, main_ref, next_ref, i, n_steps):
    prev = jnp.where(i > 0, prev_ref[...], 0.0)
    nxt = jnp.where(i < n_steps - 1, next_ref[...], 0.0)
    return jnp.concatenate([prev, main_ref[...], nxt], axis=0)


def _shifted(ext, off, tm):
    if off == 0:
        return ext[8:8 + tm]
    return pltpu.roll(ext, (-off) % ext.shape[0], 0)[8:8 + tm]


SCAN_SUB = 8


def _lru_gate(xh, pre, bias, sp, hs, d):
    r = _sigmoid(pre[:, 0:LANES] + bias[2 * d:2 * d + 1, hs])
    ig = _sigmoid(pre[:, LANES:2 * LANES] + bias[2 * d + 1:2 * d + 2, hs])
    neg_log_a = RG_LRU_C * r * sp[d:d + 1, hs]
    a = jnp.exp(-neg_log_a)
    u = jnp.tanh(neg_log_a) * (a * a + 1.0)
    inv_s = lax.rsqrt(jnp.maximum(u, jnp.finfo(F32).tiny))
    return r, ig, a, u * inv_s, inv_s


def _conv_block(xp_ref, xm_ref, xn_ref, cw_ref, cb_ref, blk, steps, tm):
    ext = _extended(xp_ref, xm_ref, xn_ref, blk, steps)
    return cb_ref[...] + sum(cw_ref[kk:kk + 1, :] * _shifted(ext, kk - 2, tm) for kk in range(4))


def _scan_tiles(a_ref, b_ref, h_ref, hprev_ref, carry_h, carry_a, rows, descending, post):
    sub = SCAN_SUB
    tiles = rows // sub
    row = lax.broadcasted_iota(jnp.int32, (sub, D_MODEL), 0)

    def shift(v, d, fill):
        if descending:
            return jnp.where(row <= sub - 1 - d, pltpu.roll(v, sub - d, 0), fill)
        return jnp.where(row >= d, pltpu.roll(v, d, 0), fill)

    def last(v):
        return jnp.broadcast_to(v[0:1, :] if descending else v[sub - 1:sub, :], v.shape)

    def tile(j, c):
        ch, ca = c
        r0 = pl.multiple_of(((tiles - 1 - j) if descending else j) * sub, sub)
        at = a_ref[pl.ds(r0, sub), :]
        bt = b_ref[pl.ds(r0, sub), :]
        coef = shift(at, 1, ca) if post else at
        acc_a, acc_b = coef, bt
        for d in (1, 2, 4):
            acc_b = acc_b + acc_a * shift(acc_b, d, 0.0)
            acc_a = acc_a * shift(acc_a, d, 1.0)
        h = acc_b + acc_a * ch
        h_ref[pl.ds(r0, sub), :] = h
        if post:
            return last(h), last(at)
        hprev_ref[pl.ds(r0, sub), :] = shift(h, 1, ch)
        return last(h), ca

    ch, ca = lax.fori_loop(0, tiles, tile, (carry_h[...], carry_a[...]), unroll=4)
    carry_h[...] = ch
    carry_a[...] = ca


def _lru_fwd(xr, xc, conv_w, conv_b, wcat, bias, lam, seq, d):
    tb = _row_tile(seq, 512)
    steps = seq // tb
    descending = d == 1
    order = (lambda i: steps - 1 - i) if descending else (lambda i: i)
    with_conv = xc is None
    n_x = 5 if with_conv else 1

    def body(*refs):
        x_refs, (w_ref, bias_ref, lam_ref) = refs[:n_x], refs[n_x:n_x + 3]
        h_ref, hp_ref, a_ref, r_ref, i_ref, s_ref, q_ref = refs[n_x + 3:n_x + 10]
        b_scr, carry_h, carry_a = refs[-3:]
        i = pl.program_id(0)

        @pl.when(i == 0)
        def _():
            carry_h[...] = jnp.zeros_like(carry_h)
            carry_a[...] = jnp.zeros_like(carry_a)

        if with_conv:
            xc_ref = refs[n_x + 10]
            xc_ref[...] = _conv_block(*x_refs, order(i), steps, tb)
        else:
            xc_ref = x_refs[0]
        sp, _ = _softplus_neg(lam_ref[...])
        bias = bias_ref[...]
        for h in range(RNN_HEADS):
            hs = slice(h * LANES, (h + 1) * LANES)
            xh = xc_ref[:, hs]
            r, ig, a, s, q = _lru_gate(xh, _mm(xh, w_ref[h, :, 2 * d * LANES:2 * (d + 1) * LANES]), bias, sp, hs, d)
            a_ref[:, hs] = a
            b_scr[:, hs] = s * ig * xh
            for ref, val in ((r_ref, r), (i_ref, ig), (s_ref, s), (q_ref, q)):
                ref[:, hs] = val.astype(ref.dtype)
        _scan_tiles(a_ref, b_scr, h_ref, hp_ref, carry_h, carry_a, tb, descending, post=False)

    row_spec = pl.BlockSpec((tb, D_MODEL), lambda i: (order(i), 0))
    if with_conv:
        x_specs, x_args = _halo_specs(tb, seq, D_MODEL, order) + [_full((4, D_MODEL)), _full((1, D_MODEL))], (xr, xr, xr, conv_w, conv_b)
    else:
        x_specs, x_args = [row_spec], (xc,)
    n_out = 8 if with_conv else 7
    return pl.pallas_call(
        body, name="lru_fwd_%d" % d, grid=(steps,),
        in_specs=x_specs + [_full((8, LANES, 512)), _full((4, D_MODEL)), _full((2, D_MODEL))],
        out_specs=[row_spec] * n_out,
        out_shape=[_sds((seq, D_MODEL))] * 3 + [_sds((seq, D_MODEL), ACT_DTYPE)] * 4 + [_sds((seq, D_MODEL))] * (n_out - 7),
        scratch_shapes=[pltpu.VMEM((tb, D_MODEL), F32)] + [pltpu.VMEM((SCAN_SUB, D_MODEL), F32)] * 2,
        compiler_params=_params("arbitrary"),
    )(*x_args, wcat, bias, lam)


def _odd_out_and_loss(hf, hr, g, x1, tgt, mod, w_out, ln_g, ln_b, seq):
    tm = _row_tile(seq, 512)

    def body(hf_ref, hr_ref, g_ref, x_ref, t_ref, mod_ref, w_ref, lg_ref, lb_ref,
             dhs_ref, dg_ref, dres_ref, loss_ref, dw_ref, vec_ref):
        @pl.when(pl.program_id(0) == 0)
        def _():
            loss_ref[...] = jnp.zeros_like(loss_ref)
            dw_ref[...] = jnp.zeros_like(dw_ref)
            vec_ref[...] = jnp.zeros_like(vec_ref)

        gg = g_ref[...].astype(F32)
        sg = _sigmoid(gg)
        silu = gg * sg
        hsum = hf_ref[...] + hr_ref[...]
        y = hsum * silu
        out = _mm(y, w_ref[...])
        gate = mod_ref[2:3, :]
        z = ALPHA * x_ref[...] + gate * out
        zhat, rstd = _ln_stats(z)
        x2 = zhat * lg_ref[...] + lb_ref[...]
        err = x2 - t_ref[...]
        loss_ref[...] += 0.5 * jnp.sum(jnp.mean(err * err, axis=-1, keepdims=True))
        dx2 = err * (1.0 / D_MODEL)
        dz = _ln_bwd(dx2, zhat, rstd, lg_ref[...])
        vec_ref[0:1, :] += jnp.sum(dx2 * zhat, axis=0, keepdims=True)
        vec_ref[1:2, :] += jnp.sum(dx2, axis=0, keepdims=True)
        vec_ref[2:3, :] += jnp.sum(dz * out, axis=0, keepdims=True)
        dres_ref[...] = ALPHA * dz
        dout = gate * dz
        dw_ref[...] += _mm_tn(y, dout)
        dy = _mm_nt(dout, w_ref[...])
        dhs_ref[...] = dy * silu
        dg_ref[...] = (dy * hsum * (sg * (1.0 + gg * (1.0 - sg)))).astype(dg_ref.dtype)

    return pl.pallas_call(
        body, name="odd_out_loss", grid=(seq // tm,),
        in_specs=[_rows(tm, D_MODEL)] * 5 + [_full((3, D_MODEL)), _const((D_MODEL, D_MODEL)),
                                             _full((1, D_MODEL)), _full((1, D_MODEL))],
        out_specs=[_rows(tm, D_MODEL)] * 3 + [_full((8, LANES)), _full((D_MODEL, D_MODEL)), _full((8, D_MODEL))],
        out_shape=[_sds((seq, D_MODEL)), _sds((seq, D_MODEL), ACT_DTYPE), _sds((seq, D_MODEL)), _sds((8, LANES)),
                   _sds((D_MODEL, D_MODEL)), _sds((8, D_MODEL))],
        compiler_params=_params("arbitrary"),
    )(hf, hr, g, x1, tgt, mod, w_out, ln_g, ln_b)


def _lru_bwd(xc, dhs, hprev, a_d, r_d, i_d, s_d, q_d, wcat, lam, seq, d):
    tb = _row_tile(seq, 512)
    steps = seq // tb
    descending = d == 0
    order = (lambda i: steps - 1 - i) if descending else (lambda i: i)
    cols = slice(2 * d * LANES, 2 * (d + 1) * LANES)

    def body(xc_ref, dhs_ref, hp_ref, a_ref, r_ref, i_ref, s_ref, q_ref, w_ref, lam_ref, dxc_ref, dw_ref, vec_ref,
             g_scr, carry_h, carry_a):
        i = pl.program_id(0)

        @pl.when(i == 0)
        def _():
            dw_ref[...] = jnp.zeros_like(dw_ref)
            vec_ref[...] = jnp.zeros_like(vec_ref)
            carry_h[...] = jnp.zeros_like(carry_h)
            carry_a[...] = jnp.zeros_like(carry_a)

        sp, dsp = _softplus_neg(lam_ref[...])
        _scan_tiles(a_ref, dhs_ref, g_scr, None, carry_h, carry_a, tb, descending, post=True)
        for h in range(RNN_HEADS):
            hs = slice(h * LANES, (h + 1) * LANES)
            xh, a = xc_ref[:, hs], a_ref[:, hs]
            r, ig, s = r_ref[:, hs].astype(F32), i_ref[:, hs].astype(F32), s_ref[:, hs].astype(F32)
            db = g_scr[:, hs]
            da = db * hp_ref[:, hs]
            dlog_a = da * a - (db * ig * xh) * (a * a * q_ref[:, hs].astype(F32))
            dpr = dlog_a * (-RG_LRU_C) * sp[d:d + 1, hs] * r * (1.0 - r)
            dpi = db * s * xh * ig * (1.0 - ig)
            vec_ref[0:1, hs] += jnp.sum(dpr, axis=0, keepdims=True)
            vec_ref[1:2, hs] += jnp.sum(dpi, axis=0, keepdims=True)
            vec_ref[2:3, hs] += jnp.sum(dlog_a * r, axis=0, keepdims=True) * (-RG_LRU_C) * dsp[d:d + 1, hs]
            dcat = jnp.concatenate([dpr, dpi], axis=1)
            dw_ref[h] += _mm_tn(xh, dcat)
            dxc_ref[:, hs] = db * s * ig + _mm_nt(dcat, w_ref[h, :, cols])

    row_spec = pl.BlockSpec((tb, D_MODEL), lambda i: (order(i), 0))
    return pl.pallas_call(
        body, name="lru_bwd_%d" % d, grid=(steps,),
        in_specs=[row_spec] * 8 + [_full((8, LANES, 512)), _full((2, D_MODEL))],
        out_specs=[row_spec, _full((8, LANES, 2 * LANES)), _full((8, D_MODEL))],
        out_shape=[_sds((seq, D_MODEL)), _sds((8, LANES, 2 * LANES)), _sds((8, D_MODEL))],
        scratch_shapes=[pltpu.VMEM((tb, D_MODEL), F32)] + [pltpu.VMEM((SCAN_SUB, D_MODEL), F32)] * 2,
        compiler_params=_params("arbitrary"),
    )(xc, dhs, hprev, a_d, r_d, i_d, s_d, q_d, wcat, lam)


def _odd_proj_bwd(dxc_f, dxc_r, xr, dg, x1, dres, mod, conv_w, w_in4, seq):
    tm = _row_tile(seq, 512)
    steps = seq // tm

    def body(fp_ref, fm_ref, fn_ref, rp_ref, rm_ref, rn_ref, xp_ref, xm_ref, xn_ref, dg_ref, x_ref, dres_ref, mod_ref, cw_ref,
             w_ref, dx_ref, dw_ref, vec_ref, dpb_ref):
        i = pl.program_id(0)

        @pl.when(i == 0)
        def _():
            vec_ref[...] = jnp.zeros_like(vec_ref)
            dw_ref[...] = jnp.zeros_like(dw_ref)

        dxc_m = fm_ref[...] + rm_ref[...]
        dext = jnp.concatenate([jnp.where(i > 0, fp_ref[...] + rp_ref[...], 0.0), dxc_m,
                                jnp.where(i < steps - 1, fn_ref[...] + rn_ref[...], 0.0)], axis=0)
        xext = _extended(xp_ref, xm_ref, xn_ref, i, steps)
        dxr = sum(cw_ref[kk:kk + 1, :] * _shifted(dext, 2 - kk, tm) for kk in range(4))
        for kk in range(4):
            vec_ref[kk:kk + 1, :] += jnp.sum(dxc_m * _shifted(xext, kk - 2, tm), axis=0, keepdims=True)
        vec_ref[4:5, :] += jnp.sum(dxc_m, axis=0, keepdims=True)
        dpb_ref[:, :D_MODEL] = dxr.astype(dpb_ref.dtype)
        dpb_ref[:, D_MODEL:] = dg_ref[...].astype(dpb_ref.dtype)
        cs = ODD_IN // 4
        dh = sum(_mm_nt(dpb_ref[:, s * cs:(s + 1) * cs], w_ref[s]) for s in range(4))
        x = x_ref[...]
        h_t = (x * (1.0 + mod_ref[1:2, :]) + mod_ref[0:1, :]).T.astype(MXU_DTYPE)
        for s in range(4):
            dw_ref[s] += jnp.dot(h_t, dpb_ref[:, s * cs:(s + 1) * cs], preferred_element_type=F32)
        vec_ref[5:6, :] += jnp.sum(dh, axis=0, keepdims=True)
        vec_ref[6:7, :] += jnp.sum(dh * x, axis=0, keepdims=True)
        dx_ref[...] = dres_ref[...] + dh * (1.0 + mod_ref[1:2, :])

    return pl.pallas_call(
        body, name="odd_proj_bwd", grid=(steps,),
        in_specs=_halo_specs(tm, seq, D_MODEL) * 3 + [_rows(tm, D_MODEL)] * 3
        + [_full((3, D_MODEL)), _full((4, D_MODEL)), _const((4, D_MODEL, ODD_IN // 4))],
        out_specs=[_rows(tm, D_MODEL), _const((4, D_MODEL, ODD_IN // 4)), _full((8, D_MODEL))],
        out_shape=[_sds((seq, D_MODEL)), _sds((4, D_MODEL, ODD_IN // 4)), _sds((8, D_MODEL))],
        scratch_shapes=[pltpu.VMEM((tm, ODD_IN), MXU_DTYPE)],
        compiler_params=_params("arbitrary"),
    )(dxc_f, dxc_f, dxc_f, dxc_r, dxc_r, dxc_r, xr, xr, xr, dg, x1, dres, mod, conv_w, w_in4)


def _even_out_bwd(dx1, z, ycat, g, mod, ln_g, w_out, seq, rider=None):
    tm = _row_tile(seq, 512)
    steps = seq // tm

    def body(dx_ref, z_ref, y_ref, g_ref, mod_ref, lg_ref, w_ref, dy_ref, dg_ref, dres_ref, dw_ref, vec_ref):
        i = pl.program_id(0)

        @pl.when(i == 0)
        def _():
            dw_ref[...] = jnp.zeros_like(dw_ref)
            vec_ref[...] = jnp.zeros_like(vec_ref)

        zhat, rstd = _ln_stats(z_ref[...])
        dx1_ = dx_ref[...]
        dz = _ln_bwd(dx1_, zhat, rstd, lg_ref[...])
        vec_ref[0:1, :] += jnp.sum(dx1_ * zhat, axis=0, keepdims=True)
        vec_ref[1:2, :] += jnp.sum(dx1_, axis=0, keepdims=True)
        dres_ref[...] = ALPHA * dz
        gate = mod_ref[2:3, :]
        gg = g_ref[...].astype(F32)
        sg = _sigmoid(gg)
        silu = gg * sg
        ycat_ = y_ref[...].astype(F32)
        dw_ref[...] += _mm_tn(ycat_ * silu, dz)
        dy = _mm_nt(gate * dz, w_ref[...])
        dy_ref[...] = (dy * silu).astype(dy_ref.dtype)
        dg_ref[...] = (dy * ycat_ * (sg * (1.0 + gg * (1.0 - sg)))).astype(dg_ref.dtype)

        @pl.when(i == steps - 1)
        def _():
            m_acc = dw_ref[...]
            vec_ref[2:3, :] = jnp.sum(w_ref[...].astype(F32) * m_acc, axis=0, keepdims=True)
            dw_ref[...] = m_acc * gate

    return _call(
        body, "even_out_bwd", (steps,),
        [_rows(tm, D_MODEL)] * 4 + [_full((3, D_MODEL)), _full((1, D_MODEL)), _const((D_MODEL, D_MODEL))],
        [_rows(tm, D_MODEL)] * 3 + [_full((D_MODEL, D_MODEL)), _full((8, D_MODEL))],
        [_sds((seq, D_MODEL), ACT_DTYPE), _sds((seq, D_MODEL), ACT_DTYPE), _sds((seq, D_MODEL)), _sds((D_MODEL, D_MODEL)),
         _sds((8, D_MODEL))],
        (dx1, z, ycat, g, mod, ln_g, w_out), "arbitrary", rider=rider)


def _even_mix_bwd(q, k, v, lse, ycat, dycat, su, svo_s, vhat_s, rstd_s, sink, sgln_g, sgln_b, sgw, e2, e8, seq, rider=None):
    nb = seq // BLK

    def body(sink_ref, q_ref, k_ref, v_ref, lse_ref, y_ref, dy_ref, su_ref, svo_ref, vhat_ref, rstd_ref, lng_ref, lnb_ref, sgw_ref,
             e2_ref, e8_ref, dq_ref, dsu_ref, dsv_ref, dk_ref, dv_ref, dsgw_ref, dsgb_ref, vec_ref, dsink_ref, dsgb_acc):
        n = pl.program_id(0)

        @pl.when(n == 0)
        def _():
            dk_ref[...] = jnp.zeros_like(dk_ref)
            dv_ref[...] = jnp.zeros_like(dv_ref)
            dsgw_ref[...] = jnp.zeros_like(dsgw_ref)
            dsgb_acc[...] = jnp.zeros_like(dsgb_acc)
            vec_ref[...] = jnp.zeros_like(vec_ref)
            dsink_ref[...] = jnp.zeros_like(dsink_ref)

        kband = _band(k_ref, n, nb)
        vband = _band(v_ref, n, nb)
        bias = _band_bias(n, seq)
        lane = _lane_iota((BLK, LANES))
        row8 = lax.broadcasted_iota(jnp.int32, (8, LANES), 0)
        lse = lse_ref[...]
        dkb = jnp.zeros((LANES, 3 * BLK), F32)
        dvb = jnp.zeros((LANES, 3 * BLK), F32)
        dsink = jnp.zeros((8, LANES), F32)
        q_tile = lambda j: q_ref[:, j * LANES:(j + 1) * LANES].astype(F32)
        do_tile = lambda j: dy_ref[:, j * LANES:(j + 1) * LANES].astype(F32)
        dq = [jnp.zeros((BLK, LANES), F32) for _ in range(ATTN_WIDTH // LANES)]
        for kv in range(N_Q_HEADS // Q_PER_KV):
            heads = range(Q_PER_KV * kv, Q_PER_KV * (kv + 1))
            lse4, delta4 = [], []
            for h in heads:
                head_lanes = (lane < HEAD_DIM) if h % 2 == 0 else (lane >= HEAD_DIM)
                lse4.append(jnp.sum(jnp.where(lane == h, lse, 0.0), axis=1, keepdims=True))
                o_tile = y_ref[:, (h // 2) * LANES:(h // 2 + 1) * LANES].astype(F32)
                delta4.append(jnp.sum(jnp.where(head_lanes, do_tile(h // 2) * o_tile, 0.0), axis=1, keepdims=True))
            lse4, delta4 = jnp.concatenate(lse4, axis=0), jnp.concatenate(delta4, axis=0)
            q4, do4 = _stack_heads(q_tile, kv), _stack_heads(do_tile, kv)
            s = _mm_nt(q4, kband) * (HEAD_DIM ** -0.5) + bias
            p = jnp.exp(s - lse4)
            wsink = jnp.exp(_per_head_column([sink_ref[h] for h in heads]) - lse4) * delta4
            ds = p * (_mm_nt(do4, vband) - delta4) * (HEAD_DIM ** -0.5)
            dq4 = _mm(ds, kband)
            dkb = dkb + _mm_tn(q4, ds)
            dvb = dvb + _mm_tn(do4, p)
            for g, h in enumerate(heads):
                dq[h // 2] = dq[h // 2] + _from_kv_lanes(dq4[g * BLK:(g + 1) * BLK], h)
                dsink = dsink + jnp.where(row8 == h, -jnp.sum(wsink[g * BLK:(g + 1) * BLK]), 0.0)
        for j in range(ATTN_WIDTH // LANES):
            dq_ref[:, j * LANES:(j + 1) * LANES] = dq[j].astype(dq_ref.dtype)
        dsink_ref[...] += dsink
        prev = jnp.maximum(n - 1, 0)
        nxt = jnp.minimum(n + 1, nb - 1)
        for part, blk_i in enumerate((prev, n, nxt)):
            rows = pl.ds(pl.multiple_of(blk_i * BLK, BLK), BLK)
            dk_ref[rows, :] += dkb[:, part * BLK:(part + 1) * BLK].T
            dv_ref[rows, :] += dvb[:, part * BLK:(part + 1) * BLK].T

        e2 = e2_ref[...]
        lng, lnb = lng_ref[...], lnb_ref[...]
        for j in range(SG_WIDTH // LANES):
            cs = slice(j * LANES, (j + 1) * LANES)
            vhat = vhat_ref[:, cs].astype(F32)
            vn = vhat * lng[:, cs] + lnb[:, cs]
            dysg = dy_ref[:, ATTN_WIDTH + j * LANES:ATTN_WIDTH + (j + 1) * LANES].astype(F32)
            dsu_ref[:, cs] = (dysg * svo_ref[:, cs].astype(F32)).astype(dsu_ref.dtype)
            dsvo = dysg * su_ref[:, cs].astype(F32)
            dsgb_acc[:, cs] += dsvo
            d_lo = jnp.where(lane < HEAD_DIM, dsvo, 0.0)
            d_hi = dsvo - d_lo
            dsgw_ref[2 * j] += _mm_nt(d_lo, vn)
            dsgw_ref[2 * j + 1] += _mm_nt(d_hi, vn)
            dvn = _mm_tn(sgw_ref[2 * j], d_lo) + _mm_tn(sgw_ref[2 * j + 1], d_hi)
            vec_ref[0:1, cs] += jnp.sum(dvn * vhat, axis=0, keepdims=True)
            vec_ref[1:2, cs] += jnp.sum(dvn, axis=0, keepdims=True)
            dvh = dvn * lng[:, cs]
            m1 = _group_sum(dvh, e2) * (1.0 / HEAD_DIM)
            m2 = _group_sum(dvh * vhat, e2) * (1.0 / HEAD_DIM)
            dsv_ref[:, cs] = (rstd_ref[:, cs].astype(F32) * (dvh - m1 - vhat * m2)).astype(dsv_ref.dtype)

        @pl.when(n == nb - 1)
        def _():
            rest = dsgb_acc[...]
            total = jnp.zeros((8, BLK), F32)
            for _ in range(3):
                part = rest.astype(MXU_DTYPE)
                total = total + lax.dot_general(e8_ref[...], part, (((1,), (1,)), ((), ())), preferred_element_type=F32)
                rest = rest - part.astype(F32)
            dsgb_ref[...] = total

    blk = lambda w: pl.BlockSpec((BLK, w), lambda n: (n, 0))
    return _call(
        body, "even_mix_bwd", (nb,),
        [pl.BlockSpec(memory_space=pltpu.SMEM), blk(512), _full((seq, LANES)), _full((seq, LANES)), blk(LANES),
         blk(D_MODEL), blk(D_MODEL), blk(512), blk(512), blk(512), blk(512), _full((1, 512)), _full((1, 512)), _full((8, BLK, BLK)),
         _full((LANES, LANES)), _full((8, 512))],
        [blk(512), blk(512), blk(512), _full((seq, LANES)), _full((seq, LANES)), _full((8, BLK, BLK)),
         _full((8, BLK)), _full((8, 512)), _full((8, LANES))],
        [_sds((seq, 512), ACT_DTYPE), _sds((seq, 512), ACT_DTYPE), _sds((seq, 512), ACT_DTYPE), _sds((seq, LANES)), _sds((seq, LANES)),
         _sds((8, BLK, BLK)), _sds((8, BLK)), _sds((8, 512)), _sds((8, LANES))],
        (sink, q, k, v, lse, ycat, dycat, su, svo_s, vhat_s, rstd_s, sgln_g, sgln_b, sgw, e2, e8), "arbitrary",
        scratch=[pltpu.VMEM((BLK, 512), F32)], rider=rider)


def _even_proj_bwd(dq, dk, dv, dsu, dsv, dg, x, dres, mod, tabs, w_in_t, seq):
    tm = _row_tile(seq, 512)

    def body(dq_ref, dk_ref, dv_ref, dsu_ref, dsv_ref, dg_ref, x_ref, dres_ref, mod_ref, cos_ref, sp_ref, sm_ref, wt_ref,
             dx_ref, dw_ref, vec_ref, dpb_ref):
        @pl.when(pl.program_id(0) == 0)
        def _():
            vec_ref[...] = jnp.zeros_like(vec_ref)
            dw_ref[...] = jnp.zeros_like(dw_ref)

        cos_t, sin_p, sin_m = cos_ref[...], sp_ref[...], sm_ref[...]
        dt = dpb_ref.dtype
        for j in range(ATTN_WIDTH // LANES):
            cs = slice(j * LANES, (j + 1) * LANES)
            dpb_ref[:, cs] = _rope_t(dq_ref[:, cs].astype(F32), cos_t, sin_p, sin_m).astype(dt)
        dpb_ref[:, 512:640] = _rope_t(dk_ref[...], cos_t, sin_p, sin_m).astype(dt)
        dpb_ref[:, 640:768] = dv_ref[...].astype(dt)
        dpb_ref[:, 768:1280] = dsu_ref[...].astype(dt)
        dpb_ref[:, 1280:1792] = dsv_ref[...].astype(dt)
        dpb_ref[:, 1792:2816] = dg_ref[...].astype(dt)
        dpb = dpb_ref[...]
        dh = jnp.dot(dpb, wt_ref[...], preferred_element_type=F32)
        x_ = x_ref[...]
        hb = (x_ * (1.0 + mod_ref[1:2, :]) + mod_ref[0:1, :]).astype(MXU_DTYPE)
        dw_ref[...] += _mm_tn(dpb, hb)
        vec_ref[0:1, :] += jnp.sum(dh, axis=0, keepdims=True)
        vec_ref[1:2, :] += jnp.sum(dh * x_, axis=0, keepdims=True)
        dx_ref[...] = dres_ref[...] + dh * (1.0 + mod_ref[1:2, :])

    return pl.pallas_call(
        body, name="even_proj_bwd", grid=(seq // tm,),
        in_specs=[_rows(tm, 512), _rows(tm, LANES), _rows(tm, LANES), _rows(tm, 512), _rows(tm, 512), _rows(tm, D_MODEL),
                  _rows(tm, D_MODEL), _rows(tm, D_MODEL), _full((3, D_MODEL))] + [_rows(tm, LANES)] * 3
        + [_const((EVEN_IN, D_MODEL))],
        out_specs=[_rows(tm, D_MODEL), _const((EVEN_IN, D_MODEL)), _full((8, D_MODEL))],
        out_shape=[_sds((seq, D_MODEL)), _sds((EVEN_IN, D_MODEL)), _sds((8, D_MODEL))],
        scratch_shapes=[pltpu.VMEM((tm, EVEN_IN), MXU_DTYPE)],
        compiler_params=_params("arbitrary"),
    )(dq, dk, dv, dsu, dsv, dg, x, dres, mod, *tabs, w_in_t)


def _local_step(x, posf, tgt, mod, w, seq, ride=None):
    rid = lambda make, *a: None if ride is None else make(*a)
    mxu = lambda a: a.astype(MXU_DTYPE)
    row = lambda a: a.reshape(1, -1)
    tabs = _rope_tables(posf, seq)
    e2 = mxu(jnp.kron(jnp.eye(2, dtype=F32), jnp.ones((HEAD_DIM, HEAD_DIM), F32)))
    e8 = mxu(jnp.repeat(jnp.eye(N_SG_GROUPS, dtype=F32), HEAD_DIM, axis=1))
    sgw = mxu(w["ev_sg_w"])
    sgb_full = jnp.repeat(w["ev_sg_b"].T, HEAD_DIM, axis=1)
    sgln_g, sgln_b = row(w["ev_sg_ln_g"]), row(w["ev_sg_ln_b"])
    sink = w["ev_sink"].reshape(N_Q_HEADS)
    ev_w_in_t = mxu(w["ev_w_in_t"])
    if ride is None:
        ev_w_out, od_w_in, od_w_out = mxu(w["ev_w_out"]), mxu(w["od_w_in"]), mxu(w["od_w_out"])
    wcat = mxu(jnp.concatenate([w["od_w_a"][0], w["od_w_x"][0], w["od_w_a"][1], w["od_w_x"][1]], axis=2))
    gate_bias = jnp.stack([w["od_b_a"][0], w["od_b_x"][0], w["od_b_a"][1], w["od_b_x"][1]])
    conv_b = row(w["od_conv_b"])
    ln_g, ln_b = w["ln_g"], w["ln_b"]

    (q, k, v, su, sv, g0), got = _even_proj(x, mod[0], ev_w_in_t, tabs, seq, rid(_gather_rider, ride and ride["ev_w_out"]))
    if ride is not None:
        ev_w_out = got[0].reshape(D_MODEL, D_MODEL)
    (ycat, lse, *sg_saved), got = _even_mix(q, k, v, su, sv, sink, sgln_g, sgln_b, sgw, sgb_full, e2, seq,
                                 rid(_gather_rider, ride and ride["od_w_in"]))
    if ride is not None:
        od_w_in = got[0]
    (z0, x1, xr, g1), got = _even_out(ycat, g0, x, mod[0], mod[1], ev_w_out, od_w_in, ln_g[0:1], ln_b[0:1], seq,
                                      rid(_gather_rider, ride and ride["od_w_out"]))
    if ride is not None:
        od_w_out = got[0].reshape(D_MODEL, D_MODEL)
    lru = (w["od_conv_w"], conv_b, wcat, gate_bias, w["od_lam"], seq)
    hf, hpf, *saved_f, xc = _lru_fwd(xr, None, *lru, 0)
    hr, hpr, *saved_r = _lru_fwd(xr, xc, *lru, 1)
    dhs, dg1, dres1, loss, d_od_w_out, vec_o = _odd_out_and_loss(hf, hr, g1, x1, tgt, mod[1], od_w_out, ln_g[1:2], ln_b[1:2], seq)
    dxc_f, dw_f, vec_f = _lru_bwd(xc, dhs, hpf, *saved_f, wcat, w["od_lam"], seq, 0)
    dxc_r, dw_r, vec_r = _lru_bwd(xc, dhs, hpr, *saved_r, wcat, w["od_lam"], seq, 1)
    dx1, d_od_w_in, vec_p = _odd_proj_bwd(dxc_f, dxc_r, xr, dg1, x1, dres1, mod[1], w["od_conv_w"], od_w_in, seq)
    d_od_w_a = jnp.stack([dw_f[:, :, 0:128], dw_r[:, :, 0:128]])
    d_od_w_x = jnp.stack([dw_f[:, :, 128:256], dw_r[:, :, 128:256]])
    od_parts = [d_od_w_in.reshape(4, 2, 512, 512), d_od_w_out.reshape(4, 2, 128, D_MODEL),
                d_od_w_a.reshape(4, 2, 2 * BLK, BLK), d_od_w_x.reshape(4, 2, 2 * BLK, BLK)]
    (dycat, dg0, dres0, d_ev_w_out, vec_e), got_od = _even_out_bwd(dx1, z0, ycat, g0, mod[0], ln_g[0:1], ev_w_out, seq,
                                                                   rid(_sibling_swap_rider, od_parts))
    if ride is not None:
        od_sums = _sum_sibling(ride["core"], od_parts, got_od, [ride["wire"]] * 4, "sum_sibling_od")
    (dq, dsu, dsv, dk, dv, d_sgw, d_sgb, vec_s, d_sink), od_slots = _even_mix_bwd(
        q, k, v, lse, ycat, dycat, su, *sg_saved, sink, sgln_g, sgln_b, sgw, e2, e8, seq,
        rid(_chip_exchange_rider, ride and od_sums))
    grad_x, d_ev_w_in_t, vec_x = _even_proj_bwd(dq, dk, dv, dsu, dsv, dg0, x, dres0, mod[0], tabs, ev_w_in_t, seq)

    rows, dmod_blk = _pack_small(vec_x, vec_e, vec_p, vec_o, vec_f, vec_r, vec_s, d_sink, d_sgb, loss)
    grads = {"rows": rows, "dmod_blk": dmod_blk, "ev_w_in_t": d_ev_w_in_t, "ev_w_out": d_ev_w_out, "ev_sg_w": d_sgw}
    if ride is None:
        grads.update({"od_w_in": d_od_w_in, "od_w_out": d_od_w_out, "od_w_a": d_od_w_a, "od_w_x": d_od_w_x})
    else:
        grads["od_slots"] = od_slots
    return grad_x, grads


ROW_DMOD, ROW_LN, ROW_SG_LN, ROW_SG_B, ROW_CONV_W, ROW_CONV_B, ROW_B_A, ROW_B_X, ROW_LAM, ROW_SINK, ROW_LOSS = (
    0, 6, 10, 11, 12, 16, 17, 19, 21, 23, 24)
SMALL_ROWS = 64


def _pack_small(vec_x, vec_e, vec_p, vec_o, vec_f, vec_r, vec_s, d_sink, d_sgb, loss):
    def body(x_ref, e_ref, p_ref, o_ref, f_ref, r_ref, s_ref, sink_ref, sgb_ref, loss_ref, rows_ref, dmod_ref):
        rows_ref[...] = jnp.zeros_like(rows_ref)
        dmod_ref[...] = jnp.zeros_like(dmod_ref)
        put = [(ROW_DMOD, x_ref, 0), (ROW_DMOD + 1, x_ref, 1), (ROW_DMOD + 2, e_ref, 2), (ROW_DMOD + 3, p_ref, 5),
               (ROW_DMOD + 4, p_ref, 6), (ROW_DMOD + 5, o_ref, 2), (ROW_LN, e_ref, 0), (ROW_LN + 1, e_ref, 1),
               (ROW_LN + 2, o_ref, 0), (ROW_LN + 3, o_ref, 1), (ROW_CONV_B, p_ref, 4), (ROW_B_A, f_ref, 0),
               (ROW_B_A + 1, r_ref, 0), (ROW_B_X, f_ref, 1), (ROW_B_X + 1, r_ref, 1), (ROW_LAM, f_ref, 2), (ROW_LAM + 1, r_ref, 2)]
        put += [(ROW_CONV_W + k, p_ref, k) for k in range(4)]
        for dst, ref, src in put:
            rows_ref[dst:dst + 1, :] = ref[src:src + 1, :]
            if dst < 6:
                dmod_ref[dst:dst + 1, :] = ref[src:src + 1, :]
        rows_ref[ROW_SG_LN:ROW_SG_LN + 1, 0:SG_WIDTH] = s_ref[0:1, :]
        rows_ref[ROW_SG_LN:ROW_SG_LN + 1, SG_WIDTH:2 * SG_WIDTH] = s_ref[1:2, :]
        lane = _lane_iota((1, LANES))
        sink = jnp.zeros((1, LANES), F32)
        for h in range(N_Q_HEADS):
            rows_ref[ROW_SG_B:ROW_SG_B + 1, h * LANES:(h + 1) * LANES] = sgb_ref[h:h + 1, :]
            sink = jnp.where(lane == h, sink_ref[h:h + 1, :], sink)
        rows_ref[ROW_SINK:ROW_SINK + 1, 0:LANES] = sink
        rows_ref[ROW_LOSS:ROW_LOSS + 1, 0:LANES] = jnp.where(lane == 0, loss_ref[0:1, :], 0.0)

    return pl.pallas_call(body, name="pack_small", out_shape=[_sds((SMALL_ROWS, D_MODEL)), _sds((8, D_MODEL))])(
        vec_x, vec_e, vec_p, vec_o, vec_f, vec_r, vec_s, d_sink, d_sgb, loss)


def _allgather8(block, name):
    m_per, n = block.shape

    def body(x_ref, out_ref, send_sems, recv_sems, local_sem):
        x, y, c = _place()
        me, sibling = (x, y, c), (x, y, 1 - c)
        chips = [(1 - x, y), (x, 1 - y), (1 - x, 1 - y)]

        def rows(px, py, pc):
            return out_ref.at[pl.ds((4 * px + 2 * py + pc) * m_per, m_per), :]

        def copy(k, blk, to, src=None):
            return pltpu.make_async_remote_copy(src_ref=rows(*blk) if src is None else src, dst_ref=rows(*blk),
                                                send_sem=send_sems.at[k], recv_sem=recv_sems.at[k], device_id=to,
                                                device_id_type=MESH)

        mine = pltpu.make_async_copy(x_ref, rows(*me), local_sem)
        mine.start()
        first = [copy(0, me, sibling, src=x_ref)] + [copy(1 + j, me, (*chip, c), src=x_ref) for j, chip in enumerate(chips)]
        for cp in first:
            cp.start()
        passed = [copy(4 + j, (*chip, c), sibling) for j, chip in enumerate(chips)]
        for j, chip in enumerate(chips):
            copy(1 + j, (*chip, c), me).wait_recv()
            passed[j].start()
        copy(0, sibling, me).wait_recv()
        for j, chip in enumerate(chips):
            copy(4 + j, (*chip, 1 - c), me).wait_recv()
        for cp in first + passed:
            cp.wait_send()
        mine.wait()

    return pl.pallas_call(
        body, name=name, out_shape=_sds((8 * m_per, n), block.dtype),
        in_specs=[pl.BlockSpec(memory_space=pltpu.VMEM)], out_specs=pl.BlockSpec(memory_space=pltpu.VMEM),
        scratch_shapes=[pltpu.SemaphoreType.DMA((7,)), pltpu.SemaphoreType.DMA((7,)), pltpu.SemaphoreType.DMA],
        compiler_params=pltpu.CompilerParams(vmem_limit_bytes=VMEM_LIMIT),
    )(block)


class _Copies:
    def __init__(self, send_sems, recv_sems, local_sems, stages):
        self.send_sems, self.recv_sems, self.local_sems, self.stages = send_sems, recv_sems, local_sems, stages
        self.sent, self.staged, self.locals = [], [], []

    def remote(self, k, src, dst, to):
        return pltpu.make_async_remote_copy(src_ref=src, dst_ref=dst, send_sem=self.send_sems.at[k], recv_sem=self.recv_sems.at[k],
                                            device_id=to, device_id_type=MESH)

    def send(self, k, src, dst, to):
        cp = self.remote(k, src, dst, to)
        cp.start()
        self.sent.append(cp)

    def arrived(self, k, dst, frm):
        self.remote(k, dst, dst, frm).wait_recv()

    def local(self, src, dst):
        k = len(self.staged)
        cp = pltpu.make_async_copy(src, self.stages[k], self.local_sems.at[2 * k])
        cp.start()
        self.staged.append((cp, dst))

    def flush(self):
        for k in range(len(self.locals), len(self.staged)):
            cp, dst = self.staged[k]
            cp.wait()
            out = pltpu.make_async_copy(self.stages[k], dst, self.local_sems.at[2 * k + 1])
            out.start()
            self.locals.append(out)

    def drain(self):
        self.flush()
        for cp in self.sent:
            cp.wait_send()
        for cp in self.locals:
            cp.wait()


def _comm_call(body, name, ins, out_shapes, n_remote, stages):
    n_in, n_out = len(ins), len(out_shapes)

    def kern(*refs):
        in_refs, out_refs = refs[:n_in], refs[n_in:n_in + n_out]
        send_sems, recv_sems, local_sems = refs[n_in + n_out:n_in + n_out + 3]
        body(_Copies(send_sems, recv_sems, local_sems, refs[n_in + n_out + 3:]), in_refs, out_refs)

    hbm = pl.BlockSpec(memory_space=pl.ANY)
    return pl.pallas_call(
        kern, name=name, out_shape=out_shapes, in_specs=[hbm] * n_in, out_specs=[hbm] * n_out,
        scratch_shapes=[pltpu.SemaphoreType.DMA((n_remote,)), pltpu.SemaphoreType.DMA((n_remote,)),
                        pltpu.SemaphoreType.DMA((2 * len(stages),))] + [pltpu.VMEM(s, d) for s, d in stages],
        compiler_params=pltpu.CompilerParams(vmem_limit_bytes=VMEM_LIMIT),
    )(*ins)


def _gather_to_all(cps, pairs, me, sibling, other_chips, c, base):
    idx = lambda p: 4 * p[0] + 2 * p[1] + p[2]
    for i, (src, dst) in enumerate(pairs):
        cps.local(src, dst.at[idx(me)])
        cps.send(base + 7 * i, src, dst.at[idx(me)], sibling)
        for j, chip in enumerate(other_chips):
            cps.send(base + 7 * i + 1 + j, src, dst.at[idx(me)], (*chip, c))
    cps.flush()
    for j, chip in enumerate(other_chips):
        for i, (_, dst) in enumerate(pairs):
            got = dst.at[idx((*chip, c))]
            cps.arrived(base + 7 * i + 1 + j, got, (*chip, c))
            cps.send(base + 7 * i + 4 + j, got, got, sibling)
    for i, (_, dst) in enumerate(pairs):
        cps.arrived(base + 7 * i, dst.at[idx(sibling)], sibling)
        for j, chip in enumerate(other_chips):
            cps.arrived(base + 7 * i + 4 + j, dst.at[idx((*chip, 1 - c))], sibling)


def _gather_weights(shards, small):
    n = len(shards)

    def body(cps, ins, outs):
        x, y, c = _place()
        me, sibling, mine = (x, y, c), (x, y, 1 - c), 2 * x + y
        chips = [(1 - x, y), (x, 1 - y), (1 - x, 1 - y)]
        for i in range(n):
            cps.local(ins[i], outs[i].at[mine])
        for j, (px, py) in enumerate(chips):
            for i in range(n):
                hr = shards[i].shape[0] // 2
                rows = pl.ds(c * hr, hr)
                cps.send(6 * i + j, ins[i].at[rows], outs[i].at[mine, rows], (px, py, c))
        _gather_to_all(cps, [(ins[n], outs[n])], me, sibling, chips, c, 6 * n)
        for j, (px, py) in enumerate(chips):
            for i in range(n):
                hr = shards[i].shape[0] // 2
                got = outs[i].at[2 * px + py, pl.ds(c * hr, hr)]
                cps.arrived(6 * i + j, got, (px, py, c))
                cps.send(6 * i + 3 + j, got, got, sibling)
        for j, (px, py) in enumerate(chips):
            for i in range(n):
                hr = shards[i].shape[0] // 2
                cps.arrived(6 * i + 3 + j, outs[i].at[2 * px + py, pl.ds((1 - c) * hr, hr)], sibling)
        cps.drain()

    return _comm_call(body, "gather_weights", list(shards) + [small],
                      [_sds((4,) + s.shape, s.dtype) for s in shards] + [_sds((8,) + small.shape, small.dtype)], 6 * n + 7,
                      [(a.shape, a.dtype) for a in list(shards) + [small]])


def _reduce_sibling(parts, dmod_rows):
    n = len(parts)

    def body(cps, ins, outs):
        x, y, c = _place()
        me, sibling = (x, y, c), (x, y, 1 - c)
        chips = [(1 - x, y), (x, 1 - y), (1 - x, 1 - y)]
        for i in range(n):
            cps.send(i, ins[i].at[:, 1 - c], outs[i], sibling)
        _gather_to_all(cps, [(ins[n], outs[n])], me, sibling, chips, c, n)
        for i in range(n):
            cps.arrived(i, outs[i], sibling)
        cps.drain()

    return _comm_call(body, "reduce_sibling", list(parts) + [dmod_rows],
                      [_sds((4,) + p.shape[2:], p.dtype) for p in parts] + [_sds((8,) + dmod_rows.shape, dmod_rows.dtype)], n + 7,
                      [(dmod_rows.shape, dmod_rows.dtype)])


def _reduce_chips(parts):
    n = len(parts)

    def body(cps, ins, outs):
        x, y, c = _place()
        mine = 2 * x + y
        chips = _other_chips(x, y)
        for i in range(n):
            cps.local(ins[i].at[mine], outs[i].at[mine])
        for j, (px, py) in enumerate(chips):
            for i in range(n):
                cps.send(3 * i + j, ins[i].at[2 * px + py], outs[i].at[mine], (px, py, c))
        cps.flush()
        for j, (px, py) in enumerate(chips):
            for i in range(n):
                cps.arrived(3 * i + j, outs[i].at[2 * px + py], (px, py, c))
        cps.drain()

    return _comm_call(body, "reduce_chips", list(parts), [_sds(p.shape, p.dtype) for p in parts], 3 * n,
                      [(p.shape[1:], p.dtype) for p in parts])


def _gather_reduced(shard_parts, repl_parts):
    ns, nr = len(shard_parts), len(repl_parts)

    def body(cps, ins, outs):
        x, y, c = _place()
        me, sibling = (x, y, c), (x, y, 1 - c)
        chips = [(1 - x, y), (x, 1 - y), (1 - x, 1 - y)]
        for i in range(ns):
            cps.local(ins[i], outs[i].at[c])
            cps.send(i, ins[i], outs[i].at[c], sibling)
        _gather_to_all(cps, [(ins[ns + i], outs[ns + i]) for i in range(nr)], me, sibling, chips, c, ns)
        for i in range(ns):
            cps.arrived(i, outs[i].at[1 - c], sibling)
        cps.drain()

    return _comm_call(body, "gather_reduced", list(shard_parts) + list(repl_parts),
                      [_sds((2,) + p.shape, p.dtype) for p in shard_parts] + [_sds((8,) + p.shape, p.dtype) for p in repl_parts],
                      ns + 7 * nr, [(p.shape, p.dtype) for p in list(shard_parts) + list(repl_parts)])


def _sum_sibling(core, parts, got, wire, name):
    n = len(parts)

    def body(core_ref, *refs):
        for i in range(n):
            refs[2 * n + i][0] = (refs[i][0] + refs[n + i][0]).astype(wire[i])

    keep_spec = lambda p: pl.BlockSpec((1, None) + p.shape[2:], lambda s, core_ref: (s, core_ref[0], 0, 0))
    slot_spec = lambda p: pl.BlockSpec((1,) + p.shape[2:], lambda s, core_ref: (s, 0, 0))
    return pl.pallas_call(
        body, name=name,
        grid_spec=pltpu.PrefetchScalarGridSpec(
            num_scalar_prefetch=1, grid=(4,), in_specs=[keep_spec(p) for p in parts] + [slot_spec(p) for p in parts],
            out_specs=[slot_spec(p) for p in parts]),
        out_shape=[_sds((4,) + p.shape[2:], wire[i]) for i, p in enumerate(parts)],
        compiler_params=_params("parallel"),
    )(core, *parts, *got)


def _sum_slots(slots, name):
    n = len(slots)

    def spec_pair(p):
        k, rows, cols = p.shape
        sub = 16 if p.dtype == BF16 else 8
        if (rows // 2) % sub == 0:
            return pl.BlockSpec((k, rows // 2, cols), lambda i: (0, i, 0)), pl.BlockSpec((rows // 2, cols), lambda i: (i, 0))
        return pl.BlockSpec((k, rows, cols), lambda i: (0, 0, 0)), pl.BlockSpec((rows, cols), lambda i: (0, 0))

    pairs = [spec_pair(p) for p in slots]

    def body(*refs):
        for i in range(n):
            acc = refs[i][0].astype(F32)
            for j in range(1, slots[i].shape[0]):
                acc = acc + refs[i][j].astype(F32)
            refs[n + i][...] = acc

    return pl.pallas_call(
        body, name=name, grid=(2,), in_specs=[a for a, _ in pairs], out_specs=[b for _, b in pairs],
        out_shape=[_sds(p.shape[1:]) for p in slots], compiler_params=_params("arbitrary"),
    )(*slots)


def _modulation(c_all, ada_w, ada_b):
    cols = ada_w.shape[2]

    def body(c_ref, w_ref, b_ref, o_ref):
        cc = c_ref[...]
        o_ref[0] = _mm(cc * _sigmoid(cc), w_ref[0]) + b_ref[0]

    return pl.pallas_call(
        body, name="modulation", grid=(2,),
        in_specs=[_full((8, D_MODEL)), pl.BlockSpec((1, D_MODEL, cols), lambda l: (l, 0, 0)), pl.BlockSpec((1, 1, cols), lambda l: (l, 0, 0))],
        out_specs=pl.BlockSpec((1, 8, cols), lambda l: (l, 0, 0)), out_shape=_sds((2, 8, cols)),
        compiler_params=_params("parallel"),
    )(c_all, ada_w, ada_b)


def _adamw_math(w, g, m, v):
    m = ADAM_B1 * m + (1.0 - ADAM_B1) * g
    v = ADAM_B2 * v + (1.0 - ADAM_B2) * (g * g)
    m_hat = m / (1.0 - ADAM_B1 ** ADAM_STEP)
    v_hat = v / (1.0 - ADAM_B2 ** ADAM_STEP)
    delta = -ADAM_LR * (m_hat / (jnp.sqrt(v_hat) + ADAM_EPS) + ADAM_WD * w)
    return delta, m, v


def _ada_update(c_all, dmod, w, m, v, rider=None):
    cols = w.shape[2]
    tr = 256
    per = D_MODEL // tr
    spec3 = pl.BlockSpec((1, tr, cols), lambda i: (i // per, i % per, 0))

    def body(c_ref, d_ref, w_ref, m_ref, v_ref, g_ref, dl_ref, nm_ref, nv_ref):
        cc = c_ref[...]
        g = _mm_tn(cc * _sigmoid(cc), d_ref[0])
        g_ref[0] = g
        dl_ref[0], nm_ref[0], nv_ref[0] = _adamw_math(w_ref[0], g, m_ref[0], v_ref[0])

    return _call(
        body, "ada_update", (2 * per,),
        [pl.BlockSpec((8, tr), lambda i: (0, i % per)), pl.BlockSpec((1, 8, cols), lambda i: (i // per, 0, 0)), spec3, spec3, spec3],
        [spec3] * 4, [_sds(w.shape)] * 4, (c_all, dmod, w, m, v), "parallel", rider=rider)


def _adamw_matrices(params):
    n = len(params)
    steps = 8

    def body(*refs):
        ins, outs = refs[:4 * n], refs[4 * n:]
        for j in range(n):
            w_ref, g_ref, m_ref, v_ref = ins[4 * j:4 * j + 4]
            g = g_ref[...]
            outs[4 * j][...] = g
            outs[4 * j + 1][...], outs[4 * j + 2][...], outs[4 * j + 3][...] = _adamw_math(w_ref[...], g, m_ref[...], v_ref[...])

    spec = lambda p: _rows(p[0].shape[0] // steps, p[0].shape[1])
    res = pl.pallas_call(
        body, name="adamw_matrices", grid=(steps,), in_specs=[spec(p) for p in params for _ in range(4)],
        out_specs=[spec(p) for p in params for _ in range(4)], out_shape=[_sds(p[0].shape) for p in params for _ in range(4)],
        compiler_params=_params("parallel"),
    )(*[a for p in params for a in p])
    return [tuple(res[4 * j:4 * j + 4]) for j in range(n)]


def _adamw_small(params):
    n = len(params)

    def body(*refs):
        ins, outs = refs[:4 * n], refs[4 * n:]
        for j in range(n):
            w_ref, g_ref, m_ref, v_ref = ins[4 * j:4 * j + 4]
            outs[3 * j][...], outs[3 * j + 1][...], outs[3 * j + 2][...] = _adamw_math(w_ref[...], g_ref[...], m_ref[...], v_ref[...])

    flat = [a for p in params for a in p]
    res = pl.pallas_call(body, name="adamw_small", out_shape=[_sds(p[0].shape) for p in params for _ in range(3)])(*flat)
    return [tuple(res[3 * j:3 * j + 3]) for j in range(n)]


def _cols(a, start, size):
    return lax.dynamic_slice_in_dim(a, start, size, axis=a.ndim - 1)


def kernel(x, c, positions, ada_w, ada_b, ln_g, ln_b, ev_w_in, ev_w_out, ev_sink, ev_sg_ln_g, ev_sg_ln_b, ev_sg_w, ev_sg_b, od_w_in, od_conv_w, od_conv_b, od_w_a, od_b_a, od_w_x, od_b_x, od_lam, od_w_out, loss_target, m_ada_w, m_ada_b, m_ln_g, m_ln_b, m_ev_w_in, m_ev_w_out, m_ev_sink, m_ev_sg_ln_g, m_ev_sg_ln_b, m_ev_sg_w, m_ev_sg_b, m_od_w_in, m_od_conv_w, m_od_conv_b, m_od_w_a, m_od_b_a, m_od_w_x, m_od_b_x, m_od_lam, m_od_w_out, v_ada_w, v_ada_b, v_ln_g, v_ln_b, v_ev_w_in, v_ev_w_out, v_ev_sink, v_ev_sg_ln_g, v_ev_sg_ln_b, v_ev_sg_w, v_ev_sg_b, v_od_w_in, v_od_conv_w, v_od_conv_b, v_od_w_a, v_od_b_a, v_od_w_x, v_od_b_x, v_od_lam, v_od_w_out):
    seq = x.shape[1]
    px, py, pc = _place()
    chip = 2 * px + py
    dev = 2 * chip + pc

    small = jnp.concatenate([od_conv_w[0].reshape(-1), od_conv_b[0], od_b_a[0].reshape(-1), jnp.zeros((256,), F32),
                             od_b_x[0].reshape(-1), od_lam[0].reshape(-1)]).reshape(3, D_MODEL)
    blk = jnp.concatenate([c, small, jnp.zeros((4, D_MODEL), F32)], axis=0)
    tr = lambda a: jnp.swapaxes(a, -1, -2)
    wire_w = lambda a: a.astype(MXU_DTYPE)
    ev_w_in4, g_small = _gather_weights([wire_w(tr(ev_w_in[0]))], blk)
    core = pc.astype(jnp.int32).reshape(1)
    ride = {"ev_w_out": wire_w(ev_w_out[0]), "od_w_in": wire_w(od_w_in[0]), "od_w_out": wire_w(od_w_out[0]),
            "core": core, "wire": MXU_DTYPE}
    c_all = g_small[:, 0, :]
    per_chip = g_small[0::2]
    conv_w = per_chip[:, 1].reshape(4, 4, 256).transpose(1, 0, 2).reshape(4, D_MODEL)
    conv_b = per_chip[:, 2, 0:256].reshape(D_MODEL)
    b_a = per_chip[:, 2, 256:768].reshape(4, 2, 256).transpose(1, 0, 2).reshape(2, D_MODEL)
    b_x = per_chip[:, 3, 0:512].reshape(4, 2, 256).transpose(1, 0, 2).reshape(2, D_MODEL)
    lam = per_chip[:, 3, 512:1024].reshape(4, 2, 256).transpose(1, 0, 2).reshape(2, D_MODEL)

    w_full = {
        "ev_w_in_t": ev_w_in4.reshape(EVEN_IN, D_MODEL),
        "ev_sink": ev_sink[0], "ev_sg_ln_g": ev_sg_ln_g[0], "ev_sg_ln_b": ev_sg_ln_b[0], "ev_sg_w": ev_sg_w[0],
        "ev_sg_b": ev_sg_b[0], "od_conv_w": conv_w, "od_conv_b": conv_b, "od_w_a": od_w_a[0], "od_b_a": b_a,
        "od_w_x": od_w_x[0], "od_b_x": b_x, "od_lam": lam, "ln_g": ln_g, "ln_b": ln_b,
    }

    ada_cols = ada_w.shape[2]
    mod_sh = _modulation(c_all, ada_w, _cols(ada_b, chip * ada_cols, ada_cols).reshape(2, 1, ada_cols))
    mod_all = _allgather8(mod_sh.reshape(16, ada_cols), "gather_mod").reshape(4, 2, 2, 8, ada_cols)[:, 0]
    mod_mine = lax.dynamic_index_in_dim(mod_all, dev, axis=2, keepdims=False)
    mod = mod_mine.transpose(1, 0, 2).reshape(2, 3, D_MODEL)

    posf = positions.astype(F32).reshape(seq, 1)
    grad_x, g = _local_step(x[0], posf, loss_target[0], mod, w_full, seq, ride)

    parts = [g["ev_w_in_t"].reshape(4, 2, 352, D_MODEL), g["ev_w_out"].reshape(4, 2, 128, D_MODEL),
             g["ev_sg_w"].reshape(4, 2, BLK, BLK), g["rows"].reshape(4, 2, SMALL_ROWS // 8, D_MODEL)]
    wire = [MXU_DTYPE] * 3 + [F32]
    *got, dmod_gathered = _reduce_sibling(parts, g["dmod_blk"])
    ev_slots = list(_reduce_chips(_sum_sibling(core, parts, got, wire, "sum_sibling")))
    od_slots = list(g["od_slots"])
    mine = _sum_slots(ev_slots[0:2] + od_slots[0:2] + ev_slots[2:3] + od_slots[2:4] + ev_slots[3:4], "sum_chips")
    reduced = _gather_reduced(mine[:4], mine[4:])
    g_ev_w_in_t = reduced[0].reshape(704, D_MODEL)
    g_ev_w_out = reduced[1].reshape(256, D_MODEL)
    g_od_w_in = reduced[2].reshape(D_MODEL, 512)
    g_od_w_out = reduced[3].reshape(256, D_MODEL)
    g_sg_w = reduced[4].reshape(8 * BLK, BLK)
    g_w_a = reduced[5].reshape(16 * BLK, BLK)
    g_w_x = reduced[6].reshape(16 * BLK, BLK)
    gs = reduced[7].reshape(SMALL_ROWS, D_MODEL)
    loss = gs[ROW_LOSS, 0]
    dmod_all = dmod_gathered[:, 0:6].reshape(8, 2, 3 * D_MODEL)
    dmod_sh = _cols(dmod_all, chip * ada_cols, ada_cols).transpose(1, 0, 2)
    (g_ada_w, d_ada_w, nm_ada_w, nv_ada_w), _ = _ada_update(c_all, dmod_sh, ada_w, m_ada_w, v_ada_w)

    mats = (("ev_w_out", ev_w_out, g_ev_w_out, m_ev_w_out, v_ev_w_out), ("od_w_in", od_w_in, g_od_w_in, m_od_w_in, v_od_w_in),
            ("od_w_out", od_w_out, g_od_w_out, m_od_w_out, v_od_w_out), ("ev_sg_w", ev_sg_w, g_sg_w, m_ev_sg_w, v_ev_sg_w),
            ("od_w_a", od_w_a, g_w_a, m_od_w_a, v_od_w_a), ("od_w_x", od_w_x, g_w_x, m_od_w_x, v_od_w_x))
    upd = _adamw_matrices([(tr(ev_w_in[0]), g_ev_w_in_t, tr(m_ev_w_in[0]), tr(v_ev_w_in[0]))]
                          + [(w_.reshape(g_.shape), g_, m_.reshape(g_.shape), v_.reshape(g_.shape)) for _, w_, g_, m_, v_ in mats])
    big = {"ev_w_in": tuple(tr(a).reshape(ev_w_in.shape) for a in upd[0])}
    for (name, w_, _, _, _), u in zip(mats, upd[1:]):
        big[name] = tuple(a.reshape(w_.shape) for a in u)
    big["ada_w"] = (g_ada_w, d_ada_w, nm_ada_w, nv_ada_w)

    sh = lambda a: _cols(a, chip * 256, 256)
    small_g = {
        "ada_b": gs[ROW_DMOD:ROW_DMOD + 6].reshape(2, 3 * D_MODEL),
        "ln_g": jnp.stack([gs[ROW_LN], gs[ROW_LN + 2]]), "ln_b": jnp.stack([gs[ROW_LN + 1], gs[ROW_LN + 3]]),
        "ev_sink": gs[ROW_SINK:ROW_SINK + 1, 0:N_Q_HEADS], "ev_sg_ln_g": gs[ROW_SG_LN:ROW_SG_LN + 1, 0:SG_WIDTH],
        "ev_sg_ln_b": gs[ROW_SG_LN:ROW_SG_LN + 1, SG_WIDTH:2 * SG_WIDTH], "ev_sg_b": gs[ROW_SG_B].reshape(N_SG_GROUPS, BLK),
        "od_conv_w": sh(gs[ROW_CONV_W:ROW_CONV_W + 4]), "od_conv_b": sh(gs[ROW_CONV_B:ROW_CONV_B + 1]),
        "od_b_a": sh(gs[ROW_B_A:ROW_B_A + 2]), "od_b_x": sh(gs[ROW_B_X:ROW_B_X + 2]), "od_lam": sh(gs[ROW_LAM:ROW_LAM + 2]),
    }
    small_in = {"ada_b": (ada_b, m_ada_b, v_ada_b), "ln_g": (ln_g, m_ln_g, v_ln_g), "ln_b": (ln_b, m_ln_b, v_ln_b),
                "ev_sink": (ev_sink, m_ev_sink, v_ev_sink), "ev_sg_ln_g": (ev_sg_ln_g, m_ev_sg_ln_g, v_ev_sg_ln_g),
                "ev_sg_ln_b": (ev_sg_ln_b, m_ev_sg_ln_b, v_ev_sg_ln_b), "ev_sg_b": (ev_sg_b, m_ev_sg_b, v_ev_sg_b),
                "od_conv_w": (od_conv_w, m_od_conv_w, v_od_conv_w), "od_conv_b": (od_conv_b, m_od_conv_b, v_od_conv_b),
                "od_b_a": (od_b_a, m_od_b_a, v_od_b_a), "od_b_x": (od_b_x, m_od_b_x, v_od_b_x),
                "od_lam": (od_lam, m_od_lam, v_od_lam)}
    names_small = list(small_g)
    upd = _adamw_small([(small_in[n][0].reshape(small_g[n].shape), small_g[n], small_in[n][1].reshape(small_g[n].shape),
                         small_in[n][2].reshape(small_g[n].shape)) for n in names_small])
    res = dict(big)
    for n, (d_, nm_, nv_) in zip(names_small, upd):
        shape = small_in[n][0].shape
        res[n] = tuple(a.reshape(shape) for a in (small_g[n], d_, nm_, nv_))

    order = ["ada_w", "ada_b", "ln_g", "ln_b", "ev_w_in", "ev_w_out", "ev_sink", "ev_sg_ln_g", "ev_sg_ln_b", "ev_sg_w", "ev_sg_b",
             "od_w_in", "od_conv_w", "od_conv_b", "od_w_a", "od_b_a", "od_w_x", "od_b_x", "od_lam", "od_w_out"]
    return (loss, grad_x.reshape(x.shape), *[res[n][0] for n in order], *[res[n][1] for n in order],
            *[res[n][2] for n in order], *[res[n][3] for n in order])
```

```python
import jax
import jax.numpy as jnp
import numpy as np
from jax import lax
from jax.experimental import pallas as pl
from jax.experimental.pallas import tpu as pltpu

F32 = jnp.float32
BF16 = jnp.bfloat16
MXU_DTYPE = BF16
ACT_DTYPE = MXU_DTYPE

D_MODEL = 1024
HEAD_DIM = 64
N_Q_HEADS = 8
Q_PER_KV = 4
ATTN_WIDTH = 512
BLK = 128
ROPE_DIM = 16
ROPE_THETA = 500000.0
N_SG_GROUPS = 8
SG_WIDTH = 512
EVEN_IN = 2816
ODD_IN = 2048
RNN_HEADS = 8
RG_LRU_C = 8.0
ALPHA = (2 * 2) ** 0.25
LN_EPS = 1e-5
NEG_INF = -1e30
ADAM_LR, ADAM_B1, ADAM_B2, ADAM_EPS, ADAM_WD, ADAM_STEP = 0.001, 0.9, 0.999, 1e-08, 0.01, 10

LANES = 128
VMEM_LIMIT = 56 * 1024 * 1024
MESH = pl.DeviceIdType.MESH


def _mm(a, b):
    return jnp.dot(a.astype(MXU_DTYPE), b.astype(MXU_DTYPE), preferred_element_type=F32)


def _mm_nt(a, b):
    return lax.dot_general(a.astype(MXU_DTYPE), b.astype(MXU_DTYPE), (((1,), (1,)), ((), ())), preferred_element_type=F32)


def _mm_tn(a, b):
    return lax.dot_general(a.astype(MXU_DTYPE), b.astype(MXU_DTYPE), (((0,), (0,)), ((), ())), preferred_element_type=F32)


def _sigmoid(x):
    return 1.0 / (1.0 + jnp.exp(-x))


def _ln_stats(z):
    mu = jnp.mean(z, axis=-1, keepdims=True)
    d = z - mu
    var = jnp.mean(d * d, axis=-1, keepdims=True)
    rstd = lax.rsqrt(var + LN_EPS)
    return d * rstd, rstd


def _ln_bwd(dout, zhat, rstd, g):
    dzh = dout * g
    m1 = jnp.mean(dzh, axis=-1, keepdims=True)
    m2 = jnp.mean(dzh * zhat, axis=-1, keepdims=True)
    return rstd * (dzh - m1 - zhat * m2)


def _group_sum(x, e2):
    hi = x.astype(MXU_DTYPE)
    lo = (x - hi.astype(F32)).astype(MXU_DTYPE)
    return jnp.dot(hi, e2, preferred_element_type=F32) + jnp.dot(lo, e2, preferred_element_type=F32)


def _lane_iota(shape):
    return lax.broadcasted_iota(jnp.int32, shape, 1)


def _to_kv_lanes(t, h):
    src_lo = (h % 2 == 0)
    dst_lo = (h // Q_PER_KV == 0)
    if src_lo != dst_lo:
        t = pltpu.roll(t, HEAD_DIM, 1)
    lane = _lane_iota(t.shape)
    keep = (lane < HEAD_DIM) if dst_lo else (lane >= HEAD_DIM)
    return jnp.where(keep, t, 0.0)


def _from_kv_lanes(t, h):
    src_lo = (h // Q_PER_KV == 0)
    dst_lo = (h % 2 == 0)
    lane = _lane_iota(t.shape)
    keep = (lane < HEAD_DIM) if src_lo else (lane >= HEAD_DIM)
    t = jnp.where(keep, t, 0.0)
    if src_lo != dst_lo:
        t = pltpu.roll(t, HEAD_DIM, 1)
    return t


def _rope(t, cos_t, sin_p, sin_m):
    half = ROPE_DIM // 2
    return t * cos_t + pltpu.roll(t, half, 1) * sin_p + pltpu.roll(t, LANES - half, 1) * sin_m


def _rope_t(d, cos_t, sin_p, sin_m):
    half = ROPE_DIM // 2
    return d * cos_t + pltpu.roll(d * sin_p, LANES - half, 1) + pltpu.roll(d * sin_m, half, 1)


def _band(ref, n, nb):
    prev = jnp.maximum(n - 1, 0)
    nxt = jnp.minimum(n + 1, nb - 1)
    rows = [ref[pl.ds(pl.multiple_of(j * BLK, BLK), BLK), :] for j in (prev, n, nxt)]
    return jnp.concatenate(rows, axis=0)


def _band_bias(n, seq):
    qi = lax.broadcasted_iota(jnp.int32, (BLK, 3 * BLK), 0)
    kj = lax.broadcasted_iota(jnp.int32, (BLK, 3 * BLK), 1)
    k_abs = n * BLK - BLK + kj
    valid = (jnp.abs(kj - BLK - qi) <= BLK) & (k_abs >= 0) & (k_abs < seq)
    bias = jnp.where(valid, 0.0, NEG_INF)
    return jnp.concatenate([bias] * Q_PER_KV, axis=0)


def _stack_heads(tile_of, kv):
    return jnp.concatenate([_to_kv_lanes(tile_of(h // 2), h) for h in range(Q_PER_KV * kv, Q_PER_KV * (kv + 1))], axis=0)


def _per_head_column(vals):
    row = lax.broadcasted_iota(jnp.int32, (Q_PER_KV * BLK, 1), 0)
    return jnp.where(row < BLK, vals[0], jnp.where(row < 2 * BLK, vals[1], jnp.where(row < 3 * BLK, vals[2], vals[3])))


def _softplus_neg(lam):
    e = jnp.exp(-jnp.abs(lam))
    u = 1.0 + e
    log1p_e = jnp.where(u == 1.0, e, jnp.log(u) * (e / (u - 1.0)))
    sp = jnp.maximum(-lam, 0.0) + log1p_e
    dsp = -1.0 / (1.0 + jnp.exp(lam))
    return sp, dsp


def _full(shape):
    return pl.BlockSpec(shape, lambda *_: (0,) * len(shape))


def _const(shape):
    return pl.BlockSpec(shape, lambda *_: (0,) * len(shape), pipeline_mode=pl.Buffered(1))


def _rows(tm, n):
    return pl.BlockSpec((tm, n), lambda i: (i, 0))


def _params(*sem):
    return pltpu.CompilerParams(dimension_semantics=sem, vmem_limit_bytes=VMEM_LIMIT)


def _sds(shape, dtype=F32):
    return jax.ShapeDtypeStruct(shape, dtype)


def _place():
    return lax.axis_index("x"), lax.axis_index("y"), lax.axis_index("c")


class _Rider:
    def __init__(self, ins, out_shapes, n_remote, n_local, plan):
        self.ins, self.out_shapes, self.n_remote, self.n_local, self.plan = list(ins), list(out_shapes), n_remote, n_local, plan

    def scratch(self):
        return [pltpu.SemaphoreType.DMA((self.n_remote,)), pltpu.SemaphoreType.DMA((self.n_remote,)),
                pltpu.SemaphoreType.DMA((max(self.n_local, 1),))]

    def run(self, first, in_refs, out_refs, sems):
        send_sems, recv_sems, local_sems = sems
        sends, recvs, locals_ = self.plan(in_refs, out_refs)
        remote = lambda k, src, dst, to: pltpu.make_async_remote_copy(
            src_ref=src, dst_ref=dst, send_sem=send_sems.at[k], recv_sem=recv_sems.at[k], device_id=to, device_id_type=MESH)
        if first:
            for k, src, dst, to in sends:
                remote(k, src, dst, to).start()
            for j, (src, dst) in enumerate(locals_):
                pltpu.make_async_copy(src, dst, local_sems.at[j]).start()
        else:
            for k, dst, frm in recvs:
                remote(k, dst, dst, frm).wait_recv()
            for k, src, dst, to in sends:
                remote(k, src, dst, to).wait_send()
            for j, (src, dst) in enumerate(locals_):
                pltpu.make_async_copy(src, dst, local_sems.at[j]).wait()


def _other_chips(x, y):
    return [(1 - x, y), (x, 1 - y), (1 - x, 1 - y)]


def _gather_rider(shard):
    hr = shard.shape[0] // 2

    def plan(ins, outs):
        x, y, c = _place()
        mine, src, dst = 2 * x + y, ins[0], outs[0]
        sends, recvs = [], []
        for j, (px, py) in enumerate(_other_chips(x, y)):
            for flip in range(2):
                tc = c if flip == 0 else 1 - c
                sends.append((2 * j + flip, src.at[pl.ds(c * hr, hr)], dst.at[mine, pl.ds(c * hr, hr)], (px, py, tc)))
                recvs.append((2 * j + flip, dst.at[2 * px + py, pl.ds(tc * hr, hr)], (px, py, tc)))
        return sends, recvs, [(src, dst.at[mine])]

    return _Rider([shard], [_sds((4,) + shard.shape, shard.dtype)], 6, 1, plan)


def _sibling_swap_rider(parts):
    n = len(parts)

    def plan(ins, outs):
        x, y, c = _place()
        sibling = (x, y, 1 - c)
        return ([(i, ins[i].at[:, 1 - c], outs[i], sibling) for i in range(n)], [(i, outs[i], sibling) for i in range(n)], [])

    return _Rider(parts, [_sds((4,) + p.shape[2:], p.dtype) for p in parts], n, 0, plan)


def _chip_exchange_rider(parts):
    n = len(parts)

    def plan(ins, outs):
        x, y, c = _place()
        mine = 2 * x + y
        sends, recvs = [], []
        for i in range(n):
            for j, (px, py) in enumerate(_other_chips(x, y)):
                sends.append((3 * i + j, ins[i].at[2 * px + py], outs[i].at[mine], (px, py, c)))
                recvs.append((3 * i + j, outs[i].at[2 * px + py], (px, py, c)))
        return sends, recvs, [(ins[i].at[mine], outs[i].at[mine]) for i in range(n)]

    return _Rider(parts, [_sds(p.shape, p.dtype) for p in parts], 3 * n, n, plan)


def _call(body, name, grid, in_specs, out_specs, out_shape, args, sem, scratch=(), rider=None):
    if rider is None:
        return list(pl.pallas_call(body, name=name, grid=grid, in_specs=in_specs, out_specs=out_specs, out_shape=out_shape,
                                   scratch_shapes=list(scratch), compiler_params=_params(sem))(*args)), []
    n_in, n_out, n_scr = len(in_specs), len(out_specs), len(scratch)
    r_in, r_out = len(rider.ins), len(rider.out_shapes)
    steps = grid[0]

    def riding(*refs):
        ins, r_ins = refs[:n_in], refs[n_in:n_in + r_in]
        outs = refs[n_in + r_in:n_in + r_in + n_out]
        r_outs = refs[n_in + r_in + n_out:n_in + r_in + n_out + r_out]
        scr = refs[n_in + r_in + n_out + r_out:n_in + r_in + n_out + r_out + n_scr]
        sems = refs[n_in + r_in + n_out + r_out + n_scr:]

        @pl.when(pl.program_id(0) == 0)
        def _():
            rider.run(True, r_ins, r_outs, sems)

        body(*ins, *outs, *scr)

        @pl.when(pl.program_id(0) == steps - 1)
        def _():
            rider.run(False, r_ins, r_outs, sems)

    hbm = pl.BlockSpec(memory_space=pl.ANY)
    res = pl.pallas_call(
        riding, name=name, grid=grid, in_specs=list(in_specs) + [hbm] * r_in, out_specs=list(out_specs) + [hbm] * r_out,
        out_shape=list(out_shape) + rider.out_shapes, scratch_shapes=list(scratch) + rider.scratch(),
        compiler_params=_params("arbitrary"),
    )(*args, *rider.ins)
    return list(res[:n_out]), list(res[n_out:])


def _row_tile(seq, want):
    return want if seq % want == 0 else seq


def _rope_tables(posf, seq):
    half = ROPE_DIM // 2
    inv_freq = np.power(np.float32(ROPE_THETA), -np.arange(half, dtype=np.float32) / np.float32(half)).astype(np.float32)
    j = np.arange(LANES) % HEAD_DIM
    invf = jnp.asarray(np.where(j < ROPE_DIM, inv_freq[j % half], 0.0).astype(np.float32).reshape(1, LANES))
    m_p = jnp.asarray(((j >= half) & (j < ROPE_DIM)).astype(np.float32).reshape(1, LANES))
    m_m = jnp.asarray(-(j < half).astype(np.float32).reshape(1, LANES))
    tm = _row_tile(seq, 512)

    def body(pos_ref, invf_ref, mp_ref, mm_ref, cos_ref, sp_ref, sm_ref):
        ang = pos_ref[...] * invf_ref[...]
        s = jnp.sin(ang)
        cos_ref[...] = jnp.cos(ang)
        sp_ref[...] = s * mp_ref[...]
        sm_ref[...] = s * mm_ref[...]

    return pl.pallas_call(
        body, name="rope_tables", grid=(seq // tm,),
        in_specs=[_rows(tm, 1), _full((1, LANES)), _full((1, LANES)), _full((1, LANES))],
        out_specs=[_rows(tm, LANES)] * 3, out_shape=[_sds((seq, LANES))] * 3,
        compiler_params=_params("parallel"),
    )(posf, invf, m_p, m_m)


def _even_proj(x, mod, w_in_t, tabs, seq, rider=None):
    tm = _row_tile(seq, 512)

    def body(x_ref, mod_ref, w_ref, cos_ref, sp_ref, sm_ref, q_ref, k_ref, v_ref, su_ref, sv_ref, g_ref):
        h = x_ref[...] * (1.0 + mod_ref[1:2, :]) + mod_ref[0:1, :]
        p = _mm_nt(h, w_ref[...])
        cos_t, sin_p, sin_m = cos_ref[...], sp_ref[...], sm_ref[...]
        for j in range(ATTN_WIDTH // LANES):
            q_ref[:, j * LANES:(j + 1) * LANES] = _rope(p[:, j * LANES:(j + 1) * LANES], cos_t, sin_p, sin_m).astype(q_ref.dtype)
        k_ref[...] = _rope(p[:, 512:640], cos_t, sin_p, sin_m).astype(k_ref.dtype)
        v_ref[...] = p[:, 640:768].astype(v_ref.dtype)
        su_ref[...] = p[:, 768:1280].astype(su_ref.dtype)
        sv_ref[...] = p[:, 1280:1792].astype(sv_ref.dtype)
        g_ref[...] = p[:, 1792:2816].astype(g_ref.dtype)

    return _call(
        body, "even_proj", (seq // tm,),
        [_rows(tm, D_MODEL), _full((3, D_MODEL)), _const((EVEN_IN, D_MODEL))] + [_rows(tm, LANES)] * 3,
        [_rows(tm, 512), _rows(tm, LANES), _rows(tm, LANES), _rows(tm, 512), _rows(tm, 512), _rows(tm, D_MODEL)],
        [_sds((seq, 512), MXU_DTYPE), _sds((seq, LANES), MXU_DTYPE), _sds((seq, LANES), MXU_DTYPE), _sds((seq, 512), ACT_DTYPE),
         _sds((seq, 512), ACT_DTYPE), _sds((seq, D_MODEL), ACT_DTYPE)],
        (x, mod, w_in_t, *tabs), "parallel", rider=rider)


def _sg_forward(sv, lng, lnb, sgw_ref, sgb, e2):
    vn, vhat, rstd, svo = [], [], [], []
    for j in range(SG_WIDTH // LANES):
        t = sv[:, j * LANES:(j + 1) * LANES]
        mu = _group_sum(t, e2) * (1.0 / HEAD_DIM)
        d = t - mu
        var = _group_sum(d * d, e2) * (1.0 / HEAD_DIM)
        r = lax.rsqrt(var + LN_EPS)
        vh = d * r
        vhat.append(vh)
        rstd.append(r)
        vn.append(vh * lng[:, j * LANES:(j + 1) * LANES] + lnb[:, j * LANES:(j + 1) * LANES])
    lane = _lane_iota((BLK, LANES))
    for j in range(SG_WIDTH // LANES):
        lo = _mm(sgw_ref[2 * j], vn[j])
        hi = _mm(sgw_ref[2 * j + 1], vn[j])
        svo.append(jnp.where(lane < HEAD_DIM, lo, hi) + sgb[:, j * LANES:(j + 1) * LANES])
    return svo, vn, vhat, rstd


def _even_mix(q, k, v, su, sv, sink, sgln_g, sgln_b, sgw, sgb_full, e2, seq, rider=None):
    nb = seq // BLK

    def body(sink_ref, q_ref, k_ref, v_ref, su_ref, sv_ref, lng_ref, lnb_ref, sgw_ref, sgb_ref, e2_ref, ycat_ref, lse_ref,
             svo_ref, vhat_ref, rstd_ref):
        n = pl.program_id(0)
        kband = _band(k_ref, n, nb)
        vband = _band(v_ref, n, nb)
        bias = _band_bias(n, seq)
        lane = _lane_iota((BLK, LANES))
        lse = jnp.zeros((BLK, LANES), F32)
        q_tile = lambda j: q_ref[:, j * LANES:(j + 1) * LANES].astype(F32)
        acc = [jnp.zeros((BLK, LANES), F32) for _ in range(ATTN_WIDTH // LANES)]
        for kv in range(N_Q_HEADS // Q_PER_KV):
            heads = range(Q_PER_KV * kv, Q_PER_KV * (kv + 1))
            sink = _per_head_column([sink_ref[h] for h in heads])
            s = _mm_nt(_stack_heads(q_tile, kv), kband) * (HEAD_DIM ** -0.5) + bias
            m = jnp.maximum(jnp.max(s, axis=1, keepdims=True), sink)
            p = jnp.exp(s - m)
            denom = jnp.sum(p, axis=1, keepdims=True) + jnp.exp(sink - m)
            o4 = _mm(p / denom, vband)
            l4 = m + jnp.log(denom)
            for g, h in enumerate(heads):
                acc[h // 2] = acc[h // 2] + _from_kv_lanes(o4[g * BLK:(g + 1) * BLK], h)
                lse = jnp.where(lane == h, l4[g * BLK:(g + 1) * BLK], lse)
        for j in range(ATTN_WIDTH // LANES):
            ycat_ref[:, j * LANES:(j + 1) * LANES] = acc[j].astype(ycat_ref.dtype)
        lse_ref[...] = lse
        svo, _, vhat, rstd = _sg_forward(sv_ref[...].astype(F32), lng_ref[...], lnb_ref[...], sgw_ref, sgb_ref[...], e2_ref[...])
        for j in range(SG_WIDTH // LANES):
            cs = slice(j * LANES, (j + 1) * LANES)
            ysg = su_ref[:, cs].astype(F32) * svo[j]
            ycat_ref[:, ATTN_WIDTH + j * LANES:ATTN_WIDTH + (j + 1) * LANES] = ysg.astype(ycat_ref.dtype)
            svo_ref[:, cs], vhat_ref[:, cs], rstd_ref[:, cs] = (t.astype(svo_ref.dtype) for t in (svo[j], vhat[j], rstd[j]))

    blk = lambda w: pl.BlockSpec((BLK, w), lambda n: (n, 0))
    return _call(
        body, "even_mix", (nb,),
        [pl.BlockSpec(memory_space=pltpu.SMEM), blk(512), _full((seq, LANES)), _full((seq, LANES)), blk(512), blk(512),
         _full((1, 512)), _full((1, 512)), _full((8, BLK, BLK)), _full((BLK, 512)), _full((LANES, LANES))],
        [blk(D_MODEL), blk(LANES)] + [blk(SG_WIDTH)] * 3,
        [_sds((seq, D_MODEL), ACT_DTYPE), _sds((seq, LANES))] + [_sds((seq, SG_WIDTH), ACT_DTYPE)] * 3,
        (sink, q, k, v, su, sv, sgln_g, sgln_b, sgw, sgb_full, e2), "parallel", rider=rider)


def _even_out(ycat, g, x, mod, mod_next, w_out, w_in4_next, ln_g, ln_b, seq, rider=None):
    tm = _row_tile(seq, 512)
    cs = ODD_IN // 4

    def body(y_ref, g_ref, x_ref, mod_ref, modn_ref, wo_ref, wi_ref, g1_ref, b1_ref, z_ref, x1_ref, xr_ref, gn_ref):
        gg = g_ref[...].astype(F32)
        out = _mm(y_ref[...].astype(F32) * (gg * _sigmoid(gg)), wo_ref[...])
        z = ALPHA * x_ref[...] + mod_ref[2:3, :] * out
        z_ref[...] = z
        zhat, _ = _ln_stats(z)
        x1 = zhat * g1_ref[...] + b1_ref[...]
        x1_ref[...] = x1
        hb = (x1 * (1.0 + modn_ref[1:2, :]) + modn_ref[0:1, :]).astype(MXU_DTYPE)
        for s in range(2):
            xr_ref[:, s * cs:(s + 1) * cs] = jnp.dot(hb, wi_ref[s], preferred_element_type=F32)
            gn_ref[:, s * cs:(s + 1) * cs] = jnp.dot(hb, wi_ref[2 + s], preferred_element_type=F32).astype(gn_ref.dtype)

    return _call(
        body, "even_out", (seq // tm,),
        [_rows(tm, D_MODEL)] * 3 + [_full((3, D_MODEL)), _full((3, D_MODEL)), _const((D_MODEL, D_MODEL)), _const((4, D_MODEL, cs)),
                                    _full((1, D_MODEL)), _full((1, D_MODEL))],
        [_rows(tm, D_MODEL)] * 4, [_sds((seq, D_MODEL))] * 3 + [_sds((seq, D_MODEL), ACT_DTYPE)],
        (ycat, g, x, mod, mod_next, w_out, w_in4_next, ln_g, ln_b), "parallel", rider=rider)


def _halo_specs(tm, seq, width, order=lambda i: i):
    per = tm // 8
    last = seq // 8 - 1
    return [pl.BlockSpec((8, width), lambda i: (jnp.maximum(order(i) * per - 1, 0), 0)),
            pl.BlockSpec((tm, width), lambda i: (order(i), 0)),
            pl.BlockSpec((8, width), lambda i: (jnp.minimum((order(i) + 1) * per, last), 0))]


def _extended(prev_ref, main_ref, next_ref, i, n_steps):
    prev = jnp.where(i > 0, prev_ref[...], 0.0)
    nxt = jnp.where(i < n_steps - 1, next_ref[...], 0.0)
    return jnp.concatenate([prev, main_ref[...], nxt], axis=0)


def _shifted(ext, off, tm):
    if off == 0:
        return ext[8:8 + tm]
    return pltpu.roll(ext, (-off) % ext.shape[0], 0)[8:8 + tm]


SCAN_SUB = 8


def _lru_gate(xh, pre, bias, sp, hs, d):
    r = _sigmoid(pre[:, 0:LANES] + bias[2 * d:2 * d + 1, hs])
    ig = _sigmoid(pre[:, LANES:2 * LANES] + bias[2 * d + 1:2 * d + 2, hs])
    neg_log_a = RG_LRU_C * r * sp[d:d + 1, hs]
    a = jnp.exp(-neg_log_a)
    u = jnp.tanh(neg_log_a) * (a * a + 1.0)
    inv_s = lax.rsqrt(jnp.maximum(u, jnp.finfo(F32).tiny))
    return r, ig, a, u * inv_s, inv_s


def _conv_block(xp_ref, xm_ref, xn_ref, cw_ref, cb_ref, blk, steps, tm):
    ext = _extended(xp_ref, xm_ref, xn_ref, blk, steps)
    return cb_ref[...] + sum(cw_ref[kk:kk + 1, :] * _shifted(ext, kk - 2, tm) for kk in range(4))


def _scan_segments(a_s, b_s, h_s, hp_s, w_s, carry_h, carry_a, rows, descending, post):
    sub = SCAN_SUB
    seg_len = rows // sub
    tiles = D_MODEL // LANES
    row = lax.broadcasted_iota(jnp.int32, (sub, LANES), 0)
    at_step = lambda k: pl.ds((seg_len - 1 - k) if descending else k, sub, stride=seg_len)
    lanes = lambda j: slice(j * LANES, (j + 1) * LANES)

    def shift(v, d, fill):
        if descending:
            return jnp.where(row <= sub - 1 - d, pltpu.roll(v, sub - d, 0), fill)
        return jnp.where(row >= d, pltpu.roll(v, d, 0), fill)

    def last(v):
        return jnp.broadcast_to(v[0:1, :] if descending else v[sub - 1:sub, :], v.shape)

    def pass1(k, c):
        hs, ws, prevs = c
        out_h, out_w, out_p = [], [], []
        for j in range(tiles):
            at, bt = a_s.at[j][at_step(k), :], b_s.at[j][at_step(k), :]
            coef = prevs[j] if post else at
            h, w = coef * hs[j] + bt, coef * ws[j]
            h_s.at[j][at_step(k), :] = h
            w_s.at[j][at_step(k), :] = w
            out_h.append(h)
            out_w.append(w)
            out_p.append(at)
        return tuple(out_h), tuple(out_w), tuple(out_p) if post else prevs

    zeros = tuple(jnp.zeros((sub, LANES), F32) for _ in range(tiles))
    ones = tuple(jnp.ones((sub, LANES), F32) for _ in range(tiles))
    prev0 = tuple(shift(a_s.at[j][at_step(seg_len - 1), :], 1, carry_a[:, lanes(j)]) for j in range(tiles)) if post else zeros
    ends, prods, a_last = lax.fori_loop(0, seg_len, pass1, (zeros, ones, prev0), unroll=4)

    incoming = []
    for j in range(tiles):
        ch = carry_h[:, lanes(j)]
        acc_a, acc_b = prods[j], ends[j]
        for d in (1, 2, 4):
            acc_b = acc_b + acc_a * shift(acc_b, d, 0.0)
            acc_a = acc_a * shift(acc_a, d, 1.0)
        end_state = acc_b + acc_a * ch
        incoming.append(shift(end_state, 1, ch))
        carry_h[:, lanes(j)] = last(end_state)
        if post:
            carry_a[:, lanes(j)] = last(a_last[j])

    def pass2(k, prevs):
        out = []
        for j in range(tiles):
            h = h_s.at[j][at_step(k), :] + w_s.at[j][at_step(k), :] * incoming[j]
            h_s.at[j][at_step(k), :] = h
            if not post:
                hp_s.at[j][at_step(k), :] = prevs[j]
            out.append(h)
        return tuple(out)

    lax.fori_loop(0, seg_len, pass2, tuple(incoming), unroll=4)


def _lru_fwd(xr, xc, conv_w, conv_b, wcat, bias, lam, seq, d):
    tb = _row_tile(seq, 512)
    steps = seq // tb
    descending = d == 1
    order = (lambda i: steps - 1 - i) if descending else (lambda i: i)
    with_conv = xc is None
    n_x = 5 if with_conv else 1

    def body(*refs):
        x_refs, (w_ref, bias_ref, lam_ref) = refs[:n_x], refs[n_x:n_x + 3]
        h_ref, hp_ref, a_ref, r_ref, i_ref, s_ref, q_ref = refs[n_x + 3:n_x + 10]
        a_s, b_s, h_s, hp_s, w_s, carry_h, carry_a = refs[-7:]
        i = pl.program_id(0)

        @pl.when(i == 0)
        def _():
            carry_h[...] = jnp.zeros_like(carry_h)
            carry_a[...] = jnp.zeros_like(carry_a)

        if with_conv:
            xc_ref = refs[n_x + 10]
            xc_ref[...] = _conv_block(*x_refs, order(i), steps, tb)
        else:
            xc_ref = x_refs[0]
        sp, _ = _softplus_neg(lam_ref[...])
        bias = bias_ref[...]
        for h in range(RNN_HEADS):
            hs = slice(h * LANES, (h + 1) * LANES)
            xh = xc_ref[:, hs]
            r, ig, a, s, q = _lru_gate(xh, _mm(xh, w_ref[h, :, 2 * d * LANES:2 * (d + 1) * LANES]), bias, sp, hs, d)
            a_ref[:, hs] = a
            a_s[h] = a
            b_s[h] = s * ig * xh
            for ref, val in ((r_ref, r), (i_ref, ig), (s_ref, s), (q_ref, q)):
                ref[:, hs] = val.astype(ref.dtype)
        _scan_segments(a_s, b_s, h_s, hp_s, w_s, carry_h, carry_a, tb, descending, post=False)
        for h in range(RNN_HEADS):
            h_ref[:, h * LANES:(h + 1) * LANES] = h_s[h]
            hp_ref[:, h * LANES:(h + 1) * LANES] = hp_s[h]

    row_spec = pl.BlockSpec((tb, D_MODEL), lambda i: (order(i), 0))
    if with_conv:
        x_specs, x_args = _halo_specs(tb, seq, D_MODEL, order) + [_full((4, D_MODEL)), _full((1, D_MODEL))], (xr, xr, xr, conv_w, conv_b)
    else:
        x_specs, x_args = [row_spec], (xc,)
    n_out = 8 if with_conv else 7
    return pl.pallas_call(
        body, name="lru_fwd_%d" % d, grid=(steps,),
        in_specs=x_specs + [_full((8, LANES, 512)), _full((4, D_MODEL)), _full((2, D_MODEL))],
        out_specs=[row_spec] * n_out,
        out_shape=[_sds((seq, D_MODEL))] * 3 + [_sds((seq, D_MODEL), ACT_DTYPE)] * 4 + [_sds((seq, D_MODEL))] * (n_out - 7),
        scratch_shapes=[pltpu.VMEM((RNN_HEADS, tb, LANES), F32)] * 5 + [pltpu.VMEM((SCAN_SUB, D_MODEL), F32)] * 2,
        compiler_params=_params("arbitrary"),
    )(*x_args, wcat, bias, lam)


def _odd_out_and_loss(hf, hr, g, x1, tgt, mod, w_out, ln_g, ln_b, seq):
    tm = _row_tile(seq, 512)

    def body(hf_ref, hr_ref, g_ref, x_ref, t_ref, mod_ref, w_ref, lg_ref, lb_ref,
             dhs_ref, dg_ref, dres_ref, loss_ref, dw_ref, vec_ref):
        @pl.when(pl.program_id(0) == 0)
        def _():
            loss_ref[...] = jnp.zeros_like(loss_ref)
            dw_ref[...] = jnp.zeros_like(dw_ref)
            vec_ref[...] = jnp.zeros_like(vec_ref)

        gg = g_ref[...].astype(F32)
        sg = _sigmoid(gg)
        silu = gg * sg
        hsum = hf_ref[...] + hr_ref[...]
        y = hsum * silu
        out = _mm(y, w_ref[...])
        gate = mod_ref[2:3, :]
        z = ALPHA * x_ref[...] + gate * out
        zhat, rstd = _ln_stats(z)
        x2 = zhat * lg_ref[...] + lb_ref[...]
        err = x2 - t_ref[...]
        loss_ref[...] += 0.5 * jnp.sum(jnp.mean(err * err, axis=-1, keepdims=True))
        dx2 = err * (1.0 / D_MODEL)
        dz = _ln_bwd(dx2, zhat, rstd, lg_ref[...])
        vec_ref[0:1, :] += jnp.sum(dx2 * zhat, axis=0, keepdims=True)
        vec_ref[1:2, :] += jnp.sum(dx2, axis=0, keepdims=True)
        vec_ref[2:3, :] += jnp.sum(dz * out, axis=0, keepdims=True)
        dres_ref[...] = ALPHA * dz
        dout = gate * dz
        dw_ref[...] += _mm_tn(y, dout)
        dy = _mm_nt(dout, w_ref[...])
        dhs_ref[...] = dy * silu
        dg_ref[...] = (dy * hsum * (sg * (1.0 + gg * (1.0 - sg)))).astype(dg_ref.dtype)

    return pl.pallas_call(
        body, name="odd_out_loss", grid=(seq // tm,),
        in_specs=[_rows(tm, D_MODEL)] * 5 + [_full((3, D_MODEL)), _const((D_MODEL, D_MODEL)),
                                             _full((1, D_MODEL)), _full((1, D_MODEL))],
        out_specs=[_rows(tm, D_MODEL)] * 3 + [_full((8, LANES)), _full((D_MODEL, D_MODEL)), _full((8, D_MODEL))],
        out_shape=[_sds((seq, D_MODEL)), _sds((seq, D_MODEL), ACT_DTYPE), _sds((seq, D_MODEL)), _sds((8, LANES)),
                   _sds((D_MODEL, D_MODEL)), _sds((8, D_MODEL))],
        compiler_params=_params("arbitrary"),
    )(hf, hr, g, x1, tgt, mod, w_out, ln_g, ln_b)


def _lru_bwd(xc, dhs, hprev, a_d, r_d, i_d, s_d, q_d, wcat, lam, seq, d):
    tb = _row_tile(seq, 512)
    steps = seq // tb
    descending = d == 0
    order = (lambda i: steps - 1 - i) if descending else (lambda i: i)
    cols = slice(2 * d * LANES, 2 * (d + 1) * LANES)

    def body(xc_ref, dhs_ref, hp_ref, a_ref, r_ref, i_ref, s_ref, q_ref, w_ref, lam_ref, dxc_ref, dw_ref, vec_ref,
             a_s, b_s, g_s, w_s, carry_h, carry_a):
        i = pl.program_id(0)

        @pl.when(i == 0)
        def _():
            dw_ref[...] = jnp.zeros_like(dw_ref)
            vec_ref[...] = jnp.zeros_like(vec_ref)
            carry_h[...] = jnp.zeros_like(carry_h)
            carry_a[...] = jnp.zeros_like(carry_a)

        sp, dsp = _softplus_neg(lam_ref[...])
        for h in range(RNN_HEADS):
            a_s[h] = a_ref[:, h * LANES:(h + 1) * LANES]
            b_s[h] = dhs_ref[:, h * LANES:(h + 1) * LANES]
        _scan_segments(a_s, b_s, g_s, None, w_s, carry_h, carry_a, tb, descending, post=True)
        for h in range(RNN_HEADS):
            hs = slice(h * LANES, (h + 1) * LANES)
            xh, a = xc_ref[:, hs], a_s[h]
            r, ig, s = r_ref[:, hs].astype(F32), i_ref[:, hs].astype(F32), s_ref[:, hs].astype(F32)
            db = g_s[h]
            da = db * hp_ref[:, hs]
            dlog_a = da * a - (db * ig * xh) * (a * a * q_ref[:, hs].astype(F32))
            dpr = dlog_a * (-RG_LRU_C) * sp[d:d + 1, hs] * r * (1.0 - r)
            dpi = db * s * xh * ig * (1.0 - ig)
            vec_ref[0:1, hs] += jnp.sum(dpr, axis=0, keepdims=True)
            vec_ref[1:2, hs] += jnp.sum(dpi, axis=0, keepdims=True)
            vec_ref[2:3, hs] += jnp.sum(dlog_a * r, axis=0, keepdims=True) * (-RG_LRU_C) * dsp[d:d + 1, hs]
            dcat = jnp.concatenate([dpr, dpi], axis=1)
            dw_ref[h] += _mm_tn(xh, dcat)
            dxc_ref[:, hs] = db * s * ig + _mm_nt(dcat, w_ref[h, :, cols])

    row_spec = pl.BlockSpec((tb, D_MODEL), lambda i: (order(i), 0))
    return pl.pallas_call(
        body, name="lru_bwd_%d" % d, grid=(steps,),
        in_specs=[row_spec] * 8 + [_full((8, LANES, 512)), _full((2, D_MODEL))],
        out_specs=[row_spec, _full((8, LANES, 2 * LANES)), _full((8, D_MODEL))],
        out_shape=[_sds((seq, D_MODEL)), _sds((8, LANES, 2 * LANES)), _sds((8, D_MODEL))],
        scratch_shapes=[pltpu.VMEM((RNN_HEADS, tb, LANES), F32)] * 4 + [pltpu.VMEM((SCAN_SUB, D_MODEL), F32)] * 2,
        compiler_params=_params("arbitrary"),
    )(xc, dhs, hprev, a_d, r_d, i_d, s_d, q_d, wcat, lam)


def _odd_proj_bwd(dxc_f, dxc_r, xr, dg, x1, dres, mod, conv_w, w_in4, seq):
    tm = _row_tile(seq, 512)
    steps = seq // tm

    def body(fp_ref, fm_ref, fn_ref, rp_ref, rm_ref, rn_ref, xp_ref, xm_ref, xn_ref, dg_ref, x_ref, dres_ref, mod_ref, cw_ref,
             w_ref, dx_ref, dw_ref, vec_ref, dpb_ref):
        i = pl.program_id(0)

        @pl.when(i == 0)
        def _():
            vec_ref[...] = jnp.zeros_like(vec_ref)
            dw_ref[...] = jnp.zeros_like(dw_ref)

        dxc_m = fm_ref[...] + rm_ref[...]
        dext = jnp.concatenate([jnp.where(i > 0, fp_ref[...] + rp_ref[...], 0.0), dxc_m,
                                jnp.where(i < steps - 1, fn_ref[...] + rn_ref[...], 0.0)], axis=0)
        xext = _extended(xp_ref, xm_ref, xn_ref, i, steps)
        dxr = sum(cw_ref[kk:kk + 1, :] * _shifted(dext, 2 - kk, tm) for kk in range(4))
        for kk in range(4):
            vec_ref[kk:kk + 1, :] += jnp.sum(dxc_m * _shifted(xext, kk - 2, tm), axis=0, keepdims=True)
        vec_ref[4:5, :] += jnp.sum(dxc_m, axis=0, keepdims=True)
        dpb_ref[:, :D_MODEL] = dxr.astype(dpb_ref.dtype)
        dpb_ref[:, D_MODEL:] = dg_ref[...].astype(dpb_ref.dtype)
        cs = ODD_IN // 4
        dh = sum(_mm_nt(dpb_ref[:, s * cs:(s + 1) * cs], w_ref[s]) for s in range(4))
        x = x_ref[...]
        h_t = (x * (1.0 + mod_ref[1:2, :]) + mod_ref[0:1, :]).T.astype(MXU_DTYPE)
        for s in range(4):
            dw_ref[s] += jnp.dot(h_t, dpb_ref[:, s * cs:(s + 1) * cs], preferred_element_type=F32)
        vec_ref[5:6, :] += jnp.sum(dh, axis=0, keepdims=True)
        vec_ref[6:7, :] += jnp.sum(dh * x, axis=0, keepdims=True)
        dx_ref[...] = dres_ref[...] + dh * (1.0 + mod_ref[1:2, :])

    return pl.pallas_call(
        body, name="odd_proj_bwd", grid=(steps,),
        in_specs=_halo_specs(tm, seq, D_MODEL) * 3 + [_rows(tm, D_MODEL)] * 3
        + [_full((3, D_MODEL)), _full((4, D_MODEL)), _const((4, D_MODEL, ODD_IN // 4))],
        out_specs=[_rows(tm, D_MODEL), _const((4, D_MODEL, ODD_IN // 4)), _full((8, D_MODEL))],
        out_shape=[_sds((seq, D_MODEL)), _sds((4, D_MODEL, ODD_IN // 4)), _sds((8, D_MODEL))],
        scratch_shapes=[pltpu.VMEM((tm, ODD_IN), MXU_DTYPE)],
        compiler_params=_params("arbitrary"),
    )(dxc_f, dxc_f, dxc_f, dxc_r, dxc_r, dxc_r, xr, xr, xr, dg, x1, dres, mod, conv_w, w_in4)


def _even_out_bwd(dx1, z, ycat, g, mod, ln_g, w_out, seq, rider=None):
    tm = _row_tile(seq, 512)
    steps = seq // tm

    def body(dx_ref, z_ref, y_ref, g_ref, mod_ref, lg_ref, w_ref, dy_ref, dg_ref, dres_ref, dw_ref, vec_ref):
        i = pl.program_id(0)

        @pl.when(i == 0)
        def _():
            dw_ref[...] = jnp.zeros_like(dw_ref)
            vec_ref[...] = jnp.zeros_like(vec_ref)

        zhat, rstd = _ln_stats(z_ref[...])
        dx1_ = dx_ref[...]
        dz = _ln_bwd(dx1_, zhat, rstd, lg_ref[...])
        vec_ref[0:1, :] += jnp.sum(dx1_ * zhat, axis=0, keepdims=True)
        vec_ref[1:2, :] += jnp.sum(dx1_, axis=0, keepdims=True)
        dres_ref[...] = ALPHA * dz
        gate = mod_ref[2:3, :]
        gg = g_ref[...].astype(F32)
        sg = _sigmoid(gg)
        silu = gg * sg
        ycat_ = y_ref[...].astype(F32)
        dw_ref[...] += _mm_tn(ycat_ * silu, dz)
        dy = _mm_nt(gate * dz, w_ref[...])
        dy_ref[...] = (dy * silu).astype(dy_ref.dtype)
        dg_ref[...] = (dy * ycat_ * (sg * (1.0 + gg * (1.0 - sg)))).astype(dg_ref.dtype)

        @pl.when(i == steps - 1)
        def _():
            m_acc = dw_ref[...]
            vec_ref[2:3, :] = jnp.sum(w_ref[...].astype(F32) * m_acc, axis=0, keepdims=True)
            dw_ref[...] = m_acc * gate

    return _call(
        body, "even_out_bwd", (steps,),
        [_rows(tm, D_MODEL)] * 4 + [_full((3, D_MODEL)), _full((1, D_MODEL)), _const((D_MODEL, D_MODEL))],
        [_rows(tm, D_MODEL)] * 3 + [_full((D_MODEL, D_MODEL)), _full((8, D_MODEL))],
        [_sds((seq, D_MODEL), ACT_DTYPE), _sds((seq, D_MODEL), ACT_DTYPE), _sds((seq, D_MODEL)), _sds((D_MODEL, D_MODEL)),
         _sds((8, D_MODEL))],
        (dx1, z, ycat, g, mod, ln_g, w_out), "arbitrary", rider=rider)


def _even_mix_bwd(q, k, v, lse, ycat, dycat, su, svo_s, vhat_s, rstd_s, sink, sgln_g, sgln_b, sgw, e2, e8, seq, rider=None):
    nb = seq // BLK

    def body(sink_ref, q_ref, k_ref, v_ref, lse_ref, y_ref, dy_ref, su_ref, svo_ref, vhat_ref, rstd_ref, lng_ref, lnb_ref, sgw_ref,
             e2_ref, e8_ref, dq_ref, dsu_ref, dsv_ref, dk_ref, dv_ref, dsgw_ref, dsgb_ref, vec_ref, dsink_ref, dsgb_acc):
        n = pl.program_id(0)

        @pl.when(n == 0)
        def _():
            dk_ref[...] = jnp.zeros_like(dk_ref)
            dv_ref[...] = jnp.zeros_like(dv_ref)
            dsgw_ref[...] = jnp.zeros_like(dsgw_ref)
            dsgb_acc[...] = jnp.zeros_like(dsgb_acc)
            vec_ref[...] = jnp.zeros_like(vec_ref)
            dsink_ref[...] = jnp.zeros_like(dsink_ref)

        kband = _band(k_ref, n, nb)
        vband = _band(v_ref, n, nb)
        bias = _band_bias(n, seq)
        lane = _lane_iota((BLK, LANES))
        row8 = lax.broadcasted_iota(jnp.int32, (8, LANES), 0)
        lse = lse_ref[...]
        dkb = jnp.zeros((LANES, 3 * BLK), F32)
        dvb = jnp.zeros((LANES, 3 * BLK), F32)
        dsink = jnp.zeros((8, LANES), F32)
        q_tile = lambda j: q_ref[:, j * LANES:(j + 1) * LANES].astype(F32)
        do_tile = lambda j: dy_ref[:, j * LANES:(j + 1) * LANES].astype(F32)
        dq = [jnp.zeros((BLK, LANES), F32) for _ in range(ATTN_WIDTH // LANES)]
        for kv in range(N_Q_HEADS // Q_PER_KV):
            heads = range(Q_PER_KV * kv, Q_PER_KV * (kv + 1))
            lse4, delta4 = [], []
            for h in heads:
                head_lanes = (lane < HEAD_DIM) if h % 2 == 0 else (lane >= HEAD_DIM)
                lse4.append(jnp.sum(jnp.where(lane == h, lse, 0.0), axis=1, keepdims=True))
                o_tile = y_ref[:, (h // 2) * LANES:(h // 2 + 1) * LANES].astype(F32)
                delta4.append(jnp.sum(jnp.where(head_lanes, do_tile(h // 2) * o_tile, 0.0), axis=1, keepdims=True))
            lse4, delta4 = jnp.concatenate(lse4, axis=0), jnp.concatenate(delta4, axis=0)
            q4, do4 = _stack_heads(q_tile, kv), _stack_heads(do_tile, kv)
            s = _mm_nt(q4, kband) * (HEAD_DIM ** -0.5) + bias
            p = jnp.exp(s - lse4)
            wsink = jnp.exp(_per_head_column([sink_ref[h] for h in heads]) - lse4) * delta4
            ds = p * (_mm_nt(do4, vband) - delta4) * (HEAD_DIM ** -0.5)
            dq4 = _mm(ds, kband)
            dkb = dkb + _mm_tn(q4, ds)
            dvb = dvb + _mm_tn(do4, p)
            for g, h in enumerate(heads):
                dq[h // 2] = dq[h // 2] + _from_kv_lanes(dq4[g * BLK:(g + 1) * BLK], h)
                dsink = dsink + jnp.where(row8 == h, -jnp.sum(wsink[g * BLK:(g + 1) * BLK]), 0.0)
        for j in range(ATTN_WIDTH // LANES):
            dq_ref[:, j * LANES:(j + 1) * LANES] = dq[j].astype(dq_ref.dtype)
        dsink_ref[...] += dsink
        prev = jnp.maximum(n - 1, 0)
        nxt = jnp.minimum(n + 1, nb - 1)
        for part, blk_i in enumerate((prev, n, nxt)):
            rows = pl.ds(pl.multiple_of(blk_i * BLK, BLK), BLK)
            dk_ref[rows, :] += dkb[:, part * BLK:(part + 1) * BLK].T
            dv_ref[rows, :] += dvb[:, part * BLK:(part + 1) * BLK].T

        e2 = e2_ref[...]
        lng, lnb = lng_ref[...], lnb_ref[...]
        for j in range(SG_WIDTH // LANES):
            cs = slice(j * LANES, (j + 1) * LANES)
            vhat = vhat_ref[:, cs].astype(F32)
            vn = vhat * lng[:, cs] + lnb[:, cs]
            dysg = dy_ref[:, ATTN_WIDTH + j * LANES:ATTN_WIDTH + (j + 1) * LANES].astype(F32)
            dsu_ref[:, cs] = (dysg * svo_ref[:, cs].astype(F32)).astype(dsu_ref.dtype)
            dsvo = dysg * su_ref[:, cs].astype(F32)
            dsgb_acc[:, cs] += dsvo
            d_lo = jnp.where(lane < HEAD_DIM, dsvo, 0.0)
            d_hi = dsvo - d_lo
            dsgw_ref[2 * j] += _mm_nt(d_lo, vn)
            dsgw_ref[2 * j + 1] += _mm_nt(d_hi, vn)
            dvn = _mm_tn(sgw_ref[2 * j], d_lo) + _mm_tn(sgw_ref[2 * j + 1], d_hi)
            vec_ref[0:1, cs] += jnp.sum(dvn * vhat, axis=0, keepdims=True)
            vec_ref[1:2, cs] += jnp.sum(dvn, axis=0, keepdims=True)
            dvh = dvn * lng[:, cs]
            m1 = _group_sum(dvh, e2) * (1.0 / HEAD_DIM)
            m2 = _group_sum(dvh * vhat, e2) * (1.0 / HEAD_DIM)
            dsv_ref[:, cs] = (rstd_ref[:, cs].astype(F32) * (dvh - m1 - vhat * m2)).astype(dsv_ref.dtype)

        @pl.when(n == nb - 1)
        def _():
            rest = dsgb_acc[...]
            total = jnp.zeros((8, BLK), F32)
            for _ in range(3):
                part = rest.astype(MXU_DTYPE)
                total = total + lax.dot_general(e8_ref[...], part, (((1,), (1,)), ((), ())), preferred_element_type=F32)
                rest = rest - part.astype(F32)
            dsgb_ref[...] = total

    blk = lambda w: pl.BlockSpec((BLK, w), lambda n: (n, 0))
    return _call(
        body, "even_mix_bwd", (nb,),
        [pl.BlockSpec(memory_space=pltpu.SMEM), blk(512), _full((seq, LANES)), _full((seq, LANES)), blk(LANES),
         blk(D_MODEL), blk(D_MODEL), blk(512), blk(512), blk(512), blk(512), _full((1, 512)), _full((1, 512)), _full((8, BLK, BLK)),
         _full((LANES, LANES)), _full((8, 512))],
        [blk(512), blk(512), blk(512), _full((seq, LANES)), _full((seq, LANES)), _full((8, BLK, BLK)),
         _full((8, BLK)), _full((8, 512)), _full((8, LANES))],
        [_sds((seq, 512), ACT_DTYPE), _sds((seq, 512), ACT_DTYPE), _sds((seq, 512), ACT_DTYPE), _sds((seq, LANES)), _sds((seq, LANES)),
         _sds((8, BLK, BLK)), _sds((8, BLK)), _sds((8, 512)), _sds((8, LANES))],
        (sink, q, k, v, lse, ycat, dycat, su, svo_s, vhat_s, rstd_s, sgln_g, sgln_b, sgw, e2, e8), "arbitrary",
        scratch=[pltpu.VMEM((BLK, 512), F32)], rider=rider)


def _even_proj_bwd(dq, dk, dv, dsu, dsv, dg, x, dres, mod, tabs, w_in_t, seq):
    tm = _row_tile(seq, 512)

    def body(dq_ref, dk_ref, dv_ref, dsu_ref, dsv_ref, dg_ref, x_ref, dres_ref, mod_ref, cos_ref, sp_ref, sm_ref, wt_ref,
             dx_ref, dw_ref, vec_ref, dpb_ref):
        @pl.when(pl.program_id(0) == 0)
        def _():
            vec_ref[...] = jnp.zeros_like(vec_ref)
            dw_ref[...] = jnp.zeros_like(dw_ref)

        cos_t, sin_p, sin_m = cos_ref[...], sp_ref[...], sm_ref[...]
        dt = dpb_ref.dtype
        for j in range(ATTN_WIDTH // LANES):
            cs = slice(j * LANES, (j + 1) * LANES)
            dpb_ref[:, cs] = _rope_t(dq_ref[:, cs].astype(F32), cos_t, sin_p, sin_m).astype(dt)
        dpb_ref[:, 512:640] = _rope_t(dk_ref[...], cos_t, sin_p, sin_m).astype(dt)
        dpb_ref[:, 640:768] = dv_ref[...].astype(dt)
        dpb_ref[:, 768:1280] = dsu_ref[...].astype(dt)
        dpb_ref[:, 1280:1792] = dsv_ref[...].astype(dt)
        dpb_ref[:, 1792:2816] = dg_ref[...].astype(dt)
        dpb = dpb_ref[...]
        dh = jnp.dot(dpb, wt_ref[...], preferred_element_type=F32)
        x_ = x_ref[...]
        hb = (x_ * (1.0 + mod_ref[1:2, :]) + mod_ref[0:1, :]).astype(MXU_DTYPE)
        dw_ref[...] += _mm_tn(dpb, hb)
        vec_ref[0:1, :] += jnp.sum(dh, axis=0, keepdims=True)
        vec_ref[1:2, :] += jnp.sum(dh * x_, axis=0, keepdims=True)
        dx_ref[...] = dres_ref[...] + dh * (1.0 + mod_ref[1:2, :])

    return pl.pallas_call(
        body, name="even_proj_bwd", grid=(seq // tm,),
        in_specs=[_rows(tm, 512), _rows(tm, LANES), _rows(tm, LANES), _rows(tm, 512), _rows(tm, 512), _rows(tm, D_MODEL),
                  _rows(tm, D_MODEL), _rows(tm, D_MODEL), _full((3, D_MODEL))] + [_rows(tm, LANES)] * 3
        + [_const((EVEN_IN, D_MODEL))],
        out_specs=[_rows(tm, D_MODEL), _const((EVEN_IN, D_MODEL)), _full((8, D_MODEL))],
        out_shape=[_sds((seq, D_MODEL)), _sds((EVEN_IN, D_MODEL)), _sds((8, D_MODEL))],
        scratch_shapes=[pltpu.VMEM((tm, EVEN_IN), MXU_DTYPE)],
        compiler_params=_params("arbitrary"),
    )(dq, dk, dv, dsu, dsv, dg, x, dres, mod, *tabs, w_in_t)


def _local_step(x, posf, tgt, mod, w, seq, ride=None):
    rid = lambda make, *a: None if ride is None else make(*a)
    mxu = lambda a: a.astype(MXU_DTYPE)
    row = lambda a: a.reshape(1, -1)
    tabs = _rope_tables(posf, seq)
    e2 = mxu(jnp.kron(jnp.eye(2, dtype=F32), jnp.ones((HEAD_DIM, HEAD_DIM), F32)))
    e8 = mxu(jnp.repeat(jnp.eye(N_SG_GROUPS, dtype=F32), HEAD_DIM, axis=1))
    sgw = mxu(w["ev_sg_w"])
    sgb_full = jnp.repeat(w["ev_sg_b"].T, HEAD_DIM, axis=1)
    sgln_g, sgln_b = row(w["ev_sg_ln_g"]), row(w["ev_sg_ln_b"])
    sink = w["ev_sink"].reshape(N_Q_HEADS)
    ev_w_in_t = mxu(w["ev_w_in_t"])
    if ride is None:
        ev_w_out, od_w_in, od_w_out = mxu(w["ev_w_out"]), mxu(w["od_w_in"]), mxu(w["od_w_out"])
    wcat = mxu(jnp.concatenate([w["od_w_a"][0], w["od_w_x"][0], w["od_w_a"][1], w["od_w_x"][1]], axis=2))
    gate_bias = jnp.stack([w["od_b_a"][0], w["od_b_x"][0], w["od_b_a"][1], w["od_b_x"][1]])
    conv_b = row(w["od_conv_b"])
    ln_g, ln_b = w["ln_g"], w["ln_b"]

    (q, k, v, su, sv, g0), got = _even_proj(x, mod[0], ev_w_in_t, tabs, seq, rid(_gather_rider, ride and ride["ev_w_out"]))
    if ride is not None:
        ev_w_out = got[0].reshape(D_MODEL, D_MODEL)
    (ycat, lse, *sg_saved), got = _even_mix(q, k, v, su, sv, sink, sgln_g, sgln_b, sgw, sgb_full, e2, seq,
                                 rid(_gather_rider, ride and ride["od_w_in"]))
    if ride is not None:
        od_w_in = got[0]
    (z0, x1, xr, g1), got = _even_out(ycat, g0, x, mod[0], mod[1], ev_w_out, od_w_in, ln_g[0:1], ln_b[0:1], seq,
                                      rid(_gather_rider, ride and ride["od_w_out"]))
    if ride is not None:
        od_w_out = got[0].reshape(D_MODEL, D_MODEL)
    lru = (w["od_conv_w"], conv_b, wcat, gate_bias, w["od_lam"], seq)
    hf, hpf, *saved_f, xc = _lru_fwd(xr, None, *lru, 0)
    hr, hpr, *saved_r = _lru_fwd(xr, xc, *lru, 1)
    dhs, dg1, dres1, loss, d_od_w_out, vec_o = _odd_out_and_loss(hf, hr, g1, x1, tgt, mod[1], od_w_out, ln_g[1:2], ln_b[1:2], seq)
    dxc_f, dw_f, vec_f = _lru_bwd(xc, dhs, hpf, *saved_f, wcat, w["od_lam"], seq, 0)
    dxc_r, dw_r, vec_r = _lru_bwd(xc, dhs, hpr, *saved_r, wcat, w["od_lam"], seq, 1)
    dx1, d_od_w_in, vec_p = _odd_proj_bwd(dxc_f, dxc_r, xr, dg1, x1, dres1, mod[1], w["od_conv_w"], od_w_in, seq)
    d_od_w_a = jnp.stack([dw_f[:, :, 0:128], dw_r[:, :, 0:128]])
    d_od_w_x = jnp.stack([dw_f[:, :, 128:256], dw_r[:, :, 128:256]])
    od_parts = [d_od_w_in.reshape(4, 2, 512, 512), d_od_w_out.reshape(4, 2, 128, D_MODEL),
                d_od_w_a.reshape(4, 2, 2 * BLK, BLK), d_od_w_x.reshape(4, 2, 2 * BLK, BLK)]
    (dycat, dg0, dres0, d_ev_w_out, vec_e), got_od = _even_out_bwd(dx1, z0, ycat, g0, mod[0], ln_g[0:1], ev_w_out, seq,
                                                                   rid(_sibling_swap_rider, od_parts))
    if ride is not None:
        od_sums = _sum_sibling(ride["core"], od_parts, got_od, [ride["wire"]] * 4, "sum_sibling_od")
    (dq, dsu, dsv, dk, dv, d_sgw, d_sgb, vec_s, d_sink), od_slots = _even_mix_bwd(
        q, k, v, lse, ycat, dycat, su, *sg_saved, sink, sgln_g, sgln_b, sgw, e2, e8, seq,
        rid(_chip_exchange_rider, ride and od_sums))
    grad_x, d_ev_w_in_t, vec_x = _even_proj_bwd(dq, dk, dv, dsu, dsv, dg0, x, dres0, mod[0], tabs, ev_w_in_t, seq)

    rows, dmod_blk = _pack_small(vec_x, vec_e, vec_p, vec_o, vec_f, vec_r, vec_s, d_sink, d_sgb, loss)
    grads = {"rows": rows, "dmod_blk": dmod_blk, "ev_w_in_t": d_ev_w_in_t, "ev_w_out": d_ev_w_out, "ev_sg_w": d_sgw}
    if ride is None:
        grads.update({"od_w_in": d_od_w_in, "od_w_out": d_od_w_out, "od_w_a": d_od_w_a, "od_w_x": d_od_w_x})
    else:
        grads["od_slots"] = od_slots
    return grad_x, grads


ROW_DMOD, ROW_LN, ROW_SG_LN, ROW_SG_B, ROW_CONV_W, ROW_CONV_B, ROW_B_A, ROW_B_X, ROW_LAM, ROW_SINK, ROW_LOSS = (
    0, 6, 10, 11, 12, 16, 17, 19, 21, 23, 24)
SMALL_ROWS = 64


def _pack_small(vec_x, vec_e, vec_p, vec_o, vec_f, vec_r, vec_s, d_sink, d_sgb, loss):
    def body(x_ref, e_ref, p_ref, o_ref, f_ref, r_ref, s_ref, sink_ref, sgb_ref, loss_ref, rows_ref, dmod_ref):
        rows_ref[...] = jnp.zeros_like(rows_ref)
        dmod_ref[...] = jnp.zeros_like(dmod_ref)
        put = [(ROW_DMOD, x_ref, 0), (ROW_DMOD + 1, x_ref, 1), (ROW_DMOD + 2, e_ref, 2), (ROW_DMOD + 3, p_ref, 5),
               (ROW_DMOD + 4, p_ref, 6), (ROW_DMOD + 5, o_ref, 2), (ROW_LN, e_ref, 0), (ROW_LN + 1, e_ref, 1),
               (ROW_LN + 2, o_ref, 0), (ROW_LN + 3, o_ref, 1), (ROW_CONV_B, p_ref, 4), (ROW_B_A, f_ref, 0),
               (ROW_B_A + 1, r_ref, 0), (ROW_B_X, f_ref, 1), (ROW_B_X + 1, r_ref, 1), (ROW_LAM, f_ref, 2), (ROW_LAM + 1, r_ref, 2)]
        put += [(ROW_CONV_W + k, p_ref, k) for k in range(4)]
        for dst, ref, src in put:
            rows_ref[dst:dst + 1, :] = ref[src:src + 1, :]
            if dst < 6:
                dmod_ref[dst:dst + 1, :] = ref[src:src + 1, :]
        rows_ref[ROW_SG_LN:ROW_SG_LN + 1, 0:SG_WIDTH] = s_ref[0:1, :]
        rows_ref[ROW_SG_LN:ROW_SG_LN + 1, SG_WIDTH:2 * SG_WIDTH] = s_ref[1:2, :]
        lane = _lane_iota((1, LANES))
        sink = jnp.zeros((1, LANES), F32)
        for h in range(N_Q_HEADS):
            rows_ref[ROW_SG_B:ROW_SG_B + 1, h * LANES:(h + 1) * LANES] = sgb_ref[h:h + 1, :]
            sink = jnp.where(lane == h, sink_ref[h:h + 1, :], sink)
        rows_ref[ROW_SINK:ROW_SINK + 1, 0:LANES] = sink
        rows_ref[ROW_LOSS:ROW_LOSS + 1, 0:LANES] = jnp.where(lane == 0, loss_ref[0:1, :], 0.0)

    return pl.pallas_call(body, name="pack_small", out_shape=[_sds((SMALL_ROWS, D_MODEL)), _sds((8, D_MODEL))])(
        vec_x, vec_e, vec_p, vec_o, vec_f, vec_r, vec_s, d_sink, d_sgb, loss)


def _allgather8(block, name):
    m_per, n = block.shape

    def body(x_ref, out_ref, send_sems, recv_sems, local_sem):
        x, y, c = _place()
        me, sibling = (x, y, c), (x, y, 1 - c)
        chips = [(1 - x, y), (x, 1 - y), (1 - x, 1 - y)]

        def rows(px, py, pc):
            return out_ref.at[pl.ds((4 * px + 2 * py + pc) * m_per, m_per), :]

        def copy(k, blk, to, src=None):
            return pltpu.make_async_remote_copy(src_ref=rows(*blk) if src is None else src, dst_ref=rows(*blk),
                                                send_sem=send_sems.at[k], recv_sem=recv_sems.at[k], device_id=to,
                                                device_id_type=MESH)

        mine = pltpu.make_async_copy(x_ref, rows(*me), local_sem)
        mine.start()
        first = [copy(0, me, sibling, src=x_ref)] + [copy(1 + j, me, (*chip, c), src=x_ref) for j, chip in enumerate(chips)]
        for cp in first:
            cp.start()
        passed = [copy(4 + j, (*chip, c), sibling) for j, chip in enumerate(chips)]
        for j, chip in enumerate(chips):
            copy(1 + j, (*chip, c), me).wait_recv()
            passed[j].start()
        copy(0, sibling, me).wait_recv()
        for j, chip in enumerate(chips):
            copy(4 + j, (*chip, 1 - c), me).wait_recv()
        for cp in first + passed:
            cp.wait_send()
        mine.wait()

    return pl.pallas_call(
        body, name=name, out_shape=_sds((8 * m_per, n), block.dtype),
        in_specs=[pl.BlockSpec(memory_space=pltpu.VMEM)], out_specs=pl.BlockSpec(memory_space=pltpu.VMEM),
        scratch_shapes=[pltpu.SemaphoreType.DMA((7,)), pltpu.SemaphoreType.DMA((7,)), pltpu.SemaphoreType.DMA],
        compiler_params=pltpu.CompilerParams(vmem_limit_bytes=VMEM_LIMIT),
    )(block)


class _Copies:
    def __init__(self, send_sems, recv_sems, local_sems, stages):
        self.send_sems, self.recv_sems, self.local_sems, self.stages = send_sems, recv_sems, local_sems, stages
        self.sent, self.staged, self.locals = [], [], []

    def remote(self, k, src, dst, to):
        return pltpu.make_async_remote_copy(src_ref=src, dst_ref=dst, send_sem=self.send_sems.at[k], recv_sem=self.recv_sems.at[k],
                                            device_id=to, device_id_type=MESH)

    def send(self, k, src, dst, to):
        cp = self.remote(k, src, dst, to)
        cp.start()
        self.sent.append(cp)

    def arrived(self, k, dst, frm):
        self.remote(k, dst, dst, frm).wait_recv()

    def local(self, src, dst):
        k = len(self.staged)
        cp = pltpu.make_async_copy(src, self.stages[k], self.local_sems.at[2 * k])
        cp.start()
        self.staged.append((cp, dst))

    def flush(self):
        for k in range(len(self.locals), len(self.staged)):
            cp, dst = self.staged[k]
            cp.wait()
            out = pltpu.make_async_copy(self.stages[k], dst, self.local_sems.at[2 * k + 1])
            out.start()
            self.locals.append(out)

    def drain(self):
        self.flush()
        for cp in self.sent:
            cp.wait_send()
        for cp in self.locals:
            cp.wait()


def _comm_call(body, name, ins, out_shapes, n_remote, stages):
    n_in, n_out = len(ins), len(out_shapes)

    def kern(*refs):
        in_refs, out_refs = refs[:n_in], refs[n_in:n_in + n_out]
        send_sems, recv_sems, local_sems = refs[n_in + n_out:n_in + n_out + 3]
        body(_Copies(send_sems, recv_sems, local_sems, refs[n_in + n_out + 3:]), in_refs, out_refs)

    hbm = pl.BlockSpec(memory_space=pl.ANY)
    return pl.pallas_call(
        kern, name=name, out_shape=out_shapes, in_specs=[hbm] * n_in, out_specs=[hbm] * n_out,
        scratch_shapes=[pltpu.SemaphoreType.DMA((n_remote,)), pltpu.SemaphoreType.DMA((n_remote,)),
                        pltpu.SemaphoreType.DMA((2 * len(stages),))] + [pltpu.VMEM(s, d) for s, d in stages],
        compiler_params=pltpu.CompilerParams(vmem_limit_bytes=VMEM_LIMIT),
    )(*ins)


def _gather_to_all(cps, pairs, me, sibling, other_chips, c, base):
    idx = lambda p: 4 * p[0] + 2 * p[1] + p[2]
    for i, (src, dst) in enumerate(pairs):
        cps.local(src, dst.at[idx(me)])
        cps.send(base + 7 * i, src, dst.at[idx(me)], sibling)
        for j, chip in enumerate(other_chips):
            cps.send(base + 7 * i + 1 + j, src, dst.at[idx(me)], (*chip, c))
    cps.flush()
    for j, chip in enumerate(other_chips):
        for i, (_, dst) in enumerate(pairs):
            got = dst.at[idx((*chip, c))]
            cps.arrived(base + 7 * i + 1 + j, got, (*chip, c))
            cps.send(base + 7 * i + 4 + j, got, got, sibling)
    for i, (_, dst) in enumerate(pairs):
        cps.arrived(base + 7 * i, dst.at[idx(sibling)], sibling)
        for j, chip in enumerate(other_chips):
            cps.arrived(base + 7 * i + 4 + j, dst.at[idx((*chip, 1 - c))], sibling)


def _gather_weights(shards, small):
    n = len(shards)

    def body(cps, ins, outs):
        x, y, c = _place()
        me, sibling, mine = (x, y, c), (x, y, 1 - c), 2 * x + y
        chips = [(1 - x, y), (x, 1 - y), (1 - x, 1 - y)]
        for i in range(n):
            cps.local(ins[i], outs[i].at[mine])
        for j, (px, py) in enumerate(chips):
            for i in range(n):
                hr = shards[i].shape[0] // 2
                rows = pl.ds(c * hr, hr)
                cps.send(6 * i + j, ins[i].at[rows], outs[i].at[mine, rows], (px, py, c))
        _gather_to_all(cps, [(ins[n], outs[n])], me, sibling, chips, c, 6 * n)
        for j, (px, py) in enumerate(chips):
            for i in range(n):
                hr = shards[i].shape[0] // 2
                got = outs[i].at[2 * px + py, pl.ds(c * hr, hr)]
                cps.arrived(6 * i + j, got, (px, py, c))
                cps.send(6 * i + 3 + j, got, got, sibling)
        for j, (px, py) in enumerate(chips):
            for i in range(n):
                hr = shards[i].shape[0] // 2
                cps.arrived(6 * i + 3 + j, outs[i].at[2 * px + py, pl.ds((1 - c) * hr, hr)], sibling)
        cps.drain()

    return _comm_call(body, "gather_weights", list(shards) + [small],
                      [_sds((4,) + s.shape, s.dtype) for s in shards] + [_sds((8,) + small.shape, small.dtype)], 6 * n + 7,
                      [(a.shape, a.dtype) for a in list(shards) + [small]])


def _reduce_sibling(parts, dmod_rows):
    n = len(parts)

    def body(cps, ins, outs):
        x, y, c = _place()
        me, sibling = (x, y, c), (x, y, 1 - c)
        chips = [(1 - x, y), (x, 1 - y), (1 - x, 1 - y)]
        for i in range(n):
            cps.send(i, ins[i].at[:, 1 - c], outs[i], sibling)
        _gather_to_all(cps, [(ins[n], outs[n])], me, sibling, chips, c, n)
        for i in range(n):
            cps.arrived(i, outs[i], sibling)
        cps.drain()

    return _comm_call(body, "reduce_sibling", list(parts) + [dmod_rows],
                      [_sds((4,) + p.shape[2:], p.dtype) for p in parts] + [_sds((8,) + dmod_rows.shape, dmod_rows.dtype)], n + 7,
                      [(dmod_rows.shape, dmod_rows.dtype)])


def _reduce_chips(parts):
    n = len(parts)

    def body(cps, ins, outs):
        x, y, c = _place()
        mine = 2 * x + y
        chips = _other_chips(x, y)
        for i in range(n):
            cps.local(ins[i].at[mine], outs[i].at[mine])
        for j, (px, py) in enumerate(chips):
            for i in range(n):
                cps.send(3 * i + j, ins[i].at[2 * px + py], outs[i].at[mine], (px, py, c))
        cps.flush()
        for j, (px, py) in enumerate(chips):
            for i in range(n):
                cps.arrived(3 * i + j, outs[i].at[2 * px + py], (px, py, c))
        cps.drain()

    return _comm_call(body, "reduce_chips", list(parts), [_sds(p.shape, p.dtype) for p in parts], 3 * n,
                      [(p.shape[1:], p.dtype) for p in parts])


def _gather_reduced(shard_parts, repl_parts):
    ns, nr = len(shard_parts), len(repl_parts)

    def body(cps, ins, outs):
        x, y, c = _place()
        me, sibling = (x, y, c), (x, y, 1 - c)
        chips = [(1 - x, y), (x, 1 - y), (1 - x, 1 - y)]
        for i in range(ns):
            cps.local(ins[i], outs[i].at[c])
            cps.send(i, ins[i], outs[i].at[c], sibling)
        _gather_to_all(cps, [(ins[ns + i], outs[ns + i]) for i in range(nr)], me, sibling, chips, c, ns)
        for i in range(ns):
            cps.arrived(i, outs[i].at[1 - c], sibling)
        cps.drain()

    return _comm_call(body, "gather_reduced", list(shard_parts) + list(repl_parts),
                      [_sds((2,) + p.shape, p.dtype) for p in shard_parts] + [_sds((8,) + p.shape, p.dtype) for p in repl_parts],
                      ns + 7 * nr, [(p.shape, p.dtype) for p in list(shard_parts) + list(repl_parts)])


def _sum_sibling(core, parts, got, wire, name):
    n = len(parts)

    def body(core_ref, *refs):
        for i in range(n):
            refs[2 * n + i][0] = (refs[i][0] + refs[n + i][0]).astype(wire[i])

    keep_spec = lambda p: pl.BlockSpec((1, None) + p.shape[2:], lambda s, core_ref: (s, core_ref[0], 0, 0))
    slot_spec = lambda p: pl.BlockSpec((1,) + p.shape[2:], lambda s, core_ref: (s, 0, 0))
    return pl.pallas_call(
        body, name=name,
        grid_spec=pltpu.PrefetchScalarGridSpec(
            num_scalar_prefetch=1, grid=(4,), in_specs=[keep_spec(p) for p in parts] + [slot_spec(p) for p in parts],
            out_specs=[slot_spec(p) for p in parts]),
        out_shape=[_sds((4,) + p.shape[2:], wire[i]) for i, p in enumerate(parts)],
        compiler_params=_params("parallel"),
    )(core, *parts, *got)


def _sum_slots(slots, name):
    n = len(slots)

    def spec_pair(p):
        k, rows, cols = p.shape
        sub = 16 if p.dtype == BF16 else 8
        if (rows // 2) % sub == 0:
            return pl.BlockSpec((k, rows // 2, cols), lambda i: (0, i, 0)), pl.BlockSpec((rows // 2, cols), lambda i: (i, 0))
        return pl.BlockSpec((k, rows, cols), lambda i: (0, 0, 0)), pl.BlockSpec((rows, cols), lambda i: (0, 0))

    pairs = [spec_pair(p) for p in slots]

    def body(*refs):
        for i in range(n):
            acc = refs[i][0].astype(F32)
            for j in range(1, slots[i].shape[0]):
                acc = acc + refs[i][j].astype(F32)
            refs[n + i][...] = acc

    return pl.pallas_call(
        body, name=name, grid=(2,), in_specs=[a for a, _ in pairs], out_specs=[b for _, b in pairs],
        out_shape=[_sds(p.shape[1:]) for p in slots], compiler_params=_params("arbitrary"),
    )(*slots)


def _modulation(c_all, ada_w, ada_b):
    cols = ada_w.shape[2]

    def body(c_ref, w_ref, b_ref, o_ref):
        cc = c_ref[...]
        o_ref[0] = _mm(cc * _sigmoid(cc), w_ref[0]) + b_ref[0]

    return pl.pallas_call(
        body, name="modulation", grid=(2,),
        in_specs=[_full((8, D_MODEL)), pl.BlockSpec((1, D_MODEL, cols), lambda l: (l, 0, 0)), pl.BlockSpec((1, 1, cols), lambda l: (l, 0, 0))],
        out_specs=pl.BlockSpec((1, 8, cols), lambda l: (l, 0, 0)), out_shape=_sds((2, 8, cols)),
        compiler_params=_params("parallel"),
    )(c_all, ada_w, ada_b)


def _adamw_math(w, g, m, v):
    m = ADAM_B1 * m + (1.0 - ADAM_B1) * g
    v = ADAM_B2 * v + (1.0 - ADAM_B2) * (g * g)
    m_hat = m / (1.0 - ADAM_B1 ** ADAM_STEP)
    v_hat = v / (1.0 - ADAM_B2 ** ADAM_STEP)
    delta = -ADAM_LR * (m_hat / (jnp.sqrt(v_hat) + ADAM_EPS) + ADAM_WD * w)
    return delta, m, v


def _ada_update(c_all, dmod, w, m, v, rider=None):
    cols = w.shape[2]
    tr = 256
    per = D_MODEL // tr
    spec3 = pl.BlockSpec((1, tr, cols), lambda i: (i // per, i % per, 0))

    def body(c_ref, d_ref, w_ref, m_ref, v_ref, g_ref, dl_ref, nm_ref, nv_ref):
        cc = c_ref[...]
        g = _mm_tn(cc * _sigmoid(cc), d_ref[0])
        g_ref[0] = g
        dl_ref[0], nm_ref[0], nv_ref[0] = _adamw_math(w_ref[0], g, m_ref[0], v_ref[0])

    return _call(
        body, "ada_update", (2 * per,),
        [pl.BlockSpec((8, tr), lambda i: (0, i % per)), pl.BlockSpec((1, 8, cols), lambda i: (i // per, 0, 0)), spec3, spec3, spec3],
        [spec3] * 4, [_sds(w.shape)] * 4, (c_all, dmod, w, m, v), "parallel", rider=rider)


def _adamw_matrices(params):
    n = len(params)
    steps = 8

    def body(*refs):
        ins, outs = refs[:4 * n], refs[4 * n:]
        for j in range(n):
            w_ref, g_ref, m_ref, v_ref = ins[4 * j:4 * j + 4]
            g = g_ref[...]
            outs[4 * j][...] = g
            outs[4 * j + 1][...], outs[4 * j + 2][...], outs[4 * j + 3][...] = _adamw_math(w_ref[...], g, m_ref[...], v_ref[...])

    spec = lambda p: _rows(p[0].shape[0] // steps, p[0].shape[1])
    res = pl.pallas_call(
        body, name="adamw_matrices", grid=(steps,), in_specs=[spec(p) for p in params for _ in range(4)],
        out_specs=[spec(p) for p in params for _ in range(4)], out_shape=[_sds(p[0].shape) for p in params for _ in range(4)],
        compiler_params=_params("parallel"),
    )(*[a for p in params for a in p])
    return [tuple(res[4 * j:4 * j + 4]) for j in range(n)]


def _adamw_small(params):
    n = len(params)

    def body(*refs):
        ins, outs = refs[:4 * n], refs[4 * n:]
        for j in range(n):
            w_ref, g_ref, m_ref, v_ref = ins[4 * j:4 * j + 4]
            outs[3 * j][...], outs[3 * j + 1][...], outs[3 * j + 2][...] = _adamw_math(w_ref[...], g_ref[...], m_ref[...], v_ref[...])

    flat = [a for p in params for a in p]
    res = pl.pallas_call(body, name="adamw_small", out_shape=[_sds(p[0].shape) for p in params for _ in range(3)])(*flat)
    return [tuple(res[3 * j:3 * j + 3]) for j in range(n)]


def _cols(a, start, size):
    return lax.dynamic_slice_in_dim(a, start, size, axis=a.ndim - 1)


def kernel(x, c, positions, ada_w, ada_b, ln_g, ln_b, ev_w_in, ev_w_out, ev_sink, ev_sg_ln_g, ev_sg_ln_b, ev_sg_w, ev_sg_b, od_w_in, od_conv_w, od_conv_b, od_w_a, od_b_a, od_w_x, od_b_x, od_lam, od_w_out, loss_target, m_ada_w, m_ada_b, m_ln_g, m_ln_b, m_ev_w_in, m_ev_w_out, m_ev_sink, m_ev_sg_ln_g, m_ev_sg_ln_b, m_ev_sg_w, m_ev_sg_b, m_od_w_in, m_od_conv_w, m_od_conv_b, m_od_w_a, m_od_b_a, m_od_w_x, m_od_b_x, m_od_lam, m_od_w_out, v_ada_w, v_ada_b, v_ln_g, v_ln_b, v_ev_w_in, v_ev_w_out, v_ev_sink, v_ev_sg_ln_g, v_ev_sg_ln_b, v_ev_sg_w, v_ev_sg_b, v_od_w_in, v_od_conv_w, v_od_conv_b, v_od_w_a, v_od_b_a, v_od_w_x, v_od_b_x, v_od_lam, v_od_w_out):
    seq = x.shape[1]
    px, py, pc = _place()
    chip = 2 * px + py
    dev = 2 * chip + pc

    small = jnp.concatenate([od_conv_w[0].reshape(-1), od_conv_b[0], od_b_a[0].reshape(-1), jnp.zeros((256,), F32),
                             od_b_x[0].reshape(-1), od_lam[0].reshape(-1)]).reshape(3, D_MODEL)
    blk = jnp.concatenate([c, small, jnp.zeros((4, D_MODEL), F32)], axis=0)
    tr = lambda a: jnp.swapaxes(a, -1, -2)
    wire_w = lambda a: a.astype(MXU_DTYPE)
    ev_w_in4, g_small = _gather_weights([wire_w(tr(ev_w_in[0]))], blk)
    core = pc.astype(jnp.int32).reshape(1)
    ride = {"ev_w_out": wire_w(ev_w_out[0]), "od_w_in": wire_w(od_w_in[0]), "od_w_out": wire_w(od_w_out[0]),
            "core": core, "wire": MXU_DTYPE}
    c_all = g_small[:, 0, :]
    per_chip = g_small[0::2]
    conv_w = per_chip[:, 1].reshape(4, 4, 256).transpose(1, 0, 2).reshape(4, D_MODEL)
    conv_b = per_chip[:, 2, 0:256].reshape(D_MODEL)
    b_a = per_chip[:, 2, 256:768].reshape(4, 2, 256).transpose(1, 0, 2).reshape(2, D_MODEL)
    b_x = per_chip[:, 3, 0:512].reshape(4, 2, 256).transpose(1, 0, 2).reshape(2, D_MODEL)
    lam = per_chip[:, 3, 512:1024].reshape(4, 2, 256).transpose(1, 0, 2).reshape(2, D_MODEL)

    w_full = {
        "ev_w_in_t": ev_w_in4.reshape(EVEN_IN, D_MODEL),
        "ev_sink": ev_sink[0], "ev_sg_ln_g": ev_sg_ln_g[0], "ev_sg_ln_b": ev_sg_ln_b[0], "ev_sg_w": ev_sg_w[0],
        "ev_sg_b": ev_sg_b[0], "od_conv_w": conv_w, "od_conv_b": conv_b, "od_w_a": od_w_a[0], "od_b_a": b_a,
        "od_w_x": od_w_x[0], "od_b_x": b_x, "od_lam": lam, "ln_g": ln_g, "ln_b": ln_b,
    }

    ada_cols = ada_w.shape[2]
    mod_sh = _modulation(c_all, ada_w, _cols(ada_b, chip * ada_cols, ada_cols).reshape(2, 1, ada_cols))
    mod_all = _allgather8(mod_sh.reshape(16, ada_cols), "gather_mod").reshape(4, 2, 2, 8, ada_cols)[:, 0]
    mod_mine = lax.dynamic_index_in_dim(mod_all, dev, axis=2, keepdims=False)
    mod = mod_mine.transpose(1, 0, 2).reshape(2, 3, D_MODEL)

    posf = positions.astype(F32).reshape(seq, 1)
    grad_x, g = _local_step(x[0], posf, loss_target[0], mod, w_full, seq, ride)

    parts = [g["ev_w_in_t"].reshape(4, 2, 352, D_MODEL), g["ev_w_out"].reshape(4, 2, 128, D_MODEL),
             g["ev_sg_w"].reshape(4, 2, BLK, BLK), g["rows"].reshape(4, 2, SMALL_ROWS // 8, D_MODEL)]
    wire = [MXU_DTYPE] * 3 + [F32]
    *got, dmod_gathered = _reduce_sibling(parts, g["dmod_blk"])
    ev_slots = list(_reduce_chips(_sum_sibling(core, parts, got, wire, "sum_sibling")))
    od_slots = list(g["od_slots"])
    mine = _sum_slots(ev_slots[0:2] + od_slots[0:2] + ev_slots[2:3] + od_slots[2:4] + ev_slots[3:4], "sum_chips")
    reduced = _gather_reduced(mine[:4], mine[4:])
    g_ev_w_in_t = reduced[0].reshape(704, D_MODEL)
    g_ev_w_out = reduced[1].reshape(256, D_MODEL)
    g_od_w_in = reduced[2].reshape(D_MODEL, 512)
    g_od_w_out = reduced[3].reshape(256, D_MODEL)
    g_sg_w = reduced[4].reshape(8 * BLK, BLK)
    g_w_a = reduced[5].reshape(16 * BLK, BLK)
    g_w_x = reduced[6].reshape(16 * BLK, BLK)
    gs = reduced[7].reshape(SMALL_ROWS, D_MODEL)
    loss = gs[ROW_LOSS, 0]
    dmod_all = dmod_gathered[:, 0:6].reshape(8, 2, 3 * D_MODEL)
    dmod_sh = _cols(dmod_all, chip * ada_cols, ada_cols).transpose(1, 0, 2)
    (g_ada_w, d_ada_w, nm_ada_w, nv_ada_w), _ = _ada_update(c_all, dmod_sh, ada_w, m_ada_w, v_ada_w)

    mats = (("ev_w_out", ev_w_out, g_ev_w_out, m_ev_w_out, v_ev_w_out), ("od_w_in", od_w_in, g_od_w_in, m_od_w_in, v_od_w_in),
            ("od_w_out", od_w_out, g_od_w_out, m_od_w_out, v_od_w_out), ("ev_sg_w", ev_sg_w, g_sg_w, m_ev_sg_w, v_ev_sg_w),
            ("od_w_a", od_w_a, g_w_a, m_od_w_a, v_od_w_a), ("od_w_x", od_w_x, g_w_x, m_od_w_x, v_od_w_x))
    upd = _adamw_matrices([(tr(ev_w_in[0]), g_ev_w_in_t, tr(m_ev_w_in[0]), tr(v_ev_w_in[0]))]
                          + [(w_.reshape(g_.shape), g_, m_.reshape(g_.shape), v_.reshape(g_.shape)) for _, w_, g_, m_, v_ in mats])
    big = {"ev_w_in": tuple(tr(a).reshape(ev_w_in.shape) for a in upd[0])}
    for (name, w_, _, _, _), u in zip(mats, upd[1:]):
        big[name] = tuple(a.reshape(w_.shape) for a in u)
    big["ada_w"] = (g_ada_w, d_ada_w, nm_ada_w, nv_ada_w)

    sh = lambda a: _cols(a, chip * 256, 256)
    small_g = {
        "ada_b": gs[ROW_DMOD:ROW_DMOD + 6].reshape(2, 3 * D_MODEL),
        "ln_g": jnp.stack([gs[ROW_LN], gs[ROW_LN + 2]]), "ln_b": jnp.stack([gs[ROW_LN + 1], gs[ROW_LN + 3]]),
        "ev_sink": gs[ROW_SINK:ROW_SINK + 1, 0:N_Q_HEADS], "ev_sg_ln_g": gs[ROW_SG_LN:ROW_SG_LN + 1, 0:SG_WIDTH],
        "ev_sg_ln_b": gs[ROW_SG_LN:ROW_SG_LN + 1, SG_WIDTH:2 * SG_WIDTH], "ev_sg_b": gs[ROW_SG_B].reshape(N_SG_GROUPS, BLK),
        "od_conv_w": sh(gs[ROW_CONV_W:ROW_CONV_W + 4]), "od_conv_b": sh(gs[ROW_CONV_B:ROW_CONV_B + 1]),
        "od_b_a": sh(gs[ROW_B_A:ROW_B_A + 2]), "od_b_x": sh(gs[ROW_B_X:ROW_B_X + 2]), "od_lam": sh(gs[ROW_LAM:ROW_LAM + 2]),
    }
    small_in = {"ada_b": (ada_b, m_ada_b, v_ada_b), "ln_g": (ln_g, m_ln_g, v_ln_g), "ln_b": (ln_b, m_ln_b, v_ln_b),
                "ev_sink": (ev_sink, m_ev_sink, v_ev_sink), "ev_sg_ln_g": (ev_sg_ln_g, m_ev_sg_ln_g, v_ev_sg_ln_g),
                "ev_sg_ln_b": (ev_sg_ln_b, m_ev_sg_ln_b, v_ev_sg_ln_b), "ev_sg_b": (ev_sg_b, m_ev_sg_b, v_ev_sg_b),
                "od_conv_w": (od_conv_w, m_od_conv_w, v_od_conv_w), "od_conv_b": (od_conv_b, m_od_conv_b, v_od_conv_b),
                "od_b_a": (od_b_a, m_od_b_a, v_od_b_a), "od_b_x": (od_b_x, m_od_b_x, v_od_b_x),
                "od_lam": (od_lam, m_od_lam, v_od_lam)}
    names_small = list(small_g)
    upd = _adamw_small([(small_in[n][0].reshape(small_g[n].shape), small_g[n], small_in[n][1].reshape(small_g[n].shape),
                         small_in[n][2].reshape(small_g[n].shape)) for n in names_small])
    res = dict(big)
    for n, (d_, nm_, nv_) in zip(names_small, upd):
        shape = small_in[n][0].shape
        res[n] = tuple(a.reshape(shape) for a in (small_g[n], d_, nm_, nv_))

    order = ["ada_w", "ada_b", "ln_g", "ln_b", "ev_w_in", "ev_w_out", "ev_sink", "ev_sg_ln_g", "ev_sg_ln_b", "ev_sg_w", "ev_sg_b",
             "od_w_in", "od_conv_w", "od_conv_b", "od_w_a", "od_b_a", "od_w_x", "od_b_x", "od_lam", "od_w_out"]
    return (loss, grad_x.reshape(x.shape), *[res[n][0] for n in order], *[res[n][1] for n in order],
            *[res[n][2] for n in order], *[res[n][3] for n in order])
```

```python
import jax
import jax.numpy as jnp
import numpy as np
from jax import lax
from jax.experimental import pallas as pl
from jax.experimental.pallas import tpu as pltpu

F32 = jnp.float32
BF16 = jnp.bfloat16
MXU_DTYPE = BF16
ACT_DTYPE = MXU_DTYPE

D_MODEL = 1024
HEAD_DIM = 64
N_Q_HEADS = 8
Q_PER_KV = 4
ATTN_WIDTH = 512
BLK = 128
ROPE_DIM = 16
ROPE_THETA = 500000.0
N_SG_GROUPS = 8
SG_WIDTH = 512
EVEN_IN = 2816
ODD_IN = 2048
RNN_HEADS = 8
RG_LRU_C = 8.0
ALPHA = (2 * 2) ** 0.25
LN_EPS = 1e-5
NEG_INF = -1e30
ADAM_LR, ADAM_B1, ADAM_B2, ADAM_EPS, ADAM_WD, ADAM_STEP = 0.001, 0.9, 0.999, 1e-08, 0.01, 10

LANES = 128
VMEM_LIMIT = 56 * 1024 * 1024
MESH = pl.DeviceIdType.MESH


def _mm(a, b):
    return jnp.dot(a.astype(MXU_DTYPE), b.astype(MXU_DTYPE), preferred_element_type=F32)


def _mm_nt(a, b):
    return lax.dot_general(a.astype(MXU_DTYPE), b.astype(MXU_DTYPE), (((1,), (1,)), ((), ())), preferred_element_type=F32)


def _mm_tn(a, b):
    return lax.dot_general(a.astype(MXU_DTYPE), b.astype(MXU_DTYPE), (((0,), (0,)), ((), ())), preferred_element_type=F32)


def _sigmoid(x):
    return 1.0 / (1.0 + jnp.exp(-x))


def _ln_stats(z):
    mu = jnp.mean(z, axis=-1, keepdims=True)
    d = z - mu
    var = jnp.mean(d * d, axis=-1, keepdims=True)
    rstd = lax.rsqrt(var + LN_EPS)
    return d * rstd, rstd


def _ln_bwd(dout, zhat, rstd, g):
    dzh = dout * g
    m1 = jnp.mean(dzh, axis=-1, keepdims=True)
    m2 = jnp.mean(dzh * zhat, axis=-1, keepdims=True)
    return rstd * (dzh - m1 - zhat * m2)


def _group_sum(x, e2):
    hi = x.astype(MXU_DTYPE)
    lo = (x - hi.astype(F32)).astype(MXU_DTYPE)
    return jnp.dot(hi, e2, preferred_element_type=F32) + jnp.dot(lo, e2, preferred_element_type=F32)


def _lane_iota(shape):
    return lax.broadcasted_iota(jnp.int32, shape, 1)


def _to_kv_lanes(t, h):
    src_lo = (h % 2 == 0)
    dst_lo = (h // Q_PER_KV == 0)
    if src_lo != dst_lo:
        t = pltpu.roll(t, HEAD_DIM, 1)
    lane = _lane_iota(t.shape)
    keep = (lane < HEAD_DIM) if dst_lo else (lane >= HEAD_DIM)
    return jnp.where(keep, t, 0.0)


def _from_kv_lanes(t, h):
    src_lo = (h // Q_PER_KV == 0)
    dst_lo = (h % 2 == 0)
    lane = _lane_iota(t.shape)
    keep = (lane < HEAD_DIM) if src_lo else (lane >= HEAD_DIM)
    t = jnp.where(keep, t, 0.0)
    if src_lo != dst_lo:
        t = pltpu.roll(t, HEAD_DIM, 1)
    return t


def _rope(t, cos_t, sin_p, sin_m):
    half = ROPE_DIM // 2
    return t * cos_t + pltpu.roll(t, half, 1) * sin_p + pltpu.roll(t, LANES - half, 1) * sin_m


def _rope_t(d, cos_t, sin_p, sin_m):
    half = ROPE_DIM // 2
    return d * cos_t + pltpu.roll(d * sin_p, LANES - half, 1) + pltpu.roll(d * sin_m, half, 1)


def _band(ref, n, nb):
    prev = jnp.maximum(n - 1, 0)
    nxt = jnp.minimum(n + 1, nb - 1)
    rows = [ref[pl.ds(pl.multiple_of(j * BLK, BLK), BLK), :] for j in (prev, n, nxt)]
    return jnp.concatenate(rows, axis=0)


def _band_bias(n, seq):
    qi = lax.broadcasted_iota(jnp.int32, (BLK, 3 * BLK), 0)
    kj = lax.broadcasted_iota(jnp.int32, (BLK, 3 * BLK), 1)
    k_abs = n * BLK - BLK + kj
    valid = (jnp.abs(kj - BLK - qi) <= BLK) & (k_abs >= 0) & (k_abs < seq)
    bias = jnp.where(valid, 0.0, NEG_INF)
    return jnp.concatenate([bias] * Q_PER_KV, axis=0)


def _stack_heads(tile_of, kv):
    return jnp.concatenate([_to_kv_lanes(tile_of(h // 2), h) for h in range(Q_PER_KV * kv, Q_PER_KV * (kv + 1))], axis=0)


def _per_head_column(vals):
    row = lax.broadcasted_iota(jnp.int32, (Q_PER_KV * BLK, 1), 0)
    return jnp.where(row < BLK, vals[0], jnp.where(row < 2 * BLK, vals[1], jnp.where(row < 3 * BLK, vals[2], vals[3])))


def _softplus_neg(lam):
    e = jnp.exp(-jnp.abs(lam))
    u = 1.0 + e
    log1p_e = jnp.where(u == 1.0, e, jnp.log(u) * (e / (u - 1.0)))
    sp = jnp.maximum(-lam, 0.0) + log1p_e
    dsp = -1.0 / (1.0 + jnp.exp(lam))
    return sp, dsp


def _full(shape):
    return pl.BlockSpec(shape, lambda *_: (0,) * len(shape))


def _const(shape):
    return pl.BlockSpec(shape, lambda *_: (0,) * len(shape), pipeline_mode=pl.Buffered(1))


def _rows(tm, n):
    return pl.BlockSpec((tm, n), lambda i: (i, 0))


def _params(*sem):
    return pltpu.CompilerParams(dimension_semantics=sem, vmem_limit_bytes=VMEM_LIMIT)


def _sds(shape, dtype=F32):
    return jax.ShapeDtypeStruct(shape, dtype)


def _place():
    return lax.axis_index("x"), lax.axis_index("y"), lax.axis_index("c")


class _Rider:
    def __init__(self, ins, out_shapes, n_remote, n_local, plan):
        self.ins, self.out_shapes, self.n_remote, self.n_local, self.plan = list(ins), list(out_shapes), n_remote, n_local, plan

    def scratch(self):
        return [pltpu.SemaphoreType.DMA((self.n_remote,)), pltpu.SemaphoreType.DMA((self.n_remote,)),
                pltpu.SemaphoreType.DMA((max(self.n_local, 1),))]

    def run(self, first, in_refs, out_refs, sems):
        send_sems, recv_sems, local_sems = sems
        sends, recvs, locals_ = self.plan(in_refs, out_refs)
        remote = lambda k, src, dst, to: pltpu.make_async_remote_copy(
            src_ref=src, dst_ref=dst, send_sem=send_sems.at[k], recv_sem=recv_sems.at[k], device_id=to, device_id_type=MESH)
        if first:
            for k, src, dst, to in sends:
                remote(k, src, dst, to).start()
            for j, (src, dst) in enumerate(locals_):
                pltpu.make_async_copy(src, dst, local_sems.at[j]).start()
        else:
            for k, dst, frm in recvs:
                remote(k, dst, dst, frm).wait_recv()
            for k, src, dst, to in sends:
                remote(k, src, dst, to).wait_send()
            for j, (src, dst) in enumerate(locals_):
                pltpu.make_async_copy(src, dst, local_sems.at[j]).wait()


def _other_chips(x, y):
    return [(1 - x, y), (x, 1 - y), (1 - x, 1 - y)]


def _gather_rider(shard):
    hr = shard.shape[0] // 2

    def plan(ins, outs):
        x, y, c = _place()
        mine, src, dst = 2 * x + y, ins[0], outs[0]
        sends, recvs = [], []
        for j, (px, py) in enumerate(_other_chips(x, y)):
            for flip in range(2):
                tc = c if flip == 0 else 1 - c
                sends.append((2 * j + flip, src.at[pl.ds(c * hr, hr)], dst.at[mine, pl.ds(c * hr, hr)], (px, py, tc)))
                recvs.append((2 * j + flip, dst.at[2 * px + py, pl.ds(tc * hr, hr)], (px, py, tc)))
        return sends, recvs, [(src, dst.at[mine])]

    return _Rider([shard], [_sds((4,) + shard.shape, shard.dtype)], 6, 1, plan)


def _sibling_swap_rider(parts):
    n = len(parts)

    def plan(ins, outs):
        x, y, c = _place()
        sibling = (x, y, 1 - c)
        return ([(i, ins[i].at[:, 1 - c], outs[i], sibling) for i in range(n)], [(i, outs[i], sibling) for i in range(n)], [])

    return _Rider(parts, [_sds((4,) + p.shape[2:], p.dtype) for p in parts], n, 0, plan)


def _chip_exchange_rider(parts):
    n = len(parts)

    def plan(ins, outs):
        x, y, c = _place()
        mine = 2 * x + y
        sends, recvs = [], []
        for i in range(n):
            for j, (px, py) in enumerate(_other_chips(x, y)):
                sends.append((3 * i + j, ins[i].at[2 * px + py], outs[i].at[mine], (px, py, c)))
                recvs.append((3 * i + j, outs[i].at[2 * px + py], (px, py, c)))
        return sends, recvs, [(ins[i].at[mine], outs[i].at[mine]) for i in range(n)]

    return _Rider(parts, [_sds(p.shape, p.dtype) for p in parts], 3 * n, n, plan)


def _call(body, name, grid, in_specs, out_specs, out_shape, args, sem, scratch=(), rider=None):
    if rider is None:
        return list(pl.pallas_call(body, name=name, grid=grid, in_specs=in_specs, out_specs=out_specs, out_shape=out_shape,
                                   scratch_shapes=list(scratch), compiler_params=_params(sem))(*args)), []
    n_in, n_out, n_scr = len(in_specs), len(out_specs), len(scratch)
    r_in, r_out = len(rider.ins), len(rider.out_shapes)
    steps = grid[0]

    def riding(*refs):
        ins, r_ins = refs[:n_in], refs[n_in:n_in + r_in]
        outs = refs[n_in + r_in:n_in + r_in + n_out]
        r_outs = refs[n_in + r_in + n_out:n_in + r_in + n_out + r_out]
        scr = refs[n_in + r_in + n_out + r_out:n_in + r_in + n_out + r_out + n_scr]
        sems = refs[n_in + r_in + n_out + r_out + n_scr:]

        @pl.when(pl.program_id(0) == 0)
        def _():
            rider.run(True, r_ins, r_outs, sems)

        body(*ins, *outs, *scr)

        @pl.when(pl.program_id(0) == steps - 1)
        def _():
            rider.run(False, r_ins, r_outs, sems)

    hbm = pl.BlockSpec(memory_space=pl.ANY)
    res = pl.pallas_call(
        riding, name=name, grid=grid, in_specs=list(in_specs) + [hbm] * r_in, out_specs=list(out_specs) + [hbm] * r_out,
        out_shape=list(out_shape) + rider.out_shapes, scratch_shapes=list(scratch) + rider.scratch(),
        compiler_params=_params("arbitrary"),
    )(*args, *rider.ins)
    return list(res[:n_out]), list(res[n_out:])


def _row_tile(seq, want):
    return want if seq % want == 0 else seq


def _rope_tables(posf, seq):
    half = ROPE_DIM // 2
    inv_freq = np.power(np.float32(ROPE_THETA), -np.arange(half, dtype=np.float32) / np.float32(half)).astype(np.float32)
    j = np.arange(LANES) % HEAD_DIM
    invf = jnp.asarray(np.where(j < ROPE_DIM, inv_freq[j % half], 0.0).astype(np.float32).reshape(1, LANES))
    m_p = jnp.asarray(((j >= half) & (j < ROPE_DIM)).astype(np.float32).reshape(1, LANES))
    m_m = jnp.asarray(-(j < half).astype(np.float32).reshape(1, LANES))
    tm = _row_tile(seq, 512)

    def body(pos_ref, invf_ref, mp_ref, mm_ref, cos_ref, sp_ref, sm_ref):
        ang = pos_ref[...] * invf_ref[...]
        s = jnp.sin(ang)
        cos_ref[...] = jnp.cos(ang)
        sp_ref[...] = s * mp_ref[...]
        sm_ref[...] = s * mm_ref[...]

    return pl.pallas_call(
        body, name="rope_tables", grid=(seq // tm,),
        in_specs=[_rows(tm, 1), _full((1, LANES)), _full((1, LANES)), _full((1, LANES))],
        out_specs=[_rows(tm, LANES)] * 3, out_shape=[_sds((seq, LANES))] * 3,
        compiler_params=_params("parallel"),
    )(posf, invf, m_p, m_m)


def _even_proj(x, mod, w_in_t, tabs, seq, rider=None):
    tm = _row_tile(seq, 512)

    def body(x_ref, mod_ref, w_ref, cos_ref, sp_ref, sm_ref, q_ref, k_ref, v_ref, su_ref, sv_ref, g_ref):
        h = x_ref[...] * (1.0 + mod_ref[1:2, :]) + mod_ref[0:1, :]
        p = _mm_nt(h, w_ref[...])
        cos_t, sin_p, sin_m = cos_ref[...], sp_ref[...], sm_ref[...]
        for j in range(ATTN_WIDTH // LANES):
            q_ref[:, j * LANES:(j + 1) * LANES] = _rope(p[:, j * LANES:(j + 1) * LANES], cos_t, sin_p, sin_m).astype(q_ref.dtype)
        k_ref[...] = _rope(p[:, 512:640], cos_t, sin_p, sin_m).astype(k_ref.dtype)
        v_ref[...] = p[:, 640:768].astype(v_ref.dtype)
        su_ref[...] = p[:, 768:1280].astype(su_ref.dtype)
        sv_ref[...] = p[:, 1280:1792].astype(sv_ref.dtype)
        g_ref[...] = p[:, 1792:2816].astype(g_ref.dtype)

    return _call(
        body, "even_proj", (seq // tm,),
        [_rows(tm, D_MODEL), _full((3, D_MODEL)), _const((EVEN_IN, D_MODEL))] + [_rows(tm, LANES)] * 3,
        [_rows(tm, 512), _rows(tm, LANES), _rows(tm, LANES), _rows(tm, 512), _rows(tm, 512), _rows(tm, D_MODEL)],
        [_sds((seq, 512), MXU_DTYPE), _sds((seq, LANES), MXU_DTYPE), _sds((seq, LANES), MXU_DTYPE), _sds((seq, 512), ACT_DTYPE),
         _sds((seq, 512), ACT_DTYPE), _sds((seq, D_MODEL), ACT_DTYPE)],
        (x, mod, w_in_t, *tabs), "parallel", rider=rider)


def _sg_forward(sv, lng, lnb, sgw_ref, sgb, e2):
    vn, vhat, rstd, svo = [], [], [], []
    for j in range(SG_WIDTH // LANES):
        t = sv[:, j * LANES:(j + 1) * LANES]
        mu = _group_sum(t, e2) * (1.0 / HEAD_DIM)
        d = t - mu
        var = _group_sum(d * d, e2) * (1.0 / HEAD_DIM)
        r = lax.rsqrt(var + LN_EPS)
        vh = d * r
        vhat.append(vh)
        rstd.append(r)
        vn.append(vh * lng[:, j * LANES:(j + 1) * LANES] + lnb[:, j * LANES:(j + 1) * LANES])
    lane = _lane_iota((BLK, LANES))
    for j in range(SG_WIDTH // LANES):
        lo = _mm(sgw_ref[2 * j], vn[j])
        hi = _mm(sgw_ref[2 * j + 1], vn[j])
        svo.append(jnp.where(lane < HEAD_DIM, lo, hi) + sgb[:, j * LANES:(j + 1) * LANES])
    return svo, vn, vhat, rstd


def _even_mix(q, k, v, su, sv, sink, sgln_g, sgln_b, sgw, sgb_full, e2, seq, rider=None):
    nb = seq // BLK

    def body(sink_ref, q_ref, k_ref, v_ref, su_ref, sv_ref, lng_ref, lnb_ref, sgw_ref, sgb_ref, e2_ref, ycat_ref, lse_ref,
             svo_ref, vhat_ref, rstd_ref):
        n = pl.program_id(0)
        kband = _band(k_ref, n, nb)
        vband = _band(v_ref, n, nb)
        bias = _band_bias(n, seq)
        lane = _lane_iota((BLK, LANES))
        lse = jnp.zeros((BLK, LANES), F32)
        q_tile = lambda j: q_ref[:, j * LANES:(j + 1) * LANES].astype(F32)
        acc = [jnp.zeros((BLK, LANES), F32) for _ in range(ATTN_WIDTH // LANES)]
        for kv in range(N_Q_HEADS // Q_PER_KV):
            heads = range(Q_PER_KV * kv, Q_PER_KV * (kv + 1))
            sink = _per_head_column([sink_ref[h] for h in heads])
            s = _mm_nt(_stack_heads(q_tile, kv), kband) * (HEAD_DIM ** -0.5) + bias
            m = jnp.maximum(jnp.max(s, axis=1, keepdims=True), sink)
            p = jnp.exp(s - m)
            denom = jnp.sum(p, axis=1, keepdims=True) + jnp.exp(sink - m)
            o4 = _mm(p / denom, vband)
            l4 = m + jnp.log(denom)
            for g, h in enumerate(heads):
                acc[h // 2] = acc[h // 2] + _from_kv_lanes(o4[g * BLK:(g + 1) * BLK], h)
                lse = jnp.where(lane == h, l4[g * BLK:(g + 1) * BLK], lse)
        for j in range(ATTN_WIDTH // LANES):
            ycat_ref[:, j * LANES:(j + 1) * LANES] = acc[j].astype(ycat_ref.dtype)
        lse_ref[...] = lse
        svo, _, vhat, rstd = _sg_forward(sv_ref[...].astype(F32), lng_ref[...], lnb_ref[...], sgw_ref, sgb_ref[...], e2_ref[...])
        for j in range(SG_WIDTH // LANES):
            cs = slice(j * LANES, (j + 1) * LANES)
            ysg = su_ref[:, cs].astype(F32) * svo[j]
            ycat_ref[:, ATTN_WIDTH + j * LANES:ATTN_WIDTH + (j + 1) * LANES] = ysg.astype(ycat_ref.dtype)
            svo_ref[:, cs], vhat_ref[:, cs], rstd_ref[:, cs] = (t.astype(svo_ref.dtype) for t in (svo[j], vhat[j], rstd[j]))

    blk = lambda w: pl.BlockSpec((BLK, w), lambda n: (n, 0))
    return _call(
        body, "even_mix", (nb,),
        [pl.BlockSpec(memory_space=pltpu.SMEM), blk(512), _full((seq, LANES)), _full((seq, LANES)), blk(512), blk(512),
         _full((1, 512)), _full((1, 512)), _full((8, BLK, BLK)), _full((BLK, 512)), _full((LANES, LANES))],
        [blk(D_MODEL), blk(LANES)] + [blk(SG_WIDTH)] * 3,
        [_sds((seq, D_MODEL), ACT_DTYPE), _sds((seq, LANES))] + [_sds((seq, SG_WIDTH), ACT_DTYPE)] * 3,
        (sink, q, k, v, su, sv, sgln_g, sgln_b, sgw, sgb_full, e2), "parallel", rider=rider)


def _even_out(ycat, g, x, mod, mod_next, w_out, w_in4_next, ln_g, ln_b, seq, rider=None):
    tm = _row_tile(seq, 512)
    cs = ODD_IN // 4

    def body(y_ref, g_ref, x_ref, mod_ref, modn_ref, wo_ref, wi_ref, g1_ref, b1_ref, zhat_ref, rstd_ref, x1_ref, xr_ref, gn_ref):
        gg = g_ref[...].astype(F32)
        out = _mm(y_ref[...].astype(F32) * (gg * _sigmoid(gg)), wo_ref[...])
        z = ALPHA * x_ref[...] + mod_ref[2:3, :] * out
        zhat, rstd = _ln_stats(z)
        zhat_ref[...] = zhat
        rstd_ref[...] = rstd
        x1 = zhat * g1_ref[...] + b1_ref[...]
        x1_ref[...] = x1
        hb = (x1 * (1.0 + modn_ref[1:2, :]) + modn_ref[0:1, :]).astype(MXU_DTYPE)
        for s in range(2):
            xr_ref[:, s * cs:(s + 1) * cs] = jnp.dot(hb, wi_ref[s], preferred_element_type=F32)
            gn_ref[:, s * cs:(s + 1) * cs] = jnp.dot(hb, wi_ref[2 + s], preferred_element_type=F32).astype(gn_ref.dtype)

    return _call(
        body, "even_out", (seq // tm,),
        [_rows(tm, D_MODEL)] * 3 + [_full((3, D_MODEL)), _full((3, D_MODEL)), _const((D_MODEL, D_MODEL)), _const((4, D_MODEL, cs)),
                                    _full((1, D_MODEL)), _full((1, D_MODEL))],
        [_rows(tm, D_MODEL), _rows(tm, 1)] + [_rows(tm, D_MODEL)] * 3,
        [_sds((seq, D_MODEL)), _sds((seq, 1))] + [_sds((seq, D_MODEL))] * 2 + [_sds((seq, D_MODEL), ACT_DTYPE)],
        (ycat, g, x, mod, mod_next, w_out, w_in4_next, ln_g, ln_b), "parallel", rider=rider)


def _halo_specs(tm, seq, width, order=lambda i: i):
    per = tm // 8
    last = seq // 8 - 1
    return [pl.BlockSpec((8, width), lambda i: (jnp.maximum(order(i) * per - 1, 0), 0)),
            pl.BlockSpec((tm, width), lambda i: (order(i), 0)),
            pl.BlockSpec((8, width), lambda i: (jnp.minimum((order(i) + 1) * per, last), 0))]


def _extended(prev_ref, main_ref, next_ref, i, n_steps):
    prev = jnp.where(i > 0, prev_ref[...], 0.0)
    nxt = jnp.where(i < n_steps - 1, next_ref[...], 0.0)
    return jnp.concatenate([prev, main_ref[...], nxt], axis=0)


def _shifted(ext, off, tm):
    if off == 0:
        return ext[8:8 + tm]
    return pltpu.roll(ext, (-off) % ext.shape[0], 0)[8:8 + tm]


SCAN_SUB = 8


def _lru_gate(xh, pre, bias, sp, hs, d):
    r = _sigmoid(pre[:, 0:LANES] + bias[2 * d:2 * d + 1, hs])
    ig = _sigmoid(pre[:, LANES:2 * LANES] + bias[2 * d + 1:2 * d + 2, hs])
    neg_log_a = RG_LRU_C * r * sp[d:d + 1, hs]
    a = jnp.exp(-neg_log_a)
    u = jnp.tanh(neg_log_a) * (a * a + 1.0)
    inv_s = lax.rsqrt(jnp.maximum(u, jnp.finfo(F32).tiny))
    return r, ig, a, u * inv_s, inv_s


def _conv_block(xp_ref, xm_ref, xn_ref, cw_ref, cb_ref, blk, steps, tm):
    ext = _extended(xp_ref, xm_ref, xn_ref, blk, steps)
    return cb_ref[...] + sum(cw_ref[kk:kk + 1, :] * _shifted(ext, kk - 2, tm) for kk in range(4))


def _scan_tiles(a_ref, b_ref, h_ref, hprev_ref, carry_h, carry_a, rows, descending, post):
    sub = SCAN_SUB
    tiles = rows // sub
    row = lax.broadcasted_iota(jnp.int32, (sub, D_MODEL), 0)

    def shift(v, d, fill):
        if descending:
            return jnp.where(row <= sub - 1 - d, pltpu.roll(v, sub - d, 0), fill)
        return jnp.where(row >= d, pltpu.roll(v, d, 0), fill)

    def last(v):
        return jnp.broadcast_to(v[0:1, :] if descending else v[sub - 1:sub, :], v.shape)

    def tile(j, c):
        ch, ca = c
        r0 = pl.multiple_of(((tiles - 1 - j) if descending else j) * sub, sub)
        at = a_ref[pl.ds(r0, sub), :]
        bt = b_ref[pl.ds(r0, sub), :]
        coef = shift(at, 1, ca) if post else at
        acc_a, acc_b = coef, bt
        for d in (1, 2, 4):
            acc_b = acc_b + acc_a * shift(acc_b, d, 0.0)
            acc_a = acc_a * shift(acc_a, d, 1.0)
        h = acc_b + acc_a * ch
        h_ref[pl.ds(r0, sub), :] = h
        if post:
            return last(h), last(at)
        hprev_ref[pl.ds(r0, sub), :] = shift(h, 1, ch)
        return last(h), ca

    ch, ca = lax.fori_loop(0, tiles, tile, (carry_h[...], carry_a[...]), unroll=4)
    carry_h[...] = ch
    carry_a[...] = ca


def _lru_fwd(xr, xc, conv_w, conv_b, wcat, bias, lam, seq, d):
    tb = _row_tile(seq, 512)
    steps = seq // tb
    descending = d == 1
    order = (lambda i: steps - 1 - i) if descending else (lambda i: i)
    with_conv = xc is None
    n_x = 5 if with_conv else 1

    def body(*refs):
        x_refs, (w_ref, bias_ref, lam_ref) = refs[:n_x], refs[n_x:n_x + 3]
        h_ref, hp_ref, a_ref, r_ref, i_ref, s_ref, q_ref = refs[n_x + 3:n_x + 10]
        b_scr, carry_h, carry_a = refs[-3:]
        i = pl.program_id(0)

        @pl.when(i == 0)
        def _():
            carry_h[...] = jnp.zeros_like(carry_h)
            carry_a[...] = jnp.zeros_like(carry_a)

        if with_conv:
            xc_ref = refs[n_x + 10]
            xc_ref[...] = _conv_block(*x_refs, order(i), steps, tb)
        else:
            xc_ref = x_refs[0]
        sp, _ = _softplus_neg(lam_ref[...])
        bias = bias_ref[...]
        for h in range(RNN_HEADS):
            hs = slice(h * LANES, (h + 1) * LANES)
            xh = xc_ref[:, hs]
            r, ig, a, s, q = _lru_gate(xh, _mm(xh, w_ref[h, :, 2 * d * LANES:2 * (d + 1) * LANES]), bias, sp, hs, d)
            a_ref[:, hs] = a
            b_scr[:, hs] = s * ig * xh
            for ref, val in ((r_ref, r), (i_ref, ig), (s_ref, s), (q_ref, q)):
                ref[:, hs] = val.astype(ref.dtype)
        _scan_tiles(a_ref, b_scr, h_ref, hp_ref, carry_h, carry_a, tb, descending, post=False)

    row_spec = pl.BlockSpec((tb, D_MODEL), lambda i: (order(i), 0))
    if with_conv:
        x_specs, x_args = _halo_specs(tb, seq, D_MODEL, order) + [_full((4, D_MODEL)), _full((1, D_MODEL))], (xr, xr, xr, conv_w, conv_b)
    else:
        x_specs, x_args = [row_spec], (xc,)
    n_out = 8 if with_conv else 7
    return pl.pallas_call(
        body, name="lru_fwd_%d" % d, grid=(steps,),
        in_specs=x_specs + [_full((8, LANES, 512)), _full((4, D_MODEL)), _full((2, D_MODEL))],
        out_specs=[row_spec] * n_out,
        out_shape=[_sds((seq, D_MODEL))] * 3 + [_sds((seq, D_MODEL), ACT_DTYPE)] * 4 + [_sds((seq, D_MODEL))] * (n_out - 7),
        scratch_shapes=[pltpu.VMEM((tb, D_MODEL), F32)] + [pltpu.VMEM((SCAN_SUB, D_MODEL), F32)] * 2,
        compiler_params=_params("arbitrary"),
    )(*x_args, wcat, bias, lam)


def _odd_out_and_loss(hf, hr, g, x1, tgt, mod, w_out, ln_g, ln_b, seq):
    tm = _row_tile(seq, 512)

    def body(hf_ref, hr_ref, g_ref, x_ref, t_ref, mod_ref, w_ref, lg_ref, lb_ref,
             dhs_ref, dg_ref, dres_ref, loss_ref, dw_ref, vec_ref):
        @pl.when(pl.program_id(0) == 0)
        def _():
            loss_ref[...] = jnp.zeros_like(loss_ref)
            dw_ref[...] = jnp.zeros_like(dw_ref)
            vec_ref[...] = jnp.zeros_like(vec_ref)

        gg = g_ref[...].astype(F32)
        sg = _sigmoid(gg)
        silu = gg * sg
        hsum = hf_ref[...] + hr_ref[...]
        y = hsum * silu
        out = _mm(y, w_ref[...])
        gate = mod_ref[2:3, :]
        z = ALPHA * x_ref[...] + gate * out
        zhat, rstd = _ln_stats(z)
        x2 = zhat * lg_ref[...] + lb_ref[...]
        err = x2 - t_ref[...]
        loss_ref[...] += 0.5 * jnp.sum(jnp.mean(err * err, axis=-1, keepdims=True))
        dx2 = err * (1.0 / D_MODEL)
        dz = _ln_bwd(dx2, zhat, rstd, lg_ref[...])
        vec_ref[0:1, :] += jnp.sum(dx2 * zhat, axis=0, keepdims=True)
        vec_ref[1:2, :] += jnp.sum(dx2, axis=0, keepdims=True)
        vec_ref[2:3, :] += jnp.sum(dz * out, axis=0, keepdims=True)
        dres_ref[...] = ALPHA * dz
        dout = gate * dz
        dw_ref[...] += _mm_tn(y, dout)
        dy = _mm_nt(dout, w_ref[...])
        dhs_ref[...] = dy * silu
        dg_ref[...] = (dy * hsum * (sg * (1.0 + gg * (1.0 - sg)))).astype(dg_ref.dtype)

    return pl.pallas_call(
        body, name="odd_out_loss", grid=(seq // tm,),
        in_specs=[_rows(tm, D_MODEL)] * 5 + [_full((3, D_MODEL)), _const((D_MODEL, D_MODEL)),
                                             _full((1, D_MODEL)), _full((1, D_MODEL))],
        out_specs=[_rows(tm, D_MODEL)] * 3 + [_full((8, LANES)), _full((D_MODEL, D_MODEL)), _full((8, D_MODEL))],
        out_shape=[_sds((seq, D_MODEL)), _sds((seq, D_MODEL), ACT_DTYPE), _sds((seq, D_MODEL)), _sds((8, LANES)),
                   _sds((D_MODEL, D_MODEL)), _sds((8, D_MODEL))],
        compiler_params=_params("arbitrary"),
    )(hf, hr, g, x1, tgt, mod, w_out, ln_g, ln_b)


def _lru_bwd(xc, dhs, hprev, a_d, r_d, i_d, s_d, q_d, wcat, lam, seq, d):
    tb = _row_tile(seq, 512)
    steps = seq // tb
    descending = d == 0
    order = (lambda i: steps - 1 - i) if descending else (lambda i: i)
    cols = slice(2 * d * LANES, 2 * (d + 1) * LANES)

    def body(xc_ref, dhs_ref, hp_ref, a_ref, r_ref, i_ref, s_ref, q_ref, w_ref, lam_ref, dxc_ref, dw_ref, vec_ref,
             g_scr, carry_h, carry_a):
        i = pl.program_id(0)

        @pl.when(i == 0)
        def _():
            dw_ref[...] = jnp.zeros_like(dw_ref)
            vec_ref[...] = jnp.zeros_like(vec_ref)
            carry_h[...] = jnp.zeros_like(carry_h)
            carry_a[...] = jnp.zeros_like(carry_a)

        sp, dsp = _softplus_neg(lam_ref[...])
        _scan_tiles(a_ref, dhs_ref, g_scr, None, carry_h, carry_a, tb, descending, post=True)
        for h in range(RNN_HEADS):
            hs = slice(h * LANES, (h + 1) * LANES)
            xh, a = xc_ref[:, hs], a_ref[:, hs]
            r, ig, s = r_ref[:, hs].astype(F32), i_ref[:, hs].astype(F32), s_ref[:, hs].astype(F32)
            db = g_scr[:, hs]
            da = db * hp_ref[:, hs]
            dlog_a = da * a - (db * ig * xh) * (a * a * q_ref[:, hs].astype(F32))
            dpr = dlog_a * (-RG_LRU_C) * sp[d:d + 1, hs] * r * (1.0 - r)
            dpi = db * s * xh * ig * (1.0 - ig)
            vec_ref[0:1, hs] += jnp.sum(dpr, axis=0, keepdims=True)
            vec_ref[1:2, hs] += jnp.sum(dpi, axis=0, keepdims=True)
            vec_ref[2:3, hs] += jnp.sum(dlog_a * r, axis=0, keepdims=True) * (-RG_LRU_C) * dsp[d:d + 1, hs]
            dcat = jnp.concatenate([dpr, dpi], axis=1)
            dw_ref[h] += _mm_tn(xh, dcat)
            dxc_ref[:, hs] = db * s * ig + _mm_nt(dcat, w_ref[h, :, cols])

    row_spec = pl.BlockSpec((tb, D_MODEL), lambda i: (order(i), 0))
    return pl.pallas_call(
        body, name="lru_bwd_%d" % d, grid=(steps,),
        in_specs=[row_spec] * 8 + [_full((8, LANES, 512)), _full((2, D_MODEL))],
        out_specs=[row_spec, _full((8, LANES, 2 * LANES)), _full((8, D_MODEL))],
        out_shape=[_sds((seq, D_MODEL)), _sds((8, LANES, 2 * LANES)), _sds((8, D_MODEL))],
        scratch_shapes=[pltpu.VMEM((tb, D_MODEL), F32)] + [pltpu.VMEM((SCAN_SUB, D_MODEL), F32)] * 2,
        compiler_params=_params("arbitrary"),
    )(xc, dhs, hprev, a_d, r_d, i_d, s_d, q_d, wcat, lam)


def _odd_proj_bwd(dxc_f, dxc_r, xr, dg, x1, dres, mod, conv_w, w_in4, seq):
    tm = _row_tile(seq, 512)
    steps = seq // tm

    def body(fp_ref, fm_ref, fn_ref, rp_ref, rm_ref, rn_ref, xp_ref, xm_ref, xn_ref, dg_ref, x_ref, dres_ref, mod_ref, cw_ref,
             w_ref, dx_ref, dw_ref, vec_ref, dpb_ref):
        i = pl.program_id(0)

        @pl.when(i == 0)
        def _():
            vec_ref[...] = jnp.zeros_like(vec_ref)
            dw_ref[...] = jnp.zeros_like(dw_ref)

        dxc_m = fm_ref[...] + rm_ref[...]
        dext = jnp.concatenate([jnp.where(i > 0, fp_ref[...] + rp_ref[...], 0.0), dxc_m,
                                jnp.where(i < steps - 1, fn_ref[...] + rn_ref[...], 0.0)], axis=0)
        xext = _extended(xp_ref, xm_ref, xn_ref, i, steps)
        dxr = sum(cw_ref[kk:kk + 1, :] * _shifted(dext, 2 - kk, tm) for kk in range(4))
        for kk in range(4):
            vec_ref[kk:kk + 1, :] += jnp.sum(dxc_m * _shifted(xext, kk - 2, tm), axis=0, keepdims=True)
        vec_ref[4:5, :] += jnp.sum(dxc_m, axis=0, keepdims=True)
        dpb_ref[:, :D_MODEL] = dxr.astype(dpb_ref.dtype)
        dpb_ref[:, D_MODEL:] = dg_ref[...].astype(dpb_ref.dtype)
        cs = ODD_IN // 4
        dh = sum(_mm_nt(dpb_ref[:, s * cs:(s + 1) * cs], w_ref[s]) for s in range(4))
        x = x_ref[...]
        h_t = (x * (1.0 + mod_ref[1:2, :]) + mod_ref[0:1, :]).T.astype(MXU_DTYPE)
        for s in range(4):
            dw_ref[s] += jnp.dot(h_t, dpb_ref[:, s * cs:(s + 1) * cs], preferred_element_type=F32)
        vec_ref[5:6, :] += jnp.sum(dh, axis=0, keepdims=True)
        vec_ref[6:7, :] += jnp.sum(dh * x, axis=0, keepdims=True)
        dx_ref[...] = dres_ref[...] + dh * (1.0 + mod_ref[1:2, :])

    return pl.pallas_call(
        body, name="odd_proj_bwd", grid=(steps,),
        in_specs=_halo_specs(tm, seq, D_MODEL) * 3 + [_rows(tm, D_MODEL)] * 3
        + [_full((3, D_MODEL)), _full((4, D_MODEL)), _const((4, D_MODEL, ODD_IN // 4))],
        out_specs=[_rows(tm, D_MODEL), _const((4, D_MODEL, ODD_IN // 4)), _full((8, D_MODEL))],
        out_shape=[_sds((seq, D_MODEL)), _sds((4, D_MODEL, ODD_IN // 4)), _sds((8, D_MODEL))],
        scratch_shapes=[pltpu.VMEM((tm, ODD_IN), MXU_DTYPE)],
        compiler_params=_params("arbitrary"),
    )(dxc_f, dxc_f, dxc_f, dxc_r, dxc_r, dxc_r, xr, xr, xr, dg, x1, dres, mod, conv_w, w_in4)


def _even_out_bwd(dx1, zhat, rstd, ycat, g, mod, ln_g, w_out, seq, rider=None):
    tm = _row_tile(seq, 512)
    steps = seq // tm

    def body(dx_ref, zh_ref, rs_ref, y_ref, g_ref, mod_ref, lg_ref, w_ref, dy_ref, dg_ref, dres_ref, dw_ref, vec_ref):
        i = pl.program_id(0)

        @pl.when(i == 0)
        def _():
            dw_ref[...] = jnp.zeros_like(dw_ref)
            vec_ref[...] = jnp.zeros_like(vec_ref)

        zhat = zh_ref[...]
        dx1_ = dx_ref[...]
        dz = _ln_bwd(dx1_, zhat, rs_ref[...], lg_ref[...])
        vec_ref[0:1, :] += jnp.sum(dx1_ * zhat, axis=0, keepdims=True)
        vec_ref[1:2, :] += jnp.sum(dx1_, axis=0, keepdims=True)
        dres_ref[...] = ALPHA * dz
        gate = mod_ref[2:3, :]
        gg = g_ref[...].astype(F32)
        sg = _sigmoid(gg)
        silu = gg * sg
        ycat_ = y_ref[...].astype(F32)
        dw_ref[...] += _mm_tn(ycat_ * silu, dz)
        dy = _mm_nt(gate * dz, w_ref[...])
        dy_ref[...] = (dy * silu).astype(dy_ref.dtype)
        dg_ref[...] = (dy * ycat_ * (sg * (1.0 + gg * (1.0 - sg)))).astype(dg_ref.dtype)

        @pl.when(i == steps - 1)
        def _():
            m_acc = dw_ref[...]
            vec_ref[2:3, :] = jnp.sum(w_ref[...].astype(F32) * m_acc, axis=0, keepdims=True)
            dw_ref[...] = m_acc * gate

    return _call(
        body, "even_out_bwd", (steps,),
        [_rows(tm, D_MODEL), _rows(tm, D_MODEL), _rows(tm, 1), _rows(tm, D_MODEL), _rows(tm, D_MODEL), _full((3, D_MODEL)),
         _full((1, D_MODEL)), _const((D_MODEL, D_MODEL))],
        [_rows(tm, D_MODEL)] * 3 + [_full((D_MODEL, D_MODEL)), _full((8, D_MODEL))],
        [_sds((seq, D_MODEL), ACT_DTYPE), _sds((seq, D_MODEL), ACT_DTYPE), _sds((seq, D_MODEL)), _sds((D_MODEL, D_MODEL)),
         _sds((8, D_MODEL))],
        (dx1, zhat, rstd, ycat, g, mod, ln_g, w_out), "arbitrary", rider=rider)


def _even_mix_bwd(q, k, v, lse, ycat, dycat, su, svo_s, vhat_s, rstd_s, sink, sgln_g, sgln_b, sgw, e2, e8, seq, rider=None):
    nb = seq // BLK

    def body(sink_ref, q_ref, k_ref, v_ref, lse_ref, y_ref, dy_ref, su_ref, svo_ref, vhat_ref, rstd_ref, lng_ref, lnb_ref, sgw_ref,
             e2_ref, e8_ref, dq_ref, dsu_ref, dsv_ref, dk_ref, dv_ref, dsgw_ref, dsgb_ref, vec_ref, dsink_ref, dsgb_acc):
        n = pl.program_id(0)

        @pl.when(n == 0)
        def _():
            dk_ref[...] = jnp.zeros_like(dk_ref)
            dv_ref[...] = jnp.zeros_like(dv_ref)
            dsgw_ref[...] = jnp.zeros_like(dsgw_ref)
            dsgb_acc[...] = jnp.zeros_like(dsgb_acc)
            vec_ref[...] = jnp.zeros_like(vec_ref)
            dsink_ref[...] = jnp.zeros_like(dsink_ref)

        kband = _band(k_ref, n, nb)
        vband = _band(v_ref, n, nb)
        bias = _band_bias(n, seq)
        lane = _lane_iota((BLK, LANES))
        row8 = lax.broadcasted_iota(jnp.int32, (8, LANES), 0)
        lse = lse_ref[...]
        dkb = jnp.zeros((LANES, 3 * BLK), F32)
        dvb = jnp.zeros((LANES, 3 * BLK), F32)
        dsink = jnp.zeros((8, LANES), F32)
        q_tile = lambda j: q_ref[:, j * LANES:(j + 1) * LANES].astype(F32)
        do_tile = lambda j: dy_ref[:, j * LANES:(j + 1) * LANES].astype(F32)
        dq = [jnp.zeros((BLK, LANES), F32) for _ in range(ATTN_WIDTH // LANES)]
        for kv in range(N_Q_HEADS // Q_PER_KV):
            heads = range(Q_PER_KV * kv, Q_PER_KV * (kv + 1))
            lse4, delta4 = [], []
            for h in heads:
                head_lanes = (lane < HEAD_DIM) if h % 2 == 0 else (lane >= HEAD_DIM)
                lse4.append(jnp.sum(jnp.where(lane == h, lse, 0.0), axis=1, keepdims=True))
                o_tile = y_ref[:, (h // 2) * LANES:(h // 2 + 1) * LANES].astype(F32)
                delta4.append(jnp.sum(jnp.where(head_lanes, do_tile(h // 2) * o_tile, 0.0), axis=1, keepdims=True))
            lse4, delta4 = jnp.concatenate(lse4, axis=0), jnp.concatenate(delta4, axis=0)
            q4, do4 = _stack_heads(q_tile, kv), _stack_heads(do_tile, kv)
            s = _mm_nt(q4, kband) * (HEAD_DIM ** -0.5) + bias
            p = jnp.exp(s - lse4)
            wsink = jnp.exp(_per_head_column([sink_ref[h] for h in heads]) - lse4) * delta4
            ds = p * (_mm_nt(do4, vband) - delta4) * (HEAD_DIM ** -0.5)
            dq4 = _mm(ds, kband)
            dkb = dkb + _mm_tn(q4, ds)
            dvb = dvb + _mm_tn(do4, p)
            for g, h in enumerate(heads):
                dq[h // 2] = dq[h // 2] + _from_kv_lanes(dq4[g * BLK:(g + 1) * BLK], h)
                dsink = dsink + jnp.where(row8 == h, -jnp.sum(wsink[g * BLK:(g + 1) * BLK]), 0.0)
        for j in range(ATTN_WIDTH // LANES):
            dq_ref[:, j * LANES:(j + 1) * LANES] = dq[j].astype(dq_ref.dtype)
        dsink_ref[...] += dsink
        prev = jnp.maximum(n - 1, 0)
        nxt = jnp.minimum(n + 1, nb - 1)
        for part, blk_i in enumerate((prev, n, nxt)):
            rows = pl.ds(pl.multiple_of(blk_i * BLK, BLK), BLK)
            dk_ref[rows, :] += dkb[:, part * BLK:(part + 1) * BLK].T
            dv_ref[rows, :] += dvb[:, part * BLK:(part + 1) * BLK].T

        e2 = e2_ref[...]
        lng, lnb = lng_ref[...], lnb_ref[...]
        for j in range(SG_WIDTH // LANES):
            cs = slice(j * LANES, (j + 1) * LANES)
            vhat = vhat_ref[:, cs].astype(F32)
            vn = vhat * lng[:, cs] + lnb[:, cs]
            dysg = dy_ref[:, ATTN_WIDTH + j * LANES:ATTN_WIDTH + (j + 1) * LANES].astype(F32)
            dsu_ref[:, cs] = (dysg * svo_ref[:, cs].astype(F32)).astype(dsu_ref.dtype)
            dsvo = dysg * su_ref[:, cs].astype(F32)
            dsgb_acc[:, cs] += dsvo
            d_lo = jnp.where(lane < HEAD_DIM, dsvo, 0.0)
            d_hi = dsvo - d_lo
            dsgw_ref[2 * j] += _mm_nt(d_lo, vn)
            dsgw_ref[2 * j + 1] += _mm_nt(d_hi, vn)
            dvn = _mm_tn(sgw_ref[2 * j], d_lo) + _mm_tn(sgw_ref[2 * j + 1], d_hi)
            vec_ref[0:1, cs] += jnp.sum(dvn * vhat, axis=0, keepdims=True)
            vec_ref[1:2, cs] += jnp.sum(dvn, axis=0, keepdims=True)
            dvh = dvn * lng[:, cs]
            m1 = _group_sum(dvh, e2) * (1.0 / HEAD_DIM)
            m2 = _group_sum(dvh * vhat, e2) * (1.0 / HEAD_DIM)
            dsv_ref[:, cs] = (rstd_ref[:, cs].astype(F32) * (dvh - m1 - vhat * m2)).astype(dsv_ref.dtype)

        @pl.when(n == nb - 1)
        def _():
            rest = dsgb_acc[...]
            total = jnp.zeros((8, BLK), F32)
            for _ in range(3):
                part = rest.astype(MXU_DTYPE)
                total = total + lax.dot_general(e8_ref[...], part, (((1,), (1,)), ((), ())), preferred_element_type=F32)
                rest = rest - part.astype(F32)
            dsgb_ref[...] = total

    blk = lambda w: pl.BlockSpec((BLK, w), lambda n: (n, 0))
    return _call(
        body, "even_mix_bwd", (nb,),
        [pl.BlockSpec(memory_space=pltpu.SMEM), blk(512), _full((seq, LANES)), _full((seq, LANES)), blk(LANES),
         blk(D_MODEL), blk(D_MODEL), blk(512), blk(512), blk(512), blk(512), _full((1, 512)), _full((1, 512)), _full((8, BLK, BLK)),
         _full((LANES, LANES)), _full((8, 512))],
        [blk(512), blk(512), blk(512), _full((seq, LANES)), _full((seq, LANES)), _full((8, BLK, BLK)),
         _full((8, BLK)), _full((8, 512)), _full((8, LANES))],
        [_sds((seq, 512), ACT_DTYPE), _sds((seq, 512), ACT_DTYPE), _sds((seq, 512), ACT_DTYPE), _sds((seq, LANES)), _sds((seq, LANES)),
         _sds((8, BLK, BLK)), _sds((8, BLK)), _sds((8, 512)), _sds((8, LANES))],
        (sink, q, k, v, lse, ycat, dycat, su, svo_s, vhat_s, rstd_s, sgln_g, sgln_b, sgw, e2, e8), "arbitrary",
        scratch=[pltpu.VMEM((BLK, 512), F32)], rider=rider)


def _even_proj_bwd(dq, dk, dv, dsu, dsv, dg, x, dres, mod, tabs, w_in_t, seq):
    tm = _row_tile(seq, 512)

    def body(dq_ref, dk_ref, dv_ref, dsu_ref, dsv_ref, dg_ref, x_ref, dres_ref, mod_ref, cos_ref, sp_ref, sm_ref, wt_ref,
             dx_ref, dw_ref, vec_ref, dpb_ref):
        @pl.when(pl.program_id(0) == 0)
        def _():
            vec_ref[...] = jnp.zeros_like(vec_ref)
            dw_ref[...] = jnp.zeros_like(dw_ref)

        cos_t, sin_p, sin_m = cos_ref[...], sp_ref[...], sm_ref[...]
        dt = dpb_ref.dtype
        for j in range(ATTN_WIDTH // LANES):
            cs = slice(j * LANES, (j + 1) * LANES)
            dpb_ref[:, cs] = _rope_t(dq_ref[:, cs].astype(F32), cos_t, sin_p, sin_m).astype(dt)
        dpb_ref[:, 512:640] = _rope_t(dk_ref[...], cos_t, sin_p, sin_m).astype(dt)
        dpb_ref[:, 640:768] = dv_ref[...].astype(dt)
        dpb_ref[:, 768:1280] = dsu_ref[...].astype(dt)
        dpb_ref[:, 1280:1792] = dsv_ref[...].astype(dt)
        dpb_ref[:, 1792:2816] = dg_ref[...].astype(dt)
        dpb = dpb_ref[...]
        dh = jnp.dot(dpb, wt_ref[...], preferred_element_type=F32)
        x_ = x_ref[...]
        hb = (x_ * (1.0 + mod_ref[1:2, :]) + mod_ref[0:1, :]).astype(MXU_DTYPE)
        dw_ref[...] += _mm_tn(dpb, hb)
        vec_ref[0:1, :] += jnp.sum(dh, axis=0, keepdims=True)
        vec_ref[1:2, :] += jnp.sum(dh * x_, axis=0, keepdims=True)
        dx_ref[...] = dres_ref[...] + dh * (1.0 + mod_ref[1:2, :])

    return pl.pallas_call(
        body, name="even_proj_bwd", grid=(seq // tm,),
        in_specs=[_rows(tm, 512), _rows(tm, LANES), _rows(tm, LANES), _rows(tm, 512), _rows(tm, 512), _rows(tm, D_MODEL),
                  _rows(tm, D_MODEL), _rows(tm, D_MODEL), _full((3, D_MODEL))] + [_rows(tm, LANES)] * 3
        + [_const((EVEN_IN, D_MODEL))],
        out_specs=[_rows(tm, D_MODEL), _const((EVEN_IN, D_MODEL)), _full((8, D_MODEL))],
        out_shape=[_sds((seq, D_MODEL)), _sds((EVEN_IN, D_MODEL)), _sds((8, D_MODEL))],
        scratch_shapes=[pltpu.VMEM((tm, EVEN_IN), MXU_DTYPE)],
        compiler_params=_params("arbitrary"),
    )(dq, dk, dv, dsu, dsv, dg, x, dres, mod, *tabs, w_in_t)


def _local_step(x, posf, tgt, mod, w, seq, ride=None):
    rid = lambda make, *a: None if ride is None else make(*a)
    mxu = lambda a: a.astype(MXU_DTYPE)
    row = lambda a: a.reshape(1, -1)
    tabs = _rope_tables(posf, seq)
    e2 = mxu(jnp.kron(jnp.eye(2, dtype=F32), jnp.ones((HEAD_DIM, HEAD_DIM), F32)))
    e8 = mxu(jnp.repeat(jnp.eye(N_SG_GROUPS, dtype=F32), HEAD_DIM, axis=1))
    sgw = mxu(w["ev_sg_w"])
    sgb_full = jnp.repeat(w["ev_sg_b"].T, HEAD_DIM, axis=1)
    sgln_g, sgln_b = row(w["ev_sg_ln_g"]), row(w["ev_sg_ln_b"])
    sink = w["ev_sink"].reshape(N_Q_HEADS)
    ev_w_in_t = mxu(w["ev_w_in_t"])
    if ride is None:
        ev_w_out, od_w_in, od_w_out = mxu(w["ev_w_out"]), mxu(w["od_w_in"]), mxu(w["od_w_out"])
    wcat = mxu(jnp.concatenate([w["od_w_a"][0], w["od_w_x"][0], w["od_w_a"][1], w["od_w_x"][1]], axis=2))
    gate_bias = jnp.stack([w["od_b_a"][0], w["od_b_x"][0], w["od_b_a"][1], w["od_b_x"][1]])
    conv_b = row(w["od_conv_b"])
    ln_g, ln_b = w["ln_g"], w["ln_b"]

    (q, k, v, su, sv, g0), got = _even_proj(x, mod[0], ev_w_in_t, tabs, seq, rid(_gather_rider, ride and ride["ev_w_out"]))
    if ride is not None:
        ev_w_out = got[0].reshape(D_MODEL, D_MODEL)
    (ycat, lse, *sg_saved), got = _even_mix(q, k, v, su, sv, sink, sgln_g, sgln_b, sgw, sgb_full, e2, seq,
                                 rid(_gather_rider, ride and ride["od_w_in"]))
    if ride is not None:
        od_w_in = got[0]
    (zhat0, rstd0, x1, xr, g1), got = _even_out(ycat, g0, x, mod[0], mod[1], ev_w_out, od_w_in, ln_g[0:1], ln_b[0:1], seq,
                                      rid(_gather_rider, ride and ride["od_w_out"]))
    if ride is not None:
        od_w_out = got[0].reshape(D_MODEL, D_MODEL)
    lru = (w["od_conv_w"], conv_b, wcat, gate_bias, w["od_lam"], seq)
    hf, hpf, *saved_f, xc = _lru_fwd(xr, None, *lru, 0)
    hr, hpr, *saved_r = _lru_fwd(xr, xc, *lru, 1)
    dhs, dg1, dres1, loss, d_od_w_out, vec_o = _odd_out_and_loss(hf, hr, g1, x1, tgt, mod[1], od_w_out, ln_g[1:2], ln_b[1:2], seq)
    dxc_f, dw_f, vec_f = _lru_bwd(xc, dhs, hpf, *saved_f, wcat, w["od_lam"], seq, 0)
    dxc_r, dw_r, vec_r = _lru_bwd(xc, dhs, hpr, *saved_r, wcat, w["od_lam"], seq, 1)
    dx1, d_od_w_in, vec_p = _odd_proj_bwd(dxc_f, dxc_r, xr, dg1, x1, dres1, mod[1], w["od_conv_w"], od_w_in, seq)
    d_od_w_a = jnp.stack([dw_f[:, :, 0:128], dw_r[:, :, 0:128]])
    d_od_w_x = jnp.stack([dw_f[:, :, 128:256], dw_r[:, :, 128:256]])
    od_parts = [d_od_w_in.reshape(4, 2, 512, 512), d_od_w_out.reshape(4, 2, 128, D_MODEL),
                d_od_w_a.reshape(4, 2, 2 * BLK, BLK), d_od_w_x.reshape(4, 2, 2 * BLK, BLK)]
    (dycat, dg0, dres0, d_ev_w_out, vec_e), got_od = _even_out_bwd(dx1, zhat0, rstd0, ycat, g0, mod[0], ln_g[0:1], ev_w_out, seq,
                                                                   rid(_sibling_swap_rider, od_parts))
    if ride is not None:
        od_sums = _sum_sibling(ride["core"], od_parts, got_od, [ride["wire"]] * 4, "sum_sibling_od")
    (dq, dsu, dsv, dk, dv, d_sgw, d_sgb, vec_s, d_sink), od_slots = _even_mix_bwd(
        q, k, v, lse, ycat, dycat, su, *sg_saved, sink, sgln_g, sgln_b, sgw, e2, e8, seq,
        rid(_chip_exchange_rider, ride and od_sums))
    grad_x, d_ev_w_in_t, vec_x = _even_proj_bwd(dq, dk, dv, dsu, dsv, dg0, x, dres0, mod[0], tabs, ev_w_in_t, seq)

    rows, dmod_blk = _pack_small(vec_x, vec_e, vec_p, vec_o, vec_f, vec_r, vec_s, d_sink, d_sgb, loss)
    grads = {"rows": rows, "dmod_blk": dmod_blk, "ev_w_in_t": d_ev_w_in_t, "ev_w_out": d_ev_w_out, "ev_sg_w": d_sgw}
    if ride is None:
        grads.update({"od_w_in": d_od_w_in, "od_w_out": d_od_w_out, "od_w_a": d_od_w_a, "od_w_x": d_od_w_x})
    else:
        grads["od_slots"] = od_slots
    return grad_x, grads


ROW_DMOD, ROW_LN, ROW_SG_LN, ROW_SG_B, ROW_CONV_W, ROW_CONV_B, ROW_B_A, ROW_B_X, ROW_LAM, ROW_SINK, ROW_LOSS = (
    0, 6, 10, 11, 12, 16, 17, 19, 21, 23, 24)
SMALL_ROWS = 64


def _pack_small(vec_x, vec_e, vec_p, vec_o, vec_f, vec_r, vec_s, d_sink, d_sgb, loss):
    def body(x_ref, e_ref, p_ref, o_ref, f_ref, r_ref, s_ref, sink_ref, sgb_ref, loss_ref, rows_ref, dmod_ref):
        rows_ref[...] = jnp.zeros_like(rows_ref)
        dmod_ref[...] = jnp.zeros_like(dmod_ref)
        put = [(ROW_DMOD, x_ref, 0), (ROW_DMOD + 1, x_ref, 1), (ROW_DMOD + 2, e_ref, 2), (ROW_DMOD + 3, p_ref, 5),
               (ROW_DMOD + 4, p_ref, 6), (ROW_DMOD + 5, o_ref, 2), (ROW_LN, e_ref, 0), (ROW_LN + 1, e_ref, 1),
               (ROW_LN + 2, o_ref, 0), (ROW_LN + 3, o_ref, 1), (ROW_CONV_B, p_ref, 4), (ROW_B_A, f_ref, 0),
               (ROW_B_A + 1, r_ref, 0), (ROW_B_X, f_ref, 1), (ROW_B_X + 1, r_ref, 1), (ROW_LAM, f_ref, 2), (ROW_LAM + 1, r_ref, 2)]
        put += [(ROW_CONV_W + k, p_ref, k) for k in range(4)]
        for dst, ref, src in put:
            rows_ref[dst:dst + 1, :] = ref[src:src + 1, :]
            if dst < 6:
                dmod_ref[dst:dst + 1, :] = ref[src:src + 1, :]
        rows_ref[ROW_SG_LN:ROW_SG_LN + 1, 0:SG_WIDTH] = s_ref[0:1, :]
        rows_ref[ROW_SG_LN:ROW_SG_LN + 1, SG_WIDTH:2 * SG_WIDTH] = s_ref[1:2, :]
        lane = _lane_iota((1, LANES))
        sink = jnp.zeros((1, LANES), F32)
        for h in range(N_Q_HEADS):
            rows_ref[ROW_SG_B:ROW_SG_B + 1, h * LANES:(h + 1) * LANES] = sgb_ref[h:h + 1, :]
            sink = jnp.where(lane == h, sink_ref[h:h + 1, :], sink)
        rows_ref[ROW_SINK:ROW_SINK + 1, 0:LANES] = sink
        rows_ref[ROW_LOSS:ROW_LOSS + 1, 0:LANES] = jnp.where(lane == 0, loss_ref[0:1, :], 0.0)

    return pl.pallas_call(body, name="pack_small", out_shape=[_sds((SMALL_ROWS, D_MODEL)), _sds((8, D_MODEL))])(
        vec_x, vec_e, vec_p, vec_o, vec_f, vec_r, vec_s, d_sink, d_sgb, loss)


def _allgather8(block, name):
    m_per, n = block.shape

    def body(x_ref, out_ref, send_sems, recv_sems, local_sem):
        x, y, c = _place()
        me, sibling = (x, y, c), (x, y, 1 - c)
        chips = [(1 - x, y), (x, 1 - y), (1 - x, 1 - y)]

        def rows(px, py, pc):
            return out_ref.at[pl.ds((4 * px + 2 * py + pc) * m_per, m_per), :]

        def copy(k, blk, to, src=None):
            return pltpu.make_async_remote_copy(src_ref=rows(*blk) if src is None else src, dst_ref=rows(*blk),
                                                send_sem=send_sems.at[k], recv_sem=recv_sems.at[k], device_id=to,
                                                device_id_type=MESH)

        mine = pltpu.make_async_copy(x_ref, rows(*me), local_sem)
        mine.start()
        first = [copy(0, me, sibling, src=x_ref)] + [copy(1 + j, me, (*chip, c), src=x_ref) for j, chip in enumerate(chips)]
        for cp in first:
            cp.start()
        passed = [copy(4 + j, (*chip, c), sibling) for j, chip in enumerate(chips)]
        for j, chip in enumerate(chips):
            copy(1 + j, (*chip, c), me).wait_recv()
            passed[j].start()
        copy(0, sibling, me).wait_recv()
        for j, chip in enumerate(chips):
            copy(4 + j, (*chip, 1 - c), me).wait_recv()
        for cp in first + passed:
            cp.wait_send()
        mine.wait()

    return pl.pallas_call(
        body, name=name, out_shape=_sds((8 * m_per, n), block.dtype),
        in_specs=[pl.BlockSpec(memory_space=pltpu.VMEM)], out_specs=pl.BlockSpec(memory_space=pltpu.VMEM),
        scratch_shapes=[pltpu.SemaphoreType.DMA((7,)), pltpu.SemaphoreType.DMA((7,)), pltpu.SemaphoreType.DMA],
        compiler_params=pltpu.CompilerParams(vmem_limit_bytes=VMEM_LIMIT),
    )(block)


class _Copies:
    def __init__(self, send_sems, recv_sems, local_sems, stages):
        self.send_sems, self.recv_sems, self.local_sems, self.stages = send_sems, recv_sems, local_sems, stages
        self.sent, self.staged, self.locals = [], [], []

    def remote(self, k, src, dst, to):
        return pltpu.make_async_remote_copy(src_ref=src, dst_ref=dst, send_sem=self.send_sems.at[k], recv_sem=self.recv_sems.at[k],
                                            device_id=to, device_id_type=MESH)

    def send(self, k, src, dst, to):
        cp = self.remote(k, src, dst, to)
        cp.start()
        self.sent.append(cp)

    def arrived(self, k, dst, frm):
        self.remote(k, dst, dst, frm).wait_recv()

    def local(self, src, dst):
        k = len(self.staged)
        cp = pltpu.make_async_copy(src, self.stages[k], self.local_sems.at[2 * k])
        cp.start()
        self.staged.append((cp, dst))

    def flush(self):
        for k in range(len(self.locals), len(self.staged)):
            cp, dst = self.staged[k]
            cp.wait()
            out = pltpu.make_async_copy(self.stages[k], dst, self.local_sems.at[2 * k + 1])
            out.start()
            self.locals.append(out)

    def drain(self):
        self.flush()
        for cp in self.sent:
            cp.wait_send()
        for cp in self.locals:
            cp.wait()


def _comm_call(body, name, ins, out_shapes, n_remote, stages):
    n_in, n_out = len(ins), len(out_shapes)

    def kern(*refs):
        in_refs, out_refs = refs[:n_in], refs[n_in:n_in + n_out]
        send_sems, recv_sems, local_sems = refs[n_in + n_out:n_in + n_out + 3]
        body(_Copies(send_sems, recv_sems, local_sems, refs[n_in + n_out + 3:]), in_refs, out_refs)

    hbm = pl.BlockSpec(memory_space=pl.ANY)
    return pl.pallas_call(
        kern, name=name, out_shape=out_shapes, in_specs=[hbm] * n_in, out_specs=[hbm] * n_out,
        scratch_shapes=[pltpu.SemaphoreType.DMA((n_remote,)), pltpu.SemaphoreType.DMA((n_remote,)),
                        pltpu.SemaphoreType.DMA((2 * len(stages),))] + [pltpu.VMEM(s, d) for s, d in stages],
        compiler_params=pltpu.CompilerParams(vmem_limit_bytes=VMEM_LIMIT),
    )(*ins)


def _gather_to_all(cps, pairs, me, sibling, other_chips, c, base):
    idx = lambda p: 4 * p[0] + 2 * p[1] + p[2]
    for i, (src, dst) in enumerate(pairs):
        cps.local(src, dst.at[idx(me)])
        cps.send(base + 7 * i, src, dst.at[idx(me)], sibling)
        for j, chip in enumerate(other_chips):
            cps.send(base + 7 * i + 1 + j, src, dst.at[idx(me)], (*chip, c))
    cps.flush()
    for j, chip in enumerate(other_chips):
        for i, (_, dst) in enumerate(pairs):
            got = dst.at[idx((*chip, c))]
            cps.arrived(base + 7 * i + 1 + j, got, (*chip, c))
            cps.send(base + 7 * i + 4 + j, got, got, sibling)
    for i, (_, dst) in enumerate(pairs):
        cps.arrived(base + 7 * i, dst.at[idx(sibling)], sibling)
        for j, chip in enumerate(other_chips):
            cps.arrived(base + 7 * i + 4 + j, dst.at[idx((*chip, 1 - c))], sibling)


def _gather_weights(shards, small):
    n = len(shards)

    def body(cps, ins, outs):
        x, y, c = _place()
        me, sibling, mine = (x, y, c), (x, y, 1 - c), 2 * x + y
        chips = [(1 - x, y), (x, 1 - y), (1 - x, 1 - y)]
        for i in range(n):
            cps.local(ins[i], outs[i].at[mine])
        for j, (px, py) in enumerate(chips):
            for i in range(n):
                hr = shards[i].shape[0] // 2
                rows = pl.ds(c * hr, hr)
                cps.send(6 * i + j, ins[i].at[rows], outs[i].at[mine, rows], (px, py, c))
        _gather_to_all(cps, [(ins[n], outs[n])], me, sibling, chips, c, 6 * n)
        for j, (px, py) in enumerate(chips):
            for i in range(n):
                hr = shards[i].shape[0] // 2
                got = outs[i].at[2 * px + py, pl.ds(c * hr, hr)]
                cps.arrived(6 * i + j, got, (px, py, c))
                cps.send(6 * i + 3 + j, got, got, sibling)
        for j, (px, py) in enumerate(chips):
            for i in range(n):
                hr = shards[i].shape[0] // 2
                cps.arrived(6 * i + 3 + j, outs[i].at[2 * px + py, pl.ds((1 - c) * hr, hr)], sibling)
        cps.drain()

    return _comm_call(body, "gather_weights", list(shards) + [small],
                      [_sds((4,) + s.shape, s.dtype) for s in shards] + [_sds((8,) + small.shape, small.dtype)], 6 * n + 7,
                      [(a.shape, a.dtype) for a in list(shards) + [small]])


def _reduce_sibling(parts, dmod_rows):
    n = len(parts)

    def body(cps, ins, outs):
        x, y, c = _place()
        me, sibling = (x, y, c), (x, y, 1 - c)
        chips = [(1 - x, y), (x, 1 - y), (1 - x, 1 - y)]
        for i in range(n):
            cps.send(i, ins[i].at[:, 1 - c], outs[i], sibling)
        _gather_to_all(cps, [(ins[n], outs[n])], me, sibling, chips, c, n)
        for i in range(n):
            cps.arrived(i, outs[i], sibling)
        cps.drain()

    return _comm_call(body, "reduce_sibling", list(parts) + [dmod_rows],
                      [_sds((4,) + p.shape[2:], p.dtype) for p in parts] + [_sds((8,) + dmod_rows.shape, dmod_rows.dtype)], n + 7,
                      [(dmod_rows.shape, dmod_rows.dtype)])


def _reduce_chips(parts):
    n = len(parts)

    def body(cps, ins, outs):
        x, y, c = _place()
        mine = 2 * x + y
        chips = _other_chips(x, y)
        for i in range(n):
            cps.local(ins[i].at[mine], outs[i].at[mine])
        for j, (px, py) in enumerate(chips):
            for i in range(n):
                cps.send(3 * i + j, ins[i].at[2 * px + py], outs[i].at[mine], (px, py, c))
        cps.flush()
        for j, (px, py) in enumerate(chips):
            for i in range(n):
                cps.arrived(3 * i + j, outs[i].at[2 * px + py], (px, py, c))
        cps.drain()

    return _comm_call(body, "reduce_chips", list(parts), [_sds(p.shape, p.dtype) for p in parts], 3 * n,
                      [(p.shape[1:], p.dtype) for p in parts])


def _gather_reduced(shard_parts, repl_parts):
    ns, nr = len(shard_parts), len(repl_parts)

    def body(cps, ins, outs):
        x, y, c = _place()
        me, sibling = (x, y, c), (x, y, 1 - c)
        chips = [(1 - x, y), (x, 1 - y), (1 - x, 1 - y)]
        for i in range(ns):
            cps.local(ins[i], outs[i].at[c])
            cps.send(i, ins[i], outs[i].at[c], sibling)
        _gather_to_all(cps, [(ins[ns + i], outs[ns + i]) for i in range(nr)], me, sibling, chips, c, ns)
        for i in range(ns):
            cps.arrived(i, outs[i].at[1 - c], sibling)
        cps.drain()

    return _comm_call(body, "gather_reduced", list(shard_parts) + list(repl_parts),
                      [_sds((2,) + p.shape, p.dtype) for p in shard_parts] + [_sds((8,) + p.shape, p.dtype) for p in repl_parts],
                      ns + 7 * nr, [(p.shape, p.dtype) for p in list(shard_parts) + list(repl_parts)])


def _sum_sibling(core, parts, got, wire, name):
    n = len(parts)

    def body(core_ref, *refs):
        for i in range(n):
            refs[2 * n + i][0] = (refs[i][0] + refs[n + i][0]).astype(wire[i])

    keep_spec = lambda p: pl.BlockSpec((1, None) + p.shape[2:], lambda s, core_ref: (s, core_ref[0], 0, 0))
    slot_spec = lambda p: pl.BlockSpec((1,) + p.shape[2:], lambda s, core_ref: (s, 0, 0))
    return pl.pallas_call(
        body, name=name,
        grid_spec=pltpu.PrefetchScalarGridSpec(
            num_scalar_prefetch=1, grid=(4,), in_specs=[keep_spec(p) for p in parts] + [slot_spec(p) for p in parts],
            out_specs=[slot_spec(p) for p in parts]),
        out_shape=[_sds((4,) + p.shape[2:], wire[i]) for i, p in enumerate(parts)],
        compiler_params=_params("parallel"),
    )(core, *parts, *got)


def _sum_slots(slots, name):
    n = len(slots)

    def spec_pair(p):
        k, rows, cols = p.shape
        sub = 16 if p.dtype == BF16 else 8
        if (rows // 2) % sub == 0:
            return pl.BlockSpec((k, rows // 2, cols), lambda i: (0, i, 0)), pl.BlockSpec((rows // 2, cols), lambda i: (i, 0))
        return pl.BlockSpec((k, rows, cols), lambda i: (0, 0, 0)), pl.BlockSpec((rows, cols), lambda i: (0, 0))

    pairs = [spec_pair(p) for p in slots]

    def body(*refs):
        for i in range(n):
            acc = refs[i][0].astype(F32)
            for j in range(1, slots[i].shape[0]):
                acc = acc + refs[i][j].astype(F32)
            refs[n + i][...] = acc

    return pl.pallas_call(
        body, name=name, grid=(2,), in_specs=[a for a, _ in pairs], out_specs=[b for _, b in pairs],
        out_shape=[_sds(p.shape[1:]) for p in slots], compiler_params=_params("arbitrary"),
    )(*slots)


def _modulation(c_all, ada_w, ada_b):
    cols = ada_w.shape[2]

    def body(c_ref, w_ref, b_ref, o_ref):
        cc = c_ref[...]
        o_ref[0] = _mm(cc * _sigmoid(cc), w_ref[0]) + b_ref[0]

    return pl.pallas_call(
        body, name="modulation", grid=(2,),
        in_specs=[_full((8, D_MODEL)), pl.BlockSpec((1, D_MODEL, cols), lambda l: (l, 0, 0)), pl.BlockSpec((1, 1, cols), lambda l: (l, 0, 0))],
        out_specs=pl.BlockSpec((1, 8, cols), lambda l: (l, 0, 0)), out_shape=_sds((2, 8, cols)),
        compiler_params=_params("parallel"),
    )(c_all, ada_w, ada_b)


def _adamw_math(w, g, m, v):
    m = ADAM_B1 * m + (1.0 - ADAM_B1) * g
    v = ADAM_B2 * v + (1.0 - ADAM_B2) * (g * g)
    m_hat = m / (1.0 - ADAM_B1 ** ADAM_STEP)
    v_hat = v / (1.0 - ADAM_B2 ** ADAM_STEP)
    delta = -ADAM_LR * (m_hat / (jnp.sqrt(v_hat) + ADAM_EPS) + ADAM_WD * w)
    return delta, m, v


def _ada_update(c_all, dmod, w, m, v, rider=None):
    cols = w.shape[2]
    tr = 256
    per = D_MODEL // tr
    spec3 = pl.BlockSpec((1, tr, cols), lambda i: (i // per, i % per, 0))

    def body(c_ref, d_ref, w_ref, m_ref, v_ref, g_ref, dl_ref, nm_ref, nv_ref):
        cc = c_ref[...]
        g = _mm_tn(cc * _sigmoid(cc), d_ref[0])
        g_ref[0] = g
        dl_ref[0], nm_ref[0], nv_ref[0] = _adamw_math(w_ref[0], g, m_ref[0], v_ref[0])

    return _call(
        body, "ada_update", (2 * per,),
        [pl.BlockSpec((8, tr), lambda i: (0, i % per)), pl.BlockSpec((1, 8, cols), lambda i: (i // per, 0, 0)), spec3, spec3, spec3],
        [spec3] * 4, [_sds(w.shape)] * 4, (c_all, dmod, w, m, v), "parallel", rider=rider)


def _adamw_matrices(params):
    n = len(params)
    steps = 8

    def body(*refs):
        ins, outs = refs[:4 * n], refs[4 * n:]
        for j in range(n):
            w_ref, g_ref, m_ref, v_ref = ins[4 * j:4 * j + 4]
            g = g_ref[...]
            outs[4 * j][...] = g
            outs[4 * j + 1][...], outs[4 * j + 2][...], outs[4 * j + 3][...] = _adamw_math(w_ref[...], g, m_ref[...], v_ref[...])

    spec = lambda p: _rows(p[0].shape[0] // steps, p[0].shape[1])
    res = pl.pallas_call(
        body, name="adamw_matrices", grid=(steps,), in_specs=[spec(p) for p in params for _ in range(4)],
        out_specs=[spec(p) for p in params for _ in range(4)], out_shape=[_sds(p[0].shape) for p in params for _ in range(4)],
        compiler_params=_params("parallel"),
    )(*[a for p in params for a in p])
    return [tuple(res[4 * j:4 * j + 4]) for j in range(n)]


def _adamw_small(params):
    n = len(params)

    def body(*refs):
        ins, outs = refs[:4 * n], refs[4 * n:]
        for j in range(n):
            w_ref, g_ref, m_ref, v_ref = ins[4 * j:4 * j + 4]
            outs[3 * j][...], outs[3 * j + 1][...], outs[3 * j + 2][...] = _adamw_math(w_ref[...], g_ref[...], m_ref[...], v_ref[...])

    flat = [a for p in params for a in p]
    res = pl.pallas_call(body, name="adamw_small", out_shape=[_sds(p[0].shape) for p in params for _ in range(3)])(*flat)
    return [tuple(res[3 * j:3 * j + 3]) for j in range(n)]


def _cols(a, start, size):
    return lax.dynamic_slice_in_dim(a, start, size, axis=a.ndim - 1)


def kernel(x, c, positions, ada_w, ada_b, ln_g, ln_b, ev_w_in, ev_w_out, ev_sink, ev_sg_ln_g, ev_sg_ln_b, ev_sg_w, ev_sg_b, od_w_in, od_conv_w, od_conv_b, od_w_a, od_b_a, od_w_x, od_b_x, od_lam, od_w_out, loss_target, m_ada_w, m_ada_b, m_ln_g, m_ln_b, m_ev_w_in, m_ev_w_out, m_ev_sink, m_ev_sg_ln_g, m_ev_sg_ln_b, m_ev_sg_w, m_ev_sg_b, m_od_w_in, m_od_conv_w, m_od_conv_b, m_od_w_a, m_od_b_a, m_od_w_x, m_od_b_x, m_od_lam, m_od_w_out, v_ada_w, v_ada_b, v_ln_g, v_ln_b, v_ev_w_in, v_ev_w_out, v_ev_sink, v_ev_sg_ln_g, v_ev_sg_ln_b, v_ev_sg_w, v_ev_sg_b, v_od_w_in, v_od_conv_w, v_od_conv_b, v_od_w_a, v_od_b_a, v_od_w_x, v_od_b_x, v_od_lam, v_od_w_out):
    seq = x.shape[1]
    px, py, pc = _place()
    chip = 2 * px + py
    dev = 2 * chip + pc

    small = jnp.concatenate([od_conv_w[0].reshape(-1), od_conv_b[0], od_b_a[0].reshape(-1), jnp.zeros((256,), F32),
                             od_b_x[0].reshape(-1), od_lam[0].reshape(-1)]).reshape(3, D_MODEL)
    blk = jnp.concatenate([c, small, jnp.zeros((4, D_MODEL), F32)], axis=0)
    tr = lambda a: jnp.swapaxes(a, -1, -2)
    wire_w = lambda a: a.astype(MXU_DTYPE)
    ev_w_in4, g_small = _gather_weights([wire_w(tr(ev_w_in[0]))], blk)
    core = pc.astype(jnp.int32).reshape(1)
    ride = {"ev_w_out": wire_w(ev_w_out[0]), "od_w_in": wire_w(od_w_in[0]), "od_w_out": wire_w(od_w_out[0]),
            "core": core, "wire": MXU_DTYPE}
    c_all = g_small[:, 0, :]
    per_chip = g_small[0::2]
    conv_w = per_chip[:, 1].reshape(4, 4, 256).transpose(1, 0, 2).reshape(4, D_MODEL)
    conv_b = per_chip[:, 2, 0:256].reshape(D_MODEL)
    b_a = per_chip[:, 2, 256:768].reshape(4, 2, 256).transpose(1, 0, 2).reshape(2, D_MODEL)
    b_x = per_chip[:, 3, 0:512].reshape(4, 2, 256).transpose(1, 0, 2).reshape(2, D_MODEL)
    lam = per_chip[:, 3, 512:1024].reshape(4, 2, 256).transpose(1, 0, 2).reshape(2, D_MODEL)

    w_full = {
        "ev_w_in_t": ev_w_in4.reshape(EVEN_IN, D_MODEL),
        "ev_sink": ev_sink[0], "ev_sg_ln_g": ev_sg_ln_g[0], "ev_sg_ln_b": ev_sg_ln_b[0], "ev_sg_w": ev_sg_w[0],
        "ev_sg_b": ev_sg_b[0], "od_conv_w": conv_w, "od_conv_b": conv_b, "od_w_a": od_w_a[0], "od_b_a": b_a,
        "od_w_x": od_w_x[0], "od_b_x": b_x, "od_lam": lam, "ln_g": ln_g, "ln_b": ln_b,
    }

    ada_cols = ada_w.shape[2]
    mod_sh = _modulation(c_all, ada_w, _cols(ada_b, chip * ada_cols, ada_cols).reshape(2, 1, ada_cols))
    mod_all = _allgather8(mod_sh.reshape(16, ada_cols), "gather_mod").reshape(4, 2, 2, 8, ada_cols)[:, 0]
    mod_mine = lax.dynamic_index_in_dim(mod_all, dev, axis=2, keepdims=False)
    mod = mod_mine.transpose(1, 0, 2).reshape(2, 3, D_MODEL)

    posf = positions.astype(F32).reshape(seq, 1)
    grad_x, g = _local_step(x[0], posf, loss_target[0], mod, w_full, seq, ride)

    parts = [g["ev_w_in_t"].reshape(4, 2, 352, D_MODEL), g["ev_w_out"].reshape(4, 2, 128, D_MODEL),
             g["ev_sg_w"].reshape(4, 2, BLK, BLK), g["rows"].reshape(4, 2, SMALL_ROWS // 8, D_MODEL)]
    wire = [MXU_DTYPE] * 3 + [F32]
    *got, dmod_gathered = _reduce_sibling(parts, g["dmod_blk"])
    ev_slots = list(_reduce_chips(_sum_sibling(core, parts, got, wire, "sum_sibling")))
    od_slots = list(g["od_slots"])
    mine = _sum_slots(ev_slots[0:2] + od_slots[0:2] + ev_slots[2:3] + od_slots[2:4] + ev_slots[3:4], "sum_chips")
    reduced = _gather_reduced(mine[:4], mine[4:])
    g_ev_w_in_t = reduced[0].reshape(704, D_MODEL)
    g_ev_w_out = reduced[1].reshape(256, D_MODEL)
    g_od_w_in = reduced[2].reshape(D_MODEL, 512)
    g_od_w_out = reduced[3].reshape(256, D_MODEL)
    g_sg_w = reduced[4].reshape(8 * BLK, BLK)
    g_w_a = reduced[5].reshape(16 * BLK, BLK)
    g_w_x = reduced[6].reshape(16 * BLK, BLK)
    gs = reduced[7].reshape(SMALL_ROWS, D_MODEL)
    loss = gs[ROW_LOSS, 0]
    dmod_all = dmod_gathered[:, 0:6].reshape(8, 2, 3 * D_MODEL)
    dmod_sh = _cols(dmod_all, chip * ada_cols, ada_cols).transpose(1, 0, 2)
    (g_ada_w, d_ada_w, nm_ada_w, nv_ada_w), _ = _ada_update(c_all, dmod_sh, ada_w, m_ada_w, v_ada_w)

    mats = (("ev_w_out", ev_w_out, g_ev_w_out, m_ev_w_out, v_ev_w_out), ("od_w_in", od_w_in, g_od_w_in, m_od_w_in, v_od_w_in),
            ("od_w_out", od_w_out, g_od_w_out, m_od_w_out, v_od_w_out), ("ev_sg_w", ev_sg_w, g_sg_w, m_ev_sg_w, v_ev_sg_w),
            ("od_w_a", od_w_a, g_w_a, m_od_w_a, v_od_w_a), ("od_w_x", od_w_x, g_w_x, m_od_w_x, v_od_w_x))
    upd = _adamw_matrices([(tr(ev_w_in[0]), g_ev_w_in_t, tr(m_ev_w_in[0]), tr(v_ev_w_in[0]))]
                          + [(w_.reshape(g_.shape), g_, m_.reshape(g_.shape), v_.reshape(g_.shape)) for _, w_, g_, m_, v_ in mats])
    big = {"ev_w_in": tuple(tr(a).reshape(ev_w_in.shape) for a in upd[0])}
    for (name, w_, _, _, _), u in zip(mats, upd[1:]):
        big[name] = tuple(a.reshape(w_.shape) for a in u)
    big["ada_w"] = (g_ada_w, d_ada_w, nm_ada_w, nv_ada_w)

    sh = lambda a: _cols(a, chip * 256, 256)
    small_g = {
        "ada_b": gs[ROW_DMOD:ROW_DMOD + 6].reshape(2, 3 * D_MODEL),
        "ln_g": jnp.stack([gs[ROW_LN], gs[ROW_LN + 2]]), "ln_b": jnp.stack([gs[ROW_LN + 1], gs[ROW_LN + 3]]),
        "ev_sink": gs[ROW_SINK:ROW_SINK + 1, 0:N_Q_HEADS], "ev_sg_ln_g": gs[ROW_SG_LN:ROW_SG_LN + 1, 0:SG_WIDTH],
        "ev_sg_ln_b": gs[ROW_SG_LN:ROW_SG_LN + 1, SG_WIDTH:2 * SG_WIDTH], "ev_sg_b": gs[ROW_SG_B].reshape(N_SG_GROUPS, BLK),
        "od_conv_w": sh(gs[ROW_CONV_W:ROW_CONV_W + 4]), "od_conv_b": sh(gs[ROW_CONV_B:ROW_CONV_B + 1]),
        "od_b_a": sh(gs[ROW_B_A:ROW_B_A + 2]), "od_b_x": sh(gs[ROW_B_X:ROW_B_X + 2]), "od_lam": sh(gs[ROW_LAM:ROW_LAM + 2]),
    }
    small_in = {"ada_b": (ada_b, m_ada_b, v_ada_b), "ln_g": (ln_g, m_ln_g, v_ln_g), "ln_b": (ln_b, m_ln_b, v_ln_b),
                "ev_sink": (ev_sink, m_ev_sink, v_ev_sink), "ev_sg_ln_g": (ev_sg_ln_g, m_ev_sg_ln_g, v_ev_sg_ln_g),
                "ev_sg_ln_b": (ev_sg_ln_b, m_ev_sg_ln_b, v_ev_sg_ln_b), "ev_sg_b": (ev_sg_b, m_ev_sg_b, v_ev_sg_b),
                "od_conv_w": (od_conv_w, m_od_conv_w, v_od_conv_w), "od_conv_b": (od_conv_b, m_od_conv_b, v_od_conv_b),
                "od_b_a": (od_b_a, m_od_b_a, v_od_b_a), "od_b_x": (od_b_x, m_od_b_x, v_od_b_x),
                "od_lam": (od_lam, m_od_lam, v_od_lam)}
    names_small = list(small_g)
    upd = _adamw_small([(small_in[n][0].reshape(small_g[n].shape), small_g[n], small_in[n][1].reshape(small_g[n].shape),
                         small_in[n][2].reshape(small_g[n].shape)) for n in names_small])
    res = dict(big)
    for n, (d_, nm_, nv_) in zip(names_small, upd):
        shape = small_in[n][0].shape
        res[n] = tuple(a.reshape(shape) for a in (small_g[n], d_, nm_, nv_))

    order = ["ada_w", "ada_b", "ln_g", "ln_b", "ev_w_in", "ev_w_out", "ev_sink", "ev_sg_ln_g", "ev_sg_ln_b", "ev_sg_w", "ev_sg_b",
             "od_w_in", "od_conv_w", "od_conv_b", "od_w_a", "od_b_a", "od_w_x", "od_b_x", "od_lam", "od_w_out"]
    return (loss, grad_x.reshape(x.shape), *[res[n][0] for n in order], *[res[n][1] for n in order],
            *[res[n][2] for n in order], *[res[n][3] for n in order])
```

```python
import jax
import jax.numpy as jnp
import numpy as np
from jax import lax
from jax.experimental import pallas as pl
from jax.experimental.pallas import tpu as pltpu

F32 = jnp.float32
BF16 = jnp.bfloat16
MXU_DTYPE = BF16
ACT_DTYPE = MXU_DTYPE

D_MODEL = 1024
HEAD_DIM = 64
N_Q_HEADS = 8
Q_PER_KV = 4
ATTN_WIDTH = 512
BLK = 128
ROPE_DIM = 16
ROPE_THETA = 500000.0
N_SG_GROUPS = 8
SG_WIDTH = 512
EVEN_IN = 2816
ODD_IN = 2048
RNN_HEADS = 8
RG_LRU_C = 8.0
ALPHA = (2 * 2) ** 0.25
LN_EPS = 1e-5
NEG_INF = -1e30
ADAM_LR, ADAM_B1, ADAM_B2, ADAM_EPS, ADAM_WD, ADAM_STEP = 0.001, 0.9, 0.999, 1e-08, 0.01, 10

LANES = 128
VMEM_LIMIT = 56 * 1024 * 1024
MESH = pl.DeviceIdType.MESH


def _mm(a, b):
    return jnp.dot(a.astype(MXU_DTYPE), b.astype(MXU_DTYPE), preferred_element_type=F32)


def _mm_nt(a, b):
    return lax.dot_general(a.astype(MXU_DTYPE), b.astype(MXU_DTYPE), (((1,), (1,)), ((), ())), preferred_element_type=F32)


def _mm_tn(a, b):
    return lax.dot_general(a.astype(MXU_DTYPE), b.astype(MXU_DTYPE), (((0,), (0,)), ((), ())), preferred_element_type=F32)


def _sigmoid(x):
    return 1.0 / (1.0 + jnp.exp(-x))


def _ln_stats(z):
    mu = jnp.mean(z, axis=-1, keepdims=True)
    d = z - mu
    var = jnp.mean(d * d, axis=-1, keepdims=True)
    rstd = lax.rsqrt(var + LN_EPS)
    return d * rstd, rstd


def _ln_bwd(dout, zhat, rstd, g):
    dzh = dout * g
    m1 = jnp.mean(dzh, axis=-1, keepdims=True)
    m2 = jnp.mean(dzh * zhat, axis=-1, keepdims=True)
    return rstd * (dzh - m1 - zhat * m2)


def _group_sum(x, e2):
    hi = x.astype(MXU_DTYPE)
    lo = (x - hi.astype(F32)).astype(MXU_DTYPE)
    return jnp.dot(hi, e2, preferred_element_type=F32) + jnp.dot(lo, e2, preferred_element_type=F32)


def _lane_iota(shape):
    return lax.broadcasted_iota(jnp.int32, shape, 1)


def _to_kv_lanes(t, h):
    src_lo = (h % 2 == 0)
    dst_lo = (h // Q_PER_KV == 0)
    if src_lo != dst_lo:
        t = pltpu.roll(t, HEAD_DIM, 1)
    lane = _lane_iota(t.shape)
    keep = (lane < HEAD_DIM) if dst_lo else (lane >= HEAD_DIM)
    return jnp.where(keep, t, 0.0)


def _from_kv_lanes(t, h):
    src_lo = (h // Q_PER_KV == 0)
    dst_lo = (h % 2 == 0)
    lane = _lane_iota(t.shape)
    keep = (lane < HEAD_DIM) if src_lo else (lane >= HEAD_DIM)
    t = jnp.where(keep, t, 0.0)
    if src_lo != dst_lo:
        t = pltpu.roll(t, HEAD_DIM, 1)
    return t


def _rope(t, cos_t, sin_p, sin_m):
    half = ROPE_DIM // 2
    return t * cos_t + pltpu.roll(t, half, 1) * sin_p + pltpu.roll(t, LANES - half, 1) * sin_m


def _rope_t(d, cos_t, sin_p, sin_m):
    half = ROPE_DIM // 2
    return d * cos_t + pltpu.roll(d * sin_p, LANES - half, 1) + pltpu.roll(d * sin_m, half, 1)


def _band(ref, n, nb):
    prev = jnp.maximum(n - 1, 0)
    nxt = jnp.minimum(n + 1, nb - 1)
    rows = [ref[pl.ds(pl.multiple_of(j * BLK, BLK), BLK), :] for j in (prev, n, nxt)]
    return jnp.concatenate(rows, axis=0)


def _band_bias(n, seq):
    qi = lax.broadcasted_iota(jnp.int32, (BLK, 3 * BLK), 0)
    kj = lax.broadcasted_iota(jnp.int32, (BLK, 3 * BLK), 1)
    k_abs = n * BLK - BLK + kj
    valid = (jnp.abs(kj - BLK - qi) <= BLK) & (k_abs >= 0) & (k_abs < seq)
    bias = jnp.where(valid, 0.0, NEG_INF)
    return jnp.concatenate([bias] * Q_PER_KV, axis=0)


def _stack_heads(tile_of, kv):
    return jnp.concatenate([_to_kv_lanes(tile_of(h // 2), h) for h in range(Q_PER_KV * kv, Q_PER_KV * (kv + 1))], axis=0)


def _per_head_column(vals):
    row = lax.broadcasted_iota(jnp.int32, (Q_PER_KV * BLK, 1), 0)
    return jnp.where(row < BLK, vals[0], jnp.where(row < 2 * BLK, vals[1], jnp.where(row < 3 * BLK, vals[2], vals[3])))


def _softplus_neg(lam):
    e = jnp.exp(-jnp.abs(lam))
    u = 1.0 + e
    log1p_e = jnp.where(u == 1.0, e, jnp.log(u) * (e / (u - 1.0)))
    sp = jnp.maximum(-lam, 0.0) + log1p_e
    dsp = -1.0 / (1.0 + jnp.exp(lam))
    return sp, dsp


def _full(shape):
    return pl.BlockSpec(shape, lambda *_: (0,) * len(shape))


def _const(shape):
    return pl.BlockSpec(shape, lambda *_: (0,) * len(shape), pipeline_mode=pl.Buffered(1))


def _rows(tm, n):
    return pl.BlockSpec((tm, n), lambda i: (i, 0))


def _params(*sem):
    return pltpu.CompilerParams(dimension_semantics=sem, vmem_limit_bytes=VMEM_LIMIT)


def _sds(shape, dtype=F32):
    return jax.ShapeDtypeStruct(shape, dtype)


def _place():
    return lax.axis_index("x"), lax.axis_index("y"), lax.axis_index("c")


class _Rider:
    def __init__(self, ins, out_shapes, n_remote, n_local, plan):
        self.ins, self.out_shapes, self.n_remote, self.n_local, self.plan = list(ins), list(out_shapes), n_remote, n_local, plan

    def scratch(self):
        return [pltpu.SemaphoreType.DMA((self.n_remote,)), pltpu.SemaphoreType.DMA((self.n_remote,)),
                pltpu.SemaphoreType.DMA((max(self.n_local, 1),))]

    def run(self, first, in_refs, out_refs, sems):
        send_sems, recv_sems, local_sems = sems
        sends, recvs, locals_ = self.plan(in_refs, out_refs)
        remote = lambda k, src, dst, to: pltpu.make_async_remote_copy(
            src_ref=src, dst_ref=dst, send_sem=send_sems.at[k], recv_sem=recv_sems.at[k], device_id=to, device_id_type=MESH)
        if first:
            for k, src, dst, to in sends:
                remote(k, src, dst, to).start()
            for j, (src, dst) in enumerate(locals_):
                pltpu.make_async_copy(src, dst, local_sems.at[j]).start()
        else:
            for k, dst, frm in recvs:
                remote(k, dst, dst, frm).wait_recv()
            for k, src, dst, to in sends:
                remote(k, src, dst, to).wait_send()
            for j, (src, dst) in enumerate(locals_):
                pltpu.make_async_copy(src, dst, local_sems.at[j]).wait()


def _other_chips(x, y):
    return [(1 - x, y), (x, 1 - y), (1 - x, 1 - y)]


def _gather_rider(shard):
    hr = shard.shape[0] // 2

    def plan(ins, outs):
        x, y, c = _place()
        mine, src, dst = 2 * x + y, ins[0], outs[0]
        sends, recvs = [], []
        for j, (px, py) in enumerate(_other_chips(x, y)):
            for flip in range(2):
                tc = c if flip == 0 else 1 - c
                sends.append((2 * j + flip, src.at[pl.ds(c * hr, hr)], dst.at[mine, pl.ds(c * hr, hr)], (px, py, tc)))
                recvs.append((2 * j + flip, dst.at[2 * px + py, pl.ds(tc * hr, hr)], (px, py, tc)))
        return sends, recvs, [(src, dst.at[mine])]

    return _Rider([shard], [_sds((4,) + shard.shape, shard.dtype)], 6, 1, plan)


def _sibling_swap_rider(parts):
    n = len(parts)

    def plan(ins, outs):
        x, y, c = _place()
        sibling = (x, y, 1 - c)
        return ([(i, ins[i].at[:, 1 - c], outs[i], sibling) for i in range(n)], [(i, outs[i], sibling) for i in range(n)], [])

    return _Rider(parts, [_sds((4,) + p.shape[2:], p.dtype) for p in parts], n, 0, plan)


def _chip_exchange_rider(parts):
    n = len(parts)

    def plan(ins, outs):
        x, y, c = _place()
        mine = 2 * x + y
        sends, recvs = [], []
        for i in range(n):
            for j, (px, py) in enumerate(_other_chips(x, y)):
                sends.append((3 * i + j, ins[i].at[2 * px + py], outs[i].at[mine], (px, py, c)))
                recvs.append((3 * i + j, outs[i].at[2 * px + py], (px, py, c)))
        return sends, recvs, [(ins[i].at[mine], outs[i].at[mine]) for i in range(n)]

    return _Rider(parts, [_sds(p.shape, p.dtype) for p in parts], 3 * n, n, plan)


def _call(body, name, grid, in_specs, out_specs, out_shape, args, sem, scratch=(), rider=None):
    if rider is None:
        return list(pl.pallas_call(body, name=name, grid=grid, in_specs=in_specs, out_specs=out_specs, out_shape=out_shape,
                                   scratch_shapes=list(scratch), compiler_params=_params(sem))(*args)), []
    n_in, n_out, n_scr = len(in_specs), len(out_specs), len(scratch)
    r_in, r_out = len(rider.ins), len(rider.out_shapes)
    steps = grid[0]

    def riding(*refs):
        ins, r_ins = refs[:n_in], refs[n_in:n_in + r_in]
        outs = refs[n_in + r_in:n_in + r_in + n_out]
        r_outs = refs[n_in + r_in + n_out:n_in + r_in + n_out + r_out]
        scr = refs[n_in + r_in + n_out + r_out:n_in + r_in + n_out + r_out + n_scr]
        sems = refs[n_in + r_in + n_out + r_out + n_scr:]

        @pl.when(pl.program_id(0) == 0)
        def _():
            rider.run(True, r_ins, r_outs, sems)

        body(*ins, *outs, *scr)

        @pl.when(pl.program_id(0) == steps - 1)
        def _():
            rider.run(False, r_ins, r_outs, sems)

    hbm = pl.BlockSpec(memory_space=pl.ANY)
    res = pl.pallas_call(
        riding, name=name, grid=grid, in_specs=list(in_specs) + [hbm] * r_in, out_specs=list(out_specs) + [hbm] * r_out,
        out_shape=list(out_shape) + rider.out_shapes, scratch_shapes=list(scratch) + rider.scratch(),
        compiler_params=_params("arbitrary"),
    )(*args, *rider.ins)
    return list(res[:n_out]), list(res[n_out:])


def _row_tile(seq, want):
    return want if seq % want == 0 else seq


def _rope_tables(posf, seq):
    half = ROPE_DIM // 2
    inv_freq = np.power(np.float32(ROPE_THETA), -np.arange(half, dtype=np.float32) / np.float32(half)).astype(np.float32)
    j = np.arange(LANES) % HEAD_DIM
    invf = jnp.asarray(np.where(j < ROPE_DIM, inv_freq[j % half], 0.0).astype(np.float32).reshape(1, LANES))
    m_p = jnp.asarray(((j >= half) & (j < ROPE_DIM)).astype(np.float32).reshape(1, LANES))
    m_m = jnp.asarray(-(j < half).astype(np.float32).reshape(1, LANES))
    tm = _row_tile(seq, 512)

    def body(pos_ref, invf_ref, mp_ref, mm_ref, cos_ref, sp_ref, sm_ref):
        ang = pos_ref[...] * invf_ref[...]
        s = jnp.sin(ang)
        cos_ref[...] = jnp.cos(ang)
        sp_ref[...] = s * mp_ref[...]
        sm_ref[...] = s * mm_ref[...]

    return pl.pallas_call(
        body, name="rope_tables", grid=(seq // tm,),
        in_specs=[_rows(tm, 1), _full((1, LANES)), _full((1, LANES)), _full((1, LANES))],
        out_specs=[_rows(tm, LANES)] * 3, out_shape=[_sds((seq, LANES))] * 3,
        compiler_params=_params("parallel"),
    )(posf, invf, m_p, m_m)


def _even_proj(x, mod, w_in_t, tabs, seq, rider=None):
    tm = _row_tile(seq, 512)

    def body(x_ref, mod_ref, w_ref, cos_ref, sp_ref, sm_ref, q_ref, k_ref, v_ref, su_ref, sv_ref, g_ref):
        h = x_ref[...] * (1.0 + mod_ref[1:2, :]) + mod_ref[0:1, :]
        p = _mm_nt(h, w_ref[...])
        cos_t, sin_p, sin_m = cos_ref[...], sp_ref[...], sm_ref[...]
        for j in range(ATTN_WIDTH // LANES):
            q_ref[:, j * LANES:(j + 1) * LANES] = _rope(p[:, j * LANES:(j + 1) * LANES], cos_t, sin_p, sin_m).astype(q_ref.dtype)
        k_ref[...] = _rope(p[:, 512:640], cos_t, sin_p, sin_m).astype(k_ref.dtype)
        v_ref[...] = p[:, 640:768].astype(v_ref.dtype)
        su_ref[...] = p[:, 768:1280].astype(su_ref.dtype)
        sv_ref[...] = p[:, 1280:1792].astype(sv_ref.dtype)
        g_ref[...] = p[:, 1792:2816].astype(g_ref.dtype)

    return _call(
        body, "even_proj", (seq // tm,),
        [_rows(tm, D_MODEL), _full((3, D_MODEL)), _const((EVEN_IN, D_MODEL))] + [_rows(tm, LANES)] * 3,
        [_rows(tm, 512), _rows(tm, LANES), _rows(tm, LANES), _rows(tm, 512), _rows(tm, 512), _rows(tm, D_MODEL)],
        [_sds((seq, 512), MXU_DTYPE), _sds((seq, LANES), MXU_DTYPE), _sds((seq, LANES), MXU_DTYPE), _sds((seq, 512), ACT_DTYPE),
         _sds((seq, 512), ACT_DTYPE), _sds((seq, D_MODEL), ACT_DTYPE)],
        (x, mod, w_in_t, *tabs), "parallel", rider=rider)


def _sg_forward(sv, lng, lnb, sgw_ref, sgb, e2):
    vn, vhat, rstd, svo = [], [], [], []
    for j in range(SG_WIDTH // LANES):
        t = sv[:, j * LANES:(j + 1) * LANES]
        mu = _group_sum(t, e2) * (1.0 / HEAD_DIM)
        d = t - mu
        var = _group_sum(d * d, e2) * (1.0 / HEAD_DIM)
        r = lax.rsqrt(var + LN_EPS)
        vh = d * r
        vhat.append(vh)
        rstd.append(r)
        vn.append(vh * lng[:, j * LANES:(j + 1) * LANES] + lnb[:, j * LANES:(j + 1) * LANES])
    lane = _lane_iota((BLK, LANES))
    for j in range(SG_WIDTH // LANES):
        lo = _mm(sgw_ref[2 * j], vn[j])
        hi = _mm(sgw_ref[2 * j + 1], vn[j])
        svo.append(jnp.where(lane < HEAD_DIM, lo, hi) + sgb[:, j * LANES:(j + 1) * LANES])
    return svo, vn, vhat, rstd


def _even_mix(q, k, v, su, sv, sink, sgln_g, sgln_b, sgw, sgb_full, e2, seq, rider=None):
    nb = seq // BLK

    def body(sink_ref, q_ref, k_ref, v_ref, su_ref, sv_ref, lng_ref, lnb_ref, sgw_ref, sgb_ref, e2_ref, ycat_ref, lse_ref,
             svo_ref, vhat_ref, rstd_ref):
        n = pl.program_id(0)
        kband = _band(k_ref, n, nb)
        vband = _band(v_ref, n, nb)
        bias = _band_bias(n, seq)
        lane = _lane_iota((BLK, LANES))
        lse = jnp.zeros((BLK, LANES), F32)
        q_tile = lambda j: q_ref[:, j * LANES:(j + 1) * LANES].astype(F32)
        acc = [jnp.zeros((BLK, LANES), F32) for _ in range(ATTN_WIDTH // LANES)]
        for kv in range(N_Q_HEADS // Q_PER_KV):
            heads = range(Q_PER_KV * kv, Q_PER_KV * (kv + 1))
            sink = _per_head_column([sink_ref[h] for h in heads])
            s = _mm_nt(_stack_heads(q_tile, kv), kband) * (HEAD_DIM ** -0.5) + bias
            m = jnp.maximum(jnp.max(s, axis=1, keepdims=True), sink)
            p = jnp.exp(s - m)
            denom = jnp.sum(p, axis=1, keepdims=True) + jnp.exp(sink - m)
            o4 = _mm(p / denom, vband)
            l4 = m + jnp.log(denom)
            for g, h in enumerate(heads):
                acc[h // 2] = acc[h // 2] + _from_kv_lanes(o4[g * BLK:(g + 1) * BLK], h)
                lse = jnp.where(lane == h, l4[g * BLK:(g + 1) * BLK], lse)
        for j in range(ATTN_WIDTH // LANES):
            ycat_ref[:, j * LANES:(j + 1) * LANES] = acc[j].astype(ycat_ref.dtype)
        lse_ref[...] = lse
        svo, _, vhat, rstd = _sg_forward(sv_ref[...].astype(F32), lng_ref[...], lnb_ref[...], sgw_ref, sgb_ref[...], e2_ref[...])
        for j in range(SG_WIDTH // LANES):
            cs = slice(j * LANES, (j + 1) * LANES)
            ysg = su_ref[:, cs].astype(F32) * svo[j]
            ycat_ref[:, ATTN_WIDTH + j * LANES:ATTN_WIDTH + (j + 1) * LANES] = ysg.astype(ycat_ref.dtype)
            svo_ref[:, cs], vhat_ref[:, cs], rstd_ref[:, cs] = (t.astype(svo_ref.dtype) for t in (svo[j], vhat[j], rstd[j]))

    blk = lambda w: pl.BlockSpec((BLK, w), lambda n: (n, 0))
    return _call(
        body, "even_mix", (nb,),
        [pl.BlockSpec(memory_space=pltpu.SMEM), blk(512), _full((seq, LANES)), _full((seq, LANES)), blk(512), blk(512),
         _full((1, 512)), _full((1, 512)), _full((8, BLK, BLK)), _full((BLK, 512)), _full((LANES, LANES))],
        [blk(D_MODEL), blk(LANES)] + [blk(SG_WIDTH)] * 3,
        [_sds((seq, D_MODEL), ACT_DTYPE), _sds((seq, LANES))] + [_sds((seq, SG_WIDTH), ACT_DTYPE)] * 3,
        (sink, q, k, v, su, sv, sgln_g, sgln_b, sgw, sgb_full, e2), "parallel", rider=rider)


def _even_out(ycat, g, x, mod, mod_next, w_out, w_in4_next, ln_g, ln_b, seq, rider=None):
    tm = _row_tile(seq, 512)
    cs = ODD_IN // 4

    def body(y_ref, g_ref, x_ref, mod_ref, modn_ref, wo_ref, wi_ref, g1_ref, b1_ref, zhat_ref, rstd_ref, x1_ref, xr_ref, gn_ref):
        gg = g_ref[...].astype(F32)
        out = _mm(y_ref[...].astype(F32) * (gg * _sigmoid(gg)), wo_ref[...])
        z = ALPHA * x_ref[...] + mod_ref[2:3, :] * out
        zhat, rstd = _ln_stats(z)
        zhat_ref[...] = zhat
        rstd_ref[...] = rstd
        x1 = zhat * g1_ref[...] + b1_ref[...]
        x1_ref[...] = x1
        hb = (x1 * (1.0 + modn_ref[1:2, :]) + modn_ref[0:1, :]).astype(MXU_DTYPE)
        for s in range(2):
            xr_ref[:, s * cs:(s + 1) * cs] = jnp.dot(hb, wi_ref[s], preferred_element_type=F32)
            gn_ref[:, s * cs:(s + 1) * cs] = jnp.dot(hb, wi_ref[2 + s], preferred_element_type=F32).astype(gn_ref.dtype)

    return _call(
        body, "even_out", (seq // tm,),
        [_rows(tm, D_MODEL)] * 3 + [_full((3, D_MODEL)), _full((3, D_MODEL)), _const((D_MODEL, D_MODEL)), _const((4, D_MODEL, cs)),
                                    _full((1, D_MODEL)), _full((1, D_MODEL))],
        [_rows(tm, D_MODEL), _rows(tm, 1)] + [_rows(tm, D_MODEL)] * 3,
        [_sds((seq, D_MODEL)), _sds((seq, 1))] + [_sds((seq, D_MODEL))] * 2 + [_sds((seq, D_MODEL), ACT_DTYPE)],
        (ycat, g, x, mod, mod_next, w_out, w_in4_next, ln_g, ln_b), "parallel", rider=rider)


def _halo_specs(tm, seq, width, order=lambda i: i):
    per = tm // 8
    last = seq // 8 - 1
    return [pl.BlockSpec((8, width), lambda i: (jnp.maximum(order(i) * per - 1, 0), 0)),
            pl.BlockSpec((tm, width), lambda i: (order(i), 0)),
            pl.BlockSpec((8, width), lambda i: (jnp.minimum((order(i) + 1) * per, last), 0))]


def _extended(prev_ref, main_ref, next_ref, i, n_steps):
    prev = jnp.where(i > 0, prev_ref[...], 0.0)
    nxt = jnp.where(i < n_steps - 1, next_ref[...], 0.0)
    return jnp.concatenate([prev, main_ref[...], nxt], axis=0)


def _shifted(ext, off, tm):
    if off == 0:
        return ext[8:8 + tm]
    return pltpu.roll(ext, (-off) % ext.shape[0], 0)[8:8 + tm]


SCAN_SUB = 8


def _lru_gate(xh, pre, bias, sp, hs, d):
    r = _sigmoid(pre[:, 0:LANES] + bias[2 * d:2 * d + 1, hs])
    ig = _sigmoid(pre[:, LANES:2 * LANES] + bias[2 * d + 1:2 * d + 2, hs])
    neg_log_a = RG_LRU_C * r * sp[d:d + 1, hs]
    a = jnp.exp(-neg_log_a)
    u = jnp.tanh(neg_log_a) * (a * a + 1.0)
    inv_s = lax.rsqrt(jnp.maximum(u, jnp.finfo(F32).tiny))
    return r, ig, a, u * inv_s, inv_s


def _conv_block(xp_ref, xm_ref, xn_ref, cw_ref, cb_ref, blk, steps, tm):
    ext = _extended(xp_ref, xm_ref, xn_ref, blk, steps)
    return cb_ref[...] + sum(cw_ref[kk:kk + 1, :] * _shifted(ext, kk - 2, tm) for kk in range(4))


def _scan_tiles(a_ref, b_ref, h_ref, hprev_ref, carry_h, carry_a, rows, descending, post):
    sub = SCAN_SUB
    tiles = rows // sub
    row = lax.broadcasted_iota(jnp.int32, (sub, D_MODEL), 0)

    def shift(v, d, fill):
        if descending:
            return jnp.where(row <= sub - 1 - d, pltpu.roll(v, sub - d, 0), fill)
        return jnp.where(row >= d, pltpu.roll(v, d, 0), fill)

    def last(v):
        return jnp.broadcast_to(v[0:1, :] if descending else v[sub - 1:sub, :], v.shape)

    def tile(j, c):
        ch, ca = c
        r0 = pl.multiple_of(((tiles - 1 - j) if descending else j) * sub, sub)
        at = a_ref[pl.ds(r0, sub), :]
        bt = b_ref[pl.ds(r0, sub), :]
        coef = shift(at, 1, ca) if post else at
        acc_a, acc_b = coef, bt
        for d in (1, 2, 4):
            acc_b = acc_b + acc_a * shift(acc_b, d, 0.0)
            acc_a = acc_a * shift(acc_a, d, 1.0)
        h = acc_b + acc_a * ch
        h_ref[pl.ds(r0, sub), :] = h
        if post:
            return last(h), last(at)
        hprev_ref[pl.ds(r0, sub), :] = shift(h, 1, ch)
        return last(h), ca

    ch, ca = lax.fori_loop(0, tiles, tile, (carry_h[...], carry_a[...]), unroll=4)
    carry_h[...] = ch
    carry_a[...] = ca


def _lru_fwd(xr, xc, conv_w, conv_b, wcat, bias, lam, seq, d):
    tb = _row_tile(seq, 512)
    steps = seq // tb
    descending = d == 1
    order = (lambda i: steps - 1 - i) if descending else (lambda i: i)
    with_conv = xc is None
    n_x = 5 if with_conv else 1

    def body(*refs):
        x_refs, (w_ref, bias_ref, lam_ref) = refs[:n_x], refs[n_x:n_x + 3]
        h_ref, hp_ref, a_ref, r_ref, i_ref, s_ref, q_ref = refs[n_x + 3:n_x + 10]
        b_scr, carry_h, carry_a = refs[-3:]
        i = pl.program_id(0)

        @pl.when(i == 0)
        def _():
            carry_h[...] = jnp.zeros_like(carry_h)
            carry_a[...] = jnp.zeros_like(carry_a)

        if with_conv:
            xc_ref = refs[n_x + 10]
            xc_ref[...] = _conv_block(*x_refs, order(i), steps, tb)
        else:
            xc_ref = x_refs[0]
        sp, _ = _softplus_neg(lam_ref[...])
        bias = bias_ref[...]
        for h in range(RNN_HEADS):
            hs = slice(h * LANES, (h + 1) * LANES)
            xh = xc_ref[:, hs]
            r, ig, a, s, q = _lru_gate(xh, _mm(xh, w_ref[h, :, 2 * d * LANES:2 * (d + 1) * LANES]), bias, sp, hs, d)
            a_ref[:, hs] = a
            b_scr[:, hs] = s * ig * xh
            for ref, val in ((r_ref, r), (i_ref, ig), (s_ref, s), (q_ref, q)):
                ref[:, hs] = val.astype(ref.dtype)
        _scan_tiles(a_ref, b_scr, h_ref, hp_ref, carry_h, carry_a, tb, descending, post=False)

    row_spec = pl.BlockSpec((tb, D_MODEL), lambda i: (order(i), 0))
    if with_conv:
        x_specs, x_args = _halo_specs(tb, seq, D_MODEL, order) + [_full((4, D_MODEL)), _full((1, D_MODEL))], (xr, xr, xr, conv_w, conv_b)
    else:
        x_specs, x_args = [row_spec], (xc,)
    n_out = 8 if with_conv else 7
    return pl.pallas_call(
        body, name="lru_fwd_%d" % d, grid=(steps,),
        in_specs=x_specs + [_full((8, LANES, 512)), _full((4, D_MODEL)), _full((2, D_MODEL))],
        out_specs=[row_spec] * n_out,
        out_shape=[_sds((seq, D_MODEL))] * 3 + [_sds((seq, D_MODEL), ACT_DTYPE)] * 4 + [_sds((seq, D_MODEL))] * (n_out - 7),
        scratch_shapes=[pltpu.VMEM((tb, D_MODEL), F32)] + [pltpu.VMEM((SCAN_SUB, D_MODEL), F32)] * 2,
        compiler_params=_params("arbitrary"),
    )(*x_args, wcat, bias, lam)


def _odd_out_and_loss(hf, hr, g, x1, tgt, mod, w_out, ln_g, ln_b, seq):
    tm = _row_tile(seq, 512)

    def body(hf_ref, hr_ref, g_ref, x_ref, t_ref, mod_ref, w_ref, lg_ref, lb_ref,
             dhs_ref, dg_ref, dres_ref, loss_ref, dw_ref, vec_ref):
        @pl.when(pl.program_id(0) == 0)
        def _():
            loss_ref[...] = jnp.zeros_like(loss_ref)
            dw_ref[...] = jnp.zeros_like(dw_ref)
            vec_ref[...] = jnp.zeros_like(vec_ref)

        gg = g_ref[...].astype(F32)
        sg = _sigmoid(gg)
        silu = gg * sg
        hsum = hf_ref[...] + hr_ref[...]
        y = hsum * silu
        out = _mm(y, w_ref[...])
        gate = mod_ref[2:3, :]
        z = ALPHA * x_ref[...] + gate * out
        zhat, rstd = _ln_stats(z)
        x2 = zhat * lg_ref[...] + lb_ref[...]
        err = x2 - t_ref[...]
        loss_ref[...] += 0.5 * jnp.sum(jnp.mean(err * err, axis=-1, keepdims=True))
        dx2 = err * (1.0 / D_MODEL)
        dz = _ln_bwd(dx2, zhat, rstd, lg_ref[...])
        vec_ref[0:1, :] += jnp.sum(dx2 * zhat, axis=0, keepdims=True)
        vec_ref[1:2, :] += jnp.sum(dx2, axis=0, keepdims=True)
        vec_ref[2:3, :] += jnp.sum(dz * out, axis=0, keepdims=True)
        dres_ref[...] = ALPHA * dz
        dout = gate * dz
        dw_ref[...] += _mm_tn(y, dout)
        dy = _mm_nt(dout, w_ref[...])
        dhs_ref[...] = dy * silu
        dg_ref[...] = (dy * hsum * (sg * (1.0 + gg * (1.0 - sg)))).astype(dg_ref.dtype)

    return pl.pallas_call(
        body, name="odd_out_loss", grid=(seq // tm,),
        in_specs=[_rows(tm, D_MODEL)] * 5 + [_full((3, D_MODEL)), _const((D_MODEL, D_MODEL)),
                                             _full((1, D_MODEL)), _full((1, D_MODEL))],
        out_specs=[_rows(tm, D_MODEL)] * 3 + [_full((8, LANES)), _full((D_MODEL, D_MODEL)), _full((8, D_MODEL))],
        out_shape=[_sds((seq, D_MODEL)), _sds((seq, D_MODEL), ACT_DTYPE), _sds((seq, D_MODEL)), _sds((8, LANES)),
                   _sds((D_MODEL, D_MODEL)), _sds((8, D_MODEL))],
        compiler_params=_params("arbitrary"),
    )(hf, hr, g, x1, tgt, mod, w_out, ln_g, ln_b)


def _lru_bwd(xc, dhs, hprev, a_d, r_d, i_d, s_d, q_d, wcat, lam, seq, d):
    tb = _row_tile(seq, 512)
    steps = seq // tb
    descending = d == 0
    order = (lambda i: steps - 1 - i) if descending else (lambda i: i)
    cols = slice(2 * d * LANES, 2 * (d + 1) * LANES)

    def body(xc_ref, dhs_ref, hp_ref, a_ref, r_ref, i_ref, s_ref, q_ref, w_ref, lam_ref, dxc_ref, dw_ref, vec_ref,
             g_scr, carry_h, carry_a):
        i = pl.program_id(0)

        @pl.when(i == 0)
        def _():
            dw_ref[...] = jnp.zeros_like(dw_ref)
            vec_ref[...] = jnp.zeros_like(vec_ref)
            carry_h[...] = jnp.zeros_like(carry_h)
            carry_a[...] = jnp.zeros_like(carry_a)

        sp, dsp = _softplus_neg(lam_ref[...])
        _scan_tiles(a_ref, dhs_ref, g_scr, None, carry_h, carry_a, tb, descending, post=True)
        for h in range(RNN_HEADS):
            hs = slice(h * LANES, (h + 1) * LANES)
            xh, a = xc_ref[:, hs], a_ref[:, hs]
            r, ig, s = r_ref[:, hs].astype(F32), i_ref[:, hs].astype(F32), s_ref[:, hs].astype(F32)
            db = g_scr[:, hs]
            da = db * hp_ref[:, hs]
            dlog_a = da * a - (db * ig * xh) * (a * a * q_ref[:, hs].astype(F32))
            dpr = dlog_a * (-RG_LRU_C) * sp[d:d + 1, hs] * r * (1.0 - r)
            dpi = db * s * xh * ig * (1.0 - ig)
            vec_ref[0:1, hs] += jnp.sum(dpr, axis=0, keepdims=True)
            vec_ref[1:2, hs] += jnp.sum(dpi, axis=0, keepdims=True)
            vec_ref[2:3, hs] += jnp.sum(dlog_a * r, axis=0, keepdims=True) * (-RG_LRU_C) * dsp[d:d + 1, hs]
            dcat = jnp.concatenate([dpr, dpi], axis=1)
            dw_ref[h] += _mm_tn(xh, dcat)
            dxc_ref[:, hs] = db * s * ig + _mm_nt(dcat, w_ref[h, :, cols])

    row_spec = pl.BlockSpec((tb, D_MODEL), lambda i: (order(i), 0))
    return pl.pallas_call(
        body, name="lru_bwd_%d" % d, grid=(steps,),
        in_specs=[row_spec] * 8 + [_full((8, LANES, 512)), _full((2, D_MODEL))],
        out_specs=[row_spec, _full((8, LANES, 2 * LANES)), _full((8, D_MODEL))],
        out_shape=[_sds((seq, D_MODEL)), _sds((8, LANES, 2 * LANES)), _sds((8, D_MODEL))],
        scratch_shapes=[pltpu.VMEM((tb, D_MODEL), F32)] + [pltpu.VMEM((SCAN_SUB, D_MODEL), F32)] * 2,
        compiler_params=_params("arbitrary"),
    )(xc, dhs, hprev, a_d, r_d, i_d, s_d, q_d, wcat, lam)


def _odd_proj_bwd(dxc_f, dxc_r, xr, dg, x1, dres, mod, conv_w, w_in4, seq):
    tm = _row_tile(seq, 512)
    steps = seq // tm

    def body(fp_ref, fm_ref, fn_ref, rp_ref, rm_ref, rn_ref, xp_ref, xm_ref, xn_ref, dg_ref, x_ref, dres_ref, mod_ref, cw_ref,
             w_ref, dx_ref, dw_ref, vec_ref, dpb_ref):
        i = pl.program_id(0)

        @pl.when(i == 0)
        def _():
            vec_ref[...] = jnp.zeros_like(vec_ref)
            dw_ref[...] = jnp.zeros_like(dw_ref)

        dxc_m = fm_ref[...] + rm_ref[...]
        dext = jnp.concatenate([jnp.where(i > 0, fp_ref[...] + rp_ref[...], 0.0), dxc_m,
                                jnp.where(i < steps - 1, fn_ref[...] + rn_ref[...], 0.0)], axis=0)
        xext = _extended(xp_ref, xm_ref, xn_ref, i, steps)
        dxr = sum(cw_ref[kk:kk + 1, :] * _shifted(dext, 2 - kk, tm) for kk in range(4))
        for kk in range(4):
            vec_ref[kk:kk + 1, :] += jnp.sum(dxc_m * _shifted(xext, kk - 2, tm), axis=0, keepdims=True)
        vec_ref[4:5, :] += jnp.sum(dxc_m, axis=0, keepdims=True)
        dpb_ref[:, :D_MODEL] = dxr.astype(dpb_ref.dtype)
        dpb_ref[:, D_MODEL:] = dg_ref[...].astype(dpb_ref.dtype)
        cs = ODD_IN // 4
        dh = sum(_mm_nt(dpb_ref[:, s * cs:(s + 1) * cs], w_ref[s]) for s in range(4))
        x = x_ref[...]
        h_t = (x * (1.0 + mod_ref[1:2, :]) + mod_ref[0:1, :]).T.astype(MXU_DTYPE)
        for s in range(4):
            dw_ref[s] += jnp.dot(h_t, dpb_ref[:, s * cs:(s + 1) * cs], preferred_element_type=F32)
        vec_ref[5:6, :] += jnp.sum(dh, axis=0, keepdims=True)
        vec_ref[6:7, :] += jnp.sum(dh * x, axis=0, keepdims=True)
        dx_ref[...] = dres_ref[...] + dh * (1.0 + mod_ref[1:2, :])

    return pl.pallas_call(
        body, name="odd_proj_bwd", grid=(steps,),
        in_specs=_halo_specs(tm, seq, D_MODEL) * 3 + [_rows(tm, D_MODEL)] * 3
        + [_full((3, D_MODEL)), _full((4, D_MODEL)), _const((4, D_MODEL, ODD_IN // 4))],
        out_specs=[_rows(tm, D_MODEL), _const((4, D_MODEL, ODD_IN // 4)), _full((8, D_MODEL))],
        out_shape=[_sds((seq, D_MODEL)), _sds((4, D_MODEL, ODD_IN // 4)), _sds((8, D_MODEL))],
        scratch_shapes=[pltpu.VMEM((tm, ODD_IN), MXU_DTYPE)],
        compiler_params=_params("arbitrary"),
    )(dxc_f, dxc_f, dxc_f, dxc_r, dxc_r, dxc_r, xr, xr, xr, dg, x1, dres, mod, conv_w, w_in4)


def _even_out_bwd(dx1, zhat, rstd, ycat, g, mod, ln_g, w_out, seq, rider=None):
    tm = _row_tile(seq, 512)
    steps = seq // tm

    def body(dx_ref, zh_ref, rs_ref, y_ref, g_ref, mod_ref, lg_ref, w_ref, dy_ref, dg_ref, dres_ref, dw_ref, vec_ref):
        i = pl.program_id(0)

        @pl.when(i == 0)
        def _():
            dw_ref[...] = jnp.zeros_like(dw_ref)
            vec_ref[...] = jnp.zeros_like(vec_ref)

        zhat = zh_ref[...]
        dx1_ = dx_ref[...]
        dz = _ln_bwd(dx1_, zhat, rs_ref[...], lg_ref[...])
        vec_ref[0:1, :] += jnp.sum(dx1_ * zhat, axis=0, keepdims=True)
        vec_ref[1:2, :] += jnp.sum(dx1_, axis=0, keepdims=True)
        dres_ref[...] = ALPHA * dz
        gate = mod_ref[2:3, :]
        gg = g_ref[...].astype(F32)
        sg = _sigmoid(gg)
        silu = gg * sg
        ycat_ = y_ref[...].astype(F32)
        dw_ref[...] += _mm_tn(ycat_ * silu, dz)
        dy = _mm_nt(gate * dz, w_ref[...])
        dy_ref[...] = (dy * silu).astype(dy_ref.dtype)
        dg_ref[...] = (dy * ycat_ * (sg * (1.0 + gg * (1.0 - sg)))).astype(dg_ref.dtype)

        @pl.when(i == steps - 1)
        def _():
            m_acc = dw_ref[...]
            vec_ref[2:3, :] = jnp.sum(w_ref[...].astype(F32) * m_acc, axis=0, keepdims=True)
            dw_ref[...] = m_acc * gate

    return _call(
        body, "even_out_bwd", (steps,),
        [_rows(tm, D_MODEL), _rows(tm, D_MODEL), _rows(tm, 1), _rows(tm, D_MODEL), _rows(tm, D_MODEL), _full((3, D_MODEL)),
         _full((1, D_MODEL)), _const((D_MODEL, D_MODEL))],
        [_rows(tm, D_MODEL)] * 3 + [_full((D_MODEL, D_MODEL)), _full((8, D_MODEL))],
        [_sds((seq, D_MODEL), ACT_DTYPE), _sds((seq, D_MODEL), ACT_DTYPE), _sds((seq, D_MODEL)), _sds((D_MODEL, D_MODEL)),
         _sds((8, D_MODEL))],
        (dx1, zhat, rstd, ycat, g, mod, ln_g, w_out), "arbitrary", rider=rider)


def _even_mix_bwd(q, k, v, lse, ycat, dycat, su, svo_s, vhat_s, rstd_s, sink, sgln_g, sgln_b, sgw, e2, e8, seq, rider=None):
    nb = seq // BLK

    def body(sink_ref, q_ref, k_ref, v_ref, lse_ref, y_ref, dy_ref, su_ref, svo_ref, vhat_ref, rstd_ref, lng_ref, lnb_ref, sgw_ref,
             e2_ref, e8_ref, dq_ref, dsu_ref, dsv_ref, dk_ref, dv_ref, dsgw_ref, dsgb_ref, vec_ref, dsink_ref, dsgb_acc):
        n = pl.program_id(0)

        @pl.when(n == 0)
        def _():
            dk_ref[...] = jnp.zeros_like(dk_ref)
            dv_ref[...] = jnp.zeros_like(dv_ref)
            dsgw_ref[...] = jnp.zeros_like(dsgw_ref)
            dsgb_acc[...] = jnp.zeros_like(dsgb_acc)
            vec_ref[...] = jnp.zeros_like(vec_ref)
            dsink_ref[...] = jnp.zeros_like(dsink_ref)

        kband = _band(k_ref, n, nb)
        vband = _band(v_ref, n, nb)
        bias = _band_bias(n, seq)
        lane = _lane_iota((BLK, LANES))
        row8 = lax.broadcasted_iota(jnp.int32, (8, LANES), 0)
        lse = lse_ref[...]
        dkb = jnp.zeros((LANES, 3 * BLK), F32)
        dvb = jnp.zeros((LANES, 3 * BLK), F32)
        dsink = jnp.zeros((8, LANES), F32)
        q_tile = lambda j: q_ref[:, j * LANES:(j + 1) * LANES].astype(F32)
        do_tile = lambda j: dy_ref[:, j * LANES:(j + 1) * LANES].astype(F32)
        dq = [jnp.zeros((BLK, LANES), F32) for _ in range(ATTN_WIDTH // LANES)]
        for kv in range(N_Q_HEADS // Q_PER_KV):
            heads = range(Q_PER_KV * kv, Q_PER_KV * (kv + 1))
            lse4, delta4 = [], []
            for h in heads:
                head_lanes = (lane < HEAD_DIM) if h % 2 == 0 else (lane >= HEAD_DIM)
                lse4.append(jnp.sum(jnp.where(lane == h, lse, 0.0), axis=1, keepdims=True))
                o_tile = y_ref[:, (h // 2) * LANES:(h // 2 + 1) * LANES].astype(F32)
                delta4.append(jnp.sum(jnp.where(head_lanes, do_tile(h // 2) * o_tile, 0.0), axis=1, keepdims=True))
            lse4, delta4 = jnp.concatenate(lse4, axis=0), jnp.concatenate(delta4, axis=0)
            q4, do4 = _stack_heads(q_tile, kv), _stack_heads(do_tile, kv)
            s = _mm_nt(q4, kband) * (HEAD_DIM ** -0.5) + bias
            p = jnp.exp(s - lse4)
            wsink = jnp.exp(_per_head_column([sink_ref[h] for h in heads]) - lse4) * delta4
            ds = p * (_mm_nt(do4, vband) - delta4) * (HEAD_DIM ** -0.5)
            dq4 = _mm(ds, kband)
            dkb = dkb + _mm_tn(q4, ds)
            dvb = dvb + _mm_tn(do4, p)
            for g, h in enumerate(heads):
                dq[h // 2] = dq[h // 2] + _from_kv_lanes(dq4[g * BLK:(g + 1) * BLK], h)
                dsink = dsink + jnp.where(row8 == h, -jnp.sum(wsink[g * BLK:(g + 1) * BLK]), 0.0)
        for j in range(ATTN_WIDTH // LANES):
            dq_ref[:, j * LANES:(j + 1) * LANES] = dq[j].astype(dq_ref.dtype)
        dsink_ref[...] += dsink
        prev = jnp.maximum(n - 1, 0)
        nxt = jnp.minimum(n + 1, nb - 1)
        for part, blk_i in enumerate((prev, n, nxt)):
            rows = pl.ds(pl.multiple_of(blk_i * BLK, BLK), BLK)
            dk_ref[rows, :] += dkb[:, part * BLK:(part + 1) * BLK].T
            dv_ref[rows, :] += dvb[:, part * BLK:(part + 1) * BLK].T

        e2 = e2_ref[...]
        lng, lnb = lng_ref[...], lnb_ref[...]
        for j in range(SG_WIDTH // LANES):
            cs = slice(j * LANES, (j + 1) * LANES)
            vhat = vhat_ref[:, cs].astype(F32)
            vn = vhat * lng[:, cs] + lnb[:, cs]
            dysg = dy_ref[:, ATTN_WIDTH + j * LANES:ATTN_WIDTH + (j + 1) * LANES].astype(F32)
            dsu_ref[:, cs] = (dysg * svo_ref[:, cs].astype(F32)).astype(dsu_ref.dtype)
            dsvo = dysg * su_ref[:, cs].astype(F32)
            dsgb_acc[:, cs] += dsvo
            d_lo = jnp.where(lane < HEAD_DIM, dsvo, 0.0)
            d_hi = dsvo - d_lo
            dsgw_ref[2 * j] += _mm_nt(d_lo, vn)
            dsgw_ref[2 * j + 1] += _mm_nt(d_hi, vn)
            dvn = _mm_tn(sgw_ref[2 * j], d_lo) + _mm_tn(sgw_ref[2 * j + 1], d_hi)
            vec_ref[0:1, cs] += jnp.sum(dvn * vhat, axis=0, keepdims=True)
            vec_ref[1:2, cs] += jnp.sum(dvn, axis=0, keepdims=True)
            dvh = dvn * lng[:, cs]
            m1 = _group_sum(dvh, e2) * (1.0 / HEAD_DIM)
            m2 = _group_sum(dvh * vhat, e2) * (1.0 / HEAD_DIM)
            dsv_ref[:, cs] = (rstd_ref[:, cs].astype(F32) * (dvh - m1 - vhat * m2)).astype(dsv_ref.dtype)

        @pl.when(n == nb - 1)
        def _():
            rest = dsgb_acc[...]
            total = jnp.zeros((8, BLK), F32)
            for _ in range(3):
                part = rest.astype(MXU_DTYPE)
                total = total + lax.dot_general(e8_ref[...], part, (((1,), (1,)), ((), ())), preferred_element_type=F32)
                rest = rest - part.astype(F32)
            dsgb_ref[...] = total

    blk = lambda w: pl.BlockSpec((BLK, w), lambda n: (n, 0))
    return _call(
        body, "even_mix_bwd", (nb,),
        [pl.BlockSpec(memory_space=pltpu.SMEM), blk(512), _full((seq, LANES)), _full((seq, LANES)), blk(LANES),
         blk(D_MODEL), blk(D_MODEL), blk(512), blk(512), blk(512), blk(512), _full((1, 512)), _full((1, 512)), _full((8, BLK, BLK)),
         _full((LANES, LANES)), _full((8, 512))],
        [blk(512), blk(512), blk(512), _full((seq, LANES)), _full((seq, LANES)), _full((8, BLK, BLK)),
         _full((8, BLK)), _full((8, 512)), _full((8, LANES))],
        [_sds((seq, 512), ACT_DTYPE), _sds((seq, 512), ACT_DTYPE), _sds((seq, 512), ACT_DTYPE), _sds((seq, LANES)), _sds((seq, LANES)),
         _sds((8, BLK, BLK)), _sds((8, BLK)), _sds((8, 512)), _sds((8, LANES))],
        (sink, q, k, v, lse, ycat, dycat, su, svo_s, vhat_s, rstd_s, sgln_g, sgln_b, sgw, e2, e8), "arbitrary",
        scratch=[pltpu.VMEM((BLK, 512), F32)], rider=rider)


def _even_proj_bwd(dq, dk, dv, dsu, dsv, dg, x, dres, mod, tabs, w_in_t, seq):
    tm = _row_tile(seq, 512)

    def body(dq_ref, dk_ref, dv_ref, dsu_ref, dsv_ref, dg_ref, x_ref, dres_ref, mod_ref, cos_ref, sp_ref, sm_ref, wt_ref,
             dx_ref, dw_ref, vec_ref, dpb_ref):
        @pl.when(pl.program_id(0) == 0)
        def _():
            vec_ref[...] = jnp.zeros_like(vec_ref)
            dw_ref[...] = jnp.zeros_like(dw_ref)

        cos_t, sin_p, sin_m = cos_ref[...], sp_ref[...], sm_ref[...]
        dt = dpb_ref.dtype
        for j in range(ATTN_WIDTH // LANES):
            cs = slice(j * LANES, (j + 1) * LANES)
            dpb_ref[:, cs] = _rope_t(dq_ref[:, cs].astype(F32), cos_t, sin_p, sin_m).astype(dt)
        dpb_ref[:, 512:640] = _rope_t(dk_ref[...], cos_t, sin_p, sin_m).astype(dt)
        dpb_ref[:, 640:768] = dv_ref[...].astype(dt)
        dpb_ref[:, 768:1280] = dsu_ref[...].astype(dt)
        dpb_ref[:, 1280:1792] = dsv_ref[...].astype(dt)
        dpb_ref[:, 1792:2816] = dg_ref[...].astype(dt)
        dpb = dpb_ref[...]
        dh = jnp.dot(dpb, wt_ref[...], preferred_element_type=F32)
        x_ = x_ref[...]
        hb = (x_ * (1.0 + mod_ref[1:2, :]) + mod_ref[0:1, :]).astype(MXU_DTYPE)
        dw_ref[...] += _mm_tn(dpb, hb)
        vec_ref[0:1, :] += jnp.sum(dh, axis=0, keepdims=True)
        vec_ref[1:2, :] += jnp.sum(dh * x_, axis=0, keepdims=True)
        dx_ref[...] = dres_ref[...] + dh * (1.0 + mod_ref[1:2, :])

    return pl.pallas_call(
        body, name="even_proj_bwd", grid=(seq // tm,),
        in_specs=[_rows(tm, 512), _rows(tm, LANES), _rows(tm, LANES), _rows(tm, 512), _rows(tm, 512), _rows(tm, D_MODEL),
                  _rows(tm, D_MODEL), _rows(tm, D_MODEL), _full((3, D_MODEL))] + [_rows(tm, LANES)] * 3
        + [_const((EVEN_IN, D_MODEL))],
        out_specs=[_rows(tm, D_MODEL), _const((EVEN_IN, D_MODEL)), _full((8, D_MODEL))],
        out_shape=[_sds((seq, D_MODEL)), _sds((EVEN_IN, D_MODEL)), _sds((8, D_MODEL))],
        scratch_shapes=[pltpu.VMEM((tm, EVEN_IN), MXU_DTYPE)],
        compiler_params=_params("arbitrary"),
    )(dq, dk, dv, dsu, dsv, dg, x, dres, mod, *tabs, w_in_t)


def _local_step(x, posf, tgt, mod, w, seq, ride=None):
    rid = lambda make, *a: None if ride is None else make(*a)
    mxu = lambda a: a.astype(MXU_DTYPE)
    row = lambda a: a.reshape(1, -1)
    tabs = _rope_tables(posf, seq)
    e2 = mxu(jnp.kron(jnp.eye(2, dtype=F32), jnp.ones((HEAD_DIM, HEAD_DIM), F32)))
    e8 = mxu(jnp.repeat(jnp.eye(N_SG_GROUPS, dtype=F32), HEAD_DIM, axis=1))
    sgw = mxu(w["ev_sg_w"])
    sgb_full = jnp.repeat(w["ev_sg_b"].T, HEAD_DIM, axis=1)
    sgln_g, sgln_b = row(w["ev_sg_ln_g"]), row(w["ev_sg_ln_b"])
    sink = w["ev_sink"].reshape(N_Q_HEADS)
    ev_w_in_t = mxu(w["ev_w_in_t"])
    if ride is None:
        ev_w_out, od_w_in, od_w_out = mxu(w["ev_w_out"]), mxu(w["od_w_in"]), mxu(w["od_w_out"])
    wcat = mxu(jnp.concatenate([w["od_w_a"][0], w["od_w_x"][0], w["od_w_a"][1], w["od_w_x"][1]], axis=2))
    gate_bias = jnp.stack([w["od_b_a"][0], w["od_b_x"][0], w["od_b_a"][1], w["od_b_x"][1]])
    conv_b = row(w["od_conv_b"])
    ln_g, ln_b = w["ln_g"], w["ln_b"]

    (q, k, v, su, sv, g0), got = _even_proj(x, mod[0], ev_w_in_t, tabs, seq, rid(_gather_rider, ride and ride["ev_w_out"]))
    if ride is not None:
        ev_w_out = got[0].reshape(D_MODEL, D_MODEL)
    (ycat, lse, *sg_saved), got = _even_mix(q, k, v, su, sv, sink, sgln_g, sgln_b, sgw, sgb_full, e2, seq,
                                 rid(_gather_rider, ride and ride["od_w_in"]))
    if ride is not None:
        od_w_in = got[0]
    (zhat0, rstd0, x1, xr, g1), got = _even_out(ycat, g0, x, mod[0], mod[1], ev_w_out, od_w_in, ln_g[0:1], ln_b[0:1], seq,
                                      rid(_gather_rider, ride and ride["od_w_out"]))
    if ride is not None:
        od_w_out = got[0].reshape(D_MODEL, D_MODEL)
    lru = (w["od_conv_w"], conv_b, wcat, gate_bias, w["od_lam"], seq)
    hf, hpf, *saved_f, xc = _lru_fwd(xr, None, *lru, 0)
    hr, hpr, *saved_r = _lru_fwd(xr, xc, *lru, 1)
    dhs, dg1, dres1, loss, d_od_w_out, vec_o = _odd_out_and_loss(hf, hr, g1, x1, tgt, mod[1], od_w_out, ln_g[1:2], ln_b[1:2], seq)
    dxc_f, dw_f, vec_f = _lru_bwd(xc, dhs, hpf, *saved_f, wcat, w["od_lam"], seq, 0)
    dxc_r, dw_r, vec_r = _lru_bwd(xc, dhs, hpr, *saved_r, wcat, w["od_lam"], seq, 1)
    dx1, d_od_w_in, vec_p = _odd_proj_bwd(dxc_f, dxc_r, xr, dg1, x1, dres1, mod[1], w["od_conv_w"], od_w_in, seq)
    d_od_w_a = jnp.stack([dw_f[:, :, 0:128], dw_r[:, :, 0:128]])
    d_od_w_x = jnp.stack([dw_f[:, :, 128:256], dw_r[:, :, 128:256]])
    od_parts = [d_od_w_in.reshape(4, 2, 512, 512), d_od_w_out.reshape(4, 2, 128, D_MODEL),
                d_od_w_a.reshape(4, 2, 2 * BLK, BLK), d_od_w_x.reshape(4, 2, 2 * BLK, BLK)]
    (dycat, dg0, dres0, d_ev_w_out, vec_e), got_od = _even_out_bwd(dx1, zhat0, rstd0, ycat, g0, mod[0], ln_g[0:1], ev_w_out, seq,
                                                                   rid(_sibling_swap_rider, od_parts))
    if ride is not None:
        od_sums = _sum_sibling(ride["core"], od_parts, got_od, [ride["wire"]] * 4, "sum_sibling_od")
    (dq, dsu, dsv, dk, dv, d_sgw, d_sgb, vec_s, d_sink), od_slots = _even_mix_bwd(
        q, k, v, lse, ycat, dycat, su, *sg_saved, sink, sgln_g, sgln_b, sgw, e2, e8, seq,
        rid(_chip_exchange_rider, ride and od_sums))
    grad_x, d_ev_w_in_t, vec_x = _even_proj_bwd(dq, dk, dv, dsu, dsv, dg0, x, dres0, mod[0], tabs, ev_w_in_t, seq)

    rows, dmod_blk = _pack_small(vec_x, vec_e, vec_p, vec_o, vec_f, vec_r, vec_s, d_sink, d_sgb, loss)
    grads = {"rows": rows, "dmod_blk": dmod_blk, "ev_w_in_t": d_ev_w_in_t, "ev_w_out": d_ev_w_out, "ev_sg_w": d_sgw}
    if ride is None:
        grads.update({"od_w_in": d_od_w_in, "od_w_out": d_od_w_out, "od_w_a": d_od_w_a, "od_w_x": d_od_w_x})
    else:
        grads["od_slots"] = od_slots
    return grad_x, grads


ROW_DMOD, ROW_LN, ROW_SG_LN, ROW_SG_B, ROW_CONV_W, ROW_CONV_B, ROW_B_A, ROW_B_X, ROW_LAM, ROW_SINK, ROW_LOSS = (
    0, 6, 10, 11, 12, 16, 17, 19, 21, 23, 24)
SMALL_ROWS = 64


def _pack_small(vec_x, vec_e, vec_p, vec_o, vec_f, vec_r, vec_s, d_sink, d_sgb, loss):
    def body(x_ref, e_ref, p_ref, o_ref, f_ref, r_ref, s_ref, sink_ref, sgb_ref, loss_ref, rows_ref, dmod_ref):
        rows_ref[...] = jnp.zeros_like(rows_ref)
        dmod_ref[...] = jnp.zeros_like(dmod_ref)
        put = [(ROW_DMOD, x_ref, 0), (ROW_DMOD + 1, x_ref, 1), (ROW_DMOD + 2, e_ref, 2), (ROW_DMOD + 3, p_ref, 5),
               (ROW_DMOD + 4, p_ref, 6), (ROW_DMOD + 5, o_ref, 2), (ROW_LN, e_ref, 0), (ROW_LN + 1, e_ref, 1),
               (ROW_LN + 2, o_ref, 0), (ROW_LN + 3, o_ref, 1), (ROW_CONV_B, p_ref, 4), (ROW_B_A, f_ref, 0),
               (ROW_B_A + 1, r_ref, 0), (ROW_B_X, f_ref, 1), (ROW_B_X + 1, r_ref, 1), (ROW_LAM, f_ref, 2), (ROW_LAM + 1, r_ref, 2)]
        put += [(ROW_CONV_W + k, p_ref, k) for k in range(4)]
        for dst, ref, src in put:
            rows_ref[dst:dst + 1, :] = ref[src:src + 1, :]
            if dst < 6:
                dmod_ref[dst:dst + 1, :] = ref[src:src + 1, :]
        rows_ref[ROW_SG_LN:ROW_SG_LN + 1, 0:SG_WIDTH] = s_ref[0:1, :]
        rows_ref[ROW_SG_LN:ROW_SG_LN + 1, SG_WIDTH:2 * SG_WIDTH] = s_ref[1:2, :]
        lane = _lane_iota((1, LANES))
        sink = jnp.zeros((1, LANES), F32)
        for h in range(N_Q_HEADS):
            rows_ref[ROW_SG_B:ROW_SG_B + 1, h * LANES:(h + 1) * LANES] = sgb_ref[h:h + 1, :]
            sink = jnp.where(lane == h, sink_ref[h:h + 1, :], sink)
        rows_ref[ROW_SINK:ROW_SINK + 1, 0:LANES] = sink
        rows_ref[ROW_LOSS:ROW_LOSS + 1, 0:LANES] = jnp.where(lane == 0, loss_ref[0:1, :], 0.0)

    return pl.pallas_call(body, name="pack_small", out_shape=[_sds((SMALL_ROWS, D_MODEL)), _sds((8, D_MODEL))])(
        vec_x, vec_e, vec_p, vec_o, vec_f, vec_r, vec_s, d_sink, d_sgb, loss)


class _Copies:
    def __init__(self, send_sems, recv_sems, local_sems, stages):
        self.send_sems, self.recv_sems, self.local_sems, self.stages = send_sems, recv_sems, local_sems, stages
        self.sent, self.staged, self.locals, self.settled = [], [], [], 0

    def remote(self, k, src, dst, to):
        return pltpu.make_async_remote_copy(src_ref=src, dst_ref=dst, send_sem=self.send_sems.at[k], recv_sem=self.recv_sems.at[k],
                                            device_id=to, device_id_type=MESH)

    def send(self, k, src, dst, to):
        cp = self.remote(k, src, dst, to)
        cp.start()
        self.sent.append(cp)

    def arrived(self, k, dst, frm):
        self.remote(k, dst, dst, frm).wait_recv()

    def local(self, src, dst):
        k = len(self.staged)
        cp = pltpu.make_async_copy(src, self.stages[k], self.local_sems.at[2 * k])
        cp.start()
        self.staged.append((cp, dst))

    def flush(self):
        for k in range(len(self.locals), len(self.staged)):
            cp, dst = self.staged[k]
            cp.wait()
            out = pltpu.make_async_copy(self.stages[k], dst, self.local_sems.at[2 * k + 1])
            out.start()
            self.locals.append(out)

    def settle(self):
        self.flush()
        for cp in self.locals[self.settled:]:
            cp.wait()
        self.settled = len(self.locals)

    def drain(self):
        self.settle()
        for cp in self.sent:
            cp.wait_send()


def _comm_call(body, name, ins, out_shapes, n_remote, stages, extra=()):
    n_in, n_out, n_st = len(ins), len(out_shapes), len(stages)

    def kern(*refs):
        in_refs, out_refs = refs[:n_in], refs[n_in:n_in + n_out]
        send_sems, recv_sems, local_sems = refs[n_in + n_out:n_in + n_out + 3]
        stage_refs = refs[n_in + n_out + 3:n_in + n_out + 3 + n_st]
        cps = _Copies(send_sems, recv_sems, local_sems, stage_refs)
        if extra:
            body(cps, in_refs, out_refs, refs[n_in + n_out + 3 + n_st:])
        else:
            body(cps, in_refs, out_refs)

    hbm = pl.BlockSpec(memory_space=pl.ANY)
    return pl.pallas_call(
        kern, name=name, out_shape=out_shapes, in_specs=[hbm] * n_in, out_specs=[hbm] * n_out,
        scratch_shapes=[pltpu.SemaphoreType.DMA((n_remote,)), pltpu.SemaphoreType.DMA((n_remote,)),
                        pltpu.SemaphoreType.DMA((2 * n_st,))] + [pltpu.VMEM(s, d) for s, d in stages] + list(extra),
        compiler_params=pltpu.CompilerParams(vmem_limit_bytes=VMEM_LIMIT),
    )(*ins)


def _gather_to_all(cps, pairs, me, sibling, other_chips, c, base):
    idx = lambda p: 4 * p[0] + 2 * p[1] + p[2]
    for i, (src, dst) in enumerate(pairs):
        cps.local(src, dst.at[idx(me)])
        cps.send(base + 7 * i, src, dst.at[idx(me)], sibling)
        for j, chip in enumerate(other_chips):
            cps.send(base + 7 * i + 1 + j, src, dst.at[idx(me)], (*chip, c))
    cps.flush()
    for j, chip in enumerate(other_chips):
        for i, (_, dst) in enumerate(pairs):
            got = dst.at[idx((*chip, c))]
            cps.arrived(base + 7 * i + 1 + j, got, (*chip, c))
            cps.send(base + 7 * i + 4 + j, got, got, sibling)
    for i, (_, dst) in enumerate(pairs):
        cps.arrived(base + 7 * i, dst.at[idx(sibling)], sibling)
        for j, chip in enumerate(other_chips):
            cps.arrived(base + 7 * i + 4 + j, dst.at[idx((*chip, 1 - c))], sibling)


def _gather_weights(shard, small, ada_w, ada_b):
    hr = shard.shape[0] // 2
    cols = ada_w.shape[2]

    def body(cps, ins, outs, extra):
        c_v, w_v, b_v, mod_v, sems = extra
        x, y, c = _place()
        me, sibling, mine = (x, y, c), (x, y, 1 - c), 2 * x + y
        chips = _other_chips(x, y)
        cps.local(ins[0], outs[0].at[mine])
        for j, (px, py) in enumerate(chips):
            rows = pl.ds(c * hr, hr)
            cps.send(j, ins[0].at[rows], outs[0].at[mine, rows], (px, py, c))
        load_w = pltpu.make_async_copy(ins[2], w_v, sems.at[0])
        load_b = pltpu.make_async_copy(ins[3], b_v, sems.at[1])
        load_w.start()
        load_b.start()
        _gather_to_all(cps, [(ins[1], outs[1])], me, sibling, chips, c, 6)
        cps.settle()
        load_c = pltpu.make_async_copy(outs[1].at[:, 0], c_v, sems.at[2])
        load_c.start()
        load_c.wait()
        load_w.wait()
        load_b.wait()
        cc = c_v[...]
        cond = cc * _sigmoid(cc)
        for layer in range(2):
            mod_v[layer * 8:(layer + 1) * 8, :] = _mm(cond, w_v[layer]) + b_v[layer]
        _gather_to_all(cps, [(mod_v, outs[2])], me, sibling, chips, c, 13)
        for j, (px, py) in enumerate(chips):
            got = outs[0].at[2 * px + py, pl.ds(c * hr, hr)]
            cps.arrived(j, got, (px, py, c))
            cps.send(3 + j, got, got, sibling)
        for j, (px, py) in enumerate(chips):
            cps.arrived(3 + j, outs[0].at[2 * px + py, pl.ds((1 - c) * hr, hr)], sibling)
        cps.drain()

    return _comm_call(
        body, "gather_weights", [shard, small, ada_w, ada_b],
        [_sds((4,) + shard.shape, shard.dtype), _sds((8,) + small.shape, small.dtype), _sds((8, 16, cols))], 6 + 7 + 7,
        [(shard.shape, shard.dtype), (small.shape, small.dtype), ((16, cols), F32)],
        extra=[pltpu.VMEM((8, D_MODEL), F32), pltpu.VMEM(ada_w.shape, F32), pltpu.VMEM(ada_b.shape, F32),
               pltpu.VMEM((16, cols), F32), pltpu.SemaphoreType.DMA((3,))])


def _reduce_sibling(parts, dmod_rows):
    n = len(parts)

    def body(cps, ins, outs):
        x, y, c = _place()
        me, sibling = (x, y, c), (x, y, 1 - c)
        chips = [(1 - x, y), (x, 1 - y), (1 - x, 1 - y)]
        for i in range(n):
            cps.send(i, ins[i].at[:, 1 - c], outs[i], sibling)
        _gather_to_all(cps, [(ins[n], outs[n])], me, sibling, chips, c, n)
        for i in range(n):
            cps.arrived(i, outs[i], sibling)
        cps.drain()

    return _comm_call(body, "reduce_sibling", list(parts) + [dmod_rows],
                      [_sds((4,) + p.shape[2:], p.dtype) for p in parts] + [_sds((8,) + dmod_rows.shape, dmod_rows.dtype)], n + 7,
                      [(dmod_rows.shape, dmod_rows.dtype)])


def _reduce_chips(parts):
    n = len(parts)

    def body(cps, ins, outs):
        x, y, c = _place()
        mine = 2 * x + y
        chips = _other_chips(x, y)
        for i in range(n):
            cps.local(ins[i].at[mine], outs[i].at[mine])
        for j, (px, py) in enumerate(chips):
            for i in range(n):
                cps.send(3 * i + j, ins[i].at[2 * px + py], outs[i].at[mine], (px, py, c))
        cps.flush()
        for j, (px, py) in enumerate(chips):
            for i in range(n):
                cps.arrived(3 * i + j, outs[i].at[2 * px + py], (px, py, c))
        cps.drain()

    return _comm_call(body, "reduce_chips", list(parts), [_sds(p.shape, p.dtype) for p in parts], 3 * n,
                      [(p.shape[1:], p.dtype) for p in parts])


def _gather_reduced(shard_parts, repl_parts):
    ns, nr = len(shard_parts), len(repl_parts)

    def body(cps, ins, outs):
        x, y, c = _place()
        me, sibling = (x, y, c), (x, y, 1 - c)
        chips = [(1 - x, y), (x, 1 - y), (1 - x, 1 - y)]
        for i in range(ns):
            cps.local(ins[i], outs[i].at[c])
            cps.send(i, ins[i], outs[i].at[c], sibling)
        _gather_to_all(cps, [(ins[ns + i], outs[ns + i]) for i in range(nr)], me, sibling, chips, c, ns)
        for i in range(ns):
            cps.arrived(i, outs[i].at[1 - c], sibling)
        cps.drain()

    return _comm_call(body, "gather_reduced", list(shard_parts) + list(repl_parts),
                      [_sds((2,) + p.shape, p.dtype) for p in shard_parts] + [_sds((8,) + p.shape, p.dtype) for p in repl_parts],
                      ns + 7 * nr, [(p.shape, p.dtype) for p in list(shard_parts) + list(repl_parts)])


def _sum_sibling(core, parts, got, wire, name):
    n = len(parts)

    def body(core_ref, *refs):
        for i in range(n):
            refs[2 * n + i][0] = (refs[i][0] + refs[n + i][0]).astype(wire[i])

    keep_spec = lambda p: pl.BlockSpec((1, None) + p.shape[2:], lambda s, core_ref: (s, core_ref[0], 0, 0))
    slot_spec = lambda p: pl.BlockSpec((1,) + p.shape[2:], lambda s, core_ref: (s, 0, 0))
    return pl.pallas_call(
        body, name=name,
        grid_spec=pltpu.PrefetchScalarGridSpec(
            num_scalar_prefetch=1, grid=(4,), in_specs=[keep_spec(p) for p in parts] + [slot_spec(p) for p in parts],
            out_specs=[slot_spec(p) for p in parts]),
        out_shape=[_sds((4,) + p.shape[2:], wire[i]) for i, p in enumerate(parts)],
        compiler_params=_params("parallel"),
    )(core, *parts, *got)


def _sum_slots(slots, name):
    n = len(slots)

    def spec_pair(p):
        k, rows, cols = p.shape
        sub = 16 if p.dtype == BF16 else 8
        if (rows // 2) % sub == 0:
            return pl.BlockSpec((k, rows // 2, cols), lambda i: (0, i, 0)), pl.BlockSpec((rows // 2, cols), lambda i: (i, 0))
        return pl.BlockSpec((k, rows, cols), lambda i: (0, 0, 0)), pl.BlockSpec((rows, cols), lambda i: (0, 0))

    pairs = [spec_pair(p) for p in slots]

    def body(*refs):
        for i in range(n):
            acc = refs[i][0].astype(F32)
            for j in range(1, slots[i].shape[0]):
                acc = acc + refs[i][j].astype(F32)
            refs[n + i][...] = acc

    return pl.pallas_call(
        body, name=name, grid=(2,), in_specs=[a for a, _ in pairs], out_specs=[b for _, b in pairs],
        out_shape=[_sds(p.shape[1:]) for p in slots], compiler_params=_params("arbitrary"),
    )(*slots)


def _adamw_math(w, g, m, v):
    m = ADAM_B1 * m + (1.0 - ADAM_B1) * g
    v = ADAM_B2 * v + (1.0 - ADAM_B2) * (g * g)
    m_hat = m / (1.0 - ADAM_B1 ** ADAM_STEP)
    v_hat = v / (1.0 - ADAM_B2 ** ADAM_STEP)
    delta = -ADAM_LR * (m_hat / (jnp.sqrt(v_hat) + ADAM_EPS) + ADAM_WD * w)
    return delta, m, v


def _ada_update(c_all, dmod, w, m, v, rider=None):
    cols = w.shape[2]
    tr = 256
    per = D_MODEL // tr
    spec3 = pl.BlockSpec((1, tr, cols), lambda i: (i // per, i % per, 0))

    def body(c_ref, d_ref, w_ref, m_ref, v_ref, g_ref, dl_ref, nm_ref, nv_ref):
        cc = c_ref[...]
        g = _mm_tn(cc * _sigmoid(cc), d_ref[0])
        g_ref[0] = g
        dl_ref[0], nm_ref[0], nv_ref[0] = _adamw_math(w_ref[0], g, m_ref[0], v_ref[0])

    return _call(
        body, "ada_update", (2 * per,),
        [pl.BlockSpec((8, tr), lambda i: (0, i % per)), pl.BlockSpec((1, 8, cols), lambda i: (i // per, 0, 0)), spec3, spec3, spec3],
        [spec3] * 4, [_sds(w.shape)] * 4, (c_all, dmod, w, m, v), "parallel", rider=rider)


def _adamw_matrices(params):
    n = len(params)
    steps = 8

    def body(*refs):
        ins, outs = refs[:4 * n], refs[4 * n:]
        for j in range(n):
            w_ref, g_ref, m_ref, v_ref = ins[4 * j:4 * j + 4]
            g = g_ref[...]
            outs[4 * j][...] = g
            outs[4 * j + 1][...], outs[4 * j + 2][...], outs[4 * j + 3][...] = _adamw_math(w_ref[...], g, m_ref[...], v_ref[...])

    spec = lambda p: _rows(p[0].shape[0] // steps, p[0].shape[1])
    res = pl.pallas_call(
        body, name="adamw_matrices", grid=(steps,), in_specs=[spec(p) for p in params for _ in range(4)],
        out_specs=[spec(p) for p in params for _ in range(4)], out_shape=[_sds(p[0].shape) for p in params for _ in range(4)],
        compiler_params=_params("parallel"),
    )(*[a for p in params for a in p])
    return [tuple(res[4 * j:4 * j + 4]) for j in range(n)]


def _adamw_small(params):
    n = len(params)

    def body(*refs):
        ins, outs = refs[:4 * n], refs[4 * n:]
        for j in range(n):
            w_ref, g_ref, m_ref, v_ref = ins[4 * j:4 * j + 4]
            outs[3 * j][...], outs[3 * j + 1][...], outs[3 * j + 2][...] = _adamw_math(w_ref[...], g_ref[...], m_ref[...], v_ref[...])

    flat = [a for p in params for a in p]
    res = pl.pallas_call(body, name="adamw_small", out_shape=[_sds(p[0].shape) for p in params for _ in range(3)])(*flat)
    return [tuple(res[3 * j:3 * j + 3]) for j in range(n)]


def _cols(a, start, size):
    return lax.dynamic_slice_in_dim(a, start, size, axis=a.ndim - 1)


def kernel(x, c, positions, ada_w, ada_b, ln_g, ln_b, ev_w_in, ev_w_out, ev_sink, ev_sg_ln_g, ev_sg_ln_b, ev_sg_w, ev_sg_b, od_w_in, od_conv_w, od_conv_b, od_w_a, od_b_a, od_w_x, od_b_x, od_lam, od_w_out, loss_target, m_ada_w, m_ada_b, m_ln_g, m_ln_b, m_ev_w_in, m_ev_w_out, m_ev_sink, m_ev_sg_ln_g, m_ev_sg_ln_b, m_ev_sg_w, m_ev_sg_b, m_od_w_in, m_od_conv_w, m_od_conv_b, m_od_w_a, m_od_b_a, m_od_w_x, m_od_b_x, m_od_lam, m_od_w_out, v_ada_w, v_ada_b, v_ln_g, v_ln_b, v_ev_w_in, v_ev_w_out, v_ev_sink, v_ev_sg_ln_g, v_ev_sg_ln_b, v_ev_sg_w, v_ev_sg_b, v_od_w_in, v_od_conv_w, v_od_conv_b, v_od_w_a, v_od_b_a, v_od_w_x, v_od_b_x, v_od_lam, v_od_w_out):
    seq = x.shape[1]
    px, py, pc = _place()
    chip = 2 * px + py
    dev = 2 * chip + pc

    small = jnp.concatenate([od_conv_w[0].reshape(-1), od_conv_b[0], od_b_a[0].reshape(-1), jnp.zeros((256,), F32),
                             od_b_x[0].reshape(-1), od_lam[0].reshape(-1)]).reshape(3, D_MODEL)
    blk = jnp.concatenate([c, small, jnp.zeros((4, D_MODEL), F32)], axis=0)
    tr = lambda a: jnp.swapaxes(a, -1, -2)
    wire_w = lambda a: a.astype(MXU_DTYPE)
    ada_cols = ada_w.shape[2]
    ev_w_in4, g_small, mod_g = _gather_weights(wire_w(tr(ev_w_in[0])), blk, ada_w,
                                                _cols(ada_b, chip * ada_cols, ada_cols).reshape(2, 1, ada_cols))
    core = pc.astype(jnp.int32).reshape(1)
    ride = {"ev_w_out": wire_w(ev_w_out[0]), "od_w_in": wire_w(od_w_in[0]), "od_w_out": wire_w(od_w_out[0]),
            "core": core, "wire": MXU_DTYPE}
    c_all = g_small[:, 0, :]
    per_chip = g_small[0::2]
    conv_w = per_chip[:, 1].reshape(4, 4, 256).transpose(1, 0, 2).reshape(4, D_MODEL)
    conv_b = per_chip[:, 2, 0:256].reshape(D_MODEL)
    b_a = per_chip[:, 2, 256:768].reshape(4, 2, 256).transpose(1, 0, 2).reshape(2, D_MODEL)
    b_x = per_chip[:, 3, 0:512].reshape(4, 2, 256).transpose(1, 0, 2).reshape(2, D_MODEL)
    lam = per_chip[:, 3, 512:1024].reshape(4, 2, 256).transpose(1, 0, 2).reshape(2, D_MODEL)

    w_full = {
        "ev_w_in_t": ev_w_in4.reshape(EVEN_IN, D_MODEL),
        "ev_sink": ev_sink[0], "ev_sg_ln_g": ev_sg_ln_g[0], "ev_sg_ln_b": ev_sg_ln_b[0], "ev_sg_w": ev_sg_w[0],
        "ev_sg_b": ev_sg_b[0], "od_conv_w": conv_w, "od_conv_b": conv_b, "od_w_a": od_w_a[0], "od_b_a": b_a,
        "od_w_x": od_w_x[0], "od_b_x": b_x, "od_lam": lam, "ln_g": ln_g, "ln_b": ln_b,
    }

    mod_all = mod_g.reshape(4, 2, 2, 8, ada_cols)[:, 0]
    mod_mine = lax.dynamic_index_in_dim(mod_all, dev, axis=2, keepdims=False)
    mod = mod_mine.transpose(1, 0, 2).reshape(2, 3, D_MODEL)

    posf = positions.astype(F32).reshape(seq, 1)
    grad_x, g = _local_step(x[0], posf, loss_target[0], mod, w_full, seq, ride)

    parts = [g["ev_w_in_t"].reshape(4, 2, 352, D_MODEL), g["ev_w_out"].reshape(4, 2, 128, D_MODEL),
             g["ev_sg_w"].reshape(4, 2, BLK, BLK), g["rows"].reshape(4, 2, SMALL_ROWS // 8, D_MODEL)]
    wire = [MXU_DTYPE] * 3 + [F32]
    *got, dmod_gathered = _reduce_sibling(parts, g["dmod_blk"])
    ev_slots = list(_reduce_chips(_sum_sibling(core, parts, got, wire, "sum_sibling")))
    od_slots = list(g["od_slots"])
    mine = _sum_slots(ev_slots[0:2] + od_slots[0:2] + ev_slots[2:3] + od_slots[2:4] + ev_slots[3:4], "sum_chips")
    reduced = _gather_reduced(mine[:4], mine[4:])
    g_ev_w_in_t = reduced[0].reshape(704, D_MODEL)
    g_ev_w_out = reduced[1].reshape(256, D_MODEL)
    g_od_w_in = reduced[2].reshape(D_MODEL, 512)
    g_od_w_out = reduced[3].reshape(256, D_MODEL)
    g_sg_w = reduced[4].reshape(8 * BLK, BLK)
    g_w_a = reduced[5].reshape(16 * BLK, BLK)
    g_w_x = reduced[6].reshape(16 * BLK, BLK)
    gs = reduced[7].reshape(SMALL_ROWS, D_MODEL)
    loss = gs[ROW_LOSS, 0]
    dmod_all = dmod_gathered[:, 0:6].reshape(8, 2, 3 * D_MODEL)
    dmod_sh = _cols(dmod_all, chip * ada_cols, ada_cols).transpose(1, 0, 2)
    (g_ada_w, d_ada_w, nm_ada_w, nv_ada_w), _ = _ada_update(c_all, dmod_sh, ada_w, m_ada_w, v_ada_w)

    mats = (("ev_w_out", ev_w_out, g_ev_w_out, m_ev_w_out, v_ev_w_out), ("od_w_in", od_w_in, g_od_w_in, m_od_w_in, v_od_w_in),
            ("od_w_out", od_w_out, g_od_w_out, m_od_w_out, v_od_w_out), ("ev_sg_w", ev_sg_w, g_sg_w, m_ev_sg_w, v_ev_sg_w),
            ("od_w_a", od_w_a, g_w_a, m_od_w_a, v_od_w_a), ("od_w_x", od_w_x, g_w_x, m_od_w_x, v_od_w_x))
    upd = _adamw_matrices([(tr(ev_w_in[0]), g_ev_w_in_t, tr(m_ev_w_in[0]), tr(v_ev_w_in[0]))]
                          + [(w_.reshape(g_.shape), g_, m_.reshape(g_.shape), v_.reshape(g_.shape)) for _, w_, g_, m_, v_ in mats])
    big = {"ev_w_in": tuple(tr(a).reshape(ev_w_in.shape) for a in upd[0])}
    for (name, w_, _, _, _), u in zip(mats, upd[1:]):
        big[name] = tuple(a.reshape(w_.shape) for a in u)
    big["ada_w"] = (g_ada_w, d_ada_w, nm_ada_w, nv_ada_w)

    sh = lambda a: _cols(a, chip * 256, 256)
    small_g = {
        "ada_b": gs[ROW_DMOD:ROW_DMOD + 6].reshape(2, 3 * D_MODEL),
        "ln_g": jnp.stack([gs[ROW_LN], gs[ROW_LN + 2]]), "ln_b": jnp.stack([gs[ROW_LN + 1], gs[ROW_LN + 3]]),
        "ev_sink": gs[ROW_SINK:ROW_SINK + 1, 0:N_Q_HEADS], "ev_sg_ln_g": gs[ROW_SG_LN:ROW_SG_LN + 1, 0:SG_WIDTH],
        "ev_sg_ln_b": gs[ROW_SG_LN:ROW_SG_LN + 1, SG_WIDTH:2 * SG_WIDTH], "ev_sg_b": gs[ROW_SG_B].reshape(N_SG_GROUPS, BLK),
        "od_conv_w": sh(gs[ROW_CONV_W:ROW_CONV_W + 4]), "od_conv_b": sh(gs[ROW_CONV_B:ROW_CONV_B + 1]),
        "od_b_a": sh(gs[ROW_B_A:ROW_B_A + 2]), "od_b_x": sh(gs[ROW_B_X:ROW_B_X + 2]), "od_lam": sh(gs[ROW_LAM:ROW_LAM + 2]),
    }
    small_in = {"ada_b": (ada_b, m_ada_b, v_ada_b), "ln_g": (ln_g, m_ln_g, v_ln_g), "ln_b": (ln_b, m_ln_b, v_ln_b),
                "ev_sink": (ev_sink, m_ev_sink, v_ev_sink), "ev_sg_ln_g": (ev_sg_ln_g, m_ev_sg_ln_g, v_ev_sg_ln_g),
                "ev_sg_ln_b": (ev_sg_ln_b, m_ev_sg_ln_b, v_ev_sg_ln_b), "ev_sg_b": (ev_sg_b, m_ev_sg_b, v_ev_sg_b),
                "od_conv_w": (od_conv_w, m_od_conv_w, v_od_conv_w), "od_conv_b": (od_conv_b, m_od_conv_b, v_od_conv_b),
                "od_b_a": (od_b_a, m_od_b_a, v_od_b_a), "od_b_x": (od_b_x, m_od_b_x, v_od_b_x),
                "od_lam": (od_lam, m_od_lam, v_od_lam)}
    names_small = list(small_g)
    upd = _adamw_small([(small_in[n][0].reshape(small_g[n].shape), small_g[n], small_in[n][1].reshape(small_g[n].shape),
                         small_in[n][2].reshape(small_g[n].shape)) for n in names_small])
    res = dict(big)
    for n, (d_, nm_, nv_) in zip(names_small, upd):
        shape = small_in[n][0].shape
        res[n] = tuple(a.reshape(shape) for a in (small_g[n], d_, nm_, nv_))

    order = ["ada_w", "ada_b", "ln_g", "ln_b", "ev_w_in", "ev_w_out", "ev_sink", "ev_sg_ln_g", "ev_sg_ln_b", "ev_sg_w", "ev_sg_b",
             "od_w_in", "od_conv_w", "od_conv_b", "od_w_a", "od_b_a", "od_w_x", "od_b_x", "od_lam", "od_w_out"]
    return (loss, grad_x.reshape(x.shape), *[res[n][0] for n in order], *[res[n][1] for n in order],
            *[res[n][2] for n in order], *[res[n][3] for n in order])
```

```python
import jax
import jax.numpy as jnp
import numpy as np
from jax import lax
from jax.experimental import pallas as pl
from jax.experimental.pallas import tpu as pltpu

F32 = jnp.float32
BF16 = jnp.bfloat16
MXU_DTYPE = BF16
ACT_DTYPE = MXU_DTYPE

D_MODEL = 1024
HEAD_DIM = 64
N_Q_HEADS = 8
Q_PER_KV = 4
ATTN_WIDTH = 512
BLK = 128
ROPE_DIM = 16
ROPE_THETA = 500000.0
N_SG_GROUPS = 8
SG_WIDTH = 512
EVEN_IN = 2816
ODD_IN = 2048
RNN_HEADS = 8
RG_LRU_C = 8.0
ALPHA = (2 * 2) ** 0.25
LN_EPS = 1e-5
NEG_INF = -1e30
ADAM_LR, ADAM_B1, ADAM_B2, ADAM_EPS, ADAM_WD, ADAM_STEP = 0.001, 0.9, 0.999, 1e-08, 0.01, 10

LANES = 128
VMEM_LIMIT = 56 * 1024 * 1024
MESH = pl.DeviceIdType.MESH


def _mm(a, b):
    return jnp.dot(a.astype(MXU_DTYPE), b.astype(MXU_DTYPE), preferred_element_type=F32)


def _mm_nt(a, b):
    return lax.dot_general(a.astype(MXU_DTYPE), b.astype(MXU_DTYPE), (((1,), (1,)), ((), ())), preferred_element_type=F32)


def _mm_tn(a, b):
    return lax.dot_general(a.astype(MXU_DTYPE), b.astype(MXU_DTYPE), (((0,), (0,)), ((), ())), preferred_element_type=F32)


def _sigmoid(x):
    return 1.0 / (1.0 + jnp.exp(-x))


def _ln_stats(z):
    mu = jnp.mean(z, axis=-1, keepdims=True)
    d = z - mu
    var = jnp.mean(d * d, axis=-1, keepdims=True)
    rstd = lax.rsqrt(var + LN_EPS)
    return d * rstd, rstd


def _ln_bwd(dout, zhat, rstd, g):
    dzh = dout * g
    m1 = jnp.mean(dzh, axis=-1, keepdims=True)
    m2 = jnp.mean(dzh * zhat, axis=-1, keepdims=True)
    return rstd * (dzh - m1 - zhat * m2)


def _group_sum(x, e2):
    hi = x.astype(MXU_DTYPE)
    lo = (x - hi.astype(F32)).astype(MXU_DTYPE)
    return jnp.dot(hi, e2, preferred_element_type=F32) + jnp.dot(lo, e2, preferred_element_type=F32)


def _lane_iota(shape):
    return lax.broadcasted_iota(jnp.int32, shape, 1)


def _to_kv_lanes(t, h):
    src_lo = (h % 2 == 0)
    dst_lo = (h // Q_PER_KV == 0)
    if src_lo != dst_lo:
        t = pltpu.roll(t, HEAD_DIM, 1)
    lane = _lane_iota(t.shape)
    keep = (lane < HEAD_DIM) if dst_lo else (lane >= HEAD_DIM)
    return jnp.where(keep, t, 0.0)


def _from_kv_lanes(t, h):
    src_lo = (h // Q_PER_KV == 0)
    dst_lo = (h % 2 == 0)
    lane = _lane_iota(t.shape)
    keep = (lane < HEAD_DIM) if src_lo else (lane >= HEAD_DIM)
    t = jnp.where(keep, t, 0.0)
    if src_lo != dst_lo:
        t = pltpu.roll(t, HEAD_DIM, 1)
    return t


def _rope(t, cos_t, sin_p, sin_m):
    half = ROPE_DIM // 2
    return t * cos_t + pltpu.roll(t, half, 1) * sin_p + pltpu.roll(t, LANES - half, 1) * sin_m


def _rope_t(d, cos_t, sin_p, sin_m):
    half = ROPE_DIM // 2
    return d * cos_t + pltpu.roll(d * sin_p, LANES - half, 1) + pltpu.roll(d * sin_m, half, 1)


def _band(ref, n, nb):
    prev = jnp.maximum(n - 1, 0)
    nxt = jnp.minimum(n + 1, nb - 1)
    rows = [ref[pl.ds(pl.multiple_of(j * BLK, BLK), BLK), :] for j in (prev, n, nxt)]
    return jnp.concatenate(rows, axis=0)


def _band_bias(n, seq):
    qi = lax.broadcasted_iota(jnp.int32, (BLK, 3 * BLK), 0)
    kj = lax.broadcasted_iota(jnp.int32, (BLK, 3 * BLK), 1)
    k_abs = n * BLK - BLK + kj
    valid = (jnp.abs(kj - BLK - qi) <= BLK) & (k_abs >= 0) & (k_abs < seq)
    bias = jnp.where(valid, 0.0, NEG_INF)
    return jnp.concatenate([bias] * Q_PER_KV, axis=0)


def _stack_heads(tile_of, kv):
    return jnp.concatenate([_to_kv_lanes(tile_of(h // 2), h) for h in range(Q_PER_KV * kv, Q_PER_KV * (kv + 1))], axis=0)


def _per_head_column(vals):
    row = lax.broadcasted_iota(jnp.int32, (Q_PER_KV * BLK, 1), 0)
    return jnp.where(row < BLK, vals[0], jnp.where(row < 2 * BLK, vals[1], jnp.where(row < 3 * BLK, vals[2], vals[3])))


def _softplus_neg(lam):
    e = jnp.exp(-jnp.abs(lam))
    u = 1.0 + e
    log1p_e = jnp.where(u == 1.0, e, jnp.log(u) * (e / (u - 1.0)))
    sp = jnp.maximum(-lam, 0.0) + log1p_e
    dsp = -1.0 / (1.0 + jnp.exp(lam))
    return sp, dsp


def _full(shape):
    return pl.BlockSpec(shape, lambda *_: (0,) * len(shape))


def _const(shape):
    return pl.BlockSpec(shape, lambda *_: (0,) * len(shape), pipeline_mode=pl.Buffered(1))


def _rows(tm, n):
    return pl.BlockSpec((tm, n), lambda i: (i, 0))


def _params(*sem):
    return pltpu.CompilerParams(dimension_semantics=sem, vmem_limit_bytes=VMEM_LIMIT)


def _sds(shape, dtype=F32):
    return jax.ShapeDtypeStruct(shape, dtype)


def _place():
    return lax.axis_index("x"), lax.axis_index("y"), lax.axis_index("c")


class _Rider:
    def __init__(self, ins, out_shapes, n_remote, n_local, plan):
        self.ins, self.out_shapes, self.n_remote, self.n_local, self.plan = list(ins), list(out_shapes), n_remote, n_local, plan

    def scratch(self):
        return [pltpu.SemaphoreType.DMA((self.n_remote,)), pltpu.SemaphoreType.DMA((self.n_remote,)),
                pltpu.SemaphoreType.DMA((max(self.n_local, 1),))]

    def run(self, first, in_refs, out_refs, sems):
        send_sems, recv_sems, local_sems = sems
        sends, recvs, locals_ = self.plan(in_refs, out_refs)
        remote = lambda k, src, dst, to: pltpu.make_async_remote_copy(
            src_ref=src, dst_ref=dst, send_sem=send_sems.at[k], recv_sem=recv_sems.at[k], device_id=to, device_id_type=MESH)
        if first:
            for k, src, dst, to in sends:
                remote(k, src, dst, to).start()
            for j, (src, dst) in enumerate(locals_):
                pltpu.make_async_copy(src, dst, local_sems.at[j]).start()
        else:
            for k, dst, frm in recvs:
                remote(k, dst, dst, frm).wait_recv()
            for k, src, dst, to in sends:
                remote(k, src, dst, to).wait_send()
            for j, (src, dst) in enumerate(locals_):
                pltpu.make_async_copy(src, dst, local_sems.at[j]).wait()


def _other_chips(x, y):
    return [(1 - x, y), (x, 1 - y), (1 - x, 1 - y)]


def _gather_rider(shard):
    hr = shard.shape[0] // 2

    def plan(ins, outs):
        x, y, c = _place()
        mine, src, dst = 2 * x + y, ins[0], outs[0]
        sends, recvs = [], []
        for j, (px, py) in enumerate(_other_chips(x, y)):
            for flip in range(2):
                tc = c if flip == 0 else 1 - c
                sends.append((2 * j + flip, src.at[pl.ds(c * hr, hr)], dst.at[mine, pl.ds(c * hr, hr)], (px, py, tc)))
                recvs.append((2 * j + flip, dst.at[2 * px + py, pl.ds(tc * hr, hr)], (px, py, tc)))
        return sends, recvs, [(src, dst.at[mine])]

    return _Rider([shard], [_sds((4,) + shard.shape, shard.dtype)], 6, 1, plan)


def _sibling_swap_rider(parts):
    n = len(parts)

    def plan(ins, outs):
        x, y, c = _place()
        sibling = (x, y, 1 - c)
        return ([(i, ins[i].at[:, 1 - c], outs[i], sibling) for i in range(n)], [(i, outs[i], sibling) for i in range(n)], [])

    return _Rider(parts, [_sds((4,) + p.shape[2:], p.dtype) for p in parts], n, 0, plan)


def _chip_exchange_rider(parts):
    n = len(parts)

    def plan(ins, outs):
        x, y, c = _place()
        mine = 2 * x + y
        sends, recvs = [], []
        for i in range(n):
            for j, (px, py) in enumerate(_other_chips(x, y)):
                sends.append((3 * i + j, ins[i].at[2 * px + py], outs[i].at[mine], (px, py, c)))
                recvs.append((3 * i + j, outs[i].at[2 * px + py], (px, py, c)))
        return sends, recvs, [(ins[i].at[mine], outs[i].at[mine]) for i in range(n)]

    return _Rider(parts, [_sds(p.shape, p.dtype) for p in parts], 3 * n, n, plan)


def _call(body, name, grid, in_specs, out_specs, out_shape, args, sem, scratch=(), rider=None):
    if rider is None:
        return list(pl.pallas_call(body, name=name, grid=grid, in_specs=in_specs, out_specs=out_specs, out_shape=out_shape,
                                   scratch_shapes=list(scratch), compiler_params=_params(sem))(*args)), []
    n_in, n_out, n_scr = len(in_specs), len(out_specs), len(scratch)
    r_in, r_out = len(rider.ins), len(rider.out_shapes)
    steps = grid[0]

    def riding(*refs):
        ins, r_ins = refs[:n_in], refs[n_in:n_in + r_in]
        outs = refs[n_in + r_in:n_in + r_in + n_out]
        r_outs = refs[n_in + r_in + n_out:n_in + r_in + n_out + r_out]
        scr = refs[n_in + r_in + n_out + r_out:n_in + r_in + n_out + r_out + n_scr]
        sems = refs[n_in + r_in + n_out + r_out + n_scr:]

        @pl.when(pl.program_id(0) == 0)
        def _():
            rider.run(True, r_ins, r_outs, sems)

        body(*ins, *outs, *scr)

        @pl.when(pl.program_id(0) == steps - 1)
        def _():
            rider.run(False, r_ins, r_outs, sems)

    hbm = pl.BlockSpec(memory_space=pl.ANY)
    res = pl.pallas_call(
        riding, name=name, grid=grid, in_specs=list(in_specs) + [hbm] * r_in, out_specs=list(out_specs) + [hbm] * r_out,
        out_shape=list(out_shape) + rider.out_shapes, scratch_shapes=list(scratch) + rider.scratch(),
        compiler_params=_params("arbitrary"),
    )(*args, *rider.ins)
    return list(res[:n_out]), list(res[n_out:])


def _row_tile(seq, want):
    return want if seq % want == 0 else seq


def _rope_consts():
    half = ROPE_DIM // 2
    inv_freq = np.power(np.float32(ROPE_THETA), -np.arange(half, dtype=np.float32) / np.float32(half)).astype(np.float32)
    j = np.arange(LANES) % HEAD_DIM
    invf = jnp.asarray(np.where(j < ROPE_DIM, inv_freq[j % half], 0.0).astype(np.float32).reshape(1, LANES))
    m_p = jnp.asarray(((j >= half) & (j < ROPE_DIM)).astype(np.float32).reshape(1, LANES))
    m_m = jnp.asarray(-(j < half).astype(np.float32).reshape(1, LANES))
    return invf, m_p, m_m


def _rope_block(pos_ref, invf_ref, mp_ref, mm_ref):
    ang = pos_ref[...] * invf_ref[...]
    s = jnp.sin(ang)
    return jnp.cos(ang), s * mp_ref[...], s * mm_ref[...]


def _even_proj(x, mod, w_in_t, tabs, seq, rider=None):
    tm = _row_tile(seq, 512)

    def body(x_ref, mod_ref, w_ref, pos_ref, invf_ref, mp_ref, mm_ref, q_ref, k_ref, v_ref, su_ref, sv_ref, g_ref):
        h = x_ref[...] * (1.0 + mod_ref[1:2, :]) + mod_ref[0:1, :]
        p = _mm_nt(h, w_ref[...])
        cos_t, sin_p, sin_m = _rope_block(pos_ref, invf_ref, mp_ref, mm_ref)
        for j in range(ATTN_WIDTH // LANES):
            q_ref[:, j * LANES:(j + 1) * LANES] = _rope(p[:, j * LANES:(j + 1) * LANES], cos_t, sin_p, sin_m).astype(q_ref.dtype)
        k_ref[...] = _rope(p[:, 512:640], cos_t, sin_p, sin_m).astype(k_ref.dtype)
        v_ref[...] = p[:, 640:768].astype(v_ref.dtype)
        su_ref[...] = p[:, 768:1280].astype(su_ref.dtype)
        sv_ref[...] = p[:, 1280:1792].astype(sv_ref.dtype)
        g_ref[...] = p[:, 1792:2816].astype(g_ref.dtype)

    return _call(
        body, "even_proj", (seq // tm,),
        [_rows(tm, D_MODEL), _full((3, D_MODEL)), _const((EVEN_IN, D_MODEL)), _rows(tm, 1)] + [_full((1, LANES))] * 3,
        [_rows(tm, 512), _rows(tm, LANES), _rows(tm, LANES), _rows(tm, 512), _rows(tm, 512), _rows(tm, D_MODEL)],
        [_sds((seq, 512), MXU_DTYPE), _sds((seq, LANES), MXU_DTYPE), _sds((seq, LANES), MXU_DTYPE), _sds((seq, 512), ACT_DTYPE),
         _sds((seq, 512), ACT_DTYPE), _sds((seq, D_MODEL), ACT_DTYPE)],
        (x, mod, w_in_t, *tabs), "parallel", rider=rider)


def _sg_forward(sv, lng, lnb, sgw_ref, sgb, e2):
    vn, vhat, rstd, svo = [], [], [], []
    for j in range(SG_WIDTH // LANES):
        t = sv[:, j * LANES:(j + 1) * LANES]
        mu = _group_sum(t, e2) * (1.0 / HEAD_DIM)
        d = t - mu
        var = _group_sum(d * d, e2) * (1.0 / HEAD_DIM)
        r = lax.rsqrt(var + LN_EPS)
        vh = d * r
        vhat.append(vh)
        rstd.append(r)
        vn.append(vh * lng[:, j * LANES:(j + 1) * LANES] + lnb[:, j * LANES:(j + 1) * LANES])
    lane = _lane_iota((BLK, LANES))
    for j in range(SG_WIDTH // LANES):
        lo = _mm(sgw_ref[2 * j], vn[j])
        hi = _mm(sgw_ref[2 * j + 1], vn[j])
        svo.append(jnp.where(lane < HEAD_DIM, lo, hi) + sgb[:, j * LANES:(j + 1) * LANES])
    return svo, vn, vhat, rstd


def _even_mix(q, k, v, su, sv, sink, sgln_g, sgln_b, sgw, sgb_full, e2, seq, rider=None):
    nb = seq // BLK

    def body(sink_ref, q_ref, k_ref, v_ref, su_ref, sv_ref, lng_ref, lnb_ref, sgw_ref, sgb_ref, e2_ref, ycat_ref, lse_ref,
             svo_ref, vhat_ref, rstd_ref):
        n = pl.program_id(0)
        kband = _band(k_ref, n, nb)
        vband = _band(v_ref, n, nb)
        bias = _band_bias(n, seq)
        lane = _lane_iota((BLK, LANES))
        lse = jnp.zeros((BLK, LANES), F32)
        q_tile = lambda j: q_ref[:, j * LANES:(j + 1) * LANES].astype(F32)
        acc = [jnp.zeros((BLK, LANES), F32) for _ in range(ATTN_WIDTH // LANES)]
        for kv in range(N_Q_HEADS // Q_PER_KV):
            heads = range(Q_PER_KV * kv, Q_PER_KV * (kv + 1))
            sink = _per_head_column([sink_ref[h] for h in heads])
            s = _mm_nt(_stack_heads(q_tile, kv), kband) * (HEAD_DIM ** -0.5) + bias
            m = jnp.maximum(jnp.max(s, axis=1, keepdims=True), sink)
            p = jnp.exp(s - m)
            denom = jnp.sum(p, axis=1, keepdims=True) + jnp.exp(sink - m)
            o4 = _mm(p / denom, vband)
            l4 = m + jnp.log(denom)
            for g, h in enumerate(heads):
                acc[h // 2] = acc[h // 2] + _from_kv_lanes(o4[g * BLK:(g + 1) * BLK], h)
                lse = jnp.where(lane == h, l4[g * BLK:(g + 1) * BLK], lse)
        for j in range(ATTN_WIDTH // LANES):
            ycat_ref[:, j * LANES:(j + 1) * LANES] = acc[j].astype(ycat_ref.dtype)
        lse_ref[...] = lse
        svo, _, vhat, rstd = _sg_forward(sv_ref[...].astype(F32), lng_ref[...], lnb_ref[...], sgw_ref, sgb_ref[...], e2_ref[...])
        for j in range(SG_WIDTH // LANES):
            cs = slice(j * LANES, (j + 1) * LANES)
            ysg = su_ref[:, cs].astype(F32) * svo[j]
            ycat_ref[:, ATTN_WIDTH + j * LANES:ATTN_WIDTH + (j + 1) * LANES] = ysg.astype(ycat_ref.dtype)
            svo_ref[:, cs], vhat_ref[:, cs], rstd_ref[:, cs] = (t.astype(svo_ref.dtype) for t in (svo[j], vhat[j], rstd[j]))

    blk = lambda w: pl.BlockSpec((BLK, w), lambda n: (n, 0))
    return _call(
        body, "even_mix", (nb,),
        [pl.BlockSpec(memory_space=pltpu.SMEM), blk(512), _full((seq, LANES)), _full((seq, LANES)), blk(512), blk(512),
         _full((1, 512)), _full((1, 512)), _full((8, BLK, BLK)), _full((BLK, 512)), _full((LANES, LANES))],
        [blk(D_MODEL), blk(LANES)] + [blk(SG_WIDTH)] * 3,
        [_sds((seq, D_MODEL), ACT_DTYPE), _sds((seq, LANES))] + [_sds((seq, SG_WIDTH), ACT_DTYPE)] * 3,
        (sink, q, k, v, su, sv, sgln_g, sgln_b, sgw, sgb_full, e2), "parallel", rider=rider)


def _even_out(ycat, g, x, mod, mod_next, w_out, w_in4_next, ln_g, ln_b, seq, rider=None):
    tm = _row_tile(seq, 512)
    cs = ODD_IN // 4

    def body(y_ref, g_ref, x_ref, mod_ref, modn_ref, wo_ref, wi_ref, g1_ref, b1_ref, zhat_ref, rstd_ref, x1_ref, xr_ref, gn_ref):
        gg = g_ref[...].astype(F32)
        out = _mm(y_ref[...].astype(F32) * (gg * _sigmoid(gg)), wo_ref[...])
        z = ALPHA * x_ref[...] + mod_ref[2:3, :] * out
        zhat, rstd = _ln_stats(z)
        zhat_ref[...] = zhat
        rstd_ref[...] = rstd
        x1 = zhat * g1_ref[...] + b1_ref[...]
        x1_ref[...] = x1
        hb = (x1 * (1.0 + modn_ref[1:2, :]) + modn_ref[0:1, :]).astype(MXU_DTYPE)
        for s in range(2):
            xr_ref[:, s * cs:(s + 1) * cs] = jnp.dot(hb, wi_ref[s], preferred_element_type=F32)
            gn_ref[:, s * cs:(s + 1) * cs] = jnp.dot(hb, wi_ref[2 + s], preferred_element_type=F32).astype(gn_ref.dtype)

    return _call(
        body, "even_out", (seq // tm,),
        [_rows(tm, D_MODEL)] * 3 + [_full((3, D_MODEL)), _full((3, D_MODEL)), _const((D_MODEL, D_MODEL)), _const((4, D_MODEL, cs)),
                                    _full((1, D_MODEL)), _full((1, D_MODEL))],
        [_rows(tm, D_MODEL), _rows(tm, 1)] + [_rows(tm, D_MODEL)] * 3,
        [_sds((seq, D_MODEL)), _sds((seq, 1))] + [_sds((seq, D_MODEL))] * 2 + [_sds((seq, D_MODEL), ACT_DTYPE)],
        (ycat, g, x, mod, mod_next, w_out, w_in4_next, ln_g, ln_b), "parallel", rider=rider)


def _halo_specs(tm, seq, width, order=lambda i: i):
    per = tm // 8
    last = seq // 8 - 1
    return [pl.BlockSpec((8, width), lambda i: (jnp.maximum(order(i) * per - 1, 0), 0)),
            pl.BlockSpec((tm, width), lambda i: (order(i), 0)),
            pl.BlockSpec((8, width), lambda i: (jnp.minimum((order(i) + 1) * per, last), 0))]


def _extended(prev_ref, main_ref, next_ref, i, n_steps):
    prev = jnp.where(i > 0, prev_ref[...], 0.0)
    nxt = jnp.where(i < n_steps - 1, next_ref[...], 0.0)
    return jnp.concatenate([prev, main_ref[...], nxt], axis=0)


def _shifted(ext, off, tm):
    if off == 0:
        return ext[8:8 + tm]
    return pltpu.roll(ext, (-off) % ext.shape[0], 0)[8:8 + tm]


SCAN_SUB = 8


def _lru_gate(xh, pre, bias, sp, hs, d):
    r = _sigmoid(pre[:, 0:LANES] + bias[2 * d:2 * d + 1, hs])
    ig = _sigmoid(pre[:, LANES:2 * LANES] + bias[2 * d + 1:2 * d + 2, hs])
    neg_log_a = RG_LRU_C * r * sp[d:d + 1, hs]
    a = jnp.exp(-neg_log_a)
    u = jnp.tanh(neg_log_a) * (a * a + 1.0)
    inv_s = lax.rsqrt(jnp.maximum(u, jnp.finfo(F32).tiny))
    return r, ig, a, u * inv_s, inv_s


def _conv_block(xp_ref, xm_ref, xn_ref, cw_ref, cb_ref, blk, steps, tm):
    ext = _extended(xp_ref, xm_ref, xn_ref, blk, steps)
    return cb_ref[...] + sum(cw_ref[kk:kk + 1, :] * _shifted(ext, kk - 2, tm) for kk in range(4))


def _scan_tiles(a_ref, b_ref, h_ref, hprev_ref, carry_h, carry_a, rows, descending, post):
    sub = SCAN_SUB
    tiles = rows // sub
    row = lax.broadcasted_iota(jnp.int32, (sub, D_MODEL), 0)

    def shift(v, d, fill):
        if descending:
            return jnp.where(row <= sub - 1 - d, pltpu.roll(v, sub - d, 0), fill)
        return jnp.where(row >= d, pltpu.roll(v, d, 0), fill)

    def last(v):
        return jnp.broadcast_to(v[0:1, :] if descending else v[sub - 1:sub, :], v.shape)

    def tile(j, c):
        ch, ca = c
        r0 = pl.multiple_of(((tiles - 1 - j) if descending else j) * sub, sub)
        at = a_ref[pl.ds(r0, sub), :]
        bt = b_ref[pl.ds(r0, sub), :]
        coef = shift(at, 1, ca) if post else at
        acc_a, acc_b = coef, bt
        for d in (1, 2, 4):
            acc_b = acc_b + acc_a * shift(acc_b, d, 0.0)
            acc_a = acc_a * shift(acc_a, d, 1.0)
        h = acc_b + acc_a * ch
        h_ref[pl.ds(r0, sub), :] = h
        if post:
            return last(h), last(at)
        hprev_ref[pl.ds(r0, sub), :] = shift(h, 1, ch)
        return last(h), ca

    ch, ca = lax.fori_loop(0, tiles, tile, (carry_h[...], carry_a[...]), unroll=4)
    carry_h[...] = ch
    carry_a[...] = ca


def _lru_fwd(xr, xc, conv_w, conv_b, wcat, bias, lam, seq, d):
    tb = _row_tile(seq, 512)
    steps = seq // tb
    descending = d == 1
    order = (lambda i: steps - 1 - i) if descending else (lambda i: i)
    with_conv = xc is None
    n_x = 5 if with_conv else 1

    def body(*refs):
        x_refs, (w_ref, bias_ref, lam_ref) = refs[:n_x], refs[n_x:n_x + 3]
        h_ref, hp_ref, a_ref, r_ref, i_ref, s_ref, q_ref = refs[n_x + 3:n_x + 10]
        b_scr, carry_h, carry_a = refs[-3:]
        i = pl.program_id(0)

        @pl.when(i == 0)
        def _():
            carry_h[...] = jnp.zeros_like(carry_h)
            carry_a[...] = jnp.zeros_like(carry_a)

        if with_conv:
            xc_ref = refs[n_x + 10]
            xc_ref[...] = _conv_block(*x_refs, order(i), steps, tb)
        else:
            xc_ref = x_refs[0]
        sp, _ = _softplus_neg(lam_ref[...])
        bias = bias_ref[...]
        for h in range(RNN_HEADS):
            hs = slice(h * LANES, (h + 1) * LANES)
            xh = xc_ref[:, hs]
            r, ig, a, s, q = _lru_gate(xh, _mm(xh, w_ref[h, :, 2 * d * LANES:2 * (d + 1) * LANES]), bias, sp, hs, d)
            a_ref[:, hs] = a
            b_scr[:, hs] = s * ig * xh
            for ref, val in ((r_ref, r), (i_ref, ig), (s_ref, s), (q_ref, q)):
                ref[:, hs] = val.astype(ref.dtype)
        _scan_tiles(a_ref, b_scr, h_ref, hp_ref, carry_h, carry_a, tb, descending, post=False)

    row_spec = pl.BlockSpec((tb, D_MODEL), lambda i: (order(i), 0))
    if with_conv:
        x_specs, x_args = _halo_specs(tb, seq, D_MODEL, order) + [_full((4, D_MODEL)), _full((1, D_MODEL))], (xr, xr, xr, conv_w, conv_b)
    else:
        x_specs, x_args = [row_spec], (xc,)
    n_out = 8 if with_conv else 7
    return pl.pallas_call(
        body, name="lru_fwd_%d" % d, grid=(steps,),
        in_specs=x_specs + [_full((8, LANES, 512)), _full((4, D_MODEL)), _full((2, D_MODEL))],
        out_specs=[row_spec] * n_out,
        out_shape=[_sds((seq, D_MODEL))] * 3 + [_sds((seq, D_MODEL), ACT_DTYPE)] * 4 + [_sds((seq, D_MODEL))] * (n_out - 7),
        scratch_shapes=[pltpu.VMEM((tb, D_MODEL), F32)] + [pltpu.VMEM((SCAN_SUB, D_MODEL), F32)] * 2,
        compiler_params=_params("arbitrary"),
    )(*x_args, wcat, bias, lam)


def _odd_out_and_loss(hf, hr, g, x1, tgt, mod, w_out, ln_g, ln_b, seq):
    tm = _row_tile(seq, 512)

    def body(hf_ref, hr_ref, g_ref, x_ref, t_ref, mod_ref, w_ref, lg_ref, lb_ref,
             dhs_ref, dg_ref, dres_ref, loss_ref, dw_ref, vec_ref):
        @pl.when(pl.program_id(0) == 0)
        def _():
            loss_ref[...] = jnp.zeros_like(loss_ref)
            dw_ref[...] = jnp.zeros_like(dw_ref)
            vec_ref[...] = jnp.zeros_like(vec_ref)

        gg = g_ref[...].astype(F32)
        sg = _sigmoid(gg)
        silu = gg * sg
        hsum = hf_ref[...] + hr_ref[...]
        y = hsum * silu
        out = _mm(y, w_ref[...])
        gate = mod_ref[2:3, :]
        z = ALPHA * x_ref[...] + gate * out
        zhat, rstd = _ln_stats(z)
        x2 = zhat * lg_ref[...] + lb_ref[...]
        err = x2 - t_ref[...]
        loss_ref[...] += 0.5 * jnp.sum(jnp.mean(err * err, axis=-1, keepdims=True))
        dx2 = err * (1.0 / D_MODEL)
        dz = _ln_bwd(dx2, zhat, rstd, lg_ref[...])
        vec_ref[0:1, :] += jnp.sum(dx2 * zhat, axis=0, keepdims=True)
        vec_ref[1:2, :] += jnp.sum(dx2, axis=0, keepdims=True)
        vec_ref[2:3, :] += jnp.sum(dz * out, axis=0, keepdims=True)
        dres_ref[...] = ALPHA * dz
        dout = gate * dz
        dw_ref[...] += _mm_tn(y, dout)
        dy = _mm_nt(dout, w_ref[...])
        dhs_ref[...] = dy * silu
        dg_ref[...] = (dy * hsum * (sg * (1.0 + gg * (1.0 - sg)))).astype(dg_ref.dtype)

    return pl.pallas_call(
        body, name="odd_out_loss", grid=(seq // tm,),
        in_specs=[_rows(tm, D_MODEL)] * 5 + [_full((3, D_MODEL)), _const((D_MODEL, D_MODEL)),
                                             _full((1, D_MODEL)), _full((1, D_MODEL))],
        out_specs=[_rows(tm, D_MODEL)] * 3 + [_full((8, LANES)), _full((D_MODEL, D_MODEL)), _full((8, D_MODEL))],
        out_shape=[_sds((seq, D_MODEL)), _sds((seq, D_MODEL), ACT_DTYPE), _sds((seq, D_MODEL)), _sds((8, LANES)),
                   _sds((D_MODEL, D_MODEL)), _sds((8, D_MODEL))],
        compiler_params=_params("arbitrary"),
    )(hf, hr, g, x1, tgt, mod, w_out, ln_g, ln_b)


def _lru_bwd(xc, dhs, hprev, a_d, r_d, i_d, s_d, q_d, wcat, lam, seq, d):
    tb = _row_tile(seq, 512)
    steps = seq // tb
    descending = d == 0
    order = (lambda i: steps - 1 - i) if descending else (lambda i: i)
    cols = slice(2 * d * LANES, 2 * (d + 1) * LANES)

    def body(xc_ref, dhs_ref, hp_ref, a_ref, r_ref, i_ref, s_ref, q_ref, w_ref, lam_ref, dxc_ref, dw_ref, vec_ref,
             g_scr, carry_h, carry_a):
        i = pl.program_id(0)

        @pl.when(i == 0)
        def _():
            dw_ref[...] = jnp.zeros_like(dw_ref)
            vec_ref[...] = jnp.zeros_like(vec_ref)
            carry_h[...] = jnp.zeros_like(carry_h)
            carry_a[...] = jnp.zeros_like(carry_a)

        sp, dsp = _softplus_neg(lam_ref[...])
        _scan_tiles(a_ref, dhs_ref, g_scr, None, carry_h, carry_a, tb, descending, post=True)
        for h in range(RNN_HEADS):
            hs = slice(h * LANES, (h + 1) * LANES)
            xh, a = xc_ref[:, hs], a_ref[:, hs]
            r, ig, s = r_ref[:, hs].astype(F32), i_ref[:, hs].astype(F32), s_ref[:, hs].astype(F32)
            db = g_scr[:, hs]
            da = db * hp_ref[:, hs]
            dlog_a = da * a - (db * ig * xh) * (a * a * q_ref[:, hs].astype(F32))
            dpr = dlog_a * (-RG_LRU_C) * sp[d:d + 1, hs] * r * (1.0 - r)
            dpi = db * s * xh * ig * (1.0 - ig)
            vec_ref[0:1, hs] += jnp.sum(dpr, axis=0, keepdims=True)
            vec_ref[1:2, hs] += jnp.sum(dpi, axis=0, keepdims=True)
            vec_ref[2:3, hs] += jnp.sum(dlog_a * r, axis=0, keepdims=True) * (-RG_LRU_C) * dsp[d:d + 1, hs]
            dcat = jnp.concatenate([dpr, dpi], axis=1)
            dw_ref[h] += _mm_tn(xh, dcat)
            dxc_ref[:, hs] = db * s * ig + _mm_nt(dcat, w_ref[h, :, cols])

    row_spec = pl.BlockSpec((tb, D_MODEL), lambda i: (order(i), 0))
    return pl.pallas_call(
        body, name="lru_bwd_%d" % d, grid=(steps,),
        in_specs=[row_spec] * 8 + [_full((8, LANES, 512)), _full((2, D_MODEL))],
        out_specs=[row_spec, _full((8, LANES, 2 * LANES)), _full((8, D_MODEL))],
        out_shape=[_sds((seq, D_MODEL)), _sds((8, LANES, 2 * LANES)), _sds((8, D_MODEL))],
        scratch_shapes=[pltpu.VMEM((tb, D_MODEL), F32)] + [pltpu.VMEM((SCAN_SUB, D_MODEL), F32)] * 2,
        compiler_params=_params("arbitrary"),
    )(xc, dhs, hprev, a_d, r_d, i_d, s_d, q_d, wcat, lam)


def _odd_proj_bwd(dxc_f, dxc_r, xr, dg, x1, dres, mod, conv_w, w_in4, seq):
    tm = _row_tile(seq, 512)
    steps = seq // tm

    def body(fp_ref, fm_ref, fn_ref, rp_ref, rm_ref, rn_ref, xp_ref, xm_ref, xn_ref, dg_ref, x_ref, dres_ref, mod_ref, cw_ref,
             w_ref, dx_ref, dw_ref, vec_ref, dpb_ref):
        i = pl.program_id(0)

        @pl.when(i == 0)
        def _():
            vec_ref[...] = jnp.zeros_like(vec_ref)
            dw_ref[...] = jnp.zeros_like(dw_ref)

        dxc_m = fm_ref[...] + rm_ref[...]
        dext = jnp.concatenate([jnp.where(i > 0, fp_ref[...] + rp_ref[...], 0.0), dxc_m,
                                jnp.where(i < steps - 1, fn_ref[...] + rn_ref[...], 0.0)], axis=0)
        xext = _extended(xp_ref, xm_ref, xn_ref, i, steps)
        dxr = sum(cw_ref[kk:kk + 1, :] * _shifted(dext, 2 - kk, tm) for kk in range(4))
        for kk in range(4):
            vec_ref[kk:kk + 1, :] += jnp.sum(dxc_m * _shifted(xext, kk - 2, tm), axis=0, keepdims=True)
        vec_ref[4:5, :] += jnp.sum(dxc_m, axis=0, keepdims=True)
        dpb_ref[:, :D_MODEL] = dxr.astype(dpb_ref.dtype)
        dpb_ref[:, D_MODEL:] = dg_ref[...].astype(dpb_ref.dtype)
        cs = ODD_IN // 4
        dh = sum(_mm_nt(dpb_ref[:, s * cs:(s + 1) * cs], w_ref[s]) for s in range(4))
        x = x_ref[...]
        h_t = (x * (1.0 + mod_ref[1:2, :]) + mod_ref[0:1, :]).T.astype(MXU_DTYPE)
        for s in range(4):
            dw_ref[s] += jnp.dot(h_t, dpb_ref[:, s * cs:(s + 1) * cs], preferred_element_type=F32)
        vec_ref[5:6, :] += jnp.sum(dh, axis=0, keepdims=True)
        vec_ref[6:7, :] += jnp.sum(dh * x, axis=0, keepdims=True)
        dx_ref[...] = dres_ref[...] + dh * (1.0 + mod_ref[1:2, :])

    return pl.pallas_call(
        body, name="odd_proj_bwd", grid=(steps,),
        in_specs=_halo_specs(tm, seq, D_MODEL) * 3 + [_rows(tm, D_MODEL)] * 3
        + [_full((3, D_MODEL)), _full((4, D_MODEL)), _const((4, D_MODEL, ODD_IN // 4))],
        out_specs=[_rows(tm, D_MODEL), _const((4, D_MODEL, ODD_IN // 4)), _full((8, D_MODEL))],
        out_shape=[_sds((seq, D_MODEL)), _sds((4, D_MODEL, ODD_IN // 4)), _sds((8, D_MODEL))],
        scratch_shapes=[pltpu.VMEM((tm, ODD_IN), MXU_DTYPE)],
        compiler_params=_params("arbitrary"),
    )(dxc_f, dxc_f, dxc_f, dxc_r, dxc_r, dxc_r, xr, xr, xr, dg, x1, dres, mod, conv_w, w_in4)


def _even_out_bwd(dx1, zhat, rstd, ycat, g, mod, ln_g, w_out, seq, rider=None):
    tm = _row_tile(seq, 512)
    steps = seq // tm

    def body(dx_ref, zh_ref, rs_ref, y_ref, g_ref, mod_ref, lg_ref, w_ref, dy_ref, dg_ref, dres_ref, dw_ref, vec_ref):
        i = pl.program_id(0)

        @pl.when(i == 0)
        def _():
            dw_ref[...] = jnp.zeros_like(dw_ref)
            vec_ref[...] = jnp.zeros_like(vec_ref)

        zhat = zh_ref[...]
        dx1_ = dx_ref[...]
        dz = _ln_bwd(dx1_, zhat, rs_ref[...], lg_ref[...])
        vec_ref[0:1, :] += jnp.sum(dx1_ * zhat, axis=0, keepdims=True)
        vec_ref[1:2, :] += jnp.sum(dx1_, axis=0, keepdims=True)
        dres_ref[...] = ALPHA * dz
        gate = mod_ref[2:3, :]
        gg = g_ref[...].astype(F32)
        sg = _sigmoid(gg)
        silu = gg * sg
        ycat_ = y_ref[...].astype(F32)
        dw_ref[...] += _mm_tn(ycat_ * silu, dz)
        dy = _mm_nt(gate * dz, w_ref[...])
        dy_ref[...] = (dy * silu).astype(dy_ref.dtype)
        dg_ref[...] = (dy * ycat_ * (sg * (1.0 + gg * (1.0 - sg)))).astype(dg_ref.dtype)

        @pl.when(i == steps - 1)
        def _():
            m_acc = dw_ref[...]
            vec_ref[2:3, :] = jnp.sum(w_ref[...].astype(F32) * m_acc, axis=0, keepdims=True)
            dw_ref[...] = m_acc * gate

    return _call(
        body, "even_out_bwd", (steps,),
        [_rows(tm, D_MODEL), _rows(tm, D_MODEL), _rows(tm, 1), _rows(tm, D_MODEL), _rows(tm, D_MODEL), _full((3, D_MODEL)),
         _full((1, D_MODEL)), _const((D_MODEL, D_MODEL))],
        [_rows(tm, D_MODEL)] * 3 + [_full((D_MODEL, D_MODEL)), _full((8, D_MODEL))],
        [_sds((seq, D_MODEL), ACT_DTYPE), _sds((seq, D_MODEL), ACT_DTYPE), _sds((seq, D_MODEL)), _sds((D_MODEL, D_MODEL)),
         _sds((8, D_MODEL))],
        (dx1, zhat, rstd, ycat, g, mod, ln_g, w_out), "arbitrary", rider=rider)


def _even_mix_bwd(q, k, v, lse, ycat, dycat, su, svo_s, vhat_s, rstd_s, sink, sgln_g, sgln_b, sgw, e2, e8, seq, rider=None):
    nb = seq // BLK

    def body(sink_ref, q_ref, k_ref, v_ref, lse_ref, y_ref, dy_ref, su_ref, svo_ref, vhat_ref, rstd_ref, lng_ref, lnb_ref, sgw_ref,
             e2_ref, e8_ref, dq_ref, dsu_ref, dsv_ref, dk_ref, dv_ref, dsgw_ref, dsgb_ref, vec_ref, dsink_ref, dsgb_acc):
        n = pl.program_id(0)

        @pl.when(n == 0)
        def _():
            dk_ref[...] = jnp.zeros_like(dk_ref)
            dv_ref[...] = jnp.zeros_like(dv_ref)
            dsgw_ref[...] = jnp.zeros_like(dsgw_ref)
            dsgb_acc[...] = jnp.zeros_like(dsgb_acc)
            vec_ref[...] = jnp.zeros_like(vec_ref)
            dsink_ref[...] = jnp.zeros_like(dsink_ref)

        kband = _band(k_ref, n, nb)
        vband = _band(v_ref, n, nb)
        bias = _band_bias(n, seq)
        lane = _lane_iota((BLK, LANES))
        row8 = lax.broadcasted_iota(jnp.int32, (8, LANES), 0)
        lse = lse_ref[...]
        dkb = jnp.zeros((LANES, 3 * BLK), F32)
        dvb = jnp.zeros((LANES, 3 * BLK), F32)
        dsink = jnp.zeros((8, LANES), F32)
        q_tile = lambda j: q_ref[:, j * LANES:(j + 1) * LANES].astype(F32)
        do_tile = lambda j: dy_ref[:, j * LANES:(j + 1) * LANES].astype(F32)
        dq = [jnp.zeros((BLK, LANES), F32) for _ in range(ATTN_WIDTH // LANES)]
        for kv in range(N_Q_HEADS // Q_PER_KV):
            heads = range(Q_PER_KV * kv, Q_PER_KV * (kv + 1))
            lse4, delta4 = [], []
            for h in heads:
                head_lanes = (lane < HEAD_DIM) if h % 2 == 0 else (lane >= HEAD_DIM)
                lse4.append(jnp.sum(jnp.where(lane == h, lse, 0.0), axis=1, keepdims=True))
                o_tile = y_ref[:, (h // 2) * LANES:(h // 2 + 1) * LANES].astype(F32)
                delta4.append(jnp.sum(jnp.where(head_lanes, do_tile(h // 2) * o_tile, 0.0), axis=1, keepdims=True))
            lse4, delta4 = jnp.concatenate(lse4, axis=0), jnp.concatenate(delta4, axis=0)
            q4, do4 = _stack_heads(q_tile, kv), _stack_heads(do_tile, kv)
            s = _mm_nt(q4, kband) * (HEAD_DIM ** -0.5) + bias
            p = jnp.exp(s - lse4)
            wsink = jnp.exp(_per_head_column([sink_ref[h] for h in heads]) - lse4) * delta4
            ds = p * (_mm_nt(do4, vband) - delta4) * (HEAD_DIM ** -0.5)
            dq4 = _mm(ds, kband)
            dkb = dkb + _mm_tn(q4, ds)
            dvb = dvb + _mm_tn(do4, p)
            for g, h in enumerate(heads):
                dq[h // 2] = dq[h // 2] + _from_kv_lanes(dq4[g * BLK:(g + 1) * BLK], h)
                dsink = dsink + jnp.where(row8 == h, -jnp.sum(wsink[g * BLK:(g + 1) * BLK]), 0.0)
        for j in range(ATTN_WIDTH // LANES):
            dq_ref[:, j * LANES:(j + 1) * LANES] = dq[j].astype(dq_ref.dtype)
        dsink_ref[...] += dsink
        prev = jnp.maximum(n - 1, 0)
        nxt = jnp.minimum(n + 1, nb - 1)
        for part, blk_i in enumerate((prev, n, nxt)):
            rows = pl.ds(pl.multiple_of(blk_i * BLK, BLK), BLK)
            dk_ref[rows, :] += dkb[:, part * BLK:(part + 1) * BLK].T
            dv_ref[rows, :] += dvb[:, part * BLK:(part + 1) * BLK].T

        e2 = e2_ref[...]
        lng, lnb = lng_ref[...], lnb_ref[...]
        for j in range(SG_WIDTH // LANES):
            cs = slice(j * LANES, (j + 1) * LANES)
            vhat = vhat_ref[:, cs].astype(F32)
            vn = vhat * lng[:, cs] + lnb[:, cs]
            dysg = dy_ref[:, ATTN_WIDTH + j * LANES:ATTN_WIDTH + (j + 1) * LANES].astype(F32)
            dsu_ref[:, cs] = (dysg * svo_ref[:, cs].astype(F32)).astype(dsu_ref.dtype)
            dsvo = dysg * su_ref[:, cs].astype(F32)
            dsgb_acc[:, cs] += dsvo
            d_lo = jnp.where(lane < HEAD_DIM, dsvo, 0.0)
            d_hi = dsvo - d_lo
            dsgw_ref[2 * j] += _mm_nt(d_lo, vn)
            dsgw_ref[2 * j + 1] += _mm_nt(d_hi, vn)
            dvn = _mm_tn(sgw_ref[2 * j], d_lo) + _mm_tn(sgw_ref[2 * j + 1], d_hi)
            vec_ref[0:1, cs] += jnp.sum(dvn * vhat, axis=0, keepdims=True)
            vec_ref[1:2, cs] += jnp.sum(dvn, axis=0, keepdims=True)
            dvh = dvn * lng[:, cs]
            m1 = _group_sum(dvh, e2) * (1.0 / HEAD_DIM)
            m2 = _group_sum(dvh * vhat, e2) * (1.0 / HEAD_DIM)
            dsv_ref[:, cs] = (rstd_ref[:, cs].astype(F32) * (dvh - m1 - vhat * m2)).astype(dsv_ref.dtype)

        @pl.when(n == nb - 1)
        def _():
            rest = dsgb_acc[...]
            total = jnp.zeros((8, BLK), F32)
            for _ in range(3):
                part = rest.astype(MXU_DTYPE)
                total = total + lax.dot_general(e8_ref[...], part, (((1,), (1,)), ((), ())), preferred_element_type=F32)
                rest = rest - part.astype(F32)
            dsgb_ref[...] = total

    blk = lambda w: pl.BlockSpec((BLK, w), lambda n: (n, 0))
    return _call(
        body, "even_mix_bwd", (nb,),
        [pl.BlockSpec(memory_space=pltpu.SMEM), blk(512), _full((seq, LANES)), _full((seq, LANES)), blk(LANES),
         blk(D_MODEL), blk(D_MODEL), blk(512), blk(512), blk(512), blk(512), _full((1, 512)), _full((1, 512)), _full((8, BLK, BLK)),
         _full((LANES, LANES)), _full((8, 512))],
        [blk(512), blk(512), blk(512), _full((seq, LANES)), _full((seq, LANES)), _full((8, BLK, BLK)),
         _full((8, BLK)), _full((8, 512)), _full((8, LANES))],
        [_sds((seq, 512), ACT_DTYPE), _sds((seq, 512), ACT_DTYPE), _sds((seq, 512), ACT_DTYPE), _sds((seq, LANES)), _sds((seq, LANES)),
         _sds((8, BLK, BLK)), _sds((8, BLK)), _sds((8, 512)), _sds((8, LANES))],
        (sink, q, k, v, lse, ycat, dycat, su, svo_s, vhat_s, rstd_s, sgln_g, sgln_b, sgw, e2, e8), "arbitrary",
        scratch=[pltpu.VMEM((BLK, 512), F32)], rider=rider)


def _even_proj_bwd(dq, dk, dv, dsu, dsv, dg, x, dres, mod, tabs, w_in_t, seq):
    tm = _row_tile(seq, 512)

    def body(dq_ref, dk_ref, dv_ref, dsu_ref, dsv_ref, dg_ref, x_ref, dres_ref, mod_ref, pos_ref, invf_ref, mp_ref, mm_ref, wt_ref,
             dx_ref, dw_ref, vec_ref, dpb_ref):
        @pl.when(pl.program_id(0) == 0)
        def _():
            vec_ref[...] = jnp.zeros_like(vec_ref)
            dw_ref[...] = jnp.zeros_like(dw_ref)

        cos_t, sin_p, sin_m = _rope_block(pos_ref, invf_ref, mp_ref, mm_ref)
        dt = dpb_ref.dtype
        for j in range(ATTN_WIDTH // LANES):
            cs = slice(j * LANES, (j + 1) * LANES)
            dpb_ref[:, cs] = _rope_t(dq_ref[:, cs].astype(F32), cos_t, sin_p, sin_m).astype(dt)
        dpb_ref[:, 512:640] = _rope_t(dk_ref[...], cos_t, sin_p, sin_m).astype(dt)
        dpb_ref[:, 640:768] = dv_ref[...].astype(dt)
        dpb_ref[:, 768:1280] = dsu_ref[...].astype(dt)
        dpb_ref[:, 1280:1792] = dsv_ref[...].astype(dt)
        dpb_ref[:, 1792:2816] = dg_ref[...].astype(dt)
        dpb = dpb_ref[...]
        dh = jnp.dot(dpb, wt_ref[...], preferred_element_type=F32)
        x_ = x_ref[...]
        hb = (x_ * (1.0 + mod_ref[1:2, :]) + mod_ref[0:1, :]).astype(MXU_DTYPE)
        dw_ref[...] += _mm_tn(dpb, hb)
        vec_ref[0:1, :] += jnp.sum(dh, axis=0, keepdims=True)
        vec_ref[1:2, :] += jnp.sum(dh * x_, axis=0, keepdims=True)
        dx_ref[...] = dres_ref[...] + dh * (1.0 + mod_ref[1:2, :])

    return pl.pallas_call(
        body, name="even_proj_bwd", grid=(seq // tm,),
        in_specs=[_rows(tm, 512), _rows(tm, LANES), _rows(tm, LANES), _rows(tm, 512), _rows(tm, 512), _rows(tm, D_MODEL),
                  _rows(tm, D_MODEL), _rows(tm, D_MODEL), _full((3, D_MODEL)), _rows(tm, 1)] + [_full((1, LANES))] * 3
        + [_const((EVEN_IN, D_MODEL))],
        out_specs=[_rows(tm, D_MODEL), _const((EVEN_IN, D_MODEL)), _full((8, D_MODEL))],
        out_shape=[_sds((seq, D_MODEL)), _sds((EVEN_IN, D_MODEL)), _sds((8, D_MODEL))],
        scratch_shapes=[pltpu.VMEM((tm, EVEN_IN), MXU_DTYPE)],
        compiler_params=_params("arbitrary"),
    )(dq, dk, dv, dsu, dsv, dg, x, dres, mod, *tabs, w_in_t)


def _local_step(x, posf, tgt, mod, w, seq, ride=None):
    rid = lambda make, *a: None if ride is None else make(*a)
    mxu = lambda a: a.astype(MXU_DTYPE)
    row = lambda a: a.reshape(1, -1)
    tabs = (posf, *_rope_consts())
    e2 = mxu(jnp.kron(jnp.eye(2, dtype=F32), jnp.ones((HEAD_DIM, HEAD_DIM), F32)))
    e8 = mxu(jnp.repeat(jnp.eye(N_SG_GROUPS, dtype=F32), HEAD_DIM, axis=1))
    sgw = mxu(w["ev_sg_w"])
    sgb_full = jnp.repeat(w["ev_sg_b"].T, HEAD_DIM, axis=1)
    sgln_g, sgln_b = row(w["ev_sg_ln_g"]), row(w["ev_sg_ln_b"])
    sink = w["ev_sink"].reshape(N_Q_HEADS)
    ev_w_in_t = mxu(w["ev_w_in_t"])
    if ride is None:
        ev_w_out, od_w_in, od_w_out = mxu(w["ev_w_out"]), mxu(w["od_w_in"]), mxu(w["od_w_out"])
    wcat = mxu(jnp.concatenate([w["od_w_a"][0], w["od_w_x"][0], w["od_w_a"][1], w["od_w_x"][1]], axis=2))
    gate_bias = jnp.stack([w["od_b_a"][0], w["od_b_x"][0], w["od_b_a"][1], w["od_b_x"][1]])
    conv_b = row(w["od_conv_b"])
    ln_g, ln_b = w["ln_g"], w["ln_b"]

    (q, k, v, su, sv, g0), got = _even_proj(x, mod[0], ev_w_in_t, tabs, seq, rid(_gather_rider, ride and ride["ev_w_out"]))
    if ride is not None:
        ev_w_out = got[0].reshape(D_MODEL, D_MODEL)
    (ycat, lse, *sg_saved), got = _even_mix(q, k, v, su, sv, sink, sgln_g, sgln_b, sgw, sgb_full, e2, seq,
                                 rid(_gather_rider, ride and ride["od_w_in"]))
    if ride is not None:
        od_w_in = got[0]
    (zhat0, rstd0, x1, xr, g1), got = _even_out(ycat, g0, x, mod[0], mod[1], ev_w_out, od_w_in, ln_g[0:1], ln_b[0:1], seq,
                                      rid(_gather_rider, ride and ride["od_w_out"]))
    if ride is not None:
        od_w_out = got[0].reshape(D_MODEL, D_MODEL)
    lru = (w["od_conv_w"], conv_b, wcat, gate_bias, w["od_lam"], seq)
    hf, hpf, *saved_f, xc = _lru_fwd(xr, None, *lru, 0)
    hr, hpr, *saved_r = _lru_fwd(xr, xc, *lru, 1)
    dhs, dg1, dres1, loss, d_od_w_out, vec_o = _odd_out_and_loss(hf, hr, g1, x1, tgt, mod[1], od_w_out, ln_g[1:2], ln_b[1:2], seq)
    dxc_f, dw_f, vec_f = _lru_bwd(xc, dhs, hpf, *saved_f, wcat, w["od_lam"], seq, 0)
    dxc_r, dw_r, vec_r = _lru_bwd(xc, dhs, hpr, *saved_r, wcat, w["od_lam"], seq, 1)
    dx1, d_od_w_in, vec_p = _odd_proj_bwd(dxc_f, dxc_r, xr, dg1, x1, dres1, mod[1], w["od_conv_w"], od_w_in, seq)
    d_od_w_a = jnp.stack([dw_f[:, :, 0:128], dw_r[:, :, 0:128]])
    d_od_w_x = jnp.stack([dw_f[:, :, 128:256], dw_r[:, :, 128:256]])
    od_parts = [d_od_w_in.reshape(4, 2, 512, 512), d_od_w_out.reshape(4, 2, 128, D_MODEL),
                d_od_w_a.reshape(4, 2, 2 * BLK, BLK), d_od_w_x.reshape(4, 2, 2 * BLK, BLK)]
    (dycat, dg0, dres0, d_ev_w_out, vec_e), got_od = _even_out_bwd(dx1, zhat0, rstd0, ycat, g0, mod[0], ln_g[0:1], ev_w_out, seq,
                                                                   rid(_sibling_swap_rider, od_parts))
    if ride is not None:
        od_sums = _sum_sibling(ride["core"], od_parts, got_od, [ride["wire"]] * 4, "sum_sibling_od")
    (dq, dsu, dsv, dk, dv, d_sgw, d_sgb, vec_s, d_sink), od_slots = _even_mix_bwd(
        q, k, v, lse, ycat, dycat, su, *sg_saved, sink, sgln_g, sgln_b, sgw, e2, e8, seq,
        rid(_chip_exchange_rider, ride and od_sums))
    grad_x, d_ev_w_in_t, vec_x = _even_proj_bwd(dq, dk, dv, dsu, dsv, dg0, x, dres0, mod[0], tabs, ev_w_in_t, seq)

    rows, dmod_blk = _pack_small(vec_x, vec_e, vec_p, vec_o, vec_f, vec_r, vec_s, d_sink, d_sgb, loss)
    grads = {"rows": rows, "dmod_blk": dmod_blk, "ev_w_in_t": d_ev_w_in_t, "ev_w_out": d_ev_w_out, "ev_sg_w": d_sgw}
    if ride is None:
        grads.update({"od_w_in": d_od_w_in, "od_w_out": d_od_w_out, "od_w_a": d_od_w_a, "od_w_x": d_od_w_x})
    else:
        grads["od_slots"] = od_slots
    return grad_x, grads


ROW_DMOD, ROW_LN, ROW_SG_LN, ROW_SG_B, ROW_CONV_W, ROW_CONV_B, ROW_B_A, ROW_B_X, ROW_LAM, ROW_SINK, ROW_LOSS = (
    0, 6, 10, 11, 12, 16, 17, 19, 21, 23, 24)
SMALL_ROWS = 64


def _pack_small(vec_x, vec_e, vec_p, vec_o, vec_f, vec_r, vec_s, d_sink, d_sgb, loss):
    def body(x_ref, e_ref, p_ref, o_ref, f_ref, r_ref, s_ref, sink_ref, sgb_ref, loss_ref, rows_ref, dmod_ref):
        rows_ref[...] = jnp.zeros_like(rows_ref)
        dmod_ref[...] = jnp.zeros_like(dmod_ref)
        put = [(ROW_DMOD, x_ref, 0), (ROW_DMOD + 1, x_ref, 1), (ROW_DMOD + 2, e_ref, 2), (ROW_DMOD + 3, p_ref, 5),
               (ROW_DMOD + 4, p_ref, 6), (ROW_DMOD + 5, o_ref, 2), (ROW_LN, e_ref, 0), (ROW_LN + 1, e_ref, 1),
               (ROW_LN + 2, o_ref, 0), (ROW_LN + 3, o_ref, 1), (ROW_CONV_B, p_ref, 4), (ROW_B_A, f_ref, 0),
               (ROW_B_A + 1, r_ref, 0), (ROW_B_X, f_ref, 1), (ROW_B_X + 1, r_ref, 1), (ROW_LAM, f_ref, 2), (ROW_LAM + 1, r_ref, 2)]
        put += [(ROW_CONV_W + k, p_ref, k) for k in range(4)]
        for dst, ref, src in put:
            rows_ref[dst:dst + 1, :] = ref[src:src + 1, :]
            if dst < 6:
                dmod_ref[dst:dst + 1, :] = ref[src:src + 1, :]
        rows_ref[ROW_SG_LN:ROW_SG_LN + 1, 0:SG_WIDTH] = s_ref[0:1, :]
        rows_ref[ROW_SG_LN:ROW_SG_LN + 1, SG_WIDTH:2 * SG_WIDTH] = s_ref[1:2, :]
        lane = _lane_iota((1, LANES))
        sink = jnp.zeros((1, LANES), F32)
        for h in range(N_Q_HEADS):
            rows_ref[ROW_SG_B:ROW_SG_B + 1, h * LANES:(h + 1) * LANES] = sgb_ref[h:h + 1, :]
            sink = jnp.where(lane == h, sink_ref[h:h + 1, :], sink)
        rows_ref[ROW_SINK:ROW_SINK + 1, 0:LANES] = sink
        rows_ref[ROW_LOSS:ROW_LOSS + 1, 0:LANES] = jnp.where(lane == 0, loss_ref[0:1, :], 0.0)

    return pl.pallas_call(body, name="pack_small", out_shape=[_sds((SMALL_ROWS, D_MODEL)), _sds((8, D_MODEL))])(
        vec_x, vec_e, vec_p, vec_o, vec_f, vec_r, vec_s, d_sink, d_sgb, loss)


def _allgather8(block, name):
    m_per, n = block.shape

    def body(x_ref, out_ref, send_sems, recv_sems, local_sem):
        x, y, c = _place()
        me, sibling = (x, y, c), (x, y, 1 - c)
        chips = [(1 - x, y), (x, 1 - y), (1 - x, 1 - y)]

        def rows(px, py, pc):
            return out_ref.at[pl.ds((4 * px + 2 * py + pc) * m_per, m_per), :]

        def copy(k, blk, to, src=None):
            return pltpu.make_async_remote_copy(src_ref=rows(*blk) if src is None else src, dst_ref=rows(*blk),
                                                send_sem=send_sems.at[k], recv_sem=recv_sems.at[k], device_id=to,
                                                device_id_type=MESH)

        mine = pltpu.make_async_copy(x_ref, rows(*me), local_sem)
        mine.start()
        first = [copy(0, me, sibling, src=x_ref)] + [copy(1 + j, me, (*chip, c), src=x_ref) for j, chip in enumerate(chips)]
        for cp in first:
            cp.start()
        passed = [copy(4 + j, (*chip, c), sibling) for j, chip in enumerate(chips)]
        for j, chip in enumerate(chips):
            copy(1 + j, (*chip, c), me).wait_recv()
            passed[j].start()
        copy(0, sibling, me).wait_recv()
        for j, chip in enumerate(chips):
            copy(4 + j, (*chip, 1 - c), me).wait_recv()
        for cp in first + passed:
            cp.wait_send()
        mine.wait()

    return pl.pallas_call(
        body, name=name, out_shape=_sds((8 * m_per, n), block.dtype),
        in_specs=[pl.BlockSpec(memory_space=pltpu.VMEM)], out_specs=pl.BlockSpec(memory_space=pltpu.VMEM),
        scratch_shapes=[pltpu.SemaphoreType.DMA((7,)), pltpu.SemaphoreType.DMA((7,)), pltpu.SemaphoreType.DMA],
        compiler_params=pltpu.CompilerParams(vmem_limit_bytes=VMEM_LIMIT),
    )(block)


class _Copies:
    def __init__(self, send_sems, recv_sems, local_sems, stages):
        self.send_sems, self.recv_sems, self.local_sems, self.stages = send_sems, recv_sems, local_sems, stages
        self.sent, self.staged, self.locals = [], [], []

    def remote(self, k, src, dst, to):
        return pltpu.make_async_remote_copy(src_ref=src, dst_ref=dst, send_sem=self.send_sems.at[k], recv_sem=self.recv_sems.at[k],
                                            device_id=to, device_id_type=MESH)

    def send(self, k, src, dst, to):
        cp = self.remote(k, src, dst, to)
        cp.start()
        self.sent.append(cp)

    def arrived(self, k, dst, frm):
        self.remote(k, dst, dst, frm).wait_recv()

    def local(self, src, dst):
        k = len(self.staged)
        cp = pltpu.make_async_copy(src, self.stages[k], self.local_sems.at[2 * k])
        cp.start()
        self.staged.append((cp, dst))

    def flush(self):
        for k in range(len(self.locals), len(self.staged)):
            cp, dst = self.staged[k]
            cp.wait()
            out = pltpu.make_async_copy(self.stages[k], dst, self.local_sems.at[2 * k + 1])
            out.start()
            self.locals.append(out)

    def drain(self):
        self.flush()
        for cp in self.sent:
            cp.wait_send()
        for cp in self.locals:
            cp.wait()


def _comm_call(body, name, ins, out_shapes, n_remote, stages):
    n_in, n_out = len(ins), len(out_shapes)

    def kern(*refs):
        in_refs, out_refs = refs[:n_in], refs[n_in:n_in + n_out]
        send_sems, recv_sems, local_sems = refs[n_in + n_out:n_in + n_out + 3]
        body(_Copies(send_sems, recv_sems, local_sems, refs[n_in + n_out + 3:]), in_refs, out_refs)

    hbm = pl.BlockSpec(memory_space=pl.ANY)
    return pl.pallas_call(
        kern, name=name, out_shape=out_shapes, in_specs=[hbm] * n_in, out_specs=[hbm] * n_out,
        scratch_shapes=[pltpu.SemaphoreType.DMA((n_remote,)), pltpu.SemaphoreType.DMA((n_remote,)),
                        pltpu.SemaphoreType.DMA((2 * len(stages),))] + [pltpu.VMEM(s, d) for s, d in stages],
        compiler_params=pltpu.CompilerParams(vmem_limit_bytes=VMEM_LIMIT),
    )(*ins)


def _gather_to_all(cps, pairs, me, sibling, other_chips, c, base):
    idx = lambda p: 4 * p[0] + 2 * p[1] + p[2]
    for i, (src, dst) in enumerate(pairs):
        cps.local(src, dst.at[idx(me)])
        cps.send(base + 7 * i, src, dst.at[idx(me)], sibling)
        for j, chip in enumerate(other_chips):
            cps.send(base + 7 * i + 1 + j, src, dst.at[idx(me)], (*chip, c))
    cps.flush()
    for j, chip in enumerate(other_chips):
        for i, (_, dst) in enumerate(pairs):
            got = dst.at[idx((*chip, c))]
            cps.arrived(base + 7 * i + 1 + j, got, (*chip, c))
            cps.send(base + 7 * i + 4 + j, got, got, sibling)
    for i, (_, dst) in enumerate(pairs):
        cps.arrived(base + 7 * i, dst.at[idx(sibling)], sibling)
        for j, chip in enumerate(other_chips):
            cps.arrived(base + 7 * i + 4 + j, dst.at[idx((*chip, 1 - c))], sibling)


def _gather_weights(shards, small):
    n = len(shards)

    def body(cps, ins, outs):
        x, y, c = _place()
        me, sibling, mine = (x, y, c), (x, y, 1 - c), 2 * x + y
        chips = [(1 - x, y), (x, 1 - y), (1 - x, 1 - y)]
        for i in range(n):
            cps.local(ins[i], outs[i].at[mine])
        for j, (px, py) in enumerate(chips):
            for i in range(n):
                hr = shards[i].shape[0] // 2
                rows = pl.ds(c * hr, hr)
                cps.send(6 * i + j, ins[i].at[rows], outs[i].at[mine, rows], (px, py, c))
        _gather_to_all(cps, [(ins[n], outs[n])], me, sibling, chips, c, 6 * n)
        for j, (px, py) in enumerate(chips):
            for i in range(n):
                hr = shards[i].shape[0] // 2
                got = outs[i].at[2 * px + py, pl.ds(c * hr, hr)]
                cps.arrived(6 * i + j, got, (px, py, c))
                cps.send(6 * i + 3 + j, got, got, sibling)
        for j, (px, py) in enumerate(chips):
            for i in range(n):
                hr = shards[i].shape[0] // 2
                cps.arrived(6 * i + 3 + j, outs[i].at[2 * px + py, pl.ds((1 - c) * hr, hr)], sibling)
        cps.drain()

    return _comm_call(body, "gather_weights", list(shards) + [small],
                      [_sds((4,) + s.shape, s.dtype) for s in shards] + [_sds((8,) + small.shape, small.dtype)], 6 * n + 7,
                      [(a.shape, a.dtype) for a in list(shards) + [small]])


def _reduce_sibling(parts, dmod_rows):
    n = len(parts)

    def body(cps, ins, outs):
        x, y, c = _place()
        me, sibling = (x, y, c), (x, y, 1 - c)
        chips = [(1 - x, y), (x, 1 - y), (1 - x, 1 - y)]
        for i in range(n):
            cps.send(i, ins[i].at[:, 1 - c], outs[i], sibling)
        _gather_to_all(cps, [(ins[n], outs[n])], me, sibling, chips, c, n)
        for i in range(n):
            cps.arrived(i, outs[i], sibling)
        cps.drain()

    return _comm_call(body, "reduce_sibling", list(parts) + [dmod_rows],
                      [_sds((4,) + p.shape[2:], p.dtype) for p in parts] + [_sds((8,) + dmod_rows.shape, dmod_rows.dtype)], n + 7,
                      [(dmod_rows.shape, dmod_rows.dtype)])


def _reduce_chips(parts):
    n = len(parts)

    def body(cps, ins, outs):
        x, y, c = _place()
        mine = 2 * x + y
        chips = _other_chips(x, y)
        for i in range(n):
            cps.local(ins[i].at[mine], outs[i].at[mine])
        for j, (px, py) in enumerate(chips):
            for i in range(n):
                cps.send(3 * i + j, ins[i].at[2 * px + py], outs[i].at[mine], (px, py, c))
        cps.flush()
        for j, (px, py) in enumerate(chips):
            for i in range(n):
                cps.arrived(3 * i + j, outs[i].at[2 * px + py], (px, py, c))
        cps.drain()

    return _comm_call(body, "reduce_chips", list(parts), [_sds(p.shape, p.dtype) for p in parts], 3 * n,
                      [(p.shape[1:], p.dtype) for p in parts])


def _gather_reduced(shard_parts, repl_parts):
    ns, nr = len(shard_parts), len(repl_parts)

    def body(cps, ins, outs):
        x, y, c = _place()
        me, sibling = (x, y, c), (x, y, 1 - c)
        chips = [(1 - x, y), (x, 1 - y), (1 - x, 1 - y)]
        for i in range(ns):
            cps.local(ins[i], outs[i].at[c])
            cps.send(i, ins[i], outs[i].at[c], sibling)
        _gather_to_all(cps, [(ins[ns + i], outs[ns + i]) for i in range(nr)], me, sibling, chips, c, ns)
        for i in range(ns):
            cps.arrived(i, outs[i].at[1 - c], sibling)
        cps.drain()

    return _comm_call(body, "gather_reduced", list(shard_parts) + list(repl_parts),
                      [_sds((2,) + p.shape, p.dtype) for p in shard_parts] + [_sds((8,) + p.shape, p.dtype) for p in repl_parts],
                      ns + 7 * nr, [(p.shape, p.dtype) for p in list(shard_parts) + list(repl_parts)])


def _sum_sibling(core, parts, got, wire, name):
    n = len(parts)

    def body(core_ref, *refs):
        for i in range(n):
            refs[2 * n + i][0] = (refs[i][0] + refs[n + i][0]).astype(wire[i])

    keep_spec = lambda p: pl.BlockSpec((1, None) + p.shape[2:], lambda s, core_ref: (s, core_ref[0], 0, 0))
    slot_spec = lambda p: pl.BlockSpec((1,) + p.shape[2:], lambda s, core_ref: (s, 0, 0))
    return pl.pallas_call(
        body, name=name,
        grid_spec=pltpu.PrefetchScalarGridSpec(
            num_scalar_prefetch=1, grid=(4,), in_specs=[keep_spec(p) for p in parts] + [slot_spec(p) for p in parts],
            out_specs=[slot_spec(p) for p in parts]),
        out_shape=[_sds((4,) + p.shape[2:], wire[i]) for i, p in enumerate(parts)],
        compiler_params=_params("parallel"),
    )(core, *parts, *got)


def _sum_slots(slots, name):
    n = len(slots)

    def spec_pair(p):
        k, rows, cols = p.shape
        sub = 16 if p.dtype == BF16 else 8
        if (rows // 2) % sub == 0:
            return pl.BlockSpec((k, rows // 2, cols), lambda i: (0, i, 0)), pl.BlockSpec((rows // 2, cols), lambda i: (i, 0))
        return pl.BlockSpec((k, rows, cols), lambda i: (0, 0, 0)), pl.BlockSpec((rows, cols), lambda i: (0, 0))

    pairs = [spec_pair(p) for p in slots]

    def body(*refs):
        for i in range(n):
            acc = refs[i][0].astype(F32)
            for j in range(1, slots[i].shape[0]):
                acc = acc + refs[i][j].astype(F32)
            refs[n + i][...] = acc

    return pl.pallas_call(
        body, name=name, grid=(2,), in_specs=[a for a, _ in pairs], out_specs=[b for _, b in pairs],
        out_shape=[_sds(p.shape[1:]) for p in slots], compiler_params=_params("arbitrary"),
    )(*slots)


def _modulation(c_all, ada_w, ada_b):
    cols = ada_w.shape[2]

    def body(c_ref, w_ref, b_ref, o_ref):
        cc = c_ref[...]
        o_ref[0] = _mm(cc * _sigmoid(cc), w_ref[0]) + b_ref[0]

    return pl.pallas_call(
        body, name="modulation", grid=(2,),
        in_specs=[_full((8, D_MODEL)), pl.BlockSpec((1, D_MODEL, cols), lambda l: (l, 0, 0)), pl.BlockSpec((1, 1, cols), lambda l: (l, 0, 0))],
        out_specs=pl.BlockSpec((1, 8, cols), lambda l: (l, 0, 0)), out_shape=_sds((2, 8, cols)),
        compiler_params=_params("parallel"),
    )(c_all, ada_w, ada_b)


def _adamw_math(w, g, m, v):
    m = ADAM_B1 * m + (1.0 - ADAM_B1) * g
    v = ADAM_B2 * v + (1.0 - ADAM_B2) * (g * g)
    m_hat = m / (1.0 - ADAM_B1 ** ADAM_STEP)
    v_hat = v / (1.0 - ADAM_B2 ** ADAM_STEP)
    delta = -ADAM_LR * (m_hat / (jnp.sqrt(v_hat) + ADAM_EPS) + ADAM_WD * w)
    return delta, m, v


def _ada_update(c_all, dmod, w, m, v, rider=None):
    cols = w.shape[2]
    tr = 256
    per = D_MODEL // tr
    spec3 = pl.BlockSpec((1, tr, cols), lambda i: (i // per, i % per, 0))

    def body(c_ref, d_ref, w_ref, m_ref, v_ref, g_ref, dl_ref, nm_ref, nv_ref):
        cc = c_ref[...]
        g = _mm_tn(cc * _sigmoid(cc), d_ref[0])
        g_ref[0] = g
        dl_ref[0], nm_ref[0], nv_ref[0] = _adamw_math(w_ref[0], g, m_ref[0], v_ref[0])

    return _call(
        body, "ada_update", (2 * per,),
        [pl.BlockSpec((8, tr), lambda i: (0, i % per)), pl.BlockSpec((1, 8, cols), lambda i: (i // per, 0, 0)), spec3, spec3, spec3],
        [spec3] * 4, [_sds(w.shape)] * 4, (c_all, dmod, w, m, v), "parallel", rider=rider)


def _adamw_matrices(params):
    n = len(params)
    steps = 8

    def body(*refs):
        ins, outs = refs[:4 * n], refs[4 * n:]
        for j in range(n):
            w_ref, g_ref, m_ref, v_ref = ins[4 * j:4 * j + 4]
            g = g_ref[...]
            outs[4 * j][...] = g
            outs[4 * j + 1][...], outs[4 * j + 2][...], outs[4 * j + 3][...] = _adamw_math(w_ref[...], g, m_ref[...], v_ref[...])

    spec = lambda p: _rows(p[0].shape[0] // steps, p[0].shape[1])
    res = pl.pallas_call(
        body, name="adamw_matrices", grid=(steps,), in_specs=[spec(p) for p in params for _ in range(4)],
        out_specs=[spec(p) for p in params for _ in range(4)], out_shape=[_sds(p[0].shape) for p in params for _ in range(4)],
        compiler_params=_params("parallel"),
    )(*[a for p in params for a in p])
    return [tuple(res[4 * j:4 * j + 4]) for j in range(n)]


def _adamw_small(params):
    n = len(params)

    def body(*refs):
        ins, outs = refs[:4 * n], refs[4 * n:]
        for j in range(n):
            w_ref, g_ref, m_ref, v_ref = ins[4 * j:4 * j + 4]
            outs[3 * j][...], outs[3 * j + 1][...], outs[3 * j + 2][...] = _adamw_math(w_ref[...], g_ref[...], m_ref[...], v_ref[...])

    flat = [a for p in params for a in p]
    res = pl.pallas_call(body, name="adamw_small", out_shape=[_sds(p[0].shape) for p in params for _ in range(3)])(*flat)
    return [tuple(res[3 * j:3 * j + 3]) for j in range(n)]


def _cols(a, start, size):
    return lax.dynamic_slice_in_dim(a, start, size, axis=a.ndim - 1)


def kernel(x, c, positions, ada_w, ada_b, ln_g, ln_b, ev_w_in, ev_w_out, ev_sink, ev_sg_ln_g, ev_sg_ln_b, ev_sg_w, ev_sg_b, od_w_in, od_conv_w, od_conv_b, od_w_a, od_b_a, od_w_x, od_b_x, od_lam, od_w_out, loss_target, m_ada_w, m_ada_b, m_ln_g, m_ln_b, m_ev_w_in, m_ev_w_out, m_ev_sink, m_ev_sg_ln_g, m_ev_sg_ln_b, m_ev_sg_w, m_ev_sg_b, m_od_w_in, m_od_conv_w, m_od_conv_b, m_od_w_a, m_od_b_a, m_od_w_x, m_od_b_x, m_od_lam, m_od_w_out, v_ada_w, v_ada_b, v_ln_g, v_ln_b, v_ev_w_in, v_ev_w_out, v_ev_sink, v_ev_sg_ln_g, v_ev_sg_ln_b, v_ev_sg_w, v_ev_sg_b, v_od_w_in, v_od_conv_w, v_od_conv_b, v_od_w_a, v_od_b_a, v_od_w_x, v_od_b_x, v_od_lam, v_od_w_out):
    seq = x.shape[1]
    px, py, pc = _place()
    chip = 2 * px + py
    dev = 2 * chip + pc

    small = jnp.concatenate([od_conv_w[0].reshape(-1), od_conv_b[0], od_b_a[0].reshape(-1), jnp.zeros((256,), F32),
                             od_b_x[0].reshape(-1), od_lam[0].reshape(-1)]).reshape(3, D_MODEL)
    blk = jnp.concatenate([c, small, jnp.zeros((4, D_MODEL), F32)], axis=0)
    tr = lambda a: jnp.swapaxes(a, -1, -2)
    wire_w = lambda a: a.astype(MXU_DTYPE)
    ev_w_in4, g_small = _gather_weights([wire_w(tr(ev_w_in[0]))], blk)
    core = pc.astype(jnp.int32).reshape(1)
    ride = {"ev_w_out": wire_w(ev_w_out[0]), "od_w_in": wire_w(od_w_in[0]), "od_w_out": wire_w(od_w_out[0]),
            "core": core, "wire": MXU_DTYPE}
    c_all = g_small[:, 0, :]
    per_chip = g_small[0::2]
    conv_w = per_chip[:, 1].reshape(4, 4, 256).transpose(1, 0, 2).reshape(4, D_MODEL)
    conv_b = per_chip[:, 2, 0:256].reshape(D_MODEL)
    b_a = per_chip[:, 2, 256:768].reshape(4, 2, 256).transpose(1, 0, 2).reshape(2, D_MODEL)
    b_x = per_chip[:, 3, 0:512].reshape(4, 2, 256).transpose(1, 0, 2).reshape(2, D_MODEL)
    lam = per_chip[:, 3, 512:1024].reshape(4, 2, 256).transpose(1, 0, 2).reshape(2, D_MODEL)

    w_full = {
        "ev_w_in_t": ev_w_in4.reshape(EVEN_IN, D_MODEL),
        "ev_sink": ev_sink[0], "ev_sg_ln_g": ev_sg_ln_g[0], "ev_sg_ln_b": ev_sg_ln_b[0], "ev_sg_w": ev_sg_w[0],
        "ev_sg_b": ev_sg_b[0], "od_conv_w": conv_w, "od_conv_b": conv_b, "od_w_a": od_w_a[0], "od_b_a": b_a,
        "od_w_x": od_w_x[0], "od_b_x": b_x, "od_lam": lam, "ln_g": ln_g, "ln_b": ln_b,
    }

    ada_cols = ada_w.shape[2]
    mod_sh = _modulation(c_all, ada_w, _cols(ada_b, chip * ada_cols, ada_cols).reshape(2, 1, ada_cols))
    mod_all = _allgather8(mod_sh.reshape(16, ada_cols), "gather_mod").reshape(4, 2, 2, 8, ada_cols)[:, 0]
    mod_mine = lax.dynamic_index_in_dim(mod_all, dev, axis=2, keepdims=False)
    mod = mod_mine.transpose(1, 0, 2).reshape(2, 3, D_MODEL)

    posf = positions.astype(F32).reshape(seq, 1)
    grad_x, g = _local_step(x[0], posf, loss_target[0], mod, w_full, seq, ride)

    parts = [g["ev_w_in_t"].reshape(4, 2, 352, D_MODEL), g["ev_w_out"].reshape(4, 2, 128, D_MODEL),
             g["ev_sg_w"].reshape(4, 2, BLK, BLK), g["rows"].reshape(4, 2, SMALL_ROWS // 8, D_MODEL)]
    wire = [MXU_DTYPE] * 3 + [F32]
    *got, dmod_gathered = _reduce_sibling(parts, g["dmod_blk"])
    ev_slots = list(_reduce_chips(_sum_sibling(core, parts, got, wire, "sum_sibling")))
    od_slots = list(g["od_slots"])
    mine = _sum_slots(ev_slots[0:2] + od_slots[0:2] + ev_slots[2:3] + od_slots[2:4] + ev_slots[3:4], "sum_chips")
    reduced = _gather_reduced(mine[:4], mine[4:])
    g_ev_w_in_t = reduced[0].reshape(704, D_MODEL)
    g_ev_w_out = reduced[1].reshape(256, D_MODEL)
    g_od_w_in = reduced[2].reshape(D_MODEL, 512)
    g_od_w_out = reduced[3].reshape(256, D_MODEL)
    g_sg_w = reduced[4].reshape(8 * BLK, BLK)
    g_w_a = reduced[5].reshape(16 * BLK, BLK)
    g_w_x = reduced[6].reshape(16 * BLK, BLK)
    gs = reduced[7].reshape(SMALL_ROWS, D_MODEL)
    loss = gs[ROW_LOSS, 0]
    dmod_all = dmod_gathered[:, 0:6].reshape(8, 2, 3 * D_MODEL)
    dmod_sh = _cols(dmod_all, chip * ada_cols, ada_cols).transpose(1, 0, 2)
    (g_ada_w, d_ada_w, nm_ada_w, nv_ada_w), _ = _ada_update(c_all, dmod_sh, ada_w, m_ada_w, v_ada_w)

    mats = (("ev_w_out", ev_w_out, g_ev_w_out, m_ev_w_out, v_ev_w_out), ("od_w_in", od_w_in, g_od_w_in, m_od_w_in, v_od_w_in),
            ("od_w_out", od_w_out, g_od_w_out, m_od_w_out, v_od_w_out), ("ev_sg_w", ev_sg_w, g_sg_w, m_ev_sg_w, v_ev_sg_w),
            ("od_w_a", od_w_a, g_w_a, m_od_w_a, v_od_w_a), ("od_w_x", od_w_x, g_w_x, m_od_w_x, v_od_w_x))
    upd = _adamw_matrices([(tr(ev_w_in[0]), g_ev_w_in_t, tr(m_ev_w_in[0]), tr(v_ev_w_in[0]))]
                          + [(w_.reshape(g_.shape), g_, m_.reshape(g_.shape), v_.reshape(g_.shape)) for _, w_, g_, m_, v_ in mats])
    big = {"ev_w_in": tuple(tr(a).reshape(ev_w_in.shape) for a in upd[0])}
    for (name, w_, _, _, _), u in zip(mats, upd[1:]):
        big[name] = tuple(a.reshape(w_.shape) for a in u)
    big["ada_w"] = (g_ada_w, d_ada_w, nm_ada_w, nv_ada_w)

    sh = lambda a: _cols(a, chip * 256, 256)
    small_g = {
        "ada_b": gs[ROW_DMOD:ROW_DMOD + 6].reshape(2, 3 * D_MODEL),
        "ln_g": jnp.stack([gs[ROW_LN], gs[ROW_LN + 2]]), "ln_b": jnp.stack([gs[ROW_LN + 1], gs[ROW_LN + 3]]),
        "ev_sink": gs[ROW_SINK:ROW_SINK + 1, 0:N_Q_HEADS], "ev_sg_ln_g": gs[ROW_SG_LN:ROW_SG_LN + 1, 0:SG_WIDTH],
        "ev_sg_ln_b": gs[ROW_SG_LN:ROW_SG_LN + 1, SG_WIDTH:2 * SG_WIDTH], "ev_sg_b": gs[ROW_SG_B].reshape(N_SG_GROUPS, BLK),
        "od_conv_w": sh(gs[ROW_CONV_W:ROW_CONV_W + 4]), "od_conv_b": sh(gs[ROW_CONV_B:ROW_CONV_B + 1]),
        "od_b_a": sh(gs[ROW_B_A:ROW_B_A + 2]), "od_b_x": sh(gs[ROW_B_X:ROW_B_X + 2]), "od_lam": sh(gs[ROW_LAM:ROW_LAM + 2]),
    }
    small_in = {"ada_b": (ada_b, m_ada_b, v_ada_b), "ln_g": (ln_g, m_ln_g, v_ln_g), "ln_b": (ln_b, m_ln_b, v_ln_b),
                "ev_sink": (ev_sink, m_ev_sink, v_ev_sink), "ev_sg_ln_g": (ev_sg_ln_g, m_ev_sg_ln_g, v_ev_sg_ln_g),
                "ev_sg_ln_b": (ev_sg_ln_b, m_ev_sg_ln_b, v_ev_sg_ln_b), "ev_sg_b": (ev_sg_b, m_ev_sg_b, v_ev_sg_b),
                "od_conv_w": (od_conv_w, m_od_conv_w, v_od_conv_w), "od_conv_b": (od_conv_b, m_od_conv_b, v_od_conv_b),
                "od_b_a": (od_b_a, m_od_b_a, v_od_b_a), "od_b_x": (od_b_x, m_od_b_x, v_od_b_x),
                "od_lam": (od_lam, m_od_lam, v_od_lam)}
    names_small = list(small_g)
    upd = _adamw_small([(small_in[n][0].reshape(small_g[n].shape), small_g[n], small_in[n][1].reshape(small_g[n].shape),
                         small_in[n][2].reshape(small_g[n].shape)) for n in names_small])
    res = dict(big)
    for n, (d_, nm_, nv_) in zip(names_small, upd):
        shape = small_in[n][0].shape
        res[n] = tuple(a.reshape(shape) for a in (small_g[n], d_, nm_, nv_))

    order = ["ada_w", "ada_b", "ln_g", "ln_b", "ev_w_in", "ev_w_out", "ev_sink", "ev_sg_ln_g", "ev_sg_ln_b", "ev_sg_w", "ev_sg_b",
             "od_w_in", "od_conv_w", "od_conv_b", "od_w_a", "od_b_a", "od_w_x", "od_b_x", "od_lam", "od_w_out"]
    return (loss, grad_x.reshape(x.shape), *[res[n][0] for n in order], *[res[n][1] for n in order],
            *[res[n][2] for n in order], *[res[n][3] for n in order])
```

```python
from functools import partial

import jax
import jax.numpy as jnp
import numpy as np
from jax import lax
from jax.experimental import pallas as pl
from jax.experimental.pallas import tpu as pltpu

F32 = jnp.float32
BF16 = jnp.bfloat16
MXU_DTYPE = BF16
ACT_DTYPE = MXU_DTYPE

D_MODEL = 1024
HEAD_DIM = 64
N_Q_HEADS = 8
Q_PER_KV = 4
ATTN_WIDTH = 512
BLK = 128
ROPE_DIM = 16
ROPE_THETA = 500000.0
N_SG_GROUPS = 8
SG_WIDTH = 512
EVEN_IN = 2816
ODD_IN = 2048
RNN_HEADS = 8
RG_LRU_C = 8.0
ALPHA = (2 * 2) ** 0.25
LN_EPS = 1e-5
NEG_INF = -1e30
ADAM_LR, ADAM_B1, ADAM_B2, ADAM_EPS, ADAM_WD, ADAM_STEP = 0.001, 0.9, 0.999, 1e-08, 0.01, 10

LANES = 128
VMEM_LIMIT = 56 * 1024 * 1024
MESH = pl.DeviceIdType.MESH


def _mm(a, b):
    return jnp.dot(a.astype(MXU_DTYPE), b.astype(MXU_DTYPE), preferred_element_type=F32)


def _mm_nt(a, b):
    return lax.dot_general(a.astype(MXU_DTYPE), b.astype(MXU_DTYPE), (((1,), (1,)), ((), ())), preferred_element_type=F32)


def _mm_tn(a, b):
    return lax.dot_general(a.astype(MXU_DTYPE), b.astype(MXU_DTYPE), (((0,), (0,)), ((), ())), preferred_element_type=F32)


def _sigmoid(x):
    return 1.0 / (1.0 + jnp.exp(-x))


def _ln_stats(z):
    mu = jnp.mean(z, axis=-1, keepdims=True)
    d = z - mu
    var = jnp.mean(d * d, axis=-1, keepdims=True)
    rstd = lax.rsqrt(var + LN_EPS)
    return d * rstd, rstd


def _ln_bwd(dout, zhat, rstd, g):
    dzh = dout * g
    m1 = jnp.mean(dzh, axis=-1, keepdims=True)
    m2 = jnp.mean(dzh * zhat, axis=-1, keepdims=True)
    return rstd * (dzh - m1 - zhat * m2)


def _group_sum(x, e2):
    hi = x.astype(MXU_DTYPE)
    lo = (x - hi.astype(F32)).astype(MXU_DTYPE)
    return jnp.dot(hi, e2, preferred_element_type=F32) + jnp.dot(lo, e2, preferred_element_type=F32)


def _lane_iota(shape):
    return lax.broadcasted_iota(jnp.int32, shape, 1)


def _to_kv_lanes(t, h):
    src_lo = (h % 2 == 0)
    dst_lo = (h // Q_PER_KV == 0)
    if src_lo != dst_lo:
        t = pltpu.roll(t, HEAD_DIM, 1)
    lane = _lane_iota(t.shape)
    keep = (lane < HEAD_DIM) if dst_lo else (lane >= HEAD_DIM)
    return jnp.where(keep, t, 0.0)


def _from_kv_lanes(t, h):
    src_lo = (h // Q_PER_KV == 0)
    dst_lo = (h % 2 == 0)
    lane = _lane_iota(t.shape)
    keep = (lane < HEAD_DIM) if src_lo else (lane >= HEAD_DIM)
    t = jnp.where(keep, t, 0.0)
    if src_lo != dst_lo:
        t = pltpu.roll(t, HEAD_DIM, 1)
    return t


def _rope(t, cos_t, sin_p, sin_m):
    half = ROPE_DIM // 2
    return t * cos_t + pltpu.roll(t, half, 1) * sin_p + pltpu.roll(t, LANES - half, 1) * sin_m


def _rope_t(d, cos_t, sin_p, sin_m):
    half = ROPE_DIM // 2
    return d * cos_t + pltpu.roll(d * sin_p, LANES - half, 1) + pltpu.roll(d * sin_m, half, 1)


def _band(ref, n, nb):
    prev = jnp.maximum(n - 1, 0)
    nxt = jnp.minimum(n + 1, nb - 1)
    rows = [ref[pl.ds(pl.multiple_of(j * BLK, BLK), BLK), :] for j in (prev, n, nxt)]
    return jnp.concatenate(rows, axis=0)


def _band_bias(n, seq):
    qi = lax.broadcasted_iota(jnp.int32, (BLK, 3 * BLK), 0)
    kj = lax.broadcasted_iota(jnp.int32, (BLK, 3 * BLK), 1)
    k_abs = n * BLK - BLK + kj
    valid = (jnp.abs(kj - BLK - qi) <= BLK) & (k_abs >= 0) & (k_abs < seq)
    bias = jnp.where(valid, 0.0, NEG_INF)
    return jnp.concatenate([bias] * Q_PER_KV, axis=0)


def _stack_heads(tile_of, kv):
    return jnp.concatenate([_to_kv_lanes(tile_of(h // 2), h) for h in range(Q_PER_KV * kv, Q_PER_KV * (kv + 1))], axis=0)


def _per_head_column(vals):
    row = lax.broadcasted_iota(jnp.int32, (Q_PER_KV * BLK, 1), 0)
    return jnp.where(row < BLK, vals[0], jnp.where(row < 2 * BLK, vals[1], jnp.where(row < 3 * BLK, vals[2], vals[3])))


def _softplus_neg(lam):
    e = jnp.exp(-jnp.abs(lam))
    u = 1.0 + e
    log1p_e = jnp.where(u == 1.0, e, jnp.log(u) * (e / (u - 1.0)))
    sp = jnp.maximum(-lam, 0.0) + log1p_e
    dsp = -1.0 / (1.0 + jnp.exp(lam))
    return sp, dsp


def _full(shape):
    return pl.BlockSpec(shape, lambda *_: (0,) * len(shape))


def _const(shape):
    return pl.BlockSpec(shape, lambda *_: (0,) * len(shape), pipeline_mode=pl.Buffered(1))


def _rows(tm, n):
    return pl.BlockSpec((tm, n), lambda i: (i, 0))


def _params(*sem):
    return pltpu.CompilerParams(dimension_semantics=sem, vmem_limit_bytes=VMEM_LIMIT)


def _sds(shape, dtype=F32):
    return jax.ShapeDtypeStruct(shape, dtype)


def _place():
    return lax.axis_index("x"), lax.axis_index("y"), lax.axis_index("c")


class _Rider:
    def __init__(self, ins, out_shapes, n_remote, n_local, plan):
        self.ins, self.out_shapes, self.n_remote, self.n_local, self.plan = list(ins), list(out_shapes), n_remote, n_local, plan

    def scratch(self):
        return [pltpu.SemaphoreType.DMA((self.n_remote,)), pltpu.SemaphoreType.DMA((self.n_remote,)),
                pltpu.SemaphoreType.DMA((max(self.n_local, 1),))]

    def run(self, first, in_refs, out_refs, sems):
        send_sems, recv_sems, local_sems = sems
        sends, recvs, locals_ = self.plan(in_refs, out_refs)
        remote = lambda k, src, dst, to: pltpu.make_async_remote_copy(
            src_ref=src, dst_ref=dst, send_sem=send_sems.at[k], recv_sem=recv_sems.at[k], device_id=to, device_id_type=MESH)
        if first:
            for k, src, dst, to in sends:
                remote(k, src, dst, to).start()
            for j, (src, dst) in enumerate(locals_):
                pltpu.make_async_copy(src, dst, local_sems.at[j]).start()
        else:
            for k, dst, frm in recvs:
                remote(k, dst, dst, frm).wait_recv()
            for k, src, dst, to in sends:
                remote(k, src, dst, to).wait_send()
            for j, (src, dst) in enumerate(locals_):
                pltpu.make_async_copy(src, dst, local_sems.at[j]).wait()


def _other_chips(x, y):
    return [(1 - x, y), (x, 1 - y), (1 - x, 1 - y)]


def _gather_rider(shard):
    hr = shard.shape[0] // 2

    def plan(ins, outs):
        x, y, c = _place()
        mine, src, dst = 2 * x + y, ins[0], outs[0]
        sends, recvs = [], []
        for j, (px, py) in enumerate(_other_chips(x, y)):
            for flip in range(2):
                tc = c if flip == 0 else 1 - c
                sends.append((2 * j + flip, src.at[pl.ds(c * hr, hr)], dst.at[mine, pl.ds(c * hr, hr)], (px, py, tc)))
                recvs.append((2 * j + flip, dst.at[2 * px + py, pl.ds(tc * hr, hr)], (px, py, tc)))
        return sends, recvs, [(src, dst.at[mine])]

    return _Rider([shard], [_sds((4,) + shard.shape, shard.dtype)], 6, 1, plan)


def _sibling_swap_rider(parts):
    n = len(parts)

    def plan(ins, outs):
        x, y, c = _place()
        sibling = (x, y, 1 - c)
        return ([(i, ins[i].at[:, 1 - c], outs[i], sibling) for i in range(n)], [(i, outs[i], sibling) for i in range(n)], [])

    return _Rider(parts, [_sds((4,) + p.shape[2:], p.dtype) for p in parts], n, 0, plan)


def _chip_exchange_rider(parts):
    n = len(parts)

    def plan(ins, outs):
        x, y, c = _place()
        mine = 2 * x + y
        sends, recvs = [], []
        for i in range(n):
            for j, (px, py) in enumerate(_other_chips(x, y)):
                sends.append((3 * i + j, ins[i].at[2 * px + py], outs[i].at[mine], (px, py, c)))
                recvs.append((3 * i + j, outs[i].at[2 * px + py], (px, py, c)))
        return sends, recvs, [(ins[i].at[mine], outs[i].at[mine]) for i in range(n)]

    return _Rider(parts, [_sds(p.shape, p.dtype) for p in parts], 3 * n, n, plan)


def _call(body, name, grid, in_specs, out_specs, out_shape, args, sem, scratch=(), rider=None):
    if rider is None:
        return list(pl.pallas_call(body, name=name, grid=grid, in_specs=in_specs, out_specs=out_specs, out_shape=out_shape,
                                   scratch_shapes=list(scratch), compiler_params=_params(sem))(*args)), []
    n_in, n_out, n_scr = len(in_specs), len(out_specs), len(scratch)
    r_in, r_out = len(rider.ins), len(rider.out_shapes)
    steps = grid[0]

    def riding(*refs):
        ins, r_ins = refs[:n_in], refs[n_in:n_in + r_in]
        outs = refs[n_in + r_in:n_in + r_in + n_out]
        r_outs = refs[n_in + r_in + n_out:n_in + r_in + n_out + r_out]
        scr = refs[n_in + r_in + n_out + r_out:n_in + r_in + n_out + r_out + n_scr]
        sems = refs[n_in + r_in + n_out + r_out + n_scr:]

        @pl.when(pl.program_id(0) == 0)
        def _():
            rider.run(True, r_ins, r_outs, sems)

        body(*ins, *outs, *scr)

        @pl.when(pl.program_id(0) == steps - 1)
        def _():
            rider.run(False, r_ins, r_outs, sems)

    hbm = pl.BlockSpec(memory_space=pl.ANY)
    res = pl.pallas_call(
        riding, name=name, grid=grid, in_specs=list(in_specs) + [hbm] * r_in, out_specs=list(out_specs) + [hbm] * r_out,
        out_shape=list(out_shape) + rider.out_shapes, scratch_shapes=list(scratch) + rider.scratch(),
        compiler_params=_params("arbitrary"),
    )(*args, *rider.ins)
    return list(res[:n_out]), list(res[n_out:])


def _row_tile(seq, want):
    return want if seq % want == 0 else seq


def _rope_tables(posf, seq):
    half = ROPE_DIM // 2
    inv_freq = np.power(np.float32(ROPE_THETA), -np.arange(half, dtype=np.float32) / np.float32(half)).astype(np.float32)
    j = np.arange(LANES) % HEAD_DIM
    invf = jnp.asarray(np.where(j < ROPE_DIM, inv_freq[j % half], 0.0).astype(np.float32).reshape(1, LANES))
    m_p = jnp.asarray(((j >= half) & (j < ROPE_DIM)).astype(np.float32).reshape(1, LANES))
    m_m = jnp.asarray(-(j < half).astype(np.float32).reshape(1, LANES))
    tm = _row_tile(seq, 512)

    def body(pos_ref, invf_ref, mp_ref, mm_ref, cos_ref, sp_ref, sm_ref):
        def block(i, carry):
            rows = pl.ds(pl.multiple_of(i * tm, tm), tm)
            ang = pos_ref[rows, :] * invf_ref[...]
            s = jnp.sin(ang)
            cos_ref[rows, :] = jnp.cos(ang)
            sp_ref[rows, :] = s * mp_ref[...]
            sm_ref[rows, :] = s * mm_ref[...]
            return carry

        lax.fori_loop(0, seq // tm, block, 0)

    return body, (posf, invf, m_p, m_m), [_sds((seq, LANES))] * 3


def _even_proj(x, mod, w_in_t, tabs, seq, rider=None):
    tm = _row_tile(seq, 512)

    def body(x_ref, mod_ref, w_ref, cos_ref, sp_ref, sm_ref, q_ref, k_ref, v_ref, su_ref, sv_ref, g_ref):
        h = x_ref[...] * (1.0 + mod_ref[1:2, :]) + mod_ref[0:1, :]
        p = _mm_nt(h, w_ref[...])
        cos_t, sin_p, sin_m = cos_ref[...], sp_ref[...], sm_ref[...]
        for j in range(ATTN_WIDTH // LANES):
            q_ref[:, j * LANES:(j + 1) * LANES] = _rope(p[:, j * LANES:(j + 1) * LANES], cos_t, sin_p, sin_m).astype(q_ref.dtype)
        k_ref[...] = _rope(p[:, 512:640], cos_t, sin_p, sin_m).astype(k_ref.dtype)
        v_ref[...] = p[:, 640:768].astype(v_ref.dtype)
        su_ref[...] = p[:, 768:1280].astype(su_ref.dtype)
        sv_ref[...] = p[:, 1280:1792].astype(sv_ref.dtype)
        g_ref[...] = p[:, 1792:2816].astype(g_ref.dtype)

    return _call(
        body, "even_proj", (seq // tm,),
        [_rows(tm, D_MODEL), _full((3, D_MODEL)), _const((EVEN_IN, D_MODEL))] + [_rows(tm, LANES)] * 3,
        [_rows(tm, 512), _rows(tm, LANES), _rows(tm, LANES), _rows(tm, 512), _rows(tm, 512), _rows(tm, D_MODEL)],
        [_sds((seq, 512), MXU_DTYPE), _sds((seq, LANES), MXU_DTYPE), _sds((seq, LANES), MXU_DTYPE), _sds((seq, 512), ACT_DTYPE),
         _sds((seq, 512), ACT_DTYPE), _sds((seq, D_MODEL), ACT_DTYPE)],
        (x, mod, w_in_t, *tabs), "parallel", rider=rider)


def _sg_forward(sv, lng, lnb, sgw_ref, sgb, e2):
    vn, vhat, rstd, svo = [], [], [], []
    for j in range(SG_WIDTH // LANES):
        t = sv[:, j * LANES:(j + 1) * LANES]
        mu = _group_sum(t, e2) * (1.0 / HEAD_DIM)
        d = t - mu
        var = _group_sum(d * d, e2) * (1.0 / HEAD_DIM)
        r = lax.rsqrt(var + LN_EPS)
        vh = d * r
        vhat.append(vh)
        rstd.append(r)
        vn.append(vh * lng[:, j * LANES:(j + 1) * LANES] + lnb[:, j * LANES:(j + 1) * LANES])
    lane = _lane_iota((BLK, LANES))
    for j in range(SG_WIDTH // LANES):
        lo = _mm(sgw_ref[2 * j], vn[j])
        hi = _mm(sgw_ref[2 * j + 1], vn[j])
        svo.append(jnp.where(lane < HEAD_DIM, lo, hi) + sgb[:, j * LANES:(j + 1) * LANES])
    return svo, vn, vhat, rstd


def _even_mix(q, k, v, su, sv, sink, sgln_g, sgln_b, sgw, sgb_full, e2, seq, rider=None):
    nb = seq // BLK

    def body(sink_ref, q_ref, k_ref, v_ref, su_ref, sv_ref, lng_ref, lnb_ref, sgw_ref, sgb_ref, e2_ref, ycat_ref, lse_ref,
             svo_ref, vhat_ref, rstd_ref):
        n = pl.program_id(0)
        kband = _band(k_ref, n, nb)
        vband = _band(v_ref, n, nb)
        bias = _band_bias(n, seq)
        lane = _lane_iota((BLK, LANES))
        lse = jnp.zeros((BLK, LANES), F32)
        q_tile = lambda j: q_ref[:, j * LANES:(j + 1) * LANES].astype(F32)
        acc = [jnp.zeros((BLK, LANES), F32) for _ in range(ATTN_WIDTH // LANES)]
        for kv in range(N_Q_HEADS // Q_PER_KV):
            heads = range(Q_PER_KV * kv, Q_PER_KV * (kv + 1))
            sink = _per_head_column([sink_ref[h] for h in heads])
            s = _mm_nt(_stack_heads(q_tile, kv), kband) * (HEAD_DIM ** -0.5) + bias
            m = jnp.maximum(jnp.max(s, axis=1, keepdims=True), sink)
            p = jnp.exp(s - m)
            denom = jnp.sum(p, axis=1, keepdims=True) + jnp.exp(sink - m)
            o4 = _mm(p / denom, vband)
            l4 = m + jnp.log(denom)
            for g, h in enumerate(heads):
                acc[h // 2] = acc[h // 2] + _from_kv_lanes(o4[g * BLK:(g + 1) * BLK], h)
                lse = jnp.where(lane == h, l4[g * BLK:(g + 1) * BLK], lse)
        for j in range(ATTN_WIDTH // LANES):
            ycat_ref[:, j * LANES:(j + 1) * LANES] = acc[j].astype(ycat_ref.dtype)
        lse_ref[...] = lse
        svo, _, vhat, rstd = _sg_forward(sv_ref[...].astype(F32), lng_ref[...], lnb_ref[...], sgw_ref, sgb_ref[...], e2_ref[...])
        for j in range(SG_WIDTH // LANES):
            cs = slice(j * LANES, (j + 1) * LANES)
            ysg = su_ref[:, cs].astype(F32) * svo[j]
            ycat_ref[:, ATTN_WIDTH + j * LANES:ATTN_WIDTH + (j + 1) * LANES] = ysg.astype(ycat_ref.dtype)
            svo_ref[:, cs], vhat_ref[:, cs], rstd_ref[:, cs] = (t.astype(svo_ref.dtype) for t in (svo[j], vhat[j], rstd[j]))

    blk = lambda w: pl.BlockSpec((BLK, w), lambda n: (n, 0))
    return _call(
        body, "even_mix", (nb,),
        [pl.BlockSpec(memory_space=pltpu.SMEM), blk(512), _full((seq, LANES)), _full((seq, LANES)), blk(512), blk(512),
         _full((1, 512)), _full((1, 512)), _full((8, BLK, BLK)), _full((BLK, 512)), _full((LANES, LANES))],
        [blk(D_MODEL), blk(LANES)] + [blk(SG_WIDTH)] * 3,
        [_sds((seq, D_MODEL), ACT_DTYPE), _sds((seq, LANES))] + [_sds((seq, SG_WIDTH), ACT_DTYPE)] * 3,
        (sink, q, k, v, su, sv, sgln_g, sgln_b, sgw, sgb_full, e2), "parallel", rider=rider)


def _even_out(ycat, g, x, mod, mod_next, w_out, w_in4_next, ln_g, ln_b, seq, rider=None):
    tm = _row_tile(seq, 512)
    cs = ODD_IN // 4

    def body(y_ref, g_ref, x_ref, mod_ref, modn_ref, wo_ref, wi_ref, g1_ref, b1_ref, zhat_ref, rstd_ref, x1_ref, xr_ref, gn_ref):
        gg = g_ref[...].astype(F32)
        out = _mm(y_ref[...].astype(F32) * (gg * _sigmoid(gg)), wo_ref[...])
        z = ALPHA * x_ref[...] + mod_ref[2:3, :] * out
        zhat, rstd = _ln_stats(z)
        zhat_ref[...] = zhat
        rstd_ref[...] = rstd
        x1 = zhat * g1_ref[...] + b1_ref[...]
        x1_ref[...] = x1
        hb = (x1 * (1.0 + modn_ref[1:2, :]) + modn_ref[0:1, :]).astype(MXU_DTYPE)
        for s in range(2):
            xr_ref[:, s * cs:(s + 1) * cs] = jnp.dot(hb, wi_ref[s], preferred_element_type=F32)
            gn_ref[:, s * cs:(s + 1) * cs] = jnp.dot(hb, wi_ref[2 + s], preferred_element_type=F32).astype(gn_ref.dtype)

    return _call(
        body, "even_out", (seq // tm,),
        [_rows(tm, D_MODEL)] * 3 + [_full((3, D_MODEL)), _full((3, D_MODEL)), _const((D_MODEL, D_MODEL)), _const((4, D_MODEL, cs)),
                                    _full((1, D_MODEL)), _full((1, D_MODEL))],
        [_rows(tm, D_MODEL), _rows(tm, 1)] + [_rows(tm, D_MODEL)] * 3,
        [_sds((seq, D_MODEL)), _sds((seq, 1))] + [_sds((seq, D_MODEL))] * 2 + [_sds((seq, D_MODEL), ACT_DTYPE)],
        (ycat, g, x, mod, mod_next, w_out, w_in4_next, ln_g, ln_b), "parallel", rider=rider)


def _halo_specs(tm, seq, width, order=lambda i: i):
    per = tm // 8
    last = seq // 8 - 1
    return [pl.BlockSpec((8, width), lambda i: (jnp.maximum(order(i) * per - 1, 0), 0)),
            pl.BlockSpec((tm, width), lambda i: (order(i), 0)),
            pl.BlockSpec((8, width), lambda i: (jnp.minimum((order(i) + 1) * per, last), 0))]


def _extended(prev_ref, main_ref, next_ref, i, n_steps):
    prev = jnp.where(i > 0, prev_ref[...], 0.0)
    nxt = jnp.where(i < n_steps - 1, next_ref[...], 0.0)
    return jnp.concatenate([prev, main_ref[...], nxt], axis=0)


def _shifted(ext, off, tm):
    if off == 0:
        return ext[8:8 + tm]
    return pltpu.roll(ext, (-off) % ext.shape[0], 0)[8:8 + tm]


SCAN_SUB = 8


def _lru_gate(xh, pre, bias, sp, hs, d):
    r = _sigmoid(pre[:, 0:LANES] + bias[2 * d:2 * d + 1, hs])
    ig = _sigmoid(pre[:, LANES:2 * LANES] + bias[2 * d + 1:2 * d + 2, hs])
    neg_log_a = RG_LRU_C * r * sp[d:d + 1, hs]
    a = jnp.exp(-neg_log_a)
    u = jnp.tanh(neg_log_a) * (a * a + 1.0)
    inv_s = lax.rsqrt(jnp.maximum(u, jnp.finfo(F32).tiny))
    return r, ig, a, u * inv_s, inv_s


def _conv_block(xp_ref, xm_ref, xn_ref, cw_ref, cb_ref, blk, steps, tm):
    ext = _extended(xp_ref, xm_ref, xn_ref, blk, steps)
    return cb_ref[...] + sum(cw_ref[kk:kk + 1, :] * _shifted(ext, kk - 2, tm) for kk in range(4))


def _scan_tiles(a_ref, b_ref, h_ref, hprev_ref, carry_h, carry_a, rows, descending, post):
    sub = SCAN_SUB
    tiles = rows // sub
    row = lax.broadcasted_iota(jnp.int32, (sub, D_MODEL), 0)

    def shift(v, d, fill):
        if descending:
            return jnp.where(row <= sub - 1 - d, pltpu.roll(v, sub - d, 0), fill)
        return jnp.where(row >= d, pltpu.roll(v, d, 0), fill)

    def last(v):
        return jnp.broadcast_to(v[0:1, :] if descending else v[sub - 1:sub, :], v.shape)

    def tile(j, c):
        ch, ca = c
        r0 = pl.multiple_of(((tiles - 1 - j) if descending else j) * sub, sub)
        at = a_ref[pl.ds(r0, sub), :]
        bt = b_ref[pl.ds(r0, sub), :]
        coef = shift(at, 1, ca) if post else at
        acc_a, acc_b = coef, bt
        for d in (1, 2, 4):
            acc_b = acc_b + acc_a * shift(acc_b, d, 0.0)
            acc_a = acc_a * shift(acc_a, d, 1.0)
        h = acc_b + acc_a * ch
        h_ref[pl.ds(r0, sub), :] = h
        if post:
            return last(h), last(at)
        hprev_ref[pl.ds(r0, sub), :] = shift(h, 1, ch)
        return last(h), ca

    ch, ca = lax.fori_loop(0, tiles, tile, (carry_h[...], carry_a[...]), unroll=4)
    carry_h[...] = ch
    carry_a[...] = ca


def _lru_fwd(xr, xc, conv_w, conv_b, wcat, bias, lam, seq, d):
    tb = _row_tile(seq, 512)
    steps = seq // tb
    descending = d == 1
    order = (lambda i: steps - 1 - i) if descending else (lambda i: i)
    with_conv = xc is None
    n_x = 5 if with_conv else 1

    def body(*refs):
        x_refs, (w_ref, bias_ref, lam_ref) = refs[:n_x], refs[n_x:n_x + 3]
        h_ref, hp_ref, a_ref, r_ref, i_ref, s_ref, q_ref = refs[n_x + 3:n_x + 10]
        b_scr, carry_h, carry_a = refs[-3:]
        i = pl.program_id(0)

        @pl.when(i == 0)
        def _():
            carry_h[...] = jnp.zeros_like(carry_h)
            carry_a[...] = jnp.zeros_like(carry_a)

        if with_conv:
            xc_ref = refs[n_x + 10]
            xc_ref[...] = _conv_block(*x_refs, order(i), steps, tb)
        else:
            xc_ref = x_refs[0]
        sp, _ = _softplus_neg(lam_ref[...])
        bias = bias_ref[...]
        for h in range(RNN_HEADS):
            hs = slice(h * LANES, (h + 1) * LANES)
            xh = xc_ref[:, hs]
            r, ig, a, s, q = _lru_gate(xh, _mm(xh, w_ref[h, :, 2 * d * LANES:2 * (d + 1) * LANES]), bias, sp, hs, d)
            a_ref[:, hs] = a
            b_scr[:, hs] = s * ig * xh
            for ref, val in ((r_ref, r), (i_ref, ig), (s_ref, s), (q_ref, q)):
                ref[:, hs] = val.astype(ref.dtype)
        _scan_tiles(a_ref, b_scr, h_ref, hp_ref, carry_h, carry_a, tb, descending, post=False)

    row_spec = pl.BlockSpec((tb, D_MODEL), lambda i: (order(i), 0))
    if with_conv:
        x_specs, x_args = _halo_specs(tb, seq, D_MODEL, order) + [_full((4, D_MODEL)), _full((1, D_MODEL))], (xr, xr, xr, conv_w, conv_b)
    else:
        x_specs, x_args = [row_spec], (xc,)
    n_out = 8 if with_conv else 7
    return pl.pallas_call(
        body, name="lru_fwd_%d" % d, grid=(steps,),
        in_specs=x_specs + [_full((8, LANES, 512)), _full((4, D_MODEL)), _full((2, D_MODEL))],
        out_specs=[row_spec] * n_out,
        out_shape=[_sds((seq, D_MODEL))] * 3 + [_sds((seq, D_MODEL), ACT_DTYPE)] * 4 + [_sds((seq, D_MODEL))] * (n_out - 7),
        scratch_shapes=[pltpu.VMEM((tb, D_MODEL), F32)] + [pltpu.VMEM((SCAN_SUB, D_MODEL), F32)] * 2,
        compiler_params=_params("arbitrary"),
    )(*x_args, wcat, bias, lam)


def _odd_out_and_loss(hf, hr, g, x1, tgt, mod, w_out, ln_g, ln_b, seq):
    tm = _row_tile(seq, 512)

    def body(hf_ref, hr_ref, g_ref, x_ref, t_ref, mod_ref, w_ref, lg_ref, lb_ref,
             dhs_ref, dg_ref, dres_ref, loss_ref, dw_ref, vec_ref):
        @pl.when(pl.program_id(0) == 0)
        def _():
            loss_ref[...] = jnp.zeros_like(loss_ref)
            dw_ref[...] = jnp.zeros_like(dw_ref)
            vec_ref[...] = jnp.zeros_like(vec_ref)

        gg = g_ref[...].astype(F32)
        sg = _sigmoid(gg)
        silu = gg * sg
        hsum = hf_ref[...] + hr_ref[...]
        y = hsum * silu
        out = _mm(y, w_ref[...])
        gate = mod_ref[2:3, :]
        z = ALPHA * x_ref[...] + gate * out
        zhat, rstd = _ln_stats(z)
        x2 = zhat * lg_ref[...] + lb_ref[...]
        err = x2 - t_ref[...]
        loss_ref[...] += 0.5 * jnp.sum(jnp.mean(err * err, axis=-1, keepdims=True))
        dx2 = err * (1.0 / D_MODEL)
        dz = _ln_bwd(dx2, zhat, rstd, lg_ref[...])
        vec_ref[0:1, :] += jnp.sum(dx2 * zhat, axis=0, keepdims=True)
        vec_ref[1:2, :] += jnp.sum(dx2, axis=0, keepdims=True)
        vec_ref[2:3, :] += jnp.sum(dz * out, axis=0, keepdims=True)
        dres_ref[...] = ALPHA * dz
        dout = gate * dz
        dw_ref[...] += _mm_tn(y, dout)
        dy = _mm_nt(dout, w_ref[...])
        dhs_ref[...] = dy * silu
        dg_ref[...] = (dy * hsum * (sg * (1.0 + gg * (1.0 - sg)))).astype(dg_ref.dtype)

    return pl.pallas_call(
        body, name="odd_out_loss", grid=(seq // tm,),
        in_specs=[_rows(tm, D_MODEL)] * 5 + [_full((3, D_MODEL)), _const((D_MODEL, D_MODEL)),
                                             _full((1, D_MODEL)), _full((1, D_MODEL))],
        out_specs=[_rows(tm, D_MODEL)] * 3 + [_full((8, LANES)), _full((D_MODEL, D_MODEL)), _full((8, D_MODEL))],
        out_shape=[_sds((seq, D_MODEL)), _sds((seq, D_MODEL), ACT_DTYPE), _sds((seq, D_MODEL)), _sds((8, LANES)),
                   _sds((D_MODEL, D_MODEL)), _sds((8, D_MODEL))],
        compiler_params=_params("arbitrary"),
    )(hf, hr, g, x1, tgt, mod, w_out, ln_g, ln_b)


def _lru_bwd(xc, dhs, hprev, a_d, r_d, i_d, s_d, q_d, wcat, lam, seq, d):
    tb = _row_tile(seq, 512)
    steps = seq // tb
    descending = d == 0
    order = (lambda i: steps - 1 - i) if descending else (lambda i: i)
    cols = slice(2 * d * LANES, 2 * (d + 1) * LANES)

    def body(xc_ref, dhs_ref, hp_ref, a_ref, r_ref, i_ref, s_ref, q_ref, w_ref, lam_ref, dxc_ref, dw_ref, vec_ref,
             g_scr, carry_h, carry_a):
        i = pl.program_id(0)

        @pl.when(i == 0)
        def _():
            dw_ref[...] = jnp.zeros_like(dw_ref)
            vec_ref[...] = jnp.zeros_like(vec_ref)
            carry_h[...] = jnp.zeros_like(carry_h)
            carry_a[...] = jnp.zeros_like(carry_a)

        sp, dsp = _softplus_neg(lam_ref[...])
        _scan_tiles(a_ref, dhs_ref, g_scr, None, carry_h, carry_a, tb, descending, post=True)
        for h in range(RNN_HEADS):
            hs = slice(h * LANES, (h + 1) * LANES)
            xh, a = xc_ref[:, hs], a_ref[:, hs]
            r, ig, s = r_ref[:, hs].astype(F32), i_ref[:, hs].astype(F32), s_ref[:, hs].astype(F32)
            db = g_scr[:, hs]
            da = db * hp_ref[:, hs]
            dlog_a = da * a - (db * ig * xh) * (a * a * q_ref[:, hs].astype(F32))
            dpr = dlog_a * (-RG_LRU_C) * sp[d:d + 1, hs] * r * (1.0 - r)
            dpi = db * s * xh * ig * (1.0 - ig)
            vec_ref[0:1, hs] += jnp.sum(dpr, axis=0, keepdims=True)
            vec_ref[1:2, hs] += jnp.sum(dpi, axis=0, keepdims=True)
            vec_ref[2:3, hs] += jnp.sum(dlog_a * r, axis=0, keepdims=True) * (-RG_LRU_C) * dsp[d:d + 1, hs]
            dcat = jnp.concatenate([dpr, dpi], axis=1)
            dw_ref[h] += _mm_tn(xh, dcat)
            dxc_ref[:, hs] = db * s * ig + _mm_nt(dcat, w_ref[h, :, cols])

    row_spec = pl.BlockSpec((tb, D_MODEL), lambda i: (order(i), 0))
    return pl.pallas_call(
        body, name="lru_bwd_%d" % d, grid=(steps,),
        in_specs=[row_spec] * 8 + [_full((8, LANES, 512)), _full((2, D_MODEL))],
        out_specs=[row_spec, _full((8, LANES, 2 * LANES)), _full((8, D_MODEL))],
        out_shape=[_sds((seq, D_MODEL)), _sds((8, LANES, 2 * LANES)), _sds((8, D_MODEL))],
        scratch_shapes=[pltpu.VMEM((tb, D_MODEL), F32)] + [pltpu.VMEM((SCAN_SUB, D_MODEL), F32)] * 2,
        compiler_params=_params("arbitrary"),
    )(xc, dhs, hprev, a_d, r_d, i_d, s_d, q_d, wcat, lam)


def _odd_proj_bwd(dxc_f, dxc_r, xr, dg, x1, dres, mod, conv_w, w_in4, seq):
    tm = _row_tile(seq, 512)
    steps = seq // tm

    def body(fp_ref, fm_ref, fn_ref, rp_ref, rm_ref, rn_ref, xp_ref, xm_ref, xn_ref, dg_ref, x_ref, dres_ref, mod_ref, cw_ref,
             w_ref, dx_ref, dw_ref, vec_ref, dpb_ref):
        i = pl.program_id(0)

        @pl.when(i == 0)
        def _():
            vec_ref[...] = jnp.zeros_like(vec_ref)
            dw_ref[...] = jnp.zeros_like(dw_ref)

        dxc_m = fm_ref[...] + rm_ref[...]
        dext = jnp.concatenate([jnp.where(i > 0, fp_ref[...] + rp_ref[...], 0.0), dxc_m,
                                jnp.where(i < steps - 1, fn_ref[...] + rn_ref[...], 0.0)], axis=0)
        xext = _extended(xp_ref, xm_ref, xn_ref, i, steps)
        dxr = sum(cw_ref[kk:kk + 1, :] * _shifted(dext, 2 - kk, tm) for kk in range(4))
        for kk in range(4):
            vec_ref[kk:kk + 1, :] += jnp.sum(dxc_m * _shifted(xext, kk - 2, tm), axis=0, keepdims=True)
        vec_ref[4:5, :] += jnp.sum(dxc_m, axis=0, keepdims=True)
        dpb_ref[:, :D_MODEL] = dxr.astype(dpb_ref.dtype)
        dpb_ref[:, D_MODEL:] = dg_ref[...].astype(dpb_ref.dtype)
        cs = ODD_IN // 4
        dh = sum(_mm_nt(dpb_ref[:, s * cs:(s + 1) * cs], w_ref[s]) for s in range(4))
        x = x_ref[...]
        h_t = (x * (1.0 + mod_ref[1:2, :]) + mod_ref[0:1, :]).T.astype(MXU_DTYPE)
        for s in range(4):
            dw_ref[s] += jnp.dot(h_t, dpb_ref[:, s * cs:(s + 1) * cs], preferred_element_type=F32)
        vec_ref[5:6, :] += jnp.sum(dh, axis=0, keepdims=True)
        vec_ref[6:7, :] += jnp.sum(dh * x, axis=0, keepdims=True)
        dx_ref[...] = dres_ref[...] + dh * (1.0 + mod_ref[1:2, :])

    return pl.pallas_call(
        body, name="odd_proj_bwd", grid=(steps,),
        in_specs=_halo_specs(tm, seq, D_MODEL) * 3 + [_rows(tm, D_MODEL)] * 3
        + [_full((3, D_MODEL)), _full((4, D_MODEL)), _const((4, D_MODEL, ODD_IN // 4))],
        out_specs=[_rows(tm, D_MODEL), _const((4, D_MODEL, ODD_IN // 4)), _full((8, D_MODEL))],
        out_shape=[_sds((seq, D_MODEL)), _sds((4, D_MODEL, ODD_IN // 4)), _sds((8, D_MODEL))],
        scratch_shapes=[pltpu.VMEM((tm, ODD_IN), MXU_DTYPE)],
        compiler_params=_params("arbitrary"),
    )(dxc_f, dxc_f, dxc_f, dxc_r, dxc_r, dxc_r, xr, xr, xr, dg, x1, dres, mod, conv_w, w_in4)


def _even_out_bwd(dx1, zhat, rstd, ycat, g, mod, ln_g, w_out, seq, rider=None):
    tm = _row_tile(seq, 512)
    steps = seq // tm

    def body(dx_ref, zh_ref, rs_ref, y_ref, g_ref, mod_ref, lg_ref, w_ref, dy_ref, dg_ref, dres_ref, dw_ref, vec_ref):
        i = pl.program_id(0)

        @pl.when(i == 0)
        def _():
            dw_ref[...] = jnp.zeros_like(dw_ref)
            vec_ref[...] = jnp.zeros_like(vec_ref)

        zhat = zh_ref[...]
        dx1_ = dx_ref[...]
        dz = _ln_bwd(dx1_, zhat, rs_ref[...], lg_ref[...])
        vec_ref[0:1, :] += jnp.sum(dx1_ * zhat, axis=0, keepdims=True)
        vec_ref[1:2, :] += jnp.sum(dx1_, axis=0, keepdims=True)
        dres_ref[...] = ALPHA * dz
        gate = mod_ref[2:3, :]
        gg = g_ref[...].astype(F32)
        sg = _sigmoid(gg)
        silu = gg * sg
        ycat_ = y_ref[...].astype(F32)
        dw_ref[...] += _mm_tn(ycat_ * silu, dz)
        dy = _mm_nt(gate * dz, w_ref[...])
        dy_ref[...] = (dy * silu).astype(dy_ref.dtype)
        dg_ref[...] = (dy * ycat_ * (sg * (1.0 + gg * (1.0 - sg)))).astype(dg_ref.dtype)

        @pl.when(i == steps - 1)
        def _():
            m_acc = dw_ref[...]
            vec_ref[2:3, :] = jnp.sum(w_ref[...].astype(F32) * m_acc, axis=0, keepdims=True)
            dw_ref[...] = m_acc * gate

    return _call(
        body, "even_out_bwd", (steps,),
        [_rows(tm, D_MODEL), _rows(tm, D_MODEL), _rows(tm, 1), _rows(tm, D_MODEL), _rows(tm, D_MODEL), _full((3, D_MODEL)),
         _full((1, D_MODEL)), _const((D_MODEL, D_MODEL))],
        [_rows(tm, D_MODEL)] * 3 + [_full((D_MODEL, D_MODEL)), _full((8, D_MODEL))],
        [_sds((seq, D_MODEL), ACT_DTYPE), _sds((seq, D_MODEL), ACT_DTYPE), _sds((seq, D_MODEL)), _sds((D_MODEL, D_MODEL)),
         _sds((8, D_MODEL))],
        (dx1, zhat, rstd, ycat, g, mod, ln_g, w_out), "arbitrary", rider=rider)


def _even_mix_bwd(q, k, v, lse, ycat, dycat, su, svo_s, vhat_s, rstd_s, sink, sgln_g, sgln_b, sgw, e2, e8, seq, rider=None):
    nb = seq // BLK

    def body(sink_ref, q_ref, k_ref, v_ref, lse_ref, y_ref, dy_ref, su_ref, svo_ref, vhat_ref, rstd_ref, lng_ref, lnb_ref, sgw_ref,
             e2_ref, e8_ref, dq_ref, dsu_ref, dsv_ref, dk_ref, dv_ref, dsgw_ref, dsgb_ref, vec_ref, dsink_ref, dsgb_acc):
        n = pl.program_id(0)

        @pl.when(n == 0)
        def _():
            dk_ref[...] = jnp.zeros_like(dk_ref)
            dv_ref[...] = jnp.zeros_like(dv_ref)
            dsgw_ref[...] = jnp.zeros_like(dsgw_ref)
            dsgb_acc[...] = jnp.zeros_like(dsgb_acc)
            vec_ref[...] = jnp.zeros_like(vec_ref)
            dsink_ref[...] = jnp.zeros_like(dsink_ref)

        kband = _band(k_ref, n, nb)
        vband = _band(v_ref, n, nb)
        bias = _band_bias(n, seq)
        lane = _lane_iota((BLK, LANES))
        row8 = lax.broadcasted_iota(jnp.int32, (8, LANES), 0)
        lse = lse_ref[...]
        dkb = jnp.zeros((LANES, 3 * BLK), F32)
        dvb = jnp.zeros((LANES, 3 * BLK), F32)
        dsink = jnp.zeros((8, LANES), F32)
        q_tile = lambda j: q_ref[:, j * LANES:(j + 1) * LANES].astype(F32)
        do_tile = lambda j: dy_ref[:, j * LANES:(j + 1) * LANES].astype(F32)
        dq = [jnp.zeros((BLK, LANES), F32) for _ in range(ATTN_WIDTH // LANES)]
        for kv in range(N_Q_HEADS // Q_PER_KV):
            heads = range(Q_PER_KV * kv, Q_PER_KV * (kv + 1))
            lse4, delta4 = [], []
            for h in heads:
                head_lanes = (lane < HEAD_DIM) if h % 2 == 0 else (lane >= HEAD_DIM)
                lse4.append(jnp.sum(jnp.where(lane == h, lse, 0.0), axis=1, keepdims=True))
                o_tile = y_ref[:, (h // 2) * LANES:(h // 2 + 1) * LANES].astype(F32)
                delta4.append(jnp.sum(jnp.where(head_lanes, do_tile(h // 2) * o_tile, 0.0), axis=1, keepdims=True))
            lse4, delta4 = jnp.concatenate(lse4, axis=0), jnp.concatenate(delta4, axis=0)
            q4, do4 = _stack_heads(q_tile, kv), _stack_heads(do_tile, kv)
            s = _mm_nt(q4, kband) * (HEAD_DIM ** -0.5) + bias
            p = jnp.exp(s - lse4)
            wsink = jnp.exp(_per_head_column([sink_ref[h] for h in heads]) - lse4) * delta4
            ds = p * (_mm_nt(do4, vband) - delta4) * (HEAD_DIM ** -0.5)
            dq4 = _mm(ds, kband)
            dkb = dkb + _mm_tn(q4, ds)
            dvb = dvb + _mm_tn(do4, p)
            for g, h in enumerate(heads):
                dq[h // 2] = dq[h // 2] + _from_kv_lanes(dq4[g * BLK:(g + 1) * BLK], h)
                dsink = dsink + jnp.where(row8 == h, -jnp.sum(wsink[g * BLK:(g + 1) * BLK]), 0.0)
        for j in range(ATTN_WIDTH // LANES):
            dq_ref[:, j * LANES:(j + 1) * LANES] = dq[j].astype(dq_ref.dtype)
        dsink_ref[...] += dsink
        prev = jnp.maximum(n - 1, 0)
        nxt = jnp.minimum(n + 1, nb - 1)
        for part, blk_i in enumerate((prev, n, nxt)):
            rows = pl.ds(pl.multiple_of(blk_i * BLK, BLK), BLK)
            dk_ref[rows, :] += dkb[:, part * BLK:(part + 1) * BLK].T
            dv_ref[rows, :] += dvb[:, part * BLK:(part + 1) * BLK].T

        e2 = e2_ref[...]
        lng, lnb = lng_ref[...], lnb_ref[...]
        for j in range(SG_WIDTH // LANES):
            cs = slice(j * LANES, (j + 1) * LANES)
            vhat = vhat_ref[:, cs].astype(F32)
            vn = vhat * lng[:, cs] + lnb[:, cs]
            dysg = dy_ref[:, ATTN_WIDTH + j * LANES:ATTN_WIDTH + (j + 1) * LANES].astype(F32)
            dsu_ref[:, cs] = (dysg * svo_ref[:, cs].astype(F32)).astype(dsu_ref.dtype)
            dsvo = dysg * su_ref[:, cs].astype(F32)
            dsgb_acc[:, cs] += dsvo
            d_lo = jnp.where(lane < HEAD_DIM, dsvo, 0.0)
            d_hi = dsvo - d_lo
            dsgw_ref[2 * j] += _mm_nt(d_lo, vn)
            dsgw_ref[2 * j + 1] += _mm_nt(d_hi, vn)
            dvn = _mm_tn(sgw_ref[2 * j], d_lo) + _mm_tn(sgw_ref[2 * j + 1], d_hi)
            vec_ref[0:1, cs] += jnp.sum(dvn * vhat, axis=0, keepdims=True)
            vec_ref[1:2, cs] += jnp.sum(dvn, axis=0, keepdims=True)
            dvh = dvn * lng[:, cs]
            m1 = _group_sum(dvh, e2) * (1.0 / HEAD_DIM)
            m2 = _group_sum(dvh * vhat, e2) * (1.0 / HEAD_DIM)
            dsv_ref[:, cs] = (rstd_ref[:, cs].astype(F32) * (dvh - m1 - vhat * m2)).astype(dsv_ref.dtype)

        @pl.when(n == nb - 1)
        def _():
            rest = dsgb_acc[...]
            total = jnp.zeros((8, BLK), F32)
            for _ in range(3):
                part = rest.astype(MXU_DTYPE)
                total = total + lax.dot_general(e8_ref[...], part, (((1,), (1,)), ((), ())), preferred_element_type=F32)
                rest = rest - part.astype(F32)
            dsgb_ref[...] = total

    blk = lambda w: pl.BlockSpec((BLK, w), lambda n: (n, 0))
    return _call(
        body, "even_mix_bwd", (nb,),
        [pl.BlockSpec(memory_space=pltpu.SMEM), blk(512), _full((seq, LANES)), _full((seq, LANES)), blk(LANES),
         blk(D_MODEL), blk(D_MODEL), blk(512), blk(512), blk(512), blk(512), _full((1, 512)), _full((1, 512)), _full((8, BLK, BLK)),
         _full((LANES, LANES)), _full((8, 512))],
        [blk(512), blk(512), blk(512), _full((seq, LANES)), _full((seq, LANES)), _full((8, BLK, BLK)),
         _full((8, BLK)), _full((8, 512)), _full((8, LANES))],
        [_sds((seq, 512), ACT_DTYPE), _sds((seq, 512), ACT_DTYPE), _sds((seq, 512), ACT_DTYPE), _sds((seq, LANES)), _sds((seq, LANES)),
         _sds((8, BLK, BLK)), _sds((8, BLK)), _sds((8, 512)), _sds((8, LANES))],
        (sink, q, k, v, lse, ycat, dycat, su, svo_s, vhat_s, rstd_s, sgln_g, sgln_b, sgw, e2, e8), "arbitrary",
        scratch=[pltpu.VMEM((BLK, 512), F32)], rider=rider)


def _even_proj_bwd(dq, dk, dv, dsu, dsv, dg, x, dres, mod, tabs, w_in_t, seq):
    tm = _row_tile(seq, 512)

    def body(dq_ref, dk_ref, dv_ref, dsu_ref, dsv_ref, dg_ref, x_ref, dres_ref, mod_ref, cos_ref, sp_ref, sm_ref, wt_ref,
             dx_ref, dw_ref, vec_ref, dpb_ref):
        @pl.when(pl.program_id(0) == 0)
        def _():
            vec_ref[...] = jnp.zeros_like(vec_ref)
            dw_ref[...] = jnp.zeros_like(dw_ref)

        cos_t, sin_p, sin_m = cos_ref[...], sp_ref[...], sm_ref[...]
        dt = dpb_ref.dtype
        for j in range(ATTN_WIDTH // LANES):
            cs = slice(j * LANES, (j + 1) * LANES)
            dpb_ref[:, cs] = _rope_t(dq_ref[:, cs].astype(F32), cos_t, sin_p, sin_m).astype(dt)
        dpb_ref[:, 512:640] = _rope_t(dk_ref[...], cos_t, sin_p, sin_m).astype(dt)
        dpb_ref[:, 640:768] = dv_ref[...].astype(dt)
        dpb_ref[:, 768:1280] = dsu_ref[...].astype(dt)
        dpb_ref[:, 1280:1792] = dsv_ref[...].astype(dt)
        dpb_ref[:, 1792:2816] = dg_ref[...].astype(dt)
        dpb = dpb_ref[...]
        dh = jnp.dot(dpb, wt_ref[...], preferred_element_type=F32)
        x_ = x_ref[...]
        hb = (x_ * (1.0 + mod_ref[1:2, :]) + mod_ref[0:1, :]).astype(MXU_DTYPE)
        dw_ref[...] += _mm_tn(dpb, hb)
        vec_ref[0:1, :] += jnp.sum(dh, axis=0, keepdims=True)
        vec_ref[1:2, :] += jnp.sum(dh * x_, axis=0, keepdims=True)
        dx_ref[...] = dres_ref[...] + dh * (1.0 + mod_ref[1:2, :])

    return pl.pallas_call(
        body, name="even_proj_bwd", grid=(seq // tm,),
        in_specs=[_rows(tm, 512), _rows(tm, LANES), _rows(tm, LANES), _rows(tm, 512), _rows(tm, 512), _rows(tm, D_MODEL),
                  _rows(tm, D_MODEL), _rows(tm, D_MODEL), _full((3, D_MODEL))] + [_rows(tm, LANES)] * 3
        + [_const((EVEN_IN, D_MODEL))],
        out_specs=[_rows(tm, D_MODEL), _const((EVEN_IN, D_MODEL)), _full((8, D_MODEL))],
        out_shape=[_sds((seq, D_MODEL)), _sds((EVEN_IN, D_MODEL)), _sds((8, D_MODEL))],
        scratch_shapes=[pltpu.VMEM((tm, EVEN_IN), MXU_DTYPE)],
        compiler_params=_params("arbitrary"),
    )(dq, dk, dv, dsu, dsv, dg, x, dres, mod, *tabs, w_in_t)


def _local_step(x, tabs, tgt, mod, w, seq, ride=None):
    rid = lambda make, *a: None if ride is None else make(*a)
    mxu = lambda a: a.astype(MXU_DTYPE)
    row = lambda a: a.reshape(1, -1)
    e2 = mxu(jnp.kron(jnp.eye(2, dtype=F32), jnp.ones((HEAD_DIM, HEAD_DIM), F32)))
    e8 = mxu(jnp.repeat(jnp.eye(N_SG_GROUPS, dtype=F32), HEAD_DIM, axis=1))
    sgw = mxu(w["ev_sg_w"])
    sgb_full = jnp.repeat(w["ev_sg_b"].T, HEAD_DIM, axis=1)
    sgln_g, sgln_b = row(w["ev_sg_ln_g"]), row(w["ev_sg_ln_b"])
    sink = w["ev_sink"].reshape(N_Q_HEADS)
    ev_w_in_t = mxu(w["ev_w_in_t"])
    if ride is None:
        ev_w_out, od_w_in, od_w_out = mxu(w["ev_w_out"]), mxu(w["od_w_in"]), mxu(w["od_w_out"])
    wcat = mxu(jnp.concatenate([w["od_w_a"][0], w["od_w_x"][0], w["od_w_a"][1], w["od_w_x"][1]], axis=2))
    gate_bias = jnp.stack([w["od_b_a"][0], w["od_b_x"][0], w["od_b_a"][1], w["od_b_x"][1]])
    conv_b = row(w["od_conv_b"])
    ln_g, ln_b = w["ln_g"], w["ln_b"]

    (q, k, v, su, sv, g0), got = _even_proj(x, mod[0], ev_w_in_t, tabs, seq, rid(_gather_rider, ride and ride["ev_w_out"]))
    if ride is not None:
        ev_w_out = got[0].reshape(D_MODEL, D_MODEL)
    (ycat, lse, *sg_saved), got = _even_mix(q, k, v, su, sv, sink, sgln_g, sgln_b, sgw, sgb_full, e2, seq,
                                 rid(_gather_rider, ride and ride["od_w_in"]))
    if ride is not None:
        od_w_in = got[0]
    (zhat0, rstd0, x1, xr, g1), got = _even_out(ycat, g0, x, mod[0], mod[1], ev_w_out, od_w_in, ln_g[0:1], ln_b[0:1], seq,
                                      rid(_gather_rider, ride and ride["od_w_out"]))
    if ride is not None:
        od_w_out = got[0].reshape(D_MODEL, D_MODEL)
    lru = (w["od_conv_w"], conv_b, wcat, gate_bias, w["od_lam"], seq)
    hf, hpf, *saved_f, xc = _lru_fwd(xr, None, *lru, 0)
    hr, hpr, *saved_r = _lru_fwd(xr, xc, *lru, 1)
    dhs, dg1, dres1, loss, d_od_w_out, vec_o = _odd_out_and_loss(hf, hr, g1, x1, tgt, mod[1], od_w_out, ln_g[1:2], ln_b[1:2], seq)
    dxc_f, dw_f, vec_f = _lru_bwd(xc, dhs, hpf, *saved_f, wcat, w["od_lam"], seq, 0)
    dxc_r, dw_r, vec_r = _lru_bwd(xc, dhs, hpr, *saved_r, wcat, w["od_lam"], seq, 1)
    dx1, d_od_w_in, vec_p = _odd_proj_bwd(dxc_f, dxc_r, xr, dg1, x1, dres1, mod[1], w["od_conv_w"], od_w_in, seq)
    d_od_w_a = jnp.stack([dw_f[:, :, 0:128], dw_r[:, :, 0:128]])
    d_od_w_x = jnp.stack([dw_f[:, :, 128:256], dw_r[:, :, 128:256]])
    od_parts = [d_od_w_in.reshape(4, 2, 512, 512), d_od_w_out.reshape(4, 2, 128, D_MODEL),
                d_od_w_a.reshape(4, 2, 2 * BLK, BLK), d_od_w_x.reshape(4, 2, 2 * BLK, BLK)]
    (dycat, dg0, dres0, d_ev_w_out, vec_e), got_od = _even_out_bwd(dx1, zhat0, rstd0, ycat, g0, mod[0], ln_g[0:1], ev_w_out, seq,
                                                                   rid(_sibling_swap_rider, od_parts))
    if ride is not None:
        od_sums = _sum_sibling(ride["core"], od_parts, got_od, [ride["wire"]] * 4, "sum_sibling_od")
    (dq, dsu, dsv, dk, dv, d_sgw, d_sgb, vec_s, d_sink), od_slots = _even_mix_bwd(
        q, k, v, lse, ycat, dycat, su, *sg_saved, sink, sgln_g, sgln_b, sgw, e2, e8, seq,
        rid(_chip_exchange_rider, ride and od_sums))
    grad_x, d_ev_w_in_t, vec_x = _even_proj_bwd(dq, dk, dv, dsu, dsv, dg0, x, dres0, mod[0], tabs, ev_w_in_t, seq)

    rows, dmod_blk = _pack_small(vec_x, vec_e, vec_p, vec_o, vec_f, vec_r, vec_s, d_sink, d_sgb, loss)
    grads = {"rows": rows, "dmod_blk": dmod_blk, "ev_w_in_t": d_ev_w_in_t, "ev_w_out": d_ev_w_out, "ev_sg_w": d_sgw}
    if ride is None:
        grads.update({"od_w_in": d_od_w_in, "od_w_out": d_od_w_out, "od_w_a": d_od_w_a, "od_w_x": d_od_w_x})
    else:
        grads["od_slots"] = od_slots
    return grad_x, grads


ROW_DMOD, ROW_LN, ROW_SG_LN, ROW_SG_B, ROW_CONV_W, ROW_CONV_B, ROW_B_A, ROW_B_X, ROW_LAM, ROW_SINK, ROW_LOSS = (
    0, 6, 10, 11, 12, 16, 17, 19, 21, 23, 24)
SMALL_ROWS = 64


def _pack_small(vec_x, vec_e, vec_p, vec_o, vec_f, vec_r, vec_s, d_sink, d_sgb, loss):
    def body(x_ref, e_ref, p_ref, o_ref, f_ref, r_ref, s_ref, sink_ref, sgb_ref, loss_ref, rows_ref, dmod_ref):
        rows_ref[...] = jnp.zeros_like(rows_ref)
        dmod_ref[...] = jnp.zeros_like(dmod_ref)
        put = [(ROW_DMOD, x_ref, 0), (ROW_DMOD + 1, x_ref, 1), (ROW_DMOD + 2, e_ref, 2), (ROW_DMOD + 3, p_ref, 5),
               (ROW_DMOD + 4, p_ref, 6), (ROW_DMOD + 5, o_ref, 2), (ROW_LN, e_ref, 0), (ROW_LN + 1, e_ref, 1),
               (ROW_LN + 2, o_ref, 0), (ROW_LN + 3, o_ref, 1), (ROW_CONV_B, p_ref, 4), (ROW_B_A, f_ref, 0),
               (ROW_B_A + 1, r_ref, 0), (ROW_B_X, f_ref, 1), (ROW_B_X + 1, r_ref, 1), (ROW_LAM, f_ref, 2), (ROW_LAM + 1, r_ref, 2)]
        put += [(ROW_CONV_W + k, p_ref, k) for k in range(4)]
        for dst, ref, src in put:
            rows_ref[dst:dst + 1, :] = ref[src:src + 1, :]
            if dst < 6:
                dmod_ref[dst:dst + 1, :] = ref[src:src + 1, :]
        rows_ref[ROW_SG_LN:ROW_SG_LN + 1, 0:SG_WIDTH] = s_ref[0:1, :]
        rows_ref[ROW_SG_LN:ROW_SG_LN + 1, SG_WIDTH:2 * SG_WIDTH] = s_ref[1:2, :]
        lane = _lane_iota((1, LANES))
        sink = jnp.zeros((1, LANES), F32)
        for h in range(N_Q_HEADS):
            rows_ref[ROW_SG_B:ROW_SG_B + 1, h * LANES:(h + 1) * LANES] = sgb_ref[h:h + 1, :]
            sink = jnp.where(lane == h, sink_ref[h:h + 1, :], sink)
        rows_ref[ROW_SINK:ROW_SINK + 1, 0:LANES] = sink
        rows_ref[ROW_LOSS:ROW_LOSS + 1, 0:LANES] = jnp.where(lane == 0, loss_ref[0:1, :], 0.0)

    return pl.pallas_call(body, name="pack_small", out_shape=[_sds((SMALL_ROWS, D_MODEL)), _sds((8, D_MODEL))])(
        vec_x, vec_e, vec_p, vec_o, vec_f, vec_r, vec_s, d_sink, d_sgb, loss)


def _allgather8(block, name):
    m_per, n = block.shape

    def body(x_ref, out_ref, send_sems, recv_sems, local_sem):
        x, y, c = _place()
        me, sibling = (x, y, c), (x, y, 1 - c)
        chips = [(1 - x, y), (x, 1 - y), (1 - x, 1 - y)]

        def rows(px, py, pc):
            return out_ref.at[pl.ds((4 * px + 2 * py + pc) * m_per, m_per), :]

        def copy(k, blk, to, src=None):
            return pltpu.make_async_remote_copy(src_ref=rows(*blk) if src is None else src, dst_ref=rows(*blk),
                                                send_sem=send_sems.at[k], recv_sem=recv_sems.at[k], device_id=to,
                                                device_id_type=MESH)

        mine = pltpu.make_async_copy(x_ref, rows(*me), local_sem)
        mine.start()
        first = [copy(0, me, sibling, src=x_ref)] + [copy(1 + j, me, (*chip, c), src=x_ref) for j, chip in enumerate(chips)]
        for cp in first:
            cp.start()
        passed = [copy(4 + j, (*chip, c), sibling) for j, chip in enumerate(chips)]
        for j, chip in enumerate(chips):
            copy(1 + j, (*chip, c), me).wait_recv()
            passed[j].start()
        copy(0, sibling, me).wait_recv()
        for j, chip in enumerate(chips):
            copy(4 + j, (*chip, 1 - c), me).wait_recv()
        for cp in first + passed:
            cp.wait_send()
        mine.wait()

    return pl.pallas_call(
        body, name=name, out_shape=_sds((8 * m_per, n), block.dtype),
        in_specs=[pl.BlockSpec(memory_space=pltpu.VMEM)], out_specs=pl.BlockSpec(memory_space=pltpu.VMEM),
        scratch_shapes=[pltpu.SemaphoreType.DMA((7,)), pltpu.SemaphoreType.DMA((7,)), pltpu.SemaphoreType.DMA],
        compiler_params=pltpu.CompilerParams(vmem_limit_bytes=VMEM_LIMIT),
    )(block)


class _Copies:
    def __init__(self, send_sems, recv_sems, local_sems, stages):
        self.send_sems, self.recv_sems, self.local_sems, self.stages = send_sems, recv_sems, local_sems, stages
        self.sent, self.staged, self.locals = [], [], []

    def remote(self, k, src, dst, to):
        return pltpu.make_async_remote_copy(src_ref=src, dst_ref=dst, send_sem=self.send_sems.at[k], recv_sem=self.recv_sems.at[k],
                                            device_id=to, device_id_type=MESH)

    def send(self, k, src, dst, to):
        cp = self.remote(k, src, dst, to)
        cp.start()
        self.sent.append(cp)

    def arrived(self, k, dst, frm):
        self.remote(k, dst, dst, frm).wait_recv()

    def local(self, src, dst):
        k = len(self.staged)
        cp = pltpu.make_async_copy(src, self.stages[k], self.local_sems.at[2 * k])
        cp.start()
        self.staged.append((cp, dst))

    def flush(self):
        for k in range(len(self.locals), len(self.staged)):
            cp, dst = self.staged[k]
            cp.wait()
            out = pltpu.make_async_copy(self.stages[k], dst, self.local_sems.at[2 * k + 1])
            out.start()
            self.locals.append(out)

    def drain(self):
        self.flush()
        for cp in self.sent:
            cp.wait_send()
        for cp in self.locals:
            cp.wait()


def _comm_call(body, name, ins, out_shapes, n_remote, stages, side=None):
    n_in, n_out = len(ins), len(out_shapes)
    side_fn, side_ins, side_outs = side if side is not None else (None, (), [])
    s_in, s_out = len(side_ins), len(side_outs)

    def kern(*refs):
        in_refs, refs = refs[:n_in], refs[n_in:]
        side_in_refs, refs = refs[:s_in], refs[s_in:]
        out_refs, refs = refs[:n_out], refs[n_out:]
        side_out_refs, refs = refs[:s_out], refs[s_out:]
        cps = _Copies(refs[0], refs[1], refs[2], refs[3:])
        if side is None:
            body(cps, in_refs, out_refs)
        else:
            body(cps, in_refs, out_refs, partial(side_fn, *side_in_refs, *side_out_refs))

    hbm, vmem = pl.BlockSpec(memory_space=pl.ANY), pl.BlockSpec(memory_space=pltpu.VMEM)
    return pl.pallas_call(
        kern, name=name, out_shape=list(out_shapes) + list(side_outs), in_specs=[hbm] * n_in + [vmem] * s_in,
        out_specs=[hbm] * n_out + [vmem] * s_out,
        scratch_shapes=[pltpu.SemaphoreType.DMA((n_remote,)), pltpu.SemaphoreType.DMA((n_remote,)),
                        pltpu.SemaphoreType.DMA((2 * len(stages),))] + [pltpu.VMEM(s, d) for s, d in stages],
        compiler_params=pltpu.CompilerParams(vmem_limit_bytes=VMEM_LIMIT),
    )(*ins, *side_ins)


def _gather_to_all(cps, pairs, me, sibling, other_chips, c, base):
    idx = lambda p: 4 * p[0] + 2 * p[1] + p[2]
    for i, (src, dst) in enumerate(pairs):
        cps.local(src, dst.at[idx(me)])
        cps.send(base + 7 * i, src, dst.at[idx(me)], sibling)
        for j, chip in enumerate(other_chips):
            cps.send(base + 7 * i + 1 + j, src, dst.at[idx(me)], (*chip, c))
    cps.flush()
    for j, chip in enumerate(other_chips):
        for i, (_, dst) in enumerate(pairs):
            got = dst.at[idx((*chip, c))]
            cps.arrived(base + 7 * i + 1 + j, got, (*chip, c))
            cps.send(base + 7 * i + 4 + j, got, got, sibling)
    for i, (_, dst) in enumerate(pairs):
        cps.arrived(base + 7 * i, dst.at[idx(sibling)], sibling)
        for j, chip in enumerate(other_chips):
            cps.arrived(base + 7 * i + 4 + j, dst.at[idx((*chip, 1 - c))], sibling)


def _gather_weights(shards, small, side):
    n = len(shards)

    def body(cps, ins, outs, run_side):
        x, y, c = _place()
        me, sibling, mine = (x, y, c), (x, y, 1 - c), 2 * x + y
        chips = [(1 - x, y), (x, 1 - y), (1 - x, 1 - y)]
        for i in range(n):
            cps.local(ins[i], outs[i].at[mine])
        for j, (px, py) in enumerate(chips):
            for i in range(n):
                hr = shards[i].shape[0] // 2
                rows = pl.ds(c * hr, hr)
                cps.send(6 * i + j, ins[i].at[rows], outs[i].at[mine, rows], (px, py, c))
        run_side()
        _gather_to_all(cps, [(ins[n], outs[n])], me, sibling, chips, c, 6 * n)
        for j, (px, py) in enumerate(chips):
            for i in range(n):
                hr = shards[i].shape[0] // 2
                got = outs[i].at[2 * px + py, pl.ds(c * hr, hr)]
                cps.arrived(6 * i + j, got, (px, py, c))
                cps.send(6 * i + 3 + j, got, got, sibling)
        for j, (px, py) in enumerate(chips):
            for i in range(n):
                hr = shards[i].shape[0] // 2
                cps.arrived(6 * i + 3 + j, outs[i].at[2 * px + py, pl.ds((1 - c) * hr, hr)], sibling)
        cps.drain()

    return _comm_call(body, "gather_weights", list(shards) + [small],
                      [_sds((4,) + s.shape, s.dtype) for s in shards] + [_sds((8,) + small.shape, small.dtype)], 6 * n + 7,
                      [(a.shape, a.dtype) for a in list(shards) + [small]], side)


def _reduce_sibling(parts, dmod_rows):
    n = len(parts)

    def body(cps, ins, outs):
        x, y, c = _place()
        me, sibling = (x, y, c), (x, y, 1 - c)
        chips = [(1 - x, y), (x, 1 - y), (1 - x, 1 - y)]
        for i in range(n):
            cps.send(i, ins[i].at[:, 1 - c], outs[i], sibling)
        _gather_to_all(cps, [(ins[n], outs[n])], me, sibling, chips, c, n)
        for i in range(n):
            cps.arrived(i, outs[i], sibling)
        cps.drain()

    return _comm_call(body, "reduce_sibling", list(parts) + [dmod_rows],
                      [_sds((4,) + p.shape[2:], p.dtype) for p in parts] + [_sds((8,) + dmod_rows.shape, dmod_rows.dtype)], n + 7,
                      [(dmod_rows.shape, dmod_rows.dtype)])


def _reduce_chips(parts):
    n = len(parts)

    def body(cps, ins, outs):
        x, y, c = _place()
        mine = 2 * x + y
        chips = _other_chips(x, y)
        for i in range(n):
            cps.local(ins[i].at[mine], outs[i].at[mine])
        for j, (px, py) in enumerate(chips):
            for i in range(n):
                cps.send(3 * i + j, ins[i].at[2 * px + py], outs[i].at[mine], (px, py, c))
        cps.flush()
        for j, (px, py) in enumerate(chips):
            for i in range(n):
                cps.arrived(3 * i + j, outs[i].at[2 * px + py], (px, py, c))
        cps.drain()

    return _comm_call(body, "reduce_chips", list(parts), [_sds(p.shape, p.dtype) for p in parts], 3 * n,
                      [(p.shape[1:], p.dtype) for p in parts])


def _gather_reduced(shard_parts, repl_parts):
    ns, nr = len(shard_parts), len(repl_parts)

    def body(cps, ins, outs):
        x, y, c = _place()
        me, sibling = (x, y, c), (x, y, 1 - c)
        chips = [(1 - x, y), (x, 1 - y), (1 - x, 1 - y)]
        for i in range(ns):
            cps.local(ins[i], outs[i].at[c])
            cps.send(i, ins[i], outs[i].at[c], sibling)
        _gather_to_all(cps, [(ins[ns + i], outs[ns + i]) for i in range(nr)], me, sibling, chips, c, ns)
        for i in range(ns):
            cps.arrived(i, outs[i].at[1 - c], sibling)
        cps.drain()

    return _comm_call(body, "gather_reduced", list(shard_parts) + list(repl_parts),
                      [_sds((2,) + p.shape, p.dtype) for p in shard_parts] + [_sds((8,) + p.shape, p.dtype) for p in repl_parts],
                      ns + 7 * nr, [(p.shape, p.dtype) for p in list(shard_parts) + list(repl_parts)])


def _sum_sibling(core, parts, got, wire, name):
    n = len(parts)

    def body(core_ref, *refs):
        for i in range(n):
            refs[2 * n + i][0] = (refs[i][0] + refs[n + i][0]).astype(wire[i])

    keep_spec = lambda p: pl.BlockSpec((1, None) + p.shape[2:], lambda s, core_ref: (s, core_ref[0], 0, 0))
    slot_spec = lambda p: pl.BlockSpec((1,) + p.shape[2:], lambda s, core_ref: (s, 0, 0))
    return pl.pallas_call(
        body, name=name,
        grid_spec=pltpu.PrefetchScalarGridSpec(
            num_scalar_prefetch=1, grid=(4,), in_specs=[keep_spec(p) for p in parts] + [slot_spec(p) for p in parts],
            out_specs=[slot_spec(p) for p in parts]),
        out_shape=[_sds((4,) + p.shape[2:], wire[i]) for i, p in enumerate(parts)],
        compiler_params=_params("parallel"),
    )(core, *parts, *got)


def _sum_slots(slots, name):
    n = len(slots)

    def spec_pair(p):
        k, rows, cols = p.shape
        sub = 16 if p.dtype == BF16 else 8
        if (rows // 2) % sub == 0:
            return pl.BlockSpec((k, rows // 2, cols), lambda i: (0, i, 0)), pl.BlockSpec((rows // 2, cols), lambda i: (i, 0))
        return pl.BlockSpec((k, rows, cols), lambda i: (0, 0, 0)), pl.BlockSpec((rows, cols), lambda i: (0, 0))

    pairs = [spec_pair(p) for p in slots]

    def body(*refs):
        for i in range(n):
            acc = refs[i][0].astype(F32)
            for j in range(1, slots[i].shape[0]):
                acc = acc + refs[i][j].astype(F32)
            refs[n + i][...] = acc

    return pl.pallas_call(
        body, name=name, grid=(2,), in_specs=[a for a, _ in pairs], out_specs=[b for _, b in pairs],
        out_shape=[_sds(p.shape[1:]) for p in slots], compiler_params=_params("arbitrary"),
    )(*slots)


def _modulation(c_all, ada_w, ada_b):
    cols = ada_w.shape[2]

    def body(c_ref, w_ref, b_ref, o_ref):
        cc = c_ref[...]
        o_ref[0] = _mm(cc * _sigmoid(cc), w_ref[0]) + b_ref[0]

    return pl.pallas_call(
        body, name="modulation", grid=(2,),
        in_specs=[_full((8, D_MODEL)), pl.BlockSpec((1, D_MODEL, cols), lambda l: (l, 0, 0)), pl.BlockSpec((1, 1, cols), lambda l: (l, 0, 0))],
        out_specs=pl.BlockSpec((1, 8, cols), lambda l: (l, 0, 0)), out_shape=_sds((2, 8, cols)),
        compiler_params=_params("parallel"),
    )(c_all, ada_w, ada_b)


def _adamw_math(w, g, m, v):
    m = ADAM_B1 * m + (1.0 - ADAM_B1) * g
    v = ADAM_B2 * v + (1.0 - ADAM_B2) * (g * g)
    m_hat = m / (1.0 - ADAM_B1 ** ADAM_STEP)
    v_hat = v / (1.0 - ADAM_B2 ** ADAM_STEP)
    delta = -ADAM_LR * (m_hat / (jnp.sqrt(v_hat) + ADAM_EPS) + ADAM_WD * w)
    return delta, m, v


def _ada_update(c_all, dmod, w, m, v, rider=None):
    cols = w.shape[2]
    tr = 256
    per = D_MODEL // tr
    spec3 = pl.BlockSpec((1, tr, cols), lambda i: (i // per, i % per, 0))

    def body(c_ref, d_ref, w_ref, m_ref, v_ref, g_ref, dl_ref, nm_ref, nv_ref):
        cc = c_ref[...]
        g = _mm_tn(cc * _sigmoid(cc), d_ref[0])
        g_ref[0] = g
        dl_ref[0], nm_ref[0], nv_ref[0] = _adamw_math(w_ref[0], g, m_ref[0], v_ref[0])

    return _call(
        body, "ada_update", (2 * per,),
        [pl.BlockSpec((8, tr), lambda i: (0, i % per)), pl.BlockSpec((1, 8, cols), lambda i: (i // per, 0, 0)), spec3, spec3, spec3],
        [spec3] * 4, [_sds(w.shape)] * 4, (c_all, dmod, w, m, v), "parallel", rider=rider)


def _adamw_matrices(params):
    n = len(params)
    steps = 8

    def body(*refs):
        ins, outs = refs[:4 * n], refs[4 * n:]
        for j in range(n):
            w_ref, g_ref, m_ref, v_ref = ins[4 * j:4 * j + 4]
            g = g_ref[...]
            outs[4 * j][...] = g
            outs[4 * j + 1][...], outs[4 * j + 2][...], outs[4 * j + 3][...] = _adamw_math(w_ref[...], g, m_ref[...], v_ref[...])

    spec = lambda p: _rows(p[0].shape[0] // steps, p[0].shape[1])
    res = pl.pallas_call(
        body, name="adamw_matrices", grid=(steps,), in_specs=[spec(p) for p in params for _ in range(4)],
        out_specs=[spec(p) for p in params for _ in range(4)], out_shape=[_sds(p[0].shape) for p in params for _ in range(4)],
        compiler_params=_params("parallel"),
    )(*[a for p in params for a in p])
    return [tuple(res[4 * j:4 * j + 4]) for j in range(n)]


def _adamw_small(params):
    n = len(params)

    def body(*refs):
        ins, outs = refs[:4 * n], refs[4 * n:]
        for j in range(n):
            w_ref, g_ref, m_ref, v_ref = ins[4 * j:4 * j + 4]
            outs[3 * j][...], outs[3 * j + 1][...], outs[3 * j + 2][...] = _adamw_math(w_ref[...], g_ref[...], m_ref[...], v_ref[...])

    flat = [a for p in params for a in p]
    res = pl.pallas_call(body, name="adamw_small", out_shape=[_sds(p[0].shape) for p in params for _ in range(3)])(*flat)
    return [tuple(res[3 * j:3 * j + 3]) for j in range(n)]


def _cols(a, start, size):
    return lax.dynamic_slice_in_dim(a, start, size, axis=a.ndim - 1)


def kernel(x, c, positions, ada_w, ada_b, ln_g, ln_b, ev_w_in, ev_w_out, ev_sink, ev_sg_ln_g, ev_sg_ln_b, ev_sg_w, ev_sg_b, od_w_in, od_conv_w, od_conv_b, od_w_a, od_b_a, od_w_x, od_b_x, od_lam, od_w_out, loss_target, m_ada_w, m_ada_b, m_ln_g, m_ln_b, m_ev_w_in, m_ev_w_out, m_ev_sink, m_ev_sg_ln_g, m_ev_sg_ln_b, m_ev_sg_w, m_ev_sg_b, m_od_w_in, m_od_conv_w, m_od_conv_b, m_od_w_a, m_od_b_a, m_od_w_x, m_od_b_x, m_od_lam, m_od_w_out, v_ada_w, v_ada_b, v_ln_g, v_ln_b, v_ev_w_in, v_ev_w_out, v_ev_sink, v_ev_sg_ln_g, v_ev_sg_ln_b, v_ev_sg_w, v_ev_sg_b, v_od_w_in, v_od_conv_w, v_od_conv_b, v_od_w_a, v_od_b_a, v_od_w_x, v_od_b_x, v_od_lam, v_od_w_out):
    seq = x.shape[1]
    px, py, pc = _place()
    chip = 2 * px + py
    dev = 2 * chip + pc

    small = jnp.concatenate([od_conv_w[0].reshape(-1), od_conv_b[0], od_b_a[0].reshape(-1), jnp.zeros((256,), F32),
                             od_b_x[0].reshape(-1), od_lam[0].reshape(-1)]).reshape(3, D_MODEL)
    blk = jnp.concatenate([c, small, jnp.zeros((4, D_MODEL), F32)], axis=0)
    tr = lambda a: jnp.swapaxes(a, -1, -2)
    wire_w = lambda a: a.astype(MXU_DTYPE)
    posf = positions.astype(F32).reshape(seq, 1)
    ev_w_in4, g_small, *tabs = _gather_weights([wire_w(tr(ev_w_in[0]))], blk, _rope_tables(posf, seq))
    core = pc.astype(jnp.int32).reshape(1)
    ride = {"ev_w_out": wire_w(ev_w_out[0]), "od_w_in": wire_w(od_w_in[0]), "od_w_out": wire_w(od_w_out[0]),
            "core": core, "wire": MXU_DTYPE}
    c_all = g_small[:, 0, :]
    per_chip = g_small[0::2]
    conv_w = per_chip[:, 1].reshape(4, 4, 256).transpose(1, 0, 2).reshape(4, D_MODEL)
    conv_b = per_chip[:, 2, 0:256].reshape(D_MODEL)
    b_a = per_chip[:, 2, 256:768].reshape(4, 2, 256).transpose(1, 0, 2).reshape(2, D_MODEL)
    b_x = per_chip[:, 3, 0:512].reshape(4, 2, 256).transpose(1, 0, 2).reshape(2, D_MODEL)
    lam = per_chip[:, 3, 512:1024].reshape(4, 2, 256).transpose(1, 0, 2).reshape(2, D_MODEL)

    w_full = {
        "ev_w_in_t": ev_w_in4.reshape(EVEN_IN, D_MODEL),
        "ev_sink": ev_sink[0], "ev_sg_ln_g": ev_sg_ln_g[0], "ev_sg_ln_b": ev_sg_ln_b[0], "ev_sg_w": ev_sg_w[0],
        "ev_sg_b": ev_sg_b[0], "od_conv_w": conv_w, "od_conv_b": conv_b, "od_w_a": od_w_a[0], "od_b_a": b_a,
        "od_w_x": od_w_x[0], "od_b_x": b_x, "od_lam": lam, "ln_g": ln_g, "ln_b": ln_b,
    }

    ada_cols = ada_w.shape[2]
    mod_sh = _modulation(c_all, ada_w, _cols(ada_b, chip * ada_cols, ada_cols).reshape(2, 1, ada_cols))
    mod_all = _allgather8(mod_sh.reshape(16, ada_cols), "gather_mod").reshape(4, 2, 2, 8, ada_cols)[:, 0]
    mod_mine = lax.dynamic_index_in_dim(mod_all, dev, axis=2, keepdims=False)
    mod = mod_mine.transpose(1, 0, 2).reshape(2, 3, D_MODEL)

    grad_x, g = _local_step(x[0], tabs, loss_target[0], mod, w_full, seq, ride)

    parts = [g["ev_w_in_t"].reshape(4, 2, 352, D_MODEL), g["ev_w_out"].reshape(4, 2, 128, D_MODEL),
             g["ev_sg_w"].reshape(4, 2, BLK, BLK), g["rows"].reshape(4, 2, SMALL_ROWS // 8, D_MODEL)]
    wire = [MXU_DTYPE] * 3 + [F32]
    *got, dmod_gathered = _reduce_sibling(parts, g["dmod_blk"])
    ev_slots = list(_reduce_chips(_sum_sibling(core, parts, got, wire, "sum_sibling")))
    od_slots = list(g["od_slots"])
    mine = _sum_slots(ev_slots[0:2] + od_slots[0:2] + ev_slots[2:3] + od_slots[2:4] + ev_slots[3:4], "sum_chips")
    reduced = _gather_reduced(mine[:4], mine[4:])
    g_ev_w_in_t = reduced[0].reshape(704, D_MODEL)
    g_ev_w_out = reduced[1].reshape(256, D_MODEL)
    g_od_w_in = reduced[2].reshape(D_MODEL, 512)
    g_od_w_out = reduced[3].reshape(256, D_MODEL)
    g_sg_w = reduced[4].reshape(8 * BLK, BLK)
    g_w_a = reduced[5].reshape(16 * BLK, BLK)
    g_w_x = reduced[6].reshape(16 * BLK, BLK)
    gs = reduced[7].reshape(SMALL_ROWS, D_MODEL)
    loss = gs[ROW_LOSS, 0]
    dmod_all = dmod_gathered[:, 0:6].reshape(8, 2, 3 * D_MODEL)
    dmod_sh = _cols(dmod_all, chip * ada_cols, ada_cols).transpose(1, 0, 2)
    (g_ada_w, d_ada_w, nm_ada_w, nv_ada_w), _ = _ada_update(c_all, dmod_sh, ada_w, m_ada_w, v_ada_w)

    mats = (("ev_w_out", ev_w_out, g_ev_w_out, m_ev_w_out, v_ev_w_out), ("od_w_in", od_w_in, g_od_w_in, m_od_w_in, v_od_w_in),
            ("od_w_out", od_w_out, g_od_w_out, m_od_w_out, v_od_w_out), ("ev_sg_w", ev_sg_w, g_sg_w, m_ev_sg_w, v_ev_sg_w),
            ("od_w_a", od_w_a, g_w_a, m_od_w_a, v_od_w_a), ("od_w_x", od_w_x, g_w_x, m_od_w_x, v_od_w_x))
    upd = _adamw_matrices([(tr(ev_w_in[0]), g_ev_w_in_t, tr(m_ev_w_in[0]), tr(v_ev_w_in[0]))]
                          + [(w_.reshape(g_.shape), g_, m_.reshape(g_.shape), v_.reshape(g_.shape)) for _, w_, g_, m_, v_ in mats])
    big = {"ev_w_in": tuple(tr(a).reshape(ev_w_in.shape) for a in upd[0])}
    for (name, w_, _, _, _), u in zip(mats, upd[1:]):
        big[name] = tuple(a.reshape(w_.shape) for a in u)
    big["ada_w"] = (g_ada_w, d_ada_w, nm_ada_w, nv_ada_w)

    sh = lambda a: _cols(a, chip * 256, 256)
    small_g = {
        "ada_b": gs[ROW_DMOD:ROW_DMOD + 6].reshape(2, 3 * D_MODEL),
        "ln_g": jnp.stack([gs[ROW_LN], gs[ROW_LN + 2]]), "ln_b": jnp.stack([gs[ROW_LN + 1], gs[ROW_LN + 3]]),
        "ev_sink": gs[ROW_SINK:ROW_SINK + 1, 0:N_Q_HEADS], "ev_sg_ln_g": gs[ROW_SG_LN:ROW_SG_LN + 1, 0:SG_WIDTH],
        "ev_sg_ln_b": gs[ROW_SG_LN:ROW_SG_LN + 1, SG_WIDTH:2 * SG_WIDTH], "ev_sg_b": gs[ROW_SG_B].reshape(N_SG_GROUPS, BLK),
        "od_conv_w": sh(gs[ROW_CONV_W:ROW_CONV_W + 4]), "od_conv_b": sh(gs[ROW_CONV_B:ROW_CONV_B + 1]),
        "od_b_a": sh(gs[ROW_B_A:ROW_B_A + 2]), "od_b_x": sh(gs[ROW_B_X:ROW_B_X + 2]), "od_lam": sh(gs[ROW_LAM:ROW_LAM + 2]),
    }
    small_in = {"ada_b": (ada_b, m_ada_b, v_ada_b), "ln_g": (ln_g, m_ln_g, v_ln_g), "ln_b": (ln_b, m_ln_b, v_ln_b),
                "ev_sink": (ev_sink, m_ev_sink, v_ev_sink), "ev_sg_ln_g": (ev_sg_ln_g, m_ev_sg_ln_g, v_ev_sg_ln_g),
                "ev_sg_ln_b": (ev_sg_ln_b, m_ev_sg_ln_b, v_ev_sg_ln_b), "ev_sg_b": (ev_sg_b, m_ev_sg_b, v_ev_sg_b),
                "od_conv_w": (od_conv_w, m_od_conv_w, v_od_conv_w), "od_conv_b": (od_conv_b, m_od_conv_b, v_od_conv_b),
                "od_b_a": (od_b_a, m_od_b_a, v_od_b_a), "od_b_x": (od_b_x, m_od_b_x, v_od_b_x),
                "od_lam": (od_lam, m_od_lam, v_od_lam)}
    names_small = list(small_g)
    upd = _adamw_small([(small_in[n][0].reshape(small_g[n].shape), small_g[n], small_in[n][1].reshape(small_g[n].shape),
                         small_in[n][2].reshape(small_g[n].shape)) for n in names_small])
    res = dict(big)
    for n, (d_, nm_, nv_) in zip(names_small, upd):
        shape = small_in[n][0].shape
        res[n] = tuple(a.reshape(shape) for a in (small_g[n], d_, nm_, nv_))

    order = ["ada_w", "ada_b", "ln_g", "ln_b", "ev_w_in", "ev_w_out", "ev_sink", "ev_sg_ln_g", "ev_sg_ln_b", "ev_sg_w", "ev_sg_b",
             "od_w_in", "od_conv_w", "od_conv_b", "od_w_a", "od_b_a", "od_w_x", "od_b_x", "od_lam", "od_w_out"]
    return (loss, grad_x.reshape(x.shape), *[res[n][0] for n in order], *[res[n][1] for n in order],
            *[res[n][2] for n in order], *[res[n][3] for n in order])
```

```python
from functools import partial

import jax
import jax.numpy as jnp
import numpy as np
from jax import lax
from jax.experimental import pallas as pl
from jax.experimental.pallas import tpu as pltpu

F32 = jnp.float32
BF16 = jnp.bfloat16
MXU_DTYPE = BF16
ACT_DTYPE = MXU_DTYPE

D_MODEL = 1024
HEAD_DIM = 64
N_Q_HEADS = 8
Q_PER_KV = 4
ATTN_WIDTH = 512
BLK = 128
ROPE_DIM = 16
ROPE_THETA = 500000.0
N_SG_GROUPS = 8
SG_WIDTH = 512
EVEN_IN = 2816
ODD_IN = 2048
RNN_HEADS = 8
RG_LRU_C = 8.0
ALPHA = (2 * 2) ** 0.25
LN_EPS = 1e-5
NEG_INF = -1e30
ADAM_LR, ADAM_B1, ADAM_B2, ADAM_EPS, ADAM_WD, ADAM_STEP = 0.001, 0.9, 0.999, 1e-08, 0.01, 10

LANES = 128
VMEM_LIMIT = 56 * 1024 * 1024
MESH = pl.DeviceIdType.MESH


def _mm(a, b):
    return jnp.dot(a.astype(MXU_DTYPE), b.astype(MXU_DTYPE), preferred_element_type=F32)


def _mm_nt(a, b):
    return lax.dot_general(a.astype(MXU_DTYPE), b.astype(MXU_DTYPE), (((1,), (1,)), ((), ())), preferred_element_type=F32)


def _mm_tn(a, b):
    return lax.dot_general(a.astype(MXU_DTYPE), b.astype(MXU_DTYPE), (((0,), (0,)), ((), ())), preferred_element_type=F32)


def _sigmoid(x):
    return 1.0 / (1.0 + jnp.exp(-x))


def _ln_stats(z):
    mu = jnp.mean(z, axis=-1, keepdims=True)
    d = z - mu
    var = jnp.mean(d * d, axis=-1, keepdims=True)
    rstd = lax.rsqrt(var + LN_EPS)
    return d * rstd, rstd


def _ln_bwd(dout, zhat, rstd, g):
    dzh = dout * g
    m1 = jnp.mean(dzh, axis=-1, keepdims=True)
    m2 = jnp.mean(dzh * zhat, axis=-1, keepdims=True)
    return rstd * (dzh - m1 - zhat * m2)


def _group_sum(x, e2):
    hi = x.astype(MXU_DTYPE)
    lo = (x - hi.astype(F32)).astype(MXU_DTYPE)
    return jnp.dot(hi, e2, preferred_element_type=F32) + jnp.dot(lo, e2, preferred_element_type=F32)


def _lane_iota(shape):
    return lax.broadcasted_iota(jnp.int32, shape, 1)


def _to_kv_lanes(t, h):
    src_lo = (h % 2 == 0)
    dst_lo = (h // Q_PER_KV == 0)
    if src_lo != dst_lo:
        t = pltpu.roll(t, HEAD_DIM, 1)
    lane = _lane_iota(t.shape)
    keep = (lane < HEAD_DIM) if dst_lo else (lane >= HEAD_DIM)
    return jnp.where(keep, t, 0.0)


def _from_kv_lanes(t, h):
    src_lo = (h // Q_PER_KV == 0)
    dst_lo = (h % 2 == 0)
    lane = _lane_iota(t.shape)
    keep = (lane < HEAD_DIM) if src_lo else (lane >= HEAD_DIM)
    t = jnp.where(keep, t, 0.0)
    if src_lo != dst_lo:
        t = pltpu.roll(t, HEAD_DIM, 1)
    return t


def _rope(t, cos_t, sin_p, sin_m):
    half = ROPE_DIM // 2
    return t * cos_t + pltpu.roll(t, half, 1) * sin_p + pltpu.roll(t, LANES - half, 1) * sin_m


def _rope_t(d, cos_t, sin_p, sin_m):
    half = ROPE_DIM // 2
    return d * cos_t + pltpu.roll(d * sin_p, LANES - half, 1) + pltpu.roll(d * sin_m, half, 1)


def _band(ref, n, nb):
    prev = jnp.maximum(n - 1, 0)
    nxt = jnp.minimum(n + 1, nb - 1)
    rows = [ref[pl.ds(pl.multiple_of(j * BLK, BLK), BLK), :] for j in (prev, n, nxt)]
    return jnp.concatenate(rows, axis=0)


def _band_bias(n, seq):
    qi = lax.broadcasted_iota(jnp.int32, (BLK, 3 * BLK), 0)
    kj = lax.broadcasted_iota(jnp.int32, (BLK, 3 * BLK), 1)
    k_abs = n * BLK - BLK + kj
    valid = (jnp.abs(kj - BLK - qi) <= BLK) & (k_abs >= 0) & (k_abs < seq)
    bias = jnp.where(valid, 0.0, NEG_INF)
    return jnp.concatenate([bias] * Q_PER_KV, axis=0)


def _stack_heads(tile_of, kv):
    return jnp.concatenate([_to_kv_lanes(tile_of(h // 2), h) for h in range(Q_PER_KV * kv, Q_PER_KV * (kv + 1))], axis=0)


def _per_head_column(vals):
    row = lax.broadcasted_iota(jnp.int32, (Q_PER_KV * BLK, 1), 0)
    return jnp.where(row < BLK, vals[0], jnp.where(row < 2 * BLK, vals[1], jnp.where(row < 3 * BLK, vals[2], vals[3])))


def _softplus_neg(lam):
    e = jnp.exp(-jnp.abs(lam))
    u = 1.0 + e
    log1p_e = jnp.where(u == 1.0, e, jnp.log(u) * (e / (u - 1.0)))
    sp = jnp.maximum(-lam, 0.0) + log1p_e
    dsp = -1.0 / (1.0 + jnp.exp(lam))
    return sp, dsp


def _full(shape):
    return pl.BlockSpec(shape, lambda *_: (0,) * len(shape))


def _const(shape):
    return pl.BlockSpec(shape, lambda *_: (0,) * len(shape), pipeline_mode=pl.Buffered(1))


def _rows(tm, n):
    return pl.BlockSpec((tm, n), lambda i: (i, 0))


def _params(*sem):
    return pltpu.CompilerParams(dimension_semantics=sem, vmem_limit_bytes=VMEM_LIMIT)


def _sds(shape, dtype=F32):
    return jax.ShapeDtypeStruct(shape, dtype)


def _place():
    return lax.axis_index("x"), lax.axis_index("y"), lax.axis_index("c")


class _Rider:
    def __init__(self, ins, out_shapes, n_remote, n_local, plan):
        self.ins, self.out_shapes, self.n_remote, self.n_local, self.plan = list(ins), list(out_shapes), n_remote, n_local, plan

    def scratch(self):
        return [pltpu.SemaphoreType.DMA((self.n_remote,)), pltpu.SemaphoreType.DMA((self.n_remote,)),
                pltpu.SemaphoreType.DMA((max(self.n_local, 1),))]

    def run(self, first, in_refs, out_refs, sems):
        send_sems, recv_sems, local_sems = sems
        sends, recvs, locals_ = self.plan(in_refs, out_refs)
        remote = lambda k, src, dst, to: pltpu.make_async_remote_copy(
            src_ref=src, dst_ref=dst, send_sem=send_sems.at[k], recv_sem=recv_sems.at[k], device_id=to, device_id_type=MESH)
        if first:
            for k, src, dst, to in sends:
                remote(k, src, dst, to).start()
            for j, (src, dst) in enumerate(locals_):
                pltpu.make_async_copy(src, dst, local_sems.at[j]).start()
        else:
            for k, dst, frm in recvs:
                remote(k, dst, dst, frm).wait_recv()
            for k, src, dst, to in sends:
                remote(k, src, dst, to).wait_send()
            for j, (src, dst) in enumerate(locals_):
                pltpu.make_async_copy(src, dst, local_sems.at[j]).wait()


def _other_chips(x, y):
    return [(1 - x, y), (x, 1 - y), (1 - x, 1 - y)]


def _gather_rider(shard):
    hr = shard.shape[0] // 2

    def plan(ins, outs):
        x, y, c = _place()
        mine, src, dst = 2 * x + y, ins[0], outs[0]
        sends, recvs = [], []
        for j, (px, py) in enumerate(_other_chips(x, y)):
            for flip in range(2):
                tc = c if flip == 0 else 1 - c
                sends.append((2 * j + flip, src.at[pl.ds(c * hr, hr)], dst.at[mine, pl.ds(c * hr, hr)], (px, py, tc)))
                recvs.append((2 * j + flip, dst.at[2 * px + py, pl.ds(tc * hr, hr)], (px, py, tc)))
        return sends, recvs, [(src, dst.at[mine])]

    return _Rider([shard], [_sds((4,) + shard.shape, shard.dtype)], 6, 1, plan)


def _sibling_swap_rider(parts):
    n = len(parts)

    def plan(ins, outs):
        x, y, c = _place()
        sibling = (x, y, 1 - c)
        return ([(i, ins[i].at[:, 1 - c], outs[i], sibling) for i in range(n)], [(i, outs[i], sibling) for i in range(n)], [])

    return _Rider(parts, [_sds((4,) + p.shape[2:], p.dtype) for p in parts], n, 0, plan)


def _chip_exchange_rider(parts):
    n = len(parts)

    def plan(ins, outs):
        x, y, c = _place()
        mine = 2 * x + y
        sends, recvs = [], []
        for i in range(n):
            for j, (px, py) in enumerate(_other_chips(x, y)):
                sends.append((3 * i + j, ins[i].at[2 * px + py], outs[i].at[mine], (px, py, c)))
                recvs.append((3 * i + j, outs[i].at[2 * px + py], (px, py, c)))
        return sends, recvs, [(ins[i].at[mine], outs[i].at[mine]) for i in range(n)]

    return _Rider(parts, [_sds(p.shape, p.dtype) for p in parts], 3 * n, n, plan)


def _call(body, name, grid, in_specs, out_specs, out_shape, args, sem, scratch=(), rider=None):
    if rider is None:
        return list(pl.pallas_call(body, name=name, grid=grid, in_specs=in_specs, out_specs=out_specs, out_shape=out_shape,
                                   scratch_shapes=list(scratch), compiler_params=_params(sem))(*args)), []
    n_in, n_out, n_scr = len(in_specs), len(out_specs), len(scratch)
    r_in, r_out = len(rider.ins), len(rider.out_shapes)
    steps = grid[0]

    def riding(*refs):
        ins, r_ins = refs[:n_in], refs[n_in:n_in + r_in]
        outs = refs[n_in + r_in:n_in + r_in + n_out]
        r_outs = refs[n_in + r_in + n_out:n_in + r_in + n_out + r_out]
        scr = refs[n_in + r_in + n_out + r_out:n_in + r_in + n_out + r_out + n_scr]
        sems = refs[n_in + r_in + n_out + r_out + n_scr:]

        @pl.when(pl.program_id(0) == 0)
        def _():
            rider.run(True, r_ins, r_outs, sems)

        body(*ins, *outs, *scr)

        @pl.when(pl.program_id(0) == steps - 1)
        def _():
            rider.run(False, r_ins, r_outs, sems)

    hbm = pl.BlockSpec(memory_space=pl.ANY)
    res = pl.pallas_call(
        riding, name=name, grid=grid, in_specs=list(in_specs) + [hbm] * r_in, out_specs=list(out_specs) + [hbm] * r_out,
        out_shape=list(out_shape) + rider.out_shapes, scratch_shapes=list(scratch) + rider.scratch(),
        compiler_params=_params("arbitrary"),
    )(*args, *rider.ins)
    return list(res[:n_out]), list(res[n_out:])


def _row_tile(seq, want):
    return want if seq % want == 0 else seq


def _rope_tables(posf, seq):
    half = ROPE_DIM // 2
    inv_freq = np.power(np.float32(ROPE_THETA), -np.arange(half, dtype=np.float32) / np.float32(half)).astype(np.float32)
    j = np.arange(LANES) % HEAD_DIM
    invf = jnp.asarray(np.where(j < ROPE_DIM, inv_freq[j % half], 0.0).astype(np.float32).reshape(1, LANES))
    m_p = jnp.asarray(((j >= half) & (j < ROPE_DIM)).astype(np.float32).reshape(1, LANES))
    m_m = jnp.asarray(-(j < half).astype(np.float32).reshape(1, LANES))
    tm = _row_tile(seq, 512)

    def body(pos_ref, invf_ref, mp_ref, mm_ref, cos_ref, sp_ref, sm_ref):
        def block(i, carry):
            rows = pl.ds(pl.multiple_of(i * tm, tm), tm)
            ang = pos_ref[rows, :] * invf_ref[...]
            s = jnp.sin(ang)
            cos_ref[rows, :] = jnp.cos(ang)
            sp_ref[rows, :] = s * mp_ref[...]
            sm_ref[rows, :] = s * mm_ref[...]
            return carry

        lax.fori_loop(0, seq // tm, block, 0)

    return body, (posf, invf, m_p, m_m), [_sds((seq, LANES))] * 3


def _even_proj(x, mod, w_in_t, tabs, seq, rider=None):
    tm = _row_tile(seq, 512)

    def body(x_ref, mod_ref, w_ref, cos_ref, sp_ref, sm_ref, q_ref, k_ref, v_ref, su_ref, sv_ref, g_ref):
        h = x_ref[...] * (1.0 + mod_ref[1:2, :]) + mod_ref[0:1, :]
        p = _mm_nt(h, w_ref[...])
        cos_t, sin_p, sin_m = cos_ref[...], sp_ref[...], sm_ref[...]
        for j in range(ATTN_WIDTH // LANES):
            q_ref[:, j * LANES:(j + 1) * LANES] = _rope(p[:, j * LANES:(j + 1) * LANES], cos_t, sin_p, sin_m).astype(q_ref.dtype)
        k_ref[...] = _rope(p[:, 512:640], cos_t, sin_p, sin_m).astype(k_ref.dtype)
        v_ref[...] = p[:, 640:768].astype(v_ref.dtype)
        su_ref[...] = p[:, 768:1280].astype(su_ref.dtype)
        sv_ref[...] = p[:, 1280:1792].astype(sv_ref.dtype)
        g_ref[...] = p[:, 1792:2816].astype(g_ref.dtype)

    return _call(
        body, "even_proj", (seq // tm,),
        [_rows(tm, D_MODEL), _full((3, D_MODEL)), _const((EVEN_IN, D_MODEL))] + [_rows(tm, LANES)] * 3,
        [_rows(tm, 512), _rows(tm, LANES), _rows(tm, LANES), _rows(tm, 512), _rows(tm, 512), _rows(tm, D_MODEL)],
        [_sds((seq, 512), MXU_DTYPE), _sds((seq, LANES), MXU_DTYPE), _sds((seq, LANES), MXU_DTYPE), _sds((seq, 512), ACT_DTYPE),
         _sds((seq, 512), ACT_DTYPE), _sds((seq, D_MODEL), ACT_DTYPE)],
        (x, mod, w_in_t, *tabs), "parallel", rider=rider)


def _sg_forward(sv, lng, lnb, sgw_ref, sgb, e2):
    vn, vhat, rstd, svo = [], [], [], []
    for j in range(SG_WIDTH // LANES):
        t = sv[:, j * LANES:(j + 1) * LANES]
        mu = _group_sum(t, e2) * (1.0 / HEAD_DIM)
        d = t - mu
        var = _group_sum(d * d, e2) * (1.0 / HEAD_DIM)
        r = lax.rsqrt(var + LN_EPS)
        vh = d * r
        vhat.append(vh)
        rstd.append(r)
        vn.append(vh * lng[:, j * LANES:(j + 1) * LANES] + lnb[:, j * LANES:(j + 1) * LANES])
    lane = _lane_iota((BLK, LANES))
    for j in range(SG_WIDTH // LANES):
        lo = _mm(sgw_ref[2 * j], vn[j])
        hi = _mm(sgw_ref[2 * j + 1], vn[j])
        svo.append(jnp.where(lane < HEAD_DIM, lo, hi) + sgb[:, j * LANES:(j + 1) * LANES])
    return svo, vn, vhat, rstd


def _even_mix(q, k, v, su, sv, sink, sgln_g, sgln_b, sgw, sgb_full, e2, seq, rider=None):
    nb = seq // BLK

    def body(sink_ref, q_ref, k_ref, v_ref, su_ref, sv_ref, lng_ref, lnb_ref, sgw_ref, sgb_ref, e2_ref, ycat_ref, lse_ref,
             svo_ref, vhat_ref, rstd_ref):
        n = pl.program_id(0)
        kband = _band(k_ref, n, nb)
        vband = _band(v_ref, n, nb)
        bias = _band_bias(n, seq)
        lane = _lane_iota((BLK, LANES))
        lse = jnp.zeros((BLK, LANES), F32)
        q_tile = lambda j: q_ref[:, j * LANES:(j + 1) * LANES].astype(F32)
        acc = [jnp.zeros((BLK, LANES), F32) for _ in range(ATTN_WIDTH // LANES)]
        for kv in range(N_Q_HEADS // Q_PER_KV):
            heads = range(Q_PER_KV * kv, Q_PER_KV * (kv + 1))
            sink = _per_head_column([sink_ref[h] for h in heads])
            s = _mm_nt(_stack_heads(q_tile, kv), kband) * (HEAD_DIM ** -0.5) + bias
            m = jnp.maximum(jnp.max(s, axis=1, keepdims=True), sink)
            p = jnp.exp(s - m)
            denom = jnp.sum(p, axis=1, keepdims=True) + jnp.exp(sink - m)
            o4 = _mm(p / denom, vband)
            l4 = m + jnp.log(denom)
            for g, h in enumerate(heads):
                acc[h // 2] = acc[h // 2] + _from_kv_lanes(o4[g * BLK:(g + 1) * BLK], h)
                lse = jnp.where(lane == h, l4[g * BLK:(g + 1) * BLK], lse)
        for j in range(ATTN_WIDTH // LANES):
            ycat_ref[:, j * LANES:(j + 1) * LANES] = acc[j].astype(ycat_ref.dtype)
        lse_ref[...] = lse
        svo, _, vhat, rstd = _sg_forward(sv_ref[...].astype(F32), lng_ref[...], lnb_ref[...], sgw_ref, sgb_ref[...], e2_ref[...])
        for j in range(SG_WIDTH // LANES):
            cs = slice(j * LANES, (j + 1) * LANES)
            ysg = su_ref[:, cs].astype(F32) * svo[j]
            ycat_ref[:, ATTN_WIDTH + j * LANES:ATTN_WIDTH + (j + 1) * LANES] = ysg.astype(ycat_ref.dtype)
            svo_ref[:, cs], vhat_ref[:, cs], rstd_ref[:, cs] = (t.astype(svo_ref.dtype) for t in (svo[j], vhat[j], rstd[j]))

    blk = lambda w: pl.BlockSpec((BLK, w), lambda n: (n, 0))
    return _call(
        body, "even_mix", (nb,),
        [pl.BlockSpec(memory_space=pltpu.SMEM), blk(512), _full((seq, LANES)), _full((seq, LANES)), blk(512), blk(512),
         _full((1, 512)), _full((1, 512)), _full((8, BLK, BLK)), _full((BLK, 512)), _full((LANES, LANES))],
        [blk(D_MODEL), blk(LANES)] + [blk(SG_WIDTH)] * 3,
        [_sds((seq, D_MODEL), ACT_DTYPE), _sds((seq, LANES))] + [_sds((seq, SG_WIDTH), ACT_DTYPE)] * 3,
        (sink, q, k, v, su, sv, sgln_g, sgln_b, sgw, sgb_full, e2), "parallel", rider=rider)


def _even_out(ycat, g, x, mod, mod_next, w_out, w_in4_next, ln_g, ln_b, seq, rider=None):
    tm = _row_tile(seq, 512)
    cs = ODD_IN // 4

    def body(y_ref, g_ref, x_ref, mod_ref, modn_ref, wo_ref, wi_ref, g1_ref, b1_ref, zhat_ref, rstd_ref, x1_ref, xr_ref, gn_ref):
        gg = g_ref[...].astype(F32)
        out = _mm(y_ref[...].astype(F32) * (gg * _sigmoid(gg)), wo_ref[...])
        z = ALPHA * x_ref[...] + mod_ref[2:3, :] * out
        zhat, rstd = _ln_stats(z)
        zhat_ref[...] = zhat
        rstd_ref[...] = rstd
        x1 = zhat * g1_ref[...] + b1_ref[...]
        x1_ref[...] = x1
        hb = (x1 * (1.0 + modn_ref[1:2, :]) + modn_ref[0:1, :]).astype(MXU_DTYPE)
        for s in range(2):
            xr_ref[:, s * cs:(s + 1) * cs] = jnp.dot(hb, wi_ref[s], preferred_element_type=F32)
            gn_ref[:, s * cs:(s + 1) * cs] = jnp.dot(hb, wi_ref[2 + s], preferred_element_type=F32).astype(gn_ref.dtype)

    return _call(
        body, "even_out", (seq // tm,),
        [_rows(tm, D_MODEL)] * 3 + [_full((3, D_MODEL)), _full((3, D_MODEL)), _const((D_MODEL, D_MODEL)), _const((4, D_MODEL, cs)),
                                    _full((1, D_MODEL)), _full((1, D_MODEL))],
        [_rows(tm, D_MODEL), _rows(tm, 1)] + [_rows(tm, D_MODEL)] * 3,
        [_sds((seq, D_MODEL)), _sds((seq, 1))] + [_sds((seq, D_MODEL))] * 2 + [_sds((seq, D_MODEL), ACT_DTYPE)],
        (ycat, g, x, mod, mod_next, w_out, w_in4_next, ln_g, ln_b), "parallel", rider=rider)


def _halo_specs(tm, seq, width, order=lambda i: i):
    per = tm // 8
    last = seq // 8 - 1
    return [pl.BlockSpec((8, width), lambda i: (jnp.maximum(order(i) * per - 1, 0), 0)),
            pl.BlockSpec((tm, width), lambda i: (order(i), 0)),
            pl.BlockSpec((8, width), lambda i: (jnp.minimum((order(i) + 1) * per, last), 0))]


def _extended(prev_ref, main_ref, next_ref, i, n_steps):
    prev = jnp.where(i > 0, prev_ref[...], 0.0)
    nxt = jnp.where(i < n_steps - 1, next_ref[...], 0.0)
    return jnp.concatenate([prev, main_ref[...], nxt], axis=0)


def _shifted(ext, off, tm):
    if off == 0:
        return ext[8:8 + tm]
    return pltpu.roll(ext, (-off) % ext.shape[0], 0)[8:8 + tm]


SCAN_SUB = 8


def _lru_gate(xh, pre, bias, sp, hs, d):
    r = _sigmoid(pre[:, 0:LANES] + bias[2 * d:2 * d + 1, hs])
    ig = _sigmoid(pre[:, LANES:2 * LANES] + bias[2 * d + 1:2 * d + 2, hs])
    neg_log_a = RG_LRU_C * r * sp[d:d + 1, hs]
    a = jnp.exp(-neg_log_a)
    u = jnp.tanh(neg_log_a) * (a * a + 1.0)
    inv_s = lax.rsqrt(jnp.maximum(u, jnp.finfo(F32).tiny))
    return r, ig, a, u * inv_s, inv_s


def _conv_block(xp_ref, xm_ref, xn_ref, cw_ref, cb_ref, blk, steps, tm):
    ext = _extended(xp_ref, xm_ref, xn_ref, blk, steps)
    return cb_ref[...] + sum(cw_ref[kk:kk + 1, :] * _shifted(ext, kk - 2, tm) for kk in range(4))


def _scan_tiles(a_ref, b_ref, h_ref, hprev_ref, carry_h, carry_a, rows, descending, post):
    sub = SCAN_SUB
    tiles = rows // sub
    row = lax.broadcasted_iota(jnp.int32, (sub, D_MODEL), 0)

    def shift(v, d, fill):
        if descending:
            return jnp.where(row <= sub - 1 - d, pltpu.roll(v, sub - d, 0), fill)
        return jnp.where(row >= d, pltpu.roll(v, d, 0), fill)

    def last(v):
        return jnp.broadcast_to(v[0:1, :] if descending else v[sub - 1:sub, :], v.shape)

    def tile(j, c):
        ch, ca = c
        r0 = pl.multiple_of(((tiles - 1 - j) if descending else j) * sub, sub)
        at = a_ref[pl.ds(r0, sub), :]
        bt = b_ref[pl.ds(r0, sub), :]
        coef = shift(at, 1, ca) if post else at
        acc_a, acc_b = coef, bt
        for d in (1, 2, 4):
            acc_b = acc_b + acc_a * shift(acc_b, d, 0.0)
            acc_a = acc_a * shift(acc_a, d, 1.0)
        h = acc_b + acc_a * ch
        h_ref[pl.ds(r0, sub), :] = h
        if post:
            return last(h), last(at)
        hprev_ref[pl.ds(r0, sub), :] = shift(h, 1, ch)
        return last(h), ca

    ch, ca = lax.fori_loop(0, tiles, tile, (carry_h[...], carry_a[...]), unroll=4)
    carry_h[...] = ch
    carry_a[...] = ca


def _lru_fwd(xr, xc, conv_w, conv_b, wcat, bias, lam, seq, d):
    tb = _row_tile(seq, 512)
    steps = seq // tb
    descending = d == 1
    order = (lambda i: steps - 1 - i) if descending else (lambda i: i)
    with_conv = xc is None
    n_x = 5 if with_conv else 1

    def body(*refs):
        x_refs, (w_ref, bias_ref, lam_ref) = refs[:n_x], refs[n_x:n_x + 3]
        h_ref, hp_ref, a_ref, r_ref, i_ref, s_ref, q_ref = refs[n_x + 3:n_x + 10]
        b_scr, carry_h, carry_a = refs[-3:]
        i = pl.program_id(0)

        @pl.when(i == 0)
        def _():
            carry_h[...] = jnp.zeros_like(carry_h)
            carry_a[...] = jnp.zeros_like(carry_a)

        if with_conv:
            xc_ref = refs[n_x + 10]
            xc_ref[...] = _conv_block(*x_refs, order(i), steps, tb)
        else:
            xc_ref = x_refs[0]
        sp, _ = _softplus_neg(lam_ref[...])
        bias = bias_ref[...]
        for h in range(RNN_HEADS):
            hs = slice(h * LANES, (h + 1) * LANES)
            xh = xc_ref[:, hs]
            r, ig, a, s, q = _lru_gate(xh, _mm(xh, w_ref[h, :, 2 * d * LANES:2 * (d + 1) * LANES]), bias, sp, hs, d)
            a_ref[:, hs] = a
            b_scr[:, hs] = s * ig * xh
            for ref, val in ((r_ref, r), (i_ref, ig), (s_ref, s), (q_ref, q)):
                ref[:, hs] = val.astype(ref.dtype)
        _scan_tiles(a_ref, b_scr, h_ref, hp_ref, carry_h, carry_a, tb, descending, post=False)

    row_spec = pl.BlockSpec((tb, D_MODEL), lambda i: (order(i), 0))
    if with_conv:
        x_specs, x_args = _halo_specs(tb, seq, D_MODEL, order) + [_full((4, D_MODEL)), _full((1, D_MODEL))], (xr, xr, xr, conv_w, conv_b)
    else:
        x_specs, x_args = [row_spec], (xc,)
    n_out = 8 if with_conv else 7
    return pl.pallas_call(
        body, name="lru_fwd_%d" % d, grid=(steps,),
        in_specs=x_specs + [_full((8, LANES, 512)), _full((4, D_MODEL)), _full((2, D_MODEL))],
        out_specs=[row_spec] * n_out,
        out_shape=[_sds((seq, D_MODEL))] * 3 + [_sds((seq, D_MODEL), ACT_DTYPE)] * 4 + [_sds((seq, D_MODEL))] * (n_out - 7),
        scratch_shapes=[pltpu.VMEM((tb, D_MODEL), F32)] + [pltpu.VMEM((SCAN_SUB, D_MODEL), F32)] * 2,
        compiler_params=_params("arbitrary"),
    )(*x_args, wcat, bias, lam)


def _odd_out_and_loss(hf, hr, g, x1, tgt, mod, w_out, ln_g, ln_b, seq):
    tm = _row_tile(seq, 512)

    def body(hf_ref, hr_ref, g_ref, x_ref, t_ref, mod_ref, w_ref, lg_ref, lb_ref,
             dhs_ref, dg_ref, dres_ref, loss_ref, dw_ref, vec_ref):
        @pl.when(pl.program_id(0) == 0)
        def _():
            loss_ref[...] = jnp.zeros_like(loss_ref)
            dw_ref[...] = jnp.zeros_like(dw_ref)
            vec_ref[...] = jnp.zeros_like(vec_ref)

        gg = g_ref[...].astype(F32)
        sg = _sigmoid(gg)
        silu = gg * sg
        hsum = hf_ref[...] + hr_ref[...]
        y = hsum * silu
        out = _mm(y, w_ref[...])
        gate = mod_ref[2:3, :]
        z = ALPHA * x_ref[...] + gate * out
        zhat, rstd = _ln_stats(z)
        x2 = zhat * lg_ref[...] + lb_ref[...]
        err = x2 - t_ref[...]
        loss_ref[...] += 0.5 * jnp.sum(jnp.mean(err * err, axis=-1, keepdims=True))
        dx2 = err * (1.0 / D_MODEL)
        dz = _ln_bwd(dx2, zhat, rstd, lg_ref[...])
        vec_ref[0:1, :] += jnp.sum(dx2 * zhat, axis=0, keepdims=True)
        vec_ref[1:2, :] += jnp.sum(dx2, axis=0, keepdims=True)
        vec_ref[2:3, :] += jnp.sum(dz * out, axis=0, keepdims=True)
        dres_ref[...] = ALPHA * dz
        dout = gate * dz
        dw_ref[...] += _mm_tn(y, dout)
        dy = _mm_nt(dout, w_ref[...])
        dhs_ref[...] = dy * silu
        dg_ref[...] = (dy * hsum * (sg * (1.0 + gg * (1.0 - sg)))).astype(dg_ref.dtype)

    return pl.pallas_call(
        body, name="odd_out_loss", grid=(seq // tm,),
        in_specs=[_rows(tm, D_MODEL)] * 5 + [_full((3, D_MODEL)), _const((D_MODEL, D_MODEL)),
                                             _full((1, D_MODEL)), _full((1, D_MODEL))],
        out_specs=[_rows(tm, D_MODEL)] * 3 + [_full((8, LANES)), _full((D_MODEL, D_MODEL)), _full((8, D_MODEL))],
        out_shape=[_sds((seq, D_MODEL)), _sds((seq, D_MODEL), ACT_DTYPE), _sds((seq, D_MODEL)), _sds((8, LANES)),
                   _sds((D_MODEL, D_MODEL)), _sds((8, D_MODEL))],
        compiler_params=_params("arbitrary"),
    )(hf, hr, g, x1, tgt, mod, w_out, ln_g, ln_b)


def _lru_bwd(xc, dhs, hprev, a_d, r_d, i_d, s_d, q_d, wcat, lam, seq, d):
    tb = _row_tile(seq, 512)
    steps = seq // tb
    descending = d == 0
    order = (lambda i: steps - 1 - i) if descending else (lambda i: i)
    cols = slice(2 * d * LANES, 2 * (d + 1) * LANES)

    def body(xc_ref, dhs_ref, hp_ref, a_ref, r_ref, i_ref, s_ref, q_ref, w_ref, lam_ref, dxc_ref, dw_ref, vec_ref,
             g_scr, carry_h, carry_a):
        i = pl.program_id(0)

        @pl.when(i == 0)
        def _():
            dw_ref[...] = jnp.zeros_like(dw_ref)
            vec_ref[...] = jnp.zeros_like(vec_ref)
            carry_h[...] = jnp.zeros_like(carry_h)
            carry_a[...] = jnp.zeros_like(carry_a)

        sp, dsp = _softplus_neg(lam_ref[...])
        _scan_tiles(a_ref, dhs_ref, g_scr, None, carry_h, carry_a, tb, descending, post=True)
        for h in range(RNN_HEADS):
            hs = slice(h * LANES, (h + 1) * LANES)
            xh, a = xc_ref[:, hs], a_ref[:, hs]
            r, ig, s = r_ref[:, hs].astype(F32), i_ref[:, hs].astype(F32), s_ref[:, hs].astype(F32)
            db = g_scr[:, hs]
            da = db * hp_ref[:, hs]
            dlog_a = da * a - (db * ig * xh) * (a * a * q_ref[:, hs].astype(F32))
            dpr = dlog_a * (-RG_LRU_C) * sp[d:d + 1, hs] * r * (1.0 - r)
            dpi = db * s * xh * ig * (1.0 - ig)
            vec_ref[0:1, hs] += jnp.sum(dpr, axis=0, keepdims=True)
            vec_ref[1:2, hs] += jnp.sum(dpi, axis=0, keepdims=True)
            vec_ref[2:3, hs] += jnp.sum(dlog_a * r, axis=0, keepdims=True) * (-RG_LRU_C) * dsp[d:d + 1, hs]
            dcat = jnp.concatenate([dpr, dpi], axis=1)
            dw_ref[h] += _mm_tn(xh, dcat)
            dxc_ref[:, hs] = db * s * ig + _mm_nt(dcat, w_ref[h, :, cols])

    row_spec = pl.BlockSpec((tb, D_MODEL), lambda i: (order(i), 0))
    return pl.pallas_call(
        body, name="lru_bwd_%d" % d, grid=(steps,),
        in_specs=[row_spec] * 8 + [_full((8, LANES, 512)), _full((2, D_MODEL))],
        out_specs=[row_spec, _full((8, LANES, 2 * LANES)), _full((8, D_MODEL))],
        out_shape=[_sds((seq, D_MODEL)), _sds((8, LANES, 2 * LANES)), _sds((8, D_MODEL))],
        scratch_shapes=[pltpu.VMEM((tb, D_MODEL), F32)] + [pltpu.VMEM((SCAN_SUB, D_MODEL), F32)] * 2,
        compiler_params=_params("arbitrary"),
    )(xc, dhs, hprev, a_d, r_d, i_d, s_d, q_d, wcat, lam)


def _odd_proj_bwd(dxc_f, dxc_r, xr, dg, x1, dres, mod, conv_w, w_in4, seq):
    tm = _row_tile(seq, 512)
    steps = seq // tm

    def body(fp_ref, fm_ref, fn_ref, rp_ref, rm_ref, rn_ref, xp_ref, xm_ref, xn_ref, dg_ref, x_ref, dres_ref, mod_ref, cw_ref,
             w_ref, dx_ref, dw_ref, vec_ref, dpb_ref):
        i = pl.program_id(0)

        @pl.when(i == 0)
        def _():
            vec_ref[...] = jnp.zeros_like(vec_ref)
            dw_ref[...] = jnp.zeros_like(dw_ref)

        dxc_m = fm_ref[...] + rm_ref[...]
        dext = jnp.concatenate([jnp.where(i > 0, fp_ref[...] + rp_ref[...], 0.0), dxc_m,
                                jnp.where(i < steps - 1, fn_ref[...] + rn_ref[...], 0.0)], axis=0)
        xext = _extended(xp_ref, xm_ref, xn_ref, i, steps)
        dxr = sum(cw_ref[kk:kk + 1, :] * _shifted(dext, 2 - kk, tm) for kk in range(4))
        for kk in range(4):
            vec_ref[kk:kk + 1, :] += jnp.sum(dxc_m * _shifted(xext, kk - 2, tm), axis=0, keepdims=True)
        vec_ref[4:5, :] += jnp.sum(dxc_m, axis=0, keepdims=True)
        dpb_ref[:, :D_MODEL] = dxr.astype(dpb_ref.dtype)
        dpb_ref[:, D_MODEL:] = dg_ref[...].astype(dpb_ref.dtype)
        cs = ODD_IN // 4
        dh = sum(_mm_nt(dpb_ref[:, s * cs:(s + 1) * cs], w_ref[s]) for s in range(4))
        x = x_ref[...]
        h_t = (x * (1.0 + mod_ref[1:2, :]) + mod_ref[0:1, :]).T.astype(MXU_DTYPE)
        for s in range(4):
            dw_ref[s] += jnp.dot(h_t, dpb_ref[:, s * cs:(s + 1) * cs], preferred_element_type=F32)
        vec_ref[5:6, :] += jnp.sum(dh, axis=0, keepdims=True)
        vec_ref[6:7, :] += jnp.sum(dh * x, axis=0, keepdims=True)
        dx_ref[...] = dres_ref[...] + dh * (1.0 + mod_ref[1:2, :])

    return pl.pallas_call(
        body, name="odd_proj_bwd", grid=(steps,),
        in_specs=_halo_specs(tm, seq, D_MODEL) * 3 + [_rows(tm, D_MODEL)] * 3
        + [_full((3, D_MODEL)), _full((4, D_MODEL)), _const((4, D_MODEL, ODD_IN // 4))],
        out_specs=[_rows(tm, D_MODEL), _const((4, D_MODEL, ODD_IN // 4)), _full((8, D_MODEL))],
        out_shape=[_sds((seq, D_MODEL)), _sds((4, D_MODEL, ODD_IN // 4)), _sds((8, D_MODEL))],
        scratch_shapes=[pltpu.VMEM((tm, ODD_IN), MXU_DTYPE)],
        compiler_params=_params("arbitrary"),
    )(dxc_f, dxc_f, dxc_f, dxc_r, dxc_r, dxc_r, xr, xr, xr, dg, x1, dres, mod, conv_w, w_in4)


def _even_out_bwd(dx1, zhat, rstd, ycat, g, mod, ln_g, w_out, seq, rider=None):
    tm = _row_tile(seq, 512)
    steps = seq // tm

    def body(dx_ref, zh_ref, rs_ref, y_ref, g_ref, mod_ref, lg_ref, w_ref, dy_ref, dg_ref, dres_ref, dw_ref, vec_ref):
        i = pl.program_id(0)

        @pl.when(i == 0)
        def _():
            dw_ref[...] = jnp.zeros_like(dw_ref)
            vec_ref[...] = jnp.zeros_like(vec_ref)

        zhat = zh_ref[...]
        dx1_ = dx_ref[...]
        dz = _ln_bwd(dx1_, zhat, rs_ref[...], lg_ref[...])
        vec_ref[0:1, :] += jnp.sum(dx1_ * zhat, axis=0, keepdims=True)
        vec_ref[1:2, :] += jnp.sum(dx1_, axis=0, keepdims=True)
        dres_ref[...] = ALPHA * dz
        gate = mod_ref[2:3, :]
        gg = g_ref[...].astype(F32)
        sg = _sigmoid(gg)
        silu = gg * sg
        ycat_ = y_ref[...].astype(F32)
        dw_ref[...] += _mm_tn(ycat_ * silu, dz)
        dy = _mm_nt(gate * dz, w_ref[...])
        dy_ref[...] = (dy * silu).astype(dy_ref.dtype)
        dg_ref[...] = (dy * ycat_ * (sg * (1.0 + gg * (1.0 - sg)))).astype(dg_ref.dtype)

        @pl.when(i == steps - 1)
        def _():
            m_acc = dw_ref[...]
            vec_ref[2:3, :] = jnp.sum(w_ref[...].astype(F32) * m_acc, axis=0, keepdims=True)
            dw_ref[...] = m_acc * gate

    return _call(
        body, "even_out_bwd", (steps,),
        [_rows(tm, D_MODEL), _rows(tm, D_MODEL), _rows(tm, 1), _rows(tm, D_MODEL), _rows(tm, D_MODEL), _full((3, D_MODEL)),
         _full((1, D_MODEL)), _const((D_MODEL, D_MODEL))],
        [_rows(tm, D_MODEL)] * 3 + [_full((D_MODEL, D_MODEL)), _full((8, D_MODEL))],
        [_sds((seq, D_MODEL), ACT_DTYPE), _sds((seq, D_MODEL), ACT_DTYPE), _sds((seq, D_MODEL)), _sds((D_MODEL, D_MODEL)),
         _sds((8, D_MODEL))],
        (dx1, zhat, rstd, ycat, g, mod, ln_g, w_out), "arbitrary", rider=rider)


def _even_mix_bwd(q, k, v, lse, ycat, dycat, su, svo_s, vhat_s, rstd_s, sink, sgln_g, sgln_b, sgw, e2, e8, seq, rider=None):
    nb = seq // BLK

    def body(sink_ref, q_ref, k_ref, v_ref, lse_ref, y_ref, dy_ref, su_ref, svo_ref, vhat_ref, rstd_ref, lng_ref, lnb_ref, sgw_ref,
             e2_ref, e8_ref, dq_ref, dsu_ref, dsv_ref, dk_ref, dv_ref, dsgw_ref, dsgb_ref, vec_ref, dsink_ref, dsgb_acc):
        n = pl.program_id(0)

        @pl.when(n == 0)
        def _():
            dk_ref[...] = jnp.zeros_like(dk_ref)
            dv_ref[...] = jnp.zeros_like(dv_ref)
            dsgw_ref[...] = jnp.zeros_like(dsgw_ref)
            dsgb_acc[...] = jnp.zeros_like(dsgb_acc)
            vec_ref[...] = jnp.zeros_like(vec_ref)
            dsink_ref[...] = jnp.zeros_like(dsink_ref)

        kband = _band(k_ref, n, nb)
        vband = _band(v_ref, n, nb)
        bias = _band_bias(n, seq)
        lane = _lane_iota((BLK, LANES))
        row8 = lax.broadcasted_iota(jnp.int32, (8, LANES), 0)
        lse = lse_ref[...]
        dkb = jnp.zeros((LANES, 3 * BLK), F32)
        dvb = jnp.zeros((LANES, 3 * BLK), F32)
        dsink = jnp.zeros((8, LANES), F32)
        q_tile = lambda j: q_ref[:, j * LANES:(j + 1) * LANES].astype(F32)
        do_tile = lambda j: dy_ref[:, j * LANES:(j + 1) * LANES].astype(F32)
        dq = [jnp.zeros((BLK, LANES), F32) for _ in range(ATTN_WIDTH // LANES)]
        for kv in range(N_Q_HEADS // Q_PER_KV):
            heads = range(Q_PER_KV * kv, Q_PER_KV * (kv + 1))
            lse4, delta4 = [], []
            for h in heads:
                head_lanes = (lane < HEAD_DIM) if h % 2 == 0 else (lane >= HEAD_DIM)
                lse4.append(jnp.sum(jnp.where(lane == h, lse, 0.0), axis=1, keepdims=True))
                o_tile = y_ref[:, (h // 2) * LANES:(h // 2 + 1) * LANES].astype(F32)
                delta4.append(jnp.sum(jnp.where(head_lanes, do_tile(h // 2) * o_tile, 0.0), axis=1, keepdims=True))
            lse4, delta4 = jnp.concatenate(lse4, axis=0), jnp.concatenate(delta4, axis=0)
            q4, do4 = _stack_heads(q_tile, kv), _stack_heads(do_tile, kv)
            s = _mm_nt(q4, kband) * (HEAD_DIM ** -0.5) + bias
            p = jnp.exp(s - lse4)
            wsink = jnp.exp(_per_head_column([sink_ref[h] for h in heads]) - lse4) * delta4
            ds = p * (_mm_nt(do4, vband) - delta4) * (HEAD_DIM ** -0.5)
            dq4 = _mm(ds, kband)
            dkb = dkb + _mm_tn(q4, ds)
            dvb = dvb + _mm_tn(do4, p)
            for g, h in enumerate(heads):
                dq[h // 2] = dq[h // 2] + _from_kv_lanes(dq4[g * BLK:(g + 1) * BLK], h)
                dsink = dsink + jnp.where(row8 == h, -jnp.sum(wsink[g * BLK:(g + 1) * BLK]), 0.0)
        for j in range(ATTN_WIDTH // LANES):
            dq_ref[:, j * LANES:(j + 1) * LANES] = dq[j].astype(dq_ref.dtype)
        dsink_ref[...] += dsink
        prev = jnp.maximum(n - 1, 0)
        nxt = jnp.minimum(n + 1, nb - 1)
        for part, blk_i in enumerate((prev, n, nxt)):
            rows = pl.ds(pl.multiple_of(blk_i * BLK, BLK), BLK)
            dk_ref[rows, :] += dkb[:, part * BLK:(part + 1) * BLK].T
            dv_ref[rows, :] += dvb[:, part * BLK:(part + 1) * BLK].T

        e2 = e2_ref[...]
        lng, lnb = lng_ref[...], lnb_ref[...]
        for j in range(SG_WIDTH // LANES):
            cs = slice(j * LANES, (j + 1) * LANES)
            vhat = vhat_ref[:, cs].astype(F32)
            vn = vhat * lng[:, cs] + lnb[:, cs]
            dysg = dy_ref[:, ATTN_WIDTH + j * LANES:ATTN_WIDTH + (j + 1) * LANES].astype(F32)
            dsu_ref[:, cs] = (dysg * svo_ref[:, cs].astype(F32)).astype(dsu_ref.dtype)
            dsvo = dysg * su_ref[:, cs].astype(F32)
            dsgb_acc[:, cs] += dsvo
            d_lo = jnp.where(lane < HEAD_DIM, dsvo, 0.0)
            d_hi = dsvo - d_lo
            dsgw_ref[2 * j] += _mm_nt(d_lo, vn)
            dsgw_ref[2 * j + 1] += _mm_nt(d_hi, vn)
            dvn = _mm_tn(sgw_ref[2 * j], d_lo) + _mm_tn(sgw_ref[2 * j + 1], d_hi)
            vec_ref[0:1, cs] += jnp.sum(dvn * vhat, axis=0, keepdims=True)
            vec_ref[1:2, cs] += jnp.sum(dvn, axis=0, keepdims=True)
            dvh = dvn * lng[:, cs]
            m1 = _group_sum(dvh, e2) * (1.0 / HEAD_DIM)
            m2 = _group_sum(dvh * vhat, e2) * (1.0 / HEAD_DIM)
            dsv_ref[:, cs] = (rstd_ref[:, cs].astype(F32) * (dvh - m1 - vhat * m2)).astype(dsv_ref.dtype)

        @pl.when(n == nb - 1)
        def _():
            rest = dsgb_acc[...]
            total = jnp.zeros((8, BLK), F32)
            for _ in range(3):
                part = rest.astype(MXU_DTYPE)
                total = total + lax.dot_general(e8_ref[...], part, (((1,), (1,)), ((), ())), preferred_element_type=F32)
                rest = rest - part.astype(F32)
            dsgb_ref[...] = total

    blk = lambda w: pl.BlockSpec((BLK, w), lambda n: (n, 0))
    return _call(
        body, "even_mix_bwd", (nb,),
        [pl.BlockSpec(memory_space=pltpu.SMEM), blk(512), _full((seq, LANES)), _full((seq, LANES)), blk(LANES),
         blk(D_MODEL), blk(D_MODEL), blk(512), blk(512), blk(512), blk(512), _full((1, 512)), _full((1, 512)), _full((8, BLK, BLK)),
         _full((LANES, LANES)), _full((8, 512))],
        [blk(512), blk(512), blk(512), _full((seq, LANES)), _full((seq, LANES)), _full((8, BLK, BLK)),
         _full((8, BLK)), _full((8, 512)), _full((8, LANES))],
        [_sds((seq, 512), ACT_DTYPE), _sds((seq, 512), ACT_DTYPE), _sds((seq, 512), ACT_DTYPE), _sds((seq, LANES)), _sds((seq, LANES)),
         _sds((8, BLK, BLK)), _sds((8, BLK)), _sds((8, 512)), _sds((8, LANES))],
        (sink, q, k, v, lse, ycat, dycat, su, svo_s, vhat_s, rstd_s, sgln_g, sgln_b, sgw, e2, e8), "arbitrary",
        scratch=[pltpu.VMEM((BLK, 512), F32)], rider=rider)


def _even_proj_bwd(dq, dk, dv, dsu, dsv, dg, x, dres, mod, tabs, w_in_t, seq):
    tm = _row_tile(seq, 512)

    def body(dq_ref, dk_ref, dv_ref, dsu_ref, dsv_ref, dg_ref, x_ref, dres_ref, mod_ref, cos_ref, sp_ref, sm_ref, wt_ref,
             dx_ref, dw_ref, vec_ref, dpb_ref):
        @pl.when(pl.program_id(0) == 0)
        def _():
            vec_ref[...] = jnp.zeros_like(vec_ref)
            dw_ref[...] = jnp.zeros_like(dw_ref)

        cos_t, sin_p, sin_m = cos_ref[...], sp_ref[...], sm_ref[...]
        dt = dpb_ref.dtype
        for j in range(ATTN_WIDTH // LANES):
            cs = slice(j * LANES, (j + 1) * LANES)
            dpb_ref[:, cs] = _rope_t(dq_ref[:, cs].astype(F32), cos_t, sin_p, sin_m).astype(dt)
        dpb_ref[:, 512:640] = _rope_t(dk_ref[...], cos_t, sin_p, sin_m).astype(dt)
        dpb_ref[:, 640:768] = dv_ref[...].astype(dt)
        dpb_ref[:, 768:1280] = dsu_ref[...].astype(dt)
        dpb_ref[:, 1280:1792] = dsv_ref[...].astype(dt)
        dpb_ref[:, 1792:2816] = dg_ref[...].astype(dt)
        dpb = dpb_ref[...]
        dh = jnp.dot(dpb, wt_ref[...], preferred_element_type=F32)
        x_ = x_ref[...]
        hb = (x_ * (1.0 + mod_ref[1:2, :]) + mod_ref[0:1, :]).astype(MXU_DTYPE)
        dw_ref[...] += _mm_tn(dpb, hb)
        vec_ref[0:1, :] += jnp.sum(dh, axis=0, keepdims=True)
        vec_ref[1:2, :] += jnp.sum(dh * x_, axis=0, keepdims=True)
        dx_ref[...] = dres_ref[...] + dh * (1.0 + mod_ref[1:2, :])

    return pl.pallas_call(
        body, name="even_proj_bwd", grid=(seq // tm,),
        in_specs=[_rows(tm, 512), _rows(tm, LANES), _rows(tm, LANES), _rows(tm, 512), _rows(tm, 512), _rows(tm, D_MODEL),
                  _rows(tm, D_MODEL), _rows(tm, D_MODEL), _full((3, D_MODEL))] + [_rows(tm, LANES)] * 3
        + [_const((EVEN_IN, D_MODEL))],
        out_specs=[_rows(tm, D_MODEL), _const((EVEN_IN, D_MODEL)), _full((8, D_MODEL))],
        out_shape=[_sds((seq, D_MODEL)), _sds((EVEN_IN, D_MODEL)), _sds((8, D_MODEL))],
        scratch_shapes=[pltpu.VMEM((tm, EVEN_IN), MXU_DTYPE)],
        compiler_params=_params("arbitrary"),
    )(dq, dk, dv, dsu, dsv, dg, x, dres, mod, *tabs, w_in_t)


def _local_step(x, tabs, tgt, mod, w, seq, ride=None):
    rid = lambda make, *a: None if ride is None else make(*a)
    mxu = lambda a: a.astype(MXU_DTYPE)
    row = lambda a: a.reshape(1, -1)
    e2 = mxu(jnp.kron(jnp.eye(2, dtype=F32), jnp.ones((HEAD_DIM, HEAD_DIM), F32)))
    e8 = mxu(jnp.repeat(jnp.eye(N_SG_GROUPS, dtype=F32), HEAD_DIM, axis=1))
    sgw = mxu(w["ev_sg_w"])
    sgb_full = jnp.repeat(w["ev_sg_b"].T, HEAD_DIM, axis=1)
    sgln_g, sgln_b = row(w["ev_sg_ln_g"]), row(w["ev_sg_ln_b"])
    sink = w["ev_sink"].reshape(N_Q_HEADS)
    ev_w_in_t = mxu(w["ev_w_in_t"])
    if ride is None:
        ev_w_out, od_w_in, od_w_out = mxu(w["ev_w_out"]), mxu(w["od_w_in"]), mxu(w["od_w_out"])
    wcat = mxu(jnp.concatenate([w["od_w_a"][0], w["od_w_x"][0], w["od_w_a"][1], w["od_w_x"][1]], axis=2))
    gate_bias = jnp.stack([w["od_b_a"][0], w["od_b_x"][0], w["od_b_a"][1], w["od_b_x"][1]])
    conv_b = row(w["od_conv_b"])
    ln_g, ln_b = w["ln_g"], w["ln_b"]

    (q, k, v, su, sv, g0), got = _even_proj(x, mod[0], ev_w_in_t, tabs, seq, rid(_gather_rider, ride and ride["ev_w_out"]))
    if ride is not None:
        ev_w_out = got[0].reshape(D_MODEL, D_MODEL)
    (ycat, lse, *sg_saved), got = _even_mix(q, k, v, su, sv, sink, sgln_g, sgln_b, sgw, sgb_full, e2, seq,
                                 rid(_gather_rider, ride and ride["od_w_in"]))
    if ride is not None:
        od_w_in = got[0]
    (zhat0, rstd0, x1, xr, g1), got = _even_out(ycat, g0, x, mod[0], mod[1], ev_w_out, od_w_in, ln_g[0:1], ln_b[0:1], seq,
                                      rid(_gather_rider, ride and ride["od_w_out"]))
    if ride is not None:
        od_w_out = got[0].reshape(D_MODEL, D_MODEL)
    lru = (w["od_conv_w"], conv_b, wcat, gate_bias, w["od_lam"], seq)
    hf, hpf, *saved_f, xc = _lru_fwd(xr, None, *lru, 0)
    hr, hpr, *saved_r = _lru_fwd(xr, xc, *lru, 1)
    dhs, dg1, dres1, loss, d_od_w_out, vec_o = _odd_out_and_loss(hf, hr, g1, x1, tgt, mod[1], od_w_out, ln_g[1:2], ln_b[1:2], seq)
    dxc_f, dw_f, vec_f = _lru_bwd(xc, dhs, hpf, *saved_f, wcat, w["od_lam"], seq, 0)
    dxc_r, dw_r, vec_r = _lru_bwd(xc, dhs, hpr, *saved_r, wcat, w["od_lam"], seq, 1)
    dx1, d_od_w_in, vec_p = _odd_proj_bwd(dxc_f, dxc_r, xr, dg1, x1, dres1, mod[1], w["od_conv_w"], od_w_in, seq)
    d_od_w_a = jnp.stack([dw_f[:, :, 0:128], dw_r[:, :, 0:128]])
    d_od_w_x = jnp.stack([dw_f[:, :, 128:256], dw_r[:, :, 128:256]])
    od_parts = [d_od_w_in.reshape(4, 2, 512, 512), d_od_w_out.reshape(4, 2, 128, D_MODEL),
                d_od_w_a.reshape(4, 2, 2 * BLK, BLK), d_od_w_x.reshape(4, 2, 2 * BLK, BLK)]
    (dycat, dg0, dres0, d_ev_w_out, vec_e), got_od = _even_out_bwd(dx1, zhat0, rstd0, ycat, g0, mod[0], ln_g[0:1], ev_w_out, seq,
                                                                   rid(_sibling_swap_rider, od_parts))
    if ride is not None:
        od_sums = _sum_sibling(ride["core"], od_parts, got_od, [ride["wire"]] * 4, "sum_sibling_od")
    (dq, dsu, dsv, dk, dv, d_sgw, d_sgb, vec_s, d_sink), od_slots = _even_mix_bwd(
        q, k, v, lse, ycat, dycat, su, *sg_saved, sink, sgln_g, sgln_b, sgw, e2, e8, seq,
        rid(_chip_exchange_rider, ride and od_sums))
    grad_x, d_ev_w_in_t, vec_x = _even_proj_bwd(dq, dk, dv, dsu, dsv, dg0, x, dres0, mod[0], tabs, ev_w_in_t, seq)

    rows, dmod_blk = _pack_small(vec_x, vec_e, vec_p, vec_o, vec_f, vec_r, vec_s, d_sink, d_sgb, loss)
    grads = {"rows": rows, "dmod_blk": dmod_blk, "ev_w_in_t": d_ev_w_in_t, "ev_w_out": d_ev_w_out, "ev_sg_w": d_sgw}
    if ride is None:
        grads.update({"od_w_in": d_od_w_in, "od_w_out": d_od_w_out, "od_w_a": d_od_w_a, "od_w_x": d_od_w_x})
    else:
        grads["od_slots"] = od_slots
    return grad_x, grads


ROW_DMOD, ROW_LN, ROW_SG_LN, ROW_SG_B, ROW_CONV_W, ROW_CONV_B, ROW_B_A, ROW_B_X, ROW_LAM, ROW_SINK, ROW_LOSS = (
    0, 6, 10, 11, 12, 16, 17, 19, 21, 23, 24)
SMALL_ROWS = 64


def _pack_small(vec_x, vec_e, vec_p, vec_o, vec_f, vec_r, vec_s, d_sink, d_sgb, loss):
    def body(x_ref, e_ref, p_ref, o_ref, f_ref, r_ref, s_ref, sink_ref, sgb_ref, loss_ref, rows_ref, dmod_ref):
        rows_ref[...] = jnp.zeros_like(rows_ref)
        dmod_ref[...] = jnp.zeros_like(dmod_ref)
        put = [(ROW_DMOD, x_ref, 0), (ROW_DMOD + 1, x_ref, 1), (ROW_DMOD + 2, e_ref, 2), (ROW_DMOD + 3, p_ref, 5),
               (ROW_DMOD + 4, p_ref, 6), (ROW_DMOD + 5, o_ref, 2), (ROW_LN, e_ref, 0), (ROW_LN + 1, e_ref, 1),
               (ROW_LN + 2, o_ref, 0), (ROW_LN + 3, o_ref, 1), (ROW_CONV_B, p_ref, 4), (ROW_B_A, f_ref, 0),
               (ROW_B_A + 1, r_ref, 0), (ROW_B_X, f_ref, 1), (ROW_B_X + 1, r_ref, 1), (ROW_LAM, f_ref, 2), (ROW_LAM + 1, r_ref, 2)]
        put += [(ROW_CONV_W + k, p_ref, k) for k in range(4)]
        for dst, ref, src in put:
            rows_ref[dst:dst + 1, :] = ref[src:src + 1, :]
            if dst < 6:
                dmod_ref[dst:dst + 1, :] = ref[src:src + 1, :]
        rows_ref[ROW_SG_LN:ROW_SG_LN + 1, 0:SG_WIDTH] = s_ref[0:1, :]
        rows_ref[ROW_SG_LN:ROW_SG_LN + 1, SG_WIDTH:2 * SG_WIDTH] = s_ref[1:2, :]
        lane = _lane_iota((1, LANES))
        sink = jnp.zeros((1, LANES), F32)
        for h in range(N_Q_HEADS):
            rows_ref[ROW_SG_B:ROW_SG_B + 1, h * LANES:(h + 1) * LANES] = sgb_ref[h:h + 1, :]
            sink = jnp.where(lane == h, sink_ref[h:h + 1, :], sink)
        rows_ref[ROW_SINK:ROW_SINK + 1, 0:LANES] = sink
        rows_ref[ROW_LOSS:ROW_LOSS + 1, 0:LANES] = jnp.where(lane == 0, loss_ref[0:1, :], 0.0)

    return pl.pallas_call(body, name="pack_small", out_shape=[_sds((SMALL_ROWS, D_MODEL)), _sds((8, D_MODEL))])(
        vec_x, vec_e, vec_p, vec_o, vec_f, vec_r, vec_s, d_sink, d_sgb, loss)


def _allgather8(block, name):
    m_per, n = block.shape

    def body(x_ref, out_ref, send_sems, recv_sems, local_sem):
        x, y, c = _place()
        me, sibling = (x, y, c), (x, y, 1 - c)
        chips = [(1 - x, y), (x, 1 - y), (1 - x, 1 - y)]

        def rows(px, py, pc):
            return out_ref.at[pl.ds((4 * px + 2 * py + pc) * m_per, m_per), :]

        def copy(k, blk, to, src=None):
            return pltpu.make_async_remote_copy(src_ref=rows(*blk) if src is None else src, dst_ref=rows(*blk),
                                                send_sem=send_sems.at[k], recv_sem=recv_sems.at[k], device_id=to,
                                                device_id_type=MESH)

        mine = pltpu.make_async_copy(x_ref, rows(*me), local_sem)
        mine.start()
        first = [copy(0, me, sibling, src=x_ref)] + [copy(1 + j, me, (*chip, c), src=x_ref) for j, chip in enumerate(chips)]
        for cp in first:
            cp.start()
        passed = [copy(4 + j, (*chip, c), sibling) for j, chip in enumerate(chips)]
        for j, chip in enumerate(chips):
            copy(1 + j, (*chip, c), me).wait_recv()
            passed[j].start()
        copy(0, sibling, me).wait_recv()
        for j, chip in enumerate(chips):
            copy(4 + j, (*chip, 1 - c), me).wait_recv()
        for cp in first + passed:
            cp.wait_send()
        mine.wait()

    return pl.pallas_call(
        body, name=name, out_shape=_sds((8 * m_per, n), block.dtype),
        in_specs=[pl.BlockSpec(memory_space=pltpu.VMEM)], out_specs=pl.BlockSpec(memory_space=pltpu.VMEM),
        scratch_shapes=[pltpu.SemaphoreType.DMA((7,)), pltpu.SemaphoreType.DMA((7,)), pltpu.SemaphoreType.DMA],
        compiler_params=pltpu.CompilerParams(vmem_limit_bytes=VMEM_LIMIT),
    )(block)


class _Copies:
    def __init__(self, send_sems, recv_sems, local_sems, stages):
        self.send_sems, self.recv_sems, self.local_sems, self.stages = send_sems, recv_sems, local_sems, stages
        self.sent, self.staged, self.locals = [], [], []

    def remote(self, k, src, dst, to):
        return pltpu.make_async_remote_copy(src_ref=src, dst_ref=dst, send_sem=self.send_sems.at[k], recv_sem=self.recv_sems.at[k],
                                            device_id=to, device_id_type=MESH)

    def send(self, k, src, dst, to):
        cp = self.remote(k, src, dst, to)
        cp.start()
        self.sent.append(cp)

    def arrived(self, k, dst, frm):
        self.remote(k, dst, dst, frm).wait_recv()

    def local(self, src, dst):
        k = len(self.staged)
        cp = pltpu.make_async_copy(src, self.stages[k], self.local_sems.at[2 * k])
        cp.start()
        self.staged.append((cp, dst))

    def flush(self):
        for k in range(len(self.locals), len(self.staged)):
            cp, dst = self.staged[k]
            cp.wait()
            out = pltpu.make_async_copy(self.stages[k], dst, self.local_sems.at[2 * k + 1])
            out.start()
            self.locals.append(out)

    def drain(self):
        self.flush()
        for cp in self.sent:
            cp.wait_send()
        for cp in self.locals:
            cp.wait()


def _comm_call(body, name, ins, out_shapes, n_remote, stages, side=None):
    n_in, n_out = len(ins), len(out_shapes)
    side_fn, side_ins, side_outs = side if side is not None else (None, (), [])
    s_in, s_out = len(side_ins), len(side_outs)

    def kern(*refs):
        in_refs, refs = refs[:n_in], refs[n_in:]
        side_in_refs, refs = refs[:s_in], refs[s_in:]
        out_refs, refs = refs[:n_out], refs[n_out:]
        side_out_refs, refs = refs[:s_out], refs[s_out:]
        if side is None:
            body(_Copies(refs[0], refs[1], refs[2], refs[3:]), in_refs, out_refs)
            return
        side_bufs, side_sems, refs = refs[:s_out], refs[s_out], refs[s_out + 1:]
        cps = _Copies(refs[0], refs[1], refs[2], refs[3:])
        leave =[pltpu.make_async_copy(side_bufs[k], side_out_refs[k], side_sems.at[k]) for k in range(s_out)]

        def run_side():
            side_fn(*side_in_refs, *side_bufs)
            for cp in leave:
                cp.start()

        body(cps, in_refs, out_refs, run_side)
        for cp in leave:
            cp.wait()

    hbm, vmem = pl.BlockSpec(memory_space=pl.ANY), pl.BlockSpec(memory_space=pltpu.VMEM)
    side_scratch = [] if side is None else [pltpu.VMEM(o.shape, o.dtype) for o in side_outs] + [pltpu.SemaphoreType.DMA((s_out,))]
    return pl.pallas_call(
        kern, name=name, out_shape=list(out_shapes) + list(side_outs), in_specs=[hbm] * n_in + [vmem] * s_in,
        out_specs=[hbm] * (n_out + s_out),
        scratch_shapes=side_scratch + [pltpu.SemaphoreType.DMA((n_remote,)), pltpu.SemaphoreType.DMA((n_remote,)),
                                       pltpu.SemaphoreType.DMA((2 * len(stages),))] + [pltpu.VMEM(s, d) for s, d in stages],
        compiler_params=pltpu.CompilerParams(vmem_limit_bytes=VMEM_LIMIT),
    )(*ins, *side_ins)


def _gather_to_all(cps, pairs, me, sibling, other_chips, c, base, meanwhile=None):
    idx = lambda p: 4 * p[0] + 2 * p[1] + p[2]
    for i, (src, dst) in enumerate(pairs):
        cps.local(src, dst.at[idx(me)])
        cps.send(base + 7 * i, src, dst.at[idx(me)], sibling)
        for j, chip in enumerate(other_chips):
            cps.send(base + 7 * i + 1 + j, src, dst.at[idx(me)], (*chip, c))
    cps.flush()
    if meanwhile is not None:
        meanwhile()
    for j, chip in enumerate(other_chips):
        for i, (_, dst) in enumerate(pairs):
            got = dst.at[idx((*chip, c))]
            cps.arrived(base + 7 * i + 1 + j, got, (*chip, c))
            cps.send(base + 7 * i + 4 + j, got, got, sibling)
    for i, (_, dst) in enumerate(pairs):
        cps.arrived(base + 7 * i, dst.at[idx(sibling)], sibling)
        for j, chip in enumerate(other_chips):
            cps.arrived(base + 7 * i + 4 + j, dst.at[idx((*chip, 1 - c))], sibling)


def _gather_weights(shards, small, side):
    n = len(shards)

    def body(cps, ins, outs, run_side):
        x, y, c = _place()
        me, sibling, mine = (x, y, c), (x, y, 1 - c), 2 * x + y
        chips = [(1 - x, y), (x, 1 - y), (1 - x, 1 - y)]
        for i in range(n):
            cps.local(ins[i], outs[i].at[mine])
        for j, (px, py) in enumerate(chips):
            for i in range(n):
                hr = shards[i].shape[0] // 2
                rows = pl.ds(c * hr, hr)
                cps.send(6 * i + j, ins[i].at[rows], outs[i].at[mine, rows], (px, py, c))
        _gather_to_all(cps, [(ins[n], outs[n])], me, sibling, chips, c, 6 * n, meanwhile=run_side)
        for j, (px, py) in enumerate(chips):
            for i in range(n):
                hr = shards[i].shape[0] // 2
                got = outs[i].at[2 * px + py, pl.ds(c * hr, hr)]
                cps.arrived(6 * i + j, got, (px, py, c))
                cps.send(6 * i + 3 + j, got, got, sibling)
        for j, (px, py) in enumerate(chips):
            for i in range(n):
                hr = shards[i].shape[0] // 2
                cps.arrived(6 * i + 3 + j, outs[i].at[2 * px + py, pl.ds((1 - c) * hr, hr)], sibling)
        cps.drain()

    return _comm_call(body, "gather_weights", list(shards) + [small],
                      [_sds((4,) + s.shape, s.dtype) for s in shards] + [_sds((8,) + small.shape, small.dtype)], 6 * n + 7,
                      [(a.shape, a.dtype) for a in list(shards) + [small]], side)


def _reduce_sibling(parts, dmod_rows):
    n = len(parts)

    def body(cps, ins, outs):
        x, y, c = _place()
        me, sibling = (x, y, c), (x, y, 1 - c)
        chips = [(1 - x, y), (x, 1 - y), (1 - x, 1 - y)]
        for i in range(n):
            cps.send(i, ins[i].at[:, 1 - c], outs[i], sibling)
        _gather_to_all(cps, [(ins[n], outs[n])], me, sibling, chips, c, n)
        for i in range(n):
            cps.arrived(i, outs[i], sibling)
        cps.drain()

    return _comm_call(body, "reduce_sibling", list(parts) + [dmod_rows],
                      [_sds((4,) + p.shape[2:], p.dtype) for p in parts] + [_sds((8,) + dmod_rows.shape, dmod_rows.dtype)], n + 7,
                      [(dmod_rows.shape, dmod_rows.dtype)])


def _reduce_chips(parts):
    n = len(parts)

    def body(cps, ins, outs):
        x, y, c = _place()
        mine = 2 * x + y
        chips = _other_chips(x, y)
        for i in range(n):
            cps.local(ins[i].at[mine], outs[i].at[mine])
        for j, (px, py) in enumerate(chips):
            for i in range(n):
                cps.send(3 * i + j, ins[i].at[2 * px + py], outs[i].at[mine], (px, py, c))
        cps.flush()
        for j, (px, py) in enumerate(chips):
            for i in range(n):
                cps.arrived(3 * i + j, outs[i].at[2 * px + py], (px, py, c))
        cps.drain()

    return _comm_call(body, "reduce_chips", list(parts), [_sds(p.shape, p.dtype) for p in parts], 3 * n,
                      [(p.shape[1:], p.dtype) for p in parts])


def _gather_reduced(shard_parts, repl_parts):
    ns, nr = len(shard_parts), len(repl_parts)

    def body(cps, ins, outs):
        x, y, c = _place()
        me, sibling = (x, y, c), (x, y, 1 - c)
        chips = [(1 - x, y), (x, 1 - y), (1 - x, 1 - y)]
        for i in range(ns):
            cps.local(ins[i], outs[i].at[c])
            cps.send(i, ins[i], outs[i].at[c], sibling)
        _gather_to_all(cps, [(ins[ns + i], outs[ns + i]) for i in range(nr)], me, sibling, chips, c, ns)
        for i in range(ns):
            cps.arrived(i, outs[i].at[1 - c], sibling)
        cps.drain()

    return _comm_call(body, "gather_reduced", list(shard_parts) + list(repl_parts),
                      [_sds((2,) + p.shape, p.dtype) for p in shard_parts] + [_sds((8,) + p.shape, p.dtype) for p in repl_parts],
                      ns + 7 * nr, [(p.shape, p.dtype) for p in list(shard_parts) + list(repl_parts)])


def _sum_sibling(core, parts, got, wire, name):
    n = len(parts)

    def body(core_ref, *refs):
        for i in range(n):
            refs[2 * n + i][0] = (refs[i][0] + refs[n + i][0]).astype(wire[i])

    keep_spec = lambda p: pl.BlockSpec((1, None) + p.shape[2:], lambda s, core_ref: (s, core_ref[0], 0, 0))
    slot_spec = lambda p: pl.BlockSpec((1,) + p.shape[2:], lambda s, core_ref: (s, 0, 0))
    return pl.pallas_call(
        body, name=name,
        grid_spec=pltpu.PrefetchScalarGridSpec(
            num_scalar_prefetch=1, grid=(4,), in_specs=[keep_spec(p) for p in parts] + [slot_spec(p) for p in parts],
            out_specs=[slot_spec(p) for p in parts]),
        out_shape=[_sds((4,) + p.shape[2:], wire[i]) for i, p in enumerate(parts)],
        compiler_params=_params("parallel"),
    )(core, *parts, *got)


def _sum_slots(slots, name):
    n = len(slots)

    def spec_pair(p):
        k, rows, cols = p.shape
        sub = 16 if p.dtype == BF16 else 8
        if (rows // 2) % sub == 0:
            return pl.BlockSpec((k, rows // 2, cols), lambda i: (0, i, 0)), pl.BlockSpec((rows // 2, cols), lambda i: (i, 0))
        return pl.BlockSpec((k, rows, cols), lambda i: (0, 0, 0)), pl.BlockSpec((rows, cols), lambda i: (0, 0))

    pairs = [spec_pair(p) for p in slots]

    def body(*refs):
        for i in range(n):
            acc = refs[i][0].astype(F32)
            for j in range(1, slots[i].shape[0]):
                acc = acc + refs[i][j].astype(F32)
            refs[n + i][...] = acc

    return pl.pallas_call(
        body, name=name, grid=(2,), in_specs=[a for a, _ in pairs], out_specs=[b for _, b in pairs],
        out_shape=[_sds(p.shape[1:]) for p in slots], compiler_params=_params("arbitrary"),
    )(*slots)


def _modulation(c_all, ada_w, ada_b):
    cols = ada_w.shape[2]

    def body(c_ref, w_ref, b_ref, o_ref):
        cc = c_ref[...]
        o_ref[0] = _mm(cc * _sigmoid(cc), w_ref[0]) + b_ref[0]

    return pl.pallas_call(
        body, name="modulation", grid=(2,),
        in_specs=[_full((8, D_MODEL)), pl.BlockSpec((1, D_MODEL, cols), lambda l: (l, 0, 0)), pl.BlockSpec((1, 1, cols), lambda l: (l, 0, 0))],
        out_specs=pl.BlockSpec((1, 8, cols), lambda l: (l, 0, 0)), out_shape=_sds((2, 8, cols)),
        compiler_params=_params("parallel"),
    )(c_all, ada_w, ada_b)


def _adamw_math(w, g, m, v):
    m = ADAM_B1 * m + (1.0 - ADAM_B1) * g
    v = ADAM_B2 * v + (1.0 - ADAM_B2) * (g * g)
    m_hat = m / (1.0 - ADAM_B1 ** ADAM_STEP)
    v_hat = v / (1.0 - ADAM_B2 ** ADAM_STEP)
    delta = -ADAM_LR * (m_hat / (jnp.sqrt(v_hat) + ADAM_EPS) + ADAM_WD * w)
    return delta, m, v


def _ada_update(c_all, dmod, w, m, v, rider=None):
    cols = w.shape[2]
    tr = 256
    per = D_MODEL // tr
    spec3 = pl.BlockSpec((1, tr, cols), lambda i: (i // per, i % per, 0))

    def body(c_ref, d_ref, w_ref, m_ref, v_ref, g_ref, dl_ref, nm_ref, nv_ref):
        cc = c_ref[...]
        g = _mm_tn(cc * _sigmoid(cc), d_ref[0])
        g_ref[0] = g
        dl_ref[0], nm_ref[0], nv_ref[0] = _adamw_math(w_ref[0], g, m_ref[0], v_ref[0])

    return _call(
        body, "ada_update", (2 * per,),
        [pl.BlockSpec((8, tr), lambda i: (0, i % per)), pl.BlockSpec((1, 8, cols), lambda i: (i // per, 0, 0)), spec3, spec3, spec3],
        [spec3] * 4, [_sds(w.shape)] * 4, (c_all, dmod, w, m, v), "parallel", rider=rider)


def _adamw_matrices(params):
    n = len(params)
    steps = 8

    def body(*refs):
        ins, outs = refs[:4 * n], refs[4 * n:]
        for j in range(n):
            w_ref, g_ref, m_ref, v_ref = ins[4 * j:4 * j + 4]
            g = g_ref[...]
            outs[4 * j][...] = g
            outs[4 * j + 1][...], outs[4 * j + 2][...], outs[4 * j + 3][...] = _adamw_math(w_ref[...], g, m_ref[...], v_ref[...])

    spec = lambda p: _rows(p[0].shape[0] // steps, p[0].shape[1])
    res = pl.pallas_call(
        body, name="adamw_matrices", grid=(steps,), in_specs=[spec(p) for p in params for _ in range(4)],
        out_specs=[spec(p) for p in params for _ in range(4)], out_shape=[_sds(p[0].shape) for p in params for _ in range(4)],
        compiler_params=_params("parallel"),
    )(*[a for p in params for a in p])
    return [tuple(res[4 * j:4 * j + 4]) for j in range(n)]


def _adamw_small(params):
    n = len(params)

    def body(*refs):
        ins, outs = refs[:4 * n], refs[4 * n:]
        for j in range(n):
            w_ref, g_ref, m_ref, v_ref = ins[4 * j:4 * j + 4]
            outs[3 * j][...], outs[3 * j + 1][...], outs[3 * j + 2][...] = _adamw_math(w_ref[...], g_ref[...], m_ref[...], v_ref[...])

    flat = [a for p in params for a in p]
    res = pl.pallas_call(body, name="adamw_small", out_shape=[_sds(p[0].shape) for p in params for _ in range(3)])(*flat)
    return [tuple(res[3 * j:3 * j + 3]) for j in range(n)]


def _cols(a, start, size):
    return lax.dynamic_slice_in_dim(a, start, size, axis=a.ndim - 1)


def kernel(x, c, positions, ada_w, ada_b, ln_g, ln_b, ev_w_in, ev_w_out, ev_sink, ev_sg_ln_g, ev_sg_ln_b, ev_sg_w, ev_sg_b, od_w_in, od_conv_w, od_conv_b, od_w_a, od_b_a, od_w_x, od_b_x, od_lam, od_w_out, loss_target, m_ada_w, m_ada_b, m_ln_g, m_ln_b, m_ev_w_in, m_ev_w_out, m_ev_sink, m_ev_sg_ln_g, m_ev_sg_ln_b, m_ev_sg_w, m_ev_sg_b, m_od_w_in, m_od_conv_w, m_od_conv_b, m_od_w_a, m_od_b_a, m_od_w_x, m_od_b_x, m_od_lam, m_od_w_out, v_ada_w, v_ada_b, v_ln_g, v_ln_b, v_ev_w_in, v_ev_w_out, v_ev_sink, v_ev_sg_ln_g, v_ev_sg_ln_b, v_ev_sg_w, v_ev_sg_b, v_od_w_in, v_od_conv_w, v_od_conv_b, v_od_w_a, v_od_b_a, v_od_w_x, v_od_b_x, v_od_lam, v_od_w_out):
    seq = x.shape[1]
    px, py, pc = _place()
    chip = 2 * px + py
    dev = 2 * chip + pc

    small = jnp.concatenate([od_conv_w[0].reshape(-1), od_conv_b[0], od_b_a[0].reshape(-1), jnp.zeros((256,), F32),
                             od_b_x[0].reshape(-1), od_lam[0].reshape(-1)]).reshape(3, D_MODEL)
    blk = jnp.concatenate([c, small, jnp.zeros((4, D_MODEL), F32)], axis=0)
    tr = lambda a: jnp.swapaxes(a, -1, -2)
    wire_w = lambda a: a.astype(MXU_DTYPE)
    posf = positions.astype(F32).reshape(seq, 1)
    ev_w_in4, g_small, *tabs = _gather_weights([wire_w(tr(ev_w_in[0]))], blk, _rope_tables(posf, seq))
    core = pc.astype(jnp.int32).reshape(1)
    ride = {"ev_w_out": wire_w(ev_w_out[0]), "od_w_in": wire_w(od_w_in[0]), "od_w_out": wire_w(od_w_out[0]),
            "core": core, "wire": MXU_DTYPE}
    c_all = g_small[:, 0, :]
    per_chip = g_small[0::2]
    conv_w = per_chip[:, 1].reshape(4, 4, 256).transpose(1, 0, 2).reshape(4, D_MODEL)
    conv_b = per_chip[:, 2, 0:256].reshape(D_MODEL)
    b_a = per_chip[:, 2, 256:768].reshape(4, 2, 256).transpose(1, 0, 2).reshape(2, D_MODEL)
    b_x = per_chip[:, 3, 0:512].reshape(4, 2, 256).transpose(1, 0, 2).reshape(2, D_MODEL)
    lam = per_chip[:, 3, 512:1024].reshape(4, 2, 256).transpose(1, 0, 2).reshape(2, D_MODEL)

    w_full = {
        "ev_w_in_t": ev_w_in4.reshape(EVEN_IN, D_MODEL),
        "ev_sink": ev_sink[0], "ev_sg_ln_g": ev_sg_ln_g[0], "ev_sg_ln_b": ev_sg_ln_b[0], "ev_sg_w": ev_sg_w[0],
        "ev_sg_b": ev_sg_b[0], "od_conv_w": conv_w, "od_conv_b": conv_b, "od_w_a": od_w_a[0], "od_b_a": b_a,
        "od_w_x": od_w_x[0], "od_b_x": b_x, "od_lam": lam, "ln_g": ln_g, "ln_b": ln_b,
    }

    ada_cols = ada_w.shape[2]
    mod_sh = _modulation(c_all, ada_w, _cols(ada_b, chip * ada_cols, ada_cols).reshape(2, 1, ada_cols))
    mod_all = _allgather8(mod_sh.reshape(16, ada_cols), "gather_mod").reshape(4, 2, 2, 8, ada_cols)[:, 0]
    mod_mine = lax.dynamic_index_in_dim(mod_all, dev, axis=2, keepdims=False)
    mod = mod_mine.transpose(1, 0, 2).reshape(2, 3, D_MODEL)

    grad_x, g = _local_step(x[0], tabs, loss_target[0], mod, w_full, seq, ride)

    parts = [g["ev_w_in_t"].reshape(4, 2, 352, D_MODEL), g["ev_w_out"].reshape(4, 2, 128, D_MODEL),
             g["ev_sg_w"].reshape(4, 2, BLK, BLK), g["rows"].reshape(4, 2, SMALL_ROWS // 8, D_MODEL)]
    wire = [MXU_DTYPE] * 3 + [F32]
    *got, dmod_gathered = _reduce_sibling(parts, g["dmod_blk"])
    ev_slots = list(_reduce_chips(_sum_sibling(core, parts, got, wire, "sum_sibling")))
    od_slots = list(g["od_slots"])
    mine = _sum_slots(ev_slots[0:2] + od_slots[0:2] + ev_slots[2:3] + od_slots[2:4] + ev_slots[3:4], "sum_chips")
    reduced = _gather_reduced(mine[:4], mine[4:])
    g_ev_w_in_t = reduced[0].reshape(704, D_MODEL)
    g_ev_w_out = reduced[1].reshape(256, D_MODEL)
    g_od_w_in = reduced[2].reshape(D_MODEL, 512)
    g_od_w_out = reduced[3].reshape(256, D_MODEL)
    g_sg_w = reduced[4].reshape(8 * BLK, BLK)
    g_w_a = reduced[5].reshape(16 * BLK, BLK)
    g_w_x = reduced[6].reshape(16 * BLK, BLK)
    gs = reduced[7].reshape(SMALL_ROWS, D_MODEL)
    loss = gs[ROW_LOSS, 0]
    dmod_all = dmod_gathered[:, 0:6].reshape(8, 2, 3 * D_MODEL)
    dmod_sh = _cols(dmod_all, chip * ada_cols, ada_cols).transpose(1, 0, 2)
    (g_ada_w, d_ada_w, nm_ada_w, nv_ada_w), _ = _ada_update(c_all, dmod_sh, ada_w, m_ada_w, v_ada_w)

    mats = (("ev_w_out", ev_w_out, g_ev_w_out, m_ev_w_out, v_ev_w_out), ("od_w_in", od_w_in, g_od_w_in, m_od_w_in, v_od_w_in),
            ("od_w_out", od_w_out, g_od_w_out, m_od_w_out, v_od_w_out), ("ev_sg_w", ev_sg_w, g_sg_w, m_ev_sg_w, v_ev_sg_w),
            ("od_w_a", od_w_a, g_w_a, m_od_w_a, v_od_w_a), ("od_w_x", od_w_x, g_w_x, m_od_w_x, v_od_w_x))
    upd = _adamw_matrices([(tr(ev_w_in[0]), g_ev_w_in_t, tr(m_ev_w_in[0]), tr(v_ev_w_in[0]))]
                          + [(w_.reshape(g_.shape), g_, m_.reshape(g_.shape), v_.reshape(g_.shape)) for _, w_, g_, m_, v_ in mats])
    big = {"ev_w_in": tuple(tr(a).reshape(ev_w_in.shape) for a in upd[0])}
    for (name, w_, _, _, _), u in zip(mats, upd[1:]):
        big[name] = tuple(a.reshape(w_.shape) for a in u)
    big["ada_w"] = (g_ada_w, d_ada_w, nm_ada_w, nv_ada_w)

    sh = lambda a: _cols(a, chip * 256, 256)
    small_g = {
        "ada_b": gs[ROW_DMOD:ROW_DMOD + 6].reshape(2, 3 * D_MODEL),
        "ln_g": jnp.stack([gs[ROW_LN], gs[ROW_LN + 2]]), "ln_b": jnp.stack([gs[ROW_LN + 1], gs[ROW_LN + 3]]),
        "ev_sink": gs[ROW_SINK:ROW_SINK + 1, 0:N_Q_HEADS], "ev_sg_ln_g": gs[ROW_SG_LN:ROW_SG_LN + 1, 0:SG_WIDTH],
        "ev_sg_ln_b": gs[ROW_SG_LN:ROW_SG_LN + 1, SG_WIDTH:2 * SG_WIDTH], "ev_sg_b": gs[ROW_SG_B].reshape(N_SG_GROUPS, BLK),
        "od_conv_w": sh(gs[ROW_CONV_W:ROW_CONV_W + 4]), "od_conv_b": sh(gs[ROW_CONV_B:ROW_CONV_B + 1]),
        "od_b_a": sh(gs[ROW_B_A:ROW_B_A + 2]), "od_b_x": sh(gs[ROW_B_X:ROW_B_X + 2]), "od_lam": sh(gs[ROW_LAM:ROW_LAM + 2]),
    }
    small_in = {"ada_b": (ada_b, m_ada_b, v_ada_b), "ln_g": (ln_g, m_ln_g, v_ln_g), "ln_b": (ln_b, m_ln_b, v_ln_b),
                "ev_sink": (ev_sink, m_ev_sink, v_ev_sink), "ev_sg_ln_g": (ev_sg_ln_g, m_ev_sg_ln_g, v_ev_sg_ln_g),
                "ev_sg_ln_b": (ev_sg_ln_b, m_ev_sg_ln_b, v_ev_sg_ln_b), "ev_sg_b": (ev_sg_b, m_ev_sg_b, v_ev_sg_b),
                "od_conv_w": (od_conv_w, m_od_conv_w, v_od_conv_w), "od_conv_b": (od_conv_b, m_od_conv_b, v_od_conv_b),
                "od_b_a": (od_b_a, m_od_b_a, v_od_b_a), "od_b_x": (od_b_x, m_od_b_x, v_od_b_x),
                "od_lam": (od_lam, m_od_lam, v_od_lam)}
    names_small = list(small_g)
    upd = _adamw_small([(small_in[n][0].reshape(small_g[n].shape), small_g[n], small_in[n][1].reshape(small_g[n].shape),
                         small_in[n][2].reshape(small_g[n].shape)) for n in names_small])
    res = dict(big)
    for n, (d_, nm_, nv_) in zip(names_small, upd):
        shape = small_in[n][0].shape
        res[n] = tuple(a.reshape(shape) for a in (small_g[n], d_, nm_, nv_))

    order = ["ada_w", "ada_b", "ln_g", "ln_b", "ev_w_in", "ev_w_out", "ev_sink", "ev_sg_ln_g", "ev_sg_ln_b", "ev_sg_w", "ev_sg_b",
             "od_w_in", "od_conv_w", "od_conv_b", "od_w_a", "od_b_a", "od_w_x", "od_b_x", "od_lam", "od_w_out"]
    return (loss, grad_x.reshape(x.shape), *[res[n][0] for n in order], *[res[n][1] for n in order],
            *[res[n][2] for n in order], *[res[n][3] for n in order])
```

```python
from functools import partial

import jax
import jax.numpy as jnp
import numpy as np
from jax import lax
from jax.experimental import pallas as pl
from jax.experimental.pallas import tpu as pltpu

F32 = jnp.float32
BF16 = jnp.bfloat16
MXU_DTYPE = BF16
ACT_DTYPE = MXU_DTYPE

D_MODEL = 1024
HEAD_DIM = 64
N_Q_HEADS = 8
Q_PER_KV = 4
ATTN_WIDTH = 512
BLK = 128
ROPE_DIM = 16
ROPE_THETA = 500000.0
N_SG_GROUPS = 8
SG_WIDTH = 512
EVEN_IN = 2816
ODD_IN = 2048
RNN_HEADS = 8
RG_LRU_C = 8.0
ALPHA = (2 * 2) ** 0.25
LN_EPS = 1e-5
NEG_INF = -1e30
ADAM_LR, ADAM_B1, ADAM_B2, ADAM_EPS, ADAM_WD, ADAM_STEP = 0.001, 0.9, 0.999, 1e-08, 0.01, 10

LANES = 128
VMEM_LIMIT = 56 * 1024 * 1024
MESH = pl.DeviceIdType.MESH


def _mm(a, b):
    return jnp.dot(a.astype(MXU_DTYPE), b.astype(MXU_DTYPE), preferred_element_type=F32)


def _mm_nt(a, b):
    return lax.dot_general(a.astype(MXU_DTYPE), b.astype(MXU_DTYPE), (((1,), (1,)), ((), ())), preferred_element_type=F32)


def _mm_tn(a, b):
    return lax.dot_general(a.astype(MXU_DTYPE), b.astype(MXU_DTYPE), (((0,), (0,)), ((), ())), preferred_element_type=F32)


def _sigmoid(x):
    return 1.0 / (1.0 + jnp.exp(-x))


def _ln_stats(z):
    mu = jnp.mean(z, axis=-1, keepdims=True)
    d = z - mu
    var = jnp.mean(d * d, axis=-1, keepdims=True)
    rstd = lax.rsqrt(var + LN_EPS)
    return d * rstd, rstd


def _ln_bwd(dout, zhat, rstd, g):
    dzh = dout * g
    m1 = jnp.mean(dzh, axis=-1, keepdims=True)
    m2 = jnp.mean(dzh * zhat, axis=-1, keepdims=True)
    return rstd * (dzh - m1 - zhat * m2)


def _group_sum(x, e2):
    hi = x.astype(MXU_DTYPE)
    lo = (x - hi.astype(F32)).astype(MXU_DTYPE)
    return jnp.dot(hi, e2, preferred_element_type=F32) + jnp.dot(lo, e2, preferred_element_type=F32)


def _lane_iota(shape):
    return lax.broadcasted_iota(jnp.int32, shape, 1)


def _to_kv_lanes(t, h):
    src_lo = (h % 2 == 0)
    dst_lo = (h // Q_PER_KV == 0)
    if src_lo != dst_lo:
        t = pltpu.roll(t, HEAD_DIM, 1)
    lane = _lane_iota(t.shape)
    keep = (lane < HEAD_DIM) if dst_lo else (lane >= HEAD_DIM)
    return jnp.where(keep, t, 0.0)


def _from_kv_lanes(t, h):
    src_lo = (h // Q_PER_KV == 0)
    dst_lo = (h % 2 == 0)
    lane = _lane_iota(t.shape)
    keep = (lane < HEAD_DIM) if src_lo else (lane >= HEAD_DIM)
    t = jnp.where(keep, t, 0.0)
    if src_lo != dst_lo:
        t = pltpu.roll(t, HEAD_DIM, 1)
    return t


def _rope(t, cos_t, sin_p, sin_m):
    half = ROPE_DIM // 2
    return t * cos_t + pltpu.roll(t, half, 1) * sin_p + pltpu.roll(t, LANES - half, 1) * sin_m


def _rope_t(d, cos_t, sin_p, sin_m):
    half = ROPE_DIM // 2
    return d * cos_t + pltpu.roll(d * sin_p, LANES - half, 1) + pltpu.roll(d * sin_m, half, 1)


def _band(ref, n, nb):
    prev = jnp.maximum(n - 1, 0)
    nxt = jnp.minimum(n + 1, nb - 1)
    rows = [ref[pl.ds(pl.multiple_of(j * BLK, BLK), BLK), :] for j in (prev, n, nxt)]
    return jnp.concatenate(rows, axis=0)


def _band_bias(n, seq):
    qi = lax.broadcasted_iota(jnp.int32, (BLK, 3 * BLK), 0)
    kj = lax.broadcasted_iota(jnp.int32, (BLK, 3 * BLK), 1)
    k_abs = n * BLK - BLK + kj
    valid = (jnp.abs(kj - BLK - qi) <= BLK) & (k_abs >= 0) & (k_abs < seq)
    bias = jnp.where(valid, 0.0, NEG_INF)
    return jnp.concatenate([bias] * Q_PER_KV, axis=0)


def _stack_heads(tile_of, kv):
    return jnp.concatenate([_to_kv_lanes(tile_of(h // 2), h) for h in range(Q_PER_KV * kv, Q_PER_KV * (kv + 1))], axis=0)


def _per_head_column(vals):
    row = lax.broadcasted_iota(jnp.int32, (Q_PER_KV * BLK, 1), 0)
    return jnp.where(row < BLK, vals[0], jnp.where(row < 2 * BLK, vals[1], jnp.where(row < 3 * BLK, vals[2], vals[3])))


def _softplus_neg(lam):
    e = jnp.exp(-jnp.abs(lam))
    u = 1.0 + e
    log1p_e = jnp.where(u == 1.0, e, jnp.log(u) * (e / (u - 1.0)))
    sp = jnp.maximum(-lam, 0.0) + log1p_e
    dsp = -1.0 / (1.0 + jnp.exp(lam))
    return sp, dsp


def _full(shape):
    return pl.BlockSpec(shape, lambda *_: (0,) * len(shape))


def _const(shape):
    return pl.BlockSpec(shape, lambda *_: (0,) * len(shape), pipeline_mode=pl.Buffered(1))


def _rows(tm, n):
    return pl.BlockSpec((tm, n), lambda i: (i, 0))


def _params(*sem):
    return pltpu.CompilerParams(dimension_semantics=sem, vmem_limit_bytes=VMEM_LIMIT)


def _sds(shape, dtype=F32):
    return jax.ShapeDtypeStruct(shape, dtype)


def _place():
    return lax.axis_index("x"), lax.axis_index("y"), lax.axis_index("c")


class _Rider:
    def __init__(self, ins, out_shapes, n_remote, n_local, plan):
        self.ins, self.out_shapes, self.n_remote, self.n_local, self.plan = list(ins), list(out_shapes), n_remote, n_local, plan

    def scratch(self):
        return [pltpu.SemaphoreType.DMA((self.n_remote,)), pltpu.SemaphoreType.DMA((self.n_remote,)),
                pltpu.SemaphoreType.DMA((max(self.n_local, 1),))]

    def run(self, first, in_refs, out_refs, sems):
        send_sems, recv_sems, local_sems = sems
        sends, recvs, locals_ = self.plan(in_refs, out_refs)
        remote = lambda k, src, dst, to: pltpu.make_async_remote_copy(
            src_ref=src, dst_ref=dst, send_sem=send_sems.at[k], recv_sem=recv_sems.at[k], device_id=to, device_id_type=MESH)
        if first:
            for k, src, dst, to in sends:
                remote(k, src, dst, to).start()
            for j, (src, dst) in enumerate(locals_):
                pltpu.make_async_copy(src, dst, local_sems.at[j]).start()
        else:
            for k, dst, frm in recvs:
                remote(k, dst, dst, frm).wait_recv()
            for k, src, dst, to in sends:
                remote(k, src, dst, to).wait_send()
            for j, (src, dst) in enumerate(locals_):
                pltpu.make_async_copy(src, dst, local_sems.at[j]).wait()


def _other_chips(x, y):
    return [(1 - x, y), (x, 1 - y), (1 - x, 1 - y)]


def _gather_rider(shard):
    hr = shard.shape[0] // 2

    def plan(ins, outs):
        x, y, c = _place()
        mine, src, dst = 2 * x + y, ins[0], outs[0]
        sends, recvs = [], []
        for j, (px, py) in enumerate(_other_chips(x, y)):
            for flip in range(2):
                tc = c if flip == 0 else 1 - c
                sends.append((2 * j + flip, src.at[pl.ds(c * hr, hr)], dst.at[mine, pl.ds(c * hr, hr)], (px, py, tc)))
                recvs.append((2 * j + flip, dst.at[2 * px + py, pl.ds(tc * hr, hr)], (px, py, tc)))
        return sends, recvs, [(src, dst.at[mine])]

    return _Rider([shard], [_sds((4,) + shard.shape, shard.dtype)], 6, 1, plan)


def _sibling_swap_rider(parts):
    n = len(parts)

    def plan(ins, outs):
        x, y, c = _place()
        sibling = (x, y, 1 - c)
        return ([(i, ins[i].at[:, 1 - c], outs[i], sibling) for i in range(n)], [(i, outs[i], sibling) for i in range(n)], [])

    return _Rider(parts, [_sds((4,) + p.shape[2:], p.dtype) for p in parts], n, 0, plan)


def _chip_exchange_rider(parts):
    n = len(parts)

    def plan(ins, outs):
        x, y, c = _place()
        mine = 2 * x + y
        sends, recvs = [], []
        for i in range(n):
            for j, (px, py) in enumerate(_other_chips(x, y)):
                sends.append((3 * i + j, ins[i].at[2 * px + py], outs[i].at[mine], (px, py, c)))
                recvs.append((3 * i + j, outs[i].at[2 * px + py], (px, py, c)))
        return sends, recvs, [(ins[i].at[mine], outs[i].at[mine]) for i in range(n)]

    return _Rider(parts, [_sds(p.shape, p.dtype) for p in parts], 3 * n, n, plan)


def _call(body, name, grid, in_specs, out_specs, out_shape, args, sem, scratch=(), rider=None):
    if rider is None:
        return list(pl.pallas_call(body, name=name, grid=grid, in_specs=in_specs, out_specs=out_specs, out_shape=out_shape,
                                   scratch_shapes=list(scratch), compiler_params=_params(sem))(*args)), []
    n_in, n_out, n_scr = len(in_specs), len(out_specs), len(scratch)
    r_in, r_out = len(rider.ins), len(rider.out_shapes)
    steps = grid[0]

    def riding(*refs):
        ins, r_ins = refs[:n_in], refs[n_in:n_in + r_in]
        outs = refs[n_in + r_in:n_in + r_in + n_out]
        r_outs = refs[n_in + r_in + n_out:n_in + r_in + n_out + r_out]
        scr = refs[n_in + r_in + n_out + r_out:n_in + r_in + n_out + r_out + n_scr]
        sems = refs[n_in + r_in + n_out + r_out + n_scr:]

        @pl.when(pl.program_id(0) == 0)
        def _():
            rider.run(True, r_ins, r_outs, sems)

        body(*ins, *outs, *scr)

        @pl.when(pl.program_id(0) == steps - 1)
        def _():
            rider.run(False, r_ins, r_outs, sems)

    hbm = pl.BlockSpec(memory_space=pl.ANY)
    res = pl.pallas_call(
        riding, name=name, grid=grid, in_specs=list(in_specs) + [hbm] * r_in, out_specs=list(out_specs) + [hbm] * r_out,
        out_shape=list(out_shape) + rider.out_shapes, scratch_shapes=list(scratch) + rider.scratch(),
        compiler_params=_params("arbitrary"),
    )(*args, *rider.ins)
    return list(res[:n_out]), list(res[n_out:])


def _row_tile(seq, want):
    return want if seq % want == 0 else seq


def _rope_tables(posf, seq):
    half = ROPE_DIM // 2
    inv_freq = np.power(np.float32(ROPE_THETA), -np.arange(half, dtype=np.float32) / np.float32(half)).astype(np.float32)
    j = np.arange(LANES) % HEAD_DIM
    invf = jnp.asarray(np.where(j < ROPE_DIM, inv_freq[j % half], 0.0).astype(np.float32).reshape(1, LANES))
    m_p = jnp.asarray(((j >= half) & (j < ROPE_DIM)).astype(np.float32).reshape(1, LANES))
    m_m = jnp.asarray(-(j < half).astype(np.float32).reshape(1, LANES))
    tm = _row_tile(seq, 512)

    def body(pos_ref, invf_ref, mp_ref, mm_ref, cos_ref, sp_ref, sm_ref):
        def block(i, carry):
            rows = pl.ds(pl.multiple_of(i * tm, tm), tm)
            ang = pos_ref[rows, :] * invf_ref[...]
            s = jnp.sin(ang)
            cos_ref[rows, :] = jnp.cos(ang)
            sp_ref[rows, :] = s * mp_ref[...]
            sm_ref[rows, :] = s * mm_ref[...]
            return carry

        lax.fori_loop(0, seq // tm, block, 0)

    return body, (posf, invf, m_p, m_m), [_sds((seq, LANES))] * 3


def _even_proj(x, mod, w_in_t, tabs, seq, rider=None):
    tm = _row_tile(seq, 512)

    def body(x_ref, mod_ref, w_ref, cos_ref, sp_ref, sm_ref, q_ref, k_ref, v_ref, su_ref, sv_ref, g_ref):
        h = x_ref[...] * (1.0 + mod_ref[1:2, :]) + mod_ref[0:1, :]
        p = _mm_nt(h, w_ref[...])
        cos_t, sin_p, sin_m = cos_ref[...], sp_ref[...], sm_ref[...]
        for j in range(ATTN_WIDTH // LANES):
            q_ref[:, j * LANES:(j + 1) * LANES] = _rope(p[:, j * LANES:(j + 1) * LANES], cos_t, sin_p, sin_m).astype(q_ref.dtype)
        k_ref[...] = _rope(p[:, 512:640], cos_t, sin_p, sin_m).astype(k_ref.dtype)
        v_ref[...] = p[:, 640:768].astype(v_ref.dtype)
        su_ref[...] = p[:, 768:1280].astype(su_ref.dtype)
        sv_ref[...] = p[:, 1280:1792].astype(sv_ref.dtype)
        g_ref[...] = p[:, 1792:2816].astype(g_ref.dtype)

    return _call(
        body, "even_proj", (seq // tm,),
        [_rows(tm, D_MODEL), _full((3, D_MODEL)), _const((EVEN_IN, D_MODEL))] + [_rows(tm, LANES)] * 3,
        [_rows(tm, 512), _rows(tm, LANES), _rows(tm, LANES), _rows(tm, 512), _rows(tm, 512), _rows(tm, D_MODEL)],
        [_sds((seq, 512), MXU_DTYPE), _sds((seq, LANES), MXU_DTYPE), _sds((seq, LANES), MXU_DTYPE), _sds((seq, 512), ACT_DTYPE),
         _sds((seq, 512), ACT_DTYPE), _sds((seq, D_MODEL), ACT_DTYPE)],
        (x, mod, w_in_t, *tabs), "parallel", rider=rider)


def _sg_forward(sv, lng, lnb, sgw_ref, sgb, e2):
    vn, vhat, rstd, svo = [], [], [], []
    for j in range(SG_WIDTH // LANES):
        t = sv[:, j * LANES:(j + 1) * LANES]
        mu = _group_sum(t, e2) * (1.0 / HEAD_DIM)
        d = t - mu
        var = _group_sum(d * d, e2) * (1.0 / HEAD_DIM)
        r = lax.rsqrt(var + LN_EPS)
        vh = d * r
        vhat.append(vh)
        rstd.append(r)
        vn.append(vh * lng[:, j * LANES:(j + 1) * LANES] + lnb[:, j * LANES:(j + 1) * LANES])
    lane = _lane_iota((BLK, LANES))
    for j in range(SG_WIDTH // LANES):
        lo = _mm(sgw_ref[2 * j], vn[j])
        hi = _mm(sgw_ref[2 * j + 1], vn[j])
        svo.append(jnp.where(lane < HEAD_DIM, lo, hi) + sgb[:, j * LANES:(j + 1) * LANES])
    return svo, vn, vhat, rstd


def _even_mix(q, k, v, su, sv, sink, sgln_g, sgln_b, sgw, sgb_full, e2, seq, rider=None):
    nb = seq // BLK

    def body(sink_ref, q_ref, k_ref, v_ref, su_ref, sv_ref, lng_ref, lnb_ref, sgw_ref, sgb_ref, e2_ref, ycat_ref, lse_ref,
             svo_ref, vhat_ref, rstd_ref):
        n = pl.program_id(0)
        kband = _band(k_ref, n, nb)
        vband = _band(v_ref, n, nb)
        bias = _band_bias(n, seq)
        lane = _lane_iota((BLK, LANES))
        lse = jnp.zeros((BLK, LANES), F32)
        q_tile = lambda j: q_ref[:, j * LANES:(j + 1) * LANES].astype(F32)
        acc = [jnp.zeros((BLK, LANES), F32) for _ in range(ATTN_WIDTH // LANES)]
        for kv in range(N_Q_HEADS // Q_PER_KV):
            heads = range(Q_PER_KV * kv, Q_PER_KV * (kv + 1))
            sink = _per_head_column([sink_ref[h] for h in heads])
            s = _mm_nt(_stack_heads(q_tile, kv), kband) * (HEAD_DIM ** -0.5) + bias
            m = jnp.maximum(jnp.max(s, axis=1, keepdims=True), sink)
            p = jnp.exp(s - m)
            denom = jnp.sum(p, axis=1, keepdims=True) + jnp.exp(sink - m)
            o4 = _mm(p / denom, vband)
            l4 = m + jnp.log(denom)
            for g, h in enumerate(heads):
                acc[h // 2] = acc[h // 2] + _from_kv_lanes(o4[g * BLK:(g + 1) * BLK], h)
                lse = jnp.where(lane == h, l4[g * BLK:(g + 1) * BLK], lse)
        for j in range(ATTN_WIDTH // LANES):
            ycat_ref[:, j * LANES:(j + 1) * LANES] = acc[j].astype(ycat_ref.dtype)
        lse_ref[...] = lse
        svo, _, vhat, rstd = _sg_forward(sv_ref[...].astype(F32), lng_ref[...], lnb_ref[...], sgw_ref, sgb_ref[...], e2_ref[...])
        for j in range(SG_WIDTH // LANES):
            cs = slice(j * LANES, (j + 1) * LANES)
            ysg = su_ref[:, cs].astype(F32) * svo[j]
            ycat_ref[:, ATTN_WIDTH + j * LANES:ATTN_WIDTH + (j + 1) * LANES] = ysg.astype(ycat_ref.dtype)
            svo_ref[:, cs], vhat_ref[:, cs], rstd_ref[:, cs] = (t.astype(svo_ref.dtype) for t in (svo[j], vhat[j], rstd[j]))

    blk = lambda w: pl.BlockSpec((BLK, w), lambda n: (n, 0))
    return _call(
        body, "even_mix", (nb,),
        [pl.BlockSpec(memory_space=pltpu.SMEM), blk(512), _full((seq, LANES)), _full((seq, LANES)), blk(512), blk(512),
         _full((1, 512)), _full((1, 512)), _full((8, BLK, BLK)), _full((BLK, 512)), _full((LANES, LANES))],
        [blk(D_MODEL), blk(LANES)] + [blk(SG_WIDTH)] * 3,
        [_sds((seq, D_MODEL), ACT_DTYPE), _sds((seq, LANES))] + [_sds((seq, SG_WIDTH), ACT_DTYPE)] * 3,
        (sink, q, k, v, su, sv, sgln_g, sgln_b, sgw, sgb_full, e2), "parallel", rider=rider)


def _even_out(ycat, g, x, mod, mod_next, w_out, w_in4_next, ln_g, ln_b, seq, rider=None):
    tm = _row_tile(seq, 512)
    cs = ODD_IN // 4

    def body(y_ref, g_ref, x_ref, mod_ref, modn_ref, wo_ref, wi_ref, g1_ref, b1_ref, zhat_ref, rstd_ref, x1_ref, xr_ref, gn_ref):
        gg = g_ref[...].astype(F32)
        out = _mm(y_ref[...].astype(F32) * (gg * _sigmoid(gg)), wo_ref[...])
        z = ALPHA * x_ref[...] + mod_ref[2:3, :] * out
        zhat, rstd = _ln_stats(z)
        zhat_ref[...] = zhat
        rstd_ref[...] = rstd
        x1 = zhat * g1_ref[...] + b1_ref[...]
        x1_ref[...] = x1
        hb = (x1 * (1.0 + modn_ref[1:2, :]) + modn_ref[0:1, :]).astype(MXU_DTYPE)
        for s in range(2):
            xr_ref[:, s * cs:(s + 1) * cs] = jnp.dot(hb, wi_ref[s], preferred_element_type=F32)
            gn_ref[:, s * cs:(s + 1) * cs] = jnp.dot(hb, wi_ref[2 + s], preferred_element_type=F32).astype(gn_ref.dtype)

    return _call(
        body, "even_out", (seq // tm,),
        [_rows(tm, D_MODEL)] * 3 + [_full((3, D_MODEL)), _full((3, D_MODEL)), _const((D_MODEL, D_MODEL)), _const((4, D_MODEL, cs)),
                                    _full((1, D_MODEL)), _full((1, D_MODEL))],
        [_rows(tm, D_MODEL), _rows(tm, 1)] + [_rows(tm, D_MODEL)] * 3,
        [_sds((seq, D_MODEL)), _sds((seq, 1))] + [_sds((seq, D_MODEL))] * 2 + [_sds((seq, D_MODEL), ACT_DTYPE)],
        (ycat, g, x, mod, mod_next, w_out, w_in4_next, ln_g, ln_b), "parallel", rider=rider)


def _halo_specs(tm, seq, width, order=lambda i: i):
    per = tm // 8
    last = seq // 8 - 1
    return [pl.BlockSpec((8, width), lambda i: (jnp.maximum(order(i) * per - 1, 0), 0)),
            pl.BlockSpec((tm, width), lambda i: (order(i), 0)),
            pl.BlockSpec((8, width), lambda i: (jnp.minimum((order(i) + 1) * per, last), 0))]


def _extended(prev_ref, main_ref, next_ref, i, n_steps):
    prev = jnp.where(i > 0, prev_ref[...], 0.0)
    nxt = jnp.where(i < n_steps - 1, next_ref[...], 0.0)
    return jnp.concatenate([prev, main_ref[...], nxt], axis=0)


def _shifted(ext, off, tm):
    if off == 0:
        return ext[8:8 + tm]
    return pltpu.roll(ext, (-off) % ext.shape[0], 0)[8:8 + tm]


SCAN_SUB = 8


def _lru_gate(xh, pre, bias, sp, hs, d):
    r = _sigmoid(pre[:, 0:LANES] + bias[2 * d:2 * d + 1, hs])
    ig = _sigmoid(pre[:, LANES:2 * LANES] + bias[2 * d + 1:2 * d + 2, hs])
    neg_log_a = RG_LRU_C * r * sp[d:d + 1, hs]
    a = jnp.exp(-neg_log_a)
    u = jnp.tanh(neg_log_a) * (a * a + 1.0)
    inv_s = lax.rsqrt(jnp.maximum(u, jnp.finfo(F32).tiny))
    return r, ig, a, u * inv_s, inv_s


def _conv_block(xp_ref, xm_ref, xn_ref, cw_ref, cb_ref, blk, steps, tm):
    ext = _extended(xp_ref, xm_ref, xn_ref, blk, steps)
    return cb_ref[...] + sum(cw_ref[kk:kk + 1, :] * _shifted(ext, kk - 2, tm) for kk in range(4))


def _scan_tiles(a_ref, b_ref, h_ref, hprev_ref, carry_h, carry_a, rows, descending, post):
    sub = SCAN_SUB
    tiles = rows // sub
    row = lax.broadcasted_iota(jnp.int32, (sub, D_MODEL), 0)

    def shift(v, d, fill):
        if descending:
            return jnp.where(row <= sub - 1 - d, pltpu.roll(v, sub - d, 0), fill)
        return jnp.where(row >= d, pltpu.roll(v, d, 0), fill)

    def last(v):
        return jnp.broadcast_to(v[0:1, :] if descending else v[sub - 1:sub, :], v.shape)

    def tile(j, c):
        ch, ca = c
        r0 = pl.multiple_of(((tiles - 1 - j) if descending else j) * sub, sub)
        at = a_ref[pl.ds(r0, sub), :]
        bt = b_ref[pl.ds(r0, sub), :]
        coef = shift(at, 1, ca) if post else at
        acc_a, acc_b = coef, bt
        for d in (1, 2, 4):
            acc_b = acc_b + acc_a * shift(acc_b, d, 0.0)
            acc_a = acc_a * shift(acc_a, d, 1.0)
        h = acc_b + acc_a * ch
        h_ref[pl.ds(r0, sub), :] = h
        if post:
            return last(h), last(at)
        hprev_ref[pl.ds(r0, sub), :] = shift(h, 1, ch)
        return last(h), ca

    ch, ca = lax.fori_loop(0, tiles, tile, (carry_h[...], carry_a[...]), unroll=4)
    carry_h[...] = ch
    carry_a[...] = ca


def _lru_fwd(xr, xc, conv_w, conv_b, wcat, bias, lam, seq, d):
    tb = _row_tile(seq, 512)
    steps = seq // tb
    descending = d == 1
    order = (lambda i: steps - 1 - i) if descending else (lambda i: i)
    with_conv = xc is None
    n_x = 5 if with_conv else 1

    def body(*refs):
        x_refs, (w_ref, bias_ref, lam_ref) = refs[:n_x], refs[n_x:n_x + 3]
        h_ref, hp_ref, a_ref, r_ref, i_ref, s_ref, q_ref = refs[n_x + 3:n_x + 10]
        b_scr, carry_h, carry_a = refs[-3:]
        i = pl.program_id(0)

        @pl.when(i == 0)
        def _():
            carry_h[...] = jnp.zeros_like(carry_h)
            carry_a[...] = jnp.zeros_like(carry_a)

        if with_conv:
            xc_ref = refs[n_x + 10]
            xc_ref[...] = _conv_block(*x_refs, order(i), steps, tb)
        else:
            xc_ref = x_refs[0]
        sp, _ = _softplus_neg(lam_ref[...])
        bias = bias_ref[...]
        for h in range(RNN_HEADS):
            hs = slice(h * LANES, (h + 1) * LANES)
            xh = xc_ref[:, hs]
            r, ig, a, s, q = _lru_gate(xh, _mm(xh, w_ref[h, :, 2 * d * LANES:2 * (d + 1) * LANES]), bias, sp, hs, d)
            a_ref[:, hs] = a
            b_scr[:, hs] = s * ig * xh
            for ref, val in ((r_ref, r), (i_ref, ig), (s_ref, s), (q_ref, q)):
                ref[:, hs] = val.astype(ref.dtype)
        _scan_tiles(a_ref, b_scr, h_ref, hp_ref, carry_h, carry_a, tb, descending, post=False)

    row_spec = pl.BlockSpec((tb, D_MODEL), lambda i: (order(i), 0))
    if with_conv:
        x_specs, x_args = _halo_specs(tb, seq, D_MODEL, order) + [_full((4, D_MODEL)), _full((1, D_MODEL))], (xr, xr, xr, conv_w, conv_b)
    else:
        x_specs, x_args = [row_spec], (xc,)
    n_out = 8 if with_conv else 7
    return pl.pallas_call(
        body, name="lru_fwd_%d" % d, grid=(steps,),
        in_specs=x_specs + [_full((8, LANES, 512)), _full((4, D_MODEL)), _full((2, D_MODEL))],
        out_specs=[row_spec] * n_out,
        out_shape=[_sds((seq, D_MODEL))] * 3 + [_sds((seq, D_MODEL), ACT_DTYPE)] * 4 + [_sds((seq, D_MODEL))] * (n_out - 7),
        scratch_shapes=[pltpu.VMEM((tb, D_MODEL), F32)] + [pltpu.VMEM((SCAN_SUB, D_MODEL), F32)] * 2,
        compiler_params=_params("arbitrary"),
    )(*x_args, wcat, bias, lam)


def _odd_out_and_loss(hf, hr, g, x1, tgt, mod, w_out, ln_g, ln_b, seq):
    tm = _row_tile(seq, 512)

    def body(hf_ref, hr_ref, g_ref, x_ref, t_ref, mod_ref, w_ref, lg_ref, lb_ref,
             dhs_ref, dg_ref, dres_ref, loss_ref, dw_ref, vec_ref):
        @pl.when(pl.program_id(0) == 0)
        def _():
            loss_ref[...] = jnp.zeros_like(loss_ref)
            dw_ref[...] = jnp.zeros_like(dw_ref)
            vec_ref[...] = jnp.zeros_like(vec_ref)

        gg = g_ref[...].astype(F32)
        sg = _sigmoid(gg)
        silu = gg * sg
        hsum = hf_ref[...] + hr_ref[...]
        y = hsum * silu
        out = _mm(y, w_ref[...])
        gate = mod_ref[2:3, :]
        z = ALPHA * x_ref[...] + gate * out
        zhat, rstd = _ln_stats(z)
        x2 = zhat * lg_ref[...] + lb_ref[...]
        err = x2 - t_ref[...]
        loss_ref[...] += 0.5 * jnp.sum(jnp.mean(err * err, axis=-1, keepdims=True))
        dx2 = err * (1.0 / D_MODEL)
        dz = _ln_bwd(dx2, zhat, rstd, lg_ref[...])
        vec_ref[0:1, :] += jnp.sum(dx2 * zhat, axis=0, keepdims=True)
        vec_ref[1:2, :] += jnp.sum(dx2, axis=0, keepdims=True)
        vec_ref[2:3, :] += jnp.sum(dz * out, axis=0, keepdims=True)
        dres_ref[...] = ALPHA * dz
        dout = gate * dz
        dw_ref[...] += _mm_tn(y, dout)
        dy = _mm_nt(dout, w_ref[...])
        dhs_ref[...] = dy * silu
        dg_ref[...] = (dy * hsum * (sg * (1.0 + gg * (1.0 - sg)))).astype(dg_ref.dtype)

    return pl.pallas_call(
        body, name="odd_out_loss", grid=(seq // tm,),
        in_specs=[_rows(tm, D_MODEL)] * 5 + [_full((3, D_MODEL)), _const((D_MODEL, D_MODEL)),
                                             _full((1, D_MODEL)), _full((1, D_MODEL))],
        out_specs=[_rows(tm, D_MODEL)] * 3 + [_full((8, LANES)), _full((D_MODEL, D_MODEL)), _full((8, D_MODEL))],
        out_shape=[_sds((seq, D_MODEL)), _sds((seq, D_MODEL), ACT_DTYPE), _sds((seq, D_MODEL)), _sds((8, LANES)),
                   _sds((D_MODEL, D_MODEL)), _sds((8, D_MODEL))],
        compiler_params=_params("arbitrary"),
    )(hf, hr, g, x1, tgt, mod, w_out, ln_g, ln_b)


def _lru_bwd(xc, dhs, hprev, a_d, r_d, i_d, s_d, q_d, wcat, lam, seq, d):
    tb = _row_tile(seq, 512)
    steps = seq // tb
    descending = d == 0
    order = (lambda i: steps - 1 - i) if descending else (lambda i: i)
    cols = slice(2 * d * LANES, 2 * (d + 1) * LANES)

    def body(xc_ref, dhs_ref, hp_ref, a_ref, r_ref, i_ref, s_ref, q_ref, w_ref, lam_ref, dxc_ref, dw_ref, vec_ref,
             g_scr, carry_h, carry_a):
        i = pl.program_id(0)

        @pl.when(i == 0)
        def _():
            dw_ref[...] = jnp.zeros_like(dw_ref)
            vec_ref[...] = jnp.zeros_like(vec_ref)
            carry_h[...] = jnp.zeros_like(carry_h)
            carry_a[...] = jnp.zeros_like(carry_a)

        sp, dsp = _softplus_neg(lam_ref[...])
        _scan_tiles(a_ref, dhs_ref, g_scr, None, carry_h, carry_a, tb, descending, post=True)
        for h in range(RNN_HEADS):
            hs = slice(h * LANES, (h + 1) * LANES)
            xh, a = xc_ref[:, hs], a_ref[:, hs]
            r, ig, s = r_ref[:, hs].astype(F32), i_ref[:, hs].astype(F32), s_ref[:, hs].astype(F32)
            db = g_scr[:, hs]
            da = db * hp_ref[:, hs]
            dlog_a = da * a - (db * ig * xh) * (a * a * q_ref[:, hs].astype(F32))
            dpr = dlog_a * (-RG_LRU_C) * sp[d:d + 1, hs] * r * (1.0 - r)
            dpi = db * s * xh * ig * (1.0 - ig)
            vec_ref[0:1, hs] += jnp.sum(dpr, axis=0, keepdims=True)
            vec_ref[1:2, hs] += jnp.sum(dpi, axis=0, keepdims=True)
            vec_ref[2:3, hs] += jnp.sum(dlog_a * r, axis=0, keepdims=True) * (-RG_LRU_C) * dsp[d:d + 1, hs]
            dcat = jnp.concatenate([dpr, dpi], axis=1)
            dw_ref[h] += _mm_tn(xh, dcat)
            dxc_ref[:, hs] = db * s * ig + _mm_nt(dcat, w_ref[h, :, cols])

    row_spec = pl.BlockSpec((tb, D_MODEL), lambda i: (order(i), 0))
    return pl.pallas_call(
        body, name="lru_bwd_%d" % d, grid=(steps,),
        in_specs=[row_spec] * 8 + [_full((8, LANES, 512)), _full((2, D_MODEL))],
        out_specs=[row_spec, _full((8, LANES, 2 * LANES)), _full((8, D_MODEL))],
        out_shape=[_sds((seq, D_MODEL)), _sds((8, LANES, 2 * LANES)), _sds((8, D_MODEL))],
        scratch_shapes=[pltpu.VMEM((tb, D_MODEL), F32)] + [pltpu.VMEM((SCAN_SUB, D_MODEL), F32)] * 2,
        compiler_params=_params("arbitrary"),
    )(xc, dhs, hprev, a_d, r_d, i_d, s_d, q_d, wcat, lam)


def _odd_proj_bwd(dxc_f, dxc_r, xr, dg, x1, dres, mod, conv_w, w_in4, seq):
    tm = _row_tile(seq, 512)
    steps = seq // tm

    def body(fp_ref, fm_ref, fn_ref, rp_ref, rm_ref, rn_ref, xp_ref, xm_ref, xn_ref, dg_ref, x_ref, dres_ref, mod_ref, cw_ref,
             w_ref, dx_ref, dw_ref, vec_ref, dpb_ref):
        i = pl.program_id(0)

        @pl.when(i == 0)
        def _():
            vec_ref[...] = jnp.zeros_like(vec_ref)
            dw_ref[...] = jnp.zeros_like(dw_ref)

        dxc_m = fm_ref[...] + rm_ref[...]
        dext = jnp.concatenate([jnp.where(i > 0, fp_ref[...] + rp_ref[...], 0.0), dxc_m,
                                jnp.where(i < steps - 1, fn_ref[...] + rn_ref[...], 0.0)], axis=0)
        xext = _extended(xp_ref, xm_ref, xn_ref, i, steps)
        dxr = sum(cw_ref[kk:kk + 1, :] * _shifted(dext, 2 - kk, tm) for kk in range(4))
        for kk in range(4):
            vec_ref[kk:kk + 1, :] += jnp.sum(dxc_m * _shifted(xext, kk - 2, tm), axis=0, keepdims=True)
        vec_ref[4:5, :] += jnp.sum(dxc_m, axis=0, keepdims=True)
        dpb_ref[:, :D_MODEL] = dxr.astype(dpb_ref.dtype)
        dpb_ref[:, D_MODEL:] = dg_ref[...].astype(dpb_ref.dtype)
        cs = ODD_IN // 4
        dh = sum(_mm_nt(dpb_ref[:, s * cs:(s + 1) * cs], w_ref[s]) for s in range(4))
        x = x_ref[...]
        h_t = (x * (1.0 + mod_ref[1:2, :]) + mod_ref[0:1, :]).T.astype(MXU_DTYPE)
        for s in range(4):
            dw_ref[s] += jnp.dot(h_t, dpb_ref[:, s * cs:(s + 1) * cs], preferred_element_type=F32)
        vec_ref[5:6, :] += jnp.sum(dh, axis=0, keepdims=True)
        vec_ref[6:7, :] += jnp.sum(dh * x, axis=0, keepdims=True)
        dx_ref[...] = dres_ref[...] + dh * (1.0 + mod_ref[1:2, :])

    return pl.pallas_call(
        body, name="odd_proj_bwd", grid=(steps,),
        in_specs=_halo_specs(tm, seq, D_MODEL) * 3 + [_rows(tm, D_MODEL)] * 3
        + [_full((3, D_MODEL)), _full((4, D_MODEL)), _const((4, D_MODEL, ODD_IN // 4))],
        out_specs=[_rows(tm, D_MODEL), _const((4, D_MODEL, ODD_IN // 4)), _full((8, D_MODEL))],
        out_shape=[_sds((seq, D_MODEL)), _sds((4, D_MODEL, ODD_IN // 4)), _sds((8, D_MODEL))],
        scratch_shapes=[pltpu.VMEM((tm, ODD_IN), MXU_DTYPE)],
        compiler_params=_params("arbitrary"),
    )(dxc_f, dxc_f, dxc_f, dxc_r, dxc_r, dxc_r, xr, xr, xr, dg, x1, dres, mod, conv_w, w_in4)


def _even_out_bwd(dx1, zhat, rstd, ycat, g, mod, ln_g, w_out, seq, rider=None):
    tm = _row_tile(seq, 512)
    steps = seq // tm

    def body(dx_ref, zh_ref, rs_ref, y_ref, g_ref, mod_ref, lg_ref, w_ref, dy_ref, dg_ref, dres_ref, dw_ref, vec_ref):
        i = pl.program_id(0)

        @pl.when(i == 0)
        def _():
            dw_ref[...] = jnp.zeros_like(dw_ref)
            vec_ref[...] = jnp.zeros_like(vec_ref)

        zhat = zh_ref[...]
        dx1_ = dx_ref[...]
        dz = _ln_bwd(dx1_, zhat, rs_ref[...], lg_ref[...])
        vec_ref[0:1, :] += jnp.sum(dx1_ * zhat, axis=0, keepdims=True)
        vec_ref[1:2, :] += jnp.sum(dx1_, axis=0, keepdims=True)
        dres_ref[...] = ALPHA * dz
        gate = mod_ref[2:3, :]
        gg = g_ref[...].astype(F32)
        sg = _sigmoid(gg)
        silu = gg * sg
        ycat_ = y_ref[...].astype(F32)
        dw_ref[...] += _mm_tn(ycat_ * silu, dz)
        dy = _mm_nt(gate * dz, w_ref[...])
        dy_ref[...] = (dy * silu).astype(dy_ref.dtype)
        dg_ref[...] = (dy * ycat_ * (sg * (1.0 + gg * (1.0 - sg)))).astype(dg_ref.dtype)

        @pl.when(i == steps - 1)
        def _():
            m_acc = dw_ref[...]
            vec_ref[2:3, :] = jnp.sum(w_ref[...].astype(F32) * m_acc, axis=0, keepdims=True)
            dw_ref[...] = m_acc * gate

    return _call(
        body, "even_out_bwd", (steps,),
        [_rows(tm, D_MODEL), _rows(tm, D_MODEL), _rows(tm, 1), _rows(tm, D_MODEL), _rows(tm, D_MODEL), _full((3, D_MODEL)),
         _full((1, D_MODEL)), _const((D_MODEL, D_MODEL))],
        [_rows(tm, D_MODEL)] * 3 + [_full((D_MODEL, D_MODEL)), _full((8, D_MODEL))],
        [_sds((seq, D_MODEL), ACT_DTYPE), _sds((seq, D_MODEL), ACT_DTYPE), _sds((seq, D_MODEL)), _sds((D_MODEL, D_MODEL)),
         _sds((8, D_MODEL))],
        (dx1, zhat, rstd, ycat, g, mod, ln_g, w_out), "arbitrary", rider=rider)


def _even_mix_bwd(q, k, v, lse, ycat, dycat, su, svo_s, vhat_s, rstd_s, sink, sgln_g, sgln_b, sgw, e2, e8, seq, rider=None):
    nb = seq // BLK

    def body(sink_ref, q_ref, k_ref, v_ref, lse_ref, y_ref, dy_ref, su_ref, svo_ref, vhat_ref, rstd_ref, lng_ref, lnb_ref, sgw_ref,
             e2_ref, e8_ref, dq_ref, dsu_ref, dsv_ref, dk_ref, dv_ref, dsgw_ref, dsgb_ref, vec_ref, dsink_ref, dsgb_acc):
        n = pl.program_id(0)

        @pl.when(n == 0)
        def _():
            dk_ref[...] = jnp.zeros_like(dk_ref)
            dv_ref[...] = jnp.zeros_like(dv_ref)
            dsgw_ref[...] = jnp.zeros_like(dsgw_ref)
            dsgb_acc[...] = jnp.zeros_like(dsgb_acc)
            vec_ref[...] = jnp.zeros_like(vec_ref)
            dsink_ref[...] = jnp.zeros_like(dsink_ref)

        kband = _band(k_ref, n, nb)
        vband = _band(v_ref, n, nb)
        bias = _band_bias(n, seq)
        lane = _lane_iota((BLK, LANES))
        row8 = lax.broadcasted_iota(jnp.int32, (8, LANES), 0)
        lse = lse_ref[...]
        dkb = jnp.zeros((LANES, 3 * BLK), F32)
        dvb = jnp.zeros((LANES, 3 * BLK), F32)
        dsink = jnp.zeros((8, LANES), F32)
        q_tile = lambda j: q_ref[:, j * LANES:(j + 1) * LANES].astype(F32)
        do_tile = lambda j: dy_ref[:, j * LANES:(j + 1) * LANES].astype(F32)
        dq = [jnp.zeros((BLK, LANES), F32) for _ in range(ATTN_WIDTH // LANES)]
        for kv in range(N_Q_HEADS // Q_PER_KV):
            heads = range(Q_PER_KV * kv, Q_PER_KV * (kv + 1))
            lse4, delta4 = [], []
            for h in heads:
                head_lanes = (lane < HEAD_DIM) if h % 2 == 0 else (lane >= HEAD_DIM)
                lse4.append(jnp.sum(jnp.where(lane == h, lse, 0.0), axis=1, keepdims=True))
                o_tile = y_ref[:, (h // 2) * LANES:(h // 2 + 1) * LANES].astype(F32)
                delta4.append(jnp.sum(jnp.where(head_lanes, do_tile(h // 2) * o_tile, 0.0), axis=1, keepdims=True))
            lse4, delta4 = jnp.concatenate(lse4, axis=0), jnp.concatenate(delta4, axis=0)
            q4, do4 = _stack_heads(q_tile, kv), _stack_heads(do_tile, kv)
            s = _mm_nt(q4, kband) * (HEAD_DIM ** -0.5) + bias
            p = jnp.exp(s - lse4)
            wsink = jnp.exp(_per_head_column([sink_ref[h] for h in heads]) - lse4) * delta4
            ds = p * (_mm_nt(do4, vband) - delta4) * (HEAD_DIM ** -0.5)
            dq4 = _mm(ds, kband)
            dkb = dkb + _mm_tn(q4, ds)
            dvb = dvb + _mm_tn(do4, p)
            for g, h in enumerate(heads):
                dq[h // 2] = dq[h // 2] + _from_kv_lanes(dq4[g * BLK:(g + 1) * BLK], h)
                dsink = dsink + jnp.where(row8 == h, -jnp.sum(wsink[g * BLK:(g + 1) * BLK]), 0.0)
        for j in range(ATTN_WIDTH // LANES):
            dq_ref[:, j * LANES:(j + 1) * LANES] = dq[j].astype(dq_ref.dtype)
        dsink_ref[...] += dsink
        prev = jnp.maximum(n - 1, 0)
        nxt = jnp.minimum(n + 1, nb - 1)
        for part, blk_i in enumerate((prev, n, nxt)):
            rows = pl.ds(pl.multiple_of(blk_i * BLK, BLK), BLK)
            dk_ref[rows, :] += dkb[:, part * BLK:(part + 1) * BLK].T
            dv_ref[rows, :] += dvb[:, part * BLK:(part + 1) * BLK].T

        e2 = e2_ref[...]
        lng, lnb = lng_ref[...], lnb_ref[...]
        for j in range(SG_WIDTH // LANES):
            cs = slice(j * LANES, (j + 1) * LANES)
            vhat = vhat_ref[:, cs].astype(F32)
            vn = vhat * lng[:, cs] + lnb[:, cs]
            dysg = dy_ref[:, ATTN_WIDTH + j * LANES:ATTN_WIDTH + (j + 1) * LANES].astype(F32)
            dsu_ref[:, cs] = (dysg * svo_ref[:, cs].astype(F32)).astype(dsu_ref.dtype)
            dsvo = dysg * su_ref[:, cs].astype(F32)
            dsgb_acc[:, cs] += dsvo
            d_lo = jnp.where(lane < HEAD_DIM, dsvo, 0.0)
            d_hi = dsvo - d_lo
            dsgw_ref[2 * j] += _mm_nt(d_lo, vn)
            dsgw_ref[2 * j + 1] += _mm_nt(d_hi, vn)
            dvn = _mm_tn(sgw_ref[2 * j], d_lo) + _mm_tn(sgw_ref[2 * j + 1], d_hi)
            vec_ref[0:1, cs] += jnp.sum(dvn * vhat, axis=0, keepdims=True)
            vec_ref[1:2, cs] += jnp.sum(dvn, axis=0, keepdims=True)
            dvh = dvn * lng[:, cs]
            m1 = _group_sum(dvh, e2) * (1.0 / HEAD_DIM)
            m2 = _group_sum(dvh * vhat, e2) * (1.0 / HEAD_DIM)
            dsv_ref[:, cs] = (rstd_ref[:, cs].astype(F32) * (dvh - m1 - vhat * m2)).astype(dsv_ref.dtype)

        @pl.when(n == nb - 1)
        def _():
            rest = dsgb_acc[...]
            total = jnp.zeros((8, BLK), F32)
            for _ in range(3):
                part = rest.astype(MXU_DTYPE)
                total = total + lax.dot_general(e8_ref[...], part, (((1,), (1,)), ((), ())), preferred_element_type=F32)
                rest = rest - part.astype(F32)
            dsgb_ref[...] = total

    blk = lambda w: pl.BlockSpec((BLK, w), lambda n: (n, 0))
    return _call(
        body, "even_mix_bwd", (nb,),
        [pl.BlockSpec(memory_space=pltpu.SMEM), blk(512), _full((seq, LANES)), _full((seq, LANES)), blk(LANES),
         blk(D_MODEL), blk(D_MODEL), blk(512), blk(512), blk(512), blk(512), _full((1, 512)), _full((1, 512)), _full((8, BLK, BLK)),
         _full((LANES, LANES)), _full((8, 512))],
        [blk(512), blk(512), blk(512), _full((seq, LANES)), _full((seq, LANES)), _full((8, BLK, BLK)),
         _full((8, BLK)), _full((8, 512)), _full((8, LANES))],
        [_sds((seq, 512), ACT_DTYPE), _sds((seq, 512), ACT_DTYPE), _sds((seq, 512), ACT_DTYPE), _sds((seq, LANES)), _sds((seq, LANES)),
         _sds((8, BLK, BLK)), _sds((8, BLK)), _sds((8, 512)), _sds((8, LANES))],
        (sink, q, k, v, lse, ycat, dycat, su, svo_s, vhat_s, rstd_s, sgln_g, sgln_b, sgw, e2, e8), "arbitrary",
        scratch=[pltpu.VMEM((BLK, 512), F32)], rider=rider)


def _even_proj_bwd(dq, dk, dv, dsu, dsv, dg, x, dres, mod, tabs, w_in_t, seq):
    tm = _row_tile(seq, 512)

    def body(dq_ref, dk_ref, dv_ref, dsu_ref, dsv_ref, dg_ref, x_ref, dres_ref, mod_ref, cos_ref, sp_ref, sm_ref, wt_ref,
             dx_ref, dw_ref, vec_ref, dpb_ref):
        @pl.when(pl.program_id(0) == 0)
        def _():
            vec_ref[...] = jnp.zeros_like(vec_ref)
            dw_ref[...] = jnp.zeros_like(dw_ref)

        cos_t, sin_p, sin_m = cos_ref[...], sp_ref[...], sm_ref[...]
        dt = dpb_ref.dtype
        for j in range(ATTN_WIDTH // LANES):
            cs = slice(j * LANES, (j + 1) * LANES)
            dpb_ref[:, cs] = _rope_t(dq_ref[:, cs].astype(F32), cos_t, sin_p, sin_m).astype(dt)
        dpb_ref[:, 512:640] = _rope_t(dk_ref[...], cos_t, sin_p, sin_m).astype(dt)
        dpb_ref[:, 640:768] = dv_ref[...].astype(dt)
        dpb_ref[:, 768:1280] = dsu_ref[...].astype(dt)
        dpb_ref[:, 1280:1792] = dsv_ref[...].astype(dt)
        dpb_ref[:, 1792:2816] = dg_ref[...].astype(dt)
        dpb = dpb_ref[...]
        dh = jnp.dot(dpb, wt_ref[...], preferred_element_type=F32)
        x_ = x_ref[...]
        hb = (x_ * (1.0 + mod_ref[1:2, :]) + mod_ref[0:1, :]).astype(MXU_DTYPE)
        dw_ref[...] += _mm_tn(dpb, hb)
        vec_ref[0:1, :] += jnp.sum(dh, axis=0, keepdims=True)
        vec_ref[1:2, :] += jnp.sum(dh * x_, axis=0, keepdims=True)
        dx_ref[...] = dres_ref[...] + dh * (1.0 + mod_ref[1:2, :])

    return pl.pallas_call(
        body, name="even_proj_bwd", grid=(seq // tm,),
        in_specs=[_rows(tm, 512), _rows(tm, LANES), _rows(tm, LANES), _rows(tm, 512), _rows(tm, 512), _rows(tm, D_MODEL),
                  _rows(tm, D_MODEL), _rows(tm, D_MODEL), _full((3, D_MODEL))] + [_rows(tm, LANES)] * 3
        + [_const((EVEN_IN, D_MODEL))],
        out_specs=[_rows(tm, D_MODEL), _const((EVEN_IN, D_MODEL)), _full((8, D_MODEL))],
        out_shape=[_sds((seq, D_MODEL)), _sds((EVEN_IN, D_MODEL)), _sds((8, D_MODEL))],
        scratch_shapes=[pltpu.VMEM((tm, EVEN_IN), MXU_DTYPE)],
        compiler_params=_params("arbitrary"),
    )(dq, dk, dv, dsu, dsv, dg, x, dres, mod, *tabs, w_in_t)


def _local_step(x, tabs, tgt, mod, w, seq, ride=None):
    rid = lambda make, *a: None if ride is None else make(*a)
    mxu = lambda a: a.astype(MXU_DTYPE)
    row = lambda a: a.reshape(1, -1)
    e2 = mxu(jnp.kron(jnp.eye(2, dtype=F32), jnp.ones((HEAD_DIM, HEAD_DIM), F32)))
    e8 = mxu(jnp.repeat(jnp.eye(N_SG_GROUPS, dtype=F32), HEAD_DIM, axis=1))
    sgw = mxu(w["ev_sg_w"])
    sgb_full = jnp.repeat(w["ev_sg_b"].T, HEAD_DIM, axis=1)
    sgln_g, sgln_b = row(w["ev_sg_ln_g"]), row(w["ev_sg_ln_b"])
    sink = w["ev_sink"].reshape(N_Q_HEADS)
    ev_w_in_t = mxu(w["ev_w_in_t"])
    if ride is None:
        ev_w_out, od_w_in, od_w_out = mxu(w["ev_w_out"]), mxu(w["od_w_in"]), mxu(w["od_w_out"])
    wcat = mxu(jnp.concatenate([w["od_w_a"][0], w["od_w_x"][0], w["od_w_a"][1], w["od_w_x"][1]], axis=2))
    gate_bias = jnp.stack([w["od_b_a"][0], w["od_b_x"][0], w["od_b_a"][1], w["od_b_x"][1]])
    conv_b = row(w["od_conv_b"])
    ln_g, ln_b = w["ln_g"], w["ln_b"]

    (q, k, v, su, sv, g0), got = _even_proj(x, mod[0], ev_w_in_t, tabs, seq, rid(_gather_rider, ride and ride["ev_w_out"]))
    if ride is not None:
        ev_w_out = got[0].reshape(D_MODEL, D_MODEL)
    (ycat, lse, *sg_saved), got = _even_mix(q, k, v, su, sv, sink, sgln_g, sgln_b, sgw, sgb_full, e2, seq,
                                 rid(_gather_rider, ride and ride["od_w_in"]))
    if ride is not None:
        od_w_in = got[0]
    (zhat0, rstd0, x1, xr, g1), got = _even_out(ycat, g0, x, mod[0], mod[1], ev_w_out, od_w_in, ln_g[0:1], ln_b[0:1], seq,
                                      rid(_gather_rider, ride and ride["od_w_out"]))
    if ride is not None:
        od_w_out = got[0].reshape(D_MODEL, D_MODEL)
    lru = (w["od_conv_w"], conv_b, wcat, gate_bias, w["od_lam"], seq)
    hf, hpf, *saved_f, xc = _lru_fwd(xr, None, *lru, 0)
    hr, hpr, *saved_r = _lru_fwd(xr, xc, *lru, 1)
    dhs, dg1, dres1, loss, d_od_w_out, vec_o = _odd_out_and_loss(hf, hr, g1, x1, tgt, mod[1], od_w_out, ln_g[1:2], ln_b[1:2], seq)
    dxc_f, dw_f, vec_f = _lru_bwd(xc, dhs, hpf, *saved_f, wcat, w["od_lam"], seq, 0)
    dxc_r, dw_r, vec_r = _lru_bwd(xc, dhs, hpr, *saved_r, wcat, w["od_lam"], seq, 1)
    dx1, d_od_w_in, vec_p = _odd_proj_bwd(dxc_f, dxc_r, xr, dg1, x1, dres1, mod[1], w["od_conv_w"], od_w_in, seq)
    d_od_w_a = jnp.stack([dw_f[:, :, 0:128], dw_r[:, :, 0:128]])
    d_od_w_x = jnp.stack([dw_f[:, :, 128:256], dw_r[:, :, 128:256]])
    od_parts = [d_od_w_in.reshape(4, 2, 512, 512), d_od_w_out.reshape(4, 2, 128, D_MODEL),
                d_od_w_a.reshape(4, 2, 2 * BLK, BLK), d_od_w_x.reshape(4, 2, 2 * BLK, BLK)]
    (dycat, dg0, dres0, d_ev_w_out, vec_e), got_od = _even_out_bwd(dx1, zhat0, rstd0, ycat, g0, mod[0], ln_g[0:1], ev_w_out, seq,
                                                                   rid(_sibling_swap_rider, od_parts))
    if ride is not None:
        od_sums = _sum_sibling(ride["core"], od_parts, got_od, [ride["wire"]] * 4, "sum_sibling_od")
    (dq, dsu, dsv, dk, dv, d_sgw, d_sgb, vec_s, d_sink), od_slots = _even_mix_bwd(
        q, k, v, lse, ycat, dycat, su, *sg_saved, sink, sgln_g, sgln_b, sgw, e2, e8, seq,
        rid(_chip_exchange_rider, ride and od_sums))
    grad_x, d_ev_w_in_t, vec_x = _even_proj_bwd(dq, dk, dv, dsu, dsv, dg0, x, dres0, mod[0], tabs, ev_w_in_t, seq)

    rows, dmod_blk = _pack_small(vec_x, vec_e, vec_p, vec_o, vec_f, vec_r, vec_s, d_sink, d_sgb, loss)
    grads = {"rows": rows, "dmod_blk": dmod_blk, "ev_w_in_t": d_ev_w_in_t, "ev_w_out": d_ev_w_out, "ev_sg_w": d_sgw}
    if ride is None:
        grads.update({"od_w_in": d_od_w_in, "od_w_out": d_od_w_out, "od_w_a": d_od_w_a, "od_w_x": d_od_w_x})
    else:
        grads["od_slots"] = od_slots
    return grad_x, grads


ROW_DMOD, ROW_LN, ROW_SG_LN, ROW_SG_B, ROW_CONV_W, ROW_CONV_B, ROW_B_A, ROW_B_X, ROW_LAM, ROW_SINK, ROW_LOSS = (
    0, 6, 10, 11, 12, 16, 17, 19, 21, 23, 24)
SMALL_ROWS = 64


def _pack_small(vec_x, vec_e, vec_p, vec_o, vec_f, vec_r, vec_s, d_sink, d_sgb, loss):
    def body(x_ref, e_ref, p_ref, o_ref, f_ref, r_ref, s_ref, sink_ref, sgb_ref, loss_ref, rows_ref, dmod_ref):
        rows_ref[...] = jnp.zeros_like(rows_ref)
        dmod_ref[...] = jnp.zeros_like(dmod_ref)
        put = [(ROW_DMOD, x_ref, 0), (ROW_DMOD + 1, x_ref, 1), (ROW_DMOD + 2, e_ref, 2), (ROW_DMOD + 3, p_ref, 5),
               (ROW_DMOD + 4, p_ref, 6), (ROW_DMOD + 5, o_ref, 2), (ROW_LN, e_ref, 0), (ROW_LN + 1, e_ref, 1),
               (ROW_LN + 2, o_ref, 0), (ROW_LN + 3, o_ref, 1), (ROW_CONV_B, p_ref, 4), (ROW_B_A, f_ref, 0),
               (ROW_B_A + 1, r_ref, 0), (ROW_B_X, f_ref, 1), (ROW_B_X + 1, r_ref, 1), (ROW_LAM, f_ref, 2), (ROW_LAM + 1, r_ref, 2)]
        put += [(ROW_CONV_W + k, p_ref, k) for k in range(4)]
        for dst, ref, src in put:
            rows_ref[dst:dst + 1, :] = ref[src:src + 1, :]
            if dst < 6:
                dmod_ref[dst:dst + 1, :] = ref[src:src + 1, :]
        rows_ref[ROW_SG_LN:ROW_SG_LN + 1, 0:SG_WIDTH] = s_ref[0:1, :]
        rows_ref[ROW_SG_LN:ROW_SG_LN + 1, SG_WIDTH:2 * SG_WIDTH] = s_ref[1:2, :]
        lane = _lane_iota((1, LANES))
        sink = jnp.zeros((1, LANES), F32)
        for h in range(N_Q_HEADS):
            rows_ref[ROW_SG_B:ROW_SG_B + 1, h * LANES:(h + 1) * LANES] = sgb_ref[h:h + 1, :]
            sink = jnp.where(lane == h, sink_ref[h:h + 1, :], sink)
        rows_ref[ROW_SINK:ROW_SINK + 1, 0:LANES] = sink
        rows_ref[ROW_LOSS:ROW_LOSS + 1, 0:LANES] = jnp.where(lane == 0, loss_ref[0:1, :], 0.0)

    return pl.pallas_call(body, name="pack_small", out_shape=[_sds((SMALL_ROWS, D_MODEL)), _sds((8, D_MODEL))])(
        vec_x, vec_e, vec_p, vec_o, vec_f, vec_r, vec_s, d_sink, d_sgb, loss)


def _allgather8(block, name):
    m_per, n = block.shape

    def body(x_ref, out_ref, send_sems, recv_sems, local_sem):
        x, y, c = _place()
        me, sibling = (x, y, c), (x, y, 1 - c)
        chips = [(1 - x, y), (x, 1 - y), (1 - x, 1 - y)]

        def rows(px, py, pc):
            return out_ref.at[pl.ds((4 * px + 2 * py + pc) * m_per, m_per), :]

        def copy(k, blk, to, src=None):
            return pltpu.make_async_remote_copy(src_ref=rows(*blk) if src is None else src, dst_ref=rows(*blk),
                                                send_sem=send_sems.at[k], recv_sem=recv_sems.at[k], device_id=to,
                                                device_id_type=MESH)

        mine = pltpu.make_async_copy(x_ref, rows(*me), local_sem)
        mine.start()
        first = [copy(0, me, sibling, src=x_ref)] + [copy(1 + j, me, (*chip, c), src=x_ref) for j, chip in enumerate(chips)]
        for cp in first:
            cp.start()
        passed = [copy(4 + j, (*chip, c), sibling) for j, chip in enumerate(chips)]
        for j, chip in enumerate(chips):
            copy(1 + j, (*chip, c), me).wait_recv()
            passed[j].start()
        copy(0, sibling, me).wait_recv()
        for j, chip in enumerate(chips):
            copy(4 + j, (*chip, 1 - c), me).wait_recv()
        for cp in first + passed:
            cp.wait_send()
        mine.wait()

    return pl.pallas_call(
        body, name=name, out_shape=_sds((8 * m_per, n), block.dtype),
        in_specs=[pl.BlockSpec(memory_space=pltpu.VMEM)], out_specs=pl.BlockSpec(memory_space=pltpu.VMEM),
        scratch_shapes=[pltpu.SemaphoreType.DMA((7,)), pltpu.SemaphoreType.DMA((7,)), pltpu.SemaphoreType.DMA],
        compiler_params=pltpu.CompilerParams(vmem_limit_bytes=VMEM_LIMIT),
    )(block)


class _Copies:
    def __init__(self, send_sems, recv_sems, local_sems, stages):
        self.send_sems, self.recv_sems, self.local_sems, self.stages = send_sems, recv_sems, local_sems, stages
        self.sent, self.staged, self.locals = [], [], []

    def remote(self, k, src, dst, to):
        return pltpu.make_async_remote_copy(src_ref=src, dst_ref=dst, send_sem=self.send_sems.at[k], recv_sem=self.recv_sems.at[k],
                                            device_id=to, device_id_type=MESH)

    def send(self, k, src, dst, to):
        cp = self.remote(k, src, dst, to)
        cp.start()
        self.sent.append(cp)

    def arrived(self, k, dst, frm):
        self.remote(k, dst, dst, frm).wait_recv()

    def local(self, src, dst):
        k = len(self.staged)
        cp = pltpu.make_async_copy(src, self.stages[k], self.local_sems.at[2 * k])
        cp.start()
        self.staged.append((cp, dst))

    def flush(self):
        for k in range(len(self.locals), len(self.staged)):
            cp, dst = self.staged[k]
            cp.wait()
            out = pltpu.make_async_copy(self.stages[k], dst, self.local_sems.at[2 * k + 1])
            out.start()
            self.locals.append(out)

    def drain(self):
        self.flush()
        for cp in self.sent:
            cp.wait_send()
        for cp in self.locals:
            cp.wait()


def _comm_call(body, name, ins, out_shapes, n_remote, stages, side=None):
    n_in, n_out = len(ins), len(out_shapes)
    side_fn, side_ins, side_outs = side if side is not None else (None, (), [])
    s_in, s_out = len(side_ins), len(side_outs)

    def kern(*refs):
        in_refs, refs = refs[:n_in], refs[n_in:]
        side_in_refs, refs = refs[:s_in], refs[s_in:]
        out_refs, refs = refs[:n_out], refs[n_out:]
        side_out_refs, refs = refs[:s_out], refs[s_out:]
        if side is None:
            body(_Copies(refs[0], refs[1], refs[2], refs[3:]), in_refs, out_refs)
            return
        side_bufs, side_sems, refs = refs[:s_out], refs[s_out], refs[s_out + 1:]
        cps = _Copies(refs[0], refs[1], refs[2], refs[3:])
        leave = [pltpu.make_async_copy(side_bufs[k], side_out_refs[k], side_sems.at[k]) for k in range(s_out)]

        def run_side():
            side_fn(*side_in_refs, *side_bufs)
            for cp in leave:
                cp.start()

        body(cps, in_refs, out_refs, run_side)
        for cp in leave:
            cp.wait()

    hbm, vmem = pl.BlockSpec(memory_space=pl.ANY), pl.BlockSpec(memory_space=pltpu.VMEM)
    side_scratch = [] if side is None else [pltpu.VMEM(o.shape, o.dtype) for o in side_outs] + [pltpu.SemaphoreType.DMA((s_out,))]
    return pl.pallas_call(
        kern, name=name, out_shape=list(out_shapes) + list(side_outs), in_specs=[hbm] * n_in + [vmem] * s_in,
        out_specs=[hbm] * (n_out + s_out),
        scratch_shapes=side_scratch + [pltpu.SemaphoreType.DMA((n_remote,)), pltpu.SemaphoreType.DMA((n_remote,)),
                                       pltpu.SemaphoreType.DMA((2 * len(stages),))] + [pltpu.VMEM(s, d) for s, d in stages],
        compiler_params=pltpu.CompilerParams(vmem_limit_bytes=VMEM_LIMIT),
    )(*ins, *side_ins)


def _gather_to_all(cps, pairs, me, sibling, other_chips, c, base, meanwhile=None):
    idx = lambda p: 4 * p[0] + 2 * p[1] + p[2]
    for i, (src, dst) in enumerate(pairs):
        cps.local(src, dst.at[idx(me)])
        cps.send(base + 7 * i, src, dst.at[idx(me)], sibling)
        for j, chip in enumerate(other_chips):
            cps.send(base + 7 * i + 1 + j, src, dst.at[idx(me)], (*chip, c))
    cps.flush()
    if meanwhile is not None:
        meanwhile()
    for j, chip in enumerate(other_chips):
        for i, (_, dst) in enumerate(pairs):
            got = dst.at[idx((*chip, c))]
            cps.arrived(base + 7 * i + 1 + j, got, (*chip, c))
            cps.send(base + 7 * i + 4 + j, got, got, sibling)
    for i, (_, dst) in enumerate(pairs):
        cps.arrived(base + 7 * i, dst.at[idx(sibling)], sibling)
        for j, chip in enumerate(other_chips):
            cps.arrived(base + 7 * i + 4 + j, dst.at[idx((*chip, 1 - c))], sibling)


def _gather_weights(shards, small, side):
    n = len(shards)

    def body(cps, ins, outs, run_side):
        x, y, c = _place()
        me, sibling, mine = (x, y, c), (x, y, 1 - c), 2 * x + y
        chips = [(1 - x, y), (x, 1 - y), (1 - x, 1 - y)]
        for i in range(n):
            cps.local(ins[i], outs[i].at[mine])
        for j, (px, py) in enumerate(chips):
            for i in range(n):
                hr = shards[i].shape[0] // 2
                rows = pl.ds(c * hr, hr)
                cps.send(6 * i + j, ins[i].at[rows], outs[i].at[mine, rows], (px, py, c))
        _gather_to_all(cps, [(ins[n], outs[n])], me, sibling, chips, c, 6 * n, meanwhile=run_side)
        for j, (px, py) in enumerate(chips):
            for i in range(n):
                hr = shards[i].shape[0] // 2
                got = outs[i].at[2 * px + py, pl.ds(c * hr, hr)]
                cps.arrived(6 * i + j, got, (px, py, c))
                cps.send(6 * i + 3 + j, got, got, sibling)
        for j, (px, py) in enumerate(chips):
            for i in range(n):
                hr = shards[i].shape[0] // 2
                cps.arrived(6 * i + 3 + j, outs[i].at[2 * px + py, pl.ds((1 - c) * hr, hr)], sibling)
        cps.drain()

    return _comm_call(body, "gather_weights", list(shards) + [small],
                      [_sds((4,) + s.shape, s.dtype) for s in shards] + [_sds((8,) + small.shape, small.dtype)], 6 * n + 7,
                      [(a.shape, a.dtype) for a in list(shards) + [small]], side)


def _reduce_sibling(parts, dmod_rows):
    n = len(parts)

    def body(cps, ins, outs):
        x, y, c = _place()
        me, sibling = (x, y, c), (x, y, 1 - c)
        chips = [(1 - x, y), (x, 1 - y), (1 - x, 1 - y)]
        for i in range(n):
            cps.send(i, ins[i].at[:, 1 - c], outs[i], sibling)
        _gather_to_all(cps, [(ins[n], outs[n])], me, sibling, chips, c, n)
        for i in range(n):
            cps.arrived(i, outs[i], sibling)
        cps.drain()

    return _comm_call(body, "reduce_sibling", list(parts) + [dmod_rows],
                      [_sds((4,) + p.shape[2:], p.dtype) for p in parts] + [_sds((8,) + dmod_rows.shape, dmod_rows.dtype)], n + 7,
                      [(dmod_rows.shape, dmod_rows.dtype)])


def _reduce_chips(parts):
    n = len(parts)

    def body(cps, ins, outs):
        x, y, c = _place()
        mine = 2 * x + y
        chips = _other_chips(x, y)
        for i in range(n):
            cps.local(ins[i].at[mine], outs[i].at[mine])
        for j, (px, py) in enumerate(chips):
            for i in range(n):
                cps.send(3 * i + j, ins[i].at[2 * px + py], outs[i].at[mine], (px, py, c))
        cps.flush()
        for j, (px, py) in enumerate(chips):
            for i in range(n):
                cps.arrived(3 * i + j, outs[i].at[2 * px + py], (px, py, c))
        cps.drain()

    return _comm_call(body, "reduce_chips", list(parts), [_sds(p.shape, p.dtype) for p in parts], 3 * n,
                      [(p.shape[1:], p.dtype) for p in parts])


def _gather_reduced(shard_parts, repl_parts):
    ns, nr = len(shard_parts), len(repl_parts)

    def body(cps, ins, outs):
        x, y, c = _place()
        me, sibling = (x, y, c), (x, y, 1 - c)
        chips = [(1 - x, y), (x, 1 - y), (1 - x, 1 - y)]
        for i in range(ns):
            cps.local(ins[i], outs[i].at[c])
            cps.send(i, ins[i], outs[i].at[c], sibling)
        _gather_to_all(cps, [(ins[ns + i], outs[ns + i]) for i in range(nr)], me, sibling, chips, c, ns)
        for i in range(ns):
            cps.arrived(i, outs[i].at[1 - c], sibling)
        cps.drain()

    return _comm_call(body, "gather_reduced", list(shard_parts) + list(repl_parts),
                      [_sds((2,) + p.shape, p.dtype) for p in shard_parts] + [_sds((8,) + p.shape, p.dtype) for p in repl_parts],
                      ns + 7 * nr, [(p.shape, p.dtype) for p in list(shard_parts) + list(repl_parts)])


def _sum_sibling(core, parts, got, wire, name):
    n = len(parts)

    def body(core_ref, *refs):
        for i in range(n):
            refs[2 * n + i][0] = (refs[i][0] + refs[n + i][0]).astype(wire[i])

    keep_spec = lambda p: pl.BlockSpec((1, None) + p.shape[2:], lambda s, core_ref: (s, core_ref[0], 0, 0))
    slot_spec = lambda p: pl.BlockSpec((1,) + p.shape[2:], lambda s, core_ref: (s, 0, 0))
    return pl.pallas_call(
        body, name=name,
        grid_spec=pltpu.PrefetchScalarGridSpec(
            num_scalar_prefetch=1, grid=(4,), in_specs=[keep_spec(p) for p in parts] + [slot_spec(p) for p in parts],
            out_specs=[slot_spec(p) for p in parts]),
        out_shape=[_sds((4,) + p.shape[2:], wire[i]) for i, p in enumerate(parts)],
        compiler_params=_params("parallel"),
    )(core, *parts, *got)


def _sum_slots(slots, name):
    n = len(slots)

    def spec_pair(p):
        k, rows, cols = p.shape
        sub = 16 if p.dtype == BF16 else 8
        if (rows // 2) % sub == 0:
            return pl.BlockSpec((k, rows // 2, cols), lambda i: (0, i, 0)), pl.BlockSpec((rows // 2, cols), lambda i: (i, 0))
        return pl.BlockSpec((k, rows, cols), lambda i: (0, 0, 0)), pl.BlockSpec((rows, cols), lambda i: (0, 0))

    pairs = [spec_pair(p) for p in slots]

    def body(*refs):
        for i in range(n):
            acc = refs[i][0].astype(F32)
            for j in range(1, slots[i].shape[0]):
                acc = acc + refs[i][j].astype(F32)
            refs[n + i][...] = acc

    return pl.pallas_call(
        body, name=name, grid=(2,), in_specs=[a for a, _ in pairs], out_specs=[b for _, b in pairs],
        out_shape=[_sds(p.shape[1:]) for p in slots], compiler_params=_params("arbitrary"),
    )(*slots)


def _modulation(c_all, ada_w, ada_b):
    cols = ada_w.shape[2]

    def body(c_ref, w_ref, b_ref, o_ref):
        cc = c_ref[...]
        o_ref[0] = _mm(cc * _sigmoid(cc), w_ref[0]) + b_ref[0]

    return pl.pallas_call(
        body, name="modulation", grid=(2,),
        in_specs=[_full((8, D_MODEL)), pl.BlockSpec((1, D_MODEL, cols), lambda l: (l, 0, 0)), pl.BlockSpec((1, 1, cols), lambda l: (l, 0, 0))],
        out_specs=pl.BlockSpec((1, 8, cols), lambda l: (l, 0, 0)), out_shape=_sds((2, 8, cols)),
        compiler_params=_params("parallel"),
    )(c_all, ada_w, ada_b)


def _adamw_math(w, g, m, v):
    m = ADAM_B1 * m + (1.0 - ADAM_B1) * g
    v = ADAM_B2 * v + (1.0 - ADAM_B2) * (g * g)
    m_hat = m / (1.0 - ADAM_B1 ** ADAM_STEP)
    v_hat = v / (1.0 - ADAM_B2 ** ADAM_STEP)
    delta = -ADAM_LR * (m_hat / (jnp.sqrt(v_hat) + ADAM_EPS) + ADAM_WD * w)
    return delta, m, v


def _ada_update(c_all, dmod, w, m, v, rider=None):
    cols = w.shape[2]
    tr = 256
    per = D_MODEL // tr
    spec3 = pl.BlockSpec((1, tr, cols), lambda i: (i // per, i % per, 0))

    def body(c_ref, d_ref, w_ref, m_ref, v_ref, g_ref, dl_ref, nm_ref, nv_ref):
        cc = c_ref[...]
        g = _mm_tn(cc * _sigmoid(cc), d_ref[0])
        g_ref[0] = g
        dl_ref[0], nm_ref[0], nv_ref[0] = _adamw_math(w_ref[0], g, m_ref[0], v_ref[0])

    return _call(
        body, "ada_update", (2 * per,),
        [pl.BlockSpec((8, tr), lambda i: (0, i % per)), pl.BlockSpec((1, 8, cols), lambda i: (i // per, 0, 0)), spec3, spec3, spec3],
        [spec3] * 4, [_sds(w.shape)] * 4, (c_all, dmod, w, m, v), "parallel", rider=rider)


def _adamw_matrices(params):
    n = len(params)
    steps = 8

    def body(*refs):
        ins, outs = refs[:4 * n], refs[4 * n:]
        for j in range(n):
            w_ref, g_ref, m_ref, v_ref = ins[4 * j:4 * j + 4]
            g = g_ref[...]
            outs[4 * j][...] = g
            outs[4 * j + 1][...], outs[4 * j + 2][...], outs[4 * j + 3][...] = _adamw_math(w_ref[...], g, m_ref[...], v_ref[...])

    spec = lambda p: _rows(p[0].shape[0] // steps, p[0].shape[1])
    res = pl.pallas_call(
        body, name="adamw_matrices", grid=(steps,), in_specs=[spec(p) for p in params for _ in range(4)],
        out_specs=[spec(p) for p in params for _ in range(4)], out_shape=[_sds(p[0].shape) for p in params for _ in range(4)],
        compiler_params=_params("parallel"),
    )(*[a for p in params for a in p])
    return [tuple(res[4 * j:4 * j + 4]) for j in range(n)]


def _adamw_small(gs, chip, params):
    n = len(params)
    shard_cols = D_MODEL // 4

    def body(chip_ref, rows_ref, cols_ref, *refs):
        ins, outs = refs[:3 * n], refs[3 * n:]
        for j in range(n):
            w_ref, m_ref, v_ref = ins[3 * j:3 * j + 3]
            g_ref, d_ref, nm_ref, nv_ref = outs[4 * j:4 * j + 4]
            for dst, sharded, src in params[j][3]:
                g = (cols_ref if sharded else rows_ref)[src]
                g_ref[dst] = g
                d_ref[dst], nm_ref[dst], nv_ref[dst] = _adamw_math(w_ref[dst], g, m_ref[dst], v_ref[dst])

    whole = lambda a: pl.BlockSpec(a.shape, lambda i, chip_ref: (0, 0))
    flat = [a for p in params for a in p[:3]]
    res = pl.pallas_call(
        body, name="adamw_small",
        grid_spec=pltpu.PrefetchScalarGridSpec(
            num_scalar_prefetch=1, grid=(1,),
            in_specs=[whole(gs), pl.BlockSpec((gs.shape[0], shard_cols), lambda i, chip_ref: (0, chip_ref[0]))] + [whole(a) for a in flat],
            out_specs=[whole(p[0]) for p in params for _ in range(4)]),
        out_shape=[_sds(p[0].shape) for p in params for _ in range(4)],
        compiler_params=_params("arbitrary"),
    )(chip, gs, gs, *flat)
    return [tuple(res[4 * j:4 * j + 4]) for j in range(n)]


def _cols(a, start, size):
    return lax.dynamic_slice_in_dim(a, start, size, axis=a.ndim - 1)


def kernel(x, c, positions, ada_w, ada_b, ln_g, ln_b, ev_w_in, ev_w_out, ev_sink, ev_sg_ln_g, ev_sg_ln_b, ev_sg_w, ev_sg_b, od_w_in, od_conv_w, od_conv_b, od_w_a, od_b_a, od_w_x, od_b_x, od_lam, od_w_out, loss_target, m_ada_w, m_ada_b, m_ln_g, m_ln_b, m_ev_w_in, m_ev_w_out, m_ev_sink, m_ev_sg_ln_g, m_ev_sg_ln_b, m_ev_sg_w, m_ev_sg_b, m_od_w_in, m_od_conv_w, m_od_conv_b, m_od_w_a, m_od_b_a, m_od_w_x, m_od_b_x, m_od_lam, m_od_w_out, v_ada_w, v_ada_b, v_ln_g, v_ln_b, v_ev_w_in, v_ev_w_out, v_ev_sink, v_ev_sg_ln_g, v_ev_sg_ln_b, v_ev_sg_w, v_ev_sg_b, v_od_w_in, v_od_conv_w, v_od_conv_b, v_od_w_a, v_od_b_a, v_od_w_x, v_od_b_x, v_od_lam, v_od_w_out):
    seq = x.shape[1]
    px, py, pc = _place()
    chip = 2 * px + py
    dev = 2 * chip + pc

    small = jnp.concatenate([od_conv_w[0].reshape(-1), od_conv_b[0], od_b_a[0].reshape(-1), jnp.zeros((256,), F32),
                             od_b_x[0].reshape(-1), od_lam[0].reshape(-1)]).reshape(3, D_MODEL)
    blk = jnp.concatenate([c, small, jnp.zeros((4, D_MODEL), F32)], axis=0)
    tr = lambda a: jnp.swapaxes(a, -1, -2)
    wire_w = lambda a: a.astype(MXU_DTYPE)
    posf = positions.astype(F32).reshape(seq, 1)
    ev_w_in4, g_small, *tabs = _gather_weights([wire_w(tr(ev_w_in[0]))], blk, _rope_tables(posf, seq))
    core = pc.astype(jnp.int32).reshape(1)
    ride = {"ev_w_out": wire_w(ev_w_out[0]), "od_w_in": wire_w(od_w_in[0]), "od_w_out": wire_w(od_w_out[0]),
            "core": core, "wire": MXU_DTYPE}
    c_all = g_small[:, 0, :]
    per_chip = g_small[0::2]
    conv_w = per_chip[:, 1].reshape(4, 4, 256).transpose(1, 0, 2).reshape(4, D_MODEL)
    conv_b = per_chip[:, 2, 0:256].reshape(D_MODEL)
    b_a = per_chip[:, 2, 256:768].reshape(4, 2, 256).transpose(1, 0, 2).reshape(2, D_MODEL)
    b_x = per_chip[:, 3, 0:512].reshape(4, 2, 256).transpose(1, 0, 2).reshape(2, D_MODEL)
    lam = per_chip[:, 3, 512:1024].reshape(4, 2, 256).transpose(1, 0, 2).reshape(2, D_MODEL)

    w_full = {
        "ev_w_in_t": ev_w_in4.reshape(EVEN_IN, D_MODEL),
        "ev_sink": ev_sink[0], "ev_sg_ln_g": ev_sg_ln_g[0], "ev_sg_ln_b": ev_sg_ln_b[0], "ev_sg_w": ev_sg_w[0],
        "ev_sg_b": ev_sg_b[0], "od_conv_w": conv_w, "od_conv_b": conv_b, "od_w_a": od_w_a[0], "od_b_a": b_a,
        "od_w_x": od_w_x[0], "od_b_x": b_x, "od_lam": lam, "ln_g": ln_g, "ln_b": ln_b,
    }

    ada_cols = ada_w.shape[2]
    mod_sh = _modulation(c_all, ada_w, _cols(ada_b, chip * ada_cols, ada_cols).reshape(2, 1, ada_cols))
    mod_all = _allgather8(mod_sh.reshape(16, ada_cols), "gather_mod").reshape(4, 2, 2, 8, ada_cols)[:, 0]
    mod_mine = lax.dynamic_index_in_dim(mod_all, dev, axis=2, keepdims=False)
    mod = mod_mine.transpose(1, 0, 2).reshape(2, 3, D_MODEL)

    grad_x, g = _local_step(x[0], tabs, loss_target[0], mod, w_full, seq, ride)

    parts = [g["ev_w_in_t"].reshape(4, 2, 352, D_MODEL), g["ev_w_out"].reshape(4, 2, 128, D_MODEL),
             g["ev_sg_w"].reshape(4, 2, BLK, BLK), g["rows"].reshape(4, 2, SMALL_ROWS // 8, D_MODEL)]
    wire = [MXU_DTYPE] * 3 + [F32]
    *got, dmod_gathered = _reduce_sibling(parts, g["dmod_blk"])
    ev_slots = list(_reduce_chips(_sum_sibling(core, parts, got, wire, "sum_sibling")))
    od_slots = list(g["od_slots"])
    mine = _sum_slots(ev_slots[0:2] + od_slots[0:2] + ev_slots[2:3] + od_slots[2:4] + ev_slots[3:4], "sum_chips")
    reduced = _gather_reduced(mine[:4], mine[4:])
    g_ev_w_in_t = reduced[0].reshape(704, D_MODEL)
    g_ev_w_out = reduced[1].reshape(256, D_MODEL)
    g_od_w_in = reduced[2].reshape(D_MODEL, 512)
    g_od_w_out = reduced[3].reshape(256, D_MODEL)
    g_sg_w = reduced[4].reshape(8 * BLK, BLK)
    g_w_a = reduced[5].reshape(16 * BLK, BLK)
    g_w_x = reduced[6].reshape(16 * BLK, BLK)
    gs = reduced[7].reshape(SMALL_ROWS, D_MODEL)
    loss = gs[ROW_LOSS, 0]
    dmod_all = dmod_gathered[:, 0:6].reshape(8, 2, 3 * D_MODEL)
    dmod_sh = _cols(dmod_all, chip * ada_cols, ada_cols).transpose(1, 0, 2)
    (g_ada_w, d_ada_w, nm_ada_w, nv_ada_w), _ = _ada_update(c_all, dmod_sh, ada_w, m_ada_w, v_ada_w)

    mats = (("ev_w_out", ev_w_out, g_ev_w_out, m_ev_w_out, v_ev_w_out), ("od_w_in", od_w_in, g_od_w_in, m_od_w_in, v_od_w_in),
            ("od_w_out", od_w_out, g_od_w_out, m_od_w_out, v_od_w_out), ("ev_sg_w", ev_sg_w, g_sg_w, m_ev_sg_w, v_ev_sg_w),
            ("od_w_a", od_w_a, g_w_a, m_od_w_a, v_od_w_a), ("od_w_x", od_w_x, g_w_x, m_od_w_x, v_od_w_x))
    upd = _adamw_matrices([(tr(ev_w_in[0]), g_ev_w_in_t, tr(m_ev_w_in[0]), tr(v_ev_w_in[0]))]
                          + [(w_.reshape(g_.shape), g_, m_.reshape(g_.shape), v_.reshape(g_.shape)) for _, w_, g_, m_, v_ in mats])
    big = {"ev_w_in": tuple(tr(a).reshape(ev_w_in.shape) for a in upd[0])}
    for (name, w_, _, _, _), u in zip(mats, upd[1:]):
        big[name] = tuple(a.reshape(w_.shape) for a in u)
    big["ada_w"] = (g_ada_w, d_ada_w, nm_ada_w, nv_ada_w)

    at = lambda r0, nr, c0, nc: (slice(r0, r0 + nr), slice(c0, c0 + nc))
    local = lambda r0, nr: ((nr, 256), [(at(0, nr, 0, 256), True, at(r0, nr, 0, 256))])
    small_g = {
        "ada_b": ((2, 3 * D_MODEL), [(at(l, 1, k * D_MODEL, D_MODEL), False, at(ROW_DMOD + 3 * l + k, 1, 0, D_MODEL))
                                     for l in range(2) for k in range(3)]),
        "ln_g": ((2, D_MODEL), [(at(l, 1, 0, D_MODEL), False, at(ROW_LN + 2 * l, 1, 0, D_MODEL)) for l in range(2)]),
        "ln_b": ((2, D_MODEL), [(at(l, 1, 0, D_MODEL), False, at(ROW_LN + 1 + 2 * l, 1, 0, D_MODEL)) for l in range(2)]),
        "ev_sink": ((1, N_Q_HEADS), [(at(0, 1, 0, N_Q_HEADS), False, at(ROW_SINK, 1, 0, N_Q_HEADS))]),
        "ev_sg_ln_g": ((1, SG_WIDTH), [(at(0, 1, 0, SG_WIDTH), False, at(ROW_SG_LN, 1, 0, SG_WIDTH))]),
        "ev_sg_ln_b": ((1, SG_WIDTH), [(at(0, 1, 0, SG_WIDTH), False, at(ROW_SG_LN, 1, SG_WIDTH, SG_WIDTH))]),
        "ev_sg_b": ((N_SG_GROUPS, BLK), [(at(j, 1, 0, BLK), False, at(ROW_SG_B, 1, j * BLK, BLK)) for j in range(N_SG_GROUPS)]),
        "od_conv_w": local(ROW_CONV_W, 4), "od_conv_b": local(ROW_CONV_B, 1), "od_b_a": local(ROW_B_A, 2),
        "od_b_x": local(ROW_B_X, 2), "od_lam": local(ROW_LAM, 2),
    }
    small_in = {"ada_b": (ada_b, m_ada_b, v_ada_b), "ln_g": (ln_g, m_ln_g, v_ln_g), "ln_b": (ln_b, m_ln_b, v_ln_b),
                "ev_sink": (ev_sink, m_ev_sink, v_ev_sink), "ev_sg_ln_g": (ev_sg_ln_g, m_ev_sg_ln_g, v_ev_sg_ln_g),
                "ev_sg_ln_b": (ev_sg_ln_b, m_ev_sg_ln_b, v_ev_sg_ln_b), "ev_sg_b": (ev_sg_b, m_ev_sg_b, v_ev_sg_b),
                "od_conv_w": (od_conv_w, m_od_conv_w, v_od_conv_w), "od_conv_b": (od_conv_b, m_od_conv_b, v_od_conv_b),
                "od_b_a": (od_b_a, m_od_b_a, v_od_b_a), "od_b_x": (od_b_x, m_od_b_x, v_od_b_x),
                "od_lam": (od_lam, m_od_lam, v_od_lam)}
    names_small = list(small_g)
    upd = _adamw_small(gs, chip.astype(jnp.int32).reshape(1),
                       [tuple(a.reshape(small_g[n][0]) for a in small_in[n]) + (small_g[n][1],) for n in names_small])
    res = dict(big)
    for n, u in zip(names_small, upd):
        res[n] = tuple(a.reshape(small_in[n][0].shape) for a in u)

    order = ["ada_w", "ada_b", "ln_g", "ln_b", "ev_w_in", "ev_w_out", "ev_sink", "ev_sg_ln_g", "ev_sg_ln_b", "ev_sg_w", "ev_sg_b",
             "od_w_in", "od_conv_w", "od_conv_b", "od_w_a", "od_b_a", "od_w_x", "od_b_x", "od_lam", "od_w_out"]
    return (loss, grad_x.reshape(x.shape), *[res[n][0] for n in order], *[res[n][1] for n in order],
            *[res[n][2] for n in order], *[res[n][3] for n in order])
```

```python
from functools import partial

import jax
import jax.numpy as jnp
import numpy as np
from jax import lax
from jax.experimental import pallas as pl
from jax.experimental.pallas import tpu as pltpu

F32 = jnp.float32
BF16 = jnp.bfloat16
MXU_DTYPE = BF16
ACT_DTYPE = MXU_DTYPE

D_MODEL = 1024
HEAD_DIM = 64
N_Q_HEADS = 8
Q_PER_KV = 4
ATTN_WIDTH = 512
BLK = 128
ROPE_DIM = 16
ROPE_THETA = 500000.0
N_SG_GROUPS = 8
SG_WIDTH = 512
EVEN_IN = 2816
ODD_IN = 2048
RNN_HEADS = 8
RG_LRU_C = 8.0
ALPHA = (2 * 2) ** 0.25
LN_EPS = 1e-5
NEG_INF = -1e30
ADAM_LR, ADAM_B1, ADAM_B2, ADAM_EPS, ADAM_WD, ADAM_STEP = 0.001, 0.9, 0.999, 1e-08, 0.01, 10

LANES = 128
VMEM_LIMIT = 56 * 1024 * 1024
MESH = pl.DeviceIdType.MESH


def _mm(a, b):
    return jnp.dot(a.astype(MXU_DTYPE), b.astype(MXU_DTYPE), preferred_element_type=F32)


def _mm_nt(a, b):
    return lax.dot_general(a.astype(MXU_DTYPE), b.astype(MXU_DTYPE), (((1,), (1,)), ((), ())), preferred_element_type=F32)


def _mm_tn(a, b):
    return lax.dot_general(a.astype(MXU_DTYPE), b.astype(MXU_DTYPE), (((0,), (0,)), ((), ())), preferred_element_type=F32)


def _sigmoid(x):
    return 1.0 / (1.0 + jnp.exp(-x))


def _ln_stats(z):
    mu = jnp.mean(z, axis=-1, keepdims=True)
    d = z - mu
    var = jnp.mean(d * d, axis=-1, keepdims=True)
    rstd = lax.rsqrt(var + LN_EPS)
    return d * rstd, rstd


def _ln_bwd(dout, zhat, rstd, g):
    dzh = dout * g
    m1 = jnp.mean(dzh, axis=-1, keepdims=True)
    m2 = jnp.mean(dzh * zhat, axis=-1, keepdims=True)
    return rstd * (dzh - m1 - zhat * m2)


def _group_sum(x, e2):
    hi = x.astype(MXU_DTYPE)
    lo = (x - hi.astype(F32)).astype(MXU_DTYPE)
    return jnp.dot(hi, e2, preferred_element_type=F32) + jnp.dot(lo, e2, preferred_element_type=F32)


def _lane_iota(shape):
    return lax.broadcasted_iota(jnp.int32, shape, 1)


def _to_kv_lanes(t, h):
    src_lo = (h % 2 == 0)
    dst_lo = (h // Q_PER_KV == 0)
    if src_lo != dst_lo:
        t = pltpu.roll(t, HEAD_DIM, 1)
    lane = _lane_iota(t.shape)
    keep = (lane < HEAD_DIM) if dst_lo else (lane >= HEAD_DIM)
    return jnp.where(keep, t, 0.0)


def _from_kv_lanes(t, h):
    src_lo = (h // Q_PER_KV == 0)
    dst_lo = (h % 2 == 0)
    lane = _lane_iota(t.shape)
    keep = (lane < HEAD_DIM) if src_lo else (lane >= HEAD_DIM)
    t = jnp.where(keep, t, 0.0)
    if src_lo != dst_lo:
        t = pltpu.roll(t, HEAD_DIM, 1)
    return t


def _rope(t, cos_t, sin_p, sin_m):
    half = ROPE_DIM // 2
    return t * cos_t + pltpu.roll(t, half, 1) * sin_p + pltpu.roll(t, LANES - half, 1) * sin_m


def _rope_t(d, cos_t, sin_p, sin_m):
    half = ROPE_DIM // 2
    return d * cos_t + pltpu.roll(d * sin_p, LANES - half, 1) + pltpu.roll(d * sin_m, half, 1)


def _band(ref, n, nb):
    prev = jnp.maximum(n - 1, 0)
    nxt = jnp.minimum(n + 1, nb - 1)
    rows = [ref[pl.ds(pl.multiple_of(j * BLK, BLK), BLK), :] for j in (prev, n, nxt)]
    return jnp.concatenate(rows, axis=0)


def _band_bias(n, seq):
    qi = lax.broadcasted_iota(jnp.int32, (BLK, 3 * BLK), 0)
    kj = lax.broadcasted_iota(jnp.int32, (BLK, 3 * BLK), 1)
    k_abs = n * BLK - BLK + kj
    valid = (jnp.abs(kj - BLK - qi) <= BLK) & (k_abs >= 0) & (k_abs < seq)
    bias = jnp.where(valid, 0.0, NEG_INF)
    return jnp.concatenate([bias] * Q_PER_KV, axis=0)


def _stack_heads(tile_of, kv):
    return jnp.concatenate([_to_kv_lanes(tile_of(h // 2), h) for h in range(Q_PER_KV * kv, Q_PER_KV * (kv + 1))], axis=0)


def _per_head_column(vals):
    row = lax.broadcasted_iota(jnp.int32, (Q_PER_KV * BLK, 1), 0)
    return jnp.where(row < BLK, vals[0], jnp.where(row < 2 * BLK, vals[1], jnp.where(row < 3 * BLK, vals[2], vals[3])))


def _softplus_neg(lam):
    e = jnp.exp(-jnp.abs(lam))
    u = 1.0 + e
    log1p_e = jnp.where(u == 1.0, e, jnp.log(u) * (e / (u - 1.0)))
    sp = jnp.maximum(-lam, 0.0) + log1p_e
    dsp = -1.0 / (1.0 + jnp.exp(lam))
    return sp, dsp


def _full(shape):
    return pl.BlockSpec(shape, lambda *_: (0,) * len(shape))


def _const(shape):
    return pl.BlockSpec(shape, lambda *_: (0,) * len(shape), pipeline_mode=pl.Buffered(1))


def _rows(tm, n):
    return pl.BlockSpec((tm, n), lambda i: (i, 0))


def _params(*sem):
    return pltpu.CompilerParams(dimension_semantics=sem, vmem_limit_bytes=VMEM_LIMIT)


def _sds(shape, dtype=F32):
    return jax.ShapeDtypeStruct(shape, dtype)


def _place():
    return lax.axis_index("x"), lax.axis_index("y"), lax.axis_index("c")


class _Rider:
    def __init__(self, ins, out_shapes, n_remote, n_local, plan):
        self.ins, self.out_shapes, self.n_remote, self.n_local, self.plan = list(ins), list(out_shapes), n_remote, n_local, plan

    def scratch(self):
        return [pltpu.SemaphoreType.DMA((self.n_remote,)), pltpu.SemaphoreType.DMA((self.n_remote,)),
                pltpu.SemaphoreType.DMA((max(self.n_local, 1),))]

    def run(self, first, in_refs, out_refs, sems):
        send_sems, recv_sems, local_sems = sems
        sends, recvs, locals_ = self.plan(in_refs, out_refs)
        remote = lambda k, src, dst, to: pltpu.make_async_remote_copy(
            src_ref=src, dst_ref=dst, send_sem=send_sems.at[k], recv_sem=recv_sems.at[k], device_id=to, device_id_type=MESH)
        if first:
            for k, src, dst, to in sends:
                remote(k, src, dst, to).start()
            for j, (src, dst) in enumerate(locals_):
                pltpu.make_async_copy(src, dst, local_sems.at[j]).start()
        else:
            for k, dst, frm in recvs:
                remote(k, dst, dst, frm).wait_recv()
            for k, src, dst, to in sends:
                remote(k, src, dst, to).wait_send()
            for j, (src, dst) in enumerate(locals_):
                pltpu.make_async_copy(src, dst, local_sems.at[j]).wait()


def _other_chips(x, y):
    return [(1 - x, y), (x, 1 - y), (1 - x, 1 - y)]


def _gather_rider(shard):
    hr = shard.shape[0] // 2

    def plan(ins, outs):
        x, y, c = _place()
        mine, src, dst = 2 * x + y, ins[0], outs[0]
        sends, recvs = [], []
        for j, (px, py) in enumerate(_other_chips(x, y)):
            for flip in range(2):
                tc = c if flip == 0 else 1 - c
                sends.append((2 * j + flip, src.at[pl.ds(c * hr, hr)], dst.at[mine, pl.ds(c * hr, hr)], (px, py, tc)))
                recvs.append((2 * j + flip, dst.at[2 * px + py, pl.ds(tc * hr, hr)], (px, py, tc)))
        return sends, recvs, [(src, dst.at[mine])]

    return _Rider([shard], [_sds((4,) + shard.shape, shard.dtype)], 6, 1, plan)


def _sibling_swap_rider(parts):
    n = len(parts)

    def plan(ins, outs):
        x, y, c = _place()
        sibling = (x, y, 1 - c)
        return ([(i, ins[i].at[:, 1 - c], outs[i], sibling) for i in range(n)], [(i, outs[i], sibling) for i in range(n)], [])

    return _Rider(parts, [_sds((4,) + p.shape[2:], p.dtype) for p in parts], n, 0, plan)


def _chip_exchange_rider(parts):
    n = len(parts)

    def plan(ins, outs):
        x, y, c = _place()
        mine = 2 * x + y
        sends, recvs = [], []
        for i in range(n):
            for j, (px, py) in enumerate(_other_chips(x, y)):
                sends.append((3 * i + j, ins[i].at[2 * px + py], outs[i].at[mine], (px, py, c)))
                recvs.append((3 * i + j, outs[i].at[2 * px + py], (px, py, c)))
        return sends, recvs, [(ins[i].at[mine], outs[i].at[mine]) for i in range(n)]

    return _Rider(parts, [_sds(p.shape, p.dtype) for p in parts], 3 * n, n, plan)


def _call(body, name, grid, in_specs, out_specs, out_shape, args, sem, scratch=(), rider=None):
    if rider is None:
        return list(pl.pallas_call(body, name=name, grid=grid, in_specs=in_specs, out_specs=out_specs, out_shape=out_shape,
                                   scratch_shapes=list(scratch), compiler_params=_params(sem))(*args)), []
    n_in, n_out, n_scr = len(in_specs), len(out_specs), len(scratch)
    r_in, r_out = len(rider.ins), len(rider.out_shapes)
    steps = grid[0]

    def riding(*refs):
        ins, r_ins = refs[:n_in], refs[n_in:n_in + r_in]
        outs = refs[n_in + r_in:n_in + r_in + n_out]
        r_outs = refs[n_in + r_in + n_out:n_in + r_in + n_out + r_out]
        scr = refs[n_in + r_in + n_out + r_out:n_in + r_in + n_out + r_out + n_scr]
        sems = refs[n_in + r_in + n_out + r_out + n_scr:]

        @pl.when(pl.program_id(0) == 0)
        def _():
            rider.run(True, r_ins, r_outs, sems)

        body(*ins, *outs, *scr)

        @pl.when(pl.program_id(0) == steps - 1)
        def _():
            rider.run(False, r_ins, r_outs, sems)

    hbm = pl.BlockSpec(memory_space=pl.ANY)
    res = pl.pallas_call(
        riding, name=name, grid=grid, in_specs=list(in_specs) + [hbm] * r_in, out_specs=list(out_specs) + [hbm] * r_out,
        out_shape=list(out_shape) + rider.out_shapes, scratch_shapes=list(scratch) + rider.scratch(),
        compiler_params=_params("arbitrary"),
    )(*args, *rider.ins)
    return list(res[:n_out]), list(res[n_out:])


def _row_tile(seq, want):
    return want if seq % want == 0 else seq


def _rope_tables(posf, seq):
    half = ROPE_DIM // 2
    inv_freq = np.power(np.float32(ROPE_THETA), -np.arange(half, dtype=np.float32) / np.float32(half)).astype(np.float32)
    j = np.arange(LANES) % HEAD_DIM
    invf = jnp.asarray(np.where(j < ROPE_DIM, inv_freq[j % half], 0.0).astype(np.float32).reshape(1, LANES))
    m_p = jnp.asarray(((j >= half) & (j < ROPE_DIM)).astype(np.float32).reshape(1, LANES))
    m_m = jnp.asarray(-(j < half).astype(np.float32).reshape(1, LANES))
    tm = _row_tile(seq, 512)

    def body(pos_ref, invf_ref, mp_ref, mm_ref, cos_ref, sp_ref, sm_ref):
        def block(i, carry):
            rows = pl.ds(pl.multiple_of(i * tm, tm), tm)
            ang = pos_ref[rows, :] * invf_ref[...]
            s = jnp.sin(ang)
            cos_ref[rows, :] = jnp.cos(ang)
            sp_ref[rows, :] = s * mp_ref[...]
            sm_ref[rows, :] = s * mm_ref[...]
            return carry

        lax.fori_loop(0, seq // tm, block, 0)

    return body, (posf, invf, m_p, m_m), [_sds((seq, LANES))] * 3


def _even_proj(x, mod, w_in_t, tabs, seq, rider=None):
    tm = _row_tile(seq, 512)

    def body(x_ref, mod_ref, w_ref, cos_ref, sp_ref, sm_ref, q_ref, k_ref, v_ref, su_ref, sv_ref, g_ref):
        h = x_ref[...] * (1.0 + mod_ref[1:2, :]) + mod_ref[0:1, :]
        p = _mm_nt(h, w_ref[...])
        cos_t, sin_p, sin_m = cos_ref[...], sp_ref[...], sm_ref[...]
        for j in range(ATTN_WIDTH // LANES):
            q_ref[:, j * LANES:(j + 1) * LANES] = _rope(p[:, j * LANES:(j + 1) * LANES], cos_t, sin_p, sin_m).astype(q_ref.dtype)
        k_ref[...] = _rope(p[:, 512:640], cos_t, sin_p, sin_m).astype(k_ref.dtype)
        v_ref[...] = p[:, 640:768].astype(v_ref.dtype)
        su_ref[...] = p[:, 768:1280].astype(su_ref.dtype)
        sv_ref[...] = p[:, 1280:1792].astype(sv_ref.dtype)
        g_ref[...] = p[:, 1792:2816].astype(g_ref.dtype)

    return _call(
        body, "even_proj", (seq // tm,),
        [_rows(tm, D_MODEL), _full((3, D_MODEL)), _const((EVEN_IN, D_MODEL))] + [_rows(tm, LANES)] * 3,
        [_rows(tm, 512), _rows(tm, LANES), _rows(tm, LANES), _rows(tm, 512), _rows(tm, 512), _rows(tm, D_MODEL)],
        [_sds((seq, 512), MXU_DTYPE), _sds((seq, LANES), MXU_DTYPE), _sds((seq, LANES), MXU_DTYPE), _sds((seq, 512), ACT_DTYPE),
         _sds((seq, 512), ACT_DTYPE), _sds((seq, D_MODEL), ACT_DTYPE)],
        (x, mod, w_in_t, *tabs), "parallel", rider=rider)


def _sg_forward(sv, lng, lnb, sgw_ref, sgb, e2):
    vn, vhat, rstd, svo = [], [], [], []
    for j in range(SG_WIDTH // LANES):
        t = sv[:, j * LANES:(j + 1) * LANES]
        mu = _group_sum(t, e2) * (1.0 / HEAD_DIM)
        d = t - mu
        var = _group_sum(d * d, e2) * (1.0 / HEAD_DIM)
        r = lax.rsqrt(var + LN_EPS)
        vh = d * r
        vhat.append(vh)
        rstd.append(r)
        vn.append(vh * lng[:, j * LANES:(j + 1) * LANES] + lnb[:, j * LANES:(j + 1) * LANES])
    lane = _lane_iota((BLK, LANES))
    for j in range(SG_WIDTH // LANES):
        lo = _mm(sgw_ref[2 * j], vn[j])
        hi = _mm(sgw_ref[2 * j + 1], vn[j])
        svo.append(jnp.where(lane < HEAD_DIM, lo, hi) + sgb[:, j * LANES:(j + 1) * LANES])
    return svo, vn, vhat, rstd


def _even_mix(q, k, v, su, sv, sink, sgln_g, sgln_b, sgw, sgb_full, e2, seq, rider=None):
    nb = seq // BLK

    def body(sink_ref, q_ref, k_ref, v_ref, su_ref, sv_ref, lng_ref, lnb_ref, sgw_ref, sgb_ref, e2_ref, ycat_ref, lse_ref,
             svo_ref, vhat_ref, rstd_ref):
        n = pl.program_id(0)
        kband = _band(k_ref, n, nb)
        vband = _band(v_ref, n, nb)
        bias = _band_bias(n, seq)
        lane = _lane_iota((BLK, LANES))
        lse = jnp.zeros((BLK, LANES), F32)
        q_tile = lambda j: q_ref[:, j * LANES:(j + 1) * LANES].astype(F32)
        acc = [jnp.zeros((BLK, LANES), F32) for _ in range(ATTN_WIDTH // LANES)]
        for kv in range(N_Q_HEADS // Q_PER_KV):
            heads = range(Q_PER_KV * kv, Q_PER_KV * (kv + 1))
            sink = _per_head_column([sink_ref[h] for h in heads])
            s = _mm_nt(_stack_heads(q_tile, kv), kband) * (HEAD_DIM ** -0.5) + bias
            m = jnp.maximum(jnp.max(s, axis=1, keepdims=True), sink)
            p = jnp.exp(s - m)
            denom = jnp.sum(p, axis=1, keepdims=True) + jnp.exp(sink - m)
            o4 = _mm(p / denom, vband)
            l4 = m + jnp.log(denom)
            for g, h in enumerate(heads):
                acc[h // 2] = acc[h // 2] + _from_kv_lanes(o4[g * BLK:(g + 1) * BLK], h)
                lse = jnp.where(lane == h, l4[g * BLK:(g + 1) * BLK], lse)
        for j in range(ATTN_WIDTH // LANES):
            ycat_ref[:, j * LANES:(j + 1) * LANES] = acc[j].astype(ycat_ref.dtype)
        lse_ref[...] = lse
        svo, _, vhat, rstd = _sg_forward(sv_ref[...].astype(F32), lng_ref[...], lnb_ref[...], sgw_ref, sgb_ref[...], e2_ref[...])
        for j in range(SG_WIDTH // LANES):
            cs = slice(j * LANES, (j + 1) * LANES)
            ysg = su_ref[:, cs].astype(F32) * svo[j]
            ycat_ref[:, ATTN_WIDTH + j * LANES:ATTN_WIDTH + (j + 1) * LANES] = ysg.astype(ycat_ref.dtype)
            svo_ref[:, cs], vhat_ref[:, cs], rstd_ref[:, cs] = (t.astype(svo_ref.dtype) for t in (svo[j], vhat[j], rstd[j]))

    blk = lambda w: pl.BlockSpec((BLK, w), lambda n: (n, 0))
    return _call(
        body, "even_mix", (nb,),
        [pl.BlockSpec(memory_space=pltpu.SMEM), blk(512), _full((seq, LANES)), _full((seq, LANES)), blk(512), blk(512),
         _full((1, 512)), _full((1, 512)), _full((8, BLK, BLK)), _full((BLK, 512)), _full((LANES, LANES))],
        [blk(D_MODEL), blk(LANES)] + [blk(SG_WIDTH)] * 3,
        [_sds((seq, D_MODEL), ACT_DTYPE), _sds((seq, LANES))] + [_sds((seq, SG_WIDTH), ACT_DTYPE)] * 3,
        (sink, q, k, v, su, sv, sgln_g, sgln_b, sgw, sgb_full, e2), "parallel", rider=rider)


def _even_out(ycat, g, x, mod, mod_next, w_out, w_in4_next, ln_g, ln_b, seq, rider=None):
    tm = _row_tile(seq, 512)
    cs = ODD_IN // 4

    def body(y_ref, g_ref, x_ref, mod_ref, modn_ref, wo_ref, wi_ref, g1_ref, b1_ref, zhat_ref, rstd_ref, x1_ref, xr_ref, gn_ref):
        gg = g_ref[...].astype(F32)
        out = _mm(y_ref[...].astype(F32) * (gg * _sigmoid(gg)), wo_ref[...])
        z = ALPHA * x_ref[...] + mod_ref[2:3, :] * out
        zhat, rstd = _ln_stats(z)
        zhat_ref[...] = zhat
        rstd_ref[...] = rstd
        x1 = zhat * g1_ref[...] + b1_ref[...]
        x1_ref[...] = x1
        hb = (x1 * (1.0 + modn_ref[1:2, :]) + modn_ref[0:1, :]).astype(MXU_DTYPE)
        for s in range(2):
            xr_ref[:, s * cs:(s + 1) * cs] = jnp.dot(hb, wi_ref[s], preferred_element_type=F32)
            gn_ref[:, s * cs:(s + 1) * cs] = jnp.dot(hb, wi_ref[2 + s], preferred_element_type=F32).astype(gn_ref.dtype)

    return _call(
        body, "even_out", (seq // tm,),
        [_rows(tm, D_MODEL)] * 3 + [_full((3, D_MODEL)), _full((3, D_MODEL)), _const((D_MODEL, D_MODEL)), _const((4, D_MODEL, cs)),
                                    _full((1, D_MODEL)), _full((1, D_MODEL))],
        [_rows(tm, D_MODEL), _rows(tm, 1)] + [_rows(tm, D_MODEL)] * 3,
        [_sds((seq, D_MODEL)), _sds((seq, 1))] + [_sds((seq, D_MODEL))] * 2 + [_sds((seq, D_MODEL), ACT_DTYPE)],
        (ycat, g, x, mod, mod_next, w_out, w_in4_next, ln_g, ln_b), "parallel", rider=rider)


def _halo_specs(tm, seq, width, order=lambda i: i):
    per = tm // 8
    last = seq // 8 - 1
    return [pl.BlockSpec((8, width), lambda i: (jnp.maximum(order(i) * per - 1, 0), 0)),
            pl.BlockSpec((tm, width), lambda i: (order(i), 0)),
            pl.BlockSpec((8, width), lambda i: (jnp.minimum((order(i) + 1) * per, last), 0))]


def _extended(prev_ref, main_ref, next_ref, i, n_steps):
    prev = jnp.where(i > 0, prev_ref[...], 0.0)
    nxt = jnp.where(i < n_steps - 1, next_ref[...], 0.0)
    return jnp.concatenate([prev, main_ref[...], nxt], axis=0)


def _shifted(ext, off, tm):
    if off == 0:
        return ext[8:8 + tm]
    return pltpu.roll(ext, (-off) % ext.shape[0], 0)[8:8 + tm]


SCAN_SUB = 8


def _lru_gate(xh, pre, bias, sp, hs, d):
    r = _sigmoid(pre[:, 0:LANES] + bias[2 * d:2 * d + 1, hs])
    ig = _sigmoid(pre[:, LANES:2 * LANES] + bias[2 * d + 1:2 * d + 2, hs])
    neg_log_a = RG_LRU_C * r * sp[d:d + 1, hs]
    a = jnp.exp(-neg_log_a)
    u = jnp.tanh(neg_log_a) * (a * a + 1.0)
    inv_s = lax.rsqrt(jnp.maximum(u, jnp.finfo(F32).tiny))
    return r, ig, a, u * inv_s, inv_s


def _conv_block(xp_ref, xm_ref, xn_ref, cw_ref, cb_ref, blk, steps, tm):
    ext = _extended(xp_ref, xm_ref, xn_ref, blk, steps)
    return cb_ref[...] + sum(cw_ref[kk:kk + 1, :] * _shifted(ext, kk - 2, tm) for kk in range(4))


def _scan_tiles(a_ref, b_ref, h_ref, hprev_ref, carry_h, carry_a, rows, descending, post):
    sub = SCAN_SUB
    tiles = rows // sub
    row = lax.broadcasted_iota(jnp.int32, (sub, D_MODEL), 0)

    def shift(v, d, fill):
        if descending:
            return jnp.where(row <= sub - 1 - d, pltpu.roll(v, sub - d, 0), fill)
        return jnp.where(row >= d, pltpu.roll(v, d, 0), fill)

    def last(v):
        return jnp.broadcast_to(v[0:1, :] if descending else v[sub - 1:sub, :], v.shape)

    def tile(j, c):
        ch, ca = c
        r0 = pl.multiple_of(((tiles - 1 - j) if descending else j) * sub, sub)
        at = a_ref[pl.ds(r0, sub), :]
        bt = b_ref[pl.ds(r0, sub), :]
        coef = shift(at, 1, ca) if post else at
        acc_a, acc_b = coef, bt
        for d in (1, 2, 4):
            acc_b = acc_b + acc_a * shift(acc_b, d, 0.0)
            acc_a = acc_a * shift(acc_a, d, 1.0)
        h = acc_b + acc_a * ch
        h_ref[pl.ds(r0, sub), :] = h
        if post:
            return last(h), last(at)
        hprev_ref[pl.ds(r0, sub), :] = shift(h, 1, ch)
        return last(h), ca

    ch, ca = lax.fori_loop(0, tiles, tile, (carry_h[...], carry_a[...]), unroll=4)
    carry_h[...] = ch
    carry_a[...] = ca


def _lru_fwd(xr, xc, conv_w, conv_b, wcat, bias, lam, seq, d):
    tb = _row_tile(seq, 512)
    steps = seq // tb
    descending = d == 1
    order = (lambda i: steps - 1 - i) if descending else (lambda i: i)
    with_conv = xc is None
    n_x = 5 if with_conv else 1

    def body(*refs):
        x_refs, (w_ref, bias_ref, lam_ref) = refs[:n_x], refs[n_x:n_x + 3]
        h_ref, hp_ref, a_ref, r_ref, i_ref, s_ref, q_ref = refs[n_x + 3:n_x + 10]
        b_scr, carry_h, carry_a = refs[-3:]
        i = pl.program_id(0)

        @pl.when(i == 0)
        def _():
            carry_h[...] = jnp.zeros_like(carry_h)
            carry_a[...] = jnp.zeros_like(carry_a)

        if with_conv:
            xc_ref = refs[n_x + 10]
            xc_ref[...] = _conv_block(*x_refs, order(i), steps, tb)
        else:
            xc_ref = x_refs[0]
        sp, _ = _softplus_neg(lam_ref[...])
        bias = bias_ref[...]
        for h in range(RNN_HEADS):
            hs = slice(h * LANES, (h + 1) * LANES)
            xh = xc_ref[:, hs]
            r, ig, a, s, q = _lru_gate(xh, _mm(xh, w_ref[h, :, 2 * d * LANES:2 * (d + 1) * LANES]), bias, sp, hs, d)
            a_ref[:, hs] = a
            b_scr[:, hs] = s * ig * xh
            for ref, val in ((r_ref, r), (i_ref, ig), (s_ref, s), (q_ref, q)):
                ref[:, hs] = val.astype(ref.dtype)
        _scan_tiles(a_ref, b_scr, h_ref, hp_ref, carry_h, carry_a, tb, descending, post=False)

    row_spec = pl.BlockSpec((tb, D_MODEL), lambda i: (order(i), 0))
    if with_conv:
        x_specs, x_args = _halo_specs(tb, seq, D_MODEL, order) + [_full((4, D_MODEL)), _full((1, D_MODEL))], (xr, xr, xr, conv_w, conv_b)
    else:
        x_specs, x_args = [row_spec], (xc,)
    n_out = 8 if with_conv else 7
    return pl.pallas_call(
        body, name="lru_fwd_%d" % d, grid=(steps,),
        in_specs=x_specs + [_full((8, LANES, 512)), _full((4, D_MODEL)), _full((2, D_MODEL))],
        out_specs=[row_spec] * n_out,
        out_shape=[_sds((seq, D_MODEL))] * 3 + [_sds((seq, D_MODEL), ACT_DTYPE)] * 4 + [_sds((seq, D_MODEL))] * (n_out - 7),
        scratch_shapes=[pltpu.VMEM((tb, D_MODEL), F32)] + [pltpu.VMEM((SCAN_SUB, D_MODEL), F32)] * 2,
        compiler_params=_params("arbitrary"),
    )(*x_args, wcat, bias, lam)


def _odd_out_and_loss(hf, hr, g, x1, tgt, mod, w_out, ln_g, ln_b, seq):
    tm = _row_tile(seq, 512)

    def body(hf_ref, hr_ref, g_ref, x_ref, t_ref, mod_ref, w_ref, lg_ref, lb_ref,
             dhs_ref, dg_ref, dres_ref, loss_ref, dw_ref, vec_ref):
        @pl.when(pl.program_id(0) == 0)
        def _():
            loss_ref[...] = jnp.zeros_like(loss_ref)
            dw_ref[...] = jnp.zeros_like(dw_ref)
            vec_ref[...] = jnp.zeros_like(vec_ref)

        gg = g_ref[...].astype(F32)
        sg = _sigmoid(gg)
        silu = gg * sg
        hsum = hf_ref[...] + hr_ref[...]
        y = hsum * silu
        out = _mm(y, w_ref[...])
        gate = mod_ref[2:3, :]
        z = ALPHA * x_ref[...] + gate * out
        zhat, rstd = _ln_stats(z)
        x2 = zhat * lg_ref[...] + lb_ref[...]
        err = x2 - t_ref[...]
        loss_ref[...] += 0.5 * jnp.sum(jnp.mean(err * err, axis=-1, keepdims=True))
        dx2 = err * (1.0 / D_MODEL)
        dz = _ln_bwd(dx2, zhat, rstd, lg_ref[...])
        vec_ref[0:1, :] += jnp.sum(dx2 * zhat, axis=0, keepdims=True)
        vec_ref[1:2, :] += jnp.sum(dx2, axis=0, keepdims=True)
        vec_ref[2:3, :] += jnp.sum(dz * out, axis=0, keepdims=True)
        dres_ref[...] = ALPHA * dz
        dout = gate * dz
        dw_ref[...] += _mm_tn(y, dout)
        dy = _mm_nt(dout, w_ref[...])
        dhs_ref[...] = dy * silu
        dg_ref[...] = (dy * hsum * (sg * (1.0 + gg * (1.0 - sg)))).astype(dg_ref.dtype)

    return pl.pallas_call(
        body, name="odd_out_loss", grid=(seq // tm,),
        in_specs=[_rows(tm, D_MODEL)] * 5 + [_full((3, D_MODEL)), _const((D_MODEL, D_MODEL)),
                                             _full((1, D_MODEL)), _full((1, D_MODEL))],
        out_specs=[_rows(tm, D_MODEL)] * 3 + [_full((8, LANES)), _full((D_MODEL, D_MODEL)), _full((8, D_MODEL))],
        out_shape=[_sds((seq, D_MODEL)), _sds((seq, D_MODEL), ACT_DTYPE), _sds((seq, D_MODEL)), _sds((8, LANES)),
                   _sds((D_MODEL, D_MODEL)), _sds((8, D_MODEL))],
        compiler_params=_params("arbitrary"),
    )(hf, hr, g, x1, tgt, mod, w_out, ln_g, ln_b)


def _lru_bwd(xc, dhs, hprev, a_d, r_d, i_d, s_d, q_d, wcat, lam, seq, d):
    tb = _row_tile(seq, 512)
    steps = seq // tb
    descending = d == 0
    order = (lambda i: steps - 1 - i) if descending else (lambda i: i)
    cols = slice(2 * d * LANES, 2 * (d + 1) * LANES)

    def body(xc_ref, dhs_ref, hp_ref, a_ref, r_ref, i_ref, s_ref, q_ref, w_ref, lam_ref, dxc_ref, dw_ref, vec_ref,
             g_scr, carry_h, carry_a):
        i = pl.program_id(0)

        @pl.when(i == 0)
        def _():
            dw_ref[...] = jnp.zeros_like(dw_ref)
            vec_ref[...] = jnp.zeros_like(vec_ref)
            carry_h[...] = jnp.zeros_like(carry_h)
            carry_a[...] = jnp.zeros_like(carry_a)

        sp, dsp = _softplus_neg(lam_ref[...])
        _scan_tiles(a_ref, dhs_ref, g_scr, None, carry_h, carry_a, tb, descending, post=True)
        for h in range(RNN_HEADS):
            hs = slice(h * LANES, (h + 1) * LANES)
            xh, a = xc_ref[:, hs], a_ref[:, hs]
            r, ig, s = r_ref[:, hs].astype(F32), i_ref[:, hs].astype(F32), s_ref[:, hs].astype(F32)
            db = g_scr[:, hs]
            da = db * hp_ref[:, hs]
            dlog_a = da * a - (db * ig * xh) * (a * a * q_ref[:, hs].astype(F32))
            dpr = dlog_a * (-RG_LRU_C) * sp[d:d + 1, hs] * r * (1.0 - r)
            dpi = db * s * xh * ig * (1.0 - ig)
            vec_ref[0:1, hs] += jnp.sum(dpr, axis=0, keepdims=True)
            vec_ref[1:2, hs] += jnp.sum(dpi, axis=0, keepdims=True)
            vec_ref[2:3, hs] += jnp.sum(dlog_a * r, axis=0, keepdims=True) * (-RG_LRU_C) * dsp[d:d + 1, hs]
            dcat = jnp.concatenate([dpr, dpi], axis=1)
            dw_ref[h] += _mm_tn(xh, dcat)
            dxc_ref[:, hs] = db * s * ig + _mm_nt(dcat, w_ref[h, :, cols])

    row_spec = pl.BlockSpec((tb, D_MODEL), lambda i: (order(i), 0))
    return pl.pallas_call(
        body, name="lru_bwd_%d" % d, grid=(steps,),
        in_specs=[row_spec] * 8 + [_full((8, LANES, 512)), _full((2, D_MODEL))],
        out_specs=[row_spec, _full((8, LANES, 2 * LANES)), _full((8, D_MODEL))],
        out_shape=[_sds((seq, D_MODEL)), _sds((8, LANES, 2 * LANES)), _sds((8, D_MODEL))],
        scratch_shapes=[pltpu.VMEM((tb, D_MODEL), F32)] + [pltpu.VMEM((SCAN_SUB, D_MODEL), F32)] * 2,
        compiler_params=_params("arbitrary"),
    )(xc, dhs, hprev, a_d, r_d, i_d, s_d, q_d, wcat, lam)


def _odd_proj_bwd(dxc_f, dxc_r, xr, dg, x1, dres, mod, conv_w, w_in4, seq):
    tm = _row_tile(seq, 512)
    steps = seq // tm

    def body(fp_ref, fm_ref, fn_ref, rp_ref, rm_ref, rn_ref, xp_ref, xm_ref, xn_ref, dg_ref, x_ref, dres_ref, mod_ref, cw_ref,
             w_ref, dx_ref, dw_ref, vec_ref, dpb_ref):
        i = pl.program_id(0)

        @pl.when(i == 0)
        def _():
            vec_ref[...] = jnp.zeros_like(vec_ref)
            dw_ref[...] = jnp.zeros_like(dw_ref)

        dxc_m = fm_ref[...] + rm_ref[...]
        dext = jnp.concatenate([jnp.where(i > 0, fp_ref[...] + rp_ref[...], 0.0), dxc_m,
                                jnp.where(i < steps - 1, fn_ref[...] + rn_ref[...], 0.0)], axis=0)
        xext = _extended(xp_ref, xm_ref, xn_ref, i, steps)
        dxr = sum(cw_ref[kk:kk + 1, :] * _shifted(dext, 2 - kk, tm) for kk in range(4))
        for kk in range(4):
            vec_ref[kk:kk + 1, :] += jnp.sum(dxc_m * _shifted(xext, kk - 2, tm), axis=0, keepdims=True)
        vec_ref[4:5, :] += jnp.sum(dxc_m, axis=0, keepdims=True)
        dpb_ref[:, :D_MODEL] = dxr.astype(dpb_ref.dtype)
        dpb_ref[:, D_MODEL:] = dg_ref[...].astype(dpb_ref.dtype)
        cs = ODD_IN // 4
        dh = sum(_mm_nt(dpb_ref[:, s * cs:(s + 1) * cs], w_ref[s]) for s in range(4))
        x = x_ref[...]
        h_t = (x * (1.0 + mod_ref[1:2, :]) + mod_ref[0:1, :]).T.astype(MXU_DTYPE)
        for s in range(4):
            dw_ref[s] += jnp.dot(h_t, dpb_ref[:, s * cs:(s + 1) * cs], preferred_element_type=F32)
        vec_ref[5:6, :] += jnp.sum(dh, axis=0, keepdims=True)
        vec_ref[6:7, :] += jnp.sum(dh * x, axis=0, keepdims=True)
        dx_ref[...] = dres_ref[...] + dh * (1.0 + mod_ref[1:2, :])

    return pl.pallas_call(
        body, name="odd_proj_bwd", grid=(steps,),
        in_specs=_halo_specs(tm, seq, D_MODEL) * 3 + [_rows(tm, D_MODEL)] * 3
        + [_full((3, D_MODEL)), _full((4, D_MODEL)), _const((4, D_MODEL, ODD_IN // 4))],
        out_specs=[_rows(tm, D_MODEL), _const((4, D_MODEL, ODD_IN // 4)), _full((8, D_MODEL))],
        out_shape=[_sds((seq, D_MODEL)), _sds((4, D_MODEL, ODD_IN // 4)), _sds((8, D_MODEL))],
        scratch_shapes=[pltpu.VMEM((tm, ODD_IN), MXU_DTYPE)],
        compiler_params=_params("arbitrary"),
    )(dxc_f, dxc_f, dxc_f, dxc_r, dxc_r, dxc_r, xr, xr, xr, dg, x1, dres, mod, conv_w, w_in4)


def _even_out_bwd(dx1, zhat, rstd, ycat, g, mod, ln_g, w_out, seq, rider=None):
    tm = _row_tile(seq, 512)
    steps = seq // tm

    def body(dx_ref, zh_ref, rs_ref, y_ref, g_ref, mod_ref, lg_ref, w_ref, dy_ref, dg_ref, dres_ref, dw_ref, vec_ref):
        i = pl.program_id(0)

        @pl.when(i == 0)
        def _():
            dw_ref[...] = jnp.zeros_like(dw_ref)
            vec_ref[...] = jnp.zeros_like(vec_ref)

        zhat = zh_ref[...]
        dx1_ = dx_ref[...]
        dz = _ln_bwd(dx1_, zhat, rs_ref[...], lg_ref[...])
        vec_ref[0:1, :] += jnp.sum(dx1_ * zhat, axis=0, keepdims=True)
        vec_ref[1:2, :] += jnp.sum(dx1_, axis=0, keepdims=True)
        dres_ref[...] = ALPHA * dz
        gate = mod_ref[2:3, :]
        gg = g_ref[...].astype(F32)
        sg = _sigmoid(gg)
        silu = gg * sg
        ycat_ = y_ref[...].astype(F32)
        dw_ref[...] += _mm_tn(ycat_ * silu, dz)
        dy = _mm_nt(gate * dz, w_ref[...])
        dy_ref[...] = (dy * silu).astype(dy_ref.dtype)
        dg_ref[...] = (dy * ycat_ * (sg * (1.0 + gg * (1.0 - sg)))).astype(dg_ref.dtype)

        @pl.when(i == steps - 1)
        def _():
            m_acc = dw_ref[...]
            vec_ref[2:3, :] = jnp.sum(w_ref[...].astype(F32) * m_acc, axis=0, keepdims=True)
            dw_ref[...] = m_acc * gate

    return _call(
        body, "even_out_bwd", (steps,),
        [_rows(tm, D_MODEL), _rows(tm, D_MODEL), _rows(tm, 1), _rows(tm, D_MODEL), _rows(tm, D_MODEL), _full((3, D_MODEL)),
         _full((1, D_MODEL)), _const((D_MODEL, D_MODEL))],
        [_rows(tm, D_MODEL)] * 3 + [_full((D_MODEL, D_MODEL)), _full((8, D_MODEL))],
        [_sds((seq, D_MODEL), ACT_DTYPE), _sds((seq, D_MODEL), ACT_DTYPE), _sds((seq, D_MODEL)), _sds((D_MODEL, D_MODEL)),
         _sds((8, D_MODEL))],
        (dx1, zhat, rstd, ycat, g, mod, ln_g, w_out), "arbitrary", rider=rider)


def _even_mix_bwd(q, k, v, lse, ycat, dycat, su, svo_s, vhat_s, rstd_s, sink, sgln_g, sgln_b, sgw, e2, e8, seq, rider=None):
    nb = seq // BLK

    def body(sink_ref, q_ref, k_ref, v_ref, lse_ref, y_ref, dy_ref, su_ref, svo_ref, vhat_ref, rstd_ref, lng_ref, lnb_ref, sgw_ref,
             e2_ref, e8_ref, dq_ref, dsu_ref, dsv_ref, dk_ref, dv_ref, dsgw_ref, dsgb_ref, vec_ref, dsink_ref, dsgb_acc):
        n = pl.program_id(0)

        @pl.when(n == 0)
        def _():
            dk_ref[...] = jnp.zeros_like(dk_ref)
            dv_ref[...] = jnp.zeros_like(dv_ref)
            dsgw_ref[...] = jnp.zeros_like(dsgw_ref)
            dsgb_acc[...] = jnp.zeros_like(dsgb_acc)
            vec_ref[...] = jnp.zeros_like(vec_ref)
            dsink_ref[...] = jnp.zeros_like(dsink_ref)

        kband = _band(k_ref, n, nb)
        vband = _band(v_ref, n, nb)
        bias = _band_bias(n, seq)
        lane = _lane_iota((BLK, LANES))
        row8 = lax.broadcasted_iota(jnp.int32, (8, LANES), 0)
        lse = lse_ref[...]
        dkb = jnp.zeros((LANES, 3 * BLK), F32)
        dvb = jnp.zeros((LANES, 3 * BLK), F32)
        dsink = jnp.zeros((8, LANES), F32)
        q_tile = lambda j: q_ref[:, j * LANES:(j + 1) * LANES].astype(F32)
        do_tile = lambda j: dy_ref[:, j * LANES:(j + 1) * LANES].astype(F32)
        dq = [jnp.zeros((BLK, LANES), F32) for _ in range(ATTN_WIDTH // LANES)]
        for kv in range(N_Q_HEADS // Q_PER_KV):
            heads = range(Q_PER_KV * kv, Q_PER_KV * (kv + 1))
            lse4, delta4 = [], []
            for h in heads:
                head_lanes = (lane < HEAD_DIM) if h % 2 == 0 else (lane >= HEAD_DIM)
                lse4.append(jnp.sum(jnp.where(lane == h, lse, 0.0), axis=1, keepdims=True))
                o_tile = y_ref[:, (h // 2) * LANES:(h // 2 + 1) * LANES].astype(F32)
                delta4.append(jnp.sum(jnp.where(head_lanes, do_tile(h // 2) * o_tile, 0.0), axis=1, keepdims=True))
            lse4, delta4 = jnp.concatenate(lse4, axis=0), jnp.concatenate(delta4, axis=0)
            q4, do4 = _stack_heads(q_tile, kv), _stack_heads(do_tile, kv)
            s = _mm_nt(q4, kband) * (HEAD_DIM ** -0.5) + bias
            p = jnp.exp(s - lse4)
            wsink = jnp.exp(_per_head_column([sink_ref[h] for h in heads]) - lse4) * delta4
            ds = p * (_mm_nt(do4, vband) - delta4) * (HEAD_DIM ** -0.5)
            dq4 = _mm(ds, kband)
            dkb = dkb + _mm_tn(q4, ds)
            dvb = dvb + _mm_tn(do4, p)
            for g, h in enumerate(heads):
                dq[h // 2] = dq[h // 2] + _from_kv_lanes(dq4[g * BLK:(g + 1) * BLK], h)
                dsink = dsink + jnp.where(row8 == h, -jnp.sum(wsink[g * BLK:(g + 1) * BLK]), 0.0)
        for j in range(ATTN_WIDTH // LANES):
            dq_ref[:, j * LANES:(j + 1) * LANES] = dq[j].astype(dq_ref.dtype)
        dsink_ref[...] += dsink
        prev = jnp.maximum(n - 1, 0)
        nxt = jnp.minimum(n + 1, nb - 1)
        for part, blk_i in enumerate((prev, n, nxt)):
            rows = pl.ds(pl.multiple_of(blk_i * BLK, BLK), BLK)
            dk_ref[rows, :] += dkb[:, part * BLK:(part + 1) * BLK].T
            dv_ref[rows, :] += dvb[:, part * BLK:(part + 1) * BLK].T

        e2 = e2_ref[...]
        lng, lnb = lng_ref[...], lnb_ref[...]
        for j in range(SG_WIDTH // LANES):
            cs = slice(j * LANES, (j + 1) * LANES)
            vhat = vhat_ref[:, cs].astype(F32)
            vn = vhat * lng[:, cs] + lnb[:, cs]
            dysg = dy_ref[:, ATTN_WIDTH + j * LANES:ATTN_WIDTH + (j + 1) * LANES].astype(F32)
            dsu_ref[:, cs] = (dysg * svo_ref[:, cs].astype(F32)).astype(dsu_ref.dtype)
            dsvo = dysg * su_ref[:, cs].astype(F32)
            dsgb_acc[:, cs] += dsvo
            d_lo = jnp.where(lane < HEAD_DIM, dsvo, 0.0)
            d_hi = dsvo - d_lo
            dsgw_ref[2 * j] += _mm_nt(d_lo, vn)
            dsgw_ref[2 * j + 1] += _mm_nt(d_hi, vn)
            dvn = _mm_tn(sgw_ref[2 * j], d_lo) + _mm_tn(sgw_ref[2 * j + 1], d_hi)
            vec_ref[0:1, cs] += jnp.sum(dvn * vhat, axis=0, keepdims=True)
            vec_ref[1:2, cs] += jnp.sum(dvn, axis=0, keepdims=True)
            dvh = dvn * lng[:, cs]
            m1 = _group_sum(dvh, e2) * (1.0 / HEAD_DIM)
            m2 = _group_sum(dvh * vhat, e2) * (1.0 / HEAD_DIM)
            dsv_ref[:, cs] = (rstd_ref[:, cs].astype(F32) * (dvh - m1 - vhat * m2)).astype(dsv_ref.dtype)

        @pl.when(n == nb - 1)
        def _():
            rest = dsgb_acc[...]
            total = jnp.zeros((8, BLK), F32)
            for _ in range(3):
                part = rest.astype(MXU_DTYPE)
                total = total + lax.dot_general(e8_ref[...], part, (((1,), (1,)), ((), ())), preferred_element_type=F32)
                rest = rest - part.astype(F32)
            dsgb_ref[...] = total

    blk = lambda w: pl.BlockSpec((BLK, w), lambda n: (n, 0))
    return _call(
        body, "even_mix_bwd", (nb,),
        [pl.BlockSpec(memory_space=pltpu.SMEM), blk(512), _full((seq, LANES)), _full((seq, LANES)), blk(LANES),
         blk(D_MODEL), blk(D_MODEL), blk(512), blk(512), blk(512), blk(512), _full((1, 512)), _full((1, 512)), _full((8, BLK, BLK)),
         _full((LANES, LANES)), _full((8, 512))],
        [blk(512), blk(512), blk(512), _full((seq, LANES)), _full((seq, LANES)), _full((8, BLK, BLK)),
         _full((8, BLK)), _full((8, 512)), _full((8, LANES))],
        [_sds((seq, 512), ACT_DTYPE), _sds((seq, 512), ACT_DTYPE), _sds((seq, 512), ACT_DTYPE), _sds((seq, LANES)), _sds((seq, LANES)),
         _sds((8, BLK, BLK)), _sds((8, BLK)), _sds((8, 512)), _sds((8, LANES))],
        (sink, q, k, v, lse, ycat, dycat, su, svo_s, vhat_s, rstd_s, sgln_g, sgln_b, sgw, e2, e8), "arbitrary",
        scratch=[pltpu.VMEM((BLK, 512), F32)], rider=rider)


def _even_proj_bwd(dq, dk, dv, dsu, dsv, dg, x, dres, mod, tabs, w_in_t, seq):
    tm = _row_tile(seq, 512)

    def body(dq_ref, dk_ref, dv_ref, dsu_ref, dsv_ref, dg_ref, x_ref, dres_ref, mod_ref, cos_ref, sp_ref, sm_ref, wt_ref,
             dx_ref, dw_ref, vec_ref, dpb_ref):
        @pl.when(pl.program_id(0) == 0)
        def _():
            vec_ref[...] = jnp.zeros_like(vec_ref)
            dw_ref[...] = jnp.zeros_like(dw_ref)

        cos_t, sin_p, sin_m = cos_ref[...], sp_ref[...], sm_ref[...]
        dt = dpb_ref.dtype
        for j in range(ATTN_WIDTH // LANES):
            cs = slice(j * LANES, (j + 1) * LANES)
            dpb_ref[:, cs] = _rope_t(dq_ref[:, cs].astype(F32), cos_t, sin_p, sin_m).astype(dt)
        dpb_ref[:, 512:640] = _rope_t(dk_ref[...], cos_t, sin_p, sin_m).astype(dt)
        dpb_ref[:, 640:768] = dv_ref[...].astype(dt)
        dpb_ref[:, 768:1280] = dsu_ref[...].astype(dt)
        dpb_ref[:, 1280:1792] = dsv_ref[...].astype(dt)
        dpb_ref[:, 1792:2816] = dg_ref[...].astype(dt)
        dpb = dpb_ref[...]
        dh = jnp.dot(dpb, wt_ref[...], preferred_element_type=F32)
        x_ = x_ref[...]
        hb = (x_ * (1.0 + mod_ref[1:2, :]) + mod_ref[0:1, :]).astype(MXU_DTYPE)
        dw_ref[...] += _mm_tn(dpb, hb)
        vec_ref[0:1, :] += jnp.sum(dh, axis=0, keepdims=True)
        vec_ref[1:2, :] += jnp.sum(dh * x_, axis=0, keepdims=True)
        dx_ref[...] = dres_ref[...] + dh * (1.0 + mod_ref[1:2, :])

    return pl.pallas_call(
        body, name="even_proj_bwd", grid=(seq // tm,),
        in_specs=[_rows(tm, 512), _rows(tm, LANES), _rows(tm, LANES), _rows(tm, 512), _rows(tm, 512), _rows(tm, D_MODEL),
                  _rows(tm, D_MODEL), _rows(tm, D_MODEL), _full((3, D_MODEL))] + [_rows(tm, LANES)] * 3
        + [_const((EVEN_IN, D_MODEL))],
        out_specs=[_rows(tm, D_MODEL), _const((EVEN_IN, D_MODEL)), _full((8, D_MODEL))],
        out_shape=[_sds((seq, D_MODEL)), _sds((EVEN_IN, D_MODEL)), _sds((8, D_MODEL))],
        scratch_shapes=[pltpu.VMEM((tm, EVEN_IN), MXU_DTYPE)],
        compiler_params=_params("arbitrary"),
    )(dq, dk, dv, dsu, dsv, dg, x, dres, mod, *tabs, w_in_t)


def _local_step(x, tabs, tgt, mod, w, seq, ride=None):
    rid = lambda make, *a: None if ride is None else make(*a)
    mxu = lambda a: a.astype(MXU_DTYPE)
    row = lambda a: a.reshape(1, -1)
    e2 = mxu(jnp.kron(jnp.eye(2, dtype=F32), jnp.ones((HEAD_DIM, HEAD_DIM), F32)))
    e8 = mxu(jnp.repeat(jnp.eye(N_SG_GROUPS, dtype=F32), HEAD_DIM, axis=1))
    sgw = mxu(w["ev_sg_w"])
    sgb_full = jnp.repeat(w["ev_sg_b"].T, HEAD_DIM, axis=1)
    sgln_g, sgln_b = row(w["ev_sg_ln_g"]), row(w["ev_sg_ln_b"])
    sink = w["ev_sink"].reshape(N_Q_HEADS)
    ev_w_in_t = mxu(w["ev_w_in_t"])
    if ride is None:
        ev_w_out, od_w_in, od_w_out = mxu(w["ev_w_out"]), mxu(w["od_w_in"]), mxu(w["od_w_out"])
    wcat = mxu(jnp.concatenate([w["od_w_a"][0], w["od_w_x"][0], w["od_w_a"][1], w["od_w_x"][1]], axis=2))
    gate_bias = jnp.stack([w["od_b_a"][0], w["od_b_x"][0], w["od_b_a"][1], w["od_b_x"][1]])
    conv_b = row(w["od_conv_b"])
    ln_g, ln_b = w["ln_g"], w["ln_b"]

    (q, k, v, su, sv, g0), got = _even_proj(x, mod[0], ev_w_in_t, tabs, seq, rid(_gather_rider, ride and ride["ev_w_out"]))
    if ride is not None:
        ev_w_out = got[0].reshape(D_MODEL, D_MODEL)
    (ycat, lse, *sg_saved), got = _even_mix(q, k, v, su, sv, sink, sgln_g, sgln_b, sgw, sgb_full, e2, seq,
                                 rid(_gather_rider, ride and ride["od_w_in"]))
    if ride is not None:
        od_w_in = got[0]
    (zhat0, rstd0, x1, xr, g1), got = _even_out(ycat, g0, x, mod[0], mod[1], ev_w_out, od_w_in, ln_g[0:1], ln_b[0:1], seq,
                                      rid(_gather_rider, ride and ride["od_w_out"]))
    if ride is not None:
        od_w_out = got[0].reshape(D_MODEL, D_MODEL)
    lru = (w["od_conv_w"], conv_b, wcat, gate_bias, w["od_lam"], seq)
    hf, hpf, *saved_f, xc = _lru_fwd(xr, None, *lru, 0)
    hr, hpr, *saved_r = _lru_fwd(xr, xc, *lru, 1)
    dhs, dg1, dres1, loss, d_od_w_out, vec_o = _odd_out_and_loss(hf, hr, g1, x1, tgt, mod[1], od_w_out, ln_g[1:2], ln_b[1:2], seq)
    dxc_f, dw_f, vec_f = _lru_bwd(xc, dhs, hpf, *saved_f, wcat, w["od_lam"], seq, 0)
    dxc_r, dw_r, vec_r = _lru_bwd(xc, dhs, hpr, *saved_r, wcat, w["od_lam"], seq, 1)
    dx1, d_od_w_in, vec_p = _odd_proj_bwd(dxc_f, dxc_r, xr, dg1, x1, dres1, mod[1], w["od_conv_w"], od_w_in, seq)
    d_od_w_a = jnp.stack([dw_f[:, :, 0:128], dw_r[:, :, 0:128]])
    d_od_w_x = jnp.stack([dw_f[:, :, 128:256], dw_r[:, :, 128:256]])
    od_parts = [d_od_w_in.reshape(4, 2, 512, 512), d_od_w_out.reshape(4, 2, 128, D_MODEL),
                d_od_w_a.reshape(4, 2, 2 * BLK, BLK), d_od_w_x.reshape(4, 2, 2 * BLK, BLK)]
    (dycat, dg0, dres0, d_ev_w_out, vec_e), got_od = _even_out_bwd(dx1, zhat0, rstd0, ycat, g0, mod[0], ln_g[0:1], ev_w_out, seq,
                                                                   rid(_sibling_swap_rider, od_parts))
    if ride is not None:
        od_sums = _sum_sibling(ride["core"], od_parts, got_od, [ride["wire"]] * 4, "sum_sibling_od")
    (dq, dsu, dsv, dk, dv, d_sgw, d_sgb, vec_s, d_sink), od_slots = _even_mix_bwd(
        q, k, v, lse, ycat, dycat, su, *sg_saved, sink, sgln_g, sgln_b, sgw, e2, e8, seq,
        rid(_chip_exchange_rider, ride and od_sums))
    grad_x, d_ev_w_in_t, vec_x = _even_proj_bwd(dq, dk, dv, dsu, dsv, dg0, x, dres0, mod[0], tabs, ev_w_in_t, seq)

    rows, dmod_blk = _pack_small(vec_x, vec_e, vec_p, vec_o, vec_f, vec_r, vec_s, d_sink, d_sgb, loss)
    grads = {"rows": rows, "dmod_blk": dmod_blk, "ev_w_in_t": d_ev_w_in_t, "ev_w_out": d_ev_w_out, "ev_sg_w": d_sgw}
    if ride is None:
        grads.update({"od_w_in": d_od_w_in, "od_w_out": d_od_w_out, "od_w_a": d_od_w_a, "od_w_x": d_od_w_x})
    else:
        grads["od_slots"] = od_slots
    return grad_x, grads


ROW_DMOD, ROW_LN, ROW_SG_LN, ROW_SG_B, ROW_CONV_W, ROW_CONV_B, ROW_B_A, ROW_B_X, ROW_LAM, ROW_SINK, ROW_LOSS = (
    0, 6, 10, 11, 12, 16, 17, 19, 21, 23, 24)
SMALL_ROWS = 64


def _pack_small(vec_x, vec_e, vec_p, vec_o, vec_f, vec_r, vec_s, d_sink, d_sgb, loss):
    def body(x_ref, e_ref, p_ref, o_ref, f_ref, r_ref, s_ref, sink_ref, sgb_ref, loss_ref, rows_ref, dmod_ref):
        rows_ref[...] = jnp.zeros_like(rows_ref)
        dmod_ref[...] = jnp.zeros_like(dmod_ref)
        put = [(ROW_DMOD, x_ref, 0), (ROW_DMOD + 1, x_ref, 1), (ROW_DMOD + 2, e_ref, 2), (ROW_DMOD + 3, p_ref, 5),
               (ROW_DMOD + 4, p_ref, 6), (ROW_DMOD + 5, o_ref, 2), (ROW_LN, e_ref, 0), (ROW_LN + 1, e_ref, 1),
               (ROW_LN + 2, o_ref, 0), (ROW_LN + 3, o_ref, 1), (ROW_CONV_B, p_ref, 4), (ROW_B_A, f_ref, 0),
               (ROW_B_A + 1, r_ref, 0), (ROW_B_X, f_ref, 1), (ROW_B_X + 1, r_ref, 1), (ROW_LAM, f_ref, 2), (ROW_LAM + 1, r_ref, 2)]
        put += [(ROW_CONV_W + k, p_ref, k) for k in range(4)]
        for dst, ref, src in put:
            rows_ref[dst:dst + 1, :] = ref[src:src + 1, :]
            if dst < 6:
                dmod_ref[dst:dst + 1, :] = ref[src:src + 1, :]
        rows_ref[ROW_SG_LN:ROW_SG_LN + 1, 0:SG_WIDTH] = s_ref[0:1, :]
        rows_ref[ROW_SG_LN:ROW_SG_LN + 1, SG_WIDTH:2 * SG_WIDTH] = s_ref[1:2, :]
        lane = _lane_iota((1, LANES))
        sink = jnp.zeros((1, LANES), F32)
        for h in range(N_Q_HEADS):
            rows_ref[ROW_SG_B:ROW_SG_B + 1, h * LANES:(h + 1) * LANES] = sgb_ref[h:h + 1, :]
            sink = jnp.where(lane == h, sink_ref[h:h + 1, :], sink)
        rows_ref[ROW_SINK:ROW_SINK + 1, 0:LANES] = sink
        rows_ref[ROW_LOSS:ROW_LOSS + 1, 0:LANES] = jnp.where(lane == 0, loss_ref[0:1, :], 0.0)

    return pl.pallas_call(body, name="pack_small", out_shape=[_sds((SMALL_ROWS, D_MODEL)), _sds((8, D_MODEL))])(
        vec_x, vec_e, vec_p, vec_o, vec_f, vec_r, vec_s, d_sink, d_sgb, loss)


def _allgather8(block, name):
    m_per, n = block.shape

    def body(x_ref, out_ref, send_sems, recv_sems, local_sem):
        x, y, c = _place()
        me, sibling = (x, y, c), (x, y, 1 - c)
        chips = [(1 - x, y), (x, 1 - y), (1 - x, 1 - y)]

        def rows(px, py, pc):
            return out_ref.at[pl.ds((4 * px + 2 * py + pc) * m_per, m_per), :]

        def copy(k, blk, to, src=None):
            return pltpu.make_async_remote_copy(src_ref=rows(*blk) if src is None else src, dst_ref=rows(*blk),
                                                send_sem=send_sems.at[k], recv_sem=recv_sems.at[k], device_id=to,
                                                device_id_type=MESH)

        mine = pltpu.make_async_copy(x_ref, rows(*me), local_sem)
        mine.start()
        first = [copy(0, me, sibling, src=x_ref)] + [copy(1 + j, me, (*chip, c), src=x_ref) for j, chip in enumerate(chips)]
        for cp in first:
            cp.start()
        passed = [copy(4 + j, (*chip, c), sibling) for j, chip in enumerate(chips)]
        for j, chip in enumerate(chips):
            copy(1 + j, (*chip, c), me).wait_recv()
            passed[j].start()
        copy(0, sibling, me).wait_recv()
        for j, chip in enumerate(chips):
            copy(4 + j, (*chip, 1 - c), me).wait_recv()
        for cp in first + passed:
            cp.wait_send()
        mine.wait()

    return pl.pallas_call(
        body, name=name, out_shape=_sds((8 * m_per, n), block.dtype),
        in_specs=[pl.BlockSpec(memory_space=pltpu.VMEM)], out_specs=pl.BlockSpec(memory_space=pltpu.VMEM),
        scratch_shapes=[pltpu.SemaphoreType.DMA((7,)), pltpu.SemaphoreType.DMA((7,)), pltpu.SemaphoreType.DMA],
        compiler_params=pltpu.CompilerParams(vmem_limit_bytes=VMEM_LIMIT),
    )(block)


class _Copies:
    def __init__(self, send_sems, recv_sems, local_sems, stages):
        self.send_sems, self.recv_sems, self.local_sems, self.stages = send_sems, recv_sems, local_sems, stages
        self.sent, self.staged, self.locals = [], [], []

    def remote(self, k, src, dst, to):
        return pltpu.make_async_remote_copy(src_ref=src, dst_ref=dst, send_sem=self.send_sems.at[k], recv_sem=self.recv_sems.at[k],
                                            device_id=to, device_id_type=MESH)

    def send(self, k, src, dst, to):
        cp = self.remote(k, src, dst, to)
        cp.start()
        self.sent.append(cp)

    def arrived(self, k, dst, frm):
        self.remote(k, dst, dst, frm).wait_recv()

    def local(self, src, dst):
        k = len(self.staged)
        cp = pltpu.make_async_copy(src, self.stages[k], self.local_sems.at[2 * k])
        cp.start()
        self.staged.append((cp, dst))

    def flush(self):
        for k in range(len(self.locals), len(self.staged)):
            cp, dst = self.staged[k]
            cp.wait()
            out = pltpu.make_async_copy(self.stages[k], dst, self.local_sems.at[2 * k + 1])
            out.start()
            self.locals.append(out)

    def drain(self):
        self.flush()
        for cp in self.sent:
            cp.wait_send()
        for cp in self.locals:
            cp.wait()


def _comm_call(body, name, ins, out_shapes, n_remote, stages, side=None):
    n_in, n_out = len(ins), len(out_shapes)
    side_fn, side_ins, side_outs = side if side is not None else (None, (), [])
    s_in, s_out = len(side_ins), len(side_outs)

    def kern(*refs):
        in_refs, refs = refs[:n_in], refs[n_in:]
        side_in_refs, refs = refs[:s_in], refs[s_in:]
        out_refs, refs = refs[:n_out], refs[n_out:]
        side_out_refs, refs = refs[:s_out], refs[s_out:]
        if side is None:
            body(_Copies(refs[0], refs[1], refs[2], refs[3:]), in_refs, out_refs)
            return
        side_bufs, side_sems, refs = refs[:s_out], refs[s_out], refs[s_out + 1:]
        cps = _Copies(refs[0], refs[1], refs[2], refs[3:])
        leave = [pltpu.make_async_copy(side_bufs[k], side_out_refs[k], side_sems.at[k]) for k in range(s_out)]

        def run_side():
            side_fn(*side_in_refs, *side_bufs)
            for cp in leave:
                cp.start()

        body(cps, in_refs, out_refs, run_side)
        for cp in leave:
            cp.wait()

    hbm, vmem = pl.BlockSpec(memory_space=pl.ANY), pl.BlockSpec(memory_space=pltpu.VMEM)
    side_scratch = [] if side is None else [pltpu.VMEM(o.shape, o.dtype) for o in side_outs] + [pltpu.SemaphoreType.DMA((s_out,))]
    return pl.pallas_call(
        kern, name=name, out_shape=list(out_shapes) + list(side_outs), in_specs=[hbm] * n_in + [vmem] * s_in,
        out_specs=[hbm] * (n_out + s_out),
        scratch_shapes=side_scratch + [pltpu.SemaphoreType.DMA((n_remote,)), pltpu.SemaphoreType.DMA((n_remote,)),
                                       pltpu.SemaphoreType.DMA((2 * len(stages),))] + [pltpu.VMEM(s, d) for s, d in stages],
        compiler_params=pltpu.CompilerParams(vmem_limit_bytes=VMEM_LIMIT),
    )(*ins, *side_ins)


def _gather_to_all(cps, pairs, me, sibling, other_chips, c, base, meanwhile=None):
    idx = lambda p: 4 * p[0] + 2 * p[1] + p[2]
    for i, (src, dst) in enumerate(pairs):
        cps.local(src, dst.at[idx(me)])
        cps.send(base + 7 * i, src, dst.at[idx(me)], sibling)
        for j, chip in enumerate(other_chips):
            cps.send(base + 7 * i + 1 + j, src, dst.at[idx(me)], (*chip, c))
    cps.flush()
    if meanwhile is not None:
        meanwhile()
    for j, chip in enumerate(other_chips):
        for i, (_, dst) in enumerate(pairs):
            got = dst.at[idx((*chip, c))]
            cps.arrived(base + 7 * i + 1 + j, got, (*chip, c))
            cps.send(base + 7 * i + 4 + j, got, got, sibling)
    for i, (_, dst) in enumerate(pairs):
        cps.arrived(base + 7 * i, dst.at[idx(sibling)], sibling)
        for j, chip in enumerate(other_chips):
            cps.arrived(base + 7 * i + 4 + j, dst.at[idx((*chip, 1 - c))], sibling)


def _gather_weights(shards, small, side):
    n = len(shards)

    def body(cps, ins, outs, run_side):
        x, y, c = _place()
        me, sibling, mine = (x, y, c), (x, y, 1 - c), 2 * x + y
        chips = [(1 - x, y), (x, 1 - y), (1 - x, 1 - y)]
        for i in range(n):
            cps.local(ins[i], outs[i].at[mine])
        for j, (px, py) in enumerate(chips):
            for i in range(n):
                hr = shards[i].shape[0] // 2
                rows = pl.ds(c * hr, hr)
                cps.send(6 * i + j, ins[i].at[rows], outs[i].at[mine, rows], (px, py, c))
        _gather_to_all(cps, [(ins[n], outs[n])], me, sibling, chips, c, 6 * n, meanwhile=run_side)
        for j, (px, py) in enumerate(chips):
            for i in range(n):
                hr = shards[i].shape[0] // 2
                got = outs[i].at[2 * px + py, pl.ds(c * hr, hr)]
                cps.arrived(6 * i + j, got, (px, py, c))
                cps.send(6 * i + 3 + j, got, got, sibling)
        for j, (px, py) in enumerate(chips):
            for i in range(n):
                hr = shards[i].shape[0] // 2
                cps.arrived(6 * i + 3 + j, outs[i].at[2 * px + py, pl.ds((1 - c) * hr, hr)], sibling)
        cps.drain()

    return _comm_call(body, "gather_weights", list(shards) + [small],
                      [_sds((4,) + s.shape, s.dtype) for s in shards] + [_sds((8,) + small.shape, small.dtype)], 6 * n + 7,
                      [(a.shape, a.dtype) for a in list(shards) + [small]], side)


def _reduce_sibling(parts, dmod_rows):
    n = len(parts)

    def body(cps, ins, outs):
        x, y, c = _place()
        me, sibling = (x, y, c), (x, y, 1 - c)
        chips = [(1 - x, y), (x, 1 - y), (1 - x, 1 - y)]
        for i in range(n):
            cps.send(i, ins[i].at[:, 1 - c], outs[i], sibling)
        _gather_to_all(cps, [(ins[n], outs[n])], me, sibling, chips, c, n)
        for i in range(n):
            cps.arrived(i, outs[i], sibling)
        cps.drain()

    return _comm_call(body, "reduce_sibling", list(parts) + [dmod_rows],
                      [_sds((4,) + p.shape[2:], p.dtype) for p in parts] + [_sds((8,) + dmod_rows.shape, dmod_rows.dtype)], n + 7,
                      [(dmod_rows.shape, dmod_rows.dtype)])


def _reduce_chips(parts):
    n = len(parts)

    def body(cps, ins, outs):
        x, y, c = _place()
        mine = 2 * x + y
        chips = _other_chips(x, y)
        for i in range(n):
            cps.local(ins[i].at[mine], outs[i].at[mine])
        for j, (px, py) in enumerate(chips):
            for i in range(n):
                cps.send(3 * i + j, ins[i].at[2 * px + py], outs[i].at[mine], (px, py, c))
        cps.flush()
        for j, (px, py) in enumerate(chips):
            for i in range(n):
                cps.arrived(3 * i + j, outs[i].at[2 * px + py], (px, py, c))
        cps.drain()

    return _comm_call(body, "reduce_chips", list(parts), [_sds(p.shape, p.dtype) for p in parts], 3 * n,
                      [(p.shape[1:], p.dtype) for p in parts])


def _gather_reduced(shard_parts, repl_parts):
    ns, nr = len(shard_parts), len(repl_parts)

    def body(cps, ins, outs):
        x, y, c = _place()
        me, sibling = (x, y, c), (x, y, 1 - c)
        chips = [(1 - x, y), (x, 1 - y), (1 - x, 1 - y)]
        for i in range(ns):
            cps.local(ins[i], outs[i].at[c])
            cps.send(i, ins[i], outs[i].at[c], sibling)
        _gather_to_all(cps, [(ins[ns + i], outs[ns + i]) for i in range(nr)], me, sibling, chips, c, ns)
        for i in range(ns):
            cps.arrived(i, outs[i].at[1 - c], sibling)
        cps.drain()

    return _comm_call(body, "gather_reduced", list(shard_parts) + list(repl_parts),
                      [_sds((2,) + p.shape, p.dtype) for p in shard_parts] + [_sds((8,) + p.shape, p.dtype) for p in repl_parts],
                      ns + 7 * nr, [(p.shape, p.dtype) for p in list(shard_parts) + list(repl_parts)])


def _sum_sibling(core, parts, got, wire, name):
    n = len(parts)

    def body(core_ref, *refs):
        for i in range(n):
            refs[2 * n + i][0] = (refs[i][0] + refs[n + i][0]).astype(wire[i])

    keep_spec = lambda p: pl.BlockSpec((1, None) + p.shape[2:], lambda s, core_ref: (s, core_ref[0], 0, 0))
    slot_spec = lambda p: pl.BlockSpec((1,) + p.shape[2:], lambda s, core_ref: (s, 0, 0))
    return pl.pallas_call(
        body, name=name,
        grid_spec=pltpu.PrefetchScalarGridSpec(
            num_scalar_prefetch=1, grid=(4,), in_specs=[keep_spec(p) for p in parts] + [slot_spec(p) for p in parts],
            out_specs=[slot_spec(p) for p in parts]),
        out_shape=[_sds((4,) + p.shape[2:], wire[i]) for i, p in enumerate(parts)],
        compiler_params=_params("parallel"),
    )(core, *parts, *got)


def _sum_slots(slots, name):
    n = len(slots)

    def spec_pair(p):
        k, rows, cols = p.shape
        sub = 16 if p.dtype == BF16 else 8
        if (rows // 2) % sub == 0:
            return pl.BlockSpec((k, rows // 2, cols), lambda i: (0, i, 0)), pl.BlockSpec((rows // 2, cols), lambda i: (i, 0))
        return pl.BlockSpec((k, rows, cols), lambda i: (0, 0, 0)), pl.BlockSpec((rows, cols), lambda i: (0, 0))

    pairs = [spec_pair(p) for p in slots]

    def body(*refs):
        for i in range(n):
            acc = refs[i][0].astype(F32)
            for j in range(1, slots[i].shape[0]):
                acc = acc + refs[i][j].astype(F32)
            refs[n + i][...] = acc

    return pl.pallas_call(
        body, name=name, grid=(2,), in_specs=[a for a, _ in pairs], out_specs=[b for _, b in pairs],
        out_shape=[_sds(p.shape[1:]) for p in slots], compiler_params=_params("arbitrary"),
    )(*slots)


def _unpack_small(g_small):
    q = D_MODEL // 4

    def body(g_ref, c_ref, cw_ref, cb_ref, ba_ref, bx_ref, lam_ref):
        for d in range(8):
            c_ref[d:d + 1, :] = g_ref[d, 0:1, :]
        for s in range(4):
            cols = slice(s * q, (s + 1) * q)
            for k in range(4):
                cw_ref[k:k + 1, cols] = g_ref[2 * s, 1:2, k * q:(k + 1) * q]
            cb_ref[0:1, cols] = g_ref[2 * s, 2:3, 0:q]
            for k in range(2):
                ba_ref[k:k + 1, cols] = g_ref[2 * s, 2:3, (1 + k) * q:(2 + k) * q]
                bx_ref[k:k + 1, cols] = g_ref[2 * s, 3:4, k * q:(k + 1) * q]
                lam_ref[k:k + 1, cols] = g_ref[2 * s, 3:4, (2 + k) * q:(3 + k) * q]

    return pl.pallas_call(
        body, name="unpack_small",
        out_shape=[_sds((8, D_MODEL)), _sds((4, D_MODEL)), _sds((1, D_MODEL))] + [_sds((2, D_MODEL))] * 3,
    )(g_small)


def _modulation(c_all, ada_w, ada_b):
    cols = ada_w.shape[2]

    def body(c_ref, w_ref, b_ref, o_ref):
        cc = c_ref[...]
        o_ref[0] = _mm(cc * _sigmoid(cc), w_ref[0]) + b_ref[0]

    return pl.pallas_call(
        body, name="modulation", grid=(2,),
        in_specs=[_full((8, D_MODEL)), pl.BlockSpec((1, D_MODEL, cols), lambda l: (l, 0, 0)), pl.BlockSpec((1, 1, cols), lambda l: (l, 0, 0))],
        out_specs=pl.BlockSpec((1, 8, cols), lambda l: (l, 0, 0)), out_shape=_sds((2, 8, cols)),
        compiler_params=_params("parallel"),
    )(c_all, ada_w, ada_b)


def _adamw_math(w, g, m, v):
    m = ADAM_B1 * m + (1.0 - ADAM_B1) * g
    v = ADAM_B2 * v + (1.0 - ADAM_B2) * (g * g)
    m_hat = m / (1.0 - ADAM_B1 ** ADAM_STEP)
    v_hat = v / (1.0 - ADAM_B2 ** ADAM_STEP)
    delta = -ADAM_LR * (m_hat / (jnp.sqrt(v_hat) + ADAM_EPS) + ADAM_WD * w)
    return delta, m, v


def _ada_update(c_all, dmod, w, m, v, rider=None):
    cols = w.shape[2]
    tr = 256
    per = D_MODEL // tr
    spec3 = pl.BlockSpec((1, tr, cols), lambda i: (i // per, i % per, 0))

    def body(c_ref, d_ref, w_ref, m_ref, v_ref, g_ref, dl_ref, nm_ref, nv_ref):
        cc = c_ref[...]
        g = _mm_tn(cc * _sigmoid(cc), d_ref[0])
        g_ref[0] = g
        dl_ref[0], nm_ref[0], nv_ref[0] = _adamw_math(w_ref[0], g, m_ref[0], v_ref[0])

    return _call(
        body, "ada_update", (2 * per,),
        [pl.BlockSpec((8, tr), lambda i: (0, i % per)), pl.BlockSpec((1, 8, cols), lambda i: (i // per, 0, 0)), spec3, spec3, spec3],
        [spec3] * 4, [_sds(w.shape)] * 4, (c_all, dmod, w, m, v), "parallel", rider=rider)


def _adamw_matrices(params):
    n = len(params)
    steps = 8

    def body(*refs):
        ins, outs = refs[:4 * n], refs[4 * n:]
        for j in range(n):
            w_ref, g_ref, m_ref, v_ref = ins[4 * j:4 * j + 4]
            g = g_ref[...]
            outs[4 * j][...] = g
            outs[4 * j + 1][...], outs[4 * j + 2][...], outs[4 * j + 3][...] = _adamw_math(w_ref[...], g, m_ref[...], v_ref[...])

    spec = lambda p: _rows(p[0].shape[0] // steps, p[0].shape[1])
    res = pl.pallas_call(
        body, name="adamw_matrices", grid=(steps,), in_specs=[spec(p) for p in params for _ in range(4)],
        out_specs=[spec(p) for p in params for _ in range(4)], out_shape=[_sds(p[0].shape) for p in params for _ in range(4)],
        compiler_params=_params("parallel"),
    )(*[a for p in params for a in p])
    return [tuple(res[4 * j:4 * j + 4]) for j in range(n)]


def _adamw_small(gs, chip, params):
    n = len(params)
    shard_cols = D_MODEL // 4

    def body(chip_ref, rows_ref, cols_ref, *refs):
        ins, outs = refs[:3 * n], refs[3 * n:]
        for j in range(n):
            w_ref, m_ref, v_ref = ins[3 * j:3 * j + 3]
            g_ref, d_ref, nm_ref, nv_ref = outs[4 * j:4 * j + 4]
            for dst, sharded, src in params[j][3]:
                g = (cols_ref if sharded else rows_ref)[src]
                g_ref[dst] = g
                d_ref[dst], nm_ref[dst], nv_ref[dst] = _adamw_math(w_ref[dst], g, m_ref[dst], v_ref[dst])

    whole = lambda a: pl.BlockSpec(a.shape, lambda i, chip_ref: (0, 0))
    flat = [a for p in params for a in p[:3]]
    res = pl.pallas_call(
        body, name="adamw_small",
        grid_spec=pltpu.PrefetchScalarGridSpec(
            num_scalar_prefetch=1, grid=(1,),
            in_specs=[whole(gs), pl.BlockSpec((gs.shape[0], shard_cols), lambda i, chip_ref: (0, chip_ref[0]))] + [whole(a) for a in flat],
            out_specs=[whole(p[0]) for p in params for _ in range(4)]),
        out_shape=[_sds(p[0].shape) for p in params for _ in range(4)],
        compiler_params=_params("arbitrary"),
    )(chip, gs, gs, *flat)
    return [tuple(res[4 * j:4 * j + 4]) for j in range(n)]


def _cols(a, start, size):
    return lax.dynamic_slice_in_dim(a, start, size, axis=a.ndim - 1)


def kernel(x, c, positions, ada_w, ada_b, ln_g, ln_b, ev_w_in, ev_w_out, ev_sink, ev_sg_ln_g, ev_sg_ln_b, ev_sg_w, ev_sg_b, od_w_in, od_conv_w, od_conv_b, od_w_a, od_b_a, od_w_x, od_b_x, od_lam, od_w_out, loss_target, m_ada_w, m_ada_b, m_ln_g, m_ln_b, m_ev_w_in, m_ev_w_out, m_ev_sink, m_ev_sg_ln_g, m_ev_sg_ln_b, m_ev_sg_w, m_ev_sg_b, m_od_w_in, m_od_conv_w, m_od_conv_b, m_od_w_a, m_od_b_a, m_od_w_x, m_od_b_x, m_od_lam, m_od_w_out, v_ada_w, v_ada_b, v_ln_g, v_ln_b, v_ev_w_in, v_ev_w_out, v_ev_sink, v_ev_sg_ln_g, v_ev_sg_ln_b, v_ev_sg_w, v_ev_sg_b, v_od_w_in, v_od_conv_w, v_od_conv_b, v_od_w_a, v_od_b_a, v_od_w_x, v_od_b_x, v_od_lam, v_od_w_out):
    seq = x.shape[1]
    px, py, pc = _place()
    chip = 2 * px + py
    dev = 2 * chip + pc

    small = jnp.concatenate([od_conv_w[0].reshape(-1), od_conv_b[0], od_b_a[0].reshape(-1), jnp.zeros((256,), F32),
                             od_b_x[0].reshape(-1), od_lam[0].reshape(-1)]).reshape(3, D_MODEL)
    blk = jnp.concatenate([c, small, jnp.zeros((4, D_MODEL), F32)], axis=0)
    tr = lambda a: jnp.swapaxes(a, -1, -2)
    wire_w = lambda a: a.astype(MXU_DTYPE)
    posf = positions.astype(F32).reshape(seq, 1)
    ev_w_in4, g_small, *tabs = _gather_weights([wire_w(tr(ev_w_in[0]))], blk, _rope_tables(posf, seq))
    core = pc.astype(jnp.int32).reshape(1)
    ride = {"ev_w_out": wire_w(ev_w_out[0]), "od_w_in": wire_w(od_w_in[0]), "od_w_out": wire_w(od_w_out[0]),
            "core": core, "wire": MXU_DTYPE}
    c_all, conv_w, conv_b, b_a, b_x, lam = _unpack_small(g_small)
    conv_b = conv_b.reshape(D_MODEL)

    w_full = {
        "ev_w_in_t": ev_w_in4.reshape(EVEN_IN, D_MODEL),
        "ev_sink": ev_sink[0], "ev_sg_ln_g": ev_sg_ln_g[0], "ev_sg_ln_b": ev_sg_ln_b[0], "ev_sg_w": ev_sg_w[0],
        "ev_sg_b": ev_sg_b[0], "od_conv_w": conv_w, "od_conv_b": conv_b, "od_w_a": od_w_a[0], "od_b_a": b_a,
        "od_w_x": od_w_x[0], "od_b_x": b_x, "od_lam": lam, "ln_g": ln_g, "ln_b": ln_b,
    }

    ada_cols = ada_w.shape[2]
    mod_sh = _modulation(c_all, ada_w, _cols(ada_b, chip * ada_cols, ada_cols).reshape(2, 1, ada_cols))
    mod_all = _allgather8(mod_sh.reshape(16, ada_cols), "gather_mod").reshape(4, 2, 2, 8, ada_cols)[:, 0]
    mod_mine = lax.dynamic_index_in_dim(mod_all, dev, axis=2, keepdims=False)
    mod = mod_mine.transpose(1, 0, 2).reshape(2, 3, D_MODEL)

    grad_x, g = _local_step(x[0], tabs, loss_target[0], mod, w_full, seq, ride)

    parts = [g["ev_w_in_t"].reshape(4, 2, 352, D_MODEL), g["ev_w_out"].reshape(4, 2, 128, D_MODEL),
             g["ev_sg_w"].reshape(4, 2, BLK, BLK), g["rows"].reshape(4, 2, SMALL_ROWS // 8, D_MODEL)]
    wire = [MXU_DTYPE] * 3 + [F32]
    *got, dmod_gathered = _reduce_sibling(parts, g["dmod_blk"])
    ev_slots = list(_reduce_chips(_sum_sibling(core, parts, got, wire, "sum_sibling")))
    od_slots = list(g["od_slots"])
    mine = _sum_slots(ev_slots[0:2] + od_slots[0:2] + ev_slots[2:3] + od_slots[2:4] + ev_slots[3:4], "sum_chips")
    reduced = _gather_reduced(mine[:4], mine[4:])
    g_ev_w_in_t = reduced[0].reshape(704, D_MODEL)
    g_ev_w_out = reduced[1].reshape(256, D_MODEL)
    g_od_w_in = reduced[2].reshape(D_MODEL, 512)
    g_od_w_out = reduced[3].reshape(256, D_MODEL)
    g_sg_w = reduced[4].reshape(8 * BLK, BLK)
    g_w_a = reduced[5].reshape(16 * BLK, BLK)
    g_w_x = reduced[6].reshape(16 * BLK, BLK)
    gs = reduced[7].reshape(SMALL_ROWS, D_MODEL)
    loss = gs[ROW_LOSS, 0]
    dmod_all = dmod_gathered[:, 0:6].reshape(8, 2, 3 * D_MODEL)
    dmod_sh = _cols(dmod_all, chip * ada_cols, ada_cols).transpose(1, 0, 2)
    (g_ada_w, d_ada_w, nm_ada_w, nv_ada_w), _ = _ada_update(c_all, dmod_sh, ada_w, m_ada_w, v_ada_w)

    mats = (("ev_w_out", ev_w_out, g_ev_w_out, m_ev_w_out, v_ev_w_out), ("od_w_in", od_w_in, g_od_w_in, m_od_w_in, v_od_w_in),
            ("od_w_out", od_w_out, g_od_w_out, m_od_w_out, v_od_w_out), ("ev_sg_w", ev_sg_w, g_sg_w, m_ev_sg_w, v_ev_sg_w),
            ("od_w_a", od_w_a, g_w_a, m_od_w_a, v_od_w_a), ("od_w_x", od_w_x, g_w_x, m_od_w_x, v_od_w_x))
    upd = _adamw_matrices([(tr(ev_w_in[0]), g_ev_w_in_t, tr(m_ev_w_in[0]), tr(v_ev_w_in[0]))]
                          + [(w_.reshape(g_.shape), g_, m_.reshape(g_.shape), v_.reshape(g_.shape)) for _, w_, g_, m_, v_ in mats])
    big = {"ev_w_in": tuple(tr(a).reshape(ev_w_in.shape) for a in upd[0])}
    for (name, w_, _, _, _), u in zip(mats, upd[1:]):
        big[name] = tuple(a.reshape(w_.shape) for a in u)
    big["ada_w"] = (g_ada_w, d_ada_w, nm_ada_w, nv_ada_w)

    at = lambda r0, nr, c0, nc: (slice(r0, r0 + nr), slice(c0, c0 + nc))
    local = lambda r0, nr: ((nr, 256), [(at(0, nr, 0, 256), True, at(r0, nr, 0, 256))])
    small_g = {
        "ada_b": ((2, 3 * D_MODEL), [(at(l, 1, k * D_MODEL, D_MODEL), False, at(ROW_DMOD + 3 * l + k, 1, 0, D_MODEL))
                                     for l in range(2) for k in range(3)]),
        "ln_g": ((2, D_MODEL), [(at(l, 1, 0, D_MODEL), False, at(ROW_LN + 2 * l, 1, 0, D_MODEL)) for l in range(2)]),
        "ln_b": ((2, D_MODEL), [(at(l, 1, 0, D_MODEL), False, at(ROW_LN + 1 + 2 * l, 1, 0, D_MODEL)) for l in range(2)]),
        "ev_sink": ((1, N_Q_HEADS), [(at(0, 1, 0, N_Q_HEADS), False, at(ROW_SINK, 1, 0, N_Q_HEADS))]),
        "ev_sg_ln_g": ((1, SG_WIDTH), [(at(0, 1, 0, SG_WIDTH), False, at(ROW_SG_LN, 1, 0, SG_WIDTH))]),
        "ev_sg_ln_b": ((1, SG_WIDTH), [(at(0, 1, 0, SG_WIDTH), False, at(ROW_SG_LN, 1, SG_WIDTH, SG_WIDTH))]),
        "ev_sg_b": ((N_SG_GROUPS, BLK), [(at(j, 1, 0, BLK), False, at(ROW_SG_B, 1, j * BLK, BLK)) for j in range(N_SG_GROUPS)]),
        "od_conv_w": local(ROW_CONV_W, 4), "od_conv_b": local(ROW_CONV_B, 1), "od_b_a": local(ROW_B_A, 2),
        "od_b_x": local(ROW_B_X, 2), "od_lam": local(ROW_LAM, 2),
    }
    small_in = {"ada_b": (ada_b, m_ada_b, v_ada_b), "ln_g": (ln_g, m_ln_g, v_ln_g), "ln_b": (ln_b, m_ln_b, v_ln_b),
                "ev_sink": (ev_sink, m_ev_sink, v_ev_sink), "ev_sg_ln_g": (ev_sg_ln_g, m_ev_sg_ln_g, v_ev_sg_ln_g),
                "ev_sg_ln_b": (ev_sg_ln_b, m_ev_sg_ln_b, v_ev_sg_ln_b), "ev_sg_b": (ev_sg_b, m_ev_sg_b, v_ev_sg_b),
                "od_conv_w": (od_conv_w, m_od_conv_w, v_od_conv_w), "od_conv_b": (od_conv_b, m_od_conv_b, v_od_conv_b),
                "od_b_a": (od_b_a, m_od_b_a, v_od_b_a), "od_b_x": (od_b_x, m_od_b_x, v_od_b_x),
                "od_lam": (od_lam, m_od_lam, v_od_lam)}
    names_small = list(small_g)
    upd = _adamw_small(gs, chip.astype(jnp.int32).reshape(1),
                       [tuple(a.reshape(small_g[n][0]) for a in small_in[n]) + (small_g[n][1],) for n in names_small])
    res = dict(big)
    for n, u in zip(names_small, upd):
        res[n] = tuple(a.reshape(small_in[n][0].shape) for a in u)

    order = ["ada_w", "ada_b", "ln_g", "ln_b", "ev_w_in", "ev_w_out", "ev_sink", "ev_sg_ln_g", "ev_sg_ln_b", "ev_sg_w", "ev_sg_b",
             "od_w_in", "od_conv_w", "od_conv_b", "od_w_a", "od_b_a", "od_w_x", "od_b_x", "od_lam", "od_w_out"]
    return (loss, grad_x.reshape(x.shape), *[res[n][0] for n in order], *[res[n][1] for n in order],
            *[res[n][2] for n in order], *[res[n][3] for n in order])
```

```python
from functools import partial

import jax
import jax.numpy as jnp
import numpy as np
from jax import lax
from jax.experimental import pallas as pl
from jax.experimental.pallas import tpu as pltpu

F32 = jnp.float32
BF16 = jnp.bfloat16
MXU_DTYPE = BF16
ACT_DTYPE = MXU_DTYPE

D_MODEL = 1024
HEAD_DIM = 64
N_Q_HEADS = 8
Q_PER_KV = 4
ATTN_WIDTH = 512
BLK = 128
ROPE_DIM = 16
ROPE_THETA = 500000.0
N_SG_GROUPS = 8
SG_WIDTH = 512
EVEN_IN = 2816
ODD_IN = 2048
RNN_HEADS = 8
RG_LRU_C = 8.0
ALPHA = (2 * 2) ** 0.25
LN_EPS = 1e-5
NEG_INF = -1e30
ADAM_LR, ADAM_B1, ADAM_B2, ADAM_EPS, ADAM_WD, ADAM_STEP = 0.001, 0.9, 0.999, 1e-08, 0.01, 10

LANES = 128
VMEM_LIMIT = 56 * 1024 * 1024
MESH = pl.DeviceIdType.MESH


def _mm(a, b):
    return jnp.dot(a.astype(MXU_DTYPE), b.astype(MXU_DTYPE), preferred_element_type=F32)


def _mm_nt(a, b):
    return lax.dot_general(a.astype(MXU_DTYPE), b.astype(MXU_DTYPE), (((1,), (1,)), ((), ())), preferred_element_type=F32)


def _mm_tn(a, b):
    return lax.dot_general(a.astype(MXU_DTYPE), b.astype(MXU_DTYPE), (((0,), (0,)), ((), ())), preferred_element_type=F32)


def _sigmoid(x):
    return 1.0 / (1.0 + jnp.exp(-x))


def _ln_stats(z):
    mu = jnp.mean(z, axis=-1, keepdims=True)
    d = z - mu
    var = jnp.mean(d * d, axis=-1, keepdims=True)
    rstd = lax.rsqrt(var + LN_EPS)
    return d * rstd, rstd


def _ln_bwd(dout, zhat, rstd, g):
    dzh = dout * g
    m1 = jnp.mean(dzh, axis=-1, keepdims=True)
    m2 = jnp.mean(dzh * zhat, axis=-1, keepdims=True)
    return rstd * (dzh - m1 - zhat * m2)


def _group_sum(x, e2):
    hi = x.astype(MXU_DTYPE)
    lo = (x - hi.astype(F32)).astype(MXU_DTYPE)
    return jnp.dot(hi, e2, preferred_element_type=F32) + jnp.dot(lo, e2, preferred_element_type=F32)


def _lane_iota(shape):
    return lax.broadcasted_iota(jnp.int32, shape, 1)


def _to_kv_lanes(t, h):
    src_lo = (h % 2 == 0)
    dst_lo = (h // Q_PER_KV == 0)
    if src_lo != dst_lo:
        t = pltpu.roll(t, HEAD_DIM, 1)
    lane = _lane_iota(t.shape)
    keep = (lane < HEAD_DIM) if dst_lo else (lane >= HEAD_DIM)
    return jnp.where(keep, t, 0.0)


def _from_kv_lanes(t, h):
    src_lo = (h // Q_PER_KV == 0)
    dst_lo = (h % 2 == 0)
    lane = _lane_iota(t.shape)
    keep = (lane < HEAD_DIM) if src_lo else (lane >= HEAD_DIM)
    t = jnp.where(keep, t, 0.0)
    if src_lo != dst_lo:
        t = pltpu.roll(t, HEAD_DIM, 1)
    return t


def _rope(t, cos_t, sin_p, sin_m):
    half = ROPE_DIM // 2
    return t * cos_t + pltpu.roll(t, half, 1) * sin_p + pltpu.roll(t, LANES - half, 1) * sin_m


def _rope_t(d, cos_t, sin_p, sin_m):
    half = ROPE_DIM // 2
    return d * cos_t + pltpu.roll(d * sin_p, LANES - half, 1) + pltpu.roll(d * sin_m, half, 1)


def _band(ref, n, nb):
    prev = jnp.maximum(n - 1, 0)
    nxt = jnp.minimum(n + 1, nb - 1)
    rows = [ref[pl.ds(pl.multiple_of(j * BLK, BLK), BLK), :] for j in (prev, n, nxt)]
    return jnp.concatenate(rows, axis=0)


def _band_bias(n, seq):
    qi = lax.broadcasted_iota(jnp.int32, (BLK, 3 * BLK), 0)
    kj = lax.broadcasted_iota(jnp.int32, (BLK, 3 * BLK), 1)
    k_abs = n * BLK - BLK + kj
    valid = (jnp.abs(kj - BLK - qi) <= BLK) & (k_abs >= 0) & (k_abs < seq)
    bias = jnp.where(valid, 0.0, NEG_INF)
    return jnp.concatenate([bias] * Q_PER_KV, axis=0)


def _stack_heads(tile_of, kv):
    return jnp.concatenate([_to_kv_lanes(tile_of(h // 2), h) for h in range(Q_PER_KV * kv, Q_PER_KV * (kv + 1))], axis=0)


def _per_head_column(vals):
    row = lax.broadcasted_iota(jnp.int32, (Q_PER_KV * BLK, 1), 0)
    return jnp.where(row < BLK, vals[0], jnp.where(row < 2 * BLK, vals[1], jnp.where(row < 3 * BLK, vals[2], vals[3])))


def _softplus_neg(lam):
    e = jnp.exp(-jnp.abs(lam))
    u = 1.0 + e
    log1p_e = jnp.where(u == 1.0, e, jnp.log(u) * (e / (u - 1.0)))
    sp = jnp.maximum(-lam, 0.0) + log1p_e
    dsp = -1.0 / (1.0 + jnp.exp(lam))
    return sp, dsp


def _full(shape):
    return pl.BlockSpec(shape, lambda *_: (0,) * len(shape))


def _const(shape):
    return pl.BlockSpec(shape, lambda *_: (0,) * len(shape), pipeline_mode=pl.Buffered(1))


def _rows(tm, n):
    return pl.BlockSpec((tm, n), lambda i: (i, 0))


def _params(*sem):
    return pltpu.CompilerParams(dimension_semantics=sem, vmem_limit_bytes=VMEM_LIMIT)


def _sds(shape, dtype=F32):
    return jax.ShapeDtypeStruct(shape, dtype)


def _place():
    return lax.axis_index("x"), lax.axis_index("y"), lax.axis_index("c")


class _Rider:
    def __init__(self, ins, out_shapes, n_remote, n_local, plan):
        self.ins, self.out_shapes, self.n_remote, self.n_local, self.plan = list(ins), list(out_shapes), n_remote, n_local, plan

    def scratch(self):
        return [pltpu.SemaphoreType.DMA((self.n_remote,)), pltpu.SemaphoreType.DMA((self.n_remote,)),
                pltpu.SemaphoreType.DMA((max(self.n_local, 1),))]

    def run(self, first, in_refs, out_refs, sems):
        send_sems, recv_sems, local_sems = sems
        sends, recvs, locals_ = self.plan(in_refs, out_refs)
        remote = lambda k, src, dst, to: pltpu.make_async_remote_copy(
            src_ref=src, dst_ref=dst, send_sem=send_sems.at[k], recv_sem=recv_sems.at[k], device_id=to, device_id_type=MESH)
        if first:
            for k, src, dst, to in sends:
                remote(k, src, dst, to).start()
            for j, (src, dst) in enumerate(locals_):
                pltpu.make_async_copy(src, dst, local_sems.at[j]).start()
        else:
            for k, dst, frm in recvs:
                remote(k, dst, dst, frm).wait_recv()
            for k, src, dst, to in sends:
                remote(k, src, dst, to).wait_send()
            for j, (src, dst) in enumerate(locals_):
                pltpu.make_async_copy(src, dst, local_sems.at[j]).wait()


def _other_chips(x, y):
    return [(1 - x, y), (x, 1 - y), (1 - x, 1 - y)]


def _gather_rider(shard):
    hr = shard.shape[0] // 2

    def plan(ins, outs):
        x, y, c = _place()
        mine, src, dst = 2 * x + y, ins[0], outs[0]
        sends, recvs = [], []
        for j, (px, py) in enumerate(_other_chips(x, y)):
            for flip in range(2):
                tc = c if flip == 0 else 1 - c
                sends.append((2 * j + flip, src.at[pl.ds(c * hr, hr)], dst.at[mine, pl.ds(c * hr, hr)], (px, py, tc)))
                recvs.append((2 * j + flip, dst.at[2 * px + py, pl.ds(tc * hr, hr)], (px, py, tc)))
        return sends, recvs, [(src, dst.at[mine])]

    return _Rider([shard], [_sds((4,) + shard.shape, shard.dtype)], 6, 1, plan)


def _sibling_swap_rider(parts):
    n = len(parts)

    def plan(ins, outs):
        x, y, c = _place()
        sibling = (x, y, 1 - c)
        return ([(i, ins[i].at[:, 1 - c], outs[i], sibling) for i in range(n)], [(i, outs[i], sibling) for i in range(n)], [])

    return _Rider(parts, [_sds((4,) + p.shape[2:], p.dtype) for p in parts], n, 0, plan)


def _chip_exchange_rider(parts):
    n = len(parts)

    def plan(ins, outs):
        x, y, c = _place()
        mine = 2 * x + y
        sends, recvs = [], []
        for i in range(n):
            for j, (px, py) in enumerate(_other_chips(x, y)):
                sends.append((3 * i + j, ins[i].at[2 * px + py], outs[i].at[mine], (px, py, c)))
                recvs.append((3 * i + j, outs[i].at[2 * px + py], (px, py, c)))
        return sends, recvs, [(ins[i].at[mine], outs[i].at[mine]) for i in range(n)]

    return _Rider(parts, [_sds(p.shape, p.dtype) for p in parts], 3 * n, n, plan)


def _call(body, name, grid, in_specs, out_specs, out_shape, args, sem, scratch=(), rider=None):
    if rider is None:
        return list(pl.pallas_call(body, name=name, grid=grid, in_specs=in_specs, out_specs=out_specs, out_shape=out_shape,
                                   scratch_shapes=list(scratch), compiler_params=_params(sem))(*args)), []
    n_in, n_out, n_scr = len(in_specs), len(out_specs), len(scratch)
    r_in, r_out = len(rider.ins), len(rider.out_shapes)
    steps = grid[0]

    def riding(*refs):
        ins, r_ins = refs[:n_in], refs[n_in:n_in + r_in]
        outs = refs[n_in + r_in:n_in + r_in + n_out]
        r_outs = refs[n_in + r_in + n_out:n_in + r_in + n_out + r_out]
        scr = refs[n_in + r_in + n_out + r_out:n_in + r_in + n_out + r_out + n_scr]
        sems = refs[n_in + r_in + n_out + r_out + n_scr:]

        @pl.when(pl.program_id(0) == 0)
        def _():
            rider.run(True, r_ins, r_outs, sems)

        body(*ins, *outs, *scr)

        @pl.when(pl.program_id(0) == steps - 1)
        def _():
            rider.run(False, r_ins, r_outs, sems)

    hbm = pl.BlockSpec(memory_space=pl.ANY)
    res = pl.pallas_call(
        riding, name=name, grid=grid, in_specs=list(in_specs) + [hbm] * r_in, out_specs=list(out_specs) + [hbm] * r_out,
        out_shape=list(out_shape) + rider.out_shapes, scratch_shapes=list(scratch) + rider.scratch(),
        compiler_params=_params("arbitrary"),
    )(*args, *rider.ins)
    return list(res[:n_out]), list(res[n_out:])


def _row_tile(seq, want):
    return want if seq % want == 0 else seq


def _rope_tables(posf, seq):
    half = ROPE_DIM // 2
    inv_freq = np.power(np.float32(ROPE_THETA), -np.arange(half, dtype=np.float32) / np.float32(half)).astype(np.float32)
    j = np.arange(LANES) % HEAD_DIM
    invf = jnp.asarray(np.where(j < ROPE_DIM, inv_freq[j % half], 0.0).astype(np.float32).reshape(1, LANES))
    m_p = jnp.asarray(((j >= half) & (j < ROPE_DIM)).astype(np.float32).reshape(1, LANES))
    m_m = jnp.asarray(-(j < half).astype(np.float32).reshape(1, LANES))
    tm = _row_tile(seq, 512)

    def body(pos_ref, invf_ref, mp_ref, mm_ref, cos_ref, sp_ref, sm_ref):
        def block(i, carry):
            rows = pl.ds(pl.multiple_of(i * tm, tm), tm)
            ang = pos_ref[rows, :] * invf_ref[...]
            s = jnp.sin(ang)
            cos_ref[rows, :] = jnp.cos(ang)
            sp_ref[rows, :] = s * mp_ref[...]
            sm_ref[rows, :] = s * mm_ref[...]
            return carry

        lax.fori_loop(0, seq // tm, block, 0)

    return body, (posf, invf, m_p, m_m), [_sds((seq, LANES))] * 3


def _even_proj(x, mod, w_in_t, tabs, seq, rider=None):
    tm = _row_tile(seq, 512)

    def body(x_ref, mod_ref, w_ref, cos_ref, sp_ref, sm_ref, q_ref, k_ref, v_ref, su_ref, sv_ref, g_ref):
        h = x_ref[...] * (1.0 + mod_ref[1:2, :]) + mod_ref[0:1, :]
        p = _mm_nt(h, w_ref[...])
        cos_t, sin_p, sin_m = cos_ref[...], sp_ref[...], sm_ref[...]
        for j in range(ATTN_WIDTH // LANES):
            q_ref[:, j * LANES:(j + 1) * LANES] = _rope(p[:, j * LANES:(j + 1) * LANES], cos_t, sin_p, sin_m).astype(q_ref.dtype)
        k_ref[...] = _rope(p[:, 512:640], cos_t, sin_p, sin_m).astype(k_ref.dtype)
        v_ref[...] = p[:, 640:768].astype(v_ref.dtype)
        su_ref[...] = p[:, 768:1280].astype(su_ref.dtype)
        sv_ref[...] = p[:, 1280:1792].astype(sv_ref.dtype)
        g_ref[...] = p[:, 1792:2816].astype(g_ref.dtype)

    return _call(
        body, "even_proj", (seq // tm,),
        [_rows(tm, D_MODEL), _full((3, D_MODEL)), _const((EVEN_IN, D_MODEL))] + [_rows(tm, LANES)] * 3,
        [_rows(tm, 512), _rows(tm, LANES), _rows(tm, LANES), _rows(tm, 512), _rows(tm, 512), _rows(tm, D_MODEL)],
        [_sds((seq, 512), MXU_DTYPE), _sds((seq, LANES), MXU_DTYPE), _sds((seq, LANES), MXU_DTYPE), _sds((seq, 512), ACT_DTYPE),
         _sds((seq, 512), ACT_DTYPE), _sds((seq, D_MODEL), ACT_DTYPE)],
        (x, mod, w_in_t, *tabs), "parallel", rider=rider)


def _sg_forward(sv, lng, lnb, sgw_ref, sgb, e2):
    vn, vhat, rstd, svo = [], [], [], []
    for j in range(SG_WIDTH // LANES):
        t = sv[:, j * LANES:(j + 1) * LANES]
        mu = _group_sum(t, e2) * (1.0 / HEAD_DIM)
        d = t - mu
        var = _group_sum(d * d, e2) * (1.0 / HEAD_DIM)
        r = lax.rsqrt(var + LN_EPS)
        vh = d * r
        vhat.append(vh)
        rstd.append(r)
        vn.append(vh * lng[:, j * LANES:(j + 1) * LANES] + lnb[:, j * LANES:(j + 1) * LANES])
    lane = _lane_iota((BLK, LANES))
    for j in range(SG_WIDTH // LANES):
        lo = _mm(sgw_ref[2 * j], vn[j])
        hi = _mm(sgw_ref[2 * j + 1], vn[j])
        svo.append(jnp.where(lane < HEAD_DIM, lo, hi) + sgb[:, j * LANES:(j + 1) * LANES])
    return svo, vn, vhat, rstd


def _even_mix(q, k, v, su, sv, sink, sgln_g, sgln_b, sgw, sgb_full, e2, seq, rider=None):
    nb = seq // BLK

    def body(sink_ref, q_ref, k_ref, v_ref, su_ref, sv_ref, lng_ref, lnb_ref, sgw_ref, sgb_ref, e2_ref, ycat_ref, lse_ref,
             svo_ref, vhat_ref, rstd_ref):
        n = pl.program_id(0)
        kband = _band(k_ref, n, nb)
        vband = _band(v_ref, n, nb)
        bias = _band_bias(n, seq)
        lane = _lane_iota((BLK, LANES))
        lse = jnp.zeros((BLK, LANES), F32)
        q_tile = lambda j: q_ref[:, j * LANES:(j + 1) * LANES].astype(F32)
        acc = [jnp.zeros((BLK, LANES), F32) for _ in range(ATTN_WIDTH // LANES)]
        for kv in range(N_Q_HEADS // Q_PER_KV):
            heads = range(Q_PER_KV * kv, Q_PER_KV * (kv + 1))
            sink = _per_head_column([sink_ref[h] for h in heads])
            s = _mm_nt(_stack_heads(q_tile, kv), kband) * (HEAD_DIM ** -0.5) + bias
            m = jnp.maximum(jnp.max(s, axis=1, keepdims=True), sink)
            p = jnp.exp(s - m)
            denom = jnp.sum(p, axis=1, keepdims=True) + jnp.exp(sink - m)
            o4 = _mm(p / denom, vband)
            l4 = m + jnp.log(denom)
            for g, h in enumerate(heads):
                acc[h // 2] = acc[h // 2] + _from_kv_lanes(o4[g * BLK:(g + 1) * BLK], h)
                lse = jnp.where(lane == h, l4[g * BLK:(g + 1) * BLK], lse)
        for j in range(ATTN_WIDTH // LANES):
            ycat_ref[:, j * LANES:(j + 1) * LANES] = acc[j].astype(ycat_ref.dtype)
        lse_ref[...] = lse
        svo, _, vhat, rstd = _sg_forward(sv_ref[...].astype(F32), lng_ref[...], lnb_ref[...], sgw_ref, sgb_ref[...], e2_ref[...])
        for j in range(SG_WIDTH // LANES):
            cs = slice(j * LANES, (j + 1) * LANES)
            ysg = su_ref[:, cs].astype(F32) * svo[j]
            ycat_ref[:, ATTN_WIDTH + j * LANES:ATTN_WIDTH + (j + 1) * LANES] = ysg.astype(ycat_ref.dtype)
            svo_ref[:, cs], vhat_ref[:, cs], rstd_ref[:, cs] = (t.astype(svo_ref.dtype) for t in (svo[j], vhat[j], rstd[j]))

    blk = lambda w: pl.BlockSpec((BLK, w), lambda n: (n, 0))
    return _call(
        body, "even_mix", (nb,),
        [pl.BlockSpec(memory_space=pltpu.SMEM), blk(512), _full((seq, LANES)), _full((seq, LANES)), blk(512), blk(512),
         _full((1, 512)), _full((1, 512)), _full((8, BLK, BLK)), _full((BLK, 512)), _full((LANES, LANES))],
        [blk(D_MODEL), blk(LANES)] + [blk(SG_WIDTH)] * 3,
        [_sds((seq, D_MODEL), ACT_DTYPE), _sds((seq, LANES))] + [_sds((seq, SG_WIDTH), ACT_DTYPE)] * 3,
        (sink, q, k, v, su, sv, sgln_g, sgln_b, sgw, sgb_full, e2), "parallel", rider=rider)


def _even_out(ycat, g, x, mod, mod_next, w_out, w_in4_next, ln_g, ln_b, seq, rider=None):
    tm = _row_tile(seq, 512)
    cs = ODD_IN // 4

    def body(y_ref, g_ref, x_ref, mod_ref, modn_ref, wo_ref, wi_ref, g1_ref, b1_ref, zhat_ref, rstd_ref, x1_ref, xr_ref, gn_ref):
        gg = g_ref[...].astype(F32)
        out = _mm(y_ref[...].astype(F32) * (gg * _sigmoid(gg)), wo_ref[...])
        z = ALPHA * x_ref[...] + mod_ref[2:3, :] * out
        zhat, rstd = _ln_stats(z)
        zhat_ref[...] = zhat
        rstd_ref[...] = rstd
        x1 = zhat * g1_ref[...] + b1_ref[...]
        x1_ref[...] = x1
        hb = (x1 * (1.0 + modn_ref[1:2, :]) + modn_ref[0:1, :]).astype(MXU_DTYPE)
        for s in range(2):
            xr_ref[:, s * cs:(s + 1) * cs] = jnp.dot(hb, wi_ref[s], preferred_element_type=F32)
            gn_ref[:, s * cs:(s + 1) * cs] = jnp.dot(hb, wi_ref[2 + s], preferred_element_type=F32).astype(gn_ref.dtype)

    return _call(
        body, "even_out", (seq // tm,),
        [_rows(tm, D_MODEL)] * 3 + [_full((3, D_MODEL)), _full((3, D_MODEL)), _const((D_MODEL, D_MODEL)), _const((4, D_MODEL, cs)),
                                    _full((1, D_MODEL)), _full((1, D_MODEL))],
        [_rows(tm, D_MODEL), _rows(tm, 1)] + [_rows(tm, D_MODEL)] * 3,
        [_sds((seq, D_MODEL)), _sds((seq, 1))] + [_sds((seq, D_MODEL))] * 2 + [_sds((seq, D_MODEL), ACT_DTYPE)],
        (ycat, g, x, mod, mod_next, w_out, w_in4_next, ln_g, ln_b), "parallel", rider=rider)


def _halo_specs(tm, seq, width, order=lambda i: i):
    per = tm // 8
    last = seq // 8 - 1
    return [pl.BlockSpec((8, width), lambda i: (jnp.maximum(order(i) * per - 1, 0), 0)),
            pl.BlockSpec((tm, width), lambda i: (order(i), 0)),
            pl.BlockSpec((8, width), lambda i: (jnp.minimum((order(i) + 1) * per, last), 0))]


def _extended(prev_ref, main_ref, next_ref, i, n_steps):
    prev = jnp.where(i > 0, prev_ref[...], 0.0)
    nxt = jnp.where(i < n_steps - 1, next_ref[...], 0.0)
    return jnp.concatenate([prev, main_ref[...], nxt], axis=0)


def _shifted(ext, off, tm):
    if off == 0:
        return ext[8:8 + tm]
    return pltpu.roll(ext, (-off) % ext.shape[0], 0)[8:8 + tm]


SCAN_SUB = 8


def _lru_gate(xh, pre, bias, sp, hs, d):
    r = _sigmoid(pre[:, 0:LANES] + bias[2 * d:2 * d + 1, hs])
    ig = _sigmoid(pre[:, LANES:2 * LANES] + bias[2 * d + 1:2 * d + 2, hs])
    neg_log_a = RG_LRU_C * r * sp[d:d + 1, hs]
    a = jnp.exp(-neg_log_a)
    u = jnp.tanh(neg_log_a) * (a * a + 1.0)
    inv_s = lax.rsqrt(jnp.maximum(u, jnp.finfo(F32).tiny))
    return r, ig, a, u * inv_s, inv_s


def _conv_block(xp_ref, xm_ref, xn_ref, cw_ref, cb_ref, blk, steps, tm):
    ext = _extended(xp_ref, xm_ref, xn_ref, blk, steps)
    return cb_ref[...] + sum(cw_ref[kk:kk + 1, :] * _shifted(ext, kk - 2, tm) for kk in range(4))


def _scan_tiles(a_ref, b_ref, h_ref, hprev_ref, carry_h, carry_a, rows, descending, post):
    sub = SCAN_SUB
    tiles = rows // sub
    row = lax.broadcasted_iota(jnp.int32, (sub, D_MODEL), 0)

    def shift(v, d, fill):
        if descending:
            return jnp.where(row <= sub - 1 - d, pltpu.roll(v, sub - d, 0), fill)
        return jnp.where(row >= d, pltpu.roll(v, d, 0), fill)

    def last(v):
        return jnp.broadcast_to(v[0:1, :] if descending else v[sub - 1:sub, :], v.shape)

    def tile(j, c):
        ch, ca = c
        r0 = pl.multiple_of(((tiles - 1 - j) if descending else j) * sub, sub)
        at = a_ref[pl.ds(r0, sub), :]
        bt = b_ref[pl.ds(r0, sub), :]
        coef = shift(at, 1, ca) if post else at
        acc_a, acc_b = coef, bt
        for d in (1, 2, 4):
            acc_b = acc_b + acc_a * shift(acc_b, d, 0.0)
            acc_a = acc_a * shift(acc_a, d, 1.0)
        h = acc_b + acc_a * ch
        h_ref[pl.ds(r0, sub), :] = h
        if post:
            return last(h), last(at)
        hprev_ref[pl.ds(r0, sub), :] = shift(h, 1, ch)
        return last(h), ca

    ch, ca = lax.fori_loop(0, tiles, tile, (carry_h[...], carry_a[...]), unroll=4)
    carry_h[...] = ch
    carry_a[...] = ca


def _lru_fwd(xr, xc, conv_w, conv_b, wcat, bias, lam, seq, d):
    tb = _row_tile(seq, 512)
    steps = seq // tb
    descending = d == 1
    order = (lambda i: steps - 1 - i) if descending else (lambda i: i)
    with_conv = xc is None
    n_x = 5 if with_conv else 1

    def body(*refs):
        x_refs, (w_ref, bias_ref, lam_ref) = refs[:n_x], refs[n_x:n_x + 3]
        h_ref, hp_ref, a_ref, r_ref, i_ref, s_ref, q_ref = refs[n_x + 3:n_x + 10]
        b_scr, carry_h, carry_a = refs[-3:]
        i = pl.program_id(0)

        @pl.when(i == 0)
        def _():
            carry_h[...] = jnp.zeros_like(carry_h)
            carry_a[...] = jnp.zeros_like(carry_a)

        if with_conv:
            xc_ref = refs[n_x + 10]
            xc_ref[...] = _conv_block(*x_refs, order(i), steps, tb)
        else:
            xc_ref = x_refs[0]
        sp, _ = _softplus_neg(lam_ref[...])
        bias = bias_ref[...]
        for h in range(RNN_HEADS):
            hs = slice(h * LANES, (h + 1) * LANES)
            xh = xc_ref[:, hs]
            r, ig, a, s, q = _lru_gate(xh, _mm(xh, w_ref[h, :, 2 * d * LANES:2 * (d + 1) * LANES]), bias, sp, hs, d)
            a_ref[:, hs] = a
            b_scr[:, hs] = s * ig * xh
            for ref, val in ((r_ref, r), (i_ref, ig), (s_ref, s), (q_ref, q)):
                ref[:, hs] = val.astype(ref.dtype)
        _scan_tiles(a_ref, b_scr, h_ref, hp_ref, carry_h, carry_a, tb, descending, post=False)

    row_spec = pl.BlockSpec((tb, D_MODEL), lambda i: (order(i), 0))
    if with_conv:
        x_specs, x_args = _halo_specs(tb, seq, D_MODEL, order) + [_full((4, D_MODEL)), _full((1, D_MODEL))], (xr, xr, xr, conv_w, conv_b)
    else:
        x_specs, x_args = [row_spec], (xc,)
    n_out = 8 if with_conv else 7
    return pl.pallas_call(
        body, name="lru_fwd_%d" % d, grid=(steps,),
        in_specs=x_specs + [_full((8, LANES, 512)), _full((4, D_MODEL)), _full((2, D_MODEL))],
        out_specs=[row_spec] * n_out,
        out_shape=[_sds((seq, D_MODEL))] * 3 + [_sds((seq, D_MODEL), ACT_DTYPE)] * 4 + [_sds((seq, D_MODEL))] * (n_out - 7),
        scratch_shapes=[pltpu.VMEM((tb, D_MODEL), F32)] + [pltpu.VMEM((SCAN_SUB, D_MODEL), F32)] * 2,
        compiler_params=_params("arbitrary"),
    )(*x_args, wcat, bias, lam)


def _odd_out_and_loss(hf, hr, g, x1, tgt, mod, w_out, ln_g, ln_b, seq):
    tm = _row_tile(seq, 512)

    def body(hf_ref, hr_ref, g_ref, x_ref, t_ref, mod_ref, w_ref, lg_ref, lb_ref,
             dhs_ref, dg_ref, dres_ref, loss_ref, dw_ref, vec_ref):
        @pl.when(pl.program_id(0) == 0)
        def _():
            loss_ref[...] = jnp.zeros_like(loss_ref)
            dw_ref[...] = jnp.zeros_like(dw_ref)
            vec_ref[...] = jnp.zeros_like(vec_ref)

        gg = g_ref[...].astype(F32)
        sg = _sigmoid(gg)
        silu = gg * sg
        hsum = hf_ref[...] + hr_ref[...]
        y = hsum * silu
        out = _mm(y, w_ref[...])
        gate = mod_ref[2:3, :]
        z = ALPHA * x_ref[...] + gate * out
        zhat, rstd = _ln_stats(z)
        x2 = zhat * lg_ref[...] + lb_ref[...]
        err = x2 - t_ref[...]
        loss_ref[...] += 0.5 * jnp.sum(jnp.mean(err * err, axis=-1, keepdims=True))
        dx2 = err * (1.0 / D_MODEL)
        dz = _ln_bwd(dx2, zhat, rstd, lg_ref[...])
        vec_ref[0:1, :] += jnp.sum(dx2 * zhat, axis=0, keepdims=True)
        vec_ref[1:2, :] += jnp.sum(dx2, axis=0, keepdims=True)
        vec_ref[2:3, :] += jnp.sum(dz * out, axis=0, keepdims=True)
        dres_ref[...] = ALPHA * dz
        dout = gate * dz
        dw_ref[...] += _mm_tn(y, dout)
        dy = _mm_nt(dout, w_ref[...])
        dhs_ref[...] = dy * silu
        dg_ref[...] = (dy * hsum * (sg * (1.0 + gg * (1.0 - sg)))).astype(dg_ref.dtype)

    return pl.pallas_call(
        body, name="odd_out_loss", grid=(seq // tm,),
        in_specs=[_rows(tm, D_MODEL)] * 5 + [_full((3, D_MODEL)), _const((D_MODEL, D_MODEL)),
                                             _full((1, D_MODEL)), _full((1, D_MODEL))],
        out_specs=[_rows(tm, D_MODEL)] * 3 + [_full((8, LANES)), _full((D_MODEL, D_MODEL)), _full((8, D_MODEL))],
        out_shape=[_sds((seq, D_MODEL)), _sds((seq, D_MODEL), ACT_DTYPE), _sds((seq, D_MODEL)), _sds((8, LANES)),
                   _sds((D_MODEL, D_MODEL)), _sds((8, D_MODEL))],
        compiler_params=_params("arbitrary"),
    )(hf, hr, g, x1, tgt, mod, w_out, ln_g, ln_b)


def _lru_bwd(xc, dhs, hprev, a_d, r_d, i_d, s_d, q_d, wcat, lam, seq, d):
    tb = _row_tile(seq, 512)
    steps = seq // tb
    descending = d == 0
    order = (lambda i: steps - 1 - i) if descending else (lambda i: i)
    cols = slice(2 * d * LANES, 2 * (d + 1) * LANES)

    def body(xc_ref, dhs_ref, hp_ref, a_ref, r_ref, i_ref, s_ref, q_ref, w_ref, lam_ref, dxc_ref, dw_ref, vec_ref,
             g_scr, carry_h, carry_a):
        i = pl.program_id(0)

        @pl.when(i == 0)
        def _():
            dw_ref[...] = jnp.zeros_like(dw_ref)
            vec_ref[...] = jnp.zeros_like(vec_ref)
            carry_h[...] = jnp.zeros_like(carry_h)
            carry_a[...] = jnp.zeros_like(carry_a)

        sp, dsp = _softplus_neg(lam_ref[...])
        _scan_tiles(a_ref, dhs_ref, g_scr, None, carry_h, carry_a, tb, descending, post=True)
        for h in range(RNN_HEADS):
            hs = slice(h * LANES, (h + 1) * LANES)
            xh, a = xc_ref[:, hs], a_ref[:, hs]
            r, ig, s = r_ref[:, hs].astype(F32), i_ref[:, hs].astype(F32), s_ref[:, hs].astype(F32)
            db = g_scr[:, hs]
            da = db * hp_ref[:, hs]
            dlog_a = da * a - (db * ig * xh) * (a * a * q_ref[:, hs].astype(F32))
            dpr = dlog_a * (-RG_LRU_C) * sp[d:d + 1, hs] * r * (1.0 - r)
            dpi = db * s * xh * ig * (1.0 - ig)
            vec_ref[0:1, hs] += jnp.sum(dpr, axis=0, keepdims=True)
            vec_ref[1:2, hs] += jnp.sum(dpi, axis=0, keepdims=True)
            vec_ref[2:3, hs] += jnp.sum(dlog_a * r, axis=0, keepdims=True) * (-RG_LRU_C) * dsp[d:d + 1, hs]
            dcat = jnp.concatenate([dpr, dpi], axis=1)
            dw_ref[h] += _mm_tn(xh, dcat)
            dxc_ref[:, hs] = db * s * ig + _mm_nt(dcat, w_ref[h, :, cols])

    row_spec = pl.BlockSpec((tb, D_MODEL), lambda i: (order(i), 0))
    return pl.pallas_call(
        body, name="lru_bwd_%d" % d, grid=(steps,),
        in_specs=[row_spec] * 8 + [_full((8, LANES, 512)), _full((2, D_MODEL))],
        out_specs=[row_spec, _full((8, LANES, 2 * LANES)), _full((8, D_MODEL))],
        out_shape=[_sds((seq, D_MODEL)), _sds((8, LANES, 2 * LANES)), _sds((8, D_MODEL))],
        scratch_shapes=[pltpu.VMEM((tb, D_MODEL), F32)] + [pltpu.VMEM((SCAN_SUB, D_MODEL), F32)] * 2,
        compiler_params=_params("arbitrary"),
    )(xc, dhs, hprev, a_d, r_d, i_d, s_d, q_d, wcat, lam)


def _odd_proj_bwd(dxc_f, dxc_r, xr, dg, x1, dres, mod, conv_w, w_in4, seq):
    tm = _row_tile(seq, 512)
    steps = seq // tm

    def body(fp_ref, fm_ref, fn_ref, rp_ref, rm_ref, rn_ref, xp_ref, xm_ref, xn_ref, dg_ref, x_ref, dres_ref, mod_ref, cw_ref,
             w_ref, dx_ref, dw_ref, vec_ref, dpb_ref):
        i = pl.program_id(0)

        @pl.when(i == 0)
        def _():
            vec_ref[...] = jnp.zeros_like(vec_ref)
            dw_ref[...] = jnp.zeros_like(dw_ref)

        dxc_m = fm_ref[...] + rm_ref[...]
        dext = jnp.concatenate([jnp.where(i > 0, fp_ref[...] + rp_ref[...], 0.0), dxc_m,
                                jnp.where(i < steps - 1, fn_ref[...] + rn_ref[...], 0.0)], axis=0)
        xext = _extended(xp_ref, xm_ref, xn_ref, i, steps)
        dxr = sum(cw_ref[kk:kk + 1, :] * _shifted(dext, 2 - kk, tm) for kk in range(4))
        for kk in range(4):
            vec_ref[kk:kk + 1, :] += jnp.sum(dxc_m * _shifted(xext, kk - 2, tm), axis=0, keepdims=True)
        vec_ref[4:5, :] += jnp.sum(dxc_m, axis=0, keepdims=True)
        dpb_ref[:, :D_MODEL] = dxr.astype(dpb_ref.dtype)
        dpb_ref[:, D_MODEL:] = dg_ref[...].astype(dpb_ref.dtype)
        cs = ODD_IN // 4
        dh = sum(_mm_nt(dpb_ref[:, s * cs:(s + 1) * cs], w_ref[s]) for s in range(4))
        x = x_ref[...]
        h_t = (x * (1.0 + mod_ref[1:2, :]) + mod_ref[0:1, :]).T.astype(MXU_DTYPE)
        for s in range(4):
            dw_ref[s] += jnp.dot(h_t, dpb_ref[:, s * cs:(s + 1) * cs], preferred_element_type=F32)
        vec_ref[5:6, :] += jnp.sum(dh, axis=0, keepdims=True)
        vec_ref[6:7, :] += jnp.sum(dh * x, axis=0, keepdims=True)
        dx_ref[...] = dres_ref[...] + dh * (1.0 + mod_ref[1:2, :])

    return pl.pallas_call(
        body, name="odd_proj_bwd", grid=(steps,),
        in_specs=_halo_specs(tm, seq, D_MODEL) * 3 + [_rows(tm, D_MODEL)] * 3
        + [_full((3, D_MODEL)), _full((4, D_MODEL)), _const((4, D_MODEL, ODD_IN // 4))],
        out_specs=[_rows(tm, D_MODEL), _const((4, D_MODEL, ODD_IN // 4)), _full((8, D_MODEL))],
        out_shape=[_sds((seq, D_MODEL)), _sds((4, D_MODEL, ODD_IN // 4)), _sds((8, D_MODEL))],
        scratch_shapes=[pltpu.VMEM((tm, ODD_IN), MXU_DTYPE)],
        compiler_params=_params("arbitrary"),
    )(dxc_f, dxc_f, dxc_f, dxc_r, dxc_r, dxc_r, xr, xr, xr, dg, x1, dres, mod, conv_w, w_in4)


def _even_out_bwd(dx1, zhat, rstd, ycat, g, mod, ln_g, w_out, seq, rider=None):
    tm = _row_tile(seq, 512)
    steps = seq // tm

    def body(dx_ref, zh_ref, rs_ref, y_ref, g_ref, mod_ref, lg_ref, w_ref, dy_ref, dg_ref, dres_ref, dw_ref, vec_ref):
        i = pl.program_id(0)

        @pl.when(i == 0)
        def _():
            dw_ref[...] = jnp.zeros_like(dw_ref)
            vec_ref[...] = jnp.zeros_like(vec_ref)

        zhat = zh_ref[...]
        dx1_ = dx_ref[...]
        dz = _ln_bwd(dx1_, zhat, rs_ref[...], lg_ref[...])
        vec_ref[0:1, :] += jnp.sum(dx1_ * zhat, axis=0, keepdims=True)
        vec_ref[1:2, :] += jnp.sum(dx1_, axis=0, keepdims=True)
        dres_ref[...] = ALPHA * dz
        gate = mod_ref[2:3, :]
        gg = g_ref[...].astype(F32)
        sg = _sigmoid(gg)
        silu = gg * sg
        ycat_ = y_ref[...].astype(F32)
        dw_ref[...] += _mm_tn(ycat_ * silu, dz)
        dy = _mm_nt(gate * dz, w_ref[...])
        dy_ref[...] = (dy * silu).astype(dy_ref.dtype)
        dg_ref[...] = (dy * ycat_ * (sg * (1.0 + gg * (1.0 - sg)))).astype(dg_ref.dtype)

        @pl.when(i == steps - 1)
        def _():
            m_acc = dw_ref[...]
            vec_ref[2:3, :] = jnp.sum(w_ref[...].astype(F32) * m_acc, axis=0, keepdims=True)
            dw_ref[...] = m_acc * gate

    return _call(
        body, "even_out_bwd", (steps,),
        [_rows(tm, D_MODEL), _rows(tm, D_MODEL), _rows(tm, 1), _rows(tm, D_MODEL), _rows(tm, D_MODEL), _full((3, D_MODEL)),
         _full((1, D_MODEL)), _const((D_MODEL, D_MODEL))],
        [_rows(tm, D_MODEL)] * 3 + [_full((D_MODEL, D_MODEL)), _full((8, D_MODEL))],
        [_sds((seq, D_MODEL), ACT_DTYPE), _sds((seq, D_MODEL), ACT_DTYPE), _sds((seq, D_MODEL)), _sds((D_MODEL, D_MODEL)),
         _sds((8, D_MODEL))],
        (dx1, zhat, rstd, ycat, g, mod, ln_g, w_out), "arbitrary", rider=rider)


def _even_mix_bwd(q, k, v, lse, ycat, dycat, su, svo_s, vhat_s, rstd_s, sink, sgln_g, sgln_b, sgw, e2, e8, seq, rider=None):
    nb = seq // BLK

    def body(sink_ref, q_ref, k_ref, v_ref, lse_ref, y_ref, dy_ref, su_ref, svo_ref, vhat_ref, rstd_ref, lng_ref, lnb_ref, sgw_ref,
             e2_ref, e8_ref, dq_ref, dsu_ref, dsv_ref, dk_ref, dv_ref, dsgw_ref, dsgb_ref, vec_ref, dsink_ref, dsgb_acc):
        n = pl.program_id(0)

        @pl.when(n == 0)
        def _():
            dk_ref[...] = jnp.zeros_like(dk_ref)
            dv_ref[...] = jnp.zeros_like(dv_ref)
            dsgw_ref[...] = jnp.zeros_like(dsgw_ref)
            dsgb_acc[...] = jnp.zeros_like(dsgb_acc)
            vec_ref[...] = jnp.zeros_like(vec_ref)
            dsink_ref[...] = jnp.zeros_like(dsink_ref)

        kband = _band(k_ref, n, nb)
        vband = _band(v_ref, n, nb)
        bias = _band_bias(n, seq)
        lane = _lane_iota((BLK, LANES))
        row8 = lax.broadcasted_iota(jnp.int32, (8, LANES), 0)
        lse = lse_ref[...]
        dkb = jnp.zeros((LANES, 3 * BLK), F32)
        dvb = jnp.zeros((LANES, 3 * BLK), F32)
        dsink = jnp.zeros((8, LANES), F32)
        q_tile = lambda j: q_ref[:, j * LANES:(j + 1) * LANES].astype(F32)
        do_tile = lambda j: dy_ref[:, j * LANES:(j + 1) * LANES].astype(F32)
        dq = [jnp.zeros((BLK, LANES), F32) for _ in range(ATTN_WIDTH // LANES)]
        for kv in range(N_Q_HEADS // Q_PER_KV):
            heads = range(Q_PER_KV * kv, Q_PER_KV * (kv + 1))
            lse4, delta4 = [], []
            for h in heads:
                head_lanes = (lane < HEAD_DIM) if h % 2 == 0 else (lane >= HEAD_DIM)
                lse4.append(jnp.sum(jnp.where(lane == h, lse, 0.0), axis=1, keepdims=True))
                o_tile = y_ref[:, (h // 2) * LANES:(h // 2 + 1) * LANES].astype(F32)
                delta4.append(jnp.sum(jnp.where(head_lanes, do_tile(h // 2) * o_tile, 0.0), axis=1, keepdims=True))
            lse4, delta4 = jnp.concatenate(lse4, axis=0), jnp.concatenate(delta4, axis=0)
            q4, do4 = _stack_heads(q_tile, kv), _stack_heads(do_tile, kv)
            s = _mm_nt(q4, kband) * (HEAD_DIM ** -0.5) + bias
            p = jnp.exp(s - lse4)
            wsink = jnp.exp(_per_head_column([sink_ref[h] for h in heads]) - lse4) * delta4
            ds = p * (_mm_nt(do4, vband) - delta4) * (HEAD_DIM ** -0.5)
            dq4 = _mm(ds, kband)
            dkb = dkb + _mm_tn(q4, ds)
            dvb = dvb + _mm_tn(do4, p)
            for g, h in enumerate(heads):
                dq[h // 2] = dq[h // 2] + _from_kv_lanes(dq4[g * BLK:(g + 1) * BLK], h)
                dsink = dsink + jnp.where(row8 == h, -jnp.sum(wsink[g * BLK:(g + 1) * BLK]), 0.0)
        for j in range(ATTN_WIDTH // LANES):
            dq_ref[:, j * LANES:(j + 1) * LANES] = dq[j].astype(dq_ref.dtype)
        dsink_ref[...] += dsink
        prev = jnp.maximum(n - 1, 0)
        nxt = jnp.minimum(n + 1, nb - 1)
        for part, blk_i in enumerate((prev, n, nxt)):
            rows = pl.ds(pl.multiple_of(blk_i * BLK, BLK), BLK)
            dk_ref[rows, :] += dkb[:, part * BLK:(part + 1) * BLK].T
            dv_ref[rows, :] += dvb[:, part * BLK:(part + 1) * BLK].T

        e2 = e2_ref[...]
        lng, lnb = lng_ref[...], lnb_ref[...]
        for j in range(SG_WIDTH // LANES):
            cs = slice(j * LANES, (j + 1) * LANES)
            vhat = vhat_ref[:, cs].astype(F32)
            vn = vhat * lng[:, cs] + lnb[:, cs]
            dysg = dy_ref[:, ATTN_WIDTH + j * LANES:ATTN_WIDTH + (j + 1) * LANES].astype(F32)
            dsu_ref[:, cs] = (dysg * svo_ref[:, cs].astype(F32)).astype(dsu_ref.dtype)
            dsvo = dysg * su_ref[:, cs].astype(F32)
            dsgb_acc[:, cs] += dsvo
            d_lo = jnp.where(lane < HEAD_DIM, dsvo, 0.0)
            d_hi = dsvo - d_lo
            dsgw_ref[2 * j] += _mm_nt(d_lo, vn)
            dsgw_ref[2 * j + 1] += _mm_nt(d_hi, vn)
            dvn = _mm_tn(sgw_ref[2 * j], d_lo) + _mm_tn(sgw_ref[2 * j + 1], d_hi)
            vec_ref[0:1, cs] += jnp.sum(dvn * vhat, axis=0, keepdims=True)
            vec_ref[1:2, cs] += jnp.sum(dvn, axis=0, keepdims=True)
            dvh = dvn * lng[:, cs]
            m1 = _group_sum(dvh, e2) * (1.0 / HEAD_DIM)
            m2 = _group_sum(dvh * vhat, e2) * (1.0 / HEAD_DIM)
            dsv_ref[:, cs] = (rstd_ref[:, cs].astype(F32) * (dvh - m1 - vhat * m2)).astype(dsv_ref.dtype)

        @pl.when(n == nb - 1)
        def _():
            rest = dsgb_acc[...]
            total = jnp.zeros((8, BLK), F32)
            for _ in range(3):
                part = rest.astype(MXU_DTYPE)
                total = total + lax.dot_general(e8_ref[...], part, (((1,), (1,)), ((), ())), preferred_element_type=F32)
                rest = rest - part.astype(F32)
            dsgb_ref[...] = total

    blk = lambda w: pl.BlockSpec((BLK, w), lambda n: (n, 0))
    return _call(
        body, "even_mix_bwd", (nb,),
        [pl.BlockSpec(memory_space=pltpu.SMEM), blk(512), _full((seq, LANES)), _full((seq, LANES)), blk(LANES),
         blk(D_MODEL), blk(D_MODEL), blk(512), blk(512), blk(512), blk(512), _full((1, 512)), _full((1, 512)), _full((8, BLK, BLK)),
         _full((LANES, LANES)), _full((8, 512))],
        [blk(512), blk(512), blk(512), _full((seq, LANES)), _full((seq, LANES)), _full((8, BLK, BLK)),
         _full((8, BLK)), _full((8, 512)), _full((8, LANES))],
        [_sds((seq, 512), ACT_DTYPE), _sds((seq, 512), ACT_DTYPE), _sds((seq, 512), ACT_DTYPE), _sds((seq, LANES)), _sds((seq, LANES)),
         _sds((8, BLK, BLK)), _sds((8, BLK)), _sds((8, 512)), _sds((8, LANES))],
        (sink, q, k, v, lse, ycat, dycat, su, svo_s, vhat_s, rstd_s, sgln_g, sgln_b, sgw, e2, e8), "arbitrary",
        scratch=[pltpu.VMEM((BLK, 512), F32)], rider=rider)


def _even_proj_bwd(dq, dk, dv, dsu, dsv, dg, x, dres, mod, tabs, w_in_t, seq):
    tm = _row_tile(seq, 512)

    def body(dq_ref, dk_ref, dv_ref, dsu_ref, dsv_ref, dg_ref, x_ref, dres_ref, mod_ref, cos_ref, sp_ref, sm_ref, wt_ref,
             dx_ref, dw_ref, vec_ref, dpb_ref):
        @pl.when(pl.program_id(0) == 0)
        def _():
            vec_ref[...] = jnp.zeros_like(vec_ref)
            dw_ref[...] = jnp.zeros_like(dw_ref)

        cos_t, sin_p, sin_m = cos_ref[...], sp_ref[...], sm_ref[...]
        dt = dpb_ref.dtype
        for j in range(ATTN_WIDTH // LANES):
            cs = slice(j * LANES, (j + 1) * LANES)
            dpb_ref[:, cs] = _rope_t(dq_ref[:, cs].astype(F32), cos_t, sin_p, sin_m).astype(dt)
        dpb_ref[:, 512:640] = _rope_t(dk_ref[...], cos_t, sin_p, sin_m).astype(dt)
        dpb_ref[:, 640:768] = dv_ref[...].astype(dt)
        dpb_ref[:, 768:1280] = dsu_ref[...].astype(dt)
        dpb_ref[:, 1280:1792] = dsv_ref[...].astype(dt)
        dpb_ref[:, 1792:2816] = dg_ref[...].astype(dt)
        dpb = dpb_ref[...]
        dh = jnp.dot(dpb, wt_ref[...], preferred_element_type=F32)
        x_ = x_ref[...]
        hb = (x_ * (1.0 + mod_ref[1:2, :]) + mod_ref[0:1, :]).astype(MXU_DTYPE)
        dw_ref[...] += _mm_tn(dpb, hb)
        vec_ref[0:1, :] += jnp.sum(dh, axis=0, keepdims=True)
        vec_ref[1:2, :] += jnp.sum(dh * x_, axis=0, keepdims=True)
        dx_ref[...] = dres_ref[...] + dh * (1.0 + mod_ref[1:2, :])

    return pl.pallas_call(
        body, name="even_proj_bwd", grid=(seq // tm,),
        in_specs=[_rows(tm, 512), _rows(tm, LANES), _rows(tm, LANES), _rows(tm, 512), _rows(tm, 512), _rows(tm, D_MODEL),
                  _rows(tm, D_MODEL), _rows(tm, D_MODEL), _full((3, D_MODEL))] + [_rows(tm, LANES)] * 3
        + [_const((EVEN_IN, D_MODEL))],
        out_specs=[_rows(tm, D_MODEL), _const((EVEN_IN, D_MODEL)), _full((8, D_MODEL))],
        out_shape=[_sds((seq, D_MODEL)), _sds((EVEN_IN, D_MODEL)), _sds((8, D_MODEL))],
        scratch_shapes=[pltpu.VMEM((tm, EVEN_IN), MXU_DTYPE)],
        compiler_params=_params("arbitrary"),
    )(dq, dk, dv, dsu, dsv, dg, x, dres, mod, *tabs, w_in_t)


def _local_step(x, tabs, tgt, mod, w, seq, ride=None):
    rid = lambda make, *a: None if ride is None else make(*a)
    mxu = lambda a: a.astype(MXU_DTYPE)
    row = lambda a: a.reshape(1, -1)
    e2 = mxu(jnp.kron(jnp.eye(2, dtype=F32), jnp.ones((HEAD_DIM, HEAD_DIM), F32)))
    e8 = mxu(jnp.repeat(jnp.eye(N_SG_GROUPS, dtype=F32), HEAD_DIM, axis=1))
    sgw = mxu(w["ev_sg_w"])
    sgb_full = jnp.repeat(w["ev_sg_b"].T, HEAD_DIM, axis=1)
    sgln_g, sgln_b = row(w["ev_sg_ln_g"]), row(w["ev_sg_ln_b"])
    sink = w["ev_sink"].reshape(N_Q_HEADS)
    ev_w_in_t = mxu(w["ev_w_in_t"])
    if ride is None:
        ev_w_out, od_w_in, od_w_out = mxu(w["ev_w_out"]), mxu(w["od_w_in"]), mxu(w["od_w_out"])
    wcat = mxu(jnp.concatenate([w["od_w_a"][0], w["od_w_x"][0], w["od_w_a"][1], w["od_w_x"][1]], axis=2))
    gate_bias = jnp.stack([w["od_b_a"][0], w["od_b_x"][0], w["od_b_a"][1], w["od_b_x"][1]])
    conv_b = row(w["od_conv_b"])
    ln_g, ln_b = w["ln_g"], w["ln_b"]

    (q, k, v, su, sv, g0), got = _even_proj(x, mod[0], ev_w_in_t, tabs, seq, rid(_gather_rider, ride and ride["ev_w_out"]))
    if ride is not None:
        ev_w_out = got[0].reshape(D_MODEL, D_MODEL)
    (ycat, lse, *sg_saved), got = _even_mix(q, k, v, su, sv, sink, sgln_g, sgln_b, sgw, sgb_full, e2, seq,
                                 rid(_gather_rider, ride and ride["od_w_in"]))
    if ride is not None:
        od_w_in = got[0]
    (zhat0, rstd0, x1, xr, g1), got = _even_out(ycat, g0, x, mod[0], mod[1], ev_w_out, od_w_in, ln_g[0:1], ln_b[0:1], seq,
                                      rid(_gather_rider, ride and ride["od_w_out"]))
    if ride is not None:
        od_w_out = got[0].reshape(D_MODEL, D_MODEL)
    lru = (w["od_conv_w"], conv_b, wcat, gate_bias, w["od_lam"], seq)
    hf, hpf, *saved_f, xc = _lru_fwd(xr, None, *lru, 0)
    hr, hpr, *saved_r = _lru_fwd(xr, xc, *lru, 1)
    dhs, dg1, dres1, loss, d_od_w_out, vec_o = _odd_out_and_loss(hf, hr, g1, x1, tgt, mod[1], od_w_out, ln_g[1:2], ln_b[1:2], seq)
    dxc_f, dw_f, vec_f = _lru_bwd(xc, dhs, hpf, *saved_f, wcat, w["od_lam"], seq, 0)
    dxc_r, dw_r, vec_r = _lru_bwd(xc, dhs, hpr, *saved_r, wcat, w["od_lam"], seq, 1)
    dx1, d_od_w_in, vec_p = _odd_proj_bwd(dxc_f, dxc_r, xr, dg1, x1, dres1, mod[1], w["od_conv_w"], od_w_in, seq)
    d_od_w_a = jnp.stack([dw_f[:, :, 0:128], dw_r[:, :, 0:128]])
    d_od_w_x = jnp.stack([dw_f[:, :, 128:256], dw_r[:, :, 128:256]])
    od_parts = [d_od_w_in.reshape(4, 2, 512, 512), d_od_w_out.reshape(4, 2, 128, D_MODEL),
                d_od_w_a.reshape(4, 2, 2 * BLK, BLK), d_od_w_x.reshape(4, 2, 2 * BLK, BLK)]
    (dycat, dg0, dres0, d_ev_w_out, vec_e), got_od = _even_out_bwd(dx1, zhat0, rstd0, ycat, g0, mod[0], ln_g[0:1], ev_w_out, seq,
                                                                   rid(_sibling_swap_rider, od_parts))
    if ride is not None:
        od_sums = _sum_sibling(ride["core"], od_parts, got_od, [ride["wire"]] * 4, "sum_sibling_od")
    (dq, dsu, dsv, dk, dv, d_sgw, d_sgb, vec_s, d_sink), od_slots = _even_mix_bwd(
        q, k, v, lse, ycat, dycat, su, *sg_saved, sink, sgln_g, sgln_b, sgw, e2, e8, seq,
        rid(_chip_exchange_rider, ride and od_sums))
    grad_x, d_ev_w_in_t, vec_x = _even_proj_bwd(dq, dk, dv, dsu, dsv, dg0, x, dres0, mod[0], tabs, ev_w_in_t, seq)

    rows, dmod_blk = _pack_small(vec_x, vec_e, vec_p, vec_o, vec_f, vec_r, vec_s, d_sink, d_sgb, loss)
    grads = {"rows": rows, "dmod_blk": dmod_blk, "ev_w_in_t": d_ev_w_in_t, "ev_w_out": d_ev_w_out, "ev_sg_w": d_sgw}
    if ride is None:
        grads.update({"od_w_in": d_od_w_in, "od_w_out": d_od_w_out, "od_w_a": d_od_w_a, "od_w_x": d_od_w_x})
    else:
        grads["od_slots"] = od_slots
    return grad_x, grads


ROW_DMOD, ROW_LN, ROW_SG_LN, ROW_SG_B, ROW_CONV_W, ROW_CONV_B, ROW_B_A, ROW_B_X, ROW_LAM, ROW_SINK, ROW_LOSS = (
    0, 6, 10, 11, 12, 16, 17, 19, 21, 23, 24)
SMALL_ROWS = 64


def _pack_small(vec_x, vec_e, vec_p, vec_o, vec_f, vec_r, vec_s, d_sink, d_sgb, loss):
    def body(x_ref, e_ref, p_ref, o_ref, f_ref, r_ref, s_ref, sink_ref, sgb_ref, loss_ref, rows_ref, dmod_ref):
        rows_ref[...] = jnp.zeros_like(rows_ref)
        dmod_ref[...] = jnp.zeros_like(dmod_ref)
        put = [(ROW_DMOD, x_ref, 0), (ROW_DMOD + 1, x_ref, 1), (ROW_DMOD + 2, e_ref, 2), (ROW_DMOD + 3, p_ref, 5),
               (ROW_DMOD + 4, p_ref, 6), (ROW_DMOD + 5, o_ref, 2), (ROW_LN, e_ref, 0), (ROW_LN + 1, e_ref, 1),
               (ROW_LN + 2, o_ref, 0), (ROW_LN + 3, o_ref, 1), (ROW_CONV_B, p_ref, 4), (ROW_B_A, f_ref, 0),
               (ROW_B_A + 1, r_ref, 0), (ROW_B_X, f_ref, 1), (ROW_B_X + 1, r_ref, 1), (ROW_LAM, f_ref, 2), (ROW_LAM + 1, r_ref, 2)]
        put += [(ROW_CONV_W + k, p_ref, k) for k in range(4)]
        for dst, ref, src in put:
            rows_ref[dst:dst + 1, :] = ref[src:src + 1, :]
            if dst < 6:
                dmod_ref[dst:dst + 1, :] = ref[src:src + 1, :]
        rows_ref[ROW_SG_LN:ROW_SG_LN + 1, 0:SG_WIDTH] = s_ref[0:1, :]
        rows_ref[ROW_SG_LN:ROW_SG_LN + 1, SG_WIDTH:2 * SG_WIDTH] = s_ref[1:2, :]
        lane = _lane_iota((1, LANES))
        sink = jnp.zeros((1, LANES), F32)
        for h in range(N_Q_HEADS):
            rows_ref[ROW_SG_B:ROW_SG_B + 1, h * LANES:(h + 1) * LANES] = sgb_ref[h:h + 1, :]
            sink = jnp.where(lane == h, sink_ref[h:h + 1, :], sink)
        rows_ref[ROW_SINK:ROW_SINK + 1, 0:LANES] = sink
        rows_ref[ROW_LOSS:ROW_LOSS + 1, 0:LANES] = jnp.where(lane == 0, loss_ref[0:1, :], 0.0)

    return pl.pallas_call(body, name="pack_small", out_shape=[_sds((SMALL_ROWS, D_MODEL)), _sds((8, D_MODEL))])(
        vec_x, vec_e, vec_p, vec_o, vec_f, vec_r, vec_s, d_sink, d_sgb, loss)


def _allgather8(block, name):
    m_per, n = block.shape

    def body(x_ref, out_ref, send_sems, recv_sems, local_sem):
        x, y, c = _place()
        me, sibling = (x, y, c), (x, y, 1 - c)
        chips = [(1 - x, y), (x, 1 - y), (1 - x, 1 - y)]

        def rows(px, py, pc):
            return out_ref.at[pl.ds((4 * px + 2 * py + pc) * m_per, m_per), :]

        def copy(k, blk, to, src=None):
            return pltpu.make_async_remote_copy(src_ref=rows(*blk) if src is None else src, dst_ref=rows(*blk),
                                                send_sem=send_sems.at[k], recv_sem=recv_sems.at[k], device_id=to,
                                                device_id_type=MESH)

        mine = pltpu.make_async_copy(x_ref, rows(*me), local_sem)
        mine.start()
        first = [copy(0, me, sibling, src=x_ref)] + [copy(1 + j, me, (*chip, c), src=x_ref) for j, chip in enumerate(chips)]
        for cp in first:
            cp.start()
        passed = [copy(4 + j, (*chip, c), sibling) for j, chip in enumerate(chips)]
        for j, chip in enumerate(chips):
            copy(1 + j, (*chip, c), me).wait_recv()
            passed[j].start()
        copy(0, sibling, me).wait_recv()
        for j, chip in enumerate(chips):
            copy(4 + j, (*chip, 1 - c), me).wait_recv()
        for cp in first + passed:
            cp.wait_send()
        mine.wait()

    return pl.pallas_call(
        body, name=name, out_shape=_sds((8 * m_per, n), block.dtype),
        in_specs=[pl.BlockSpec(memory_space=pltpu.VMEM)], out_specs=pl.BlockSpec(memory_space=pltpu.VMEM),
        scratch_shapes=[pltpu.SemaphoreType.DMA((7,)), pltpu.SemaphoreType.DMA((7,)), pltpu.SemaphoreType.DMA],
        compiler_params=pltpu.CompilerParams(vmem_limit_bytes=VMEM_LIMIT),
    )(block)


class _Copies:
    def __init__(self, send_sems, recv_sems, local_sems, stages):
        self.send_sems, self.recv_sems, self.local_sems, self.stages = send_sems, recv_sems, local_sems, stages
        self.sent, self.staged, self.locals = [], [], []

    def remote(self, k, src, dst, to):
        return pltpu.make_async_remote_copy(src_ref=src, dst_ref=dst, send_sem=self.send_sems.at[k], recv_sem=self.recv_sems.at[k],
                                            device_id=to, device_id_type=MESH)

    def send(self, k, src, dst, to):
        cp = self.remote(k, src, dst, to)
        cp.start()
        self.sent.append(cp)

    def arrived(self, k, dst, frm):
        self.remote(k, dst, dst, frm).wait_recv()

    def local(self, src, dst):
        k = len(self.staged)
        cp = pltpu.make_async_copy(src, self.stages[k], self.local_sems.at[2 * k])
        cp.start()
        self.staged.append((cp, dst))

    def flush(self):
        for k in range(len(self.locals), len(self.staged)):
            cp, dst = self.staged[k]
            cp.wait()
            out = pltpu.make_async_copy(self.stages[k], dst, self.local_sems.at[2 * k + 1])
            out.start(priority=1)
            self.locals.append(out)

    def drain(self):
        self.flush()
        for cp in self.sent:
            cp.wait_send()
        for cp in self.locals:
            cp.wait()


def _comm_call(body, name, ins, out_shapes, n_remote, stages, side=None):
    n_in, n_out = len(ins), len(out_shapes)
    side_fn, side_ins, side_outs = side if side is not None else (None, (), [])
    s_in, s_out = len(side_ins), len(side_outs)

    def kern(*refs):
        in_refs, refs = refs[:n_in], refs[n_in:]
        side_in_refs, refs = refs[:s_in], refs[s_in:]
        out_refs, refs = refs[:n_out], refs[n_out:]
        side_out_refs, refs = refs[:s_out], refs[s_out:]
        if side is None:
            body(_Copies(refs[0], refs[1], refs[2], refs[3:]), in_refs, out_refs)
            return
        side_bufs, side_sems, refs = refs[:s_out], refs[s_out], refs[s_out + 1:]
        cps = _Copies(refs[0], refs[1], refs[2], refs[3:])
        leave = [pltpu.make_async_copy(side_bufs[k], side_out_refs[k], side_sems.at[k]) for k in range(s_out)]

        def run_side():
            side_fn(*side_in_refs, *side_bufs)
            for k, cp in enumerate(leave):
                cp.start(priority=k % 2)

        body(cps, in_refs, out_refs, run_side)
        for cp in leave:
            cp.wait()

    hbm, vmem = pl.BlockSpec(memory_space=pl.ANY), pl.BlockSpec(memory_space=pltpu.VMEM)
    side_scratch = [] if side is None else [pltpu.VMEM(o.shape, o.dtype) for o in side_outs] + [pltpu.SemaphoreType.DMA((s_out,))]
    return pl.pallas_call(
        kern, name=name, out_shape=list(out_shapes) + list(side_outs), in_specs=[hbm] * n_in + [vmem] * s_in,
        out_specs=[hbm] * (n_out + s_out),
        scratch_shapes=side_scratch + [pltpu.SemaphoreType.DMA((n_remote,)), pltpu.SemaphoreType.DMA((n_remote,)),
                                       pltpu.SemaphoreType.DMA((2 * len(stages),))] + [pltpu.VMEM(s, d) for s, d in stages],
        compiler_params=pltpu.CompilerParams(vmem_limit_bytes=VMEM_LIMIT),
    )(*ins, *side_ins)


def _gather_to_all(cps, pairs, me, sibling, other_chips, c, base, meanwhile=None):
    idx = lambda p: 4 * p[0] + 2 * p[1] + p[2]
    for i, (src, dst) in enumerate(pairs):
        cps.local(src, dst.at[idx(me)])
        cps.send(base + 7 * i, src, dst.at[idx(me)], sibling)
        for j, chip in enumerate(other_chips):
            cps.send(base + 7 * i + 1 + j, src, dst.at[idx(me)], (*chip, c))
    cps.flush()
    if meanwhile is not None:
        meanwhile()
    for j, chip in enumerate(other_chips):
        for i, (_, dst) in enumerate(pairs):
            got = dst.at[idx((*chip, c))]
            cps.arrived(base + 7 * i + 1 + j, got, (*chip, c))
            cps.send(base + 7 * i + 4 + j, got, got, sibling)
    for i, (_, dst) in enumerate(pairs):
        cps.arrived(base + 7 * i, dst.at[idx(sibling)], sibling)
        for j, chip in enumerate(other_chips):
            cps.arrived(base + 7 * i + 4 + j, dst.at[idx((*chip, 1 - c))], sibling)


def _gather_weights(shards, small, side):
    n = len(shards)

    def body(cps, ins, outs, run_side):
        x, y, c = _place()
        me, sibling, mine = (x, y, c), (x, y, 1 - c), 2 * x + y
        chips = [(1 - x, y), (x, 1 - y), (1 - x, 1 - y)]
        for i in range(n):
            cps.local(ins[i], outs[i].at[mine])
        for j, (px, py) in enumerate(chips):
            for i in range(n):
                hr = shards[i].shape[0] // 2
                rows = pl.ds(c * hr, hr)
                cps.send(6 * i + j, ins[i].at[rows], outs[i].at[mine, rows], (px, py, c))
        _gather_to_all(cps, [(ins[n], outs[n])], me, sibling, chips, c, 6 * n, meanwhile=run_side)
        for j, (px, py) in enumerate(chips):
            for i in range(n):
                hr = shards[i].shape[0] // 2
                got = outs[i].at[2 * px + py, pl.ds(c * hr, hr)]
                cps.arrived(6 * i + j, got, (px, py, c))
                cps.send(6 * i + 3 + j, got, got, sibling)
        for j, (px, py) in enumerate(chips):
            for i in range(n):
                hr = shards[i].shape[0] // 2
                cps.arrived(6 * i + 3 + j, outs[i].at[2 * px + py, pl.ds((1 - c) * hr, hr)], sibling)
        cps.drain()

    return _comm_call(body, "gather_weights", list(shards) + [small],
                      [_sds((4,) + s.shape, s.dtype) for s in shards] + [_sds((8,) + small.shape, small.dtype)], 6 * n + 7,
                      [(a.shape, a.dtype) for a in list(shards) + [small]], side)


def _reduce_sibling(parts, dmod_rows):
    n = len(parts)

    def body(cps, ins, outs):
        x, y, c = _place()
        me, sibling = (x, y, c), (x, y, 1 - c)
        chips = [(1 - x, y), (x, 1 - y), (1 - x, 1 - y)]
        for i in range(n):
            cps.send(i, ins[i].at[:, 1 - c], outs[i], sibling)
        _gather_to_all(cps, [(ins[n], outs[n])], me, sibling, chips, c, n)
        for i in range(n):
            cps.arrived(i, outs[i], sibling)
        cps.drain()

    return _comm_call(body, "reduce_sibling", list(parts) + [dmod_rows],
                      [_sds((4,) + p.shape[2:], p.dtype) for p in parts] + [_sds((8,) + dmod_rows.shape, dmod_rows.dtype)], n + 7,
                      [(dmod_rows.shape, dmod_rows.dtype)])


def _reduce_chips(parts):
    n = len(parts)

    def body(cps, ins, outs):
        x, y, c = _place()
        mine = 2 * x + y
        chips = _other_chips(x, y)
        for i in range(n):
            cps.local(ins[i].at[mine], outs[i].at[mine])
        for j, (px, py) in enumerate(chips):
            for i in range(n):
                cps.send(3 * i + j, ins[i].at[2 * px + py], outs[i].at[mine], (px, py, c))
        cps.flush()
        for j, (px, py) in enumerate(chips):
            for i in range(n):
                cps.arrived(3 * i + j, outs[i].at[2 * px + py], (px, py, c))
        cps.drain()

    return _comm_call(body, "reduce_chips", list(parts), [_sds(p.shape, p.dtype) for p in parts], 3 * n,
                      [(p.shape[1:], p.dtype) for p in parts])


def _gather_reduced(shard_parts, repl_parts):
    ns, nr = len(shard_parts), len(repl_parts)

    def body(cps, ins, outs):
        x, y, c = _place()
        me, sibling = (x, y, c), (x, y, 1 - c)
        chips = [(1 - x, y), (x, 1 - y), (1 - x, 1 - y)]
        for i in range(ns):
            cps.local(ins[i], outs[i].at[c])
            cps.send(i, ins[i], outs[i].at[c], sibling)
        _gather_to_all(cps, [(ins[ns + i], outs[ns + i]) for i in range(nr)], me, sibling, chips, c, ns)
        for i in range(ns):
            cps.arrived(i, outs[i].at[1 - c], sibling)
        cps.drain()

    return _comm_call(body, "gather_reduced", list(shard_parts) + list(repl_parts),
                      [_sds((2,) + p.shape, p.dtype) for p in shard_parts] + [_sds((8,) + p.shape, p.dtype) for p in repl_parts],
                      ns + 7 * nr, [(p.shape, p.dtype) for p in list(shard_parts) + list(repl_parts)])


def _sum_sibling(core, parts, got, wire, name):
    n = len(parts)

    def body(core_ref, *refs):
        for i in range(n):
            refs[2 * n + i][0] = (refs[i][0] + refs[n + i][0]).astype(wire[i])

    keep_spec = lambda p: pl.BlockSpec((1, None) + p.shape[2:], lambda s, core_ref: (s, core_ref[0], 0, 0))
    slot_spec = lambda p: pl.BlockSpec((1,) + p.shape[2:], lambda s, core_ref: (s, 0, 0))
    return pl.pallas_call(
        body, name=name,
        grid_spec=pltpu.PrefetchScalarGridSpec(
            num_scalar_prefetch=1, grid=(4,), in_specs=[keep_spec(p) for p in parts] + [slot_spec(p) for p in parts],
            out_specs=[slot_spec(p) for p in parts]),
        out_shape=[_sds((4,) + p.shape[2:], wire[i]) for i, p in enumerate(parts)],
        compiler_params=_params("parallel"),
    )(core, *parts, *got)


def _sum_slots(slots, name):
    n = len(slots)

    def spec_pair(p):
        k, rows, cols = p.shape
        sub = 16 if p.dtype == BF16 else 8
        if (rows // 2) % sub == 0:
            return pl.BlockSpec((k, rows // 2, cols), lambda i: (0, i, 0)), pl.BlockSpec((rows // 2, cols), lambda i: (i, 0))
        return pl.BlockSpec((k, rows, cols), lambda i: (0, 0, 0)), pl.BlockSpec((rows, cols), lambda i: (0, 0))

    pairs = [spec_pair(p) for p in slots]

    def body(*refs):
        for i in range(n):
            acc = refs[i][0].astype(F32)
            for j in range(1, slots[i].shape[0]):
                acc = acc + refs[i][j].astype(F32)
            refs[n + i][...] = acc

    return pl.pallas_call(
        body, name=name, grid=(2,), in_specs=[a for a, _ in pairs], out_specs=[b for _, b in pairs],
        out_shape=[_sds(p.shape[1:]) for p in slots], compiler_params=_params("arbitrary"),
    )(*slots)


def _unpack_small(g_small):
    q = D_MODEL // 4

    def body(g_ref, c_ref, cw_ref, cb_ref, ba_ref, bx_ref, lam_ref):
        for d in range(8):
            c_ref[d:d + 1, :] = g_ref[d, 0:1, :]
        for s in range(4):
            cols = slice(s * q, (s + 1) * q)
            for k in range(4):
                cw_ref[k:k + 1, cols] = g_ref[2 * s, 1:2, k * q:(k + 1) * q]
            cb_ref[0:1, cols] = g_ref[2 * s, 2:3, 0:q]
            for k in range(2):
                ba_ref[k:k + 1, cols] = g_ref[2 * s, 2:3, (1 + k) * q:(2 + k) * q]
                bx_ref[k:k + 1, cols] = g_ref[2 * s, 3:4, k * q:(k + 1) * q]
                lam_ref[k:k + 1, cols] = g_ref[2 * s, 3:4, (2 + k) * q:(3 + k) * q]

    return pl.pallas_call(
        body, name="unpack_small",
        out_shape=[_sds((8, D_MODEL)), _sds((4, D_MODEL)), _sds((1, D_MODEL))] + [_sds((2, D_MODEL))] * 3,
    )(g_small)


def _modulation(c_all, ada_w, ada_b):
    cols = ada_w.shape[2]

    def body(c_ref, w_ref, b_ref, o_ref):
        cc = c_ref[...]
        o_ref[0] = _mm(cc * _sigmoid(cc), w_ref[0]) + b_ref[0]

    return pl.pallas_call(
        body, name="modulation", grid=(2,),
        in_specs=[_full((8, D_MODEL)), pl.BlockSpec((1, D_MODEL, cols), lambda l: (l, 0, 0)), pl.BlockSpec((1, 1, cols), lambda l: (l, 0, 0))],
        out_specs=pl.BlockSpec((1, 8, cols), lambda l: (l, 0, 0)), out_shape=_sds((2, 8, cols)),
        compiler_params=_params("parallel"),
    )(c_all, ada_w, ada_b)


def _adamw_math(w, g, m, v):
    m = ADAM_B1 * m + (1.0 - ADAM_B1) * g
    v = ADAM_B2 * v + (1.0 - ADAM_B2) * (g * g)
    m_hat = m / (1.0 - ADAM_B1 ** ADAM_STEP)
    v_hat = v / (1.0 - ADAM_B2 ** ADAM_STEP)
    delta = -ADAM_LR * (m_hat / (jnp.sqrt(v_hat) + ADAM_EPS) + ADAM_WD * w)
    return delta, m, v


def _ada_update(c_all, dmod, w, m, v, rider=None):
    cols = w.shape[2]
    tr = 256
    per = D_MODEL // tr
    spec3 = pl.BlockSpec((1, tr, cols), lambda i: (i // per, i % per, 0))

    def body(c_ref, d_ref, w_ref, m_ref, v_ref, g_ref, dl_ref, nm_ref, nv_ref):
        cc = c_ref[...]
        g = _mm_tn(cc * _sigmoid(cc), d_ref[0])
        g_ref[0] = g
        dl_ref[0], nm_ref[0], nv_ref[0] = _adamw_math(w_ref[0], g, m_ref[0], v_ref[0])

    return _call(
        body, "ada_update", (2 * per,),
        [pl.BlockSpec((8, tr), lambda i: (0, i % per)), pl.BlockSpec((1, 8, cols), lambda i: (i // per, 0, 0)), spec3, spec3, spec3],
        [spec3] * 4, [_sds(w.shape)] * 4, (c_all, dmod, w, m, v), "parallel", rider=rider)


def _adamw_matrices(params):
    n = len(params)
    steps = 8

    def body(*refs):
        ins, outs = refs[:4 * n], refs[4 * n:]
        for j in range(n):
            w_ref, g_ref, m_ref, v_ref = ins[4 * j:4 * j + 4]
            g = g_ref[...]
            outs[4 * j][...] = g
            outs[4 * j + 1][...], outs[4 * j + 2][...], outs[4 * j + 3][...] = _adamw_math(w_ref[...], g, m_ref[...], v_ref[...])

    spec = lambda p: _rows(p[0].shape[0] // steps, p[0].shape[1])
    res = pl.pallas_call(
        body, name="adamw_matrices", grid=(steps,), in_specs=[spec(p) for p in params for _ in range(4)],
        out_specs=[spec(p) for p in params for _ in range(4)], out_shape=[_sds(p[0].shape) for p in params for _ in range(4)],
        compiler_params=_params("parallel"),
    )(*[a for p in params for a in p])
    return [tuple(res[4 * j:4 * j + 4]) for j in range(n)]


def _adamw_small(gs, chip, params):
    n = len(params)
    shard_cols = D_MODEL // 4

    def body(chip_ref, rows_ref, cols_ref, *refs):
        ins, outs = refs[:3 * n], refs[3 * n:]
        for j in range(n):
            w_ref, m_ref, v_ref = ins[3 * j:3 * j + 3]
            g_ref, d_ref, nm_ref, nv_ref = outs[4 * j:4 * j + 4]
            for dst, sharded, src in params[j][3]:
                g = (cols_ref if sharded else rows_ref)[src]
                g_ref[dst] = g
                d_ref[dst], nm_ref[dst], nv_ref[dst] = _adamw_math(w_ref[dst], g, m_ref[dst], v_ref[dst])

    whole = lambda a: pl.BlockSpec(a.shape, lambda i, chip_ref: (0, 0))
    flat = [a for p in params for a in p[:3]]
    res = pl.pallas_call(
        body, name="adamw_small",
        grid_spec=pltpu.PrefetchScalarGridSpec(
            num_scalar_prefetch=1, grid=(1,),
            in_specs=[whole(gs), pl.BlockSpec((gs.shape[0], shard_cols), lambda i, chip_ref: (0, chip_ref[0]))] + [whole(a) for a in flat],
            out_specs=[whole(p[0]) for p in params for _ in range(4)]),
        out_shape=[_sds(p[0].shape) for p in params for _ in range(4)],
        compiler_params=_params("arbitrary"),
    )(chip, gs, gs, *flat)
    return [tuple(res[4 * j:4 * j + 4]) for j in range(n)]


def _cols(a, start, size):
    return lax.dynamic_slice_in_dim(a, start, size, axis=a.ndim - 1)


def kernel(x, c, positions, ada_w, ada_b, ln_g, ln_b, ev_w_in, ev_w_out, ev_sink, ev_sg_ln_g, ev_sg_ln_b, ev_sg_w, ev_sg_b, od_w_in, od_conv_w, od_conv_b, od_w_a, od_b_a, od_w_x, od_b_x, od_lam, od_w_out, loss_target, m_ada_w, m_ada_b, m_ln_g, m_ln_b, m_ev_w_in, m_ev_w_out, m_ev_sink, m_ev_sg_ln_g, m_ev_sg_ln_b, m_ev_sg_w, m_ev_sg_b, m_od_w_in, m_od_conv_w, m_od_conv_b, m_od_w_a, m_od_b_a, m_od_w_x, m_od_b_x, m_od_lam, m_od_w_out, v_ada_w, v_ada_b, v_ln_g, v_ln_b, v_ev_w_in, v_ev_w_out, v_ev_sink, v_ev_sg_ln_g, v_ev_sg_ln_b, v_ev_sg_w, v_ev_sg_b, v_od_w_in, v_od_conv_w, v_od_conv_b, v_od_w_a, v_od_b_a, v_od_w_x, v_od_b_x, v_od_lam, v_od_w_out):
    seq = x.shape[1]
    px, py, pc = _place()
    chip = 2 * px + py
    dev = 2 * chip + pc

    small = jnp.concatenate([od_conv_w[0].reshape(-1), od_conv_b[0], od_b_a[0].reshape(-1), jnp.zeros((256,), F32),
                             od_b_x[0].reshape(-1), od_lam[0].reshape(-1)]).reshape(3, D_MODEL)
    blk = jnp.concatenate([c, small, jnp.zeros((4, D_MODEL), F32)], axis=0)
    tr = lambda a: jnp.swapaxes(a, -1, -2)
    wire_w = lambda a: a.astype(MXU_DTYPE)
    posf = positions.astype(F32).reshape(seq, 1)
    ev_w_in4, g_small, *tabs = _gather_weights([wire_w(tr(ev_w_in[0]))], blk, _rope_tables(posf, seq))
    core = pc.astype(jnp.int32).reshape(1)
    ride = {"ev_w_out": wire_w(ev_w_out[0]), "od_w_in": wire_w(od_w_in[0]), "od_w_out": wire_w(od_w_out[0]),
            "core": core, "wire": MXU_DTYPE}
    c_all, conv_w, conv_b, b_a, b_x, lam = _unpack_small(g_small)
    conv_b = conv_b.reshape(D_MODEL)

    w_full = {
        "ev_w_in_t": ev_w_in4.reshape(EVEN_IN, D_MODEL),
        "ev_sink": ev_sink[0], "ev_sg_ln_g": ev_sg_ln_g[0], "ev_sg_ln_b": ev_sg_ln_b[0], "ev_sg_w": ev_sg_w[0],
        "ev_sg_b": ev_sg_b[0], "od_conv_w": conv_w, "od_conv_b": conv_b, "od_w_a": od_w_a[0], "od_b_a": b_a,
        "od_w_x": od_w_x[0], "od_b_x": b_x, "od_lam": lam, "ln_g": ln_g, "ln_b": ln_b,
    }

    ada_cols = ada_w.shape[2]
    mod_sh = _modulation(c_all, ada_w, _cols(ada_b, chip * ada_cols, ada_cols).reshape(2, 1, ada_cols))
    mod_all = _allgather8(mod_sh.reshape(16, ada_cols), "gather_mod").reshape(4, 2, 2, 8, ada_cols)[:, 0]
    mod_mine = lax.dynamic_index_in_dim(mod_all, dev, axis=2, keepdims=False)
    mod = mod_mine.transpose(1, 0, 2).reshape(2, 3, D_MODEL)

    grad_x, g = _local_step(x[0], tabs, loss_target[0], mod, w_full, seq, ride)

    parts = [g["ev_w_in_t"].reshape(4, 2, 352, D_MODEL), g["ev_w_out"].reshape(4, 2, 128, D_MODEL),
             g["ev_sg_w"].reshape(4, 2, BLK, BLK), g["rows"].reshape(4, 2, SMALL_ROWS // 8, D_MODEL)]
    wire = [MXU_DTYPE] * 3 + [F32]
    *got, dmod_gathered = _reduce_sibling(parts, g["dmod_blk"])
    ev_slots = list(_reduce_chips(_sum_sibling(core, parts, got, wire, "sum_sibling")))
    od_slots = list(g["od_slots"])
    mine = _sum_slots(ev_slots[0:2] + od_slots[0:2] + ev_slots[2:3] + od_slots[2:4] + ev_slots[3:4], "sum_chips")
    reduced = _gather_reduced(mine[:4], mine[4:])
    g_ev_w_in_t = reduced[0].reshape(704, D_MODEL)
    g_ev_w_out = reduced[1].reshape(256, D_MODEL)
    g_od_w_in = reduced[2].reshape(D_MODEL, 512)
    g_od_w_out = reduced[3].reshape(256, D_MODEL)
    g_sg_w = reduced[4].reshape(8 * BLK, BLK)
    g_w_a = reduced[5].reshape(16 * BLK, BLK)
    g_w_x = reduced[6].reshape(16 * BLK, BLK)
    gs = reduced[7].reshape(SMALL_ROWS, D_MODEL)
    loss = gs[ROW_LOSS, 0]
    dmod_all = dmod_gathered[:, 0:6].reshape(8, 2, 3 * D_MODEL)
    dmod_sh = _cols(dmod_all, chip * ada_cols, ada_cols).transpose(1, 0, 2)
    (g_ada_w, d_ada_w, nm_ada_w, nv_ada_w), _ = _ada_update(c_all, dmod_sh, ada_w, m_ada_w, v_ada_w)

    mats = (("ev_w_out", ev_w_out, g_ev_w_out, m_ev_w_out, v_ev_w_out), ("od_w_in", od_w_in, g_od_w_in, m_od_w_in, v_od_w_in),
            ("od_w_out", od_w_out, g_od_w_out, m_od_w_out, v_od_w_out), ("ev_sg_w", ev_sg_w, g_sg_w, m_ev_sg_w, v_ev_sg_w),
            ("od_w_a", od_w_a, g_w_a, m_od_w_a, v_od_w_a), ("od_w_x", od_w_x, g_w_x, m_od_w_x, v_od_w_x))
    upd = _adamw_matrices([(tr(ev_w_in[0]), g_ev_w_in_t, tr(m_ev_w_in[0]), tr(v_ev_w_in[0]))]
                          + [(w_.reshape(g_.shape), g_, m_.reshape(g_.shape), v_.reshape(g_.shape)) for _, w_, g_, m_, v_ in mats])
    big = {"ev_w_in": tuple(tr(a).reshape(ev_w_in.shape) for a in upd[0])}
    for (name, w_, _, _, _), u in zip(mats, upd[1:]):
        big[name] = tuple(a.reshape(w_.shape) for a in u)
    big["ada_w"] = (g_ada_w, d_ada_w, nm_ada_w, nv_ada_w)

    at = lambda r0, nr, c0, nc: (slice(r0, r0 + nr), slice(c0, c0 + nc))
    local = lambda r0, nr: ((nr, 256), [(at(0, nr, 0, 256), True, at(r0, nr, 0, 256))])
    small_g = {
        "ada_b": ((2, 3 * D_MODEL), [(at(l, 1, k * D_MODEL, D_MODEL), False, at(ROW_DMOD + 3 * l + k, 1, 0, D_MODEL))
                                     for l in range(2) for k in range(3)]),
        "ln_g": ((2, D_MODEL), [(at(l, 1, 0, D_MODEL), False, at(ROW_LN + 2 * l, 1, 0, D_MODEL)) for l in range(2)]),
        "ln_b": ((2, D_MODEL), [(at(l, 1, 0, D_MODEL), False, at(ROW_LN + 1 + 2 * l, 1, 0, D_MODEL)) for l in range(2)]),
        "ev_sink": ((1, N_Q_HEADS), [(at(0, 1, 0, N_Q_HEADS), False, at(ROW_SINK, 1, 0, N_Q_HEADS))]),
        "ev_sg_ln_g": ((1, SG_WIDTH), [(at(0, 1, 0, SG_WIDTH), False, at(ROW_SG_LN, 1, 0, SG_WIDTH))]),
        "ev_sg_ln_b": ((1, SG_WIDTH), [(at(0, 1, 0, SG_WIDTH), False, at(ROW_SG_LN, 1, SG_WIDTH, SG_WIDTH))]),
        "ev_sg_b": ((N_SG_GROUPS, BLK), [(at(j, 1, 0, BLK), False, at(ROW_SG_B, 1, j * BLK, BLK)) for j in range(N_SG_GROUPS)]),
        "od_conv_w": local(ROW_CONV_W, 4), "od_conv_b": local(ROW_CONV_B, 1), "od_b_a": local(ROW_B_A, 2),
        "od_b_x": local(ROW_B_X, 2), "od_lam": local(ROW_LAM, 2),
    }
    small_in = {"ada_b": (ada_b, m_ada_b, v_ada_b), "ln_g": (ln_g, m_ln_g, v_ln_g), "ln_b": (ln_b, m_ln_b, v_ln_b),
                "ev_sink": (ev_sink, m_ev_sink, v_ev_sink), "ev_sg_ln_g": (ev_sg_ln_g, m_ev_sg_ln_g, v_ev_sg_ln_g),
                "ev_sg_ln_b": (ev_sg_ln_b, m_ev_sg_ln_b, v_ev_sg_ln_b), "ev_sg_b": (ev_sg_b, m_ev_sg_b, v_ev_sg_b),
                "od_conv_w": (od_conv_w, m_od_conv_w, v_od_conv_w), "od_conv_b": (od_conv_b, m_od_conv_b, v_od_conv_b),
                "od_b_a": (od_b_a, m_od_b_a, v_od_b_a), "od_b_x": (od_b_x, m_od_b_x, v_od_b_x),
                "od_lam": (od_lam, m_od_lam, v_od_lam)}
    names_small = list(small_g)
    upd = _adamw_small(gs, chip.astype(jnp.int32).reshape(1),
                       [tuple(a.reshape(small_g[n][0]) for a in small_in[n]) + (small_g[n][1],) for n in names_small])
    res = dict(big)
    for n, u in zip(names_small, upd):
        res[n] = tuple(a.reshape(small_in[n][0].shape) for a in u)

    order = ["ada_w", "ada_b", "ln_g", "ln_b", "ev_w_in", "ev_w_out", "ev_sink", "ev_sg_ln_g", "ev_sg_ln_b", "ev_sg_w", "ev_sg_b",
             "od_w_in", "od_conv_w", "od_conv_b", "od_w_a", "od_b_a", "od_w_x", "od_b_x", "od_lam", "od_w_out"]
    return (loss, grad_x.reshape(x.shape), *[res[n][0] for n in order], *[res[n][1] for n in order],
            *[res[n][2] for n in order], *[res[n][3] for n in order])
```

```python
from functools import partial

import jax
import jax.numpy as jnp
import numpy as np
from jax import lax
from jax.experimental import pallas as pl
from jax.experimental.pallas import tpu as pltpu

F32 = jnp.float32
BF16 = jnp.bfloat16
MXU_DTYPE = BF16
ACT_DTYPE = MXU_DTYPE

D_MODEL = 1024
HEAD_DIM = 64
N_Q_HEADS = 8
Q_PER_KV = 4
ATTN_WIDTH = 512
BLK = 128
ROPE_DIM = 16
ROPE_THETA = 500000.0
N_SG_GROUPS = 8
SG_WIDTH = 512
EVEN_IN = 2816
ODD_IN = 2048
RNN_HEADS = 8
RG_LRU_C = 8.0
ALPHA = (2 * 2) ** 0.25
LN_EPS = 1e-5
NEG_INF = -1e30
ADAM_LR, ADAM_B1, ADAM_B2, ADAM_EPS, ADAM_WD, ADAM_STEP = 0.001, 0.9, 0.999, 1e-08, 0.01, 10

LANES = 128
VMEM_LIMIT = 56 * 1024 * 1024
MESH = pl.DeviceIdType.MESH


def _mm(a, b):
    return jnp.dot(a.astype(MXU_DTYPE), b.astype(MXU_DTYPE), preferred_element_type=F32)


def _mm_nt(a, b):
    return lax.dot_general(a.astype(MXU_DTYPE), b.astype(MXU_DTYPE), (((1,), (1,)), ((), ())), preferred_element_type=F32)


def _mm_tn(a, b):
    return lax.dot_general(a.astype(MXU_DTYPE), b.astype(MXU_DTYPE), (((0,), (0,)), ((), ())), preferred_element_type=F32)


def _sigmoid(x):
    return 1.0 / (1.0 + jnp.exp(-x))


def _ln_stats(z):
    mu = jnp.mean(z, axis=-1, keepdims=True)
    d = z - mu
    var = jnp.mean(d * d, axis=-1, keepdims=True)
    rstd = lax.rsqrt(var + LN_EPS)
    return d * rstd, rstd


def _ln_bwd(dout, zhat, rstd, g):
    dzh = dout * g
    m1 = jnp.mean(dzh, axis=-1, keepdims=True)
    m2 = jnp.mean(dzh * zhat, axis=-1, keepdims=True)
    return rstd * (dzh - m1 - zhat * m2)


def _group_sum(x, e2):
    hi = x.astype(MXU_DTYPE)
    lo = (x - hi.astype(F32)).astype(MXU_DTYPE)
    return jnp.dot(hi, e2, preferred_element_type=F32) + jnp.dot(lo, e2, preferred_element_type=F32)


def _lane_iota(shape):
    return lax.broadcasted_iota(jnp.int32, shape, 1)


def _to_kv_lanes(t, h):
    src_lo = (h % 2 == 0)
    dst_lo = (h // Q_PER_KV == 0)
    if src_lo != dst_lo:
        t = pltpu.roll(t, HEAD_DIM, 1)
    lane = _lane_iota(t.shape)
    keep = (lane < HEAD_DIM) if dst_lo else (lane >= HEAD_DIM)
    return jnp.where(keep, t, 0.0)


def _from_kv_lanes(t, h):
    src_lo = (h // Q_PER_KV == 0)
    dst_lo = (h % 2 == 0)
    lane = _lane_iota(t.shape)
    keep = (lane < HEAD_DIM) if src_lo else (lane >= HEAD_DIM)
    t = jnp.where(keep, t, 0.0)
    if src_lo != dst_lo:
        t = pltpu.roll(t, HEAD_DIM, 1)
    return t


def _rope(t, cos_t, sin_p, sin_m):
    half = ROPE_DIM // 2
    return t * cos_t + pltpu.roll(t, half, 1) * sin_p + pltpu.roll(t, LANES - half, 1) * sin_m


def _rope_t(d, cos_t, sin_p, sin_m):
    half = ROPE_DIM // 2
    return d * cos_t + pltpu.roll(d * sin_p, LANES - half, 1) + pltpu.roll(d * sin_m, half, 1)


def _band(ref, n, nb):
    prev = jnp.maximum(n - 1, 0)
    nxt = jnp.minimum(n + 1, nb - 1)
    rows = [ref[pl.ds(pl.multiple_of(j * BLK, BLK), BLK), :] for j in (prev, n, nxt)]
    return jnp.concatenate(rows, axis=0)


def _band_bias(n, seq):
    qi = lax.broadcasted_iota(jnp.int32, (BLK, 3 * BLK), 0)
    kj = lax.broadcasted_iota(jnp.int32, (BLK, 3 * BLK), 1)
    k_abs = n * BLK - BLK + kj
    valid = (jnp.abs(kj - BLK - qi) <= BLK) & (k_abs >= 0) & (k_abs < seq)
    bias = jnp.where(valid, 0.0, NEG_INF)
    return jnp.concatenate([bias] * Q_PER_KV, axis=0)


def _stack_heads(tile_of, kv):
    return jnp.concatenate([_to_kv_lanes(tile_of(h // 2), h) for h in range(Q_PER_KV * kv, Q_PER_KV * (kv + 1))], axis=0)


def _per_head_column(vals):
    row = lax.broadcasted_iota(jnp.int32, (Q_PER_KV * BLK, 1), 0)
    return jnp.where(row < BLK, vals[0], jnp.where(row < 2 * BLK, vals[1], jnp.where(row < 3 * BLK, vals[2], vals[3])))


def _softplus_neg(lam):
    e = jnp.exp(-jnp.abs(lam))
    u = 1.0 + e
    log1p_e = jnp.where(u == 1.0, e, jnp.log(u) * (e / (u - 1.0)))
    sp = jnp.maximum(-lam, 0.0) + log1p_e
    dsp = -1.0 / (1.0 + jnp.exp(lam))
    return sp, dsp


def _full(shape):
    return pl.BlockSpec(shape, lambda *_: (0,) * len(shape))


def _const(shape):
    return pl.BlockSpec(shape, lambda *_: (0,) * len(shape), pipeline_mode=pl.Buffered(1))


def _rows(tm, n):
    return pl.BlockSpec((tm, n), lambda i: (i, 0))


def _params(*sem):
    return pltpu.CompilerParams(dimension_semantics=sem, vmem_limit_bytes=VMEM_LIMIT)


def _sds(shape, dtype=F32):
    return jax.ShapeDtypeStruct(shape, dtype)


def _place():
    return lax.axis_index("x"), lax.axis_index("y"), lax.axis_index("c")


class _Rider:
    def __init__(self, ins, out_shapes, n_remote, n_local, plan):
        self.ins, self.out_shapes, self.n_remote, self.n_local, self.plan = list(ins), list(out_shapes), n_remote, n_local, plan

    def scratch(self):
        return [pltpu.SemaphoreType.DMA((self.n_remote,)), pltpu.SemaphoreType.DMA((self.n_remote,)),
                pltpu.SemaphoreType.DMA((max(self.n_local, 1),))]

    def run(self, first, in_refs, out_refs, sems):
        send_sems, recv_sems, local_sems = sems
        sends, recvs, locals_ = self.plan(in_refs, out_refs)
        remote = lambda k, src, dst, to: pltpu.make_async_remote_copy(
            src_ref=src, dst_ref=dst, send_sem=send_sems.at[k], recv_sem=recv_sems.at[k], device_id=to, device_id_type=MESH)
        if first:
            for k, src, dst, to in sends:
                remote(k, src, dst, to).start()
            for j, (src, dst) in enumerate(locals_):
                pltpu.make_async_copy(src, dst, local_sems.at[j]).start()
        else:
            for k, dst, frm in recvs:
                remote(k, dst, dst, frm).wait_recv()
            for k, src, dst, to in sends:
                remote(k, src, dst, to).wait_send()
            for j, (src, dst) in enumerate(locals_):
                pltpu.make_async_copy(src, dst, local_sems.at[j]).wait()


def _other_chips(x, y):
    return [(1 - x, y), (x, 1 - y), (1 - x, 1 - y)]


def _gather_rider(shard):
    hr = shard.shape[0] // 2

    def plan(ins, outs):
        x, y, c = _place()
        mine, src, dst = 2 * x + y, ins[0], outs[0]
        sends, recvs = [], []
        for j, (px, py) in enumerate(_other_chips(x, y)):
            for flip in range(2):
                tc = c if flip == 0 else 1 - c
                sends.append((2 * j + flip, src.at[pl.ds(c * hr, hr)], dst.at[mine, pl.ds(c * hr, hr)], (px, py, tc)))
                recvs.append((2 * j + flip, dst.at[2 * px + py, pl.ds(tc * hr, hr)], (px, py, tc)))
        return sends, recvs, [(src, dst.at[mine])]

    return _Rider([shard], [_sds((4,) + shard.shape, shard.dtype)], 6, 1, plan)


def _sibling_swap_rider(parts):
    n = len(parts)

    def plan(ins, outs):
        x, y, c = _place()
        sibling = (x, y, 1 - c)
        return ([(i, ins[i].at[:, 1 - c], outs[i], sibling) for i in range(n)], [(i, outs[i], sibling) for i in range(n)], [])

    return _Rider(parts, [_sds((4,) + p.shape[2:], p.dtype) for p in parts], n, 0, plan)


def _chip_exchange_rider(parts):
    n = len(parts)

    def plan(ins, outs):
        x, y, c = _place()
        mine = 2 * x + y
        sends, recvs = [], []
        for i in range(n):
            for j, (px, py) in enumerate(_other_chips(x, y)):
                sends.append((3 * i + j, ins[i].at[2 * px + py], outs[i].at[mine], (px, py, c)))
                recvs.append((3 * i + j, outs[i].at[2 * px + py], (px, py, c)))
        return sends, recvs, [(ins[i].at[mine], outs[i].at[mine]) for i in range(n)]

    return _Rider(parts, [_sds(p.shape, p.dtype) for p in parts], 3 * n, n, plan)


def _call(body, name, grid, in_specs, out_specs, out_shape, args, sem, scratch=(), rider=None):
    if rider is None:
        return list(pl.pallas_call(body, name=name, grid=grid, in_specs=in_specs, out_specs=out_specs, out_shape=out_shape,
                                   scratch_shapes=list(scratch), compiler_params=_params(sem))(*args)), []
    n_in, n_out, n_scr = len(in_specs), len(out_specs), len(scratch)
    r_in, r_out = len(rider.ins), len(rider.out_shapes)
    steps = grid[0]

    def riding(*refs):
        ins, r_ins = refs[:n_in], refs[n_in:n_in + r_in]
        outs = refs[n_in + r_in:n_in + r_in + n_out]
        r_outs = refs[n_in + r_in + n_out:n_in + r_in + n_out + r_out]
        scr = refs[n_in + r_in + n_out + r_out:n_in + r_in + n_out + r_out + n_scr]
        sems = refs[n_in + r_in + n_out + r_out + n_scr:]

        @pl.when(pl.program_id(0) == 0)
        def _():
            rider.run(True, r_ins, r_outs, sems)

        body(*ins, *outs, *scr)

        @pl.when(pl.program_id(0) == steps - 1)
        def _():
            rider.run(False, r_ins, r_outs, sems)

    hbm = pl.BlockSpec(memory_space=pl.ANY)
    res = pl.pallas_call(
        riding, name=name, grid=grid, in_specs=list(in_specs) + [hbm] * r_in, out_specs=list(out_specs) + [hbm] * r_out,
        out_shape=list(out_shape) + rider.out_shapes, scratch_shapes=list(scratch) + rider.scratch(),
        compiler_params=_params("arbitrary"),
    )(*args, *rider.ins)
    return list(res[:n_out]), list(res[n_out:])


def _row_tile(seq, want):
    return want if seq % want == 0 else seq


def _rope_tables(posf, seq):
    half = ROPE_DIM // 2
    inv_freq = np.power(np.float32(ROPE_THETA), -np.arange(half, dtype=np.float32) / np.float32(half)).astype(np.float32)
    j = np.arange(LANES) % HEAD_DIM
    invf = jnp.asarray(np.where(j < ROPE_DIM, inv_freq[j % half], 0.0).astype(np.float32).reshape(1, LANES))
    m_p = jnp.asarray(((j >= half) & (j < ROPE_DIM)).astype(np.float32).reshape(1, LANES))
    m_m = jnp.asarray(-(j < half).astype(np.float32).reshape(1, LANES))
    tm = _row_tile(seq, 512)

    def body(pos_ref, invf_ref, mp_ref, mm_ref, cos_ref, sp_ref, sm_ref):
        def block(i, carry):
            rows = pl.ds(pl.multiple_of(i * tm, tm), tm)
            ang = pos_ref[rows, :] * invf_ref[...]
            s = jnp.sin(ang)
            cos_ref[rows, :] = jnp.cos(ang)
            sp_ref[rows, :] = s * mp_ref[...]
            sm_ref[rows, :] = s * mm_ref[...]
            return carry

        lax.fori_loop(0, seq // tm, block, 0)

    return body, (posf, invf, m_p, m_m), [_sds((seq, LANES))] * 3


def _even_proj(x, mod, w_in_t, tabs, seq, rider=None):
    tm = _row_tile(seq, 512)

    def body(x_ref, mod_ref, w_ref, cos_ref, sp_ref, sm_ref, q_ref, k_ref, v_ref, su_ref, sv_ref, g_ref):
        h = x_ref[...] * (1.0 + mod_ref[1:2, :]) + mod_ref[0:1, :]
        p = _mm_nt(h, w_ref[...])
        cos_t, sin_p, sin_m = cos_ref[...], sp_ref[...], sm_ref[...]
        for j in range(ATTN_WIDTH // LANES):
            q_ref[:, j * LANES:(j + 1) * LANES] = _rope(p[:, j * LANES:(j + 1) * LANES], cos_t, sin_p, sin_m).astype(q_ref.dtype)
        k_ref[...] = _rope(p[:, 512:640], cos_t, sin_p, sin_m).astype(k_ref.dtype)
        v_ref[...] = p[:, 640:768].astype(v_ref.dtype)
        su_ref[...] = p[:, 768:1280].astype(su_ref.dtype)
        sv_ref[...] = p[:, 1280:1792].astype(sv_ref.dtype)
        g_ref[...] = p[:, 1792:2816].astype(g_ref.dtype)

    return _call(
        body, "even_proj", (seq // tm,),
        [_rows(tm, D_MODEL), _full((3, D_MODEL)), _const((EVEN_IN, D_MODEL))] + [_rows(tm, LANES)] * 3,
        [_rows(tm, 512), _rows(tm, LANES), _rows(tm, LANES), _rows(tm, 512), _rows(tm, 512), _rows(tm, D_MODEL)],
        [_sds((seq, 512), MXU_DTYPE), _sds((seq, LANES), MXU_DTYPE), _sds((seq, LANES), MXU_DTYPE), _sds((seq, 512), ACT_DTYPE),
         _sds((seq, 512), ACT_DTYPE), _sds((seq, D_MODEL), ACT_DTYPE)],
        (x, mod, w_in_t, *tabs), "parallel", rider=rider)


def _sg_forward(sv, lng, lnb, sgw_ref, sgb, e2):
    vn, vhat, rstd, svo = [], [], [], []
    for j in range(SG_WIDTH // LANES):
        t = sv[:, j * LANES:(j + 1) * LANES]
        mu = _group_sum(t, e2) * (1.0 / HEAD_DIM)
        d = t - mu
        var = _group_sum(d * d, e2) * (1.0 / HEAD_DIM)
        r = lax.rsqrt(var + LN_EPS)
        vh = d * r
        vhat.append(vh)
        rstd.append(r)
        vn.append(vh * lng[:, j * LANES:(j + 1) * LANES] + lnb[:, j * LANES:(j + 1) * LANES])
    lane = _lane_iota((BLK, LANES))
    for j in range(SG_WIDTH // LANES):
        lo = _mm(sgw_ref[2 * j], vn[j])
        hi = _mm(sgw_ref[2 * j + 1], vn[j])
        svo.append(jnp.where(lane < HEAD_DIM, lo, hi) + sgb[:, j * LANES:(j + 1) * LANES])
    return svo, vn, vhat, rstd


def _even_mix(q, k, v, su, sv, sink, sgln_g, sgln_b, sgw, sgb_full, e2, seq, rider=None):
    nb = seq // BLK

    def body(sink_ref, q_ref, k_ref, v_ref, su_ref, sv_ref, lng_ref, lnb_ref, sgw_ref, sgb_ref, e2_ref, ycat_ref, lse_ref,
             svo_ref, vhat_ref, rstd_ref):
        n = pl.program_id(0)
        kband = _band(k_ref, n, nb)
        vband = _band(v_ref, n, nb)
        bias = _band_bias(n, seq)
        lane = _lane_iota((BLK, LANES))
        lse = jnp.zeros((BLK, LANES), F32)
        q_tile = lambda j: q_ref[:, j * LANES:(j + 1) * LANES].astype(F32)
        acc = [jnp.zeros((BLK, LANES), F32) for _ in range(ATTN_WIDTH // LANES)]
        for kv in range(N_Q_HEADS // Q_PER_KV):
            heads = range(Q_PER_KV * kv, Q_PER_KV * (kv + 1))
            sink = _per_head_column([sink_ref[h] for h in heads])
            s = _mm_nt(_stack_heads(q_tile, kv), kband) * (HEAD_DIM ** -0.5) + bias
            m = jnp.maximum(jnp.max(s, axis=1, keepdims=True), sink)
            p = jnp.exp(s - m)
            denom = jnp.sum(p, axis=1, keepdims=True) + jnp.exp(sink - m)
            o4 = _mm(p / denom, vband)
            l4 = m + jnp.log(denom)
            for g, h in enumerate(heads):
                acc[h // 2] = acc[h // 2] + _from_kv_lanes(o4[g * BLK:(g + 1) * BLK], h)
                lse = jnp.where(lane == h, l4[g * BLK:(g + 1) * BLK], lse)
        for j in range(ATTN_WIDTH // LANES):
            ycat_ref[:, j * LANES:(j + 1) * LANES] = acc[j].astype(ycat_ref.dtype)
        lse_ref[...] = lse
        svo, _, vhat, rstd = _sg_forward(sv_ref[...].astype(F32), lng_ref[...], lnb_ref[...], sgw_ref, sgb_ref[...], e2_ref[...])
        for j in range(SG_WIDTH // LANES):
            cs = slice(j * LANES, (j + 1) * LANES)
            ysg = su_ref[:, cs].astype(F32) * svo[j]
            ycat_ref[:, ATTN_WIDTH + j * LANES:ATTN_WIDTH + (j + 1) * LANES] = ysg.astype(ycat_ref.dtype)
            svo_ref[:, cs], vhat_ref[:, cs], rstd_ref[:, cs] = (t.astype(svo_ref.dtype) for t in (svo[j], vhat[j], rstd[j]))

    blk = lambda w: pl.BlockSpec((BLK, w), lambda n: (n, 0))
    return _call(
        body, "even_mix", (nb,),
        [pl.BlockSpec(memory_space=pltpu.SMEM), blk(512), _full((seq, LANES)), _full((seq, LANES)), blk(512), blk(512),
         _full((1, 512)), _full((1, 512)), _full((8, BLK, BLK)), _full((BLK, 512)), _full((LANES, LANES))],
        [blk(D_MODEL), blk(LANES)] + [blk(SG_WIDTH)] * 3,
        [_sds((seq, D_MODEL), ACT_DTYPE), _sds((seq, LANES))] + [_sds((seq, SG_WIDTH), ACT_DTYPE)] * 3,
        (sink, q, k, v, su, sv, sgln_g, sgln_b, sgw, sgb_full, e2), "parallel", rider=rider)


def _even_out(ycat, g, x, mod, mod_next, w_out, w_in4_next, ln_g, ln_b, seq, rider=None):
    tm = _row_tile(seq, 512)
    cs = ODD_IN // 4

    def body(y_ref, g_ref, x_ref, mod_ref, modn_ref, wo_ref, wi_ref, g1_ref, b1_ref, zhat_ref, rstd_ref, x1_ref, xr_ref, gn_ref):
        gg = g_ref[...].astype(F32)
        out = _mm(y_ref[...].astype(F32) * (gg * _sigmoid(gg)), wo_ref[...])
        z = ALPHA * x_ref[...] + mod_ref[2:3, :] * out
        zhat, rstd = _ln_stats(z)
        zhat_ref[...] = zhat
        rstd_ref[...] = rstd
        x1 = zhat * g1_ref[...] + b1_ref[...]
        x1_ref[...] = x1
        hb = (x1 * (1.0 + modn_ref[1:2, :]) + modn_ref[0:1, :]).astype(MXU_DTYPE)
        for s in range(2):
            xr_ref[:, s * cs:(s + 1) * cs] = jnp.dot(hb, wi_ref[s], preferred_element_type=F32)
            gn_ref[:, s * cs:(s + 1) * cs] = jnp.dot(hb, wi_ref[2 + s], preferred_element_type=F32).astype(gn_ref.dtype)

    return _call(
        body, "even_out", (seq // tm,),
        [_rows(tm, D_MODEL)] * 3 + [_full((3, D_MODEL)), _full((3, D_MODEL)), _const((D_MODEL, D_MODEL)), _const((4, D_MODEL, cs)),
                                    _full((1, D_MODEL)), _full((1, D_MODEL))],
        [_rows(tm, D_MODEL), _rows(tm, 1)] + [_rows(tm, D_MODEL)] * 3,
        [_sds((seq, D_MODEL)), _sds((seq, 1))] + [_sds((seq, D_MODEL))] * 2 + [_sds((seq, D_MODEL), ACT_DTYPE)],
        (ycat, g, x, mod, mod_next, w_out, w_in4_next, ln_g, ln_b), "parallel", rider=rider)


def _halo_specs(tm, seq, width, order=lambda i: i):
    per = tm // 8
    last = seq // 8 - 1
    return [pl.BlockSpec((8, width), lambda i: (jnp.maximum(order(i) * per - 1, 0), 0)),
            pl.BlockSpec((tm, width), lambda i: (order(i), 0)),
            pl.BlockSpec((8, width), lambda i: (jnp.minimum((order(i) + 1) * per, last), 0))]


def _extended(prev_ref, main_ref, next_ref, i, n_steps):
    prev = jnp.where(i > 0, prev_ref[...], 0.0)
    nxt = jnp.where(i < n_steps - 1, next_ref[...], 0.0)
    return jnp.concatenate([prev, main_ref[...], nxt], axis=0)


def _shifted(ext, off, tm):
    if off == 0:
        return ext[8:8 + tm]
    return pltpu.roll(ext, (-off) % ext.shape[0], 0)[8:8 + tm]


SCAN_SUB = 8


def _lru_gate(xh, pre, bias, sp, hs, d):
    r = _sigmoid(pre[:, 0:LANES] + bias[2 * d:2 * d + 1, hs])
    ig = _sigmoid(pre[:, LANES:2 * LANES] + bias[2 * d + 1:2 * d + 2, hs])
    neg_log_a = RG_LRU_C * r * sp[d:d + 1, hs]
    a = jnp.exp(-neg_log_a)
    u = jnp.tanh(neg_log_a) * (a * a + 1.0)
    inv_s = lax.rsqrt(jnp.maximum(u, jnp.finfo(F32).tiny))
    return r, ig, a, u * inv_s, inv_s


def _conv_block(xp_ref, xm_ref, xn_ref, cw_ref, cb_ref, blk, steps, tm):
    ext = _extended(xp_ref, xm_ref, xn_ref, blk, steps)
    return cb_ref[...] + sum(cw_ref[kk:kk + 1, :] * _shifted(ext, kk - 2, tm) for kk in range(4))


def _scan_tiles(a_ref, b_ref, h_ref, hprev_ref, carry_h, carry_a, rows, descending, post):
    sub = SCAN_SUB
    tiles = rows // sub
    row = lax.broadcasted_iota(jnp.int32, (sub, D_MODEL), 0)

    def shift(v, d, fill):
        if descending:
            return jnp.where(row <= sub - 1 - d, pltpu.roll(v, sub - d, 0), fill)
        return jnp.where(row >= d, pltpu.roll(v, d, 0), fill)

    def last(v):
        return jnp.broadcast_to(v[0:1, :] if descending else v[sub - 1:sub, :], v.shape)

    def tile(j, c):
        ch, ca = c
        r0 = pl.multiple_of(((tiles - 1 - j) if descending else j) * sub, sub)
        at = a_ref[pl.ds(r0, sub), :]
        bt = b_ref[pl.ds(r0, sub), :]
        coef = shift(at, 1, ca) if post else at
        acc_a, acc_b = coef, bt
        for d in (1, 2, 4):
            acc_b = acc_b + acc_a * shift(acc_b, d, 0.0)
            acc_a = acc_a * shift(acc_a, d, 1.0)
        h = acc_b + acc_a * ch
        h_ref[pl.ds(r0, sub), :] = h
        if post:
            return last(h), last(at)
        hprev_ref[pl.ds(r0, sub), :] = shift(h, 1, ch)
        return last(h), ca

    ch, ca = lax.fori_loop(0, tiles, tile, (carry_h[...], carry_a[...]), unroll=4)
    carry_h[...] = ch
    carry_a[...] = ca


def _lru_fwd(xr, xc, conv_w, conv_b, wcat, bias, lam, seq, d):
    tb = _row_tile(seq, 512)
    steps = seq // tb
    descending = d == 1
    order = (lambda i: steps - 1 - i) if descending else (lambda i: i)
    with_conv = xc is None
    n_x = 5 if with_conv else 1

    def body(*refs):
        x_refs, (w_ref, bias_ref, lam_ref) = refs[:n_x], refs[n_x:n_x + 3]
        h_ref, hp_ref, a_ref, r_ref, i_ref, s_ref, q_ref = refs[n_x + 3:n_x + 10]
        b_scr, carry_h, carry_a = refs[-3:]
        i = pl.program_id(0)

        @pl.when(i == 0)
        def _():
            carry_h[...] = jnp.zeros_like(carry_h)
            carry_a[...] = jnp.zeros_like(carry_a)

        if with_conv:
            xc_ref = refs[n_x + 10]
            xc_ref[...] = _conv_block(*x_refs, order(i), steps, tb)
        else:
            xc_ref = x_refs[0]
        sp, _ = _softplus_neg(lam_ref[...])
        bias = bias_ref[...]
        for h in range(RNN_HEADS):
            hs = slice(h * LANES, (h + 1) * LANES)
            xh = xc_ref[:, hs]
            r, ig, a, s, q = _lru_gate(xh, _mm(xh, w_ref[h, :, 2 * d * LANES:2 * (d + 1) * LANES]), bias, sp, hs, d)
            a_ref[:, hs] = a
            b_scr[:, hs] = s * ig * xh
            for ref, val in ((r_ref, r), (i_ref, ig), (s_ref, s), (q_ref, q)):
                ref[:, hs] = val.astype(ref.dtype)
        _scan_tiles(a_ref, b_scr, h_ref, hp_ref, carry_h, carry_a, tb, descending, post=False)

    row_spec = pl.BlockSpec((tb, D_MODEL), lambda i: (order(i), 0))
    if with_conv:
        x_specs, x_args = _halo_specs(tb, seq, D_MODEL, order) + [_full((4, D_MODEL)), _full((1, D_MODEL))], (xr, xr, xr, conv_w, conv_b)
    else:
        x_specs, x_args = [row_spec], (xc,)
    n_out = 8 if with_conv else 7
    return pl.pallas_call(
        body, name="lru_fwd_%d" % d, grid=(steps,),
        in_specs=x_specs + [_full((8, LANES, 512)), _full((4, D_MODEL)), _full((2, D_MODEL))],
        out_specs=[row_spec] * n_out,
        out_shape=[_sds((seq, D_MODEL))] * 3 + [_sds((seq, D_MODEL), ACT_DTYPE)] * 4 + [_sds((seq, D_MODEL))] * (n_out - 7),
        scratch_shapes=[pltpu.VMEM((tb, D_MODEL), F32)] + [pltpu.VMEM((SCAN_SUB, D_MODEL), F32)] * 2,
        compiler_params=_params("arbitrary"),
    )(*x_args, wcat, bias, lam)


def _odd_out_and_loss(hf, hr, g, x1, tgt, mod, w_out, ln_g, ln_b, seq):
    tm = _row_tile(seq, 512)

    def body(hf_ref, hr_ref, g_ref, x_ref, t_ref, mod_ref, w_ref, lg_ref, lb_ref,
             dhs_ref, dg_ref, dres_ref, loss_ref, dw_ref, vec_ref):
        @pl.when(pl.program_id(0) == 0)
        def _():
            loss_ref[...] = jnp.zeros_like(loss_ref)
            dw_ref[...] = jnp.zeros_like(dw_ref)
            vec_ref[...] = jnp.zeros_like(vec_ref)

        gg = g_ref[...].astype(F32)
        sg = _sigmoid(gg)
        silu = gg * sg
        hsum = hf_ref[...] + hr_ref[...]
        y = hsum * silu
        out = _mm(y, w_ref[...])
        gate = mod_ref[2:3, :]
        z = ALPHA * x_ref[...] + gate * out
        zhat, rstd = _ln_stats(z)
        x2 = zhat * lg_ref[...] + lb_ref[...]
        err = x2 - t_ref[...]
        loss_ref[...] += 0.5 * jnp.sum(jnp.mean(err * err, axis=-1, keepdims=True))
        dx2 = err * (1.0 / D_MODEL)
        dz = _ln_bwd(dx2, zhat, rstd, lg_ref[...])
        vec_ref[0:1, :] += jnp.sum(dx2 * zhat, axis=0, keepdims=True)
        vec_ref[1:2, :] += jnp.sum(dx2, axis=0, keepdims=True)
        vec_ref[2:3, :] += jnp.sum(dz * out, axis=0, keepdims=True)
        dres_ref[...] = ALPHA * dz
        dout = gate * dz
        dw_ref[...] += _mm_tn(y, dout)
        dy = _mm_nt(dout, w_ref[...])
        dhs_ref[...] = dy * silu
        dg_ref[...] = (dy * hsum * (sg * (1.0 + gg * (1.0 - sg)))).astype(dg_ref.dtype)

    return pl.pallas_call(
        body, name="odd_out_loss", grid=(seq // tm,),
        in_specs=[_rows(tm, D_MODEL)] * 5 + [_full((3, D_MODEL)), _const((D_MODEL, D_MODEL)),
                                             _full((1, D_MODEL)), _full((1, D_MODEL))],
        out_specs=[_rows(tm, D_MODEL)] * 3 + [_full((8, LANES)), _full((D_MODEL, D_MODEL)), _full((8, D_MODEL))],
        out_shape=[_sds((seq, D_MODEL)), _sds((seq, D_MODEL), ACT_DTYPE), _sds((seq, D_MODEL)), _sds((8, LANES)),
                   _sds((D_MODEL, D_MODEL)), _sds((8, D_MODEL))],
        compiler_params=_params("arbitrary"),
    )(hf, hr, g, x1, tgt, mod, w_out, ln_g, ln_b)


def _lru_bwd(xc, dhs, hprev, a_d, r_d, i_d, s_d, q_d, wcat, lam, seq, d):
    tb = _row_tile(seq, 512)
    steps = seq // tb
    descending = d == 0
    order = (lambda i: steps - 1 - i) if descending else (lambda i: i)
    cols = slice(2 * d * LANES, 2 * (d + 1) * LANES)

    def body(xc_ref, dhs_ref, hp_ref, a_ref, r_ref, i_ref, s_ref, q_ref, w_ref, lam_ref, dxc_ref, dw_ref, vec_ref,
             g_scr, carry_h, carry_a):
        i = pl.program_id(0)

        @pl.when(i == 0)
        def _():
            dw_ref[...] = jnp.zeros_like(dw_ref)
            vec_ref[...] = jnp.zeros_like(vec_ref)
            carry_h[...] = jnp.zeros_like(carry_h)
            carry_a[...] = jnp.zeros_like(carry_a)

        sp, dsp = _softplus_neg(lam_ref[...])
        _scan_tiles(a_ref, dhs_ref, g_scr, None, carry_h, carry_a, tb, descending, post=True)
        for h in range(RNN_HEADS):
            hs = slice(h * LANES, (h + 1) * LANES)
            xh, a = xc_ref[:, hs], a_ref[:, hs]
            r, ig, s = r_ref[:, hs].astype(F32), i_ref[:, hs].astype(F32), s_ref[:, hs].astype(F32)
            db = g_scr[:, hs]
            da = db * hp_ref[:, hs]
            dlog_a = da * a - (db * ig * xh) * (a * a * q_ref[:, hs].astype(F32))
            dpr = dlog_a * (-RG_LRU_C) * sp[d:d + 1, hs] * r * (1.0 - r)
            dpi = db * s * xh * ig * (1.0 - ig)
            vec_ref[0:1, hs] += jnp.sum(dpr, axis=0, keepdims=True)
            vec_ref[1:2, hs] += jnp.sum(dpi, axis=0, keepdims=True)
            vec_ref[2:3, hs] += jnp.sum(dlog_a * r, axis=0, keepdims=True) * (-RG_LRU_C) * dsp[d:d + 1, hs]
            dcat = jnp.concatenate([dpr, dpi], axis=1)
            dw_ref[h] += _mm_tn(xh, dcat)
            dxc_ref[:, hs] = db * s * ig + _mm_nt(dcat, w_ref[h, :, cols])

    row_spec = pl.BlockSpec((tb, D_MODEL), lambda i: (order(i), 0))
    return pl.pallas_call(
        body, name="lru_bwd_%d" % d, grid=(steps,),
        in_specs=[row_spec] * 8 + [_full((8, LANES, 512)), _full((2, D_MODEL))],
        out_specs=[row_spec, _full((8, LANES, 2 * LANES)), _full((8, D_MODEL))],
        out_shape=[_sds((seq, D_MODEL)), _sds((8, LANES, 2 * LANES)), _sds((8, D_MODEL))],
        scratch_shapes=[pltpu.VMEM((tb, D_MODEL), F32)] + [pltpu.VMEM((SCAN_SUB, D_MODEL), F32)] * 2,
        compiler_params=_params("arbitrary"),
    )(xc, dhs, hprev, a_d, r_d, i_d, s_d, q_d, wcat, lam)


def _odd_proj_bwd(dxc_f, dxc_r, xr, dg, x1, dres, mod, conv_w, w_in4, seq):
    tm = _row_tile(seq, 512)
    steps = seq // tm

    def body(fp_ref, fm_ref, fn_ref, rp_ref, rm_ref, rn_ref, xp_ref, xm_ref, xn_ref, dg_ref, x_ref, dres_ref, mod_ref, cw_ref,
             w_ref, dx_ref, dw_ref, vec_ref, dpb_ref):
        i = pl.program_id(0)

        @pl.when(i == 0)
        def _():
            vec_ref[...] = jnp.zeros_like(vec_ref)
            dw_ref[...] = jnp.zeros_like(dw_ref)

        dxc_m = fm_ref[...] + rm_ref[...]
        dext = jnp.concatenate([jnp.where(i > 0, fp_ref[...] + rp_ref[...], 0.0), dxc_m,
                                jnp.where(i < steps - 1, fn_ref[...] + rn_ref[...], 0.0)], axis=0)
        xext = _extended(xp_ref, xm_ref, xn_ref, i, steps)
        dxr = sum(cw_ref[kk:kk + 1, :] * _shifted(dext, 2 - kk, tm) for kk in range(4))
        for kk in range(4):
            vec_ref[kk:kk + 1, :] += jnp.sum(dxc_m * _shifted(xext, kk - 2, tm), axis=0, keepdims=True)
        vec_ref[4:5, :] += jnp.sum(dxc_m, axis=0, keepdims=True)
        dpb_ref[:, :D_MODEL] = dxr.astype(dpb_ref.dtype)
        dpb_ref[:, D_MODEL:] = dg_ref[...].astype(dpb_ref.dtype)
        cs = ODD_IN // 4
        dh = sum(_mm_nt(dpb_ref[:, s * cs:(s + 1) * cs], w_ref[s]) for s in range(4))
        x = x_ref[...]
        h_t = (x * (1.0 + mod_ref[1:2, :]) + mod_ref[0:1, :]).T.astype(MXU_DTYPE)
        for s in range(4):
            dw_ref[s] += jnp.dot(h_t, dpb_ref[:, s * cs:(s + 1) * cs], preferred_element_type=F32)
        vec_ref[5:6, :] += jnp.sum(dh, axis=0, keepdims=True)
        vec_ref[6:7, :] += jnp.sum(dh * x, axis=0, keepdims=True)
        dx_ref[...] = dres_ref[...] + dh * (1.0 + mod_ref[1:2, :])

    return pl.pallas_call(
        body, name="odd_proj_bwd", grid=(steps,),
        in_specs=_halo_specs(tm, seq, D_MODEL) * 3 + [_rows(tm, D_MODEL)] * 3
        + [_full((3, D_MODEL)), _full((4, D_MODEL)), _const((4, D_MODEL, ODD_IN // 4))],
        out_specs=[_rows(tm, D_MODEL), _const((4, D_MODEL, ODD_IN // 4)), _full((8, D_MODEL))],
        out_shape=[_sds((seq, D_MODEL)), _sds((4, D_MODEL, ODD_IN // 4)), _sds((8, D_MODEL))],
        scratch_shapes=[pltpu.VMEM((tm, ODD_IN), MXU_DTYPE)],
        compiler_params=_params("arbitrary"),
    )(dxc_f, dxc_f, dxc_f, dxc_r, dxc_r, dxc_r, xr, xr, xr, dg, x1, dres, mod, conv_w, w_in4)


def _even_out_bwd(dx1, zhat, rstd, ycat, g, mod, ln_g, w_out, seq, rider=None):
    tm = _row_tile(seq, 512)
    steps = seq // tm

    def body(dx_ref, zh_ref, rs_ref, y_ref, g_ref, mod_ref, lg_ref, w_ref, dy_ref, dg_ref, dres_ref, dw_ref, vec_ref):
        i = pl.program_id(0)

        @pl.when(i == 0)
        def _():
            dw_ref[...] = jnp.zeros_like(dw_ref)
            vec_ref[...] = jnp.zeros_like(vec_ref)

        zhat = zh_ref[...]
        dx1_ = dx_ref[...]
        dz = _ln_bwd(dx1_, zhat, rs_ref[...], lg_ref[...])
        vec_ref[0:1, :] += jnp.sum(dx1_ * zhat, axis=0, keepdims=True)
        vec_ref[1:2, :] += jnp.sum(dx1_, axis=0, keepdims=True)
        dres_ref[...] = ALPHA * dz
        gate = mod_ref[2:3, :]
        gg = g_ref[...].astype(F32)
        sg = _sigmoid(gg)
        silu = gg * sg
        ycat_ = y_ref[...].astype(F32)
        dw_ref[...] += _mm_tn(ycat_ * silu, dz)
        dy = _mm_nt(gate * dz, w_ref[...])
        dy_ref[...] = (dy * silu).astype(dy_ref.dtype)
        dg_ref[...] = (dy * ycat_ * (sg * (1.0 + gg * (1.0 - sg)))).astype(dg_ref.dtype)

        @pl.when(i == steps - 1)
        def _():
            m_acc = dw_ref[...]
            vec_ref[2:3, :] = jnp.sum(w_ref[...].astype(F32) * m_acc, axis=0, keepdims=True)
            dw_ref[...] = m_acc * gate

    return _call(
        body, "even_out_bwd", (steps,),
        [_rows(tm, D_MODEL), _rows(tm, D_MODEL), _rows(tm, 1), _rows(tm, D_MODEL), _rows(tm, D_MODEL), _full((3, D_MODEL)),
         _full((1, D_MODEL)), _const((D_MODEL, D_MODEL))],
        [_rows(tm, D_MODEL)] * 3 + [_full((D_MODEL, D_MODEL)), _full((8, D_MODEL))],
        [_sds((seq, D_MODEL), ACT_DTYPE), _sds((seq, D_MODEL), ACT_DTYPE), _sds((seq, D_MODEL)), _sds((D_MODEL, D_MODEL)),
         _sds((8, D_MODEL))],
        (dx1, zhat, rstd, ycat, g, mod, ln_g, w_out), "arbitrary", rider=rider)


def _even_mix_bwd(q, k, v, lse, ycat, dycat, su, svo_s, vhat_s, rstd_s, sink, sgln_g, sgln_b, sgw, e2, e8, seq, rider=None):
    nb = seq // BLK

    def body(sink_ref, q_ref, k_ref, v_ref, lse_ref, y_ref, dy_ref, su_ref, svo_ref, vhat_ref, rstd_ref, lng_ref, lnb_ref, sgw_ref,
             e2_ref, e8_ref, dq_ref, dsu_ref, dsv_ref, dk_ref, dv_ref, dsgw_ref, dsgb_ref, vec_ref, dsink_ref, dsgb_acc):
        n = pl.program_id(0)

        @pl.when(n == 0)
        def _():
            dk_ref[...] = jnp.zeros_like(dk_ref)
            dv_ref[...] = jnp.zeros_like(dv_ref)
            dsgw_ref[...] = jnp.zeros_like(dsgw_ref)
            dsgb_acc[...] = jnp.zeros_like(dsgb_acc)
            vec_ref[...] = jnp.zeros_like(vec_ref)
            dsink_ref[...] = jnp.zeros_like(dsink_ref)

        kband = _band(k_ref, n, nb)
        vband = _band(v_ref, n, nb)
        bias = _band_bias(n, seq)
        lane = _lane_iota((BLK, LANES))
        row8 = lax.broadcasted_iota(jnp.int32, (8, LANES), 0)
        lse = lse_ref[...]
        dkb = jnp.zeros((LANES, 3 * BLK), F32)
        dvb = jnp.zeros((LANES, 3 * BLK), F32)
        dsink = jnp.zeros((8, LANES), F32)
        q_tile = lambda j: q_ref[:, j * LANES:(j + 1) * LANES].astype(F32)
        do_tile = lambda j: dy_ref[:, j * LANES:(j + 1) * LANES].astype(F32)
        dq = [jnp.zeros((BLK, LANES), F32) for _ in range(ATTN_WIDTH // LANES)]
        for kv in range(N_Q_HEADS // Q_PER_KV):
            heads = range(Q_PER_KV * kv, Q_PER_KV * (kv + 1))
            lse4, delta4 = [], []
            for h in heads:
                head_lanes = (lane < HEAD_DIM) if h % 2 == 0 else (lane >= HEAD_DIM)
                lse4.append(jnp.sum(jnp.where(lane == h, lse, 0.0), axis=1, keepdims=True))
                o_tile = y_ref[:, (h // 2) * LANES:(h // 2 + 1) * LANES].astype(F32)
                delta4.append(jnp.sum(jnp.where(head_lanes, do_tile(h // 2) * o_tile, 0.0), axis=1, keepdims=True))
            lse4, delta4 = jnp.concatenate(lse4, axis=0), jnp.concatenate(delta4, axis=0)
            q4, do4 = _stack_heads(q_tile, kv), _stack_heads(do_tile, kv)
            s = _mm_nt(q4, kband) * (HEAD_DIM ** -0.5) + bias
            p = jnp.exp(s - lse4)
            wsink = jnp.exp(_per_head_column([sink_ref[h] for h in heads]) - lse4) * delta4
            ds = p * (_mm_nt(do4, vband) - delta4) * (HEAD_DIM ** -0.5)
            dq4 = _mm(ds, kband)
            dkb = dkb + _mm_tn(q4, ds)
            dvb = dvb + _mm_tn(do4, p)
            for g, h in enumerate(heads):
                dq[h // 2] = dq[h // 2] + _from_kv_lanes(dq4[g * BLK:(g + 1) * BLK], h)
                dsink = dsink + jnp.where(row8 == h, -jnp.sum(wsink[g * BLK:(g + 1) * BLK]), 0.0)
        for j in range(ATTN_WIDTH // LANES):
            dq_ref[:, j * LANES:(j + 1) * LANES] = dq[j].astype(dq_ref.dtype)
        dsink_ref[...] += dsink
        prev = jnp.maximum(n - 1, 0)
        nxt = jnp.minimum(n + 1, nb - 1)
        for part, blk_i in enumerate((prev, n, nxt)):
            rows = pl.ds(pl.multiple_of(blk_i * BLK, BLK), BLK)
            dk_ref[rows, :] += dkb[:, part * BLK:(part + 1) * BLK].T
            dv_ref[rows, :] += dvb[:, part * BLK:(part + 1) * BLK].T

        e2 = e2_ref[...]
        lng, lnb = lng_ref[...], lnb_ref[...]
        for j in range(SG_WIDTH // LANES):
            cs = slice(j * LANES, (j + 1) * LANES)
            vhat = vhat_ref[:, cs].astype(F32)
            vn = vhat * lng[:, cs] + lnb[:, cs]
            dysg = dy_ref[:, ATTN_WIDTH + j * LANES:ATTN_WIDTH + (j + 1) * LANES].astype(F32)
            dsu_ref[:, cs] = (dysg * svo_ref[:, cs].astype(F32)).astype(dsu_ref.dtype)
            dsvo = dysg * su_ref[:, cs].astype(F32)
            dsgb_acc[:, cs] += dsvo
            d_lo = jnp.where(lane < HEAD_DIM, dsvo, 0.0)
            d_hi = dsvo - d_lo
            dsgw_ref[2 * j] += _mm_nt(d_lo, vn)
            dsgw_ref[2 * j + 1] += _mm_nt(d_hi, vn)
            dvn = _mm_tn(sgw_ref[2 * j], d_lo) + _mm_tn(sgw_ref[2 * j + 1], d_hi)
            vec_ref[0:1, cs] += jnp.sum(dvn * vhat, axis=0, keepdims=True)
            vec_ref[1:2, cs] += jnp.sum(dvn, axis=0, keepdims=True)
            dvh = dvn * lng[:, cs]
            m1 = _group_sum(dvh, e2) * (1.0 / HEAD_DIM)
            m2 = _group_sum(dvh * vhat, e2) * (1.0 / HEAD_DIM)
            dsv_ref[:, cs] = (rstd_ref[:, cs].astype(F32) * (dvh - m1 - vhat * m2)).astype(dsv_ref.dtype)

        @pl.when(n == nb - 1)
        def _():
            rest = dsgb_acc[...]
            total = jnp.zeros((8, BLK), F32)
            for _ in range(3):
                part = rest.astype(MXU_DTYPE)
                total = total + lax.dot_general(e8_ref[...], part, (((1,), (1,)), ((), ())), preferred_element_type=F32)
                rest = rest - part.astype(F32)
            dsgb_ref[...] = total

    blk = lambda w: pl.BlockSpec((BLK, w), lambda n: (n, 0))
    return _call(
        body, "even_mix_bwd", (nb,),
        [pl.BlockSpec(memory_space=pltpu.SMEM), blk(512), _full((seq, LANES)), _full((seq, LANES)), blk(LANES),
         blk(D_MODEL), blk(D_MODEL), blk(512), blk(512), blk(512), blk(512), _full((1, 512)), _full((1, 512)), _full((8, BLK, BLK)),
         _full((LANES, LANES)), _full((8, 512))],
        [blk(512), blk(512), blk(512), _full((seq, LANES)), _full((seq, LANES)), _full((8, BLK, BLK)),
         _full((8, BLK)), _full((8, 512)), _full((8, LANES))],
        [_sds((seq, 512), ACT_DTYPE), _sds((seq, 512), ACT_DTYPE), _sds((seq, 512), ACT_DTYPE), _sds((seq, LANES)), _sds((seq, LANES)),
         _sds((8, BLK, BLK)), _sds((8, BLK)), _sds((8, 512)), _sds((8, LANES))],
        (sink, q, k, v, lse, ycat, dycat, su, svo_s, vhat_s, rstd_s, sgln_g, sgln_b, sgw, e2, e8), "arbitrary",
        scratch=[pltpu.VMEM((BLK, 512), F32)], rider=rider)


def _even_proj_bwd(dq, dk, dv, dsu, dsv, dg, x, dres, mod, tabs, w_in_t, seq):
    tm = _row_tile(seq, 512)

    def body(dq_ref, dk_ref, dv_ref, dsu_ref, dsv_ref, dg_ref, x_ref, dres_ref, mod_ref, cos_ref, sp_ref, sm_ref, wt_ref,
             dx_ref, dw_ref, vec_ref, dpb_ref):
        @pl.when(pl.program_id(0) == 0)
        def _():
            vec_ref[...] = jnp.zeros_like(vec_ref)
            dw_ref[...] = jnp.zeros_like(dw_ref)

        cos_t, sin_p, sin_m = cos_ref[...], sp_ref[...], sm_ref[...]
        dt = dpb_ref.dtype
        for j in range(ATTN_WIDTH // LANES):
            cs = slice(j * LANES, (j + 1) * LANES)
            dpb_ref[:, cs] = _rope_t(dq_ref[:, cs].astype(F32), cos_t, sin_p, sin_m).astype(dt)
        dpb_ref[:, 512:640] = _rope_t(dk_ref[...], cos_t, sin_p, sin_m).astype(dt)
        dpb_ref[:, 640:768] = dv_ref[...].astype(dt)
        dpb_ref[:, 768:1280] = dsu_ref[...].astype(dt)
        dpb_ref[:, 1280:1792] = dsv_ref[...].astype(dt)
        dpb_ref[:, 1792:2816] = dg_ref[...].astype(dt)
        dpb = dpb_ref[...]
        dh = jnp.dot(dpb, wt_ref[...], preferred_element_type=F32)
        x_ = x_ref[...]
        hb = (x_ * (1.0 + mod_ref[1:2, :]) + mod_ref[0:1, :]).astype(MXU_DTYPE)
        dw_ref[...] += _mm_tn(dpb, hb)
        vec_ref[0:1, :] += jnp.sum(dh, axis=0, keepdims=True)
        vec_ref[1:2, :] += jnp.sum(dh * x_, axis=0, keepdims=True)
        dx_ref[...] = dres_ref[...] + dh * (1.0 + mod_ref[1:2, :])

    return pl.pallas_call(
        body, name="even_proj_bwd", grid=(seq // tm,),
        in_specs=[_rows(tm, 512), _rows(tm, LANES), _rows(tm, LANES), _rows(tm, 512), _rows(tm, 512), _rows(tm, D_MODEL),
                  _rows(tm, D_MODEL), _rows(tm, D_MODEL), _full((3, D_MODEL))] + [_rows(tm, LANES)] * 3
        + [_const((EVEN_IN, D_MODEL))],
        out_specs=[_rows(tm, D_MODEL), _const((EVEN_IN, D_MODEL)), _full((8, D_MODEL))],
        out_shape=[_sds((seq, D_MODEL)), _sds((EVEN_IN, D_MODEL)), _sds((8, D_MODEL))],
        scratch_shapes=[pltpu.VMEM((tm, EVEN_IN), MXU_DTYPE)],
        compiler_params=_params("arbitrary"),
    )(dq, dk, dv, dsu, dsv, dg, x, dres, mod, *tabs, w_in_t)


def _local_step(x, tabs, tgt, mod, w, seq, ride=None):
    rid = lambda make, *a: None if ride is None else make(*a)
    mxu = lambda a: a.astype(MXU_DTYPE)
    row = lambda a: a.reshape(1, -1)
    e2 = mxu(jnp.kron(jnp.eye(2, dtype=F32), jnp.ones((HEAD_DIM, HEAD_DIM), F32)))
    e8 = mxu(jnp.repeat(jnp.eye(N_SG_GROUPS, dtype=F32), HEAD_DIM, axis=1))
    sgw = mxu(w["ev_sg_w"])
    sgb_full = jnp.repeat(w["ev_sg_b"].T, HEAD_DIM, axis=1)
    sgln_g, sgln_b = row(w["ev_sg_ln_g"]), row(w["ev_sg_ln_b"])
    sink = w["ev_sink"].reshape(N_Q_HEADS)
    ev_w_in_t = mxu(w["ev_w_in_t"])
    if ride is None:
        ev_w_out, od_w_in, od_w_out = mxu(w["ev_w_out"]), mxu(w["od_w_in"]), mxu(w["od_w_out"])
    wcat = mxu(jnp.concatenate([w["od_w_a"][0], w["od_w_x"][0], w["od_w_a"][1], w["od_w_x"][1]], axis=2))
    gate_bias = jnp.stack([w["od_b_a"][0], w["od_b_x"][0], w["od_b_a"][1], w["od_b_x"][1]])
    conv_b = row(w["od_conv_b"])
    ln_g, ln_b = w["ln_g"], w["ln_b"]

    (q, k, v, su, sv, g0), got = _even_proj(x, mod[0], ev_w_in_t, tabs, seq, rid(_gather_rider, ride and ride["ev_w_out"]))
    if ride is not None:
        ev_w_out = got[0].reshape(D_MODEL, D_MODEL)
    (ycat, lse, *sg_saved), got = _even_mix(q, k, v, su, sv, sink, sgln_g, sgln_b, sgw, sgb_full, e2, seq,
                                 rid(_gather_rider, ride and ride["od_w_in"]))
    if ride is not None:
        od_w_in = got[0]
    (zhat0, rstd0, x1, xr, g1), got = _even_out(ycat, g0, x, mod[0], mod[1], ev_w_out, od_w_in, ln_g[0:1], ln_b[0:1], seq,
                                      rid(_gather_rider, ride and ride["od_w_out"]))
    if ride is not None:
        od_w_out = got[0].reshape(D_MODEL, D_MODEL)
    lru = (w["od_conv_w"], conv_b, wcat, gate_bias, w["od_lam"], seq)
    hf, hpf, *saved_f, xc = _lru_fwd(xr, None, *lru, 0)
    hr, hpr, *saved_r = _lru_fwd(xr, xc, *lru, 1)
    dhs, dg1, dres1, loss, d_od_w_out, vec_o = _odd_out_and_loss(hf, hr, g1, x1, tgt, mod[1], od_w_out, ln_g[1:2], ln_b[1:2], seq)
    dxc_f, dw_f, vec_f = _lru_bwd(xc, dhs, hpf, *saved_f, wcat, w["od_lam"], seq, 0)
    dxc_r, dw_r, vec_r = _lru_bwd(xc, dhs, hpr, *saved_r, wcat, w["od_lam"], seq, 1)
    dx1, d_od_w_in, vec_p = _odd_proj_bwd(dxc_f, dxc_r, xr, dg1, x1, dres1, mod[1], w["od_conv_w"], od_w_in, seq)
    d_od_w_a = jnp.stack([dw_f[:, :, 0:128], dw_r[:, :, 0:128]])
    d_od_w_x = jnp.stack([dw_f[:, :, 128:256], dw_r[:, :, 128:256]])
    od_parts = [d_od_w_in.reshape(4, 2, 512, 512), d_od_w_out.reshape(4, 2, 128, D_MODEL),
                d_od_w_a.reshape(4, 2, 2 * BLK, BLK), d_od_w_x.reshape(4, 2, 2 * BLK, BLK)]
    (dycat, dg0, dres0, d_ev_w_out, vec_e), got_od = _even_out_bwd(dx1, zhat0, rstd0, ycat, g0, mod[0], ln_g[0:1], ev_w_out, seq,
                                                                   rid(_sibling_swap_rider, od_parts))
    if ride is not None:
        od_sums = _sum_sibling(ride["core"], od_parts, got_od, [ride["wire"]] * 4, "sum_sibling_od")
    (dq, dsu, dsv, dk, dv, d_sgw, d_sgb, vec_s, d_sink), od_slots = _even_mix_bwd(
        q, k, v, lse, ycat, dycat, su, *sg_saved, sink, sgln_g, sgln_b, sgw, e2, e8, seq,
        rid(_chip_exchange_rider, ride and od_sums))
    grad_x, d_ev_w_in_t, vec_x = _even_proj_bwd(dq, dk, dv, dsu, dsv, dg0, x, dres0, mod[0], tabs, ev_w_in_t, seq)

    rows, dmod_blk = _pack_small(vec_x, vec_e, vec_p, vec_o, vec_f, vec_r, vec_s, d_sink, d_sgb, loss)
    grads = {"rows": rows, "dmod_blk": dmod_blk, "ev_w_in_t": d_ev_w_in_t, "ev_w_out": d_ev_w_out, "ev_sg_w": d_sgw}
    if ride is None:
        grads.update({"od_w_in": d_od_w_in, "od_w_out": d_od_w_out, "od_w_a": d_od_w_a, "od_w_x": d_od_w_x})
    else:
        grads["od_slots"] = od_slots
    return grad_x, grads


ROW_DMOD, ROW_LN, ROW_SG_LN, ROW_SG_B, ROW_CONV_W, ROW_CONV_B, ROW_B_A, ROW_B_X, ROW_LAM, ROW_SINK, ROW_LOSS = (
    0, 6, 10, 11, 12, 16, 17, 19, 21, 23, 24)
SMALL_ROWS = 64


def _pack_small(vec_x, vec_e, vec_p, vec_o, vec_f, vec_r, vec_s, d_sink, d_sgb, loss):
    def body(x_ref, e_ref, p_ref, o_ref, f_ref, r_ref, s_ref, sink_ref, sgb_ref, loss_ref, rows_ref, dmod_ref):
        rows_ref[...] = jnp.zeros_like(rows_ref)
        dmod_ref[...] = jnp.zeros_like(dmod_ref)
        put = [(ROW_DMOD, x_ref, 0), (ROW_DMOD + 1, x_ref, 1), (ROW_DMOD + 2, e_ref, 2), (ROW_DMOD + 3, p_ref, 5),
               (ROW_DMOD + 4, p_ref, 6), (ROW_DMOD + 5, o_ref, 2), (ROW_LN, e_ref, 0), (ROW_LN + 1, e_ref, 1),
               (ROW_LN + 2, o_ref, 0), (ROW_LN + 3, o_ref, 1), (ROW_CONV_B, p_ref, 4), (ROW_B_A, f_ref, 0),
               (ROW_B_A + 1, r_ref, 0), (ROW_B_X, f_ref, 1), (ROW_B_X + 1, r_ref, 1), (ROW_LAM, f_ref, 2), (ROW_LAM + 1, r_ref, 2)]
        put += [(ROW_CONV_W + k, p_ref, k) for k in range(4)]
        for dst, ref, src in put:
            rows_ref[dst:dst + 1, :] = ref[src:src + 1, :]
            if dst < 6:
                dmod_ref[dst:dst + 1, :] = ref[src:src + 1, :]
        rows_ref[ROW_SG_LN:ROW_SG_LN + 1, 0:SG_WIDTH] = s_ref[0:1, :]
        rows_ref[ROW_SG_LN:ROW_SG_LN + 1, SG_WIDTH:2 * SG_WIDTH] = s_ref[1:2, :]
        lane = _lane_iota((1, LANES))
        sink = jnp.zeros((1, LANES), F32)
        for h in range(N_Q_HEADS):
            rows_ref[ROW_SG_B:ROW_SG_B + 1, h * LANES:(h + 1) * LANES] = sgb_ref[h:h + 1, :]
            sink = jnp.where(lane == h, sink_ref[h:h + 1, :], sink)
        rows_ref[ROW_SINK:ROW_SINK + 1, 0:LANES] = sink
        rows_ref[ROW_LOSS:ROW_LOSS + 1, 0:LANES] = jnp.where(lane == 0, loss_ref[0:1, :], 0.0)

    return pl.pallas_call(body, name="pack_small", out_shape=[_sds((SMALL_ROWS, D_MODEL)), _sds((8, D_MODEL))])(
        vec_x, vec_e, vec_p, vec_o, vec_f, vec_r, vec_s, d_sink, d_sgb, loss)


def _allgather8(block, name):
    m_per, n = block.shape

    def body(x_ref, out_ref, send_sems, recv_sems, local_sem):
        x, y, c = _place()
        me, sibling = (x, y, c), (x, y, 1 - c)
        chips = [(1 - x, y), (x, 1 - y), (1 - x, 1 - y)]

        def rows(px, py, pc):
            return out_ref.at[pl.ds((4 * px + 2 * py + pc) * m_per, m_per), :]

        def copy(k, blk, to, src=None):
            return pltpu.make_async_remote_copy(src_ref=rows(*blk) if src is None else src, dst_ref=rows(*blk),
                                                send_sem=send_sems.at[k], recv_sem=recv_sems.at[k], device_id=to,
                                                device_id_type=MESH)

        mine = pltpu.make_async_copy(x_ref, rows(*me), local_sem)
        mine.start()
        first = [copy(0, me, sibling, src=x_ref)] + [copy(1 + j, me, (*chip, c), src=x_ref) for j, chip in enumerate(chips)]
        for cp in first:
            cp.start()
        passed = [copy(4 + j, (*chip, c), sibling) for j, chip in enumerate(chips)]
        for j, chip in enumerate(chips):
            copy(1 + j, (*chip, c), me).wait_recv()
            passed[j].start()
        copy(0, sibling, me).wait_recv()
        for j, chip in enumerate(chips):
            copy(4 + j, (*chip, 1 - c), me).wait_recv()
        for cp in first + passed:
            cp.wait_send()
        mine.wait()

    return pl.pallas_call(
        body, name=name, out_shape=_sds((8 * m_per, n), block.dtype),
        in_specs=[pl.BlockSpec(memory_space=pltpu.VMEM)], out_specs=pl.BlockSpec(memory_space=pltpu.VMEM),
        scratch_shapes=[pltpu.SemaphoreType.DMA((7,)), pltpu.SemaphoreType.DMA((7,)), pltpu.SemaphoreType.DMA],
        compiler_params=pltpu.CompilerParams(vmem_limit_bytes=VMEM_LIMIT),
    )(block)


class _Copies:
    def __init__(self, send_sems, recv_sems, local_sems, stages):
        self.send_sems, self.recv_sems, self.local_sems, self.stages = send_sems, recv_sems, local_sems, stages
        self.sent, self.staged, self.locals = [], [], []

    def remote(self, k, src, dst, to):
        return pltpu.make_async_remote_copy(src_ref=src, dst_ref=dst, send_sem=self.send_sems.at[k], recv_sem=self.recv_sems.at[k],
                                            device_id=to, device_id_type=MESH)

    def send(self, k, src, dst, to):
        cp = self.remote(k, src, dst, to)
        cp.start()
        self.sent.append(cp)

    def arrived(self, k, dst, frm):
        self.remote(k, dst, dst, frm).wait_recv()

    def local(self, src, dst):
        k = len(self.staged)
        cp = pltpu.make_async_copy(src, self.stages[k], self.local_sems.at[2 * k])
        cp.start()
        self.staged.append((cp, dst))

    def flush(self):
        for k in range(len(self.locals), len(self.staged)):
            cp, dst = self.staged[k]
            cp.wait()
            out = pltpu.make_async_copy(self.stages[k], dst, self.local_sems.at[2 * k + 1])
            out.start()
            self.locals.append(out)

    def drain(self):
        self.flush()
        for cp in self.sent:
            cp.wait_send()
        for cp in self.locals:
            cp.wait()


def _comm_call(body, name, ins, out_shapes, n_remote, stages, side=None):
    n_in, n_out = len(ins), len(out_shapes)
    side_fn, side_ins, side_outs = side if side is not None else (None, (), [])
    s_in, s_out = len(side_ins), len(side_outs)

    def kern(*refs):
        in_refs, refs = refs[:n_in], refs[n_in:]
        side_in_refs, refs = refs[:s_in], refs[s_in:]
        out_refs, refs = refs[:n_out], refs[n_out:]
        side_out_refs, refs = refs[:s_out], refs[s_out:]
        if side is None:
            body(_Copies(refs[0], refs[1], refs[2], refs[3:]), in_refs, out_refs)
            return
        side_bufs, side_sems, refs = refs[:s_out], refs[s_out], refs[s_out + 1:]
        cps = _Copies(refs[0], refs[1], refs[2], refs[3:])
        leave = [pltpu.make_async_copy(side_bufs[k], side_out_refs[k], side_sems.at[k]) for k in range(s_out)]

        def run_side():
            side_fn(*side_in_refs, *side_bufs)
            for cp in leave:
                cp.start()

        body(cps, in_refs, out_refs, run_side)
        for cp in leave:
            cp.wait()

    hbm, vmem = pl.BlockSpec(memory_space=pl.ANY), pl.BlockSpec(memory_space=pltpu.VMEM)
    side_scratch = [] if side is None else [pltpu.VMEM(o.shape, o.dtype) for o in side_outs] + [pltpu.SemaphoreType.DMA((s_out,))]
    return pl.pallas_call(
        kern, name=name, out_shape=list(out_shapes) + list(side_outs), in_specs=[hbm] * n_in + [vmem] * s_in,
        out_specs=[hbm] * (n_out + s_out),
        scratch_shapes=side_scratch + [pltpu.SemaphoreType.DMA((n_remote,)), pltpu.SemaphoreType.DMA((n_remote,)),
                                       pltpu.SemaphoreType.DMA((2 * len(stages),))] + [pltpu.VMEM(s, d) for s, d in stages],
        compiler_params=pltpu.CompilerParams(vmem_limit_bytes=VMEM_LIMIT),
    )(*ins, *side_ins)


def _gather_to_all(cps, pairs, me, sibling, other_chips, c, base, meanwhile=None):
    idx = lambda p: 4 * p[0] + 2 * p[1] + p[2]
    for i, (src, dst) in enumerate(pairs):
        cps.local(src, dst.at[idx(me)])
        cps.send(base + 7 * i, src, dst.at[idx(me)], sibling)
        for j, chip in enumerate(other_chips):
            cps.send(base + 7 * i + 1 + j, src, dst.at[idx(me)], (*chip, c))
    cps.flush()
    if meanwhile is not None:
        meanwhile()
    for j, chip in enumerate(other_chips):
        for i, (_, dst) in enumerate(pairs):
            got = dst.at[idx((*chip, c))]
            cps.arrived(base + 7 * i + 1 + j, got, (*chip, c))
            cps.send(base + 7 * i + 4 + j, got, got, sibling)
    for i, (_, dst) in enumerate(pairs):
        cps.arrived(base + 7 * i, dst.at[idx(sibling)], sibling)
        for j, chip in enumerate(other_chips):
            cps.arrived(base + 7 * i + 4 + j, dst.at[idx((*chip, 1 - c))], sibling)


def _gather_weights(shards, small, side):
    n = len(shards)

    def body(cps, ins, outs, run_side):
        x, y, c = _place()
        me, sibling, mine = (x, y, c), (x, y, 1 - c), 2 * x + y
        chips = [(1 - x, y), (x, 1 - y), (1 - x, 1 - y)]
        for i in range(n):
            cps.local(ins[i], outs[i].at[mine])
        for j, (px, py) in enumerate(chips):
            for i in range(n):
                hr = shards[i].shape[0] // 2
                rows = pl.ds(c * hr, hr)
                cps.send(6 * i + j, ins[i].at[rows], outs[i].at[mine, rows], (px, py, c))
        _gather_to_all(cps, [(ins[n], outs[n])], me, sibling, chips, c, 6 * n, meanwhile=run_side)
        for j, (px, py) in enumerate(chips):
            for i in range(n):
                hr = shards[i].shape[0] // 2
                got = outs[i].at[2 * px + py, pl.ds(c * hr, hr)]
                cps.arrived(6 * i + j, got, (px, py, c))
                cps.send(6 * i + 3 + j, got, got, sibling)
        for j, (px, py) in enumerate(chips):
            for i in range(n):
                hr = shards[i].shape[0] // 2
                cps.arrived(6 * i + 3 + j, outs[i].at[2 * px + py, pl.ds((1 - c) * hr, hr)], sibling)
        cps.drain()

    return _comm_call(body, "gather_weights", list(shards) + [small],
                      [_sds((4,) + s.shape, s.dtype) for s in shards] + [_sds((8,) + small.shape, small.dtype)], 6 * n + 7,
                      [(a.shape, a.dtype) for a in list(shards) + [small]], side)


def _reduce_sibling(parts, dmod_rows):
    n = len(parts)

    def body(cps, ins, outs):
        x, y, c = _place()
        me, sibling = (x, y, c), (x, y, 1 - c)
        chips = [(1 - x, y), (x, 1 - y), (1 - x, 1 - y)]
        for i in range(n):
            cps.send(i, ins[i].at[:, 1 - c], outs[i], sibling)
        _gather_to_all(cps, [(ins[n], outs[n])], me, sibling, chips, c, n)
        for i in range(n):
            cps.arrived(i, outs[i], sibling)
        cps.drain()

    return _comm_call(body, "reduce_sibling", list(parts) + [dmod_rows],
                      [_sds((4,) + p.shape[2:], p.dtype) for p in parts] + [_sds((8,) + dmod_rows.shape, dmod_rows.dtype)], n + 7,
                      [(dmod_rows.shape, dmod_rows.dtype)])


def _reduce_chips(parts):
    n = len(parts)

    def body(cps, ins, outs):
        x, y, c = _place()
        mine = 2 * x + y
        chips = _other_chips(x, y)
        for i in range(n):
            cps.local(ins[i].at[mine], outs[i].at[mine])
        for j, (px, py) in enumerate(chips):
            for i in range(n):
                cps.send(3 * i + j, ins[i].at[2 * px + py], outs[i].at[mine], (px, py, c))
        cps.flush()
        for j, (px, py) in enumerate(chips):
            for i in range(n):
                cps.arrived(3 * i + j, outs[i].at[2 * px + py], (px, py, c))
        cps.drain()

    return _comm_call(body, "reduce_chips", list(parts), [_sds(p.shape, p.dtype) for p in parts], 3 * n,
                      [(p.shape[1:], p.dtype) for p in parts])


def _gather_reduced(shard_parts, repl_parts):
    ns, nr = len(shard_parts), len(repl_parts)

    def body(cps, ins, outs):
        x, y, c = _place()
        me, sibling = (x, y, c), (x, y, 1 - c)
        chips = [(1 - x, y), (x, 1 - y), (1 - x, 1 - y)]
        for i in range(ns):
            cps.local(ins[i], outs[i].at[c])
            cps.send(i, ins[i], outs[i].at[c], sibling)
        _gather_to_all(cps, [(ins[ns + i], outs[ns + i]) for i in range(nr)], me, sibling, chips, c, ns)
        for i in range(ns):
            cps.arrived(i, outs[i].at[1 - c], sibling)
        cps.drain()

    return _comm_call(body, "gather_reduced", list(shard_parts) + list(repl_parts),
                      [_sds((2,) + p.shape, p.dtype) for p in shard_parts] + [_sds((8,) + p.shape, p.dtype) for p in repl_parts],
                      ns + 7 * nr, [(p.shape, p.dtype) for p in list(shard_parts) + list(repl_parts)])


def _sum_sibling(core, parts, got, wire, name):
    n = len(parts)

    def body(core_ref, *refs):
        for i in range(n):
            refs[2 * n + i][0] = (refs[i][0] + refs[n + i][0]).astype(wire[i])

    keep_spec = lambda p: pl.BlockSpec((1, None) + p.shape[2:], lambda s, core_ref: (s, core_ref[0], 0, 0))
    slot_spec = lambda p: pl.BlockSpec((1,) + p.shape[2:], lambda s, core_ref: (s, 0, 0))
    return pl.pallas_call(
        body, name=name,
        grid_spec=pltpu.PrefetchScalarGridSpec(
            num_scalar_prefetch=1, grid=(4,), in_specs=[keep_spec(p) for p in parts] + [slot_spec(p) for p in parts],
            out_specs=[slot_spec(p) for p in parts]),
        out_shape=[_sds((4,) + p.shape[2:], wire[i]) for i, p in enumerate(parts)],
        compiler_params=_params("parallel"),
    )(core, *parts, *got)


def _sum_slots(slots, name):
    n = len(slots)

    def spec_pair(p):
        k, rows, cols = p.shape
        sub = 16 if p.dtype == BF16 else 8
        if (rows // 2) % sub == 0:
            return pl.BlockSpec((k, rows // 2, cols), lambda i: (0, i, 0)), pl.BlockSpec((rows // 2, cols), lambda i: (i, 0))
        return pl.BlockSpec((k, rows, cols), lambda i: (0, 0, 0)), pl.BlockSpec((rows, cols), lambda i: (0, 0))

    pairs = [spec_pair(p) for p in slots]

    def body(*refs):
        for i in range(n):
            acc = refs[i][0].astype(F32)
            for j in range(1, slots[i].shape[0]):
                acc = acc + refs[i][j].astype(F32)
            refs[n + i][...] = acc

    return pl.pallas_call(
        body, name=name, grid=(2,), in_specs=[a for a, _ in pairs], out_specs=[b for _, b in pairs],
        out_shape=[_sds(p.shape[1:]) for p in slots], compiler_params=_params("arbitrary"),
    )(*slots)


def _pack_block(c, conv_w, conv_b, b_a, b_x, lam):
    q = D_MODEL // 4

    def body(c_ref, cw_ref, cb_ref, ba_ref, bx_ref, lam_ref, o_ref):
        o_ref[...] = jnp.zeros_like(o_ref)
        o_ref[0:1, :] = c_ref[...]
        for k in range(4):
            o_ref[1:2, k * q:(k + 1) * q] = cw_ref[k:k + 1, :]
        o_ref[2:3, 0:q] = cb_ref[...]
        for k in range(2):
            o_ref[2:3, (1 + k) * q:(2 + k) * q] = ba_ref[k:k + 1, :]
            o_ref[3:4, k * q:(k + 1) * q] = bx_ref[k:k + 1, :]
            o_ref[3:4, (2 + k) * q:(3 + k) * q] = lam_ref[k:k + 1, :]

    return pl.pallas_call(body, name="pack_block", out_shape=_sds((8, D_MODEL)))(c, conv_w, conv_b, b_a, b_x, lam)


def _unpack_small(g_small):
    q = D_MODEL // 4

    def body(g_ref, c_ref, cw_ref, cb_ref, ba_ref, bx_ref, lam_ref):
        for d in range(8):
            c_ref[d:d + 1, :] = g_ref[d, 0:1, :]
        for s in range(4):
            cols = slice(s * q, (s + 1) * q)
            for k in range(4):
                cw_ref[k:k + 1, cols] = g_ref[2 * s, 1:2, k * q:(k + 1) * q]
            cb_ref[0:1, cols] = g_ref[2 * s, 2:3, 0:q]
            for k in range(2):
                ba_ref[k:k + 1, cols] = g_ref[2 * s, 2:3, (1 + k) * q:(2 + k) * q]
                bx_ref[k:k + 1, cols] = g_ref[2 * s, 3:4, k * q:(k + 1) * q]
                lam_ref[k:k + 1, cols] = g_ref[2 * s, 3:4, (2 + k) * q:(3 + k) * q]

    return pl.pallas_call(
        body, name="unpack_small",
        out_shape=[_sds((8, D_MODEL)), _sds((4, D_MODEL)), _sds((1, D_MODEL))] + [_sds((2, D_MODEL))] * 3,
    )(g_small)


def _modulation(c_all, ada_w, ada_b):
    cols = ada_w.shape[2]

    def body(c_ref, w_ref, b_ref, o_ref):
        cc = c_ref[...]
        o_ref[0] = _mm(cc * _sigmoid(cc), w_ref[0]) + b_ref[0]

    return pl.pallas_call(
        body, name="modulation", grid=(2,),
        in_specs=[_full((8, D_MODEL)), pl.BlockSpec((1, D_MODEL, cols), lambda l: (l, 0, 0)), pl.BlockSpec((1, 1, cols), lambda l: (l, 0, 0))],
        out_specs=pl.BlockSpec((1, 8, cols), lambda l: (l, 0, 0)), out_shape=_sds((2, 8, cols)),
        compiler_params=_params("parallel"),
    )(c_all, ada_w, ada_b)


def _adamw_math(w, g, m, v):
    m = ADAM_B1 * m + (1.0 - ADAM_B1) * g
    v = ADAM_B2 * v + (1.0 - ADAM_B2) * (g * g)
    m_hat = m / (1.0 - ADAM_B1 ** ADAM_STEP)
    v_hat = v / (1.0 - ADAM_B2 ** ADAM_STEP)
    delta = -ADAM_LR * (m_hat / (jnp.sqrt(v_hat) + ADAM_EPS) + ADAM_WD * w)
    return delta, m, v


def _ada_update(c_all, dmod, w, m, v, rider=None):
    cols = w.shape[2]
    tr = 256
    per = D_MODEL // tr
    spec3 = pl.BlockSpec((1, tr, cols), lambda i: (i // per, i % per, 0))

    def body(c_ref, d_ref, w_ref, m_ref, v_ref, g_ref, dl_ref, nm_ref, nv_ref):
        cc = c_ref[...]
        g = _mm_tn(cc * _sigmoid(cc), d_ref[0])
        g_ref[0] = g
        dl_ref[0], nm_ref[0], nv_ref[0] = _adamw_math(w_ref[0], g, m_ref[0], v_ref[0])

    return _call(
        body, "ada_update", (2 * per,),
        [pl.BlockSpec((8, tr), lambda i: (0, i % per)), pl.BlockSpec((1, 8, cols), lambda i: (i // per, 0, 0)), spec3, spec3, spec3],
        [spec3] * 4, [_sds(w.shape)] * 4, (c_all, dmod, w, m, v), "parallel", rider=rider)


def _adamw_matrices(params):
    n = len(params)
    steps = 8

    def body(*refs):
        ins, outs = refs[:4 * n], refs[4 * n:]
        for j in range(n):
            w_ref, g_ref, m_ref, v_ref = ins[4 * j:4 * j + 4]
            g = g_ref[...]
            outs[4 * j][...] = g
            outs[4 * j + 1][...], outs[4 * j + 2][...], outs[4 * j + 3][...] = _adamw_math(w_ref[...], g, m_ref[...], v_ref[...])

    spec = lambda p: _rows(p[0].shape[0] // steps, p[0].shape[1])
    res = pl.pallas_call(
        body, name="adamw_matrices", grid=(steps,), in_specs=[spec(p) for p in params for _ in range(4)],
        out_specs=[spec(p) for p in params for _ in range(4)], out_shape=[_sds(p[0].shape) for p in params for _ in range(4)],
        compiler_params=_params("parallel"),
    )(*[a for p in params for a in p])
    return [tuple(res[4 * j:4 * j + 4]) for j in range(n)]


def _adamw_small(gs, chip, params):
    n = len(params)
    shard_cols = D_MODEL // 4

    def body(chip_ref, rows_ref, cols_ref, *refs):
        ins, outs = refs[:3 * n], refs[3 * n:]
        for j in range(n):
            w_ref, m_ref, v_ref = ins[3 * j:3 * j + 3]
            g_ref, d_ref, nm_ref, nv_ref = outs[4 * j:4 * j + 4]
            for dst, sharded, src in params[j][3]:
                g = (cols_ref if sharded else rows_ref)[src]
                g_ref[dst] = g
                d_ref[dst], nm_ref[dst], nv_ref[dst] = _adamw_math(w_ref[dst], g, m_ref[dst], v_ref[dst])

    whole = lambda a: pl.BlockSpec(a.shape, lambda i, chip_ref: (0, 0))
    flat = [a for p in params for a in p[:3]]
    res = pl.pallas_call(
        body, name="adamw_small",
        grid_spec=pltpu.PrefetchScalarGridSpec(
            num_scalar_prefetch=1, grid=(1,),
            in_specs=[whole(gs), pl.BlockSpec((gs.shape[0], shard_cols), lambda i, chip_ref: (0, chip_ref[0]))] + [whole(a) for a in flat],
            out_specs=[whole(p[0]) for p in params for _ in range(4)]),
        out_shape=[_sds(p[0].shape) for p in params for _ in range(4)],
        compiler_params=_params("arbitrary"),
    )(chip, gs, gs, *flat)
    return [tuple(res[4 * j:4 * j + 4]) for j in range(n)]


def _cols(a, start, size):
    return lax.dynamic_slice_in_dim(a, start, size, axis=a.ndim - 1)


def kernel(x, c, positions, ada_w, ada_b, ln_g, ln_b, ev_w_in, ev_w_out, ev_sink, ev_sg_ln_g, ev_sg_ln_b, ev_sg_w, ev_sg_b, od_w_in, od_conv_w, od_conv_b, od_w_a, od_b_a, od_w_x, od_b_x, od_lam, od_w_out, loss_target, m_ada_w, m_ada_b, m_ln_g, m_ln_b, m_ev_w_in, m_ev_w_out, m_ev_sink, m_ev_sg_ln_g, m_ev_sg_ln_b, m_ev_sg_w, m_ev_sg_b, m_od_w_in, m_od_conv_w, m_od_conv_b, m_od_w_a, m_od_b_a, m_od_w_x, m_od_b_x, m_od_lam, m_od_w_out, v_ada_w, v_ada_b, v_ln_g, v_ln_b, v_ev_w_in, v_ev_w_out, v_ev_sink, v_ev_sg_ln_g, v_ev_sg_ln_b, v_ev_sg_w, v_ev_sg_b, v_od_w_in, v_od_conv_w, v_od_conv_b, v_od_w_a, v_od_b_a, v_od_w_x, v_od_b_x, v_od_lam, v_od_w_out):
    seq = x.shape[1]
    px, py, pc = _place()
    chip = 2 * px + py
    dev = 2 * chip + pc

    blk = _pack_block(c, od_conv_w[0], od_conv_b[0].reshape(1, -1), od_b_a[0], od_b_x[0], od_lam[0])
    tr = lambda a: jnp.swapaxes(a, -1, -2)
    wire_w = lambda a: a.astype(MXU_DTYPE)
    posf = positions.astype(F32).reshape(seq, 1)
    ev_w_in4, g_small, *tabs = _gather_weights([wire_w(tr(ev_w_in[0]))], blk, _rope_tables(posf, seq))
    core = pc.astype(jnp.int32).reshape(1)
    ride = {"ev_w_out": wire_w(ev_w_out[0]), "od_w_in": wire_w(od_w_in[0]), "od_w_out": wire_w(od_w_out[0]),
            "core": core, "wire": MXU_DTYPE}
    c_all, conv_w, conv_b, b_a, b_x, lam = _unpack_small(g_small)
    conv_b = conv_b.reshape(D_MODEL)

    w_full = {
        "ev_w_in_t": ev_w_in4.reshape(EVEN_IN, D_MODEL),
        "ev_sink": ev_sink[0], "ev_sg_ln_g": ev_sg_ln_g[0], "ev_sg_ln_b": ev_sg_ln_b[0], "ev_sg_w": ev_sg_w[0],
        "ev_sg_b": ev_sg_b[0], "od_conv_w": conv_w, "od_conv_b": conv_b, "od_w_a": od_w_a[0], "od_b_a": b_a,
        "od_w_x": od_w_x[0], "od_b_x": b_x, "od_lam": lam, "ln_g": ln_g, "ln_b": ln_b,
    }

    ada_cols = ada_w.shape[2]
    mod_sh = _modulation(c_all, ada_w, _cols(ada_b, chip * ada_cols, ada_cols).reshape(2, 1, ada_cols))
    mod_all = _allgather8(mod_sh.reshape(16, ada_cols), "gather_mod").reshape(4, 2, 2, 8, ada_cols)[:, 0]
    mod_mine = lax.dynamic_index_in_dim(mod_all, dev, axis=2, keepdims=False)
    mod = mod_mine.transpose(1, 0, 2).reshape(2, 3, D_MODEL)

    grad_x, g = _local_step(x[0], tabs, loss_target[0], mod, w_full, seq, ride)

    parts = [g["ev_w_in_t"].reshape(4, 2, 352, D_MODEL), g["ev_w_out"].reshape(4, 2, 128, D_MODEL),
             g["ev_sg_w"].reshape(4, 2, BLK, BLK), g["rows"].reshape(4, 2, SMALL_ROWS // 8, D_MODEL)]
    wire = [MXU_DTYPE] * 3 + [F32]
    *got, dmod_gathered = _reduce_sibling(parts, g["dmod_blk"])
    ev_slots = list(_reduce_chips(_sum_sibling(core, parts, got, wire, "sum_sibling")))
    od_slots = list(g["od_slots"])
    mine = _sum_slots(ev_slots[0:2] + od_slots[0:2] + ev_slots[2:3] + od_slots[2:4] + ev_slots[3:4], "sum_chips")
    reduced = _gather_reduced(mine[:4], mine[4:])
    g_ev_w_in_t = reduced[0].reshape(704, D_MODEL)
    g_ev_w_out = reduced[1].reshape(256, D_MODEL)
    g_od_w_in = reduced[2].reshape(D_MODEL, 512)
    g_od_w_out = reduced[3].reshape(256, D_MODEL)
    g_sg_w = reduced[4].reshape(8 * BLK, BLK)
    g_w_a = reduced[5].reshape(16 * BLK, BLK)
    g_w_x = reduced[6].reshape(16 * BLK, BLK)
    gs = reduced[7].reshape(SMALL_ROWS, D_MODEL)
    loss = gs[ROW_LOSS, 0]
    dmod_all = dmod_gathered[:, 0:6].reshape(8, 2, 3 * D_MODEL)
    dmod_sh = _cols(dmod_all, chip * ada_cols, ada_cols).transpose(1, 0, 2)
    (g_ada_w, d_ada_w, nm_ada_w, nv_ada_w), _ = _ada_update(c_all, dmod_sh, ada_w, m_ada_w, v_ada_w)

    mats = (("ev_w_out", ev_w_out, g_ev_w_out, m_ev_w_out, v_ev_w_out), ("od_w_in", od_w_in, g_od_w_in, m_od_w_in, v_od_w_in),
            ("od_w_out", od_w_out, g_od_w_out, m_od_w_out, v_od_w_out), ("ev_sg_w", ev_sg_w, g_sg_w, m_ev_sg_w, v_ev_sg_w),
            ("od_w_a", od_w_a, g_w_a, m_od_w_a, v_od_w_a), ("od_w_x", od_w_x, g_w_x, m_od_w_x, v_od_w_x))
    upd = _adamw_matrices([(tr(ev_w_in[0]), g_ev_w_in_t, tr(m_ev_w_in[0]), tr(v_ev_w_in[0]))]
                          + [(w_.reshape(g_.shape), g_, m_.reshape(g_.shape), v_.reshape(g_.shape)) for _, w_, g_, m_, v_ in mats])
    big = {"ev_w_in": tuple(tr(a).reshape(ev_w_in.shape) for a in upd[0])}
    for (name, w_, _, _, _), u in zip(mats, upd[1:]):
        big[name] = tuple(a.reshape(w_.shape) for a in u)
    big["ada_w"] = (g_ada_w, d_ada_w, nm_ada_w, nv_ada_w)

    at = lambda r0, nr, c0, nc: (slice(r0, r0 + nr), slice(c0, c0 + nc))
    local = lambda r0, nr: ((nr, 256), [(at(0, nr, 0, 256), True, at(r0, nr, 0, 256))])
    small_g = {
        "ada_b": ((2, 3 * D_MODEL), [(at(l, 1, k * D_MODEL, D_MODEL), False, at(ROW_DMOD + 3 * l + k, 1, 0, D_MODEL))
                                     for l in range(2) for k in range(3)]),
        "ln_g": ((2, D_MODEL), [(at(l, 1, 0, D_MODEL), False, at(ROW_LN + 2 * l, 1, 0, D_MODEL)) for l in range(2)]),
        "ln_b": ((2, D_MODEL), [(at(l, 1, 0, D_MODEL), False, at(ROW_LN + 1 + 2 * l, 1, 0, D_MODEL)) for l in range(2)]),
        "ev_sink": ((1, N_Q_HEADS), [(at(0, 1, 0, N_Q_HEADS), False, at(ROW_SINK, 1, 0, N_Q_HEADS))]),
        "ev_sg_ln_g": ((1, SG_WIDTH), [(at(0, 1, 0, SG_WIDTH), False, at(ROW_SG_LN, 1, 0, SG_WIDTH))]),
        "ev_sg_ln_b": ((1, SG_WIDTH), [(at(0, 1, 0, SG_WIDTH), False, at(ROW_SG_LN, 1, SG_WIDTH, SG_WIDTH))]),
        "ev_sg_b": ((N_SG_GROUPS, BLK), [(at(j, 1, 0, BLK), False, at(ROW_SG_B, 1, j * BLK, BLK)) for j in range(N_SG_GROUPS)]),
        "od_conv_w": local(ROW_CONV_W, 4), "od_conv_b": local(ROW_CONV_B, 1), "od_b_a": local(ROW_B_A, 2),
        "od_b_x": local(ROW_B_X, 2), "od_lam": local(ROW_LAM, 2),
    }
    small_in = {"ada_b": (ada_b, m_ada_b, v_ada_b), "ln_g": (ln_g, m_ln_g, v_ln_g), "ln_b": (ln_b, m_ln_b, v_ln_b),
                "ev_sink": (ev_sink, m_ev_sink, v_ev_sink), "ev_sg_ln_g": (ev_sg_ln_g, m_ev_sg_ln_g, v_ev_sg_ln_g),
                "ev_sg_ln_b": (ev_sg_ln_b, m_ev_sg_ln_b, v_ev_sg_ln_b), "ev_sg_b": (ev_sg_b, m_ev_sg_b, v_ev_sg_b),
                "od_conv_w": (od_conv_w, m_od_conv_w, v_od_conv_w), "od_conv_b": (od_conv_b, m_od_conv_b, v_od_conv_b),
                "od_b_a": (od_b_a, m_od_b_a, v_od_b_a), "od_b_x": (od_b_x, m_od_b_x, v_od_b_x),
                "od_lam": (od_lam, m_od_lam, v_od_lam)}
    names_small = list(small_g)
    upd = _adamw_small(gs, chip.astype(jnp.int32).reshape(1),
                       [tuple(a.reshape(small_g[n][0]) for a in small_in[n]) + (small_g[n][1],) for n in names_small])
    res = dict(big)
    for n, u in zip(names_small, upd):
        res[n] = tuple(a.reshape(small_in[n][0].shape) for a in u)

    order = ["ada_w", "ada_b", "ln_g", "ln_b", "ev_w_in", "ev_w_out", "ev_sink", "ev_sg_ln_g", "ev_sg_ln_b", "ev_sg_w", "ev_sg_b",
             "od_w_in", "od_conv_w", "od_conv_b", "od_w_a", "od_b_a", "od_w_x", "od_b_x", "od_lam", "od_w_out"]
    return (loss, grad_x.reshape(x.shape), *[res[n][0] for n in order], *[res[n][1] for n in order],
            *[res[n][2] for n in order], *[res[n][3] for n in order])
```
